```python
import jax, jax.numpy as jnp
from jax import lax
import numpy as np

D_MODEL = 1024
BATCH = 8
SEQ = 4096
DEPTH = 2

N_META = 16
D_CONV = 512
CONV_A_WIDTH = 3
DN_HEADS = 4
DN_HEAD_DIM = 128
DN_DIM = DN_HEADS * DN_HEAD_DIM
DN_CONV_WIDTH = 4
DN_CHUNK = 64
D_MIX = D_CONV + DN_DIM
IN_DIM = 3 * D_CONV + 4 * DN_DIM + 2 * DN_HEADS
SWA_HEADS = 16
SWA_KV_HEADS = 4
SWA_HEAD_DIM = 64
SWA_WINDOW = 128
SWA_BLOCK = 128
D_FF = 2816
FFN_CONV_WIDTH = 3
EPS = 1e-6
N_EVEN = (DEPTH + 1) // 2
N_ODD = DEPTH // 2

kernel_name = "hybrid_shortconv_gdn_swa_convffn_meta"


def rms_norm(x, w):
    xf = x.astype(jnp.float32)
    y = xf * lax.rsqrt(jnp.mean(xf * xf, -1, keepdims=True) + EPS)
    return (y * w.astype(jnp.float32)).astype(x.dtype)


def l2_norm(x):
    return x * lax.rsqrt(jnp.sum(x * x, -1, keepdims=True) + EPS)


def causal_dwconv(x, w):
    k = w.shape[0]
    return lax.conv_general_dilated(
        x, w[:, None, :].astype(x.dtype), window_strides=(1,), padding=((k - 1, 0),),
        dimension_numbers=('NWC', 'WIO', 'NWC'), feature_group_count=x.shape[-1])


def gated_delta_rule(q, k, v, beta, g):
    q, k, v, beta, g = (t.astype(jnp.float32) for t in (q, k, v, beta, g))
    b, l, h, dk = q.shape
    dv = v.shape[-1]
    c = DN_CHUNK
    n = l // c

    def chunks(t):
        t = t.reshape((b, n, c, h) + t.shape[3:])
        return jnp.moveaxis(t, 3, 1)

    q, k, v, beta, g = chunks(q), chunks(k), chunks(v), chunks(beta), chunks(g)
    decay = jnp.cumsum(g, -1)
    diff = decay[..., :, None] - decay[..., None, :]
    idx = jnp.arange(c)
    strict = idx[:, None] > idx[None, :]
    incl = idx[:, None] >= idx[None, :]
    dmask = jnp.exp(jnp.where(incl, diff, -jnp.inf))
    kk = jnp.einsum('bhnid,bhnjd->bhnij', k, k)
    a_strict = jnp.where(strict, beta[..., None] * kk * dmask, 0.0)
    t_mat = a_strict + jnp.eye(c, dtype=jnp.float32)
    rhs = jnp.concatenate([v * beta[..., None], k * (beta * jnp.exp(decay))[..., None]], -1)
    sol = lax.linalg.triangular_solve(t_mat, rhs, left_side=True, lower=True, unit_diagonal=True)
    u = sol[..., :dv]
    w = sol[..., dv:]
    qk = jnp.einsum('bhnid,bhnjd->bhnij', q, k) * dmask
    q_dec = q * jnp.exp(decay)[..., None]
    k_dec = k * jnp.exp(decay[..., -1:] - decay)[..., None]
    g_last = jnp.exp(decay[..., -1])

    def step(s, inp):
        u_n, w_n, qk_n, qd_n, kd_n, gl_n = inp
        v_new = u_n - jnp.einsum('bhcd,bhde->bhce', w_n, s)
        o = jnp.einsum('bhcd,bhde->bhce', qd_n, s) + jnp.einsum('bhij,bhje->bhie', qk_n, v_new)
        s = s * gl_n[..., None, None] + jnp.einsum('bhcd,bhce->bhde', kd_n, v_new)
        return s, o

    xs = tuple(jnp.moveaxis(t, 2, 0) for t in (u, w, qk, q_dec, k_dec, g_last))
    s0 = jnp.zeros((b, h, dk, dv), jnp.float32)
    _, o = lax.scan(step, s0, xs)
    return jnp.transpose(o, (1, 0, 3, 2, 4)).reshape(b, l, h, dv)


def even_mixer(h, w_in, conv_a_w, dn_conv_w, a_log, dt_bias, dn_norm_w, w_out):
    b, l, _ = h.shape
    p = h @ w_in
    sizes = [D_CONV, D_CONV, D_CONV, 3 * DN_DIM, DN_DIM, DN_HEADS, DN_HEADS]
    splits = [sum(sizes[:i + 1]) for i in range(len(sizes) - 1)]
    a_gate_in, a_gate_out, a_h, qkv, z, beta_raw, alpha_raw = jnp.split(p, splits, -1)
    y_a = a_gate_out * causal_dwconv(a_gate_in * a_h, conv_a_w)
    qkv = jax.nn.silu(causal_dwconv(qkv, dn_conv_w))
    q, k, v = jnp.split(qkv, 3, -1)
    hd = (b, l, DN_HEADS, DN_HEAD_DIM)
    q = l2_norm(q.reshape(hd).astype(jnp.float32)) * (DN_HEAD_DIM ** -0.5)
    k = l2_norm(k.reshape(hd).astype(jnp.float32))
    v = v.reshape(hd).astype(jnp.float32)
    beta = jax.nn.sigmoid(beta_raw.astype(jnp.float32))
    g = -jnp.exp(a_log.astype(jnp.float32)) * jax.nn.softplus(
        alpha_raw.astype(jnp.float32) + dt_bias.astype(jnp.float32))
    pad = (-N_META) % DN_CHUNK
    padw = lambda t: jnp.pad(t, ((0, 0), (pad, 0)) + ((0, 0),) * (t.ndim - 2))
    o = gated_delta_rule(padw(q), padw(k), padw(v), padw(beta), padw(g))[:, pad:]
    o = rms_norm(o, dn_norm_w) * jax.nn.silu(z.reshape(hd).astype(jnp.float32))
    y_b = o.reshape(b, l, DN_DIM).astype(h.dtype)
    return jnp.concatenate([y_a, y_b], -1) @ w_out


def sink_softmax(logits, sink):
    m = jnp.maximum(jnp.max(logits, -1, keepdims=True), sink)
    e = jnp.exp(logits - m)
    return e / (jnp.sum(e, -1, keepdims=True) + jnp.exp(sink - m))


def swa_mixer(h, wq, wk, wv, q_norm_w, k_norm_w, sinks, wo):
    b, l, _ = h.shape
    kv, grp, d = SWA_KV_HEADS, SWA_HEADS // SWA_KV_HEADS, SWA_HEAD_DIM
    q = rms_norm((h @ wq).reshape(b, l, kv, grp, d), q_norm_w) * (d ** -0.5)
    k = rms_norm((h @ wk).reshape(b, l, kv, d), k_norm_w)
    v = (h @ wv).reshape(b, l, kv, d)
    sink = sinks.astype(jnp.float32).reshape(kv, grp)
    qm, qr = q[:, :N_META], q[:, N_META:]
    km, kr = k[:, :N_META], k[:, N_META:]
    vm, vr = v[:, :N_META], v[:, N_META:]
    sm = jnp.einsum('bikgd,bjkd->bkgij', qm, km).astype(jnp.float32)
    mmask = jnp.tril(jnp.ones((N_META, N_META), bool))
    pm = sink_softmax(jnp.where(mmask, sm, -jnp.inf), sink[None, :, :, None, None])
    om = jnp.einsum('bkgij,bjkd->bikgd', pm.astype(v.dtype), vm).reshape(b, N_META, SWA_HEADS * d)
    s_real = l - N_META
    nb = s_real // SWA_BLOCK
    qb = qr.reshape(b, nb, SWA_BLOCK, kv, grp, d)
    kb = kr.reshape(b, nb, SWA_BLOCK, kv, d)
    vb = vr.reshape(b, nb, SWA_BLOCK, kv, d)
    band = lambda t: jnp.concatenate(
        [jnp.concatenate([jnp.zeros_like(t[:, :1]), t[:, :-1]], 1), t], 2)
    kband, vband = band(kb), band(vb)
    s_meta = jnp.einsum('bnikgd,bjkd->bnkgij', qb, km).astype(jnp.float32)
    s_band = jnp.einsum('bnikgd,bnjkd->bnkgij', qb, kband).astype(jnp.float32)
    i = jnp.arange(SWA_BLOCK)[:, None]
    j = jnp.arange(2 * SWA_BLOCK)[None, :]
    n = jnp.arange(nb)[:, None, None]
    rel = i + SWA_BLOCK - j
    valid = (rel >= 0) & (rel < SWA_WINDOW) & (n * SWA_BLOCK - SWA_BLOCK + j >= 0)
    s_band = jnp.where(valid[None, :, None, None], s_band, -jnp.inf)
    p = sink_softmax(jnp.concatenate([s_meta, s_band], -1), sink[None, None, :, :, None, None])
    p = p.astype(v.dtype)
    orr = (jnp.einsum('bnkgim,bmkd->bnikgd', p[..., :N_META], vm)
           + jnp.einsum('bnkgij,bnjkd->bnikgd', p[..., N_META:], vband))
    orr = orr.reshape(b, s_real, SWA_HEADS * d)
    return jnp.concatenate([om, orr], 1) @ wo


def conv_ffn(h, w_up, conv_w, w_down):
    gate, val = jnp.split(h @ w_up, 2, -1)
    gate = causal_dwconv(gate, conv_w)
    return (jax.nn.silu(gate) * val) @ w_down


def _fwd_setup_inputs(seed: int = 0) -> dict:
    key = jax.random.key(seed)
    ks = jax.random.split(key, 24)
    nrm = lambda k, s, scale: jax.random.normal(k, s, jnp.float32) * scale
    gain = lambda k, s: 1.0 + 0.05 * jax.random.normal(k, s, jnp.float32)
    dt = jnp.exp(jax.random.uniform(ks[7], (N_EVEN, DN_HEADS), jnp.float32, np.log(1e-3), np.log(1e-1)))
    return {
        "x": nrm(ks[0], (BATCH, SEQ, D_MODEL), 1.0),
        "meta_tokens": nrm(ks[1], (N_META, D_MODEL), 1.0),
        "attn_norm_w": gain(ks[2], (DEPTH, D_MODEL)),
        "ffn_norm_w": gain(ks[3], (DEPTH, D_MODEL)),
        "mix_w_in": nrm(ks[4], (N_EVEN, D_MODEL, IN_DIM), D_MODEL ** -0.5),
        "conv_a_w": nrm(ks[5], (N_EVEN, CONV_A_WIDTH, D_CONV), CONV_A_WIDTH ** -0.5),
        "dn_conv_w": nrm(ks[6], (N_EVEN, DN_CONV_WIDTH, 3 * DN_DIM), DN_CONV_WIDTH ** -0.5),
        "dn_a_log": jnp.log(jax.random.uniform(ks[8], (N_EVEN, DN_HEADS), jnp.float32, 1.0, 16.0)),
        "dn_dt_bias": dt + jnp.log(-jnp.expm1(-dt)),
        "dn_norm_w": gain(ks[9], (N_EVEN, DN_HEAD_DIM)),
        "mix_w_out": nrm(ks[10], (N_EVEN, D_MIX, D_MODEL), D_MIX ** -0.5),
        "swa_wq": nrm(ks[11], (N_ODD, D_MODEL, SWA_HEADS * SWA_HEAD_DIM), D_MODEL ** -0.5),
        "swa_wk": nrm(ks[12], (N_ODD, D_MODEL, SWA_KV_HEADS * SWA_HEAD_DIM), D_MODEL ** -0.5),
        "swa_wv": nrm(ks[13], (N_ODD, D_MODEL, SWA_KV_HEADS * SWA_HEAD_DIM), D_MODEL ** -0.5),
        "swa_q_norm_w": gain(ks[14], (N_ODD, SWA_HEAD_DIM)),
        "swa_k_norm_w": gain(ks[15], (N_ODD, SWA_HEAD_DIM)),
        "swa_sinks": nrm(ks[16], (N_ODD, SWA_HEADS), 0.5),
        "swa_wo": nrm(ks[17], (N_ODD, SWA_HEADS * SWA_HEAD_DIM, D_MODEL), (SWA_HEADS * SWA_HEAD_DIM) ** -0.5),
        "ffn_w_up": nrm(ks[18], (DEPTH, D_MODEL, 2 * D_FF), D_MODEL ** -0.5),
        "ffn_conv_w": nrm(ks[19], (DEPTH, FFN_CONV_WIDTH, D_FF), FFN_CONV_WIDTH ** -0.5),
        "ffn_w_down": nrm(ks[20], (DEPTH, D_FF, D_MODEL), D_FF ** -0.5),
    }


def _fwd_reference(x, meta_tokens, attn_norm_w, ffn_norm_w, mix_w_in, conv_a_w, dn_conv_w, dn_a_log,
              dn_dt_bias, dn_norm_w, mix_w_out, swa_wq, swa_wk, swa_wv, swa_q_norm_w, swa_k_norm_w,
              swa_sinks, swa_wo, ffn_w_up, ffn_conv_w, ffn_w_down):
    b = x.shape[0]
    meta = jnp.broadcast_to(meta_tokens[None].astype(x.dtype), (b, N_META, D_MODEL))
    h = jnp.concatenate([meta, x], 1)
    for layer in range(DEPTH):
        i = layer // 2
        hn = rms_norm(h, attn_norm_w[layer])
        if layer % 2 == 0:
            mix = even_mixer(hn, mix_w_in[i], conv_a_w[i], dn_conv_w[i], dn_a_log[i],
                             dn_dt_bias[i], dn_norm_w[i], mix_w_out[i])
        else:
            mix = swa_mixer(hn, swa_wq[i], swa_wk[i], swa_wv[i], swa_q_norm_w[i],
                            swa_k_norm_w[i], swa_sinks[i], swa_wo[i])
        h = h + mix.astype(h.dtype)
        ff = conv_ffn(rms_norm(h, ffn_norm_w[layer]), ffn_w_up[layer], ffn_conv_w[layer], ffn_w_down[layer])
        h = h + ff.astype(h.dtype)
    return h[:, N_META:]


import jax as _jax
import jax.numpy as _jnp

TWIN_FORMAT = 'train_step'
FWD_PARAMS = ['x', 'meta_tokens', 'attn_norm_w', 'ffn_norm_w', 'mix_w_in', 'conv_a_w', 'dn_conv_w', 'dn_a_log', 'dn_dt_bias', 'dn_norm_w', 'mix_w_out', 'swa_wq', 'swa_wk', 'swa_wv', 'swa_q_norm_w', 'swa_k_norm_w', 'swa_sinks', 'swa_wo', 'ffn_w_up', 'ffn_conv_w', 'ffn_w_down']
TWIN_WEIGHTS = ['meta_tokens', 'attn_norm_w', 'ffn_norm_w', 'mix_w_in', 'conv_a_w', 'dn_conv_w', 'dn_a_log', 'dn_dt_bias', 'dn_norm_w', 'mix_w_out', 'swa_wq', 'swa_wk', 'swa_wv', 'swa_q_norm_w', 'swa_k_norm_w', 'swa_sinks', 'swa_wo', 'ffn_w_up', 'ffn_conv_w', 'ffn_w_down']
TWIN_DIFF_INPUT = 'x'
TWIN_INPUTS = ['x', 'meta_tokens', 'attn_norm_w', 'ffn_norm_w', 'mix_w_in', 'conv_a_w', 'dn_conv_w', 'dn_a_log', 'dn_dt_bias', 'dn_norm_w', 'mix_w_out', 'swa_wq', 'swa_wk', 'swa_wv', 'swa_q_norm_w', 'swa_k_norm_w', 'swa_sinks', 'swa_wo', 'ffn_w_up', 'ffn_conv_w', 'ffn_w_down', 'loss_target', 'm_meta_tokens', 'm_attn_norm_w', 'm_ffn_norm_w', 'm_mix_w_in', 'm_conv_a_w', 'm_dn_conv_w', 'm_dn_a_log', 'm_dn_dt_bias', 'm_dn_norm_w', 'm_mix_w_out', 'm_swa_wq', 'm_swa_wk', 'm_swa_wv', 'm_swa_q_norm_w', 'm_swa_k_norm_w', 'm_swa_sinks', 'm_swa_wo', 'm_ffn_w_up', 'm_ffn_conv_w', 'm_ffn_w_down', 'v_meta_tokens', 'v_attn_norm_w', 'v_ffn_norm_w', 'v_mix_w_in', 'v_conv_a_w', 'v_dn_conv_w', 'v_dn_a_log', 'v_dn_dt_bias', 'v_dn_norm_w', 'v_mix_w_out', 'v_swa_wq', 'v_swa_wk', 'v_swa_wv', 'v_swa_q_norm_w', 'v_swa_k_norm_w', 'v_swa_sinks', 'v_swa_wo', 'v_ffn_w_up', 'v_ffn_conv_w', 'v_ffn_w_down']
TWIN_OUTPUTS = ['loss', 'grad_x', 'grad_meta_tokens', 'grad_attn_norm_w', 'grad_ffn_norm_w', 'grad_mix_w_in', 'grad_conv_a_w', 'grad_dn_conv_w', 'grad_dn_a_log', 'grad_dn_dt_bias', 'grad_dn_norm_w', 'grad_mix_w_out', 'grad_swa_wq', 'grad_swa_wk', 'grad_swa_wv', 'grad_swa_q_norm_w', 'grad_swa_k_norm_w', 'grad_swa_sinks', 'grad_swa_wo', 'grad_ffn_w_up', 'grad_ffn_conv_w', 'grad_ffn_w_down', 'delta_meta_tokens', 'delta_attn_norm_w', 'delta_ffn_norm_w', 'delta_mix_w_in', 'delta_conv_a_w', 'delta_dn_conv_w', 'delta_dn_a_log', 'delta_dn_dt_bias', 'delta_dn_norm_w', 'delta_mix_w_out', 'delta_swa_wq', 'delta_swa_wk', 'delta_swa_wv', 'delta_swa_q_norm_w', 'delta_swa_k_norm_w', 'delta_swa_sinks', 'delta_swa_wo', 'delta_ffn_w_up', 'delta_ffn_conv_w', 'delta_ffn_w_down', 'new_m_meta_tokens', 'new_m_attn_norm_w', 'new_m_ffn_norm_w', 'new_m_mix_w_in', 'new_m_conv_a_w', 'new_m_dn_conv_w', 'new_m_dn_a_log', 'new_m_dn_dt_bias', 'new_m_dn_norm_w', 'new_m_mix_w_out', 'new_m_swa_wq', 'new_m_swa_wk', 'new_m_swa_wv', 'new_m_swa_q_norm_w', 'new_m_swa_k_norm_w', 'new_m_swa_sinks', 'new_m_swa_wo', 'new_m_ffn_w_up', 'new_m_ffn_conv_w', 'new_m_ffn_w_down', 'new_v_meta_tokens', 'new_v_attn_norm_w', 'new_v_ffn_norm_w', 'new_v_mix_w_in', 'new_v_conv_a_w', 'new_v_dn_conv_w', 'new_v_dn_a_log', 'new_v_dn_dt_bias', 'new_v_dn_norm_w', 'new_v_mix_w_out', 'new_v_swa_wq', 'new_v_swa_wk', 'new_v_swa_wv', 'new_v_swa_q_norm_w', 'new_v_swa_k_norm_w', 'new_v_swa_sinks', 'new_v_swa_wo', 'new_v_ffn_w_up', 'new_v_ffn_conv_w', 'new_v_ffn_w_down']
TWIN_LEAF_KINDS = {'loss': 'loss', 'grad_x': 'grad_x', 'grad_meta_tokens': 'grad_w', 'grad_attn_norm_w': 'grad_w', 'grad_ffn_norm_w': 'grad_w', 'grad_mix_w_in': 'grad_w', 'grad_conv_a_w': 'grad_w', 'grad_dn_conv_w': 'grad_w', 'grad_dn_a_log': 'grad_w', 'grad_dn_dt_bias': 'grad_w', 'grad_dn_norm_w': 'grad_w', 'grad_mix_w_out': 'grad_w', 'grad_swa_wq': 'grad_w', 'grad_swa_wk': 'grad_w', 'grad_swa_wv': 'grad_w', 'grad_swa_q_norm_w': 'grad_w', 'grad_swa_k_norm_w': 'grad_w', 'grad_swa_sinks': 'grad_w', 'grad_swa_wo': 'grad_w', 'grad_ffn_w_up': 'grad_w', 'grad_ffn_conv_w': 'grad_w', 'grad_ffn_w_down': 'grad_w', 'delta_meta_tokens': 'delta_w', 'delta_attn_norm_w': 'delta_w', 'delta_ffn_norm_w': 'delta_w', 'delta_mix_w_in': 'delta_w', 'delta_conv_a_w': 'delta_w', 'delta_dn_conv_w': 'delta_w', 'delta_dn_a_log': 'delta_w', 'delta_dn_dt_bias': 'delta_w', 'delta_dn_norm_w': 'delta_w', 'delta_mix_w_out': 'delta_w', 'delta_swa_wq': 'delta_w', 'delta_swa_wk': 'delta_w', 'delta_swa_wv': 'delta_w', 'delta_swa_q_norm_w': 'delta_w', 'delta_swa_k_norm_w': 'delta_w', 'delta_swa_sinks': 'delta_w', 'delta_swa_wo': 'delta_w', 'delta_ffn_w_up': 'delta_w', 'delta_ffn_conv_w': 'delta_w', 'delta_ffn_w_down': 'delta_w', 'new_m_meta_tokens': 'new_m', 'new_m_attn_norm_w': 'new_m', 'new_m_ffn_norm_w': 'new_m', 'new_m_mix_w_in': 'new_m', 'new_m_conv_a_w': 'new_m', 'new_m_dn_conv_w': 'new_m', 'new_m_dn_a_log': 'new_m', 'new_m_dn_dt_bias': 'new_m', 'new_m_dn_norm_w': 'new_m', 'new_m_mix_w_out': 'new_m', 'new_m_swa_wq': 'new_m', 'new_m_swa_wk': 'new_m', 'new_m_swa_wv': 'new_m', 'new_m_swa_q_norm_w': 'new_m', 'new_m_swa_k_norm_w': 'new_m', 'new_m_swa_sinks': 'new_m', 'new_m_swa_wo': 'new_m', 'new_m_ffn_w_up': 'new_m', 'new_m_ffn_conv_w': 'new_m', 'new_m_ffn_w_down': 'new_m', 'new_v_meta_tokens': 'new_v', 'new_v_attn_norm_w': 'new_v', 'new_v_ffn_norm_w': 'new_v', 'new_v_mix_w_in': 'new_v', 'new_v_conv_a_w': 'new_v', 'new_v_dn_conv_w': 'new_v', 'new_v_dn_a_log': 'new_v', 'new_v_dn_dt_bias': 'new_v', 'new_v_dn_norm_w': 'new_v', 'new_v_mix_w_out': 'new_v', 'new_v_swa_wq': 'new_v', 'new_v_swa_wk': 'new_v', 'new_v_swa_wv': 'new_v', 'new_v_swa_q_norm_w': 'new_v', 'new_v_swa_k_norm_w': 'new_v', 'new_v_swa_sinks': 'new_v', 'new_v_swa_wo': 'new_v', 'new_v_ffn_w_up': 'new_v', 'new_v_ffn_conv_w': 'new_v', 'new_v_ffn_w_down': 'new_v'}


def _forward(args):
    return _fwd_reference(*[args[k] for k in FWD_PARAMS])


def _output_shape():
    out = _jax.eval_shape(lambda: _forward(_fwd_setup_inputs(0)))
    return out.shape, out.dtype

N_MICROBATCH = 1
ADAM_LR = 0.001
ADAM_B1 = 0.9
ADAM_B2 = 0.999
ADAM_EPS = 1e-08
ADAM_WD = 0.01
ADAM_STEP = 10
PER_EXAMPLE_BATCH_AXIS = {'x': 0, 'loss_target': 0}
SHARED_INPUTS = []
_WEIGHT_DTYPES = {'meta_tokens': _jnp.float32, 'attn_norm_w': _jnp.float32, 'ffn_norm_w': _jnp.float32, 'mix_w_in': _jnp.float32, 'conv_a_w': _jnp.float32, 'dn_conv_w': _jnp.float32, 'dn_a_log': _jnp.float32, 'dn_dt_bias': _jnp.float32, 'dn_norm_w': _jnp.float32, 'mix_w_out': _jnp.float32, 'swa_wq': _jnp.float32, 'swa_wk': _jnp.float32, 'swa_wv': _jnp.float32, 'swa_q_norm_w': _jnp.float32, 'swa_k_norm_w': _jnp.float32, 'swa_sinks': _jnp.float32, 'swa_wo': _jnp.float32, 'ffn_w_up': _jnp.float32, 'ffn_conv_w': _jnp.float32, 'ffn_w_down': _jnp.float32}
MOMENT_SCALE = {'meta_tokens': 4.982307e-02, 'attn_norm_w': 3.895469e+01, 'ffn_norm_w': 2.542662e+01, 'mix_w_in': 7.803366e-01, 'conv_a_w': 1.806454e+01, 'dn_conv_w': 6.164966e-01, 'dn_a_log': 6.089575e+00, 'dn_dt_bias': 5.981374e+00, 'dn_norm_w': 4.951207e+01, 'mix_w_out': 1.168904e+00, 'swa_wq': 7.052724e-02, 'swa_wk': 1.428368e-01, 'swa_wv': 8.677013e-01, 'swa_q_norm_w': 6.045185e+00, 'swa_k_norm_w': 6.085933e+00, 'swa_sinks': 1.756465e-01, 'swa_wo': 3.435217e-01, 'ffn_w_up': 2.951994e-01, 'ffn_conv_w': 2.857182e+00, 'ffn_w_down': 3.864860e-01}


def _to_microbatches(a, axis):
    t = _jnp.moveaxis(a, axis, 0)
    t = t.reshape((N_MICROBATCH, t.shape[0] // N_MICROBATCH) + t.shape[1:])
    return _jnp.moveaxis(t, 1, axis + 1)


def setup_inputs(seed: int = 0) -> dict:
    inp = _fwd_setup_inputs(seed)
    key = _jax.random.fold_in(_jax.random.key(seed), 7919)
    shape, _ = _output_shape()
    out = dict(inp)
    out["loss_target"] = _jax.random.normal(_jax.random.fold_in(key, 0), shape, _jnp.float32)
    for i, name in enumerate(TWIN_WEIGHTS):
        w = inp[name].astype(_jnp.float32)
        if MOMENT_SCALE is None:
            s = _jnp.sqrt(_jnp.mean(_jnp.square(w)) + 1e-30)
        else:
            s = MOMENT_SCALE[name]
        km, kv = _jax.random.split(_jax.random.fold_in(key, i + 1))
        out[name] = w
        out["m_" + name] = s * _jax.random.normal(km, w.shape, _jnp.float32)
        out["v_" + name] = (s * s) * _jax.random.uniform(kv, w.shape, _jnp.float32, 0.5, 1.5)
    if N_MICROBATCH > 1:
        for name, axis in PER_EXAMPLE_BATCH_AXIS.items():
            out[name] = _to_microbatches(out[name], axis)
    return {'x': out['x'], 'meta_tokens': out['meta_tokens'], 'attn_norm_w': out['attn_norm_w'], 'ffn_norm_w': out['ffn_norm_w'], 'mix_w_in': out['mix_w_in'], 'conv_a_w': out['conv_a_w'], 'dn_conv_w': out['dn_conv_w'], 'dn_a_log': out['dn_a_log'], 'dn_dt_bias': out['dn_dt_bias'], 'dn_norm_w': out['dn_norm_w'], 'mix_w_out': out['mix_w_out'], 'swa_wq': out['swa_wq'], 'swa_wk': out['swa_wk'], 'swa_wv': out['swa_wv'], 'swa_q_norm_w': out['swa_q_norm_w'], 'swa_k_norm_w': out['swa_k_norm_w'], 'swa_sinks': out['swa_sinks'], 'swa_wo': out['swa_wo'], 'ffn_w_up': out['ffn_w_up'], 'ffn_conv_w': out['ffn_conv_w'], 'ffn_w_down': out['ffn_w_down'], 'loss_target': out['loss_target'], 'm_meta_tokens': out['m_meta_tokens'], 'm_attn_norm_w': out['m_attn_norm_w'], 'm_ffn_norm_w': out['m_ffn_norm_w'], 'm_mix_w_in': out['m_mix_w_in'], 'm_conv_a_w': out['m_conv_a_w'], 'm_dn_conv_w': out['m_dn_conv_w'], 'm_dn_a_log': out['m_dn_a_log'], 'm_dn_dt_bias': out['m_dn_dt_bias'], 'm_dn_norm_w': out['m_dn_norm_w'], 'm_mix_w_out': out['m_mix_w_out'], 'm_swa_wq': out['m_swa_wq'], 'm_swa_wk': out['m_swa_wk'], 'm_swa_wv': out['m_swa_wv'], 'm_swa_q_norm_w': out['m_swa_q_norm_w'], 'm_swa_k_norm_w': out['m_swa_k_norm_w'], 'm_swa_sinks': out['m_swa_sinks'], 'm_swa_wo': out['m_swa_wo'], 'm_ffn_w_up': out['m_ffn_w_up'], 'm_ffn_conv_w': out['m_ffn_conv_w'], 'm_ffn_w_down': out['m_ffn_w_down'], 'v_meta_tokens': out['v_meta_tokens'], 'v_attn_norm_w': out['v_attn_norm_w'], 'v_ffn_norm_w': out['v_ffn_norm_w'], 'v_mix_w_in': out['v_mix_w_in'], 'v_conv_a_w': out['v_conv_a_w'], 'v_dn_conv_w': out['v_dn_conv_w'], 'v_dn_a_log': out['v_dn_a_log'], 'v_dn_dt_bias': out['v_dn_dt_bias'], 'v_dn_norm_w': out['v_dn_norm_w'], 'v_mix_w_out': out['v_mix_w_out'], 'v_swa_wq': out['v_swa_wq'], 'v_swa_wk': out['v_swa_wk'], 'v_swa_wv': out['v_swa_wv'], 'v_swa_q_norm_w': out['v_swa_q_norm_w'], 'v_swa_k_norm_w': out['v_swa_k_norm_w'], 'v_swa_sinks': out['v_swa_sinks'], 'v_swa_wo': out['v_swa_wo'], 'v_ffn_w_up': out['v_ffn_w_up'], 'v_ffn_conv_w': out['v_ffn_conv_w'], 'v_ffn_w_down': out['v_ffn_w_down']}


def _loss(weights, diff, rest, loss_target):
    with _jax.named_scope("forward"):
        args = {**rest, TWIN_DIFF_INPUT: diff, **{k: w.astype(_WEIGHT_DTYPES[k]) for k, w in weights.items()}}
        y = _forward(args)
    with _jax.named_scope("loss_head"):
        err = _jnp.square(y.astype(_jnp.float32) - loss_target)
        return 0.5 * _jnp.sum(_jnp.mean(err, axis=-1)) if err.ndim else 0.5 * err


def _adamw(w, g, m, v):
    m = ADAM_B1 * m + (1.0 - ADAM_B1) * g
    v = ADAM_B2 * v + (1.0 - ADAM_B2) * _jnp.square(g)
    m_hat = m / (1.0 - ADAM_B1 ** ADAM_STEP)
    v_hat = v / (1.0 - ADAM_B2 ** ADAM_STEP)
    delta = -ADAM_LR * (m_hat / (_jnp.sqrt(v_hat) + ADAM_EPS) + ADAM_WD * w)
    return delta, m, v


def reference(x, meta_tokens, attn_norm_w, ffn_norm_w, mix_w_in, conv_a_w, dn_conv_w, dn_a_log, dn_dt_bias, dn_norm_w, mix_w_out, swa_wq, swa_wk, swa_wv, swa_q_norm_w, swa_k_norm_w, swa_sinks, swa_wo, ffn_w_up, ffn_conv_w, ffn_w_down, loss_target, m_meta_tokens, m_attn_norm_w, m_ffn_norm_w, m_mix_w_in, m_conv_a_w, m_dn_conv_w, m_dn_a_log, m_dn_dt_bias, m_dn_norm_w, m_mix_w_out, m_swa_wq, m_swa_wk, m_swa_wv, m_swa_q_norm_w, m_swa_k_norm_w, m_swa_sinks, m_swa_wo, m_ffn_w_up, m_ffn_conv_w, m_ffn_w_down, v_meta_tokens, v_attn_norm_w, v_ffn_norm_w, v_mix_w_in, v_conv_a_w, v_dn_conv_w, v_dn_a_log, v_dn_dt_bias, v_dn_norm_w, v_mix_w_out, v_swa_wq, v_swa_wk, v_swa_wv, v_swa_q_norm_w, v_swa_k_norm_w, v_swa_sinks, v_swa_wo, v_ffn_w_up, v_ffn_conv_w, v_ffn_w_down):
    given = dict(x=x, meta_tokens=meta_tokens, attn_norm_w=attn_norm_w, ffn_norm_w=ffn_norm_w, mix_w_in=mix_w_in, conv_a_w=conv_a_w, dn_conv_w=dn_conv_w, dn_a_log=dn_a_log, dn_dt_bias=dn_dt_bias, dn_norm_w=dn_norm_w, mix_w_out=mix_w_out, swa_wq=swa_wq, swa_wk=swa_wk, swa_wv=swa_wv, swa_q_norm_w=swa_q_norm_w, swa_k_norm_w=swa_k_norm_w, swa_sinks=swa_sinks, swa_wo=swa_wo, ffn_w_up=ffn_w_up, ffn_conv_w=ffn_conv_w, ffn_w_down=ffn_w_down, loss_target=loss_target, m_meta_tokens=m_meta_tokens, m_attn_norm_w=m_attn_norm_w, m_ffn_norm_w=m_ffn_norm_w, m_mix_w_in=m_mix_w_in, m_conv_a_w=m_conv_a_w, m_dn_conv_w=m_dn_conv_w, m_dn_a_log=m_dn_a_log, m_dn_dt_bias=m_dn_dt_bias, m_dn_norm_w=m_dn_norm_w, m_mix_w_out=m_mix_w_out, m_swa_wq=m_swa_wq, m_swa_wk=m_swa_wk, m_swa_wv=m_swa_wv, m_swa_q_norm_w=m_swa_q_norm_w, m_swa_k_norm_w=m_swa_k_norm_w, m_swa_sinks=m_swa_sinks, m_swa_wo=m_swa_wo, m_ffn_w_up=m_ffn_w_up, m_ffn_conv_w=m_ffn_conv_w, m_ffn_w_down=m_ffn_w_down, v_meta_tokens=v_meta_tokens, v_attn_norm_w=v_attn_norm_w, v_ffn_norm_w=v_ffn_norm_w, v_mix_w_in=v_mix_w_in, v_conv_a_w=v_conv_a_w, v_dn_conv_w=v_dn_conv_w, v_dn_a_log=v_dn_a_log, v_dn_dt_bias=v_dn_dt_bias, v_dn_norm_w=v_dn_norm_w, v_mix_w_out=v_mix_w_out, v_swa_wq=v_swa_wq, v_swa_wk=v_swa_wk, v_swa_wv=v_swa_wv, v_swa_q_norm_w=v_swa_q_norm_w, v_swa_k_norm_w=v_swa_k_norm_w, v_swa_sinks=v_swa_sinks, v_swa_wo=v_swa_wo, v_ffn_w_up=v_ffn_w_up, v_ffn_conv_w=v_ffn_conv_w, v_ffn_w_down=v_ffn_w_down)
    weights = {n: given[n] for n in TWIN_WEIGHTS}
    shared = {n: given[n] for n in SHARED_INPUTS}
    per_example = {n: given[n] for n in ['x']}
    grad_fn = _jax.value_and_grad(_loss, argnums=(0, 1))

    def one_microbatch(ex, loss_target):
        ex = dict(ex)
        diff = ex.pop(TWIN_DIFF_INPUT)
        return grad_fn(weights, diff, {**shared, **ex}, loss_target)

    if N_MICROBATCH == 1:
        loss, (grad_w, grad_x) = one_microbatch(per_example, given["loss_target"])
    else:
        def body(carry, xs):
            loss_sum, grad_sum = carry
            l_k, (gw_k, gx_k) = one_microbatch(xs[0], xs[1])
            with _jax.named_scope("update"):
                return (loss_sum + l_k, _jax.tree.map(_jnp.add, grad_sum, gw_k)), gx_k

        init = (_jnp.zeros((), _jnp.float32), _jax.tree.map(_jnp.zeros_like, weights))
        (loss, grad_w), grad_x = _jax.lax.scan(body, init, (per_example, given["loss_target"]))
    with _jax.named_scope("update"):
        delta_w, new_m, new_v = {}, {}, {}
        for n in TWIN_WEIGHTS:
            delta_w[n], new_m[n], new_v[n] = _adamw(weights[n], grad_w[n], given["m_" + n], given["v_" + n])
    return (loss, grad_x, *[grad_w[n] for n in TWIN_WEIGHTS], *[delta_w[n] for n in TWIN_WEIGHTS],
            *[new_m[n] for n in TWIN_WEIGHTS], *[new_v[n] for n in TWIN_WEIGHTS])
```

```python
import functools

import jax
import jax.numpy as jnp
from jax import lax
from jax.experimental import pallas as pl
from jax.experimental.pallas import tpu as pltpu

F32 = jnp.float32
BF16 = jnp.bfloat16
HI = lax.Precision.HIGHEST
MESH = pl.DeviceIdType.MESH

D = 1024
N_META = 16
PAD = 112
HEAD0 = PAD + N_META
D_CONV = 512
DN_H = 4
DN_D = 128
DN_DIM = 512
CH = 64
IN_DIM = 3592
P_W = 3840
BG0 = 3584
SWA_H = 16
SWA_KV = 4
SWA_D = 64
BLK = 128
D_FF = 2816
EPS = 1e-6
LR, B1, B2, AEPS, WD, STEP = 0.001, 0.9, 0.999, 1e-08, 0.01, 10
VMEM_LIMIT = 48 * 1024 * 1024
R_BIG = 6144
R_HALF = R_BIG // 2
SV_ROWS = 48
SW_ROWS = 16


def _pick(n, cands):
    for c in cands:
        if n % c == 0:
            return c
    return n


def _params(sem=None):
    return pltpu.CompilerParams(dimension_semantics=sem, vmem_limit_bytes=VMEM_LIMIT)


def _dot(a, b, ca=1, cb=0, prec=None):
    return lax.dot_general(a, b, (((ca,), (cb,)), ((), ())), precision=prec,
                           preferred_element_type=F32)


def _sigmoid(x):
    return 1.0 / (1.0 + jnp.exp(-x))


def _silu(x):
    return x * _sigmoid(x)


def _dsilu(x):
    s = _sigmoid(x)
    return s * (1.0 + x * (1.0 - s))


def _softplus(x):
    return jnp.maximum(x, 0.0) + jnp.log(1.0 + jnp.exp(-jnp.abs(x)))


def mm(a, b, *, name, ta=False, tb=False, out_dtype=F32, add=None, tm=None, tn=None, tk=None):
    m, k = (a.shape[1], a.shape[0]) if ta else a.shape
    n = b.shape[0] if tb else b.shape[1]
    tm = tm or (_pick(m, (1408, 512, 384, 256, 128)) if ta else _pick(m, (704, 512, 384, 256, 128)))
    tn = tn or _pick(n, (1408, 1024, 768, 512, 256, 128))
    tk = tk or (_pick(k, (704, 384, 128)) if ta else _pick(k, (1024, 1408, 768, 512, 128)))
    nk = k // tk
    dims = (((0 if ta else 1,), (1 if tb else 0,)), ((), ()))

    def body(*refs):
        if add is None:
            a_ref, b_ref, o_ref, acc_ref = refs
            add_ref = None
        else:
            a_ref, b_ref, add_ref, o_ref, acc_ref = refs
        part = lax.dot_general(a_ref[...].astype(BF16), b_ref[...].astype(BF16), dims,
                               preferred_element_type=F32)

        def finish(total):
            if add_ref is not None:
                total = total + add_ref[...]
            o_ref[...] = total.astype(out_dtype)

        if nk == 1:
            finish(part)
        else:
            kk = pl.program_id(2)

            @pl.when(kk == 0)
            def _():
                acc_ref[...] = part

            @pl.when(kk > 0)
            def _():
                acc_ref[...] += part

            @pl.when(kk == nk - 1)
            def _():
                finish(acc_ref[...])

    a_spec = pl.BlockSpec((tk, tm), lambda i, j, kk: (kk, i)) if ta else pl.BlockSpec((tm, tk), lambda i, j, kk: (i, kk))
    b_spec = pl.BlockSpec((tn, tk), lambda i, j, kk: (j, kk)) if tb else pl.BlockSpec((tk, tn), lambda i, j, kk: (kk, j))
    o_spec = pl.BlockSpec((tm, tn), lambda i, j, kk: (i, j))
    in_specs = [a_spec, b_spec] + ([o_spec] if add is not None else [])
    args = [a, b] + ([add] if add is not None else [])
    return pl.pallas_call(
        body, name=name, interpret=False,
        out_shape=jax.ShapeDtypeStruct((m, n), out_dtype),
        grid=(m // tm, n // tn, nk), in_specs=in_specs, out_specs=o_spec,
        scratch_shapes=[pltpu.VMEM((tm, tn) if nk > 1 else (8, 128), F32)],
        compiler_params=_params(("parallel", "parallel", "arbitrary")),
    )(*args)


def cols(arr, tr, width=None, cb=0):
    width = width or arr.shape[1]
    return (arr, (tr, width), lambda i: (i, cb))


def heads(arr, tr):
    return (arr, (arr.shape[0], tr, arr.shape[2]), lambda i: (0, i, 0))


def whole(arr):
    nd = arr.ndim
    return (arr, arr.shape, lambda i: (0,) * nd)


def rowwise(fn, ins, outs, *, steps, name, accs=()):
    n_in, n_out, n_acc = len(ins), len(outs), len(accs)

    def body(*refs):
        i = pl.program_id(0)
        res = fn(i, *[r[...] for r in refs[:n_in]])
        if not isinstance(res, (tuple, list)):
            res = (res,)
        for r, v in zip(refs[n_in:n_in + n_out], res[:n_out]):
            r[...] = v.astype(r.dtype)
        if n_acc:
            acc_refs = refs[n_in + n_out:]

            @pl.when(i == 0)
            def _():
                for r in acc_refs:
                    r[...] = jnp.zeros(r.shape, r.dtype)

            for r, v in zip(acc_refs, res[n_out:]):
                r[...] += jnp.broadcast_to(v, r.shape).astype(r.dtype)

    def zmap(nd):
        return lambda i: (0,) * nd

    in_specs = [pl.BlockSpec(bs, im) for _, bs, im in ins]
    out_specs = [pl.BlockSpec(bs, im) for _, _, bs, im in outs]
    out_specs += [pl.BlockSpec(s, zmap(len(s))) for s, _ in accs]
    out_shape = [jax.ShapeDtypeStruct(s, d) for s, d, _, _ in outs]
    out_shape += [jax.ShapeDtypeStruct(s, d) for s, d in accs]
    res = pl.pallas_call(
        body, name=name, interpret=False, out_shape=out_shape, grid=(steps,),
        in_specs=in_specs, out_specs=out_specs,
        compiler_params=_params(("arbitrary",)),
    )(*[a for a, _, _ in ins])
    return res


def out2d(rows, width, dtype, tr):
    return ((rows, width), dtype, (tr, width), lambda i: (i, 0))


def conv_fwd(xs, w8, kw, *, rows, c, tc, tr, name, post, extras=(), outs=(), pre=None):
    nx, ne, no = len(xs), len(extras), len(outs)
    nr, nc = rows // tr, c // tc
    r8 = tr // 8

    def body(*refs):
        x_refs = refs[:2 * nx]
        w_ref = refs[2 * nx]
        e_refs = refs[2 * nx + 1:2 * nx + 1 + ne]
        o_refs = refs[2 * nx + 1 + ne:2 * nx + 1 + ne + no]
        scr = refs[-1]
        j, i = pl.program_id(0), pl.program_id(1)
        cur = [x_refs[2 * q][...].astype(F32) for q in range(nx)]
        halo = [x_refs[2 * q + 1][...].astype(F32) for q in range(nx)]
        x = pre(*cur) if pre else cur[0]
        h = pre(*halo) if pre else halo[0]
        scr[0:8, :] = jnp.where(i > 0, h, 0.0)
        scr[8:8 + tr, :] = x
        y = jnp.zeros((tr, tc), F32)
        for q in range(kw):
            s = kw - 1 - q
            y = y + w_ref[q:q + 1, :] * scr[8 - s:8 - s + tr, :]
        res = post(j, y, *[e[...] for e in e_refs])
        if not isinstance(res, (tuple, list)):
            res = (res,)
        for r, v in zip(o_refs, res):
            r[...] = v.astype(r.dtype)

    in_specs, args = [], []
    for arr, cb0 in xs:
        in_specs.append(pl.BlockSpec((tr, tc), lambda j, i, cb0=cb0: (i, cb0 + j)))
        in_specs.append(pl.BlockSpec((8, tc), lambda j, i, cb0=cb0: (jnp.maximum(i * r8 - 1, 0), cb0 + j)))
        args += [arr, arr]
    in_specs.append(pl.BlockSpec((8, tc), lambda j, i: (0, j)))
    args.append(w8)
    for arr, cb0 in extras:
        in_specs.append(pl.BlockSpec((tr, tc), lambda j, i, cb0=cb0: (i, cb0 + j)))
        args.append(arr)
    return pl.pallas_call(
        body, name=name, interpret=False,
        out_shape=[jax.ShapeDtypeStruct((rows, c), dt) for dt in outs],
        grid=(nc, nr), in_specs=in_specs,
        out_specs=[pl.BlockSpec((tr, tc), lambda j, i: (i, j)) for _ in outs],
        scratch_shapes=[pltpu.VMEM((tr + 8, tc), F32)],
        compiler_params=_params(("parallel", "arbitrary")),
    )(*args)


def conv_bwd(xs, w8, kw, dy, *, rows, c, tc, tr, name, post, extras=(), outs=(), pre=None):
    nx, ne, no = len(xs), len(extras), len(outs)
    nr, nc = rows // tr, c // tc
    r8 = tr // 8

    def body(*refs):
        x_refs = refs[:2 * nx]
        w_ref, dy_ref, dyn_ref = refs[2 * nx:2 * nx + 3]
        e_refs = refs[2 * nx + 3:2 * nx + 3 + ne]
        o_refs = refs[2 * nx + 3 + ne:2 * nx + 3 + ne + no]
        dw_ref = refs[2 * nx + 3 + ne + no]
        xscr, gscr = refs[-2], refs[-1]
        i = pl.program_id(1)
        cur = [x_refs[2 * q][...].astype(F32) for q in range(nx)]
        halo = [x_refs[2 * q + 1][...].astype(F32) for q in range(nx)]
        x = pre(*cur) if pre else cur[0]
        h = pre(*halo) if pre else halo[0]
        xscr[0:8, :] = jnp.where(i > 0, h, 0.0)
        xscr[8:8 + tr, :] = x
        g = dy_ref[...].astype(F32)
        gscr[0:tr, :] = g
        gscr[tr:tr + 8, :] = jnp.where(i < nr - 1, dyn_ref[...].astype(F32), 0.0)
        dx = jnp.zeros((tr, tc), F32)
        dws = []
        for q in range(kw):
            s = kw - 1 - q
            dx = dx + w_ref[q:q + 1, :] * gscr[s:s + tr, :]
            dws.append(jnp.sum(g * xscr[8 - s:8 - s + tr, :], axis=0, keepdims=True))
        dws.append(jnp.zeros((8 - kw, tc), F32))
        res = post(dx, *[e[...] for e in e_refs])
        if not isinstance(res, (tuple, list)):
            res = (res,)
        for r, v in zip(o_refs, res):
            r[...] = v.astype(r.dtype)

        @pl.when(i == 0)
        def _():
            dw_ref[...] = jnp.zeros((8, tc), F32)

        dw_ref[...] += jnp.concatenate(dws, axis=0)

    in_specs, args = [], []
    for arr, cb0 in xs:
        in_specs.append(pl.BlockSpec((tr, tc), lambda j, i, cb0=cb0: (i, cb0 + j)))
        in_specs.append(pl.BlockSpec((8, tc), lambda j, i, cb0=cb0: (jnp.maximum(i * r8 - 1, 0), cb0 + j)))
        args += [arr, arr]
    in_specs.append(pl.BlockSpec((8, tc), lambda j, i: (0, j)))
    in_specs.append(pl.BlockSpec((tr, tc), lambda j, i: (i, j)))
    in_specs.append(pl.BlockSpec((8, tc), lambda j, i: (jnp.minimum((i + 1) * r8, nr * r8 - 1), j)))
    args += [w8, dy, dy]
    for arr, cb0 in extras:
        in_specs.append(pl.BlockSpec((tr, tc), lambda j, i, cb0=cb0: (i, cb0 + j)))
        args.append(arr)
    return pl.pallas_call(
        body, name=name, interpret=False,
        out_shape=[jax.ShapeDtypeStruct((rows, c), dt) for dt in outs] + [jax.ShapeDtypeStruct((8, c), F32)],
        grid=(nc, nr), in_specs=in_specs,
        out_specs=[pl.BlockSpec((tr, tc), lambda j, i: (i, j)) for _ in outs] + [pl.BlockSpec((8, tc), lambda j, i: (0, j))],
        scratch_shapes=[pltpu.VMEM((tr + 8, tc), F32), pltpu.VMEM((tr + 8, tc), F32)],
        compiler_params=_params(("parallel", "arbitrary")),
    )(*args)


def rms_fwd(h, w, *, name):
    rows = h.shape[0]
    tr = _pick(rows, (384, 128))

    def fn(i, x, wv):
        r = lax.rsqrt(jnp.mean(x * x, axis=1, keepdims=True) + EPS)
        return x * r * wv

    return rowwise(fn, [cols(h, tr), whole(w)], [out2d(rows, D, BF16, tr)], steps=rows // tr, name=name)[0]


def rms_bwd(h, w, dy, dres, *, name):
    rows = h.shape[0]
    tr = _pick(rows, (384, 128))

    def fn(i, x, wv, g, dr):
        r = lax.rsqrt(jnp.mean(x * x, axis=1, keepdims=True) + EPS)
        xh = x * r
        gw = g * wv
        dx = r * (gw - xh * jnp.mean(gw * xh, axis=1, keepdims=True))
        row = i * tr + lax.broadcasted_iota(jnp.int32, (tr, 1), 0)
        return jnp.where(row >= PAD, dr + dx, 0.0), jnp.sum(g * xh, axis=0, keepdims=True)

    return rowwise(fn, [cols(h, tr), whole(w), cols(dy, tr), cols(dres, tr)], [out2d(rows, D, F32, tr)],
                   steps=rows // tr, name=name, accs=[((1, D), F32)])


def loss_grad(h, target):
    rows = h.shape[0]

    def fn(i, y, t):
        diff = jnp.where(i > 0, y - t, 0.0)
        part = jnp.sum(jnp.sum(diff * diff, axis=1, keepdims=True), axis=0, keepdims=True)
        return diff * (1.0 / D), part * (0.5 / D)

    tgt = (target, (BLK, D), lambda i: (jnp.maximum(i - 1, 0), 0))
    return rowwise(fn, [cols(h, BLK), tgt], [out2d(rows, D, F32, BLK)], steps=rows // BLK,
                   name="loss_grad", accs=[((1, 128), F32)])


def adamw(w, g, m, v, *, name):
    shape = w.shape
    w2, g2, m2, v2 = (t.reshape(-1, shape[-1]) for t in (w, g, m, v))
    rows, width = w2.shape
    tr = _pick(rows, (256, 176, 128, 64, 16, 8))

    def fn(i, wv, gv, mv, vv):
        mn = B1 * mv + (1.0 - B1) * gv
        vn = B2 * vv + (1.0 - B2) * gv * gv
        mh = mn / (1.0 - B1 ** STEP)
        vh = vn / (1.0 - B2 ** STEP)
        return -LR * (mh / (jnp.sqrt(vh) + AEPS) + WD * wv), mn, vn

    res = rowwise(fn, [cols(t, tr) for t in (w2, g2, m2, v2)], [out2d(rows, width, F32, tr)] * 3,
                  steps=rows // tr, name=name)
    return [r.reshape(shape) for r in res]


def _tri_consts():
    row = lax.broadcasted_iota(jnp.int32, (CH, CH), 0)
    col = lax.broadcasted_iota(jnp.int32, (CH, CH), 1)
    return row, col


def _tri_inv(a, blk, eye):
    d = functools.partial(_dot, prec=HI)
    ad = jnp.where(blk, a, 0.0)
    lo = a - ad
    a2 = d(ad, ad)
    a4 = d(a2, a2)
    a8 = d(a4, a4)
    dgi = d(d(d(eye - ad, eye + a2), eye + a4), eye + a8)
    n = d(dgi, lo)
    return d(d(eye - n, eye + d(n, n)), dgi)


def _dn_chunk(q, k, v, b_c, d_c, d_r, incl, strict):
    d = functools.partial(_dot, prec=HI)
    dm = jnp.where(incl, jnp.exp(jnp.where(incl, d_c - d_r, 0.0)), 0.0)
    kk = d(k, k, 1, 1)
    a = jnp.where(strict, b_c * kk * dm, 0.0)
    ed = jnp.exp(d_c)
    rhs = jnp.concatenate([v * b_c, k * (b_c * ed)], axis=1)
    qkr = d(q, k, 1, 1)
    d_last = d_c[CH - 1:CH, :]
    ekd = jnp.exp(d_last - d_c)
    gl = jnp.exp(d_last)
    return dm, kk, a, ed, rhs, qkr, ekd, gl


def dn_fwd(qkv_n, bgcol, bgrow):
    rows = qkv_n.shape[0]
    nch = rows // CH

    def body(q_ref, k_ref, v_ref, bc_ref, br_ref, o_ref, s_out, ti_out, s_scr):
        n = pl.program_id(0)

        @pl.when(n == 0)
        def _():
            s_scr[...] = jnp.zeros(s_scr.shape, F32)

        d = functools.partial(_dot, prec=HI)
        row, col = _tri_consts()
        incl, strict = row >= col, row > col
        blk = (row // 16) == (col // 16)
        eye = (row == col).astype(F32)
        bc = bc_ref[...]
        br = br_ref[0]
        dcol = d(incl.astype(F32), bc)
        drow = d(br, (row <= col).astype(F32))
        for h in range(DN_H):
            sl = slice(h * DN_D, (h + 1) * DN_D)
            q, k, v = q_ref[:, sl], k_ref[:, sl], v_ref[:, sl]
            b_c, d_c, d_r = bc[:, h:h + 1], dcol[:, 4 + h:5 + h], drow[4 + h:5 + h, :]
            dm, kk, a, ed, rhs, qkr, ekd, gl = _dn_chunk(q, k, v, b_c, d_c, d_r, incl, strict)
            tinv = _tri_inv(a, blk, eye)
            sol = d(tinv, rhs)
            u, w = sol[:, :DN_D], sol[:, DN_D:]
            s = s_scr[h]
            s_out[0, h] = s
            ti_out[0, h] = tinv
            v_new = u - d(w, s)
            o_ref[:, sl] = d(q * ed, s) + d(qkr * dm, v_new)
            s_scr[h] = gl * s + d(k * ekd, v_new, 0, 0)

    return pl.pallas_call(
        body, name="dn_fwd", interpret=False,
        out_shape=[jax.ShapeDtypeStruct((rows, DN_DIM), F32),
                   jax.ShapeDtypeStruct((nch, DN_H, DN_D, DN_D), F32),
                   jax.ShapeDtypeStruct((nch, DN_H, CH, CH), F32)],
        grid=(nch,),
        in_specs=[pl.BlockSpec((CH, DN_DIM), lambda n: (n, 0)),
                  pl.BlockSpec((CH, DN_DIM), lambda n: (n, 1)),
                  pl.BlockSpec((CH, DN_DIM), lambda n: (n, 2)),
                  pl.BlockSpec((CH, 128), lambda n: (n, 0)),
                  pl.BlockSpec((1, 8, CH), lambda n: (n, 0, 0))],
        out_specs=[pl.BlockSpec((CH, DN_DIM), lambda n: (n, 0)),
                   pl.BlockSpec((1, DN_H, DN_D, DN_D), lambda n: (n, 0, 0, 0)),
                   pl.BlockSpec((1, DN_H, CH, CH), lambda n: (n, 0, 0, 0))],
        scratch_shapes=[pltpu.VMEM((DN_H, DN_D, DN_D), F32)],
        compiler_params=_params(("arbitrary",)),
    )(qkv_n, qkv_n, qkv_n, bgcol, bgrow)


def dn_bwd(qkv_n, bgcol, bgrow, s_all, ti_all, do):
    rows = qkv_n.shape[0]
    nch = rows // CH

    def body(q_ref, k_ref, v_ref, bc_ref, br_ref, s_ref, ti_ref, do_ref, dq_ref, dk_ref, dv_ref, dbg_ref, ds_scr):
        n = pl.program_id(0)

        @pl.when(n == 0)
        def _():
            ds_scr[...] = jnp.zeros(ds_scr.shape, F32)

        d = functools.partial(_dot, prec=HI)
        row, col = _tri_consts()
        incl, strict = row >= col, row > col
        upper = (row <= col).astype(F32)
        bc = bc_ref[...]
        br = br_ref[0]
        dcol = d(incl.astype(F32), bc)
        drow = d(br, upper)
        lane = lax.broadcasted_iota(jnp.int32, (CH, 128), 1)
        rowi = lax.broadcasted_iota(jnp.int32, (CH, 1), 0)
        ones = jnp.ones((CH, 128), F32)
        dbeta_mat = jnp.zeros((CH, 128), F32)
        dd_mat = jnp.zeros((CH, 128), F32)
        rsum = lambda t: jnp.sum(t, axis=1, keepdims=True)
        for h in range(DN_H):
            sl = slice(h * DN_D, (h + 1) * DN_D)
            q, k, v = q_ref[:, sl], k_ref[:, sl], v_ref[:, sl]
            b_c, d_c, d_r = bc[:, h:h + 1], dcol[:, 4 + h:5 + h], drow[4 + h:5 + h, :]
            dm, kk, a, ed, rhs, qkr, ekd, gl = _dn_chunk(q, k, v, b_c, d_c, d_r, incl, strict)
            tinv = ti_ref[0, h]
            s = s_ref[0, h]
            ds_next = ds_scr[h]
            g_o = do_ref[:, sl]
            sol = d(tinv, rhs)
            u, w = sol[:, :DN_D], sol[:, DN_D:]
            qk = qkr * dm
            qd, kd = q * ed, k * ekd
            v_new = u - d(w, s)
            dv_new = d(qk, g_o, 0, 0) + d(kd, ds_next)
            dqk = d(g_o, v_new, 1, 1)
            dqd = d(g_o, s, 1, 1)
            dkd = d(v_new, ds_next, 1, 1)
            dgl = jnp.sum(rsum(s * ds_next), axis=0, keepdims=True)
            ds_scr[h] = d(qd, g_o, 0, 0) + gl * ds_next - d(w, dv_new, 0, 0)
            dsol = jnp.concatenate([dv_new, -d(dv_new, s, 1, 1)], axis=1)
            drhs = d(tinv, dsol, 0, 0)
            da = jnp.where(strict, -d(drhs, sol, 1, 1), 0.0)
            drhs_u, drhs_w = drhs[:, :DN_D], drhs[:, DN_D:]
            s2 = rsum(drhs_w * k)
            dbeta = rsum(drhs_u * v) + s2 * ed + rsum(da * kk * dm)
            dkk = da * b_c * dm
            dqkr = dqk * dm
            mmat = da * a + dqk * qk
            tmp = rsum(dkd * kd)
            dd = (s2 * b_c * ed + rsum(mmat) - d(mmat, ones, 0, 0)[:, :1] + rsum(dqd * qd) - tmp)
            dd_last = jnp.sum(tmp, axis=0, keepdims=True) + dgl * gl
            dd = dd + jnp.where(rowi == CH - 1, dd_last, 0.0)
            dq_ref[:, sl] = d(dqkr, k) + dqd * ed
            dk_ref[:, sl] = (drhs_w * (b_c * ed) + d(dkk, k) + d(dkk, k, 0, 0) + d(dqkr, q, 0, 0) + dkd * ekd)
            dv_ref[:, sl] = drhs_u * b_c
            dbeta_mat = dbeta_mat + jnp.where(lane == h, dbeta, 0.0)
            dd_mat = dd_mat + jnp.where(lane == 4 + h, dd, 0.0)
        dbg_ref[...] = dbeta_mat + d(upper, dd_mat)

    rev = lambda n: nch - 1 - n
    return pl.pallas_call(
        body, name="dn_bwd", interpret=False,
        out_shape=[jax.ShapeDtypeStruct((rows, DN_DIM), F32)] * 3 + [jax.ShapeDtypeStruct((rows, 128), F32)],
        grid=(nch,),
        in_specs=[pl.BlockSpec((CH, DN_DIM), lambda n: (rev(n), 0)),
                  pl.BlockSpec((CH, DN_DIM), lambda n: (rev(n), 1)),
                  pl.BlockSpec((CH, DN_DIM), lambda n: (rev(n), 2)),
                  pl.BlockSpec((CH, 128), lambda n: (rev(n), 0)),
                  pl.BlockSpec((1, 8, CH), lambda n: (rev(n), 0, 0)),
                  pl.BlockSpec((1, DN_H, DN_D, DN_D), lambda n: (rev(n), 0, 0, 0)),
                  pl.BlockSpec((1, DN_H, CH, CH), lambda n: (rev(n), 0, 0, 0)),
                  pl.BlockSpec((CH, DN_DIM), lambda n: (rev(n), 0))],
        out_specs=[pl.BlockSpec((CH, DN_DIM), lambda n: (rev(n), 0))] * 3 + [pl.BlockSpec((CH, 128), lambda n: (rev(n), 0))],
        scratch_shapes=[pltpu.VMEM((DN_H, DN_D, DN_D), F32)],
        compiler_params=_params(("arbitrary",)),
    )(qkv_n, qkv_n, qkv_n, bgcol, bgrow, s_all, ti_all, do)


def _swa_valid(n):
    r = lax.broadcasted_iota(jnp.int32, (BLK, 3 * BLK), 0)
    c3 = lax.broadcasted_iota(jnp.int32, (BLK, 3 * BLK), 1)
    c = c3 % BLK
    lo = jnp.where(c3 < BLK, PAD, jnp.where(c3 < 2 * BLK, r + 1 + jnp.where(n >= 2, 0, BLK), 0))
    hi = jnp.where(c3 < BLK, r + jnp.where(n >= 1, BLK, 0), jnp.where(c3 < 2 * BLK, BLK, r - jnp.where(n >= 1, 0, BLK)))
    return jnp.logical_and(c >= lo, c <= hi)


def _swa_probs(q, kcat, valid, sink):
    s = jnp.where(valid, _dot(q, kcat, 1, 1), -1e30)
    m = jnp.maximum(jnp.max(s, axis=1, keepdims=True), sink)
    e = jnp.where(valid, jnp.exp(s - m), 0.0)
    es = jnp.exp(sink - m)
    inv = 1.0 / (jnp.sum(e, axis=1, keepdims=True) + es)
    return e * inv, es * inv


def _swa_specs(hn_order):
    if hn_order:
        q = pl.BlockSpec((4, BLK, SWA_D), lambda h, n: (h, n, 0))
        km = pl.BlockSpec((1, BLK, SWA_D), lambda h, n: (h, 0, 0))
        kp = pl.BlockSpec((1, BLK, SWA_D), lambda h, n: (h, jnp.maximum(n - 1, 0), 0))
        kc = pl.BlockSpec((1, BLK, SWA_D), lambda h, n: (h, n, 0))
    else:
        q = pl.BlockSpec((4, BLK, SWA_D), lambda n, h: (h, n, 0))
        km = pl.BlockSpec((1, BLK, SWA_D), lambda n, h: (h, 0, 0))
        kp = pl.BlockSpec((1, BLK, SWA_D), lambda n, h: (h, jnp.maximum(n - 1, 0), 0))
        kc = pl.BlockSpec((1, BLK, SWA_D), lambda n, h: (h, n, 0))
    return [q, km, kp, kc, km, kp, kc]


def swa_fwd(qh, kh, vh, sinks):
    rows = qh.shape[1]
    nb = rows // BLK

    def body(q_ref, km, kp, kc, vm, vp, vc, sk_ref, o_ref):
        n, h = pl.program_id(0), pl.program_id(1)
        kcat = jnp.concatenate([km[0], kp[0], kc[0]], axis=0)
        vcat = jnp.concatenate([vm[0], vp[0], vc[0]], axis=0)
        valid = _swa_valid(n)
        outs = []
        for g in range(4):
            p, _ = _swa_probs(q_ref[g], kcat, valid, sk_ref[4 * h + g])
            outs.append(_dot(p.astype(BF16), vcat))
        o_ref[...] = jnp.concatenate(outs, axis=1).astype(BF16)

    return pl.pallas_call(
        body, name="swa_fwd", interpret=False,
        out_shape=jax.ShapeDtypeStruct((rows, SWA_H * SWA_D), BF16),
        grid=(nb, SWA_KV),
        in_specs=_swa_specs(False) + [pl.BlockSpec(memory_space=pltpu.SMEM)],
        out_specs=pl.BlockSpec((BLK, 4 * SWA_D), lambda n, h: (n, h)),
        compiler_params=_params(("parallel", "parallel")),
    )(qh, kh, kh, kh, vh, vh, vh, sinks)


def swa_bwd(qh, kh, vh, sinks, do):
    rows = qh.shape[1]
    nb = rows // BLK

    def body(q_ref, km, kp, kc, vm, vp, vc, do_ref, sk_ref, dq_ref, dk_ref, dv_ref, dsk_ref):
        h, n = pl.program_id(0), pl.program_id(1)

        @pl.when(n == 0)
        def _():
            dk_ref[...] = jnp.zeros(dk_ref.shape, F32)
            dv_ref[...] = jnp.zeros(dv_ref.shape, F32)

        kcat = jnp.concatenate([km[0], kp[0], kc[0]], axis=0)
        vcat = jnp.concatenate([vm[0], vp[0], vc[0]], axis=0)
        valid = _swa_valid(n)
        g_all = do_ref[...]
        dkc = jnp.zeros((3 * BLK, SWA_D), F32)
        dvc = jnp.zeros((3 * BLK, SWA_D), F32)
        rowi = lax.broadcasted_iota(jnp.int32, (8, 128), 0)
        dsk = jnp.zeros((8, 128), F32)
        for g in range(4):
            q = q_ref[g]
            p, ps = _swa_probs(q, kcat, valid, sk_ref[4 * h + g])
            g_o = g_all[:, g * SWA_D:(g + 1) * SWA_D]
            dp = _dot(g_o, vcat, 1, 1)
            delta = jnp.sum(p * dp, axis=1, keepdims=True)
            ds = (p * (dp - delta)).astype(BF16)
            dq_ref[g] = _dot(ds, kcat)
            dkc = dkc + _dot(ds, q, 0, 0)
            dvc = dvc + _dot(p.astype(BF16), g_o, 0, 0)
            dsk = dsk + jnp.where(rowi == g, -jnp.sum(ps * delta, axis=0, keepdims=True), 0.0)
        dsk_ref[0, 0] = dsk
        for ref, val in ((dk_ref, dkc), (dv_ref, dvc)):
            ref[0, 0:BLK, :] += val[0:BLK]
            pm = pl.multiple_of(jnp.maximum(n - 1, 0) * BLK, BLK)
            ref[0, pl.ds(pm, BLK), :] += val[BLK:2 * BLK]
            ref[0, pl.ds(pl.multiple_of(n * BLK, BLK), BLK), :] += val[2 * BLK:]

    return pl.pallas_call(
        body, name="swa_bwd", interpret=False,
        out_shape=[jax.ShapeDtypeStruct((SWA_H, rows, SWA_D), F32),
                   jax.ShapeDtypeStruct((SWA_KV, rows, SWA_D), F32),
                   jax.ShapeDtypeStruct((SWA_KV, rows, SWA_D), F32),
                   jax.ShapeDtypeStruct((SWA_KV, nb, 8, 128), F32)],
        grid=(SWA_KV, nb),
        in_specs=_swa_specs(True) + [pl.BlockSpec((BLK, 4 * SWA_D), lambda h, n: (n, h)),
                                     pl.BlockSpec(memory_space=pltpu.SMEM)],
        out_specs=[pl.BlockSpec((4, BLK, SWA_D), lambda h, n: (h, n, 0)),
                   pl.BlockSpec((1, rows, SWA_D), lambda h, n: (h, 0, 0)),
                   pl.BlockSpec((1, rows, SWA_D), lambda h, n: (h, 0, 0)),
                   pl.BlockSpec((1, 1, 8, 128), lambda h, n: (h, n, 0, 0))],
        compiler_params=_params(("arbitrary", "arbitrary")),
    )(qh, kh, kh, kh, vh, vh, vh, do, sinks)


def qknorm_fwd(qkv, qw, kw):
    rows = qkv.shape[0]
    tr = _pick(rows, (384, 128))
    scale = SWA_D ** -0.5

    def fn(i, x, qwv, kwv):
        def normed(j, wv, sc):
            xs = x[:, j * SWA_D:(j + 1) * SWA_D]
            r = lax.rsqrt(jnp.mean(xs * xs, axis=1, keepdims=True) + EPS)
            return (xs * r * wv * sc)[None]
        qo = jnp.concatenate([normed(j, qwv, scale) for j in range(SWA_H)], axis=0)
        ko = jnp.concatenate([normed(SWA_H + j, kwv, 1.0) for j in range(SWA_KV)], axis=0)
        vo = jnp.concatenate([x[:, (SWA_H + SWA_KV + j) * SWA_D:(SWA_H + SWA_KV + j + 1) * SWA_D][None]
                              for j in range(SWA_KV)], axis=0)
        return qo, ko, vo

    hm = lambda nh: ((nh, rows, SWA_D), BF16, (nh, tr, SWA_D), lambda i: (0, i, 0))
    return rowwise(fn, [cols(qkv, tr), whole(qw), whole(kw)], [hm(SWA_H), hm(SWA_KV), hm(SWA_KV)],
                   steps=rows // tr, name="qknorm_fwd")


def qknorm_bwd(qkv, qw, kw, dqh, dkh, dvh):
    rows = qkv.shape[0]
    tr = _pick(rows, (384, 128))
    scale = SWA_D ** -0.5

    def fn(i, x, qwv, kwv, dq, dk, dv):
        pieces = []
        dws = [jnp.zeros((1, SWA_D), F32), jnp.zeros((1, SWA_D), F32)]

        def one(j, dy, wv, sc, which):
            xs = x[:, j * SWA_D:(j + 1) * SWA_D]
            r = lax.rsqrt(jnp.mean(xs * xs, axis=1, keepdims=True) + EPS)
            xh = xs * r
            gw = dy * wv * sc
            pieces.append(r * (gw - xh * jnp.mean(gw * xh, axis=1, keepdims=True)))
            dws[which] = dws[which] + jnp.sum(dy * sc * xh, axis=0, keepdims=True)

        for j in range(SWA_H):
            one(j, dq[j], qwv, scale, 0)
        for j in range(SWA_KV):
            one(SWA_H + j, dk[j], kwv, 1.0, 1)
        for j in range(SWA_KV):
            pieces.append(dv[j])
        return jnp.concatenate(pieces, axis=1), dws[0], dws[1]

    return rowwise(fn, [cols(qkv, tr), whole(qw), whole(kw), heads(dqh, tr), heads(dkh, tr), heads(dvh, tr)],
                   [out2d(rows, 1536, BF16, tr)], steps=rows // tr, name="qknorm_bwd",
                   accs=[((1, SWA_D), F32), ((1, SWA_D), F32)])


def _place():
    return lax.axis_index("x"), lax.axis_index("y"), lax.axis_index("c")


ANY = pl.BlockSpec(memory_space=pl.ANY)


def gather_weights(big, small):
    def body(big_ref, small_ref, obig, osmall, ssem, rsem, lsem):
        x, y, c = _place()
        me = 2 * x + y
        chips = [(1 - x, y), (x, 1 - y), (1 - x, 1 - y)]

        def half(s, hh):
            return obig.at[s, pl.ds(hh * R_HALF, R_HALF), :]

        def rcopy(k, src, dst, to):
            return pltpu.make_async_remote_copy(src_ref=src, dst_ref=dst, send_sem=ssem.at[k], recv_sem=rsem.at[k],
                                                device_id=to, device_id_type=MESH)

        loc = [pltpu.make_async_copy(big_ref, obig.at[me], lsem.at[0]),
               pltpu.make_async_copy(small_ref, osmall.at[me], lsem.at[1])]
        for cp in loc:
            cp.start()
        sends = []
        for j, (px, py) in enumerate(chips):
            sends.append(rcopy(j, big_ref.at[pl.ds(c * R_HALF, R_HALF), :], half(me, c), (px, py, c)))
            sends.append(rcopy(6 + j, small_ref, osmall.at[me], (px, py, c)))
        for cp in sends:
            cp.start()
        for j, (px, py) in enumerate(chips):
            s = 2 * px + py
            rcopy(j, half(s, c), half(s, c), (x, y, c)).wait_recv()
            fwd = rcopy(3 + j, half(s, c), half(s, c), (x, y, 1 - c))
            fwd.start()
            sends.append(fwd)
        for j, (px, py) in enumerate(chips):
            s = 2 * px + py
            rcopy(3 + j, half(s, 1 - c), half(s, 1 - c), (x, y, c)).wait_recv()
            rcopy(6 + j, osmall.at[s], osmall.at[s], (x, y, c)).wait_recv()
        for cp in sends:
            cp.wait_send()
        for cp in loc:
            cp.wait()

    return pl.pallas_call(
        body, name="gather_weights", interpret=False,
        out_shape=[jax.ShapeDtypeStruct((4, R_BIG, 1024), BF16), jax.ShapeDtypeStruct((4, SW_ROWS, 1024), F32)],
        in_specs=[ANY, ANY], out_specs=[ANY, ANY],
        scratch_shapes=[pltpu.SemaphoreType.DMA((9,)), pltpu.SemaphoreType.DMA((9,)), pltpu.SemaphoreType.DMA((2,))],
    )(big, small)


def swap_halves(g):
    def body(g_ref, o_ref, ssem, rsem):
        x, y, c = _place()
        cp = pltpu.make_async_remote_copy(
            src_ref=g_ref.at[:, pl.ds((1 - c) * R_HALF, R_HALF), :], dst_ref=o_ref,
            send_sem=ssem, recv_sem=rsem, device_id=(x, y, 1 - c), device_id_type=MESH)
        cp.start()
        cp.wait()

    return pl.pallas_call(
        body, name="swap_halves", interpret=False,
        out_shape=jax.ShapeDtypeStruct((4, R_HALF, 1024), BF16),
        in_specs=[ANY], out_specs=ANY,
        scratch_shapes=[pltpu.SemaphoreType.DMA, pltpu.SemaphoreType.DMA],
    )(g)


def pair_sum(g, other, c_idx):
    tr = 512
    nbk = R_HALF // tr

    def body(c_ref, g_ref, o_ref, out_ref):
        out_ref[...] = (g_ref[...].astype(F32) + o_ref[...].astype(F32)).astype(BF16)

    return pl.pallas_call(
        body, name="pair_sum", interpret=False,
        out_shape=jax.ShapeDtypeStruct((4, R_HALF, 1024), BF16),
        grid_spec=pltpu.PrefetchScalarGridSpec(
            num_scalar_prefetch=1, grid=(4, nbk),
            in_specs=[pl.BlockSpec((1, tr, 1024), lambda s, i, c_ref: (s, c_ref[0] * nbk + i, 0)),
                      pl.BlockSpec((1, tr, 1024), lambda s, i, c_ref: (s, i, 0))],
            out_specs=pl.BlockSpec((1, tr, 1024), lambda s, i, c_ref: (s, i, 0))),
        compiler_params=_params(("parallel", "parallel")),
    )(c_idx, g, other)


def scatter_chips(p):
    def body(p_ref, o_ref, ssem, rsem):
        x, y, c = _place()
        chips = [(1 - x, y), (x, 1 - y), (1 - x, 1 - y)]
        cps = [pltpu.make_async_remote_copy(src_ref=p_ref.at[2 * px + py], dst_ref=o_ref.at[j],
                                            send_sem=ssem.at[j], recv_sem=rsem.at[j],
                                            device_id=(px, py, c), device_id_type=MESH)
               for j, (px, py) in enumerate(chips)]
        for cp in cps:
            cp.start()
        for cp in cps:
            cp.wait()

    return pl.pallas_call(
        body, name="scatter_chips", interpret=False,
        out_shape=jax.ShapeDtypeStruct((3, R_HALF, 1024), BF16),
        in_specs=[ANY], out_specs=ANY,
        scratch_shapes=[pltpu.SemaphoreType.DMA((3,)), pltpu.SemaphoreType.DMA((3,))],
    )(p)


def chip_sum(p, got, me_idx):
    tr = 512
    nbk = R_HALF // tr

    def body(me_ref, p_ref, g_ref, out_ref):
        acc = p_ref[0].astype(F32)
        for j in range(3):
            acc = acc + g_ref[j].astype(F32)
        out_ref[...] = acc

    return pl.pallas_call(
        body, name="chip_sum", interpret=False,
        out_shape=jax.ShapeDtypeStruct((R_HALF, 1024), F32),
        grid_spec=pltpu.PrefetchScalarGridSpec(
            num_scalar_prefetch=1, grid=(nbk,),
            in_specs=[pl.BlockSpec((1, tr, 1024), lambda i, me_ref: (me_ref[0], i, 0)),
                      pl.BlockSpec((3, tr, 1024), lambda i, me_ref: (0, i, 0))],
            out_specs=pl.BlockSpec((tr, 1024), lambda i, me_ref: (i, 0))),
        compiler_params=_params(("parallel",)),
    )(me_idx, p, got)


def join_halves(q):
    def body(q_ref, o_ref, ssem, rsem, lsem):
        x, y, c = _place()
        loc = pltpu.make_async_copy(q_ref, o_ref.at[c], lsem)
        loc.start()
        cp = pltpu.make_async_remote_copy(src_ref=q_ref, dst_ref=o_ref.at[c], send_sem=ssem, recv_sem=rsem,
                                          device_id=(x, y, 1 - c), device_id_type=MESH)
        cp.start()
        pltpu.make_async_remote_copy(src_ref=q_ref, dst_ref=o_ref.at[1 - c], send_sem=ssem, recv_sem=rsem,
                                     device_id=(x, y, 1 - c), device_id_type=MESH).wait_recv()
        cp.wait_send()
        loc.wait()

    return pl.pallas_call(
        body, name="join_halves", interpret=False,
        out_shape=jax.ShapeDtypeStruct((2, R_HALF, 1024), F32),
        in_specs=[ANY], out_specs=ANY,
        scratch_shapes=[pltpu.SemaphoreType.DMA, pltpu.SemaphoreType.DMA, pltpu.SemaphoreType.DMA],
    )(q)


def gather_small(v):
    def body(v_ref, o_ref, ssem, rsem, lsem):
        x, y, c = _place()
        loc = pltpu.make_async_copy(v_ref, o_ref.at[4 * x + 2 * y + c], lsem)
        loc.start()
        cps = []
        for k in range(1, 8):
            fx, fy, fc = (k >> 2) & 1, (k >> 1) & 1, k & 1
            px = 1 - x if fx else x
            py = 1 - y if fy else y
            pc = 1 - c if fc else c
            cps.append((pltpu.make_async_remote_copy(
                src_ref=v_ref, dst_ref=o_ref.at[4 * x + 2 * y + c], send_sem=ssem.at[k - 1], recv_sem=rsem.at[k - 1],
                device_id=(px, py, pc), device_id_type=MESH), 4 * px + 2 * py + pc))
        for cp, _ in cps:
            cp.start()
        for k, (cp, peer) in enumerate(cps):
            pltpu.make_async_remote_copy(
                src_ref=v_ref, dst_ref=o_ref.at[peer], send_sem=ssem.at[k], recv_sem=rsem.at[k],
                device_id=(x, y, c), device_id_type=MESH).wait_recv()
        for cp, _ in cps:
            cp.wait_send()
        loc.wait()

    return pl.pallas_call(
        body, name="gather_small", interpret=False,
        out_shape=jax.ShapeDtypeStruct((8, SV_ROWS, 1024), F32),
        in_specs=[ANY], out_specs=ANY,
        scratch_shapes=[pltpu.SemaphoreType.DMA((7,)), pltpu.SemaphoreType.DMA((7,)), pltpu.SemaphoreType.DMA],
    )(v)


def sum_slots(a):
    def fn(i, t):
        acc = t[0]
        for k in range(1, 8):
            acc = acc + t[k]
        return acc

    return rowwise(fn, [whole(a)], [((SV_ROWS, 1024), F32, (SV_ROWS, 1024), lambda i: (0, 0))], steps=1,
                   name="sum_slots")[0]


def _head_rms(x, nw):
    xs, rs = [], []
    for h in range(DN_H):
        xh = x[:, h * DN_D:(h + 1) * DN_D]
        r = lax.rsqrt(jnp.mean(xh * xh, axis=1, keepdims=True) + EPS)
        xs.append(xh * r)
        rs.append(r)
    return xs, rs


def bg_fwd(p, alog, dtb):
    rows = p.shape[0]
    tr = _pick(rows, (384, 128))

    def fn(i, x, al, dt):
        lane = lax.broadcasted_iota(jnp.int32, (tr, 128), 1)
        row = i * tr + lax.broadcasted_iota(jnp.int32, (tr, 128), 0)
        g = -jnp.exp(al) * _softplus(x + dt)
        out = jnp.where(lane < 4, _sigmoid(x), jnp.where(lane < 8, g, 0.0))
        return jnp.where(row >= PAD, out, 0.0)

    return rowwise(fn, [cols(p, tr, 128, BG0 // 128), whole(alog), whole(dtb)], [out2d(rows, 128, F32, tr)],
                   steps=rows // tr, name="bg_fwd")[0]


def bg_bwd(p, alog, dtb, dbg):
    rows = p.shape[0]
    tr = _pick(rows, (384, 128))

    def fn(i, x, al, dt, g_in):
        lane = lax.broadcasted_iota(jnp.int32, (tr, 128), 1)
        row = i * tr + lax.broadcasted_iota(jnp.int32, (tr, 128), 0)
        live = row >= PAD
        is_b = jnp.logical_and(live, lane < 4)
        is_g = jnp.logical_and(live, jnp.logical_and(lane >= 4, lane < 8))
        beta = _sigmoid(x)
        ea = jnp.exp(al)
        g = -ea * _softplus(x + dt)
        dalpha = jnp.where(is_g, g_in * (-ea) * _sigmoid(x + dt), 0.0)
        dx = jnp.where(is_b, g_in * beta * (1.0 - beta), dalpha)
        dal = jnp.sum(jnp.where(is_g, g_in * g, 0.0), axis=0, keepdims=True)
        return jnp.concatenate([dx, jnp.zeros((tr, 128), F32)], axis=1), dal, jnp.sum(dalpha, axis=0, keepdims=True)

    return rowwise(fn, [cols(p, tr, 128, BG0 // 128), whole(alog), whole(dtb), cols(dbg, tr)],
                   [out2d(rows, 256, BF16, tr)], steps=rows // tr, name="bg_bwd",
                   accs=[((1, 128), F32), ((1, 128), F32)])


def dn_qkv_post(j, y):
    xs = _silu(y)
    sc = jnp.where(j == 0, DN_D ** -0.5, 1.0)
    outs = []
    for h in range(DN_H):
        xh = xs[:, h * DN_D:(h + 1) * DN_D]
        r = lax.rsqrt(jnp.sum(xh * xh, axis=1, keepdims=True) + EPS)
        outs.append(jnp.where(j < 2, xh * r * sc, xh))
    return jnp.concatenate(outs, axis=1), y


def dn_qkv_bwd(cq, dq, dk, dv):
    rows = cq.shape[0]
    tr = _pick(rows, (384, 128))

    def fn(i, c0, c1, c2, g0, g1, g2):
        pieces = []
        for kind, (cv, g) in enumerate(((c0, g0), (c1, g1), (c2, g2))):
            xs = _silu(cv)
            if kind < 2:
                sc = DN_D ** -0.5 if kind == 0 else 1.0
                ds = []
                for h in range(DN_H):
                    sl = slice(h * DN_D, (h + 1) * DN_D)
                    xh, gh = xs[:, sl], g[:, sl]
                    r = lax.rsqrt(jnp.sum(xh * xh, axis=1, keepdims=True) + EPS)
                    xn = xh * r
                    ds.append(sc * r * (gh - xn * jnp.sum(gh * xn, axis=1, keepdims=True)))
                dxs = jnp.concatenate(ds, axis=1)
            else:
                dxs = g
            pieces.append(dxs * _dsilu(cv))
        return jnp.concatenate(pieces, axis=1)

    ins = [cols(cq, tr, DN_DIM, k) for k in range(3)] + [cols(t, tr) for t in (dq, dk, dv)]
    return rowwise(fn, ins, [out2d(rows, 3 * DN_DIM, F32, tr)], steps=rows // tr, name="dn_qkv_bwd")[0]


def dn_out_fwd(o, p, nw):
    rows = o.shape[0]
    tr = _pick(rows, (384, 128))

    def fn(i, ov, z, w):
        xs, _ = _head_rms(ov, w)
        return jnp.concatenate(xs, axis=1) * jnp.concatenate([w] * DN_H, axis=1) * _silu(z)

    return rowwise(fn, [cols(o, tr), cols(p, tr, DN_DIM, 6), whole(nw)], [out2d(rows, DN_DIM, BF16, tr)],
                   steps=rows // tr, name="dn_out_fwd")[0]


def dn_out_bwd(o, p, nw, dymix):
    rows = o.shape[0]
    tr = _pick(rows, (384, 128))

    def fn(i, ov, z, w, dy):
        xs, rs = _head_rms(ov, w)
        sz = _silu(z)
        dn = dy * sz
        dos, dw = [], jnp.zeros((1, DN_D), F32)
        for h in range(DN_H):
            sl = slice(h * DN_D, (h + 1) * DN_D)
            gw = dn[:, sl] * w
            dos.append(rs[h] * (gw - xs[h] * jnp.mean(gw * xs[h], axis=1, keepdims=True)))
            dw = dw + jnp.sum(dn[:, sl] * xs[h], axis=0, keepdims=True)
        n = jnp.concatenate(xs, axis=1) * jnp.concatenate([w] * DN_H, axis=1)
        return jnp.concatenate(dos, axis=1), dy * n * _dsilu(z), dw

    return rowwise(fn, [cols(o, tr), cols(p, tr, DN_DIM, 6), whole(nw), cols(dymix, tr, DN_DIM, 1)],
                   [out2d(rows, DN_DIM, F32, tr), out2d(rows, DN_DIM, BF16, tr)], steps=rows // tr,
                   name="dn_out_bwd", accs=[((1, DN_D), F32)])


def conv_a_pre_bwd(dymix, cv, p):
    rows = cv.shape[0]
    tr = _pick(rows, (384, 128))

    def fn(i, dy, c, go):
        return dy * c, dy * go

    return rowwise(fn, [cols(dymix, tr, D_CONV, 0), cols(cv, tr), cols(p, tr, D_CONV, 1)],
                   [out2d(rows, D_CONV, BF16, tr), out2d(rows, D_CONV, F32, tr)], steps=rows // tr,
                   name="conv_a_pre_bwd")


def ffn_act_bwd(da, gc, u):
    rows = da.shape[0]
    tr = _pick(rows, (384, 128))

    def fn(i, g, c, val):
        g, c, val = g.astype(F32), c.astype(F32), val.astype(F32)
        return g * _silu(c), g * val * _dsilu(c)

    return rowwise(fn, [cols(da, tr), cols(gc, tr), cols(u, tr, D_FF, 1)],
                   [out2d(rows, D_FF, BF16, tr), out2d(rows, D_FF, F32, tr)], steps=rows // tr, name="ffn_act_bwd")


def _rows8(w):
    return jnp.pad(w.astype(F32), ((0, 8 - w.shape[0]), (0, 0)))


def _lanes(v, at):
    return jnp.pad(v.astype(F32), (at, 128 - at - v.shape[0]))[None]


def ffn_fwd(h, nw, w_up, cw8, w_down, tag):
    rows = h.shape[0]
    tr = _pick(rows, (384, 128))
    hn = rms_fwd(h, nw, name=f"ffn{tag}_norm")
    u = mm(hn, w_up, out_dtype=BF16, name=f"ffn{tag}_up")
    a, gc = conv_fwd([(u, 0)], cw8, 3, rows=rows, c=D_FF, tc=1408, tr=tr, name=f"ffn{tag}_conv",
                     post=lambda j, y, val: (_silu(y) * val.astype(F32), y), extras=[(u, 2)], outs=[BF16, BF16])
    out = mm(a, w_down, add=h, name=f"ffn{tag}_down")
    return out, (hn, u, a, gc)


def ffn_bwd(h, nw, w_up, cw8, w_down, saved, dh, tag):
    hn, u, a, gc = saved
    rows = h.shape[0]
    tr = _pick(rows, (384, 128))
    da = mm(dh, w_down, tb=True, out_dtype=BF16, name=f"ffn{tag}_down_dx")
    d_w_down = mm(a, dh, ta=True, name=f"ffn{tag}_down_dw")
    dval, dgc = ffn_act_bwd(da, gc, u)
    dgate, d_cw = conv_bwd([(u, 0)], cw8, 3, dgc, rows=rows, c=D_FF, tc=1408, tr=tr, name=f"ffn{tag}_conv_bwd",
                           post=lambda dx: dx, outs=[BF16])
    du = jnp.concatenate([dgate, dval], axis=1)
    dhn = mm(du, w_up, tb=True, name=f"ffn{tag}_up_dx")
    d_w_up = mm(hn, du, ta=True, name=f"ffn{tag}_up_dw")
    dh_new, d_nw = rms_bwd(h, nw, dhn, dh, name=f"ffn{tag}_norm_bwd")
    return dh_new, d_nw, d_w_up, d_cw, d_w_down


def mixer_fwd(h, nw, w_in, ca8, dc8, alog, dtb, dnw, w_out):
    rows = h.shape[0]
    tr = _pick(rows, (384, 128))
    hn = rms_fwd(h, nw, name="mix_norm")
    p = mm(hn, w_in, name="mix_in")
    y_a, cv = conv_fwd([(p, 0), (p, 2)], ca8, 3, rows=rows, c=D_CONV, tc=D_CONV, tr=tr, name="conv_a",
                       pre=lambda gi, ah: gi * ah, post=lambda j, y, go: (go * y, y), extras=[(p, 1)],
                       outs=[BF16, F32])
    qkv_n, cq = conv_fwd([(p, 3)], dc8, 4, rows=rows, c=3 * DN_DIM, tc=DN_DIM, tr=tr, name="dn_conv",
                         post=dn_qkv_post, outs=[F32, F32])
    bgcol = bg_fwd(p, alog, dtb)
    bgrow = bgcol[:, :8].reshape(rows // CH, CH, 8).transpose(0, 2, 1)
    o, s_all, ti_all = dn_fwd(qkv_n, bgcol, bgrow)
    y_b = dn_out_fwd(o, p, dnw)
    ymix = jnp.concatenate([y_a, y_b], axis=1)
    out = mm(ymix, w_out, add=h, name="mix_out")
    return out, (hn, p, cv, qkv_n, cq, bgcol, bgrow, o, s_all, ti_all, ymix)


def mixer_bwd(h, nw, w_in, ca8, dc8, alog, dtb, dnw, w_out, saved, dh):
    hn, p, cv, qkv_n, cq, bgcol, bgrow, o, s_all, ti_all, ymix = saved
    rows = h.shape[0]
    tr = _pick(rows, (384, 128))
    dymix = mm(dh, w_out, tb=True, name="mix_out_dx")
    d_w_out = mm(ymix, dh, ta=True, name="mix_out_dw")
    do, dz, d_dnw = dn_out_bwd(o, p, dnw, dymix)
    dq, dk, dv, dbg = dn_bwd(qkv_n, bgcol, bgrow, s_all, ti_all, do)
    dbg_p, d_alog, d_dtb = bg_bwd(p, alog, dtb, dbg)
    dcq = dn_qkv_bwd(cq, dq, dk, dv)
    dqkv, d_dc = conv_bwd([(p, 3)], dc8, 4, dcq, rows=rows, c=3 * DN_DIM, tc=DN_DIM, tr=tr, name="dn_conv_bwd",
                          post=lambda dx: dx, outs=[BF16])
    dgo, dcv = conv_a_pre_bwd(dymix, cv, p)
    dgi, dah, d_ca = conv_bwd([(p, 0), (p, 2)], ca8, 3, dcv, rows=rows, c=D_CONV, tc=D_CONV, tr=tr,
                              name="conv_a_bwd", pre=lambda gi, ah: gi * ah,
                              post=lambda dm, gi, ah: (dm * ah, dm * gi), extras=[(p, 0), (p, 2)], outs=[BF16, BF16])
    dp = jnp.concatenate([dgi, dgo, dah, dqkv, dz, dbg_p], axis=1)
    dhn = mm(dp, w_in, tb=True, name="mix_in_dx")
    d_w_in = mm(hn, dp, ta=True, name="mix_in_dw")
    dh_new, d_nw = rms_bwd(h, nw, dhn, dh, name="mix_norm_bwd")
    return dh_new, d_nw, d_w_in, d_ca, d_dc, d_alog, d_dtb, d_dnw, d_w_out


def swa_layer_fwd(h, nw, wqkv, qw, kw, sinks, wo):
    hn = rms_fwd(h, nw, name="swa_norm")
    qkv = mm(hn, wqkv, name="swa_qkv")
    qh, kh, vh = qknorm_fwd(qkv, qw, kw)
    att = swa_fwd(qh, kh, vh, sinks)
    out = mm(att, wo, add=h, name="swa_out")
    return out, (hn, qkv, qh, kh, vh, att)


def swa_layer_bwd(h, nw, wqkv, qw, kw, sinks, wo, saved, dh):
    hn, qkv, qh, kh, vh, att = saved
    datt = mm(dh, wo, tb=True, out_dtype=BF16, name="swa_out_dx")
    d_wo = mm(att, dh, ta=True, name="swa_out_dw")
    dqh, dkh, dvh, dsk = swa_bwd(qh, kh, vh, sinks, datt)
    dqkv, d_qw, d_kw = qknorm_bwd(qkv, qw, kw, dqh, dkh, dvh)
    dhn = mm(dqkv, wqkv, tb=True, name="swa_qkv_dx")
    d_wqkv = mm(hn, dqkv, ta=True, name="swa_qkv_dw")
    dh_new, d_nw = rms_bwd(h, nw, dhn, dh, name="swa_norm_bwd")
    d_sinks = jnp.sum(dsk[:, :, :4, 0], axis=1).reshape(SWA_H)
    return dh_new, d_nw, d_wqkv, d_qw, d_kw, d_sinks, d_wo


BIG_ROWS = (898, 256, 256, 64, 64, 256, 2816, 1408)


def _big_offsets():
    offs, o = [], 0
    for r in BIG_ROWS:
        offs.append(o)
        o += r
    return offs, o


def pack_big_shard(parts, dtype):
    flat = [t.astype(dtype).reshape(-1, 1024) for t in parts]
    used = sum(t.shape[0] for t in flat)
    return jnp.concatenate(flat + [jnp.zeros((R_BIG - used, 1024), dtype)], axis=0)


def unpack_big_shard(flat, shapes):
    offs, _ = _big_offsets()
    return [flat[o:o + r].reshape(s) for o, r, s in zip(offs, BIG_ROWS, shapes)]


def unpack_big_full(g):
    offs, _ = _big_offsets()
    sl = lambda k: g[:, offs[k]:offs[k] + BIG_ROWS[k]]
    w_in = sl(0).reshape(4, D, 898).transpose(1, 0, 2).reshape(D, IN_DIM)
    w_out = sl(1).reshape(D, D)
    wq = sl(2).reshape(D, D)
    wk = sl(3).reshape(D, 256)
    wv = sl(4).reshape(D, 256)
    wo = sl(5).reshape(D, D)
    w_up = sl(6).reshape(4, 2, D, 1408).transpose(1, 2, 0, 3).reshape(2, D, 2 * D_FF)
    w_down = sl(7).reshape(4, 2, 704, D).transpose(1, 0, 2, 3).reshape(2, D_FF, D)
    return w_in, w_out, wq, wk, wv, wo, w_up, w_down


def pack_big_full(w_in, w_out, wq, wk, wv, wo, w_up, w_down, dtype):
    parts = [
        w_in.reshape(D, 4, 898).transpose(1, 0, 2),
        w_out.reshape(4, 256, D), wq.reshape(4, 256, D), wk.reshape(4, 256, 256), wv.reshape(4, 256, 256),
        wo.reshape(4, 256, D),
        w_up.reshape(2, D, 4, 1408).transpose(2, 0, 1, 3),
        w_down.reshape(2, 4, 704, D).transpose(1, 0, 2, 3),
    ]
    flat = [t.astype(dtype).reshape(4, -1, 1024) for t in parts]
    used = sum(t.shape[1] for t in flat)
    return jnp.concatenate(flat + [jnp.zeros((4, R_BIG - used, 1024), dtype)], axis=1)


def _flat_pad(parts, rows):
    v = jnp.concatenate([t.astype(F32).reshape(-1) for t in parts])
    return jnp.pad(v, (0, rows * 1024 - v.shape[0])).reshape(rows, 1024)


def _split_flat(flat, shapes):
    v = flat.reshape(-1)
    out, o = [], 0
    for s in shapes:
        n = 1
        for d_ in s:
            n *= d_
        out.append(v[o:o + n].reshape(s))
        o += n
    return out


def local_step(x0, target0, meta_full, anw, fnw, w_in, ca8, dc8, alog, dtb, dnw, w_out, wqkv, qw, kw, sinks, wo,
               w_up, fc8, w_down):
    h0 = jnp.concatenate([jnp.zeros((PAD, D), F32), meta_full, x0], axis=0)
    h1, s_mix = mixer_fwd(h0, anw[0], w_in, ca8, dc8, alog, dtb, dnw, w_out)
    h2, s_f0 = ffn_fwd(h1, fnw[0], w_up[0], fc8[0], w_down[0], 0)
    h3, s_swa = swa_layer_fwd(h2, anw[1], wqkv, qw, kw, sinks, wo)
    h4, s_f1 = ffn_fwd(h3, fnw[1], w_up[1], fc8[1], w_down[1], 1)
    dh, loss_l = loss_grad(h4, target0)
    dh, d_fnw1, d_up1, d_fc1, d_down1 = ffn_bwd(h3, fnw[1], w_up[1], fc8[1], w_down[1], s_f1, dh, 1)
    dh, d_anw1, d_wqkv, d_qw, d_kw, d_sinks, d_wo = swa_layer_bwd(h2, anw[1], wqkv, qw, kw, sinks, wo, s_swa, dh)
    dh, d_fnw0, d_up0, d_fc0, d_down0 = ffn_bwd(h1, fnw[0], w_up[0], fc8[0], w_down[0], s_f0, dh, 0)
    dh, d_anw0, d_w_in, d_ca, d_dc, d_alog, d_dtb, d_dnw, d_w_out = mixer_bwd(
        h0, anw[0], w_in, ca8, dc8, alog, dtb, dnw, w_out, s_mix, dh)
    return (dh, loss_l, d_anw0, d_anw1, d_fnw0, d_fnw1, d_w_in, d_ca, d_dc, d_alog, d_dtb, d_dnw, d_w_out, d_wqkv,
            d_qw, d_kw, d_sinks, d_wo, d_up0, d_up1, d_fc0, d_fc1, d_down0, d_down1)


def kernel(x, meta_tokens, attn_norm_w, ffn_norm_w, mix_w_in, conv_a_w, dn_conv_w, dn_a_log, dn_dt_bias, dn_norm_w, mix_w_out, swa_wq, swa_wk, swa_wv, swa_q_norm_w, swa_k_norm_w, swa_sinks, swa_wo, ffn_w_up, ffn_conv_w, ffn_w_down, loss_target, m_meta_tokens, m_attn_norm_w, m_ffn_norm_w, m_mix_w_in, m_conv_a_w, m_dn_conv_w, m_dn_a_log, m_dn_dt_bias, m_dn_norm_w, m_mix_w_out, m_swa_wq, m_swa_wk, m_swa_wv, m_swa_q_norm_w, m_swa_k_norm_w, m_swa_sinks, m_swa_wo, m_ffn_w_up, m_ffn_conv_w, m_ffn_w_down, v_meta_tokens, v_attn_norm_w, v_ffn_norm_w, v_mix_w_in, v_conv_a_w, v_dn_conv_w, v_dn_a_log, v_dn_dt_bias, v_dn_norm_w, v_mix_w_out, v_swa_wq, v_swa_wk, v_swa_wv, v_swa_q_norm_w, v_swa_k_norm_w, v_swa_sinks, v_swa_wo, v_ffn_w_up, v_ffn_conv_w, v_ffn_w_down):
    ix, iy, ic = lax.axis_index("x"), lax.axis_index("y"), lax.axis_index("c")
    chip = 2 * ix + iy
    seq = x.shape[1]
    rows = HEAD0 + seq

    big_names = (mix_w_in, mix_w_out, swa_wq, swa_wk, swa_wv, swa_wo, ffn_w_up, ffn_w_down)
    small_sharded = (conv_a_w, dn_conv_w, ffn_conv_w, meta_tokens)
    g_big, g_small = gather_weights(pack_big_shard(big_names, BF16), _flat_pad(small_sharded, SW_ROWS))
    w_in, w_out, wq, wk, wv, wo, w_up, w_down = unpack_big_full(g_big)
    w_in = jnp.pad(w_in, ((0, 0), (0, P_W - IN_DIM)))
    wqkv = jnp.concatenate([wq, wk, wv], axis=1)
    gs = g_small.reshape(4, -1)
    ca_full = gs[:, 0:384].reshape(4, 3, 128).transpose(1, 0, 2).reshape(3, D_CONV)
    dc_full = gs[:, 384:1920].reshape(4, 4, 384).transpose(1, 0, 2).reshape(4, 3 * DN_DIM)
    fc_full = gs[:, 1920:6144].reshape(4, 2, 3, 704).transpose(1, 2, 0, 3).reshape(2, 3, D_FF)
    meta_full = gs[:, 6144:10240].reshape(4, N_META, 256).transpose(1, 0, 2).reshape(N_META, D)
    ca8, dc8 = _rows8(ca_full), _rows8(dc_full)
    fc8 = [_rows8(fc_full[0]), _rows8(fc_full[1])]
    alog, dtb = _lanes(dn_a_log[0], 4), _lanes(dn_dt_bias[0], 4)
    dnw = dn_norm_w.astype(F32)
    qw, kw = swa_q_norm_w.astype(F32), swa_k_norm_w.astype(F32)
    sinks = swa_sinks[0].astype(F32)
    anw = [attn_norm_w[0:1], attn_norm_w[1:2]]
    fnw = [ffn_norm_w[0:1], ffn_norm_w[1:2]]

    (dh, loss_l, d_anw0, d_anw1, d_fnw0, d_fnw1, d_w_in, d_ca, d_dc, d_alog, d_dtb, d_dnw, d_w_out, d_wqkv, d_qw,
     d_kw, d_sinks, d_wo, d_up0, d_up1, d_fc0, d_fc1, d_down0, d_down1) = local_step(
        x[0], loss_target[0], meta_full, anw, fnw, w_in, ca8, dc8, alog, dtb, dnw, w_out, wqkv, qw, kw, sinks, wo,
        w_up, fc8, w_down)
    grad_x = dh[HEAD0:][None]

    small_parts = [jnp.concatenate([d_anw0, d_anw1], axis=0), jnp.concatenate([d_fnw0, d_fnw1], axis=0),
                   d_alog[0, 4:8], d_dtb[0, 4:8], d_dnw, d_qw, d_kw, d_sinks,
                   d_ca[:3], d_dc[:4], jnp.stack([d_fc0[:3], d_fc1[:3]]), dh[PAD:HEAD0], loss_l[0, 0:1]]
    small_shapes = [(2, D), (2, D), (1, 4), (1, 4), (1, DN_D), (1, SWA_D), (1, SWA_D), (1, SWA_H),
                    (1, 3, D_CONV), (1, 4, 3 * DN_DIM), (2, 3, D_FF), (N_META, D), ()]
    red = sum_slots(gather_small(_flat_pad(small_parts, SV_ROWS)))
    (g_anw, g_fnw, g_alog, g_dtb, g_dnw, g_qw, g_kw, g_sinks, g_ca_f, g_dc_f, g_fc_f, g_meta_f,
     loss) = _split_flat(red, small_shapes)
    g_ca = lax.dynamic_slice_in_dim(g_ca_f, chip * 128, 128, axis=2)
    g_dc = lax.dynamic_slice_in_dim(g_dc_f, chip * 384, 384, axis=2)
    g_fc = lax.dynamic_slice_in_dim(g_fc_f, chip * 704, 704, axis=2)
    g_meta = lax.dynamic_slice_in_dim(g_meta_f, chip * 256, 256, axis=1)

    d_up = jnp.stack([d_up0, d_up1])
    d_down = jnp.stack([d_down0, d_down1])
    g_full = pack_big_full(d_w_in[:, :IN_DIM], d_w_out, d_wqkv[:, :D], d_wqkv[:, D:D + 256], d_wqkv[:, D + 256:],
                           d_wo, d_up, d_down, BF16)
    c_idx = jnp.reshape(ic, (1,)).astype(jnp.int32)
    chip_idx = jnp.reshape(chip, (1,)).astype(jnp.int32)
    pair = pair_sum(g_full, swap_halves(g_full), c_idx)
    mine = chip_sum(pair, scatter_chips(pair), chip_idx)
    g_shard = join_halves(mine).reshape(R_BIG, 1024)
    big_shapes = [t.shape for t in big_names]
    g_w_in, g_w_out, g_wq, g_wk, g_wv, g_wo, g_up, g_down = unpack_big_shard(g_shard, big_shapes)

    grads = dict(meta_tokens=g_meta, attn_norm_w=g_anw, ffn_norm_w=g_fnw, mix_w_in=g_w_in, conv_a_w=g_ca,
                 dn_conv_w=g_dc, dn_a_log=g_alog, dn_dt_bias=g_dtb, dn_norm_w=g_dnw, mix_w_out=g_w_out,
                 swa_wq=g_wq, swa_wk=g_wk, swa_wv=g_wv, swa_q_norm_w=g_qw, swa_k_norm_w=g_kw, swa_sinks=g_sinks,
                 swa_wo=g_wo, ffn_w_up=g_up, ffn_conv_w=g_fc, ffn_w_down=g_down)
    weights = dict(meta_tokens=meta_tokens, attn_norm_w=attn_norm_w, ffn_norm_w=ffn_norm_w, mix_w_in=mix_w_in,
                   conv_a_w=conv_a_w, dn_conv_w=dn_conv_w, dn_a_log=dn_a_log, dn_dt_bias=dn_dt_bias,
                   dn_norm_w=dn_norm_w, mix_w_out=mix_w_out, swa_wq=swa_wq, swa_wk=swa_wk, swa_wv=swa_wv,
                   swa_q_norm_w=swa_q_norm_w, swa_k_norm_w=swa_k_norm_w, swa_sinks=swa_sinks, swa_wo=swa_wo,
                   ffn_w_up=ffn_w_up, ffn_conv_w=ffn_conv_w, ffn_w_down=ffn_w_down)
    m_in = dict(meta_tokens=m_meta_tokens, attn_norm_w=m_attn_norm_w, ffn_norm_w=m_ffn_norm_w, mix_w_in=m_mix_w_in,
                conv_a_w=m_conv_a_w, dn_conv_w=m_dn_conv_w, dn_a_log=m_dn_a_log, dn_dt_bias=m_dn_dt_bias,
                dn_norm_w=m_dn_norm_w, mix_w_out=m_mix_w_out, swa_wq=m_swa_wq, swa_wk=m_swa_wk, swa_wv=m_swa_wv,
                swa_q_norm_w=m_swa_q_norm_w, swa_k_norm_w=m_swa_k_norm_w, swa_sinks=m_swa_sinks, swa_wo=m_swa_wo,
                ffn_w_up=m_ffn_w_up, ffn_conv_w=m_ffn_conv_w, ffn_w_down=m_ffn_w_down)
    v_in = dict(meta_tokens=v_meta_tokens, attn_norm_w=v_attn_norm_w, ffn_norm_w=v_ffn_norm_w, mix_w_in=v_mix_w_in,
                conv_a_w=v_conv_a_w, dn_conv_w=v_dn_conv_w, dn_a_log=v_dn_a_log, dn_dt_bias=v_dn_dt_bias,
                dn_norm_w=v_dn_norm_w, mix_w_out=v_mix_w_out, swa_wq=v_swa_wq, swa_wk=v_swa_wk, swa_wv=v_swa_wv,
                swa_q_norm_w=v_swa_q_norm_w, swa_k_norm_w=v_swa_k_norm_w, swa_sinks=v_swa_sinks, swa_wo=v_swa_wo,
                ffn_w_up=v_ffn_w_up, ffn_conv_w=v_ffn_conv_w, ffn_w_down=v_ffn_w_down)
    names = list(weights)
    big = ("mix_w_in", "mix_w_out", "swa_wq", "swa_wk", "swa_wv", "swa_wo", "ffn_w_up", "ffn_w_down")
    small = [n for n in names if n not in big]
    grads = {n: grads[n].reshape(weights[n].shape) for n in names}
    delta, new_m, new_v = {}, {}, {}
    for n in big:
        delta[n], new_m[n], new_v[n] = adamw(weights[n], grads[n], m_in[n], v_in[n], name=f"adamw_{n}")
    shapes = [weights[n].shape for n in small]
    packed = [_flat_pad([t[n] for n in small], SW_ROWS) for t in (weights, grads, m_in, v_in)]
    for store, flat in zip((delta, new_m, new_v), adamw(*packed, name="adamw_small")):
        for n, t in zip(small, _split_flat(flat, shapes)):
            store[n] = t
    return (loss, grad_x, *[grads[n] for n in names], *[delta[n] for n in names],
            *[new_m[n] for n in names], *[new_v[n] for n in names])
```

```python
import functools

import jax
import jax.numpy as jnp
from jax import lax
from jax.experimental import pallas as pl
from jax.experimental.pallas import tpu as pltpu

F32 = jnp.float32
BF16 = jnp.bfloat16
HI = lax.Precision.HIGHEST
MESH = pl.DeviceIdType.MESH

D = 1024
N_META = 16
PAD = 112
HEAD0 = PAD + N_META
D_CONV = 512
DN_H = 4
DN_D = 128
DN_DIM = 512
CH = 64
IN_DIM = 3592
P_W = 3840
BG0 = 3584
SWA_H = 16
SWA_KV = 4
SWA_D = 64
BLK = 128
D_FF = 2816
EPS = 1e-6
LR, B1, B2, AEPS, WD, STEP = 0.001, 0.9, 0.999, 1e-08, 0.01, 10
VMEM_LIMIT = 48 * 1024 * 1024
R_BIG = 6144
R_HALF = R_BIG // 2
SV_ROWS = 48
SW_ROWS = 16


def _pick(n, cands):
    for c in cands:
        if n % c == 0:
            return c
    return n


def _params(sem=None):
    return pltpu.CompilerParams(dimension_semantics=sem, vmem_limit_bytes=VMEM_LIMIT)


def _dot(a, b, ca=1, cb=0, prec=None):
    return lax.dot_general(a, b, (((ca,), (cb,)), ((), ())), precision=prec,
                           preferred_element_type=F32)


def _sigmoid(x):
    return 1.0 / (1.0 + jnp.exp(-x))


def _silu(x):
    return x * _sigmoid(x)


def _dsilu(x):
    s = _sigmoid(x)
    return s * (1.0 + x * (1.0 - s))


def _softplus(x):
    return jnp.maximum(x, 0.0) + jnp.log(1.0 + jnp.exp(-jnp.abs(x)))


def mm(a, b, *, name, ta=False, tb=False, out_dtype=F32, add=None, tm=None, tn=None, tk=None):
    m, k = (a.shape[1], a.shape[0]) if ta else a.shape
    n = b.shape[0] if tb else b.shape[1]
    tm = tm or (_pick(m, (1408, 512, 384, 256, 128)) if ta else _pick(m, (704, 512, 384, 256, 128)))
    tn = tn or _pick(n, (1408, 1024, 768, 512, 256, 128))
    tk = tk or (_pick(k, (704, 384, 128)) if ta else _pick(k, (1024, 1408, 768, 512, 128)))
    nk = k // tk
    dims = (((0 if ta else 1,), (1 if tb else 0,)), ((), ()))

    def body(*refs):
        if add is None:
            a_ref, b_ref, o_ref, acc_ref = refs
            add_ref = None
        else:
            a_ref, b_ref, add_ref, o_ref, acc_ref = refs
        part = lax.dot_general(a_ref[...].astype(BF16), b_ref[...].astype(BF16), dims,
                               preferred_element_type=F32)

        def finish(total):
            if add_ref is not None:
                total = total + add_ref[...]
            o_ref[...] = total.astype(out_dtype)

        if nk == 1:
            finish(part)
        else:
            kk = pl.program_id(2)

            @pl.when(kk == 0)
            def _():
                acc_ref[...] = part

            @pl.when(kk > 0)
            def _():
                acc_ref[...] += part

            @pl.when(kk == nk - 1)
            def _():
                finish(acc_ref[...])

    a_spec = pl.BlockSpec((tk, tm), lambda i, j, kk: (kk, i)) if ta else pl.BlockSpec((tm, tk), lambda i, j, kk: (i, kk))
    b_spec = pl.BlockSpec((tn, tk), lambda i, j, kk: (j, kk)) if tb else pl.BlockSpec((tk, tn), lambda i, j, kk: (kk, j))
    o_spec = pl.BlockSpec((tm, tn), lambda i, j, kk: (i, j))
    in_specs = [a_spec, b_spec] + ([o_spec] if add is not None else [])
    args = [a, b] + ([add] if add is not None else [])
    return pl.pallas_call(
        body, name=name, interpret=False,
        out_shape=jax.ShapeDtypeStruct((m, n), out_dtype),
        grid=(m // tm, n // tn, nk), in_specs=in_specs, out_specs=o_spec,
        scratch_shapes=[pltpu.VMEM((tm, tn) if nk > 1 else (8, 128), F32)],
        compiler_params=_params(("parallel", "parallel", "arbitrary")),
    )(*args)


def cols(arr, tr, width=None, cb=0):
    width = width or arr.shape[1]
    return (arr, (tr, width), lambda i: (i, cb))


def heads(arr, tr):
    return (arr, (arr.shape[0], tr, arr.shape[2]), lambda i: (0, i, 0))


def whole(arr):
    nd = arr.ndim
    return (arr, arr.shape, lambda i: (0,) * nd)


def rowwise(fn, ins, outs, *, steps, name, accs=()):
    n_in, n_out, n_acc = len(ins), len(outs), len(accs)

    def body(*refs):
        i = pl.program_id(0)
        res = fn(i, *[r[...] for r in refs[:n_in]])
        if not isinstance(res, (tuple, list)):
            res = (res,)
        for r, v in zip(refs[n_in:n_in + n_out], res[:n_out]):
            r[...] = v.astype(r.dtype)
        if n_acc:
            acc_refs = refs[n_in + n_out:]

            @pl.when(i == 0)
            def _():
                for r in acc_refs:
                    r[...] = jnp.zeros(r.shape, r.dtype)

            for r, v in zip(acc_refs, res[n_out:]):
                r[...] += jnp.broadcast_to(v, r.shape).astype(r.dtype)

    def zmap(nd):
        return lambda i: (0,) * nd

    in_specs = [pl.BlockSpec(bs, im) for _, bs, im in ins]
    out_specs = [pl.BlockSpec(bs, im) for _, _, bs, im in outs]
    out_specs += [pl.BlockSpec(s, zmap(len(s))) for s, _ in accs]
    out_shape = [jax.ShapeDtypeStruct(s, d) for s, d, _, _ in outs]
    out_shape += [jax.ShapeDtypeStruct(s, d) for s, d in accs]
    res = pl.pallas_call(
        body, name=name, interpret=False, out_shape=out_shape, grid=(steps,),
        in_specs=in_specs, out_specs=out_specs,
        compiler_params=_params(("arbitrary",)),
    )(*[a for a, _, _ in ins])
    return res


def out2d(rows, width, dtype, tr):
    return ((rows, width), dtype, (tr, width), lambda i: (i, 0))


def conv_fwd(xs, w8, kw, *, rows, c, tc, tr, name, post, extras=(), outs=(), pre=None):
    nx, ne, no = len(xs), len(extras), len(outs)
    nr, nc = rows // tr, c // tc
    r8 = tr // 8

    def body(*refs):
        x_refs = refs[:2 * nx]
        w_ref = refs[2 * nx]
        e_refs = refs[2 * nx + 1:2 * nx + 1 + ne]
        o_refs = refs[2 * nx + 1 + ne:2 * nx + 1 + ne + no]
        scr = refs[-1]
        j, i = pl.program_id(0), pl.program_id(1)
        cur = [x_refs[2 * q][...].astype(F32) for q in range(nx)]
        halo = [x_refs[2 * q + 1][...].astype(F32) for q in range(nx)]
        x = pre(*cur) if pre else cur[0]
        h = pre(*halo) if pre else halo[0]
        scr[0:8, :] = jnp.where(i > 0, h, 0.0)
        scr[8:8 + tr, :] = x
        y = jnp.zeros((tr, tc), F32)
        for q in range(kw):
            s = kw - 1 - q
            y = y + w_ref[q:q + 1, :] * scr[8 - s:8 - s + tr, :]
        res = post(j, y, *[e[...] for e in e_refs])
        if not isinstance(res, (tuple, list)):
            res = (res,)
        for r, v in zip(o_refs, res):
            r[...] = v.astype(r.dtype)

    in_specs, args = [], []
    for arr, cb0 in xs:
        in_specs.append(pl.BlockSpec((tr, tc), lambda j, i, cb0=cb0: (i, cb0 + j)))
        in_specs.append(pl.BlockSpec((8, tc), lambda j, i, cb0=cb0: (jnp.maximum(i * r8 - 1, 0), cb0 + j)))
        args += [arr, arr]
    in_specs.append(pl.BlockSpec((8, tc), lambda j, i: (0, j)))
    args.append(w8)
    for arr, cb0 in extras:
        in_specs.append(pl.BlockSpec((tr, tc), lambda j, i, cb0=cb0: (i, cb0 + j)))
        args.append(arr)
    return pl.pallas_call(
        body, name=name, interpret=False,
        out_shape=[jax.ShapeDtypeStruct((rows, c), dt) for dt in outs],
        grid=(nc, nr), in_specs=in_specs,
        out_specs=[pl.BlockSpec((tr, tc), lambda j, i: (i, j)) for _ in outs],
        scratch_shapes=[pltpu.VMEM((tr + 8, tc), F32)],
        compiler_params=_params(("parallel", "arbitrary")),
    )(*args)


def conv_bwd(xs, w8, kw, dy, *, rows, c, tc, tr, name, post, extras=(), outs=(), pre=None):
    nx, ne, no = len(xs), len(extras), len(outs)
    nr, nc = rows // tr, c // tc
    r8 = tr // 8

    def body(*refs):
        x_refs = refs[:2 * nx]
        w_ref, dy_ref, dyn_ref = refs[2 * nx:2 * nx + 3]
        e_refs = refs[2 * nx + 3:2 * nx + 3 + ne]
        o_refs = refs[2 * nx + 3 + ne:2 * nx + 3 + ne + no]
        dw_ref = refs[2 * nx + 3 + ne + no]
        xscr, gscr = refs[-2], refs[-1]
        i = pl.program_id(1)
        cur = [x_refs[2 * q][...].astype(F32) for q in range(nx)]
        halo = [x_refs[2 * q + 1][...].astype(F32) for q in range(nx)]
        x = pre(*cur) if pre else cur[0]
        h = pre(*halo) if pre else halo[0]
        xscr[0:8, :] = jnp.where(i > 0, h, 0.0)
        xscr[8:8 + tr, :] = x
        g = dy_ref[...].astype(F32)
        gscr[0:tr, :] = g
        gscr[tr:tr + 8, :] = jnp.where(i < nr - 1, dyn_ref[...].astype(F32), 0.0)
        dx = jnp.zeros((tr, tc), F32)
        dws = []
        for q in range(kw):
            s = kw - 1 - q
            dx = dx + w_ref[q:q + 1, :] * gscr[s:s + tr, :]
            dws.append(jnp.sum(g * xscr[8 - s:8 - s + tr, :], axis=0, keepdims=True))
        dws.append(jnp.zeros((8 - kw, tc), F32))
        res = post(dx, *[e[...] for e in e_refs])
        if not isinstance(res, (tuple, list)):
            res = (res,)
        for r, v in zip(o_refs, res):
            r[...] = v.astype(r.dtype)

        @pl.when(i == 0)
        def _():
            dw_ref[...] = jnp.zeros((8, tc), F32)

        dw_ref[...] += jnp.concatenate(dws, axis=0)

    in_specs, args = [], []
    for arr, cb0 in xs:
        in_specs.append(pl.BlockSpec((tr, tc), lambda j, i, cb0=cb0: (i, cb0 + j)))
        in_specs.append(pl.BlockSpec((8, tc), lambda j, i, cb0=cb0: (jnp.maximum(i * r8 - 1, 0), cb0 + j)))
        args += [arr, arr]
    in_specs.append(pl.BlockSpec((8, tc), lambda j, i: (0, j)))
    in_specs.append(pl.BlockSpec((tr, tc), lambda j, i: (i, j)))
    in_specs.append(pl.BlockSpec((8, tc), lambda j, i: (jnp.minimum((i + 1) * r8, nr * r8 - 1), j)))
    args += [w8, dy, dy]
    for arr, cb0 in extras:
        in_specs.append(pl.BlockSpec((tr, tc), lambda j, i, cb0=cb0: (i, cb0 + j)))
        args.append(arr)
    return pl.pallas_call(
        body, name=name, interpret=False,
        out_shape=[jax.ShapeDtypeStruct((rows, c), dt) for dt in outs] + [jax.ShapeDtypeStruct((8, c), F32)],
        grid=(nc, nr), in_specs=in_specs,
        out_specs=[pl.BlockSpec((tr, tc), lambda j, i: (i, j)) for _ in outs] + [pl.BlockSpec((8, tc), lambda j, i: (0, j))],
        scratch_shapes=[pltpu.VMEM((tr + 8, tc), F32), pltpu.VMEM((tr + 8, tc), F32)],
        compiler_params=_params(("parallel", "arbitrary")),
    )(*args)


def rms_fwd(h, w, *, name):
    rows = h.shape[0]
    tr = _pick(rows, (384, 128))

    def fn(i, x, wv):
        r = lax.rsqrt(jnp.mean(x * x, axis=1, keepdims=True) + EPS)
        return x * r * wv

    return rowwise(fn, [cols(h, tr), whole(w)], [out2d(rows, D, BF16, tr)], steps=rows // tr, name=name)[0]


def rms_bwd(h, w, dy, dres, *, name):
    rows = h.shape[0]
    tr = _pick(rows, (384, 128))

    def fn(i, x, wv, g, dr):
        r = lax.rsqrt(jnp.mean(x * x, axis=1, keepdims=True) + EPS)
        xh = x * r
        gw = g * wv
        dx = r * (gw - xh * jnp.mean(gw * xh, axis=1, keepdims=True))
        row = i * tr + lax.broadcasted_iota(jnp.int32, (tr, 1), 0)
        return jnp.where(row >= PAD, dr + dx, 0.0), jnp.sum(g * xh, axis=0, keepdims=True)

    return rowwise(fn, [cols(h, tr), whole(w), cols(dy, tr), cols(dres, tr)], [out2d(rows, D, F32, tr)],
                   steps=rows // tr, name=name, accs=[((1, D), F32)])


def loss_grad(h, target):
    rows = h.shape[0]

    def fn(i, y, t):
        diff = jnp.where(i > 0, y - t, 0.0)
        part = jnp.sum(jnp.sum(diff * diff, axis=1, keepdims=True), axis=0, keepdims=True)
        return diff * (1.0 / D), part * (0.5 / D)

    tgt = (target, (BLK, D), lambda i: (jnp.maximum(i - 1, 0), 0))
    return rowwise(fn, [cols(h, BLK), tgt], [out2d(rows, D, F32, BLK)], steps=rows // BLK,
                   name="loss_grad", accs=[((1, 128), F32)])


def adamw(w, g, m, v, *, name):
    shape = w.shape
    w2, g2, m2, v2 = (t.reshape(-1, shape[-1]) for t in (w, g, m, v))
    rows, width = w2.shape
    tr = _pick(rows, (256, 176, 128, 64, 16, 8))

    def fn(i, wv, gv, mv, vv):
        mn = B1 * mv + (1.0 - B1) * gv
        vn = B2 * vv + (1.0 - B2) * gv * gv
        mh = mn / (1.0 - B1 ** STEP)
        vh = vn / (1.0 - B2 ** STEP)
        return -LR * (mh / (jnp.sqrt(vh) + AEPS) + WD * wv), mn, vn

    res = rowwise(fn, [cols(t, tr) for t in (w2, g2, m2, v2)], [out2d(rows, width, F32, tr)] * 3,
                  steps=rows // tr, name=name)
    return [r.reshape(shape) for r in res]


HB = DN_H * CH


def _split(a):
    hi = a.astype(BF16)
    return hi, (a - hi.astype(F32)).astype(BF16)


def _dot1(a, b, ca=1, cb=0):
    return _dot(a.astype(BF16), b.astype(BF16), ca, cb)


def _dot3(a, b, ca=1, cb=0):
    ah, al = _split(a)
    bh, bl = _split(b)
    return _dot(ah, bh, ca, cb) + (_dot(ah, bl, ca, cb) + _dot(al, bh, ca, cb))


def _dot01(m01, b, ca=1, cb=0):
    bh, bl = _split(b)
    m = m01.astype(BF16)
    return _dot(m, bh, ca, cb) + _dot(m, bl, ca, cb)


def _stack(x):
    return jnp.concatenate([x[:, h * DN_D:(h + 1) * DN_D] for h in range(DN_H)], axis=0)


def _unstack(x):
    return jnp.concatenate([x[h * CH:(h + 1) * CH] for h in range(DN_H)], axis=1)


def _tri_inv(a, blk, eye):
    ad = jnp.where(blk, a, 0.0)
    lo = a - ad
    a2 = _dot3(ad, ad)
    a4 = _dot3(a2, a2)
    a8 = _dot3(a4, a4)
    dgi = _dot3(_dot3(_dot3(eye - ad, eye + a2), eye + a4), eye + a8)
    n = _dot3(dgi, lo)
    return _dot3(_dot3(eye - n, eye + _dot3(n, n)), dgi)


def _dn_masks():
    row = lax.broadcasted_iota(jnp.int32, (HB, HB), 0)
    col = lax.broadcasted_iota(jnp.int32, (HB, HB), 1)
    same = (row // CH) == (col // CH)
    incl = jnp.logical_and(same, row >= col)
    strict = jnp.logical_and(same, row > col)
    upper = jnp.logical_and(same, row <= col)
    blk = (row // 16) == (col // 16)
    eye = (row == col).astype(F32)
    return incl, strict, upper, blk, eye


def _dn_chunk(q_ref, k_ref, v_ref, bc_ref, br_ref, incl, strict):
    r64 = lax.broadcasted_iota(jnp.int32, (CH, CH), 0)
    c64 = lax.broadcasted_iota(jnp.int32, (CH, CH), 1)
    bc = bc_ref[...]
    dcol = _dot01((r64 >= c64).astype(F32), bc)
    drow = _dot3(br_ref[0], (r64 <= c64).astype(F32))
    col = lambda m, l0: jnp.concatenate([m[:, l0 + h:l0 + h + 1] for h in range(DN_H)], axis=0)
    b_c = col(bc, 0)
    d_c = col(dcol, 4)
    d_r = jnp.concatenate([drow[4 + h:5 + h, :] for h in range(DN_H)], axis=1)
    d_last_h = [dcol[CH - 1:CH, 4 + h:5 + h] for h in range(DN_H)]
    d_last = jnp.concatenate([jnp.broadcast_to(t, (CH, 1)) for t in d_last_h], axis=0)
    q, k, v = _stack(q_ref[...]), _stack(k_ref[...]), _stack(v_ref[...])
    dm = jnp.where(incl, jnp.exp(jnp.where(incl, d_c - d_r, 0.0)), 0.0)
    kk = _dot1(k, k, 1, 1)
    a = jnp.where(strict, b_c * kk * dm, 0.0)
    ed = jnp.exp(d_c)
    rhs = jnp.concatenate([v * b_c, k * (b_c * ed)], axis=1)
    qk = _dot1(q, k, 1, 1) * dm
    ekd = jnp.exp(d_last - d_c)
    gl = [jnp.exp(t) for t in d_last_h]
    return q, k, v, b_c, dm, kk, a, ed, rhs, qk, ekd, gl


def dn_fwd(qkv_n, bgcol, bgrow):
    rows = qkv_n.shape[0]
    nch = rows // CH

    def body(q_ref, k_ref, v_ref, bc_ref, br_ref, o_ref, s_out, ti_out, s_scr):
        n = pl.program_id(0)

        @pl.when(n == 0)
        def _():
            s_scr[...] = jnp.zeros(s_scr.shape, F32)

        incl, strict, _, blk, eye = _dn_masks()
        q, k, v, b_c, dm, kk, a, ed, rhs, qk, ekd, gl = _dn_chunk(q_ref, k_ref, v_ref, bc_ref, br_ref, incl, strict)
        tinv = _tri_inv(a, blk, eye)
        ti_out[0] = tinv
        sol = _dot3(tinv, rhs)
        u, w = sol[:, :DN_D], sol[:, DN_D:]
        qd, kd = q * ed, k * ekd
        v_new, o_state = [], []
        for h in range(DN_H):
            rs = slice(h * CH, (h + 1) * CH)
            s = s_scr[h]
            s_out[0, h] = s
            vn = u[rs] - _dot1(w[rs], s)
            v_new.append(vn)
            o_state.append(_dot1(qd[rs], s))
            s_scr[h] = gl[h] * s + _dot1(kd[rs], vn, 0, 0)
        o = jnp.concatenate(o_state, axis=0) + _dot1(qk, jnp.concatenate(v_new, axis=0))
        o_ref[...] = _unstack(o)

    return pl.pallas_call(
        body, name="dn_fwd", interpret=False,
        out_shape=[jax.ShapeDtypeStruct((rows, DN_DIM), F32),
                   jax.ShapeDtypeStruct((nch, DN_H, DN_D, DN_D), F32),
                   jax.ShapeDtypeStruct((nch, HB, HB), F32)],
        grid=(nch,),
        in_specs=[pl.BlockSpec((CH, DN_DIM), lambda n: (n, 0)),
                  pl.BlockSpec((CH, DN_DIM), lambda n: (n, 1)),
                  pl.BlockSpec((CH, DN_DIM), lambda n: (n, 2)),
                  pl.BlockSpec((CH, 128), lambda n: (n, 0)),
                  pl.BlockSpec((1, 8, CH), lambda n: (n, 0, 0))],
        out_specs=[pl.BlockSpec((CH, DN_DIM), lambda n: (n, 0)),
                   pl.BlockSpec((1, DN_H, DN_D, DN_D), lambda n: (n, 0, 0, 0)),
                   pl.BlockSpec((1, HB, HB), lambda n: (n, 0, 0))],
        scratch_shapes=[pltpu.VMEM((DN_H, DN_D, DN_D), F32)],
        compiler_params=_params(("arbitrary",)),
    )(qkv_n, qkv_n, qkv_n, bgcol, bgrow)


def dn_bwd(qkv_n, bgcol, bgrow, s_all, ti_all, do):
    rows = qkv_n.shape[0]
    nch = rows // CH

    def body(q_ref, k_ref, v_ref, bc_ref, br_ref, s_ref, ti_ref, do_ref, dq_ref, dk_ref, dv_ref, dbg_ref, ds_scr):
        n = pl.program_id(0)

        @pl.when(n == 0)
        def _():
            ds_scr[...] = jnp.zeros(ds_scr.shape, F32)

        incl, strict, upper, _, _ = _dn_masks()
        q, k, v, b_c, dm, kk, a, ed, rhs, qk, ekd, gl = _dn_chunk(q_ref, k_ref, v_ref, bc_ref, br_ref, incl, strict)
        tinv = ti_ref[0]
        g_o = _stack(do_ref[...])
        sol = _dot3(tinv, rhs)
        u, w = sol[:, :DN_D], sol[:, DN_D:]
        qd, kd = q * ed, k * ekd
        rsum = lambda t: jnp.sum(t, axis=1, keepdims=True)
        rows_of = [slice(h * CH, (h + 1) * CH) for h in range(DN_H)]
        s_h = [s_ref[0, h] for h in range(DN_H)]
        ds_h = [ds_scr[h] for h in range(DN_H)]
        v_new = jnp.concatenate([u[rs] - _dot1(w[rs], s) for rs, s in zip(rows_of, s_h)], axis=0)
        dv_new = _dot1(qk, g_o, 0, 0) + jnp.concatenate([_dot1(kd[rs], t) for rs, t in zip(rows_of, ds_h)], axis=0)
        dqd = jnp.concatenate([_dot1(g_o[rs], s, 1, 1) for rs, s in zip(rows_of, s_h)], axis=0)
        dkd = jnp.concatenate([_dot1(v_new[rs], t, 1, 1) for rs, t in zip(rows_of, ds_h)], axis=0)
        for h, rs in enumerate(rows_of):
            ds_scr[h] = _dot1(qd[rs], g_o[rs], 0, 0) + gl[h] * ds_h[h] - _dot1(w[rs], dv_new[rs], 0, 0)
        dw = jnp.concatenate([-_dot1(dv_new[rs], s, 1, 1) for rs, s in zip(rows_of, s_h)], axis=0)
        dqk = _dot1(g_o, v_new, 1, 1)
        drhs = _dot3(tinv, jnp.concatenate([dv_new, dw], axis=1), 0, 0)
        da = jnp.where(strict, -_dot1(drhs, sol, 1, 1), 0.0)
        drhs_u, drhs_w = drhs[:, :DN_D], drhs[:, DN_D:]
        s2 = rsum(drhs_w * k)
        dbeta = rsum(drhs_u * v) + s2 * ed + rsum(da * kk * dm)
        dkk = da * b_c * dm
        dqkr = dqk * dm
        mmat = da * a + dqk * qk
        tmp = rsum(dkd * kd)
        dd = (s2 * b_c * ed + rsum(mmat) - _dot3(mmat, jnp.ones((HB, 128), F32), 0, 0)[:, :1] + rsum(dqd * qd) - tmp)
        rowi = lax.broadcasted_iota(jnp.int32, (CH, 1), 0)
        last = []
        for h, rs in enumerate(rows_of):
            dgl = jnp.sum(rsum(s_h[h] * ds_h[h]), axis=0, keepdims=True)
            dd_last = jnp.sum(tmp[rs], axis=0, keepdims=True) + dgl * gl[h]
            last.append(jnp.where(rowi == CH - 1, dd_last, 0.0))
        dd = dd + jnp.concatenate(last, axis=0)
        dq_ref[...] = _unstack(_dot1(dqkr, k) + dqd * ed)
        dk_ref[...] = _unstack(drhs_w * (b_c * ed) + _dot1(dkk, k) + _dot1(dkk, k, 0, 0) + _dot1(dqkr, q, 0, 0)
                               + dkd * ekd)
        dv_ref[...] = _unstack(drhs_u * b_c)
        dg = _dot01(upper.astype(F32), jnp.broadcast_to(dd, (HB, 128)))[:, :1]
        lane = lax.broadcasted_iota(jnp.int32, (CH, 128), 1)
        out = jnp.zeros((CH, 128), F32)
        for h, rs in enumerate(rows_of):
            out = out + jnp.where(lane == h, dbeta[rs], 0.0) + jnp.where(lane == 4 + h, dg[rs], 0.0)
        dbg_ref[...] = out

    rev = lambda n: nch - 1 - n
    return pl.pallas_call(
        body, name="dn_bwd", interpret=False,
        out_shape=[jax.ShapeDtypeStruct((rows, DN_DIM), F32)] * 3 + [jax.ShapeDtypeStruct((rows, 128), F32)],
        grid=(nch,),
        in_specs=[pl.BlockSpec((CH, DN_DIM), lambda n: (rev(n), 0)),
                  pl.BlockSpec((CH, DN_DIM), lambda n: (rev(n), 1)),
                  pl.BlockSpec((CH, DN_DIM), lambda n: (rev(n), 2)),
                  pl.BlockSpec((CH, 128), lambda n: (rev(n), 0)),
                  pl.BlockSpec((1, 8, CH), lambda n: (rev(n), 0, 0)),
                  pl.BlockSpec((1, DN_H, DN_D, DN_D), lambda n: (rev(n), 0, 0, 0)),
                  pl.BlockSpec((1, HB, HB), lambda n: (rev(n), 0, 0)),
                  pl.BlockSpec((CH, DN_DIM), lambda n: (rev(n), 0))],
        out_specs=[pl.BlockSpec((CH, DN_DIM), lambda n: (rev(n), 0))] * 3 + [pl.BlockSpec((CH, 128), lambda n: (rev(n), 0))],
        scratch_shapes=[pltpu.VMEM((DN_H, DN_D, DN_D), F32)],
        compiler_params=_params(("arbitrary",)),
    )(qkv_n, qkv_n, qkv_n, bgcol, bgrow, s_all, ti_all, do)


def _swa_valid(n):
    r = lax.broadcasted_iota(jnp.int32, (BLK, 3 * BLK), 0)
    c3 = lax.broadcasted_iota(jnp.int32, (BLK, 3 * BLK), 1)
    c = c3 % BLK
    lo = jnp.where(c3 < BLK, PAD, jnp.where(c3 < 2 * BLK, r + 1 + jnp.where(n >= 2, 0, BLK), 0))
    hi = jnp.where(c3 < BLK, r + jnp.where(n >= 1, BLK, 0), jnp.where(c3 < 2 * BLK, BLK, r - jnp.where(n >= 1, 0, BLK)))
    return jnp.logical_and(c >= lo, c <= hi)


def _swa_probs(q, kcat, valid, sink):
    s = jnp.where(valid, _dot(q, kcat, 1, 1), -1e30)
    m = jnp.maximum(jnp.max(s, axis=1, keepdims=True), sink)
    e = jnp.where(valid, jnp.exp(s - m), 0.0)
    es = jnp.exp(sink - m)
    inv = 1.0 / (jnp.sum(e, axis=1, keepdims=True) + es)
    return e * inv, es * inv


def _swa_specs():
    q = pl.BlockSpec((SWA_H, BLK, SWA_D), lambda n: (0, n, 0))
    km = pl.BlockSpec((SWA_KV, BLK, SWA_D), lambda n: (0, 0, 0))
    kp = pl.BlockSpec((SWA_KV, BLK, SWA_D), lambda n: (0, jnp.maximum(n - 1, 0), 0))
    kc = pl.BlockSpec((SWA_KV, BLK, SWA_D), lambda n: (0, n, 0))
    return [q, km, kp, kc, km, kp, kc]


def swa_fwd(qh, kh, vh, sinks):
    rows = qh.shape[1]
    nb = rows // BLK

    def body(q_ref, km, kp, kc, vm, vp, vc, sk_ref, o_ref):
        n = pl.program_id(0)
        valid = _swa_valid(n)
        outs = []
        for h in range(SWA_KV):
            kcat = jnp.concatenate([km[h], kp[h], kc[h]], axis=0)
            vcat = jnp.concatenate([vm[h], vp[h], vc[h]], axis=0)
            for g in range(4):
                p, _ = _swa_probs(q_ref[4 * h + g], kcat, valid, sk_ref[4 * h + g])
                outs.append(_dot(p.astype(BF16), vcat))
        o_ref[...] = jnp.concatenate(outs, axis=1).astype(BF16)

    return pl.pallas_call(
        body, name="swa_fwd", interpret=False,
        out_shape=jax.ShapeDtypeStruct((rows, SWA_H * SWA_D), BF16),
        grid=(nb,),
        in_specs=_swa_specs() + [pl.BlockSpec(memory_space=pltpu.SMEM)],
        out_specs=pl.BlockSpec((BLK, SWA_H * SWA_D), lambda n: (n, 0)),
        compiler_params=_params(("parallel",)),
    )(qh, kh, kh, kh, vh, vh, vh, sinks)


def swa_bwd(qh, kh, vh, sinks, do):
    rows = qh.shape[1]
    nb = rows // BLK

    def body(q_ref, km, kp, kc, vm, vp, vc, do_ref, sk_ref, dq_ref, dk_ref, dv_ref, dsk_ref):
        n = pl.program_id(0)

        @pl.when(n == 0)
        def _():
            dk_ref[...] = jnp.zeros(dk_ref.shape, F32)
            dv_ref[...] = jnp.zeros(dv_ref.shape, F32)

        valid = _swa_valid(n)
        g_all = do_ref[...]
        rowi = lax.broadcasted_iota(jnp.int32, (SWA_H, 128), 0)
        dsk = jnp.zeros((SWA_H, 128), F32)
        pm = pl.multiple_of(jnp.maximum(n - 1, 0) * BLK, BLK)
        pc = pl.multiple_of(n * BLK, BLK)
        for h in range(SWA_KV):
            kcat = jnp.concatenate([km[h], kp[h], kc[h]], axis=0)
            vcat = jnp.concatenate([vm[h], vp[h], vc[h]], axis=0)
            dkc = jnp.zeros((3 * BLK, SWA_D), F32)
            dvc = jnp.zeros((3 * BLK, SWA_D), F32)
            for g in range(4):
                j = 4 * h + g
                q = q_ref[j]
                p, ps = _swa_probs(q, kcat, valid, sk_ref[j])
                g_o = g_all[:, j * SWA_D:(j + 1) * SWA_D]
                dp = _dot(g_o, vcat, 1, 1)
                delta = jnp.sum(p * dp, axis=1, keepdims=True)
                ds = (p * (dp - delta)).astype(BF16)
                dq_ref[j] = _dot(ds, kcat)
                dkc = dkc + _dot(ds, q, 0, 0)
                dvc = dvc + _dot(p.astype(BF16), g_o, 0, 0)
                dsk = dsk + jnp.where(rowi == j, -jnp.sum(ps * delta, axis=0, keepdims=True), 0.0)
            lanes = slice(h * SWA_D, (h + 1) * SWA_D)
            for ref, val in ((dk_ref, dkc), (dv_ref, dvc)):
                ref[0:BLK, lanes] += val[0:BLK]
                ref[pl.ds(pm, BLK), lanes] += val[BLK:2 * BLK]
                ref[pl.ds(pc, BLK), lanes] += val[2 * BLK:]
        dsk_ref[0] = dsk

    return pl.pallas_call(
        body, name="swa_bwd", interpret=False,
        out_shape=[jax.ShapeDtypeStruct((SWA_H, rows, SWA_D), F32),
                   jax.ShapeDtypeStruct((rows, SWA_KV * SWA_D), F32),
                   jax.ShapeDtypeStruct((rows, SWA_KV * SWA_D), F32),
                   jax.ShapeDtypeStruct((nb, SWA_H, 128), F32)],
        grid=(nb,),
        in_specs=_swa_specs() + [pl.BlockSpec((BLK, SWA_H * SWA_D), lambda n: (n, 0)),
                                 pl.BlockSpec(memory_space=pltpu.SMEM)],
        out_specs=[pl.BlockSpec((SWA_H, BLK, SWA_D), lambda n: (0, n, 0)),
                   pl.BlockSpec((rows, SWA_KV * SWA_D), lambda n: (0, 0)),
                   pl.BlockSpec((rows, SWA_KV * SWA_D), lambda n: (0, 0)),
                   pl.BlockSpec((1, SWA_H, 128), lambda n: (n, 0, 0))],
        compiler_params=_params(("arbitrary",)),
    )(qh, kh, kh, kh, vh, vh, vh, do, sinks)


def qknorm_fwd(qkv, qw, kw):
    rows = qkv.shape[0]
    tr = _pick(rows, (384, 128))
    scale = SWA_D ** -0.5

    def fn(i, x, qwv, kwv):
        def normed(j, wv, sc):
            xs = x[:, j * SWA_D:(j + 1) * SWA_D]
            r = lax.rsqrt(jnp.mean(xs * xs, axis=1, keepdims=True) + EPS)
            return (xs * r * wv * sc)[None]
        qo = jnp.concatenate([normed(j, qwv, scale) for j in range(SWA_H)], axis=0)
        ko = jnp.concatenate([normed(SWA_H + j, kwv, 1.0) for j in range(SWA_KV)], axis=0)
        vo = jnp.concatenate([x[:, (SWA_H + SWA_KV + j) * SWA_D:(SWA_H + SWA_KV + j + 1) * SWA_D][None]
                              for j in range(SWA_KV)], axis=0)
        return qo, ko, vo

    hm = lambda nh: ((nh, rows, SWA_D), BF16, (nh, tr, SWA_D), lambda i: (0, i, 0))
    return rowwise(fn, [cols(qkv, tr), whole(qw), whole(kw)], [hm(SWA_H), hm(SWA_KV), hm(SWA_KV)],
                   steps=rows // tr, name="qknorm_fwd")


def qknorm_bwd(qkv, qw, kw, dqh, dkh, dvh):
    rows = qkv.shape[0]
    tr = _pick(rows, (384, 128))
    scale = SWA_D ** -0.5

    def fn(i, x, qwv, kwv, dq, dk, dv):
        pieces = []
        dws = [jnp.zeros((1, SWA_D), F32), jnp.zeros((1, SWA_D), F32)]

        def one(j, dy, wv, sc, which):
            xs = x[:, j * SWA_D:(j + 1) * SWA_D]
            r = lax.rsqrt(jnp.mean(xs * xs, axis=1, keepdims=True) + EPS)
            xh = xs * r
            gw = dy * wv * sc
            pieces.append(r * (gw - xh * jnp.mean(gw * xh, axis=1, keepdims=True)))
            dws[which] = dws[which] + jnp.sum(dy * sc * xh, axis=0, keepdims=True)

        for j in range(SWA_H):
            one(j, dq[j], qwv, scale, 0)
        for j in range(SWA_KV):
            one(SWA_H + j, dk[:, j * SWA_D:(j + 1) * SWA_D], kwv, 1.0, 1)
        pieces.append(dv)
        return jnp.concatenate(pieces, axis=1), dws[0], dws[1]

    return rowwise(fn, [cols(qkv, tr), whole(qw), whole(kw), heads(dqh, tr), cols(dkh, tr), cols(dvh, tr)],
                   [out2d(rows, 1536, BF16, tr)], steps=rows // tr, name="qknorm_bwd",
                   accs=[((1, SWA_D), F32), ((1, SWA_D), F32)])


def _place():
    return lax.axis_index("x"), lax.axis_index("y"), lax.axis_index("c")


ANY = pl.BlockSpec(memory_space=pl.ANY)


def gather_weights(big, small):
    def body(big_ref, small_ref, obig, osmall, ssem, rsem, lsem):
        x, y, c = _place()
        me = 2 * x + y
        chips = [(1 - x, y), (x, 1 - y), (1 - x, 1 - y)]

        def half(s, hh):
            return obig.at[s, pl.ds(hh * R_HALF, R_HALF), :]

        def rcopy(k, src, dst, to):
            return pltpu.make_async_remote_copy(src_ref=src, dst_ref=dst, send_sem=ssem.at[k], recv_sem=rsem.at[k],
                                                device_id=to, device_id_type=MESH)

        loc = [pltpu.make_async_copy(small_ref, osmall.at[me], lsem.at[0])]
        for cp in loc:
            cp.start()
        sends = []
        for j, (px, py) in enumerate(chips):
            sends.append(rcopy(j, big_ref.at[pl.ds(c * R_HALF, R_HALF), :], half(me, c), (px, py, c)))
            sends.append(rcopy(6 + j, small_ref, osmall.at[me], (px, py, c)))
        for cp in sends:
            cp.start()
        for j, (px, py) in enumerate(chips):
            s = 2 * px + py
            rcopy(j, half(s, c), half(s, c), (x, y, c)).wait_recv()
            fwd = rcopy(3 + j, half(s, c), half(s, c), (x, y, 1 - c))
            fwd.start()
            sends.append(fwd)
        for j, (px, py) in enumerate(chips):
            s = 2 * px + py
            rcopy(3 + j, half(s, 1 - c), half(s, 1 - c), (x, y, c)).wait_recv()
            rcopy(6 + j, osmall.at[s], osmall.at[s], (x, y, c)).wait_recv()
        for cp in sends:
            cp.wait_send()
        for cp in loc:
            cp.wait()

    return pl.pallas_call(
        body, name="gather_weights", interpret=False,
        out_shape=[jax.ShapeDtypeStruct((4, R_BIG, 1024), BF16), jax.ShapeDtypeStruct((4, SW_ROWS, 1024), F32)],
        in_specs=[ANY, ANY], out_specs=[ANY, ANY],
        scratch_shapes=[pltpu.SemaphoreType.DMA((9,)), pltpu.SemaphoreType.DMA((9,)), pltpu.SemaphoreType.DMA((1,))],
    )(big, small)


def swap_halves(g):
    def body(g_ref, o_ref, ssem, rsem):
        x, y, c = _place()
        cp = pltpu.make_async_remote_copy(
            src_ref=g_ref.at[:, pl.ds((1 - c) * R_HALF, R_HALF), :], dst_ref=o_ref,
            send_sem=ssem, recv_sem=rsem, device_id=(x, y, 1 - c), device_id_type=MESH)
        cp.start()
        cp.wait()

    return pl.pallas_call(
        body, name="swap_halves", interpret=False,
        out_shape=jax.ShapeDtypeStruct((4, R_HALF, 1024), BF16),
        in_specs=[ANY], out_specs=ANY,
        scratch_shapes=[pltpu.SemaphoreType.DMA, pltpu.SemaphoreType.DMA],
    )(g)


def pair_sum(g, other, c_idx):
    tr = 512
    nbk = R_HALF // tr

    def body(c_ref, g_ref, o_ref, out_ref):
        out_ref[...] = (g_ref[...].astype(F32) + o_ref[...].astype(F32)).astype(BF16)

    return pl.pallas_call(
        body, name="pair_sum", interpret=False,
        out_shape=jax.ShapeDtypeStruct((4, R_HALF, 1024), BF16),
        grid_spec=pltpu.PrefetchScalarGridSpec(
            num_scalar_prefetch=1, grid=(4, nbk),
            in_specs=[pl.BlockSpec((1, tr, 1024), lambda s, i, c_ref: (s, c_ref[0] * nbk + i, 0)),
                      pl.BlockSpec((1, tr, 1024), lambda s, i, c_ref: (s, i, 0))],
            out_specs=pl.BlockSpec((1, tr, 1024), lambda s, i, c_ref: (s, i, 0))),
        compiler_params=_params(("parallel", "parallel")),
    )(c_idx, g, other)


def scatter_chips(p):
    def body(p_ref, o_ref, ssem, rsem):
        x, y, c = _place()
        chips = [(1 - x, y), (x, 1 - y), (1 - x, 1 - y)]
        cps = [pltpu.make_async_remote_copy(src_ref=p_ref.at[2 * px + py], dst_ref=o_ref.at[j],
                                            send_sem=ssem.at[j], recv_sem=rsem.at[j],
                                            device_id=(px, py, c), device_id_type=MESH)
               for j, (px, py) in enumerate(chips)]
        for cp in cps:
            cp.start()
        for cp in cps:
            cp.wait()

    return pl.pallas_call(
        body, name="scatter_chips", interpret=False,
        out_shape=jax.ShapeDtypeStruct((3, R_HALF, 1024), BF16),
        in_specs=[ANY], out_specs=ANY,
        scratch_shapes=[pltpu.SemaphoreType.DMA((3,)), pltpu.SemaphoreType.DMA((3,))],
    )(p)


def chip_sum(p, got, idx):
    tr = 512
    nbk = R_HALF // tr

    def body(idx_ref, p_ref, g_ref, out_ref):
        acc = p_ref[0].astype(F32)
        for j in range(3):
            acc = acc + g_ref[j].astype(F32)
        out_ref[0] = acc

    return pl.pallas_call(
        body, name="chip_sum", interpret=False,
        out_shape=jax.ShapeDtypeStruct((2, R_HALF, 1024), F32),
        grid_spec=pltpu.PrefetchScalarGridSpec(
            num_scalar_prefetch=1, grid=(nbk,),
            in_specs=[pl.BlockSpec((1, tr, 1024), lambda i, idx_ref: (idx_ref[0], i, 0)),
                      pl.BlockSpec((3, tr, 1024), lambda i, idx_ref: (0, i, 0))],
            out_specs=pl.BlockSpec((1, tr, 1024), lambda i, idx_ref: (idx_ref[1], i, 0))),
        compiler_params=_params(("parallel",)),
    )(idx, p, got)


def join_halves(q):
    def body(q_ref, o_ref, ssem, rsem):
        x, y, c = _place()
        cp = pltpu.make_async_remote_copy(src_ref=q_ref.at[c], dst_ref=o_ref.at[c], send_sem=ssem, recv_sem=rsem,
                                          device_id=(x, y, 1 - c), device_id_type=MESH)
        cp.start()
        pltpu.make_async_remote_copy(src_ref=q_ref.at[c], dst_ref=o_ref.at[1 - c], send_sem=ssem, recv_sem=rsem,
                                     device_id=(x, y, 1 - c), device_id_type=MESH).wait_recv()
        cp.wait_send()

    return pl.pallas_call(
        body, name="join_halves", interpret=False,
        out_shape=jax.ShapeDtypeStruct((2, R_HALF, 1024), F32),
        in_specs=[ANY], out_specs=ANY, input_output_aliases={0: 0},
        scratch_shapes=[pltpu.SemaphoreType.DMA, pltpu.SemaphoreType.DMA],
    )(q)


def gather_small(v):
    def body(v_ref, o_ref, ssem, rsem, lsem):
        x, y, c = _place()
        loc = pltpu.make_async_copy(v_ref, o_ref.at[4 * x + 2 * y + c], lsem)
        loc.start()
        cps = []
        for k in range(1, 8):
            fx, fy, fc = (k >> 2) & 1, (k >> 1) & 1, k & 1
            px = 1 - x if fx else x
            py = 1 - y if fy else y
            pc = 1 - c if fc else c
            cps.append((pltpu.make_async_remote_copy(
                src_ref=v_ref, dst_ref=o_ref.at[4 * x + 2 * y + c], send_sem=ssem.at[k - 1], recv_sem=rsem.at[k - 1],
                device_id=(px, py, pc), device_id_type=MESH), 4 * px + 2 * py + pc))
        for cp, _ in cps:
            cp.start()
        for k, (cp, peer) in enumerate(cps):
            pltpu.make_async_remote_copy(
                src_ref=v_ref, dst_ref=o_ref.at[peer], send_sem=ssem.at[k], recv_sem=rsem.at[k],
                device_id=(x, y, c), device_id_type=MESH).wait_recv()
        for cp, _ in cps:
            cp.wait_send()
        loc.wait()

    return pl.pallas_call(
        body, name="gather_small", interpret=False,
        out_shape=jax.ShapeDtypeStruct((8, SV_ROWS, 1024), F32),
        in_specs=[ANY], out_specs=ANY,
        scratch_shapes=[pltpu.SemaphoreType.DMA((7,)), pltpu.SemaphoreType.DMA((7,)), pltpu.SemaphoreType.DMA],
    )(v)


def sum_slots(a):
    def fn(i, t):
        acc = t[0]
        for k in range(1, 8):
            acc = acc + t[k]
        return acc

    return rowwise(fn, [whole(a)], [((SV_ROWS, 1024), F32, (SV_ROWS, 1024), lambda i: (0, 0))], steps=1,
                   name="sum_slots")[0]


def _head_rms(x, nw):
    xs, rs = [], []
    for h in range(DN_H):
        xh = x[:, h * DN_D:(h + 1) * DN_D]
        r = lax.rsqrt(jnp.mean(xh * xh, axis=1, keepdims=True) + EPS)
        xs.append(xh * r)
        rs.append(r)
    return xs, rs


def bg_fwd(p, alog, dtb):
    rows = p.shape[0]
    tr = _pick(rows, (384, 128))

    def fn(i, x, al, dt):
        lane = lax.broadcasted_iota(jnp.int32, (tr, 128), 1)
        row = i * tr + lax.broadcasted_iota(jnp.int32, (tr, 128), 0)
        g = -jnp.exp(al) * _softplus(x + dt)
        out = jnp.where(lane < 4, _sigmoid(x), jnp.where(lane < 8, g, 0.0))
        return jnp.where(row >= PAD, out, 0.0)

    return rowwise(fn, [cols(p, tr, 128, BG0 // 128), whole(alog), whole(dtb)], [out2d(rows, 128, F32, tr)],
                   steps=rows // tr, name="bg_fwd")[0]


def bg_bwd(p, alog, dtb, dbg):
    rows = p.shape[0]
    tr = _pick(rows, (384, 128))

    def fn(i, x, al, dt, g_in):
        lane = lax.broadcasted_iota(jnp.int32, (tr, 128), 1)
        row = i * tr + lax.broadcasted_iota(jnp.int32, (tr, 128), 0)
        live = row >= PAD
        is_b = jnp.logical_and(live, lane < 4)
        is_g = jnp.logical_and(live, jnp.logical_and(lane >= 4, lane < 8))
        beta = _sigmoid(x)
        ea = jnp.exp(al)
        g = -ea * _softplus(x + dt)
        dalpha = jnp.where(is_g, g_in * (-ea) * _sigmoid(x + dt), 0.0)
        dx = jnp.where(is_b, g_in * beta * (1.0 - beta), dalpha)
        dal = jnp.sum(jnp.where(is_g, g_in * g, 0.0), axis=0, keepdims=True)
        return jnp.concatenate([dx, jnp.zeros((tr, 128), F32)], axis=1), dal, jnp.sum(dalpha, axis=0, keepdims=True)

    return rowwise(fn, [cols(p, tr, 128, BG0 // 128), whole(alog), whole(dtb), cols(dbg, tr)],
                   [out2d(rows, 256, BF16, tr)], steps=rows // tr, name="bg_bwd",
                   accs=[((1, 128), F32), ((1, 128), F32)])


def dn_qkv_post(j, y):
    xs = _silu(y)
    sc = jnp.where(j == 0, DN_D ** -0.5, 1.0)
    outs = []
    for h in range(DN_H):
        xh = xs[:, h * DN_D:(h + 1) * DN_D]
        r = lax.rsqrt(jnp.sum(xh * xh, axis=1, keepdims=True) + EPS)
        outs.append(jnp.where(j < 2, xh * r * sc, xh))
    return jnp.concatenate(outs, axis=1), y


def dn_qkv_bwd(cq, dq, dk, dv):
    rows = cq.shape[0]
    tr = _pick(rows, (384, 128))

    def fn(i, c0, c1, c2, g0, g1, g2):
        pieces = []
        for kind, (cv, g) in enumerate(((c0, g0), (c1, g1), (c2, g2))):
            xs = _silu(cv)
            if kind < 2:
                sc = DN_D ** -0.5 if kind == 0 else 1.0
                ds = []
                for h in range(DN_H):
                    sl = slice(h * DN_D, (h + 1) * DN_D)
                    xh, gh = xs[:, sl], g[:, sl]
                    r = lax.rsqrt(jnp.sum(xh * xh, axis=1, keepdims=True) + EPS)
                    xn = xh * r
                    ds.append(sc * r * (gh - xn * jnp.sum(gh * xn, axis=1, keepdims=True)))
                dxs = jnp.concatenate(ds, axis=1)
            else:
                dxs = g
            pieces.append(dxs * _dsilu(cv))
        return jnp.concatenate(pieces, axis=1)

    ins = [cols(cq, tr, DN_DIM, k) for k in range(3)] + [cols(t, tr) for t in (dq, dk, dv)]
    return rowwise(fn, ins, [out2d(rows, 3 * DN_DIM, F32, tr)], steps=rows // tr, name="dn_qkv_bwd")[0]


def dn_out_fwd(o, p, nw):
    rows = o.shape[0]
    tr = _pick(rows, (384, 128))

    def fn(i, ov, z, w):
        xs, _ = _head_rms(ov, w)
        return jnp.concatenate(xs, axis=1) * jnp.concatenate([w] * DN_H, axis=1) * _silu(z)

    return rowwise(fn, [cols(o, tr), cols(p, tr, DN_DIM, 6), whole(nw)], [out2d(rows, DN_DIM, BF16, tr)],
                   steps=rows // tr, name="dn_out_fwd")[0]


def dn_out_bwd(o, p, nw, dymix):
    rows = o.shape[0]
    tr = _pick(rows, (384, 128))

    def fn(i, ov, z, w, dy):
        xs, rs = _head_rms(ov, w)
        sz = _silu(z)
        dn = dy * sz
        dos, dw = [], jnp.zeros((1, DN_D), F32)
        for h in range(DN_H):
            sl = slice(h * DN_D, (h + 1) * DN_D)
            gw = dn[:, sl] * w
            dos.append(rs[h] * (gw - xs[h] * jnp.mean(gw * xs[h], axis=1, keepdims=True)))
            dw = dw + jnp.sum(dn[:, sl] * xs[h], axis=0, keepdims=True)
        n = jnp.concatenate(xs, axis=1) * jnp.concatenate([w] * DN_H, axis=1)
        return jnp.concatenate(dos, axis=1), dy * n * _dsilu(z), dw

    return rowwise(fn, [cols(o, tr), cols(p, tr, DN_DIM, 6), whole(nw), cols(dymix, tr, DN_DIM, 1)],
                   [out2d(rows, DN_DIM, F32, tr), out2d(rows, DN_DIM, BF16, tr)], steps=rows // tr,
                   name="dn_out_bwd", accs=[((1, DN_D), F32)])


def conv_a_pre_bwd(dymix, cv, p):
    rows = cv.shape[0]
    tr = _pick(rows, (384, 128))

    def fn(i, dy, c, go):
        return dy * c, dy * go

    return rowwise(fn, [cols(dymix, tr, D_CONV, 0), cols(cv, tr), cols(p, tr, D_CONV, 1)],
                   [out2d(rows, D_CONV, BF16, tr), out2d(rows, D_CONV, F32, tr)], steps=rows // tr,
                   name="conv_a_pre_bwd")


def ffn_act_bwd(da, gc, u):
    rows = da.shape[0]
    tr = _pick(rows, (384, 128))

    def fn(i, g, c, val):
        g, c, val = g.astype(F32), c.astype(F32), val.astype(F32)
        return g * _silu(c), g * val * _dsilu(c)

    return rowwise(fn, [cols(da, tr), cols(gc, tr), cols(u, tr, D_FF, 1)],
                   [out2d(rows, D_FF, BF16, tr), out2d(rows, D_FF, F32, tr)], steps=rows // tr, name="ffn_act_bwd")


def _rows8(w):
    return jnp.pad(w.astype(F32), ((0, 8 - w.shape[0]), (0, 0)))


def _lanes(v, at):
    return jnp.pad(v.astype(F32), (at, 128 - at - v.shape[0]))[None]


def ffn_fwd(h, nw, w_up, cw8, w_down, tag):
    rows = h.shape[0]
    tr = _pick(rows, (384, 128))
    hn = rms_fwd(h, nw, name=f"ffn{tag}_norm")
    u = mm(hn, w_up, out_dtype=BF16, name=f"ffn{tag}_up")
    a, gc = conv_fwd([(u, 0)], cw8, 3, rows=rows, c=D_FF, tc=1408, tr=tr, name=f"ffn{tag}_conv",
                     post=lambda j, y, val: (_silu(y) * val.astype(F32), y), extras=[(u, 2)], outs=[BF16, BF16])
    out = mm(a, w_down, add=h, name=f"ffn{tag}_down")
    return out, (hn, u, a, gc)


def ffn_bwd(h, nw, w_up, cw8, w_down, saved, dh, tag):
    hn, u, a, gc = saved
    rows = h.shape[0]
    tr = _pick(rows, (384, 128))
    da = mm(dh, w_down, tb=True, out_dtype=BF16, name=f"ffn{tag}_down_dx")
    d_w_down = mm(a, dh, ta=True, name=f"ffn{tag}_down_dw")
    dval, dgc = ffn_act_bwd(da, gc, u)
    dgate, d_cw = conv_bwd([(u, 0)], cw8, 3, dgc, rows=rows, c=D_FF, tc=1408, tr=tr, name=f"ffn{tag}_conv_bwd",
                           post=lambda dx: dx, outs=[BF16])
    du = jnp.concatenate([dgate, dval], axis=1)
    dhn = mm(du, w_up, tb=True, name=f"ffn{tag}_up_dx")
    d_w_up = mm(hn, du, ta=True, name=f"ffn{tag}_up_dw")
    dh_new, d_nw = rms_bwd(h, nw, dhn, dh, name=f"ffn{tag}_norm_bwd")
    return dh_new, d_nw, d_w_up, d_cw, d_w_down


def mixer_fwd(h, nw, w_in, ca8, dc8, alog, dtb, dnw, w_out):
    rows = h.shape[0]
    tr = _pick(rows, (384, 128))
    hn = rms_fwd(h, nw, name="mix_norm")
    p = mm(hn, w_in, name="mix_in")
    y_a, cv = conv_fwd([(p, 0), (p, 2)], ca8, 3, rows=rows, c=D_CONV, tc=D_CONV, tr=tr, name="conv_a",
                       pre=lambda gi, ah: gi * ah, post=lambda j, y, go: (go * y, y), extras=[(p, 1)],
                       outs=[BF16, F32])
    qkv_n, cq = conv_fwd([(p, 3)], dc8, 4, rows=rows, c=3 * DN_DIM, tc=DN_DIM, tr=tr, name="dn_conv",
                         post=dn_qkv_post, outs=[F32, F32])
    bgcol = bg_fwd(p, alog, dtb)
    bgrow = bgcol[:, :8].reshape(rows // CH, CH, 8).transpose(0, 2, 1)
    o, s_all, ti_all = dn_fwd(qkv_n, bgcol, bgrow)
    y_b = dn_out_fwd(o, p, dnw)
    ymix = jnp.concatenate([y_a, y_b], axis=1)
    out = mm(ymix, w_out, add=h, name="mix_out")
    return out, (hn, p, cv, qkv_n, cq, bgcol, bgrow, o, s_all, ti_all, ymix)


def mixer_bwd(h, nw, w_in, ca8, dc8, alog, dtb, dnw, w_out, saved, dh):
    hn, p, cv, qkv_n, cq, bgcol, bgrow, o, s_all, ti_all, ymix = saved
    rows = h.shape[0]
    tr = _pick(rows, (384, 128))
    dymix = mm(dh, w_out, tb=True, name="mix_out_dx")
    d_w_out = mm(ymix, dh, ta=True, name="mix_out_dw")
    do, dz, d_dnw = dn_out_bwd(o, p, dnw, dymix)
    dq, dk, dv, dbg = dn_bwd(qkv_n, bgcol, bgrow, s_all, ti_all, do)
    dbg_p, d_alog, d_dtb = bg_bwd(p, alog, dtb, dbg)
    dcq = dn_qkv_bwd(cq, dq, dk, dv)
    dqkv, d_dc = conv_bwd([(p, 3)], dc8, 4, dcq, rows=rows, c=3 * DN_DIM, tc=DN_DIM, tr=tr, name="dn_conv_bwd",
                          post=lambda dx: dx, outs=[BF16])
    dgo, dcv = conv_a_pre_bwd(dymix, cv, p)
    dgi, dah, d_ca = conv_bwd([(p, 0), (p, 2)], ca8, 3, dcv, rows=rows, c=D_CONV, tc=D_CONV, tr=tr,
                              name="conv_a_bwd", pre=lambda gi, ah: gi * ah,
                              post=lambda dm, gi, ah: (dm * ah, dm * gi), extras=[(p, 0), (p, 2)], outs=[BF16, BF16])
    dp = jnp.concatenate([dgi, dgo, dah, dqkv, dz, dbg_p], axis=1)
    dhn = mm(dp, w_in, tb=True, name="mix_in_dx")
    d_w_in = mm(hn, dp, ta=True, name="mix_in_dw")
    dh_new, d_nw = rms_bwd(h, nw, dhn, dh, name="mix_norm_bwd")
    return dh_new, d_nw, d_w_in, d_ca, d_dc, d_alog, d_dtb, d_dnw, d_w_out


def swa_layer_fwd(h, nw, wqkv, qw, kw, sinks, wo):
    hn = rms_fwd(h, nw, name="swa_norm")
    qkv = mm(hn, wqkv, name="swa_qkv")
    qh, kh, vh = qknorm_fwd(qkv, qw, kw)
    att = swa_fwd(qh, kh, vh, sinks)
    out = mm(att, wo, add=h, name="swa_out")
    return out, (hn, qkv, qh, kh, vh, att)


def swa_layer_bwd(h, nw, wqkv, qw, kw, sinks, wo, saved, dh):
    hn, qkv, qh, kh, vh, att = saved
    datt = mm(dh, wo, tb=True, out_dtype=BF16, name="swa_out_dx")
    d_wo = mm(att, dh, ta=True, name="swa_out_dw")
    dqh, dkh, dvh, dsk = swa_bwd(qh, kh, vh, sinks, datt)
    dqkv, d_qw, d_kw = qknorm_bwd(qkv, qw, kw, dqh, dkh, dvh)
    dhn = mm(dqkv, wqkv, tb=True, name="swa_qkv_dx")
    d_wqkv = mm(hn, dqkv, ta=True, name="swa_qkv_dw")
    dh_new, d_nw = rms_bwd(h, nw, dhn, dh, name="swa_norm_bwd")
    d_sinks = jnp.sum(dsk[:, :, 0], axis=0)
    return dh_new, d_nw, d_wqkv, d_qw, d_kw, d_sinks, d_wo


BIG_ROWS = (898, 256, 256, 64, 64, 256, 2816, 1408)


def _big_offsets():
    offs, o = [], 0
    for r in BIG_ROWS:
        offs.append(o)
        o += r
    return offs, o


def pack_big_shard(parts, dtype):
    flat = [t.astype(dtype).reshape(-1, 1024) for t in parts]
    used = sum(t.shape[0] for t in flat)
    return jnp.concatenate(flat + [jnp.zeros((R_BIG - used, 1024), dtype)], axis=0)


def unpack_big_shard(flat, shapes):
    offs, _ = _big_offsets()
    return [flat[o:o + r].reshape(s) for o, r, s in zip(offs, BIG_ROWS, shapes)]


def unpack_big_full(g):
    offs, _ = _big_offsets()
    sl = lambda k: g[:, offs[k]:offs[k] + BIG_ROWS[k]]
    w_in = sl(0).reshape(4, D, 898).transpose(1, 0, 2).reshape(D, IN_DIM)
    w_out = sl(1).reshape(D, D)
    wq = sl(2).reshape(D, D)
    wk = sl(3).reshape(D, 256)
    wv = sl(4).reshape(D, 256)
    wo = sl(5).reshape(D, D)
    w_up = sl(6).reshape(4, 2, D, 1408).transpose(1, 2, 0, 3).reshape(2, D, 2 * D_FF)
    w_down = sl(7).reshape(4, 2, 704, D).transpose(1, 0, 2, 3).reshape(2, D_FF, D)
    return w_in, w_out, wq, wk, wv, wo, w_up, w_down


def pack_big_full(w_in, w_out, wq, wk, wv, wo, w_up, w_down, dtype):
    parts = [
        w_in.reshape(D, 4, 898).transpose(1, 0, 2),
        w_out.reshape(4, 256, D), wq.reshape(4, 256, D), wk.reshape(4, 256, 256), wv.reshape(4, 256, 256),
        wo.reshape(4, 256, D),
        w_up.reshape(2, D, 4, 1408).transpose(2, 0, 1, 3),
        w_down.reshape(2, 4, 704, D).transpose(1, 0, 2, 3),
    ]
    flat = [t.astype(dtype).reshape(4, -1, 1024) for t in parts]
    used = sum(t.shape[1] for t in flat)
    return jnp.concatenate(flat + [jnp.zeros((4, R_BIG - used, 1024), dtype)], axis=1)


def _flat_pad(parts, rows):
    v = jnp.concatenate([t.astype(F32).reshape(-1) for t in parts])
    return jnp.pad(v, (0, rows * 1024 - v.shape[0])).reshape(rows, 1024)


def _split_flat(flat, shapes):
    v = flat.reshape(-1)
    out, o = [], 0
    for s in shapes:
        n = 1
        for d_ in s:
            n *= d_
        out.append(v[o:o + n].reshape(s))
        o += n
    return out


def local_step(x0, target0, meta_full, anw, fnw, w_in, ca8, dc8, alog, dtb, dnw, w_out, wqkv, qw, kw, sinks, wo,
               w_up, fc8, w_down):
    h0 = jnp.concatenate([jnp.zeros((PAD, D), F32), meta_full, x0], axis=0)
    h1, s_mix = mixer_fwd(h0, anw[0], w_in, ca8, dc8, alog, dtb, dnw, w_out)
    h2, s_f0 = ffn_fwd(h1, fnw[0], w_up[0], fc8[0], w_down[0], 0)
    h3, s_swa = swa_layer_fwd(h2, anw[1], wqkv, qw, kw, sinks, wo)
    h4, s_f1 = ffn_fwd(h3, fnw[1], w_up[1], fc8[1], w_down[1], 1)
    dh, loss_l = loss_grad(h4, target0)
    dh, d_fnw1, d_up1, d_fc1, d_down1 = ffn_bwd(h3, fnw[1], w_up[1], fc8[1], w_down[1], s_f1, dh, 1)
    dh, d_anw1, d_wqkv, d_qw, d_kw, d_sinks, d_wo = swa_layer_bwd(h2, anw[1], wqkv, qw, kw, sinks, wo, s_swa, dh)
    dh, d_fnw0, d_up0, d_fc0, d_down0 = ffn_bwd(h1, fnw[0], w_up[0], fc8[0], w_down[0], s_f0, dh, 0)
    dh, d_anw0, d_w_in, d_ca, d_dc, d_alog, d_dtb, d_dnw, d_w_out = mixer_bwd(
        h0, anw[0], w_in, ca8, dc8, alog, dtb, dnw, w_out, s_mix, dh)
    return (dh, loss_l, d_anw0, d_anw1, d_fnw0, d_fnw1, d_w_in, d_ca, d_dc, d_alog, d_dtb, d_dnw, d_w_out, d_wqkv,
            d_qw, d_kw, d_sinks, d_wo, d_up0, d_up1, d_fc0, d_fc1, d_down0, d_down1)


def kernel(x, meta_tokens, attn_norm_w, ffn_norm_w, mix_w_in, conv_a_w, dn_conv_w, dn_a_log, dn_dt_bias, dn_norm_w, mix_w_out, swa_wq, swa_wk, swa_wv, swa_q_norm_w, swa_k_norm_w, swa_sinks, swa_wo, ffn_w_up, ffn_conv_w, ffn_w_down, loss_target, m_meta_tokens, m_attn_norm_w, m_ffn_norm_w, m_mix_w_in, m_conv_a_w, m_dn_conv_w, m_dn_a_log, m_dn_dt_bias, m_dn_norm_w, m_mix_w_out, m_swa_wq, m_swa_wk, m_swa_wv, m_swa_q_norm_w, m_swa_k_norm_w, m_swa_sinks, m_swa_wo, m_ffn_w_up, m_ffn_conv_w, m_ffn_w_down, v_meta_tokens, v_attn_norm_w, v_ffn_norm_w, v_mix_w_in, v_conv_a_w, v_dn_conv_w, v_dn_a_log, v_dn_dt_bias, v_dn_norm_w, v_mix_w_out, v_swa_wq, v_swa_wk, v_swa_wv, v_swa_q_norm_w, v_swa_k_norm_w, v_swa_sinks, v_swa_wo, v_ffn_w_up, v_ffn_conv_w, v_ffn_w_down):
    ix, iy, ic = lax.axis_index("x"), lax.axis_index("y"), lax.axis_index("c")
    chip = 2 * ix + iy
    seq = x.shape[1]
    rows = HEAD0 + seq

    big_names = (mix_w_in, mix_w_out, swa_wq, swa_wk, swa_wv, swa_wo, ffn_w_up, ffn_w_down)
    small_sharded = (conv_a_w, dn_conv_w, ffn_conv_w, meta_tokens)
    own_big = pack_big_shard(big_names, BF16)
    g_big, g_small = gather_weights(own_big, _flat_pad(small_sharded, SW_ROWS))
    g_big = lax.dynamic_update_slice_in_dim(g_big, own_big[None], chip, axis=0)
    w_in, w_out, wq, wk, wv, wo, w_up, w_down = unpack_big_full(g_big)
    w_in = jnp.pad(w_in, ((0, 0), (0, P_W - IN_DIM)))
    wqkv = jnp.concatenate([wq, wk, wv], axis=1)
    gs = g_small.reshape(4, -1)
    ca_full = gs[:, 0:384].reshape(4, 3, 128).transpose(1, 0, 2).reshape(3, D_CONV)
    dc_full = gs[:, 384:1920].reshape(4, 4, 384).transpose(1, 0, 2).reshape(4, 3 * DN_DIM)
    fc_full = gs[:, 1920:6144].reshape(4, 2, 3, 704).transpose(1, 2, 0, 3).reshape(2, 3, D_FF)
    meta_full = gs[:, 6144:10240].reshape(4, N_META, 256).transpose(1, 0, 2).reshape(N_META, D)
    ca8, dc8 = _rows8(ca_full), _rows8(dc_full)
    fc8 = [_rows8(fc_full[0]), _rows8(fc_full[1])]
    alog, dtb = _lanes(dn_a_log[0], 4), _lanes(dn_dt_bias[0], 4)
    dnw = dn_norm_w.astype(F32)
    qw, kw = swa_q_norm_w.astype(F32), swa_k_norm_w.astype(F32)
    sinks = swa_sinks[0].astype(F32)
    anw = [attn_norm_w[0:1], attn_norm_w[1:2]]
    fnw = [ffn_norm_w[0:1], ffn_norm_w[1:2]]

    (dh, loss_l, d_anw0, d_anw1, d_fnw0, d_fnw1, d_w_in, d_ca, d_dc, d_alog, d_dtb, d_dnw, d_w_out, d_wqkv, d_qw,
     d_kw, d_sinks, d_wo, d_up0, d_up1, d_fc0, d_fc1, d_down0, d_down1) = local_step(
        x[0], loss_target[0], meta_full, anw, fnw, w_in, ca8, dc8, alog, dtb, dnw, w_out, wqkv, qw, kw, sinks, wo,
        w_up, fc8, w_down)
    grad_x = dh[HEAD0:][None]

    small_parts = [jnp.concatenate([d_anw0, d_anw1], axis=0), jnp.concatenate([d_fnw0, d_fnw1], axis=0),
                   d_alog[0, 4:8], d_dtb[0, 4:8], d_dnw, d_qw, d_kw, d_sinks,
                   d_ca[:3], d_dc[:4], jnp.stack([d_fc0[:3], d_fc1[:3]]), dh[PAD:HEAD0], loss_l[0, 0:1]]
    small_shapes = [(2, D), (2, D), (1, 4), (1, 4), (1, DN_D), (1, SWA_D), (1, SWA_D), (1, SWA_H),
                    (1, 3, D_CONV), (1, 4, 3 * DN_DIM), (2, 3, D_FF), (N_META, D), ()]
    red = sum_slots(gather_small(_flat_pad(small_parts, SV_ROWS)))
    (g_anw, g_fnw, g_alog, g_dtb, g_dnw, g_qw, g_kw, g_sinks, g_ca_f, g_dc_f, g_fc_f, g_meta_f,
     loss) = _split_flat(red, small_shapes)
    g_ca = lax.dynamic_slice_in_dim(g_ca_f, chip * 128, 128, axis=2)
    g_dc = lax.dynamic_slice_in_dim(g_dc_f, chip * 384, 384, axis=2)
    g_fc = lax.dynamic_slice_in_dim(g_fc_f, chip * 704, 704, axis=2)
    g_meta = lax.dynamic_slice_in_dim(g_meta_f, chip * 256, 256, axis=1)

    d_up = jnp.stack([d_up0, d_up1])
    d_down = jnp.stack([d_down0, d_down1])
    g_full = pack_big_full(d_w_in[:, :IN_DIM], d_w_out, d_wqkv[:, :D], d_wqkv[:, D:D + 256], d_wqkv[:, D + 256:],
                           d_wo, d_up, d_down, BF16)
    c_idx = jnp.reshape(ic, (1,)).astype(jnp.int32)
    chip_idx = jnp.stack([chip, ic]).astype(jnp.int32)
    pair = pair_sum(g_full, swap_halves(g_full), c_idx)
    mine = chip_sum(pair, scatter_chips(pair), chip_idx)
    g_shard = join_halves(mine).reshape(R_BIG, 1024)
    big_shapes = [t.shape for t in big_names]
    g_w_in, g_w_out, g_wq, g_wk, g_wv, g_wo, g_up, g_down = unpack_big_shard(g_shard, big_shapes)

    grads = dict(meta_tokens=g_meta, attn_norm_w=g_anw, ffn_norm_w=g_fnw, mix_w_in=g_w_in, conv_a_w=g_ca,
                 dn_conv_w=g_dc, dn_a_log=g_alog, dn_dt_bias=g_dtb, dn_norm_w=g_dnw, mix_w_out=g_w_out,
                 swa_wq=g_wq, swa_wk=g_wk, swa_wv=g_wv, swa_q_norm_w=g_qw, swa_k_norm_w=g_kw, swa_sinks=g_sinks,
                 swa_wo=g_wo, ffn_w_up=g_up, ffn_conv_w=g_fc, ffn_w_down=g_down)
    weights = dict(meta_tokens=meta_tokens, attn_norm_w=attn_norm_w, ffn_norm_w=ffn_norm_w, mix_w_in=mix_w_in,
                   conv_a_w=conv_a_w, dn_conv_w=dn_conv_w, dn_a_log=dn_a_log, dn_dt_bias=dn_dt_bias,
                   dn_norm_w=dn_norm_w, mix_w_out=mix_w_out, swa_wq=swa_wq, swa_wk=swa_wk, swa_wv=swa_wv,
                   swa_q_norm_w=swa_q_norm_w, swa_k_norm_w=swa_k_norm_w, swa_sinks=swa_sinks, swa_wo=swa_wo,
                   ffn_w_up=ffn_w_up, ffn_conv_w=ffn_conv_w, ffn_w_down=ffn_w_down)
    m_in = dict(meta_tokens=m_meta_tokens, attn_norm_w=m_attn_norm_w, ffn_norm_w=m_ffn_norm_w, mix_w_in=m_mix_w_in,
                conv_a_w=m_conv_a_w, dn_conv_w=m_dn_conv_w, dn_a_log=m_dn_a_log, dn_dt_bias=m_dn_dt_bias,
                dn_norm_w=m_dn_norm_w, mix_w_out=m_mix_w_out, swa_wq=m_swa_wq, swa_wk=m_swa_wk, swa_wv=m_swa_wv,
                swa_q_norm_w=m_swa_q_norm_w, swa_k_norm_w=m_swa_k_norm_w, swa_sinks=m_swa_sinks, swa_wo=m_swa_wo,
                ffn_w_up=m_ffn_w_up, ffn_conv_w=m_ffn_conv_w, ffn_w_down=m_ffn_w_down)
    v_in = dict(meta_tokens=v_meta_tokens, attn_norm_w=v_attn_norm_w, ffn_norm_w=v_ffn_norm_w, mix_w_in=v_mix_w_in,
                conv_a_w=v_conv_a_w, dn_conv_w=v_dn_conv_w, dn_a_log=v_dn_a_log, dn_dt_bias=v_dn_dt_bias,
                dn_norm_w=v_dn_norm_w, mix_w_out=v_mix_w_out, swa_wq=v_swa_wq, swa_wk=v_swa_wk, swa_wv=v_swa_wv,
                swa_q_norm_w=v_swa_q_norm_w, swa_k_norm_w=v_swa_k_norm_w, swa_sinks=v_swa_sinks, swa_wo=v_swa_wo,
                ffn_w_up=v_ffn_w_up, ffn_conv_w=v_ffn_conv_w, ffn_w_down=v_ffn_w_down)
    names = list(weights)
    big = ("mix_w_in", "mix_w_out", "swa_wq", "swa_wk", "swa_wv", "swa_wo", "ffn_w_up", "ffn_w_down")
    small = [n for n in names if n not in big]
    grads = {n: grads[n].reshape(weights[n].shape) for n in names}
    delta, new_m, new_v = {}, {}, {}
    for n in big:
        delta[n], new_m[n], new_v[n] = adamw(weights[n], grads[n], m_in[n], v_in[n], name=f"adamw_{n}")
    shapes = [weights[n].shape for n in small]
    packed = [_flat_pad([t[n] for n in small], SW_ROWS) for t in (weights, grads, m_in, v_in)]
    for store, flat in zip((delta, new_m, new_v), adamw(*packed, name="adamw_small")):
        for n, t in zip(small, _split_flat(flat, shapes)):
            store[n] = t
    return (loss, grad_x, *[grads[n] for n in names], *[delta[n] for n in names],
            *[new_m[n] for n in names], *[new_v[n] for n in names])
```

```python
import functools

import jax
import jax.numpy as jnp
from jax import lax
from jax.experimental import pallas as pl
from jax.experimental.pallas import tpu as pltpu

F32 = jnp.float32
BF16 = jnp.bfloat16
HI = lax.Precision.HIGHEST
MESH = pl.DeviceIdType.MESH

D = 1024
N_META = 16
PAD = 112
HEAD0 = PAD + N_META
D_CONV = 512
DN_H = 4
DN_D = 128
DN_DIM = 512
CH = 64
IN_DIM = 3592
P_W = 3840
BG0 = 3584
SWA_H = 16
SWA_KV = 4
SWA_D = 64
BLK = 128
D_FF = 2816
EPS = 1e-6
LR, B1, B2, AEPS, WD, STEP = 0.001, 0.9, 0.999, 1e-08, 0.01, 10
VMEM_LIMIT = 48 * 1024 * 1024
R_BIG = 6144
R_HALF = R_BIG // 2
SV_ROWS = 48
SW_ROWS = 16


def _pick(n, cands):
    for c in cands:
        if n % c == 0:
            return c
    return n


def _params(sem=None):
    return pltpu.CompilerParams(dimension_semantics=sem, vmem_limit_bytes=VMEM_LIMIT)


def _dot(a, b, ca=1, cb=0, prec=None):
    return lax.dot_general(a, b, (((ca,), (cb,)), ((), ())), precision=prec,
                           preferred_element_type=F32)


def _sigmoid(x):
    return 1.0 / (1.0 + jnp.exp(-x))


def _silu(x):
    return x * _sigmoid(x)


def _dsilu(x):
    s = _sigmoid(x)
    return s * (1.0 + x * (1.0 - s))


def _softplus(x):
    return jnp.maximum(x, 0.0) + jnp.log(1.0 + jnp.exp(-jnp.abs(x)))


def mm(a, b, *, name, ta=False, tb=False, out_dtype=F32, add=None, tm=None, tn=None, tk=None,
       b_chip=False, out_chip=False):
    m, k = (a.shape[1], a.shape[0]) if ta else a.shape
    if b_chip:
        n = b.shape[1] if tb else 4 * b.shape[2]
        if tb:
            tk = b.shape[2]
        else:
            tn = b.shape[2]
    else:
        n = b.shape[0] if tb else b.shape[1]
    if out_chip:
        tn = n // 4
    tm = tm or (_pick(m, (1408, 512, 384, 256, 128)) if ta else _pick(m, (704, 512, 384, 256, 128)))
    tn = tn or _pick(n, (1408, 1024, 768, 512, 256, 128))
    tk = tk or (_pick(k, (704, 384, 128)) if ta else _pick(k, (1024, 1408, 768, 512, 128)))
    nk = k // tk
    dims = (((0 if ta else 1,), (1 if tb else 0,)), ((), ()))

    def body(*refs):
        if add is None:
            a_ref, b_ref, o_ref, acc_ref = refs
            add_ref = None
        else:
            a_ref, b_ref, add_ref, o_ref, acc_ref = refs
        part = lax.dot_general(a_ref[...].astype(BF16), b_ref[...].astype(BF16), dims,
                               preferred_element_type=F32)

        def finish(total):
            if add_ref is not None:
                total = total + add_ref[...]
            o_ref[...] = total.astype(out_dtype)

        if nk == 1:
            finish(part)
        else:
            kk = pl.program_id(2)

            @pl.when(kk == 0)
            def _():
                acc_ref[...] = part

            @pl.when(kk > 0)
            def _():
                acc_ref[...] += part

            @pl.when(kk == nk - 1)
            def _():
                finish(acc_ref[...])

    a_spec = pl.BlockSpec((tk, tm), lambda i, j, kk: (kk, i)) if ta else pl.BlockSpec((tm, tk), lambda i, j, kk: (i, kk))
    if b_chip and tb:
        b_spec = pl.BlockSpec((None, tn, tk), lambda i, j, kk: (kk, j, 0))
    elif b_chip:
        b_spec = pl.BlockSpec((None, tk, tn), lambda i, j, kk: (j, kk, 0))
    elif tb:
        b_spec = pl.BlockSpec((tn, tk), lambda i, j, kk: (j, kk))
    else:
        b_spec = pl.BlockSpec((tk, tn), lambda i, j, kk: (kk, j))
    o_spec = pl.BlockSpec((tm, tn), lambda i, j, kk: (i, j))
    in_specs = [a_spec, b_spec] + ([o_spec] if add is not None else [])
    args = [a, b] + ([add] if add is not None else [])
    out_spec = pl.BlockSpec((None, tm, tn), lambda i, j, kk: (j, i, 0)) if out_chip else o_spec
    return pl.pallas_call(
        body, name=name, interpret=False,
        out_shape=jax.ShapeDtypeStruct((4, m, tn) if out_chip else (m, n), out_dtype),
        grid=(m // tm, n // tn, nk), in_specs=in_specs, out_specs=out_spec,
        scratch_shapes=[pltpu.VMEM((tm, tn) if nk > 1 else (8, 128), F32)],
        compiler_params=_params(("parallel", "parallel", "arbitrary")),
    )(*args)


def cols(arr, tr, width=None, cb=0):
    width = width or arr.shape[1]
    return (arr, (tr, width), lambda i: (i, cb))


def heads(arr, tr):
    return (arr, (arr.shape[0], tr, arr.shape[2]), lambda i: (0, i, 0))


def whole(arr):
    nd = arr.ndim
    return (arr, arr.shape, lambda i: (0,) * nd)


def rowwise(fn, ins, outs, *, steps, name, accs=()):
    n_in, n_out, n_acc = len(ins), len(outs), len(accs)

    def body(*refs):
        i = pl.program_id(0)
        res = fn(i, *[r[...] for r in refs[:n_in]])
        if not isinstance(res, (tuple, list)):
            res = (res,)
        for r, v in zip(refs[n_in:n_in + n_out], res[:n_out]):
            r[...] = v.astype(r.dtype)
        if n_acc:
            acc_refs = refs[n_in + n_out:]

            @pl.when(i == 0)
            def _():
                for r in acc_refs:
                    r[...] = jnp.zeros(r.shape, r.dtype)

            for r, v in zip(acc_refs, res[n_out:]):
                r[...] += jnp.broadcast_to(v, r.shape).astype(r.dtype)

    def zmap(nd):
        return lambda i: (0,) * nd

    in_specs = [pl.BlockSpec(bs, im) for _, bs, im in ins]
    out_specs = [pl.BlockSpec(bs, im) for _, _, bs, im in outs]
    out_specs += [pl.BlockSpec(s, zmap(len(s))) for s, _ in accs]
    out_shape = [jax.ShapeDtypeStruct(s, d) for s, d, _, _ in outs]
    out_shape += [jax.ShapeDtypeStruct(s, d) for s, d in accs]
    res = pl.pallas_call(
        body, name=name, interpret=False, out_shape=out_shape, grid=(steps,),
        in_specs=in_specs, out_specs=out_specs,
        compiler_params=_params(("arbitrary",)),
    )(*[a for a, _, _ in ins])
    return res


def out2d(rows, width, dtype, tr):
    return ((rows, width), dtype, (tr, width), lambda i: (i, 0))


def conv_fwd(xs, w8, kw, *, rows, c, tc, tr, name, post, extras=(), outs=(), pre=None):
    nx, ne, no = len(xs), len(extras), len(outs)
    nr, nc = rows // tr, c // tc
    r8 = tr // 8

    def body(*refs):
        x_refs = refs[:2 * nx]
        w_ref = refs[2 * nx]
        e_refs = refs[2 * nx + 1:2 * nx + 1 + ne]
        o_refs = refs[2 * nx + 1 + ne:2 * nx + 1 + ne + no]
        scr = refs[-1]
        j, i = pl.program_id(0), pl.program_id(1)
        cur = [x_refs[2 * q][...].astype(F32) for q in range(nx)]
        halo = [x_refs[2 * q + 1][...].astype(F32) for q in range(nx)]
        x = pre(*cur) if pre else cur[0]
        h = pre(*halo) if pre else halo[0]
        scr[0:8, :] = jnp.where(i > 0, h, 0.0)
        scr[8:8 + tr, :] = x
        y = jnp.zeros((tr, tc), F32)
        for q in range(kw):
            s = kw - 1 - q
            y = y + w_ref[q:q + 1, :] * scr[8 - s:8 - s + tr, :]
        res = post(j, y, *[e[...] for e in e_refs])
        if not isinstance(res, (tuple, list)):
            res = (res,)
        for r, v in zip(o_refs, res):
            r[...] = v.astype(r.dtype)

    in_specs, args = [], []
    for arr, cb0 in xs:
        in_specs.append(pl.BlockSpec((tr, tc), lambda j, i, cb0=cb0: (i, cb0 + j)))
        in_specs.append(pl.BlockSpec((8, tc), lambda j, i, cb0=cb0: (jnp.maximum(i * r8 - 1, 0), cb0 + j)))
        args += [arr, arr]
    in_specs.append(pl.BlockSpec((8, tc), lambda j, i: (0, j)))
    args.append(w8)
    for arr, cb0 in extras:
        in_specs.append(pl.BlockSpec((tr, tc), lambda j, i, cb0=cb0: (i, cb0 + j)))
        args.append(arr)
    return pl.pallas_call(
        body, name=name, interpret=False,
        out_shape=[jax.ShapeDtypeStruct((rows, c), dt) for dt in outs],
        grid=(nc, nr), in_specs=in_specs,
        out_specs=[pl.BlockSpec((tr, tc), lambda j, i: (i, j)) for _ in outs],
        scratch_shapes=[pltpu.VMEM((tr + 8, tc), F32)],
        compiler_params=_params(("parallel", "arbitrary")),
    )(*args)


def conv_bwd(xs, w8, kw, dy, *, rows, c, tc, tr, name, post, extras=(), outs=(), pre=None):
    nx, ne, no = len(xs), len(extras), len(outs)
    nr, nc = rows // tr, c // tc
    r8 = tr // 8

    def body(*refs):
        x_refs = refs[:2 * nx]
        w_ref, dy_ref, dyn_ref = refs[2 * nx:2 * nx + 3]
        e_refs = refs[2 * nx + 3:2 * nx + 3 + ne]
        o_refs = refs[2 * nx + 3 + ne:2 * nx + 3 + ne + no]
        dw_ref = refs[2 * nx + 3 + ne + no]
        xscr, gscr = refs[-2], refs[-1]
        i = pl.program_id(1)
        cur = [x_refs[2 * q][...].astype(F32) for q in range(nx)]
        halo = [x_refs[2 * q + 1][...].astype(F32) for q in range(nx)]
        x = pre(*cur) if pre else cur[0]
        h = pre(*halo) if pre else halo[0]
        xscr[0:8, :] = jnp.where(i > 0, h, 0.0)
        xscr[8:8 + tr, :] = x
        g = dy_ref[...].astype(F32)
        gscr[0:tr, :] = g
        gscr[tr:tr + 8, :] = jnp.where(i < nr - 1, dyn_ref[...].astype(F32), 0.0)
        dx = jnp.zeros((tr, tc), F32)
        dws = []
        for q in range(kw):
            s = kw - 1 - q
            dx = dx + w_ref[q:q + 1, :] * gscr[s:s + tr, :]
            dws.append(jnp.sum(g * xscr[8 - s:8 - s + tr, :], axis=0, keepdims=True))
        dws.append(jnp.zeros((8 - kw, tc), F32))
        res = post(dx, *[e[...] for e in e_refs])
        if not isinstance(res, (tuple, list)):
            res = (res,)
        for r, v in zip(o_refs, res):
            r[...] = v.astype(r.dtype)

        @pl.when(i == 0)
        def _():
            dw_ref[...] = jnp.zeros((8, tc), F32)

        dw_ref[...] += jnp.concatenate(dws, axis=0)

    in_specs, args = [], []
    for arr, cb0 in xs:
        in_specs.append(pl.BlockSpec((tr, tc), lambda j, i, cb0=cb0: (i, cb0 + j)))
        in_specs.append(pl.BlockSpec((8, tc), lambda j, i, cb0=cb0: (jnp.maximum(i * r8 - 1, 0), cb0 + j)))
        args += [arr, arr]
    in_specs.append(pl.BlockSpec((8, tc), lambda j, i: (0, j)))
    in_specs.append(pl.BlockSpec((tr, tc), lambda j, i: (i, j)))
    in_specs.append(pl.BlockSpec((8, tc), lambda j, i: (jnp.minimum((i + 1) * r8, nr * r8 - 1), j)))
    args += [w8, dy, dy]
    for arr, cb0 in extras:
        in_specs.append(pl.BlockSpec((tr, tc), lambda j, i, cb0=cb0: (i, cb0 + j)))
        args.append(arr)
    return pl.pallas_call(
        body, name=name, interpret=False,
        out_shape=[jax.ShapeDtypeStruct((rows, c), dt) for dt in outs] + [jax.ShapeDtypeStruct((8, c), F32)],
        grid=(nc, nr), in_specs=in_specs,
        out_specs=[pl.BlockSpec((tr, tc), lambda j, i: (i, j)) for _ in outs] + [pl.BlockSpec((8, tc), lambda j, i: (0, j))],
        scratch_shapes=[pltpu.VMEM((tr + 8, tc), F32), pltpu.VMEM((tr + 8, tc), F32)],
        compiler_params=_params(("parallel", "arbitrary")),
    )(*args)


def rms_fwd(h, w, *, name):
    rows = h.shape[0]
    tr = _pick(rows, (384, 128))

    def fn(i, x, wv):
        r = lax.rsqrt(jnp.mean(x * x, axis=1, keepdims=True) + EPS)
        return x * r * wv

    return rowwise(fn, [cols(h, tr), whole(w)], [out2d(rows, D, BF16, tr)], steps=rows // tr, name=name)[0]


def rms_bwd(h, w, dy, dres, *, name):
    rows = h.shape[0]
    tr = _pick(rows, (384, 128))

    def fn(i, x, wv, g, dr):
        r = lax.rsqrt(jnp.mean(x * x, axis=1, keepdims=True) + EPS)
        xh = x * r
        gw = g * wv
        dx = r * (gw - xh * jnp.mean(gw * xh, axis=1, keepdims=True))
        row = i * tr + lax.broadcasted_iota(jnp.int32, (tr, 1), 0)
        return jnp.where(row >= PAD, dr + dx, 0.0), jnp.sum(g * xh, axis=0, keepdims=True)

    return rowwise(fn, [cols(h, tr), whole(w), cols(dy, tr), cols(dres, tr)], [out2d(rows, D, F32, tr)],
                   steps=rows // tr, name=name, accs=[((1, D), F32)])


def loss_grad(h, target):
    rows = h.shape[0]

    def fn(i, y, t):
        diff = jnp.where(i > 0, y - t, 0.0)
        part = jnp.sum(jnp.sum(diff * diff, axis=1, keepdims=True), axis=0, keepdims=True)
        return diff * (1.0 / D), part * (0.5 / D)

    tgt = (target, (BLK, D), lambda i: (jnp.maximum(i - 1, 0), 0))
    return rowwise(fn, [cols(h, BLK), tgt], [out2d(rows, D, F32, BLK)], steps=rows // BLK,
                   name="loss_grad", accs=[((1, 128), F32)])


def adamw(w, g, m, v, *, name):
    shape = w.shape
    gs = list(g) if isinstance(g, (list, tuple)) else [g]
    nl = len(gs)
    w2, m2, v2 = (t.reshape(-1, shape[-1]) for t in (w, m, v))
    rows, width = w2.shape
    rl = rows // nl
    tr = _pick(rl, (256, 176, 128, 64, 16, 8))
    nr = rl // tr

    def fn(i, wv, mv, vv, *gvs):
        gv = gvs[0]
        for layer in range(1, nl):
            gv = jnp.where(i >= layer * nr, gvs[layer], gv)
        mn = B1 * mv + (1.0 - B1) * gv
        vn = B2 * vv + (1.0 - B2) * gv * gv
        mh = mn / (1.0 - B1 ** STEP)
        vh = vn / (1.0 - B2 ** STEP)
        return -LR * (mh / (jnp.sqrt(vh) + AEPS) + WD * wv), mn, vn, gv

    g_ins = [(t.reshape(rl, width), (tr, width), lambda i, layer=layer: (jnp.clip(i - layer * nr, 0, nr - 1), 0))
             for layer, t in enumerate(gs)]
    res = rowwise(fn, [cols(t, tr) for t in (w2, m2, v2)] + g_ins, [out2d(rows, width, F32, tr)] * 4,
                  steps=rows // tr, name=name)
    return [r.reshape(shape) for r in res]


HB = DN_H * CH


def _split(a):
    hi = a.astype(BF16)
    return hi, (a - hi.astype(F32)).astype(BF16)


def _dot1(a, b, ca=1, cb=0):
    return _dot(a.astype(BF16), b.astype(BF16), ca, cb)


def _dot3(a, b, ca=1, cb=0):
    ah, al = _split(a)
    bh, bl = _split(b)
    return _dot(ah, bh, ca, cb) + (_dot(ah, bl, ca, cb) + _dot(al, bh, ca, cb))


def _dot01(m01, b, ca=1, cb=0):
    bh, bl = _split(b)
    m = m01.astype(BF16)
    return _dot(m, bh, ca, cb) + _dot(m, bl, ca, cb)


def _stack(x):
    return jnp.concatenate([x[:, h * DN_D:(h + 1) * DN_D] for h in range(DN_H)], axis=0)


def _unstack(x):
    return jnp.concatenate([x[h * CH:(h + 1) * CH] for h in range(DN_H)], axis=1)


def _tri_inv(a, blk, eye):
    ad = jnp.where(blk, a, 0.0)
    lo = a - ad
    a2 = _dot3(ad, ad)
    a4 = _dot3(a2, a2)
    a8 = _dot3(a4, a4)
    dgi = _dot3(_dot3(_dot3(eye - ad, eye + a2), eye + a4), eye + a8)
    n = _dot3(dgi, lo)
    return _dot3(_dot3(eye - n, eye + _dot3(n, n)), dgi)


def _dn_masks():
    row = lax.broadcasted_iota(jnp.int32, (HB, HB), 0)
    col = lax.broadcasted_iota(jnp.int32, (HB, HB), 1)
    same = (row // CH) == (col // CH)
    incl = jnp.logical_and(same, row >= col)
    strict = jnp.logical_and(same, row > col)
    upper = jnp.logical_and(same, row <= col)
    blk = (row // 16) == (col // 16)
    eye = (row == col).astype(F32)
    return incl, strict, upper, blk, eye


def _dn_chunk(q_ref, k_ref, v_ref, bc_ref, br_ref, incl, strict):
    r64 = lax.broadcasted_iota(jnp.int32, (CH, CH), 0)
    c64 = lax.broadcasted_iota(jnp.int32, (CH, CH), 1)
    bc = bc_ref[...]
    dcol = _dot01((r64 >= c64).astype(F32), bc)
    drow = _dot3(br_ref[0], (r64 <= c64).astype(F32))
    col = lambda m, l0: jnp.concatenate([m[:, l0 + h:l0 + h + 1] for h in range(DN_H)], axis=0)
    b_c = col(bc, 0)
    d_c = col(dcol, 4)
    d_r = jnp.concatenate([drow[4 + h:5 + h, :] for h in range(DN_H)], axis=1)
    d_last_h = [dcol[CH - 1:CH, 4 + h:5 + h] for h in range(DN_H)]
    d_last = jnp.concatenate([jnp.broadcast_to(t, (CH, 1)) for t in d_last_h], axis=0)
    q, k, v = _stack(q_ref[...]), _stack(k_ref[...]), _stack(v_ref[...])
    dm = jnp.where(incl, jnp.exp(jnp.where(incl, d_c - d_r, 0.0)), 0.0)
    kk = _dot1(k, k, 1, 1)
    a = jnp.where(strict, b_c * kk * dm, 0.0)
    ed = jnp.exp(d_c)
    rhs = jnp.concatenate([v * b_c, k * (b_c * ed)], axis=1)
    qk = _dot1(q, k, 1, 1) * dm
    ekd = jnp.exp(d_last - d_c)
    gl = [jnp.exp(t) for t in d_last_h]
    return q, k, v, b_c, dm, kk, a, ed, rhs, qk, ekd, gl


def dn_fwd(qkv_n, bgcol, bgrow):
    rows = qkv_n.shape[0]
    nch = rows // CH

    def body(q_ref, k_ref, v_ref, bc_ref, br_ref, o_ref, s_out, ti_out, s_scr):
        n = pl.program_id(0)

        @pl.when(n == 0)
        def _():
            s_scr[...] = jnp.zeros(s_scr.shape, F32)

        incl, strict, _, blk, eye = _dn_masks()
        q, k, v, b_c, dm, kk, a, ed, rhs, qk, ekd, gl = _dn_chunk(q_ref, k_ref, v_ref, bc_ref, br_ref, incl, strict)
        tinv = _tri_inv(a, blk, eye)
        ti_out[0] = tinv
        sol = _dot3(tinv, rhs)
        u, w = sol[:, :DN_D], sol[:, DN_D:]
        qd, kd = q * ed, k * ekd
        v_new, o_state = [], []
        for h in range(DN_H):
            rs = slice(h * CH, (h + 1) * CH)
            s = s_scr[h]
            s_out[0, h] = s
            vn = u[rs] - _dot1(w[rs], s)
            v_new.append(vn)
            o_state.append(_dot1(qd[rs], s))
            s_scr[h] = gl[h] * s + _dot1(kd[rs], vn, 0, 0)
        o = jnp.concatenate(o_state, axis=0) + _dot1(qk, jnp.concatenate(v_new, axis=0))
        o_ref[...] = _unstack(o)

    return pl.pallas_call(
        body, name="dn_fwd", interpret=False,
        out_shape=[jax.ShapeDtypeStruct((rows, DN_DIM), F32),
                   jax.ShapeDtypeStruct((nch, DN_H, DN_D, DN_D), F32),
                   jax.ShapeDtypeStruct((nch, HB, HB), F32)],
        grid=(nch,),
        in_specs=[pl.BlockSpec((CH, DN_DIM), lambda n: (n, 0)),
                  pl.BlockSpec((CH, DN_DIM), lambda n: (n, 1)),
                  pl.BlockSpec((CH, DN_DIM), lambda n: (n, 2)),
                  pl.BlockSpec((CH, 128), lambda n: (n, 0)),
                  pl.BlockSpec((1, 8, CH), lambda n: (n, 0, 0))],
        out_specs=[pl.BlockSpec((CH, DN_DIM), lambda n: (n, 0)),
                   pl.BlockSpec((1, DN_H, DN_D, DN_D), lambda n: (n, 0, 0, 0)),
                   pl.BlockSpec((1, HB, HB), lambda n: (n, 0, 0))],
        scratch_shapes=[pltpu.VMEM((DN_H, DN_D, DN_D), F32)],
        compiler_params=_params(("arbitrary",)),
    )(qkv_n, qkv_n, qkv_n, bgcol, bgrow)


def dn_bwd(qkv_n, bgcol, bgrow, s_all, ti_all, do):
    rows = qkv_n.shape[0]
    nch = rows // CH

    def body(q_ref, k_ref, v_ref, bc_ref, br_ref, s_ref, ti_ref, do_ref, dq_ref, dk_ref, dv_ref, dbg_ref, ds_scr):
        n = pl.program_id(0)

        @pl.when(n == 0)
        def _():
            ds_scr[...] = jnp.zeros(ds_scr.shape, F32)

        incl, strict, upper, _, _ = _dn_masks()
        q, k, v, b_c, dm, kk, a, ed, rhs, qk, ekd, gl = _dn_chunk(q_ref, k_ref, v_ref, bc_ref, br_ref, incl, strict)
        tinv = ti_ref[0]
        g_o = _stack(do_ref[...])
        sol = _dot3(tinv, rhs)
        u, w = sol[:, :DN_D], sol[:, DN_D:]
        qd, kd = q * ed, k * ekd
        rsum = lambda t: jnp.sum(t, axis=1, keepdims=True)
        rows_of = [slice(h * CH, (h + 1) * CH) for h in range(DN_H)]
        s_h = [s_ref[0, h] for h in range(DN_H)]
        ds_h = [ds_scr[h] for h in range(DN_H)]
        v_new = jnp.concatenate([u[rs] - _dot1(w[rs], s) for rs, s in zip(rows_of, s_h)], axis=0)
        dv_new = _dot1(qk, g_o, 0, 0) + jnp.concatenate([_dot1(kd[rs], t) for rs, t in zip(rows_of, ds_h)], axis=0)
        dqd = jnp.concatenate([_dot1(g_o[rs], s, 1, 1) for rs, s in zip(rows_of, s_h)], axis=0)
        dkd = jnp.concatenate([_dot1(v_new[rs], t, 1, 1) for rs, t in zip(rows_of, ds_h)], axis=0)
        for h, rs in enumerate(rows_of):
            ds_scr[h] = _dot1(qd[rs], g_o[rs], 0, 0) + gl[h] * ds_h[h] - _dot1(w[rs], dv_new[rs], 0, 0)
        dw = jnp.concatenate([-_dot1(dv_new[rs], s, 1, 1) for rs, s in zip(rows_of, s_h)], axis=0)
        dqk = _dot1(g_o, v_new, 1, 1)
        drhs = _dot3(tinv, jnp.concatenate([dv_new, dw], axis=1), 0, 0)
        da = jnp.where(strict, -_dot1(drhs, sol, 1, 1), 0.0)
        drhs_u, drhs_w = drhs[:, :DN_D], drhs[:, DN_D:]
        s2 = rsum(drhs_w * k)
        dbeta = rsum(drhs_u * v) + s2 * ed + rsum(da * kk * dm)
        dkk = da * b_c * dm
        dqkr = dqk * dm
        mmat = da * a + dqk * qk
        tmp = rsum(dkd * kd)
        dd = (s2 * b_c * ed + rsum(mmat) - _dot3(mmat, jnp.ones((HB, 128), F32), 0, 0)[:, :1] + rsum(dqd * qd) - tmp)
        rowi = lax.broadcasted_iota(jnp.int32, (CH, 1), 0)
        last = []
        for h, rs in enumerate(rows_of):
            dgl = jnp.sum(rsum(s_h[h] * ds_h[h]), axis=0, keepdims=True)
            dd_last = jnp.sum(tmp[rs], axis=0, keepdims=True) + dgl * gl[h]
            last.append(jnp.where(rowi == CH - 1, dd_last, 0.0))
        dd = dd + jnp.concatenate(last, axis=0)
        dq_ref[...] = _unstack(_dot1(dqkr, k) + dqd * ed)
        dk_ref[...] = _unstack(drhs_w * (b_c * ed) + _dot1(dkk, k) + _dot1(dkk, k, 0, 0) + _dot1(dqkr, q, 0, 0)
                               + dkd * ekd)
        dv_ref[...] = _unstack(drhs_u * b_c)
        dg = _dot01(upper.astype(F32), jnp.broadcast_to(dd, (HB, 128)))[:, :1]
        lane = lax.broadcasted_iota(jnp.int32, (CH, 128), 1)
        out = jnp.zeros((CH, 128), F32)
        for h, rs in enumerate(rows_of):
            out = out + jnp.where(lane == h, dbeta[rs], 0.0) + jnp.where(lane == 4 + h, dg[rs], 0.0)
        dbg_ref[...] = out

    rev = lambda n: nch - 1 - n
    return pl.pallas_call(
        body, name="dn_bwd", interpret=False,
        out_shape=[jax.ShapeDtypeStruct((rows, DN_DIM), F32)] * 3 + [jax.ShapeDtypeStruct((rows, 128), F32)],
        grid=(nch,),
        in_specs=[pl.BlockSpec((CH, DN_DIM), lambda n: (rev(n), 0)),
                  pl.BlockSpec((CH, DN_DIM), lambda n: (rev(n), 1)),
                  pl.BlockSpec((CH, DN_DIM), lambda n: (rev(n), 2)),
                  pl.BlockSpec((CH, 128), lambda n: (rev(n), 0)),
                  pl.BlockSpec((1, 8, CH), lambda n: (rev(n), 0, 0)),
                  pl.BlockSpec((1, DN_H, DN_D, DN_D), lambda n: (rev(n), 0, 0, 0)),
                  pl.BlockSpec((1, HB, HB), lambda n: (rev(n), 0, 0)),
                  pl.BlockSpec((CH, DN_DIM), lambda n: (rev(n), 0))],
        out_specs=[pl.BlockSpec((CH, DN_DIM), lambda n: (rev(n), 0))] * 3 + [pl.BlockSpec((CH, 128), lambda n: (rev(n), 0))],
        scratch_shapes=[pltpu.VMEM((DN_H, DN_D, DN_D), F32)],
        compiler_params=_params(("arbitrary",)),
    )(qkv_n, qkv_n, qkv_n, bgcol, bgrow, s_all, ti_all, do)


def _swa_valid(n):
    r = lax.broadcasted_iota(jnp.int32, (BLK, 3 * BLK), 0)
    c3 = lax.broadcasted_iota(jnp.int32, (BLK, 3 * BLK), 1)
    c = c3 % BLK
    lo = jnp.where(c3 < BLK, PAD, jnp.where(c3 < 2 * BLK, r + 1 + jnp.where(n >= 2, 0, BLK), 0))
    hi = jnp.where(c3 < BLK, r + jnp.where(n >= 1, BLK, 0), jnp.where(c3 < 2 * BLK, BLK, r - jnp.where(n >= 1, 0, BLK)))
    return jnp.logical_and(c >= lo, c <= hi)


def _swa_probs(q, kcat, valid, sink):
    s = jnp.where(valid, _dot(q, kcat, 1, 1), -1e30)
    m = jnp.maximum(jnp.max(s, axis=1, keepdims=True), sink)
    e = jnp.where(valid, jnp.exp(s - m), 0.0)
    es = jnp.exp(sink - m)
    inv = 1.0 / (jnp.sum(e, axis=1, keepdims=True) + es)
    return e * inv, es * inv


def _swa_group(q_ref, sk_ref, h):
    q4 = jnp.concatenate([q_ref[4 * h + g] for g in range(4)], axis=0)
    sink4 = jnp.concatenate([jnp.full((BLK, 1), sk_ref[4 * h + g], F32) for g in range(4)], axis=0)
    return q4, sink4


def _swa_specs():
    q = pl.BlockSpec((SWA_H, BLK, SWA_D), lambda n: (0, n, 0))
    km = pl.BlockSpec((SWA_KV, BLK, SWA_D), lambda n: (0, 0, 0))
    kp = pl.BlockSpec((SWA_KV, BLK, SWA_D), lambda n: (0, jnp.maximum(n - 1, 0), 0))
    kc = pl.BlockSpec((SWA_KV, BLK, SWA_D), lambda n: (0, n, 0))
    return [q, km, kp, kc, km, kp, kc]


def swa_fwd(qh, kh, vh, sinks):
    rows = qh.shape[1]
    nb = rows // BLK

    def body(q_ref, km, kp, kc, vm, vp, vc, sk_ref, o_ref):
        n = pl.program_id(0)
        valid = jnp.concatenate([_swa_valid(n)] * 4, axis=0)
        outs = []
        for h in range(SWA_KV):
            kcat = jnp.concatenate([km[h], kp[h], kc[h]], axis=0)
            vcat = jnp.concatenate([vm[h], vp[h], vc[h]], axis=0)
            q4, sink4 = _swa_group(q_ref, sk_ref, h)
            p, _ = _swa_probs(q4, kcat, valid, sink4)
            o4 = _dot(p.astype(BF16), vcat)
            outs += [o4[g * BLK:(g + 1) * BLK] for g in range(4)]
        o_ref[...] = jnp.concatenate(outs, axis=1).astype(BF16)

    return pl.pallas_call(
        body, name="swa_fwd", interpret=False,
        out_shape=jax.ShapeDtypeStruct((rows, SWA_H * SWA_D), BF16),
        grid=(nb,),
        in_specs=_swa_specs() + [pl.BlockSpec(memory_space=pltpu.SMEM)],
        out_specs=pl.BlockSpec((BLK, SWA_H * SWA_D), lambda n: (n, 0)),
        compiler_params=_params(("parallel",)),
    )(qh, kh, kh, kh, vh, vh, vh, sinks)


def swa_bwd(qh, kh, vh, sinks, do):
    rows = qh.shape[1]
    nb = rows // BLK

    def body(q_ref, km, kp, kc, vm, vp, vc, do_ref, sk_ref, dq_ref, dk_ref, dv_ref, dsk_ref):
        n = pl.program_id(0)

        @pl.when(n == 0)
        def _():
            dk_ref[...] = jnp.zeros(dk_ref.shape, F32)
            dv_ref[...] = jnp.zeros(dv_ref.shape, F32)

        valid = jnp.concatenate([_swa_valid(n)] * 4, axis=0)
        g_all = do_ref[...]
        rowi = lax.broadcasted_iota(jnp.int32, (SWA_H, 128), 0)
        dsk = jnp.zeros((SWA_H, 128), F32)
        pm = pl.multiple_of(jnp.maximum(n - 1, 0) * BLK, BLK)
        pc = pl.multiple_of(n * BLK, BLK)
        for h in range(SWA_KV):
            kcat = jnp.concatenate([km[h], kp[h], kc[h]], axis=0)
            vcat = jnp.concatenate([vm[h], vp[h], vc[h]], axis=0)
            q4, sink4 = _swa_group(q_ref, sk_ref, h)
            p, ps = _swa_probs(q4, kcat, valid, sink4)
            g4 = jnp.concatenate([g_all[:, (4 * h + g) * SWA_D:(4 * h + g + 1) * SWA_D] for g in range(4)], axis=0)
            dp = _dot(g4, vcat, 1, 1)
            delta = jnp.sum(p * dp, axis=1, keepdims=True)
            ds = (p * (dp - delta)).astype(BF16)
            dq4 = _dot(ds, kcat)
            dkc = _dot(ds, q4, 0, 0)
            dvc = _dot(p.astype(BF16), g4, 0, 0)
            t = ps * delta
            for g in range(4):
                dq_ref[4 * h + g] = dq4[g * BLK:(g + 1) * BLK]
                part = -jnp.sum(t[g * BLK:(g + 1) * BLK], axis=0, keepdims=True)
                dsk = dsk + jnp.where(rowi == 4 * h + g, part, 0.0)
            lanes = slice(h * SWA_D, (h + 1) * SWA_D)
            for ref, val in ((dk_ref, dkc), (dv_ref, dvc)):
                ref[0:BLK, lanes] += val[0:BLK]
                ref[pl.ds(pm, BLK), lanes] += val[BLK:2 * BLK]
                ref[pl.ds(pc, BLK), lanes] += val[2 * BLK:]
        dsk_ref[0] = dsk

    return pl.pallas_call(
        body, name="swa_bwd", interpret=False,
        out_shape=[jax.ShapeDtypeStruct((SWA_H, rows, SWA_D), F32),
                   jax.ShapeDtypeStruct((rows, SWA_KV * SWA_D), F32),
                   jax.ShapeDtypeStruct((rows, SWA_KV * SWA_D), F32),
                   jax.ShapeDtypeStruct((nb, SWA_H, 128), F32)],
        grid=(nb,),
        in_specs=_swa_specs() + [pl.BlockSpec((BLK, SWA_H * SWA_D), lambda n: (n, 0)),
                                 pl.BlockSpec(memory_space=pltpu.SMEM)],
        out_specs=[pl.BlockSpec((SWA_H, BLK, SWA_D), lambda n: (0, n, 0)),
                   pl.BlockSpec((rows, SWA_KV * SWA_D), lambda n: (0, 0)),
                   pl.BlockSpec((rows, SWA_KV * SWA_D), lambda n: (0, 0)),
                   pl.BlockSpec((1, SWA_H, 128), lambda n: (n, 0, 0))],
        compiler_params=_params(("arbitrary",)),
    )(qh, kh, kh, kh, vh, vh, vh, do, sinks)


def qknorm_fwd(qkv, qw, kw):
    rows = qkv.shape[0]
    tr = _pick(rows, (384, 128))
    scale = SWA_D ** -0.5

    def fn(i, x, qwv, kwv):
        def normed(j, wv, sc):
            xs = x[:, j * SWA_D:(j + 1) * SWA_D]
            r = lax.rsqrt(jnp.mean(xs * xs, axis=1, keepdims=True) + EPS)
            return (xs * r * wv * sc)[None]
        qo = jnp.concatenate([normed(j, qwv, scale) for j in range(SWA_H)], axis=0)
        ko = jnp.concatenate([normed(SWA_H + j, kwv, 1.0) for j in range(SWA_KV)], axis=0)
        vo = jnp.concatenate([x[:, (SWA_H + SWA_KV + j) * SWA_D:(SWA_H + SWA_KV + j + 1) * SWA_D][None]
                              for j in range(SWA_KV)], axis=0)
        return qo, ko, vo

    hm = lambda nh: ((nh, rows, SWA_D), BF16, (nh, tr, SWA_D), lambda i: (0, i, 0))
    return rowwise(fn, [cols(qkv, tr), whole(qw), whole(kw)], [hm(SWA_H), hm(SWA_KV), hm(SWA_KV)],
                   steps=rows // tr, name="qknorm_fwd")


def qknorm_bwd(qkv, qw, kw, dqh, dkh, dvh):
    rows = qkv.shape[0]
    tr = _pick(rows, (384, 128))
    scale = SWA_D ** -0.5

    def fn(i, x, qwv, kwv, dq, dk, dv):
        pieces = []
        dws = [jnp.zeros((1, SWA_D), F32), jnp.zeros((1, SWA_D), F32)]

        def one(j, dy, wv, sc, which):
            xs = x[:, j * SWA_D:(j + 1) * SWA_D]
            r = lax.rsqrt(jnp.mean(xs * xs, axis=1, keepdims=True) + EPS)
            xh = xs * r
            gw = dy * wv * sc
            pieces.append(r * (gw - xh * jnp.mean(gw * xh, axis=1, keepdims=True)))
            dws[which] = dws[which] + jnp.sum(dy * sc * xh, axis=0, keepdims=True)

        for j in range(SWA_H):
            one(j, dq[j], qwv, scale, 0)
        for j in range(SWA_KV):
            one(SWA_H + j, dk[:, j * SWA_D:(j + 1) * SWA_D], kwv, 1.0, 1)
        pieces.append(dv)
        return jnp.concatenate(pieces, axis=1), dws[0], dws[1]

    return rowwise(fn, [cols(qkv, tr), whole(qw), whole(kw), heads(dqh, tr), cols(dkh, tr), cols(dvh, tr)],
                   [out2d(rows, 1536, BF16, tr)], steps=rows // tr, name="qknorm_bwd",
                   accs=[((1, SWA_D), F32), ((1, SWA_D), F32)])


def _place():
    return lax.axis_index("x"), lax.axis_index("y"), lax.axis_index("c")


ANY = pl.BlockSpec(memory_space=pl.ANY)


def _rcopy(ssem, rsem, k, src, dst, to):
    return pltpu.make_async_remote_copy(src_ref=src, dst_ref=dst, send_sem=ssem.at[k], recv_sem=rsem.at[k],
                                        device_id=to, device_id_type=MESH)


def gather_weights(shards, small):
    n = len(shards)
    halves = [t.shape[0] // 2 for t in shards]

    def body(*refs):
        s_refs, small_ref = refs[:n], refs[n]
        o_refs, osmall = refs[n + 1:2 * n + 1], refs[2 * n + 1]
        ssem, rsem, lsem = refs[2 * n + 2:]
        x, y, c = _place()
        me = 2 * x + y
        chips = [(1 - x, y), (x, 1 - y), (1 - x, 1 - y)]

        def half(k, s, hh):
            return o_refs[k].at[s, pl.ds(hh * halves[k], halves[k]), :]

        loc = pltpu.make_async_copy(small_ref, osmall.at[me], lsem)
        loc.start()
        sends = []
        for k in range(n):
            for j, (px, py) in enumerate(chips):
                sends.append(_rcopy(ssem, rsem, 6 * k + j, s_refs[k].at[pl.ds(c * halves[k], halves[k]), :],
                                    half(k, me, c), (px, py, c)))
        for j, (px, py) in enumerate(chips):
            sends.append(_rcopy(ssem, rsem, 6 * n + j, small_ref, osmall.at[me], (px, py, c)))
        for cp in sends:
            cp.start()
        for k in range(n):
            for j, (px, py) in enumerate(chips):
                s = 2 * px + py
                _rcopy(ssem, rsem, 6 * k + j, half(k, s, c), half(k, s, c), (x, y, c)).wait_recv()
                fwd = _rcopy(ssem, rsem, 6 * k + 3 + j, half(k, s, c), half(k, s, c), (x, y, 1 - c))
                fwd.start()
                sends.append(fwd)
        for k in range(n):
            for j, (px, py) in enumerate(chips):
                s = 2 * px + py
                _rcopy(ssem, rsem, 6 * k + 3 + j, half(k, s, 1 - c), half(k, s, 1 - c), (x, y, c)).wait_recv()
        for j, (px, py) in enumerate(chips):
            s = 2 * px + py
            _rcopy(ssem, rsem, 6 * n + j, osmall.at[s], osmall.at[s], (x, y, c)).wait_recv()
        for cp in sends:
            cp.wait_send()
        loc.wait()

    res = pl.pallas_call(
        body, name="gather_weights", interpret=False,
        out_shape=[jax.ShapeDtypeStruct((4,) + t.shape, t.dtype) for t in shards]
        + [jax.ShapeDtypeStruct((4, SW_ROWS, 1024), F32)],
        in_specs=[ANY] * (n + 1), out_specs=[ANY] * (n + 1),
        scratch_shapes=[pltpu.SemaphoreType.DMA((6 * n + 3,)), pltpu.SemaphoreType.DMA((6 * n + 3,)),
                        pltpu.SemaphoreType.DMA],
    )(*shards, small)
    return res[:n], res[n]


def swap_halves(gs):
    n = len(gs)

    def body(*refs):
        g_refs, o_refs, ssem, rsem = refs[:n], refs[n:2 * n], refs[2 * n], refs[2 * n + 1]
        x, y, c = _place()
        cps = []
        for k in range(n):
            hk = g_refs[k].shape[1] // 2
            cps.append(_rcopy(ssem, rsem, k, g_refs[k].at[:, pl.ds((1 - c) * hk, hk), :], o_refs[k], (x, y, 1 - c)))
        for cp in cps:
            cp.start()
        for cp in cps:
            cp.wait()

    return pl.pallas_call(
        body, name="swap_halves", interpret=False,
        out_shape=[jax.ShapeDtypeStruct((4, t.shape[1] // 2, t.shape[2]), t.dtype) for t in gs],
        in_specs=[ANY] * n, out_specs=[ANY] * n,
        scratch_shapes=[pltpu.SemaphoreType.DMA((n,)), pltpu.SemaphoreType.DMA((n,))],
    )(*gs)


def _sum_rows(hk):
    return _pick(hk, (512, 352, 256, 128))


def pair_sum(g, other, c_idx, *, name):
    _, hk, width = other.shape
    tr = _sum_rows(hk)
    nbk = hk // tr

    def body(c_ref, g_ref, o_ref, out_ref):
        out_ref[...] = (g_ref[...].astype(F32) + o_ref[...].astype(F32)).astype(BF16)

    return pl.pallas_call(
        body, name=name, interpret=False,
        out_shape=jax.ShapeDtypeStruct((4, hk, width), BF16),
        grid_spec=pltpu.PrefetchScalarGridSpec(
            num_scalar_prefetch=1, grid=(4, nbk),
            in_specs=[pl.BlockSpec((1, tr, width), lambda s, i, c_ref: (s, c_ref[0] * nbk + i, 0)),
                      pl.BlockSpec((1, tr, width), lambda s, i, c_ref: (s, i, 0))],
            out_specs=pl.BlockSpec((1, tr, width), lambda s, i, c_ref: (s, i, 0))),
        compiler_params=_params(("parallel", "parallel")),
    )(c_idx, g, other)


def scatter_chips(ps):
    n = len(ps)

    def body(*refs):
        p_refs, o_refs, ssem, rsem = refs[:n], refs[n:2 * n], refs[2 * n], refs[2 * n + 1]
        x, y, c = _place()
        chips = [(1 - x, y), (x, 1 - y), (1 - x, 1 - y)]
        cps = [_rcopy(ssem, rsem, 3 * k + j, p_refs[k].at[2 * px + py], o_refs[k].at[j], (px, py, c))
               for k in range(n) for j, (px, py) in enumerate(chips)]
        for cp in cps:
            cp.start()
        for cp in cps:
            cp.wait()

    return pl.pallas_call(
        body, name="scatter_chips", interpret=False,
        out_shape=[jax.ShapeDtypeStruct((3,) + t.shape[1:], t.dtype) for t in ps],
        in_specs=[ANY] * n, out_specs=[ANY] * n,
        scratch_shapes=[pltpu.SemaphoreType.DMA((3 * n,)), pltpu.SemaphoreType.DMA((3 * n,))],
    )(*ps)


def chip_sum(p, got, idx, *, name):
    _, hk, width = got.shape
    tr = _sum_rows(hk)
    nbk = hk // tr

    def body(idx_ref, p_ref, g_ref, out_ref):
        acc = p_ref[0].astype(F32)
        for j in range(3):
            acc = acc + g_ref[j].astype(F32)
        out_ref[0] = acc

    return pl.pallas_call(
        body, name=name, interpret=False,
        out_shape=jax.ShapeDtypeStruct((2, hk, width), F32),
        grid_spec=pltpu.PrefetchScalarGridSpec(
            num_scalar_prefetch=1, grid=(nbk,),
            in_specs=[pl.BlockSpec((1, tr, width), lambda i, idx_ref: (idx_ref[0], i, 0)),
                      pl.BlockSpec((3, tr, width), lambda i, idx_ref: (0, i, 0))],
            out_specs=pl.BlockSpec((1, tr, width), lambda i, idx_ref: (idx_ref[1], i, 0))),
        compiler_params=_params(("parallel",)),
    )(idx, p, got)


def join_halves(qs):
    n = len(qs)

    def body(*refs):
        q_refs, o_refs, ssem, rsem = refs[:n], refs[n:2 * n], refs[2 * n], refs[2 * n + 1]
        x, y, c = _place()
        cps = [_rcopy(ssem, rsem, k, q_refs[k].at[c], o_refs[k].at[c], (x, y, 1 - c)) for k in range(n)]
        for cp in cps:
            cp.start()
        for k in range(n):
            _rcopy(ssem, rsem, k, q_refs[k].at[c], o_refs[k].at[1 - c], (x, y, 1 - c)).wait_recv()
        for cp in cps:
            cp.wait_send()

    return pl.pallas_call(
        body, name="join_halves", interpret=False,
        out_shape=[jax.ShapeDtypeStruct(t.shape, t.dtype) for t in qs],
        in_specs=[ANY] * n, out_specs=[ANY] * n, input_output_aliases={k: k for k in range(n)},
        scratch_shapes=[pltpu.SemaphoreType.DMA((n,)), pltpu.SemaphoreType.DMA((n,))],
    )(*qs)


def reduce_scatter(gs, c_idx, idx, names):
    others = swap_halves(gs)
    pairs = [pair_sum(g, o, c_idx, name=f"pair_sum_{nm}") for g, o, nm in zip(gs, others, names)]
    gots = scatter_chips(pairs)
    mine = [chip_sum(p, g, idx, name=f"chip_sum_{nm}") for p, g, nm in zip(pairs, gots, names)]
    return [q.reshape(2 * q.shape[1], q.shape[2]) for q in join_halves(mine)]


def gather_small(v):
    def body(v_ref, o_ref, ssem, rsem, lsem):
        x, y, c = _place()
        loc = pltpu.make_async_copy(v_ref, o_ref.at[4 * x + 2 * y + c], lsem)
        loc.start()
        cps = []
        for k in range(1, 8):
            fx, fy, fc = (k >> 2) & 1, (k >> 1) & 1, k & 1
            px = 1 - x if fx else x
            py = 1 - y if fy else y
            pc = 1 - c if fc else c
            cps.append((pltpu.make_async_remote_copy(
                src_ref=v_ref, dst_ref=o_ref.at[4 * x + 2 * y + c], send_sem=ssem.at[k - 1], recv_sem=rsem.at[k - 1],
                device_id=(px, py, pc), device_id_type=MESH), 4 * px + 2 * py + pc))
        for cp, _ in cps:
            cp.start()
        for k, (cp, peer) in enumerate(cps):
            pltpu.make_async_remote_copy(
                src_ref=v_ref, dst_ref=o_ref.at[peer], send_sem=ssem.at[k], recv_sem=rsem.at[k],
                device_id=(x, y, c), device_id_type=MESH).wait_recv()
        for cp, _ in cps:
            cp.wait_send()
        loc.wait()

    return pl.pallas_call(
        body, name="gather_small", interpret=False,
        out_shape=jax.ShapeDtypeStruct((8, SV_ROWS, 1024), F32),
        in_specs=[ANY], out_specs=ANY,
        scratch_shapes=[pltpu.SemaphoreType.DMA((7,)), pltpu.SemaphoreType.DMA((7,)), pltpu.SemaphoreType.DMA],
    )(v)


def sum_slots(a):
    def fn(i, t):
        acc = t[0]
        for k in range(1, 8):
            acc = acc + t[k]
        return acc

    return rowwise(fn, [whole(a)], [((SV_ROWS, 1024), F32, (SV_ROWS, 1024), lambda i: (0, 0))], steps=1,
                   name="sum_slots")[0]


def _head_rms(x, nw):
    xs, rs = [], []
    for h in range(DN_H):
        xh = x[:, h * DN_D:(h + 1) * DN_D]
        r = lax.rsqrt(jnp.mean(xh * xh, axis=1, keepdims=True) + EPS)
        xs.append(xh * r)
        rs.append(r)
    return xs, rs


def bg_fwd(p, alog, dtb):
    rows = p.shape[0]
    tr = _pick(rows, (384, 128))

    def fn(i, x, al, dt):
        lane = lax.broadcasted_iota(jnp.int32, (tr, 128), 1)
        row = i * tr + lax.broadcasted_iota(jnp.int32, (tr, 128), 0)
        g = -jnp.exp(al) * _softplus(x + dt)
        out = jnp.where(lane < 4, _sigmoid(x), jnp.where(lane < 8, g, 0.0))
        return jnp.where(row >= PAD, out, 0.0)

    return rowwise(fn, [cols(p, tr, 128, BG0 // 128), whole(alog), whole(dtb)], [out2d(rows, 128, F32, tr)],
                   steps=rows // tr, name="bg_fwd")[0]


def bg_bwd(p, alog, dtb, dbg):
    rows = p.shape[0]
    tr = _pick(rows, (384, 128))

    def fn(i, x, al, dt, g_in):
        lane = lax.broadcasted_iota(jnp.int32, (tr, 128), 1)
        row = i * tr + lax.broadcasted_iota(jnp.int32, (tr, 128), 0)
        live = row >= PAD
        is_b = jnp.logical_and(live, lane < 4)
        is_g = jnp.logical_and(live, jnp.logical_and(lane >= 4, lane < 8))
        beta = _sigmoid(x)
        ea = jnp.exp(al)
        g = -ea * _softplus(x + dt)
        dalpha = jnp.where(is_g, g_in * (-ea) * _sigmoid(x + dt), 0.0)
        dx = jnp.where(is_b, g_in * beta * (1.0 - beta), dalpha)
        dal = jnp.sum(jnp.where(is_g, g_in * g, 0.0), axis=0, keepdims=True)
        return jnp.concatenate([dx, jnp.zeros((tr, 128), F32)], axis=1), dal, jnp.sum(dalpha, axis=0, keepdims=True)

    return rowwise(fn, [cols(p, tr, 128, BG0 // 128), whole(alog), whole(dtb), cols(dbg, tr)],
                   [out2d(rows, 256, BF16, tr)], steps=rows // tr, name="bg_bwd",
                   accs=[((1, 128), F32), ((1, 128), F32)])


def dn_qkv_post(j, y):
    xs = _silu(y)
    sc = jnp.where(j == 0, DN_D ** -0.5, 1.0)
    outs = []
    for h in range(DN_H):
        xh = xs[:, h * DN_D:(h + 1) * DN_D]
        r = lax.rsqrt(jnp.sum(xh * xh, axis=1, keepdims=True) + EPS)
        outs.append(jnp.where(j < 2, xh * r * sc, xh))
    return jnp.concatenate(outs, axis=1), y


def dn_qkv_bwd(cq, dq, dk, dv):
    rows = cq.shape[0]
    tr = _pick(rows, (384, 128))

    def fn(i, c0, c1, c2, g0, g1, g2):
        pieces = []
        for kind, (cv, g) in enumerate(((c0, g0), (c1, g1), (c2, g2))):
            xs = _silu(cv)
            if kind < 2:
                sc = DN_D ** -0.5 if kind == 0 else 1.0
                ds = []
                for h in range(DN_H):
                    sl = slice(h * DN_D, (h + 1) * DN_D)
                    xh, gh = xs[:, sl], g[:, sl]
                    r = lax.rsqrt(jnp.sum(xh * xh, axis=1, keepdims=True) + EPS)
                    xn = xh * r
                    ds.append(sc * r * (gh - xn * jnp.sum(gh * xn, axis=1, keepdims=True)))
                dxs = jnp.concatenate(ds, axis=1)
            else:
                dxs = g
            pieces.append(dxs * _dsilu(cv))
        return jnp.concatenate(pieces, axis=1)

    ins = [cols(cq, tr, DN_DIM, k) for k in range(3)] + [cols(t, tr) for t in (dq, dk, dv)]
    return rowwise(fn, ins, [out2d(rows, 3 * DN_DIM, F32, tr)], steps=rows // tr, name="dn_qkv_bwd")[0]


def dn_out_fwd(o, p, nw):
    rows = o.shape[0]
    tr = _pick(rows, (384, 128))

    def fn(i, ov, z, w):
        xs, _ = _head_rms(ov, w)
        return jnp.concatenate(xs, axis=1) * jnp.concatenate([w] * DN_H, axis=1) * _silu(z)

    return rowwise(fn, [cols(o, tr), cols(p, tr, DN_DIM, 6), whole(nw)], [out2d(rows, DN_DIM, BF16, tr)],
                   steps=rows // tr, name="dn_out_fwd")[0]


def dn_out_bwd(o, p, nw, dymix):
    rows = o.shape[0]
    tr = _pick(rows, (384, 128))

    def fn(i, ov, z, w, dy):
        xs, rs = _head_rms(ov, w)
        sz = _silu(z)
        dn = dy * sz
        dos, dw = [], jnp.zeros((1, DN_D), F32)
        for h in range(DN_H):
            sl = slice(h * DN_D, (h + 1) * DN_D)
            gw = dn[:, sl] * w
            dos.append(rs[h] * (gw - xs[h] * jnp.mean(gw * xs[h], axis=1, keepdims=True)))
            dw = dw + jnp.sum(dn[:, sl] * xs[h], axis=0, keepdims=True)
        n = jnp.concatenate(xs, axis=1) * jnp.concatenate([w] * DN_H, axis=1)
        return jnp.concatenate(dos, axis=1), dy * n * _dsilu(z), dw

    return rowwise(fn, [cols(o, tr), cols(p, tr, DN_DIM, 6), whole(nw), cols(dymix, tr, DN_DIM, 1)],
                   [out2d(rows, DN_DIM, F32, tr), out2d(rows, DN_DIM, BF16, tr)], steps=rows // tr,
                   name="dn_out_bwd", accs=[((1, DN_D), F32)])


def conv_a_pre_bwd(dymix, cv, p):
    rows = cv.shape[0]
    tr = _pick(rows, (384, 128))

    def fn(i, dy, c, go):
        return dy * c, dy * go

    return rowwise(fn, [cols(dymix, tr, D_CONV, 0), cols(cv, tr), cols(p, tr, D_CONV, 1)],
                   [out2d(rows, D_CONV, BF16, tr), out2d(rows, D_CONV, F32, tr)], steps=rows // tr,
                   name="conv_a_pre_bwd")


def ffn_act_bwd(da, gc, u):
    rows = da.shape[0]
    tr = _pick(rows, (384, 128))

    def fn(i, g, c, val):
        g, c, val = g.astype(F32), c.astype(F32), val.astype(F32)
        return g * _silu(c), g * val * _dsilu(c)

    return rowwise(fn, [cols(da, tr), cols(gc, tr), cols(u, tr, D_FF, 1)],
                   [out2d(rows, D_FF, BF16, tr), out2d(rows, D_FF, F32, tr)], steps=rows // tr, name="ffn_act_bwd")


def _rows8(w):
    return jnp.pad(w.astype(F32), ((0, 8 - w.shape[0]), (0, 0)))


def _lanes(v, at):
    return jnp.pad(v.astype(F32), (at, 128 - at - v.shape[0]))[None]


def ffn_fwd(h, nw, w_up, cw8, w_down, tag):
    rows = h.shape[0]
    tr = _pick(rows, (384, 128))
    hn = rms_fwd(h, nw, name=f"ffn{tag}_norm")
    u = mm(hn, w_up, out_dtype=BF16, b_chip=True, name=f"ffn{tag}_up")
    a, gc = conv_fwd([(u, 0)], cw8, 3, rows=rows, c=D_FF, tc=1408, tr=tr, name=f"ffn{tag}_conv",
                     post=lambda j, y, val: (_silu(y) * val.astype(F32), y), extras=[(u, 2)], outs=[BF16, BF16])
    out = mm(a, w_down, add=h, name=f"ffn{tag}_down")
    return out, (hn, u, a, gc)


def ffn_bwd(h, nw, w_up, cw8, w_down, saved, dh, tag):
    hn, u, a, gc = saved
    rows = h.shape[0]
    tr = _pick(rows, (384, 128))
    da = mm(dh, w_down, tb=True, out_dtype=BF16, name=f"ffn{tag}_down_dx")
    d_w_down = mm(a, dh, ta=True, out_dtype=BF16, name=f"ffn{tag}_down_dw")
    dval, dgc = ffn_act_bwd(da, gc, u)
    dgate, d_cw = conv_bwd([(u, 0)], cw8, 3, dgc, rows=rows, c=D_FF, tc=1408, tr=tr, name=f"ffn{tag}_conv_bwd",
                           post=lambda dx: dx, outs=[BF16])
    du = jnp.concatenate([dgate, dval], axis=1)
    dhn = mm(du, w_up, tb=True, b_chip=True, name=f"ffn{tag}_up_dx")
    d_w_up = mm(hn, du, ta=True, out_dtype=BF16, out_chip=True, name=f"ffn{tag}_up_dw")
    dh_new, d_nw = rms_bwd(h, nw, dhn, dh, name=f"ffn{tag}_norm_bwd")
    return dh_new, d_nw, d_w_up, d_cw, d_w_down


def mixer_fwd(h, nw, w_in, ca8, dc8, alog, dtb, dnw, w_out):
    rows = h.shape[0]
    tr = _pick(rows, (384, 128))
    hn = rms_fwd(h, nw, name="mix_norm")
    p = mm(hn, w_in, name="mix_in")
    y_a, cv = conv_fwd([(p, 0), (p, 2)], ca8, 3, rows=rows, c=D_CONV, tc=D_CONV, tr=tr, name="conv_a",
                       pre=lambda gi, ah: gi * ah, post=lambda j, y, go: (go * y, y), extras=[(p, 1)],
                       outs=[BF16, F32])
    qkv_n, cq = conv_fwd([(p, 3)], dc8, 4, rows=rows, c=3 * DN_DIM, tc=DN_DIM, tr=tr, name="dn_conv",
                         post=dn_qkv_post, outs=[F32, F32])
    bgcol = bg_fwd(p, alog, dtb)
    bgrow = bgcol[:, :8].reshape(rows // CH, CH, 8).transpose(0, 2, 1)
    o, s_all, ti_all = dn_fwd(qkv_n, bgcol, bgrow)
    y_b = dn_out_fwd(o, p, dnw)
    ymix = jnp.concatenate([y_a, y_b], axis=1)
    out = mm(ymix, w_out, add=h, name="mix_out")
    return out, (hn, p, cv, qkv_n, cq, bgcol, bgrow, o, s_all, ti_all, ymix)


def mixer_bwd(h, nw, w_in, ca8, dc8, alog, dtb, dnw, w_out, saved, dh):
    hn, p, cv, qkv_n, cq, bgcol, bgrow, o, s_all, ti_all, ymix = saved
    rows = h.shape[0]
    tr = _pick(rows, (384, 128))
    dymix = mm(dh, w_out, tb=True, name="mix_out_dx")
    d_w_out = mm(ymix, dh, ta=True, out_dtype=BF16, name="mix_out_dw")
    do, dz, d_dnw = dn_out_bwd(o, p, dnw, dymix)
    dq, dk, dv, dbg = dn_bwd(qkv_n, bgcol, bgrow, s_all, ti_all, do)
    dbg_p, d_alog, d_dtb = bg_bwd(p, alog, dtb, dbg)
    dcq = dn_qkv_bwd(cq, dq, dk, dv)
    dqkv, d_dc = conv_bwd([(p, 3)], dc8, 4, dcq, rows=rows, c=3 * DN_DIM, tc=DN_DIM, tr=tr, name="dn_conv_bwd",
                          post=lambda dx: dx, outs=[BF16])
    dgo, dcv = conv_a_pre_bwd(dymix, cv, p)
    dgi, dah, d_ca = conv_bwd([(p, 0), (p, 2)], ca8, 3, dcv, rows=rows, c=D_CONV, tc=D_CONV, tr=tr,
                              name="conv_a_bwd", pre=lambda gi, ah: gi * ah,
                              post=lambda dm, gi, ah: (dm * ah, dm * gi), extras=[(p, 0), (p, 2)], outs=[BF16, BF16])
    dp = jnp.concatenate([dgi, dgo, dah, dqkv, dz, dbg_p], axis=1)
    dhn = mm(dp, w_in, tb=True, name="mix_in_dx")
    d_w_in = mm(hn, dp, ta=True, out_dtype=BF16, name="mix_in_dw")
    dh_new, d_nw = rms_bwd(h, nw, dhn, dh, name="mix_norm_bwd")
    return dh_new, d_nw, d_w_in, d_ca, d_dc, d_alog, d_dtb, d_dnw, d_w_out


def swa_layer_fwd(h, nw, wqkv, qw, kw, sinks, wo):
    hn = rms_fwd(h, nw, name="swa_norm")
    qkv = mm(hn, wqkv, name="swa_qkv")
    qh, kh, vh = qknorm_fwd(qkv, qw, kw)
    att = swa_fwd(qh, kh, vh, sinks)
    out = mm(att, wo, add=h, name="swa_out")
    return out, (hn, qkv, qh, kh, vh, att)


def swa_layer_bwd(h, nw, wqkv, qw, kw, sinks, wo, saved, dh):
    hn, qkv, qh, kh, vh, att = saved
    datt = mm(dh, wo, tb=True, out_dtype=BF16, name="swa_out_dx")
    d_wo = mm(att, dh, ta=True, out_dtype=BF16, name="swa_out_dw")
    dqh, dkh, dvh, dsk = swa_bwd(qh, kh, vh, sinks, datt)
    dqkv, d_qw, d_kw = qknorm_bwd(qkv, qw, kw, dqh, dkh, dvh)
    dhn = mm(dqkv, wqkv, tb=True, name="swa_qkv_dx")
    d_wqkv = mm(hn, dqkv, ta=True, out_dtype=BF16, name="swa_qkv_dw")
    dh_new, d_nw = rms_bwd(h, nw, dhn, dh, name="swa_norm_bwd")
    d_sinks = jnp.sum(dsk[:, :, 0], axis=0)
    return dh_new, d_nw, d_wqkv, d_qw, d_kw, d_sinks, d_wo


BIG = ("mix_w_in", "mix_w_out", "swa_wq", "swa_wk", "swa_wv", "swa_wo", "ffn_w_up", "ffn_w_down")


def _flat_pad(parts, rows):
    v = jnp.concatenate([t.astype(F32).reshape(-1) for t in parts])
    return jnp.pad(v, (0, rows * 1024 - v.shape[0])).reshape(rows, 1024)


def _split_flat(flat, shapes):
    v = flat.reshape(-1)
    out, o = [], 0
    for s in shapes:
        n = 1
        for d_ in s:
            n *= d_
        out.append(v[o:o + n].reshape(s))
        o += n
    return out


def local_step(x0, target0, meta_full, anw, fnw, w_in, ca8, dc8, alog, dtb, dnw, w_out, wqkv, qw, kw, sinks, wo,
               w_up, fc8, w_down):
    h0 = jnp.concatenate([jnp.zeros((PAD, D), F32), meta_full, x0], axis=0)
    h1, s_mix = mixer_fwd(h0, anw[0], w_in, ca8, dc8, alog, dtb, dnw, w_out)
    h2, s_f0 = ffn_fwd(h1, fnw[0], w_up[0], fc8[0], w_down[0], 0)
    h3, s_swa = swa_layer_fwd(h2, anw[1], wqkv, qw, kw, sinks, wo)
    h4, s_f1 = ffn_fwd(h3, fnw[1], w_up[1], fc8[1], w_down[1], 1)
    dh, loss_l = loss_grad(h4, target0)
    dh, d_fnw1, d_up1, d_fc1, d_down1 = ffn_bwd(h3, fnw[1], w_up[1], fc8[1], w_down[1], s_f1, dh, 1)
    dh, d_anw1, d_wqkv, d_qw, d_kw, d_sinks, d_wo = swa_layer_bwd(h2, anw[1], wqkv, qw, kw, sinks, wo, s_swa, dh)
    dh, d_fnw0, d_up0, d_fc0, d_down0 = ffn_bwd(h1, fnw[0], w_up[0], fc8[0], w_down[0], s_f0, dh, 0)
    dh, d_anw0, d_w_in, d_ca, d_dc, d_alog, d_dtb, d_dnw, d_w_out = mixer_bwd(
        h0, anw[0], w_in, ca8, dc8, alog, dtb, dnw, w_out, s_mix, dh)
    return (dh, loss_l, d_anw0, d_anw1, d_fnw0, d_fnw1, d_w_in, d_ca, d_dc, d_alog, d_dtb, d_dnw, d_w_out, d_wqkv,
            d_qw, d_kw, d_sinks, d_wo, d_up0, d_up1, d_fc0, d_fc1, d_down0, d_down1)


def kernel(x, meta_tokens, attn_norm_w, ffn_norm_w, mix_w_in, conv_a_w, dn_conv_w, dn_a_log, dn_dt_bias, dn_norm_w, mix_w_out, swa_wq, swa_wk, swa_wv, swa_q_norm_w, swa_k_norm_w, swa_sinks, swa_wo, ffn_w_up, ffn_conv_w, ffn_w_down, loss_target, m_meta_tokens, m_attn_norm_w, m_ffn_norm_w, m_mix_w_in, m_conv_a_w, m_dn_conv_w, m_dn_a_log, m_dn_dt_bias, m_dn_norm_w, m_mix_w_out, m_swa_wq, m_swa_wk, m_swa_wv, m_swa_q_norm_w, m_swa_k_norm_w, m_swa_sinks, m_swa_wo, m_ffn_w_up, m_ffn_conv_w, m_ffn_w_down, v_meta_tokens, v_attn_norm_w, v_ffn_norm_w, v_mix_w_in, v_conv_a_w, v_dn_conv_w, v_dn_a_log, v_dn_dt_bias, v_dn_norm_w, v_mix_w_out, v_swa_wq, v_swa_wk, v_swa_wv, v_swa_q_norm_w, v_swa_k_norm_w, v_swa_sinks, v_swa_wo, v_ffn_w_up, v_ffn_conv_w, v_ffn_w_down):
    ix, iy, ic = lax.axis_index("x"), lax.axis_index("y"), lax.axis_index("c")
    chip = 2 * ix + iy
    seq = x.shape[1]
    rows = HEAD0 + seq

    small_sharded = (conv_a_w, dn_conv_w, ffn_conv_w, meta_tokens)
    up_b, down_b = ffn_w_up.astype(BF16), ffn_w_down.astype(BF16)
    own = [mix_w_in[0].astype(BF16), mix_w_out[0].astype(BF16), swa_wq[0].astype(BF16), swa_wk[0].astype(BF16),
           swa_wv[0].astype(BF16), swa_wo[0].astype(BF16), up_b[0], up_b[1], down_b[0], down_b[1]]
    gathered, g_small = gather_weights(own, _flat_pad(small_sharded, SW_ROWS))
    g_in, g_out, g_q, g_k, g_v, g_o, g_up0, g_up1, g_dn0, g_dn1 = [
        lax.dynamic_update_slice_in_dim(g, t[None], chip, axis=0) for g, t in zip(gathered, own)]
    w_in = jnp.pad(g_in.transpose(1, 0, 2).reshape(D, IN_DIM), ((0, 0), (0, P_W - IN_DIM)))
    w_out, wo = g_out.reshape(D, D), g_o.reshape(D, D)
    wqkv = jnp.concatenate([g_q.reshape(D, D), g_k.reshape(D, 256), g_v.reshape(D, 256)], axis=1)
    w_up = [g_up0, g_up1]
    w_down = [g_dn0.reshape(D_FF, D), g_dn1.reshape(D_FF, D)]
    gs = g_small.reshape(4, -1)
    ca_full = gs[:, 0:384].reshape(4, 3, 128).transpose(1, 0, 2).reshape(3, D_CONV)
    dc_full = gs[:, 384:1920].reshape(4, 4, 384).transpose(1, 0, 2).reshape(4, 3 * DN_DIM)
    fc_full = gs[:, 1920:6144].reshape(4, 2, 3, 704).transpose(1, 2, 0, 3).reshape(2, 3, D_FF)
    meta_full = gs[:, 6144:10240].reshape(4, N_META, 256).transpose(1, 0, 2).reshape(N_META, D)
    ca8, dc8 = _rows8(ca_full), _rows8(dc_full)
    fc8 = [_rows8(fc_full[0]), _rows8(fc_full[1])]
    alog, dtb = _lanes(dn_a_log[0], 4), _lanes(dn_dt_bias[0], 4)
    dnw = dn_norm_w.astype(F32)
    qw, kw = swa_q_norm_w.astype(F32), swa_k_norm_w.astype(F32)
    sinks = swa_sinks[0].astype(F32)
    anw = [attn_norm_w[0:1], attn_norm_w[1:2]]
    fnw = [ffn_norm_w[0:1], ffn_norm_w[1:2]]

    (dh, loss_l, d_anw0, d_anw1, d_fnw0, d_fnw1, d_w_in, d_ca, d_dc, d_alog, d_dtb, d_dnw, d_w_out, d_wqkv, d_qw,
     d_kw, d_sinks, d_wo, d_up0, d_up1, d_fc0, d_fc1, d_down0, d_down1) = local_step(
        x[0], loss_target[0], meta_full, anw, fnw, w_in, ca8, dc8, alog, dtb, dnw, w_out, wqkv, qw, kw, sinks, wo,
        w_up, fc8, w_down)
    grad_x = dh[HEAD0:][None]

    small_parts = [jnp.concatenate([d_anw0, d_anw1], axis=0), jnp.concatenate([d_fnw0, d_fnw1], axis=0),
                   d_alog[0, 4:8], d_dtb[0, 4:8], d_dnw, d_qw, d_kw, d_sinks,
                   d_ca[:3], d_dc[:4], jnp.stack([d_fc0[:3], d_fc1[:3]]), dh[PAD:HEAD0], loss_l[0, 0:1]]
    small_shapes = [(2, D), (2, D), (1, 4), (1, 4), (1, DN_D), (1, SWA_D), (1, SWA_D), (1, SWA_H),
                    (1, 3, D_CONV), (1, 4, 3 * DN_DIM), (2, 3, D_FF), (N_META, D), ()]
    red = sum_slots(gather_small(_flat_pad(small_parts, SV_ROWS)))
    (g_anw, g_fnw, g_alog, g_dtb, g_dnw, g_qw, g_kw, g_sinks, g_ca_f, g_dc_f, g_fc_f, g_meta_f,
     loss) = _split_flat(red, small_shapes)
    g_ca = lax.dynamic_slice_in_dim(g_ca_f, chip * 128, 128, axis=2)
    g_dc = lax.dynamic_slice_in_dim(g_dc_f, chip * 384, 384, axis=2)
    g_fc = lax.dynamic_slice_in_dim(g_fc_f, chip * 704, 704, axis=2)
    g_meta = lax.dynamic_slice_in_dim(g_meta_f, chip * 256, 256, axis=1)

    local = [d_w_in[:, :IN_DIM].reshape(D, 4, 898).transpose(1, 0, 2), d_w_out.reshape(4, 256, D),
             d_wqkv[:, :D].reshape(4, 256, D), d_wqkv[:, D:D + 256].reshape(4, 256, 256),
             d_wqkv[:, D + 256:].reshape(4, 256, 256), d_wo.reshape(4, 256, D), d_up0, d_up1,
             d_down0.reshape(4, 704, D), d_down1.reshape(4, 704, D)]
    c_idx = jnp.reshape(ic, (1,)).astype(jnp.int32)
    chip_idx = jnp.stack([chip, ic]).astype(jnp.int32)
    g_w_in, g_w_out, g_wq, g_wk, g_wv, g_wo, g_up0, g_up1, g_dn0, g_dn1 = reduce_scatter(
        local, c_idx, chip_idx, ("w_in", "w_out", "wq", "wk", "wv", "wo", "up0", "up1", "down0", "down1"))

    grads = dict(meta_tokens=g_meta, attn_norm_w=g_anw, ffn_norm_w=g_fnw, mix_w_in=g_w_in, conv_a_w=g_ca,
                 dn_conv_w=g_dc, dn_a_log=g_alog, dn_dt_bias=g_dtb, dn_norm_w=g_dnw, mix_w_out=g_w_out,
                 swa_wq=g_wq, swa_wk=g_wk, swa_wv=g_wv, swa_q_norm_w=g_qw, swa_k_norm_w=g_kw, swa_sinks=g_sinks,
                 swa_wo=g_wo, ffn_w_up=[g_up0, g_up1], ffn_conv_w=g_fc, ffn_w_down=[g_dn0, g_dn1])
    weights = dict(meta_tokens=meta_tokens, attn_norm_w=attn_norm_w, ffn_norm_w=ffn_norm_w, mix_w_in=mix_w_in,
                   conv_a_w=conv_a_w, dn_conv_w=dn_conv_w, dn_a_log=dn_a_log, dn_dt_bias=dn_dt_bias,
                   dn_norm_w=dn_norm_w, mix_w_out=mix_w_out, swa_wq=swa_wq, swa_wk=swa_wk, swa_wv=swa_wv,
                   swa_q_norm_w=swa_q_norm_w, swa_k_norm_w=swa_k_norm_w, swa_sinks=swa_sinks, swa_wo=swa_wo,
                   ffn_w_up=ffn_w_up, ffn_conv_w=ffn_conv_w, ffn_w_down=ffn_w_down)
    m_in = dict(meta_tokens=m_meta_tokens, attn_norm_w=m_attn_norm_w, ffn_norm_w=m_ffn_norm_w, mix_w_in=m_mix_w_in,
                conv_a_w=m_conv_a_w, dn_conv_w=m_dn_conv_w, dn_a_log=m_dn_a_log, dn_dt_bias=m_dn_dt_bias,
                dn_norm_w=m_dn_norm_w, mix_w_out=m_mix_w_out, swa_wq=m_swa_wq, swa_wk=m_swa_wk, swa_wv=m_swa_wv,
                swa_q_norm_w=m_swa_q_norm_w, swa_k_norm_w=m_swa_k_norm_w, swa_sinks=m_swa_sinks, swa_wo=m_swa_wo,
                ffn_w_up=m_ffn_w_up, ffn_conv_w=m_ffn_conv_w, ffn_w_down=m_ffn_w_down)
    v_in = dict(meta_tokens=v_meta_tokens, attn_norm_w=v_attn_norm_w, ffn_norm_w=v_ffn_norm_w, mix_w_in=v_mix_w_in,
                conv_a_w=v_conv_a_w, dn_conv_w=v_dn_conv_w, dn_a_log=v_dn_a_log, dn_dt_bias=v_dn_dt_bias,
                dn_norm_w=v_dn_norm_w, mix_w_out=v_mix_w_out, swa_wq=v_swa_wq, swa_wk=v_swa_wk, swa_wv=v_swa_wv,
                swa_q_norm_w=v_swa_q_norm_w, swa_k_norm_w=v_swa_k_norm_w, swa_sinks=v_swa_sinks, swa_wo=v_swa_wo,
                ffn_w_up=v_ffn_w_up, ffn_conv_w=v_ffn_conv_w, ffn_w_down=v_ffn_w_down)
    names = list(weights)
    small = [n for n in names if n not in BIG]
    delta, new_m, new_v = {}, {}, {}
    for n in BIG:
        delta[n], new_m[n], new_v[n], grads[n] = adamw(weights[n], grads[n], m_in[n], v_in[n], name=f"adamw_{n}")
    grads = {n: grads[n].reshape(weights[n].shape) for n in names}
    shapes = [weights[n].shape for n in small]
    packed = [_flat_pad([t[n] for n in small], SW_ROWS) for t in (weights, grads, m_in, v_in)]
    for store, flat in zip((delta, new_m, new_v), adamw(*packed, name="adamw_small")):
        for n, t in zip(small, _split_flat(flat, shapes)):
            store[n] = t
    return (loss, grad_x, *[grads[n] for n in names], *[delta[n] for n in names],
            *[new_m[n] for n in names], *[new_v[n] for n in names])
```

```python
import functools

import jax
import jax.numpy as jnp
from jax import lax
from jax.experimental import pallas as pl
from jax.experimental.pallas import tpu as pltpu

F32 = jnp.float32
BF16 = jnp.bfloat16
HI = lax.Precision.HIGHEST
MESH = pl.DeviceIdType.MESH

D = 1024
N_META = 16
PAD = 112
HEAD0 = PAD + N_META
D_CONV = 512
DN_H = 4
DN_D = 128
DN_DIM = 512
CH = 64
IN_DIM = 3592
P_W = 3840
BG0 = 3584
SWA_H = 16
SWA_KV = 4
SWA_D = 64
BLK = 128
D_FF = 2816
EPS = 1e-6
LR, B1, B2, AEPS, WD, STEP = 0.001, 0.9, 0.999, 1e-08, 0.01, 10
VMEM_LIMIT = 48 * 1024 * 1024
MM_VMEM_BUDGET = 34 * 1024 * 1024
R_BIG = 6144
R_HALF = R_BIG // 2
SV_ROWS = 48
SW_ROWS = 16


def _pick(n, cands):
    for c in cands:
        if n % c == 0:
            return c
    return n


def _params(sem=None):
    return pltpu.CompilerParams(dimension_semantics=sem, vmem_limit_bytes=VMEM_LIMIT)


def _dot(a, b, ca=1, cb=0, prec=None):
    return lax.dot_general(a, b, (((ca,), (cb,)), ((), ())), precision=prec,
                           preferred_element_type=F32)


def _sigmoid(x):
    return 1.0 / (1.0 + jnp.exp(-x))


def _silu(x):
    return x * _sigmoid(x)


def _dsilu(x):
    s = _sigmoid(x)
    return s * (1.0 + x * (1.0 - s))


def _softplus(x):
    return jnp.maximum(x, 0.0) + jnp.log(1.0 + jnp.exp(-jnp.abs(x)))


def mm(a, b, *, name, ta=False, tb=False, out_dtype=F32, add=None, tm=None, tn=None, tk=None,
       b_chip=False, out_chip=False):
    m, k = (a.shape[1], a.shape[0]) if ta else a.shape
    if b_chip:
        n = b.shape[1] if tb else 4 * b.shape[2]
        if tb:
            tk = b.shape[2]
        else:
            tn = b.shape[2]
    else:
        n = b.shape[0] if tb else b.shape[1]
    if out_chip:
        tn = n // 4
    tn = tn or _pick(n, (1408, 1024, 768, 512, 256, 128))
    tk = tk or (_pick(k, (1408, 704, 384, 128)) if ta else _pick(k, (1024, 1408, 768, 512, 128)))
    nk = k // tk
    if tm is None:
        isz = lambda t: jnp.dtype(t.dtype).itemsize
        osz = jnp.dtype(out_dtype).itemsize
        for tm in ((1408, 1024, 512, 384, 256, 128) if ta else (1408, 704, 512, 384, 256, 128)):
            need = 2 * (tm * tk * isz(a) + tk * tn * isz(b) + tm * tn * osz + (tm * tn * 4 if add is not None else 0))
            need += tm * tn * 4 if nk > 1 else 0
            if m % tm == 0 and need <= MM_VMEM_BUDGET:
                break
        else:
            tm = m
    dims = (((0 if ta else 1,), (1 if tb else 0,)), ((), ()))

    def body(*refs):
        if add is None:
            a_ref, b_ref, o_ref, acc_ref = refs
            add_ref = None
        else:
            a_ref, b_ref, add_ref, o_ref, acc_ref = refs
        part = lax.dot_general(a_ref[...].astype(BF16), b_ref[...].astype(BF16), dims,
                               preferred_element_type=F32)

        def finish(total):
            if add_ref is not None:
                total = total + add_ref[...]
            o_ref[...] = total.astype(out_dtype)

        if nk == 1:
            finish(part)
        else:
            kk = pl.program_id(2)

            @pl.when(kk == 0)
            def _():
                acc_ref[...] = part

            @pl.when(kk > 0)
            def _():
                acc_ref[...] += part

            @pl.when(kk == nk - 1)
            def _():
                finish(acc_ref[...])

    a_spec = pl.BlockSpec((tk, tm), lambda i, j, kk: (kk, i)) if ta else pl.BlockSpec((tm, tk), lambda i, j, kk: (i, kk))
    if b_chip and tb:
        b_spec = pl.BlockSpec((None, tn, tk), lambda i, j, kk: (kk, j, 0))
    elif b_chip:
        b_spec = pl.BlockSpec((None, tk, tn), lambda i, j, kk: (j, kk, 0))
    elif tb:
        b_spec = pl.BlockSpec((tn, tk), lambda i, j, kk: (j, kk))
    else:
        b_spec = pl.BlockSpec((tk, tn), lambda i, j, kk: (kk, j))
    o_spec = pl.BlockSpec((tm, tn), lambda i, j, kk: (i, j))
    in_specs = [a_spec, b_spec] + ([o_spec] if add is not None else [])
    args = [a, b] + ([add] if add is not None else [])
    out_spec = pl.BlockSpec((None, tm, tn), lambda i, j, kk: (j, i, 0)) if out_chip else o_spec
    return pl.pallas_call(
        body, name=name, interpret=False,
        out_shape=jax.ShapeDtypeStruct((4, m, tn) if out_chip else (m, n), out_dtype),
        grid=(m // tm, n // tn, nk), in_specs=in_specs, out_specs=out_spec,
        scratch_shapes=[pltpu.VMEM((tm, tn) if nk > 1 else (8, 128), F32)],
        compiler_params=_params(("parallel", "parallel", "arbitrary")),
    )(*args)


def cols(arr, tr, width=None, cb=0):
    width = width or arr.shape[1]
    return (arr, (tr, width), lambda i: (i, cb))


def heads(arr, tr):
    return (arr, (arr.shape[0], tr, arr.shape[2]), lambda i: (0, i, 0))


def whole(arr):
    nd = arr.ndim
    return (arr, arr.shape, lambda i: (0,) * nd)


def rowwise(fn, ins, outs, *, steps, name, accs=()):
    n_in, n_out, n_acc = len(ins), len(outs), len(accs)

    def body(*refs):
        i = pl.program_id(0)
        res = fn(i, *[r[...] for r in refs[:n_in]])
        if not isinstance(res, (tuple, list)):
            res = (res,)
        for r, v in zip(refs[n_in:n_in + n_out], res[:n_out]):
            r[...] = v.astype(r.dtype)
        if n_acc:
            acc_refs = refs[n_in + n_out:]

            @pl.when(i == 0)
            def _():
                for r in acc_refs:
                    r[...] = jnp.zeros(r.shape, r.dtype)

            for r, v in zip(acc_refs, res[n_out:]):
                r[...] += jnp.broadcast_to(v, r.shape).astype(r.dtype)

    def zmap(nd):
        return lambda i: (0,) * nd

    in_specs = [pl.BlockSpec(bs, im) for _, bs, im in ins]
    out_specs = [pl.BlockSpec(bs, im) for _, _, bs, im in outs]
    out_specs += [pl.BlockSpec(s, zmap(len(s))) for s, _ in accs]
    out_shape = [jax.ShapeDtypeStruct(s, d) for s, d, _, _ in outs]
    out_shape += [jax.ShapeDtypeStruct(s, d) for s, d in accs]
    res = pl.pallas_call(
        body, name=name, interpret=False, out_shape=out_shape, grid=(steps,),
        in_specs=in_specs, out_specs=out_specs,
        compiler_params=_params(("arbitrary",)),
    )(*[a for a, _, _ in ins])
    return res


def out2d(rows, width, dtype, tr):
    return ((rows, width), dtype, (tr, width), lambda i: (i, 0))


def conv_fwd(xs, w8, kw, *, rows, c, tc, tr, name, post, extras=(), outs=(), pre=None):
    nx, ne, no = len(xs), len(extras), len(outs)
    nr, nc = rows // tr, c // tc
    r8 = tr // 8

    def body(*refs):
        x_refs = refs[:2 * nx]
        w_ref = refs[2 * nx]
        e_refs = refs[2 * nx + 1:2 * nx + 1 + ne]
        o_refs = refs[2 * nx + 1 + ne:2 * nx + 1 + ne + no]
        scr = refs[-1]
        j, i = pl.program_id(0), pl.program_id(1)
        cur = [x_refs[2 * q][...].astype(F32) for q in range(nx)]
        halo = [x_refs[2 * q + 1][...].astype(F32) for q in range(nx)]
        x = pre(*cur) if pre else cur[0]
        h = pre(*halo) if pre else halo[0]
        scr[0:8, :] = jnp.where(i > 0, h, 0.0)
        scr[8:8 + tr, :] = x
        y = jnp.zeros((tr, tc), F32)
        for q in range(kw):
            s = kw - 1 - q
            y = y + w_ref[q:q + 1, :] * scr[8 - s:8 - s + tr, :]
        res = post(j, y, *[e[...] for e in e_refs])
        if not isinstance(res, (tuple, list)):
            res = (res,)
        for r, v in zip(o_refs, res):
            r[...] = v.astype(r.dtype)

    in_specs, args = [], []
    for arr, cb0 in xs:
        in_specs.append(pl.BlockSpec((tr, tc), lambda j, i, cb0=cb0: (i, cb0 + j)))
        in_specs.append(pl.BlockSpec((8, tc), lambda j, i, cb0=cb0: (jnp.maximum(i * r8 - 1, 0), cb0 + j)))
        args += [arr, arr]
    in_specs.append(pl.BlockSpec((8, tc), lambda j, i: (0, j)))
    args.append(w8)
    for arr, cb0 in extras:
        in_specs.append(pl.BlockSpec((tr, tc), lambda j, i, cb0=cb0: (i, cb0 + j)))
        args.append(arr)
    return pl.pallas_call(
        body, name=name, interpret=False,
        out_shape=[jax.ShapeDtypeStruct((rows, c), dt) for dt in outs],
        grid=(nc, nr), in_specs=in_specs,
        out_specs=[pl.BlockSpec((tr, tc), lambda j, i: (i, j)) for _ in outs],
        scratch_shapes=[pltpu.VMEM((tr + 8, tc), F32)],
        compiler_params=_params(("parallel", "arbitrary")),
    )(*args)


def conv_bwd(xs, w8, kw, dy, *, rows, c, tc, tr, name, post, extras=(), outs=(), pre=None):
    nx, ne, no = len(xs), len(extras), len(outs)
    nr, nc = rows // tr, c // tc
    r8 = tr // 8

    def body(*refs):
        x_refs = refs[:2 * nx]
        w_ref, dy_ref, dyn_ref = refs[2 * nx:2 * nx + 3]
        e_refs = refs[2 * nx + 3:2 * nx + 3 + ne]
        o_refs = refs[2 * nx + 3 + ne:2 * nx + 3 + ne + no]
        dw_ref = refs[2 * nx + 3 + ne + no]
        xscr, gscr = refs[-2], refs[-1]
        i = pl.program_id(1)
        cur = [x_refs[2 * q][...].astype(F32) for q in range(nx)]
        halo = [x_refs[2 * q + 1][...].astype(F32) for q in range(nx)]
        x = pre(*cur) if pre else cur[0]
        h = pre(*halo) if pre else halo[0]
        xscr[0:8, :] = jnp.where(i > 0, h, 0.0)
        xscr[8:8 + tr, :] = x
        g = dy_ref[...].astype(F32)
        gscr[0:tr, :] = g
        gscr[tr:tr + 8, :] = jnp.where(i < nr - 1, dyn_ref[...].astype(F32), 0.0)
        dx = jnp.zeros((tr, tc), F32)
        dws = []
        for q in range(kw):
            s = kw - 1 - q
            dx = dx + w_ref[q:q + 1, :] * gscr[s:s + tr, :]
            dws.append(jnp.sum(g * xscr[8 - s:8 - s + tr, :], axis=0, keepdims=True))
        dws.append(jnp.zeros((8 - kw, tc), F32))
        res = post(dx, *[e[...] for e in e_refs])
        if not isinstance(res, (tuple, list)):
            res = (res,)
        for r, v in zip(o_refs, res):
            r[...] = v.astype(r.dtype)

        @pl.when(i == 0)
        def _():
            dw_ref[...] = jnp.zeros((8, tc), F32)

        dw_ref[...] += jnp.concatenate(dws, axis=0)

    in_specs, args = [], []
    for arr, cb0 in xs:
        in_specs.append(pl.BlockSpec((tr, tc), lambda j, i, cb0=cb0: (i, cb0 + j)))
        in_specs.append(pl.BlockSpec((8, tc), lambda j, i, cb0=cb0: (jnp.maximum(i * r8 - 1, 0), cb0 + j)))
        args += [arr, arr]
    in_specs.append(pl.BlockSpec((8, tc), lambda j, i: (0, j)))
    in_specs.append(pl.BlockSpec((tr, tc), lambda j, i: (i, j)))
    in_specs.append(pl.BlockSpec((8, tc), lambda j, i: (jnp.minimum((i + 1) * r8, nr * r8 - 1), j)))
    args += [w8, dy, dy]
    for arr, cb0 in extras:
        in_specs.append(pl.BlockSpec((tr, tc), lambda j, i, cb0=cb0: (i, cb0 + j)))
        args.append(arr)
    return pl.pallas_call(
        body, name=name, interpret=False,
        out_shape=[jax.ShapeDtypeStruct((rows, c), dt) for dt in outs] + [jax.ShapeDtypeStruct((8, c), F32)],
        grid=(nc, nr), in_specs=in_specs,
        out_specs=[pl.BlockSpec((tr, tc), lambda j, i: (i, j)) for _ in outs] + [pl.BlockSpec((8, tc), lambda j, i: (0, j))],
        scratch_shapes=[pltpu.VMEM((tr + 8, tc), F32), pltpu.VMEM((tr + 8, tc), F32)],
        compiler_params=_params(("parallel", "arbitrary")),
    )(*args)


def rms_fwd(h, w, *, name):
    rows = h.shape[0]
    tr = _pick(rows, (384, 128))

    def fn(i, x, wv):
        r = lax.rsqrt(jnp.mean(x * x, axis=1, keepdims=True) + EPS)
        return x * r * wv

    return rowwise(fn, [cols(h, tr), whole(w)], [out2d(rows, D, BF16, tr)], steps=rows // tr, name=name)[0]


def rms_bwd(h, w, dy, dres, *, name):
    rows = h.shape[0]
    tr = _pick(rows, (384, 128))

    def fn(i, x, wv, g, dr):
        r = lax.rsqrt(jnp.mean(x * x, axis=1, keepdims=True) + EPS)
        xh = x * r
        gw = g * wv
        dx = r * (gw - xh * jnp.mean(gw * xh, axis=1, keepdims=True))
        row = i * tr + lax.broadcasted_iota(jnp.int32, (tr, 1), 0)
        return jnp.where(row >= PAD, dr + dx, 0.0), jnp.sum(g * xh, axis=0, keepdims=True)

    return rowwise(fn, [cols(h, tr), whole(w), cols(dy, tr), cols(dres, tr)], [out2d(rows, D, F32, tr)],
                   steps=rows // tr, name=name, accs=[((1, D), F32)])


def loss_grad(h, target):
    rows = h.shape[0]

    def fn(i, y, t):
        diff = jnp.where(i > 0, y - t, 0.0)
        part = jnp.sum(jnp.sum(diff * diff, axis=1, keepdims=True), axis=0, keepdims=True)
        return diff * (1.0 / D), part * (0.5 / D)

    tgt = (target, (BLK, D), lambda i: (jnp.maximum(i - 1, 0), 0))
    return rowwise(fn, [cols(h, BLK), tgt], [out2d(rows, D, F32, BLK)], steps=rows // BLK,
                   name="loss_grad", accs=[((1, 128), F32)])


def adamw(w, g, m, v, *, name):
    shape = w.shape
    gs = list(g) if isinstance(g, (list, tuple)) else [g]
    nl = len(gs)
    w2, m2, v2 = (t.reshape(-1, shape[-1]) for t in (w, m, v))
    rows, width = w2.shape
    rl = rows // nl
    tr = _pick(rl, (256, 176, 128, 64, 16, 8))
    nr = rl // tr

    def fn(i, wv, mv, vv, *gvs):
        gv = gvs[0]
        for layer in range(1, nl):
            gv = jnp.where(i >= layer * nr, gvs[layer], gv)
        mn = B1 * mv + (1.0 - B1) * gv
        vn = B2 * vv + (1.0 - B2) * gv * gv
        mh = mn / (1.0 - B1 ** STEP)
        vh = vn / (1.0 - B2 ** STEP)
        return -LR * (mh / (jnp.sqrt(vh) + AEPS) + WD * wv), mn, vn, gv

    g_ins = [(t.reshape(rl, width), (tr, width), lambda i, layer=layer: (jnp.clip(i - layer * nr, 0, nr - 1), 0))
             for layer, t in enumerate(gs)]
    res = rowwise(fn, [cols(t, tr) for t in (w2, m2, v2)] + g_ins, [out2d(rows, width, F32, tr)] * 4,
                  steps=rows // tr, name=name)
    return [r.reshape(shape) for r in res]


HB = DN_H * CH


def _split(a):
    hi = a.astype(BF16)
    return hi, (a - hi.astype(F32)).astype(BF16)


def _dot1(a, b, ca=1, cb=0):
    return _dot(a.astype(BF16), b.astype(BF16), ca, cb)


def _dot3(a, b, ca=1, cb=0):
    ah, al = _split(a)
    bh, bl = _split(b)
    return _dot(ah, bh, ca, cb) + (_dot(ah, bl, ca, cb) + _dot(al, bh, ca, cb))


def _dot01(m01, b, ca=1, cb=0):
    bh, bl = _split(b)
    m = m01.astype(BF16)
    return _dot(m, bh, ca, cb) + _dot(m, bl, ca, cb)


def _stack(x):
    return jnp.concatenate([x[:, h * DN_D:(h + 1) * DN_D] for h in range(DN_H)], axis=0)


def _unstack(x):
    return jnp.concatenate([x[h * CH:(h + 1) * CH] for h in range(DN_H)], axis=1)


def _tri_inv(a, blk, eye):
    ad = jnp.where(blk, a, 0.0)
    lo = a - ad
    a2 = _dot3(ad, ad)
    a4 = _dot3(a2, a2)
    a8 = _dot3(a4, a4)
    dgi = _dot3(_dot3(_dot3(eye - ad, eye + a2), eye + a4), eye + a8)
    n = _dot3(dgi, lo)
    return _dot3(_dot3(eye - n, eye + _dot3(n, n)), dgi)


def _dn_masks():
    row = lax.broadcasted_iota(jnp.int32, (HB, HB), 0)
    col = lax.broadcasted_iota(jnp.int32, (HB, HB), 1)
    same = (row // CH) == (col // CH)
    incl = jnp.logical_and(same, row >= col)
    strict = jnp.logical_and(same, row > col)
    upper = jnp.logical_and(same, row <= col)
    blk = (row // 16) == (col // 16)
    eye = (row == col).astype(F32)
    return incl, strict, upper, blk, eye


def _dn_chunk(q_ref, k_ref, v_ref, bc_ref, br_ref, incl, strict):
    r64 = lax.broadcasted_iota(jnp.int32, (CH, CH), 0)
    c64 = lax.broadcasted_iota(jnp.int32, (CH, CH), 1)
    bc = bc_ref[...]
    dcol = _dot01((r64 >= c64).astype(F32), bc)
    drow = _dot3(br_ref[0], (r64 <= c64).astype(F32))
    col = lambda m, l0: jnp.concatenate([m[:, l0 + h:l0 + h + 1] for h in range(DN_H)], axis=0)
    b_c = col(bc, 0)
    d_c = col(dcol, 4)
    d_r = jnp.concatenate([drow[4 + h:5 + h, :] for h in range(DN_H)], axis=1)
    d_last_h = [dcol[CH - 1:CH, 4 + h:5 + h] for h in range(DN_H)]
    d_last = jnp.concatenate([jnp.broadcast_to(t, (CH, 1)) for t in d_last_h], axis=0)
    q, k, v = _stack(q_ref[...]), _stack(k_ref[...]), _stack(v_ref[...])
    dm = jnp.where(incl, jnp.exp(jnp.where(incl, d_c - d_r, 0.0)), 0.0)
    kk = _dot1(k, k, 1, 1)
    a = jnp.where(strict, b_c * kk * dm, 0.0)
    ed = jnp.exp(d_c)
    rhs = jnp.concatenate([v * b_c, k * (b_c * ed)], axis=1)
    qk = _dot1(q, k, 1, 1) * dm
    ekd = jnp.exp(d_last - d_c)
    gl = [jnp.exp(t) for t in d_last_h]
    return q, k, v, b_c, dm, kk, a, ed, rhs, qk, ekd, gl


def dn_fwd(qkv_n, bgcol, bgrow):
    rows = qkv_n.shape[0]
    nch = rows // CH

    def body(q_ref, k_ref, v_ref, bc_ref, br_ref, o_ref, s_out, ti_out, s_scr):
        n = pl.program_id(0)

        @pl.when(n == 0)
        def _():
            s_scr[...] = jnp.zeros(s_scr.shape, F32)

        incl, strict, _, blk, eye = _dn_masks()
        q, k, v, b_c, dm, kk, a, ed, rhs, qk, ekd, gl = _dn_chunk(q_ref, k_ref, v_ref, bc_ref, br_ref, incl, strict)
        tinv = _tri_inv(a, blk, eye)
        ti_out[0] = tinv
        sol = _dot3(tinv, rhs)
        u, w = sol[:, :DN_D], sol[:, DN_D:]
        qd, kd = q * ed, k * ekd
        v_new, o_state = [], []
        for h in range(DN_H):
            rs = slice(h * CH, (h + 1) * CH)
            s = s_scr[h]
            s_out[0, h] = s
            vn = u[rs] - _dot1(w[rs], s)
            v_new.append(vn)
            o_state.append(_dot1(qd[rs], s))
            s_scr[h] = gl[h] * s + _dot1(kd[rs], vn, 0, 0)
        o = jnp.concatenate(o_state, axis=0) + _dot1(qk, jnp.concatenate(v_new, axis=0))
        o_ref[...] = _unstack(o)

    return pl.pallas_call(
        body, name="dn_fwd", interpret=False,
        out_shape=[jax.ShapeDtypeStruct((rows, DN_DIM), F32),
                   jax.ShapeDtypeStruct((nch, DN_H, DN_D, DN_D), F32),
                   jax.ShapeDtypeStruct((nch, HB, HB), F32)],
        grid=(nch,),
        in_specs=[pl.BlockSpec((CH, DN_DIM), lambda n: (n, 0)),
                  pl.BlockSpec((CH, DN_DIM), lambda n: (n, 1)),
                  pl.BlockSpec((CH, DN_DIM), lambda n: (n, 2)),
                  pl.BlockSpec((CH, 128), lambda n: (n, 0)),
                  pl.BlockSpec((1, 8, CH), lambda n: (n, 0, 0))],
        out_specs=[pl.BlockSpec((CH, DN_DIM), lambda n: (n, 0)),
                   pl.BlockSpec((1, DN_H, DN_D, DN_D), lambda n: (n, 0, 0, 0)),
                   pl.BlockSpec((1, HB, HB), lambda n: (n, 0, 0))],
        scratch_shapes=[pltpu.VMEM((DN_H, DN_D, DN_D), F32)],
        compiler_params=_params(("arbitrary",)),
    )(qkv_n, qkv_n, qkv_n, bgcol, bgrow)


def dn_bwd(qkv_n, bgcol, bgrow, s_all, ti_all, do):
    rows = qkv_n.shape[0]
    nch = rows // CH

    def body(q_ref, k_ref, v_ref, bc_ref, br_ref, s_ref, ti_ref, do_ref, dq_ref, dk_ref, dv_ref, dbg_ref, ds_scr):
        n = pl.program_id(0)

        @pl.when(n == 0)
        def _():
            ds_scr[...] = jnp.zeros(ds_scr.shape, F32)

        incl, strict, upper, _, _ = _dn_masks()
        q, k, v, b_c, dm, kk, a, ed, rhs, qk, ekd, gl = _dn_chunk(q_ref, k_ref, v_ref, bc_ref, br_ref, incl, strict)
        tinv = ti_ref[0]
        g_o = _stack(do_ref[...])
        sol = _dot3(tinv, rhs)
        u, w = sol[:, :DN_D], sol[:, DN_D:]
        qd, kd = q * ed, k * ekd
        rsum = lambda t: jnp.sum(t, axis=1, keepdims=True)
        rows_of = [slice(h * CH, (h + 1) * CH) for h in range(DN_H)]
        s_h = [s_ref[0, h] for h in range(DN_H)]
        ds_h = [ds_scr[h] for h in range(DN_H)]
        v_new = jnp.concatenate([u[rs] - _dot1(w[rs], s) for rs, s in zip(rows_of, s_h)], axis=0)
        dv_new = _dot1(qk, g_o, 0, 0) + jnp.concatenate([_dot1(kd[rs], t) for rs, t in zip(rows_of, ds_h)], axis=0)
        dqd = jnp.concatenate([_dot1(g_o[rs], s, 1, 1) for rs, s in zip(rows_of, s_h)], axis=0)
        dkd = jnp.concatenate([_dot1(v_new[rs], t, 1, 1) for rs, t in zip(rows_of, ds_h)], axis=0)
        for h, rs in enumerate(rows_of):
            ds_scr[h] = _dot1(qd[rs], g_o[rs], 0, 0) + gl[h] * ds_h[h] - _dot1(w[rs], dv_new[rs], 0, 0)
        dw = jnp.concatenate([-_dot1(dv_new[rs], s, 1, 1) for rs, s in zip(rows_of, s_h)], axis=0)
        dqk = _dot1(g_o, v_new, 1, 1)
        drhs = _dot3(tinv, jnp.concatenate([dv_new, dw], axis=1), 0, 0)
        da = jnp.where(strict, -_dot1(drhs, sol, 1, 1), 0.0)
        drhs_u, drhs_w = drhs[:, :DN_D], drhs[:, DN_D:]
        s2 = rsum(drhs_w * k)
        dbeta = rsum(drhs_u * v) + s2 * ed + rsum(da * kk * dm)
        dkk = da * b_c * dm
        dqkr = dqk * dm
        mmat = da * a + dqk * qk
        tmp = rsum(dkd * kd)
        dd = (s2 * b_c * ed + rsum(mmat) - _dot3(mmat, jnp.ones((HB, 128), F32), 0, 0)[:, :1] + rsum(dqd * qd) - tmp)
        rowi = lax.broadcasted_iota(jnp.int32, (CH, 1), 0)
        last = []
        for h, rs in enumerate(rows_of):
            dgl = jnp.sum(rsum(s_h[h] * ds_h[h]), axis=0, keepdims=True)
            dd_last = jnp.sum(tmp[rs], axis=0, keepdims=True) + dgl * gl[h]
            last.append(jnp.where(rowi == CH - 1, dd_last, 0.0))
        dd = dd + jnp.concatenate(last, axis=0)
        dq_ref[...] = _unstack(_dot1(dqkr, k) + dqd * ed)
        dk_ref[...] = _unstack(drhs_w * (b_c * ed) + _dot1(dkk, k) + _dot1(dkk, k, 0, 0) + _dot1(dqkr, q, 0, 0)
                               + dkd * ekd)
        dv_ref[...] = _unstack(drhs_u * b_c)
        dg = _dot01(upper.astype(F32), jnp.broadcast_to(dd, (HB, 128)))[:, :1]
        lane = lax.broadcasted_iota(jnp.int32, (CH, 128), 1)
        out = jnp.zeros((CH, 128), F32)
        for h, rs in enumerate(rows_of):
            out = out + jnp.where(lane == h, dbeta[rs], 0.0) + jnp.where(lane == 4 + h, dg[rs], 0.0)
        dbg_ref[...] = out

    rev = lambda n: nch - 1 - n
    return pl.pallas_call(
        body, name="dn_bwd", interpret=False,
        out_shape=[jax.ShapeDtypeStruct((rows, DN_DIM), F32)] * 3 + [jax.ShapeDtypeStruct((rows, 128), F32)],
        grid=(nch,),
        in_specs=[pl.BlockSpec((CH, DN_DIM), lambda n: (rev(n), 0)),
                  pl.BlockSpec((CH, DN_DIM), lambda n: (rev(n), 1)),
                  pl.BlockSpec((CH, DN_DIM), lambda n: (rev(n), 2)),
                  pl.BlockSpec((CH, 128), lambda n: (rev(n), 0)),
                  pl.BlockSpec((1, 8, CH), lambda n: (rev(n), 0, 0)),
                  pl.BlockSpec((1, DN_H, DN_D, DN_D), lambda n: (rev(n), 0, 0, 0)),
                  pl.BlockSpec((1, HB, HB), lambda n: (rev(n), 0, 0)),
                  pl.BlockSpec((CH, DN_DIM), lambda n: (rev(n), 0))],
        out_specs=[pl.BlockSpec((CH, DN_DIM), lambda n: (rev(n), 0))] * 3 + [pl.BlockSpec((CH, 128), lambda n: (rev(n), 0))],
        scratch_shapes=[pltpu.VMEM((DN_H, DN_D, DN_D), F32)],
        compiler_params=_params(("arbitrary",)),
    )(qkv_n, qkv_n, qkv_n, bgcol, bgrow, s_all, ti_all, do)


def _swa_valid(n):
    c3 = lax.broadcasted_iota(jnp.int32, (3 * BLK, 4 * BLK), 0)
    r = lax.broadcasted_iota(jnp.int32, (3 * BLK, 4 * BLK), 1) % BLK
    c = c3 % BLK
    lo = jnp.where(c3 < BLK, PAD, jnp.where(c3 < 2 * BLK, r + 1 + jnp.where(n >= 2, 0, BLK), 0))
    hi = jnp.where(c3 < BLK, r + jnp.where(n >= 1, BLK, 0), jnp.where(c3 < 2 * BLK, BLK, r - jnp.where(n >= 1, 0, BLK)))
    return jnp.logical_and(c >= lo, c <= hi)


def _swa_probs(q, kcat, valid, sink):
    s = jnp.where(valid, _dot(kcat, q, 1, 1), -1e30)
    m = jnp.maximum(jnp.max(s, axis=0, keepdims=True), sink)
    e = jnp.where(valid, jnp.exp(s - m), 0.0)
    es = jnp.exp(sink - m)
    inv = 1.0 / (jnp.sum(e, axis=0, keepdims=True) + es)
    return e * inv, es * inv


def _swa_group(q_ref, sk_ref, h):
    q4 = jnp.concatenate([q_ref[4 * h + g] for g in range(4)], axis=0)
    sink4 = jnp.concatenate([jnp.full((1, BLK), sk_ref[4 * h + g], F32) for g in range(4)], axis=1)
    return q4, sink4


def _swa_specs():
    q = pl.BlockSpec((SWA_H, BLK, SWA_D), lambda n: (0, n, 0))
    km = pl.BlockSpec((SWA_KV, BLK, SWA_D), lambda n: (0, 0, 0))
    kp = pl.BlockSpec((SWA_KV, BLK, SWA_D), lambda n: (0, jnp.maximum(n - 1, 0), 0))
    kc = pl.BlockSpec((SWA_KV, BLK, SWA_D), lambda n: (0, n, 0))
    return [q, km, kp, kc, km, kp, kc]


def swa_fwd(qh, kh, vh, sinks):
    rows = qh.shape[1]
    nb = rows // BLK

    def body(q_ref, km, kp, kc, vm, vp, vc, sk_ref, o_ref):
        n = pl.program_id(0)
        valid = _swa_valid(n)
        outs = []
        for h in range(SWA_KV):
            kcat = jnp.concatenate([km[h], kp[h], kc[h]], axis=0)
            vcat = jnp.concatenate([vm[h], vp[h], vc[h]], axis=0)
            q4, sink4 = _swa_group(q_ref, sk_ref, h)
            p, _ = _swa_probs(q4, kcat, valid, sink4)
            o4 = _dot(p.astype(BF16), vcat, 0, 0)
            outs += [o4[g * BLK:(g + 1) * BLK] for g in range(4)]
        o_ref[...] = jnp.concatenate(outs, axis=1).astype(BF16)

    return pl.pallas_call(
        body, name="swa_fwd", interpret=False,
        out_shape=jax.ShapeDtypeStruct((rows, SWA_H * SWA_D), BF16),
        grid=(nb,),
        in_specs=_swa_specs() + [pl.BlockSpec(memory_space=pltpu.SMEM)],
        out_specs=pl.BlockSpec((BLK, SWA_H * SWA_D), lambda n: (n, 0)),
        compiler_params=_params(("parallel",)),
    )(qh, kh, kh, kh, vh, vh, vh, sinks)


def swa_bwd(qh, kh, vh, sinks, do):
    rows = qh.shape[1]
    nb = rows // BLK

    def body(q_ref, km, kp, kc, vm, vp, vc, do_ref, sk_ref, dq_ref, dk_ref, dv_ref, dsk_ref):
        n = pl.program_id(0)

        @pl.when(n == 0)
        def _():
            dk_ref[...] = jnp.zeros(dk_ref.shape, F32)
            dv_ref[...] = jnp.zeros(dv_ref.shape, F32)

        valid = _swa_valid(n)
        g_all = do_ref[...]
        rowi = lax.broadcasted_iota(jnp.int32, (SWA_H, 128), 0)
        dsk = jnp.zeros((SWA_H, 128), F32)
        pm = pl.multiple_of(jnp.maximum(n - 1, 0) * BLK, BLK)
        pc = pl.multiple_of(n * BLK, BLK)
        for h in range(SWA_KV):
            kcat = jnp.concatenate([km[h], kp[h], kc[h]], axis=0)
            vcat = jnp.concatenate([vm[h], vp[h], vc[h]], axis=0)
            q4, sink4 = _swa_group(q_ref, sk_ref, h)
            p, ps = _swa_probs(q4, kcat, valid, sink4)
            g4 = jnp.concatenate([g_all[:, (4 * h + g) * SWA_D:(4 * h + g + 1) * SWA_D] for g in range(4)], axis=0)
            dp = _dot(vcat, g4, 1, 1)
            delta = jnp.sum(p * dp, axis=0, keepdims=True)
            ds = (p * (dp - delta)).astype(BF16)
            dq4 = _dot(ds, kcat, 0, 0)
            dkc = _dot(ds, q4)
            dvc = _dot(p.astype(BF16), g4)
            t = ps * delta
            for g in range(4):
                dq_ref[4 * h + g] = dq4[g * BLK:(g + 1) * BLK]
                part = -jnp.sum(t[:, g * BLK:(g + 1) * BLK], axis=1, keepdims=True)
                dsk = dsk + jnp.where(rowi == 4 * h + g, part, 0.0)
            lanes = slice(h * SWA_D, (h + 1) * SWA_D)
            for ref, val in ((dk_ref, dkc), (dv_ref, dvc)):
                ref[0:BLK, lanes] += val[0:BLK]
                ref[pl.ds(pm, BLK), lanes] += val[BLK:2 * BLK]
                ref[pl.ds(pc, BLK), lanes] += val[2 * BLK:]
        dsk_ref[0] = dsk

    return pl.pallas_call(
        body, name="swa_bwd", interpret=False,
        out_shape=[jax.ShapeDtypeStruct((SWA_H, rows, SWA_D), F32),
                   jax.ShapeDtypeStruct((rows, SWA_KV * SWA_D), F32),
                   jax.ShapeDtypeStruct((rows, SWA_KV * SWA_D), F32),
                   jax.ShapeDtypeStruct((nb, SWA_H, 128), F32)],
        grid=(nb,),
        in_specs=_swa_specs() + [pl.BlockSpec((BLK, SWA_H * SWA_D), lambda n: (n, 0)),
                                 pl.BlockSpec(memory_space=pltpu.SMEM)],
        out_specs=[pl.BlockSpec((SWA_H, BLK, SWA_D), lambda n: (0, n, 0)),
                   pl.BlockSpec((rows, SWA_KV * SWA_D), lambda n: (0, 0)),
                   pl.BlockSpec((rows, SWA_KV * SWA_D), lambda n: (0, 0)),
                   pl.BlockSpec((1, SWA_H, 128), lambda n: (n, 0, 0))],
        compiler_params=_params(("arbitrary",)),
    )(qh, kh, kh, kh, vh, vh, vh, do, sinks)


def qknorm_fwd(qkv, qw, kw):
    rows = qkv.shape[0]
    tr = _pick(rows, (384, 128))
    scale = SWA_D ** -0.5

    def fn(i, x, qwv, kwv):
        def normed(j, wv, sc):
            xs = x[:, j * SWA_D:(j + 1) * SWA_D]
            r = lax.rsqrt(jnp.mean(xs * xs, axis=1, keepdims=True) + EPS)
            return (xs * r * wv * sc)[None]
        qo = jnp.concatenate([normed(j, qwv, scale) for j in range(SWA_H)], axis=0)
        ko = jnp.concatenate([normed(SWA_H + j, kwv, 1.0) for j in range(SWA_KV)], axis=0)
        vo = jnp.concatenate([x[:, (SWA_H + SWA_KV + j) * SWA_D:(SWA_H + SWA_KV + j + 1) * SWA_D][None]
                              for j in range(SWA_KV)], axis=0)
        return qo, ko, vo

    hm = lambda nh: ((nh, rows, SWA_D), BF16, (nh, tr, SWA_D), lambda i: (0, i, 0))
    return rowwise(fn, [cols(qkv, tr), whole(qw), whole(kw)], [hm(SWA_H), hm(SWA_KV), hm(SWA_KV)],
                   steps=rows // tr, name="qknorm_fwd")


def qknorm_bwd(qkv, qw, kw, dqh, dkh, dvh):
    rows = qkv.shape[0]
    tr = _pick(rows, (384, 128))
    scale = SWA_D ** -0.5

    def fn(i, x, qwv, kwv, dq, dk, dv):
        pieces = []
        dws = [jnp.zeros((1, SWA_D), F32), jnp.zeros((1, SWA_D), F32)]

        def one(j, dy, wv, sc, which):
            xs = x[:, j * SWA_D:(j + 1) * SWA_D]
            r = lax.rsqrt(jnp.mean(xs * xs, axis=1, keepdims=True) + EPS)
            xh = xs * r
            gw = dy * wv * sc
            pieces.append(r * (gw - xh * jnp.mean(gw * xh, axis=1, keepdims=True)))
            dws[which] = dws[which] + jnp.sum(dy * sc * xh, axis=0, keepdims=True)

        for j in range(SWA_H):
            one(j, dq[j], qwv, scale, 0)
        for j in range(SWA_KV):
            one(SWA_H + j, dk[:, j * SWA_D:(j + 1) * SWA_D], kwv, 1.0, 1)
        pieces.append(dv)
        return jnp.concatenate(pieces, axis=1), dws[0], dws[1]

    return rowwise(fn, [cols(qkv, tr), whole(qw), whole(kw), heads(dqh, tr), cols(dkh, tr), cols(dvh, tr)],
                   [out2d(rows, 1536, BF16, tr)], steps=rows // tr, name="qknorm_bwd",
                   accs=[((1, SWA_D), F32), ((1, SWA_D), F32)])


def _place():
    return lax.axis_index("x"), lax.axis_index("y"), lax.axis_index("c")


ANY = pl.BlockSpec(memory_space=pl.ANY)


def _rcopy(ssem, rsem, k, src, dst, to):
    return pltpu.make_async_remote_copy(src_ref=src, dst_ref=dst, send_sem=ssem.at[k], recv_sem=rsem.at[k],
                                        device_id=to, device_id_type=MESH)


def gather_weights(shards, small):
    n = len(shards)
    halves = [t.shape[0] // 2 for t in shards]

    def body(*refs):
        s_refs, small_ref = refs[:n], refs[n]
        o_refs, osmall = refs[n + 1:2 * n + 1], refs[2 * n + 1]
        ssem, rsem, lsem = refs[2 * n + 2:]
        x, y, c = _place()
        me = 2 * x + y
        chips = [(1 - x, y), (x, 1 - y), (1 - x, 1 - y)]

        def half(k, s, hh):
            return o_refs[k].at[s, pl.ds(hh * halves[k], halves[k]), :]

        loc = pltpu.make_async_copy(small_ref, osmall.at[me], lsem)
        loc.start()
        sends = []
        for k in range(n):
            for j, (px, py) in enumerate(chips):
                sends.append(_rcopy(ssem, rsem, 6 * k + j, s_refs[k].at[pl.ds(c * halves[k], halves[k]), :],
                                    half(k, me, c), (px, py, c)))
        for j, (px, py) in enumerate(chips):
            sends.append(_rcopy(ssem, rsem, 6 * n + j, small_ref, osmall.at[me], (px, py, c)))
        for cp in sends:
            cp.start()
        for k in range(n):
            for j, (px, py) in enumerate(chips):
                s = 2 * px + py
                _rcopy(ssem, rsem, 6 * k + j, half(k, s, c), half(k, s, c), (x, y, c)).wait_recv()
                fwd = _rcopy(ssem, rsem, 6 * k + 3 + j, half(k, s, c), half(k, s, c), (x, y, 1 - c))
                fwd.start()
                sends.append(fwd)
        for k in range(n):
            for j, (px, py) in enumerate(chips):
                s = 2 * px + py
                _rcopy(ssem, rsem, 6 * k + 3 + j, half(k, s, 1 - c), half(k, s, 1 - c), (x, y, c)).wait_recv()
        for j, (px, py) in enumerate(chips):
            s = 2 * px + py
            _rcopy(ssem, rsem, 6 * n + j, osmall.at[s], osmall.at[s], (x, y, c)).wait_recv()
        for cp in sends:
            cp.wait_send()
        loc.wait()

    res = pl.pallas_call(
        body, name="gather_weights", interpret=False,
        out_shape=[jax.ShapeDtypeStruct((4,) + t.shape, t.dtype) for t in shards]
        + [jax.ShapeDtypeStruct((4, SW_ROWS, 1024), F32)],
        in_specs=[ANY] * (n + 1), out_specs=[ANY] * (n + 1),
        scratch_shapes=[pltpu.SemaphoreType.DMA((6 * n + 3,)), pltpu.SemaphoreType.DMA((6 * n + 3,)),
                        pltpu.SemaphoreType.DMA],
    )(*shards, small)
    return res[:n], res[n]


def swap_halves(gs):
    n = len(gs)

    def body(*refs):
        g_refs, o_refs, ssem, rsem = refs[:n], refs[n:2 * n], refs[2 * n], refs[2 * n + 1]
        x, y, c = _place()
        cps = []
        for k in range(n):
            hk = g_refs[k].shape[1] // 2
            cps.append(_rcopy(ssem, rsem, k, g_refs[k].at[:, pl.ds((1 - c) * hk, hk), :], o_refs[k], (x, y, 1 - c)))
        for cp in cps:
            cp.start()
        for cp in cps:
            cp.wait()

    return pl.pallas_call(
        body, name="swap_halves", interpret=False,
        out_shape=[jax.ShapeDtypeStruct((4, t.shape[1] // 2, t.shape[2]), t.dtype) for t in gs],
        in_specs=[ANY] * n, out_specs=[ANY] * n,
        scratch_shapes=[pltpu.SemaphoreType.DMA((n,)), pltpu.SemaphoreType.DMA((n,))],
    )(*gs)


def _sum_rows(hk):
    return _pick(hk, (512, 352, 256, 128))


def pair_sum(g, other, c_idx, *, name):
    _, hk, width = other.shape
    tr = _sum_rows(hk)
    nbk = hk // tr

    def body(c_ref, g_ref, o_ref, out_ref):
        out_ref[...] = (g_ref[...].astype(F32) + o_ref[...].astype(F32)).astype(BF16)

    return pl.pallas_call(
        body, name=name, interpret=False,
        out_shape=jax.ShapeDtypeStruct((4, hk, width), BF16),
        grid_spec=pltpu.PrefetchScalarGridSpec(
            num_scalar_prefetch=1, grid=(4, nbk),
            in_specs=[pl.BlockSpec((1, tr, width), lambda s, i, c_ref: (s, c_ref[0] * nbk + i, 0)),
                      pl.BlockSpec((1, tr, width), lambda s, i, c_ref: (s, i, 0))],
            out_specs=pl.BlockSpec((1, tr, width), lambda s, i, c_ref: (s, i, 0))),
        compiler_params=_params(("parallel", "parallel")),
    )(c_idx, g, other)


def scatter_chips(ps):
    n = len(ps)

    def body(*refs):
        p_refs, o_refs, ssem, rsem = refs[:n], refs[n:2 * n], refs[2 * n], refs[2 * n + 1]
        x, y, c = _place()
        chips = [(1 - x, y), (x, 1 - y), (1 - x, 1 - y)]
        cps = [_rcopy(ssem, rsem, 3 * k + j, p_refs[k].at[2 * px + py], o_refs[k].at[j], (px, py, c))
               for k in range(n) for j, (px, py) in enumerate(chips)]
        for cp in cps:
            cp.start()
        for cp in cps:
            cp.wait()

    return pl.pallas_call(
        body, name="scatter_chips", interpret=False,
        out_shape=[jax.ShapeDtypeStruct((3,) + t.shape[1:], t.dtype) for t in ps],
        in_specs=[ANY] * n, out_specs=[ANY] * n,
        scratch_shapes=[pltpu.SemaphoreType.DMA((3 * n,)), pltpu.SemaphoreType.DMA((3 * n,))],
    )(*ps)


def chip_sum(p, got, idx, *, name):
    _, hk, width = got.shape
    tr = _sum_rows(hk)
    nbk = hk // tr

    def body(idx_ref, p_ref, g_ref, out_ref):
        acc = p_ref[0].astype(F32)
        for j in range(3):
            acc = acc + g_ref[j].astype(F32)
        out_ref[0] = acc

    return pl.pallas_call(
        body, name=name, interpret=False,
        out_shape=jax.ShapeDtypeStruct((2, hk, width), F32),
        grid_spec=pltpu.PrefetchScalarGridSpec(
            num_scalar_prefetch=1, grid=(nbk,),
            in_specs=[pl.BlockSpec((1, tr, width), lambda i, idx_ref: (idx_ref[0], i, 0)),
                      pl.BlockSpec((3, tr, width), lambda i, idx_ref: (0, i, 0))],
            out_specs=pl.BlockSpec((1, tr, width), lambda i, idx_ref: (idx_ref[1], i, 0))),
        compiler_params=_params(("parallel",)),
    )(idx, p, got)


def join_halves(qs):
    n = len(qs)

    def body(*refs):
        q_refs, o_refs, ssem, rsem = refs[:n], refs[n:2 * n], refs[2 * n], refs[2 * n + 1]
        x, y, c = _place()
        cps = [_rcopy(ssem, rsem, k, q_refs[k].at[c], o_refs[k].at[c], (x, y, 1 - c)) for k in range(n)]
        for cp in cps:
            cp.start()
        for k in range(n):
            _rcopy(ssem, rsem, k, q_refs[k].at[c], o_refs[k].at[1 - c], (x, y, 1 - c)).wait_recv()
        for cp in cps:
            cp.wait_send()

    return pl.pallas_call(
        body, name="join_halves", interpret=False,
        out_shape=[jax.ShapeDtypeStruct(t.shape, t.dtype) for t in qs],
        in_specs=[ANY] * n, out_specs=[ANY] * n, input_output_aliases={k: k for k in range(n)},
        scratch_shapes=[pltpu.SemaphoreType.DMA((n,)), pltpu.SemaphoreType.DMA((n,))],
    )(*qs)


def reduce_scatter(gs, c_idx, idx, names):
    others = swap_halves(gs)
    pairs = [pair_sum(g, o, c_idx, name=f"pair_sum_{nm}") for g, o, nm in zip(gs, others, names)]
    gots = scatter_chips(pairs)
    mine = [chip_sum(p, g, idx, name=f"chip_sum_{nm}") for p, g, nm in zip(pairs, gots, names)]
    return [q.reshape(2 * q.shape[1], q.shape[2]) for q in join_halves(mine)]


def gather_small(v):
    def body(v_ref, o_ref, ssem, rsem, lsem):
        x, y, c = _place()
        loc = pltpu.make_async_copy(v_ref, o_ref.at[4 * x + 2 * y + c], lsem)
        loc.start()
        cps = []
        for k in range(1, 8):
            fx, fy, fc = (k >> 2) & 1, (k >> 1) & 1, k & 1
            px = 1 - x if fx else x
            py = 1 - y if fy else y
            pc = 1 - c if fc else c
            cps.append((pltpu.make_async_remote_copy(
                src_ref=v_ref, dst_ref=o_ref.at[4 * x + 2 * y + c], send_sem=ssem.at[k - 1], recv_sem=rsem.at[k - 1],
                device_id=(px, py, pc), device_id_type=MESH), 4 * px + 2 * py + pc))
        for cp, _ in cps:
            cp.start()
        for k, (cp, peer) in enumerate(cps):
            pltpu.make_async_remote_copy(
                src_ref=v_ref, dst_ref=o_ref.at[peer], send_sem=ssem.at[k], recv_sem=rsem.at[k],
                device_id=(x, y, c), device_id_type=MESH).wait_recv()
        for cp, _ in cps:
            cp.wait_send()
        loc.wait()

    return pl.pallas_call(
        body, name="gather_small", interpret=False,
        out_shape=jax.ShapeDtypeStruct((8, SV_ROWS, 1024), F32),
        in_specs=[ANY], out_specs=ANY,
        scratch_shapes=[pltpu.SemaphoreType.DMA((7,)), pltpu.SemaphoreType.DMA((7,)), pltpu.SemaphoreType.DMA],
    )(v)


def sum_slots(a):
    def fn(i, t):
        acc = t[0]
        for k in range(1, 8):
            acc = acc + t[k]
        return acc

    return rowwise(fn, [whole(a)], [((SV_ROWS, 1024), F32, (SV_ROWS, 1024), lambda i: (0, 0))], steps=1,
                   name="sum_slots")[0]


def _head_rms(x, nw):
    xs, rs = [], []
    for h in range(DN_H):
        xh = x[:, h * DN_D:(h + 1) * DN_D]
        r = lax.rsqrt(jnp.mean(xh * xh, axis=1, keepdims=True) + EPS)
        xs.append(xh * r)
        rs.append(r)
    return xs, rs


def bg_fwd(p, alog, dtb):
    rows = p.shape[0]
    tr = _pick(rows, (384, 128))

    def fn(i, x, al, dt):
        lane = lax.broadcasted_iota(jnp.int32, (tr, 128), 1)
        row = i * tr + lax.broadcasted_iota(jnp.int32, (tr, 128), 0)
        g = -jnp.exp(al) * _softplus(x + dt)
        out = jnp.where(lane < 4, _sigmoid(x), jnp.where(lane < 8, g, 0.0))
        return jnp.where(row >= PAD, out, 0.0)

    return rowwise(fn, [cols(p, tr, 128, BG0 // 128), whole(alog), whole(dtb)], [out2d(rows, 128, F32, tr)],
                   steps=rows // tr, name="bg_fwd")[0]


def bg_bwd(p, alog, dtb, dbg):
    rows = p.shape[0]
    tr = _pick(rows, (384, 128))

    def fn(i, x, al, dt, g_in):
        lane = lax.broadcasted_iota(jnp.int32, (tr, 128), 1)
        row = i * tr + lax.broadcasted_iota(jnp.int32, (tr, 128), 0)
        live = row >= PAD
        is_b = jnp.logical_and(live, lane < 4)
        is_g = jnp.logical_and(live, jnp.logical_and(lane >= 4, lane < 8))
        beta = _sigmoid(x)
        ea = jnp.exp(al)
        g = -ea * _softplus(x + dt)
        dalpha = jnp.where(is_g, g_in * (-ea) * _sigmoid(x + dt), 0.0)
        dx = jnp.where(is_b, g_in * beta * (1.0 - beta), dalpha)
        dal = jnp.sum(jnp.where(is_g, g_in * g, 0.0), axis=0, keepdims=True)
        return jnp.concatenate([dx, jnp.zeros((tr, 128), F32)], axis=1), dal, jnp.sum(dalpha, axis=0, keepdims=True)

    return rowwise(fn, [cols(p, tr, 128, BG0 // 128), whole(alog), whole(dtb), cols(dbg, tr)],
                   [out2d(rows, 256, BF16, tr)], steps=rows // tr, name="bg_bwd",
                   accs=[((1, 128), F32), ((1, 128), F32)])


def dn_qkv_post(j, y):
    xs = _silu(y)
    sc = jnp.where(j == 0, DN_D ** -0.5, 1.0)
    outs = []
    for h in range(DN_H):
        xh = xs[:, h * DN_D:(h + 1) * DN_D]
        r = lax.rsqrt(jnp.sum(xh * xh, axis=1, keepdims=True) + EPS)
        outs.append(jnp.where(j < 2, xh * r * sc, xh))
    return jnp.concatenate(outs, axis=1), y


def dn_qkv_bwd(cq, dq, dk, dv):
    rows = cq.shape[0]
    tr = _pick(rows, (384, 128))

    def fn(i, c0, c1, c2, g0, g1, g2):
        pieces = []
        for kind, (cv, g) in enumerate(((c0, g0), (c1, g1), (c2, g2))):
            xs = _silu(cv)
            if kind < 2:
                sc = DN_D ** -0.5 if kind == 0 else 1.0
                ds = []
                for h in range(DN_H):
                    sl = slice(h * DN_D, (h + 1) * DN_D)
                    xh, gh = xs[:, sl], g[:, sl]
                    r = lax.rsqrt(jnp.sum(xh * xh, axis=1, keepdims=True) + EPS)
                    xn = xh * r
                    ds.append(sc * r * (gh - xn * jnp.sum(gh * xn, axis=1, keepdims=True)))
                dxs = jnp.concatenate(ds, axis=1)
            else:
                dxs = g
            pieces.append(dxs * _dsilu(cv))
        return jnp.concatenate(pieces, axis=1)

    ins = [cols(cq, tr, DN_DIM, k) for k in range(3)] + [cols(t, tr) for t in (dq, dk, dv)]
    return rowwise(fn, ins, [out2d(rows, 3 * DN_DIM, F32, tr)], steps=rows // tr, name="dn_qkv_bwd")[0]


def dn_out_fwd(o, p, nw):
    rows = o.shape[0]
    tr = _pick(rows, (384, 128))

    def fn(i, ov, z, w):
        xs, _ = _head_rms(ov, w)
        return jnp.concatenate(xs, axis=1) * jnp.concatenate([w] * DN_H, axis=1) * _silu(z)

    return rowwise(fn, [cols(o, tr), cols(p, tr, DN_DIM, 6), whole(nw)], [out2d(rows, DN_DIM, BF16, tr)],
                   steps=rows // tr, name="dn_out_fwd")[0]


def dn_out_bwd(o, p, nw, dymix):
    rows = o.shape[0]
    tr = _pick(rows, (384, 128))

    def fn(i, ov, z, w, dy):
        xs, rs = _head_rms(ov, w)
        sz = _silu(z)
        dn = dy * sz
        dos, dw = [], jnp.zeros((1, DN_D), F32)
        for h in range(DN_H):
            sl = slice(h * DN_D, (h + 1) * DN_D)
            gw = dn[:, sl] * w
            dos.append(rs[h] * (gw - xs[h] * jnp.mean(gw * xs[h], axis=1, keepdims=True)))
            dw = dw + jnp.sum(dn[:, sl] * xs[h], axis=0, keepdims=True)
        n = jnp.concatenate(xs, axis=1) * jnp.concatenate([w] * DN_H, axis=1)
        return jnp.concatenate(dos, axis=1), dy * n * _dsilu(z), dw

    return rowwise(fn, [cols(o, tr), cols(p, tr, DN_DIM, 6), whole(nw), cols(dymix, tr, DN_DIM, 1)],
                   [out2d(rows, DN_DIM, F32, tr), out2d(rows, DN_DIM, BF16, tr)], steps=rows // tr,
                   name="dn_out_bwd", accs=[((1, DN_D), F32)])


def conv_a_pre_bwd(dymix, cv, p):
    rows = cv.shape[0]
    tr = _pick(rows, (384, 128))

    def fn(i, dy, c, go):
        return dy * c, dy * go

    return rowwise(fn, [cols(dymix, tr, D_CONV, 0), cols(cv, tr), cols(p, tr, D_CONV, 1)],
                   [out2d(rows, D_CONV, BF16, tr), out2d(rows, D_CONV, F32, tr)], steps=rows // tr,
                   name="conv_a_pre_bwd")


def ffn_act_bwd(da, gc, u):
    rows = da.shape[0]
    tr = _pick(rows, (384, 128))

    def fn(i, g, c, val):
        g, c, val = g.astype(F32), c.astype(F32), val.astype(F32)
        return g * _silu(c), g * val * _dsilu(c)

    return rowwise(fn, [cols(da, tr), cols(gc, tr), cols(u, tr, D_FF, 1)],
                   [out2d(rows, D_FF, BF16, tr), out2d(rows, D_FF, F32, tr)], steps=rows // tr, name="ffn_act_bwd")


def _rows8(w):
    return jnp.pad(w.astype(F32), ((0, 8 - w.shape[0]), (0, 0)))


def _lanes(v, at):
    return jnp.pad(v.astype(F32), (at, 128 - at - v.shape[0]))[None]


def ffn_fwd(h, nw, w_up, cw8, w_down, tag):
    rows = h.shape[0]
    tr = _pick(rows, (384, 128))
    hn = rms_fwd(h, nw, name=f"ffn{tag}_norm")
    u = mm(hn, w_up, out_dtype=BF16, b_chip=True, name=f"ffn{tag}_up")
    a, gc = conv_fwd([(u, 0)], cw8, 3, rows=rows, c=D_FF, tc=1408, tr=tr, name=f"ffn{tag}_conv",
                     post=lambda j, y, val: (_silu(y) * val.astype(F32), y), extras=[(u, 2)], outs=[BF16, BF16])
    out = mm(a, w_down, add=h, name=f"ffn{tag}_down")
    return out, (hn, u, a, gc)


def ffn_bwd(h, nw, w_up, cw8, w_down, saved, dh, tag):
    hn, u, a, gc = saved
    rows = h.shape[0]
    tr = _pick(rows, (384, 128))
    da = mm(dh, w_down, tb=True, out_dtype=BF16, name=f"ffn{tag}_down_dx")
    d_w_down = mm(a, dh, ta=True, out_dtype=BF16, name=f"ffn{tag}_down_dw")
    dval, dgc = ffn_act_bwd(da, gc, u)
    dgate, d_cw = conv_bwd([(u, 0)], cw8, 3, dgc, rows=rows, c=D_FF, tc=1408, tr=tr, name=f"ffn{tag}_conv_bwd",
                           post=lambda dx: dx, outs=[BF16])
    du = jnp.concatenate([dgate, dval], axis=1)
    dhn = mm(du, w_up, tb=True, b_chip=True, name=f"ffn{tag}_up_dx")
    d_w_up = mm(hn, du, ta=True, out_dtype=BF16, out_chip=True, name=f"ffn{tag}_up_dw")
    dh_new, d_nw = rms_bwd(h, nw, dhn, dh, name=f"ffn{tag}_norm_bwd")
    return dh_new, d_nw, d_w_up, d_cw, d_w_down


def mixer_fwd(h, nw, w_in, ca8, dc8, alog, dtb, dnw, w_out):
    rows = h.shape[0]
    tr = _pick(rows, (384, 128))
    hn = rms_fwd(h, nw, name="mix_norm")
    p = mm(hn, w_in, name="mix_in")
    y_a, cv = conv_fwd([(p, 0), (p, 2)], ca8, 3, rows=rows, c=D_CONV, tc=D_CONV, tr=tr, name="conv_a",
                       pre=lambda gi, ah: gi * ah, post=lambda j, y, go: (go * y, y), extras=[(p, 1)],
                       outs=[BF16, F32])
    qkv_n, cq = conv_fwd([(p, 3)], dc8, 4, rows=rows, c=3 * DN_DIM, tc=DN_DIM, tr=tr, name="dn_conv",
                         post=dn_qkv_post, outs=[F32, F32])
    bgcol = bg_fwd(p, alog, dtb)
    bgrow = bgcol[:, :8].reshape(rows // CH, CH, 8).transpose(0, 2, 1)
    o, s_all, ti_all = dn_fwd(qkv_n, bgcol, bgrow)
    y_b = dn_out_fwd(o, p, dnw)
    ymix = jnp.concatenate([y_a, y_b], axis=1)
    out = mm(ymix, w_out, add=h, name="mix_out")
    return out, (hn, p, cv, qkv_n, cq, bgcol, bgrow, o, s_all, ti_all, ymix)


def mixer_bwd(h, nw, w_in, ca8, dc8, alog, dtb, dnw, w_out, saved, dh):
    hn, p, cv, qkv_n, cq, bgcol, bgrow, o, s_all, ti_all, ymix = saved
    rows = h.shape[0]
    tr = _pick(rows, (384, 128))
    dymix = mm(dh, w_out, tb=True, name="mix_out_dx")
    d_w_out = mm(ymix, dh, ta=True, out_dtype=BF16, name="mix_out_dw")
    do, dz, d_dnw = dn_out_bwd(o, p, dnw, dymix)
    dq, dk, dv, dbg = dn_bwd(qkv_n, bgcol, bgrow, s_all, ti_all, do)
    dbg_p, d_alog, d_dtb = bg_bwd(p, alog, dtb, dbg)
    dcq = dn_qkv_bwd(cq, dq, dk, dv)
    dqkv, d_dc = conv_bwd([(p, 3)], dc8, 4, dcq, rows=rows, c=3 * DN_DIM, tc=DN_DIM, tr=tr, name="dn_conv_bwd",
                          post=lambda dx: dx, outs=[BF16])
    dgo, dcv = conv_a_pre_bwd(dymix, cv, p)
    dgi, dah, d_ca = conv_bwd([(p, 0), (p, 2)], ca8, 3, dcv, rows=rows, c=D_CONV, tc=D_CONV, tr=tr,
                              name="conv_a_bwd", pre=lambda gi, ah: gi * ah,
                              post=lambda dm, gi, ah: (dm * ah, dm * gi), extras=[(p, 0), (p, 2)], outs=[BF16, BF16])
    dp = jnp.concatenate([dgi, dgo, dah, dqkv, dz, dbg_p], axis=1)
    dhn = mm(dp, w_in, tb=True, name="mix_in_dx")
    d_w_in = mm(hn, dp, ta=True, out_dtype=BF16, name="mix_in_dw")
    dh_new, d_nw = rms_bwd(h, nw, dhn, dh, name="mix_norm_bwd")
    return dh_new, d_nw, d_w_in, d_ca, d_dc, d_alog, d_dtb, d_dnw, d_w_out


def swa_layer_fwd(h, nw, wqkv, qw, kw, sinks, wo):
    hn = rms_fwd(h, nw, name="swa_norm")
    qkv = mm(hn, wqkv, name="swa_qkv")
    qh, kh, vh = qknorm_fwd(qkv, qw, kw)
    att = swa_fwd(qh, kh, vh, sinks)
    out = mm(att, wo, add=h, name="swa_out")
    return out, (hn, qkv, qh, kh, vh, att)


def swa_layer_bwd(h, nw, wqkv, qw, kw, sinks, wo, saved, dh):
    hn, qkv, qh, kh, vh, att = saved
    datt = mm(dh, wo, tb=True, out_dtype=BF16, name="swa_out_dx")
    d_wo = mm(att, dh, ta=True, out_dtype=BF16, name="swa_out_dw")
    dqh, dkh, dvh, dsk = swa_bwd(qh, kh, vh, sinks, datt)
    dqkv, d_qw, d_kw = qknorm_bwd(qkv, qw, kw, dqh, dkh, dvh)
    dhn = mm(dqkv, wqkv, tb=True, name="swa_qkv_dx")
    d_wqkv = mm(hn, dqkv, ta=True, out_dtype=BF16, name="swa_qkv_dw")
    dh_new, d_nw = rms_bwd(h, nw, dhn, dh, name="swa_norm_bwd")
    d_sinks = jnp.sum(dsk[:, :, 0], axis=0)
    return dh_new, d_nw, d_wqkv, d_qw, d_kw, d_sinks, d_wo


BIG = ("mix_w_in", "mix_w_out", "swa_wq", "swa_wk", "swa_wv", "swa_wo", "ffn_w_up", "ffn_w_down")


def _flat_pad(parts, rows):
    v = jnp.concatenate([t.astype(F32).reshape(-1) for t in parts])
    return jnp.pad(v, (0, rows * 1024 - v.shape[0])).reshape(rows, 1024)


def _split_flat(flat, shapes):
    v = flat.reshape(-1)
    out, o = [], 0
    for s in shapes:
        n = 1
        for d_ in s:
            n *= d_
        out.append(v[o:o + n].reshape(s))
        o += n
    return out


def local_step(x0, target0, meta_full, anw, fnw, w_in, ca8, dc8, alog, dtb, dnw, w_out, wqkv, qw, kw, sinks, wo,
               w_up, fc8, w_down):
    h0 = jnp.concatenate([jnp.zeros((PAD, D), F32), meta_full, x0], axis=0)
    h1, s_mix = mixer_fwd(h0, anw[0], w_in, ca8, dc8, alog, dtb, dnw, w_out)
    h2, s_f0 = ffn_fwd(h1, fnw[0], w_up[0], fc8[0], w_down[0], 0)
    h3, s_swa = swa_layer_fwd(h2, anw[1], wqkv, qw, kw, sinks, wo)
    h4, s_f1 = ffn_fwd(h3, fnw[1], w_up[1], fc8[1], w_down[1], 1)
    dh, loss_l = loss_grad(h4, target0)
    dh, d_fnw1, d_up1, d_fc1, d_down1 = ffn_bwd(h3, fnw[1], w_up[1], fc8[1], w_down[1], s_f1, dh, 1)
    dh, d_anw1, d_wqkv, d_qw, d_kw, d_sinks, d_wo = swa_layer_bwd(h2, anw[1], wqkv, qw, kw, sinks, wo, s_swa, dh)
    dh, d_fnw0, d_up0, d_fc0, d_down0 = ffn_bwd(h1, fnw[0], w_up[0], fc8[0], w_down[0], s_f0, dh, 0)
    dh, d_anw0, d_w_in, d_ca, d_dc, d_alog, d_dtb, d_dnw, d_w_out = mixer_bwd(
        h0, anw[0], w_in, ca8, dc8, alog, dtb, dnw, w_out, s_mix, dh)
    return (dh, loss_l, d_anw0, d_anw1, d_fnw0, d_fnw1, d_w_in, d_ca, d_dc, d_alog, d_dtb, d_dnw, d_w_out, d_wqkv,
            d_qw, d_kw, d_sinks, d_wo, d_up0, d_up1, d_fc0, d_fc1, d_down0, d_down1)


def kernel(x, meta_tokens, attn_norm_w, ffn_norm_w, mix_w_in, conv_a_w, dn_conv_w, dn_a_log, dn_dt_bias, dn_norm_w, mix_w_out, swa_wq, swa_wk, swa_wv, swa_q_norm_w, swa_k_norm_w, swa_sinks, swa_wo, ffn_w_up, ffn_conv_w, ffn_w_down, loss_target, m_meta_tokens, m_attn_norm_w, m_ffn_norm_w, m_mix_w_in, m_conv_a_w, m_dn_conv_w, m_dn_a_log, m_dn_dt_bias, m_dn_norm_w, m_mix_w_out, m_swa_wq, m_swa_wk, m_swa_wv, m_swa_q_norm_w, m_swa_k_norm_w, m_swa_sinks, m_swa_wo, m_ffn_w_up, m_ffn_conv_w, m_ffn_w_down, v_meta_tokens, v_attn_norm_w, v_ffn_norm_w, v_mix_w_in, v_conv_a_w, v_dn_conv_w, v_dn_a_log, v_dn_dt_bias, v_dn_norm_w, v_mix_w_out, v_swa_wq, v_swa_wk, v_swa_wv, v_swa_q_norm_w, v_swa_k_norm_w, v_swa_sinks, v_swa_wo, v_ffn_w_up, v_ffn_conv_w, v_ffn_w_down):
    ix, iy, ic = lax.axis_index("x"), lax.axis_index("y"), lax.axis_index("c")
    chip = 2 * ix + iy
    seq = x.shape[1]
    rows = HEAD0 + seq

    small_sharded = (conv_a_w, dn_conv_w, ffn_conv_w, meta_tokens)
    up_b, down_b = ffn_w_up.astype(BF16), ffn_w_down.astype(BF16)
    own = [mix_w_in[0].astype(BF16), mix_w_out[0].astype(BF16), swa_wq[0].astype(BF16), swa_wk[0].astype(BF16),
           swa_wv[0].astype(BF16), swa_wo[0].astype(BF16), up_b[0], up_b[1], down_b[0], down_b[1]]
    gathered, g_small = gather_weights(own, _flat_pad(small_sharded, SW_ROWS))
    g_in, g_out, g_q, g_k, g_v, g_o, g_up0, g_up1, g_dn0, g_dn1 = [
        lax.dynamic_update_slice_in_dim(g, t[None], chip, axis=0) for g, t in zip(gathered, own)]
    w_in = jnp.pad(g_in.transpose(1, 0, 2).reshape(D, IN_DIM), ((0, 0), (0, P_W - IN_DIM)))
    w_out, wo = g_out.reshape(D, D), g_o.reshape(D, D)
    wqkv = jnp.concatenate([g_q.reshape(D, D), g_k.reshape(D, 256), g_v.reshape(D, 256)], axis=1)
    w_up = [g_up0, g_up1]
    w_down = [g_dn0.reshape(D_FF, D), g_dn1.reshape(D_FF, D)]
    gs = g_small.reshape(4, -1)
    ca_full = gs[:, 0:384].reshape(4, 3, 128).transpose(1, 0, 2).reshape(3, D_CONV)
    dc_full = gs[:, 384:1920].reshape(4, 4, 384).transpose(1, 0, 2).reshape(4, 3 * DN_DIM)
    fc_full = gs[:, 1920:6144].reshape(4, 2, 3, 704).transpose(1, 2, 0, 3).reshape(2, 3, D_FF)
    meta_full = gs[:, 6144:10240].reshape(4, N_META, 256).transpose(1, 0, 2).reshape(N_META, D)
    ca8, dc8 = _rows8(ca_full), _rows8(dc_full)
    fc8 = [_rows8(fc_full[0]), _rows8(fc_full[1])]
    alog, dtb = _lanes(dn_a_log[0], 4), _lanes(dn_dt_bias[0], 4)
    dnw = dn_norm_w.astype(F32)
    qw, kw = swa_q_norm_w.astype(F32), swa_k_norm_w.astype(F32)
    sinks = swa_sinks[0].astype(F32)
    anw = [attn_norm_w[0:1], attn_norm_w[1:2]]
    fnw = [ffn_norm_w[0:1], ffn_norm_w[1:2]]

    (dh, loss_l, d_anw0, d_anw1, d_fnw0, d_fnw1, d_w_in, d_ca, d_dc, d_alog, d_dtb, d_dnw, d_w_out, d_wqkv, d_qw,
     d_kw, d_sinks, d_wo, d_up0, d_up1, d_fc0, d_fc1, d_down0, d_down1) = local_step(
        x[0], loss_target[0], meta_full, anw, fnw, w_in, ca8, dc8, alog, dtb, dnw, w_out, wqkv, qw, kw, sinks, wo,
        w_up, fc8, w_down)
    grad_x = dh[HEAD0:][None]

    small_parts = [jnp.concatenate([d_anw0, d_anw1], axis=0), jnp.concatenate([d_fnw0, d_fnw1], axis=0),
                   d_alog[0, 4:8], d_dtb[0, 4:8], d_dnw, d_qw, d_kw, d_sinks,
                   d_ca[:3], d_dc[:4], jnp.stack([d_fc0[:3], d_fc1[:3]]), dh[PAD:HEAD0], loss_l[0, 0:1]]
    small_shapes = [(2, D), (2, D), (1, 4), (1, 4), (1, DN_D), (1, SWA_D), (1, SWA_D), (1, SWA_H),
                    (1, 3, D_CONV), (1, 4, 3 * DN_DIM), (2, 3, D_FF), (N_META, D), ()]
    red = sum_slots(gather_small(_flat_pad(small_parts, SV_ROWS)))
    (g_anw, g_fnw, g_alog, g_dtb, g_dnw, g_qw, g_kw, g_sinks, g_ca_f, g_dc_f, g_fc_f, g_meta_f,
     loss) = _split_flat(red, small_shapes)
    g_ca = lax.dynamic_slice_in_dim(g_ca_f, chip * 128, 128, axis=2)
    g_dc = lax.dynamic_slice_in_dim(g_dc_f, chip * 384, 384, axis=2)
    g_fc = lax.dynamic_slice_in_dim(g_fc_f, chip * 704, 704, axis=2)
    g_meta = lax.dynamic_slice_in_dim(g_meta_f, chip * 256, 256, axis=1)

    local = [d_w_in[:, :IN_DIM].reshape(D, 4, 898).transpose(1, 0, 2), d_w_out.reshape(4, 256, D),
             d_wqkv[:, :D].reshape(4, 256, D), d_wqkv[:, D:D + 256].reshape(4, 256, 256),
             d_wqkv[:, D + 256:].reshape(4, 256, 256), d_wo.reshape(4, 256, D), d_up0, d_up1,
             d_down0.reshape(4, 704, D), d_down1.reshape(4, 704, D)]
    c_idx = jnp.reshape(ic, (1,)).astype(jnp.int32)
    chip_idx = jnp.stack([chip, ic]).astype(jnp.int32)
    g_w_in, g_w_out, g_wq, g_wk, g_wv, g_wo, g_up0, g_up1, g_dn0, g_dn1 = reduce_scatter(
        local, c_idx, chip_idx, ("w_in", "w_out", "wq", "wk", "wv", "wo", "up0", "up1", "down0", "down1"))

    grads = dict(meta_tokens=g_meta, attn_norm_w=g_anw, ffn_norm_w=g_fnw, mix_w_in=g_w_in, conv_a_w=g_ca,
                 dn_conv_w=g_dc, dn_a_log=g_alog, dn_dt_bias=g_dtb, dn_norm_w=g_dnw, mix_w_out=g_w_out,
                 swa_wq=g_wq, swa_wk=g_wk, swa_wv=g_wv, swa_q_norm_w=g_qw, swa_k_norm_w=g_kw, swa_sinks=g_sinks,
                 swa_wo=g_wo, ffn_w_up=[g_up0, g_up1], ffn_conv_w=g_fc, ffn_w_down=[g_dn0, g_dn1])
    weights = dict(meta_tokens=meta_tokens, attn_norm_w=attn_norm_w, ffn_norm_w=ffn_norm_w, mix_w_in=mix_w_in,
                   conv_a_w=conv_a_w, dn_conv_w=dn_conv_w, dn_a_log=dn_a_log, dn_dt_bias=dn_dt_bias,
                   dn_norm_w=dn_norm_w, mix_w_out=mix_w_out, swa_wq=swa_wq, swa_wk=swa_wk, swa_wv=swa_wv,
                   swa_q_norm_w=swa_q_norm_w, swa_k_norm_w=swa_k_norm_w, swa_sinks=swa_sinks, swa_wo=swa_wo,
                   ffn_w_up=ffn_w_up, ffn_conv_w=ffn_conv_w, ffn_w_down=ffn_w_down)
    m_in = dict(meta_tokens=m_meta_tokens, attn_norm_w=m_attn_norm_w, ffn_norm_w=m_ffn_norm_w, mix_w_in=m_mix_w_in,
                conv_a_w=m_conv_a_w, dn_conv_w=m_dn_conv_w, dn_a_log=m_dn_a_log, dn_dt_bias=m_dn_dt_bias,
                dn_norm_w=m_dn_norm_w, mix_w_out=m_mix_w_out, swa_wq=m_swa_wq, swa_wk=m_swa_wk, swa_wv=m_swa_wv,
                swa_q_norm_w=m_swa_q_norm_w, swa_k_norm_w=m_swa_k_norm_w, swa_sinks=m_swa_sinks, swa_wo=m_swa_wo,
                ffn_w_up=m_ffn_w_up, ffn_conv_w=m_ffn_conv_w, ffn_w_down=m_ffn_w_down)
    v_in = dict(meta_tokens=v_meta_tokens, attn_norm_w=v_attn_norm_w, ffn_norm_w=v_ffn_norm_w, mix_w_in=v_mix_w_in,
                conv_a_w=v_conv_a_w, dn_conv_w=v_dn_conv_w, dn_a_log=v_dn_a_log, dn_dt_bias=v_dn_dt_bias,
                dn_norm_w=v_dn_norm_w, mix_w_out=v_mix_w_out, swa_wq=v_swa_wq, swa_wk=v_swa_wk, swa_wv=v_swa_wv,
                swa_q_norm_w=v_swa_q_norm_w, swa_k_norm_w=v_swa_k_norm_w, swa_sinks=v_swa_sinks, swa_wo=v_swa_wo,
                ffn_w_up=v_ffn_w_up, ffn_conv_w=v_ffn_conv_w, ffn_w_down=v_ffn_w_down)
    names = list(weights)
    small = [n for n in names if n not in BIG]
    delta, new_m, new_v = {}, {}, {}
    for n in BIG:
        delta[n], new_m[n], new_v[n], grads[n] = adamw(weights[n], grads[n], m_in[n], v_in[n], name=f"adamw_{n}")
    grads = {n: grads[n].reshape(weights[n].shape) for n in names}
    shapes = [weights[n].shape for n in small]
    packed = [_flat_pad([t[n] for n in small], SW_ROWS) for t in (weights, grads, m_in, v_in)]
    for store, flat in zip((delta, new_m, new_v), adamw(*packed, name="adamw_small")):
        for n, t in zip(small, _split_flat(flat, shapes)):
            store[n] = t
    return (loss, grad_x, *[grads[n] for n in names], *[delta[n] for n in names],
            *[new_m[n] for n in names], *[new_v[n] for n in names])
```

```python
import functools

import jax
import jax.numpy as jnp
from jax import lax
from jax.experimental import pallas as pl
from jax.experimental.pallas import tpu as pltpu
from jax.experimental.pallas import tpu_sc as plsc

F32 = jnp.float32
BF16 = jnp.bfloat16
HI = lax.Precision.HIGHEST
MESH = pl.DeviceIdType.MESH

D = 1024
N_META = 16
PAD = 112
HEAD0 = PAD + N_META
D_CONV = 512
DN_H = 4
DN_D = 128
DN_DIM = 512
CH = 64
IN_DIM = 3592
P_W = 3840
BG0 = 3584
SWA_H = 16
SWA_KV = 4
SWA_D = 64
BLK = 128
D_FF = 2816
EPS = 1e-6
LR, B1, B2, AEPS, WD, STEP = 0.001, 0.9, 0.999, 1e-08, 0.01, 10
VMEM_LIMIT = 48 * 1024 * 1024
MM_VMEM_BUDGET = 34 * 1024 * 1024
R_BIG = 6144
R_HALF = R_BIG // 2
SV_ROWS = 48
SW_ROWS = 16


def _pick(n, cands):
    for c in cands:
        if n % c == 0:
            return c
    return n


def _params(sem=None):
    return pltpu.CompilerParams(dimension_semantics=sem, vmem_limit_bytes=VMEM_LIMIT)


def _dot(a, b, ca=1, cb=0, prec=None):
    return lax.dot_general(a, b, (((ca,), (cb,)), ((), ())), precision=prec,
                           preferred_element_type=F32)


def _sigmoid(x):
    return 1.0 / (1.0 + jnp.exp(-x))


def _silu(x):
    return x * _sigmoid(x)


def _dsilu(x):
    s = _sigmoid(x)
    return s * (1.0 + x * (1.0 - s))


def _softplus(x):
    return jnp.maximum(x, 0.0) + jnp.log(1.0 + jnp.exp(-jnp.abs(x)))


def mm(a, b, *, name, ta=False, tb=False, out_dtype=F32, add=None, tm=None, tn=None, tk=None,
       b_chip=False, out_chip=False):
    m, k = (a.shape[1], a.shape[0]) if ta else a.shape
    if b_chip:
        n = b.shape[1] if tb else 4 * b.shape[2]
        if tb:
            tk = b.shape[2]
        else:
            tn = b.shape[2]
    else:
        n = b.shape[0] if tb else b.shape[1]
    if out_chip:
        tn = n // 4
    tn = tn or _pick(n, (1408, 1024, 768, 512, 256, 128))
    tk = tk or (_pick(k, (1408, 704, 384, 128)) if ta else _pick(k, (1024, 1408, 768, 512, 128)))
    nk = k // tk
    if tm is None:
        isz = lambda t: jnp.dtype(t.dtype).itemsize
        osz = jnp.dtype(out_dtype).itemsize
        for tm in ((1408, 1024, 512, 384, 256, 128) if ta else (1408, 704, 512, 384, 256, 128)):
            need = 2 * (tm * tk * isz(a) + tk * tn * isz(b) + tm * tn * osz + (tm * tn * 4 if add is not None else 0))
            need += tm * tn * 4 if nk > 1 else 0
            if m % tm == 0 and need <= MM_VMEM_BUDGET:
                break
        else:
            tm = m
    dims = (((0 if ta else 1,), (1 if tb else 0,)), ((), ()))

    def body(*refs):
        if add is None:
            a_ref, b_ref, o_ref, acc_ref = refs
            add_ref = None
        else:
            a_ref, b_ref, add_ref, o_ref, acc_ref = refs
        part = lax.dot_general(a_ref[...].astype(BF16), b_ref[...].astype(BF16), dims,
                               preferred_element_type=F32)

        def finish(total):
            if add_ref is not None:
                total = total + add_ref[...]
            o_ref[...] = total.astype(out_dtype)

        if nk == 1:
            finish(part)
        else:
            kk = pl.program_id(2)

            @pl.when(kk == 0)
            def _():
                acc_ref[...] = part

            @pl.when(kk > 0)
            def _():
                acc_ref[...] += part

            @pl.when(kk == nk - 1)
            def _():
                finish(acc_ref[...])

    a_spec = pl.BlockSpec((tk, tm), lambda i, j, kk: (kk, i)) if ta else pl.BlockSpec((tm, tk), lambda i, j, kk: (i, kk))
    if b_chip and tb:
        b_spec = pl.BlockSpec((None, tn, tk), lambda i, j, kk: (kk, j, 0))
    elif b_chip:
        b_spec = pl.BlockSpec((None, tk, tn), lambda i, j, kk: (j, kk, 0))
    elif tb:
        b_spec = pl.BlockSpec((tn, tk), lambda i, j, kk: (j, kk))
    else:
        b_spec = pl.BlockSpec((tk, tn), lambda i, j, kk: (kk, j))
    o_spec = pl.BlockSpec((tm, tn), lambda i, j, kk: (i, j))
    in_specs = [a_spec, b_spec] + ([o_spec] if add is not None else [])
    args = [a, b] + ([add] if add is not None else [])
    out_spec = pl.BlockSpec((None, tm, tn), lambda i, j, kk: (j, i, 0)) if out_chip else o_spec
    return pl.pallas_call(
        body, name=name, interpret=False,
        out_shape=jax.ShapeDtypeStruct((4, m, tn) if out_chip else (m, n), out_dtype),
        grid=(m // tm, n // tn, nk), in_specs=in_specs, out_specs=out_spec,
        scratch_shapes=[pltpu.VMEM((tm, tn) if nk > 1 else (8, 128), F32)],
        compiler_params=_params(("parallel", "parallel", "arbitrary")),
    )(*args)


def cols(arr, tr, width=None, cb=0):
    width = width or arr.shape[1]
    return (arr, (tr, width), lambda i: (i, cb), "r2")


def heads(arr, tr):
    return (arr, (arr.shape[0], tr, arr.shape[2]), lambda i: (0, i, 0), "r3")


def whole(arr):
    nd = arr.ndim
    return (arr, arr.shape, lambda i: (0,) * nd, "w")


STRIP = 16


def _rows_of(ref, kind, r0, n):
    if kind == "r2":
        return ref[pl.ds(r0, n), :]
    if kind == "r3":
        return ref[:, pl.ds(r0, n), :]
    return ref[...]


def _set_rows(ref, kind, r0, n, v):
    if kind == "r2":
        ref[pl.ds(r0, n), :] = v.astype(ref.dtype)
    elif kind == "r3":
        ref[:, pl.ds(r0, n), :] = v.astype(ref.dtype)
    else:
        ref[...] = v.astype(ref.dtype)


def rowwise(fn, ins, outs, *, steps, name, accs=(), strip=None):
    n_in, n_out, n_acc = len(ins), len(outs), len(accs)
    kin = [t[3] for t in ins]
    kout = [t[4] for t in outs]
    tr = next((t[1][0] if t[3] == "r2" else t[1][1] for t in ins if t[3] != "w"), 0)

    def body(*refs):
        i = pl.program_id(0)
        in_refs, out_refs, acc_refs = refs[:n_in], refs[n_in:n_in + n_out], refs[n_in + n_out:]
        if n_acc:
            @pl.when(i == 0)
            def _():
                for r in acc_refs:
                    r[...] = jnp.zeros(r.shape, r.dtype)

        def run(r0, n):
            res = fn(i * tr + r0, *[_rows_of(r, k, r0, n) for r, k in zip(in_refs, kin)])
            if not isinstance(res, (tuple, list)):
                res = (res,)
            for r, k, v in zip(out_refs, kout, res[:n_out]):
                _set_rows(r, k, r0, n, v)
            for r, v in zip(acc_refs, res[n_out:]):
                r[...] += jnp.broadcast_to(v, r.shape).astype(r.dtype)

        if strip is None or tr <= strip:
            run(0, tr)
        else:
            def step(s, carry):
                run(pl.multiple_of(s * strip, strip), strip)
                return carry
            lax.fori_loop(0, tr // strip, step, 0)

    def zmap(nd):
        return lambda i: (0,) * nd

    in_specs = [pl.BlockSpec(t[1], t[2]) for t in ins]
    out_specs = [pl.BlockSpec(t[2], t[3]) for t in outs]
    out_specs += [pl.BlockSpec(s, zmap(len(s))) for s, _ in accs]
    out_shape = [jax.ShapeDtypeStruct(t[0], t[1]) for t in outs]
    out_shape += [jax.ShapeDtypeStruct(s, d) for s, d in accs]
    res = pl.pallas_call(
        body, name=name, interpret=False, out_shape=out_shape, grid=(steps,),
        in_specs=in_specs, out_specs=out_specs,
        compiler_params=_params(("arbitrary",)),
    )(*[t[0] for t in ins])
    return res


def out2d(rows, width, dtype, tr):
    return ((rows, width), dtype, (tr, width), lambda i: (i, 0), "r2")


def conv_fwd(xs, w8, kw, *, rows, c, tc, tr, name, post, extras=(), outs=(), pre=None):
    nx, ne, no = len(xs), len(extras), len(outs)
    nr, nc = rows // tr, c // tc
    r8 = tr // 8

    def body(*refs):
        x_refs = refs[:2 * nx]
        w_ref = refs[2 * nx]
        e_refs = refs[2 * nx + 1:2 * nx + 1 + ne]
        o_refs = refs[2 * nx + 1 + ne:2 * nx + 1 + ne + no]
        scr = refs[-1]
        j, i = pl.program_id(0), pl.program_id(1)
        halo = [x_refs[2 * q + 1][...].astype(F32) for q in range(nx)]
        scr[0:8, :] = jnp.where(i > 0, pre(*halo) if pre else halo[0], 0.0)

        def fill(s, carry):
            r0 = pl.multiple_of(s * STRIP, STRIP)
            cur = [x_refs[2 * q][pl.ds(r0, STRIP), :].astype(F32) for q in range(nx)]
            scr[pl.ds(8 + r0, STRIP), :] = pre(*cur) if pre else cur[0]
            return carry

        def comp(s, carry):
            r0 = pl.multiple_of(s * STRIP, STRIP)
            win = scr[pl.ds(r0, STRIP + 8), :]
            y = jnp.zeros((STRIP, tc), F32)
            for q in range(kw):
                sh = kw - 1 - q
                y = y + w_ref[q:q + 1, :] * win[8 - sh:8 - sh + STRIP]
            res = post(j, y, *[e[pl.ds(r0, STRIP), :] for e in e_refs])
            if not isinstance(res, (tuple, list)):
                res = (res,)
            for r, v in zip(o_refs, res):
                r[pl.ds(r0, STRIP), :] = v.astype(r.dtype)
            return carry

        lax.fori_loop(0, tr // STRIP, fill, 0)
        lax.fori_loop(0, tr // STRIP, comp, 0)

    in_specs, args = [], []
    for arr, cb0 in xs:
        in_specs.append(pl.BlockSpec((tr, tc), lambda j, i, cb0=cb0: (i, cb0 + j)))
        in_specs.append(pl.BlockSpec((8, tc), lambda j, i, cb0=cb0: (jnp.maximum(i * r8 - 1, 0), cb0 + j)))
        args += [arr, arr]
    in_specs.append(pl.BlockSpec((8, tc), lambda j, i: (0, j)))
    args.append(w8)
    for arr, cb0 in extras:
        in_specs.append(pl.BlockSpec((tr, tc), lambda j, i, cb0=cb0: (i, cb0 + j)))
        args.append(arr)
    return pl.pallas_call(
        body, name=name, interpret=False,
        out_shape=[jax.ShapeDtypeStruct((rows, c), dt) for dt in outs],
        grid=(nc, nr), in_specs=in_specs,
        out_specs=[pl.BlockSpec((tr, tc), lambda j, i: (i, j)) for _ in outs],
        scratch_shapes=[pltpu.VMEM((tr + 8, tc), F32)],
        compiler_params=_params(("parallel", "arbitrary")),
    )(*args)


def conv_bwd(xs, w8, kw, dy, *, rows, c, tc, tr, name, post, extras=(), outs=(), pre=None):
    nx, ne, no = len(xs), len(extras), len(outs)
    nr, nc = rows // tr, c // tc
    r8 = tr // 8

    def body(*refs):
        x_refs = refs[:2 * nx]
        w_ref, dy_ref, dyn_ref = refs[2 * nx:2 * nx + 3]
        e_refs = refs[2 * nx + 3:2 * nx + 3 + ne]
        o_refs = refs[2 * nx + 3 + ne:2 * nx + 3 + ne + no]
        dw_ref = refs[2 * nx + 3 + ne + no]
        xscr, gscr = refs[-2], refs[-1]
        i = pl.program_id(1)
        halo = [x_refs[2 * q + 1][...].astype(F32) for q in range(nx)]
        xscr[0:8, :] = jnp.where(i > 0, pre(*halo) if pre else halo[0], 0.0)
        gscr[tr:tr + 8, :] = jnp.where(i < nr - 1, dyn_ref[...].astype(F32), 0.0)

        def fill(s, carry):
            r0 = pl.multiple_of(s * STRIP, STRIP)
            cur = [x_refs[2 * q][pl.ds(r0, STRIP), :].astype(F32) for q in range(nx)]
            xscr[pl.ds(8 + r0, STRIP), :] = pre(*cur) if pre else cur[0]
            gscr[pl.ds(r0, STRIP), :] = dy_ref[pl.ds(r0, STRIP), :].astype(F32)
            return carry

        def comp(s, dws):
            r0 = pl.multiple_of(s * STRIP, STRIP)
            gwin = gscr[pl.ds(r0, STRIP + 8), :]
            xwin = xscr[pl.ds(r0, STRIP + 8), :]
            g = gwin[0:STRIP]
            dx = jnp.zeros((STRIP, tc), F32)
            new = []
            for q in range(kw):
                sh = kw - 1 - q
                dx = dx + w_ref[q:q + 1, :] * gwin[sh:sh + STRIP]
                part = g * xwin[8 - sh:8 - sh + STRIP]
                new.append(dws[q] + part[0:8] + part[8:16])
            res = post(dx, *[e[pl.ds(r0, STRIP), :] for e in e_refs])
            if not isinstance(res, (tuple, list)):
                res = (res,)
            for r, v in zip(o_refs, res):
                r[pl.ds(r0, STRIP), :] = v.astype(r.dtype)
            return tuple(new)

        lax.fori_loop(0, tr // STRIP, fill, 0)
        dws = lax.fori_loop(0, tr // STRIP, comp, tuple(jnp.zeros((8, tc), F32) for _ in range(kw)))

        @pl.when(i == 0)
        def _():
            dw_ref[...] = jnp.zeros((8, tc), F32)

        dw_ref[...] += jnp.concatenate([jnp.sum(t, axis=0, keepdims=True) for t in dws]
                                       + [jnp.zeros((8 - kw, tc), F32)], axis=0)

    in_specs, args = [], []
    for arr, cb0 in xs:
        in_specs.append(pl.BlockSpec((tr, tc), lambda j, i, cb0=cb0: (i, cb0 + j)))
        in_specs.append(pl.BlockSpec((8, tc), lambda j, i, cb0=cb0: (jnp.maximum(i * r8 - 1, 0), cb0 + j)))
        args += [arr, arr]
    in_specs.append(pl.BlockSpec((8, tc), lambda j, i: (0, j)))
    in_specs.append(pl.BlockSpec((tr, tc), lambda j, i: (i, j)))
    in_specs.append(pl.BlockSpec((8, tc), lambda j, i: (jnp.minimum((i + 1) * r8, nr * r8 - 1), j)))
    args += [w8, dy, dy]
    for arr, cb0 in extras:
        in_specs.append(pl.BlockSpec((tr, tc), lambda j, i, cb0=cb0: (i, cb0 + j)))
        args.append(arr)
    return pl.pallas_call(
        body, name=name, interpret=False,
        out_shape=[jax.ShapeDtypeStruct((rows, c), dt) for dt in outs] + [jax.ShapeDtypeStruct((8, c), F32)],
        grid=(nc, nr), in_specs=in_specs,
        out_specs=[pl.BlockSpec((tr, tc), lambda j, i: (i, j)) for _ in outs] + [pl.BlockSpec((8, tc), lambda j, i: (0, j))],
        scratch_shapes=[pltpu.VMEM((tr + 8, tc), F32), pltpu.VMEM((tr + 8, tc), F32)],
        compiler_params=_params(("parallel", "arbitrary")),
    )(*args)


def rms_fwd(h, w, *, name):
    rows = h.shape[0]
    tr = _pick(rows, (384, 128))

    def fn(i, x, wv):
        r = lax.rsqrt(jnp.mean(x * x, axis=1, keepdims=True) + EPS)
        return x * r * wv

    return rowwise(fn, [cols(h, tr), whole(w)], [out2d(rows, D, BF16, tr)], steps=rows // tr, name=name)[0]


def rms_bwd(h, w, dy, dres, *, name):
    rows = h.shape[0]
    tr = _pick(rows, (384, 128))

    def fn(i, x, wv, g, dr):
        r = lax.rsqrt(jnp.mean(x * x, axis=1, keepdims=True) + EPS)
        xh = x * r
        gw = g * wv
        dx = r * (gw - xh * jnp.mean(gw * xh, axis=1, keepdims=True))
        row = i + lax.broadcasted_iota(jnp.int32, (x.shape[0], 1), 0)
        return jnp.where(row >= PAD, dr + dx, 0.0), jnp.sum(g * xh, axis=0, keepdims=True)

    return rowwise(fn, [cols(h, tr), whole(w), cols(dy, tr), cols(dres, tr)], [out2d(rows, D, F32, tr)],
                   steps=rows // tr, name=name, accs=[((1, D), F32)])


def loss_grad(h, target):
    rows = h.shape[0]

    def fn(i, y, t):
        diff = jnp.where(i >= HEAD0, y - t, 0.0)
        part = jnp.sum(jnp.sum(diff * diff, axis=1, keepdims=True), axis=0, keepdims=True)
        return diff * (1.0 / D), part * (0.5 / D)

    tgt = (target, (BLK, D), lambda i: (jnp.maximum(i - 1, 0), 0), "r2")
    return rowwise(fn, [cols(h, BLK), tgt], [out2d(rows, D, F32, BLK)], steps=rows // BLK,
                   name="loss_grad", accs=[((1, 128), F32)])


def adamw(w, g, m, v, *, name):
    shape = w.shape
    gs = list(g) if isinstance(g, (list, tuple)) else [g]
    nl = len(gs)
    w2, m2, v2 = (t.reshape(-1, shape[-1]) for t in (w, m, v))
    rows, width = w2.shape
    rl = rows // nl
    tr = _pick(rl, (256, 176, 128, 64, 16, 8))
    nr = rl // tr

    def fn(i, wv, mv, vv, *gvs):
        gv = gvs[0]
        for layer in range(1, nl):
            gv = jnp.where(i >= layer * rl, gvs[layer], gv)
        mn = B1 * mv + (1.0 - B1) * gv
        vn = B2 * vv + (1.0 - B2) * gv * gv
        mh = mn / (1.0 - B1 ** STEP)
        vh = vn / (1.0 - B2 ** STEP)
        return -LR * (mh / (jnp.sqrt(vh) + AEPS) + WD * wv), mn, vn, gv

    g_ins = [(t.reshape(rl, width), (tr, width), lambda i, layer=layer: (jnp.clip(i - layer * nr, 0, nr - 1), 0), "r2")
             for layer, t in enumerate(gs)]
    res = rowwise(fn, [cols(t, tr) for t in (w2, m2, v2)] + g_ins, [out2d(rows, width, F32, tr)] * 4,
                  steps=rows // tr, name=name)
    return [r.reshape(shape) for r in res]


HB = DN_H * CH


def _split(a):
    hi = a.astype(BF16)
    return hi, (a - hi.astype(F32)).astype(BF16)


def _dot1(a, b, ca=1, cb=0):
    return _dot(a.astype(BF16), b.astype(BF16), ca, cb)


def _dot3(a, b, ca=1, cb=0):
    ah, al = _split(a)
    bh, bl = _split(b)
    return _dot(ah, bh, ca, cb) + (_dot(ah, bl, ca, cb) + _dot(al, bh, ca, cb))


def _dot01(m01, b, ca=1, cb=0):
    bh, bl = _split(b)
    m = m01.astype(BF16)
    return _dot(m, bh, ca, cb) + _dot(m, bl, ca, cb)


def _stack(x):
    return jnp.concatenate([x[:, h * DN_D:(h + 1) * DN_D] for h in range(DN_H)], axis=0)


def _unstack(x):
    return jnp.concatenate([x[h * CH:(h + 1) * CH] for h in range(DN_H)], axis=1)


def _tri_inv(a, blk, eye):
    ad = jnp.where(blk, a, 0.0)
    lo = a - ad
    a2 = _dot3(ad, ad)
    a4 = _dot3(a2, a2)
    a8 = _dot3(a4, a4)
    dgi = _dot3(_dot3(_dot3(eye - ad, eye + a2), eye + a4), eye + a8)
    n = _dot3(dgi, lo)
    return _dot3(_dot3(eye - n, eye + _dot3(n, n)), dgi)


def _dn_masks():
    row = lax.broadcasted_iota(jnp.int32, (HB, HB), 0)
    col = lax.broadcasted_iota(jnp.int32, (HB, HB), 1)
    same = (row // CH) == (col // CH)
    incl = jnp.logical_and(same, row >= col)
    strict = jnp.logical_and(same, row > col)
    upper = jnp.logical_and(same, row <= col)
    blk = (row // 16) == (col // 16)
    eye = (row == col).astype(F32)
    return incl, strict, upper, blk, eye


def _dn_chunk(q_ref, k_ref, v_ref, bc_ref, br_ref, incl, strict):
    r64 = lax.broadcasted_iota(jnp.int32, (CH, CH), 0)
    c64 = lax.broadcasted_iota(jnp.int32, (CH, CH), 1)
    bc = bc_ref[...]
    dcol = _dot01((r64 >= c64).astype(F32), bc)
    drow = _dot3(br_ref[0], (r64 <= c64).astype(F32))
    col = lambda m, l0: jnp.concatenate([m[:, l0 + h:l0 + h + 1] for h in range(DN_H)], axis=0)
    b_c = col(bc, 0)
    d_c = col(dcol, 4)
    d_r = jnp.concatenate([drow[4 + h:5 + h, :] for h in range(DN_H)], axis=1)
    d_last_h = [dcol[CH - 1:CH, 4 + h:5 + h] for h in range(DN_H)]
    d_last = jnp.concatenate([jnp.broadcast_to(t, (CH, 1)) for t in d_last_h], axis=0)
    q, k, v = _stack(q_ref[...]), _stack(k_ref[...]), _stack(v_ref[...])
    dm = jnp.where(incl, jnp.exp(jnp.where(incl, d_c - d_r, 0.0)), 0.0)
    kk = _dot1(k, k, 1, 1)
    a = jnp.where(strict, b_c * kk * dm, 0.0)
    ed = jnp.exp(d_c)
    rhs = jnp.concatenate([v * b_c, k * (b_c * ed)], axis=1)
    qk = _dot1(q, k, 1, 1) * dm
    ekd = jnp.exp(d_last - d_c)
    gl = [jnp.exp(t) for t in d_last_h]
    return q, k, v, b_c, dm, kk, a, ed, rhs, qk, ekd, gl


def dn_fwd(qkv_n, bgcol, bgrow):
    rows = qkv_n.shape[0]
    nch = rows // CH

    def body(q_ref, k_ref, v_ref, bc_ref, br_ref, o_ref, s_out, ti_out, s_scr):
        n = pl.program_id(0)

        @pl.when(n == 0)
        def _():
            s_scr[...] = jnp.zeros(s_scr.shape, F32)

        incl, strict, _, blk, eye = _dn_masks()
        q, k, v, b_c, dm, kk, a, ed, rhs, qk, ekd, gl = _dn_chunk(q_ref, k_ref, v_ref, bc_ref, br_ref, incl, strict)
        tinv = _tri_inv(a, blk, eye)
        ti_out[0] = tinv
        sol = _dot3(tinv, rhs)
        u, w = sol[:, :DN_D], sol[:, DN_D:]
        qd, kd = q * ed, k * ekd
        v_new, o_state = [], []
        for h in range(DN_H):
            rs = slice(h * CH, (h + 1) * CH)
            s = s_scr[h]
            s_out[0, h] = s
            vn = u[rs] - _dot1(w[rs], s)
            v_new.append(vn)
            o_state.append(_dot1(qd[rs], s))
            s_scr[h] = gl[h] * s + _dot1(kd[rs], vn, 0, 0)
        o = jnp.concatenate(o_state, axis=0) + _dot1(qk, jnp.concatenate(v_new, axis=0))
        o_ref[...] = _unstack(o)

    return pl.pallas_call(
        body, name="dn_fwd", interpret=False,
        out_shape=[jax.ShapeDtypeStruct((rows, DN_DIM), F32),
                   jax.ShapeDtypeStruct((nch, DN_H, DN_D, DN_D), F32),
                   jax.ShapeDtypeStruct((nch, HB, HB), F32)],
        grid=(nch,),
        in_specs=[pl.BlockSpec((CH, DN_DIM), lambda n: (n, 0)),
                  pl.BlockSpec((CH, DN_DIM), lambda n: (n, 1)),
                  pl.BlockSpec((CH, DN_DIM), lambda n: (n, 2)),
                  pl.BlockSpec((CH, 128), lambda n: (n, 0)),
                  pl.BlockSpec((1, 8, CH), lambda n: (n, 0, 0))],
        out_specs=[pl.BlockSpec((CH, DN_DIM), lambda n: (n, 0)),
                   pl.BlockSpec((1, DN_H, DN_D, DN_D), lambda n: (n, 0, 0, 0)),
                   pl.BlockSpec((1, HB, HB), lambda n: (n, 0, 0))],
        scratch_shapes=[pltpu.VMEM((DN_H, DN_D, DN_D), F32)],
        compiler_params=_params(("arbitrary",)),
    )(qkv_n, qkv_n, qkv_n, bgcol, bgrow)


def dn_bwd(qkv_n, bgcol, bgrow, s_all, ti_all, do):
    rows = qkv_n.shape[0]
    nch = rows // CH

    def body(q_ref, k_ref, v_ref, bc_ref, br_ref, s_ref, ti_ref, do_ref, dq_ref, dk_ref, dv_ref, dbg_ref, ds_scr):
        n = pl.program_id(0)

        @pl.when(n == 0)
        def _():
            ds_scr[...] = jnp.zeros(ds_scr.shape, F32)

        incl, strict, upper, _, _ = _dn_masks()
        q, k, v, b_c, dm, kk, a, ed, rhs, qk, ekd, gl = _dn_chunk(q_ref, k_ref, v_ref, bc_ref, br_ref, incl, strict)
        tinv = ti_ref[0]
        g_o = _stack(do_ref[...])
        sol = _dot3(tinv, rhs)
        u, w = sol[:, :DN_D], sol[:, DN_D:]
        qd, kd = q * ed, k * ekd
        rsum = lambda t: jnp.sum(t, axis=1, keepdims=True)
        rows_of = [slice(h * CH, (h + 1) * CH) for h in range(DN_H)]
        s_h = [s_ref[0, h] for h in range(DN_H)]
        ds_h = [ds_scr[h] for h in range(DN_H)]
        v_new = jnp.concatenate([u[rs] - _dot1(w[rs], s) for rs, s in zip(rows_of, s_h)], axis=0)
        dv_new = _dot1(qk, g_o, 0, 0) + jnp.concatenate([_dot1(kd[rs], t) for rs, t in zip(rows_of, ds_h)], axis=0)
        dqd = jnp.concatenate([_dot1(g_o[rs], s, 1, 1) for rs, s in zip(rows_of, s_h)], axis=0)
        dkd = jnp.concatenate([_dot1(v_new[rs], t, 1, 1) for rs, t in zip(rows_of, ds_h)], axis=0)
        for h, rs in enumerate(rows_of):
            ds_scr[h] = _dot1(qd[rs], g_o[rs], 0, 0) + gl[h] * ds_h[h] - _dot1(w[rs], dv_new[rs], 0, 0)
        dw = jnp.concatenate([-_dot1(dv_new[rs], s, 1, 1) for rs, s in zip(rows_of, s_h)], axis=0)
        dqk = _dot1(g_o, v_new, 1, 1)
        drhs = _dot3(tinv, jnp.concatenate([dv_new, dw], axis=1), 0, 0)
        da = jnp.where(strict, -_dot1(drhs, sol, 1, 1), 0.0)
        drhs_u, drhs_w = drhs[:, :DN_D], drhs[:, DN_D:]
        s2 = rsum(drhs_w * k)
        dbeta = rsum(drhs_u * v) + s2 * ed + rsum(da * kk * dm)
        dkk = da * b_c * dm
        dqkr = dqk * dm
        mmat = da * a + dqk * qk
        tmp = rsum(dkd * kd)
        dd = (s2 * b_c * ed + rsum(mmat) - _dot3(mmat, jnp.ones((HB, 128), F32), 0, 0)[:, :1] + rsum(dqd * qd) - tmp)
        rowi = lax.broadcasted_iota(jnp.int32, (CH, 1), 0)
        last = []
        for h, rs in enumerate(rows_of):
            dgl = jnp.sum(rsum(s_h[h] * ds_h[h]), axis=0, keepdims=True)
            dd_last = jnp.sum(tmp[rs], axis=0, keepdims=True) + dgl * gl[h]
            last.append(jnp.where(rowi == CH - 1, dd_last, 0.0))
        dd = dd + jnp.concatenate(last, axis=0)
        dq_ref[...] = _unstack(_dot1(dqkr, k) + dqd * ed)
        dk_ref[...] = _unstack(drhs_w * (b_c * ed) + _dot1(dkk, k) + _dot1(dkk, k, 0, 0) + _dot1(dqkr, q, 0, 0)
                               + dkd * ekd)
        dv_ref[...] = _unstack(drhs_u * b_c)
        dg = _dot01(upper.astype(F32), jnp.broadcast_to(dd, (HB, 128)))[:, :1]
        lane = lax.broadcasted_iota(jnp.int32, (CH, 128), 1)
        out = jnp.zeros((CH, 128), F32)
        for h, rs in enumerate(rows_of):
            out = out + jnp.where(lane == h, dbeta[rs], 0.0) + jnp.where(lane == 4 + h, dg[rs], 0.0)
        dbg_ref[...] = out

    rev = lambda n: nch - 1 - n
    return pl.pallas_call(
        body, name="dn_bwd", interpret=False,
        out_shape=[jax.ShapeDtypeStruct((rows, DN_DIM), F32)] * 3 + [jax.ShapeDtypeStruct((rows, 128), F32)],
        grid=(nch,),
        in_specs=[pl.BlockSpec((CH, DN_DIM), lambda n: (rev(n), 0)),
                  pl.BlockSpec((CH, DN_DIM), lambda n: (rev(n), 1)),
                  pl.BlockSpec((CH, DN_DIM), lambda n: (rev(n), 2)),
                  pl.BlockSpec((CH, 128), lambda n: (rev(n), 0)),
                  pl.BlockSpec((1, 8, CH), lambda n: (rev(n), 0, 0)),
                  pl.BlockSpec((1, DN_H, DN_D, DN_D), lambda n: (rev(n), 0, 0, 0)),
                  pl.BlockSpec((1, HB, HB), lambda n: (rev(n), 0, 0)),
                  pl.BlockSpec((CH, DN_DIM), lambda n: (rev(n), 0))],
        out_specs=[pl.BlockSpec((CH, DN_DIM), lambda n: (rev(n), 0))] * 3 + [pl.BlockSpec((CH, 128), lambda n: (rev(n), 0))],
        scratch_shapes=[pltpu.VMEM((DN_H, DN_D, DN_D), F32)],
        compiler_params=_params(("arbitrary",)),
    )(qkv_n, qkv_n, qkv_n, bgcol, bgrow, s_all, ti_all, do)


def _swa_valid(n):
    c3 = lax.broadcasted_iota(jnp.int32, (3 * BLK, 4 * BLK), 0)
    r = lax.broadcasted_iota(jnp.int32, (3 * BLK, 4 * BLK), 1) % BLK
    c = c3 % BLK
    lo = jnp.where(c3 < BLK, PAD, jnp.where(c3 < 2 * BLK, r + 1 + jnp.where(n >= 2, 0, BLK), 0))
    hi = jnp.where(c3 < BLK, r + jnp.where(n >= 1, BLK, 0), jnp.where(c3 < 2 * BLK, BLK, r - jnp.where(n >= 1, 0, BLK)))
    return jnp.logical_and(c >= lo, c <= hi)


def _swa_probs(q, kcat, valid, sink):
    s = jnp.where(valid, _dot(kcat, q, 1, 1), -1e30)
    m = jnp.maximum(jnp.max(s, axis=0, keepdims=True), sink)
    e = jnp.where(valid, jnp.exp(s - m), 0.0)
    es = jnp.exp(sink - m)
    inv = 1.0 / (jnp.sum(e, axis=0, keepdims=True) + es)
    return e * inv, es * inv


def _swa_group(q_ref, sk_ref, h):
    q4 = jnp.concatenate([q_ref[4 * h + g] for g in range(4)], axis=0)
    sink4 = jnp.concatenate([jnp.full((1, BLK), sk_ref[4 * h + g], F32) for g in range(4)], axis=1)
    return q4, sink4


def _swa_specs():
    q = pl.BlockSpec((SWA_H, BLK, SWA_D), lambda n: (0, n, 0))
    km = pl.BlockSpec((SWA_KV, BLK, SWA_D), lambda n: (0, 0, 0))
    kp = pl.BlockSpec((SWA_KV, BLK, SWA_D), lambda n: (0, jnp.maximum(n - 1, 0), 0))
    kc = pl.BlockSpec((SWA_KV, BLK, SWA_D), lambda n: (0, n, 0))
    return [q, km, kp, kc, km, kp, kc]


def swa_fwd(qh, kh, vh, sinks):
    rows = qh.shape[1]
    nb = rows // BLK

    def body(q_ref, km, kp, kc, vm, vp, vc, sk_ref, o_ref):
        n = pl.program_id(0)
        valid = _swa_valid(n)
        outs = []
        for h in range(SWA_KV):
            kcat = jnp.concatenate([km[h], kp[h], kc[h]], axis=0)
            vcat = jnp.concatenate([vm[h], vp[h], vc[h]], axis=0)
            q4, sink4 = _swa_group(q_ref, sk_ref, h)
            p, _ = _swa_probs(q4, kcat, valid, sink4)
            o4 = _dot(p.astype(BF16), vcat, 0, 0)
            outs += [o4[g * BLK:(g + 1) * BLK] for g in range(4)]
        o_ref[...] = jnp.concatenate(outs, axis=1).astype(BF16)

    return pl.pallas_call(
        body, name="swa_fwd", interpret=False,
        out_shape=jax.ShapeDtypeStruct((rows, SWA_H * SWA_D), BF16),
        grid=(nb,),
        in_specs=_swa_specs() + [pl.BlockSpec(memory_space=pltpu.SMEM)],
        out_specs=pl.BlockSpec((BLK, SWA_H * SWA_D), lambda n: (n, 0)),
        compiler_params=_params(("parallel",)),
    )(qh, kh, kh, kh, vh, vh, vh, sinks)


def swa_bwd(qh, kh, vh, sinks, do):
    rows = qh.shape[1]
    nb = rows // BLK

    def body(q_ref, km, kp, kc, vm, vp, vc, do_ref, sk_ref, dq_ref, dk_ref, dv_ref, dsk_ref):
        n = pl.program_id(0)

        @pl.when(n == 0)
        def _():
            dk_ref[...] = jnp.zeros(dk_ref.shape, F32)
            dv_ref[...] = jnp.zeros(dv_ref.shape, F32)

        valid = _swa_valid(n)
        g_all = do_ref[...]
        rowi = lax.broadcasted_iota(jnp.int32, (SWA_H, 128), 0)
        dsk = jnp.zeros((SWA_H, 128), F32)
        pm = pl.multiple_of(jnp.maximum(n - 1, 0) * BLK, BLK)
        pc = pl.multiple_of(n * BLK, BLK)
        for h in range(SWA_KV):
            kcat = jnp.concatenate([km[h], kp[h], kc[h]], axis=0)
            vcat = jnp.concatenate([vm[h], vp[h], vc[h]], axis=0)
            q4, sink4 = _swa_group(q_ref, sk_ref, h)
            p, ps = _swa_probs(q4, kcat, valid, sink4)
            g4 = jnp.concatenate([g_all[:, (4 * h + g) * SWA_D:(4 * h + g + 1) * SWA_D] for g in range(4)], axis=0)
            dp = _dot(vcat, g4, 1, 1)
            delta = jnp.sum(p * dp, axis=0, keepdims=True)
            ds = (p * (dp - delta)).astype(BF16)
            dq4 = _dot(ds, kcat, 0, 0)
            dkc = _dot(ds, q4)
            dvc = _dot(p.astype(BF16), g4)
            t = ps * delta
            for g in range(4):
                dq_ref[4 * h + g] = dq4[g * BLK:(g + 1) * BLK]
                part = -jnp.sum(t[:, g * BLK:(g + 1) * BLK], axis=1, keepdims=True)
                dsk = dsk + jnp.where(rowi == 4 * h + g, part, 0.0)
            lanes = slice(h * SWA_D, (h + 1) * SWA_D)
            for ref, val in ((dk_ref, dkc), (dv_ref, dvc)):
                ref[0:BLK, lanes] += val[0:BLK]
                ref[pl.ds(pm, BLK), lanes] += val[BLK:2 * BLK]
                ref[pl.ds(pc, BLK), lanes] += val[2 * BLK:]
        dsk_ref[0] = dsk

    return pl.pallas_call(
        body, name="swa_bwd", interpret=False,
        out_shape=[jax.ShapeDtypeStruct((SWA_H, rows, SWA_D), F32),
                   jax.ShapeDtypeStruct((rows, SWA_KV * SWA_D), F32),
                   jax.ShapeDtypeStruct((rows, SWA_KV * SWA_D), F32),
                   jax.ShapeDtypeStruct((nb, SWA_H, 128), F32)],
        grid=(nb,),
        in_specs=_swa_specs() + [pl.BlockSpec((BLK, SWA_H * SWA_D), lambda n: (n, 0)),
                                 pl.BlockSpec(memory_space=pltpu.SMEM)],
        out_specs=[pl.BlockSpec((SWA_H, BLK, SWA_D), lambda n: (0, n, 0)),
                   pl.BlockSpec((rows, SWA_KV * SWA_D), lambda n: (0, 0)),
                   pl.BlockSpec((rows, SWA_KV * SWA_D), lambda n: (0, 0)),
                   pl.BlockSpec((1, SWA_H, 128), lambda n: (n, 0, 0))],
        compiler_params=_params(("arbitrary",)),
    )(qh, kh, kh, kh, vh, vh, vh, do, sinks)


def qknorm_fwd(qkv, qw, kw):
    rows = qkv.shape[0]
    tr = _pick(rows, (384, 128))
    scale = SWA_D ** -0.5

    def fn(i, x, qwv, kwv):
        def normed(j, wv, sc):
            xs = x[:, j * SWA_D:(j + 1) * SWA_D]
            r = lax.rsqrt(jnp.mean(xs * xs, axis=1, keepdims=True) + EPS)
            return (xs * r * wv * sc)[None]
        qo = jnp.concatenate([normed(j, qwv, scale) for j in range(SWA_H)], axis=0)
        ko = jnp.concatenate([normed(SWA_H + j, kwv, 1.0) for j in range(SWA_KV)], axis=0)
        vo = jnp.concatenate([x[:, (SWA_H + SWA_KV + j) * SWA_D:(SWA_H + SWA_KV + j + 1) * SWA_D][None]
                              for j in range(SWA_KV)], axis=0)
        return qo, ko, vo

    hm = lambda nh: ((nh, rows, SWA_D), BF16, (nh, tr, SWA_D), lambda i: (0, i, 0), "r3")
    return rowwise(fn, [cols(qkv, tr), whole(qw), whole(kw)], [hm(SWA_H), hm(SWA_KV), hm(SWA_KV)],
                   steps=rows // tr, name="qknorm_fwd")


def qknorm_bwd(qkv, qw, kw, dqh, dkh, dvh):
    rows = qkv.shape[0]
    tr = _pick(rows, (384, 128))
    scale = SWA_D ** -0.5

    def fn(i, x, qwv, kwv, dq, dk, dv):
        pieces = []
        dws = [jnp.zeros((1, SWA_D), F32), jnp.zeros((1, SWA_D), F32)]

        def one(j, dy, wv, sc, which):
            xs = x[:, j * SWA_D:(j + 1) * SWA_D]
            r = lax.rsqrt(jnp.mean(xs * xs, axis=1, keepdims=True) + EPS)
            xh = xs * r
            gw = dy * wv * sc
            pieces.append(r * (gw - xh * jnp.mean(gw * xh, axis=1, keepdims=True)))
            dws[which] = dws[which] + jnp.sum(dy * sc * xh, axis=0, keepdims=True)

        for j in range(SWA_H):
            one(j, dq[j], qwv, scale, 0)
        for j in range(SWA_KV):
            one(SWA_H + j, dk[:, j * SWA_D:(j + 1) * SWA_D], kwv, 1.0, 1)
        pieces.append(dv)
        return jnp.concatenate(pieces, axis=1), dws[0], dws[1]

    return rowwise(fn, [cols(qkv, tr), whole(qw), whole(kw), heads(dqh, tr), cols(dkh, tr), cols(dvh, tr)],
                   [out2d(rows, 1536, BF16, tr)], steps=rows // tr, name="qknorm_bwd",
                   accs=[((1, SWA_D), F32), ((1, SWA_D), F32)])


def _place():
    return lax.axis_index("x"), lax.axis_index("y"), lax.axis_index("c")


ANY = pl.BlockSpec(memory_space=pl.ANY)


def _rcopy(ssem, rsem, k, src, dst, to):
    return pltpu.make_async_remote_copy(src_ref=src, dst_ref=dst, send_sem=ssem.at[k], recv_sem=rsem.at[k],
                                        device_id=to, device_id_type=MESH)


def gather_weights(shards, small):
    n = len(shards)
    halves = [t.shape[0] // 2 for t in shards]

    def body(*refs):
        s_refs, small_ref = refs[:n], refs[n]
        o_refs, osmall = refs[n + 1:2 * n + 1], refs[2 * n + 1]
        ssem, rsem, lsem = refs[2 * n + 2:]
        x, y, c = _place()
        me = 2 * x + y
        chips = [(1 - x, y), (x, 1 - y), (1 - x, 1 - y)]

        def half(k, s, hh):
            return o_refs[k].at[s, pl.ds(hh * halves[k], halves[k]), :]

        loc = pltpu.make_async_copy(small_ref, osmall.at[me], lsem)
        loc.start()
        sends = []
        for k in range(n):
            for j, (px, py) in enumerate(chips):
                sends.append(_rcopy(ssem, rsem, 6 * k + j, s_refs[k].at[pl.ds(c * halves[k], halves[k]), :],
                                    half(k, me, c), (px, py, c)))
        for j, (px, py) in enumerate(chips):
            sends.append(_rcopy(ssem, rsem, 6 * n + j, small_ref, osmall.at[me], (px, py, c)))
        for cp in sends:
            cp.start()
        for k in range(n):
            for j, (px, py) in enumerate(chips):
                s = 2 * px + py
                _rcopy(ssem, rsem, 6 * k + j, half(k, s, c), half(k, s, c), (x, y, c)).wait_recv()
                fwd = _rcopy(ssem, rsem, 6 * k + 3 + j, half(k, s, c), half(k, s, c), (x, y, 1 - c))
                fwd.start()
                sends.append(fwd)
        for k in range(n):
            for j, (px, py) in enumerate(chips):
                s = 2 * px + py
                _rcopy(ssem, rsem, 6 * k + 3 + j, half(k, s, 1 - c), half(k, s, 1 - c), (x, y, c)).wait_recv()
        for j, (px, py) in enumerate(chips):
            s = 2 * px + py
            _rcopy(ssem, rsem, 6 * n + j, osmall.at[s], osmall.at[s], (x, y, c)).wait_recv()
        for cp in sends:
            cp.wait_send()
        loc.wait()

    res = pl.pallas_call(
        body, name="gather_weights", interpret=False,
        out_shape=[jax.ShapeDtypeStruct((4,) + t.shape, t.dtype) for t in shards]
        + [jax.ShapeDtypeStruct((4, SW_ROWS, 1024), F32)],
        in_specs=[ANY] * (n + 1), out_specs=[ANY] * (n + 1),
        scratch_shapes=[pltpu.SemaphoreType.DMA((6 * n + 3,)), pltpu.SemaphoreType.DMA((6 * n + 3,)),
                        pltpu.SemaphoreType.DMA],
    )(*shards, small)
    return res[:n], res[n]


def _handshake(peers):
    barrier = pltpu.get_barrier_semaphore()
    for peer in peers:
        pl.semaphore_signal(barrier, inc=1, device_id=peer, device_id_type=MESH)
    pl.semaphore_wait(barrier, len(peers))


def gather_weights_beside(shards):
    n = len(shards)
    halves = [t.shape[0] // 2 for t in shards]

    def body(*refs):
        s_refs, o_refs, ssem, rsem = refs[:n], refs[n:2 * n], refs[2 * n], refs[2 * n + 1]
        x, y, c = _place()
        me = 2 * x + y
        chips = [(1 - x, y), (x, 1 - y), (1 - x, 1 - y)]
        _handshake([(px, py, c) for px, py in chips] + [(x, y, 1 - c)])

        def half(k, s, hh):
            return o_refs[k].at[s, pl.ds(hh * halves[k], halves[k]), :]

        sends = []
        for k in range(n):
            for j, (px, py) in enumerate(chips):
                sends.append(_rcopy(ssem, rsem, 6 * k + j, s_refs[k].at[pl.ds(c * halves[k], halves[k]), :],
                                    half(k, me, c), (px, py, c)))
        for cp in sends:
            cp.start()
        for k in range(n):
            for j, (px, py) in enumerate(chips):
                s = 2 * px + py
                _rcopy(ssem, rsem, 6 * k + j, half(k, s, c), half(k, s, c), (x, y, c)).wait_recv()
                fwd = _rcopy(ssem, rsem, 6 * k + 3 + j, half(k, s, c), half(k, s, c), (x, y, 1 - c))
                fwd.start()
                sends.append(fwd)
        for k in range(n):
            for j, (px, py) in enumerate(chips):
                s = 2 * px + py
                _rcopy(ssem, rsem, 6 * k + 3 + j, half(k, s, 1 - c), half(k, s, 1 - c), (x, y, c)).wait_recv()
        for cp in sends:
            cp.wait_send()

    return pl.kernel(
        body, name="gather_weights_beside",
        out_type=[jax.ShapeDtypeStruct((4,) + t.shape, t.dtype) for t in shards],
        mesh=plsc.ScalarSubcoreMesh(axis_name="sequencer", num_cores=1),
        scratch_types=[pltpu.SemaphoreType.DMA((6 * n,)), pltpu.SemaphoreType.DMA((6 * n,))],
        compiler_params=pltpu.CompilerParams(collective_id=1),
    )(*shards)


def swap_halves(gs):
    n = len(gs)

    def body(*refs):
        g_refs, o_refs, ssem, rsem = refs[:n], refs[n:2 * n], refs[2 * n], refs[2 * n + 1]
        x, y, c = _place()
        cps = []
        for k in range(n):
            hk = g_refs[k].shape[1] // 2
            cps.append(_rcopy(ssem, rsem, k, g_refs[k].at[:, pl.ds((1 - c) * hk, hk), :], o_refs[k], (x, y, 1 - c)))
        for cp in cps:
            cp.start()
        for cp in cps:
            cp.wait()

    return pl.pallas_call(
        body, name="swap_halves", interpret=False,
        out_shape=[jax.ShapeDtypeStruct((4, t.shape[1] // 2, t.shape[2]), t.dtype) for t in gs],
        in_specs=[ANY] * n, out_specs=[ANY] * n,
        scratch_shapes=[pltpu.SemaphoreType.DMA((n,)), pltpu.SemaphoreType.DMA((n,))],
    )(*gs)


def _sum_rows(hk):
    return _pick(hk, (512, 352, 256, 128))


def pair_sum(g, other, c_idx, *, name):
    _, hk, width = other.shape
    tr = _sum_rows(hk)
    nbk = hk // tr

    def body(c_ref, g_ref, o_ref, out_ref):
        out_ref[...] = (g_ref[...].astype(F32) + o_ref[...].astype(F32)).astype(BF16)

    return pl.pallas_call(
        body, name=name, interpret=False,
        out_shape=jax.ShapeDtypeStruct((4, hk, width), BF16),
        grid_spec=pltpu.PrefetchScalarGridSpec(
            num_scalar_prefetch=1, grid=(4, nbk),
            in_specs=[pl.BlockSpec((1, tr, width), lambda s, i, c_ref: (s, c_ref[0] * nbk + i, 0)),
                      pl.BlockSpec((1, tr, width), lambda s, i, c_ref: (s, i, 0))],
            out_specs=pl.BlockSpec((1, tr, width), lambda s, i, c_ref: (s, i, 0))),
        compiler_params=_params(("parallel", "parallel")),
    )(c_idx, g, other)


def scatter_chips(ps):
    n = len(ps)

    def body(*refs):
        p_refs, o_refs, ssem, rsem = refs[:n], refs[n:2 * n], refs[2 * n], refs[2 * n + 1]
        x, y, c = _place()
        chips = [(1 - x, y), (x, 1 - y), (1 - x, 1 - y)]
        cps = [_rcopy(ssem, rsem, 3 * k + j, p_refs[k].at[2 * px + py], o_refs[k].at[j], (px, py, c))
               for k in range(n) for j, (px, py) in enumerate(chips)]
        for cp in cps:
            cp.start()
        for cp in cps:
            cp.wait()

    return pl.pallas_call(
        body, name="scatter_chips", interpret=False,
        out_shape=[jax.ShapeDtypeStruct((3,) + t.shape[1:], t.dtype) for t in ps],
        in_specs=[ANY] * n, out_specs=[ANY] * n,
        scratch_shapes=[pltpu.SemaphoreType.DMA((3 * n,)), pltpu.SemaphoreType.DMA((3 * n,))],
    )(*ps)


def chip_sum(p, got, idx, *, name):
    _, hk, width = got.shape
    tr = _sum_rows(hk)
    nbk = hk // tr

    def body(idx_ref, p_ref, g_ref, out_ref):
        acc = p_ref[0].astype(F32)
        for j in range(3):
            acc = acc + g_ref[j].astype(F32)
        out_ref[0] = acc

    return pl.pallas_call(
        body, name=name, interpret=False,
        out_shape=jax.ShapeDtypeStruct((2, hk, width), F32),
        grid_spec=pltpu.PrefetchScalarGridSpec(
            num_scalar_prefetch=1, grid=(nbk,),
            in_specs=[pl.BlockSpec((1, tr, width), lambda i, idx_ref: (idx_ref[0], i, 0)),
                      pl.BlockSpec((3, tr, width), lambda i, idx_ref: (0, i, 0))],
            out_specs=pl.BlockSpec((1, tr, width), lambda i, idx_ref: (idx_ref[1], i, 0))),
        compiler_params=_params(("parallel",)),
    )(idx, p, got)


def join_halves(qs):
    n = len(qs)

    def body(*refs):
        q_refs, o_refs, ssem, rsem = refs[:n], refs[n:2 * n], refs[2 * n], refs[2 * n + 1]
        x, y, c = _place()
        cps = [_rcopy(ssem, rsem, k, q_refs[k].at[c], o_refs[k].at[c], (x, y, 1 - c)) for k in range(n)]
        for cp in cps:
            cp.start()
        for k in range(n):
            _rcopy(ssem, rsem, k, q_refs[k].at[c], o_refs[k].at[1 - c], (x, y, 1 - c)).wait_recv()
        for cp in cps:
            cp.wait_send()

    return pl.pallas_call(
        body, name="join_halves", interpret=False,
        out_shape=[jax.ShapeDtypeStruct(t.shape, t.dtype) for t in qs],
        in_specs=[ANY] * n, out_specs=[ANY] * n, input_output_aliases={k: k for k in range(n)},
        scratch_shapes=[pltpu.SemaphoreType.DMA((n,)), pltpu.SemaphoreType.DMA((n,))],
    )(*qs)


def reduce_scatter(gs, c_idx, idx, names):
    others = swap_halves(gs)
    pairs = [pair_sum(g, o, c_idx, name=f"pair_sum_{nm}") for g, o, nm in zip(gs, others, names)]
    gots = scatter_chips(pairs)
    mine = [chip_sum(p, g, idx, name=f"chip_sum_{nm}") for p, g, nm in zip(pairs, gots, names)]
    return [q.reshape(2 * q.shape[1], q.shape[2]) for q in join_halves(mine)]


def gather_small(v):
    def body(v_ref, o_ref, ssem, rsem, lsem):
        x, y, c = _place()
        loc = pltpu.make_async_copy(v_ref, o_ref.at[4 * x + 2 * y + c], lsem)
        loc.start()
        cps = []
        for k in range(1, 8):
            fx, fy, fc = (k >> 2) & 1, (k >> 1) & 1, k & 1
            px = 1 - x if fx else x
            py = 1 - y if fy else y
            pc = 1 - c if fc else c
            cps.append((pltpu.make_async_remote_copy(
                src_ref=v_ref, dst_ref=o_ref.at[4 * x + 2 * y + c], send_sem=ssem.at[k - 1], recv_sem=rsem.at[k - 1],
                device_id=(px, py, pc), device_id_type=MESH), 4 * px + 2 * py + pc))
        for cp, _ in cps:
            cp.start()
        for k, (cp, peer) in enumerate(cps):
            pltpu.make_async_remote_copy(
                src_ref=v_ref, dst_ref=o_ref.at[peer], send_sem=ssem.at[k], recv_sem=rsem.at[k],
                device_id=(x, y, c), device_id_type=MESH).wait_recv()
        for cp, _ in cps:
            cp.wait_send()
        loc.wait()

    return pl.pallas_call(
        body, name="gather_small", interpret=False,
        out_shape=jax.ShapeDtypeStruct((8, SV_ROWS, 1024), F32),
        in_specs=[ANY], out_specs=ANY,
        scratch_shapes=[pltpu.SemaphoreType.DMA((7,)), pltpu.SemaphoreType.DMA((7,)), pltpu.SemaphoreType.DMA],
    )(v)


def sum_slots(a):
    def fn(i, t):
        acc = t[0]
        for k in range(1, 8):
            acc = acc + t[k]
        return acc

    return rowwise(fn, [whole(a)], [((SV_ROWS, 1024), F32, (SV_ROWS, 1024), lambda i: (0, 0), "w")], steps=1,
                   name="sum_slots")[0]


def _head_rms(x, nw):
    xs, rs = [], []
    for h in range(DN_H):
        xh = x[:, h * DN_D:(h + 1) * DN_D]
        r = lax.rsqrt(jnp.mean(xh * xh, axis=1, keepdims=True) + EPS)
        xs.append(xh * r)
        rs.append(r)
    return xs, rs


def bg_fwd(p, alog, dtb):
    rows = p.shape[0]
    tr = _pick(rows, (384, 128))

    def fn(i, x, al, dt):
        lane = lax.broadcasted_iota(jnp.int32, x.shape, 1)
        row = i + lax.broadcasted_iota(jnp.int32, x.shape, 0)
        g = -jnp.exp(al) * _softplus(x + dt)
        out = jnp.where(lane < 4, _sigmoid(x), jnp.where(lane < 8, g, 0.0))
        return jnp.where(row >= PAD, out, 0.0)

    return rowwise(fn, [cols(p, tr, 128, BG0 // 128), whole(alog), whole(dtb)], [out2d(rows, 128, F32, tr)],
                   steps=rows // tr, name="bg_fwd")[0]


def bg_bwd(p, alog, dtb, dbg):
    rows = p.shape[0]
    tr = _pick(rows, (384, 128))

    def fn(i, x, al, dt, g_in):
        lane = lax.broadcasted_iota(jnp.int32, x.shape, 1)
        row = i + lax.broadcasted_iota(jnp.int32, x.shape, 0)
        live = row >= PAD
        is_b = jnp.logical_and(live, lane < 4)
        is_g = jnp.logical_and(live, jnp.logical_and(lane >= 4, lane < 8))
        beta = _sigmoid(x)
        ea = jnp.exp(al)
        g = -ea * _softplus(x + dt)
        dalpha = jnp.where(is_g, g_in * (-ea) * _sigmoid(x + dt), 0.0)
        dx = jnp.where(is_b, g_in * beta * (1.0 - beta), dalpha)
        dal = jnp.sum(jnp.where(is_g, g_in * g, 0.0), axis=0, keepdims=True)
        return jnp.concatenate([dx, jnp.zeros(x.shape, F32)], axis=1), dal, jnp.sum(dalpha, axis=0, keepdims=True)

    return rowwise(fn, [cols(p, tr, 128, BG0 // 128), whole(alog), whole(dtb), cols(dbg, tr)],
                   [out2d(rows, 256, BF16, tr)], steps=rows // tr, name="bg_bwd",
                   accs=[((1, 128), F32), ((1, 128), F32)])


def dn_qkv_post(j, y):
    xs = _silu(y)
    sc = jnp.where(j == 0, DN_D ** -0.5, 1.0)
    outs = []
    for h in range(DN_H):
        xh = xs[:, h * DN_D:(h + 1) * DN_D]
        r = lax.rsqrt(jnp.sum(xh * xh, axis=1, keepdims=True) + EPS)
        outs.append(jnp.where(j < 2, xh * r * sc, xh))
    return jnp.concatenate(outs, axis=1), y


def dn_qkv_bwd(cq, dq, dk, dv):
    rows = cq.shape[0]
    tr = _pick(rows, (384, 128))

    def fn(i, c0, c1, c2, g0, g1, g2):
        pieces = []
        for kind, (cv, g) in enumerate(((c0, g0), (c1, g1), (c2, g2))):
            xs = _silu(cv)
            if kind < 2:
                sc = DN_D ** -0.5 if kind == 0 else 1.0
                ds = []
                for h in range(DN_H):
                    sl = slice(h * DN_D, (h + 1) * DN_D)
                    xh, gh = xs[:, sl], g[:, sl]
                    r = lax.rsqrt(jnp.sum(xh * xh, axis=1, keepdims=True) + EPS)
                    xn = xh * r
                    ds.append(sc * r * (gh - xn * jnp.sum(gh * xn, axis=1, keepdims=True)))
                dxs = jnp.concatenate(ds, axis=1)
            else:
                dxs = g
            pieces.append(dxs * _dsilu(cv))
        return jnp.concatenate(pieces, axis=1)

    ins = [cols(cq, tr, DN_DIM, k) for k in range(3)] + [cols(t, tr) for t in (dq, dk, dv)]
    return rowwise(fn, ins, [out2d(rows, 3 * DN_DIM, F32, tr)], steps=rows // tr, name="dn_qkv_bwd")[0]


def dn_out_fwd(o, p, nw):
    rows = o.shape[0]
    tr = _pick(rows, (384, 128))

    def fn(i, ov, z, w):
        xs, _ = _head_rms(ov, w)
        return jnp.concatenate(xs, axis=1) * jnp.concatenate([w] * DN_H, axis=1) * _silu(z)

    return rowwise(fn, [cols(o, tr), cols(p, tr, DN_DIM, 6), whole(nw)], [out2d(rows, DN_DIM, BF16, tr)],
                   steps=rows // tr, name="dn_out_fwd")[0]


def dn_out_bwd(o, p, nw, dymix):
    rows = o.shape[0]
    tr = _pick(rows, (384, 128))

    def fn(i, ov, z, w, dy):
        xs, rs = _head_rms(ov, w)
        sz = _silu(z)
        dn = dy * sz
        dos, dw = [], jnp.zeros((1, DN_D), F32)
        for h in range(DN_H):
            sl = slice(h * DN_D, (h + 1) * DN_D)
            gw = dn[:, sl] * w
            dos.append(rs[h] * (gw - xs[h] * jnp.mean(gw * xs[h], axis=1, keepdims=True)))
            dw = dw + jnp.sum(dn[:, sl] * xs[h], axis=0, keepdims=True)
        n = jnp.concatenate(xs, axis=1) * jnp.concatenate([w] * DN_H, axis=1)
        return jnp.concatenate(dos, axis=1), dy * n * _dsilu(z), dw

    return rowwise(fn, [cols(o, tr), cols(p, tr, DN_DIM, 6), whole(nw), cols(dymix, tr, DN_DIM, 1)],
                   [out2d(rows, DN_DIM, F32, tr), out2d(rows, DN_DIM, BF16, tr)], steps=rows // tr,
                   name="dn_out_bwd", accs=[((1, DN_D), F32)])


def conv_a_pre_bwd(dymix, cv, p):
    rows = cv.shape[0]
    tr = _pick(rows, (384, 128))

    def fn(i, dy, c, go):
        return dy * c, dy * go

    return rowwise(fn, [cols(dymix, tr, D_CONV, 0), cols(cv, tr), cols(p, tr, D_CONV, 1)],
                   [out2d(rows, D_CONV, BF16, tr), out2d(rows, D_CONV, F32, tr)], steps=rows // tr,
                   name="conv_a_pre_bwd")


def ffn_act_bwd(da, gc, u):
    rows = da.shape[0]
    tr = _pick(rows, (384, 128))

    def fn(i, g, c, val):
        g, c, val = g.astype(F32), c.astype(F32), val.astype(F32)
        return g * _silu(c), g * val * _dsilu(c)

    return rowwise(fn, [cols(da, tr), cols(gc, tr), cols(u, tr, D_FF, 1)],
                   [out2d(rows, D_FF, BF16, tr), out2d(rows, D_FF, F32, tr)], steps=rows // tr, name="ffn_act_bwd")


def _rows8(w):
    return jnp.pad(w.astype(F32), ((0, 8 - w.shape[0]), (0, 0)))


def _lanes(v, at):
    return jnp.pad(v.astype(F32), (at, 128 - at - v.shape[0]))[None]


def ffn_fwd(h, nw, w_up, cw8, w_down, tag):
    rows = h.shape[0]
    tr = _pick(rows, (384, 128))
    hn = rms_fwd(h, nw, name=f"ffn{tag}_norm")
    u = mm(hn, w_up, out_dtype=BF16, b_chip=True, name=f"ffn{tag}_up")
    a, gc = conv_fwd([(u, 0)], cw8, 3, rows=rows, c=D_FF, tc=1408, tr=tr, name=f"ffn{tag}_conv",
                     post=lambda j, y, val: (_silu(y) * val.astype(F32), y), extras=[(u, 2)], outs=[BF16, BF16])
    out = mm(a, w_down, add=h, name=f"ffn{tag}_down")
    return out, (hn, u, a, gc)


def ffn_bwd(h, nw, w_up, cw8, w_down, saved, dh, tag):
    hn, u, a, gc = saved
    rows = h.shape[0]
    tr = _pick(rows, (384, 128))
    da = mm(dh, w_down, tb=True, out_dtype=BF16, name=f"ffn{tag}_down_dx")
    d_w_down = mm(a, dh, ta=True, out_dtype=BF16, name=f"ffn{tag}_down_dw")
    dval, dgc = ffn_act_bwd(da, gc, u)
    dgate, d_cw = conv_bwd([(u, 0)], cw8, 3, dgc, rows=rows, c=D_FF, tc=1408, tr=tr, name=f"ffn{tag}_conv_bwd",
                           post=lambda dx: dx, outs=[BF16])
    du = jnp.concatenate([dgate, dval], axis=1)
    dhn = mm(du, w_up, tb=True, b_chip=True, name=f"ffn{tag}_up_dx")
    d_w_up = mm(hn, du, ta=True, out_dtype=BF16, out_chip=True, name=f"ffn{tag}_up_dw")
    dh_new, d_nw = rms_bwd(h, nw, dhn, dh, name=f"ffn{tag}_norm_bwd")
    return dh_new, d_nw, d_w_up, d_cw, d_w_down


def mixer_fwd(h, nw, w_in, ca8, dc8, alog, dtb, dnw, w_out):
    rows = h.shape[0]
    tr = _pick(rows, (384, 128))
    hn = rms_fwd(h, nw, name="mix_norm")
    p = mm(hn, w_in, name="mix_in")
    y_a, cv = conv_fwd([(p, 0), (p, 2)], ca8, 3, rows=rows, c=D_CONV, tc=D_CONV, tr=tr, name="conv_a",
                       pre=lambda gi, ah: gi * ah, post=lambda j, y, go: (go * y, y), extras=[(p, 1)],
                       outs=[BF16, F32])
    qkv_n, cq = conv_fwd([(p, 3)], dc8, 4, rows=rows, c=3 * DN_DIM, tc=DN_DIM, tr=tr, name="dn_conv",
                         post=dn_qkv_post, outs=[F32, F32])
    bgcol = bg_fwd(p, alog, dtb)
    bgrow = bgcol[:, :8].reshape(rows // CH, CH, 8).transpose(0, 2, 1)
    o, s_all, ti_all = dn_fwd(qkv_n, bgcol, bgrow)
    y_b = dn_out_fwd(o, p, dnw)
    ymix = jnp.concatenate([y_a, y_b], axis=1)
    out = mm(ymix, w_out, add=h, name="mix_out")
    return out, (hn, p, cv, qkv_n, cq, bgcol, bgrow, o, s_all, ti_all, ymix)


def mixer_bwd(h, nw, w_in, ca8, dc8, alog, dtb, dnw, w_out, saved, dh):
    hn, p, cv, qkv_n, cq, bgcol, bgrow, o, s_all, ti_all, ymix = saved
    rows = h.shape[0]
    tr = _pick(rows, (384, 128))
    dymix = mm(dh, w_out, tb=True, name="mix_out_dx")
    d_w_out = mm(ymix, dh, ta=True, out_dtype=BF16, name="mix_out_dw")
    do, dz, d_dnw = dn_out_bwd(o, p, dnw, dymix)
    dq, dk, dv, dbg = dn_bwd(qkv_n, bgcol, bgrow, s_all, ti_all, do)
    dbg_p, d_alog, d_dtb = bg_bwd(p, alog, dtb, dbg)
    dcq = dn_qkv_bwd(cq, dq, dk, dv)
    dqkv, d_dc = conv_bwd([(p, 3)], dc8, 4, dcq, rows=rows, c=3 * DN_DIM, tc=DN_DIM, tr=tr, name="dn_conv_bwd",
                          post=lambda dx: dx, outs=[BF16])
    dgo, dcv = conv_a_pre_bwd(dymix, cv, p)
    dgi, dah, d_ca = conv_bwd([(p, 0), (p, 2)], ca8, 3, dcv, rows=rows, c=D_CONV, tc=D_CONV, tr=tr,
                              name="conv_a_bwd", pre=lambda gi, ah: gi * ah,
                              post=lambda dm, gi, ah: (dm * ah, dm * gi), extras=[(p, 0), (p, 2)], outs=[BF16, BF16])
    dp = jnp.concatenate([dgi, dgo, dah, dqkv, dz, dbg_p], axis=1)
    dhn = mm(dp, w_in, tb=True, name="mix_in_dx")
    d_w_in = mm(hn, dp, ta=True, out_dtype=BF16, name="mix_in_dw")
    dh_new, d_nw = rms_bwd(h, nw, dhn, dh, name="mix_norm_bwd")
    return dh_new, d_nw, d_w_in, d_ca, d_dc, d_alog, d_dtb, d_dnw, d_w_out


def swa_layer_fwd(h, nw, wqkv, qw, kw, sinks, wo):
    hn = rms_fwd(h, nw, name="swa_norm")
    qkv = mm(hn, wqkv, name="swa_qkv")
    qh, kh, vh = qknorm_fwd(qkv, qw, kw)
    att = swa_fwd(qh, kh, vh, sinks)
    out = mm(att, wo, add=h, name="swa_out")
    return out, (hn, qkv, qh, kh, vh, att)


def swa_layer_bwd(h, nw, wqkv, qw, kw, sinks, wo, saved, dh):
    hn, qkv, qh, kh, vh, att = saved
    datt = mm(dh, wo, tb=True, out_dtype=BF16, name="swa_out_dx")
    d_wo = mm(att, dh, ta=True, out_dtype=BF16, name="swa_out_dw")
    dqh, dkh, dvh, dsk = swa_bwd(qh, kh, vh, sinks, datt)
    dqkv, d_qw, d_kw = qknorm_bwd(qkv, qw, kw, dqh, dkh, dvh)
    dhn = mm(dqkv, wqkv, tb=True, name="swa_qkv_dx")
    d_wqkv = mm(hn, dqkv, ta=True, out_dtype=BF16, name="swa_qkv_dw")
    dh_new, d_nw = rms_bwd(h, nw, dhn, dh, name="swa_norm_bwd")
    d_sinks = jnp.sum(dsk[:, :, 0], axis=0)
    return dh_new, d_nw, d_wqkv, d_qw, d_kw, d_sinks, d_wo


BIG = ("mix_w_in", "mix_w_out", "swa_wq", "swa_wk", "swa_wv", "swa_wo", "ffn_w_up", "ffn_w_down")


def _flat_pad(parts, rows):
    v = jnp.concatenate([t.astype(F32).reshape(-1) for t in parts])
    return jnp.pad(v, (0, rows * 1024 - v.shape[0])).reshape(rows, 1024)


def _split_flat(flat, shapes):
    v = flat.reshape(-1)
    out, o = [], 0
    for s in shapes:
        n = 1
        for d_ in s:
            n *= d_
        out.append(v[o:o + n].reshape(s))
        o += n
    return out


def local_step(x0, target0, meta_full, anw, fnw, w_in, ca8, dc8, alog, dtb, dnw, w_out, wqkv, qw, kw, sinks, wo,
               w_up, fc8, w_down):
    h0 = jnp.concatenate([jnp.zeros((PAD, D), F32), meta_full, x0], axis=0)
    h1, s_mix = mixer_fwd(h0, anw[0], w_in, ca8, dc8, alog, dtb, dnw, w_out)
    h2, s_f0 = ffn_fwd(h1, fnw[0], w_up[0], fc8[0], w_down[0], 0)
    h3, s_swa = swa_layer_fwd(h2, anw[1], wqkv, qw, kw, sinks, wo)
    h4, s_f1 = ffn_fwd(h3, fnw[1], w_up[1], fc8[1], w_down[1], 1)
    dh, loss_l = loss_grad(h4, target0)
    dh, d_fnw1, d_up1, d_fc1, d_down1 = ffn_bwd(h3, fnw[1], w_up[1], fc8[1], w_down[1], s_f1, dh, 1)
    dh, d_anw1, d_wqkv, d_qw, d_kw, d_sinks, d_wo = swa_layer_bwd(h2, anw[1], wqkv, qw, kw, sinks, wo, s_swa, dh)
    dh, d_fnw0, d_up0, d_fc0, d_down0 = ffn_bwd(h1, fnw[0], w_up[0], fc8[0], w_down[0], s_f0, dh, 0)
    dh, d_anw0, d_w_in, d_ca, d_dc, d_alog, d_dtb, d_dnw, d_w_out = mixer_bwd(
        h0, anw[0], w_in, ca8, dc8, alog, dtb, dnw, w_out, s_mix, dh)
    return (dh, loss_l, d_anw0, d_anw1, d_fnw0, d_fnw1, d_w_in, d_ca, d_dc, d_alog, d_dtb, d_dnw, d_w_out, d_wqkv,
            d_qw, d_kw, d_sinks, d_wo, d_up0, d_up1, d_fc0, d_fc1, d_down0, d_down1)


def kernel(x, meta_tokens, attn_norm_w, ffn_norm_w, mix_w_in, conv_a_w, dn_conv_w, dn_a_log, dn_dt_bias, dn_norm_w, mix_w_out, swa_wq, swa_wk, swa_wv, swa_q_norm_w, swa_k_norm_w, swa_sinks, swa_wo, ffn_w_up, ffn_conv_w, ffn_w_down, loss_target, m_meta_tokens, m_attn_norm_w, m_ffn_norm_w, m_mix_w_in, m_conv_a_w, m_dn_conv_w, m_dn_a_log, m_dn_dt_bias, m_dn_norm_w, m_mix_w_out, m_swa_wq, m_swa_wk, m_swa_wv, m_swa_q_norm_w, m_swa_k_norm_w, m_swa_sinks, m_swa_wo, m_ffn_w_up, m_ffn_conv_w, m_ffn_w_down, v_meta_tokens, v_attn_norm_w, v_ffn_norm_w, v_mix_w_in, v_conv_a_w, v_dn_conv_w, v_dn_a_log, v_dn_dt_bias, v_dn_norm_w, v_mix_w_out, v_swa_wq, v_swa_wk, v_swa_wv, v_swa_q_norm_w, v_swa_k_norm_w, v_swa_sinks, v_swa_wo, v_ffn_w_up, v_ffn_conv_w, v_ffn_w_down):
    ix, iy, ic = lax.axis_index("x"), lax.axis_index("y"), lax.axis_index("c")
    chip = 2 * ix + iy
    seq = x.shape[1]
    rows = HEAD0 + seq

    small_sharded = (conv_a_w, dn_conv_w, ffn_conv_w, meta_tokens)
    up_b, down_b = ffn_w_up.astype(BF16), ffn_w_down.astype(BF16)
    own = [mix_w_in[0].astype(BF16), mix_w_out[0].astype(BF16), swa_wq[0].astype(BF16), swa_wk[0].astype(BF16),
           swa_wv[0].astype(BF16), swa_wo[0].astype(BF16), up_b[0], up_b[1], down_b[0], down_b[1]]
    first, g_small = gather_weights(own[:2], _flat_pad(small_sharded, SW_ROWS))
    gathered = list(first) + list(gather_weights_beside(own[2:]))
    g_in, g_out, g_q, g_k, g_v, g_o, g_up0, g_up1, g_dn0, g_dn1 = [
        lax.dynamic_update_slice_in_dim(g, t[None], chip, axis=0) for g, t in zip(gathered, own)]
    w_in = jnp.pad(g_in.transpose(1, 0, 2).reshape(D, IN_DIM), ((0, 0), (0, P_W - IN_DIM)))
    w_out, wo = g_out.reshape(D, D), g_o.reshape(D, D)
    wqkv = jnp.concatenate([g_q.reshape(D, D), g_k.reshape(D, 256), g_v.reshape(D, 256)], axis=1)
    w_up = [g_up0, g_up1]
    w_down = [g_dn0.reshape(D_FF, D), g_dn1.reshape(D_FF, D)]
    gs = g_small.reshape(4, -1)
    ca_full = gs[:, 0:384].reshape(4, 3, 128).transpose(1, 0, 2).reshape(3, D_CONV)
    dc_full = gs[:, 384:1920].reshape(4, 4, 384).transpose(1, 0, 2).reshape(4, 3 * DN_DIM)
    fc_full = gs[:, 1920:6144].reshape(4, 2, 3, 704).transpose(1, 2, 0, 3).reshape(2, 3, D_FF)
    meta_full = gs[:, 6144:10240].reshape(4, N_META, 256).transpose(1, 0, 2).reshape(N_META, D)
    ca8, dc8 = _rows8(ca_full), _rows8(dc_full)
    fc8 = [_rows8(fc_full[0]), _rows8(fc_full[1])]
    alog, dtb = _lanes(dn_a_log[0], 4), _lanes(dn_dt_bias[0], 4)
    dnw = dn_norm_w.astype(F32)
    qw, kw = swa_q_norm_w.astype(F32), swa_k_norm_w.astype(F32)
    sinks = swa_sinks[0].astype(F32)
    anw = [attn_norm_w[0:1], attn_norm_w[1:2]]
    fnw = [ffn_norm_w[0:1], ffn_norm_w[1:2]]

    (dh, loss_l, d_anw0, d_anw1, d_fnw0, d_fnw1, d_w_in, d_ca, d_dc, d_alog, d_dtb, d_dnw, d_w_out, d_wqkv, d_qw,
     d_kw, d_sinks, d_wo, d_up0, d_up1, d_fc0, d_fc1, d_down0, d_down1) = local_step(
        x[0], loss_target[0], meta_full, anw, fnw, w_in, ca8, dc8, alog, dtb, dnw, w_out, wqkv, qw, kw, sinks, wo,
        w_up, fc8, w_down)
    grad_x = dh[HEAD0:][None]

    small_parts = [jnp.concatenate([d_anw0, d_anw1], axis=0), jnp.concatenate([d_fnw0, d_fnw1], axis=0),
                   d_alog[0, 4:8], d_dtb[0, 4:8], d_dnw, d_qw, d_kw, d_sinks,
                   d_ca[:3], d_dc[:4], jnp.stack([d_fc0[:3], d_fc1[:3]]), dh[PAD:HEAD0], loss_l[0, 0:1]]
    small_shapes = [(2, D), (2, D), (1, 4), (1, 4), (1, DN_D), (1, SWA_D), (1, SWA_D), (1, SWA_H),
                    (1, 3, D_CONV), (1, 4, 3 * DN_DIM), (2, 3, D_FF), (N_META, D), ()]
    red = sum_slots(gather_small(_flat_pad(small_parts, SV_ROWS)))
    (g_anw, g_fnw, g_alog, g_dtb, g_dnw, g_qw, g_kw, g_sinks, g_ca_f, g_dc_f, g_fc_f, g_meta_f,
     loss) = _split_flat(red, small_shapes)
    g_ca = lax.dynamic_slice_in_dim(g_ca_f, chip * 128, 128, axis=2)
    g_dc = lax.dynamic_slice_in_dim(g_dc_f, chip * 384, 384, axis=2)
    g_fc = lax.dynamic_slice_in_dim(g_fc_f, chip * 704, 704, axis=2)
    g_meta = lax.dynamic_slice_in_dim(g_meta_f, chip * 256, 256, axis=1)

    local = [d_w_in[:, :IN_DIM].reshape(D, 4, 898).transpose(1, 0, 2), d_w_out.reshape(4, 256, D),
             d_wqkv[:, :D].reshape(4, 256, D), d_wqkv[:, D:D + 256].reshape(4, 256, 256),
             d_wqkv[:, D + 256:].reshape(4, 256, 256), d_wo.reshape(4, 256, D), d_up0, d_up1,
             d_down0.reshape(4, 704, D), d_down1.reshape(4, 704, D)]
    c_idx = jnp.reshape(ic, (1,)).astype(jnp.int32)
    chip_idx = jnp.stack([chip, ic]).astype(jnp.int32)
    g_w_in, g_w_out, g_wq, g_wk, g_wv, g_wo, g_up0, g_up1, g_dn0, g_dn1 = reduce_scatter(
        local, c_idx, chip_idx, ("w_in", "w_out", "wq", "wk", "wv", "wo", "up0", "up1", "down0", "down1"))

    grads = dict(meta_tokens=g_meta, attn_norm_w=g_anw, ffn_norm_w=g_fnw, mix_w_in=g_w_in, conv_a_w=g_ca,
                 dn_conv_w=g_dc, dn_a_log=g_alog, dn_dt_bias=g_dtb, dn_norm_w=g_dnw, mix_w_out=g_w_out,
                 swa_wq=g_wq, swa_wk=g_wk, swa_wv=g_wv, swa_q_norm_w=g_qw, swa_k_norm_w=g_kw, swa_sinks=g_sinks,
                 swa_wo=g_wo, ffn_w_up=[g_up0, g_up1], ffn_conv_w=g_fc, ffn_w_down=[g_dn0, g_dn1])
    weights = dict(meta_tokens=meta_tokens, attn_norm_w=attn_norm_w, ffn_norm_w=ffn_norm_w, mix_w_in=mix_w_in,
                   conv_a_w=conv_a_w, dn_conv_w=dn_conv_w, dn_a_log=dn_a_log, dn_dt_bias=dn_dt_bias,
                   dn_norm_w=dn_norm_w, mix_w_out=mix_w_out, swa_wq=swa_wq, swa_wk=swa_wk, swa_wv=swa_wv,
                   swa_q_norm_w=swa_q_norm_w, swa_k_norm_w=swa_k_norm_w, swa_sinks=swa_sinks, swa_wo=swa_wo,
                   ffn_w_up=ffn_w_up, ffn_conv_w=ffn_conv_w, ffn_w_down=ffn_w_down)
    m_in = dict(meta_tokens=m_meta_tokens, attn_norm_w=m_attn_norm_w, ffn_norm_w=m_ffn_norm_w, mix_w_in=m_mix_w_in,
                conv_a_w=m_conv_a_w, dn_conv_w=m_dn_conv_w, dn_a_log=m_dn_a_log, dn_dt_bias=m_dn_dt_bias,
                dn_norm_w=m_dn_norm_w, mix_w_out=m_mix_w_out, swa_wq=m_swa_wq, swa_wk=m_swa_wk, swa_wv=m_swa_wv,
                swa_q_norm_w=m_swa_q_norm_w, swa_k_norm_w=m_swa_k_norm_w, swa_sinks=m_swa_sinks, swa_wo=m_swa_wo,
                ffn_w_up=m_ffn_w_up, ffn_conv_w=m_ffn_conv_w, ffn_w_down=m_ffn_w_down)
    v_in = dict(meta_tokens=v_meta_tokens, attn_norm_w=v_attn_norm_w, ffn_norm_w=v_ffn_norm_w, mix_w_in=v_mix_w_in,
                conv_a_w=v_conv_a_w, dn_conv_w=v_dn_conv_w, dn_a_log=v_dn_a_log, dn_dt_bias=v_dn_dt_bias,
                dn_norm_w=v_dn_norm_w, mix_w_out=v_mix_w_out, swa_wq=v_swa_wq, swa_wk=v_swa_wk, swa_wv=v_swa_wv,
                swa_q_norm_w=v_swa_q_norm_w, swa_k_norm_w=v_swa_k_norm_w, swa_sinks=v_swa_sinks, swa_wo=v_swa_wo,
                ffn_w_up=v_ffn_w_up, ffn_conv_w=v_ffn_conv_w, ffn_w_down=v_ffn_w_down)
    names = list(weights)
    small = [n for n in names if n not in BIG]
    delta, new_m, new_v = {}, {}, {}
    for n in BIG:
        delta[n], new_m[n], new_v[n], grads[n] = adamw(weights[n], grads[n], m_in[n], v_in[n], name=f"adamw_{n}")
    grads = {n: grads[n].reshape(weights[n].shape) for n in names}
    shapes = [weights[n].shape for n in small]
    packed = [_flat_pad([t[n] for n in small], SW_ROWS) for t in (weights, grads, m_in, v_in)]
    for store, flat in zip((delta, new_m, new_v), adamw(*packed, name="adamw_small")):
        for n, t in zip(small, _split_flat(flat, shapes)):
            store[n] = t
    return (loss, grad_x, *[grads[n] for n in names], *[delta[n] for n in names],
            *[new_m[n] for n in names], *[new_v[n] for n in names])
```

```python
import functools

import jax
import jax.numpy as jnp
from jax import lax
from jax.experimental import pallas as pl
from jax.experimental.pallas import tpu as pltpu
from jax.experimental.pallas import tpu_sc as plsc

F32 = jnp.float32
BF16 = jnp.bfloat16
HI = lax.Precision.HIGHEST
MESH = pl.DeviceIdType.MESH

D = 1024
N_META = 16
PAD = 112
HEAD0 = PAD + N_META
D_CONV = 512
DN_H = 4
DN_D = 128
DN_DIM = 512
CH = 64
IN_DIM = 3592
P_W = 3840
BG0 = 3584
SWA_H = 16
SWA_KV = 4
SWA_D = 64
BLK = 128
D_FF = 2816
EPS = 1e-6
LR, B1, B2, AEPS, WD, STEP = 0.001, 0.9, 0.999, 1e-08, 0.01, 10
VMEM_LIMIT = 48 * 1024 * 1024
MM_VMEM_BUDGET = 34 * 1024 * 1024
R_BIG = 6144
R_HALF = R_BIG // 2
SV_ROWS = 48
SW_ROWS = 16


def _pick(n, cands):
    for c in cands:
        if n % c == 0:
            return c
    return n


def _params(sem=None):
    return pltpu.CompilerParams(dimension_semantics=sem, vmem_limit_bytes=VMEM_LIMIT)


def _dot(a, b, ca=1, cb=0, prec=None):
    return lax.dot_general(a, b, (((ca,), (cb,)), ((), ())), precision=prec,
                           preferred_element_type=F32)


def _sigmoid(x):
    return 1.0 / (1.0 + jnp.exp(-x))


def _silu(x):
    return x * _sigmoid(x)


def _dsilu(x):
    s = _sigmoid(x)
    return s * (1.0 + x * (1.0 - s))


def _softplus(x):
    return jnp.maximum(x, 0.0) + jnp.log(1.0 + jnp.exp(-jnp.abs(x)))


def mm(a, b, *, name, ta=False, tb=False, out_dtype=F32, add=None, tm=None, tn=None, tk=None,
       b_chip=False, out_chip=False):
    m, k = (a.shape[1], a.shape[0]) if ta else a.shape
    if b_chip:
        n = b.shape[1] if tb else 4 * b.shape[2]
        if tb:
            tk = b.shape[2]
        else:
            tn = b.shape[2]
    else:
        n = b.shape[0] if tb else b.shape[1]
    if out_chip:
        tn = n // 4
    tn = tn or _pick(n, (1408, 1024, 768, 512, 256, 128))
    tk = tk or (_pick(k, (1408, 704, 384, 128)) if ta else _pick(k, (1024, 1408, 768, 512, 128)))
    nk = k // tk
    if tm is None:
        isz = lambda t: jnp.dtype(t.dtype).itemsize
        osz = jnp.dtype(out_dtype).itemsize
        for tm in ((1408, 1024, 512, 384, 256, 128) if ta else (1408, 704, 512, 384, 256, 128)):
            need = 2 * (tm * tk * isz(a) + tk * tn * isz(b) + tm * tn * osz + (tm * tn * 4 if add is not None else 0))
            need += tm * tn * 4 if nk > 1 else 0
            if m % tm == 0 and need <= MM_VMEM_BUDGET:
                break
        else:
            tm = m
    dims = (((0 if ta else 1,), (1 if tb else 0,)), ((), ()))

    def body(*refs):
        if add is None:
            a_ref, b_ref, o_ref, acc_ref = refs
            add_ref = None
        else:
            a_ref, b_ref, add_ref, o_ref, acc_ref = refs
        part = lax.dot_general(a_ref[...].astype(BF16), b_ref[...].astype(BF16), dims,
                               preferred_element_type=F32)

        def finish(total):
            if add_ref is not None:
                total = total + add_ref[...]
            o_ref[...] = total.astype(out_dtype)

        if nk == 1:
            finish(part)
        else:
            kk = pl.program_id(2)

            @pl.when(kk == 0)
            def _():
                acc_ref[...] = part

            @pl.when(kk > 0)
            def _():
                acc_ref[...] += part

            @pl.when(kk == nk - 1)
            def _():
                finish(acc_ref[...])

    a_spec = pl.BlockSpec((tk, tm), lambda i, j, kk: (kk, i)) if ta else pl.BlockSpec((tm, tk), lambda i, j, kk: (i, kk))
    if b_chip and tb:
        b_spec = pl.BlockSpec((None, tn, tk), lambda i, j, kk: (kk, j, 0))
    elif b_chip:
        b_spec = pl.BlockSpec((None, tk, tn), lambda i, j, kk: (j, kk, 0))
    elif tb:
        b_spec = pl.BlockSpec((tn, tk), lambda i, j, kk: (j, kk))
    else:
        b_spec = pl.BlockSpec((tk, tn), lambda i, j, kk: (kk, j))
    o_spec = pl.BlockSpec((tm, tn), lambda i, j, kk: (i, j))
    in_specs = [a_spec, b_spec] + ([o_spec] if add is not None else [])
    args = [a, b] + ([add] if add is not None else [])
    out_spec = pl.BlockSpec((None, tm, tn), lambda i, j, kk: (j, i, 0)) if out_chip else o_spec
    return pl.pallas_call(
        body, name=name, interpret=False,
        out_shape=jax.ShapeDtypeStruct((4, m, tn) if out_chip else (m, n), out_dtype),
        grid=(m // tm, n // tn, nk), in_specs=in_specs, out_specs=out_spec,
        scratch_shapes=[pltpu.VMEM((tm, tn) if nk > 1 else (8, 128), F32)],
        compiler_params=_params(("parallel", "parallel", "arbitrary")),
    )(*args)


def cols(arr, tr, width=None, cb=0):
    width = width or arr.shape[1]
    return (arr, (tr, width), lambda i: (i, cb), "r2")


def heads(arr, tr):
    return (arr, (arr.shape[0], tr, arr.shape[2]), lambda i: (0, i, 0), "r3")


def whole(arr):
    nd = arr.ndim
    return (arr, arr.shape, lambda i: (0,) * nd, "w")


STRIP = 16


def _rows_of(ref, kind, r0, n):
    if kind == "r2":
        return ref[pl.ds(r0, n), :]
    if kind == "r3":
        return ref[:, pl.ds(r0, n), :]
    return ref[...]


def _set_rows(ref, kind, r0, n, v):
    if kind == "r2":
        ref[pl.ds(r0, n), :] = v.astype(ref.dtype)
    elif kind == "r3":
        ref[:, pl.ds(r0, n), :] = v.astype(ref.dtype)
    else:
        ref[...] = v.astype(ref.dtype)


def rowwise(fn, ins, outs, *, steps, name, accs=(), strip=None):
    n_in, n_out, n_acc = len(ins), len(outs), len(accs)
    kin = [t[3] for t in ins]
    kout = [t[4] for t in outs]
    tr = next((t[1][0] if t[3] == "r2" else t[1][1] for t in ins if t[3] != "w"), 0)

    def body(*refs):
        i = pl.program_id(0)
        in_refs, out_refs, acc_refs = refs[:n_in], refs[n_in:n_in + n_out], refs[n_in + n_out:]
        if n_acc:
            @pl.when(i == 0)
            def _():
                for r in acc_refs:
                    r[...] = jnp.zeros(r.shape, r.dtype)

        def run(r0, n):
            res = fn(i * tr + r0, *[_rows_of(r, k, r0, n) for r, k in zip(in_refs, kin)])
            if not isinstance(res, (tuple, list)):
                res = (res,)
            for r, k, v in zip(out_refs, kout, res[:n_out]):
                _set_rows(r, k, r0, n, v)
            for r, v in zip(acc_refs, res[n_out:]):
                r[...] += jnp.broadcast_to(v, r.shape).astype(r.dtype)

        if strip is None or tr <= strip:
            run(0, tr)
        else:
            def step(s, carry):
                run(pl.multiple_of(s * strip, strip), strip)
                return carry
            lax.fori_loop(0, tr // strip, step, 0)

    def zmap(nd):
        return lambda i: (0,) * nd

    in_specs = [pl.BlockSpec(t[1], t[2]) for t in ins]
    out_specs = [pl.BlockSpec(t[2], t[3]) for t in outs]
    out_specs += [pl.BlockSpec(s, zmap(len(s))) for s, _ in accs]
    out_shape = [jax.ShapeDtypeStruct(t[0], t[1]) for t in outs]
    out_shape += [jax.ShapeDtypeStruct(s, d) for s, d in accs]
    res = pl.pallas_call(
        body, name=name, interpret=False, out_shape=out_shape, grid=(steps,),
        in_specs=in_specs, out_specs=out_specs,
        compiler_params=_params(("arbitrary",)),
    )(*[t[0] for t in ins])
    return res


def out2d(rows, width, dtype, tr):
    return ((rows, width), dtype, (tr, width), lambda i: (i, 0), "r2")


def conv_fwd(xs, w8, kw, *, rows, c, tc, tr, name, post, extras=(), outs=(), pre=None):
    nx, ne, no = len(xs), len(extras), len(outs)
    nr, nc = rows // tr, c // tc
    r8 = tr // 8

    def body(*refs):
        x_refs = refs[:2 * nx]
        w_ref = refs[2 * nx]
        e_refs = refs[2 * nx + 1:2 * nx + 1 + ne]
        o_refs = refs[2 * nx + 1 + ne:2 * nx + 1 + ne + no]
        scr = refs[-1]
        j, i = pl.program_id(0), pl.program_id(1)
        halo = [x_refs[2 * q + 1][...].astype(F32) for q in range(nx)]
        scr[0:8, :] = jnp.where(i > 0, pre(*halo) if pre else halo[0], 0.0)

        def fill(s, carry):
            r0 = pl.multiple_of(s * STRIP, STRIP)
            cur = [x_refs[2 * q][pl.ds(r0, STRIP), :].astype(F32) for q in range(nx)]
            scr[pl.ds(8 + r0, STRIP), :] = pre(*cur) if pre else cur[0]
            return carry

        def comp(s, carry):
            r0 = pl.multiple_of(s * STRIP, STRIP)
            win = scr[pl.ds(r0, STRIP + 8), :]
            y = jnp.zeros((STRIP, tc), F32)
            for q in range(kw):
                sh = kw - 1 - q
                y = y + w_ref[q:q + 1, :] * win[8 - sh:8 - sh + STRIP]
            res = post(j, y, *[e[pl.ds(r0, STRIP), :] for e in e_refs])
            if not isinstance(res, (tuple, list)):
                res = (res,)
            for r, v in zip(o_refs, res):
                r[pl.ds(r0, STRIP), :] = v.astype(r.dtype)
            return carry

        lax.fori_loop(0, tr // STRIP, fill, 0)
        lax.fori_loop(0, tr // STRIP, comp, 0)

    in_specs, args = [], []
    for arr, cb0 in xs:
        in_specs.append(pl.BlockSpec((tr, tc), lambda j, i, cb0=cb0: (i, cb0 + j)))
        in_specs.append(pl.BlockSpec((8, tc), lambda j, i, cb0=cb0: (jnp.maximum(i * r8 - 1, 0), cb0 + j)))
        args += [arr, arr]
    in_specs.append(pl.BlockSpec((8, tc), lambda j, i: (0, j)))
    args.append(w8)
    for arr, cb0 in extras:
        in_specs.append(pl.BlockSpec((tr, tc), lambda j, i, cb0=cb0: (i, cb0 + j)))
        args.append(arr)
    return pl.pallas_call(
        body, name=name, interpret=False,
        out_shape=[jax.ShapeDtypeStruct((rows, c), dt) for dt in outs],
        grid=(nc, nr), in_specs=in_specs,
        out_specs=[pl.BlockSpec((tr, tc), lambda j, i: (i, j)) for _ in outs],
        scratch_shapes=[pltpu.VMEM((tr + 8, tc), F32)],
        compiler_params=_params(("parallel", "arbitrary")),
    )(*args)


def conv_bwd(xs, w8, kw, dy, *, rows, c, tc, tr, name, post, extras=(), outs=(), pre=None):
    nx, ne, no = len(xs), len(extras), len(outs)
    nr, nc = rows // tr, c // tc
    r8 = tr // 8

    def body(*refs):
        x_refs = refs[:2 * nx]
        w_ref, dy_ref, dyn_ref = refs[2 * nx:2 * nx + 3]
        e_refs = refs[2 * nx + 3:2 * nx + 3 + ne]
        o_refs = refs[2 * nx + 3 + ne:2 * nx + 3 + ne + no]
        dw_ref = refs[2 * nx + 3 + ne + no]
        xscr, gscr = refs[-2], refs[-1]
        i = pl.program_id(1)
        halo = [x_refs[2 * q + 1][...].astype(F32) for q in range(nx)]
        xscr[0:8, :] = jnp.where(i > 0, pre(*halo) if pre else halo[0], 0.0)
        gscr[tr:tr + 8, :] = jnp.where(i < nr - 1, dyn_ref[...].astype(F32), 0.0)

        def fill(s, carry):
            r0 = pl.multiple_of(s * STRIP, STRIP)
            cur = [x_refs[2 * q][pl.ds(r0, STRIP), :].astype(F32) for q in range(nx)]
            xscr[pl.ds(8 + r0, STRIP), :] = pre(*cur) if pre else cur[0]
            gscr[pl.ds(r0, STRIP), :] = dy_ref[pl.ds(r0, STRIP), :].astype(F32)
            return carry

        def comp(s, dws):
            r0 = pl.multiple_of(s * STRIP, STRIP)
            gwin = gscr[pl.ds(r0, STRIP + 8), :]
            xwin = xscr[pl.ds(r0, STRIP + 8), :]
            g = gwin[0:STRIP]
            dx = jnp.zeros((STRIP, tc), F32)
            new = []
            for q in range(kw):
                sh = kw - 1 - q
                dx = dx + w_ref[q:q + 1, :] * gwin[sh:sh + STRIP]
                part = g * xwin[8 - sh:8 - sh + STRIP]
                new.append(dws[q] + part[0:8] + part[8:16])
            res = post(dx, *[e[pl.ds(r0, STRIP), :] for e in e_refs])
            if not isinstance(res, (tuple, list)):
                res = (res,)
            for r, v in zip(o_refs, res):
                r[pl.ds(r0, STRIP), :] = v.astype(r.dtype)
            return tuple(new)

        lax.fori_loop(0, tr // STRIP, fill, 0)
        dws = lax.fori_loop(0, tr // STRIP, comp, tuple(jnp.zeros((8, tc), F32) for _ in range(kw)))

        @pl.when(i == 0)
        def _():
            dw_ref[...] = jnp.zeros((8, tc), F32)

        dw_ref[...] += jnp.concatenate([jnp.sum(t, axis=0, keepdims=True) for t in dws]
                                       + [jnp.zeros((8 - kw, tc), F32)], axis=0)

    in_specs, args = [], []
    for arr, cb0 in xs:
        in_specs.append(pl.BlockSpec((tr, tc), lambda j, i, cb0=cb0: (i, cb0 + j)))
        in_specs.append(pl.BlockSpec((8, tc), lambda j, i, cb0=cb0: (jnp.maximum(i * r8 - 1, 0), cb0 + j)))
        args += [arr, arr]
    in_specs.append(pl.BlockSpec((8, tc), lambda j, i: (0, j)))
    in_specs.append(pl.BlockSpec((tr, tc), lambda j, i: (i, j)))
    in_specs.append(pl.BlockSpec((8, tc), lambda j, i: (jnp.minimum((i + 1) * r8, nr * r8 - 1), j)))
    args += [w8, dy, dy]
    for arr, cb0 in extras:
        in_specs.append(pl.BlockSpec((tr, tc), lambda j, i, cb0=cb0: (i, cb0 + j)))
        args.append(arr)
    return pl.pallas_call(
        body, name=name, interpret=False,
        out_shape=[jax.ShapeDtypeStruct((rows, c), dt) for dt in outs] + [jax.ShapeDtypeStruct((8, c), F32)],
        grid=(nc, nr), in_specs=in_specs,
        out_specs=[pl.BlockSpec((tr, tc), lambda j, i: (i, j)) for _ in outs] + [pl.BlockSpec((8, tc), lambda j, i: (0, j))],
        scratch_shapes=[pltpu.VMEM((tr + 8, tc), F32), pltpu.VMEM((tr + 8, tc), F32)],
        compiler_params=_params(("parallel", "arbitrary")),
    )(*args)


def rms_fwd(h, w, *, name):
    rows = h.shape[0]
    tr = _pick(rows, (384, 128))

    def fn(i, x, wv):
        r = lax.rsqrt(jnp.mean(x * x, axis=1, keepdims=True) + EPS)
        return x * r * wv

    return rowwise(fn, [cols(h, tr), whole(w)], [out2d(rows, D, BF16, tr)], steps=rows // tr, name=name)[0]


def rms_bwd(h, w, dy, dres, *, name):
    rows = h.shape[0]
    tr = _pick(rows, (384, 128))

    def fn(i, x, wv, g, dr):
        r = lax.rsqrt(jnp.mean(x * x, axis=1, keepdims=True) + EPS)
        xh = x * r
        gw = g * wv
        dx = r * (gw - xh * jnp.mean(gw * xh, axis=1, keepdims=True))
        row = i + lax.broadcasted_iota(jnp.int32, (x.shape[0], 1), 0)
        return jnp.where(row >= PAD, dr + dx, 0.0), jnp.sum(g * xh, axis=0, keepdims=True)

    return rowwise(fn, [cols(h, tr), whole(w), cols(dy, tr), cols(dres, tr)], [out2d(rows, D, F32, tr)],
                   steps=rows // tr, name=name, accs=[((1, D), F32)])


def loss_grad(h, target):
    rows = h.shape[0]

    def fn(i, y, t):
        diff = jnp.where(i >= HEAD0, y - t, 0.0)
        part = jnp.sum(jnp.sum(diff * diff, axis=1, keepdims=True), axis=0, keepdims=True)
        return diff * (1.0 / D), part * (0.5 / D)

    tgt = (target, (BLK, D), lambda i: (jnp.maximum(i - 1, 0), 0), "r2")
    return rowwise(fn, [cols(h, BLK), tgt], [out2d(rows, D, F32, BLK)], steps=rows // BLK,
                   name="loss_grad", accs=[((1, 128), F32)])


def adamw(w, g, m, v, *, name):
    shape = w.shape
    gs = list(g) if isinstance(g, (list, tuple)) else [g]
    nl = len(gs)
    w2, m2, v2 = (t.reshape(-1, shape[-1]) for t in (w, m, v))
    rows, width = w2.shape
    rl = rows // nl
    tr = _pick(rl, (256, 176, 128, 64, 16, 8))
    nr = rl // tr

    def fn(i, wv, mv, vv, *gvs):
        gv = gvs[0]
        for layer in range(1, nl):
            gv = jnp.where(i >= layer * rl, gvs[layer], gv)
        mn = B1 * mv + (1.0 - B1) * gv
        vn = B2 * vv + (1.0 - B2) * gv * gv
        mh = mn / (1.0 - B1 ** STEP)
        vh = vn / (1.0 - B2 ** STEP)
        return -LR * (mh / (jnp.sqrt(vh) + AEPS) + WD * wv), mn, vn, gv

    g_ins = [(t.reshape(rl, width), (tr, width), lambda i, layer=layer: (jnp.clip(i - layer * nr, 0, nr - 1), 0), "r2")
             for layer, t in enumerate(gs)]
    res = rowwise(fn, [cols(t, tr) for t in (w2, m2, v2)] + g_ins, [out2d(rows, width, F32, tr)] * 4,
                  steps=rows // tr, name=name)
    return [r.reshape(shape) for r in res]


HB = DN_H * CH


def _split(a):
    hi = a.astype(BF16)
    return hi, (a - hi.astype(F32)).astype(BF16)


def _dot1(a, b, ca=1, cb=0):
    return _dot(a.astype(BF16), b.astype(BF16), ca, cb)


def _dot3(a, b, ca=1, cb=0):
    ah, al = _split(a)
    bh, bl = _split(b)
    return _dot(ah, bh, ca, cb) + (_dot(ah, bl, ca, cb) + _dot(al, bh, ca, cb))


def _dot01(m01, b, ca=1, cb=0):
    bh, bl = _split(b)
    m = m01.astype(BF16)
    return _dot(m, bh, ca, cb) + _dot(m, bl, ca, cb)


def _stack(x):
    return jnp.concatenate([x[:, h * DN_D:(h + 1) * DN_D] for h in range(DN_H)], axis=0)


def _unstack(x):
    return jnp.concatenate([x[h * CH:(h + 1) * CH] for h in range(DN_H)], axis=1)


def _tri_inv(a, blk, eye):
    ad = jnp.where(blk, a, 0.0)
    lo = a - ad
    a2 = _dot3(ad, ad)
    a4 = _dot3(a2, a2)
    a8 = _dot3(a4, a4)
    dgi = _dot3(_dot3(_dot3(eye - ad, eye + a2), eye + a4), eye + a8)
    n = _dot3(dgi, lo)
    return _dot3(_dot3(eye - n, eye + _dot3(n, n)), dgi)


def _dn_masks():
    row = lax.broadcasted_iota(jnp.int32, (HB, HB), 0)
    col = lax.broadcasted_iota(jnp.int32, (HB, HB), 1)
    same = (row // CH) == (col // CH)
    incl = jnp.logical_and(same, row >= col)
    strict = jnp.logical_and(same, row > col)
    upper = jnp.logical_and(same, row <= col)
    blk = (row // 16) == (col // 16)
    eye = (row == col).astype(F32)
    return incl, strict, upper, blk, eye


def _dn_chunk(q_ref, k_ref, v_ref, bc_ref, br_ref, incl, strict):
    r64 = lax.broadcasted_iota(jnp.int32, (CH, CH), 0)
    c64 = lax.broadcasted_iota(jnp.int32, (CH, CH), 1)
    bc = bc_ref[...]
    dcol = _dot01((r64 >= c64).astype(F32), bc)
    drow = _dot3(br_ref[0], (r64 <= c64).astype(F32))
    col = lambda m, l0: jnp.concatenate([m[:, l0 + h:l0 + h + 1] for h in range(DN_H)], axis=0)
    b_c = col(bc, 0)
    d_c = col(dcol, 4)
    d_r = jnp.concatenate([drow[4 + h:5 + h, :] for h in range(DN_H)], axis=1)
    d_last_h = [dcol[CH - 1:CH, 4 + h:5 + h] for h in range(DN_H)]
    d_last = jnp.concatenate([jnp.broadcast_to(t, (CH, 1)) for t in d_last_h], axis=0)
    q, k, v = _stack(q_ref[...]), _stack(k_ref[...]), _stack(v_ref[...])
    dm = jnp.where(incl, jnp.exp(jnp.where(incl, d_c - d_r, 0.0)), 0.0)
    kk = _dot1(k, k, 1, 1)
    a = jnp.where(strict, b_c * kk * dm, 0.0)
    ed = jnp.exp(d_c)
    rhs = jnp.concatenate([v * b_c, k * (b_c * ed)], axis=1)
    qk = _dot1(q, k, 1, 1) * dm
    ekd = jnp.exp(d_last - d_c)
    gl = [jnp.exp(t) for t in d_last_h]
    return q, k, v, b_c, dm, kk, a, ed, rhs, qk, ekd, gl


def dn_fwd(qkv_n, bgcol, bgrow):
    rows = qkv_n.shape[0]
    nch = rows // CH

    def body(q_ref, k_ref, v_ref, bc_ref, br_ref, o_ref, s_out, ti_out, s_scr):
        n = pl.program_id(0)

        @pl.when(n == 0)
        def _():
            s_scr[...] = jnp.zeros(s_scr.shape, F32)

        incl, strict, _, blk, eye = _dn_masks()
        q, k, v, b_c, dm, kk, a, ed, rhs, qk, ekd, gl = _dn_chunk(q_ref, k_ref, v_ref, bc_ref, br_ref, incl, strict)
        tinv = _tri_inv(a, blk, eye)
        ti_out[0] = tinv
        sol = _dot3(tinv, rhs)
        u, w = sol[:, :DN_D], sol[:, DN_D:]
        qd, kd = q * ed, k * ekd
        v_new, o_state = [], []
        for h in range(DN_H):
            rs = slice(h * CH, (h + 1) * CH)
            s = s_scr[h]
            s_out[0, h] = s
            vn = u[rs] - _dot1(w[rs], s)
            v_new.append(vn)
            o_state.append(_dot1(qd[rs], s))
            s_scr[h] = gl[h] * s + _dot1(kd[rs], vn, 0, 0)
        o = jnp.concatenate(o_state, axis=0) + _dot1(qk, jnp.concatenate(v_new, axis=0))
        o_ref[...] = _unstack(o)

    return pl.pallas_call(
        body, name="dn_fwd", interpret=False,
        out_shape=[jax.ShapeDtypeStruct((rows, DN_DIM), F32),
                   jax.ShapeDtypeStruct((nch, DN_H, DN_D, DN_D), F32),
                   jax.ShapeDtypeStruct((nch, HB, HB), F32)],
        grid=(nch,),
        in_specs=[pl.BlockSpec((CH, DN_DIM), lambda n: (n, 0)),
                  pl.BlockSpec((CH, DN_DIM), lambda n: (n, 1)),
                  pl.BlockSpec((CH, DN_DIM), lambda n: (n, 2)),
                  pl.BlockSpec((CH, 128), lambda n: (n, 0)),
                  pl.BlockSpec((1, 8, CH), lambda n: (n, 0, 0))],
        out_specs=[pl.BlockSpec((CH, DN_DIM), lambda n: (n, 0)),
                   pl.BlockSpec((1, DN_H, DN_D, DN_D), lambda n: (n, 0, 0, 0)),
                   pl.BlockSpec((1, HB, HB), lambda n: (n, 0, 0))],
        scratch_shapes=[pltpu.VMEM((DN_H, DN_D, DN_D), F32)],
        compiler_params=_params(("arbitrary",)),
    )(qkv_n, qkv_n, qkv_n, bgcol, bgrow)


def dn_bwd(qkv_n, bgcol, bgrow, s_all, ti_all, do):
    rows = qkv_n.shape[0]
    nch = rows // CH

    def body(q_ref, k_ref, v_ref, bc_ref, br_ref, s_ref, ti_ref, do_ref, dq_ref, dk_ref, dv_ref, dbg_ref, ds_scr):
        n = pl.program_id(0)

        @pl.when(n == 0)
        def _():
            ds_scr[...] = jnp.zeros(ds_scr.shape, F32)

        incl, strict, upper, _, _ = _dn_masks()
        q, k, v, b_c, dm, kk, a, ed, rhs, qk, ekd, gl = _dn_chunk(q_ref, k_ref, v_ref, bc_ref, br_ref, incl, strict)
        tinv = ti_ref[0]
        g_o = _stack(do_ref[...])
        sol = _dot3(tinv, rhs)
        u, w = sol[:, :DN_D], sol[:, DN_D:]
        qd, kd = q * ed, k * ekd
        rsum = lambda t: jnp.sum(t, axis=1, keepdims=True)
        rows_of = [slice(h * CH, (h + 1) * CH) for h in range(DN_H)]
        s_h = [s_ref[0, h] for h in range(DN_H)]
        ds_h = [ds_scr[h] for h in range(DN_H)]
        v_new = jnp.concatenate([u[rs] - _dot1(w[rs], s) for rs, s in zip(rows_of, s_h)], axis=0)
        dv_new = _dot1(qk, g_o, 0, 0) + jnp.concatenate([_dot1(kd[rs], t) for rs, t in zip(rows_of, ds_h)], axis=0)
        dqd = jnp.concatenate([_dot1(g_o[rs], s, 1, 1) for rs, s in zip(rows_of, s_h)], axis=0)
        dkd = jnp.concatenate([_dot1(v_new[rs], t, 1, 1) for rs, t in zip(rows_of, ds_h)], axis=0)
        for h, rs in enumerate(rows_of):
            ds_scr[h] = _dot1(qd[rs], g_o[rs], 0, 0) + gl[h] * ds_h[h] - _dot1(w[rs], dv_new[rs], 0, 0)
        dw = jnp.concatenate([-_dot1(dv_new[rs], s, 1, 1) for rs, s in zip(rows_of, s_h)], axis=0)
        dqk = _dot1(g_o, v_new, 1, 1)
        drhs = _dot3(tinv, jnp.concatenate([dv_new, dw], axis=1), 0, 0)
        da = jnp.where(strict, -_dot1(drhs, sol, 1, 1), 0.0)
        drhs_u, drhs_w = drhs[:, :DN_D], drhs[:, DN_D:]
        s2 = rsum(drhs_w * k)
        dbeta = rsum(drhs_u * v) + s2 * ed + rsum(da * kk * dm)
        dkk = da * b_c * dm
        dqkr = dqk * dm
        mmat = da * a + dqk * qk
        tmp = rsum(dkd * kd)
        dd = (s2 * b_c * ed + rsum(mmat) - _dot3(mmat, jnp.ones((HB, 128), F32), 0, 0)[:, :1] + rsum(dqd * qd) - tmp)
        rowi = lax.broadcasted_iota(jnp.int32, (CH, 1), 0)
        last = []
        for h, rs in enumerate(rows_of):
            dgl = jnp.sum(rsum(s_h[h] * ds_h[h]), axis=0, keepdims=True)
            dd_last = jnp.sum(tmp[rs], axis=0, keepdims=True) + dgl * gl[h]
            last.append(jnp.where(rowi == CH - 1, dd_last, 0.0))
        dd = dd + jnp.concatenate(last, axis=0)
        dq_ref[...] = _unstack(_dot1(dqkr, k) + dqd * ed)
        dk_ref[...] = _unstack(drhs_w * (b_c * ed) + _dot1(dkk, k) + _dot1(dkk, k, 0, 0) + _dot1(dqkr, q, 0, 0)
                               + dkd * ekd)
        dv_ref[...] = _unstack(drhs_u * b_c)
        dg = _dot01(upper.astype(F32), jnp.broadcast_to(dd, (HB, 128)))[:, :1]
        lane = lax.broadcasted_iota(jnp.int32, (CH, 128), 1)
        out = jnp.zeros((CH, 128), F32)
        for h, rs in enumerate(rows_of):
            out = out + jnp.where(lane == h, dbeta[rs], 0.0) + jnp.where(lane == 4 + h, dg[rs], 0.0)
        dbg_ref[...] = out

    rev = lambda n: nch - 1 - n
    return pl.pallas_call(
        body, name="dn_bwd", interpret=False,
        out_shape=[jax.ShapeDtypeStruct((rows, DN_DIM), F32)] * 3 + [jax.ShapeDtypeStruct((rows, 128), F32)],
        grid=(nch,),
        in_specs=[pl.BlockSpec((CH, DN_DIM), lambda n: (rev(n), 0)),
                  pl.BlockSpec((CH, DN_DIM), lambda n: (rev(n), 1)),
                  pl.BlockSpec((CH, DN_DIM), lambda n: (rev(n), 2)),
                  pl.BlockSpec((CH, 128), lambda n: (rev(n), 0)),
                  pl.BlockSpec((1, 8, CH), lambda n: (rev(n), 0, 0)),
                  pl.BlockSpec((1, DN_H, DN_D, DN_D), lambda n: (rev(n), 0, 0, 0)),
                  pl.BlockSpec((1, HB, HB), lambda n: (rev(n), 0, 0)),
                  pl.BlockSpec((CH, DN_DIM), lambda n: (rev(n), 0))],
        out_specs=[pl.BlockSpec((CH, DN_DIM), lambda n: (rev(n), 0))] * 3 + [pl.BlockSpec((CH, 128), lambda n: (rev(n), 0))],
        scratch_shapes=[pltpu.VMEM((DN_H, DN_D, DN_D), F32)],
        compiler_params=_params(("arbitrary",)),
    )(qkv_n, qkv_n, qkv_n, bgcol, bgrow, s_all, ti_all, do)


def _swa_valid(n):
    c3 = lax.broadcasted_iota(jnp.int32, (3 * BLK, 4 * BLK), 0)
    r = lax.broadcasted_iota(jnp.int32, (3 * BLK, 4 * BLK), 1) % BLK
    c = c3 % BLK
    lo = jnp.where(c3 < BLK, PAD, jnp.where(c3 < 2 * BLK, r + 1 + jnp.where(n >= 2, 0, BLK), 0))
    hi = jnp.where(c3 < BLK, r + jnp.where(n >= 1, BLK, 0), jnp.where(c3 < 2 * BLK, BLK, r - jnp.where(n >= 1, 0, BLK)))
    return jnp.logical_and(c >= lo, c <= hi)


def _swa_probs(q, kcat, valid, sink):
    s = jnp.where(valid, _dot(kcat, q, 1, 1), -1e30)
    m = jnp.maximum(jnp.max(s, axis=0, keepdims=True), sink)
    e = jnp.where(valid, jnp.exp(s - m), 0.0)
    es = jnp.exp(sink - m)
    inv = 1.0 / (jnp.sum(e, axis=0, keepdims=True) + es)
    return e * inv, es * inv


def _swa_group(q_ref, sk_ref, h):
    q4 = jnp.concatenate([q_ref[4 * h + g] for g in range(4)], axis=0)
    sink4 = jnp.concatenate([jnp.full((1, BLK), sk_ref[4 * h + g], F32) for g in range(4)], axis=1)
    return q4, sink4


def _swa_specs():
    q = pl.BlockSpec((SWA_H, BLK, SWA_D), lambda n: (0, n, 0))
    km = pl.BlockSpec((SWA_KV, BLK, SWA_D), lambda n: (0, 0, 0))
    kp = pl.BlockSpec((SWA_KV, BLK, SWA_D), lambda n: (0, jnp.maximum(n - 1, 0), 0))
    kc = pl.BlockSpec((SWA_KV, BLK, SWA_D), lambda n: (0, n, 0))
    return [q, km, kp, kc, km, kp, kc]


def swa_fwd(qh, kh, vh, sinks):
    rows = qh.shape[1]
    nb = rows // BLK

    def body(q_ref, km, kp, kc, vm, vp, vc, sk_ref, o_ref):
        n = pl.program_id(0)
        valid = _swa_valid(n)
        outs = []
        for h in range(SWA_KV):
            kcat = jnp.concatenate([km[h], kp[h], kc[h]], axis=0)
            vcat = jnp.concatenate([vm[h], vp[h], vc[h]], axis=0)
            q4, sink4 = _swa_group(q_ref, sk_ref, h)
            p, _ = _swa_probs(q4, kcat, valid, sink4)
            o4 = _dot(p.astype(BF16), vcat, 0, 0)
            outs += [o4[g * BLK:(g + 1) * BLK] for g in range(4)]
        o_ref[...] = jnp.concatenate(outs, axis=1).astype(BF16)

    return pl.pallas_call(
        body, name="swa_fwd", interpret=False,
        out_shape=jax.ShapeDtypeStruct((rows, SWA_H * SWA_D), BF16),
        grid=(nb,),
        in_specs=_swa_specs() + [pl.BlockSpec(memory_space=pltpu.SMEM)],
        out_specs=pl.BlockSpec((BLK, SWA_H * SWA_D), lambda n: (n, 0)),
        compiler_params=_params(("parallel",)),
    )(qh, kh, kh, kh, vh, vh, vh, sinks)


def swa_bwd(qh, kh, vh, sinks, do):
    rows = qh.shape[1]
    nb = rows // BLK

    def body(q_ref, km, kp, kc, vm, vp, vc, do_ref, sk_ref, dq_ref, dk_ref, dv_ref, dsk_ref):
        n = pl.program_id(0)

        @pl.when(n == 0)
        def _():
            dk_ref[...] = jnp.zeros(dk_ref.shape, F32)
            dv_ref[...] = jnp.zeros(dv_ref.shape, F32)

        valid = _swa_valid(n)
        g_all = do_ref[...]
        rowi = lax.broadcasted_iota(jnp.int32, (SWA_H, 128), 0)
        dsk = jnp.zeros((SWA_H, 128), F32)
        pm = pl.multiple_of(jnp.maximum(n - 1, 0) * BLK, BLK)
        pc = pl.multiple_of(n * BLK, BLK)
        for h in range(SWA_KV):
            kcat = jnp.concatenate([km[h], kp[h], kc[h]], axis=0)
            vcat = jnp.concatenate([vm[h], vp[h], vc[h]], axis=0)
            q4, sink4 = _swa_group(q_ref, sk_ref, h)
            p, ps = _swa_probs(q4, kcat, valid, sink4)
            g4 = jnp.concatenate([g_all[:, (4 * h + g) * SWA_D:(4 * h + g + 1) * SWA_D] for g in range(4)], axis=0)
            dp = _dot(vcat, g4, 1, 1)
            delta = jnp.sum(p * dp, axis=0, keepdims=True)
            ds = (p * (dp - delta)).astype(BF16)
            dq4 = _dot(ds, kcat, 0, 0)
            dkc = _dot(ds, q4)
            dvc = _dot(p.astype(BF16), g4)
            t = ps * delta
            for g in range(4):
                dq_ref[4 * h + g] = dq4[g * BLK:(g + 1) * BLK]
                part = -jnp.sum(t[:, g * BLK:(g + 1) * BLK], axis=1, keepdims=True)
                dsk = dsk + jnp.where(rowi == 4 * h + g, part, 0.0)
            lanes = slice(h * SWA_D, (h + 1) * SWA_D)
            for ref, val in ((dk_ref, dkc), (dv_ref, dvc)):
                ref[0:BLK, lanes] += val[0:BLK]
                ref[pl.ds(pm, BLK), lanes] += val[BLK:2 * BLK]
                ref[pl.ds(pc, BLK), lanes] += val[2 * BLK:]
        dsk_ref[0] = dsk

    return pl.pallas_call(
        body, name="swa_bwd", interpret=False,
        out_shape=[jax.ShapeDtypeStruct((SWA_H, rows, SWA_D), F32),
                   jax.ShapeDtypeStruct((rows, SWA_KV * SWA_D), F32),
                   jax.ShapeDtypeStruct((rows, SWA_KV * SWA_D), F32),
                   jax.ShapeDtypeStruct((nb, SWA_H, 128), F32)],
        grid=(nb,),
        in_specs=_swa_specs() + [pl.BlockSpec((BLK, SWA_H * SWA_D), lambda n: (n, 0)),
                                 pl.BlockSpec(memory_space=pltpu.SMEM)],
        out_specs=[pl.BlockSpec((SWA_H, BLK, SWA_D), lambda n: (0, n, 0)),
                   pl.BlockSpec((rows, SWA_KV * SWA_D), lambda n: (0, 0)),
                   pl.BlockSpec((rows, SWA_KV * SWA_D), lambda n: (0, 0)),
                   pl.BlockSpec((1, SWA_H, 128), lambda n: (n, 0, 0))],
        compiler_params=_params(("arbitrary",)),
    )(qh, kh, kh, kh, vh, vh, vh, do, sinks)


def qknorm_fwd(qkv, qw, kw):
    rows = qkv.shape[0]
    tr = _pick(rows, (384, 128))
    scale = SWA_D ** -0.5

    def fn(i, x, qwv, kwv):
        def normed(j, wv, sc):
            xs = x[:, j * SWA_D:(j + 1) * SWA_D]
            r = lax.rsqrt(jnp.mean(xs * xs, axis=1, keepdims=True) + EPS)
            return (xs * r * wv * sc)[None]
        qo = jnp.concatenate([normed(j, qwv, scale) for j in range(SWA_H)], axis=0)
        ko = jnp.concatenate([normed(SWA_H + j, kwv, 1.0) for j in range(SWA_KV)], axis=0)
        vo = jnp.concatenate([x[:, (SWA_H + SWA_KV + j) * SWA_D:(SWA_H + SWA_KV + j + 1) * SWA_D][None]
                              for j in range(SWA_KV)], axis=0)
        return qo, ko, vo

    hm = lambda nh: ((nh, rows, SWA_D), BF16, (nh, tr, SWA_D), lambda i: (0, i, 0), "r3")
    return rowwise(fn, [cols(qkv, tr), whole(qw), whole(kw)], [hm(SWA_H), hm(SWA_KV), hm(SWA_KV)],
                   steps=rows // tr, name="qknorm_fwd")


def qknorm_bwd(qkv, qw, kw, dqh, dkh, dvh):
    rows = qkv.shape[0]
    tr = _pick(rows, (384, 128))
    scale = SWA_D ** -0.5

    def fn(i, x, qwv, kwv, dq, dk, dv):
        pieces = []
        dws = [jnp.zeros((1, SWA_D), F32), jnp.zeros((1, SWA_D), F32)]

        def one(j, dy, wv, sc, which):
            xs = x[:, j * SWA_D:(j + 1) * SWA_D]
            r = lax.rsqrt(jnp.mean(xs * xs, axis=1, keepdims=True) + EPS)
            xh = xs * r
            gw = dy * wv * sc
            pieces.append(r * (gw - xh * jnp.mean(gw * xh, axis=1, keepdims=True)))
            dws[which] = dws[which] + jnp.sum(dy * sc * xh, axis=0, keepdims=True)

        for j in range(SWA_H):
            one(j, dq[j], qwv, scale, 0)
        for j in range(SWA_KV):
            one(SWA_H + j, dk[:, j * SWA_D:(j + 1) * SWA_D], kwv, 1.0, 1)
        pieces.append(dv)
        return jnp.concatenate(pieces, axis=1), dws[0], dws[1]

    return rowwise(fn, [cols(qkv, tr), whole(qw), whole(kw), heads(dqh, tr), cols(dkh, tr), cols(dvh, tr)],
                   [out2d(rows, 1536, BF16, tr)], steps=rows // tr, name="qknorm_bwd",
                   accs=[((1, SWA_D), F32), ((1, SWA_D), F32)])


def _place():
    return lax.axis_index("x"), lax.axis_index("y"), lax.axis_index("c")


ANY = pl.BlockSpec(memory_space=pl.ANY)


def _rcopy(ssem, rsem, k, src, dst, to):
    return pltpu.make_async_remote_copy(src_ref=src, dst_ref=dst, send_sem=ssem.at[k], recv_sem=rsem.at[k],
                                        device_id=to, device_id_type=MESH)


def gather_weights(shards, small):
    n = len(shards)
    halves = [t.shape[0] // 2 for t in shards]

    def body(*refs):
        s_refs, small_ref = refs[:n], refs[n]
        o_refs, osmall = refs[n + 1:2 * n + 1], refs[2 * n + 1]
        ssem, rsem, lsem = refs[2 * n + 2:]
        x, y, c = _place()
        me = 2 * x + y
        chips = [(1 - x, y), (x, 1 - y), (1 - x, 1 - y)]

        def half(k, s, hh):
            return o_refs[k].at[s, pl.ds(hh * halves[k], halves[k]), :]

        loc = pltpu.make_async_copy(small_ref, osmall.at[me], lsem)
        loc.start()
        sends = []
        for k in range(n):
            for j, (px, py) in enumerate(chips):
                sends.append(_rcopy(ssem, rsem, 6 * k + j, s_refs[k].at[pl.ds(c * halves[k], halves[k]), :],
                                    half(k, me, c), (px, py, c)))
        for j, (px, py) in enumerate(chips):
            sends.append(_rcopy(ssem, rsem, 6 * n + j, small_ref, osmall.at[me], (px, py, c)))
        for cp in sends:
            cp.start()
        for k in range(n):
            for j, (px, py) in enumerate(chips):
                s = 2 * px + py
                _rcopy(ssem, rsem, 6 * k + j, half(k, s, c), half(k, s, c), (x, y, c)).wait_recv()
                fwd = _rcopy(ssem, rsem, 6 * k + 3 + j, half(k, s, c), half(k, s, c), (x, y, 1 - c))
                fwd.start()
                sends.append(fwd)
        for k in range(n):
            for j, (px, py) in enumerate(chips):
                s = 2 * px + py
                _rcopy(ssem, rsem, 6 * k + 3 + j, half(k, s, 1 - c), half(k, s, 1 - c), (x, y, c)).wait_recv()
        for j, (px, py) in enumerate(chips):
            s = 2 * px + py
            _rcopy(ssem, rsem, 6 * n + j, osmall.at[s], osmall.at[s], (x, y, c)).wait_recv()
        for cp in sends:
            cp.wait_send()
        loc.wait()

    res = pl.pallas_call(
        body, name="gather_weights", interpret=False,
        out_shape=[jax.ShapeDtypeStruct((4,) + t.shape, t.dtype) for t in shards]
        + [jax.ShapeDtypeStruct((4, SW_ROWS, 1024), F32)],
        in_specs=[ANY] * (n + 1), out_specs=[ANY] * (n + 1),
        scratch_shapes=[pltpu.SemaphoreType.DMA((6 * n + 3,)), pltpu.SemaphoreType.DMA((6 * n + 3,)),
                        pltpu.SemaphoreType.DMA],
    )(*shards, small)
    return res[:n], res[n]


def _handshake(peers):
    barrier = pltpu.get_barrier_semaphore()
    for peer in peers:
        pl.semaphore_signal(barrier, inc=1, device_id=peer, device_id_type=MESH)
    pl.semaphore_wait(barrier, len(peers))


def gather_weights_beside(shards):
    n = len(shards)
    halves = [t.shape[0] // 2 for t in shards]

    def body(*refs):
        s_refs, o_refs, ssem, rsem = refs[:n], refs[n:2 * n], refs[2 * n], refs[2 * n + 1]
        x, y, c = _place()
        me = 2 * x + y
        chips = [(1 - x, y), (x, 1 - y), (1 - x, 1 - y)]
        _handshake([(px, py, c) for px, py in chips] + [(x, y, 1 - c)])

        def half(k, s, hh):
            return o_refs[k].at[s, pl.ds(hh * halves[k], halves[k]), :]

        sends = []
        for k in range(n):
            for j, (px, py) in enumerate(chips):
                sends.append(_rcopy(ssem, rsem, 6 * k + j, s_refs[k].at[pl.ds(c * halves[k], halves[k]), :],
                                    half(k, me, c), (px, py, c)))
        for cp in sends:
            cp.start()
        for k in range(n):
            for j, (px, py) in enumerate(chips):
                s = 2 * px + py
                _rcopy(ssem, rsem, 6 * k + j, half(k, s, c), half(k, s, c), (x, y, c)).wait_recv()
                fwd = _rcopy(ssem, rsem, 6 * k + 3 + j, half(k, s, c), half(k, s, c), (x, y, 1 - c))
                fwd.start()
                sends.append(fwd)
        for k in range(n):
            for j, (px, py) in enumerate(chips):
                s = 2 * px + py
                _rcopy(ssem, rsem, 6 * k + 3 + j, half(k, s, 1 - c), half(k, s, 1 - c), (x, y, c)).wait_recv()
        for cp in sends:
            cp.wait_send()

    return pl.kernel(
        body, name="gather_weights_beside",
        out_type=[jax.ShapeDtypeStruct((4,) + t.shape, t.dtype) for t in shards],
        mesh=plsc.ScalarSubcoreMesh(axis_name="sequencer", num_cores=1),
        scratch_types=[pltpu.SemaphoreType.DMA((6 * n,)), pltpu.SemaphoreType.DMA((6 * n,))],
        compiler_params=pltpu.CompilerParams(collective_id=1),
    )(*shards)


def swap_halves(gs, *, name):
    n = len(gs)

    def body(*refs):
        g_refs, o_refs, ssem, rsem = refs[:n], refs[n:2 * n], refs[2 * n], refs[2 * n + 1]
        x, y, c = _place()
        cps = []
        for k in range(n):
            hk = g_refs[k].shape[1] // 2
            cps.append(_rcopy(ssem, rsem, k, g_refs[k].at[:, pl.ds((1 - c) * hk, hk), :], o_refs[k], (x, y, 1 - c)))
        for cp in cps:
            cp.start()
        for cp in cps:
            cp.wait()

    return pl.pallas_call(
        body, name=name, interpret=False,
        out_shape=[jax.ShapeDtypeStruct((4, t.shape[1] // 2, t.shape[2]), t.dtype) for t in gs],
        in_specs=[ANY] * n, out_specs=[ANY] * n,
        scratch_shapes=[pltpu.SemaphoreType.DMA((n,)), pltpu.SemaphoreType.DMA((n,))],
    )(*gs)


def _sum_rows(hk):
    return _pick(hk, (512, 352, 256, 128))


def pair_sum(g, other, c_idx, *, name):
    _, hk, width = other.shape
    tr = _sum_rows(hk)
    nbk = hk // tr

    def body(c_ref, g_ref, o_ref, out_ref):
        out_ref[...] = (g_ref[...].astype(F32) + o_ref[...].astype(F32)).astype(BF16)

    return pl.pallas_call(
        body, name=name, interpret=False,
        out_shape=jax.ShapeDtypeStruct((4, hk, width), BF16),
        grid_spec=pltpu.PrefetchScalarGridSpec(
            num_scalar_prefetch=1, grid=(4, nbk),
            in_specs=[pl.BlockSpec((1, tr, width), lambda s, i, c_ref: (s, c_ref[0] * nbk + i, 0)),
                      pl.BlockSpec((1, tr, width), lambda s, i, c_ref: (s, i, 0))],
            out_specs=pl.BlockSpec((1, tr, width), lambda s, i, c_ref: (s, i, 0))),
        compiler_params=_params(("parallel", "parallel")),
    )(c_idx, g, other)


def chip_sum(p, got, idx, *, name):
    _, hk, width = got.shape
    tr = _sum_rows(hk)
    nbk = hk // tr

    def body(idx_ref, p_ref, g_ref, out_ref):
        acc = p_ref[0].astype(F32)
        for j in range(3):
            acc = acc + g_ref[j].astype(F32)
        out_ref[0] = acc

    return pl.pallas_call(
        body, name=name, interpret=False,
        out_shape=jax.ShapeDtypeStruct((2, hk, width), F32),
        grid_spec=pltpu.PrefetchScalarGridSpec(
            num_scalar_prefetch=1, grid=(nbk,),
            in_specs=[pl.BlockSpec((1, tr, width), lambda i, idx_ref: (idx_ref[0], i, 0)),
                      pl.BlockSpec((3, tr, width), lambda i, idx_ref: (0, i, 0))],
            out_specs=pl.BlockSpec((1, tr, width), lambda i, idx_ref: (idx_ref[1], i, 0))),
        compiler_params=_params(("parallel",)),
    )(idx, p, got)


def join_halves(qs):
    n = len(qs)

    def body(*refs):
        q_refs, o_refs, ssem, rsem = refs[:n], refs[n:2 * n], refs[2 * n], refs[2 * n + 1]
        x, y, c = _place()
        cps = [_rcopy(ssem, rsem, k, q_refs[k].at[c], o_refs[k].at[c], (x, y, 1 - c)) for k in range(n)]
        for cp in cps:
            cp.start()
        for k in range(n):
            _rcopy(ssem, rsem, k, q_refs[k].at[c], o_refs[k].at[1 - c], (x, y, 1 - c)).wait_recv()
        for cp in cps:
            cp.wait_send()

    return pl.pallas_call(
        body, name="join_halves", interpret=False,
        out_shape=[jax.ShapeDtypeStruct(t.shape, t.dtype) for t in qs],
        in_specs=[ANY] * n, out_specs=[ANY] * n, input_output_aliases={k: k for k in range(n)},
        scratch_shapes=[pltpu.SemaphoreType.DMA((n,)), pltpu.SemaphoreType.DMA((n,))],
    )(*qs)


def scatter_chips_beside(ps, cid, name):
    n = len(ps)

    def body(*refs):
        p_refs, o_refs, ssem, rsem = refs[:n], refs[n:2 * n], refs[2 * n], refs[2 * n + 1]
        x, y, c = _place()
        chips = [(1 - x, y), (x, 1 - y), (1 - x, 1 - y)]
        _handshake([(px, py, c) for px, py in chips])
        cps = [_rcopy(ssem, rsem, 3 * k + j, p_refs[k].at[2 * px + py], o_refs[k].at[j], (px, py, c))
               for k in range(n) for j, (px, py) in enumerate(chips)]
        for cp in cps:
            cp.start()
        for cp in cps:
            cp.wait()

    return pl.kernel(
        body, name=name, out_type=[jax.ShapeDtypeStruct((3,) + t.shape[1:], t.dtype) for t in ps],
        mesh=plsc.ScalarSubcoreMesh(axis_name="sequencer", num_cores=1),
        scratch_types=[pltpu.SemaphoreType.DMA((3 * n,)), pltpu.SemaphoreType.DMA((3 * n,))],
        compiler_params=pltpu.CompilerParams(collective_id=cid),
    )(*ps)


def reduce_begin(gs, names, c_idx, cid, tag):
    others = swap_halves(gs, name=f"swap_halves_{tag}")
    pairs = [pair_sum(g, o, c_idx, name=f"pair_sum_{nm}") for g, o, nm in zip(gs, others, names)]
    return pairs, scatter_chips_beside(pairs, cid, f"scatter_chips_{tag}")


def reduce_end(pairs, gots, names, idx):
    mine = [chip_sum(p, g, idx, name=f"chip_sum_{nm}") for p, g, nm in zip(pairs, gots, names)]
    return [q.reshape(2 * q.shape[1], q.shape[2]) for q in join_halves(mine)]


def gather_small(v):
    def body(v_ref, o_ref, ssem, rsem, lsem):
        x, y, c = _place()
        loc = pltpu.make_async_copy(v_ref, o_ref.at[4 * x + 2 * y + c], lsem)
        loc.start()
        cps = []
        for k in range(1, 8):
            fx, fy, fc = (k >> 2) & 1, (k >> 1) & 1, k & 1
            px = 1 - x if fx else x
            py = 1 - y if fy else y
            pc = 1 - c if fc else c
            cps.append((pltpu.make_async_remote_copy(
                src_ref=v_ref, dst_ref=o_ref.at[4 * x + 2 * y + c], send_sem=ssem.at[k - 1], recv_sem=rsem.at[k - 1],
                device_id=(px, py, pc), device_id_type=MESH), 4 * px + 2 * py + pc))
        for cp, _ in cps:
            cp.start()
        for k, (cp, peer) in enumerate(cps):
            pltpu.make_async_remote_copy(
                src_ref=v_ref, dst_ref=o_ref.at[peer], send_sem=ssem.at[k], recv_sem=rsem.at[k],
                device_id=(x, y, c), device_id_type=MESH).wait_recv()
        for cp, _ in cps:
            cp.wait_send()
        loc.wait()

    return pl.pallas_call(
        body, name="gather_small", interpret=False,
        out_shape=jax.ShapeDtypeStruct((8, SV_ROWS, 1024), F32),
        in_specs=[ANY], out_specs=ANY,
        scratch_shapes=[pltpu.SemaphoreType.DMA((7,)), pltpu.SemaphoreType.DMA((7,)), pltpu.SemaphoreType.DMA],
    )(v)


def sum_slots(a):
    def fn(i, t):
        acc = t[0]
        for k in range(1, 8):
            acc = acc + t[k]
        return acc

    return rowwise(fn, [whole(a)], [((SV_ROWS, 1024), F32, (SV_ROWS, 1024), lambda i: (0, 0), "w")], steps=1,
                   name="sum_slots")[0]


def _head_rms(x, nw):
    xs, rs = [], []
    for h in range(DN_H):
        xh = x[:, h * DN_D:(h + 1) * DN_D]
        r = lax.rsqrt(jnp.mean(xh * xh, axis=1, keepdims=True) + EPS)
        xs.append(xh * r)
        rs.append(r)
    return xs, rs


def bg_fwd(p, alog, dtb):
    rows = p.shape[0]
    tr = _pick(rows, (384, 128))

    def fn(i, x, al, dt):
        lane = lax.broadcasted_iota(jnp.int32, x.shape, 1)
        row = i + lax.broadcasted_iota(jnp.int32, x.shape, 0)
        g = -jnp.exp(al) * _softplus(x + dt)
        out = jnp.where(lane < 4, _sigmoid(x), jnp.where(lane < 8, g, 0.0))
        return jnp.where(row >= PAD, out, 0.0)

    return rowwise(fn, [cols(p, tr, 128, BG0 // 128), whole(alog), whole(dtb)], [out2d(rows, 128, F32, tr)],
                   steps=rows // tr, name="bg_fwd")[0]


def bg_bwd(p, alog, dtb, dbg):
    rows = p.shape[0]
    tr = _pick(rows, (384, 128))

    def fn(i, x, al, dt, g_in):
        lane = lax.broadcasted_iota(jnp.int32, x.shape, 1)
        row = i + lax.broadcasted_iota(jnp.int32, x.shape, 0)
        live = row >= PAD
        is_b = jnp.logical_and(live, lane < 4)
        is_g = jnp.logical_and(live, jnp.logical_and(lane >= 4, lane < 8))
        beta = _sigmoid(x)
        ea = jnp.exp(al)
        g = -ea * _softplus(x + dt)
        dalpha = jnp.where(is_g, g_in * (-ea) * _sigmoid(x + dt), 0.0)
        dx = jnp.where(is_b, g_in * beta * (1.0 - beta), dalpha)
        dal = jnp.sum(jnp.where(is_g, g_in * g, 0.0), axis=0, keepdims=True)
        return jnp.concatenate([dx, jnp.zeros(x.shape, F32)], axis=1), dal, jnp.sum(dalpha, axis=0, keepdims=True)

    return rowwise(fn, [cols(p, tr, 128, BG0 // 128), whole(alog), whole(dtb), cols(dbg, tr)],
                   [out2d(rows, 256, BF16, tr)], steps=rows // tr, name="bg_bwd",
                   accs=[((1, 128), F32), ((1, 128), F32)])


def dn_qkv_post(j, y):
    xs = _silu(y)
    sc = jnp.where(j == 0, DN_D ** -0.5, 1.0)
    outs = []
    for h in range(DN_H):
        xh = xs[:, h * DN_D:(h + 1) * DN_D]
        r = lax.rsqrt(jnp.sum(xh * xh, axis=1, keepdims=True) + EPS)
        outs.append(jnp.where(j < 2, xh * r * sc, xh))
    return jnp.concatenate(outs, axis=1), y


def dn_qkv_bwd(cq, dq, dk, dv):
    rows = cq.shape[0]
    tr = _pick(rows, (384, 128))

    def fn(i, c0, c1, c2, g0, g1, g2):
        pieces = []
        for kind, (cv, g) in enumerate(((c0, g0), (c1, g1), (c2, g2))):
            xs = _silu(cv)
            if kind < 2:
                sc = DN_D ** -0.5 if kind == 0 else 1.0
                ds = []
                for h in range(DN_H):
                    sl = slice(h * DN_D, (h + 1) * DN_D)
                    xh, gh = xs[:, sl], g[:, sl]
                    r = lax.rsqrt(jnp.sum(xh * xh, axis=1, keepdims=True) + EPS)
                    xn = xh * r
                    ds.append(sc * r * (gh - xn * jnp.sum(gh * xn, axis=1, keepdims=True)))
                dxs = jnp.concatenate(ds, axis=1)
            else:
                dxs = g
            pieces.append(dxs * _dsilu(cv))
        return jnp.concatenate(pieces, axis=1)

    ins = [cols(cq, tr, DN_DIM, k) for k in range(3)] + [cols(t, tr) for t in (dq, dk, dv)]
    return rowwise(fn, ins, [out2d(rows, 3 * DN_DIM, F32, tr)], steps=rows // tr, name="dn_qkv_bwd")[0]


def dn_out_fwd(o, p, nw):
    rows = o.shape[0]
    tr = _pick(rows, (384, 128))

    def fn(i, ov, z, w):
        xs, _ = _head_rms(ov, w)
        return jnp.concatenate(xs, axis=1) * jnp.concatenate([w] * DN_H, axis=1) * _silu(z)

    return rowwise(fn, [cols(o, tr), cols(p, tr, DN_DIM, 6), whole(nw)], [out2d(rows, DN_DIM, BF16, tr)],
                   steps=rows // tr, name="dn_out_fwd")[0]


def dn_out_bwd(o, p, nw, dymix):
    rows = o.shape[0]
    tr = _pick(rows, (384, 128))

    def fn(i, ov, z, w, dy):
        xs, rs = _head_rms(ov, w)
        sz = _silu(z)
        dn = dy * sz
        dos, dw = [], jnp.zeros((1, DN_D), F32)
        for h in range(DN_H):
            sl = slice(h * DN_D, (h + 1) * DN_D)
            gw = dn[:, sl] * w
            dos.append(rs[h] * (gw - xs[h] * jnp.mean(gw * xs[h], axis=1, keepdims=True)))
            dw = dw + jnp.sum(dn[:, sl] * xs[h], axis=0, keepdims=True)
        n = jnp.concatenate(xs, axis=1) * jnp.concatenate([w] * DN_H, axis=1)
        return jnp.concatenate(dos, axis=1), dy * n * _dsilu(z), dw

    return rowwise(fn, [cols(o, tr), cols(p, tr, DN_DIM, 6), whole(nw), cols(dymix, tr, DN_DIM, 1)],
                   [out2d(rows, DN_DIM, F32, tr), out2d(rows, DN_DIM, BF16, tr)], steps=rows // tr,
                   name="dn_out_bwd", accs=[((1, DN_D), F32)])


def conv_a_pre_bwd(dymix, cv, p):
    rows = cv.shape[0]
    tr = _pick(rows, (384, 128))

    def fn(i, dy, c, go):
        return dy * c, dy * go

    return rowwise(fn, [cols(dymix, tr, D_CONV, 0), cols(cv, tr), cols(p, tr, D_CONV, 1)],
                   [out2d(rows, D_CONV, BF16, tr), out2d(rows, D_CONV, F32, tr)], steps=rows // tr,
                   name="conv_a_pre_bwd")


def ffn_act_bwd(da, gc, u):
    rows = da.shape[0]
    tr = _pick(rows, (384, 128))

    def fn(i, g, c, val):
        g, c, val = g.astype(F32), c.astype(F32), val.astype(F32)
        return g * _silu(c), g * val * _dsilu(c)

    return rowwise(fn, [cols(da, tr), cols(gc, tr), cols(u, tr, D_FF, 1)],
                   [out2d(rows, D_FF, BF16, tr), out2d(rows, D_FF, F32, tr)], steps=rows // tr, name="ffn_act_bwd")


def _rows8(w):
    return jnp.pad(w.astype(F32), ((0, 8 - w.shape[0]), (0, 0)))


def _lanes(v, at):
    return jnp.pad(v.astype(F32), (at, 128 - at - v.shape[0]))[None]


def ffn_fwd(h, nw, w_up, cw8, w_down, tag):
    rows = h.shape[0]
    tr = _pick(rows, (384, 128))
    hn = rms_fwd(h, nw, name=f"ffn{tag}_norm")
    u = mm(hn, w_up, out_dtype=BF16, b_chip=True, name=f"ffn{tag}_up")
    a, gc = conv_fwd([(u, 0)], cw8, 3, rows=rows, c=D_FF, tc=1408, tr=tr, name=f"ffn{tag}_conv",
                     post=lambda j, y, val: (_silu(y) * val.astype(F32), y), extras=[(u, 2)], outs=[BF16, BF16])
    out = mm(a, w_down, add=h, name=f"ffn{tag}_down")
    return out, (hn, u, a, gc)


def ffn_bwd(h, nw, w_up, cw8, w_down, saved, dh, tag):
    hn, u, a, gc = saved
    rows = h.shape[0]
    tr = _pick(rows, (384, 128))
    da = mm(dh, w_down, tb=True, out_dtype=BF16, name=f"ffn{tag}_down_dx")
    d_w_down = mm(a, dh, ta=True, out_dtype=BF16, name=f"ffn{tag}_down_dw")
    dval, dgc = ffn_act_bwd(da, gc, u)
    dgate, d_cw = conv_bwd([(u, 0)], cw8, 3, dgc, rows=rows, c=D_FF, tc=1408, tr=tr, name=f"ffn{tag}_conv_bwd",
                           post=lambda dx: dx, outs=[BF16])
    du = jnp.concatenate([dgate, dval], axis=1)
    dhn = mm(du, w_up, tb=True, b_chip=True, name=f"ffn{tag}_up_dx")
    d_w_up = mm(hn, du, ta=True, out_dtype=BF16, out_chip=True, name=f"ffn{tag}_up_dw")
    dh_new, d_nw = rms_bwd(h, nw, dhn, dh, name=f"ffn{tag}_norm_bwd")
    return dh_new, d_nw, d_w_up, d_cw, d_w_down


def mixer_fwd(h, nw, w_in, ca8, dc8, alog, dtb, dnw, w_out):
    rows = h.shape[0]
    tr = _pick(rows, (384, 128))
    hn = rms_fwd(h, nw, name="mix_norm")
    p = mm(hn, w_in, name="mix_in")
    y_a, cv = conv_fwd([(p, 0), (p, 2)], ca8, 3, rows=rows, c=D_CONV, tc=D_CONV, tr=tr, name="conv_a",
                       pre=lambda gi, ah: gi * ah, post=lambda j, y, go: (go * y, y), extras=[(p, 1)],
                       outs=[BF16, F32])
    qkv_n, cq = conv_fwd([(p, 3)], dc8, 4, rows=rows, c=3 * DN_DIM, tc=DN_DIM, tr=tr, name="dn_conv",
                         post=dn_qkv_post, outs=[F32, F32])
    bgcol = bg_fwd(p, alog, dtb)
    bgrow = bgcol[:, :8].reshape(rows // CH, CH, 8).transpose(0, 2, 1)
    o, s_all, ti_all = dn_fwd(qkv_n, bgcol, bgrow)
    y_b = dn_out_fwd(o, p, dnw)
    ymix = jnp.concatenate([y_a, y_b], axis=1)
    out = mm(ymix, w_out, add=h, name="mix_out")
    return out, (hn, p, cv, qkv_n, cq, bgcol, bgrow, o, s_all, ti_all, ymix)


def mixer_bwd(h, nw, w_in, ca8, dc8, alog, dtb, dnw, w_out, saved, dh):
    hn, p, cv, qkv_n, cq, bgcol, bgrow, o, s_all, ti_all, ymix = saved
    rows = h.shape[0]
    tr = _pick(rows, (384, 128))
    dymix = mm(dh, w_out, tb=True, name="mix_out_dx")
    d_w_out = mm(ymix, dh, ta=True, out_dtype=BF16, name="mix_out_dw")
    do, dz, d_dnw = dn_out_bwd(o, p, dnw, dymix)
    dq, dk, dv, dbg = dn_bwd(qkv_n, bgcol, bgrow, s_all, ti_all, do)
    dbg_p, d_alog, d_dtb = bg_bwd(p, alog, dtb, dbg)
    dcq = dn_qkv_bwd(cq, dq, dk, dv)
    dqkv, d_dc = conv_bwd([(p, 3)], dc8, 4, dcq, rows=rows, c=3 * DN_DIM, tc=DN_DIM, tr=tr, name="dn_conv_bwd",
                          post=lambda dx: dx, outs=[BF16])
    dgo, dcv = conv_a_pre_bwd(dymix, cv, p)
    dgi, dah, d_ca = conv_bwd([(p, 0), (p, 2)], ca8, 3, dcv, rows=rows, c=D_CONV, tc=D_CONV, tr=tr,
                              name="conv_a_bwd", pre=lambda gi, ah: gi * ah,
                              post=lambda dm, gi, ah: (dm * ah, dm * gi), extras=[(p, 0), (p, 2)], outs=[BF16, BF16])
    dp = jnp.concatenate([dgi, dgo, dah, dqkv, dz, dbg_p], axis=1)
    dhn = mm(dp, w_in, tb=True, name="mix_in_dx")
    d_w_in = mm(hn, dp, ta=True, out_dtype=BF16, name="mix_in_dw")
    dh_new, d_nw = rms_bwd(h, nw, dhn, dh, name="mix_norm_bwd")
    return dh_new, d_nw, d_w_in, d_ca, d_dc, d_alog, d_dtb, d_dnw, d_w_out


def swa_layer_fwd(h, nw, wqkv, qw, kw, sinks, wo):
    hn = rms_fwd(h, nw, name="swa_norm")
    qkv = mm(hn, wqkv, name="swa_qkv")
    qh, kh, vh = qknorm_fwd(qkv, qw, kw)
    att = swa_fwd(qh, kh, vh, sinks)
    out = mm(att, wo, add=h, name="swa_out")
    return out, (hn, qkv, qh, kh, vh, att)


def swa_layer_bwd(h, nw, wqkv, qw, kw, sinks, wo, saved, dh):
    hn, qkv, qh, kh, vh, att = saved
    datt = mm(dh, wo, tb=True, out_dtype=BF16, name="swa_out_dx")
    d_wo = mm(att, dh, ta=True, out_dtype=BF16, name="swa_out_dw")
    dqh, dkh, dvh, dsk = swa_bwd(qh, kh, vh, sinks, datt)
    dqkv, d_qw, d_kw = qknorm_bwd(qkv, qw, kw, dqh, dkh, dvh)
    dhn = mm(dqkv, wqkv, tb=True, name="swa_qkv_dx")
    d_wqkv = mm(hn, dqkv, ta=True, out_dtype=BF16, name="swa_qkv_dw")
    dh_new, d_nw = rms_bwd(h, nw, dhn, dh, name="swa_norm_bwd")
    d_sinks = jnp.sum(dsk[:, :, 0], axis=0)
    return dh_new, d_nw, d_wqkv, d_qw, d_kw, d_sinks, d_wo


BIG = ("mix_w_in", "mix_w_out", "swa_wq", "swa_wk", "swa_wv", "swa_wo", "ffn_w_up", "ffn_w_down")


def _flat_pad(parts, rows):
    v = jnp.concatenate([t.astype(F32).reshape(-1) for t in parts])
    return jnp.pad(v, (0, rows * 1024 - v.shape[0])).reshape(rows, 1024)


def _split_flat(flat, shapes):
    v = flat.reshape(-1)
    out, o = [], 0
    for s in shapes:
        n = 1
        for d_ in s:
            n *= d_
        out.append(v[o:o + n].reshape(s))
        o += n
    return out


def local_step(x0, target0, meta_full, anw, fnw, w_in, ca8, dc8, alog, dtb, dnw, w_out, wqkv, qw, kw, sinks, wo,
               w_up, fc8, w_down, begin=None):
    begin = begin or (lambda tag, names, grads: None)
    h0 = jnp.concatenate([jnp.zeros((PAD, D), F32), meta_full, x0], axis=0)
    h1, s_mix = mixer_fwd(h0, anw[0], w_in, ca8, dc8, alog, dtb, dnw, w_out)
    h2, s_f0 = ffn_fwd(h1, fnw[0], w_up[0], fc8[0], w_down[0], 0)
    h3, s_swa = swa_layer_fwd(h2, anw[1], wqkv, qw, kw, sinks, wo)
    h4, s_f1 = ffn_fwd(h3, fnw[1], w_up[1], fc8[1], w_down[1], 1)
    dh, loss_l = loss_grad(h4, target0)
    dh, d_fnw1, d_up1, d_fc1, d_down1 = ffn_bwd(h3, fnw[1], w_up[1], fc8[1], w_down[1], s_f1, dh, 1)
    begin("ffn1", ("up1", "down1"), [d_up1, d_down1.reshape(4, 704, D)])
    dh, d_anw1, d_wqkv, d_qw, d_kw, d_sinks, d_wo = swa_layer_bwd(h2, anw[1], wqkv, qw, kw, sinks, wo, s_swa, dh)
    begin("swa", ("wq", "wk", "wv", "wo"),
          [d_wqkv[:, :D].reshape(4, 256, D), d_wqkv[:, D:D + 256].reshape(4, 256, 256),
           d_wqkv[:, D + 256:].reshape(4, 256, 256), d_wo.reshape(4, 256, D)])
    dh, d_fnw0, d_up0, d_fc0, d_down0 = ffn_bwd(h1, fnw[0], w_up[0], fc8[0], w_down[0], s_f0, dh, 0)
    begin("ffn0", ("up0", "down0"), [d_up0, d_down0.reshape(4, 704, D)])
    dh, d_anw0, d_w_in, d_ca, d_dc, d_alog, d_dtb, d_dnw, d_w_out = mixer_bwd(
        h0, anw[0], w_in, ca8, dc8, alog, dtb, dnw, w_out, s_mix, dh)
    begin("mix", ("w_in", "w_out"),
          [d_w_in[:, :IN_DIM].reshape(D, 4, 898).transpose(1, 0, 2), d_w_out.reshape(4, 256, D)])
    return (dh, loss_l, d_anw0, d_anw1, d_fnw0, d_fnw1, d_w_in, d_ca, d_dc, d_alog, d_dtb, d_dnw, d_w_out, d_wqkv,
            d_qw, d_kw, d_sinks, d_wo, d_up0, d_up1, d_fc0, d_fc1, d_down0, d_down1)


def kernel(x, meta_tokens, attn_norm_w, ffn_norm_w, mix_w_in, conv_a_w, dn_conv_w, dn_a_log, dn_dt_bias, dn_norm_w, mix_w_out, swa_wq, swa_wk, swa_wv, swa_q_norm_w, swa_k_norm_w, swa_sinks, swa_wo, ffn_w_up, ffn_conv_w, ffn_w_down, loss_target, m_meta_tokens, m_attn_norm_w, m_ffn_norm_w, m_mix_w_in, m_conv_a_w, m_dn_conv_w, m_dn_a_log, m_dn_dt_bias, m_dn_norm_w, m_mix_w_out, m_swa_wq, m_swa_wk, m_swa_wv, m_swa_q_norm_w, m_swa_k_norm_w, m_swa_sinks, m_swa_wo, m_ffn_w_up, m_ffn_conv_w, m_ffn_w_down, v_meta_tokens, v_attn_norm_w, v_ffn_norm_w, v_mix_w_in, v_conv_a_w, v_dn_conv_w, v_dn_a_log, v_dn_dt_bias, v_dn_norm_w, v_mix_w_out, v_swa_wq, v_swa_wk, v_swa_wv, v_swa_q_norm_w, v_swa_k_norm_w, v_swa_sinks, v_swa_wo, v_ffn_w_up, v_ffn_conv_w, v_ffn_w_down):
    ix, iy, ic = lax.axis_index("x"), lax.axis_index("y"), lax.axis_index("c")
    chip = 2 * ix + iy
    seq = x.shape[1]
    rows = HEAD0 + seq

    small_sharded = (conv_a_w, dn_conv_w, ffn_conv_w, meta_tokens)
    up_b, down_b = ffn_w_up.astype(BF16), ffn_w_down.astype(BF16)
    own = [mix_w_in[0].astype(BF16), mix_w_out[0].astype(BF16), swa_wq[0].astype(BF16), swa_wk[0].astype(BF16),
           swa_wv[0].astype(BF16), swa_wo[0].astype(BF16), up_b[0], up_b[1], down_b[0], down_b[1]]
    first, g_small = gather_weights(own[:2], _flat_pad(small_sharded, SW_ROWS))
    gathered = list(first) + list(gather_weights_beside(own[2:]))
    g_in, g_out, g_q, g_k, g_v, g_o, g_up0, g_up1, g_dn0, g_dn1 = [
        lax.dynamic_update_slice_in_dim(g, t[None], chip, axis=0) for g, t in zip(gathered, own)]
    w_in = jnp.pad(g_in.transpose(1, 0, 2).reshape(D, IN_DIM), ((0, 0), (0, P_W - IN_DIM)))
    w_out, wo = g_out.reshape(D, D), g_o.reshape(D, D)
    wqkv = jnp.concatenate([g_q.reshape(D, D), g_k.reshape(D, 256), g_v.reshape(D, 256)], axis=1)
    w_up = [g_up0, g_up1]
    w_down = [g_dn0.reshape(D_FF, D), g_dn1.reshape(D_FF, D)]
    gs = g_small.reshape(4, -1)
    ca_full = gs[:, 0:384].reshape(4, 3, 128).transpose(1, 0, 2).reshape(3, D_CONV)
    dc_full = gs[:, 384:1920].reshape(4, 4, 384).transpose(1, 0, 2).reshape(4, 3 * DN_DIM)
    fc_full = gs[:, 1920:6144].reshape(4, 2, 3, 704).transpose(1, 2, 0, 3).reshape(2, 3, D_FF)
    meta_full = gs[:, 6144:10240].reshape(4, N_META, 256).transpose(1, 0, 2).reshape(N_META, D)
    ca8, dc8 = _rows8(ca_full), _rows8(dc_full)
    fc8 = [_rows8(fc_full[0]), _rows8(fc_full[1])]
    alog, dtb = _lanes(dn_a_log[0], 4), _lanes(dn_dt_bias[0], 4)
    dnw = dn_norm_w.astype(F32)
    qw, kw = swa_q_norm_w.astype(F32), swa_k_norm_w.astype(F32)
    sinks = swa_sinks[0].astype(F32)
    anw = [attn_norm_w[0:1], attn_norm_w[1:2]]
    fnw = [ffn_norm_w[0:1], ffn_norm_w[1:2]]

    c_idx = jnp.reshape(ic, (1,)).astype(jnp.int32)
    chip_idx = jnp.stack([chip, ic]).astype(jnp.int32)
    begun = []

    def begin(tag, names, grads):
        pairs, gots = reduce_begin(grads, names, c_idx, 2 + len(begun), tag)
        begun.append((names, pairs, gots))

    (dh, loss_l, d_anw0, d_anw1, d_fnw0, d_fnw1, d_w_in, d_ca, d_dc, d_alog, d_dtb, d_dnw, d_w_out, d_wqkv, d_qw,
     d_kw, d_sinks, d_wo, d_up0, d_up1, d_fc0, d_fc1, d_down0, d_down1) = local_step(
        x[0], loss_target[0], meta_full, anw, fnw, w_in, ca8, dc8, alog, dtb, dnw, w_out, wqkv, qw, kw, sinks, wo,
        w_up, fc8, w_down, begin)
    grad_x = dh[HEAD0:][None]

    small_parts = [jnp.concatenate([d_anw0, d_anw1], axis=0), jnp.concatenate([d_fnw0, d_fnw1], axis=0),
                   d_alog[0, 4:8], d_dtb[0, 4:8], d_dnw, d_qw, d_kw, d_sinks,
                   d_ca[:3], d_dc[:4], jnp.stack([d_fc0[:3], d_fc1[:3]]), dh[PAD:HEAD0], loss_l[0, 0:1]]
    small_shapes = [(2, D), (2, D), (1, 4), (1, 4), (1, DN_D), (1, SWA_D), (1, SWA_D), (1, SWA_H),
                    (1, 3, D_CONV), (1, 4, 3 * DN_DIM), (2, 3, D_FF), (N_META, D), ()]
    red = sum_slots(gather_small(_flat_pad(small_parts, SV_ROWS)))
    (g_anw, g_fnw, g_alog, g_dtb, g_dnw, g_qw, g_kw, g_sinks, g_ca_f, g_dc_f, g_fc_f, g_meta_f,
     loss) = _split_flat(red, small_shapes)
    g_ca = lax.dynamic_slice_in_dim(g_ca_f, chip * 128, 128, axis=2)
    g_dc = lax.dynamic_slice_in_dim(g_dc_f, chip * 384, 384, axis=2)
    g_fc = lax.dynamic_slice_in_dim(g_fc_f, chip * 704, 704, axis=2)
    g_meta = lax.dynamic_slice_in_dim(g_meta_f, chip * 256, 256, axis=1)

    all_names = [n for names, _, _ in begun for n in names]
    red_big = dict(zip(all_names, reduce_end([p for _, ps, _ in begun for p in ps],
                                             [g for _, _, gs_ in begun for g in gs_], all_names, chip_idx)))
    g_w_in, g_w_out, g_wq, g_wk, g_wv, g_wo, g_up0, g_up1, g_dn0, g_dn1 = [
        red_big[n] for n in ("w_in", "w_out", "wq", "wk", "wv", "wo", "up0", "up1", "down0", "down1")]

    grads = dict(meta_tokens=g_meta, attn_norm_w=g_anw, ffn_norm_w=g_fnw, mix_w_in=g_w_in, conv_a_w=g_ca,
                 dn_conv_w=g_dc, dn_a_log=g_alog, dn_dt_bias=g_dtb, dn_norm_w=g_dnw, mix_w_out=g_w_out,
                 swa_wq=g_wq, swa_wk=g_wk, swa_wv=g_wv, swa_q_norm_w=g_qw, swa_k_norm_w=g_kw, swa_sinks=g_sinks,
                 swa_wo=g_wo, ffn_w_up=[g_up0, g_up1], ffn_conv_w=g_fc, ffn_w_down=[g_dn0, g_dn1])
    weights = dict(meta_tokens=meta_tokens, attn_norm_w=attn_norm_w, ffn_norm_w=ffn_norm_w, mix_w_in=mix_w_in,
                   conv_a_w=conv_a_w, dn_conv_w=dn_conv_w, dn_a_log=dn_a_log, dn_dt_bias=dn_dt_bias,
                   dn_norm_w=dn_norm_w, mix_w_out=mix_w_out, swa_wq=swa_wq, swa_wk=swa_wk, swa_wv=swa_wv,
                   swa_q_norm_w=swa_q_norm_w, swa_k_norm_w=swa_k_norm_w, swa_sinks=swa_sinks, swa_wo=swa_wo,
                   ffn_w_up=ffn_w_up, ffn_conv_w=ffn_conv_w, ffn_w_down=ffn_w_down)
    m_in = dict(meta_tokens=m_meta_tokens, attn_norm_w=m_attn_norm_w, ffn_norm_w=m_ffn_norm_w, mix_w_in=m_mix_w_in,
                conv_a_w=m_conv_a_w, dn_conv_w=m_dn_conv_w, dn_a_log=m_dn_a_log, dn_dt_bias=m_dn_dt_bias,
                dn_norm_w=m_dn_norm_w, mix_w_out=m_mix_w_out, swa_wq=m_swa_wq, swa_wk=m_swa_wk, swa_wv=m_swa_wv,
                swa_q_norm_w=m_swa_q_norm_w, swa_k_norm_w=m_swa_k_norm_w, swa_sinks=m_swa_sinks, swa_wo=m_swa_wo,
                ffn_w_up=m_ffn_w_up, ffn_conv_w=m_ffn_conv_w, ffn_w_down=m_ffn_w_down)
    v_in = dict(meta_tokens=v_meta_tokens, attn_norm_w=v_attn_norm_w, ffn_norm_w=v_ffn_norm_w, mix_w_in=v_mix_w_in,
                conv_a_w=v_conv_a_w, dn_conv_w=v_dn_conv_w, dn_a_log=v_dn_a_log, dn_dt_bias=v_dn_dt_bias,
                dn_norm_w=v_dn_norm_w, mix_w_out=v_mix_w_out, swa_wq=v_swa_wq, swa_wk=v_swa_wk, swa_wv=v_swa_wv,
                swa_q_norm_w=v_swa_q_norm_w, swa_k_norm_w=v_swa_k_norm_w, swa_sinks=v_swa_sinks, swa_wo=v_swa_wo,
                ffn_w_up=v_ffn_w_up, ffn_conv_w=v_ffn_conv_w, ffn_w_down=v_ffn_w_down)
    names = list(weights)
    small = [n for n in names if n not in BIG]
    delta, new_m, new_v = {}, {}, {}
    for n in BIG:
        delta[n], new_m[n], new_v[n], grads[n] = adamw(weights[n], grads[n], m_in[n], v_in[n], name=f"adamw_{n}")
    grads = {n: grads[n].reshape(weights[n].shape) for n in names}
    shapes = [weights[n].shape for n in small]
    packed = [_flat_pad([t[n] for n in small], SW_ROWS) for t in (weights, grads, m_in, v_in)]
    for store, flat in zip((delta, new_m, new_v), adamw(*packed, name="adamw_small")):
        for n, t in zip(small, _split_flat(flat, shapes)):
            store[n] = t
    return (loss, grad_x, *[grads[n] for n in names], *[delta[n] for n in names],
            *[new_m[n] for n in names], *[new_v[n] for n in names])
```

```python
import functools

import jax
import jax.numpy as jnp
from jax import lax
from jax.experimental import pallas as pl
from jax.experimental.pallas import tpu as pltpu
from jax.experimental.pallas import tpu_sc as plsc

F32 = jnp.float32
BF16 = jnp.bfloat16
HI = lax.Precision.HIGHEST
MESH = pl.DeviceIdType.MESH

D = 1024
N_META = 16
PAD = 112
HEAD0 = PAD + N_META
D_CONV = 512
DN_H = 4
DN_D = 128
DN_DIM = 512
CH = 64
IN_DIM = 3592
P_W = 3840
BG0 = 3584
SWA_H = 16
SWA_KV = 4
SWA_D = 64
BLK = 128
D_FF = 2816
EPS = 1e-6
LR, B1, B2, AEPS, WD, STEP = 0.001, 0.9, 0.999, 1e-08, 0.01, 10
VMEM_LIMIT = 48 * 1024 * 1024
MM_VMEM_BUDGET = 34 * 1024 * 1024
R_BIG = 6144
R_HALF = R_BIG // 2
SV_ROWS = 48
SW_ROWS = 16


def _pick(n, cands):
    for c in cands:
        if n % c == 0:
            return c
    return n


def _params(sem=None):
    return pltpu.CompilerParams(dimension_semantics=sem, vmem_limit_bytes=VMEM_LIMIT)


def _dot(a, b, ca=1, cb=0, prec=None):
    return lax.dot_general(a, b, (((ca,), (cb,)), ((), ())), precision=prec,
                           preferred_element_type=F32)


def _sigmoid(x):
    return 1.0 / (1.0 + jnp.exp(-x))


def _silu(x):
    return x * _sigmoid(x)


def _dsilu(x):
    s = _sigmoid(x)
    return s * (1.0 + x * (1.0 - s))


def _softplus(x):
    return jnp.maximum(x, 0.0) + jnp.log(1.0 + jnp.exp(-jnp.abs(x)))


def mm(a, b, *, name, ta=False, tb=False, out_dtype=F32, add=None, tm=None, tn=None, tk=None,
       b_chip=False, out_chip=False, epi=None, epi_ins=(), epi_consts=(), epi_outs=(), epi_accs=()):
    if epi is not None:
        return _mm_epi(a, b, name=name, tb=tb, tn=tn, b_chip=b_chip, epi=epi, epi_ins=epi_ins,
                       epi_consts=epi_consts, epi_outs=epi_outs, epi_accs=epi_accs)
    m, k = (a.shape[1], a.shape[0]) if ta else a.shape
    if b_chip:
        n = b.shape[1] if tb else 4 * b.shape[2]
        if tb:
            tk = b.shape[2]
        else:
            tn = b.shape[2]
    else:
        n = b.shape[0] if tb else b.shape[1]
    if out_chip:
        tn = n // 4
    tn = tn or _pick(n, (1408, 1024, 768, 512, 256, 128))
    tk = tk or (_pick(k, (1408, 704, 384, 128)) if ta else _pick(k, (1024, 1408, 768, 512, 128)))
    nk = k // tk
    if tm is None:
        isz = lambda t: jnp.dtype(t.dtype).itemsize
        osz = jnp.dtype(out_dtype).itemsize
        for tm in ((1408, 1024, 512, 384, 256, 128) if ta else (1408, 704, 512, 384, 256, 128)):
            need = 2 * (tm * tk * isz(a) + tk * tn * isz(b) + tm * tn * osz + (tm * tn * 4 if add is not None else 0))
            need += tm * tn * 4 if nk > 1 else 0
            if m % tm == 0 and need <= MM_VMEM_BUDGET:
                break
        else:
            tm = m
    dims = (((0 if ta else 1,), (1 if tb else 0,)), ((), ()))

    def body(*refs):
        if add is None:
            a_ref, b_ref, o_ref, acc_ref = refs
            add_ref = None
        else:
            a_ref, b_ref, add_ref, o_ref, acc_ref = refs
        part = lax.dot_general(a_ref[...].astype(BF16), b_ref[...].astype(BF16), dims,
                               preferred_element_type=F32)

        def finish(total):
            if add_ref is not None:
                total = total + add_ref[...]
            o_ref[...] = total.astype(out_dtype)

        if nk == 1:
            finish(part)
        else:
            kk = pl.program_id(2)

            @pl.when(kk == 0)
            def _():
                acc_ref[...] = part

            @pl.when(kk > 0)
            def _():
                acc_ref[...] += part

            @pl.when(kk == nk - 1)
            def _():
                finish(acc_ref[...])

    a_spec = pl.BlockSpec((tk, tm), lambda i, j, kk: (kk, i)) if ta else pl.BlockSpec((tm, tk), lambda i, j, kk: (i, kk))
    if b_chip and tb:
        b_spec = pl.BlockSpec((None, tn, tk), lambda i, j, kk: (kk, j, 0))
    elif b_chip:
        b_spec = pl.BlockSpec((None, tk, tn), lambda i, j, kk: (j, kk, 0))
    elif tb:
        b_spec = pl.BlockSpec((tn, tk), lambda i, j, kk: (j, kk))
    else:
        b_spec = pl.BlockSpec((tk, tn), lambda i, j, kk: (kk, j))
    o_spec = pl.BlockSpec((tm, tn), lambda i, j, kk: (i, j))
    in_specs = [a_spec, b_spec] + ([o_spec] if add is not None else [])
    args = [a, b] + ([add] if add is not None else [])
    out_spec = pl.BlockSpec((None, tm, tn), lambda i, j, kk: (j, i, 0)) if out_chip else o_spec
    return pl.pallas_call(
        body, name=name, interpret=False,
        out_shape=jax.ShapeDtypeStruct((4, m, tn) if out_chip else (m, n), out_dtype),
        grid=(m // tm, n // tn, nk), in_specs=in_specs, out_specs=out_spec,
        scratch_shapes=[pltpu.VMEM((tm, tn) if nk > 1 else (8, 128), F32)],
        compiler_params=_params(("parallel", "parallel", "arbitrary")),
    )(*args)


def _mm_epi(a, b, *, name, tb, tn, b_chip, epi, epi_ins, epi_consts, epi_outs, epi_accs):
    m, k = a.shape
    if b_chip:
        n = b.shape[1] if tb else 4 * b.shape[2]
        tk = b.shape[2] if tb else None
        tn = tn if tb else b.shape[2]
    else:
        n = b.shape[0] if tb else b.shape[1]
        tk = None
    tn = tn or _pick(n, (1408, 1024, 768, 512, 256, 128))
    tk = tk or _pick(k, (1024, 1408, 768, 512, 128))
    nk, nj = k // tk, n // tn
    isz = lambda t: jnp.dtype(t.dtype if hasattr(t, "dtype") else t).itemsize
    side = sum(isz(t) for t, _ in epi_ins) + sum(isz(t) for t in epi_outs)
    for tm in (1408, 704, 512, 384, 256, 128):
        need = 2 * (tm * tk * isz(a) + tk * tn * isz(b) + tm * tn * side) + (tm * tn * 4 if nk > 1 else 0)
        if m % tm == 0 and need <= MM_VMEM_BUDGET:
            break
    else:
        tm = m
    dims = (((1,), (1 if tb else 0,)), ((), ()))
    n_in, n_c, n_out, n_acc = len(epi_ins), len(epi_consts), len(epi_outs), len(epi_accs)

    def body(*refs):
        a_ref, b_ref = refs[:2]
        in_refs = refs[2:2 + n_in + n_c]
        out_refs = refs[2 + n_in + n_c:2 + n_in + n_c + n_out]
        acc_out = refs[2 + n_in + n_c + n_out:2 + n_in + n_c + n_out + n_acc]
        acc_ref = refs[-1]
        i, j, kk = pl.program_id(0), pl.program_id(1), pl.program_id(2)
        part = lax.dot_general(a_ref[...].astype(BF16), b_ref[...].astype(BF16), dims,
                               preferred_element_type=F32)

        def finish(total):
            res = epi(i * tm, total, *[r[...] for r in in_refs])
            if not isinstance(res, (tuple, list)):
                res = (res,)
            for r, v in zip(out_refs, res[:n_out]):
                r[...] = v.astype(r.dtype)
            if n_acc:
                @pl.when(jnp.logical_and(i == 0, j == 0))
                def _():
                    for r in acc_out:
                        r[...] = jnp.zeros(r.shape, r.dtype)

                for r, v in zip(acc_out, res[n_out:]):
                    r[...] += jnp.broadcast_to(v, r.shape).astype(r.dtype)

        if nk == 1:
            finish(part)
        else:
            @pl.when(kk == 0)
            def _():
                acc_ref[...] = part

            @pl.when(kk > 0)
            def _():
                acc_ref[...] += part

            @pl.when(kk == nk - 1)
            def _():
                finish(acc_ref[...])

    a_spec = pl.BlockSpec((tm, tk), lambda i, j, kk: (i, kk))
    if b_chip and tb:
        b_spec = pl.BlockSpec((None, tn, tk), lambda i, j, kk: (kk, j, 0))
    elif b_chip:
        b_spec = pl.BlockSpec((None, tk, tn), lambda i, j, kk: (j, kk, 0))
    elif tb:
        b_spec = pl.BlockSpec((tn, tk), lambda i, j, kk: (j, kk))
    else:
        b_spec = pl.BlockSpec((tk, tn), lambda i, j, kk: (kk, j))
    in_specs = [a_spec, b_spec]
    in_specs += [pl.BlockSpec((tm, tn), lambda i, j, kk, col=col: (i, col(j))) for _, col in epi_ins]
    in_specs += [pl.BlockSpec(t.shape, lambda i, j, kk, nd=t.ndim: (0,) * nd) for t in epi_consts]
    out_specs = [pl.BlockSpec((tm, tn), lambda i, j, kk: (i, j)) for _ in epi_outs]
    out_specs += [pl.BlockSpec(s, lambda i, j, kk, nd=len(s): (0,) * nd) for s, _ in epi_accs]
    out_shape = [jax.ShapeDtypeStruct((m, n), dt) for dt in epi_outs]
    out_shape += [jax.ShapeDtypeStruct(s, dt) for s, dt in epi_accs]
    sem = ("arbitrary", "arbitrary", "arbitrary") if n_acc else ("parallel", "parallel", "arbitrary")
    return pl.pallas_call(
        body, name=name, interpret=False, out_shape=out_shape,
        grid=(m // tm, nj, nk), in_specs=in_specs, out_specs=out_specs,
        scratch_shapes=[pltpu.VMEM((tm, tn) if nk > 1 else (8, 128), F32)],
        compiler_params=_params(sem),
    )(a, b, *[t for t, _ in epi_ins], *epi_consts)


def cols(arr, tr, width=None, cb=0):
    width = width or arr.shape[1]
    return (arr, (tr, width), lambda i: (i, cb), "r2")


def heads(arr, tr):
    return (arr, (arr.shape[0], tr, arr.shape[2]), lambda i: (0, i, 0), "r3")


def whole(arr):
    nd = arr.ndim
    return (arr, arr.shape, lambda i: (0,) * nd, "w")


STRIP = 16


def _rows_of(ref, kind, r0, n):
    if kind == "r2":
        return ref[pl.ds(r0, n), :]
    if kind == "r3":
        return ref[:, pl.ds(r0, n), :]
    return ref[...]


def _set_rows(ref, kind, r0, n, v):
    if kind == "r2":
        ref[pl.ds(r0, n), :] = v.astype(ref.dtype)
    elif kind == "r3":
        ref[:, pl.ds(r0, n), :] = v.astype(ref.dtype)
    else:
        ref[...] = v.astype(ref.dtype)


def rowwise(fn, ins, outs, *, steps, name, accs=(), strip=None):
    n_in, n_out, n_acc = len(ins), len(outs), len(accs)
    kin = [t[3] for t in ins]
    kout = [t[4] for t in outs]
    tr = next((t[1][0] if t[3] == "r2" else t[1][1] for t in ins if t[3] != "w"), 0)

    def body(*refs):
        i = pl.program_id(0)
        in_refs, out_refs, acc_refs = refs[:n_in], refs[n_in:n_in + n_out], refs[n_in + n_out:]
        if n_acc:
            @pl.when(i == 0)
            def _():
                for r in acc_refs:
                    r[...] = jnp.zeros(r.shape, r.dtype)

        def run(r0, n):
            res = fn(i * tr + r0, *[_rows_of(r, k, r0, n) for r, k in zip(in_refs, kin)])
            if not isinstance(res, (tuple, list)):
                res = (res,)
            for r, k, v in zip(out_refs, kout, res[:n_out]):
                _set_rows(r, k, r0, n, v)
            for r, v in zip(acc_refs, res[n_out:]):
                r[...] += jnp.broadcast_to(v, r.shape).astype(r.dtype)

        if strip is None or tr <= strip:
            run(0, tr)
        else:
            def step(s, carry):
                run(pl.multiple_of(s * strip, strip), strip)
                return carry
            lax.fori_loop(0, tr // strip, step, 0)

    def zmap(nd):
        return lambda i: (0,) * nd

    in_specs = [pl.BlockSpec(t[1], t[2]) for t in ins]
    out_specs = [pl.BlockSpec(t[2], t[3]) for t in outs]
    out_specs += [pl.BlockSpec(s, zmap(len(s))) for s, _ in accs]
    out_shape = [jax.ShapeDtypeStruct(t[0], t[1]) for t in outs]
    out_shape += [jax.ShapeDtypeStruct(s, d) for s, d in accs]
    res = pl.pallas_call(
        body, name=name, interpret=False, out_shape=out_shape, grid=(steps,),
        in_specs=in_specs, out_specs=out_specs,
        compiler_params=_params(("arbitrary",)),
    )(*[t[0] for t in ins])
    return res


def out2d(rows, width, dtype, tr):
    return ((rows, width), dtype, (tr, width), lambda i: (i, 0), "r2")


def conv_fwd(xs, w8, kw, *, rows, c, tc, tr, name, post, extras=(), outs=(), pre=None):
    nx, ne, no = len(xs), len(extras), len(outs)
    nr, nc = rows // tr, c // tc
    r8 = tr // 8

    def body(*refs):
        x_refs = refs[:2 * nx]
        w_ref = refs[2 * nx]
        e_refs = refs[2 * nx + 1:2 * nx + 1 + ne]
        o_refs = refs[2 * nx + 1 + ne:2 * nx + 1 + ne + no]
        scr = refs[-1]
        j, i = pl.program_id(0), pl.program_id(1)
        halo = [x_refs[2 * q + 1][...].astype(F32) for q in range(nx)]
        scr[0:8, :] = jnp.where(i > 0, pre(*halo) if pre else halo[0], 0.0)

        def fill(s, carry):
            r0 = pl.multiple_of(s * STRIP, STRIP)
            cur = [x_refs[2 * q][pl.ds(r0, STRIP), :].astype(F32) for q in range(nx)]
            scr[pl.ds(8 + r0, STRIP), :] = pre(*cur) if pre else cur[0]
            return carry

        def comp(s, carry):
            r0 = pl.multiple_of(s * STRIP, STRIP)
            win = scr[pl.ds(r0, STRIP + 8), :]
            y = jnp.zeros((STRIP, tc), F32)
            for q in range(kw):
                sh = kw - 1 - q
                y = y + w_ref[q:q + 1, :] * win[8 - sh:8 - sh + STRIP]
            res = post(j, y, *[e[pl.ds(r0, STRIP), :] for e in e_refs])
            if not isinstance(res, (tuple, list)):
                res = (res,)
            for r, v in zip(o_refs, res):
                r[pl.ds(r0, STRIP), :] = v.astype(r.dtype)
            return carry

        lax.fori_loop(0, tr // STRIP, fill, 0)
        lax.fori_loop(0, tr // STRIP, comp, 0)

    in_specs, args = [], []
    for arr, cb0 in xs:
        in_specs.append(pl.BlockSpec((tr, tc), lambda j, i, cb0=cb0: (i, cb0 + j)))
        in_specs.append(pl.BlockSpec((8, tc), lambda j, i, cb0=cb0: (jnp.maximum(i * r8 - 1, 0), cb0 + j)))
        args += [arr, arr]
    in_specs.append(pl.BlockSpec((8, tc), lambda j, i: (0, j)))
    args.append(w8)
    for arr, cb0 in extras:
        in_specs.append(pl.BlockSpec((tr, tc), lambda j, i, cb0=cb0: (i, cb0 + j)))
        args.append(arr)
    return pl.pallas_call(
        body, name=name, interpret=False,
        out_shape=[jax.ShapeDtypeStruct((rows, c), dt) for dt in outs],
        grid=(nc, nr), in_specs=in_specs,
        out_specs=[pl.BlockSpec((tr, tc), lambda j, i: (i, j)) for _ in outs],
        scratch_shapes=[pltpu.VMEM((tr + 8, tc), F32)],
        compiler_params=_params(("parallel", "arbitrary")),
    )(*args)


def conv_bwd(xs, w8, kw, dy, *, rows, c, tc, tr, name, post, extras=(), outs=(), pre=None):
    nx, ne, no = len(xs), len(extras), len(outs)
    nr, nc = rows // tr, c // tc
    r8 = tr // 8

    def body(*refs):
        x_refs = refs[:2 * nx]
        w_ref, dy_ref, dyn_ref = refs[2 * nx:2 * nx + 3]
        e_refs = refs[2 * nx + 3:2 * nx + 3 + ne]
        o_refs = refs[2 * nx + 3 + ne:2 * nx + 3 + ne + no]
        dw_ref = refs[2 * nx + 3 + ne + no]
        xscr, gscr = refs[-2], refs[-1]
        i = pl.program_id(1)
        halo = [x_refs[2 * q + 1][...].astype(F32) for q in range(nx)]
        xscr[0:8, :] = jnp.where(i > 0, pre(*halo) if pre else halo[0], 0.0)
        gscr[tr:tr + 8, :] = jnp.where(i < nr - 1, dyn_ref[...].astype(F32), 0.0)

        def fill(s, carry):
            r0 = pl.multiple_of(s * STRIP, STRIP)
            cur = [x_refs[2 * q][pl.ds(r0, STRIP), :].astype(F32) for q in range(nx)]
            xscr[pl.ds(8 + r0, STRIP), :] = pre(*cur) if pre else cur[0]
            gscr[pl.ds(r0, STRIP), :] = dy_ref[pl.ds(r0, STRIP), :].astype(F32)
            return carry

        def comp(s, dws):
            r0 = pl.multiple_of(s * STRIP, STRIP)
            gwin = gscr[pl.ds(r0, STRIP + 8), :]
            xwin = xscr[pl.ds(r0, STRIP + 8), :]
            g = gwin[0:STRIP]
            dx = jnp.zeros((STRIP, tc), F32)
            new = []
            for q in range(kw):
                sh = kw - 1 - q
                dx = dx + w_ref[q:q + 1, :] * gwin[sh:sh + STRIP]
                part = g * xwin[8 - sh:8 - sh + STRIP]
                new.append(dws[q] + part[0:8] + part[8:16])
            res = post(dx, *[e[pl.ds(r0, STRIP), :] for e in e_refs])
            if not isinstance(res, (tuple, list)):
                res = (res,)
            for r, v in zip(o_refs, res):
                r[pl.ds(r0, STRIP), :] = v.astype(r.dtype)
            return tuple(new)

        lax.fori_loop(0, tr // STRIP, fill, 0)
        dws = lax.fori_loop(0, tr // STRIP, comp, tuple(jnp.zeros((8, tc), F32) for _ in range(kw)))

        @pl.when(i == 0)
        def _():
            dw_ref[...] = jnp.zeros((8, tc), F32)

        dw_ref[...] += jnp.concatenate([jnp.sum(t, axis=0, keepdims=True) for t in dws]
                                       + [jnp.zeros((8 - kw, tc), F32)], axis=0)

    in_specs, args = [], []
    for arr, cb0 in xs:
        in_specs.append(pl.BlockSpec((tr, tc), lambda j, i, cb0=cb0: (i, cb0 + j)))
        in_specs.append(pl.BlockSpec((8, tc), lambda j, i, cb0=cb0: (jnp.maximum(i * r8 - 1, 0), cb0 + j)))
        args += [arr, arr]
    in_specs.append(pl.BlockSpec((8, tc), lambda j, i: (0, j)))
    in_specs.append(pl.BlockSpec((tr, tc), lambda j, i: (i, j)))
    in_specs.append(pl.BlockSpec((8, tc), lambda j, i: (jnp.minimum((i + 1) * r8, nr * r8 - 1), j)))
    args += [w8, dy, dy]
    for arr, cb0 in extras:
        in_specs.append(pl.BlockSpec((tr, tc), lambda j, i, cb0=cb0: (i, cb0 + j)))
        args.append(arr)
    return pl.pallas_call(
        body, name=name, interpret=False,
        out_shape=[jax.ShapeDtypeStruct((rows, c), dt) for dt in outs] + [jax.ShapeDtypeStruct((8, c), F32)],
        grid=(nc, nr), in_specs=in_specs,
        out_specs=[pl.BlockSpec((tr, tc), lambda j, i: (i, j)) for _ in outs] + [pl.BlockSpec((8, tc), lambda j, i: (0, j))],
        scratch_shapes=[pltpu.VMEM((tr + 8, tc), F32), pltpu.VMEM((tr + 8, tc), F32)],
        compiler_params=_params(("parallel", "arbitrary")),
    )(*args)


def rms_fwd(h, w, *, name):
    rows = h.shape[0]
    tr = _pick(rows, (384, 128))

    def fn(i, x, wv):
        r = lax.rsqrt(jnp.mean(x * x, axis=1, keepdims=True) + EPS)
        return x * r * wv

    return rowwise(fn, [cols(h, tr), whole(w)], [out2d(rows, D, BF16, tr)], steps=rows // tr, name=name)[0]


def _rms_bwd_epi(row0, g, x, dr, wv):
    r = lax.rsqrt(jnp.mean(x * x, axis=1, keepdims=True) + EPS)
    xh = x * r
    gw = g * wv
    dx = r * (gw - xh * jnp.mean(gw * xh, axis=1, keepdims=True))
    row = row0 + lax.broadcasted_iota(jnp.int32, (x.shape[0], 1), 0)
    return jnp.where(row >= PAD, dr + dx, 0.0), jnp.sum(g * xh, axis=0, keepdims=True)


def dx_rms_bwd(dy, w, h, nw, dres, *, name, b_chip=False):
    return mm(dy, w, tb=True, b_chip=b_chip, tn=D, name=name, epi=_rms_bwd_epi,
              epi_ins=[(h, lambda j: 0), (dres, lambda j: 0)], epi_consts=[nw], epi_outs=[F32],
              epi_accs=[((1, D), F32)])


def loss_grad(h, target):
    rows = h.shape[0]

    def fn(i, y, t):
        diff = jnp.where(i >= HEAD0, y - t, 0.0)
        part = jnp.sum(jnp.sum(diff * diff, axis=1, keepdims=True), axis=0, keepdims=True)
        return diff * (1.0 / D), part * (0.5 / D)

    tgt = (target, (BLK, D), lambda i: (jnp.maximum(i - 1, 0), 0), "r2")
    return rowwise(fn, [cols(h, BLK), tgt], [out2d(rows, D, F32, BLK)], steps=rows // BLK,
                   name="loss_grad", accs=[((1, 128), F32)])


def adamw(w, g, m, v, *, name):
    shape = w.shape
    gs = list(g) if isinstance(g, (list, tuple)) else [g]
    nl = len(gs)
    w2, m2, v2 = (t.reshape(-1, shape[-1]) for t in (w, m, v))
    rows, width = w2.shape
    rl = rows // nl
    tr = _pick(rl, (256, 176, 128, 64, 16, 8))
    nr = rl // tr

    def fn(i, wv, mv, vv, *gvs):
        gv = gvs[0]
        for layer in range(1, nl):
            gv = jnp.where(i >= layer * rl, gvs[layer], gv)
        mn = B1 * mv + (1.0 - B1) * gv
        vn = B2 * vv + (1.0 - B2) * gv * gv
        mh = mn / (1.0 - B1 ** STEP)
        vh = vn / (1.0 - B2 ** STEP)
        return -LR * (mh / (jnp.sqrt(vh) + AEPS) + WD * wv), mn, vn, gv

    g_ins = [(t.reshape(rl, width), (tr, width), lambda i, layer=layer: (jnp.clip(i - layer * nr, 0, nr - 1), 0), "r2")
             for layer, t in enumerate(gs)]
    res = rowwise(fn, [cols(t, tr) for t in (w2, m2, v2)] + g_ins, [out2d(rows, width, F32, tr)] * 4,
                  steps=rows // tr, name=name)
    return [r.reshape(shape) for r in res]


HB = DN_H * CH


def _split(a):
    hi = a.astype(BF16)
    return hi, (a - hi.astype(F32)).astype(BF16)


def _dot1(a, b, ca=1, cb=0):
    return _dot(a.astype(BF16), b.astype(BF16), ca, cb)


def _dot3(a, b, ca=1, cb=0):
    ah, al = _split(a)
    bh, bl = _split(b)
    return _dot(ah, bh, ca, cb) + (_dot(ah, bl, ca, cb) + _dot(al, bh, ca, cb))


def _dot01(m01, b, ca=1, cb=0):
    bh, bl = _split(b)
    m = m01.astype(BF16)
    return _dot(m, bh, ca, cb) + _dot(m, bl, ca, cb)


def _stack(x):
    return jnp.concatenate([x[:, h * DN_D:(h + 1) * DN_D] for h in range(DN_H)], axis=0)


def _unstack(x):
    return jnp.concatenate([x[h * CH:(h + 1) * CH] for h in range(DN_H)], axis=1)


def _tri_inv(a, blk, eye):
    ad = jnp.where(blk, a, 0.0)
    lo = a - ad
    a2 = _dot3(ad, ad)
    a4 = _dot3(a2, a2)
    a8 = _dot3(a4, a4)
    dgi = _dot3(_dot3(_dot3(eye - ad, eye + a2), eye + a4), eye + a8)
    n = _dot3(dgi, lo)
    return _dot3(_dot3(eye - n, eye + _dot3(n, n)), dgi)


def _dn_masks():
    row = lax.broadcasted_iota(jnp.int32, (HB, HB), 0)
    col = lax.broadcasted_iota(jnp.int32, (HB, HB), 1)
    same = (row // CH) == (col // CH)
    incl = jnp.logical_and(same, row >= col)
    strict = jnp.logical_and(same, row > col)
    upper = jnp.logical_and(same, row <= col)
    blk = (row // 16) == (col // 16)
    eye = (row == col).astype(F32)
    return incl, strict, upper, blk, eye


def _dn_chunk(q_ref, k_ref, v_ref, bc_ref, br_ref, incl, strict):
    r64 = lax.broadcasted_iota(jnp.int32, (CH, CH), 0)
    c64 = lax.broadcasted_iota(jnp.int32, (CH, CH), 1)
    bc = bc_ref[...]
    dcol = _dot01((r64 >= c64).astype(F32), bc)
    drow = _dot3(br_ref[0], (r64 <= c64).astype(F32))
    col = lambda m, l0: jnp.concatenate([m[:, l0 + h:l0 + h + 1] for h in range(DN_H)], axis=0)
    b_c = col(bc, 0)
    d_c = col(dcol, 4)
    d_r = jnp.concatenate([drow[4 + h:5 + h, :] for h in range(DN_H)], axis=1)
    d_last_h = [dcol[CH - 1:CH, 4 + h:5 + h] for h in range(DN_H)]
    d_last = jnp.concatenate([jnp.broadcast_to(t, (CH, 1)) for t in d_last_h], axis=0)
    q, k, v = _stack(q_ref[...]), _stack(k_ref[...]), _stack(v_ref[...])
    dm = jnp.where(incl, jnp.exp(jnp.where(incl, d_c - d_r, 0.0)), 0.0)
    kk = _dot1(k, k, 1, 1)
    a = jnp.where(strict, b_c * kk * dm, 0.0)
    ed = jnp.exp(d_c)
    rhs = jnp.concatenate([v * b_c, k * (b_c * ed)], axis=1)
    qk = _dot1(q, k, 1, 1) * dm
    ekd = jnp.exp(d_last - d_c)
    gl = [jnp.exp(t) for t in d_last_h]
    return q, k, v, b_c, dm, kk, a, ed, rhs, qk, ekd, gl


def dn_fwd(qkv_n, bgcol, bgrow):
    rows = qkv_n.shape[0]
    nch = rows // CH

    def body(q_ref, k_ref, v_ref, bc_ref, br_ref, o_ref, s_out, ti_out, s_scr, prep, prep_qk, prep_gl):
        n = pl.program_id(0)

        @pl.when(n == 0)
        def _():
            s_scr[...] = jnp.zeros(s_scr.shape, F32)
            prep[...] = jnp.zeros(prep.shape, F32)
            prep_qk[...] = jnp.zeros(prep_qk.shape, F32)
            prep_gl[...] = jnp.zeros(prep_gl.shape, F32)

        u, w, qd, kd = prep[0], prep[1], prep[2], prep[3]
        qk = prep_qk[...]
        live = n > 0
        v_new, o_state = [], []
        for h in range(DN_H):
            rs = slice(h * CH, (h + 1) * CH)
            s = s_scr[h]
            s_out[0, h] = s
            vn = u[rs] - _dot1(w[rs], s)
            v_new.append(vn)
            o_state.append(_dot1(qd[rs], s))
            s_scr[h] = jnp.where(live, prep_gl[h:h + 1, 0:1] * s + _dot1(kd[rs], vn, 0, 0), s)
        o = jnp.concatenate(o_state, axis=0) + _dot1(qk, jnp.concatenate(v_new, axis=0))
        o_ref[...] = _unstack(o)

        incl, strict, _, blk, eye = _dn_masks()
        q, k, v, b_c, dm, kk, a, ed, rhs, qk_n, ekd, gl = _dn_chunk(q_ref, k_ref, v_ref, bc_ref, br_ref, incl, strict)
        tinv = _tri_inv(a, blk, eye)
        ti_out[0] = tinv
        sol = _dot3(tinv, rhs)
        prep[0] = sol[:, :DN_D]
        prep[1] = sol[:, DN_D:]
        prep[2] = q * ed
        prep[3] = k * ekd
        prep_qk[...] = qk_n
        prep_gl[...] = jnp.concatenate([jnp.broadcast_to(t, (1, 128)) for t in gl]
                                       + [jnp.zeros((8 - DN_H, 128), F32)], axis=0)

    last = nch - 1
    return pl.pallas_call(
        body, name="dn_fwd", interpret=False,
        out_shape=[jax.ShapeDtypeStruct((rows, DN_DIM), F32),
                   jax.ShapeDtypeStruct((nch, DN_H, DN_D, DN_D), F32),
                   jax.ShapeDtypeStruct((nch, HB, HB), F32)],
        grid=(nch + 1,),
        in_specs=[pl.BlockSpec((CH, DN_DIM), lambda n: (jnp.minimum(n, last), 0)),
                  pl.BlockSpec((CH, DN_DIM), lambda n: (jnp.minimum(n, last), 1)),
                  pl.BlockSpec((CH, DN_DIM), lambda n: (jnp.minimum(n, last), 2)),
                  pl.BlockSpec((CH, 128), lambda n: (jnp.minimum(n, last), 0)),
                  pl.BlockSpec((1, 8, CH), lambda n: (jnp.minimum(n, last), 0, 0))],
        out_specs=[pl.BlockSpec((CH, DN_DIM), lambda n: (jnp.maximum(n - 1, 0), 0)),
                   pl.BlockSpec((1, DN_H, DN_D, DN_D), lambda n: (jnp.maximum(n - 1, 0), 0, 0, 0)),
                   pl.BlockSpec((1, HB, HB), lambda n: (jnp.minimum(n, last), 0, 0))],
        scratch_shapes=[pltpu.VMEM((DN_H, DN_D, DN_D), F32), pltpu.VMEM((4, HB, DN_D), F32),
                        pltpu.VMEM((HB, HB), F32), pltpu.VMEM((8, 128), F32)],
        compiler_params=_params(("arbitrary",)),
    )(qkv_n, qkv_n, qkv_n, bgcol, bgrow)


def dn_bwd(qkv_n, bgcol, bgrow, s_all, ti_all, do):
    rows = qkv_n.shape[0]
    nch = rows // CH

    def body(q_ref, k_ref, v_ref, bc_ref, br_ref, s_ref, ti_ref, do_ref, dq_ref, dk_ref, dv_ref, dbg_ref, ds_scr):
        n = pl.program_id(0)

        @pl.when(n == 0)
        def _():
            ds_scr[...] = jnp.zeros(ds_scr.shape, F32)

        incl, strict, upper, _, _ = _dn_masks()
        q, k, v, b_c, dm, kk, a, ed, rhs, qk, ekd, gl = _dn_chunk(q_ref, k_ref, v_ref, bc_ref, br_ref, incl, strict)
        tinv = ti_ref[0]
        g_o = _stack(do_ref[...])
        sol = _dot3(tinv, rhs)
        u, w = sol[:, :DN_D], sol[:, DN_D:]
        qd, kd = q * ed, k * ekd
        rsum = lambda t: jnp.sum(t, axis=1, keepdims=True)
        rows_of = [slice(h * CH, (h + 1) * CH) for h in range(DN_H)]
        s_h = [s_ref[0, h] for h in range(DN_H)]
        ds_h = [ds_scr[h] for h in range(DN_H)]
        v_new = jnp.concatenate([u[rs] - _dot1(w[rs], s) for rs, s in zip(rows_of, s_h)], axis=0)
        dv_new = _dot1(qk, g_o, 0, 0) + jnp.concatenate([_dot1(kd[rs], t) for rs, t in zip(rows_of, ds_h)], axis=0)
        dqd = jnp.concatenate([_dot1(g_o[rs], s, 1, 1) for rs, s in zip(rows_of, s_h)], axis=0)
        dkd = jnp.concatenate([_dot1(v_new[rs], t, 1, 1) for rs, t in zip(rows_of, ds_h)], axis=0)
        for h, rs in enumerate(rows_of):
            ds_scr[h] = _dot1(qd[rs], g_o[rs], 0, 0) + gl[h] * ds_h[h] - _dot1(w[rs], dv_new[rs], 0, 0)
        dw = jnp.concatenate([-_dot1(dv_new[rs], s, 1, 1) for rs, s in zip(rows_of, s_h)], axis=0)
        dqk = _dot1(g_o, v_new, 1, 1)
        drhs = _dot3(tinv, jnp.concatenate([dv_new, dw], axis=1), 0, 0)
        da = jnp.where(strict, -_dot1(drhs, sol, 1, 1), 0.0)
        drhs_u, drhs_w = drhs[:, :DN_D], drhs[:, DN_D:]
        s2 = rsum(drhs_w * k)
        dbeta = rsum(drhs_u * v) + s2 * ed + rsum(da * kk * dm)
        dkk = da * b_c * dm
        dqkr = dqk * dm
        mmat = da * a + dqk * qk
        tmp = rsum(dkd * kd)
        dd = (s2 * b_c * ed + rsum(mmat) - _dot3(mmat, jnp.ones((HB, 128), F32), 0, 0)[:, :1] + rsum(dqd * qd) - tmp)
        rowi = lax.broadcasted_iota(jnp.int32, (CH, 1), 0)
        last = []
        for h, rs in enumerate(rows_of):
            dgl = jnp.sum(rsum(s_h[h] * ds_h[h]), axis=0, keepdims=True)
            dd_last = jnp.sum(tmp[rs], axis=0, keepdims=True) + dgl * gl[h]
            last.append(jnp.where(rowi == CH - 1, dd_last, 0.0))
        dd = dd + jnp.concatenate(last, axis=0)
        dq_ref[...] = _unstack(_dot1(dqkr, k) + dqd * ed)
        dk_ref[...] = _unstack(drhs_w * (b_c * ed) + _dot1(dkk, k) + _dot1(dkk, k, 0, 0) + _dot1(dqkr, q, 0, 0)
                               + dkd * ekd)
        dv_ref[...] = _unstack(drhs_u * b_c)
        dg = _dot01(upper.astype(F32), jnp.broadcast_to(dd, (HB, 128)))[:, :1]
        lane = lax.broadcasted_iota(jnp.int32, (CH, 128), 1)
        out = jnp.zeros((CH, 128), F32)
        for h, rs in enumerate(rows_of):
            out = out + jnp.where(lane == h, dbeta[rs], 0.0) + jnp.where(lane == 4 + h, dg[rs], 0.0)
        dbg_ref[...] = out

    rev = lambda n: nch - 1 - n
    return pl.pallas_call(
        body, name="dn_bwd", interpret=False,
        out_shape=[jax.ShapeDtypeStruct((rows, DN_DIM), F32)] * 3 + [jax.ShapeDtypeStruct((rows, 128), F32)],
        grid=(nch,),
        in_specs=[pl.BlockSpec((CH, DN_DIM), lambda n: (rev(n), 0)),
                  pl.BlockSpec((CH, DN_DIM), lambda n: (rev(n), 1)),
                  pl.BlockSpec((CH, DN_DIM), lambda n: (rev(n), 2)),
                  pl.BlockSpec((CH, 128), lambda n: (rev(n), 0)),
                  pl.BlockSpec((1, 8, CH), lambda n: (rev(n), 0, 0)),
                  pl.BlockSpec((1, DN_H, DN_D, DN_D), lambda n: (rev(n), 0, 0, 0)),
                  pl.BlockSpec((1, HB, HB), lambda n: (rev(n), 0, 0)),
                  pl.BlockSpec((CH, DN_DIM), lambda n: (rev(n), 0))],
        out_specs=[pl.BlockSpec((CH, DN_DIM), lambda n: (rev(n), 0))] * 3 + [pl.BlockSpec((CH, 128), lambda n: (rev(n), 0))],
        scratch_shapes=[pltpu.VMEM((DN_H, DN_D, DN_D), F32)],
        compiler_params=_params(("arbitrary",)),
    )(qkv_n, qkv_n, qkv_n, bgcol, bgrow, s_all, ti_all, do)


def _swa_valid(n):
    c3 = lax.broadcasted_iota(jnp.int32, (3 * BLK, 4 * BLK), 0)
    r = lax.broadcasted_iota(jnp.int32, (3 * BLK, 4 * BLK), 1) % BLK
    c = c3 % BLK
    lo = jnp.where(c3 < BLK, PAD, jnp.where(c3 < 2 * BLK, r + 1 + jnp.where(n >= 2, 0, BLK), 0))
    hi = jnp.where(c3 < BLK, r + jnp.where(n >= 1, BLK, 0), jnp.where(c3 < 2 * BLK, BLK, r - jnp.where(n >= 1, 0, BLK)))
    return jnp.logical_and(c >= lo, c <= hi)


def _swa_probs(q, kcat, valid, sink):
    s = jnp.where(valid, _dot(kcat, q, 1, 1), -1e30)
    m = jnp.maximum(jnp.max(s, axis=0, keepdims=True), sink)
    e = jnp.where(valid, jnp.exp(s - m), 0.0)
    es = jnp.exp(sink - m)
    inv = 1.0 / (jnp.sum(e, axis=0, keepdims=True) + es)
    return e * inv, es * inv


def _swa_group(q_ref, sk_ref, h):
    q4 = jnp.concatenate([q_ref[4 * h + g] for g in range(4)], axis=0)
    sink4 = jnp.concatenate([jnp.full((1, BLK), sk_ref[4 * h + g], F32) for g in range(4)], axis=1)
    return q4, sink4


def _swa_specs():
    q = pl.BlockSpec((SWA_H, BLK, SWA_D), lambda n: (0, n, 0))
    km = pl.BlockSpec((SWA_KV, BLK, SWA_D), lambda n: (0, 0, 0))
    kp = pl.BlockSpec((SWA_KV, BLK, SWA_D), lambda n: (0, jnp.maximum(n - 1, 0), 0))
    kc = pl.BlockSpec((SWA_KV, BLK, SWA_D), lambda n: (0, n, 0))
    return [q, km, kp, kc, km, kp, kc]


def swa_fwd(qh, kh, vh, sinks):
    rows = qh.shape[1]
    nb = rows // BLK

    def body(q_ref, km, kp, kc, vm, vp, vc, sk_ref, o_ref):
        n = pl.program_id(0)
        valid = _swa_valid(n)
        outs = []
        for h in range(SWA_KV):
            kcat = jnp.concatenate([km[h], kp[h], kc[h]], axis=0)
            vcat = jnp.concatenate([vm[h], vp[h], vc[h]], axis=0)
            q4, sink4 = _swa_group(q_ref, sk_ref, h)
            p, _ = _swa_probs(q4, kcat, valid, sink4)
            o4 = _dot(p.astype(BF16), vcat, 0, 0)
            outs += [o4[g * BLK:(g + 1) * BLK] for g in range(4)]
        o_ref[...] = jnp.concatenate(outs, axis=1).astype(BF16)

    return pl.pallas_call(
        body, name="swa_fwd", interpret=False,
        out_shape=jax.ShapeDtypeStruct((rows, SWA_H * SWA_D), BF16),
        grid=(nb,),
        in_specs=_swa_specs() + [pl.BlockSpec(memory_space=pltpu.SMEM)],
        out_specs=pl.BlockSpec((BLK, SWA_H * SWA_D), lambda n: (n, 0)),
        compiler_params=_params(("parallel",)),
    )(qh, kh, kh, kh, vh, vh, vh, sinks)


def swa_bwd(qh, kh, vh, sinks, do):
    rows = qh.shape[1]
    nb = rows // BLK

    def body(q_ref, km, kp, kc, vm, vp, vc, do_ref, sk_ref, dq_ref, dk_ref, dv_ref, dsk_ref):
        n = pl.program_id(0)

        @pl.when(n == 0)
        def _():
            dk_ref[...] = jnp.zeros(dk_ref.shape, F32)
            dv_ref[...] = jnp.zeros(dv_ref.shape, F32)

        valid = _swa_valid(n)
        g_all = do_ref[...]
        rowi = lax.broadcasted_iota(jnp.int32, (SWA_H, 128), 0)
        dsk = jnp.zeros((SWA_H, 128), F32)
        pm = pl.multiple_of(jnp.maximum(n - 1, 0) * BLK, BLK)
        pc = pl.multiple_of(n * BLK, BLK)
        for h in range(SWA_KV):
            kcat = jnp.concatenate([km[h], kp[h], kc[h]], axis=0)
            vcat = jnp.concatenate([vm[h], vp[h], vc[h]], axis=0)
            q4, sink4 = _swa_group(q_ref, sk_ref, h)
            p, ps = _swa_probs(q4, kcat, valid, sink4)
            g4 = jnp.concatenate([g_all[:, (4 * h + g) * SWA_D:(4 * h + g + 1) * SWA_D] for g in range(4)], axis=0)
            dp = _dot(vcat, g4, 1, 1)
            delta = jnp.sum(p * dp, axis=0, keepdims=True)
            ds = (p * (dp - delta)).astype(BF16)
            dq4 = _dot(ds, kcat, 0, 0)
            dkc = _dot(ds, q4)
            dvc = _dot(p.astype(BF16), g4)
            t = ps * delta
            for g in range(4):
                dq_ref[4 * h + g] = dq4[g * BLK:(g + 1) * BLK]
                part = -jnp.sum(t[:, g * BLK:(g + 1) * BLK], axis=1, keepdims=True)
                dsk = dsk + jnp.where(rowi == 4 * h + g, part, 0.0)
            lanes = slice(h * SWA_D, (h + 1) * SWA_D)
            for ref, val in ((dk_ref, dkc), (dv_ref, dvc)):
                ref[0:BLK, lanes] += val[0:BLK]
                ref[pl.ds(pm, BLK), lanes] += val[BLK:2 * BLK]
                ref[pl.ds(pc, BLK), lanes] += val[2 * BLK:]
        dsk_ref[0] = dsk

    return pl.pallas_call(
        body, name="swa_bwd", interpret=False,
        out_shape=[jax.ShapeDtypeStruct((SWA_H, rows, SWA_D), F32),
                   jax.ShapeDtypeStruct((rows, SWA_KV * SWA_D), F32),
                   jax.ShapeDtypeStruct((rows, SWA_KV * SWA_D), F32),
                   jax.ShapeDtypeStruct((nb, SWA_H, 128), F32)],
        grid=(nb,),
        in_specs=_swa_specs() + [pl.BlockSpec((BLK, SWA_H * SWA_D), lambda n: (n, 0)),
                                 pl.BlockSpec(memory_space=pltpu.SMEM)],
        out_specs=[pl.BlockSpec((SWA_H, BLK, SWA_D), lambda n: (0, n, 0)),
                   pl.BlockSpec((rows, SWA_KV * SWA_D), lambda n: (0, 0)),
                   pl.BlockSpec((rows, SWA_KV * SWA_D), lambda n: (0, 0)),
                   pl.BlockSpec((1, SWA_H, 128), lambda n: (n, 0, 0))],
        compiler_params=_params(("arbitrary",)),
    )(qh, kh, kh, kh, vh, vh, vh, do, sinks)


def qknorm_fwd(qkv, qw, kw):
    rows = qkv.shape[0]
    tr = _pick(rows, (384, 128))
    scale = SWA_D ** -0.5

    def fn(i, x, qwv, kwv):
        def normed(j, wv, sc):
            xs = x[:, j * SWA_D:(j + 1) * SWA_D]
            r = lax.rsqrt(jnp.mean(xs * xs, axis=1, keepdims=True) + EPS)
            return (xs * r * wv * sc)[None]
        qo = jnp.concatenate([normed(j, qwv, scale) for j in range(SWA_H)], axis=0)
        ko = jnp.concatenate([normed(SWA_H + j, kwv, 1.0) for j in range(SWA_KV)], axis=0)
        vo = jnp.concatenate([x[:, (SWA_H + SWA_KV + j) * SWA_D:(SWA_H + SWA_KV + j + 1) * SWA_D][None]
                              for j in range(SWA_KV)], axis=0)
        return qo, ko, vo

    hm = lambda nh: ((nh, rows, SWA_D), BF16, (nh, tr, SWA_D), lambda i: (0, i, 0), "r3")
    return rowwise(fn, [cols(qkv, tr), whole(qw), whole(kw)], [hm(SWA_H), hm(SWA_KV), hm(SWA_KV)],
                   steps=rows // tr, name="qknorm_fwd")


def qknorm_bwd(qkv, qw, kw, dqh, dkh, dvh):
    rows = qkv.shape[0]
    tr = _pick(rows, (384, 128))
    scale = SWA_D ** -0.5

    def fn(i, x, qwv, kwv, dq, dk, dv):
        pieces = []
        dws = [jnp.zeros((1, SWA_D), F32), jnp.zeros((1, SWA_D), F32)]

        def one(j, dy, wv, sc, which):
            xs = x[:, j * SWA_D:(j + 1) * SWA_D]
            r = lax.rsqrt(jnp.mean(xs * xs, axis=1, keepdims=True) + EPS)
            xh = xs * r
            gw = dy * wv * sc
            pieces.append(r * (gw - xh * jnp.mean(gw * xh, axis=1, keepdims=True)))
            dws[which] = dws[which] + jnp.sum(dy * sc * xh, axis=0, keepdims=True)

        for j in range(SWA_H):
            one(j, dq[j], qwv, scale, 0)
        for j in range(SWA_KV):
            one(SWA_H + j, dk[:, j * SWA_D:(j + 1) * SWA_D], kwv, 1.0, 1)
        pieces.append(dv)
        return jnp.concatenate(pieces, axis=1), dws[0], dws[1]

    return rowwise(fn, [cols(qkv, tr), whole(qw), whole(kw), heads(dqh, tr), cols(dkh, tr), cols(dvh, tr)],
                   [out2d(rows, 1536, BF16, tr)], steps=rows // tr, name="qknorm_bwd",
                   accs=[((1, SWA_D), F32), ((1, SWA_D), F32)])


def _place():
    return lax.axis_index("x"), lax.axis_index("y"), lax.axis_index("c")


ANY = pl.BlockSpec(memory_space=pl.ANY)


def _rcopy(ssem, rsem, k, src, dst, to):
    return pltpu.make_async_remote_copy(src_ref=src, dst_ref=dst, send_sem=ssem.at[k], recv_sem=rsem.at[k],
                                        device_id=to, device_id_type=MESH)


def gather_weights(shards, small):
    n = len(shards)
    halves = [t.shape[0] // 2 for t in shards]

    def body(*refs):
        s_refs, small_ref = refs[:n], refs[n]
        o_refs, osmall = refs[n + 1:2 * n + 1], refs[2 * n + 1]
        ssem, rsem, lsem = refs[2 * n + 2:]
        x, y, c = _place()
        me = 2 * x + y
        chips = [(1 - x, y), (x, 1 - y), (1 - x, 1 - y)]

        def half(k, s, hh):
            return o_refs[k].at[s, pl.ds(hh * halves[k], halves[k]), :]

        loc = pltpu.make_async_copy(small_ref, osmall.at[me], lsem)
        loc.start()
        sends = []
        for k in range(n):
            for j, (px, py) in enumerate(chips):
                sends.append(_rcopy(ssem, rsem, 6 * k + j, s_refs[k].at[pl.ds(c * halves[k], halves[k]), :],
                                    half(k, me, c), (px, py, c)))
        for j, (px, py) in enumerate(chips):
            sends.append(_rcopy(ssem, rsem, 6 * n + j, small_ref, osmall.at[me], (px, py, c)))
        for cp in sends:
            cp.start()
        for k in range(n):
            for j, (px, py) in enumerate(chips):
                s = 2 * px + py
                _rcopy(ssem, rsem, 6 * k + j, half(k, s, c), half(k, s, c), (x, y, c)).wait_recv()
                fwd = _rcopy(ssem, rsem, 6 * k + 3 + j, half(k, s, c), half(k, s, c), (x, y, 1 - c))
                fwd.start()
                sends.append(fwd)
        for k in range(n):
            for j, (px, py) in enumerate(chips):
                s = 2 * px + py
                _rcopy(ssem, rsem, 6 * k + 3 + j, half(k, s, 1 - c), half(k, s, 1 - c), (x, y, c)).wait_recv()
        for j, (px, py) in enumerate(chips):
            s = 2 * px + py
            _rcopy(ssem, rsem, 6 * n + j, osmall.at[s], osmall.at[s], (x, y, c)).wait_recv()
        for cp in sends:
            cp.wait_send()
        loc.wait()

    res = pl.pallas_call(
        body, name="gather_weights", interpret=False,
        out_shape=[jax.ShapeDtypeStruct((4,) + t.shape, t.dtype) for t in shards]
        + [jax.ShapeDtypeStruct((4, SW_ROWS, 1024), F32)],
        in_specs=[ANY] * (n + 1), out_specs=[ANY] * (n + 1),
        scratch_shapes=[pltpu.SemaphoreType.DMA((6 * n + 3,)), pltpu.SemaphoreType.DMA((6 * n + 3,)),
                        pltpu.SemaphoreType.DMA],
    )(*shards, small)
    return res[:n], res[n]


def _handshake(peers):
    barrier = pltpu.get_barrier_semaphore()
    for peer in peers:
        pl.semaphore_signal(barrier, inc=1, device_id=peer, device_id_type=MESH)
    pl.semaphore_wait(barrier, len(peers))


def gather_weights_beside(shards):
    n = len(shards)
    halves = [t.shape[0] // 2 for t in shards]

    def body(*refs):
        s_refs, o_refs, ssem, rsem = refs[:n], refs[n:2 * n], refs[2 * n], refs[2 * n + 1]
        x, y, c = _place()
        me = 2 * x + y
        chips = [(1 - x, y), (x, 1 - y), (1 - x, 1 - y)]
        _handshake([(px, py, c) for px, py in chips] + [(x, y, 1 - c)])

        def half(k, s, hh):
            return o_refs[k].at[s, pl.ds(hh * halves[k], halves[k]), :]

        sends = []
        for k in range(n):
            for j, (px, py) in enumerate(chips):
                sends.append(_rcopy(ssem, rsem, 6 * k + j, s_refs[k].at[pl.ds(c * halves[k], halves[k]), :],
                                    half(k, me, c), (px, py, c)))
        for cp in sends:
            cp.start()
        for k in range(n):
            for j, (px, py) in enumerate(chips):
                s = 2 * px + py
                _rcopy(ssem, rsem, 6 * k + j, half(k, s, c), half(k, s, c), (x, y, c)).wait_recv()
                fwd = _rcopy(ssem, rsem, 6 * k + 3 + j, half(k, s, c), half(k, s, c), (x, y, 1 - c))
                fwd.start()
                sends.append(fwd)
        for k in range(n):
            for j, (px, py) in enumerate(chips):
                s = 2 * px + py
                _rcopy(ssem, rsem, 6 * k + 3 + j, half(k, s, 1 - c), half(k, s, 1 - c), (x, y, c)).wait_recv()
        for cp in sends:
            cp.wait_send()

    return pl.kernel(
        body, name="gather_weights_beside",
        out_type=[jax.ShapeDtypeStruct((4,) + t.shape, t.dtype) for t in shards],
        mesh=plsc.ScalarSubcoreMesh(axis_name="sequencer", num_cores=1),
        scratch_types=[pltpu.SemaphoreType.DMA((6 * n,)), pltpu.SemaphoreType.DMA((6 * n,))],
        compiler_params=pltpu.CompilerParams(collective_id=1),
    )(*shards)


def swap_halves(gs, *, name):
    n = len(gs)

    def body(*refs):
        g_refs, o_refs, ssem, rsem = refs[:n], refs[n:2 * n], refs[2 * n], refs[2 * n + 1]
        x, y, c = _place()
        cps = []
        for k in range(n):
            hk = g_refs[k].shape[1] // 2
            cps.append(_rcopy(ssem, rsem, k, g_refs[k].at[:, pl.ds((1 - c) * hk, hk), :], o_refs[k], (x, y, 1 - c)))
        for cp in cps:
            cp.start()
        for cp in cps:
            cp.wait()

    return pl.pallas_call(
        body, name=name, interpret=False,
        out_shape=[jax.ShapeDtypeStruct((4, t.shape[1] // 2, t.shape[2]), t.dtype) for t in gs],
        in_specs=[ANY] * n, out_specs=[ANY] * n,
        scratch_shapes=[pltpu.SemaphoreType.DMA((n,)), pltpu.SemaphoreType.DMA((n,))],
    )(*gs)


def _sum_rows(hk):
    return _pick(hk, (512, 352, 256, 128))


def pair_sum(g, other, c_idx, *, name):
    _, hk, width = other.shape
    tr = _sum_rows(hk)
    nbk = hk // tr

    def body(c_ref, g_ref, o_ref, out_ref):
        out_ref[...] = (g_ref[...].astype(F32) + o_ref[...].astype(F32)).astype(BF16)

    return pl.pallas_call(
        body, name=name, interpret=False,
        out_shape=jax.ShapeDtypeStruct((4, hk, width), BF16),
        grid_spec=pltpu.PrefetchScalarGridSpec(
            num_scalar_prefetch=1, grid=(4, nbk),
            in_specs=[pl.BlockSpec((1, tr, width), lambda s, i, c_ref: (s, c_ref[0] * nbk + i, 0)),
                      pl.BlockSpec((1, tr, width), lambda s, i, c_ref: (s, i, 0))],
            out_specs=pl.BlockSpec((1, tr, width), lambda s, i, c_ref: (s, i, 0))),
        compiler_params=_params(("parallel", "parallel")),
    )(c_idx, g, other)


def chip_sum(p, got, idx, *, name):
    _, hk, width = got.shape
    tr = _sum_rows(hk)
    nbk = hk // tr

    def body(idx_ref, p_ref, g_ref, out_ref):
        acc = p_ref[0].astype(F32)
        for j in range(3):
            acc = acc + g_ref[j].astype(F32)
        out_ref[0] = acc

    return pl.pallas_call(
        body, name=name, interpret=False,
        out_shape=jax.ShapeDtypeStruct((2, hk, width), F32),
        grid_spec=pltpu.PrefetchScalarGridSpec(
            num_scalar_prefetch=1, grid=(nbk,),
            in_specs=[pl.BlockSpec((1, tr, width), lambda i, idx_ref: (idx_ref[0], i, 0)),
                      pl.BlockSpec((3, tr, width), lambda i, idx_ref: (0, i, 0))],
            out_specs=pl.BlockSpec((1, tr, width), lambda i, idx_ref: (idx_ref[1], i, 0))),
        compiler_params=_params(("parallel",)),
    )(idx, p, got)


def join_halves(qs):
    n = len(qs)

    def body(*refs):
        q_refs, o_refs, ssem, rsem = refs[:n], refs[n:2 * n], refs[2 * n], refs[2 * n + 1]
        x, y, c = _place()
        cps = [_rcopy(ssem, rsem, k, q_refs[k].at[c], o_refs[k].at[c], (x, y, 1 - c)) for k in range(n)]
        for cp in cps:
            cp.start()
        for k in range(n):
            _rcopy(ssem, rsem, k, q_refs[k].at[c], o_refs[k].at[1 - c], (x, y, 1 - c)).wait_recv()
        for cp in cps:
            cp.wait_send()

    return pl.pallas_call(
        body, name="join_halves", interpret=False,
        out_shape=[jax.ShapeDtypeStruct(t.shape, t.dtype) for t in qs],
        in_specs=[ANY] * n, out_specs=[ANY] * n, input_output_aliases={k: k for k in range(n)},
        scratch_shapes=[pltpu.SemaphoreType.DMA((n,)), pltpu.SemaphoreType.DMA((n,))],
    )(*qs)


def scatter_chips_beside(ps, cid, name):
    n = len(ps)

    def body(*refs):
        p_refs, o_refs, ssem, rsem = refs[:n], refs[n:2 * n], refs[2 * n], refs[2 * n + 1]
        x, y, c = _place()
        chips = [(1 - x, y), (x, 1 - y), (1 - x, 1 - y)]
        _handshake([(px, py, c) for px, py in chips])
        cps = [_rcopy(ssem, rsem, 3 * k + j, p_refs[k].at[2 * px + py], o_refs[k].at[j], (px, py, c))
               for k in range(n) for j, (px, py) in enumerate(chips)]
        for cp in cps:
            cp.start()
        for cp in cps:
            cp.wait()

    return pl.kernel(
        body, name=name, out_type=[jax.ShapeDtypeStruct((3,) + t.shape[1:], t.dtype) for t in ps],
        mesh=plsc.ScalarSubcoreMesh(axis_name="sequencer", num_cores=1),
        scratch_types=[pltpu.SemaphoreType.DMA((3 * n,)), pltpu.SemaphoreType.DMA((3 * n,))],
        compiler_params=pltpu.CompilerParams(collective_id=cid),
    )(*ps)


def reduce_begin(gs, names, c_idx, cid, tag):
    others = swap_halves(gs, name=f"swap_halves_{tag}")
    pairs = [pair_sum(g, o, c_idx, name=f"pair_sum_{nm}") for g, o, nm in zip(gs, others, names)]
    return pairs, scatter_chips_beside(pairs, cid, f"scatter_chips_{tag}")


def reduce_end(pairs, gots, names, idx):
    mine = [chip_sum(p, g, idx, name=f"chip_sum_{nm}") for p, g, nm in zip(pairs, gots, names)]
    return [q.reshape(2 * q.shape[1], q.shape[2]) for q in join_halves(mine)]


def gather_small(v):
    def body(v_ref, o_ref, ssem, rsem, lsem):
        x, y, c = _place()
        loc = pltpu.make_async_copy(v_ref, o_ref.at[4 * x + 2 * y + c], lsem)
        loc.start()
        cps = []
        for k in range(1, 8):
            fx, fy, fc = (k >> 2) & 1, (k >> 1) & 1, k & 1
            px = 1 - x if fx else x
            py = 1 - y if fy else y
            pc = 1 - c if fc else c
            cps.append((pltpu.make_async_remote_copy(
                src_ref=v_ref, dst_ref=o_ref.at[4 * x + 2 * y + c], send_sem=ssem.at[k - 1], recv_sem=rsem.at[k - 1],
                device_id=(px, py, pc), device_id_type=MESH), 4 * px + 2 * py + pc))
        for cp, _ in cps:
            cp.start()
        for k, (cp, peer) in enumerate(cps):
            pltpu.make_async_remote_copy(
                src_ref=v_ref, dst_ref=o_ref.at[peer], send_sem=ssem.at[k], recv_sem=rsem.at[k],
                device_id=(x, y, c), device_id_type=MESH).wait_recv()
        for cp, _ in cps:
            cp.wait_send()
        loc.wait()

    return pl.pallas_call(
        body, name="gather_small", interpret=False,
        out_shape=jax.ShapeDtypeStruct((8, SV_ROWS, 1024), F32),
        in_specs=[ANY], out_specs=ANY,
        scratch_shapes=[pltpu.SemaphoreType.DMA((7,)), pltpu.SemaphoreType.DMA((7,)), pltpu.SemaphoreType.DMA],
    )(v)


def sum_slots(a):
    def fn(i, t):
        acc = t[0]
        for k in range(1, 8):
            acc = acc + t[k]
        return acc

    return rowwise(fn, [whole(a)], [((SV_ROWS, 1024), F32, (SV_ROWS, 1024), lambda i: (0, 0), "w")], steps=1,
                   name="sum_slots")[0]


def _head_rms(x, nw):
    xs, rs = [], []
    for h in range(DN_H):
        xh = x[:, h * DN_D:(h + 1) * DN_D]
        r = lax.rsqrt(jnp.mean(xh * xh, axis=1, keepdims=True) + EPS)
        xs.append(xh * r)
        rs.append(r)
    return xs, rs


def bg_fwd(p, alog, dtb):
    rows = p.shape[0]
    tr = _pick(rows, (384, 128))

    def fn(i, x, al, dt):
        lane = lax.broadcasted_iota(jnp.int32, x.shape, 1)
        row = i + lax.broadcasted_iota(jnp.int32, x.shape, 0)
        g = -jnp.exp(al) * _softplus(x + dt)
        out = jnp.where(lane < 4, _sigmoid(x), jnp.where(lane < 8, g, 0.0))
        return jnp.where(row >= PAD, out, 0.0)

    return rowwise(fn, [cols(p, tr, 128, BG0 // 128), whole(alog), whole(dtb)], [out2d(rows, 128, F32, tr)],
                   steps=rows // tr, name="bg_fwd")[0]


def bg_bwd(p, alog, dtb, dbg):
    rows = p.shape[0]
    tr = _pick(rows, (384, 128))

    def fn(i, x, al, dt, g_in):
        lane = lax.broadcasted_iota(jnp.int32, x.shape, 1)
        row = i + lax.broadcasted_iota(jnp.int32, x.shape, 0)
        live = row >= PAD
        is_b = jnp.logical_and(live, lane < 4)
        is_g = jnp.logical_and(live, jnp.logical_and(lane >= 4, lane < 8))
        beta = _sigmoid(x)
        ea = jnp.exp(al)
        g = -ea * _softplus(x + dt)
        dalpha = jnp.where(is_g, g_in * (-ea) * _sigmoid(x + dt), 0.0)
        dx = jnp.where(is_b, g_in * beta * (1.0 - beta), dalpha)
        dal = jnp.sum(jnp.where(is_g, g_in * g, 0.0), axis=0, keepdims=True)
        return jnp.concatenate([dx, jnp.zeros(x.shape, F32)], axis=1), dal, jnp.sum(dalpha, axis=0, keepdims=True)

    return rowwise(fn, [cols(p, tr, 128, BG0 // 128), whole(alog), whole(dtb), cols(dbg, tr)],
                   [out2d(rows, 256, BF16, tr)], steps=rows // tr, name="bg_bwd",
                   accs=[((1, 128), F32), ((1, 128), F32)])


def dn_qkv_post(j, y):
    xs = _silu(y)
    sc = jnp.where(j == 0, DN_D ** -0.5, 1.0)
    outs = []
    for h in range(DN_H):
        xh = xs[:, h * DN_D:(h + 1) * DN_D]
        r = lax.rsqrt(jnp.sum(xh * xh, axis=1, keepdims=True) + EPS)
        outs.append(jnp.where(j < 2, xh * r * sc, xh))
    return jnp.concatenate(outs, axis=1), y


def dn_qkv_bwd(cq, dq, dk, dv):
    rows = cq.shape[0]
    tr = _pick(rows, (384, 128))

    def fn(i, c0, c1, c2, g0, g1, g2):
        pieces = []
        for kind, (cv, g) in enumerate(((c0, g0), (c1, g1), (c2, g2))):
            xs = _silu(cv)
            if kind < 2:
                sc = DN_D ** -0.5 if kind == 0 else 1.0
                ds = []
                for h in range(DN_H):
                    sl = slice(h * DN_D, (h + 1) * DN_D)
                    xh, gh = xs[:, sl], g[:, sl]
                    r = lax.rsqrt(jnp.sum(xh * xh, axis=1, keepdims=True) + EPS)
                    xn = xh * r
                    ds.append(sc * r * (gh - xn * jnp.sum(gh * xn, axis=1, keepdims=True)))
                dxs = jnp.concatenate(ds, axis=1)
            else:
                dxs = g
            pieces.append(dxs * _dsilu(cv))
        return jnp.concatenate(pieces, axis=1)

    ins = [cols(cq, tr, DN_DIM, k) for k in range(3)] + [cols(t, tr) for t in (dq, dk, dv)]
    return rowwise(fn, ins, [out2d(rows, 3 * DN_DIM, F32, tr)], steps=rows // tr, name="dn_qkv_bwd")[0]


def dn_out_fwd(o, p, nw):
    rows = o.shape[0]
    tr = _pick(rows, (384, 128))

    def fn(i, ov, z, w):
        xs, _ = _head_rms(ov, w)
        return jnp.concatenate(xs, axis=1) * jnp.concatenate([w] * DN_H, axis=1) * _silu(z)

    return rowwise(fn, [cols(o, tr), cols(p, tr, DN_DIM, 6), whole(nw)], [out2d(rows, DN_DIM, BF16, tr)],
                   steps=rows // tr, name="dn_out_fwd")[0]


def dn_out_bwd(o, p, nw, dymix):
    rows = o.shape[0]
    tr = _pick(rows, (384, 128))

    def fn(i, ov, z, w, dy):
        xs, rs = _head_rms(ov, w)
        sz = _silu(z)
        dn = dy * sz
        dos, dw = [], jnp.zeros((1, DN_D), F32)
        for h in range(DN_H):
            sl = slice(h * DN_D, (h + 1) * DN_D)
            gw = dn[:, sl] * w
            dos.append(rs[h] * (gw - xs[h] * jnp.mean(gw * xs[h], axis=1, keepdims=True)))
            dw = dw + jnp.sum(dn[:, sl] * xs[h], axis=0, keepdims=True)
        n = jnp.concatenate(xs, axis=1) * jnp.concatenate([w] * DN_H, axis=1)
        return jnp.concatenate(dos, axis=1), dy * n * _dsilu(z), dw

    return rowwise(fn, [cols(o, tr), cols(p, tr, DN_DIM, 6), whole(nw), cols(dymix, tr, DN_DIM, 1)],
                   [out2d(rows, DN_DIM, F32, tr), out2d(rows, DN_DIM, BF16, tr)], steps=rows // tr,
                   name="dn_out_bwd", accs=[((1, DN_D), F32)])


def conv_a_pre_bwd(dymix, cv, p):
    rows = cv.shape[0]
    tr = _pick(rows, (384, 128))

    def fn(i, dy, c, go):
        return dy * c, dy * go

    return rowwise(fn, [cols(dymix, tr, D_CONV, 0), cols(cv, tr), cols(p, tr, D_CONV, 1)],
                   [out2d(rows, D_CONV, BF16, tr), out2d(rows, D_CONV, F32, tr)], steps=rows // tr,
                   name="conv_a_pre_bwd")


def _act_bwd_epi(row0, da, gc, val):
    c, val = gc.astype(F32), val.astype(F32)
    return da * _silu(c), da * val * _dsilu(c)


def _rows8(w):
    return jnp.pad(w.astype(F32), ((0, 8 - w.shape[0]), (0, 0)))


def _lanes(v, at):
    return jnp.pad(v.astype(F32), (at, 128 - at - v.shape[0]))[None]


def ffn_fwd(h, nw, w_up, cw8, w_down, tag):
    rows = h.shape[0]
    tr = _pick(rows, (384, 128))
    hn = rms_fwd(h, nw, name=f"ffn{tag}_norm")
    u = mm(hn, w_up, out_dtype=BF16, b_chip=True, name=f"ffn{tag}_up")
    a, gc = conv_fwd([(u, 0)], cw8, 3, rows=rows, c=D_FF, tc=1408, tr=tr, name=f"ffn{tag}_conv",
                     post=lambda j, y, val: (_silu(y) * val.astype(F32), y), extras=[(u, 2)], outs=[BF16, BF16])
    out = mm(a, w_down, add=h, name=f"ffn{tag}_down")
    return out, (hn, u, a, gc)


def ffn_bwd(h, nw, w_up, cw8, w_down, saved, dh, tag):
    hn, u, a, gc = saved
    rows = h.shape[0]
    tr = _pick(rows, (384, 128))
    dval, dgc = mm(dh, w_down, tb=True, name=f"ffn{tag}_down_dx", epi=_act_bwd_epi,
                   epi_ins=[(gc, lambda j: j), (u, lambda j: 2 + j)], epi_outs=[BF16, F32])
    d_w_down = mm(a, dh, ta=True, out_dtype=BF16, name=f"ffn{tag}_down_dw")
    dgate, d_cw = conv_bwd([(u, 0)], cw8, 3, dgc, rows=rows, c=D_FF, tc=1408, tr=tr, name=f"ffn{tag}_conv_bwd",
                           post=lambda dx: dx, outs=[BF16])
    du = jnp.concatenate([dgate, dval], axis=1)
    dh_new, d_nw = dx_rms_bwd(du, w_up, h, nw, dh, name=f"ffn{tag}_up_dx", b_chip=True)
    d_w_up = mm(hn, du, ta=True, out_dtype=BF16, out_chip=True, name=f"ffn{tag}_up_dw")
    return dh_new, d_nw, d_w_up, d_cw, d_w_down


def mixer_fwd(h, nw, w_in, ca8, dc8, alog, dtb, dnw, w_out, tie=None):
    rows = h.shape[0]
    tr = _pick(rows, (384, 128))
    hn = rms_fwd(h, nw, name="mix_norm")
    p = mm(hn, w_in, name="mix_in")
    y_a, cv = conv_fwd([(p, 0), (p, 2)], ca8, 3, rows=rows, c=D_CONV, tc=D_CONV, tr=tr, name="conv_a",
                       pre=lambda gi, ah: gi * ah, post=lambda j, y, go: (go * y, y), extras=[(p, 1)],
                       outs=[BF16, F32])
    qkv_n, cq = conv_fwd([(p, 3)], dc8, 4, rows=rows, c=3 * DN_DIM, tc=DN_DIM, tr=tr, name="dn_conv",
                         post=dn_qkv_post, outs=[F32, F32])
    bgcol = bg_fwd(p, alog, dtb)
    if tie is not None:
        bgcol = tie(bgcol)
    bgrow = bgcol[:, :8].reshape(rows // CH, CH, 8).transpose(0, 2, 1)
    o, s_all, ti_all = dn_fwd(qkv_n, bgcol, bgrow)
    y_b = dn_out_fwd(o, p, dnw)
    ymix = jnp.concatenate([y_a, y_b], axis=1)
    out = mm(ymix, w_out, add=h, name="mix_out")
    return out, (hn, p, cv, qkv_n, cq, bgcol, bgrow, o, s_all, ti_all, ymix)


def mixer_bwd(h, nw, w_in, ca8, dc8, alog, dtb, dnw, w_out, saved, dh):
    hn, p, cv, qkv_n, cq, bgcol, bgrow, o, s_all, ti_all, ymix = saved
    rows = h.shape[0]
    tr = _pick(rows, (384, 128))
    dymix = mm(dh, w_out, tb=True, name="mix_out_dx")
    d_w_out = mm(ymix, dh, ta=True, out_dtype=BF16, name="mix_out_dw")
    do, dz, d_dnw = dn_out_bwd(o, p, dnw, dymix)
    dq, dk, dv, dbg = dn_bwd(qkv_n, bgcol, bgrow, s_all, ti_all, do)
    dbg_p, d_alog, d_dtb = bg_bwd(p, alog, dtb, dbg)
    dcq = dn_qkv_bwd(cq, dq, dk, dv)
    dqkv, d_dc = conv_bwd([(p, 3)], dc8, 4, dcq, rows=rows, c=3 * DN_DIM, tc=DN_DIM, tr=tr, name="dn_conv_bwd",
                          post=lambda dx: dx, outs=[BF16])
    dgo, dcv = conv_a_pre_bwd(dymix, cv, p)
    dgi, dah, d_ca = conv_bwd([(p, 0), (p, 2)], ca8, 3, dcv, rows=rows, c=D_CONV, tc=D_CONV, tr=tr,
                              name="conv_a_bwd", pre=lambda gi, ah: gi * ah,
                              post=lambda dm, gi, ah: (dm * ah, dm * gi), extras=[(p, 0), (p, 2)], outs=[BF16, BF16])
    dp = jnp.concatenate([dgi, dgo, dah, dqkv, dz, dbg_p], axis=1)
    dh_new, d_nw = dx_rms_bwd(dp, w_in, h, nw, dh, name="mix_in_dx")
    d_w_in = mm(hn, dp, ta=True, out_dtype=BF16, name="mix_in_dw")
    return dh_new, d_nw, d_w_in, d_ca, d_dc, d_alog, d_dtb, d_dnw, d_w_out


def swa_layer_fwd(h, nw, wqkv, qw, kw, sinks, wo):
    hn = rms_fwd(h, nw, name="swa_norm")
    qkv = mm(hn, wqkv, name="swa_qkv")
    qh, kh, vh = qknorm_fwd(qkv, qw, kw)
    att = swa_fwd(qh, kh, vh, sinks)
    out = mm(att, wo, add=h, name="swa_out")
    return out, (hn, qkv, qh, kh, vh, att)


def swa_layer_bwd(h, nw, wqkv, qw, kw, sinks, wo, saved, dh):
    hn, qkv, qh, kh, vh, att = saved
    datt = mm(dh, wo, tb=True, out_dtype=BF16, name="swa_out_dx")
    d_wo = mm(att, dh, ta=True, out_dtype=BF16, name="swa_out_dw")
    dqh, dkh, dvh, dsk = swa_bwd(qh, kh, vh, sinks, datt)
    dqkv, d_qw, d_kw = qknorm_bwd(qkv, qw, kw, dqh, dkh, dvh)
    dh_new, d_nw = dx_rms_bwd(dqkv, wqkv, h, nw, dh, name="swa_qkv_dx")
    d_wqkv = mm(hn, dqkv, ta=True, out_dtype=BF16, name="swa_qkv_dw")
    d_sinks = jnp.sum(dsk[:, :, 0], axis=0)
    return dh_new, d_nw, d_wqkv, d_qw, d_kw, d_sinks, d_wo


BIG = ("mix_w_in", "mix_w_out", "swa_wq", "swa_wk", "swa_wv", "swa_wo", "ffn_w_up", "ffn_w_down")


def _flat_pad(parts, rows):
    v = jnp.concatenate([t.astype(F32).reshape(-1) for t in parts])
    return jnp.pad(v, (0, rows * 1024 - v.shape[0])).reshape(rows, 1024)


def _split_flat(flat, shapes):
    v = flat.reshape(-1)
    out, o = [], 0
    for s in shapes:
        n = 1
        for d_ in s:
            n *= d_
        out.append(v[o:o + n].reshape(s))
        o += n
    return out


def local_step(x0, target0, meta_full, anw, fnw, w_in, ca8, dc8, alog, dtb, dnw, w_out, qw, kw, sinks, fc8, late,
               begin=None, tie=None):
    begin = begin or (lambda tag, names, grads: None)
    h0 = jnp.concatenate([jnp.zeros((PAD, D), F32), meta_full, x0], axis=0)
    h1, s_mix = mixer_fwd(h0, anw[0], w_in, ca8, dc8, alog, dtb, dnw, w_out, tie)
    wqkv, wo, w_up, w_down = late()
    h2, s_f0 = ffn_fwd(h1, fnw[0], w_up[0], fc8[0], w_down[0], 0)
    h3, s_swa = swa_layer_fwd(h2, anw[1], wqkv, qw, kw, sinks, wo)
    h4, s_f1 = ffn_fwd(h3, fnw[1], w_up[1], fc8[1], w_down[1], 1)
    dh, loss_l = loss_grad(h4, target0)
    dh, d_fnw1, d_up1, d_fc1, d_down1 = ffn_bwd(h3, fnw[1], w_up[1], fc8[1], w_down[1], s_f1, dh, 1)
    begin("ffn1", ("up1", "down1"), [d_up1, d_down1.reshape(4, 704, D)])
    dh, d_anw1, d_wqkv, d_qw, d_kw, d_sinks, d_wo = swa_layer_bwd(h2, anw[1], wqkv, qw, kw, sinks, wo, s_swa, dh)
    begin("swa", ("wq", "wk", "wv", "wo"),
          [d_wqkv[:, :D].reshape(4, 256, D), d_wqkv[:, D:D + 256].reshape(4, 256, 256),
           d_wqkv[:, D + 256:].reshape(4, 256, 256), d_wo.reshape(4, 256, D)])
    dh, d_fnw0, d_up0, d_fc0, d_down0 = ffn_bwd(h1, fnw[0], w_up[0], fc8[0], w_down[0], s_f0, dh, 0)
    begin("ffn0", ("up0", "down0"), [d_up0, d_down0.reshape(4, 704, D)])
    dh, d_anw0, d_w_in, d_ca, d_dc, d_alog, d_dtb, d_dnw, d_w_out = mixer_bwd(
        h0, anw[0], w_in, ca8, dc8, alog, dtb, dnw, w_out, s_mix, dh)
    begin("mix", ("w_in", "w_out"),
          [d_w_in[:, :IN_DIM].reshape(D, 4, 898).transpose(1, 0, 2), d_w_out.reshape(4, 256, D)])
    return (dh, loss_l, d_anw0, d_anw1, d_fnw0, d_fnw1, d_w_in, d_ca, d_dc, d_alog, d_dtb, d_dnw, d_w_out, d_wqkv,
            d_qw, d_kw, d_sinks, d_wo, d_up0, d_up1, d_fc0, d_fc1, d_down0, d_down1)


def kernel(x, meta_tokens, attn_norm_w, ffn_norm_w, mix_w_in, conv_a_w, dn_conv_w, dn_a_log, dn_dt_bias, dn_norm_w, mix_w_out, swa_wq, swa_wk, swa_wv, swa_q_norm_w, swa_k_norm_w, swa_sinks, swa_wo, ffn_w_up, ffn_conv_w, ffn_w_down, loss_target, m_meta_tokens, m_attn_norm_w, m_ffn_norm_w, m_mix_w_in, m_conv_a_w, m_dn_conv_w, m_dn_a_log, m_dn_dt_bias, m_dn_norm_w, m_mix_w_out, m_swa_wq, m_swa_wk, m_swa_wv, m_swa_q_norm_w, m_swa_k_norm_w, m_swa_sinks, m_swa_wo, m_ffn_w_up, m_ffn_conv_w, m_ffn_w_down, v_meta_tokens, v_attn_norm_w, v_ffn_norm_w, v_mix_w_in, v_conv_a_w, v_dn_conv_w, v_dn_a_log, v_dn_dt_bias, v_dn_norm_w, v_mix_w_out, v_swa_wq, v_swa_wk, v_swa_wv, v_swa_q_norm_w, v_swa_k_norm_w, v_swa_sinks, v_swa_wo, v_ffn_w_up, v_ffn_conv_w, v_ffn_w_down):
    ix, iy, ic = lax.axis_index("x"), lax.axis_index("y"), lax.axis_index("c")
    chip = 2 * ix + iy
    seq = x.shape[1]
    rows = HEAD0 + seq

    small_sharded = (conv_a_w, dn_conv_w, ffn_conv_w, meta_tokens)
    up_b, down_b = ffn_w_up.astype(BF16), ffn_w_down.astype(BF16)
    own = [mix_w_in[0].astype(BF16), mix_w_out[0].astype(BF16), swa_wq[0].astype(BF16), swa_wk[0].astype(BF16),
           swa_wv[0].astype(BF16), swa_wo[0].astype(BF16), up_b[0], up_b[1], down_b[0], down_b[1]]
    fill = lambda gathered, mine: [lax.dynamic_update_slice_in_dim(g, t[None], chip, axis=0)
                                   for g, t in zip(gathered, mine)]
    first, g_small = gather_weights(own[:2], _flat_pad(small_sharded, SW_ROWS))
    g_in, g_out = fill(first, own[:2])
    w_in = jnp.pad(g_in.transpose(1, 0, 2).reshape(D, IN_DIM), ((0, 0), (0, P_W - IN_DIM)))
    w_out = g_out.reshape(D, D)
    rest = {}

    def tie(t):
        t, *mine = lax.optimization_barrier((t, *own[2:]))
        rest["w"] = fill(gather_weights_beside(mine), mine)
        return t

    def late():
        g_q, g_k, g_v, g_o, g_up0, g_up1, g_dn0, g_dn1 = rest["w"]
        wqkv = jnp.concatenate([g_q.reshape(D, D), g_k.reshape(D, 256), g_v.reshape(D, 256)], axis=1)
        return wqkv, g_o.reshape(D, D), [g_up0, g_up1], [g_dn0.reshape(D_FF, D), g_dn1.reshape(D_FF, D)]

    gs = g_small.reshape(4, -1)
    ca_full = gs[:, 0:384].reshape(4, 3, 128).transpose(1, 0, 2).reshape(3, D_CONV)
    dc_full = gs[:, 384:1920].reshape(4, 4, 384).transpose(1, 0, 2).reshape(4, 3 * DN_DIM)
    fc_full = gs[:, 1920:6144].reshape(4, 2, 3, 704).transpose(1, 2, 0, 3).reshape(2, 3, D_FF)
    meta_full = gs[:, 6144:10240].reshape(4, N_META, 256).transpose(1, 0, 2).reshape(N_META, D)
    ca8, dc8 = _rows8(ca_full), _rows8(dc_full)
    fc8 = [_rows8(fc_full[0]), _rows8(fc_full[1])]
    alog, dtb = _lanes(dn_a_log[0], 4), _lanes(dn_dt_bias[0], 4)
    dnw = dn_norm_w.astype(F32)
    qw, kw = swa_q_norm_w.astype(F32), swa_k_norm_w.astype(F32)
    sinks = swa_sinks[0].astype(F32)
    anw = [attn_norm_w[0:1], attn_norm_w[1:2]]
    fnw = [ffn_norm_w[0:1], ffn_norm_w[1:2]]

    c_idx = jnp.reshape(ic, (1,)).astype(jnp.int32)
    chip_idx = jnp.stack([chip, ic]).astype(jnp.int32)
    begun = []

    def begin(tag, names, grads):
        pairs, gots = reduce_begin(grads, names, c_idx, 2 + len(begun), tag)
        begun.append((names, pairs, gots))

    (dh, loss_l, d_anw0, d_anw1, d_fnw0, d_fnw1, d_w_in, d_ca, d_dc, d_alog, d_dtb, d_dnw, d_w_out, d_wqkv, d_qw,
     d_kw, d_sinks, d_wo, d_up0, d_up1, d_fc0, d_fc1, d_down0, d_down1) = local_step(
        x[0], loss_target[0], meta_full, anw, fnw, w_in, ca8, dc8, alog, dtb, dnw, w_out, qw, kw, sinks, fc8, late,
        begin, tie)
    grad_x = dh[HEAD0:][None]

    small_parts = [jnp.concatenate([d_anw0, d_anw1], axis=0), jnp.concatenate([d_fnw0, d_fnw1], axis=0),
                   d_alog[0, 4:8], d_dtb[0, 4:8], d_dnw, d_qw, d_kw, d_sinks,
                   d_ca[:3], d_dc[:4], jnp.stack([d_fc0[:3], d_fc1[:3]]), dh[PAD:HEAD0], loss_l[0, 0:1]]
    small_shapes = [(2, D), (2, D), (1, 4), (1, 4), (1, DN_D), (1, SWA_D), (1, SWA_D), (1, SWA_H),
                    (1, 3, D_CONV), (1, 4, 3 * DN_DIM), (2, 3, D_FF), (N_META, D), ()]
    red = sum_slots(gather_small(_flat_pad(small_parts, SV_ROWS)))
    (g_anw, g_fnw, g_alog, g_dtb, g_dnw, g_qw, g_kw, g_sinks, g_ca_f, g_dc_f, g_fc_f, g_meta_f,
     loss) = _split_flat(red, small_shapes)
    g_ca = lax.dynamic_slice_in_dim(g_ca_f, chip * 128, 128, axis=2)
    g_dc = lax.dynamic_slice_in_dim(g_dc_f, chip * 384, 384, axis=2)
    g_fc = lax.dynamic_slice_in_dim(g_fc_f, chip * 704, 704, axis=2)
    g_meta = lax.dynamic_slice_in_dim(g_meta_f, chip * 256, 256, axis=1)

    all_names = [n for names, _, _ in begun for n in names]
    red_big = dict(zip(all_names, reduce_end([p for _, ps, _ in begun for p in ps],
                                             [g for _, _, gs_ in begun for g in gs_], all_names, chip_idx)))
    g_w_in, g_w_out, g_wq, g_wk, g_wv, g_wo, g_up0, g_up1, g_dn0, g_dn1 = [
        red_big[n] for n in ("w_in", "w_out", "wq", "wk", "wv", "wo", "up0", "up1", "down0", "down1")]

    grads = dict(meta_tokens=g_meta, attn_norm_w=g_anw, ffn_norm_w=g_fnw, mix_w_in=g_w_in, conv_a_w=g_ca,
                 dn_conv_w=g_dc, dn_a_log=g_alog, dn_dt_bias=g_dtb, dn_norm_w=g_dnw, mix_w_out=g_w_out,
                 swa_wq=g_wq, swa_wk=g_wk, swa_wv=g_wv, swa_q_norm_w=g_qw, swa_k_norm_w=g_kw, swa_sinks=g_sinks,
                 swa_wo=g_wo, ffn_w_up=[g_up0, g_up1], ffn_conv_w=g_fc, ffn_w_down=[g_dn0, g_dn1])
    weights = dict(meta_tokens=meta_tokens, attn_norm_w=attn_norm_w, ffn_norm_w=ffn_norm_w, mix_w_in=mix_w_in,
                   conv_a_w=conv_a_w, dn_conv_w=dn_conv_w, dn_a_log=dn_a_log, dn_dt_bias=dn_dt_bias,
                   dn_norm_w=dn_norm_w, mix_w_out=mix_w_out, swa_wq=swa_wq, swa_wk=swa_wk, swa_wv=swa_wv,
                   swa_q_norm_w=swa_q_norm_w, swa_k_norm_w=swa_k_norm_w, swa_sinks=swa_sinks, swa_wo=swa_wo,
                   ffn_w_up=ffn_w_up, ffn_conv_w=ffn_conv_w, ffn_w_down=ffn_w_down)
    m_in = dict(meta_tokens=m_meta_tokens, attn_norm_w=m_attn_norm_w, ffn_norm_w=m_ffn_norm_w, mix_w_in=m_mix_w_in,
                conv_a_w=m_conv_a_w, dn_conv_w=m_dn_conv_w, dn_a_log=m_dn_a_log, dn_dt_bias=m_dn_dt_bias,
                dn_norm_w=m_dn_norm_w, mix_w_out=m_mix_w_out, swa_wq=m_swa_wq, swa_wk=m_swa_wk, swa_wv=m_swa_wv,
                swa_q_norm_w=m_swa_q_norm_w, swa_k_norm_w=m_swa_k_norm_w, swa_sinks=m_swa_sinks, swa_wo=m_swa_wo,
                ffn_w_up=m_ffn_w_up, ffn_conv_w=m_ffn_conv_w, ffn_w_down=m_ffn_w_down)
    v_in = dict(meta_tokens=v_meta_tokens, attn_norm_w=v_attn_norm_w, ffn_norm_w=v_ffn_norm_w, mix_w_in=v_mix_w_in,
                conv_a_w=v_conv_a_w, dn_conv_w=v_dn_conv_w, dn_a_log=v_dn_a_log, dn_dt_bias=v_dn_dt_bias,
                dn_norm_w=v_dn_norm_w, mix_w_out=v_mix_w_out, swa_wq=v_swa_wq, swa_wk=v_swa_wk, swa_wv=v_swa_wv,
                swa_q_norm_w=v_swa_q_norm_w, swa_k_norm_w=v_swa_k_norm_w, swa_sinks=v_swa_sinks, swa_wo=v_swa_wo,
                ffn_w_up=v_ffn_w_up, ffn_conv_w=v_ffn_conv_w, ffn_w_down=v_ffn_w_down)
    names = list(weights)
    small = [n for n in names if n not in BIG]
    delta, new_m, new_v = {}, {}, {}
    for n in BIG:
        delta[n], new_m[n], new_v[n], grads[n] = adamw(weights[n], grads[n], m_in[n], v_in[n], name=f"adamw_{n}")
    grads = {n: grads[n].reshape(weights[n].shape) for n in names}
    shapes = [weights[n].shape for n in small]
    packed = [_flat_pad([t[n] for n in small], SW_ROWS) for t in (weights, grads, m_in, v_in)]
    for store, flat in zip((delta, new_m, new_v), adamw(*packed, name="adamw_small")):
        for n, t in zip(small, _split_flat(flat, shapes)):
            store[n] = t
    return (loss, grad_x, *[grads[n] for n in names], *[delta[n] for n in names],
            *[new_m[n] for n in names], *[new_v[n] for n in names])
```

```python
import functools

import jax
import jax.numpy as jnp
from jax import lax
from jax.experimental import pallas as pl
from jax.experimental.pallas import tpu as pltpu
from jax.experimental.pallas import tpu_sc as plsc

F32 = jnp.float32
BF16 = jnp.bfloat16
HI = lax.Precision.HIGHEST
MESH = pl.DeviceIdType.MESH

D = 1024
N_META = 16
PAD = 112
HEAD0 = PAD + N_META
D_CONV = 512
DN_H = 4
DN_D = 128
DN_DIM = 512
CH = 64
IN_DIM = 3592
P_W = 3840
BG0 = 3584
SWA_H = 16
SWA_KV = 4
SWA_D = 64
BLK = 128
D_FF = 2816
EPS = 1e-6
LR, B1, B2, AEPS, WD, STEP = 0.001, 0.9, 0.999, 1e-08, 0.01, 10
VMEM_LIMIT = 48 * 1024 * 1024
MM_VMEM_BUDGET = 34 * 1024 * 1024
R_BIG = 6144
R_HALF = R_BIG // 2
SV_ROWS = 48
SW_ROWS = 16


def _pick(n, cands):
    for c in cands:
        if n % c == 0:
            return c
    return n


def _params(sem=None):
    return pltpu.CompilerParams(dimension_semantics=sem, vmem_limit_bytes=VMEM_LIMIT)


def _dot(a, b, ca=1, cb=0, prec=None):
    return lax.dot_general(a, b, (((ca,), (cb,)), ((), ())), precision=prec,
                           preferred_element_type=F32)


def _sigmoid(x):
    return 1.0 / (1.0 + jnp.exp(-x))


def _silu(x):
    return x * _sigmoid(x)


def _dsilu(x):
    s = _sigmoid(x)
    return s * (1.0 + x * (1.0 - s))


def _softplus(x):
    return jnp.maximum(x, 0.0) + jnp.log(1.0 + jnp.exp(-jnp.abs(x)))


def mm(a, b, *, name, ta=False, tb=False, out_dtype=F32, add=None, tm=None, tn=None, tk=None,
       b_chip=False, out_chip=False, epi=None, epi_ins=(), epi_consts=(), epi_outs=(), epi_accs=()):
    if epi is not None:
        return _mm_epi(a, b, name=name, tb=tb, tn=tn, b_chip=b_chip, epi=epi, epi_ins=epi_ins,
                       epi_consts=epi_consts, epi_outs=epi_outs, epi_accs=epi_accs)
    m, k = (a.shape[1], a.shape[0]) if ta else a.shape
    if b_chip:
        n = b.shape[1] if tb else 4 * b.shape[2]
        if tb:
            tk = b.shape[2]
        else:
            tn = b.shape[2]
    else:
        n = b.shape[0] if tb else b.shape[1]
    if out_chip:
        tn = n // 4
    tn = tn or _pick(n, (1408, 1024, 768, 512, 256, 128))
    tk = tk or (_pick(k, (1408, 704, 384, 128)) if ta else _pick(k, (1024, 1408, 768, 512, 128)))
    nk = k // tk
    if tm is None:
        isz = lambda t: jnp.dtype(t.dtype).itemsize
        osz = jnp.dtype(out_dtype).itemsize
        for tm in ((1408, 1024, 512, 384, 256, 128) if ta else (1408, 704, 512, 384, 256, 128)):
            need = 2 * (tm * tk * isz(a) + tk * tn * isz(b) + tm * tn * osz + (tm * tn * 4 if add is not None else 0))
            need += tm * tn * 4 if nk > 1 else 0
            if m % tm == 0 and need <= MM_VMEM_BUDGET:
                break
        else:
            tm = m
    dims = (((0 if ta else 1,), (1 if tb else 0,)), ((), ()))

    def body(*refs):
        if add is None:
            a_ref, b_ref, o_ref, acc_ref = refs
            add_ref = None
        else:
            a_ref, b_ref, add_ref, o_ref, acc_ref = refs
        part = lax.dot_general(a_ref[...].astype(BF16), b_ref[...].astype(BF16), dims,
                               preferred_element_type=F32)

        def finish(total):
            if add_ref is not None:
                total = total + add_ref[...]
            o_ref[...] = total.astype(out_dtype)

        if nk == 1:
            finish(part)
        else:
            kk = pl.program_id(2)

            @pl.when(kk == 0)
            def _():
                acc_ref[...] = part

            @pl.when(kk > 0)
            def _():
                acc_ref[...] += part

            @pl.when(kk == nk - 1)
            def _():
                finish(acc_ref[...])

    a_spec = pl.BlockSpec((tk, tm), lambda i, j, kk: (kk, i)) if ta else pl.BlockSpec((tm, tk), lambda i, j, kk: (i, kk))
    if b_chip and tb:
        b_spec = pl.BlockSpec((None, tn, tk), lambda i, j, kk: (kk, j, 0))
    elif b_chip:
        b_spec = pl.BlockSpec((None, tk, tn), lambda i, j, kk: (j, kk, 0))
    elif tb:
        b_spec = pl.BlockSpec((tn, tk), lambda i, j, kk: (j, kk))
    else:
        b_spec = pl.BlockSpec((tk, tn), lambda i, j, kk: (kk, j))
    o_spec = pl.BlockSpec((tm, tn), lambda i, j, kk: (i, j))
    in_specs = [a_spec, b_spec] + ([o_spec] if add is not None else [])
    args = [a, b] + ([add] if add is not None else [])
    out_spec = pl.BlockSpec((None, tm, tn), lambda i, j, kk: (j, i, 0)) if out_chip else o_spec
    return pl.pallas_call(
        body, name=name, interpret=False,
        out_shape=jax.ShapeDtypeStruct((4, m, tn) if out_chip else (m, n), out_dtype),
        grid=(m // tm, n // tn, nk), in_specs=in_specs, out_specs=out_spec,
        scratch_shapes=[pltpu.VMEM((tm, tn) if nk > 1 else (8, 128), F32)],
        compiler_params=_params(("parallel", "parallel", "arbitrary")),
    )(*args)


def _mm_epi(a, b, *, name, tb, tn, b_chip, epi, epi_ins, epi_consts, epi_outs, epi_accs):
    m, k = a.shape
    if b_chip:
        n = b.shape[1] if tb else 4 * b.shape[2]
        tk = b.shape[2] if tb else None
        tn = tn if tb else b.shape[2]
    else:
        n = b.shape[0] if tb else b.shape[1]
        tk = None
    tn = tn or _pick(n, (1408, 1024, 768, 512, 256, 128))
    tk = tk or _pick(k, (1024, 1408, 768, 512, 128))
    nk, nj = k // tk, n // tn
    isz = lambda t: jnp.dtype(t.dtype if hasattr(t, "dtype") else t).itemsize
    outs3 = [t if isinstance(t, tuple) else (t, n, lambda j: j) for t in epi_outs]
    side = sum(isz(t) for t, _ in epi_ins) + sum(isz(dt) for dt, _, _ in outs3)
    for tm in (1408, 704, 512, 384, 256, 128):
        need = 2 * (tm * tk * isz(a) + tk * tn * isz(b) + tm * tn * side) + (tm * tn * 4 if nk > 1 else 0)
        if m % tm == 0 and need <= MM_VMEM_BUDGET:
            break
    else:
        tm = m
    dims = (((1,), (1 if tb else 0,)), ((), ()))
    n_in, n_c, n_out, n_acc = len(epi_ins), len(epi_consts), len(epi_outs), len(epi_accs)

    def body(*refs):
        a_ref, b_ref = refs[:2]
        in_refs = refs[2:2 + n_in + n_c]
        out_refs = refs[2 + n_in + n_c:2 + n_in + n_c + n_out]
        acc_out = refs[2 + n_in + n_c + n_out:2 + n_in + n_c + n_out + n_acc]
        acc_ref = refs[-1]
        i, j, kk = pl.program_id(0), pl.program_id(1), pl.program_id(2)
        part = lax.dot_general(a_ref[...].astype(BF16), b_ref[...].astype(BF16), dims,
                               preferred_element_type=F32)

        def finish(total):
            res = epi(i * tm, total, *[r[...] for r in in_refs])
            if not isinstance(res, (tuple, list)):
                res = (res,)
            for r, v in zip(out_refs, res[:n_out]):
                r[...] = v.astype(r.dtype)
            if n_acc:
                @pl.when(jnp.logical_and(i == 0, j == 0))
                def _():
                    for r in acc_out:
                        r[...] = jnp.zeros(r.shape, r.dtype)

                for r, v in zip(acc_out, res[n_out:]):
                    r[...] += jnp.broadcast_to(v, r.shape).astype(r.dtype)

        if nk == 1:
            finish(part)
        else:
            @pl.when(kk == 0)
            def _():
                acc_ref[...] = part

            @pl.when(kk > 0)
            def _():
                acc_ref[...] += part

            @pl.when(kk == nk - 1)
            def _():
                finish(acc_ref[...])

    a_spec = pl.BlockSpec((tm, tk), lambda i, j, kk: (i, kk))
    if b_chip and tb:
        b_spec = pl.BlockSpec((None, tn, tk), lambda i, j, kk: (kk, j, 0))
    elif b_chip:
        b_spec = pl.BlockSpec((None, tk, tn), lambda i, j, kk: (j, kk, 0))
    elif tb:
        b_spec = pl.BlockSpec((tn, tk), lambda i, j, kk: (j, kk))
    else:
        b_spec = pl.BlockSpec((tk, tn), lambda i, j, kk: (kk, j))
    in_specs = [a_spec, b_spec]
    in_specs += [pl.BlockSpec((tm, tn), lambda i, j, kk, col=col: (i, col(j))) for _, col in epi_ins]
    in_specs += [pl.BlockSpec(t.shape, lambda i, j, kk, nd=t.ndim: (0,) * nd) for t in epi_consts]
    out_specs = [pl.BlockSpec((tm, tn), lambda i, j, kk, col=col: (i, col(j))) for _, _, col in outs3]
    out_specs += [pl.BlockSpec(s, lambda i, j, kk, nd=len(s): (0,) * nd) for s, _ in epi_accs]
    out_shape = [jax.ShapeDtypeStruct((m, width), dt) for dt, width, _ in outs3]
    out_shape += [jax.ShapeDtypeStruct(s, dt) for s, dt in epi_accs]
    sem = ("arbitrary", "arbitrary", "arbitrary") if n_acc else ("parallel", "parallel", "arbitrary")
    return pl.pallas_call(
        body, name=name, interpret=False, out_shape=out_shape,
        grid=(m // tm, nj, nk), in_specs=in_specs, out_specs=out_specs,
        scratch_shapes=[pltpu.VMEM((tm, tn) if nk > 1 else (8, 128), F32)],
        compiler_params=_params(sem),
    )(a, b, *[t for t, _ in epi_ins], *epi_consts)


def cols(arr, tr, width=None, cb=0):
    width = width or arr.shape[1]
    return (arr, (tr, width), lambda i: (i, cb), "r2")


def heads(arr, tr):
    return (arr, (arr.shape[0], tr, arr.shape[2]), lambda i: (0, i, 0), "r3")


def whole(arr):
    nd = arr.ndim
    return (arr, arr.shape, lambda i: (0,) * nd, "w")


STRIP = 16


def _rows_of(ref, kind, r0, n):
    if kind == "r2":
        return ref[pl.ds(r0, n), :]
    if kind == "r3":
        return ref[:, pl.ds(r0, n), :]
    return ref[...]


def _set_rows(ref, kind, r0, n, v):
    if kind == "r2":
        ref[pl.ds(r0, n), :] = v.astype(ref.dtype)
    elif kind == "r3":
        ref[:, pl.ds(r0, n), :] = v.astype(ref.dtype)
    else:
        ref[...] = v.astype(ref.dtype)


def rowwise(fn, ins, outs, *, steps, name, accs=(), strip=None):
    n_in, n_out, n_acc = len(ins), len(outs), len(accs)
    kin = [t[3] for t in ins]
    kout = [t[4] for t in outs]
    tr = next((t[1][0] if t[3] == "r2" else t[1][1] for t in ins if t[3] != "w"), 0)

    def body(*refs):
        i = pl.program_id(0)
        in_refs, out_refs, acc_refs = refs[:n_in], refs[n_in:n_in + n_out], refs[n_in + n_out:]
        if n_acc:
            @pl.when(i == 0)
            def _():
                for r in acc_refs:
                    r[...] = jnp.zeros(r.shape, r.dtype)

        def run(r0, n):
            res = fn(i * tr + r0, *[_rows_of(r, k, r0, n) for r, k in zip(in_refs, kin)])
            if not isinstance(res, (tuple, list)):
                res = (res,)
            for r, k, v in zip(out_refs, kout, res[:n_out]):
                _set_rows(r, k, r0, n, v)
            for r, v in zip(acc_refs, res[n_out:]):
                r[...] += jnp.broadcast_to(v, r.shape).astype(r.dtype)

        if strip is None or tr <= strip:
            run(0, tr)
        else:
            def step(s, carry):
                run(pl.multiple_of(s * strip, strip), strip)
                return carry
            lax.fori_loop(0, tr // strip, step, 0)

    def zmap(nd):
        return lambda i: (0,) * nd

    in_specs = [pl.BlockSpec(t[1], t[2]) for t in ins]
    out_specs = [pl.BlockSpec(t[2], t[3]) for t in outs]
    out_specs += [pl.BlockSpec(s, zmap(len(s))) for s, _ in accs]
    out_shape = [jax.ShapeDtypeStruct(t[0], t[1]) for t in outs]
    out_shape += [jax.ShapeDtypeStruct(s, d) for s, d in accs]
    res = pl.pallas_call(
        body, name=name, interpret=False, out_shape=out_shape, grid=(steps,),
        in_specs=in_specs, out_specs=out_specs,
        compiler_params=_params(("arbitrary",)),
    )(*[t[0] for t in ins])
    return res


def out2d(rows, width, dtype, tr):
    return ((rows, width), dtype, (tr, width), lambda i: (i, 0), "r2")


def conv_fwd(xs, w8, kw, *, rows, c, tc, tr, name, post, extras=(), outs=(), pre=None, strip=STRIP):
    nx, ne, no = len(xs), len(extras), len(outs)
    nr, nc = rows // tr, c // tc
    r8 = tr // 8
    st = strip

    def body(*refs):
        x_refs = refs[:2 * nx]
        w_ref = refs[2 * nx]
        e_refs = refs[2 * nx + 1:2 * nx + 1 + ne]
        o_refs = refs[2 * nx + 1 + ne:2 * nx + 1 + ne + no]
        scr = refs[-1]
        j, i = pl.program_id(0), pl.program_id(1)
        halo = [x_refs[2 * q + 1][...].astype(F32) for q in range(nx)]
        scr[0:8, :] = jnp.where(i > 0, pre(*halo) if pre else halo[0], 0.0)

        def fill(s, carry):
            r0 = pl.multiple_of(s * st, st)
            cur = [x_refs[2 * q][pl.ds(r0, st), :].astype(F32) for q in range(nx)]
            scr[pl.ds(8 + r0, st), :] = pre(*cur) if pre else cur[0]
            return carry

        def comp(s, carry):
            r0 = pl.multiple_of(s * st, st)
            win = scr[pl.ds(r0, st + 8), :]
            y = jnp.zeros((st, tc), F32)
            for q in range(kw):
                sh = kw - 1 - q
                y = y + w_ref[q:q + 1, :] * win[8 - sh:8 - sh + st]
            res = post(j, y, *[e[pl.ds(r0, st), :] for e in e_refs])
            if not isinstance(res, (tuple, list)):
                res = (res,)
            for r, v in zip(o_refs, res):
                r[pl.ds(r0, st), :] = v.astype(r.dtype)
            return carry

        lax.fori_loop(0, tr // st, fill, 0)
        lax.fori_loop(0, tr // st, comp, 0)

    in_specs, args = [], []
    for arr, cb0 in xs:
        in_specs.append(pl.BlockSpec((tr, tc), lambda j, i, cb0=cb0: (i, cb0 + j)))
        in_specs.append(pl.BlockSpec((8, tc), lambda j, i, cb0=cb0: (jnp.maximum(i * r8 - 1, 0), cb0 + j)))
        args += [arr, arr]
    in_specs.append(pl.BlockSpec((8, tc), lambda j, i: (0, j)))
    args.append(w8)
    for arr, cb0 in extras:
        in_specs.append(pl.BlockSpec((tr, tc), lambda j, i, cb0=cb0: (i, cb0 + j)))
        args.append(arr)
    return pl.pallas_call(
        body, name=name, interpret=False,
        out_shape=[jax.ShapeDtypeStruct((rows, c), dt) for dt in outs],
        grid=(nc, nr), in_specs=in_specs,
        out_specs=[pl.BlockSpec((tr, tc), lambda j, i: (i, j)) for _ in outs],
        scratch_shapes=[pltpu.VMEM((tr + 8, tc), F32)],
        compiler_params=_params(("parallel", "arbitrary")),
    )(*args)


def conv_bwd(xs, w8, kw, dy, *, rows, c, tc, tr, name, post, extras=(), outs=(), pre=None, into=None):
    nx, ne, no = len(xs), len(extras), len(outs)
    nr, nc = rows // tr, c // tc
    r8 = tr // 8

    def body(*refs):
        x_refs = refs[:2 * nx]
        w_ref, dy_ref, dyn_ref = refs[2 * nx:2 * nx + 3]
        e_refs = refs[2 * nx + 3:2 * nx + 3 + ne]
        first_out = 2 * nx + 3 + ne + (1 if into is not None else 0)
        o_refs = refs[first_out:first_out + no]
        dw_ref = refs[first_out + no]
        xscr, gscr = refs[-2], refs[-1]
        i = pl.program_id(1)
        halo = [x_refs[2 * q + 1][...].astype(F32) for q in range(nx)]
        xscr[0:8, :] = jnp.where(i > 0, pre(*halo) if pre else halo[0], 0.0)
        gscr[tr:tr + 8, :] = jnp.where(i < nr - 1, dyn_ref[...].astype(F32), 0.0)

        def fill(s, carry):
            r0 = pl.multiple_of(s * STRIP, STRIP)
            cur = [x_refs[2 * q][pl.ds(r0, STRIP), :].astype(F32) for q in range(nx)]
            xscr[pl.ds(8 + r0, STRIP), :] = pre(*cur) if pre else cur[0]
            gscr[pl.ds(r0, STRIP), :] = dy_ref[pl.ds(r0, STRIP), :].astype(F32)
            return carry

        def comp(s, dws):
            r0 = pl.multiple_of(s * STRIP, STRIP)
            gwin = gscr[pl.ds(r0, STRIP + 8), :]
            xwin = xscr[pl.ds(r0, STRIP + 8), :]
            g = gwin[0:STRIP]
            dx = jnp.zeros((STRIP, tc), F32)
            new = []
            for q in range(kw):
                sh = kw - 1 - q
                dx = dx + w_ref[q:q + 1, :] * gwin[sh:sh + STRIP]
                part = g * xwin[8 - sh:8 - sh + STRIP]
                new.append(dws[q] + part[0:8] + part[8:16])
            res = post(dx, *[e[pl.ds(r0, STRIP), :] for e in e_refs])
            if not isinstance(res, (tuple, list)):
                res = (res,)
            for r, v in zip(o_refs, res):
                r[pl.ds(r0, STRIP), :] = v.astype(r.dtype)
            return tuple(new)

        lax.fori_loop(0, tr // STRIP, fill, 0)
        dws = lax.fori_loop(0, tr // STRIP, comp, tuple(jnp.zeros((8, tc), F32) for _ in range(kw)))

        @pl.when(i == 0)
        def _():
            dw_ref[...] = jnp.zeros((8, tc), F32)

        dw_ref[...] += jnp.concatenate([jnp.sum(t, axis=0, keepdims=True) for t in dws]
                                       + [jnp.zeros((8 - kw, tc), F32)], axis=0)

    in_specs, args = [], []
    for arr, cb0 in xs:
        in_specs.append(pl.BlockSpec((tr, tc), lambda j, i, cb0=cb0: (i, cb0 + j)))
        in_specs.append(pl.BlockSpec((8, tc), lambda j, i, cb0=cb0: (jnp.maximum(i * r8 - 1, 0), cb0 + j)))
        args += [arr, arr]
    in_specs.append(pl.BlockSpec((8, tc), lambda j, i: (0, j)))
    in_specs.append(pl.BlockSpec((tr, tc), lambda j, i: (i, j)))
    in_specs.append(pl.BlockSpec((8, tc), lambda j, i: (jnp.minimum((i + 1) * r8, nr * r8 - 1), j)))
    args += [w8, dy, dy]
    for arr, cb0 in extras:
        in_specs.append(pl.BlockSpec((tr, tc), lambda j, i, cb0=cb0: (i, cb0 + j)))
        args.append(arr)
    out_shape = [jax.ShapeDtypeStruct((rows, c), dt) for dt in outs]
    out_specs = [pl.BlockSpec((tr, tc), lambda j, i: (i, j)) for _ in outs]
    aliases = {}
    if into is not None:
        arr, cb0 = into
        aliases = {len(args): 0}
        in_specs.append(pl.BlockSpec(memory_space=pl.ANY))
        args.append(arr)
        out_shape[0] = jax.ShapeDtypeStruct(arr.shape, arr.dtype)
        out_specs[0] = pl.BlockSpec((tr, tc), lambda j, i, cb0=cb0: (i, cb0 + j))
    return pl.pallas_call(
        body, name=name, interpret=False,
        out_shape=out_shape + [jax.ShapeDtypeStruct((8, c), F32)],
        grid=(nc, nr), in_specs=in_specs,
        out_specs=out_specs + [pl.BlockSpec((8, tc), lambda j, i: (0, j))],
        scratch_shapes=[pltpu.VMEM((tr + 8, tc), F32), pltpu.VMEM((tr + 8, tc), F32)],
        input_output_aliases=aliases,
        compiler_params=_params(("parallel", "arbitrary")),
    )(*args)


def rms_fwd(h, w, *, name):
    rows = h.shape[0]
    tr = _pick(rows, (384, 128))

    def fn(i, x, wv):
        r = lax.rsqrt(jnp.mean(x * x, axis=1, keepdims=True) + EPS)
        return x * r * wv

    return rowwise(fn, [cols(h, tr), whole(w)], [out2d(rows, D, BF16, tr)], steps=rows // tr, name=name)[0]


def _rms_bwd_epi(row0, g, x, dr, wv):
    r = lax.rsqrt(jnp.mean(x * x, axis=1, keepdims=True) + EPS)
    xh = x * r
    gw = g * wv
    dx = r * (gw - xh * jnp.mean(gw * xh, axis=1, keepdims=True))
    row = row0 + lax.broadcasted_iota(jnp.int32, (x.shape[0], 1), 0)
    return jnp.where(row >= PAD, dr + dx, 0.0), jnp.sum(g * xh, axis=0, keepdims=True)


def dx_rms_bwd(dy, w, h, nw, dres, *, name, b_chip=False):
    return mm(dy, w, tb=True, b_chip=b_chip, tn=D, name=name, epi=_rms_bwd_epi,
              epi_ins=[(h, lambda j: 0), (dres, lambda j: 0)], epi_consts=[nw], epi_outs=[F32],
              epi_accs=[((1, D), F32)])


def loss_grad(h, target):
    rows = h.shape[0]

    def fn(i, y, t):
        diff = jnp.where(i >= HEAD0, y - t, 0.0)
        part = jnp.sum(jnp.sum(diff * diff, axis=1, keepdims=True), axis=0, keepdims=True)
        return diff * (1.0 / D), part * (0.5 / D)

    tgt = (target, (BLK, D), lambda i: (jnp.maximum(i - 1, 0), 0), "r2")
    return rowwise(fn, [cols(h, BLK), tgt], [out2d(rows, D, F32, BLK)], steps=rows // BLK,
                   name="loss_grad", accs=[((1, 128), F32)])


def adamw(w, g, m, v, *, name):
    shape = w.shape
    gs = list(g) if isinstance(g, (list, tuple)) else [g]
    nl = len(gs)
    w2, m2, v2 = (t.reshape(-1, shape[-1]) for t in (w, m, v))
    rows, width = w2.shape
    rl = rows // nl
    tr = _pick(rl, (256, 176, 128, 64, 16, 8))
    nr = rl // tr

    def fn(i, wv, mv, vv, *gvs):
        gv = gvs[0]
        for layer in range(1, nl):
            gv = jnp.where(i >= layer * rl, gvs[layer], gv)
        mn = B1 * mv + (1.0 - B1) * gv
        vn = B2 * vv + (1.0 - B2) * gv * gv
        mh = mn / (1.0 - B1 ** STEP)
        vh = vn / (1.0 - B2 ** STEP)
        return -LR * (mh / (jnp.sqrt(vh) + AEPS) + WD * wv), mn, vn, gv

    g_ins = [(t.reshape(rl, width), (tr, width), lambda i, layer=layer: (jnp.clip(i - layer * nr, 0, nr - 1), 0), "r2")
             for layer, t in enumerate(gs)]
    res = rowwise(fn, [cols(t, tr) for t in (w2, m2, v2)] + g_ins, [out2d(rows, width, F32, tr)] * 4,
                  steps=rows // tr, name=name)
    return [r.reshape(shape) for r in res]


HB = DN_H * CH


def _split(a):
    hi = a.astype(BF16)
    return hi, (a - hi.astype(F32)).astype(BF16)


def _dot1(a, b, ca=1, cb=0):
    return _dot(a.astype(BF16), b.astype(BF16), ca, cb)


def _dot3(a, b, ca=1, cb=0):
    ah, al = _split(a)
    bh, bl = _split(b)
    return _dot(ah, bh, ca, cb) + (_dot(ah, bl, ca, cb) + _dot(al, bh, ca, cb))


def _dot01(m01, b, ca=1, cb=0):
    bh, bl = _split(b)
    m = m01.astype(BF16)
    return _dot(m, bh, ca, cb) + _dot(m, bl, ca, cb)


def _stack(x):
    return jnp.concatenate([x[:, h * DN_D:(h + 1) * DN_D] for h in range(DN_H)], axis=0)


def _unstack(x):
    return jnp.concatenate([x[h * CH:(h + 1) * CH] for h in range(DN_H)], axis=1)


def _tri_inv(a, blk, eye):
    ad = jnp.where(blk, a, 0.0)
    lo = a - ad
    a2 = _dot3(ad, ad)
    a4 = _dot3(a2, a2)
    a8 = _dot3(a4, a4)
    dgi = _dot3(_dot3(_dot3(eye - ad, eye + a2), eye + a4), eye + a8)
    n = _dot3(dgi, lo)
    return _dot3(_dot3(eye - n, eye + _dot3(n, n)), dgi)


def _dn_masks():
    row = lax.broadcasted_iota(jnp.int32, (HB, HB), 0)
    col = lax.broadcasted_iota(jnp.int32, (HB, HB), 1)
    same = (row // CH) == (col // CH)
    incl = jnp.logical_and(same, row >= col)
    strict = jnp.logical_and(same, row > col)
    upper = jnp.logical_and(same, row <= col)
    blk = (row // 16) == (col // 16)
    eye = (row == col).astype(F32)
    return incl, strict, upper, blk, eye


def _dn_chunk(q_ref, k_ref, v_ref, bc_ref, br_ref, incl, strict):
    r64 = lax.broadcasted_iota(jnp.int32, (CH, CH), 0)
    c64 = lax.broadcasted_iota(jnp.int32, (CH, CH), 1)
    bc = bc_ref[...]
    dcol = _dot01((r64 >= c64).astype(F32), bc)
    drow = _dot3(br_ref[0], (r64 <= c64).astype(F32))
    col = lambda m, l0: jnp.concatenate([m[:, l0 + h:l0 + h + 1] for h in range(DN_H)], axis=0)
    b_c = col(bc, 0)
    d_c = col(dcol, 4)
    d_r = jnp.concatenate([drow[4 + h:5 + h, :] for h in range(DN_H)], axis=1)
    d_last_h = [dcol[CH - 1:CH, 4 + h:5 + h] for h in range(DN_H)]
    d_last = jnp.concatenate([jnp.broadcast_to(t, (CH, 1)) for t in d_last_h], axis=0)
    q, k, v = _stack(q_ref[...]), _stack(k_ref[...]), _stack(v_ref[...])
    dm = jnp.where(incl, jnp.exp(jnp.where(incl, d_c - d_r, 0.0)), 0.0)
    kk = _dot1(k, k, 1, 1)
    a = jnp.where(strict, b_c * kk * dm, 0.0)
    ed = jnp.exp(d_c)
    rhs = jnp.concatenate([v * b_c, k * (b_c * ed)], axis=1)
    qk = _dot1(q, k, 1, 1) * dm
    ekd = jnp.exp(d_last - d_c)
    gl = [jnp.exp(t) for t in d_last_h]
    return q, k, v, b_c, dm, kk, a, ed, rhs, qk, ekd, gl


def dn_fwd(qkv_n, bgcol, bgrow):
    rows = qkv_n.shape[0]
    nch = rows // CH

    def body(q_ref, k_ref, v_ref, bc_ref, br_ref, o_ref, s_out, ti_out, s_scr, prep, prep_qk, prep_gl):
        n = pl.program_id(0)

        @pl.when(n == 0)
        def _():
            s_scr[...] = jnp.zeros(s_scr.shape, F32)
            prep[...] = jnp.zeros(prep.shape, F32)
            prep_qk[...] = jnp.zeros(prep_qk.shape, F32)
            prep_gl[...] = jnp.zeros(prep_gl.shape, F32)

        u, w, qd, kd = prep[0], prep[1], prep[2], prep[3]
        qk = prep_qk[...]
        live = n > 0
        v_new, o_state = [], []
        for h in range(DN_H):
            rs = slice(h * CH, (h + 1) * CH)
            s = s_scr[h]
            s_out[0, h] = s
            vn = u[rs] - _dot1(w[rs], s)
            v_new.append(vn)
            o_state.append(_dot1(qd[rs], s))
            s_scr[h] = jnp.where(live, prep_gl[h:h + 1, 0:1] * s + _dot1(kd[rs], vn, 0, 0), s)
        o = jnp.concatenate(o_state, axis=0) + _dot1(qk, jnp.concatenate(v_new, axis=0))
        o_ref[...] = _unstack(o)

        incl, strict, _, blk, eye = _dn_masks()
        q, k, v, b_c, dm, kk, a, ed, rhs, qk_n, ekd, gl = _dn_chunk(q_ref, k_ref, v_ref, bc_ref, br_ref, incl, strict)
        tinv = _tri_inv(a, blk, eye)
        ti_out[0] = tinv
        sol = _dot3(tinv, rhs)
        prep[0] = sol[:, :DN_D]
        prep[1] = sol[:, DN_D:]
        prep[2] = q * ed
        prep[3] = k * ekd
        prep_qk[...] = qk_n
        prep_gl[...] = jnp.concatenate([jnp.broadcast_to(t, (1, 128)) for t in gl]
                                       + [jnp.zeros((8 - DN_H, 128), F32)], axis=0)

    last = nch - 1
    return pl.pallas_call(
        body, name="dn_fwd", interpret=False,
        out_shape=[jax.ShapeDtypeStruct((rows, DN_DIM), F32),
                   jax.ShapeDtypeStruct((nch, DN_H, DN_D, DN_D), F32),
                   jax.ShapeDtypeStruct((nch, HB, HB), F32)],
        grid=(nch + 1,),
        in_specs=[pl.BlockSpec((CH, DN_DIM), lambda n: (jnp.minimum(n, last), 0)),
                  pl.BlockSpec((CH, DN_DIM), lambda n: (jnp.minimum(n, last), 1)),
                  pl.BlockSpec((CH, DN_DIM), lambda n: (jnp.minimum(n, last), 2)),
                  pl.BlockSpec((CH, 128), lambda n: (jnp.minimum(n, last), 0)),
                  pl.BlockSpec((1, 8, CH), lambda n: (jnp.minimum(n, last), 0, 0))],
        out_specs=[pl.BlockSpec((CH, DN_DIM), lambda n: (jnp.maximum(n - 1, 0), 0)),
                   pl.BlockSpec((1, DN_H, DN_D, DN_D), lambda n: (jnp.maximum(n - 1, 0), 0, 0, 0)),
                   pl.BlockSpec((1, HB, HB), lambda n: (jnp.minimum(n, last), 0, 0))],
        scratch_shapes=[pltpu.VMEM((DN_H, DN_D, DN_D), F32), pltpu.VMEM((4, HB, DN_D), F32),
                        pltpu.VMEM((HB, HB), F32), pltpu.VMEM((8, 128), F32)],
        compiler_params=_params(("arbitrary",)),
    )(qkv_n, qkv_n, qkv_n, bgcol, bgrow)


def dn_bwd(qkv_n, bgcol, bgrow, s_all, ti_all, do):
    rows = qkv_n.shape[0]
    nch = rows // CH

    def body(q_ref, k_ref, v_ref, bc_ref, br_ref, s_ref, ti_ref, do_ref, dq_ref, dk_ref, dv_ref, dbg_ref, ds_scr):
        n = pl.program_id(0)

        @pl.when(n == 0)
        def _():
            ds_scr[...] = jnp.zeros(ds_scr.shape, F32)

        incl, strict, upper, _, _ = _dn_masks()
        q, k, v, b_c, dm, kk, a, ed, rhs, qk, ekd, gl = _dn_chunk(q_ref, k_ref, v_ref, bc_ref, br_ref, incl, strict)
        tinv = ti_ref[0]
        g_o = _stack(do_ref[...])
        sol = _dot3(tinv, rhs)
        u, w = sol[:, :DN_D], sol[:, DN_D:]
        qd, kd = q * ed, k * ekd
        rsum = lambda t: jnp.sum(t, axis=1, keepdims=True)
        rows_of = [slice(h * CH, (h + 1) * CH) for h in range(DN_H)]
        s_h = [s_ref[0, h] for h in range(DN_H)]
        ds_h = [ds_scr[h] for h in range(DN_H)]
        v_new = jnp.concatenate([u[rs] - _dot1(w[rs], s) for rs, s in zip(rows_of, s_h)], axis=0)
        dv_new = _dot1(qk, g_o, 0, 0) + jnp.concatenate([_dot1(kd[rs], t) for rs, t in zip(rows_of, ds_h)], axis=0)
        dqd = jnp.concatenate([_dot1(g_o[rs], s, 1, 1) for rs, s in zip(rows_of, s_h)], axis=0)
        dkd = jnp.concatenate([_dot1(v_new[rs], t, 1, 1) for rs, t in zip(rows_of, ds_h)], axis=0)
        for h, rs in enumerate(rows_of):
            ds_scr[h] = _dot1(qd[rs], g_o[rs], 0, 0) + gl[h] * ds_h[h] - _dot1(w[rs], dv_new[rs], 0, 0)
        dw = jnp.concatenate([-_dot1(dv_new[rs], s, 1, 1) for rs, s in zip(rows_of, s_h)], axis=0)
        dqk = _dot1(g_o, v_new, 1, 1)
        drhs = _dot3(tinv, jnp.concatenate([dv_new, dw], axis=1), 0, 0)
        da = jnp.where(strict, -_dot1(drhs, sol, 1, 1), 0.0)
        drhs_u, drhs_w = drhs[:, :DN_D], drhs[:, DN_D:]
        s2 = rsum(drhs_w * k)
        dbeta = rsum(drhs_u * v) + s2 * ed + rsum(da * kk * dm)
        dkk = da * b_c * dm
        dqkr = dqk * dm
        mmat = da * a + dqk * qk
        tmp = rsum(dkd * kd)
        dd = (s2 * b_c * ed + rsum(mmat) - _dot3(mmat, jnp.ones((HB, 128), F32), 0, 0)[:, :1] + rsum(dqd * qd) - tmp)
        rowi = lax.broadcasted_iota(jnp.int32, (CH, 1), 0)
        last = []
        for h, rs in enumerate(rows_of):
            dgl = jnp.sum(rsum(s_h[h] * ds_h[h]), axis=0, keepdims=True)
            dd_last = jnp.sum(tmp[rs], axis=0, keepdims=True) + dgl * gl[h]
            last.append(jnp.where(rowi == CH - 1, dd_last, 0.0))
        dd = dd + jnp.concatenate(last, axis=0)
        dq_ref[...] = _unstack(_dot1(dqkr, k) + dqd * ed)
        dk_ref[...] = _unstack(drhs_w * (b_c * ed) + _dot1(dkk, k) + _dot1(dkk, k, 0, 0) + _dot1(dqkr, q, 0, 0)
                               + dkd * ekd)
        dv_ref[...] = _unstack(drhs_u * b_c)
        dg = _dot01(upper.astype(F32), jnp.broadcast_to(dd, (HB, 128)))[:, :1]
        lane = lax.broadcasted_iota(jnp.int32, (CH, 128), 1)
        out = jnp.zeros((CH, 128), F32)
        for h, rs in enumerate(rows_of):
            out = out + jnp.where(lane == h, dbeta[rs], 0.0) + jnp.where(lane == 4 + h, dg[rs], 0.0)
        dbg_ref[...] = out

    rev = lambda n: nch - 1 - n
    return pl.pallas_call(
        body, name="dn_bwd", interpret=False,
        out_shape=[jax.ShapeDtypeStruct((rows, DN_DIM), F32)] * 3 + [jax.ShapeDtypeStruct((rows, 128), F32)],
        grid=(nch,),
        in_specs=[pl.BlockSpec((CH, DN_DIM), lambda n: (rev(n), 0)),
                  pl.BlockSpec((CH, DN_DIM), lambda n: (rev(n), 1)),
                  pl.BlockSpec((CH, DN_DIM), lambda n: (rev(n), 2)),
                  pl.BlockSpec((CH, 128), lambda n: (rev(n), 0)),
                  pl.BlockSpec((1, 8, CH), lambda n: (rev(n), 0, 0)),
                  pl.BlockSpec((1, DN_H, DN_D, DN_D), lambda n: (rev(n), 0, 0, 0)),
                  pl.BlockSpec((1, HB, HB), lambda n: (rev(n), 0, 0)),
                  pl.BlockSpec((CH, DN_DIM), lambda n: (rev(n), 0))],
        out_specs=[pl.BlockSpec((CH, DN_DIM), lambda n: (rev(n), 0))] * 3 + [pl.BlockSpec((CH, 128), lambda n: (rev(n), 0))],
        scratch_shapes=[pltpu.VMEM((DN_H, DN_D, DN_D), F32)],
        compiler_params=_params(("arbitrary",)),
    )(qkv_n, qkv_n, qkv_n, bgcol, bgrow, s_all, ti_all, do)


def _swa_valid(n):
    c3 = lax.broadcasted_iota(jnp.int32, (3 * BLK, 4 * BLK), 0)
    r = lax.broadcasted_iota(jnp.int32, (3 * BLK, 4 * BLK), 1) % BLK
    c = c3 % BLK
    lo = jnp.where(c3 < BLK, PAD, jnp.where(c3 < 2 * BLK, r + 1 + jnp.where(n >= 2, 0, BLK), 0))
    hi = jnp.where(c3 < BLK, r + jnp.where(n >= 1, BLK, 0), jnp.where(c3 < 2 * BLK, BLK, r - jnp.where(n >= 1, 0, BLK)))
    return jnp.logical_and(c >= lo, c <= hi)


def _swa_probs(q, kcat, valid, sink):
    s = jnp.where(valid, _dot(kcat, q, 1, 1), -1e30)
    m = jnp.maximum(jnp.max(s, axis=0, keepdims=True), sink)
    e = jnp.where(valid, jnp.exp(s - m), 0.0)
    es = jnp.exp(sink - m)
    inv = 1.0 / (jnp.sum(e, axis=0, keepdims=True) + es)
    return e * inv, es * inv


def _swa_group(q_ref, sk_ref, h):
    q4 = jnp.concatenate([q_ref[4 * h + g] for g in range(4)], axis=0)
    sink4 = jnp.concatenate([jnp.full((1, BLK), sk_ref[4 * h + g], F32) for g in range(4)], axis=1)
    return q4, sink4


def _swa_specs():
    q = pl.BlockSpec((SWA_H, BLK, SWA_D), lambda n: (0, n, 0))
    km = pl.BlockSpec((SWA_KV, BLK, SWA_D), lambda n: (0, 0, 0))
    kp = pl.BlockSpec((SWA_KV, BLK, SWA_D), lambda n: (0, jnp.maximum(n - 1, 0), 0))
    kc = pl.BlockSpec((SWA_KV, BLK, SWA_D), lambda n: (0, n, 0))
    return [q, km, kp, kc, km, kp, kc]


def swa_fwd(qh, kh, vh, sinks):
    rows = qh.shape[1]
    nb = rows // BLK

    def body(q_ref, km, kp, kc, vm, vp, vc, sk_ref, o_ref):
        n = pl.program_id(0)
        valid = _swa_valid(n)
        outs = []
        for h in range(SWA_KV):
            kcat = jnp.concatenate([km[h], kp[h], kc[h]], axis=0)
            vcat = jnp.concatenate([vm[h], vp[h], vc[h]], axis=0)
            q4, sink4 = _swa_group(q_ref, sk_ref, h)
            p, _ = _swa_probs(q4, kcat, valid, sink4)
            o4 = _dot(p.astype(BF16), vcat, 0, 0)
            outs += [o4[g * BLK:(g + 1) * BLK] for g in range(4)]
        o_ref[...] = jnp.concatenate(outs, axis=1).astype(BF16)

    return pl.pallas_call(
        body, name="swa_fwd", interpret=False,
        out_shape=jax.ShapeDtypeStruct((rows, SWA_H * SWA_D), BF16),
        grid=(nb,),
        in_specs=_swa_specs() + [pl.BlockSpec(memory_space=pltpu.SMEM)],
        out_specs=pl.BlockSpec((BLK, SWA_H * SWA_D), lambda n: (n, 0)),
        compiler_params=_params(("parallel",)),
    )(qh, kh, kh, kh, vh, vh, vh, sinks)


def swa_bwd(qh, kh, vh, sinks, do):
    rows = qh.shape[1]
    nb = rows // BLK

    def body(q_ref, km, kp, kc, vm, vp, vc, do_ref, sk_ref, dq_ref, dk_ref, dv_ref, dsk_ref):
        n = pl.program_id(0)

        @pl.when(n == 0)
        def _():
            dk_ref[...] = jnp.zeros(dk_ref.shape, F32)
            dv_ref[...] = jnp.zeros(dv_ref.shape, F32)

        valid = _swa_valid(n)
        g_all = do_ref[...]
        rowi = lax.broadcasted_iota(jnp.int32, (SWA_H, 128), 0)
        dsk = jnp.zeros((SWA_H, 128), F32)
        pm = pl.multiple_of(jnp.maximum(n - 1, 0) * BLK, BLK)
        pc = pl.multiple_of(n * BLK, BLK)
        for h in range(SWA_KV):
            kcat = jnp.concatenate([km[h], kp[h], kc[h]], axis=0)
            vcat = jnp.concatenate([vm[h], vp[h], vc[h]], axis=0)
            q4, sink4 = _swa_group(q_ref, sk_ref, h)
            p, ps = _swa_probs(q4, kcat, valid, sink4)
            g4 = jnp.concatenate([g_all[:, (4 * h + g) * SWA_D:(4 * h + g + 1) * SWA_D] for g in range(4)], axis=0)
            dp = _dot(vcat, g4, 1, 1)
            delta = jnp.sum(p * dp, axis=0, keepdims=True)
            ds = (p * (dp - delta)).astype(BF16)
            dq4 = _dot(ds, kcat, 0, 0)
            dkc = _dot(ds, q4)
            dvc = _dot(p.astype(BF16), g4)
            t = ps * delta
            for g in range(4):
                dq_ref[4 * h + g] = dq4[g * BLK:(g + 1) * BLK]
                part = -jnp.sum(t[:, g * BLK:(g + 1) * BLK], axis=1, keepdims=True)
                dsk = dsk + jnp.where(rowi == 4 * h + g, part, 0.0)
            lanes = slice(h * SWA_D, (h + 1) * SWA_D)
            for ref, val in ((dk_ref, dkc), (dv_ref, dvc)):
                ref[0:BLK, lanes] += val[0:BLK]
                ref[pl.ds(pm, BLK), lanes] += val[BLK:2 * BLK]
                ref[pl.ds(pc, BLK), lanes] += val[2 * BLK:]
        dsk_ref[0] = dsk

    return pl.pallas_call(
        body, name="swa_bwd", interpret=False,
        out_shape=[jax.ShapeDtypeStruct((SWA_H, rows, SWA_D), F32),
                   jax.ShapeDtypeStruct((rows, SWA_KV * SWA_D), F32),
                   jax.ShapeDtypeStruct((rows, SWA_KV * SWA_D), F32),
                   jax.ShapeDtypeStruct((nb, SWA_H, 128), F32)],
        grid=(nb,),
        in_specs=_swa_specs() + [pl.BlockSpec((BLK, SWA_H * SWA_D), lambda n: (n, 0)),
                                 pl.BlockSpec(memory_space=pltpu.SMEM)],
        out_specs=[pl.BlockSpec((SWA_H, BLK, SWA_D), lambda n: (0, n, 0)),
                   pl.BlockSpec((rows, SWA_KV * SWA_D), lambda n: (0, 0)),
                   pl.BlockSpec((rows, SWA_KV * SWA_D), lambda n: (0, 0)),
                   pl.BlockSpec((1, SWA_H, 128), lambda n: (n, 0, 0))],
        compiler_params=_params(("arbitrary",)),
    )(qh, kh, kh, kh, vh, vh, vh, do, sinks)


def qknorm_fwd(qkv, qw, kw):
    rows = qkv.shape[0]
    tr = _pick(rows, (384, 128))
    scale = SWA_D ** -0.5

    def fn(i, x, qwv, kwv):
        def normed(j, wv, sc):
            xs = x[:, j * SWA_D:(j + 1) * SWA_D]
            r = lax.rsqrt(jnp.mean(xs * xs, axis=1, keepdims=True) + EPS)
            return (xs * r * wv * sc)[None]
        qo = jnp.concatenate([normed(j, qwv, scale) for j in range(SWA_H)], axis=0)
        ko = jnp.concatenate([normed(SWA_H + j, kwv, 1.0) for j in range(SWA_KV)], axis=0)
        vo = jnp.concatenate([x[:, (SWA_H + SWA_KV + j) * SWA_D:(SWA_H + SWA_KV + j + 1) * SWA_D][None]
                              for j in range(SWA_KV)], axis=0)
        return qo, ko, vo

    hm = lambda nh: ((nh, rows, SWA_D), BF16, (nh, tr, SWA_D), lambda i: (0, i, 0), "r3")
    return rowwise(fn, [cols(qkv, tr), whole(qw), whole(kw)], [hm(SWA_H), hm(SWA_KV), hm(SWA_KV)],
                   steps=rows // tr, name="qknorm_fwd")


def qknorm_bwd(qkv, qw, kw, dqh, dkh, dvh):
    rows = qkv.shape[0]
    tr = _pick(rows, (384, 128))
    scale = SWA_D ** -0.5

    def fn(i, x, qwv, kwv, dq, dk, dv):
        pieces = []
        dws = [jnp.zeros((1, SWA_D), F32), jnp.zeros((1, SWA_D), F32)]

        def one(j, dy, wv, sc, which):
            xs = x[:, j * SWA_D:(j + 1) * SWA_D]
            r = lax.rsqrt(jnp.mean(xs * xs, axis=1, keepdims=True) + EPS)
            xh = xs * r
            gw = dy * wv * sc
            pieces.append(r * (gw - xh * jnp.mean(gw * xh, axis=1, keepdims=True)))
            dws[which] = dws[which] + jnp.sum(dy * sc * xh, axis=0, keepdims=True)

        for j in range(SWA_H):
            one(j, dq[j], qwv, scale, 0)
        for j in range(SWA_KV):
            one(SWA_H + j, dk[:, j * SWA_D:(j + 1) * SWA_D], kwv, 1.0, 1)
        pieces.append(dv)
        return jnp.concatenate(pieces, axis=1), dws[0], dws[1]

    return rowwise(fn, [cols(qkv, tr), whole(qw), whole(kw), heads(dqh, tr), cols(dkh, tr), cols(dvh, tr)],
                   [out2d(rows, 1536, BF16, tr)], steps=rows // tr, name="qknorm_bwd",
                   accs=[((1, SWA_D), F32), ((1, SWA_D), F32)])


def _place():
    return lax.axis_index("x"), lax.axis_index("y"), lax.axis_index("c")


ANY = pl.BlockSpec(memory_space=pl.ANY)


def _rcopy(ssem, rsem, k, src, dst, to):
    return pltpu.make_async_remote_copy(src_ref=src, dst_ref=dst, send_sem=ssem.at[k], recv_sem=rsem.at[k],
                                        device_id=to, device_id_type=MESH)


def gather_weights(shards, small):
    n = len(shards)
    halves = [t.shape[0] // 2 for t in shards]

    def body(*refs):
        s_refs, small_ref = refs[:n], refs[n]
        o_refs, osmall = refs[n + 1:2 * n + 1], refs[2 * n + 1]
        ssem, rsem, lsem = refs[2 * n + 2:]
        x, y, c = _place()
        me = 2 * x + y
        chips = [(1 - x, y), (x, 1 - y), (1 - x, 1 - y)]

        def half(k, s, hh):
            return o_refs[k].at[s, pl.ds(hh * halves[k], halves[k]), :]

        loc = pltpu.make_async_copy(small_ref, osmall.at[me], lsem)
        loc.start()
        sends = []
        for k in range(n):
            for j, (px, py) in enumerate(chips):
                sends.append(_rcopy(ssem, rsem, 6 * k + j, s_refs[k].at[pl.ds(c * halves[k], halves[k]), :],
                                    half(k, me, c), (px, py, c)))
        for j, (px, py) in enumerate(chips):
            sends.append(_rcopy(ssem, rsem, 6 * n + j, small_ref, osmall.at[me], (px, py, c)))
        for cp in sends:
            cp.start()
        for k in range(n):
            for j, (px, py) in enumerate(chips):
                s = 2 * px + py
                _rcopy(ssem, rsem, 6 * k + j, half(k, s, c), half(k, s, c), (x, y, c)).wait_recv()
                fwd = _rcopy(ssem, rsem, 6 * k + 3 + j, half(k, s, c), half(k, s, c), (x, y, 1 - c))
                fwd.start()
                sends.append(fwd)
        for k in range(n):
            for j, (px, py) in enumerate(chips):
                s = 2 * px + py
                _rcopy(ssem, rsem, 6 * k + 3 + j, half(k, s, 1 - c), half(k, s, 1 - c), (x, y, c)).wait_recv()
        for j, (px, py) in enumerate(chips):
            s = 2 * px + py
            _rcopy(ssem, rsem, 6 * n + j, osmall.at[s], osmall.at[s], (x, y, c)).wait_recv()
        for cp in sends:
            cp.wait_send()
        loc.wait()

    res = pl.pallas_call(
        body, name="gather_weights", interpret=False,
        out_shape=[jax.ShapeDtypeStruct((4,) + t.shape, t.dtype) for t in shards]
        + [jax.ShapeDtypeStruct((4, SW_ROWS, 1024), F32)],
        in_specs=[ANY] * (n + 1), out_specs=[ANY] * (n + 1),
        scratch_shapes=[pltpu.SemaphoreType.DMA((6 * n + 3,)), pltpu.SemaphoreType.DMA((6 * n + 3,)),
                        pltpu.SemaphoreType.DMA],
    )(*shards, small)
    return res[:n], res[n]


def _handshake(peers):
    barrier = pltpu.get_barrier_semaphore()
    for peer in peers:
        pl.semaphore_signal(barrier, inc=1, device_id=peer, device_id_type=MESH)
    pl.semaphore_wait(barrier, len(peers))


def gather_weights_beside(shards):
    n = len(shards)
    halves = [t.shape[0] // 2 for t in shards]

    def body(*refs):
        s_refs, o_refs, ssem, rsem = refs[:n], refs[n:2 * n], refs[2 * n], refs[2 * n + 1]
        x, y, c = _place()
        me = 2 * x + y
        chips = [(1 - x, y), (x, 1 - y), (1 - x, 1 - y)]
        _handshake([(px, py, c) for px, py in chips] + [(x, y, 1 - c)])

        def half(k, s, hh):
            return o_refs[k].at[s, pl.ds(hh * halves[k], halves[k]), :]

        sends = []
        for k in range(n):
            for j, (px, py) in enumerate(chips):
                sends.append(_rcopy(ssem, rsem, 6 * k + j, s_refs[k].at[pl.ds(c * halves[k], halves[k]), :],
                                    half(k, me, c), (px, py, c)))
        for cp in sends:
            cp.start()
        for k in range(n):
            for j, (px, py) in enumerate(chips):
                s = 2 * px + py
                _rcopy(ssem, rsem, 6 * k + j, half(k, s, c), half(k, s, c), (x, y, c)).wait_recv()
                fwd = _rcopy(ssem, rsem, 6 * k + 3 + j, half(k, s, c), half(k, s, c), (x, y, 1 - c))
                fwd.start()
                sends.append(fwd)
        for k in range(n):
            for j, (px, py) in enumerate(chips):
                s = 2 * px + py
                _rcopy(ssem, rsem, 6 * k + 3 + j, half(k, s, 1 - c), half(k, s, 1 - c), (x, y, c)).wait_recv()
        for cp in sends:
            cp.wait_send()

    return pl.kernel(
        body, name="gather_weights_beside",
        out_type=[jax.ShapeDtypeStruct((4,) + t.shape, t.dtype) for t in shards],
        mesh=plsc.ScalarSubcoreMesh(axis_name="sequencer", num_cores=1),
        scratch_types=[pltpu.SemaphoreType.DMA((6 * n,)), pltpu.SemaphoreType.DMA((6 * n,))],
        compiler_params=pltpu.CompilerParams(collective_id=1),
    )(*shards)


def swap_halves(gs, *, name):
    n = len(gs)

    def body(*refs):
        g_refs, o_refs, ssem, rsem = refs[:n], refs[n:2 * n], refs[2 * n], refs[2 * n + 1]
        x, y, c = _place()
        cps = []
        for k in range(n):
            hk = g_refs[k].shape[1] // 2
            cps.append(_rcopy(ssem, rsem, k, g_refs[k].at[:, pl.ds((1 - c) * hk, hk), :], o_refs[k], (x, y, 1 - c)))
        for cp in cps:
            cp.start()
        for cp in cps:
            cp.wait()

    return pl.pallas_call(
        body, name=name, interpret=False,
        out_shape=[jax.ShapeDtypeStruct((4, t.shape[1] // 2, t.shape[2]), t.dtype) for t in gs],
        in_specs=[ANY] * n, out_specs=[ANY] * n,
        scratch_shapes=[pltpu.SemaphoreType.DMA((n,)), pltpu.SemaphoreType.DMA((n,))],
    )(*gs)


def _sum_rows(hk):
    return _pick(hk, (512, 352, 256, 128))


def pair_sum(g, other, c_idx, *, name):
    _, hk, width = other.shape
    tr = _sum_rows(hk)
    nbk = hk // tr

    def body(c_ref, g_ref, o_ref, out_ref):
        out_ref[...] = (g_ref[...].astype(F32) + o_ref[...].astype(F32)).astype(BF16)

    return pl.pallas_call(
        body, name=name, interpret=False,
        out_shape=jax.ShapeDtypeStruct((4, hk, width), BF16),
        grid_spec=pltpu.PrefetchScalarGridSpec(
            num_scalar_prefetch=1, grid=(4, nbk),
            in_specs=[pl.BlockSpec((1, tr, width), lambda s, i, c_ref: (s, c_ref[0] * nbk + i, 0)),
                      pl.BlockSpec((1, tr, width), lambda s, i, c_ref: (s, i, 0))],
            out_specs=pl.BlockSpec((1, tr, width), lambda s, i, c_ref: (s, i, 0))),
        compiler_params=_params(("parallel", "parallel")),
    )(c_idx, g, other)


def chip_sum(p, got, idx, *, name):
    _, hk, width = got.shape
    tr = _sum_rows(hk)
    nbk = hk // tr

    def body(idx_ref, p_ref, g_ref, out_ref):
        acc = p_ref[0].astype(F32)
        for j in range(3):
            acc = acc + g_ref[j].astype(F32)
        out_ref[0] = acc

    return pl.pallas_call(
        body, name=name, interpret=False,
        out_shape=jax.ShapeDtypeStruct((2, hk, width), F32),
        grid_spec=pltpu.PrefetchScalarGridSpec(
            num_scalar_prefetch=1, grid=(nbk,),
            in_specs=[pl.BlockSpec((1, tr, width), lambda i, idx_ref: (idx_ref[0], i, 0)),
                      pl.BlockSpec((3, tr, width), lambda i, idx_ref: (0, i, 0))],
            out_specs=pl.BlockSpec((1, tr, width), lambda i, idx_ref: (idx_ref[1], i, 0))),
        compiler_params=_params(("parallel",)),
    )(idx, p, got)


def join_halves(qs):
    n = len(qs)

    def body(*refs):
        q_refs, o_refs, ssem, rsem = refs[:n], refs[n:2 * n], refs[2 * n], refs[2 * n + 1]
        x, y, c = _place()
        cps = [_rcopy(ssem, rsem, k, q_refs[k].at[c], o_refs[k].at[c], (x, y, 1 - c)) for k in range(n)]
        for cp in cps:
            cp.start()
        for k in range(n):
            _rcopy(ssem, rsem, k, q_refs[k].at[c], o_refs[k].at[1 - c], (x, y, 1 - c)).wait_recv()
        for cp in cps:
            cp.wait_send()

    return pl.pallas_call(
        body, name="join_halves", interpret=False,
        out_shape=[jax.ShapeDtypeStruct(t.shape, t.dtype) for t in qs],
        in_specs=[ANY] * n, out_specs=[ANY] * n, input_output_aliases={k: k for k in range(n)},
        scratch_shapes=[pltpu.SemaphoreType.DMA((n,)), pltpu.SemaphoreType.DMA((n,))],
    )(*qs)


def scatter_chips_beside(ps, cid, name):
    n = len(ps)

    def body(*refs):
        p_refs, o_refs, ssem, rsem = refs[:n], refs[n:2 * n], refs[2 * n], refs[2 * n + 1]
        x, y, c = _place()
        chips = [(1 - x, y), (x, 1 - y), (1 - x, 1 - y)]
        _handshake([(px, py, c) for px, py in chips])
        cps = [_rcopy(ssem, rsem, 3 * k + j, p_refs[k].at[2 * px + py], o_refs[k].at[j], (px, py, c))
               for k in range(n) for j, (px, py) in enumerate(chips)]
        for cp in cps:
            cp.start()
        for cp in cps:
            cp.wait()

    return pl.kernel(
        body, name=name, out_type=[jax.ShapeDtypeStruct((3,) + t.shape[1:], t.dtype) for t in ps],
        mesh=plsc.ScalarSubcoreMesh(axis_name="sequencer", num_cores=1),
        scratch_types=[pltpu.SemaphoreType.DMA((3 * n,)), pltpu.SemaphoreType.DMA((3 * n,))],
        compiler_params=pltpu.CompilerParams(collective_id=cid),
    )(*ps)


def reduce_begin(gs, names, c_idx, cid, tag):
    others = swap_halves(gs, name=f"swap_halves_{tag}")
    pairs = [pair_sum(g, o, c_idx, name=f"pair_sum_{nm}") for g, o, nm in zip(gs, others, names)]
    return pairs, scatter_chips_beside(pairs, cid, f"scatter_chips_{tag}")


def reduce_end(pairs, gots, names, idx):
    mine = [chip_sum(p, g, idx, name=f"chip_sum_{nm}") for p, g, nm in zip(pairs, gots, names)]
    return [q.reshape(2 * q.shape[1], q.shape[2]) for q in join_halves(mine)]


def gather_small(v):
    def body(v_ref, o_ref, ssem, rsem, lsem):
        x, y, c = _place()
        loc = pltpu.make_async_copy(v_ref, o_ref.at[4 * x + 2 * y + c], lsem)
        loc.start()
        cps = []
        for k in range(1, 8):
            fx, fy, fc = (k >> 2) & 1, (k >> 1) & 1, k & 1
            px = 1 - x if fx else x
            py = 1 - y if fy else y
            pc = 1 - c if fc else c
            cps.append((pltpu.make_async_remote_copy(
                src_ref=v_ref, dst_ref=o_ref.at[4 * x + 2 * y + c], send_sem=ssem.at[k - 1], recv_sem=rsem.at[k - 1],
                device_id=(px, py, pc), device_id_type=MESH), 4 * px + 2 * py + pc))
        for cp, _ in cps:
            cp.start()
        for k, (cp, peer) in enumerate(cps):
            pltpu.make_async_remote_copy(
                src_ref=v_ref, dst_ref=o_ref.at[peer], send_sem=ssem.at[k], recv_sem=rsem.at[k],
                device_id=(x, y, c), device_id_type=MESH).wait_recv()
        for cp, _ in cps:
            cp.wait_send()
        loc.wait()

    return pl.pallas_call(
        body, name="gather_small", interpret=False,
        out_shape=jax.ShapeDtypeStruct((8, SV_ROWS, 1024), F32),
        in_specs=[ANY], out_specs=ANY,
        scratch_shapes=[pltpu.SemaphoreType.DMA((7,)), pltpu.SemaphoreType.DMA((7,)), pltpu.SemaphoreType.DMA],
    )(v)


def sum_slots(a):
    def fn(i, t):
        acc = t[0]
        for k in range(1, 8):
            acc = acc + t[k]
        return acc

    return rowwise(fn, [whole(a)], [((SV_ROWS, 1024), F32, (SV_ROWS, 1024), lambda i: (0, 0), "w")], steps=1,
                   name="sum_slots")[0]


def _head_rms(x, nw):
    xs, rs = [], []
    for h in range(DN_H):
        xh = x[:, h * DN_D:(h + 1) * DN_D]
        r = lax.rsqrt(jnp.mean(xh * xh, axis=1, keepdims=True) + EPS)
        xs.append(xh * r)
        rs.append(r)
    return xs, rs


def bg_fwd(p, alog, dtb):
    rows = p.shape[0]
    tr = _pick(rows, (384, 128))

    def fn(i, x, al, dt):
        lane = lax.broadcasted_iota(jnp.int32, x.shape, 1)
        row = i + lax.broadcasted_iota(jnp.int32, x.shape, 0)
        g = -jnp.exp(al) * _softplus(x + dt)
        out = jnp.where(lane < 4, _sigmoid(x), jnp.where(lane < 8, g, 0.0))
        return jnp.where(row >= PAD, out, 0.0)

    return rowwise(fn, [cols(p, tr, 128, BG0 // 128), whole(alog), whole(dtb)], [out2d(rows, 128, F32, tr)],
                   steps=rows // tr, name="bg_fwd")[0]


def bg_bwd(p, alog, dtb, dbg):
    rows = p.shape[0]
    tr = _pick(rows, (384, 128))

    def fn(i, x, al, dt, g_in):
        lane = lax.broadcasted_iota(jnp.int32, x.shape, 1)
        row = i + lax.broadcasted_iota(jnp.int32, x.shape, 0)
        live = row >= PAD
        is_b = jnp.logical_and(live, lane < 4)
        is_g = jnp.logical_and(live, jnp.logical_and(lane >= 4, lane < 8))
        beta = _sigmoid(x)
        ea = jnp.exp(al)
        g = -ea * _softplus(x + dt)
        dalpha = jnp.where(is_g, g_in * (-ea) * _sigmoid(x + dt), 0.0)
        dx = jnp.where(is_b, g_in * beta * (1.0 - beta), dalpha)
        dal = jnp.sum(jnp.where(is_g, g_in * g, 0.0), axis=0, keepdims=True)
        return jnp.concatenate([dx, jnp.zeros(x.shape, F32)], axis=1), dal, jnp.sum(dalpha, axis=0, keepdims=True)

    return rowwise(fn, [cols(p, tr, 128, BG0 // 128), whole(alog), whole(dtb), cols(dbg, tr)],
                   [out2d(rows, 256, BF16, tr)], steps=rows // tr, name="bg_bwd",
                   accs=[((1, 128), F32), ((1, 128), F32)])


def dn_qkv_post(j, y):
    xs = _silu(y)
    sc = jnp.where(j == 0, DN_D ** -0.5, 1.0)
    outs = []
    for h in range(DN_H):
        xh = xs[:, h * DN_D:(h + 1) * DN_D]
        r = lax.rsqrt(jnp.sum(xh * xh, axis=1, keepdims=True) + EPS)
        outs.append(jnp.where(j < 2, xh * r * sc, xh))
    return jnp.concatenate(outs, axis=1), y


def dn_qkv_bwd(cq, dq, dk, dv):
    rows = cq.shape[0]
    tr = _pick(rows, (384, 128))

    def fn(i, c0, c1, c2, g0, g1, g2):
        pieces = []
        for kind, (cv, g) in enumerate(((c0, g0), (c1, g1), (c2, g2))):
            xs = _silu(cv)
            if kind < 2:
                sc = DN_D ** -0.5 if kind == 0 else 1.0
                ds = []
                for h in range(DN_H):
                    sl = slice(h * DN_D, (h + 1) * DN_D)
                    xh, gh = xs[:, sl], g[:, sl]
                    r = lax.rsqrt(jnp.sum(xh * xh, axis=1, keepdims=True) + EPS)
                    xn = xh * r
                    ds.append(sc * r * (gh - xn * jnp.sum(gh * xn, axis=1, keepdims=True)))
                dxs = jnp.concatenate(ds, axis=1)
            else:
                dxs = g
            pieces.append(dxs * _dsilu(cv))
        return jnp.concatenate(pieces, axis=1)

    ins = [cols(cq, tr, DN_DIM, k) for k in range(3)] + [cols(t, tr) for t in (dq, dk, dv)]
    return rowwise(fn, ins, [out2d(rows, 3 * DN_DIM, F32, tr)], steps=rows // tr, name="dn_qkv_bwd")[0]


def dn_out_fwd(o, p, nw):
    rows = o.shape[0]
    tr = _pick(rows, (384, 128))

    def fn(i, ov, z, w):
        xs, _ = _head_rms(ov, w)
        return jnp.concatenate(xs, axis=1) * jnp.concatenate([w] * DN_H, axis=1) * _silu(z)

    return rowwise(fn, [cols(o, tr), cols(p, tr, DN_DIM, 6), whole(nw)], [out2d(rows, DN_DIM, BF16, tr)],
                   steps=rows // tr, name="dn_out_fwd")[0]


def dn_out_bwd(o, p, nw, dymix):
    rows = o.shape[0]
    tr = _pick(rows, (384, 128))

    def fn(i, ov, z, w, dy):
        xs, rs = _head_rms(ov, w)
        sz = _silu(z)
        dn = dy * sz
        dos, dw = [], jnp.zeros((1, DN_D), F32)
        for h in range(DN_H):
            sl = slice(h * DN_D, (h + 1) * DN_D)
            gw = dn[:, sl] * w
            dos.append(rs[h] * (gw - xs[h] * jnp.mean(gw * xs[h], axis=1, keepdims=True)))
            dw = dw + jnp.sum(dn[:, sl] * xs[h], axis=0, keepdims=True)
        n = jnp.concatenate(xs, axis=1) * jnp.concatenate([w] * DN_H, axis=1)
        return jnp.concatenate(dos, axis=1), dy * n * _dsilu(z), dw

    return rowwise(fn, [cols(o, tr), cols(p, tr, DN_DIM, 6), whole(nw), cols(dymix, tr, DN_DIM, 1)],
                   [out2d(rows, DN_DIM, F32, tr), out2d(rows, DN_DIM, BF16, tr)], steps=rows // tr,
                   name="dn_out_bwd", accs=[((1, DN_D), F32)])


def conv_a_pre_bwd(dymix, cv, p):
    rows = cv.shape[0]
    tr = _pick(rows, (384, 128))

    def fn(i, dy, c, go):
        return dy * c, dy * go

    return rowwise(fn, [cols(dymix, tr, D_CONV, 0), cols(cv, tr), cols(p, tr, D_CONV, 1)],
                   [out2d(rows, D_CONV, BF16, tr), out2d(rows, D_CONV, F32, tr)], steps=rows // tr,
                   name="conv_a_pre_bwd")


def _act_bwd_epi(row0, da, gc, val):
    c, val = gc.astype(F32), val.astype(F32)
    return da * _silu(c), da * val * _dsilu(c)


def _rows8(w):
    return jnp.pad(w.astype(F32), ((0, 8 - w.shape[0]), (0, 0)))


def _lanes(v, at):
    return jnp.pad(v.astype(F32), (at, 128 - at - v.shape[0]))[None]


def add_norm(a, w, h, next_nw, *, name):
    if next_nw is None:
        return mm(a, w, add=h, name=name), None
    return mm(a, w, name=name, epi=_add_norm_epi, epi_ins=[(h, lambda j: 0)], epi_consts=[next_nw],
              epi_outs=[F32, BF16])


def _add_norm_epi(row0, t, h, nw):
    x = t + h
    return x, x * lax.rsqrt(jnp.mean(x * x, axis=1, keepdims=True) + EPS) * nw


def ffn_fwd(h, hn, w_up, cw8, w_down, tag, next_nw):
    rows = h.shape[0]
    tr = _pick(rows, (384, 128))
    u = mm(hn, w_up, out_dtype=BF16, b_chip=True, name=f"ffn{tag}_up")
    a, gc = conv_fwd([(u, 0)], cw8, 3, rows=rows, c=D_FF, tc=1408, tr=tr, name=f"ffn{tag}_conv",
                     post=lambda j, y, val: (_silu(y) * val.astype(F32), y), extras=[(u, 2)], outs=[BF16, BF16])
    out, hn_next = add_norm(a, w_down, h, next_nw, name=f"ffn{tag}_down")
    return out, hn_next, (hn, u, a, gc)


def ffn_bwd(h, nw, w_up, cw8, w_down, saved, dh, tag):
    hn, u, a, gc = saved
    rows = h.shape[0]
    tr = _pick(rows, (384, 128))
    du_half, dgc = mm(dh, w_down, tb=True, name=f"ffn{tag}_down_dx", epi=_act_bwd_epi,
                      epi_ins=[(gc, lambda j: j), (u, lambda j: 2 + j)],
                      epi_outs=[(BF16, 2 * D_FF, lambda j: 2 + j), F32])
    d_w_down = mm(a, dh, ta=True, out_dtype=BF16, name=f"ffn{tag}_down_dw")
    du, d_cw = conv_bwd([(u, 0)], cw8, 3, dgc, rows=rows, c=D_FF, tc=1408, tr=tr, name=f"ffn{tag}_conv_bwd",
                        post=lambda dx: dx, outs=[BF16], into=(du_half, 0))
    dh_new, d_nw = dx_rms_bwd(du, w_up, h, nw, dh, name=f"ffn{tag}_up_dx", b_chip=True)
    d_w_up = mm(hn, du, ta=True, out_dtype=BF16, out_chip=True, name=f"ffn{tag}_up_dw")
    return dh_new, d_nw, d_w_up, d_cw, d_w_down


def mixer_fwd(h, nw, w_in, ca8, dc8, alog, dtb, dnw, w_out, tie=None, next_nw=None):
    rows = h.shape[0]
    tr = _pick(rows, (384, 128))
    hn = rms_fwd(h, nw, name="mix_norm")
    p = mm(hn, w_in, name="mix_in")
    y_a, cv = conv_fwd([(p, 0), (p, 2)], ca8, 3, rows=rows, c=D_CONV, tc=D_CONV, tr=tr, name="conv_a",
                       pre=lambda gi, ah: gi * ah, post=lambda j, y, go: (go * y, y), extras=[(p, 1)],
                       outs=[BF16, F32])
    qkv_n, cq = conv_fwd([(p, 3)], dc8, 4, rows=rows, c=3 * DN_DIM, tc=DN_DIM, tr=tr, name="dn_conv",
                         post=dn_qkv_post, outs=[F32, F32], strip=tr)
    bgcol = bg_fwd(p, alog, dtb)
    if tie is not None:
        bgcol = tie(bgcol)
    bgrow = bgcol[:, :8].reshape(rows // CH, CH, 8).transpose(0, 2, 1)
    o, s_all, ti_all = dn_fwd(qkv_n, bgcol, bgrow)
    y_b = dn_out_fwd(o, p, dnw)
    ymix = jnp.concatenate([y_a, y_b], axis=1)
    out, hn_next = add_norm(ymix, w_out, h, next_nw, name="mix_out")
    return out, hn_next, (hn, p, cv, qkv_n, cq, bgcol, bgrow, o, s_all, ti_all, ymix)


def mixer_bwd(h, nw, w_in, ca8, dc8, alog, dtb, dnw, w_out, saved, dh):
    hn, p, cv, qkv_n, cq, bgcol, bgrow, o, s_all, ti_all, ymix = saved
    rows = h.shape[0]
    tr = _pick(rows, (384, 128))
    dymix = mm(dh, w_out, tb=True, name="mix_out_dx")
    d_w_out = mm(ymix, dh, ta=True, out_dtype=BF16, name="mix_out_dw")
    do, dz, d_dnw = dn_out_bwd(o, p, dnw, dymix)
    dq, dk, dv, dbg = dn_bwd(qkv_n, bgcol, bgrow, s_all, ti_all, do)
    dbg_p, d_alog, d_dtb = bg_bwd(p, alog, dtb, dbg)
    dcq = dn_qkv_bwd(cq, dq, dk, dv)
    dqkv, d_dc = conv_bwd([(p, 3)], dc8, 4, dcq, rows=rows, c=3 * DN_DIM, tc=DN_DIM, tr=tr, name="dn_conv_bwd",
                          post=lambda dx: dx, outs=[BF16])
    dgo, dcv = conv_a_pre_bwd(dymix, cv, p)
    dgi, dah, d_ca = conv_bwd([(p, 0), (p, 2)], ca8, 3, dcv, rows=rows, c=D_CONV, tc=D_CONV, tr=tr,
                              name="conv_a_bwd", pre=lambda gi, ah: gi * ah,
                              post=lambda dm, gi, ah: (dm * ah, dm * gi), extras=[(p, 0), (p, 2)], outs=[BF16, BF16])
    dp = jnp.concatenate([dgi, dgo, dah, dqkv, dz, dbg_p], axis=1)
    dh_new, d_nw = dx_rms_bwd(dp, w_in, h, nw, dh, name="mix_in_dx")
    d_w_in = mm(hn, dp, ta=True, out_dtype=BF16, name="mix_in_dw")
    return dh_new, d_nw, d_w_in, d_ca, d_dc, d_alog, d_dtb, d_dnw, d_w_out


def swa_layer_fwd(h, hn, wqkv, qw, kw, sinks, wo, next_nw):
    qkv = mm(hn, wqkv, name="swa_qkv")
    qh, kh, vh = qknorm_fwd(qkv, qw, kw)
    att = swa_fwd(qh, kh, vh, sinks)
    out, hn_next = add_norm(att, wo, h, next_nw, name="swa_out")
    return out, hn_next, (hn, qkv, qh, kh, vh, att)


def swa_layer_bwd(h, nw, wqkv, qw, kw, sinks, wo, saved, dh):
    hn, qkv, qh, kh, vh, att = saved
    datt = mm(dh, wo, tb=True, out_dtype=BF16, name="swa_out_dx")
    d_wo = mm(att, dh, ta=True, out_dtype=BF16, name="swa_out_dw")
    dqh, dkh, dvh, dsk = swa_bwd(qh, kh, vh, sinks, datt)
    dqkv, d_qw, d_kw = qknorm_bwd(qkv, qw, kw, dqh, dkh, dvh)
    dh_new, d_nw = dx_rms_bwd(dqkv, wqkv, h, nw, dh, name="swa_qkv_dx")
    d_wqkv = mm(hn, dqkv, ta=True, out_dtype=BF16, name="swa_qkv_dw")
    d_sinks = jnp.sum(dsk[:, :, 0], axis=0)
    return dh_new, d_nw, d_wqkv, d_qw, d_kw, d_sinks, d_wo


BIG = ("mix_w_in", "mix_w_out", "swa_wq", "swa_wk", "swa_wv", "swa_wo", "ffn_w_up", "ffn_w_down")


def _flat_pad(parts, rows):
    v = jnp.concatenate([t.astype(F32).reshape(-1) for t in parts])
    return jnp.pad(v, (0, rows * 1024 - v.shape[0])).reshape(rows, 1024)


def _split_flat(flat, shapes):
    v = flat.reshape(-1)
    out, o = [], 0
    for s in shapes:
        n = 1
        for d_ in s:
            n *= d_
        out.append(v[o:o + n].reshape(s))
        o += n
    return out


def local_step(x0, target0, meta_full, anw, fnw, w_in, ca8, dc8, alog, dtb, dnw, w_out, qw, kw, sinks, fc8, late,
               begin=None, tie=None):
    begin = begin or (lambda tag, names, grads: None)
    h0 = jnp.concatenate([jnp.zeros((PAD, D), F32), meta_full, x0], axis=0)
    h1, hn1, s_mix = mixer_fwd(h0, anw[0], w_in, ca8, dc8, alog, dtb, dnw, w_out, tie, fnw[0])
    wqkv, wo, w_up, w_down = late()
    h2, hn2, s_f0 = ffn_fwd(h1, hn1, w_up[0], fc8[0], w_down[0], 0, anw[1])
    h3, hn3, s_swa = swa_layer_fwd(h2, hn2, wqkv, qw, kw, sinks, wo, fnw[1])
    h4, _, s_f1 = ffn_fwd(h3, hn3, w_up[1], fc8[1], w_down[1], 1, None)
    dh, loss_l = loss_grad(h4, target0)
    dh, d_fnw1, d_up1, d_fc1, d_down1 = ffn_bwd(h3, fnw[1], w_up[1], fc8[1], w_down[1], s_f1, dh, 1)
    begin("ffn1", ("up1", "down1"), [d_up1, d_down1.reshape(4, 704, D)])
    dh, d_anw1, d_wqkv, d_qw, d_kw, d_sinks, d_wo = swa_layer_bwd(h2, anw[1], wqkv, qw, kw, sinks, wo, s_swa, dh)
    begin("swa", ("wq", "wk", "wv", "wo"),
          [d_wqkv[:, :D].reshape(4, 256, D), d_wqkv[:, D:D + 256].reshape(4, 256, 256),
           d_wqkv[:, D + 256:].reshape(4, 256, 256), d_wo.reshape(4, 256, D)])
    dh, d_fnw0, d_up0, d_fc0, d_down0 = ffn_bwd(h1, fnw[0], w_up[0], fc8[0], w_down[0], s_f0, dh, 0)
    begin("ffn0", ("up0", "down0"), [d_up0, d_down0.reshape(4, 704, D)])
    dh, d_anw0, d_w_in, d_ca, d_dc, d_alog, d_dtb, d_dnw, d_w_out = mixer_bwd(
        h0, anw[0], w_in, ca8, dc8, alog, dtb, dnw, w_out, s_mix, dh)
    begin("mix", ("w_in", "w_out"),
          [d_w_in[:, :IN_DIM].reshape(D, 4, 898).transpose(1, 0, 2), d_w_out.reshape(4, 256, D)])
    return (dh, loss_l, d_anw0, d_anw1, d_fnw0, d_fnw1, d_w_in, d_ca, d_dc, d_alog, d_dtb, d_dnw, d_w_out, d_wqkv,
            d_qw, d_kw, d_sinks, d_wo, d_up0, d_up1, d_fc0, d_fc1, d_down0, d_down1)


def kernel(x, meta_tokens, attn_norm_w, ffn_norm_w, mix_w_in, conv_a_w, dn_conv_w, dn_a_log, dn_dt_bias, dn_norm_w, mix_w_out, swa_wq, swa_wk, swa_wv, swa_q_norm_w, swa_k_norm_w, swa_sinks, swa_wo, ffn_w_up, ffn_conv_w, ffn_w_down, loss_target, m_meta_tokens, m_attn_norm_w, m_ffn_norm_w, m_mix_w_in, m_conv_a_w, m_dn_conv_w, m_dn_a_log, m_dn_dt_bias, m_dn_norm_w, m_mix_w_out, m_swa_wq, m_swa_wk, m_swa_wv, m_swa_q_norm_w, m_swa_k_norm_w, m_swa_sinks, m_swa_wo, m_ffn_w_up, m_ffn_conv_w, m_ffn_w_down, v_meta_tokens, v_attn_norm_w, v_ffn_norm_w, v_mix_w_in, v_conv_a_w, v_dn_conv_w, v_dn_a_log, v_dn_dt_bias, v_dn_norm_w, v_mix_w_out, v_swa_wq, v_swa_wk, v_swa_wv, v_swa_q_norm_w, v_swa_k_norm_w, v_swa_sinks, v_swa_wo, v_ffn_w_up, v_ffn_conv_w, v_ffn_w_down):
    ix, iy, ic = lax.axis_index("x"), lax.axis_index("y"), lax.axis_index("c")
    chip = 2 * ix + iy
    seq = x.shape[1]
    rows = HEAD0 + seq

    small_sharded = (conv_a_w, dn_conv_w, ffn_conv_w, meta_tokens)
    up_b, down_b = ffn_w_up.astype(BF16), ffn_w_down.astype(BF16)
    own = [mix_w_in[0].astype(BF16), mix_w_out[0].astype(BF16), swa_wq[0].astype(BF16), swa_wk[0].astype(BF16),
           swa_wv[0].astype(BF16), swa_wo[0].astype(BF16), up_b[0], up_b[1], down_b[0], down_b[1]]
    fill = lambda gathered, mine: [lax.dynamic_update_slice_in_dim(g, t[None], chip, axis=0)
                                   for g, t in zip(gathered, mine)]
    first, g_small = gather_weights(own[:2], _flat_pad(small_sharded, SW_ROWS))
    g_in, g_out = fill(first, own[:2])
    w_in = jnp.pad(g_in.transpose(1, 0, 2).reshape(D, IN_DIM), ((0, 0), (0, P_W - IN_DIM)))
    w_out = g_out.reshape(D, D)
    rest = {}

    def tie(t):
        t, *mine = lax.optimization_barrier((t, *own[2:]))
        rest["w"] = fill(gather_weights_beside(mine), mine)
        return t

    def late():
        g_q, g_k, g_v, g_o, g_up0, g_up1, g_dn0, g_dn1 = rest["w"]
        wqkv = jnp.concatenate([g_q.reshape(D, D), g_k.reshape(D, 256), g_v.reshape(D, 256)], axis=1)
        return wqkv, g_o.reshape(D, D), [g_up0, g_up1], [g_dn0.reshape(D_FF, D), g_dn1.reshape(D_FF, D)]

    gs = g_small.reshape(4, -1)
    ca_full = gs[:, 0:384].reshape(4, 3, 128).transpose(1, 0, 2).reshape(3, D_CONV)
    dc_full = gs[:, 384:1920].reshape(4, 4, 384).transpose(1, 0, 2).reshape(4, 3 * DN_DIM)
    fc_full = gs[:, 1920:6144].reshape(4, 2, 3, 704).transpose(1, 2, 0, 3).reshape(2, 3, D_FF)
    meta_full = gs[:, 6144:10240].reshape(4, N_META, 256).transpose(1, 0, 2).reshape(N_META, D)
    ca8, dc8 = _rows8(ca_full), _rows8(dc_full)
    fc8 = [_rows8(fc_full[0]), _rows8(fc_full[1])]
    alog, dtb = _lanes(dn_a_log[0], 4), _lanes(dn_dt_bias[0], 4)
    dnw = dn_norm_w.astype(F32)
    qw, kw = swa_q_norm_w.astype(F32), swa_k_norm_w.astype(F32)
    sinks = swa_sinks[0].astype(F32)
    anw = [attn_norm_w[0:1], attn_norm_w[1:2]]
    fnw = [ffn_norm_w[0:1], ffn_norm_w[1:2]]

    c_idx = jnp.reshape(ic, (1,)).astype(jnp.int32)
    chip_idx = jnp.stack([chip, ic]).astype(jnp.int32)
    begun = []

    def begin(tag, names, grads):
        pairs, gots = reduce_begin(grads, names, c_idx, 2 + len(begun), tag)
        begun.append((names, pairs, gots))

    (dh, loss_l, d_anw0, d_anw1, d_fnw0, d_fnw1, d_w_in, d_ca, d_dc, d_alog, d_dtb, d_dnw, d_w_out, d_wqkv, d_qw,
     d_kw, d_sinks, d_wo, d_up0, d_up1, d_fc0, d_fc1, d_down0, d_down1) = local_step(
        x[0], loss_target[0], meta_full, anw, fnw, w_in, ca8, dc8, alog, dtb, dnw, w_out, qw, kw, sinks, fc8, late,
        begin, tie)
    grad_x = dh[HEAD0:][None]

    small_parts = [jnp.concatenate([d_anw0, d_anw1], axis=0), jnp.concatenate([d_fnw0, d_fnw1], axis=0),
                   d_alog[0, 4:8], d_dtb[0, 4:8], d_dnw, d_qw, d_kw, d_sinks,
                   d_ca[:3], d_dc[:4], jnp.stack([d_fc0[:3], d_fc1[:3]]), dh[PAD:HEAD0], loss_l[0, 0:1]]
    small_shapes = [(2, D), (2, D), (1, 4), (1, 4), (1, DN_D), (1, SWA_D), (1, SWA_D), (1, SWA_H),
                    (1, 3, D_CONV), (1, 4, 3 * DN_DIM), (2, 3, D_FF), (N_META, D), ()]
    red = sum_slots(gather_small(_flat_pad(small_parts, SV_ROWS)))
    (g_anw, g_fnw, g_alog, g_dtb, g_dnw, g_qw, g_kw, g_sinks, g_ca_f, g_dc_f, g_fc_f, g_meta_f,
     loss) = _split_flat(red, small_shapes)
    g_ca = lax.dynamic_slice_in_dim(g_ca_f, chip * 128, 128, axis=2)
    g_dc = lax.dynamic_slice_in_dim(g_dc_f, chip * 384, 384, axis=2)
    g_fc = lax.dynamic_slice_in_dim(g_fc_f, chip * 704, 704, axis=2)
    g_meta = lax.dynamic_slice_in_dim(g_meta_f, chip * 256, 256, axis=1)

    all_names = [n for names, _, _ in begun for n in names]
    red_big = dict(zip(all_names, reduce_end([p for _, ps, _ in begun for p in ps],
                                             [g for _, _, gs_ in begun for g in gs_], all_names, chip_idx)))
    g_w_in, g_w_out, g_wq, g_wk, g_wv, g_wo, g_up0, g_up1, g_dn0, g_dn1 = [
        red_big[n] for n in ("w_in", "w_out", "wq", "wk", "wv", "wo", "up0", "up1", "down0", "down1")]

    grads = dict(meta_tokens=g_meta, attn_norm_w=g_anw, ffn_norm_w=g_fnw, mix_w_in=g_w_in, conv_a_w=g_ca,
                 dn_conv_w=g_dc, dn_a_log=g_alog, dn_dt_bias=g_dtb, dn_norm_w=g_dnw, mix_w_out=g_w_out,
                 swa_wq=g_wq, swa_wk=g_wk, swa_wv=g_wv, swa_q_norm_w=g_qw, swa_k_norm_w=g_kw, swa_sinks=g_sinks,
                 swa_wo=g_wo, ffn_w_up=[g_up0, g_up1], ffn_conv_w=g_fc, ffn_w_down=[g_dn0, g_dn1])
    weights = dict(meta_tokens=meta_tokens, attn_norm_w=attn_norm_w, ffn_norm_w=ffn_norm_w, mix_w_in=mix_w_in,
                   conv_a_w=conv_a_w, dn_conv_w=dn_conv_w, dn_a_log=dn_a_log, dn_dt_bias=dn_dt_bias,
                   dn_norm_w=dn_norm_w, mix_w_out=mix_w_out, swa_wq=swa_wq, swa_wk=swa_wk, swa_wv=swa_wv,
                   swa_q_norm_w=swa_q_norm_w, swa_k_norm_w=swa_k_norm_w, swa_sinks=swa_sinks, swa_wo=swa_wo,
                   ffn_w_up=ffn_w_up, ffn_conv_w=ffn_conv_w, ffn_w_down=ffn_w_down)
    m_in = dict(meta_tokens=m_meta_tokens, attn_norm_w=m_attn_norm_w, ffn_norm_w=m_ffn_norm_w, mix_w_in=m_mix_w_in,
                conv_a_w=m_conv_a_w, dn_conv_w=m_dn_conv_w, dn_a_log=m_dn_a_log, dn_dt_bias=m_dn_dt_bias,
                dn_norm_w=m_dn_norm_w, mix_w_out=m_mix_w_out, swa_wq=m_swa_wq, swa_wk=m_swa_wk, swa_wv=m_swa_wv,
                swa_q_norm_w=m_swa_q_norm_w, swa_k_norm_w=m_swa_k_norm_w, swa_sinks=m_swa_sinks, swa_wo=m_swa_wo,
                ffn_w_up=m_ffn_w_up, ffn_conv_w=m_ffn_conv_w, ffn_w_down=m_ffn_w_down)
    v_in = dict(meta_tokens=v_meta_tokens, attn_norm_w=v_attn_norm_w, ffn_norm_w=v_ffn_norm_w, mix_w_in=v_mix_w_in,
                conv_a_w=v_conv_a_w, dn_conv_w=v_dn_conv_w, dn_a_log=v_dn_a_log, dn_dt_bias=v_dn_dt_bias,
                dn_norm_w=v_dn_norm_w, mix_w_out=v_mix_w_out, swa_wq=v_swa_wq, swa_wk=v_swa_wk, swa_wv=v_swa_wv,
                swa_q_norm_w=v_swa_q_norm_w, swa_k_norm_w=v_swa_k_norm_w, swa_sinks=v_swa_sinks, swa_wo=v_swa_wo,
                ffn_w_up=v_ffn_w_up, ffn_conv_w=v_ffn_conv_w, ffn_w_down=v_ffn_w_down)
    names = list(weights)
    small = [n for n in names if n not in BIG]
    delta, new_m, new_v = {}, {}, {}
    for n in BIG:
        delta[n], new_m[n], new_v[n], grads[n] = adamw(weights[n], grads[n], m_in[n], v_in[n], name=f"adamw_{n}")
    grads = {n: grads[n].reshape(weights[n].shape) for n in names}
    shapes = [weights[n].shape for n in small]
    packed = [_flat_pad([t[n] for n in small], SW_ROWS) for t in (weights, grads, m_in, v_in)]
    for store, flat in zip((delta, new_m, new_v), adamw(*packed, name="adamw_small")):
        for n, t in zip(small, _split_flat(flat, shapes)):
            store[n] = t
    return (loss, grad_x, *[grads[n] for n in names], *[delta[n] for n in names],
            *[new_m[n] for n in names], *[new_v[n] for n in names])
```

```python
import functools

import jax
import jax.numpy as jnp
from jax import lax
from jax.experimental import pallas as pl
from jax.experimental.pallas import tpu as pltpu
from jax.experimental.pallas import tpu_sc as plsc

F32 = jnp.float32
BF16 = jnp.bfloat16
HI = lax.Precision.HIGHEST
MESH = pl.DeviceIdType.MESH

D = 1024
N_META = 16
PAD = 112
HEAD0 = PAD + N_META
D_CONV = 512
DN_H = 4
DN_D = 128
DN_DIM = 512
CH = 64
IN_DIM = 3592
P_W = 3840
BG0 = 3584
SWA_H = 16
SWA_KV = 4
SWA_D = 64
BLK = 128
D_FF = 2816
EPS = 1e-6
LR, B1, B2, AEPS, WD, STEP = 0.001, 0.9, 0.999, 1e-08, 0.01, 10
VMEM_LIMIT = 48 * 1024 * 1024
MM_VMEM_BUDGET = 34 * 1024 * 1024
R_BIG = 6144
R_HALF = R_BIG // 2
SV_ROWS = 48
SW_ROWS = 16


def _pick(n, cands):
    for c in cands:
        if n % c == 0:
            return c
    return n


def _params(sem=None):
    return pltpu.CompilerParams(dimension_semantics=sem, vmem_limit_bytes=VMEM_LIMIT)


def _dot(a, b, ca=1, cb=0, prec=None):
    return lax.dot_general(a, b, (((ca,), (cb,)), ((), ())), precision=prec,
                           preferred_element_type=F32)


def _sigmoid(x):
    return 1.0 / (1.0 + jnp.exp(-x))


def _silu(x):
    return x * _sigmoid(x)


def _dsilu(x):
    s = _sigmoid(x)
    return s * (1.0 + x * (1.0 - s))


def _softplus(x):
    return jnp.maximum(x, 0.0) + jnp.log(1.0 + jnp.exp(-jnp.abs(x)))


def mm(a, b, *, name, ta=False, tb=False, out_dtype=F32, add=None, tm=None, tn=None, tk=None,
       b_chip=False, out_chip=False, epi=None, epi_ins=(), epi_consts=(), epi_outs=(), epi_accs=()):
    if epi is not None:
        return _mm_epi(a, b, name=name, tb=tb, tn=tn, b_chip=b_chip, epi=epi, epi_ins=epi_ins,
                       epi_consts=epi_consts, epi_outs=epi_outs, epi_accs=epi_accs)
    m, k = (a.shape[1], a.shape[0]) if ta else a.shape
    if b_chip:
        n = b.shape[1] if tb else 4 * b.shape[2]
        if tb:
            tk = b.shape[2]
        else:
            tn = b.shape[2]
    else:
        n = b.shape[0] if tb else b.shape[1]
    if out_chip:
        tn = n // 4
    tn = tn or _pick(n, (1408, 1024, 768, 512, 256, 128))
    tk = tk or (_pick(k, (1408, 704, 384, 128)) if ta else _pick(k, (1024, 1408, 768, 512, 128)))
    nk = k // tk
    if tm is None:
        isz = lambda t: jnp.dtype(t.dtype).itemsize
        osz = jnp.dtype(out_dtype).itemsize
        for tm in ((1408, 1024, 512, 384, 256, 128) if ta else (1408, 704, 512, 384, 256, 128)):
            need = 2 * (tm * tk * isz(a) + tk * tn * isz(b) + tm * tn * osz + (tm * tn * 4 if add is not None else 0))
            need += tm * tn * 4 if nk > 1 else 0
            if m % tm == 0 and need <= MM_VMEM_BUDGET:
                break
        else:
            tm = m
    dims = (((0 if ta else 1,), (1 if tb else 0,)), ((), ()))

    def body(*refs):
        if add is None:
            a_ref, b_ref, o_ref, acc_ref = refs
            add_ref = None
        else:
            a_ref, b_ref, add_ref, o_ref, acc_ref = refs
        part = lax.dot_general(a_ref[...].astype(BF16), b_ref[...].astype(BF16), dims,
                               preferred_element_type=F32)

        def finish(total):
            if add_ref is not None:
                total = total + add_ref[...]
            o_ref[...] = total.astype(out_dtype)

        if nk == 1:
            finish(part)
        else:
            kk = pl.program_id(2)

            @pl.when(kk == 0)
            def _():
                acc_ref[...] = part

            @pl.when(kk > 0)
            def _():
                acc_ref[...] += part

            @pl.when(kk == nk - 1)
            def _():
                finish(acc_ref[...])

    a_spec = pl.BlockSpec((tk, tm), lambda i, j, kk: (kk, i)) if ta else pl.BlockSpec((tm, tk), lambda i, j, kk: (i, kk))
    if b_chip and tb:
        b_spec = pl.BlockSpec((None, tn, tk), lambda i, j, kk: (kk, j, 0))
    elif b_chip:
        b_spec = pl.BlockSpec((None, tk, tn), lambda i, j, kk: (j, kk, 0))
    elif tb:
        b_spec = pl.BlockSpec((tn, tk), lambda i, j, kk: (j, kk))
    else:
        b_spec = pl.BlockSpec((tk, tn), lambda i, j, kk: (kk, j))
    o_spec = pl.BlockSpec((tm, tn), lambda i, j, kk: (i, j))
    in_specs = [a_spec, b_spec] + ([o_spec] if add is not None else [])
    args = [a, b] + ([add] if add is not None else [])
    out_spec = pl.BlockSpec((None, tm, tn), lambda i, j, kk: (j, i, 0)) if out_chip else o_spec
    return pl.pallas_call(
        body, name=name, interpret=False,
        out_shape=jax.ShapeDtypeStruct((4, m, tn) if out_chip else (m, n), out_dtype),
        grid=(m // tm, n // tn, nk), in_specs=in_specs, out_specs=out_spec,
        scratch_shapes=[pltpu.VMEM((tm, tn) if nk > 1 else (8, 128), F32)],
        compiler_params=_params(("parallel", "parallel", "arbitrary")),
    )(*args)


def _mm_epi(a, b, *, name, tb, tn, b_chip, epi, epi_ins, epi_consts, epi_outs, epi_accs):
    m, k = a.shape
    if b_chip:
        n = b.shape[1] if tb else 4 * b.shape[2]
        tk = b.shape[2] if tb else None
        tn = tn if tb else b.shape[2]
    else:
        n = b.shape[0] if tb else b.shape[1]
        tk = None
    tn = tn or _pick(n, (1408, 1024, 768, 512, 256, 128))
    tk = tk or _pick(k, (1024, 1408, 768, 512, 128))
    nk, nj = k // tk, n // tn
    isz = lambda t: jnp.dtype(t.dtype if hasattr(t, "dtype") else t).itemsize
    outs3 = [t if isinstance(t, tuple) else (t, n, lambda j: j) for t in epi_outs]
    side = sum(isz(t) for t, _ in epi_ins) + sum(isz(dt) for dt, _, _ in outs3)
    for tm in (1408, 704, 512, 384, 256, 128):
        need = 2 * (tm * tk * isz(a) + tk * tn * isz(b) + tm * tn * side) + (tm * tn * 4 if nk > 1 else 0)
        if m % tm == 0 and need <= MM_VMEM_BUDGET:
            break
    else:
        tm = m
    dims = (((1,), (1 if tb else 0,)), ((), ()))
    n_in, n_c, n_out, n_acc = len(epi_ins), len(epi_consts), len(epi_outs), len(epi_accs)

    def body(*refs):
        a_ref, b_ref = refs[:2]
        in_refs = refs[2:2 + n_in + n_c]
        out_refs = refs[2 + n_in + n_c:2 + n_in + n_c + n_out]
        acc_out = refs[2 + n_in + n_c + n_out:2 + n_in + n_c + n_out + n_acc]
        acc_ref = refs[-1]
        i, j, kk = pl.program_id(0), pl.program_id(1), pl.program_id(2)
        part = lax.dot_general(a_ref[...].astype(BF16), b_ref[...].astype(BF16), dims,
                               preferred_element_type=F32)

        def finish(total):
            res = epi(i * tm, total, *[r[...] for r in in_refs])
            if not isinstance(res, (tuple, list)):
                res = (res,)
            for r, v in zip(out_refs, res[:n_out]):
                r[...] = v.astype(r.dtype)
            if n_acc:
                @pl.when(jnp.logical_and(i == 0, j == 0))
                def _():
                    for r in acc_out:
                        r[...] = jnp.zeros(r.shape, r.dtype)

                for r, v in zip(acc_out, res[n_out:]):
                    r[...] += jnp.broadcast_to(v, r.shape).astype(r.dtype)

        if nk == 1:
            finish(part)
        else:
            @pl.when(kk == 0)
            def _():
                acc_ref[...] = part

            @pl.when(kk > 0)
            def _():
                acc_ref[...] += part

            @pl.when(kk == nk - 1)
            def _():
                finish(acc_ref[...])

    a_spec = pl.BlockSpec((tm, tk), lambda i, j, kk: (i, kk))
    if b_chip and tb:
        b_spec = pl.BlockSpec((None, tn, tk), lambda i, j, kk: (kk, j, 0))
    elif b_chip:
        b_spec = pl.BlockSpec((None, tk, tn), lambda i, j, kk: (j, kk, 0))
    elif tb:
        b_spec = pl.BlockSpec((tn, tk), lambda i, j, kk: (j, kk))
    else:
        b_spec = pl.BlockSpec((tk, tn), lambda i, j, kk: (kk, j))
    in_specs = [a_spec, b_spec]
    in_specs += [pl.BlockSpec((tm, tn), lambda i, j, kk, col=col: (i, col(j))) for _, col in epi_ins]
    in_specs += [pl.BlockSpec(t.shape, lambda i, j, kk, nd=t.ndim: (0,) * nd) for t in epi_consts]
    out_specs = [pl.BlockSpec((tm, tn), lambda i, j, kk, col=col: (i, col(j))) for _, _, col in outs3]
    out_specs += [pl.BlockSpec(s, lambda i, j, kk, nd=len(s): (0,) * nd) for s, _ in epi_accs]
    out_shape = [jax.ShapeDtypeStruct((m, width), dt) for dt, width, _ in outs3]
    out_shape += [jax.ShapeDtypeStruct(s, dt) for s, dt in epi_accs]
    sem = ("arbitrary", "arbitrary", "arbitrary") if n_acc else ("parallel", "parallel", "arbitrary")
    return pl.pallas_call(
        body, name=name, interpret=False, out_shape=out_shape,
        grid=(m // tm, nj, nk), in_specs=in_specs, out_specs=out_specs,
        scratch_shapes=[pltpu.VMEM((tm, tn) if nk > 1 else (8, 128), F32)],
        compiler_params=_params(sem),
    )(a, b, *[t for t, _ in epi_ins], *epi_consts)


def cols(arr, tr, width=None, cb=0):
    width = width or arr.shape[1]
    return (arr, (tr, width), lambda i: (i, cb), "r2")


def heads(arr, tr):
    return (arr, (arr.shape[0], tr, arr.shape[2]), lambda i: (0, i, 0), "r3")


def whole(arr):
    nd = arr.ndim
    return (arr, arr.shape, lambda i: (0,) * nd, "w")


STRIP = 16


def _rows_of(ref, kind, r0, n):
    if kind == "r2":
        return ref[pl.ds(r0, n), :]
    if kind == "r3":
        return ref[:, pl.ds(r0, n), :]
    return ref[...]


def _set_rows(ref, kind, r0, n, v):
    if kind == "r2":
        ref[pl.ds(r0, n), :] = v.astype(ref.dtype)
    elif kind == "r3":
        ref[:, pl.ds(r0, n), :] = v.astype(ref.dtype)
    else:
        ref[...] = v.astype(ref.dtype)


def rowwise(fn, ins, outs, *, steps, name, accs=(), strip=None):
    n_in, n_out, n_acc = len(ins), len(outs), len(accs)
    kin = [t[3] for t in ins]
    kout = [t[4] for t in outs]
    tr = next((t[1][0] if t[3] == "r2" else t[1][1] for t in ins if t[3] != "w"), 0)

    def body(*refs):
        i = pl.program_id(0)
        in_refs, out_refs, acc_refs = refs[:n_in], refs[n_in:n_in + n_out], refs[n_in + n_out:]
        if n_acc:
            @pl.when(i == 0)
            def _():
                for r in acc_refs:
                    r[...] = jnp.zeros(r.shape, r.dtype)

        def run(r0, n):
            res = fn(i * tr + r0, *[_rows_of(r, k, r0, n) for r, k in zip(in_refs, kin)])
            if not isinstance(res, (tuple, list)):
                res = (res,)
            for r, k, v in zip(out_refs, kout, res[:n_out]):
                _set_rows(r, k, r0, n, v)
            for r, v in zip(acc_refs, res[n_out:]):
                r[...] += jnp.broadcast_to(v, r.shape).astype(r.dtype)

        if strip is None or tr <= strip:
            run(0, tr)
        else:
            def step(s, carry):
                run(pl.multiple_of(s * strip, strip), strip)
                return carry
            lax.fori_loop(0, tr // strip, step, 0)

    def zmap(nd):
        return lambda i: (0,) * nd

    in_specs = [pl.BlockSpec(t[1], t[2]) for t in ins]
    out_specs = [pl.BlockSpec(t[2], t[3]) for t in outs]
    out_specs += [pl.BlockSpec(s, zmap(len(s))) for s, _ in accs]
    out_shape = [jax.ShapeDtypeStruct(t[0], t[1]) for t in outs]
    out_shape += [jax.ShapeDtypeStruct(s, d) for s, d in accs]
    res = pl.pallas_call(
        body, name=name, interpret=False, out_shape=out_shape, grid=(steps,),
        in_specs=in_specs, out_specs=out_specs,
        compiler_params=_params(("arbitrary",)),
    )(*[t[0] for t in ins])
    return res


def out2d(rows, width, dtype, tr):
    return ((rows, width), dtype, (tr, width), lambda i: (i, 0), "r2")


def conv_fwd(xs, w8, kw, *, rows, c, tc, tr, name, post, extras=(), outs=(), pre=None, strip=STRIP):
    nx, ne, no = len(xs), len(extras), len(outs)
    nr, nc = rows // tr, c // tc
    r8 = tr // 8
    st = strip

    def body(*refs):
        x_refs = refs[:2 * nx]
        w_ref = refs[2 * nx]
        e_refs = refs[2 * nx + 1:2 * nx + 1 + ne]
        o_refs = refs[2 * nx + 1 + ne:2 * nx + 1 + ne + no]
        scr = refs[-1]
        j, i = pl.program_id(0), pl.program_id(1)
        halo = [x_refs[2 * q + 1][...].astype(F32) for q in range(nx)]
        scr[0:8, :] = jnp.where(i > 0, pre(*halo) if pre else halo[0], 0.0)

        def fill(s, carry):
            r0 = pl.multiple_of(s * st, st)
            cur = [x_refs[2 * q][pl.ds(r0, st), :].astype(F32) for q in range(nx)]
            scr[pl.ds(8 + r0, st), :] = pre(*cur) if pre else cur[0]
            return carry

        def comp(s, carry):
            r0 = pl.multiple_of(s * st, st)
            win = scr[pl.ds(r0, st + 8), :]
            y = jnp.zeros((st, tc), F32)
            for q in range(kw):
                sh = kw - 1 - q
                y = y + w_ref[q:q + 1, :] * win[8 - sh:8 - sh + st]
            res = post(j, y, *[e[pl.ds(r0, st), :] for e in e_refs])
            if not isinstance(res, (tuple, list)):
                res = (res,)
            for r, v in zip(o_refs, res):
                r[pl.ds(r0, st), :] = v.astype(r.dtype)
            return carry

        lax.fori_loop(0, tr // st, fill, 0)
        lax.fori_loop(0, tr // st, comp, 0)

    in_specs, args = [], []
    for arr, cb0 in xs:
        in_specs.append(pl.BlockSpec((tr, tc), lambda j, i, cb0=cb0: (i, cb0 + j)))
        in_specs.append(pl.BlockSpec((8, tc), lambda j, i, cb0=cb0: (jnp.maximum(i * r8 - 1, 0), cb0 + j)))
        args += [arr, arr]
    in_specs.append(pl.BlockSpec((8, tc), lambda j, i: (0, j)))
    args.append(w8)
    for arr, cb0 in extras:
        in_specs.append(pl.BlockSpec((tr, tc), lambda j, i, cb0=cb0: (i, cb0 + j)))
        args.append(arr)
    return pl.pallas_call(
        body, name=name, interpret=False,
        out_shape=[jax.ShapeDtypeStruct((rows, c), dt) for dt in outs],
        grid=(nc, nr), in_specs=in_specs,
        out_specs=[pl.BlockSpec((tr, tc), lambda j, i: (i, j)) for _ in outs],
        scratch_shapes=[pltpu.VMEM((tr + 8, tc), F32)],
        compiler_params=_params(("parallel", "arbitrary")),
    )(*args)


def conv_bwd(xs, w8, kw, dy, *, rows, c, tc, tr, name, post, extras=(), outs=(), pre=None, into=None):
    nx, ne, no = len(xs), len(extras), len(outs)
    nr, nc = rows // tr, c // tc
    r8 = tr // 8

    def body(*refs):
        x_refs = refs[:2 * nx]
        w_ref, dy_ref, dyn_ref = refs[2 * nx:2 * nx + 3]
        e_refs = refs[2 * nx + 3:2 * nx + 3 + ne]
        first_out = 2 * nx + 3 + ne + (1 if into is not None else 0)
        o_refs = refs[first_out:first_out + no]
        dw_ref = refs[first_out + no]
        xscr, gscr = refs[-2], refs[-1]
        i = pl.program_id(1)
        halo = [x_refs[2 * q + 1][...].astype(F32) for q in range(nx)]
        xscr[0:8, :] = jnp.where(i > 0, pre(*halo) if pre else halo[0], 0.0)
        gscr[tr:tr + 8, :] = jnp.where(i < nr - 1, dyn_ref[...].astype(F32), 0.0)

        def fill(s, carry):
            r0 = pl.multiple_of(s * STRIP, STRIP)
            cur = [x_refs[2 * q][pl.ds(r0, STRIP), :].astype(F32) for q in range(nx)]
            xscr[pl.ds(8 + r0, STRIP), :] = pre(*cur) if pre else cur[0]
            gscr[pl.ds(r0, STRIP), :] = dy_ref[pl.ds(r0, STRIP), :].astype(F32)
            return carry

        def comp(s, dws):
            r0 = pl.multiple_of(s * STRIP, STRIP)
            gwin = gscr[pl.ds(r0, STRIP + 8), :]
            xwin = xscr[pl.ds(r0, STRIP + 8), :]
            g = gwin[0:STRIP]
            dx = jnp.zeros((STRIP, tc), F32)
            new = []
            for q in range(kw):
                sh = kw - 1 - q
                dx = dx + w_ref[q:q + 1, :] * gwin[sh:sh + STRIP]
                part = g * xwin[8 - sh:8 - sh + STRIP]
                new.append(dws[q] + part[0:8] + part[8:16])
            res = post(dx, *[e[pl.ds(r0, STRIP), :] for e in e_refs])
            if not isinstance(res, (tuple, list)):
                res = (res,)
            for r, v in zip(o_refs, res):
                r[pl.ds(r0, STRIP), :] = v.astype(r.dtype)
            return tuple(new)

        lax.fori_loop(0, tr // STRIP, fill, 0)
        dws = lax.fori_loop(0, tr // STRIP, comp, tuple(jnp.zeros((8, tc), F32) for _ in range(kw)))

        @pl.when(i == 0)
        def _():
            dw_ref[...] = jnp.zeros((8, tc), F32)

        dw_ref[...] += jnp.concatenate([jnp.sum(t, axis=0, keepdims=True) for t in dws]
                                       + [jnp.zeros((8 - kw, tc), F32)], axis=0)

    in_specs, args = [], []
    for arr, cb0 in xs:
        in_specs.append(pl.BlockSpec((tr, tc), lambda j, i, cb0=cb0: (i, cb0 + j)))
        in_specs.append(pl.BlockSpec((8, tc), lambda j, i, cb0=cb0: (jnp.maximum(i * r8 - 1, 0), cb0 + j)))
        args += [arr, arr]
    in_specs.append(pl.BlockSpec((8, tc), lambda j, i: (0, j)))
    in_specs.append(pl.BlockSpec((tr, tc), lambda j, i: (i, j)))
    in_specs.append(pl.BlockSpec((8, tc), lambda j, i: (jnp.minimum((i + 1) * r8, nr * r8 - 1), j)))
    args += [w8, dy, dy]
    for arr, cb0 in extras:
        in_specs.append(pl.BlockSpec((tr, tc), lambda j, i, cb0=cb0: (i, cb0 + j)))
        args.append(arr)
    out_shape = [jax.ShapeDtypeStruct((rows, c), dt) for dt in outs]
    out_specs = [pl.BlockSpec((tr, tc), lambda j, i: (i, j)) for _ in outs]
    aliases = {}
    if into is not None:
        arr, cb0 = into
        aliases = {len(args): 0}
        in_specs.append(pl.BlockSpec(memory_space=pl.ANY))
        args.append(arr)
        out_shape[0] = jax.ShapeDtypeStruct(arr.shape, arr.dtype)
        out_specs[0] = pl.BlockSpec((tr, tc), lambda j, i, cb0=cb0: (i, cb0 + j))
    return pl.pallas_call(
        body, name=name, interpret=False,
        out_shape=out_shape + [jax.ShapeDtypeStruct((8, c), F32)],
        grid=(nc, nr), in_specs=in_specs,
        out_specs=out_specs + [pl.BlockSpec((8, tc), lambda j, i: (0, j))],
        scratch_shapes=[pltpu.VMEM((tr + 8, tc), F32), pltpu.VMEM((tr + 8, tc), F32)],
        input_output_aliases=aliases,
        compiler_params=_params(("parallel", "arbitrary")),
    )(*args)


def rms_fwd(h, w, *, name):
    rows = h.shape[0]
    tr = _pick(rows, (384, 128))

    def fn(i, x, wv):
        r = lax.rsqrt(jnp.mean(x * x, axis=1, keepdims=True) + EPS)
        return x * r * wv

    return rowwise(fn, [cols(h, tr), whole(w)], [out2d(rows, D, BF16, tr)], steps=rows // tr, name=name)[0]


def _rms_bwd_epi(row0, g, x, dr, wv):
    r = lax.rsqrt(jnp.mean(x * x, axis=1, keepdims=True) + EPS)
    xh = x * r
    gw = g * wv
    dx = r * (gw - xh * jnp.mean(gw * xh, axis=1, keepdims=True))
    row = row0 + lax.broadcasted_iota(jnp.int32, (x.shape[0], 1), 0)
    return jnp.where(row >= PAD, dr + dx, 0.0), jnp.sum(g * xh, axis=0, keepdims=True)


def dx_rms_bwd(dy, w, h, nw, dres, *, name, b_chip=False):
    return mm(dy, w, tb=True, b_chip=b_chip, tn=D, name=name, epi=_rms_bwd_epi,
              epi_ins=[(h, lambda j: 0), (dres, lambda j: 0)], epi_consts=[nw], epi_outs=[F32],
              epi_accs=[((1, D), F32)])


def loss_grad(h, target):
    rows = h.shape[0]

    def fn(i, y, t):
        diff = jnp.where(i >= HEAD0, y - t, 0.0)
        part = jnp.sum(jnp.sum(diff * diff, axis=1, keepdims=True), axis=0, keepdims=True)
        return diff * (1.0 / D), part * (0.5 / D)

    tgt = (target, (BLK, D), lambda i: (jnp.maximum(i - 1, 0), 0), "r2")
    return rowwise(fn, [cols(h, BLK), tgt], [out2d(rows, D, F32, BLK)], steps=rows // BLK,
                   name="loss_grad", accs=[((1, 128), F32)])


def adamw(w, g, m, v, *, name):
    shape = w.shape
    gs = list(g) if isinstance(g, (list, tuple)) else [g]
    nl = len(gs)
    w2, m2, v2 = (t.reshape(-1, shape[-1]) for t in (w, m, v))
    rows, width = w2.shape
    rl = rows // nl
    tr = _pick(rl, (256, 176, 128, 64, 16, 8))
    nr = rl // tr

    def fn(i, wv, mv, vv, *gvs):
        gv = gvs[0]
        for layer in range(1, nl):
            gv = jnp.where(i >= layer * rl, gvs[layer], gv)
        mn = B1 * mv + (1.0 - B1) * gv
        vn = B2 * vv + (1.0 - B2) * gv * gv
        mh = mn / (1.0 - B1 ** STEP)
        vh = vn / (1.0 - B2 ** STEP)
        return -LR * (mh / (jnp.sqrt(vh) + AEPS) + WD * wv), mn, vn, gv

    g_ins = [(t.reshape(rl, width), (tr, width), lambda i, layer=layer: (jnp.clip(i - layer * nr, 0, nr - 1), 0), "r2")
             for layer, t in enumerate(gs)]
    res = rowwise(fn, [cols(t, tr) for t in (w2, m2, v2)] + g_ins, [out2d(rows, width, F32, tr)] * 4,
                  steps=rows // tr, name=name)
    return [r.reshape(shape) for r in res]


HB = DN_H * CH
PAIR = 2


def _split(a):
    hi = a.astype(BF16)
    return hi, (a - hi.astype(F32)).astype(BF16)


def _dot1(a, b, ca=1, cb=0):
    return _dot(a.astype(BF16), b.astype(BF16), ca, cb)


def _dot3(a, b, ca=1, cb=0):
    ah, al = _split(a)
    bh, bl = _split(b)
    return _dot(ah, bh, ca, cb) + (_dot(ah, bl, ca, cb) + _dot(al, bh, ca, cb))


def _dot01(m01, b, ca=1, cb=0):
    bh, bl = _split(b)
    m = m01.astype(BF16)
    return _dot(m, bh, ca, cb) + _dot(m, bl, ca, cb)


def _stack(x):
    return jnp.concatenate([x[:, h * DN_D:(h + 1) * DN_D] for h in range(DN_H)], axis=0)


def _unstack(x):
    return jnp.concatenate([x[h * CH:(h + 1) * CH] for h in range(DN_H)], axis=1)


def _tri_inv(a, blk, eye):
    ad = jnp.where(blk, a, 0.0)
    lo = a - ad
    a2 = _dot3(ad, ad)
    a4 = _dot3(a2, a2)
    a8 = _dot3(a4, a4)
    dgi = _dot3(_dot3(_dot3(eye - ad, eye + a2), eye + a4), eye + a8)
    n = _dot3(dgi, lo)
    return _dot3(_dot3(eye - n, eye + _dot3(n, n)), dgi)


def _dn_masks():
    row = lax.broadcasted_iota(jnp.int32, (HB, HB), 0)
    col = lax.broadcasted_iota(jnp.int32, (HB, HB), 1)
    same = (row // CH) == (col // CH)
    incl = jnp.logical_and(same, row >= col)
    strict = jnp.logical_and(same, row > col)
    upper = jnp.logical_and(same, row <= col)
    blk = (row // 16) == (col // 16)
    eye = (row == col).astype(F32)
    return incl, strict, upper, blk, eye


def _dn_chunk(qv, kv, vv, bc, br, incl, strict):
    r64 = lax.broadcasted_iota(jnp.int32, (CH, CH), 0)
    c64 = lax.broadcasted_iota(jnp.int32, (CH, CH), 1)
    dcol = _dot01((r64 >= c64).astype(F32), bc)
    drow = _dot3(br, (r64 <= c64).astype(F32))
    col = lambda m, l0: jnp.concatenate([m[:, l0 + h:l0 + h + 1] for h in range(DN_H)], axis=0)
    b_c = col(bc, 0)
    d_c = col(dcol, 4)
    d_r = jnp.concatenate([drow[4 + h:5 + h, :] for h in range(DN_H)], axis=1)
    d_last_h = [dcol[CH - 1:CH, 4 + h:5 + h] for h in range(DN_H)]
    d_last = jnp.concatenate([jnp.broadcast_to(t, (CH, 1)) for t in d_last_h], axis=0)
    q, k, v = _stack(qv), _stack(kv), _stack(vv)
    dm = jnp.where(incl, jnp.exp(jnp.where(incl, d_c - d_r, 0.0)), 0.0)
    kk = _dot1(k, k, 1, 1)
    a = jnp.where(strict, b_c * kk * dm, 0.0)
    ed = jnp.exp(d_c)
    rhs = jnp.concatenate([v * b_c, k * (b_c * ed)], axis=1)
    qk = _dot1(q, k, 1, 1) * dm
    ekd = jnp.exp(d_last - d_c)
    gl = [jnp.exp(t) for t in d_last_h]
    return q, k, v, b_c, dm, kk, a, ed, rhs, qk, ekd, gl


def dn_fwd(qkv_n, bgcol, bgrow):
    rows = qkv_n.shape[0]
    nch = rows // CH

    def body(q_ref, k_ref, v_ref, bc_ref, br_ref, o_ref, s_out, ti_out, s_scr, prep, prep_qk, prep_gl):
        n = pl.program_id(0)

        @pl.when(n == 0)
        def _():
            s_scr[...] = jnp.zeros(s_scr.shape, F32)
            prep[...] = jnp.zeros(prep.shape, F32)
            prep_qk[...] = jnp.zeros(prep_qk.shape, F32)
            prep_gl[...] = jnp.zeros(prep_gl.shape, F32)

        live = n > 0
        for c in range(PAIR):
            u, w, qd, kd = prep[c, 0], prep[c, 1], prep[c, 2], prep[c, 3]
            v_new, o_state = [], []
            for h in range(DN_H):
                rs = slice(h * CH, (h + 1) * CH)
                s = s_scr[h]
                s_out[c, h] = s
                vn = u[rs] - _dot1(w[rs], s)
                v_new.append(vn)
                o_state.append(_dot1(qd[rs], s))
                s_scr[h] = jnp.where(live, prep_gl[c, h:h + 1, 0:1] * s + _dot1(kd[rs], vn, 0, 0), s)
            o = jnp.concatenate(o_state, axis=0) + _dot1(prep_qk[c], jnp.concatenate(v_new, axis=0))
            o_ref[c * CH:(c + 1) * CH, :] = _unstack(o)

        incl, strict, _, blk, eye = _dn_masks()
        for c in range(PAIR):
            rows_c = slice(c * CH, (c + 1) * CH)
            q, k, v, b_c, dm, kk, a, ed, rhs, qk_n, ekd, gl = _dn_chunk(
                q_ref[rows_c, :], k_ref[rows_c, :], v_ref[rows_c, :], bc_ref[rows_c, :], br_ref[c], incl, strict)
            tinv = _tri_inv(a, blk, eye)
            ti_out[c] = tinv
            sol = _dot3(tinv, rhs)
            prep[c, 0] = sol[:, :DN_D]
            prep[c, 1] = sol[:, DN_D:]
            prep[c, 2] = q * ed
            prep[c, 3] = k * ekd
            prep_qk[c] = qk_n
            prep_gl[c] = jnp.concatenate([jnp.broadcast_to(t, (1, 128)) for t in gl]
                                         + [jnp.zeros((8 - DN_H, 128), F32)], axis=0)

    assert nch % PAIR == 0
    npair = nch // PAIR
    last = npair - 1
    return pl.pallas_call(
        body, name="dn_fwd", interpret=False,
        out_shape=[jax.ShapeDtypeStruct((rows, DN_DIM), F32),
                   jax.ShapeDtypeStruct((nch, DN_H, DN_D, DN_D), F32),
                   jax.ShapeDtypeStruct((nch, HB, HB), F32)],
        grid=(npair + 1,),
        in_specs=[pl.BlockSpec((PAIR * CH, DN_DIM), lambda n: (jnp.minimum(n, last), 0)),
                  pl.BlockSpec((PAIR * CH, DN_DIM), lambda n: (jnp.minimum(n, last), 1)),
                  pl.BlockSpec((PAIR * CH, DN_DIM), lambda n: (jnp.minimum(n, last), 2)),
                  pl.BlockSpec((PAIR * CH, 128), lambda n: (jnp.minimum(n, last), 0)),
                  pl.BlockSpec((PAIR, 8, CH), lambda n: (jnp.minimum(n, last), 0, 0))],
        out_specs=[pl.BlockSpec((PAIR * CH, DN_DIM), lambda n: (jnp.maximum(n - 1, 0), 0)),
                   pl.BlockSpec((PAIR, DN_H, DN_D, DN_D), lambda n: (jnp.maximum(n - 1, 0), 0, 0, 0)),
                   pl.BlockSpec((PAIR, HB, HB), lambda n: (jnp.minimum(n, last), 0, 0))],
        scratch_shapes=[pltpu.VMEM((DN_H, DN_D, DN_D), F32), pltpu.VMEM((PAIR, 4, HB, DN_D), F32),
                        pltpu.VMEM((PAIR, HB, HB), F32), pltpu.VMEM((PAIR, 8, 128), F32)],
        compiler_params=_params(("arbitrary",)),
    )(qkv_n, qkv_n, qkv_n, bgcol, bgrow)


def dn_bwd(qkv_n, bgcol, bgrow, s_all, ti_all, do):
    rows = qkv_n.shape[0]
    nch = rows // CH

    def body(q_ref, k_ref, v_ref, bc_ref, br_ref, s_ref, ti_ref, do_ref, dq_ref, dk_ref, dv_ref, dbg_ref, ds_scr):
        n = pl.program_id(0)

        @pl.when(n == 0)
        def _():
            ds_scr[...] = jnp.zeros(ds_scr.shape, F32)

        incl, strict, upper, _, _ = _dn_masks()
        q, k, v, b_c, dm, kk, a, ed, rhs, qk, ekd, gl = _dn_chunk(q_ref[...], k_ref[...], v_ref[...], bc_ref[...],
                                                                  br_ref[0], incl, strict)
        tinv = ti_ref[0]
        g_o = _stack(do_ref[...])
        sol = _dot3(tinv, rhs)
        u, w = sol[:, :DN_D], sol[:, DN_D:]
        qd, kd = q * ed, k * ekd
        rsum = lambda t: jnp.sum(t, axis=1, keepdims=True)
        rows_of = [slice(h * CH, (h + 1) * CH) for h in range(DN_H)]
        s_h = [s_ref[0, h] for h in range(DN_H)]
        ds_h = [ds_scr[h] for h in range(DN_H)]
        v_new = jnp.concatenate([u[rs] - _dot1(w[rs], s) for rs, s in zip(rows_of, s_h)], axis=0)
        dv_new = _dot1(qk, g_o, 0, 0) + jnp.concatenate([_dot1(kd[rs], t) for rs, t in zip(rows_of, ds_h)], axis=0)
        dqd = jnp.concatenate([_dot1(g_o[rs], s, 1, 1) for rs, s in zip(rows_of, s_h)], axis=0)
        dkd = jnp.concatenate([_dot1(v_new[rs], t, 1, 1) for rs, t in zip(rows_of, ds_h)], axis=0)
        for h, rs in enumerate(rows_of):
            ds_scr[h] = _dot1(qd[rs], g_o[rs], 0, 0) + gl[h] * ds_h[h] - _dot1(w[rs], dv_new[rs], 0, 0)
        dw = jnp.concatenate([-_dot1(dv_new[rs], s, 1, 1) for rs, s in zip(rows_of, s_h)], axis=0)
        dqk = _dot1(g_o, v_new, 1, 1)
        drhs = _dot3(tinv, jnp.concatenate([dv_new, dw], axis=1), 0, 0)
        da = jnp.where(strict, -_dot1(drhs, sol, 1, 1), 0.0)
        drhs_u, drhs_w = drhs[:, :DN_D], drhs[:, DN_D:]
        s2 = rsum(drhs_w * k)
        dbeta = rsum(drhs_u * v) + s2 * ed + rsum(da * kk * dm)
        dkk = da * b_c * dm
        dqkr = dqk * dm
        mmat = da * a + dqk * qk
        tmp = rsum(dkd * kd)
        dd = (s2 * b_c * ed + rsum(mmat) - _dot3(mmat, jnp.ones((HB, 128), F32), 0, 0)[:, :1] + rsum(dqd * qd) - tmp)
        rowi = lax.broadcasted_iota(jnp.int32, (CH, 1), 0)
        last = []
        for h, rs in enumerate(rows_of):
            dgl = jnp.sum(rsum(s_h[h] * ds_h[h]), axis=0, keepdims=True)
            dd_last = jnp.sum(tmp[rs], axis=0, keepdims=True) + dgl * gl[h]
            last.append(jnp.where(rowi == CH - 1, dd_last, 0.0))
        dd = dd + jnp.concatenate(last, axis=0)
        dq_ref[...] = _unstack(_dot1(dqkr, k) + dqd * ed)
        dk_ref[...] = _unstack(drhs_w * (b_c * ed) + _dot1(dkk, k) + _dot1(dkk, k, 0, 0) + _dot1(dqkr, q, 0, 0)
                               + dkd * ekd)
        dv_ref[...] = _unstack(drhs_u * b_c)
        dg = _dot01(upper.astype(F32), jnp.broadcast_to(dd, (HB, 128)))[:, :1]
        lane = lax.broadcasted_iota(jnp.int32, (CH, 128), 1)
        out = jnp.zeros((CH, 128), F32)
        for h, rs in enumerate(rows_of):
            out = out + jnp.where(lane == h, dbeta[rs], 0.0) + jnp.where(lane == 4 + h, dg[rs], 0.0)
        dbg_ref[...] = out

    rev = lambda n: nch - 1 - n
    return pl.pallas_call(
        body, name="dn_bwd", interpret=False,
        out_shape=[jax.ShapeDtypeStruct((rows, DN_DIM), F32)] * 3 + [jax.ShapeDtypeStruct((rows, 128), F32)],
        grid=(nch,),
        in_specs=[pl.BlockSpec((CH, DN_DIM), lambda n: (rev(n), 0)),
                  pl.BlockSpec((CH, DN_DIM), lambda n: (rev(n), 1)),
                  pl.BlockSpec((CH, DN_DIM), lambda n: (rev(n), 2)),
                  pl.BlockSpec((CH, 128), lambda n: (rev(n), 0)),
                  pl.BlockSpec((1, 8, CH), lambda n: (rev(n), 0, 0)),
                  pl.BlockSpec((1, DN_H, DN_D, DN_D), lambda n: (rev(n), 0, 0, 0)),
                  pl.BlockSpec((1, HB, HB), lambda n: (rev(n), 0, 0)),
                  pl.BlockSpec((CH, DN_DIM), lambda n: (rev(n), 0))],
        out_specs=[pl.BlockSpec((CH, DN_DIM), lambda n: (rev(n), 0))] * 3 + [pl.BlockSpec((CH, 128), lambda n: (rev(n), 0))],
        scratch_shapes=[pltpu.VMEM((DN_H, DN_D, DN_D), F32)],
        compiler_params=_params(("arbitrary",)),
    )(qkv_n, qkv_n, qkv_n, bgcol, bgrow, s_all, ti_all, do)


def _swa_valid(n):
    c3 = lax.broadcasted_iota(jnp.int32, (3 * BLK, 4 * BLK), 0)
    r = lax.broadcasted_iota(jnp.int32, (3 * BLK, 4 * BLK), 1) % BLK
    c = c3 % BLK
    lo = jnp.where(c3 < BLK, PAD, jnp.where(c3 < 2 * BLK, r + 1 + jnp.where(n >= 2, 0, BLK), 0))
    hi = jnp.where(c3 < BLK, r + jnp.where(n >= 1, BLK, 0), jnp.where(c3 < 2 * BLK, BLK, r - jnp.where(n >= 1, 0, BLK)))
    return jnp.logical_and(c >= lo, c <= hi)


def _swa_probs(q, kcat, valid, sink):
    s = jnp.where(valid, _dot(kcat, q, 1, 1), -1e30)
    m = jnp.maximum(jnp.max(s, axis=0, keepdims=True), sink)
    e = jnp.where(valid, jnp.exp(s - m), 0.0)
    es = jnp.exp(sink - m)
    inv = 1.0 / (jnp.sum(e, axis=0, keepdims=True) + es)
    return e * inv, es * inv


def _swa_group(q_ref, sk_ref, h):
    q4 = jnp.concatenate([q_ref[4 * h + g] for g in range(4)], axis=0)
    sink4 = jnp.concatenate([jnp.full((1, BLK), sk_ref[4 * h + g], F32) for g in range(4)], axis=1)
    return q4, sink4


def _swa_specs():
    q = pl.BlockSpec((SWA_H, BLK, SWA_D), lambda n: (0, n, 0))
    km = pl.BlockSpec((SWA_KV, BLK, SWA_D), lambda n: (0, 0, 0))
    kp = pl.BlockSpec((SWA_KV, BLK, SWA_D), lambda n: (0, jnp.maximum(n - 1, 0), 0))
    kc = pl.BlockSpec((SWA_KV, BLK, SWA_D), lambda n: (0, n, 0))
    return [q, km, kp, kc, km, kp, kc]


def swa_fwd(qh, kh, vh, sinks):
    rows = qh.shape[1]
    nb = rows // BLK

    def body(q_ref, km, kp, kc, vm, vp, vc, sk_ref, o_ref):
        n = pl.program_id(0)
        valid = _swa_valid(n)
        outs = []
        for h in range(SWA_KV):
            kcat = jnp.concatenate([km[h], kp[h], kc[h]], axis=0)
            vcat = jnp.concatenate([vm[h], vp[h], vc[h]], axis=0)
            q4, sink4 = _swa_group(q_ref, sk_ref, h)
            p, _ = _swa_probs(q4, kcat, valid, sink4)
            o4 = _dot(p.astype(BF16), vcat, 0, 0)
            outs += [o4[g * BLK:(g + 1) * BLK] for g in range(4)]
        o_ref[...] = jnp.concatenate(outs, axis=1).astype(BF16)

    return pl.pallas_call(
        body, name="swa_fwd", interpret=False,
        out_shape=jax.ShapeDtypeStruct((rows, SWA_H * SWA_D), BF16),
        grid=(nb,),
        in_specs=_swa_specs() + [pl.BlockSpec(memory_space=pltpu.SMEM)],
        out_specs=pl.BlockSpec((BLK, SWA_H * SWA_D), lambda n: (n, 0)),
        compiler_params=_params(("parallel",)),
    )(qh, kh, kh, kh, vh, vh, vh, sinks)


def swa_bwd(qh, kh, vh, sinks, do):
    rows = qh.shape[1]
    nb = rows // BLK

    def body(q_ref, km, kp, kc, vm, vp, vc, do_ref, sk_ref, dq_ref, dk_ref, dv_ref, dsk_ref):
        n = pl.program_id(0)

        @pl.when(n == 0)
        def _():
            dk_ref[...] = jnp.zeros(dk_ref.shape, F32)
            dv_ref[...] = jnp.zeros(dv_ref.shape, F32)

        valid = _swa_valid(n)
        g_all = do_ref[...]
        rowi = lax.broadcasted_iota(jnp.int32, (SWA_H, 128), 0)
        dsk = jnp.zeros((SWA_H, 128), F32)
        pm = pl.multiple_of(jnp.maximum(n - 1, 0) * BLK, BLK)
        pc = pl.multiple_of(n * BLK, BLK)
        for h in range(SWA_KV):
            kcat = jnp.concatenate([km[h], kp[h], kc[h]], axis=0)
            vcat = jnp.concatenate([vm[h], vp[h], vc[h]], axis=0)
            q4, sink4 = _swa_group(q_ref, sk_ref, h)
            p, ps = _swa_probs(q4, kcat, valid, sink4)
            g4 = jnp.concatenate([g_all[:, (4 * h + g) * SWA_D:(4 * h + g + 1) * SWA_D] for g in range(4)], axis=0)
            dp = _dot(vcat, g4, 1, 1)
            delta = jnp.sum(p * dp, axis=0, keepdims=True)
            ds = (p * (dp - delta)).astype(BF16)
            dq4 = _dot(ds, kcat, 0, 0)
            dkc = _dot(ds, q4)
            dvc = _dot(p.astype(BF16), g4)
            t = ps * delta
            for g in range(4):
                dq_ref[4 * h + g] = dq4[g * BLK:(g + 1) * BLK]
                part = -jnp.sum(t[:, g * BLK:(g + 1) * BLK], axis=1, keepdims=True)
                dsk = dsk + jnp.where(rowi == 4 * h + g, part, 0.0)
            lanes = slice(h * SWA_D, (h + 1) * SWA_D)
            for ref, val in ((dk_ref, dkc), (dv_ref, dvc)):
                ref[0:BLK, lanes] += val[0:BLK]
                ref[pl.ds(pm, BLK), lanes] += val[BLK:2 * BLK]
                ref[pl.ds(pc, BLK), lanes] += val[2 * BLK:]
        dsk_ref[0] = dsk

    return pl.pallas_call(
        body, name="swa_bwd", interpret=False,
        out_shape=[jax.ShapeDtypeStruct((SWA_H, rows, SWA_D), F32),
                   jax.ShapeDtypeStruct((rows, SWA_KV * SWA_D), F32),
                   jax.ShapeDtypeStruct((rows, SWA_KV * SWA_D), F32),
                   jax.ShapeDtypeStruct((nb, SWA_H, 128), F32)],
        grid=(nb,),
        in_specs=_swa_specs() + [pl.BlockSpec((BLK, SWA_H * SWA_D), lambda n: (n, 0)),
                                 pl.BlockSpec(memory_space=pltpu.SMEM)],
        out_specs=[pl.BlockSpec((SWA_H, BLK, SWA_D), lambda n: (0, n, 0)),
                   pl.BlockSpec((rows, SWA_KV * SWA_D), lambda n: (0, 0)),
                   pl.BlockSpec((rows, SWA_KV * SWA_D), lambda n: (0, 0)),
                   pl.BlockSpec((1, SWA_H, 128), lambda n: (n, 0, 0))],
        compiler_params=_params(("arbitrary",)),
    )(qh, kh, kh, kh, vh, vh, vh, do, sinks)


QK_W = (SWA_H + SWA_KV) * SWA_D


def _head_mean(t):
    r = lax.broadcasted_iota(jnp.int32, (128, 128), 0) // SWA_D
    c = lax.broadcasted_iota(jnp.int32, (128, 128), 1) // SWA_D
    blk = jnp.where(r == c, 1.0 / SWA_D, 0.0).astype(BF16)
    out = []
    for i in range(t.shape[1] // 128):
        hi, lo = _split(t[:, 128 * i:128 * (i + 1)])
        out.append(_dot(hi, blk) + _dot(lo, blk))
    return jnp.concatenate(out, axis=1)


def _qk_scales(qw, kw):
    scale = SWA_D ** -0.5
    wt = jnp.concatenate([jnp.tile(qw.astype(F32) * scale, (1, SWA_H)), jnp.tile(kw.astype(F32), (1, SWA_KV))], axis=1)
    st = jnp.concatenate([jnp.full((1, SWA_H * SWA_D), scale, F32), jnp.ones((1, SWA_KV * SWA_D), F32)], axis=1)
    return wt, st


def qknorm_fwd(qkv, qw, kw):
    rows = qkv.shape[0]
    tr = _pick(rows, (384, 128))
    wt, _ = _qk_scales(qw, kw)

    def fn(i, x, w):
        xq = x[:, :QK_W]
        y = xq * lax.rsqrt(_head_mean(xq * xq) + EPS) * w
        head = lambda t, j: t[:, j * SWA_D:(j + 1) * SWA_D][None]
        qo = jnp.concatenate([head(y, j) for j in range(SWA_H)], axis=0)
        ko = jnp.concatenate([head(y, SWA_H + j) for j in range(SWA_KV)], axis=0)
        vo = jnp.concatenate([head(x, SWA_H + SWA_KV + j) for j in range(SWA_KV)], axis=0)
        return qo, ko, vo

    hm = lambda nh: ((nh, rows, SWA_D), BF16, (nh, tr, SWA_D), lambda i: (0, i, 0), "r3")
    return rowwise(fn, [cols(qkv, tr), whole(wt)], [hm(SWA_H), hm(SWA_KV), hm(SWA_KV)],
                   steps=rows // tr, name="qknorm_fwd")


def qknorm_bwd(qkv, qw, kw, dqh, dk, dv):
    rows = qkv.shape[0]
    tr = _pick(rows, (384, 128))
    wt, st = _qk_scales(qw, kw)

    def fn(i, x, w, sc, dq, dkv, dvv):
        xq = x[:, :QK_W]
        dy = jnp.concatenate([dq[j] for j in range(SWA_H)] + [dkv], axis=1)
        r = lax.rsqrt(_head_mean(xq * xq) + EPS)
        xh = xq * r
        gw = dy * w
        dx = r * (gw - xh * _head_mean(gw * xh))
        return jnp.concatenate([dx, dvv], axis=1), jnp.sum(dy * sc * xh, axis=0, keepdims=True)

    dqkv, dw = rowwise(fn, [cols(qkv, tr), whole(wt), whole(st), heads(dqh, tr), cols(dk, tr), cols(dv, tr)],
                       [out2d(rows, 1536, BF16, tr)], steps=rows // tr, name="qknorm_bwd", accs=[((1, QK_W), F32)])
    dw = dw.reshape(SWA_H + SWA_KV, SWA_D)
    return dqkv, jnp.sum(dw[:SWA_H], axis=0, keepdims=True), jnp.sum(dw[SWA_H:], axis=0, keepdims=True)


def _place():
    return lax.axis_index("x"), lax.axis_index("y"), lax.axis_index("c")


ANY = pl.BlockSpec(memory_space=pl.ANY)


def _rcopy(ssem, rsem, k, src, dst, to):
    return pltpu.make_async_remote_copy(src_ref=src, dst_ref=dst, send_sem=ssem.at[k], recv_sem=rsem.at[k],
                                        device_id=to, device_id_type=MESH)


def gather_weights(shards, small):
    n = len(shards)
    halves = [t.shape[0] // 2 for t in shards]

    def body(*refs):
        s_refs, small_ref = refs[:n], refs[n]
        o_refs, osmall = refs[n + 1:2 * n + 1], refs[2 * n + 1]
        ssem, rsem, lsem = refs[2 * n + 2:]
        x, y, c = _place()
        me = 2 * x + y
        chips = [(1 - x, y), (x, 1 - y), (1 - x, 1 - y)]

        def half(k, s, hh):
            return o_refs[k].at[s, pl.ds(hh * halves[k], halves[k]), :]

        loc = pltpu.make_async_copy(small_ref, osmall.at[me], lsem)
        loc.start()
        sends = []
        for k in range(n):
            for j, (px, py) in enumerate(chips):
                sends.append(_rcopy(ssem, rsem, 6 * k + j, s_refs[k].at[pl.ds(c * halves[k], halves[k]), :],
                                    half(k, me, c), (px, py, c)))
        for j, (px, py) in enumerate(chips):
            sends.append(_rcopy(ssem, rsem, 6 * n + j, small_ref, osmall.at[me], (px, py, c)))
        for cp in sends:
            cp.start()
        for k in range(n):
            for j, (px, py) in enumerate(chips):
                s = 2 * px + py
                _rcopy(ssem, rsem, 6 * k + j, half(k, s, c), half(k, s, c), (x, y, c)).wait_recv()
                fwd = _rcopy(ssem, rsem, 6 * k + 3 + j, half(k, s, c), half(k, s, c), (x, y, 1 - c))
                fwd.start()
                sends.append(fwd)
        for k in range(n):
            for j, (px, py) in enumerate(chips):
                s = 2 * px + py
                _rcopy(ssem, rsem, 6 * k + 3 + j, half(k, s, 1 - c), half(k, s, 1 - c), (x, y, c)).wait_recv()
        for j, (px, py) in enumerate(chips):
            s = 2 * px + py
            _rcopy(ssem, rsem, 6 * n + j, osmall.at[s], osmall.at[s], (x, y, c)).wait_recv()
        for cp in sends:
            cp.wait_send()
        loc.wait()

    res = pl.pallas_call(
        body, name="gather_weights", interpret=False,
        out_shape=[jax.ShapeDtypeStruct((4,) + t.shape, t.dtype) for t in shards]
        + [jax.ShapeDtypeStruct((4, SW_ROWS, 1024), F32)],
        in_specs=[ANY] * (n + 1), out_specs=[ANY] * (n + 1),
        scratch_shapes=[pltpu.SemaphoreType.DMA((6 * n + 3,)), pltpu.SemaphoreType.DMA((6 * n + 3,)),
                        pltpu.SemaphoreType.DMA],
    )(*shards, small)
    return res[:n], res[n]


def _handshake(peers):
    barrier = pltpu.get_barrier_semaphore()
    for peer in peers:
        pl.semaphore_signal(barrier, inc=1, device_id=peer, device_id_type=MESH)
    pl.semaphore_wait(barrier, len(peers))


def gather_weights_beside(shards):
    n = len(shards)
    halves = [t.shape[0] // 2 for t in shards]

    def body(*refs):
        s_refs, o_refs, ssem, rsem = refs[:n], refs[n:2 * n], refs[2 * n], refs[2 * n + 1]
        x, y, c = _place()
        me = 2 * x + y
        chips = [(1 - x, y), (x, 1 - y), (1 - x, 1 - y)]
        _handshake([(px, py, c) for px, py in chips] + [(x, y, 1 - c)])

        def half(k, s, hh):
            return o_refs[k].at[s, pl.ds(hh * halves[k], halves[k]), :]

        sends = []
        for k in range(n):
            for j, (px, py) in enumerate(chips):
                sends.append(_rcopy(ssem, rsem, 6 * k + j, s_refs[k].at[pl.ds(c * halves[k], halves[k]), :],
                                    half(k, me, c), (px, py, c)))
        for cp in sends:
            cp.start()
        for k in range(n):
            for j, (px, py) in enumerate(chips):
                s = 2 * px + py
                _rcopy(ssem, rsem, 6 * k + j, half(k, s, c), half(k, s, c), (x, y, c)).wait_recv()
                fwd = _rcopy(ssem, rsem, 6 * k + 3 + j, half(k, s, c), half(k, s, c), (x, y, 1 - c))
                fwd.start()
                sends.append(fwd)
        for k in range(n):
            for j, (px, py) in enumerate(chips):
                s = 2 * px + py
                _rcopy(ssem, rsem, 6 * k + 3 + j, half(k, s, 1 - c), half(k, s, 1 - c), (x, y, c)).wait_recv()
        for cp in sends:
            cp.wait_send()

    return pl.kernel(
        body, name="gather_weights_beside",
        out_type=[jax.ShapeDtypeStruct((4,) + t.shape, t.dtype) for t in shards],
        mesh=plsc.ScalarSubcoreMesh(axis_name="sequencer", num_cores=1),
        scratch_types=[pltpu.SemaphoreType.DMA((6 * n,)), pltpu.SemaphoreType.DMA((6 * n,))],
        compiler_params=pltpu.CompilerParams(collective_id=1),
    )(*shards)


def swap_halves(gs, *, name):
    n = len(gs)

    def body(*refs):
        g_refs, o_refs, ssem, rsem = refs[:n], refs[n:2 * n], refs[2 * n], refs[2 * n + 1]
        x, y, c = _place()
        cps = []
        for k in range(n):
            hk = g_refs[k].shape[1] // 2
            cps.append(_rcopy(ssem, rsem, k, g_refs[k].at[:, pl.ds((1 - c) * hk, hk), :], o_refs[k], (x, y, 1 - c)))
        for cp in cps:
            cp.start()
        for cp in cps:
            cp.wait()

    return pl.pallas_call(
        body, name=name, interpret=False,
        out_shape=[jax.ShapeDtypeStruct((4, t.shape[1] // 2, t.shape[2]), t.dtype) for t in gs],
        in_specs=[ANY] * n, out_specs=[ANY] * n,
        scratch_shapes=[pltpu.SemaphoreType.DMA((n,)), pltpu.SemaphoreType.DMA((n,))],
    )(*gs)


def _sum_rows(hk):
    return _pick(hk, (512, 352, 256, 128))


def pair_sum(g, other, c_idx, *, name):
    _, hk, width = other.shape
    tr = _sum_rows(hk)
    nbk = hk // tr

    def body(c_ref, g_ref, o_ref, out_ref):
        out_ref[...] = (g_ref[...].astype(F32) + o_ref[...].astype(F32)).astype(BF16)

    return pl.pallas_call(
        body, name=name, interpret=False,
        out_shape=jax.ShapeDtypeStruct((4, hk, width), BF16),
        grid_spec=pltpu.PrefetchScalarGridSpec(
            num_scalar_prefetch=1, grid=(4, nbk),
            in_specs=[pl.BlockSpec((1, tr, width), lambda s, i, c_ref: (s, c_ref[0] * nbk + i, 0)),
                      pl.BlockSpec((1, tr, width), lambda s, i, c_ref: (s, i, 0))],
            out_specs=pl.BlockSpec((1, tr, width), lambda s, i, c_ref: (s, i, 0))),
        compiler_params=_params(("parallel", "parallel")),
    )(c_idx, g, other)


def chip_sum(p, got, idx, *, name):
    _, hk, width = got.shape
    tr = _sum_rows(hk)
    nbk = hk // tr

    def body(idx_ref, p_ref, g_ref, out_ref):
        acc = p_ref[0].astype(F32)
        for j in range(3):
            acc = acc + g_ref[j].astype(F32)
        out_ref[0] = acc

    return pl.pallas_call(
        body, name=name, interpret=False,
        out_shape=jax.ShapeDtypeStruct((2, hk, width), F32),
        grid_spec=pltpu.PrefetchScalarGridSpec(
            num_scalar_prefetch=1, grid=(nbk,),
            in_specs=[pl.BlockSpec((1, tr, width), lambda i, idx_ref: (idx_ref[0], i, 0)),
                      pl.BlockSpec((3, tr, width), lambda i, idx_ref: (0, i, 0))],
            out_specs=pl.BlockSpec((1, tr, width), lambda i, idx_ref: (idx_ref[1], i, 0))),
        compiler_params=_params(("parallel",)),
    )(idx, p, got)


def join_halves(qs):
    n = len(qs)

    def body(*refs):
        q_refs, o_refs, ssem, rsem = refs[:n], refs[n:2 * n], refs[2 * n], refs[2 * n + 1]
        x, y, c = _place()
        cps = [_rcopy(ssem, rsem, k, q_refs[k].at[c], o_refs[k].at[c], (x, y, 1 - c)) for k in range(n)]
        for cp in cps:
            cp.start()
        for k in range(n):
            _rcopy(ssem, rsem, k, q_refs[k].at[c], o_refs[k].at[1 - c], (x, y, 1 - c)).wait_recv()
        for cp in cps:
            cp.wait_send()

    return pl.pallas_call(
        body, name="join_halves", interpret=False,
        out_shape=[jax.ShapeDtypeStruct(t.shape, t.dtype) for t in qs],
        in_specs=[ANY] * n, out_specs=[ANY] * n, input_output_aliases={k: k for k in range(n)},
        scratch_shapes=[pltpu.SemaphoreType.DMA((n,)), pltpu.SemaphoreType.DMA((n,))],
    )(*qs)


def scatter_chips_beside(ps, cid, name):
    n = len(ps)

    def body(*refs):
        p_refs, o_refs, ssem, rsem = refs[:n], refs[n:2 * n], refs[2 * n], refs[2 * n + 1]
        x, y, c = _place()
        chips = [(1 - x, y), (x, 1 - y), (1 - x, 1 - y)]
        _handshake([(px, py, c) for px, py in chips])
        cps = [_rcopy(ssem, rsem, 3 * k + j, p_refs[k].at[2 * px + py], o_refs[k].at[j], (px, py, c))
               for k in range(n) for j, (px, py) in enumerate(chips)]
        for cp in cps:
            cp.start()
        for cp in cps:
            cp.wait()

    return pl.kernel(
        body, name=name, out_type=[jax.ShapeDtypeStruct((3,) + t.shape[1:], t.dtype) for t in ps],
        mesh=plsc.ScalarSubcoreMesh(axis_name="sequencer", num_cores=1),
        scratch_types=[pltpu.SemaphoreType.DMA((3 * n,)), pltpu.SemaphoreType.DMA((3 * n,))],
        compiler_params=pltpu.CompilerParams(collective_id=cid),
    )(*ps)


def reduce_begin(gs, names, c_idx, cid, tag):
    others = swap_halves(gs, name=f"swap_halves_{tag}")
    pairs = [pair_sum(g, o, c_idx, name=f"pair_sum_{nm}") for g, o, nm in zip(gs, others, names)]
    return pairs, scatter_chips_beside(pairs, cid, f"scatter_chips_{tag}")


def reduce_end(pairs, gots, names, idx):
    mine = [chip_sum(p, g, idx, name=f"chip_sum_{nm}") for p, g, nm in zip(pairs, gots, names)]
    return [q.reshape(2 * q.shape[1], q.shape[2]) for q in join_halves(mine)]


def gather_small(v):
    def body(v_ref, o_ref, ssem, rsem, lsem):
        x, y, c = _place()
        peers = []
        for k in range(1, 8):
            fx, fy, fc = (k >> 2) & 1, (k >> 1) & 1, k & 1
            peers.append((1 - x if fx else x, 1 - y if fy else y, 1 - c if fc else c))
        _handshake(peers)
        loc = pltpu.make_async_copy(v_ref, o_ref.at[4 * x + 2 * y + c], lsem)
        loc.start()
        cps = []
        for k, (px, py, pc) in enumerate(peers):
            cps.append((pltpu.make_async_remote_copy(
                src_ref=v_ref, dst_ref=o_ref.at[4 * x + 2 * y + c], send_sem=ssem.at[k], recv_sem=rsem.at[k],
                device_id=(px, py, pc), device_id_type=MESH), 4 * px + 2 * py + pc))
        for cp, _ in cps:
            cp.start()
        for k, (cp, peer) in enumerate(cps):
            pltpu.make_async_remote_copy(
                src_ref=v_ref, dst_ref=o_ref.at[peer], send_sem=ssem.at[k], recv_sem=rsem.at[k],
                device_id=(x, y, c), device_id_type=MESH).wait_recv()
        for cp, _ in cps:
            cp.wait_send()
        loc.wait()

    return pl.kernel(
        body, name="gather_small", out_type=jax.ShapeDtypeStruct((8, SV_ROWS, 1024), F32),
        mesh=plsc.ScalarSubcoreMesh(axis_name="sequencer", num_cores=1),
        scratch_types=[pltpu.SemaphoreType.DMA((7,)), pltpu.SemaphoreType.DMA((7,)), pltpu.SemaphoreType.DMA],
        compiler_params=pltpu.CompilerParams(collective_id=6),
    )(v)


def sum_slots(a):
    def fn(i, t):
        acc = t[0]
        for k in range(1, 8):
            acc = acc + t[k]
        return acc

    return rowwise(fn, [whole(a)], [((SV_ROWS, 1024), F32, (SV_ROWS, 1024), lambda i: (0, 0), "w")], steps=1,
                   name="sum_slots")[0]


def _head_rms(x, nw):
    xs, rs = [], []
    for h in range(DN_H):
        xh = x[:, h * DN_D:(h + 1) * DN_D]
        r = lax.rsqrt(jnp.mean(xh * xh, axis=1, keepdims=True) + EPS)
        xs.append(xh * r)
        rs.append(r)
    return xs, rs


def bg_fwd(p, alog, dtb):
    rows = p.shape[0]
    tr = _pick(rows, (384, 128))

    def fn(i, x, al, dt):
        lane = lax.broadcasted_iota(jnp.int32, x.shape, 1)
        row = i + lax.broadcasted_iota(jnp.int32, x.shape, 0)
        g = -jnp.exp(al) * _softplus(x + dt)
        out = jnp.where(lane < 4, _sigmoid(x), jnp.where(lane < 8, g, 0.0))
        return jnp.where(row >= PAD, out, 0.0)

    return rowwise(fn, [cols(p, tr, 128, BG0 // 128), whole(alog), whole(dtb)], [out2d(rows, 128, F32, tr)],
                   steps=rows // tr, name="bg_fwd")[0]


def bg_bwd(p, alog, dtb, dbg):
    rows = p.shape[0]
    tr = _pick(rows, (384, 128))

    def fn(i, x, al, dt, g_in):
        lane = lax.broadcasted_iota(jnp.int32, x.shape, 1)
        row = i + lax.broadcasted_iota(jnp.int32, x.shape, 0)
        live = row >= PAD
        is_b = jnp.logical_and(live, lane < 4)
        is_g = jnp.logical_and(live, jnp.logical_and(lane >= 4, lane < 8))
        beta = _sigmoid(x)
        ea = jnp.exp(al)
        g = -ea * _softplus(x + dt)
        dalpha = jnp.where(is_g, g_in * (-ea) * _sigmoid(x + dt), 0.0)
        dx = jnp.where(is_b, g_in * beta * (1.0 - beta), dalpha)
        dal = jnp.sum(jnp.where(is_g, g_in * g, 0.0), axis=0, keepdims=True)
        return jnp.concatenate([dx, jnp.zeros(x.shape, F32)], axis=1), dal, jnp.sum(dalpha, axis=0, keepdims=True)

    return rowwise(fn, [cols(p, tr, 128, BG0 // 128), whole(alog), whole(dtb), cols(dbg, tr)],
                   [out2d(rows, 256, BF16, tr)], steps=rows // tr, name="bg_bwd",
                   accs=[((1, 128), F32), ((1, 128), F32)])


def dn_qkv_post(j, y):
    xs = _silu(y)
    sc = jnp.where(j == 0, DN_D ** -0.5, 1.0)
    outs = []
    for h in range(DN_H):
        xh = xs[:, h * DN_D:(h + 1) * DN_D]
        r = lax.rsqrt(jnp.sum(xh * xh, axis=1, keepdims=True) + EPS)
        outs.append(jnp.where(j < 2, xh * r * sc, xh))
    return jnp.concatenate(outs, axis=1), y


def dn_qkv_bwd(cq, dq, dk, dv):
    rows = cq.shape[0]
    tr = _pick(rows, (384, 128))

    def fn(i, c0, c1, c2, g0, g1, g2):
        pieces = []
        for kind, (cv, g) in enumerate(((c0, g0), (c1, g1), (c2, g2))):
            xs = _silu(cv)
            if kind < 2:
                sc = DN_D ** -0.5 if kind == 0 else 1.0
                ds = []
                for h in range(DN_H):
                    sl = slice(h * DN_D, (h + 1) * DN_D)
                    xh, gh = xs[:, sl], g[:, sl]
                    r = lax.rsqrt(jnp.sum(xh * xh, axis=1, keepdims=True) + EPS)
                    xn = xh * r
                    ds.append(sc * r * (gh - xn * jnp.sum(gh * xn, axis=1, keepdims=True)))
                dxs = jnp.concatenate(ds, axis=1)
            else:
                dxs = g
            pieces.append(dxs * _dsilu(cv))
        return jnp.concatenate(pieces, axis=1)

    ins = [cols(cq, tr, DN_DIM, k) for k in range(3)] + [cols(t, tr) for t in (dq, dk, dv)]
    return rowwise(fn, ins, [out2d(rows, 3 * DN_DIM, F32, tr)], steps=rows // tr, name="dn_qkv_bwd")[0]


def dn_out_fwd(o, p, nw):
    rows = o.shape[0]
    tr = _pick(rows, (384, 128))

    def fn(i, ov, z, w):
        xs, _ = _head_rms(ov, w)
        return jnp.concatenate(xs, axis=1) * jnp.concatenate([w] * DN_H, axis=1) * _silu(z)

    return rowwise(fn, [cols(o, tr), cols(p, tr, DN_DIM, 6), whole(nw)], [out2d(rows, DN_DIM, BF16, tr)],
                   steps=rows // tr, name="dn_out_fwd")[0]


def dn_out_bwd(o, p, nw, dymix):
    rows = o.shape[0]
    tr = _pick(rows, (384, 128))

    def fn(i, ov, z, w, dy):
        xs, rs = _head_rms(ov, w)
        sz = _silu(z)
        dn = dy * sz
        dos, dw = [], jnp.zeros((1, DN_D), F32)
        for h in range(DN_H):
            sl = slice(h * DN_D, (h + 1) * DN_D)
            gw = dn[:, sl] * w
            dos.append(rs[h] * (gw - xs[h] * jnp.mean(gw * xs[h], axis=1, keepdims=True)))
            dw = dw + jnp.sum(dn[:, sl] * xs[h], axis=0, keepdims=True)
        n = jnp.concatenate(xs, axis=1) * jnp.concatenate([w] * DN_H, axis=1)
        return jnp.concatenate(dos, axis=1), dy * n * _dsilu(z), dw

    return rowwise(fn, [cols(o, tr), cols(p, tr, DN_DIM, 6), whole(nw), cols(dymix, tr, DN_DIM, 1)],
                   [out2d(rows, DN_DIM, F32, tr), out2d(rows, DN_DIM, BF16, tr)], steps=rows // tr,
                   name="dn_out_bwd", accs=[((1, DN_D), F32)])


def conv_a_pre_bwd(dymix, cv, p):
    rows = cv.shape[0]
    tr = _pick(rows, (384, 128))

    def fn(i, dy, c, go):
        return dy * c, dy * go

    return rowwise(fn, [cols(dymix, tr, D_CONV, 0), cols(cv, tr), cols(p, tr, D_CONV, 1)],
                   [out2d(rows, D_CONV, BF16, tr), out2d(rows, D_CONV, F32, tr)], steps=rows // tr,
                   name="conv_a_pre_bwd")


def _act_bwd_epi(row0, da, gc, val):
    c, val = gc.astype(F32), val.astype(F32)
    return da * _silu(c), da * val * _dsilu(c)


def _rows8(w):
    return jnp.pad(w.astype(F32), ((0, 8 - w.shape[0]), (0, 0)))


def _lanes(v, at):
    return jnp.pad(v.astype(F32), (at, 128 - at - v.shape[0]))[None]


def add_norm(a, w, h, next_nw, *, name):
    if next_nw is None:
        return mm(a, w, add=h, name=name), None
    return mm(a, w, name=name, epi=_add_norm_epi, epi_ins=[(h, lambda j: 0)], epi_consts=[next_nw],
              epi_outs=[F32, BF16])


def _add_norm_epi(row0, t, h, nw):
    x = t + h
    return x, x * lax.rsqrt(jnp.mean(x * x, axis=1, keepdims=True) + EPS) * nw


def ffn_fwd(h, hn, w_up, cw8, w_down, tag, next_nw):
    rows = h.shape[0]
    tr = _pick(rows, (384, 128))
    u = mm(hn, w_up, out_dtype=BF16, b_chip=True, name=f"ffn{tag}_up")
    a, gc = conv_fwd([(u, 0)], cw8, 3, rows=rows, c=D_FF, tc=1408, tr=tr, name=f"ffn{tag}_conv",
                     post=lambda j, y, val: (_silu(y) * val.astype(F32), y), extras=[(u, 2)], outs=[BF16, BF16])
    out, hn_next = add_norm(a, w_down, h, next_nw, name=f"ffn{tag}_down")
    return out, hn_next, (hn, u, a, gc)


def ffn_bwd(h, nw, w_up, cw8, w_down, saved, dh, tag):
    hn, u, a, gc = saved
    rows = h.shape[0]
    tr = _pick(rows, (384, 128))
    du_half, dgc = mm(dh, w_down, tb=True, name=f"ffn{tag}_down_dx", epi=_act_bwd_epi,
                      epi_ins=[(gc, lambda j: j), (u, lambda j: 2 + j)],
                      epi_outs=[(BF16, 2 * D_FF, lambda j: 2 + j), F32])
    d_w_down = mm(a, dh, ta=True, out_dtype=BF16, name=f"ffn{tag}_down_dw")
    du, d_cw = conv_bwd([(u, 0)], cw8, 3, dgc, rows=rows, c=D_FF, tc=1408, tr=tr, name=f"ffn{tag}_conv_bwd",
                        post=lambda dx: dx, outs=[BF16], into=(du_half, 0))
    dh_new, d_nw = dx_rms_bwd(du, w_up, h, nw, dh, name=f"ffn{tag}_up_dx", b_chip=True)
    d_w_up = mm(hn, du, ta=True, out_dtype=BF16, out_chip=True, name=f"ffn{tag}_up_dw")
    return dh_new, d_nw, d_w_up, d_cw, d_w_down


def mixer_fwd(h, nw, w_in, ca8, dc8, alog, dtb, dnw, w_out, tie=None, next_nw=None):
    rows = h.shape[0]
    tr = _pick(rows, (384, 128))
    hn = rms_fwd(h, nw, name="mix_norm")
    p = mm(hn, w_in, name="mix_in")
    y_a, cv = conv_fwd([(p, 0), (p, 2)], ca8, 3, rows=rows, c=D_CONV, tc=D_CONV, tr=tr, name="conv_a",
                       pre=lambda gi, ah: gi * ah, post=lambda j, y, go: (go * y, y), extras=[(p, 1)],
                       outs=[BF16, F32])
    qkv_n, cq = conv_fwd([(p, 3)], dc8, 4, rows=rows, c=3 * DN_DIM, tc=DN_DIM, tr=tr, name="dn_conv",
                         post=dn_qkv_post, outs=[F32, F32], strip=tr)
    bgcol = bg_fwd(p, alog, dtb)
    if tie is not None:
        bgcol = tie(bgcol)
    bgrow = bgcol[:, :8].reshape(rows // CH, CH, 8).transpose(0, 2, 1)
    o, s_all, ti_all = dn_fwd(qkv_n, bgcol, bgrow)
    y_b = dn_out_fwd(o, p, dnw)
    ymix = jnp.concatenate([y_a, y_b], axis=1)
    out, hn_next = add_norm(ymix, w_out, h, next_nw, name="mix_out")
    return out, hn_next, (hn, p, cv, qkv_n, cq, bgcol, bgrow, o, s_all, ti_all, ymix)


def mixer_bwd(h, nw, w_in, ca8, dc8, alog, dtb, dnw, w_out, saved, dh):
    hn, p, cv, qkv_n, cq, bgcol, bgrow, o, s_all, ti_all, ymix = saved
    rows = h.shape[0]
    tr = _pick(rows, (384, 128))
    dymix = mm(dh, w_out, tb=True, name="mix_out_dx")
    d_w_out = mm(ymix, dh, ta=True, out_dtype=BF16, name="mix_out_dw")
    do, dz, d_dnw = dn_out_bwd(o, p, dnw, dymix)
    dq, dk, dv, dbg = dn_bwd(qkv_n, bgcol, bgrow, s_all, ti_all, do)
    dbg_p, d_alog, d_dtb = bg_bwd(p, alog, dtb, dbg)
    dcq = dn_qkv_bwd(cq, dq, dk, dv)
    dqkv, d_dc = conv_bwd([(p, 3)], dc8, 4, dcq, rows=rows, c=3 * DN_DIM, tc=DN_DIM, tr=tr, name="dn_conv_bwd",
                          post=lambda dx: dx, outs=[BF16])
    dgo, dcv = conv_a_pre_bwd(dymix, cv, p)
    dgi, dah, d_ca = conv_bwd([(p, 0), (p, 2)], ca8, 3, dcv, rows=rows, c=D_CONV, tc=D_CONV, tr=tr,
                              name="conv_a_bwd", pre=lambda gi, ah: gi * ah,
                              post=lambda dm, gi, ah: (dm * ah, dm * gi), extras=[(p, 0), (p, 2)], outs=[BF16, BF16])
    dp = jnp.concatenate([dgi, dgo, dah, dqkv, dz, dbg_p], axis=1)
    dh_new, d_nw = dx_rms_bwd(dp, w_in, h, nw, dh, name="mix_in_dx")
    d_w_in = mm(hn, dp, ta=True, out_dtype=BF16, name="mix_in_dw")
    return dh_new, d_nw, d_w_in, d_ca, d_dc, d_alog, d_dtb, d_dnw, d_w_out


def swa_layer_fwd(h, hn, wqkv, qw, kw, sinks, wo, next_nw):
    qkv = mm(hn, wqkv, name="swa_qkv")
    qh, kh, vh = qknorm_fwd(qkv, qw, kw)
    att = swa_fwd(qh, kh, vh, sinks)
    out, hn_next = add_norm(att, wo, h, next_nw, name="swa_out")
    return out, hn_next, (hn, qkv, qh, kh, vh, att)


def swa_layer_bwd(h, nw, wqkv, qw, kw, sinks, wo, saved, dh):
    hn, qkv, qh, kh, vh, att = saved
    datt = mm(dh, wo, tb=True, out_dtype=BF16, name="swa_out_dx")
    d_wo = mm(att, dh, ta=True, out_dtype=BF16, name="swa_out_dw")
    dqh, dkh, dvh, dsk = swa_bwd(qh, kh, vh, sinks, datt)
    dqkv, d_qw, d_kw = qknorm_bwd(qkv, qw, kw, dqh, dkh, dvh)
    dh_new, d_nw = dx_rms_bwd(dqkv, wqkv, h, nw, dh, name="swa_qkv_dx")
    d_wqkv = mm(hn, dqkv, ta=True, out_dtype=BF16, name="swa_qkv_dw")
    d_sinks = jnp.sum(dsk[:, :, 0], axis=0)
    return dh_new, d_nw, d_wqkv, d_qw, d_kw, d_sinks, d_wo


BIG = ("mix_w_in", "mix_w_out", "swa_wq", "swa_wk", "swa_wv", "swa_wo", "ffn_w_up", "ffn_w_down")


def _flat_pad(parts, rows):
    v = jnp.concatenate([t.astype(F32).reshape(-1) for t in parts])
    return jnp.pad(v, (0, rows * 1024 - v.shape[0])).reshape(rows, 1024)


def _split_flat(flat, shapes):
    v = flat.reshape(-1)
    out, o = [], 0
    for s in shapes:
        n = 1
        for d_ in s:
            n *= d_
        out.append(v[o:o + n].reshape(s))
        o += n
    return out


def local_step(x0, target0, meta_full, anw, fnw, w_in, ca8, dc8, alog, dtb, dnw, w_out, qw, kw, sinks, fc8, late,
               begin=None, tie=None):
    begin = begin or (lambda tag, names, grads: None)
    h0 = jnp.concatenate([jnp.zeros((PAD, D), F32), meta_full, x0], axis=0)
    h1, hn1, s_mix = mixer_fwd(h0, anw[0], w_in, ca8, dc8, alog, dtb, dnw, w_out, tie, fnw[0])
    wqkv, wo, w_up, w_down = late()
    h2, hn2, s_f0 = ffn_fwd(h1, hn1, w_up[0], fc8[0], w_down[0], 0, anw[1])
    h3, hn3, s_swa = swa_layer_fwd(h2, hn2, wqkv, qw, kw, sinks, wo, fnw[1])
    h4, _, s_f1 = ffn_fwd(h3, hn3, w_up[1], fc8[1], w_down[1], 1, None)
    dh, loss_l = loss_grad(h4, target0)
    dh, d_fnw1, d_up1, d_fc1, d_down1 = ffn_bwd(h3, fnw[1], w_up[1], fc8[1], w_down[1], s_f1, dh, 1)
    begin("ffn1", ("up1", "down1"), [d_up1, d_down1.reshape(4, 704, D)])
    dh, d_anw1, d_wqkv, d_qw, d_kw, d_sinks, d_wo = swa_layer_bwd(h2, anw[1], wqkv, qw, kw, sinks, wo, s_swa, dh)
    begin("swa", ("wq", "wk", "wv", "wo"),
          [d_wqkv[:, :D].reshape(4, 256, D), d_wqkv[:, D:D + 256].reshape(4, 256, 256),
           d_wqkv[:, D + 256:].reshape(4, 256, 256), d_wo.reshape(4, 256, D)])
    dh, d_fnw0, d_up0, d_fc0, d_down0 = ffn_bwd(h1, fnw[0], w_up[0], fc8[0], w_down[0], s_f0, dh, 0)
    begin("ffn0", ("up0", "down0"), [d_up0, d_down0.reshape(4, 704, D)])
    dh, d_anw0, d_w_in, d_ca, d_dc, d_alog, d_dtb, d_dnw, d_w_out = mixer_bwd(
        h0, anw[0], w_in, ca8, dc8, alog, dtb, dnw, w_out, s_mix, dh)
    begin("mix", ("w_in", "w_out"),
          [d_w_in[:, :IN_DIM].reshape(D, 4, 898).transpose(1, 0, 2), d_w_out.reshape(4, 256, D)])
    return (dh, loss_l, d_anw0, d_anw1, d_fnw0, d_fnw1, d_w_in, d_ca, d_dc, d_alog, d_dtb, d_dnw, d_w_out, d_wqkv,
            d_qw, d_kw, d_sinks, d_wo, d_up0, d_up1, d_fc0, d_fc1, d_down0, d_down1)


def kernel(x, meta_tokens, attn_norm_w, ffn_norm_w, mix_w_in, conv_a_w, dn_conv_w, dn_a_log, dn_dt_bias, dn_norm_w, mix_w_out, swa_wq, swa_wk, swa_wv, swa_q_norm_w, swa_k_norm_w, swa_sinks, swa_wo, ffn_w_up, ffn_conv_w, ffn_w_down, loss_target, m_meta_tokens, m_attn_norm_w, m_ffn_norm_w, m_mix_w_in, m_conv_a_w, m_dn_conv_w, m_dn_a_log, m_dn_dt_bias, m_dn_norm_w, m_mix_w_out, m_swa_wq, m_swa_wk, m_swa_wv, m_swa_q_norm_w, m_swa_k_norm_w, m_swa_sinks, m_swa_wo, m_ffn_w_up, m_ffn_conv_w, m_ffn_w_down, v_meta_tokens, v_attn_norm_w, v_ffn_norm_w, v_mix_w_in, v_conv_a_w, v_dn_conv_w, v_dn_a_log, v_dn_dt_bias, v_dn_norm_w, v_mix_w_out, v_swa_wq, v_swa_wk, v_swa_wv, v_swa_q_norm_w, v_swa_k_norm_w, v_swa_sinks, v_swa_wo, v_ffn_w_up, v_ffn_conv_w, v_ffn_w_down):
    ix, iy, ic = lax.axis_index("x"), lax.axis_index("y"), lax.axis_index("c")
    chip = 2 * ix + iy
    seq = x.shape[1]
    rows = HEAD0 + seq

    small_sharded = (conv_a_w, dn_conv_w, ffn_conv_w, meta_tokens)
    up_b, down_b = ffn_w_up.astype(BF16), ffn_w_down.astype(BF16)
    own = [mix_w_in[0].astype(BF16), mix_w_out[0].astype(BF16), swa_wq[0].astype(BF16), swa_wk[0].astype(BF16),
           swa_wv[0].astype(BF16), swa_wo[0].astype(BF16), up_b[0], up_b[1], down_b[0], down_b[1]]
    fill = lambda gathered, mine: [lax.dynamic_update_slice_in_dim(g, t[None], chip, axis=0)
                                   for g, t in zip(gathered, mine)]
    first, g_small = gather_weights(own[:2], _flat_pad(small_sharded, SW_ROWS))
    g_in, g_out = fill(first, own[:2])
    w_in = jnp.pad(g_in.transpose(1, 0, 2).reshape(D, IN_DIM), ((0, 0), (0, P_W - IN_DIM)))
    w_out = g_out.reshape(D, D)
    rest = {}

    def tie(t):
        t, *mine = lax.optimization_barrier((t, *own[2:]))
        rest["w"] = fill(gather_weights_beside(mine), mine)
        return t

    def late():
        g_q, g_k, g_v, g_o, g_up0, g_up1, g_dn0, g_dn1 = rest["w"]
        wqkv = jnp.concatenate([g_q.reshape(D, D), g_k.reshape(D, 256), g_v.reshape(D, 256)], axis=1)
        return wqkv, g_o.reshape(D, D), [g_up0, g_up1], [g_dn0.reshape(D_FF, D), g_dn1.reshape(D_FF, D)]

    gs = g_small.reshape(4, -1)
    ca_full = gs[:, 0:384].reshape(4, 3, 128).transpose(1, 0, 2).reshape(3, D_CONV)
    dc_full = gs[:, 384:1920].reshape(4, 4, 384).transpose(1, 0, 2).reshape(4, 3 * DN_DIM)
    fc_full = gs[:, 1920:6144].reshape(4, 2, 3, 704).transpose(1, 2, 0, 3).reshape(2, 3, D_FF)
    meta_full = gs[:, 6144:10240].reshape(4, N_META, 256).transpose(1, 0, 2).reshape(N_META, D)
    ca8, dc8 = _rows8(ca_full), _rows8(dc_full)
    fc8 = [_rows8(fc_full[0]), _rows8(fc_full[1])]
    alog, dtb = _lanes(dn_a_log[0], 4), _lanes(dn_dt_bias[0], 4)
    dnw = dn_norm_w.astype(F32)
    qw, kw = swa_q_norm_w.astype(F32), swa_k_norm_w.astype(F32)
    sinks = swa_sinks[0].astype(F32)
    anw = [attn_norm_w[0:1], attn_norm_w[1:2]]
    fnw = [ffn_norm_w[0:1], ffn_norm_w[1:2]]

    c_idx = jnp.reshape(ic, (1,)).astype(jnp.int32)
    chip_idx = jnp.stack([chip, ic]).astype(jnp.int32)
    begun = []

    def begin(tag, names, grads):
        pairs, gots = reduce_begin(grads, names, c_idx, 2 + len(begun), tag)
        begun.append((names, pairs, gots))

    (dh, loss_l, d_anw0, d_anw1, d_fnw0, d_fnw1, d_w_in, d_ca, d_dc, d_alog, d_dtb, d_dnw, d_w_out, d_wqkv, d_qw,
     d_kw, d_sinks, d_wo, d_up0, d_up1, d_fc0, d_fc1, d_down0, d_down1) = local_step(
        x[0], loss_target[0], meta_full, anw, fnw, w_in, ca8, dc8, alog, dtb, dnw, w_out, qw, kw, sinks, fc8, late,
        begin, tie)
    grad_x = dh[HEAD0:][None]

    small_parts = [jnp.concatenate([d_anw0, d_anw1], axis=0), jnp.concatenate([d_fnw0, d_fnw1], axis=0),
                   d_alog[0, 4:8], d_dtb[0, 4:8], d_dnw, d_qw, d_kw, d_sinks,
                   d_ca[:3], d_dc[:4], jnp.stack([d_fc0[:3], d_fc1[:3]]), dh[PAD:HEAD0], loss_l[0, 0:1]]
    small_shapes = [(2, D), (2, D), (1, 4), (1, 4), (1, DN_D), (1, SWA_D), (1, SWA_D), (1, SWA_H),
                    (1, 3, D_CONV), (1, 4, 3 * DN_DIM), (2, 3, D_FF), (N_META, D), ()]
    gathered_small = gather_small(_flat_pad(small_parts, SV_ROWS))

    red_big = {}
    for part in (begun[:-1], begun[-1:]):
        part_names = [n for names, _, _ in part for n in names]
        red_big.update(zip(part_names, reduce_end([p for _, ps, _ in part for p in ps],
                                                  [g for _, _, gs_ in part for g in gs_], part_names, chip_idx)))
    g_w_in, g_w_out, g_wq, g_wk, g_wv, g_wo, g_up0, g_up1, g_dn0, g_dn1 = [
        red_big[n] for n in ("w_in", "w_out", "wq", "wk", "wv", "wo", "up0", "up1", "down0", "down1")]

    grads = dict(mix_w_in=g_w_in, mix_w_out=g_w_out, swa_wq=g_wq, swa_wk=g_wk, swa_wv=g_wv, swa_wo=g_wo,
                 ffn_w_up=[g_up0, g_up1], ffn_w_down=[g_dn0, g_dn1])
    weights = dict(meta_tokens=meta_tokens, attn_norm_w=attn_norm_w, ffn_norm_w=ffn_norm_w, mix_w_in=mix_w_in,
                   conv_a_w=conv_a_w, dn_conv_w=dn_conv_w, dn_a_log=dn_a_log, dn_dt_bias=dn_dt_bias,
                   dn_norm_w=dn_norm_w, mix_w_out=mix_w_out, swa_wq=swa_wq, swa_wk=swa_wk, swa_wv=swa_wv,
                   swa_q_norm_w=swa_q_norm_w, swa_k_norm_w=swa_k_norm_w, swa_sinks=swa_sinks, swa_wo=swa_wo,
                   ffn_w_up=ffn_w_up, ffn_conv_w=ffn_conv_w, ffn_w_down=ffn_w_down)
    m_in = dict(meta_tokens=m_meta_tokens, attn_norm_w=m_attn_norm_w, ffn_norm_w=m_ffn_norm_w, mix_w_in=m_mix_w_in,
                conv_a_w=m_conv_a_w, dn_conv_w=m_dn_conv_w, dn_a_log=m_dn_a_log, dn_dt_bias=m_dn_dt_bias,
                dn_norm_w=m_dn_norm_w, mix_w_out=m_mix_w_out, swa_wq=m_swa_wq, swa_wk=m_swa_wk, swa_wv=m_swa_wv,
                swa_q_norm_w=m_swa_q_norm_w, swa_k_norm_w=m_swa_k_norm_w, swa_sinks=m_swa_sinks, swa_wo=m_swa_wo,
                ffn_w_up=m_ffn_w_up, ffn_conv_w=m_ffn_conv_w, ffn_w_down=m_ffn_w_down)
    v_in = dict(meta_tokens=v_meta_tokens, attn_norm_w=v_attn_norm_w, ffn_norm_w=v_ffn_norm_w, mix_w_in=v_mix_w_in,
                conv_a_w=v_conv_a_w, dn_conv_w=v_dn_conv_w, dn_a_log=v_dn_a_log, dn_dt_bias=v_dn_dt_bias,
                dn_norm_w=v_dn_norm_w, mix_w_out=v_mix_w_out, swa_wq=v_swa_wq, swa_wk=v_swa_wk, swa_wv=v_swa_wv,
                swa_q_norm_w=v_swa_q_norm_w, swa_k_norm_w=v_swa_k_norm_w, swa_sinks=v_swa_sinks, swa_wo=v_swa_wo,
                ffn_w_up=v_ffn_w_up, ffn_conv_w=v_ffn_conv_w, ffn_w_down=v_ffn_w_down)
    names = list(weights)
    small = [n for n in names if n not in BIG]
    delta, new_m, new_v = {}, {}, {}
    for n in BIG:
        delta[n], new_m[n], new_v[n], grads[n] = adamw(weights[n], grads[n], m_in[n], v_in[n], name=f"adamw_{n}")
    gathered_small, _ = lax.optimization_barrier((gathered_small, new_v["ffn_w_down"]))
    (g_anw, g_fnw, g_alog, g_dtb, g_dnw, g_qw, g_kw, g_sinks, g_ca_f, g_dc_f, g_fc_f, g_meta_f,
     loss) = _split_flat(sum_slots(gathered_small), small_shapes)
    grads.update(meta_tokens=lax.dynamic_slice_in_dim(g_meta_f, chip * 256, 256, axis=1), attn_norm_w=g_anw,
                 ffn_norm_w=g_fnw, conv_a_w=lax.dynamic_slice_in_dim(g_ca_f, chip * 128, 128, axis=2),
                 dn_conv_w=lax.dynamic_slice_in_dim(g_dc_f, chip * 384, 384, axis=2), dn_a_log=g_alog,
                 dn_dt_bias=g_dtb, dn_norm_w=g_dnw, swa_q_norm_w=g_qw, swa_k_norm_w=g_kw, swa_sinks=g_sinks,
                 ffn_conv_w=lax.dynamic_slice_in_dim(g_fc_f, chip * 704, 704, axis=2))
    grads = {n: grads[n].reshape(weights[n].shape) for n in names}
    shapes = [weights[n].shape for n in small]
    packed = [_flat_pad([t[n] for n in small], SW_ROWS) for t in (weights, grads, m_in, v_in)]
    for store, flat in zip((delta, new_m, new_v), adamw(*packed, name="adamw_small")):
        for n, t in zip(small, _split_flat(flat, shapes)):
            store[n] = t
    return (loss, grad_x, *[grads[n] for n in names], *[delta[n] for n in names],
            *[new_m[n] for n in names], *[new_v[n] for n in names])
```

```python
import functools

import jax
import jax.numpy as jnp
from jax import lax
from jax.experimental import pallas as pl
from jax.experimental.pallas import tpu as pltpu
from jax.experimental.pallas import tpu_sc as plsc

F32 = jnp.float32
BF16 = jnp.bfloat16
HI = lax.Precision.HIGHEST
MESH = pl.DeviceIdType.MESH

D = 1024
N_META = 16
PAD = 112
HEAD0 = PAD + N_META
D_CONV = 512
DN_H = 4
DN_D = 128
DN_DIM = 512
CH = 64
IN_DIM = 3592
P_W = 3840
BG0 = 3584
SWA_H = 16
SWA_KV = 4
SWA_D = 64
BLK = 128
NKEY = N_META + 2 * BLK
D_FF = 2816
EPS = 1e-6
LR, B1, B2, AEPS, WD, STEP = 0.001, 0.9, 0.999, 1e-08, 0.01, 10
VMEM_LIMIT = 48 * 1024 * 1024
MM_VMEM_BUDGET = 34 * 1024 * 1024
R_BIG = 6144
R_HALF = R_BIG // 2
SV_ROWS = 48
SW_ROWS = 16


def _pick(n, cands):
    for c in cands:
        if n % c == 0:
            return c
    return n


def _params(sem=None):
    return pltpu.CompilerParams(dimension_semantics=sem, vmem_limit_bytes=VMEM_LIMIT)


def _dot(a, b, ca=1, cb=0, prec=None):
    return lax.dot_general(a, b, (((ca,), (cb,)), ((), ())), precision=prec,
                           preferred_element_type=F32)


def _sigmoid(x):
    return 1.0 / (1.0 + jnp.exp(-x))


def _silu(x):
    return x * _sigmoid(x)


def _dsilu(x):
    s = _sigmoid(x)
    return s * (1.0 + x * (1.0 - s))


def _softplus(x):
    return jnp.maximum(x, 0.0) + jnp.log(1.0 + jnp.exp(-jnp.abs(x)))


def mm(a, b, *, name, ta=False, tb=False, out_dtype=F32, add=None, tm=None, tn=None, tk=None,
       b_chip=False, out_chip=False, epi=None, epi_ins=(), epi_consts=(), epi_outs=(), epi_accs=()):
    if epi is not None:
        return _mm_epi(a, b, name=name, tb=tb, tn=tn, b_chip=b_chip, epi=epi, epi_ins=epi_ins,
                       epi_consts=epi_consts, epi_outs=epi_outs, epi_accs=epi_accs)
    m, k = (a.shape[1], a.shape[0]) if ta else a.shape
    if b_chip:
        n = b.shape[1] if tb else 4 * b.shape[2]
        if tb:
            tk = b.shape[2]
        else:
            tn = b.shape[2]
    else:
        n = b.shape[0] if tb else b.shape[1]
    if out_chip:
        tn = n // 4
    tn = tn or _pick(n, (1408, 1024, 768, 512, 256, 128))
    tk = tk or (_pick(k, (1408, 704, 384, 128)) if ta else _pick(k, (1024, 1408, 768, 512, 128)))
    nk = k // tk
    if tm is None:
        isz = lambda t: jnp.dtype(t.dtype).itemsize
        osz = jnp.dtype(out_dtype).itemsize
        for tm in ((1408, 1024, 512, 384, 256, 128) if ta else (1408, 704, 512, 384, 256, 128)):
            need = 2 * (tm * tk * isz(a) + tk * tn * isz(b) + tm * tn * osz + (tm * tn * 4 if add is not None else 0))
            need += tm * tn * 4 if nk > 1 else 0
            if m % tm == 0 and need <= MM_VMEM_BUDGET:
                break
        else:
            tm = m
    dims = (((0 if ta else 1,), (1 if tb else 0,)), ((), ()))

    def body(*refs):
        if add is None:
            a_ref, b_ref, o_ref, acc_ref = refs
            add_ref = None
        else:
            a_ref, b_ref, add_ref, o_ref, acc_ref = refs
        part = lax.dot_general(a_ref[...].astype(BF16), b_ref[...].astype(BF16), dims,
                               preferred_element_type=F32)

        def finish(total):
            if add_ref is not None:
                total = total + add_ref[...]
            o_ref[...] = total.astype(out_dtype)

        if nk == 1:
            finish(part)
        else:
            kk = pl.program_id(2)

            @pl.when(kk == 0)
            def _():
                acc_ref[...] = part

            @pl.when(kk > 0)
            def _():
                acc_ref[...] += part

            @pl.when(kk == nk - 1)
            def _():
                finish(acc_ref[...])

    a_spec = pl.BlockSpec((tk, tm), lambda i, j, kk: (kk, i)) if ta else pl.BlockSpec((tm, tk), lambda i, j, kk: (i, kk))
    if b_chip and tb:
        b_spec = pl.BlockSpec((None, tn, tk), lambda i, j, kk: (kk, j, 0))
    elif b_chip:
        b_spec = pl.BlockSpec((None, tk, tn), lambda i, j, kk: (j, kk, 0))
    elif tb:
        b_spec = pl.BlockSpec((tn, tk), lambda i, j, kk: (j, kk))
    else:
        b_spec = pl.BlockSpec((tk, tn), lambda i, j, kk: (kk, j))
    o_spec = pl.BlockSpec((tm, tn), lambda i, j, kk: (i, j))
    in_specs = [a_spec, b_spec] + ([o_spec] if add is not None else [])
    args = [a, b] + ([add] if add is not None else [])
    out_spec = pl.BlockSpec((None, tm, tn), lambda i, j, kk: (j, i, 0)) if out_chip else o_spec
    return pl.pallas_call(
        body, name=name, interpret=False,
        out_shape=jax.ShapeDtypeStruct((4, m, tn) if out_chip else (m, n), out_dtype),
        grid=(m // tm, n // tn, nk), in_specs=in_specs, out_specs=out_spec,
        scratch_shapes=[pltpu.VMEM((tm, tn) if nk > 1 else (8, 128), F32)],
        compiler_params=_params(("parallel", "parallel", "arbitrary")),
    )(*args)


def _mm_epi(a, b, *, name, tb, tn, b_chip, epi, epi_ins, epi_consts, epi_outs, epi_accs):
    m, k = a.shape
    if b_chip:
        n = b.shape[1] if tb else 4 * b.shape[2]
        tk = b.shape[2] if tb else None
        tn = tn if tb else b.shape[2]
    else:
        n = b.shape[0] if tb else b.shape[1]
        tk = None
    tn = tn or _pick(n, (1408, 1024, 768, 512, 256, 128))
    tk = tk or _pick(k, (1024, 1408, 768, 512, 128))
    nk, nj = k // tk, n // tn
    isz = lambda t: jnp.dtype(t.dtype if hasattr(t, "dtype") else t).itemsize
    outs3 = [t if isinstance(t, tuple) else (t, n, lambda j: j) for t in epi_outs]
    side = sum(isz(t) for t, _ in epi_ins) + sum(isz(dt) for dt, _, _ in outs3)
    for tm in (1408, 704, 512, 384, 256, 128):
        need = 2 * (tm * tk * isz(a) + tk * tn * isz(b) + tm * tn * side) + (tm * tn * 4 if nk > 1 else 0)
        if m % tm == 0 and need <= MM_VMEM_BUDGET:
            break
    else:
        tm = m
    dims = (((1,), (1 if tb else 0,)), ((), ()))
    n_in, n_c, n_out, n_acc = len(epi_ins), len(epi_consts), len(epi_outs), len(epi_accs)

    def body(*refs):
        a_ref, b_ref = refs[:2]
        in_refs = refs[2:2 + n_in + n_c]
        out_refs = refs[2 + n_in + n_c:2 + n_in + n_c + n_out]
        acc_out = refs[2 + n_in + n_c + n_out:2 + n_in + n_c + n_out + n_acc]
        acc_ref = refs[-1]
        i, j, kk = pl.program_id(0), pl.program_id(1), pl.program_id(2)
        part = lax.dot_general(a_ref[...].astype(BF16), b_ref[...].astype(BF16), dims,
                               preferred_element_type=F32)

        def finish(total):
            res = epi(i * tm, total, *[r[...] for r in in_refs])
            if not isinstance(res, (tuple, list)):
                res = (res,)
            for r, v in zip(out_refs, res[:n_out]):
                r[...] = v.astype(r.dtype)
            if n_acc:
                @pl.when(jnp.logical_and(i == 0, j == 0))
                def _():
                    for r in acc_out:
                        r[...] = jnp.zeros(r.shape, r.dtype)

                for r, v in zip(acc_out, res[n_out:]):
                    r[...] += jnp.broadcast_to(v, r.shape).astype(r.dtype)

        if nk == 1:
            finish(part)
        else:
            @pl.when(kk == 0)
            def _():
                acc_ref[...] = part

            @pl.when(kk > 0)
            def _():
                acc_ref[...] += part

            @pl.when(kk == nk - 1)
            def _():
                finish(acc_ref[...])

    a_spec = pl.BlockSpec((tm, tk), lambda i, j, kk: (i, kk))
    if b_chip and tb:
        b_spec = pl.BlockSpec((None, tn, tk), lambda i, j, kk: (kk, j, 0))
    elif b_chip:
        b_spec = pl.BlockSpec((None, tk, tn), lambda i, j, kk: (j, kk, 0))
    elif tb:
        b_spec = pl.BlockSpec((tn, tk), lambda i, j, kk: (j, kk))
    else:
        b_spec = pl.BlockSpec((tk, tn), lambda i, j, kk: (kk, j))
    in_specs = [a_spec, b_spec]
    in_specs += [pl.BlockSpec((tm, tn), lambda i, j, kk, col=col: (i, col(j))) for _, col in epi_ins]
    in_specs += [pl.BlockSpec(t.shape, lambda i, j, kk, nd=t.ndim: (0,) * nd) for t in epi_consts]
    out_specs = [pl.BlockSpec((tm, tn), lambda i, j, kk, col=col: (i, col(j))) for _, _, col in outs3]
    out_specs += [pl.BlockSpec(s, lambda i, j, kk, nd=len(s): (0,) * nd) for s, _ in epi_accs]
    out_shape = [jax.ShapeDtypeStruct((m, width), dt) for dt, width, _ in outs3]
    out_shape += [jax.ShapeDtypeStruct(s, dt) for s, dt in epi_accs]
    sem = ("arbitrary", "arbitrary", "arbitrary") if n_acc else ("parallel", "parallel", "arbitrary")
    return pl.pallas_call(
        body, name=name, interpret=False, out_shape=out_shape,
        grid=(m // tm, nj, nk), in_specs=in_specs, out_specs=out_specs,
        scratch_shapes=[pltpu.VMEM((tm, tn) if nk > 1 else (8, 128), F32)],
        compiler_params=_params(sem),
    )(a, b, *[t for t, _ in epi_ins], *epi_consts)


def cols(arr, tr, width=None, cb=0):
    width = width or arr.shape[1]
    return (arr, (tr, width), lambda i: (i, cb), "r2")


def heads(arr, tr):
    return (arr, (arr.shape[0], tr, arr.shape[2]), lambda i: (0, i, 0), "r3")


def whole(arr):
    nd = arr.ndim
    return (arr, arr.shape, lambda i: (0,) * nd, "w")


STRIP = 16


def _rows_of(ref, kind, r0, n):
    if kind == "r2":
        return ref[pl.ds(r0, n), :]
    if kind == "r3":
        return ref[:, pl.ds(r0, n), :]
    return ref[...]


def _set_rows(ref, kind, r0, n, v):
    if kind == "r2":
        ref[pl.ds(r0, n), :] = v.astype(ref.dtype)
    elif kind == "r3":
        ref[:, pl.ds(r0, n), :] = v.astype(ref.dtype)
    else:
        ref[...] = v.astype(ref.dtype)


def rowwise(fn, ins, outs, *, steps, name, accs=(), strip=None):
    n_in, n_out, n_acc = len(ins), len(outs), len(accs)
    kin = [t[3] for t in ins]
    kout = [t[4] for t in outs]
    tr = next((t[1][0] if t[3] == "r2" else t[1][1] for t in ins if t[3] != "w"), 0)

    def body(*refs):
        i = pl.program_id(0)
        in_refs, out_refs, acc_refs = refs[:n_in], refs[n_in:n_in + n_out], refs[n_in + n_out:]
        if n_acc:
            @pl.when(i == 0)
            def _():
                for r in acc_refs:
                    r[...] = jnp.zeros(r.shape, r.dtype)

        def run(r0, n):
            res = fn(i * tr + r0, *[_rows_of(r, k, r0, n) for r, k in zip(in_refs, kin)])
            if not isinstance(res, (tuple, list)):
                res = (res,)
            for r, k, v in zip(out_refs, kout, res[:n_out]):
                _set_rows(r, k, r0, n, v)
            for r, v in zip(acc_refs, res[n_out:]):
                r[...] += jnp.broadcast_to(v, r.shape).astype(r.dtype)

        if strip is None or tr <= strip:
            run(0, tr)
        else:
            def step(s, carry):
                run(pl.multiple_of(s * strip, strip), strip)
                return carry
            lax.fori_loop(0, tr // strip, step, 0)

    def zmap(nd):
        return lambda i: (0,) * nd

    in_specs = [pl.BlockSpec(t[1], t[2]) for t in ins]
    out_specs = [pl.BlockSpec(t[2], t[3]) for t in outs]
    out_specs += [pl.BlockSpec(s, zmap(len(s))) for s, _ in accs]
    out_shape = [jax.ShapeDtypeStruct(t[0], t[1]) for t in outs]
    out_shape += [jax.ShapeDtypeStruct(s, d) for s, d in accs]
    res = pl.pallas_call(
        body, name=name, interpret=False, out_shape=out_shape, grid=(steps,),
        in_specs=in_specs, out_specs=out_specs,
        compiler_params=_params(("arbitrary",)),
    )(*[t[0] for t in ins])
    return res


def out2d(rows, width, dtype, tr):
    return ((rows, width), dtype, (tr, width), lambda i: (i, 0), "r2")


def conv_fwd(xs, w8, kw, *, rows, c, tc, tr, name, post, extras=(), outs=(), pre=None, strip=STRIP):
    nx, ne, no = len(xs), len(extras), len(outs)
    nr, nc = rows // tr, c // tc
    r8 = tr // 8
    st = strip

    def body(*refs):
        x_refs = refs[:2 * nx]
        w_ref = refs[2 * nx]
        e_refs = refs[2 * nx + 1:2 * nx + 1 + ne]
        o_refs = refs[2 * nx + 1 + ne:2 * nx + 1 + ne + no]
        scr = refs[-1]
        j, i = pl.program_id(0), pl.program_id(1)
        halo = [x_refs[2 * q + 1][...].astype(F32) for q in range(nx)]
        scr[0:8, :] = jnp.where(i > 0, pre(*halo) if pre else halo[0], 0.0)

        def fill(s, carry):
            r0 = pl.multiple_of(s * st, st)
            cur = [x_refs[2 * q][pl.ds(r0, st), :].astype(F32) for q in range(nx)]
            scr[pl.ds(8 + r0, st), :] = pre(*cur) if pre else cur[0]
            return carry

        def comp(s, carry):
            r0 = pl.multiple_of(s * st, st)
            win = scr[pl.ds(r0, st + 8), :]
            y = jnp.zeros((st, tc), F32)
            for q in range(kw):
                sh = kw - 1 - q
                y = y + w_ref[q:q + 1, :] * win[8 - sh:8 - sh + st]
            res = post(j, y, *[e[pl.ds(r0, st), :] for e in e_refs])
            if not isinstance(res, (tuple, list)):
                res = (res,)
            for r, v in zip(o_refs, res):
                r[pl.ds(r0, st), :] = v.astype(r.dtype)
            return carry

        lax.fori_loop(0, tr // st, fill, 0)
        lax.fori_loop(0, tr // st, comp, 0)

    in_specs, args = [], []
    for arr, cb0 in xs:
        in_specs.append(pl.BlockSpec((tr, tc), lambda j, i, cb0=cb0: (i, cb0 + j)))
        in_specs.append(pl.BlockSpec((8, tc), lambda j, i, cb0=cb0: (jnp.maximum(i * r8 - 1, 0), cb0 + j)))
        args += [arr, arr]
    in_specs.append(pl.BlockSpec((8, tc), lambda j, i: (0, j)))
    args.append(w8)
    for arr, cb0 in extras:
        in_specs.append(pl.BlockSpec((tr, tc), lambda j, i, cb0=cb0: (i, cb0 + j)))
        args.append(arr)
    return pl.pallas_call(
        body, name=name, interpret=False,
        out_shape=[jax.ShapeDtypeStruct((rows, c), dt) for dt in outs],
        grid=(nc, nr), in_specs=in_specs,
        out_specs=[pl.BlockSpec((tr, tc), lambda j, i: (i, j)) for _ in outs],
        scratch_shapes=[pltpu.VMEM((tr + 8, tc), F32)],
        compiler_params=_params(("parallel", "arbitrary")),
    )(*args)


def conv_bwd(xs, w8, kw, dy, *, rows, c, tc, tr, name, post, extras=(), outs=(), pre=None, into=None):
    nx, ne, no = len(xs), len(extras), len(outs)
    nr, nc = rows // tr, c // tc
    r8 = tr // 8

    def body(*refs):
        x_refs = refs[:2 * nx]
        w_ref, dy_ref, dyn_ref = refs[2 * nx:2 * nx + 3]
        e_refs = refs[2 * nx + 3:2 * nx + 3 + ne]
        first_out = 2 * nx + 3 + ne + (1 if into is not None else 0)
        o_refs = refs[first_out:first_out + no]
        dw_ref = refs[first_out + no]
        xscr, gscr = refs[-2], refs[-1]
        i = pl.program_id(1)
        halo = [x_refs[2 * q + 1][...].astype(F32) for q in range(nx)]
        xscr[0:8, :] = jnp.where(i > 0, pre(*halo) if pre else halo[0], 0.0)
        gscr[tr:tr + 8, :] = jnp.where(i < nr - 1, dyn_ref[...].astype(F32), 0.0)

        def fill(s, carry):
            r0 = pl.multiple_of(s * STRIP, STRIP)
            cur = [x_refs[2 * q][pl.ds(r0, STRIP), :].astype(F32) for q in range(nx)]
            xscr[pl.ds(8 + r0, STRIP), :] = pre(*cur) if pre else cur[0]
            gscr[pl.ds(r0, STRIP), :] = dy_ref[pl.ds(r0, STRIP), :].astype(F32)
            return carry

        def comp(s, dws):
            r0 = pl.multiple_of(s * STRIP, STRIP)
            gwin = gscr[pl.ds(r0, STRIP + 8), :]
            xwin = xscr[pl.ds(r0, STRIP + 8), :]
            g = gwin[0:STRIP]
            dx = jnp.zeros((STRIP, tc), F32)
            new = []
            for q in range(kw):
                sh = kw - 1 - q
                dx = dx + w_ref[q:q + 1, :] * gwin[sh:sh + STRIP]
                part = g * xwin[8 - sh:8 - sh + STRIP]
                new.append(dws[q] + part[0:8] + part[8:16])
            res = post(dx, *[e[pl.ds(r0, STRIP), :] for e in e_refs])
            if not isinstance(res, (tuple, list)):
                res = (res,)
            for r, v in zip(o_refs, res):
                r[pl.ds(r0, STRIP), :] = v.astype(r.dtype)
            return tuple(new)

        lax.fori_loop(0, tr // STRIP, fill, 0)
        dws = lax.fori_loop(0, tr // STRIP, comp, tuple(jnp.zeros((8, tc), F32) for _ in range(kw)))

        @pl.when(i == 0)
        def _():
            dw_ref[...] = jnp.zeros((8, tc), F32)

        dw_ref[...] += jnp.concatenate([jnp.sum(t, axis=0, keepdims=True) for t in dws]
                                       + [jnp.zeros((8 - kw, tc), F32)], axis=0)

    in_specs, args = [], []
    for arr, cb0 in xs:
        in_specs.append(pl.BlockSpec((tr, tc), lambda j, i, cb0=cb0: (i, cb0 + j)))
        in_specs.append(pl.BlockSpec((8, tc), lambda j, i, cb0=cb0: (jnp.maximum(i * r8 - 1, 0), cb0 + j)))
        args += [arr, arr]
    in_specs.append(pl.BlockSpec((8, tc), lambda j, i: (0, j)))
    in_specs.append(pl.BlockSpec((tr, tc), lambda j, i: (i, j)))
    in_specs.append(pl.BlockSpec((8, tc), lambda j, i: (jnp.minimum((i + 1) * r8, nr * r8 - 1), j)))
    args += [w8, dy, dy]
    for arr, cb0 in extras:
        in_specs.append(pl.BlockSpec((tr, tc), lambda j, i, cb0=cb0: (i, cb0 + j)))
        args.append(arr)
    out_shape = [jax.ShapeDtypeStruct((rows, c), dt) for dt in outs]
    out_specs = [pl.BlockSpec((tr, tc), lambda j, i: (i, j)) for _ in outs]
    aliases = {}
    if into is not None:
        arr, cb0 = into
        aliases = {len(args): 0}
        in_specs.append(pl.BlockSpec(memory_space=pl.ANY))
        args.append(arr)
        out_shape[0] = jax.ShapeDtypeStruct(arr.shape, arr.dtype)
        out_specs[0] = pl.BlockSpec((tr, tc), lambda j, i, cb0=cb0: (i, cb0 + j))
    return pl.pallas_call(
        body, name=name, interpret=False,
        out_shape=out_shape + [jax.ShapeDtypeStruct((8, c), F32)],
        grid=(nc, nr), in_specs=in_specs,
        out_specs=out_specs + [pl.BlockSpec((8, tc), lambda j, i: (0, j))],
        scratch_shapes=[pltpu.VMEM((tr + 8, tc), F32), pltpu.VMEM((tr + 8, tc), F32)],
        input_output_aliases=aliases,
        compiler_params=_params(("parallel", "arbitrary")),
    )(*args)


def rms_fwd(h, w, *, name):
    rows = h.shape[0]
    tr = _pick(rows, (384, 128))

    def fn(i, x, wv):
        r = lax.rsqrt(jnp.mean(x * x, axis=1, keepdims=True) + EPS)
        return x * r * wv

    return rowwise(fn, [cols(h, tr), whole(w)], [out2d(rows, D, BF16, tr)], steps=rows // tr, name=name)[0]


def _rms_bwd_epi(row0, g, x, dr, wv):
    r = lax.rsqrt(jnp.mean(x * x, axis=1, keepdims=True) + EPS)
    xh = x * r
    gw = g * wv
    dx = r * (gw - xh * jnp.mean(gw * xh, axis=1, keepdims=True))
    row = row0 + lax.broadcasted_iota(jnp.int32, (x.shape[0], 1), 0)
    return jnp.where(row >= PAD, dr + dx, 0.0), jnp.sum(g * xh, axis=0, keepdims=True)


def dx_rms_bwd(dy, w, h, nw, dres, *, name, b_chip=False):
    return mm(dy, w, tb=True, b_chip=b_chip, tn=D, name=name, epi=_rms_bwd_epi,
              epi_ins=[(h, lambda j: 0), (dres, lambda j: 0)], epi_consts=[nw], epi_outs=[F32],
              epi_accs=[((1, D), F32)])


def loss_grad(h, target):
    rows = h.shape[0]

    def fn(i, y, t):
        diff = jnp.where(i >= HEAD0, y - t, 0.0)
        part = jnp.sum(jnp.sum(diff * diff, axis=1, keepdims=True), axis=0, keepdims=True)
        return diff * (1.0 / D), part * (0.5 / D)

    tgt = (target, (BLK, D), lambda i: (jnp.maximum(i - 1, 0), 0), "r2")
    return rowwise(fn, [cols(h, BLK), tgt], [out2d(rows, D, F32, BLK)], steps=rows // BLK,
                   name="loss_grad", accs=[((1, 128), F32)])


def adamw(w, g, m, v, *, name):
    shape = w.shape
    gs = list(g) if isinstance(g, (list, tuple)) else [g]
    nl = len(gs)
    w2, m2, v2 = (t.reshape(-1, shape[-1]) for t in (w, m, v))
    rows, width = w2.shape
    rl = rows // nl
    tr = _pick(rl, (256, 176, 128, 64, 16, 8))
    nr = rl // tr

    def fn(i, wv, mv, vv, *gvs):
        gv = gvs[0]
        for layer in range(1, nl):
            gv = jnp.where(i >= layer * rl, gvs[layer], gv)
        mn = B1 * mv + (1.0 - B1) * gv
        vn = B2 * vv + (1.0 - B2) * gv * gv
        mh = mn / (1.0 - B1 ** STEP)
        vh = vn / (1.0 - B2 ** STEP)
        return -LR * (mh / (jnp.sqrt(vh) + AEPS) + WD * wv), mn, vn, gv

    g_ins = [(t.reshape(rl, width), (tr, width), lambda i, layer=layer: (jnp.clip(i - layer * nr, 0, nr - 1), 0), "r2")
             for layer, t in enumerate(gs)]
    res = rowwise(fn, [cols(t, tr) for t in (w2, m2, v2)] + g_ins, [out2d(rows, width, F32, tr)] * 4,
                  steps=rows // tr, name=name)
    return [r.reshape(shape) for r in res]


HB = DN_H * CH
PAIR = 2


def _split(a):
    hi = a.astype(BF16)
    return hi, (a - hi.astype(F32)).astype(BF16)


def _dot1(a, b, ca=1, cb=0):
    return _dot(a.astype(BF16), b.astype(BF16), ca, cb)


def _dot3(a, b, ca=1, cb=0):
    ah, al = _split(a)
    bh, bl = _split(b)
    return _dot(ah, bh, ca, cb) + (_dot(ah, bl, ca, cb) + _dot(al, bh, ca, cb))


def _dot01(m01, b, ca=1, cb=0):
    bh, bl = _split(b)
    m = m01.astype(BF16)
    return _dot(m, bh, ca, cb) + _dot(m, bl, ca, cb)


def _stack(x):
    return jnp.concatenate([x[:, h * DN_D:(h + 1) * DN_D] for h in range(DN_H)], axis=0)


def _unstack(x):
    return jnp.concatenate([x[h * CH:(h + 1) * CH] for h in range(DN_H)], axis=1)


def _tri_inv(a, blk, eye):
    ad = jnp.where(blk, a, 0.0)
    lo = a - ad
    a2 = _dot3(ad, ad)
    a4 = _dot3(a2, a2)
    a8 = _dot3(a4, a4)
    dgi = _dot3(_dot3(_dot3(eye - ad, eye + a2), eye + a4), eye + a8)
    n = _dot3(dgi, lo)
    return _dot3(_dot3(eye - n, eye + _dot3(n, n)), dgi)


def _dn_masks():
    row = lax.broadcasted_iota(jnp.int32, (HB, HB), 0)
    col = lax.broadcasted_iota(jnp.int32, (HB, HB), 1)
    same = (row // CH) == (col // CH)
    incl = jnp.logical_and(same, row >= col)
    strict = jnp.logical_and(same, row > col)
    upper = jnp.logical_and(same, row <= col)
    blk = (row // 16) == (col // 16)
    eye = (row == col).astype(F32)
    return incl, strict, upper, blk, eye


def _dn_chunk(qv, kv, vv, bc, br, incl, strict):
    r64 = lax.broadcasted_iota(jnp.int32, (CH, CH), 0)
    c64 = lax.broadcasted_iota(jnp.int32, (CH, CH), 1)
    dcol = _dot01((r64 >= c64).astype(F32), bc)
    drow = _dot3(br, (r64 <= c64).astype(F32))
    col = lambda m, l0: jnp.concatenate([m[:, l0 + h:l0 + h + 1] for h in range(DN_H)], axis=0)
    b_c = col(bc, 0)
    d_c = col(dcol, 4)
    d_r = jnp.concatenate([drow[4 + h:5 + h, :] for h in range(DN_H)], axis=1)
    d_last_h = [dcol[CH - 1:CH, 4 + h:5 + h] for h in range(DN_H)]
    d_last = jnp.concatenate([jnp.broadcast_to(t, (CH, 1)) for t in d_last_h], axis=0)
    q, k, v = _stack(qv), _stack(kv), _stack(vv)
    dm = jnp.where(incl, jnp.exp(jnp.where(incl, d_c - d_r, 0.0)), 0.0)
    kk = _dot1(k, k, 1, 1)
    a = jnp.where(strict, b_c * kk * dm, 0.0)
    ed = jnp.exp(d_c)
    rhs = jnp.concatenate([v * b_c, k * (b_c * ed)], axis=1)
    qk = _dot1(q, k, 1, 1) * dm
    ekd = jnp.exp(d_last - d_c)
    gl = [jnp.exp(t) for t in d_last_h]
    return q, k, v, b_c, dm, kk, a, ed, rhs, qk, ekd, gl


def dn_fwd(qkv_n, bgcol, bgrow):
    rows = qkv_n.shape[0]
    nch = rows // CH

    def body(q_ref, k_ref, v_ref, bc_ref, br_ref, o_ref, s_out, ti_out, s_scr, prep, prep_qk, prep_gl):
        n = pl.program_id(0)

        @pl.when(n == 0)
        def _():
            s_scr[...] = jnp.zeros(s_scr.shape, F32)
            prep[...] = jnp.zeros(prep.shape, F32)
            prep_qk[...] = jnp.zeros(prep_qk.shape, F32)
            prep_gl[...] = jnp.zeros(prep_gl.shape, F32)

        live = n > 0
        for c in range(PAIR):
            u, w, qd, kd = prep[c, 0], prep[c, 1], prep[c, 2], prep[c, 3]
            v_new, o_state = [], []
            for h in range(DN_H):
                rs = slice(h * CH, (h + 1) * CH)
                s = s_scr[h]
                s_out[c, h] = s
                vn = u[rs] - _dot1(w[rs], s)
                v_new.append(vn)
                o_state.append(_dot1(qd[rs], s))
                s_scr[h] = jnp.where(live, prep_gl[c, h:h + 1, 0:1] * s + _dot1(kd[rs], vn, 0, 0), s)
            o = jnp.concatenate(o_state, axis=0) + _dot1(prep_qk[c], jnp.concatenate(v_new, axis=0))
            o_ref[c * CH:(c + 1) * CH, :] = _unstack(o)

        incl, strict, _, blk, eye = _dn_masks()
        for c in range(PAIR):
            rows_c = slice(c * CH, (c + 1) * CH)
            q, k, v, b_c, dm, kk, a, ed, rhs, qk_n, ekd, gl = _dn_chunk(
                q_ref[rows_c, :], k_ref[rows_c, :], v_ref[rows_c, :], bc_ref[rows_c, :], br_ref[c], incl, strict)
            tinv = _tri_inv(a, blk, eye)
            ti_out[c] = tinv
            sol = _dot3(tinv, rhs)
            prep[c, 0] = sol[:, :DN_D]
            prep[c, 1] = sol[:, DN_D:]
            prep[c, 2] = q * ed
            prep[c, 3] = k * ekd
            prep_qk[c] = qk_n
            prep_gl[c] = jnp.concatenate([jnp.broadcast_to(t, (1, 128)) for t in gl]
                                         + [jnp.zeros((8 - DN_H, 128), F32)], axis=0)

    assert nch % PAIR == 0
    npair = nch // PAIR
    last = npair - 1
    return pl.pallas_call(
        body, name="dn_fwd", interpret=False,
        out_shape=[jax.ShapeDtypeStruct((rows, DN_DIM), F32),
                   jax.ShapeDtypeStruct((nch, DN_H, DN_D, DN_D), F32),
                   jax.ShapeDtypeStruct((nch, HB, HB), F32)],
        grid=(npair + 1,),
        in_specs=[pl.BlockSpec((PAIR * CH, DN_DIM), lambda n: (jnp.minimum(n, last), 0)),
                  pl.BlockSpec((PAIR * CH, DN_DIM), lambda n: (jnp.minimum(n, last), 1)),
                  pl.BlockSpec((PAIR * CH, DN_DIM), lambda n: (jnp.minimum(n, last), 2)),
                  pl.BlockSpec((PAIR * CH, 128), lambda n: (jnp.minimum(n, last), 0)),
                  pl.BlockSpec((PAIR, 8, CH), lambda n: (jnp.minimum(n, last), 0, 0))],
        out_specs=[pl.BlockSpec((PAIR * CH, DN_DIM), lambda n: (jnp.maximum(n - 1, 0), 0)),
                   pl.BlockSpec((PAIR, DN_H, DN_D, DN_D), lambda n: (jnp.maximum(n - 1, 0), 0, 0, 0)),
                   pl.BlockSpec((PAIR, HB, HB), lambda n: (jnp.minimum(n, last), 0, 0))],
        scratch_shapes=[pltpu.VMEM((DN_H, DN_D, DN_D), F32), pltpu.VMEM((PAIR, 4, HB, DN_D), F32),
                        pltpu.VMEM((PAIR, HB, HB), F32), pltpu.VMEM((PAIR, 8, 128), F32)],
        compiler_params=_params(("arbitrary",)),
    )(qkv_n, qkv_n, qkv_n, bgcol, bgrow)


def dn_bwd(qkv_n, bgcol, bgrow, s_all, ti_all, do):
    rows = qkv_n.shape[0]
    nch = rows // CH

    def body(q_ref, k_ref, v_ref, bc_ref, br_ref, s_ref, ti_ref, do_ref, dq_ref, dk_ref, dv_ref, dbg_ref, ds_scr):
        n = pl.program_id(0)

        @pl.when(n == 0)
        def _():
            ds_scr[...] = jnp.zeros(ds_scr.shape, F32)

        incl, strict, upper, _, _ = _dn_masks()
        q, k, v, b_c, dm, kk, a, ed, rhs, qk, ekd, gl = _dn_chunk(q_ref[...], k_ref[...], v_ref[...], bc_ref[...],
                                                                  br_ref[0], incl, strict)
        tinv = ti_ref[0]
        g_o = _stack(do_ref[...])
        sol = _dot3(tinv, rhs)
        u, w = sol[:, :DN_D], sol[:, DN_D:]
        qd, kd = q * ed, k * ekd
        rsum = lambda t: jnp.sum(t, axis=1, keepdims=True)
        rows_of = [slice(h * CH, (h + 1) * CH) for h in range(DN_H)]
        s_h = [s_ref[0, h] for h in range(DN_H)]
        ds_h = [ds_scr[h] for h in range(DN_H)]
        v_new = jnp.concatenate([u[rs] - _dot1(w[rs], s) for rs, s in zip(rows_of, s_h)], axis=0)
        dv_new = _dot1(qk, g_o, 0, 0) + jnp.concatenate([_dot1(kd[rs], t) for rs, t in zip(rows_of, ds_h)], axis=0)
        dqd = jnp.concatenate([_dot1(g_o[rs], s, 1, 1) for rs, s in zip(rows_of, s_h)], axis=0)
        dkd = jnp.concatenate([_dot1(v_new[rs], t, 1, 1) for rs, t in zip(rows_of, ds_h)], axis=0)
        for h, rs in enumerate(rows_of):
            ds_scr[h] = _dot1(qd[rs], g_o[rs], 0, 0) + gl[h] * ds_h[h] - _dot1(w[rs], dv_new[rs], 0, 0)
        dw = jnp.concatenate([-_dot1(dv_new[rs], s, 1, 1) for rs, s in zip(rows_of, s_h)], axis=0)
        dqk = _dot1(g_o, v_new, 1, 1)
        drhs = _dot3(tinv, jnp.concatenate([dv_new, dw], axis=1), 0, 0)
        da = jnp.where(strict, -_dot1(drhs, sol, 1, 1), 0.0)
        drhs_u, drhs_w = drhs[:, :DN_D], drhs[:, DN_D:]
        s2 = rsum(drhs_w * k)
        dbeta = rsum(drhs_u * v) + s2 * ed + rsum(da * kk * dm)
        dkk = da * b_c * dm
        dqkr = dqk * dm
        mmat = da * a + dqk * qk
        tmp = rsum(dkd * kd)
        dd = (s2 * b_c * ed + rsum(mmat) - _dot3(mmat, jnp.ones((HB, 128), F32), 0, 0)[:, :1] + rsum(dqd * qd) - tmp)
        rowi = lax.broadcasted_iota(jnp.int32, (CH, 1), 0)
        last = []
        for h, rs in enumerate(rows_of):
            dgl = jnp.sum(rsum(s_h[h] * ds_h[h]), axis=0, keepdims=True)
            dd_last = jnp.sum(tmp[rs], axis=0, keepdims=True) + dgl * gl[h]
            last.append(jnp.where(rowi == CH - 1, dd_last, 0.0))
        dd = dd + jnp.concatenate(last, axis=0)
        dq_ref[...] = _unstack(_dot1(dqkr, k) + dqd * ed)
        dk_ref[...] = _unstack(drhs_w * (b_c * ed) + _dot1(dkk, k) + _dot1(dkk, k, 0, 0) + _dot1(dqkr, q, 0, 0)
                               + dkd * ekd)
        dv_ref[...] = _unstack(drhs_u * b_c)
        dg = _dot01(upper.astype(F32), jnp.broadcast_to(dd, (HB, 128)))[:, :1]
        lane = lax.broadcasted_iota(jnp.int32, (CH, 128), 1)
        out = jnp.zeros((CH, 128), F32)
        for h, rs in enumerate(rows_of):
            out = out + jnp.where(lane == h, dbeta[rs], 0.0) + jnp.where(lane == 4 + h, dg[rs], 0.0)
        dbg_ref[...] = out

    rev = lambda n: nch - 1 - n
    return pl.pallas_call(
        body, name="dn_bwd", interpret=False,
        out_shape=[jax.ShapeDtypeStruct((rows, DN_DIM), F32)] * 3 + [jax.ShapeDtypeStruct((rows, 128), F32)],
        grid=(nch,),
        in_specs=[pl.BlockSpec((CH, DN_DIM), lambda n: (rev(n), 0)),
                  pl.BlockSpec((CH, DN_DIM), lambda n: (rev(n), 1)),
                  pl.BlockSpec((CH, DN_DIM), lambda n: (rev(n), 2)),
                  pl.BlockSpec((CH, 128), lambda n: (rev(n), 0)),
                  pl.BlockSpec((1, 8, CH), lambda n: (rev(n), 0, 0)),
                  pl.BlockSpec((1, DN_H, DN_D, DN_D), lambda n: (rev(n), 0, 0, 0)),
                  pl.BlockSpec((1, HB, HB), lambda n: (rev(n), 0, 0)),
                  pl.BlockSpec((CH, DN_DIM), lambda n: (rev(n), 0))],
        out_specs=[pl.BlockSpec((CH, DN_DIM), lambda n: (rev(n), 0))] * 3 + [pl.BlockSpec((CH, 128), lambda n: (rev(n), 0))],
        scratch_shapes=[pltpu.VMEM((DN_H, DN_D, DN_D), F32)],
        compiler_params=_params(("arbitrary",)),
    )(qkv_n, qkv_n, qkv_n, bgcol, bgrow, s_all, ti_all, do)


def _swa_valid(n):
    c3 = lax.broadcasted_iota(jnp.int32, (NKEY, 4 * BLK), 0)
    r = lax.broadcasted_iota(jnp.int32, (NKEY, 4 * BLK), 1) % BLK
    prev0 = N_META + BLK
    c = jnp.where(c3 < N_META, PAD + c3, jnp.where(c3 < prev0, c3 - N_META, c3 - prev0))
    lo = jnp.where(c3 < N_META, 0, jnp.where(c3 < prev0, r + 1 + jnp.where(n >= 2, 0, BLK), 0))
    hi = jnp.where(c3 < N_META, r + jnp.where(n >= 1, BLK, 0),
                   jnp.where(c3 < prev0, BLK, r - jnp.where(n >= 1, 0, BLK)))
    return jnp.logical_and(c >= lo, c <= hi)


def _swa_probs(q, kcat, valid, sink):
    s = jnp.where(valid, _dot(kcat, q, 1, 1), -1e30)
    m = jnp.maximum(jnp.max(s, axis=0, keepdims=True), sink)
    e = jnp.where(valid, jnp.exp(s - m), 0.0)
    es = jnp.exp(sink - m)
    inv = 1.0 / (jnp.sum(e, axis=0, keepdims=True) + es)
    return e * inv, es * inv


def _swa_group(q_ref, sk_ref, h):
    q4 = jnp.concatenate([q_ref[4 * h + g] for g in range(4)], axis=0)
    sink4 = jnp.concatenate([jnp.full((1, BLK), sk_ref[4 * h + g], F32) for g in range(4)], axis=1)
    return q4, sink4


def _swa_specs():
    q = pl.BlockSpec((SWA_H, BLK, SWA_D), lambda n: (0, n, 0))
    km = pl.BlockSpec((SWA_KV, N_META, SWA_D), lambda n: (0, PAD // N_META, 0))
    kp = pl.BlockSpec((SWA_KV, BLK, SWA_D), lambda n: (0, jnp.maximum(n - 1, 0), 0))
    kc = pl.BlockSpec((SWA_KV, BLK, SWA_D), lambda n: (0, n, 0))
    return [q, km, kp, kc, km, kp, kc]


def swa_fwd(qh, kh, vh, sinks):
    rows = qh.shape[1]
    nb = rows // BLK

    def body(q_ref, km, kp, kc, vm, vp, vc, sk_ref, o_ref):
        n = pl.program_id(0)
        valid = _swa_valid(n)
        outs = []
        for h in range(SWA_KV):
            kcat = jnp.concatenate([km[h], kp[h], kc[h]], axis=0)
            vcat = jnp.concatenate([vm[h], vp[h], vc[h]], axis=0)
            q4, sink4 = _swa_group(q_ref, sk_ref, h)
            p, _ = _swa_probs(q4, kcat, valid, sink4)
            o4 = _dot(p.astype(BF16), vcat, 0, 0)
            outs += [o4[g * BLK:(g + 1) * BLK] for g in range(4)]
        o_ref[...] = jnp.concatenate(outs, axis=1).astype(BF16)

    return pl.pallas_call(
        body, name="swa_fwd", interpret=False,
        out_shape=jax.ShapeDtypeStruct((rows, SWA_H * SWA_D), BF16),
        grid=(nb,),
        in_specs=_swa_specs() + [pl.BlockSpec(memory_space=pltpu.SMEM)],
        out_specs=pl.BlockSpec((BLK, SWA_H * SWA_D), lambda n: (n, 0)),
        compiler_params=_params(("parallel",)),
    )(qh, kh, kh, kh, vh, vh, vh, sinks)


def swa_bwd(qh, kh, vh, sinks, do):
    rows = qh.shape[1]
    nb = rows // BLK

    def body(q_ref, km, kp, kc, vm, vp, vc, do_ref, sk_ref, dq_ref, dk_ref, dv_ref, dsk_ref):
        n = pl.program_id(0)

        @pl.when(n == 0)
        def _():
            dk_ref[...] = jnp.zeros(dk_ref.shape, F32)
            dv_ref[...] = jnp.zeros(dv_ref.shape, F32)

        valid = _swa_valid(n)
        g_all = do_ref[...]
        rowi = lax.broadcasted_iota(jnp.int32, (SWA_H, 128), 0)
        dsk = jnp.zeros((SWA_H, 128), F32)
        pm = pl.multiple_of(jnp.maximum(n - 1, 0) * BLK, BLK)
        pc = pl.multiple_of(n * BLK, BLK)
        for h in range(SWA_KV):
            kcat = jnp.concatenate([km[h], kp[h], kc[h]], axis=0)
            vcat = jnp.concatenate([vm[h], vp[h], vc[h]], axis=0)
            q4, sink4 = _swa_group(q_ref, sk_ref, h)
            p, ps = _swa_probs(q4, kcat, valid, sink4)
            g4 = jnp.concatenate([g_all[:, (4 * h + g) * SWA_D:(4 * h + g + 1) * SWA_D] for g in range(4)], axis=0)
            dp = _dot(vcat, g4, 1, 1)
            delta = jnp.sum(p * dp, axis=0, keepdims=True)
            ds = (p * (dp - delta)).astype(BF16)
            dq4 = _dot(ds, kcat, 0, 0)
            dkc = _dot(ds, q4)
            dvc = _dot(p.astype(BF16), g4)
            t = ps * delta
            for g in range(4):
                dq_ref[4 * h + g] = dq4[g * BLK:(g + 1) * BLK]
                part = -jnp.sum(t[:, g * BLK:(g + 1) * BLK], axis=1, keepdims=True)
                dsk = dsk + jnp.where(rowi == 4 * h + g, part, 0.0)
            lanes = slice(h * SWA_D, (h + 1) * SWA_D)
            for ref, val in ((dk_ref, dkc), (dv_ref, dvc)):
                ref[PAD:BLK, lanes] += val[0:N_META]
                ref[pl.ds(pm, BLK), lanes] += val[N_META:N_META + BLK]
                ref[pl.ds(pc, BLK), lanes] += val[N_META + BLK:]
        dsk_ref[0] = dsk

    return pl.pallas_call(
        body, name="swa_bwd", interpret=False,
        out_shape=[jax.ShapeDtypeStruct((SWA_H, rows, SWA_D), F32),
                   jax.ShapeDtypeStruct((rows, SWA_KV * SWA_D), F32),
                   jax.ShapeDtypeStruct((rows, SWA_KV * SWA_D), F32),
                   jax.ShapeDtypeStruct((nb, SWA_H, 128), F32)],
        grid=(nb,),
        in_specs=_swa_specs() + [pl.BlockSpec((BLK, SWA_H * SWA_D), lambda n: (n, 0)),
                                 pl.BlockSpec(memory_space=pltpu.SMEM)],
        out_specs=[pl.BlockSpec((SWA_H, BLK, SWA_D), lambda n: (0, n, 0)),
                   pl.BlockSpec((rows, SWA_KV * SWA_D), lambda n: (0, 0)),
                   pl.BlockSpec((rows, SWA_KV * SWA_D), lambda n: (0, 0)),
                   pl.BlockSpec((1, SWA_H, 128), lambda n: (n, 0, 0))],
        compiler_params=_params(("arbitrary",)),
    )(qh, kh, kh, kh, vh, vh, vh, do, sinks)


QK_W = (SWA_H + SWA_KV) * SWA_D


def _head_mean(t):
    r = lax.broadcasted_iota(jnp.int32, (128, 128), 0) // SWA_D
    c = lax.broadcasted_iota(jnp.int32, (128, 128), 1) // SWA_D
    blk = jnp.where(r == c, 1.0 / SWA_D, 0.0).astype(BF16)
    out = []
    for i in range(t.shape[1] // 128):
        hi, lo = _split(t[:, 128 * i:128 * (i + 1)])
        out.append(_dot(hi, blk) + _dot(lo, blk))
    return jnp.concatenate(out, axis=1)


def _qk_scales(qw, kw):
    scale = SWA_D ** -0.5
    wt = jnp.concatenate([jnp.tile(qw.astype(F32) * scale, (1, SWA_H)), jnp.tile(kw.astype(F32), (1, SWA_KV))], axis=1)
    st = jnp.concatenate([jnp.full((1, SWA_H * SWA_D), scale, F32), jnp.ones((1, SWA_KV * SWA_D), F32)], axis=1)
    return wt, st


def qknorm_fwd(qkv, qw, kw):
    rows = qkv.shape[0]
    tr = _pick(rows, (384, 128))
    wt, _ = _qk_scales(qw, kw)

    def fn(i, x, w):
        xq = x[:, :QK_W]
        y = xq * lax.rsqrt(_head_mean(xq * xq) + EPS) * w
        head = lambda t, j: t[:, j * SWA_D:(j + 1) * SWA_D][None]
        qo = jnp.concatenate([head(y, j) for j in range(SWA_H)], axis=0)
        ko = jnp.concatenate([head(y, SWA_H + j) for j in range(SWA_KV)], axis=0)
        vo = jnp.concatenate([head(x, SWA_H + SWA_KV + j) for j in range(SWA_KV)], axis=0)
        return qo, ko, vo

    hm = lambda nh: ((nh, rows, SWA_D), BF16, (nh, tr, SWA_D), lambda i: (0, i, 0), "r3")
    return rowwise(fn, [cols(qkv, tr), whole(wt)], [hm(SWA_H), hm(SWA_KV), hm(SWA_KV)],
                   steps=rows // tr, name="qknorm_fwd")


def qknorm_bwd(qkv, qw, kw, dqh, dk, dv):
    rows = qkv.shape[0]
    tr = _pick(rows, (384, 128))
    wt, st = _qk_scales(qw, kw)

    def fn(i, x, w, sc, dq, dkv, dvv):
        xq = x[:, :QK_W]
        dy = jnp.concatenate([dq[j] for j in range(SWA_H)] + [dkv], axis=1)
        r = lax.rsqrt(_head_mean(xq * xq) + EPS)
        xh = xq * r
        gw = dy * w
        dx = r * (gw - xh * _head_mean(gw * xh))
        return jnp.concatenate([dx, dvv], axis=1), jnp.sum(dy * sc * xh, axis=0, keepdims=True)

    dqkv, dw = rowwise(fn, [cols(qkv, tr), whole(wt), whole(st), heads(dqh, tr), cols(dk, tr), cols(dv, tr)],
                       [out2d(rows, 1536, BF16, tr)], steps=rows // tr, name="qknorm_bwd", accs=[((1, QK_W), F32)])
    dw = dw.reshape(SWA_H + SWA_KV, SWA_D)
    return dqkv, jnp.sum(dw[:SWA_H], axis=0, keepdims=True), jnp.sum(dw[SWA_H:], axis=0, keepdims=True)


def _place():
    return lax.axis_index("x"), lax.axis_index("y"), lax.axis_index("c")


ANY = pl.BlockSpec(memory_space=pl.ANY)


def _rcopy(ssem, rsem, k, src, dst, to):
    return pltpu.make_async_remote_copy(src_ref=src, dst_ref=dst, send_sem=ssem.at[k], recv_sem=rsem.at[k],
                                        device_id=to, device_id_type=MESH)


def gather_weights(shards, small):
    n = len(shards)
    halves = [t.shape[0] // 2 for t in shards]

    def body(*refs):
        s_refs, small_ref = refs[:n], refs[n]
        o_refs, osmall = refs[n + 1:2 * n + 1], refs[2 * n + 1]
        ssem, rsem, lsem = refs[2 * n + 2:]
        x, y, c = _place()
        me = 2 * x + y
        chips = [(1 - x, y), (x, 1 - y), (1 - x, 1 - y)]

        def half(k, s, hh):
            return o_refs[k].at[s, pl.ds(hh * halves[k], halves[k]), :]

        loc = pltpu.make_async_copy(small_ref, osmall.at[me], lsem)
        loc.start()
        sends = []
        for k in range(n):
            for j, (px, py) in enumerate(chips):
                sends.append(_rcopy(ssem, rsem, 6 * k + j, s_refs[k].at[pl.ds(c * halves[k], halves[k]), :],
                                    half(k, me, c), (px, py, c)))
        for j, (px, py) in enumerate(chips):
            sends.append(_rcopy(ssem, rsem, 6 * n + j, small_ref, osmall.at[me], (px, py, c)))
        for cp in sends:
            cp.start()
        for k in range(n):
            for j, (px, py) in enumerate(chips):
                s = 2 * px + py
                _rcopy(ssem, rsem, 6 * k + j, half(k, s, c), half(k, s, c), (x, y, c)).wait_recv()
                fwd = _rcopy(ssem, rsem, 6 * k + 3 + j, half(k, s, c), half(k, s, c), (x, y, 1 - c))
                fwd.start()
                sends.append(fwd)
        for k in range(n):
            for j, (px, py) in enumerate(chips):
                s = 2 * px + py
                _rcopy(ssem, rsem, 6 * k + 3 + j, half(k, s, 1 - c), half(k, s, 1 - c), (x, y, c)).wait_recv()
        for j, (px, py) in enumerate(chips):
            s = 2 * px + py
            _rcopy(ssem, rsem, 6 * n + j, osmall.at[s], osmall.at[s], (x, y, c)).wait_recv()
        for cp in sends:
            cp.wait_send()
        loc.wait()

    res = pl.pallas_call(
        body, name="gather_weights", interpret=False,
        out_shape=[jax.ShapeDtypeStruct((4,) + t.shape, t.dtype) for t in shards]
        + [jax.ShapeDtypeStruct((4, SW_ROWS, 1024), F32)],
        in_specs=[ANY] * (n + 1), out_specs=[ANY] * (n + 1),
        scratch_shapes=[pltpu.SemaphoreType.DMA((6 * n + 3,)), pltpu.SemaphoreType.DMA((6 * n + 3,)),
                        pltpu.SemaphoreType.DMA],
    )(*shards, small)
    return res[:n], res[n]


def _handshake(peers):
    barrier = pltpu.get_barrier_semaphore()
    for peer in peers:
        pl.semaphore_signal(barrier, inc=1, device_id=peer, device_id_type=MESH)
    pl.semaphore_wait(barrier, len(peers))


def gather_weights_beside(shards):
    n = len(shards)
    halves = [t.shape[0] // 2 for t in shards]

    def body(*refs):
        s_refs, o_refs, ssem, rsem = refs[:n], refs[n:2 * n], refs[2 * n], refs[2 * n + 1]
        x, y, c = _place()
        me = 2 * x + y
        chips = [(1 - x, y), (x, 1 - y), (1 - x, 1 - y)]
        _handshake([(px, py, c) for px, py in chips] + [(x, y, 1 - c)])

        def half(k, s, hh):
            return o_refs[k].at[s, pl.ds(hh * halves[k], halves[k]), :]

        sends = []
        for k in range(n):
            for j, (px, py) in enumerate(chips):
                sends.append(_rcopy(ssem, rsem, 6 * k + j, s_refs[k].at[pl.ds(c * halves[k], halves[k]), :],
                                    half(k, me, c), (px, py, c)))
        for cp in sends:
            cp.start()
        for k in range(n):
            for j, (px, py) in enumerate(chips):
                s = 2 * px + py
                _rcopy(ssem, rsem, 6 * k + j, half(k, s, c), half(k, s, c), (x, y, c)).wait_recv()
                fwd = _rcopy(ssem, rsem, 6 * k + 3 + j, half(k, s, c), half(k, s, c), (x, y, 1 - c))
                fwd.start()
                sends.append(fwd)
        for k in range(n):
            for j, (px, py) in enumerate(chips):
                s = 2 * px + py
                _rcopy(ssem, rsem, 6 * k + 3 + j, half(k, s, 1 - c), half(k, s, 1 - c), (x, y, c)).wait_recv()
        for cp in sends:
            cp.wait_send()

    return pl.kernel(
        body, name="gather_weights_beside",
        out_type=[jax.ShapeDtypeStruct((4,) + t.shape, t.dtype) for t in shards],
        mesh=plsc.ScalarSubcoreMesh(axis_name="sequencer", num_cores=1),
        scratch_types=[pltpu.SemaphoreType.DMA((6 * n,)), pltpu.SemaphoreType.DMA((6 * n,))],
        compiler_params=pltpu.CompilerParams(collective_id=1),
    )(*shards)


def swap_halves(gs, *, name):
    n = len(gs)

    def body(*refs):
        g_refs, o_refs, ssem, rsem = refs[:n], refs[n:2 * n], refs[2 * n], refs[2 * n + 1]
        x, y, c = _place()
        cps = []
        for k in range(n):
            hk = g_refs[k].shape[1] // 2
            cps.append(_rcopy(ssem, rsem, k, g_refs[k].at[:, pl.ds((1 - c) * hk, hk), :], o_refs[k], (x, y, 1 - c)))
        for cp in cps:
            cp.start()
        for cp in cps:
            cp.wait()

    return pl.pallas_call(
        body, name=name, interpret=False,
        out_shape=[jax.ShapeDtypeStruct((4, t.shape[1] // 2, t.shape[2]), t.dtype) for t in gs],
        in_specs=[ANY] * n, out_specs=[ANY] * n,
        scratch_shapes=[pltpu.SemaphoreType.DMA((n,)), pltpu.SemaphoreType.DMA((n,))],
    )(*gs)


def _sum_rows(hk):
    return _pick(hk, (512, 352, 256, 128))


def pair_sum(g, other, c_idx, *, name):
    _, hk, width = other.shape
    tr = _sum_rows(hk)
    nbk = hk // tr

    def body(c_ref, g_ref, o_ref, out_ref):
        out_ref[...] = (g_ref[...].astype(F32) + o_ref[...].astype(F32)).astype(BF16)

    return pl.pallas_call(
        body, name=name, interpret=False,
        out_shape=jax.ShapeDtypeStruct((4, hk, width), BF16),
        grid_spec=pltpu.PrefetchScalarGridSpec(
            num_scalar_prefetch=1, grid=(4, nbk),
            in_specs=[pl.BlockSpec((1, tr, width), lambda s, i, c_ref: (s, c_ref[0] * nbk + i, 0)),
                      pl.BlockSpec((1, tr, width), lambda s, i, c_ref: (s, i, 0))],
            out_specs=pl.BlockSpec((1, tr, width), lambda s, i, c_ref: (s, i, 0))),
        compiler_params=_params(("parallel", "parallel")),
    )(c_idx, g, other)


def chip_sum(p, got, idx, *, name):
    _, hk, width = got.shape
    tr = _sum_rows(hk)
    nbk = hk // tr

    def body(idx_ref, p_ref, g_ref, out_ref):
        acc = p_ref[0].astype(F32)
        for j in range(3):
            acc = acc + g_ref[j].astype(F32)
        out_ref[0] = acc

    return pl.pallas_call(
        body, name=name, interpret=False,
        out_shape=jax.ShapeDtypeStruct((2, hk, width), F32),
        grid_spec=pltpu.PrefetchScalarGridSpec(
            num_scalar_prefetch=1, grid=(nbk,),
            in_specs=[pl.BlockSpec((1, tr, width), lambda i, idx_ref: (idx_ref[0], i, 0)),
                      pl.BlockSpec((3, tr, width), lambda i, idx_ref: (0, i, 0))],
            out_specs=pl.BlockSpec((1, tr, width), lambda i, idx_ref: (idx_ref[1], i, 0))),
        compiler_params=_params(("parallel",)),
    )(idx, p, got)


def join_halves(qs):
    n = len(qs)

    def body(*refs):
        q_refs, o_refs, ssem, rsem = refs[:n], refs[n:2 * n], refs[2 * n], refs[2 * n + 1]
        x, y, c = _place()
        cps = [_rcopy(ssem, rsem, k, q_refs[k].at[c], o_refs[k].at[c], (x, y, 1 - c)) for k in range(n)]
        for cp in cps:
            cp.start()
        for k in range(n):
            _rcopy(ssem, rsem, k, q_refs[k].at[c], o_refs[k].at[1 - c], (x, y, 1 - c)).wait_recv()
        for cp in cps:
            cp.wait_send()

    return pl.pallas_call(
        body, name="join_halves", interpret=False,
        out_shape=[jax.ShapeDtypeStruct(t.shape, t.dtype) for t in qs],
        in_specs=[ANY] * n, out_specs=[ANY] * n, input_output_aliases={k: k for k in range(n)},
        scratch_shapes=[pltpu.SemaphoreType.DMA((n,)), pltpu.SemaphoreType.DMA((n,))],
    )(*qs)


def scatter_chips_beside(ps, cid, name):
    n = len(ps)

    def body(*refs):
        p_refs, o_refs, ssem, rsem = refs[:n], refs[n:2 * n], refs[2 * n], refs[2 * n + 1]
        x, y, c = _place()
        chips = [(1 - x, y), (x, 1 - y), (1 - x, 1 - y)]
        _handshake([(px, py, c) for px, py in chips])
        cps = [_rcopy(ssem, rsem, 3 * k + j, p_refs[k].at[2 * px + py], o_refs[k].at[j], (px, py, c))
               for k in range(n) for j, (px, py) in enumerate(chips)]
        for cp in cps:
            cp.start()
        for cp in cps:
            cp.wait()

    return pl.kernel(
        body, name=name, out_type=[jax.ShapeDtypeStruct((3,) + t.shape[1:], t.dtype) for t in ps],
        mesh=plsc.ScalarSubcoreMesh(axis_name="sequencer", num_cores=1),
        scratch_types=[pltpu.SemaphoreType.DMA((3 * n,)), pltpu.SemaphoreType.DMA((3 * n,))],
        compiler_params=pltpu.CompilerParams(collective_id=cid),
    )(*ps)


def reduce_begin(gs, names, c_idx, cid, tag):
    others = swap_halves(gs, name=f"swap_halves_{tag}")
    pairs = [pair_sum(g, o, c_idx, name=f"pair_sum_{nm}") for g, o, nm in zip(gs, others, names)]
    return pairs, scatter_chips_beside(pairs, cid, f"scatter_chips_{tag}")


def reduce_end(pairs, gots, names, idx):
    mine = [chip_sum(p, g, idx, name=f"chip_sum_{nm}") for p, g, nm in zip(pairs, gots, names)]
    return [q.reshape(2 * q.shape[1], q.shape[2]) for q in join_halves(mine)]


def gather_small(v):
    def body(v_ref, o_ref, ssem, rsem, lsem):
        x, y, c = _place()
        peers = []
        for k in range(1, 8):
            fx, fy, fc = (k >> 2) & 1, (k >> 1) & 1, k & 1
            peers.append((1 - x if fx else x, 1 - y if fy else y, 1 - c if fc else c))
        _handshake(peers)
        loc = pltpu.make_async_copy(v_ref, o_ref.at[4 * x + 2 * y + c], lsem)
        loc.start()
        cps = []
        for k, (px, py, pc) in enumerate(peers):
            cps.append((pltpu.make_async_remote_copy(
                src_ref=v_ref, dst_ref=o_ref.at[4 * x + 2 * y + c], send_sem=ssem.at[k], recv_sem=rsem.at[k],
                device_id=(px, py, pc), device_id_type=MESH), 4 * px + 2 * py + pc))
        for cp, _ in cps:
            cp.start()
        for k, (cp, peer) in enumerate(cps):
            pltpu.make_async_remote_copy(
                src_ref=v_ref, dst_ref=o_ref.at[peer], send_sem=ssem.at[k], recv_sem=rsem.at[k],
                device_id=(x, y, c), device_id_type=MESH).wait_recv()
        for cp, _ in cps:
            cp.wait_send()
        loc.wait()

    return pl.kernel(
        body, name="gather_small", out_type=jax.ShapeDtypeStruct((8, SV_ROWS, 1024), F32),
        mesh=plsc.ScalarSubcoreMesh(axis_name="sequencer", num_cores=1),
        scratch_types=[pltpu.SemaphoreType.DMA((7,)), pltpu.SemaphoreType.DMA((7,)), pltpu.SemaphoreType.DMA],
        compiler_params=pltpu.CompilerParams(collective_id=6),
    )(v)


def sum_slots(a):
    def fn(i, t):
        acc = t[0]
        for k in range(1, 8):
            acc = acc + t[k]
        return acc

    return rowwise(fn, [whole(a)], [((SV_ROWS, 1024), F32, (SV_ROWS, 1024), lambda i: (0, 0), "w")], steps=1,
                   name="sum_slots")[0]


def _head_rms(x, nw):
    xs, rs = [], []
    for h in range(DN_H):
        xh = x[:, h * DN_D:(h + 1) * DN_D]
        r = lax.rsqrt(jnp.mean(xh * xh, axis=1, keepdims=True) + EPS)
        xs.append(xh * r)
        rs.append(r)
    return xs, rs


def bg_fwd(p, alog, dtb):
    rows = p.shape[0]
    tr = _pick(rows, (384, 128))

    def fn(i, x, al, dt):
        lane = lax.broadcasted_iota(jnp.int32, x.shape, 1)
        row = i + lax.broadcasted_iota(jnp.int32, x.shape, 0)
        g = -jnp.exp(al) * _softplus(x + dt)
        out = jnp.where(lane < 4, _sigmoid(x), jnp.where(lane < 8, g, 0.0))
        return jnp.where(row >= PAD, out, 0.0)

    return rowwise(fn, [cols(p, tr, 128, BG0 // 128), whole(alog), whole(dtb)], [out2d(rows, 128, F32, tr)],
                   steps=rows // tr, name="bg_fwd")[0]


def bg_bwd(p, alog, dtb, dbg):
    rows = p.shape[0]
    tr = _pick(rows, (384, 128))

    def fn(i, x, al, dt, g_in):
        lane = lax.broadcasted_iota(jnp.int32, x.shape, 1)
        row = i + lax.broadcasted_iota(jnp.int32, x.shape, 0)
        live = row >= PAD
        is_b = jnp.logical_and(live, lane < 4)
        is_g = jnp.logical_and(live, jnp.logical_and(lane >= 4, lane < 8))
        beta = _sigmoid(x)
        ea = jnp.exp(al)
        g = -ea * _softplus(x + dt)
        dalpha = jnp.where(is_g, g_in * (-ea) * _sigmoid(x + dt), 0.0)
        dx = jnp.where(is_b, g_in * beta * (1.0 - beta), dalpha)
        dal = jnp.sum(jnp.where(is_g, g_in * g, 0.0), axis=0, keepdims=True)
        return jnp.concatenate([dx, jnp.zeros(x.shape, F32)], axis=1), dal, jnp.sum(dalpha, axis=0, keepdims=True)

    return rowwise(fn, [cols(p, tr, 128, BG0 // 128), whole(alog), whole(dtb), cols(dbg, tr)],
                   [out2d(rows, 256, BF16, tr)], steps=rows // tr, name="bg_bwd",
                   accs=[((1, 128), F32), ((1, 128), F32)])


def dn_qkv_post(j, y):
    xs = _silu(y)
    sc = jnp.where(j == 0, DN_D ** -0.5, 1.0)
    outs = []
    for h in range(DN_H):
        xh = xs[:, h * DN_D:(h + 1) * DN_D]
        r = lax.rsqrt(jnp.sum(xh * xh, axis=1, keepdims=True) + EPS)
        outs.append(jnp.where(j < 2, xh * r * sc, xh))
    return jnp.concatenate(outs, axis=1), y


def dn_qkv_bwd(cq, dq, dk, dv):
    rows = cq.shape[0]
    tr = _pick(rows, (384, 128))

    def fn(i, c0, c1, c2, g0, g1, g2):
        pieces = []
        for kind, (cv, g) in enumerate(((c0, g0), (c1, g1), (c2, g2))):
            xs = _silu(cv)
            if kind < 2:
                sc = DN_D ** -0.5 if kind == 0 else 1.0
                ds = []
                for h in range(DN_H):
                    sl = slice(h * DN_D, (h + 1) * DN_D)
                    xh, gh = xs[:, sl], g[:, sl]
                    r = lax.rsqrt(jnp.sum(xh * xh, axis=1, keepdims=True) + EPS)
                    xn = xh * r
                    ds.append(sc * r * (gh - xn * jnp.sum(gh * xn, axis=1, keepdims=True)))
                dxs = jnp.concatenate(ds, axis=1)
            else:
                dxs = g
            pieces.append(dxs * _dsilu(cv))
        return jnp.concatenate(pieces, axis=1)

    ins = [cols(cq, tr, DN_DIM, k) for k in range(3)] + [cols(t, tr) for t in (dq, dk, dv)]
    return rowwise(fn, ins, [out2d(rows, 3 * DN_DIM, F32, tr)], steps=rows // tr, name="dn_qkv_bwd")[0]


def dn_out_fwd(o, p, nw):
    rows = o.shape[0]
    tr = _pick(rows, (384, 128))

    def fn(i, ov, z, w):
        xs, _ = _head_rms(ov, w)
        return jnp.concatenate(xs, axis=1) * jnp.concatenate([w] * DN_H, axis=1) * _silu(z)

    return rowwise(fn, [cols(o, tr), cols(p, tr, DN_DIM, 6), whole(nw)], [out2d(rows, DN_DIM, BF16, tr)],
                   steps=rows // tr, name="dn_out_fwd")[0]


def dn_out_bwd(o, p, nw, dymix):
    rows = o.shape[0]
    tr = _pick(rows, (384, 128))

    def fn(i, ov, z, w, dy):
        xs, rs = _head_rms(ov, w)
        sz = _silu(z)
        dn = dy * sz
        dos, dw = [], jnp.zeros((1, DN_D), F32)
        for h in range(DN_H):
            sl = slice(h * DN_D, (h + 1) * DN_D)
            gw = dn[:, sl] * w
            dos.append(rs[h] * (gw - xs[h] * jnp.mean(gw * xs[h], axis=1, keepdims=True)))
            dw = dw + jnp.sum(dn[:, sl] * xs[h], axis=0, keepdims=True)
        n = jnp.concatenate(xs, axis=1) * jnp.concatenate([w] * DN_H, axis=1)
        return jnp.concatenate(dos, axis=1), dy * n * _dsilu(z), dw

    return rowwise(fn, [cols(o, tr), cols(p, tr, DN_DIM, 6), whole(nw), cols(dymix, tr, DN_DIM, 1)],
                   [out2d(rows, DN_DIM, F32, tr), out2d(rows, DN_DIM, BF16, tr)], steps=rows // tr,
                   name="dn_out_bwd", accs=[((1, DN_D), F32)])


def conv_a_pre_bwd(dymix, cv, p):
    rows = cv.shape[0]
    tr = _pick(rows, (384, 128))

    def fn(i, dy, c, go):
        return dy * c, dy * go

    return rowwise(fn, [cols(dymix, tr, D_CONV, 0), cols(cv, tr), cols(p, tr, D_CONV, 1)],
                   [out2d(rows, D_CONV, BF16, tr), out2d(rows, D_CONV, F32, tr)], steps=rows // tr,
                   name="conv_a_pre_bwd")


def _act_bwd_epi(row0, da, gc, val):
    c, val = gc.astype(F32), val.astype(F32)
    return da * _silu(c), da * val * _dsilu(c)


def _rows8(w):
    return jnp.pad(w.astype(F32), ((0, 8 - w.shape[0]), (0, 0)))


def _lanes(v, at):
    return jnp.pad(v.astype(F32), (at, 128 - at - v.shape[0]))[None]


def add_norm(a, w, h, next_nw, *, name):
    if next_nw is None:
        return mm(a, w, add=h, name=name), None
    return mm(a, w, name=name, epi=_add_norm_epi, epi_ins=[(h, lambda j: 0)], epi_consts=[next_nw],
              epi_outs=[F32, BF16])


def _add_norm_epi(row0, t, h, nw):
    x = t + h
    return x, x * lax.rsqrt(jnp.mean(x * x, axis=1, keepdims=True) + EPS) * nw


def ffn_up_conv(hn, w_up, cw8, *, name):
    rows = hn.shape[0]
    tn = w_up.shape[2]
    tm = _pick(rows, (384, 128))
    nr = rows // tm

    def body(x_ref, wg_ref, wv_ref, w_ref, ug_ref, uv_ref, gc_ref, a_ref, carry, scr):
        i = pl.program_id(1)
        x = x_ref[...]
        gate = _dot(x, wg_ref[...])
        val = _dot(x, wv_ref[...])
        ug_ref[...] = gate.astype(BF16)
        uv_ref[...] = val.astype(BF16)
        scr[0:8, :] = jnp.where(i > 0, carry[...], 0.0)
        scr[8:8 + tm, :] = gate
        carry[...] = gate[tm - 8:tm]
        y = jnp.zeros((tm, tn), F32)
        for q in range(3):
            sh = 2 - q
            y = y + w_ref[q:q + 1, :] * scr[8 - sh:8 - sh + tm, :]
        gc_ref[...] = y.astype(BF16)
        a_ref[...] = (_silu(y) * val).astype(BF16)

    half = pl.BlockSpec((tm, tn), lambda j, i: (i, j))
    return pl.pallas_call(
        body, name=name, interpret=False,
        out_shape=[jax.ShapeDtypeStruct((rows, D_FF), BF16)] * 4,
        grid=(D_FF // tn, nr),
        in_specs=[pl.BlockSpec((tm, D), lambda j, i: (i, 0)),
                  pl.BlockSpec((None, D, tn), lambda j, i: (j, 0, 0)),
                  pl.BlockSpec((None, D, tn), lambda j, i: (j + D_FF // tn, 0, 0)),
                  pl.BlockSpec((8, tn), lambda j, i: (0, j))],
        out_specs=[half] * 4,
        scratch_shapes=[pltpu.VMEM((8, tn), F32), pltpu.VMEM((tm + 8, tn), F32)],
        compiler_params=_params(("arbitrary", "arbitrary")),
    )(hn, w_up, w_up, cw8)


def ffn_fwd(h, hn, w_up, cw8, w_down, tag, next_nw):
    ug, uv, gc, a = ffn_up_conv(hn, w_up, cw8, name=f"ffn{tag}_up")
    out, hn_next = add_norm(a, w_down, h, next_nw, name=f"ffn{tag}_down")
    return out, hn_next, (hn, ug, uv, a, gc)


def ffn_bwd(h, nw, w_up, cw8, w_down, saved, dh, tag):
    hn, ug, uv, a, gc = saved
    rows = h.shape[0]
    tr = _pick(rows, (384, 128))
    du_half, dgc = mm(dh, w_down, tb=True, name=f"ffn{tag}_down_dx", epi=_act_bwd_epi,
                      epi_ins=[(gc, lambda j: j), (uv, lambda j: j)],
                      epi_outs=[(BF16, 2 * D_FF, lambda j: 2 + j), F32])
    d_w_down = mm(a, dh, ta=True, out_dtype=BF16, name=f"ffn{tag}_down_dw")
    du, d_cw = conv_bwd([(ug, 0)], cw8, 3, dgc, rows=rows, c=D_FF, tc=1408, tr=tr, name=f"ffn{tag}_conv_bwd",
                        post=lambda dx: dx, outs=[BF16], into=(du_half, 0))
    dh_new, d_nw = dx_rms_bwd(du, w_up, h, nw, dh, name=f"ffn{tag}_up_dx", b_chip=True)
    d_w_up = mm(hn, du, ta=True, out_dtype=BF16, out_chip=True, name=f"ffn{tag}_up_dw")
    return dh_new, d_nw, d_w_up, d_cw, d_w_down


def mixer_fwd(h, nw, w_in, ca8, dc8, alog, dtb, dnw, w_out, tie=None, next_nw=None):
    rows = h.shape[0]
    tr = _pick(rows, (384, 128))
    hn = rms_fwd(h, nw, name="mix_norm")
    p = mm(hn, w_in, name="mix_in")
    y_a, cv = conv_fwd([(p, 0), (p, 2)], ca8, 3, rows=rows, c=D_CONV, tc=D_CONV, tr=tr, name="conv_a",
                       pre=lambda gi, ah: gi * ah, post=lambda j, y, go: (go * y, y), extras=[(p, 1)],
                       outs=[BF16, F32])
    qkv_n, cq = conv_fwd([(p, 3)], dc8, 4, rows=rows, c=3 * DN_DIM, tc=DN_DIM, tr=tr, name="dn_conv",
                         post=dn_qkv_post, outs=[F32, F32], strip=tr)
    bgcol = bg_fwd(p, alog, dtb)
    if tie is not None:
        bgcol = tie(bgcol)
    bgrow = bgcol[:, :8].reshape(rows // CH, CH, 8).transpose(0, 2, 1)
    o, s_all, ti_all = dn_fwd(qkv_n, bgcol, bgrow)
    y_b = dn_out_fwd(o, p, dnw)
    ymix = jnp.concatenate([y_a, y_b], axis=1)
    out, hn_next = add_norm(ymix, w_out, h, next_nw, name="mix_out")
    return out, hn_next, (hn, p, cv, qkv_n, cq, bgcol, bgrow, o, s_all, ti_all, ymix)


def mixer_bwd(h, nw, w_in, ca8, dc8, alog, dtb, dnw, w_out, saved, dh):
    hn, p, cv, qkv_n, cq, bgcol, bgrow, o, s_all, ti_all, ymix = saved
    rows = h.shape[0]
    tr = _pick(rows, (384, 128))
    dymix = mm(dh, w_out, tb=True, name="mix_out_dx")
    d_w_out = mm(ymix, dh, ta=True, out_dtype=BF16, name="mix_out_dw")
    do, dz, d_dnw = dn_out_bwd(o, p, dnw, dymix)
    dq, dk, dv, dbg = dn_bwd(qkv_n, bgcol, bgrow, s_all, ti_all, do)
    dbg_p, d_alog, d_dtb = bg_bwd(p, alog, dtb, dbg)
    dcq = dn_qkv_bwd(cq, dq, dk, dv)
    dqkv, d_dc = conv_bwd([(p, 3)], dc8, 4, dcq, rows=rows, c=3 * DN_DIM, tc=DN_DIM, tr=tr, name="dn_conv_bwd",
                          post=lambda dx: dx, outs=[BF16])
    dgo, dcv = conv_a_pre_bwd(dymix, cv, p)
    dgi, dah, d_ca = conv_bwd([(p, 0), (p, 2)], ca8, 3, dcv, rows=rows, c=D_CONV, tc=D_CONV, tr=tr,
                              name="conv_a_bwd", pre=lambda gi, ah: gi * ah,
                              post=lambda dm, gi, ah: (dm * ah, dm * gi), extras=[(p, 0), (p, 2)], outs=[BF16, BF16])
    dp = jnp.concatenate([dgi, dgo, dah, dqkv, dz, dbg_p], axis=1)
    dh_new, d_nw = dx_rms_bwd(dp, w_in, h, nw, dh, name="mix_in_dx")
    d_w_in = mm(hn, dp, ta=True, out_dtype=BF16, name="mix_in_dw")
    return dh_new, d_nw, d_w_in, d_ca, d_dc, d_alog, d_dtb, d_dnw, d_w_out


def swa_layer_fwd(h, hn, wqkv, qw, kw, sinks, wo, next_nw):
    qkv = mm(hn, wqkv, name="swa_qkv")
    qh, kh, vh = qknorm_fwd(qkv, qw, kw)
    att = swa_fwd(qh, kh, vh, sinks)
    out, hn_next = add_norm(att, wo, h, next_nw, name="swa_out")
    return out, hn_next, (hn, qkv, qh, kh, vh, att)


def swa_layer_bwd(h, nw, wqkv, qw, kw, sinks, wo, saved, dh):
    hn, qkv, qh, kh, vh, att = saved
    datt = mm(dh, wo, tb=True, out_dtype=BF16, name="swa_out_dx")
    d_wo = mm(att, dh, ta=True, out_dtype=BF16, name="swa_out_dw")
    dqh, dkh, dvh, dsk = swa_bwd(qh, kh, vh, sinks, datt)
    dqkv, d_qw, d_kw = qknorm_bwd(qkv, qw, kw, dqh, dkh, dvh)
    dh_new, d_nw = dx_rms_bwd(dqkv, wqkv, h, nw, dh, name="swa_qkv_dx")
    d_wqkv = mm(hn, dqkv, ta=True, out_dtype=BF16, name="swa_qkv_dw")
    d_sinks = jnp.sum(dsk[:, :, 0], axis=0)
    return dh_new, d_nw, d_wqkv, d_qw, d_kw, d_sinks, d_wo


BIG = ("mix_w_in", "mix_w_out", "swa_wq", "swa_wk", "swa_wv", "swa_wo", "ffn_w_up", "ffn_w_down")


def _flat_pad(parts, rows):
    v = jnp.concatenate([t.astype(F32).reshape(-1) for t in parts])
    return jnp.pad(v, (0, rows * 1024 - v.shape[0])).reshape(rows, 1024)


def _split_flat(flat, shapes):
    v = flat.reshape(-1)
    out, o = [], 0
    for s in shapes:
        n = 1
        for d_ in s:
            n *= d_
        out.append(v[o:o + n].reshape(s))
        o += n
    return out


def local_step(x0, target0, meta_full, anw, fnw, w_in, ca8, dc8, alog, dtb, dnw, w_out, qw, kw, sinks, fc8, late,
               begin=None, tie=None):
    begin = begin or (lambda tag, names, grads: None)
    h0 = jnp.concatenate([jnp.zeros((PAD, D), F32), meta_full, x0], axis=0)
    h1, hn1, s_mix = mixer_fwd(h0, anw[0], w_in, ca8, dc8, alog, dtb, dnw, w_out, tie, fnw[0])
    wqkv, wo, w_up, w_down = late()
    h2, hn2, s_f0 = ffn_fwd(h1, hn1, w_up[0], fc8[0], w_down[0], 0, anw[1])
    h3, hn3, s_swa = swa_layer_fwd(h2, hn2, wqkv, qw, kw, sinks, wo, fnw[1])
    h4, _, s_f1 = ffn_fwd(h3, hn3, w_up[1], fc8[1], w_down[1], 1, None)
    dh, loss_l = loss_grad(h4, target0)
    dh, d_fnw1, d_up1, d_fc1, d_down1 = ffn_bwd(h3, fnw[1], w_up[1], fc8[1], w_down[1], s_f1, dh, 1)
    begin("ffn1", ("up1", "down1"), [d_up1, d_down1.reshape(4, 704, D)])
    dh, d_anw1, d_wqkv, d_qw, d_kw, d_sinks, d_wo = swa_layer_bwd(h2, anw[1], wqkv, qw, kw, sinks, wo, s_swa, dh)
    begin("swa", ("wq", "wk", "wv", "wo"),
          [d_wqkv[:, :D].reshape(4, 256, D), d_wqkv[:, D:D + 256].reshape(4, 256, 256),
           d_wqkv[:, D + 256:].reshape(4, 256, 256), d_wo.reshape(4, 256, D)])
    dh, d_fnw0, d_up0, d_fc0, d_down0 = ffn_bwd(h1, fnw[0], w_up[0], fc8[0], w_down[0], s_f0, dh, 0)
    begin("ffn0", ("up0", "down0"), [d_up0, d_down0.reshape(4, 704, D)])
    dh, d_anw0, d_w_in, d_ca, d_dc, d_alog, d_dtb, d_dnw, d_w_out = mixer_bwd(
        h0, anw[0], w_in, ca8, dc8, alog, dtb, dnw, w_out, s_mix, dh)
    begin("mix", ("w_in", "w_out"),
          [d_w_in[:, :IN_DIM].reshape(D, 4, 898).transpose(1, 0, 2), d_w_out.reshape(4, 256, D)])
    return (dh, loss_l, d_anw0, d_anw1, d_fnw0, d_fnw1, d_w_in, d_ca, d_dc, d_alog, d_dtb, d_dnw, d_w_out, d_wqkv,
            d_qw, d_kw, d_sinks, d_wo, d_up0, d_up1, d_fc0, d_fc1, d_down0, d_down1)


def kernel(x, meta_tokens, attn_norm_w, ffn_norm_w, mix_w_in, conv_a_w, dn_conv_w, dn_a_log, dn_dt_bias, dn_norm_w, mix_w_out, swa_wq, swa_wk, swa_wv, swa_q_norm_w, swa_k_norm_w, swa_sinks, swa_wo, ffn_w_up, ffn_conv_w, ffn_w_down, loss_target, m_meta_tokens, m_attn_norm_w, m_ffn_norm_w, m_mix_w_in, m_conv_a_w, m_dn_conv_w, m_dn_a_log, m_dn_dt_bias, m_dn_norm_w, m_mix_w_out, m_swa_wq, m_swa_wk, m_swa_wv, m_swa_q_norm_w, m_swa_k_norm_w, m_swa_sinks, m_swa_wo, m_ffn_w_up, m_ffn_conv_w, m_ffn_w_down, v_meta_tokens, v_attn_norm_w, v_ffn_norm_w, v_mix_w_in, v_conv_a_w, v_dn_conv_w, v_dn_a_log, v_dn_dt_bias, v_dn_norm_w, v_mix_w_out, v_swa_wq, v_swa_wk, v_swa_wv, v_swa_q_norm_w, v_swa_k_norm_w, v_swa_sinks, v_swa_wo, v_ffn_w_up, v_ffn_conv_w, v_ffn_w_down):
    ix, iy, ic = lax.axis_index("x"), lax.axis_index("y"), lax.axis_index("c")
    chip = 2 * ix + iy
    seq = x.shape[1]
    rows = HEAD0 + seq

    small_sharded = (conv_a_w, dn_conv_w, ffn_conv_w, meta_tokens)
    up_b, down_b = ffn_w_up.astype(BF16), ffn_w_down.astype(BF16)
    own = [mix_w_in[0].astype(BF16), mix_w_out[0].astype(BF16), swa_wq[0].astype(BF16), swa_wk[0].astype(BF16),
           swa_wv[0].astype(BF16), swa_wo[0].astype(BF16), up_b[0], up_b[1], down_b[0], down_b[1]]
    fill = lambda gathered, mine: [lax.dynamic_update_slice_in_dim(g, t[None], chip, axis=0)
                                   for g, t in zip(gathered, mine)]
    first, g_small = gather_weights(own[:2], _flat_pad(small_sharded, SW_ROWS))
    g_in, g_out = fill(first, own[:2])
    w_in = jnp.pad(g_in.transpose(1, 0, 2).reshape(D, IN_DIM), ((0, 0), (0, P_W - IN_DIM)))
    w_out = g_out.reshape(D, D)
    rest = {}

    def tie(t):
        t, *mine = lax.optimization_barrier((t, *own[2:]))
        rest["w"] = fill(gather_weights_beside(mine), mine)
        return t

    def late():
        g_q, g_k, g_v, g_o, g_up0, g_up1, g_dn0, g_dn1 = rest["w"]
        wqkv = jnp.concatenate([g_q.reshape(D, D), g_k.reshape(D, 256), g_v.reshape(D, 256)], axis=1)
        return wqkv, g_o.reshape(D, D), [g_up0, g_up1], [g_dn0.reshape(D_FF, D), g_dn1.reshape(D_FF, D)]

    gs = g_small.reshape(4, -1)
    ca_full = gs[:, 0:384].reshape(4, 3, 128).transpose(1, 0, 2).reshape(3, D_CONV)
    dc_full = gs[:, 384:1920].reshape(4, 4, 384).transpose(1, 0, 2).reshape(4, 3 * DN_DIM)
    fc_full = gs[:, 1920:6144].reshape(4, 2, 3, 704).transpose(1, 2, 0, 3).reshape(2, 3, D_FF)
    meta_full = gs[:, 6144:10240].reshape(4, N_META, 256).transpose(1, 0, 2).reshape(N_META, D)
    ca8, dc8 = _rows8(ca_full), _rows8(dc_full)
    fc8 = [_rows8(fc_full[0]), _rows8(fc_full[1])]
    alog, dtb = _lanes(dn_a_log[0], 4), _lanes(dn_dt_bias[0], 4)
    dnw = dn_norm_w.astype(F32)
    qw, kw = swa_q_norm_w.astype(F32), swa_k_norm_w.astype(F32)
    sinks = swa_sinks[0].astype(F32)
    anw = [attn_norm_w[0:1], attn_norm_w[1:2]]
    fnw = [ffn_norm_w[0:1], ffn_norm_w[1:2]]

    c_idx = jnp.reshape(ic, (1,)).astype(jnp.int32)
    chip_idx = jnp.stack([chip, ic]).astype(jnp.int32)
    begun = []

    def begin(tag, names, grads):
        pairs, gots = reduce_begin(grads, names, c_idx, 2 + len(begun), tag)
        begun.append((names, pairs, gots))

    (dh, loss_l, d_anw0, d_anw1, d_fnw0, d_fnw1, d_w_in, d_ca, d_dc, d_alog, d_dtb, d_dnw, d_w_out, d_wqkv, d_qw,
     d_kw, d_sinks, d_wo, d_up0, d_up1, d_fc0, d_fc1, d_down0, d_down1) = local_step(
        x[0], loss_target[0], meta_full, anw, fnw, w_in, ca8, dc8, alog, dtb, dnw, w_out, qw, kw, sinks, fc8, late,
        begin, tie)
    grad_x = dh[HEAD0:][None]

    small_parts = [jnp.concatenate([d_anw0, d_anw1], axis=0), jnp.concatenate([d_fnw0, d_fnw1], axis=0),
                   d_alog[0, 4:8], d_dtb[0, 4:8], d_dnw, d_qw, d_kw, d_sinks,
                   d_ca[:3], d_dc[:4], jnp.stack([d_fc0[:3], d_fc1[:3]]), dh[PAD:HEAD0], loss_l[0, 0:1]]
    small_shapes = [(2, D), (2, D), (1, 4), (1, 4), (1, DN_D), (1, SWA_D), (1, SWA_D), (1, SWA_H),
                    (1, 3, D_CONV), (1, 4, 3 * DN_DIM), (2, 3, D_FF), (N_META, D), ()]
    gathered_small = gather_small(_flat_pad(small_parts, SV_ROWS))

    red_big = {}
    for part in (begun[:-1], begun[-1:]):
        part_names = [n for names, _, _ in part for n in names]
        red_big.update(zip(part_names, reduce_end([p for _, ps, _ in part for p in ps],
                                                  [g for _, _, gs_ in part for g in gs_], part_names, chip_idx)))
    g_w_in, g_w_out, g_wq, g_wk, g_wv, g_wo, g_up0, g_up1, g_dn0, g_dn1 = [
        red_big[n] for n in ("w_in", "w_out", "wq", "wk", "wv", "wo", "up0", "up1", "down0", "down1")]

    grads = dict(mix_w_in=g_w_in, mix_w_out=g_w_out, swa_wq=g_wq, swa_wk=g_wk, swa_wv=g_wv, swa_wo=g_wo,
                 ffn_w_up=[g_up0, g_up1], ffn_w_down=[g_dn0, g_dn1])
    weights = dict(meta_tokens=meta_tokens, attn_norm_w=attn_norm_w, ffn_norm_w=ffn_norm_w, mix_w_in=mix_w_in,
                   conv_a_w=conv_a_w, dn_conv_w=dn_conv_w, dn_a_log=dn_a_log, dn_dt_bias=dn_dt_bias,
                   dn_norm_w=dn_norm_w, mix_w_out=mix_w_out, swa_wq=swa_wq, swa_wk=swa_wk, swa_wv=swa_wv,
                   swa_q_norm_w=swa_q_norm_w, swa_k_norm_w=swa_k_norm_w, swa_sinks=swa_sinks, swa_wo=swa_wo,
                   ffn_w_up=ffn_w_up, ffn_conv_w=ffn_conv_w, ffn_w_down=ffn_w_down)
    m_in = dict(meta_tokens=m_meta_tokens, attn_norm_w=m_attn_norm_w, ffn_norm_w=m_ffn_norm_w, mix_w_in=m_mix_w_in,
                conv_a_w=m_conv_a_w, dn_conv_w=m_dn_conv_w, dn_a_log=m_dn_a_log, dn_dt_bias=m_dn_dt_bias,
                dn_norm_w=m_dn_norm_w, mix_w_out=m_mix_w_out, swa_wq=m_swa_wq, swa_wk=m_swa_wk, swa_wv=m_swa_wv,
                swa_q_norm_w=m_swa_q_norm_w, swa_k_norm_w=m_swa_k_norm_w, swa_sinks=m_swa_sinks, swa_wo=m_swa_wo,
                ffn_w_up=m_ffn_w_up, ffn_conv_w=m_ffn_conv_w, ffn_w_down=m_ffn_w_down)
    v_in = dict(meta_tokens=v_meta_tokens, attn_norm_w=v_attn_norm_w, ffn_norm_w=v_ffn_norm_w, mix_w_in=v_mix_w_in,
                conv_a_w=v_conv_a_w, dn_conv_w=v_dn_conv_w, dn_a_log=v_dn_a_log, dn_dt_bias=v_dn_dt_bias,
                dn_norm_w=v_dn_norm_w, mix_w_out=v_mix_w_out, swa_wq=v_swa_wq, swa_wk=v_swa_wk, swa_wv=v_swa_wv,
                swa_q_norm_w=v_swa_q_norm_w, swa_k_norm_w=v_swa_k_norm_w, swa_sinks=v_swa_sinks, swa_wo=v_swa_wo,
                ffn_w_up=v_ffn_w_up, ffn_conv_w=v_ffn_conv_w, ffn_w_down=v_ffn_w_down)
    names = list(weights)
    small = [n for n in names if n not in BIG]
    delta, new_m, new_v = {}, {}, {}
    for n in BIG:
        delta[n], new_m[n], new_v[n], grads[n] = adamw(weights[n], grads[n], m_in[n], v_in[n], name=f"adamw_{n}")
    gathered_small, _ = lax.optimization_barrier((gathered_small, new_v["ffn_w_down"]))
    (g_anw, g_fnw, g_alog, g_dtb, g_dnw, g_qw, g_kw, g_sinks, g_ca_f, g_dc_f, g_fc_f, g_meta_f,
     loss) = _split_flat(sum_slots(gathered_small), small_shapes)
    grads.update(meta_tokens=lax.dynamic_slice_in_dim(g_meta_f, chip * 256, 256, axis=1), attn_norm_w=g_anw,
                 ffn_norm_w=g_fnw, conv_a_w=lax.dynamic_slice_in_dim(g_ca_f, chip * 128, 128, axis=2),
                 dn_conv_w=lax.dynamic_slice_in_dim(g_dc_f, chip * 384, 384, axis=2), dn_a_log=g_alog,
                 dn_dt_bias=g_dtb, dn_norm_w=g_dnw, swa_q_norm_w=g_qw, swa_k_norm_w=g_kw, swa_sinks=g_sinks,
                 ffn_conv_w=lax.dynamic_slice_in_dim(g_fc_f, chip * 704, 704, axis=2))
    grads = {n: grads[n].reshape(weights[n].shape) for n in names}
    shapes = [weights[n].shape for n in small]
    packed = [_flat_pad([t[n] for n in small], SW_ROWS) for t in (weights, grads, m_in, v_in)]
    for store, flat in zip((delta, new_m, new_v), adamw(*packed, name="adamw_small")):
        for n, t in zip(small, _split_flat(flat, shapes)):
            store[n] = t
    return (loss, grad_x, *[grads[n] for n in names], *[delta[n] for n in names],
            *[new_m[n] for n in names], *[new_v[n] for n in names])
```

```python
import functools

import jax
import jax.numpy as jnp
from jax import lax
from jax.experimental import pallas as pl
from jax.experimental.pallas import tpu as pltpu
from jax.experimental.pallas import tpu_sc as plsc

F32 = jnp.float32
BF16 = jnp.bfloat16
HI = lax.Precision.HIGHEST
MESH = pl.DeviceIdType.MESH

D = 1024
N_META = 16
PAD = 112
HEAD0 = PAD + N_META
D_CONV = 512
DN_H = 4
DN_D = 128
DN_DIM = 512
CH = 64
IN_DIM = 3592
P_W = 3840
BG0 = 3584
SWA_H = 16
SWA_KV = 4
SWA_D = 64
BLK = 128
NKEY = N_META + 2 * BLK
D_FF = 2816
EPS = 1e-6
LR, B1, B2, AEPS, WD, STEP = 0.001, 0.9, 0.999, 1e-08, 0.01, 10
VMEM_LIMIT = 48 * 1024 * 1024
MM_VMEM_BUDGET = 34 * 1024 * 1024
R_BIG = 6144
R_HALF = R_BIG // 2
SV_ROWS = 48
SW_ROWS = 16


def _pick(n, cands):
    for c in cands:
        if n % c == 0:
            return c
    return n


def _params(sem=None):
    return pltpu.CompilerParams(dimension_semantics=sem, vmem_limit_bytes=VMEM_LIMIT)


def _dot(a, b, ca=1, cb=0, prec=None):
    return lax.dot_general(a, b, (((ca,), (cb,)), ((), ())), precision=prec,
                           preferred_element_type=F32)


def _sigmoid(x):
    return 1.0 / (1.0 + jnp.exp(-x))


def _silu(x):
    return x * _sigmoid(x)


def _dsilu(x):
    s = _sigmoid(x)
    return s * (1.0 + x * (1.0 - s))


def _softplus(x):
    return jnp.maximum(x, 0.0) + jnp.log(1.0 + jnp.exp(-jnp.abs(x)))


def mm(a, b, *, name, ta=False, tb=False, out_dtype=F32, add=None, tm=None, tn=None, tk=None,
       b_chip=False, out_chip=False, swap_mid=False, epi=None, epi_ins=(), epi_consts=(), epi_outs=(), epi_accs=()):
    if epi is not None:
        return _mm_epi(a, b, name=name, tb=tb, tn=tn, b_chip=b_chip, swap_mid=swap_mid, epi=epi, epi_ins=epi_ins,
                       epi_consts=epi_consts, epi_outs=epi_outs, epi_accs=epi_accs)
    chip_of = _chip_order(swap_mid)
    m, k = (a.shape[1], a.shape[0]) if ta else a.shape
    if b_chip:
        n = b.shape[1] if tb else 4 * b.shape[2]
        if tb:
            tk = b.shape[2]
        else:
            tn = b.shape[2]
    else:
        n = b.shape[0] if tb else b.shape[1]
    if out_chip:
        tn = n // 4
    tn = tn or _pick(n, (1408, 1024, 768, 512, 256, 128))
    tk = tk or (_pick(k, (1408, 704, 384, 128)) if ta else _pick(k, (1024, 1408, 768, 512, 128)))
    nk = k // tk
    if tm is None:
        isz = lambda t: jnp.dtype(t.dtype).itemsize
        osz = jnp.dtype(out_dtype).itemsize
        for tm in ((1408, 1024, 512, 384, 256, 128) if ta else (1408, 704, 512, 384, 256, 128)):
            need = 2 * (tm * tk * isz(a) + tk * tn * isz(b) + tm * tn * osz + (tm * tn * 4 if add is not None else 0))
            need += tm * tn * 4 if nk > 1 else 0
            if m % tm == 0 and need <= MM_VMEM_BUDGET:
                break
        else:
            tm = m
    dims = (((0 if ta else 1,), (1 if tb else 0,)), ((), ()))

    def body(*refs):
        if add is None:
            a_ref, b_ref, o_ref, acc_ref = refs
            add_ref = None
        else:
            a_ref, b_ref, add_ref, o_ref, acc_ref = refs
        part = lax.dot_general(a_ref[...].astype(BF16), b_ref[...].astype(BF16), dims,
                               preferred_element_type=F32)

        def finish(total):
            if add_ref is not None:
                total = total + add_ref[...]
            o_ref[...] = total.astype(out_dtype)

        if nk == 1:
            finish(part)
        else:
            kk = pl.program_id(2)

            @pl.when(kk == 0)
            def _():
                acc_ref[...] = part

            @pl.when(kk > 0)
            def _():
                acc_ref[...] += part

            @pl.when(kk == nk - 1)
            def _():
                finish(acc_ref[...])

    a_spec = pl.BlockSpec((tk, tm), lambda i, j, kk: (kk, i)) if ta else pl.BlockSpec((tm, tk), lambda i, j, kk: (i, kk))
    if b_chip and tb:
        b_spec = pl.BlockSpec((None, tn, tk), lambda i, j, kk: (chip_of(kk), j, 0))
    elif b_chip:
        b_spec = pl.BlockSpec((None, tk, tn), lambda i, j, kk: (j, kk, 0))
    elif tb:
        b_spec = pl.BlockSpec((tn, tk), lambda i, j, kk: (j, kk))
    else:
        b_spec = pl.BlockSpec((tk, tn), lambda i, j, kk: (kk, j))
    o_spec = pl.BlockSpec((tm, tn), lambda i, j, kk: (i, j))
    in_specs = [a_spec, b_spec] + ([o_spec] if add is not None else [])
    args = [a, b] + ([add] if add is not None else [])
    out_spec = pl.BlockSpec((None, tm, tn), lambda i, j, kk: (chip_of(j), i, 0)) if out_chip else o_spec
    return pl.pallas_call(
        body, name=name, interpret=False,
        out_shape=jax.ShapeDtypeStruct((4, m, tn) if out_chip else (m, n), out_dtype),
        grid=(m // tm, n // tn, nk), in_specs=in_specs, out_specs=out_spec,
        scratch_shapes=[pltpu.VMEM((tm, tn) if nk > 1 else (8, 128), F32)],
        compiler_params=_params(("parallel", "parallel", "arbitrary")),
    )(*args)


def _chip_order(swap_mid):
    return (lambda k: (k % 2) * 2 + k // 2) if swap_mid else (lambda k: k)


def _mm_epi(a, b, *, name, tb, tn, b_chip, epi, epi_ins, epi_consts, epi_outs, epi_accs, swap_mid=False):
    chip_of = _chip_order(swap_mid)
    m, k = a.shape
    if b_chip:
        n = b.shape[1] if tb else 4 * b.shape[2]
        tk = b.shape[2] if tb else None
        tn = tn if tb else b.shape[2]
    else:
        n = b.shape[0] if tb else b.shape[1]
        tk = None
    tn = tn or _pick(n, (1408, 1024, 768, 512, 256, 128))
    tk = tk or _pick(k, (1024, 1408, 768, 512, 128))
    nk, nj = k // tk, n // tn
    isz = lambda t: jnp.dtype(t.dtype if hasattr(t, "dtype") else t).itemsize
    outs3 = [t if isinstance(t, tuple) else (t, n, lambda j: j) for t in epi_outs]
    side = sum(isz(t) for t, _ in epi_ins) + sum(isz(dt) for dt, _, _ in outs3)
    for tm in (1408, 704, 512, 384, 256, 128):
        need = 2 * (tm * tk * isz(a) + tk * tn * isz(b) + tm * tn * side) + (tm * tn * 4 if nk > 1 else 0)
        if m % tm == 0 and need <= MM_VMEM_BUDGET:
            break
    else:
        tm = m
    dims = (((1,), (1 if tb else 0,)), ((), ()))
    n_in, n_c, n_out, n_acc = len(epi_ins), len(epi_consts), len(epi_outs), len(epi_accs)

    def body(*refs):
        a_ref, b_ref = refs[:2]
        in_refs = refs[2:2 + n_in + n_c]
        out_refs = refs[2 + n_in + n_c:2 + n_in + n_c + n_out]
        acc_out = refs[2 + n_in + n_c + n_out:2 + n_in + n_c + n_out + n_acc]
        acc_ref = refs[-1]
        i, j, kk = pl.program_id(0), pl.program_id(1), pl.program_id(2)
        part = lax.dot_general(a_ref[...].astype(BF16), b_ref[...].astype(BF16), dims,
                               preferred_element_type=F32)

        def finish(total):
            res = epi(i * tm, total, *[r[...] for r in in_refs])
            if not isinstance(res, (tuple, list)):
                res = (res,)
            for r, v in zip(out_refs, res[:n_out]):
                r[...] = v.astype(r.dtype)
            if n_acc:
                @pl.when(jnp.logical_and(i == 0, j == 0))
                def _():
                    for r in acc_out:
                        r[...] = jnp.zeros(r.shape, r.dtype)

                for r, v in zip(acc_out, res[n_out:]):
                    r[...] += jnp.broadcast_to(v, r.shape).astype(r.dtype)

        if nk == 1:
            finish(part)
        else:
            @pl.when(kk == 0)
            def _():
                acc_ref[...] = part

            @pl.when(kk > 0)
            def _():
                acc_ref[...] += part

            @pl.when(kk == nk - 1)
            def _():
                finish(acc_ref[...])

    a_spec = pl.BlockSpec((tm, tk), lambda i, j, kk: (i, kk))
    if b_chip and tb:
        b_spec = pl.BlockSpec((None, tn, tk), lambda i, j, kk: (chip_of(kk), j, 0))
    elif b_chip:
        b_spec = pl.BlockSpec((None, tk, tn), lambda i, j, kk: (j, kk, 0))
    elif tb:
        b_spec = pl.BlockSpec((tn, tk), lambda i, j, kk: (j, kk))
    else:
        b_spec = pl.BlockSpec((tk, tn), lambda i, j, kk: (kk, j))
    in_specs = [a_spec, b_spec]
    in_specs += [pl.BlockSpec((tm, tn), lambda i, j, kk, col=col: (i, col(j))) for _, col in epi_ins]
    in_specs += [pl.BlockSpec(t.shape, lambda i, j, kk, nd=t.ndim: (0,) * nd) for t in epi_consts]
    out_specs = [pl.BlockSpec((tm, tn), lambda i, j, kk, col=col: (i, col(j))) for _, _, col in outs3]
    out_specs += [pl.BlockSpec(s, lambda i, j, kk, nd=len(s): (0,) * nd) for s, _ in epi_accs]
    out_shape = [jax.ShapeDtypeStruct((m, width), dt) for dt, width, _ in outs3]
    out_shape += [jax.ShapeDtypeStruct(s, dt) for s, dt in epi_accs]
    sem = ("arbitrary", "arbitrary", "arbitrary") if n_acc else ("parallel", "parallel", "arbitrary")
    return pl.pallas_call(
        body, name=name, interpret=False, out_shape=out_shape,
        grid=(m // tm, nj, nk), in_specs=in_specs, out_specs=out_specs,
        scratch_shapes=[pltpu.VMEM((tm, tn) if nk > 1 else (8, 128), F32)],
        compiler_params=_params(sem),
    )(a, b, *[t for t, _ in epi_ins], *epi_consts)


def cols(arr, tr, width=None, cb=0):
    width = width or arr.shape[1]
    return (arr, (tr, width), lambda i: (i, cb), "r2")


def heads(arr, tr):
    return (arr, (arr.shape[0], tr, arr.shape[2]), lambda i: (0, i, 0), "r3")


def whole(arr):
    nd = arr.ndim
    return (arr, arr.shape, lambda i: (0,) * nd, "w")


STRIP = 16


def _rows_of(ref, kind, r0, n):
    if kind == "r2":
        return ref[pl.ds(r0, n), :]
    if kind == "r3":
        return ref[:, pl.ds(r0, n), :]
    return ref[...]


def _set_rows(ref, kind, r0, n, v):
    if kind == "r2":
        ref[pl.ds(r0, n), :] = v.astype(ref.dtype)
    elif kind == "r3":
        ref[:, pl.ds(r0, n), :] = v.astype(ref.dtype)
    else:
        ref[...] = v.astype(ref.dtype)


def rowwise(fn, ins, outs, *, steps, name, accs=(), strip=None):
    n_in, n_out, n_acc = len(ins), len(outs), len(accs)
    kin = [t[3] for t in ins]
    kout = [t[4] for t in outs]
    tr = next((t[1][-2] for t in ins if t[3] != "w"), 0)

    def body(*refs):
        i = pl.program_id(0)
        in_refs, out_refs, acc_refs = refs[:n_in], refs[n_in:n_in + n_out], refs[n_in + n_out:]
        if n_acc:
            @pl.when(i == 0)
            def _():
                for r in acc_refs:
                    r[...] = jnp.zeros(r.shape, r.dtype)

        def run(r0, n):
            res = fn(i * tr + r0, *[_rows_of(r, k, r0, n) for r, k in zip(in_refs, kin)])
            if not isinstance(res, (tuple, list)):
                res = (res,)
            for r, k, v in zip(out_refs, kout, res[:n_out]):
                _set_rows(r, k, r0, n, v)
            for r, v in zip(acc_refs, res[n_out:]):
                r[...] += jnp.broadcast_to(v, r.shape).astype(r.dtype)

        if strip is None or tr <= strip:
            run(0, tr)
        else:
            def step(s, carry):
                run(pl.multiple_of(s * strip, strip), strip)
                return carry
            lax.fori_loop(0, tr // strip, step, 0)

    def zmap(nd):
        return lambda i: (0,) * nd

    in_specs = [pl.BlockSpec(t[1], t[2]) for t in ins]
    out_specs = [pl.BlockSpec(t[2], t[3]) for t in outs]
    out_specs += [pl.BlockSpec(s, zmap(len(s))) for s, _ in accs]
    out_shape = [jax.ShapeDtypeStruct(t[0], t[1]) for t in outs]
    out_shape += [jax.ShapeDtypeStruct(s, d) for s, d in accs]
    res = pl.pallas_call(
        body, name=name, interpret=False, out_shape=out_shape, grid=(steps,),
        in_specs=in_specs, out_specs=out_specs,
        compiler_params=_params(("arbitrary",)),
    )(*[t[0] for t in ins])
    return res


def out2d(rows, width, dtype, tr):
    return ((rows, width), dtype, (tr, width), lambda i: (i, 0), "r2")


def conv_fwd(xs, w8, kw, *, rows, c, tc, tr, name, post, extras=(), outs=(), pre=None, strip=STRIP):
    nx, ne, no = len(xs), len(extras), len(outs)
    nr, nc = rows // tr, c // tc
    r8 = tr // 8
    st = strip

    def body(*refs):
        x_refs = refs[:2 * nx]
        w_ref = refs[2 * nx]
        e_refs = refs[2 * nx + 1:2 * nx + 1 + ne]
        o_refs = refs[2 * nx + 1 + ne:2 * nx + 1 + ne + no]
        scr = refs[-1]
        j, i = pl.program_id(0), pl.program_id(1)
        halo = [x_refs[2 * q + 1][...].astype(F32) for q in range(nx)]
        scr[0:8, :] = jnp.where(i > 0, pre(*halo) if pre else halo[0], 0.0)

        def fill(s, carry):
            r0 = pl.multiple_of(s * st, st)
            cur = [x_refs[2 * q][pl.ds(r0, st), :].astype(F32) for q in range(nx)]
            scr[pl.ds(8 + r0, st), :] = pre(*cur) if pre else cur[0]
            return carry

        def comp(s, carry):
            r0 = pl.multiple_of(s * st, st)
            win = scr[pl.ds(r0, st + 8), :]
            y = jnp.zeros((st, tc), F32)
            for q in range(kw):
                sh = kw - 1 - q
                y = y + w_ref[q:q + 1, :] * win[8 - sh:8 - sh + st]
            res = post(j, y, *[e[pl.ds(r0, st), :] for e in e_refs])
            if not isinstance(res, (tuple, list)):
                res = (res,)
            for r, v in zip(o_refs, res):
                r[pl.ds(r0, st), :] = v.astype(r.dtype)
            return carry

        lax.fori_loop(0, tr // st, fill, 0)
        lax.fori_loop(0, tr // st, comp, 0)

    in_specs, args = [], []
    for arr, cb0 in xs:
        in_specs.append(pl.BlockSpec((tr, tc), lambda j, i, cb0=cb0: (i, cb0 + j)))
        in_specs.append(pl.BlockSpec((8, tc), lambda j, i, cb0=cb0: (jnp.maximum(i * r8 - 1, 0), cb0 + j)))
        args += [arr, arr]
    in_specs.append(pl.BlockSpec((8, tc), lambda j, i: (0, j)))
    args.append(w8)
    for arr, cb0 in extras:
        in_specs.append(pl.BlockSpec((tr, tc), lambda j, i, cb0=cb0: (i, cb0 + j)))
        args.append(arr)
    return pl.pallas_call(
        body, name=name, interpret=False,
        out_shape=[jax.ShapeDtypeStruct((rows, c), dt) for dt in outs],
        grid=(nc, nr), in_specs=in_specs,
        out_specs=[pl.BlockSpec((tr, tc), lambda j, i: (i, j)) for _ in outs],
        scratch_shapes=[pltpu.VMEM((tr + 8, tc), F32)],
        compiler_params=_params(("parallel", "arbitrary")),
    )(*args)


def conv_bwd(xs, w8, kw, dy, *, rows, c, tc, tr, name, post, extras=(), outs=(), pre=None, into=None):
    nx, ne, no = len(xs), len(extras), len(outs)
    nr, nc = rows // tr, c // tc
    r8 = tr // 8

    def body(*refs):
        x_refs = refs[:2 * nx]
        w_ref, dy_ref, dyn_ref = refs[2 * nx:2 * nx + 3]
        e_refs = refs[2 * nx + 3:2 * nx + 3 + ne]
        first_out = 2 * nx + 3 + ne + (1 if into is not None else 0)
        o_refs = refs[first_out:first_out + no]
        dw_ref = refs[first_out + no]
        xscr, gscr = refs[-2], refs[-1]
        i = pl.program_id(1)
        halo = [x_refs[2 * q + 1][...].astype(F32) for q in range(nx)]
        xscr[0:8, :] = jnp.where(i > 0, pre(*halo) if pre else halo[0], 0.0)
        gscr[tr:tr + 8, :] = jnp.where(i < nr - 1, dyn_ref[...].astype(F32), 0.0)

        def fill(s, carry):
            r0 = pl.multiple_of(s * STRIP, STRIP)
            cur = [x_refs[2 * q][pl.ds(r0, STRIP), :].astype(F32) for q in range(nx)]
            xscr[pl.ds(8 + r0, STRIP), :] = pre(*cur) if pre else cur[0]
            gscr[pl.ds(r0, STRIP), :] = dy_ref[pl.ds(r0, STRIP), :].astype(F32)
            return carry

        def comp(s, dws):
            r0 = pl.multiple_of(s * STRIP, STRIP)
            gwin = gscr[pl.ds(r0, STRIP + 8), :]
            xwin = xscr[pl.ds(r0, STRIP + 8), :]
            g = gwin[0:STRIP]
            dx = jnp.zeros((STRIP, tc), F32)
            new = []
            for q in range(kw):
                sh = kw - 1 - q
                dx = dx + w_ref[q:q + 1, :] * gwin[sh:sh + STRIP]
                part = g * xwin[8 - sh:8 - sh + STRIP]
                new.append(dws[q] + part[0:8] + part[8:16])
            res = post(dx, *[e[pl.ds(r0, STRIP), :] for e in e_refs])
            if not isinstance(res, (tuple, list)):
                res = (res,)
            for r, v in zip(o_refs, res):
                r[pl.ds(r0, STRIP), :] = v.astype(r.dtype)
            return tuple(new)

        lax.fori_loop(0, tr // STRIP, fill, 0)
        dws = lax.fori_loop(0, tr // STRIP, comp, tuple(jnp.zeros((8, tc), F32) for _ in range(kw)))

        @pl.when(i == 0)
        def _():
            dw_ref[...] = jnp.zeros((8, tc), F32)

        dw_ref[...] += jnp.concatenate([jnp.sum(t, axis=0, keepdims=True) for t in dws]
                                       + [jnp.zeros((8 - kw, tc), F32)], axis=0)

    in_specs, args = [], []
    for arr, cb0 in xs:
        in_specs.append(pl.BlockSpec((tr, tc), lambda j, i, cb0=cb0: (i, cb0 + j)))
        in_specs.append(pl.BlockSpec((8, tc), lambda j, i, cb0=cb0: (jnp.maximum(i * r8 - 1, 0), cb0 + j)))
        args += [arr, arr]
    in_specs.append(pl.BlockSpec((8, tc), lambda j, i: (0, j)))
    in_specs.append(pl.BlockSpec((tr, tc), lambda j, i: (i, j)))
    in_specs.append(pl.BlockSpec((8, tc), lambda j, i: (jnp.minimum((i + 1) * r8, nr * r8 - 1), j)))
    args += [w8, dy, dy]
    for arr, cb0 in extras:
        in_specs.append(pl.BlockSpec((tr, tc), lambda j, i, cb0=cb0: (i, cb0 + j)))
        args.append(arr)
    out_shape = [jax.ShapeDtypeStruct((rows, c), dt) for dt in outs]
    out_specs = [pl.BlockSpec((tr, tc), lambda j, i: (i, j)) for _ in outs]
    aliases = {}
    if into is not None:
        arr, cb0 = into
        aliases = {len(args): 0}
        in_specs.append(pl.BlockSpec(memory_space=pl.ANY))
        args.append(arr)
        out_shape[0] = jax.ShapeDtypeStruct(arr.shape, arr.dtype)
        out_specs[0] = pl.BlockSpec((tr, tc), lambda j, i, cb0=cb0: (i, cb0 + j))
    return pl.pallas_call(
        body, name=name, interpret=False,
        out_shape=out_shape + [jax.ShapeDtypeStruct((8, c), F32)],
        grid=(nc, nr), in_specs=in_specs,
        out_specs=out_specs + [pl.BlockSpec((8, tc), lambda j, i: (0, j))],
        scratch_shapes=[pltpu.VMEM((tr + 8, tc), F32), pltpu.VMEM((tr + 8, tc), F32)],
        input_output_aliases=aliases,
        compiler_params=_params(("parallel", "arbitrary")),
    )(*args)


def rms_fwd(h, w, *, name):
    rows = h.shape[0]
    tr = _pick(rows, (384, 128))

    def fn(i, x, wv):
        r = lax.rsqrt(jnp.mean(x * x, axis=1, keepdims=True) + EPS)
        return x * r * wv

    return rowwise(fn, [cols(h, tr), whole(w)], [out2d(rows, D, BF16, tr)], steps=rows // tr, name=name)[0]


def _rms_bwd_epi(row0, g, x, dr, wv):
    r = lax.rsqrt(jnp.mean(x * x, axis=1, keepdims=True) + EPS)
    xh = x * r
    gw = g * wv
    dx = r * (gw - xh * jnp.mean(gw * xh, axis=1, keepdims=True))
    row = row0 + lax.broadcasted_iota(jnp.int32, (x.shape[0], 1), 0)
    return jnp.where(row >= PAD, dr + dx, 0.0), jnp.sum(g * xh, axis=0, keepdims=True)


def dx_rms_bwd(dy, w, h, nw, dres, *, name, b_chip=False, swap_mid=False):
    return mm(dy, w, tb=True, b_chip=b_chip, swap_mid=swap_mid, tn=D, name=name, epi=_rms_bwd_epi,
              epi_ins=[(h, lambda j: 0), (dres, lambda j: 0)], epi_consts=[nw], epi_outs=[F32],
              epi_accs=[((1, D), F32)])


def loss_grad(h, target):
    rows = h.shape[0]

    def fn(i, y, t):
        diff = jnp.where(i >= HEAD0, y - t, 0.0)
        part = jnp.sum(jnp.sum(diff * diff, axis=1, keepdims=True), axis=0, keepdims=True)
        return diff * (1.0 / D), part * (0.5 / D)

    tgt = (target, (BLK, D), lambda i: (jnp.maximum(i - 1, 0), 0), "r2")
    return rowwise(fn, [cols(h, BLK), tgt], [out2d(rows, D, F32, BLK)], steps=rows // BLK,
                   name="loss_grad", accs=[((1, 128), F32)])


def adamw(w, g, m, v, *, name):
    shape = w.shape
    gs = list(g) if isinstance(g, (list, tuple)) else [g]
    nl = len(gs)
    width = shape[-1]
    rows = w.size // width
    rl = rows // nl
    tr = _pick(rl, (256, 176, 128, 64, 16, 8))
    nr = rl // tr
    if w.ndim == 3 and shape[1] % tr == 0:
        per = shape[1] // tr
        view = lambda t: (t, (None, tr, width), lambda i: (i // per, i % per, 0), "r2")
        out = (shape, F32, (None, tr, width), lambda i: (i // per, i % per, 0), "r2")
    else:
        view = lambda t: cols(t.reshape(rows, width), tr)
        out = out2d(rows, width, F32, tr)

    def fn(i, wv, mv, vv, *gvs):
        gv = gvs[0]
        for layer in range(1, nl):
            gv = jnp.where(i >= layer * rl, gvs[layer], gv)
        mn = B1 * mv + (1.0 - B1) * gv
        vn = B2 * vv + (1.0 - B2) * gv * gv
        mh = mn / (1.0 - B1 ** STEP)
        vh = vn / (1.0 - B2 ** STEP)
        return -LR * (mh / (jnp.sqrt(vh) + AEPS) + WD * wv), mn, vn, gv

    g_ins = [(t.reshape(rl, width), (tr, width), lambda i, layer=layer: (jnp.clip(i - layer * nr, 0, nr - 1), 0), "r2")
             for layer, t in enumerate(gs)]
    res = rowwise(fn, [view(t) for t in (w, m, v)] + g_ins, [out] * 4, steps=rows // tr, name=name)
    return [r.reshape(shape) for r in res]


HB = DN_H * CH
PAIR = 2


def _split(a):
    hi = a.astype(BF16)
    return hi, (a - hi.astype(F32)).astype(BF16)


def _dot1(a, b, ca=1, cb=0):
    return _dot(a.astype(BF16), b.astype(BF16), ca, cb)


def _dot3(a, b, ca=1, cb=0):
    ah, al = _split(a)
    bh, bl = _split(b)
    return _dot(ah, bh, ca, cb) + (_dot(ah, bl, ca, cb) + _dot(al, bh, ca, cb))


def _dot01(m01, b, ca=1, cb=0):
    bh, bl = _split(b)
    m = m01.astype(BF16)
    return _dot(m, bh, ca, cb) + _dot(m, bl, ca, cb)


def _stack(x):
    return jnp.concatenate([x[:, h * DN_D:(h + 1) * DN_D] for h in range(DN_H)], axis=0)


def _unstack(x):
    return jnp.concatenate([x[h * CH:(h + 1) * CH] for h in range(DN_H)], axis=1)


def _tri_inv(a, blk, eye):
    ad = jnp.where(blk, a, 0.0)
    lo = a - ad
    a2 = _dot3(ad, ad)
    a4 = _dot3(a2, a2)
    a8 = _dot3(a4, a4)
    dgi = _dot3(_dot3(_dot3(eye - ad, eye + a2), eye + a4), eye + a8)
    n = _dot3(dgi, lo)
    return _dot3(_dot3(eye - n, eye + _dot3(n, n)), dgi)


def _dn_masks():
    row = lax.broadcasted_iota(jnp.int32, (HB, HB), 0)
    col = lax.broadcasted_iota(jnp.int32, (HB, HB), 1)
    same = (row // CH) == (col // CH)
    incl = jnp.logical_and(same, row >= col)
    strict = jnp.logical_and(same, row > col)
    upper = jnp.logical_and(same, row <= col)
    blk = (row // 16) == (col // 16)
    eye = (row == col).astype(F32)
    return incl, strict, upper, blk, eye


def _dn_chunk(qv, kv, vv, bc, br, incl, strict):
    r64 = lax.broadcasted_iota(jnp.int32, (CH, CH), 0)
    c64 = lax.broadcasted_iota(jnp.int32, (CH, CH), 1)
    dcol = _dot01((r64 >= c64).astype(F32), bc)
    drow = _dot3(br, (r64 <= c64).astype(F32))
    col = lambda m, l0: jnp.concatenate([m[:, l0 + h:l0 + h + 1] for h in range(DN_H)], axis=0)
    b_c = col(bc, 0)
    d_c = col(dcol, 4)
    d_r = jnp.concatenate([drow[4 + h:5 + h, :] for h in range(DN_H)], axis=1)
    d_last_h = [dcol[CH - 1:CH, 4 + h:5 + h] for h in range(DN_H)]
    d_last = jnp.concatenate([jnp.broadcast_to(t, (CH, 1)) for t in d_last_h], axis=0)
    q, k, v = _stack(qv), _stack(kv), _stack(vv)
    dm = jnp.where(incl, jnp.exp(jnp.where(incl, d_c - d_r, 0.0)), 0.0)
    kk = _dot1(k, k, 1, 1)
    a = jnp.where(strict, b_c * kk * dm, 0.0)
    ed = jnp.exp(d_c)
    rhs = jnp.concatenate([v * b_c, k * (b_c * ed)], axis=1)
    qk = _dot1(q, k, 1, 1) * dm
    ekd = jnp.exp(d_last - d_c)
    gl = [jnp.exp(t) for t in d_last_h]
    return q, k, v, b_c, dm, kk, a, ed, rhs, qk, ekd, gl


def dn_fwd(qkv_n, bgcol, bgrow):
    rows = qkv_n.shape[0]
    nch = rows // CH

    def body(q_ref, k_ref, v_ref, bc_ref, br_ref, o_ref, s_out, ti_out, s_scr, prep, prep_qk, prep_gl):
        n = pl.program_id(0)

        @pl.when(n == 0)
        def _():
            s_scr[...] = jnp.zeros(s_scr.shape, F32)
            prep[...] = jnp.zeros(prep.shape, F32)
            prep_qk[...] = jnp.zeros(prep_qk.shape, F32)
            prep_gl[...] = jnp.zeros(prep_gl.shape, F32)

        live = n > 0
        for c in range(PAIR):
            u, w, qd, kd = prep[c, 0], prep[c, 1], prep[c, 2], prep[c, 3]
            v_new, o_state = [], []
            for h in range(DN_H):
                rs = slice(h * CH, (h + 1) * CH)
                s = s_scr[h]
                s_out[c, h] = s
                vn = u[rs] - _dot1(w[rs], s)
                v_new.append(vn)
                o_state.append(_dot1(qd[rs], s))
                s_scr[h] = jnp.where(live, prep_gl[c, h:h + 1, 0:1] * s + _dot1(kd[rs], vn, 0, 0), s)
            o = jnp.concatenate(o_state, axis=0) + _dot1(prep_qk[c], jnp.concatenate(v_new, axis=0))
            o_ref[c * CH:(c + 1) * CH, :] = _unstack(o)

        incl, strict, _, blk, eye = _dn_masks()
        for c in range(PAIR):
            rows_c = slice(c * CH, (c + 1) * CH)
            q, k, v, b_c, dm, kk, a, ed, rhs, qk_n, ekd, gl = _dn_chunk(
                q_ref[rows_c, :], k_ref[rows_c, :], v_ref[rows_c, :], bc_ref[rows_c, :], br_ref[c], incl, strict)
            tinv = _tri_inv(a, blk, eye)
            ti_out[c] = tinv
            sol = _dot3(tinv, rhs)
            prep[c, 0] = sol[:, :DN_D]
            prep[c, 1] = sol[:, DN_D:]
            prep[c, 2] = q * ed
            prep[c, 3] = k * ekd
            prep_qk[c] = qk_n
            prep_gl[c] = jnp.concatenate([jnp.broadcast_to(t, (1, 128)) for t in gl]
                                         + [jnp.zeros((8 - DN_H, 128), F32)], axis=0)

    assert nch % PAIR == 0
    npair = nch // PAIR
    last = npair - 1
    return pl.pallas_call(
        body, name="dn_fwd", interpret=False,
        out_shape=[jax.ShapeDtypeStruct((rows, DN_DIM), F32),
                   jax.ShapeDtypeStruct((nch, DN_H, DN_D, DN_D), F32),
                   jax.ShapeDtypeStruct((nch, HB, HB), F32)],
        grid=(npair + 1,),
        in_specs=[pl.BlockSpec((PAIR * CH, DN_DIM), lambda n: (jnp.minimum(n, last), 0)),
                  pl.BlockSpec((PAIR * CH, DN_DIM), lambda n: (jnp.minimum(n, last), 1)),
                  pl.BlockSpec((PAIR * CH, DN_DIM), lambda n: (jnp.minimum(n, last), 2)),
                  pl.BlockSpec((PAIR * CH, 128), lambda n: (jnp.minimum(n, last), 0)),
                  pl.BlockSpec((PAIR, 8, CH), lambda n: (jnp.minimum(n, last), 0, 0))],
        out_specs=[pl.BlockSpec((PAIR * CH, DN_DIM), lambda n: (jnp.maximum(n - 1, 0), 0)),
                   pl.BlockSpec((PAIR, DN_H, DN_D, DN_D), lambda n: (jnp.maximum(n - 1, 0), 0, 0, 0)),
                   pl.BlockSpec((PAIR, HB, HB), lambda n: (jnp.minimum(n, last), 0, 0))],
        scratch_shapes=[pltpu.VMEM((DN_H, DN_D, DN_D), F32), pltpu.VMEM((PAIR, 4, HB, DN_D), F32),
                        pltpu.VMEM((PAIR, HB, HB), F32), pltpu.VMEM((PAIR, 8, 128), F32)],
        compiler_params=_params(("arbitrary",)),
    )(qkv_n, qkv_n, qkv_n, bgcol, bgrow)


def dn_bwd(qkv_n, bgcol, bgrow, s_all, ti_all, do):
    rows = qkv_n.shape[0]
    nch = rows // CH

    def body(q_ref, k_ref, v_ref, bc_ref, br_ref, s_ref, ti_ref, do_ref, dq_ref, dk_ref, dv_ref, dbg_ref, ds_scr):
        n = pl.program_id(0)

        @pl.when(n == 0)
        def _():
            ds_scr[...] = jnp.zeros(ds_scr.shape, F32)

        incl, strict, upper, _, _ = _dn_masks()
        q, k, v, b_c, dm, kk, a, ed, rhs, qk, ekd, gl = _dn_chunk(q_ref[...], k_ref[...], v_ref[...], bc_ref[...],
                                                                  br_ref[0], incl, strict)
        tinv = ti_ref[0]
        g_o = _stack(do_ref[...])
        sol = _dot3(tinv, rhs)
        u, w = sol[:, :DN_D], sol[:, DN_D:]
        qd, kd = q * ed, k * ekd
        rsum = lambda t: jnp.sum(t, axis=1, keepdims=True)
        rows_of = [slice(h * CH, (h + 1) * CH) for h in range(DN_H)]
        s_h = [s_ref[0, h] for h in range(DN_H)]
        ds_h = [ds_scr[h] for h in range(DN_H)]
        v_new = jnp.concatenate([u[rs] - _dot1(w[rs], s) for rs, s in zip(rows_of, s_h)], axis=0)
        dv_new = _dot1(qk, g_o, 0, 0) + jnp.concatenate([_dot1(kd[rs], t) for rs, t in zip(rows_of, ds_h)], axis=0)
        dqd = jnp.concatenate([_dot1(g_o[rs], s, 1, 1) for rs, s in zip(rows_of, s_h)], axis=0)
        dkd = jnp.concatenate([_dot1(v_new[rs], t, 1, 1) for rs, t in zip(rows_of, ds_h)], axis=0)
        for h, rs in enumerate(rows_of):
            ds_scr[h] = _dot1(qd[rs], g_o[rs], 0, 0) + gl[h] * ds_h[h] - _dot1(w[rs], dv_new[rs], 0, 0)
        dw = jnp.concatenate([-_dot1(dv_new[rs], s, 1, 1) for rs, s in zip(rows_of, s_h)], axis=0)
        dqk = _dot1(g_o, v_new, 1, 1)
        drhs = _dot3(tinv, jnp.concatenate([dv_new, dw], axis=1), 0, 0)
        da = jnp.where(strict, -_dot1(drhs, sol, 1, 1), 0.0)
        drhs_u, drhs_w = drhs[:, :DN_D], drhs[:, DN_D:]
        s2 = rsum(drhs_w * k)
        dbeta = rsum(drhs_u * v) + s2 * ed + rsum(da * kk * dm)
        dkk = da * b_c * dm
        dqkr = dqk * dm
        mmat = da * a + dqk * qk
        tmp = rsum(dkd * kd)
        dd = (s2 * b_c * ed + rsum(mmat) - _dot3(mmat, jnp.ones((HB, 128), F32), 0, 0)[:, :1] + rsum(dqd * qd) - tmp)
        rowi = lax.broadcasted_iota(jnp.int32, (CH, 1), 0)
        last = []
        for h, rs in enumerate(rows_of):
            dgl = jnp.sum(rsum(s_h[h] * ds_h[h]), axis=0, keepdims=True)
            dd_last = jnp.sum(tmp[rs], axis=0, keepdims=True) + dgl * gl[h]
            last.append(jnp.where(rowi == CH - 1, dd_last, 0.0))
        dd = dd + jnp.concatenate(last, axis=0)
        dq_ref[...] = _unstack(_dot1(dqkr, k) + dqd * ed)
        dk_ref[...] = _unstack(drhs_w * (b_c * ed) + _dot1(dkk, k) + _dot1(dkk, k, 0, 0) + _dot1(dqkr, q, 0, 0)
                               + dkd * ekd)
        dv_ref[...] = _unstack(drhs_u * b_c)
        dg = _dot01(upper.astype(F32), jnp.broadcast_to(dd, (HB, 128)))[:, :1]
        lane = lax.broadcasted_iota(jnp.int32, (CH, 128), 1)
        out = jnp.zeros((CH, 128), F32)
        for h, rs in enumerate(rows_of):
            out = out + jnp.where(lane == h, dbeta[rs], 0.0) + jnp.where(lane == 4 + h, dg[rs], 0.0)
        dbg_ref[...] = out

    rev = lambda n: nch - 1 - n
    return pl.pallas_call(
        body, name="dn_bwd", interpret=False,
        out_shape=[jax.ShapeDtypeStruct((rows, DN_DIM), F32)] * 3 + [jax.ShapeDtypeStruct((rows, 128), F32)],
        grid=(nch,),
        in_specs=[pl.BlockSpec((CH, DN_DIM), lambda n: (rev(n), 0)),
                  pl.BlockSpec((CH, DN_DIM), lambda n: (rev(n), 1)),
                  pl.BlockSpec((CH, DN_DIM), lambda n: (rev(n), 2)),
                  pl.BlockSpec((CH, 128), lambda n: (rev(n), 0)),
                  pl.BlockSpec((1, 8, CH), lambda n: (rev(n), 0, 0)),
                  pl.BlockSpec((1, DN_H, DN_D, DN_D), lambda n: (rev(n), 0, 0, 0)),
                  pl.BlockSpec((1, HB, HB), lambda n: (rev(n), 0, 0)),
                  pl.BlockSpec((CH, DN_DIM), lambda n: (rev(n), 0))],
        out_specs=[pl.BlockSpec((CH, DN_DIM), lambda n: (rev(n), 0))] * 3 + [pl.BlockSpec((CH, 128), lambda n: (rev(n), 0))],
        scratch_shapes=[pltpu.VMEM((DN_H, DN_D, DN_D), F32)],
        compiler_params=_params(("arbitrary",)),
    )(qkv_n, qkv_n, qkv_n, bgcol, bgrow, s_all, ti_all, do)


def _swa_valid(n):
    c3 = lax.broadcasted_iota(jnp.int32, (NKEY, 4 * BLK), 0)
    r = lax.broadcasted_iota(jnp.int32, (NKEY, 4 * BLK), 1) % BLK
    prev0 = N_META + BLK
    c = jnp.where(c3 < N_META, PAD + c3, jnp.where(c3 < prev0, c3 - N_META, c3 - prev0))
    lo = jnp.where(c3 < N_META, 0, jnp.where(c3 < prev0, r + 1 + jnp.where(n >= 2, 0, BLK), 0))
    hi = jnp.where(c3 < N_META, r + jnp.where(n >= 1, BLK, 0),
                   jnp.where(c3 < prev0, BLK, r - jnp.where(n >= 1, 0, BLK)))
    return jnp.logical_and(c >= lo, c <= hi)


def _swa_probs(q, kcat, valid, sink):
    s = jnp.where(valid, _dot(kcat, q, 1, 1), -1e30)
    m = jnp.maximum(jnp.max(s, axis=0, keepdims=True), sink)
    e = jnp.where(valid, jnp.exp(s - m), 0.0)
    es = jnp.exp(sink - m)
    inv = 1.0 / (jnp.sum(e, axis=0, keepdims=True) + es)
    return e * inv, es * inv


def _swa_group(q_ref, sk_ref, h):
    q4 = jnp.concatenate([q_ref[4 * h + g] for g in range(4)], axis=0)
    sink4 = jnp.concatenate([jnp.full((1, BLK), sk_ref[4 * h + g], F32) for g in range(4)], axis=1)
    return q4, sink4


def _swa_specs():
    q = pl.BlockSpec((SWA_H, BLK, SWA_D), lambda n: (0, n, 0))
    km = pl.BlockSpec((SWA_KV, N_META, SWA_D), lambda n: (0, PAD // N_META, 0))
    kp = pl.BlockSpec((SWA_KV, BLK, SWA_D), lambda n: (0, jnp.maximum(n - 1, 0), 0))
    kc = pl.BlockSpec((SWA_KV, BLK, SWA_D), lambda n: (0, n, 0))
    return [q, km, kp, kc, km, kp, kc]


def swa_fwd(qh, kh, vh, sinks):
    rows = qh.shape[1]
    nb = rows // BLK

    def body(q_ref, km, kp, kc, vm, vp, vc, sk_ref, o_ref):
        n = pl.program_id(0)
        valid = _swa_valid(n)
        outs = []
        for h in range(SWA_KV):
            kcat = jnp.concatenate([km[h], kp[h], kc[h]], axis=0)
            vcat = jnp.concatenate([vm[h], vp[h], vc[h]], axis=0)
            q4, sink4 = _swa_group(q_ref, sk_ref, h)
            p, _ = _swa_probs(q4, kcat, valid, sink4)
            o4 = _dot(p.astype(BF16), vcat, 0, 0)
            outs += [o4[g * BLK:(g + 1) * BLK] for g in range(4)]
        o_ref[...] = jnp.concatenate(outs, axis=1).astype(BF16)

    return pl.pallas_call(
        body, name="swa_fwd", interpret=False,
        out_shape=jax.ShapeDtypeStruct((rows, SWA_H * SWA_D), BF16),
        grid=(nb,),
        in_specs=_swa_specs() + [pl.BlockSpec(memory_space=pltpu.SMEM)],
        out_specs=pl.BlockSpec((BLK, SWA_H * SWA_D), lambda n: (n, 0)),
        compiler_params=_params(("parallel",)),
    )(qh, kh, kh, kh, vh, vh, vh, sinks)


def swa_bwd(qh, kh, vh, sinks, do):
    rows = qh.shape[1]
    nb = rows // BLK

    def body(q_ref, km, kp, kc, vm, vp, vc, do_ref, sk_ref, dq_ref, dk_ref, dv_ref, dsk_ref):
        n = pl.program_id(0)

        @pl.when(n == 0)
        def _():
            dk_ref[...] = jnp.zeros(dk_ref.shape, F32)
            dv_ref[...] = jnp.zeros(dv_ref.shape, F32)

        valid = _swa_valid(n)
        g_all = do_ref[...]
        rowi = lax.broadcasted_iota(jnp.int32, (SWA_H, 128), 0)
        dsk = jnp.zeros((SWA_H, 128), F32)
        pm = pl.multiple_of(jnp.maximum(n - 1, 0) * BLK, BLK)
        pc = pl.multiple_of(n * BLK, BLK)
        for h in range(SWA_KV):
            kcat = jnp.concatenate([km[h], kp[h], kc[h]], axis=0)
            vcat = jnp.concatenate([vm[h], vp[h], vc[h]], axis=0)
            q4, sink4 = _swa_group(q_ref, sk_ref, h)
            p, ps = _swa_probs(q4, kcat, valid, sink4)
            g4 = jnp.concatenate([g_all[:, (4 * h + g) * SWA_D:(4 * h + g + 1) * SWA_D] for g in range(4)], axis=0)
            dp = _dot(vcat, g4, 1, 1)
            delta = jnp.sum(p * dp, axis=0, keepdims=True)
            ds = (p * (dp - delta)).astype(BF16)
            dq4 = _dot(ds, kcat, 0, 0)
            dkc = _dot(ds, q4)
            dvc = _dot(p.astype(BF16), g4)
            t = ps * delta
            for g in range(4):
                dq_ref[4 * h + g] = dq4[g * BLK:(g + 1) * BLK]
                part = -jnp.sum(t[:, g * BLK:(g + 1) * BLK], axis=1, keepdims=True)
                dsk = dsk + jnp.where(rowi == 4 * h + g, part, 0.0)
            lanes = slice(h * SWA_D, (h + 1) * SWA_D)
            for ref, val in ((dk_ref, dkc), (dv_ref, dvc)):
                ref[PAD:BLK, lanes] += val[0:N_META]
                ref[pl.ds(pm, BLK), lanes] += val[N_META:N_META + BLK]
                ref[pl.ds(pc, BLK), lanes] += val[N_META + BLK:]
        dsk_ref[0] = dsk

    return pl.pallas_call(
        body, name="swa_bwd", interpret=False,
        out_shape=[jax.ShapeDtypeStruct((SWA_H, rows, SWA_D), F32),
                   jax.ShapeDtypeStruct((rows, SWA_KV * SWA_D), F32),
                   jax.ShapeDtypeStruct((rows, SWA_KV * SWA_D), F32),
                   jax.ShapeDtypeStruct((nb, SWA_H, 128), F32)],
        grid=(nb,),
        in_specs=_swa_specs() + [pl.BlockSpec((BLK, SWA_H * SWA_D), lambda n: (n, 0)),
                                 pl.BlockSpec(memory_space=pltpu.SMEM)],
        out_specs=[pl.BlockSpec((SWA_H, BLK, SWA_D), lambda n: (0, n, 0)),
                   pl.BlockSpec((rows, SWA_KV * SWA_D), lambda n: (0, 0)),
                   pl.BlockSpec((rows, SWA_KV * SWA_D), lambda n: (0, 0)),
                   pl.BlockSpec((1, SWA_H, 128), lambda n: (n, 0, 0))],
        compiler_params=_params(("arbitrary",)),
    )(qh, kh, kh, kh, vh, vh, vh, do, sinks)


QK_W = (SWA_H + SWA_KV) * SWA_D


def _head_mean(t):
    r = lax.broadcasted_iota(jnp.int32, (128, 128), 0) // SWA_D
    c = lax.broadcasted_iota(jnp.int32, (128, 128), 1) // SWA_D
    blk = jnp.where(r == c, 1.0 / SWA_D, 0.0).astype(BF16)
    out = []
    for i in range(t.shape[1] // 128):
        hi, lo = _split(t[:, 128 * i:128 * (i + 1)])
        out.append(_dot(hi, blk) + _dot(lo, blk))
    return jnp.concatenate(out, axis=1)


def _qk_scales(qw, kw):
    scale = SWA_D ** -0.5
    wt = jnp.concatenate([jnp.tile(qw.astype(F32) * scale, (1, SWA_H)), jnp.tile(kw.astype(F32), (1, SWA_KV))], axis=1)
    st = jnp.concatenate([jnp.full((1, SWA_H * SWA_D), scale, F32), jnp.ones((1, SWA_KV * SWA_D), F32)], axis=1)
    return wt, st


def qknorm_fwd(qkv, qw, kw):
    rows = qkv.shape[0]
    tr = _pick(rows, (384, 128))
    wt, _ = _qk_scales(qw, kw)

    def fn(i, x, w):
        xq = x[:, :QK_W]
        y = xq * lax.rsqrt(_head_mean(xq * xq) + EPS) * w
        head = lambda t, j: t[:, j * SWA_D:(j + 1) * SWA_D][None]
        qo = jnp.concatenate([head(y, j) for j in range(SWA_H)], axis=0)
        ko = jnp.concatenate([head(y, SWA_H + j) for j in range(SWA_KV)], axis=0)
        vo = jnp.concatenate([head(x, SWA_H + SWA_KV + j) for j in range(SWA_KV)], axis=0)
        return qo, ko, vo

    hm = lambda nh: ((nh, rows, SWA_D), BF16, (nh, tr, SWA_D), lambda i: (0, i, 0), "r3")
    return rowwise(fn, [cols(qkv, tr), whole(wt)], [hm(SWA_H), hm(SWA_KV), hm(SWA_KV)],
                   steps=rows // tr, name="qknorm_fwd")


def qknorm_bwd(qkv, qw, kw, dqh, dk, dv):
    rows = qkv.shape[0]
    tr = _pick(rows, (384, 128))
    wt, st = _qk_scales(qw, kw)

    def fn(i, x, w, sc, dq, dkv, dvv):
        xq = x[:, :QK_W]
        dy = jnp.concatenate([dq[j] for j in range(SWA_H)] + [dkv], axis=1)
        r = lax.rsqrt(_head_mean(xq * xq) + EPS)
        xh = xq * r
        gw = dy * w
        dx = r * (gw - xh * _head_mean(gw * xh))
        return jnp.concatenate([dx, dvv], axis=1), jnp.sum(dy * sc * xh, axis=0, keepdims=True)

    dqkv, dw = rowwise(fn, [cols(qkv, tr), whole(wt), whole(st), heads(dqh, tr), cols(dk, tr), cols(dv, tr)],
                       [out2d(rows, 1536, BF16, tr)], steps=rows // tr, name="qknorm_bwd", accs=[((1, QK_W), F32)])
    dw = dw.reshape(SWA_H + SWA_KV, SWA_D)
    return dqkv, jnp.sum(dw[:SWA_H], axis=0, keepdims=True), jnp.sum(dw[SWA_H:], axis=0, keepdims=True)


def _place():
    return lax.axis_index("x"), lax.axis_index("y"), lax.axis_index("c")


ANY = pl.BlockSpec(memory_space=pl.ANY)


def _rcopy(ssem, rsem, k, src, dst, to):
    return pltpu.make_async_remote_copy(src_ref=src, dst_ref=dst, send_sem=ssem.at[k], recv_sem=rsem.at[k],
                                        device_id=to, device_id_type=MESH)


def gather_weights(shards, small):
    n = len(shards)
    halves = [t.shape[0] // 2 for t in shards]

    def body(*refs):
        s_refs, small_ref = refs[:n], refs[n]
        o_refs, osmall = refs[n + 1:2 * n + 1], refs[2 * n + 1]
        ssem, rsem, lsem = refs[2 * n + 2:]
        x, y, c = _place()
        me = 2 * x + y
        chips = [(1 - x, y), (x, 1 - y), (1 - x, 1 - y)]

        def half(k, s, hh):
            return o_refs[k].at[s, pl.ds(hh * halves[k], halves[k]), :]

        loc = pltpu.make_async_copy(small_ref, osmall.at[me], lsem)
        loc.start()
        sends = []
        for k in range(n):
            for j, (px, py) in enumerate(chips):
                sends.append(_rcopy(ssem, rsem, 6 * k + j, s_refs[k].at[pl.ds(c * halves[k], halves[k]), :],
                                    half(k, me, c), (px, py, c)))
        for j, (px, py) in enumerate(chips):
            sends.append(_rcopy(ssem, rsem, 6 * n + j, small_ref, osmall.at[me], (px, py, c)))
        for cp in sends:
            cp.start()
        for k in range(n):
            for j, (px, py) in enumerate(chips):
                s = 2 * px + py
                _rcopy(ssem, rsem, 6 * k + j, half(k, s, c), half(k, s, c), (x, y, c)).wait_recv()
                fwd = _rcopy(ssem, rsem, 6 * k + 3 + j, half(k, s, c), half(k, s, c), (x, y, 1 - c))
                fwd.start()
                sends.append(fwd)
        for k in range(n):
            for j, (px, py) in enumerate(chips):
                s = 2 * px + py
                _rcopy(ssem, rsem, 6 * k + 3 + j, half(k, s, 1 - c), half(k, s, 1 - c), (x, y, c)).wait_recv()
        for j, (px, py) in enumerate(chips):
            s = 2 * px + py
            _rcopy(ssem, rsem, 6 * n + j, osmall.at[s], osmall.at[s], (x, y, c)).wait_recv()
        for cp in sends:
            cp.wait_send()
        loc.wait()

    res = pl.pallas_call(
        body, name="gather_weights", interpret=False,
        out_shape=[jax.ShapeDtypeStruct((4,) + t.shape, t.dtype) for t in shards]
        + [jax.ShapeDtypeStruct((4, SW_ROWS, 1024), F32)],
        in_specs=[ANY] * (n + 1), out_specs=[ANY] * (n + 1),
        scratch_shapes=[pltpu.SemaphoreType.DMA((6 * n + 3,)), pltpu.SemaphoreType.DMA((6 * n + 3,)),
                        pltpu.SemaphoreType.DMA],
    )(*shards, small)
    return res[:n], res[n]


def _handshake(peers):
    barrier = pltpu.get_barrier_semaphore()
    for peer in peers:
        pl.semaphore_signal(barrier, inc=1, device_id=peer, device_id_type=MESH)
    pl.semaphore_wait(barrier, len(peers))


def gather_weights_beside(shards):
    n = len(shards)
    halves = [t.shape[0] // 2 for t in shards]

    def body(*refs):
        s_refs, o_refs, ssem, rsem = refs[:n], refs[n:2 * n], refs[2 * n], refs[2 * n + 1]
        x, y, c = _place()
        me = 2 * x + y
        chips = [(1 - x, y), (x, 1 - y), (1 - x, 1 - y)]
        _handshake([(px, py, c) for px, py in chips] + [(x, y, 1 - c)])

        def half(k, s, hh):
            return o_refs[k].at[s, pl.ds(hh * halves[k], halves[k]), :]

        sends = []
        for k in range(n):
            for j, (px, py) in enumerate(chips):
                sends.append(_rcopy(ssem, rsem, 6 * k + j, s_refs[k].at[pl.ds(c * halves[k], halves[k]), :],
                                    half(k, me, c), (px, py, c)))
        for cp in sends:
            cp.start()
        for k in range(n):
            for j, (px, py) in enumerate(chips):
                s = 2 * px + py
                _rcopy(ssem, rsem, 6 * k + j, half(k, s, c), half(k, s, c), (x, y, c)).wait_recv()
                fwd = _rcopy(ssem, rsem, 6 * k + 3 + j, half(k, s, c), half(k, s, c), (x, y, 1 - c))
                fwd.start()
                sends.append(fwd)
        for k in range(n):
            for j, (px, py) in enumerate(chips):
                s = 2 * px + py
                _rcopy(ssem, rsem, 6 * k + 3 + j, half(k, s, 1 - c), half(k, s, 1 - c), (x, y, c)).wait_recv()
        for cp in sends:
            cp.wait_send()

    return pl.kernel(
        body, name="gather_weights_beside",
        out_type=[jax.ShapeDtypeStruct((4,) + t.shape, t.dtype) for t in shards],
        mesh=plsc.ScalarSubcoreMesh(axis_name="sequencer", num_cores=1),
        scratch_types=[pltpu.SemaphoreType.DMA((6 * n,)), pltpu.SemaphoreType.DMA((6 * n,))],
        compiler_params=pltpu.CompilerParams(collective_id=1),
    )(*shards)


def swap_halves(gs, *, name):
    n = len(gs)

    def body(*refs):
        g_refs, o_refs, ssem, rsem = refs[:n], refs[n:2 * n], refs[2 * n], refs[2 * n + 1]
        x, y, c = _place()
        cps = []
        for k in range(n):
            hk = g_refs[k].shape[1] // 2
            cps.append(_rcopy(ssem, rsem, k, g_refs[k].at[:, pl.ds((1 - c) * hk, hk), :], o_refs[k], (x, y, 1 - c)))
        for cp in cps:
            cp.start()
        for cp in cps:
            cp.wait()

    return pl.pallas_call(
        body, name=name, interpret=False,
        out_shape=[jax.ShapeDtypeStruct((4, t.shape[1] // 2, t.shape[2]), t.dtype) for t in gs],
        in_specs=[ANY] * n, out_specs=[ANY] * n,
        scratch_shapes=[pltpu.SemaphoreType.DMA((n,)), pltpu.SemaphoreType.DMA((n,))],
    )(*gs)


def _sum_rows(hk):
    return _pick(hk, (512, 352, 256, 128))


def pair_sum(g, other, c_idx, *, name):
    _, hk, width = other.shape
    tr = _sum_rows(hk)
    nbk = hk // tr

    def body(c_ref, g_ref, o_ref, out_ref):
        out_ref[...] = (g_ref[...].astype(F32) + o_ref[...].astype(F32)).astype(BF16)

    return pl.pallas_call(
        body, name=name, interpret=False,
        out_shape=jax.ShapeDtypeStruct((4, hk, width), BF16),
        grid_spec=pltpu.PrefetchScalarGridSpec(
            num_scalar_prefetch=1, grid=(4, nbk),
            in_specs=[pl.BlockSpec((1, tr, width), lambda s, i, c_ref: (s, c_ref[0] * nbk + i, 0)),
                      pl.BlockSpec((1, tr, width), lambda s, i, c_ref: (s, i, 0))],
            out_specs=pl.BlockSpec((1, tr, width), lambda s, i, c_ref: (s, i, 0))),
        compiler_params=_params(("parallel", "parallel")),
    )(c_idx, g, other)


def chip_sum(p, got, idx, *, name):
    _, hk, width = got.shape
    tr = _sum_rows(hk)
    nbk = hk // tr

    def body(idx_ref, p_ref, g_ref, out_ref):
        acc = p_ref[0].astype(F32)
        for j in range(3):
            acc = acc + g_ref[j].astype(F32)
        out_ref[0] = acc

    return pl.pallas_call(
        body, name=name, interpret=False,
        out_shape=jax.ShapeDtypeStruct((2, hk, width), F32),
        grid_spec=pltpu.PrefetchScalarGridSpec(
            num_scalar_prefetch=1, grid=(nbk,),
            in_specs=[pl.BlockSpec((1, tr, width), lambda i, idx_ref: (idx_ref[0], i, 0)),
                      pl.BlockSpec((3, tr, width), lambda i, idx_ref: (0, i, 0))],
            out_specs=pl.BlockSpec((1, tr, width), lambda i, idx_ref: (idx_ref[1], i, 0))),
        compiler_params=_params(("parallel",)),
    )(idx, p, got)


def join_halves(qs):
    n = len(qs)

    def body(*refs):
        q_refs, o_refs, ssem, rsem = refs[:n], refs[n:2 * n], refs[2 * n], refs[2 * n + 1]
        x, y, c = _place()
        cps = [_rcopy(ssem, rsem, k, q_refs[k].at[c], o_refs[k].at[c], (x, y, 1 - c)) for k in range(n)]
        for cp in cps:
            cp.start()
        for k in range(n):
            _rcopy(ssem, rsem, k, q_refs[k].at[c], o_refs[k].at[1 - c], (x, y, 1 - c)).wait_recv()
        for cp in cps:
            cp.wait_send()

    return pl.pallas_call(
        body, name="join_halves", interpret=False,
        out_shape=[jax.ShapeDtypeStruct(t.shape, t.dtype) for t in qs],
        in_specs=[ANY] * n, out_specs=[ANY] * n, input_output_aliases={k: k for k in range(n)},
        scratch_shapes=[pltpu.SemaphoreType.DMA((n,)), pltpu.SemaphoreType.DMA((n,))],
    )(*qs)


def scatter_chips_beside(ps, cid, name):
    n = len(ps)

    def body(*refs):
        p_refs, o_refs, ssem, rsem = refs[:n], refs[n:2 * n], refs[2 * n], refs[2 * n + 1]
        x, y, c = _place()
        chips = [(1 - x, y), (x, 1 - y), (1 - x, 1 - y)]
        _handshake([(px, py, c) for px, py in chips])
        cps = [_rcopy(ssem, rsem, 3 * k + j, p_refs[k].at[2 * px + py], o_refs[k].at[j], (px, py, c))
               for k in range(n) for j, (px, py) in enumerate(chips)]
        for cp in cps:
            cp.start()
        for cp in cps:
            cp.wait()

    return pl.kernel(
        body, name=name, out_type=[jax.ShapeDtypeStruct((3,) + t.shape[1:], t.dtype) for t in ps],
        mesh=plsc.ScalarSubcoreMesh(axis_name="sequencer", num_cores=1),
        scratch_types=[pltpu.SemaphoreType.DMA((3 * n,)), pltpu.SemaphoreType.DMA((3 * n,))],
        compiler_params=pltpu.CompilerParams(collective_id=cid),
    )(*ps)


def reduce_begin(gs, names, c_idx, cid, tag):
    others = swap_halves(gs, name=f"swap_halves_{tag}")
    pairs = [pair_sum(g, o, c_idx, name=f"pair_sum_{nm}") for g, o, nm in zip(gs, others, names)]
    return pairs, scatter_chips_beside(pairs, cid, f"scatter_chips_{tag}")


def reduce_end(pairs, gots, names, idx):
    mine = [chip_sum(p, g, idx, name=f"chip_sum_{nm}") for p, g, nm in zip(pairs, gots, names)]
    return [q.reshape(2 * q.shape[1], q.shape[2]) for q in join_halves(mine)]


def gather_small(v):
    def body(v_ref, o_ref, ssem, rsem, lsem):
        x, y, c = _place()
        peers = []
        for k in range(1, 8):
            fx, fy, fc = (k >> 2) & 1, (k >> 1) & 1, k & 1
            peers.append((1 - x if fx else x, 1 - y if fy else y, 1 - c if fc else c))
        _handshake(peers)
        loc = pltpu.make_async_copy(v_ref, o_ref.at[4 * x + 2 * y + c], lsem)
        loc.start()
        cps = []
        for k, (px, py, pc) in enumerate(peers):
            cps.append((pltpu.make_async_remote_copy(
                src_ref=v_ref, dst_ref=o_ref.at[4 * x + 2 * y + c], send_sem=ssem.at[k], recv_sem=rsem.at[k],
                device_id=(px, py, pc), device_id_type=MESH), 4 * px + 2 * py + pc))
        for cp, _ in cps:
            cp.start()
        for k, (cp, peer) in enumerate(cps):
            pltpu.make_async_remote_copy(
                src_ref=v_ref, dst_ref=o_ref.at[peer], send_sem=ssem.at[k], recv_sem=rsem.at[k],
                device_id=(x, y, c), device_id_type=MESH).wait_recv()
        for cp, _ in cps:
            cp.wait_send()
        loc.wait()

    return pl.kernel(
        body, name="gather_small", out_type=jax.ShapeDtypeStruct((8, SV_ROWS, 1024), F32),
        mesh=plsc.ScalarSubcoreMesh(axis_name="sequencer", num_cores=1),
        scratch_types=[pltpu.SemaphoreType.DMA((7,)), pltpu.SemaphoreType.DMA((7,)), pltpu.SemaphoreType.DMA],
        compiler_params=pltpu.CompilerParams(collective_id=6),
    )(v)


def sum_slots(a):
    def fn(i, t):
        acc = t[0]
        for k in range(1, 8):
            acc = acc + t[k]
        return acc

    return rowwise(fn, [whole(a)], [((SV_ROWS, 1024), F32, (SV_ROWS, 1024), lambda i: (0, 0), "w")], steps=1,
                   name="sum_slots")[0]


def _head_rms(x, nw):
    xs, rs = [], []
    for h in range(DN_H):
        xh = x[:, h * DN_D:(h + 1) * DN_D]
        r = lax.rsqrt(jnp.mean(xh * xh, axis=1, keepdims=True) + EPS)
        xs.append(xh * r)
        rs.append(r)
    return xs, rs


def bg_fwd(p, alog, dtb):
    rows = p.shape[0]
    tr = _pick(rows, (384, 128))

    def fn(i, x, al, dt):
        lane = lax.broadcasted_iota(jnp.int32, x.shape, 1)
        row = i + lax.broadcasted_iota(jnp.int32, x.shape, 0)
        g = -jnp.exp(al) * _softplus(x + dt)
        out = jnp.where(lane < 4, _sigmoid(x), jnp.where(lane < 8, g, 0.0))
        return jnp.where(row >= PAD, out, 0.0)

    return rowwise(fn, [cols(p, tr, 128, BG0 // 128), whole(alog), whole(dtb)], [out2d(rows, 128, F32, tr)],
                   steps=rows // tr, name="bg_fwd")[0]


def bg_bwd(p, alog, dtb, dbg):
    rows = p.shape[0]
    tr = _pick(rows, (384, 128))

    def fn(i, x, al, dt, g_in):
        lane = lax.broadcasted_iota(jnp.int32, x.shape, 1)
        row = i + lax.broadcasted_iota(jnp.int32, x.shape, 0)
        live = row >= PAD
        is_b = jnp.logical_and(live, lane < 4)
        is_g = jnp.logical_and(live, jnp.logical_and(lane >= 4, lane < 8))
        beta = _sigmoid(x)
        ea = jnp.exp(al)
        g = -ea * _softplus(x + dt)
        dalpha = jnp.where(is_g, g_in * (-ea) * _sigmoid(x + dt), 0.0)
        dx = jnp.where(is_b, g_in * beta * (1.0 - beta), dalpha)
        dal = jnp.sum(jnp.where(is_g, g_in * g, 0.0), axis=0, keepdims=True)
        return jnp.concatenate([dx, jnp.zeros(x.shape, F32)], axis=1), dal, jnp.sum(dalpha, axis=0, keepdims=True)

    return rowwise(fn, [cols(p, tr, 128, BG0 // 128), whole(alog), whole(dtb), cols(dbg, tr)],
                   [out2d(rows, 256, BF16, tr)], steps=rows // tr, name="bg_bwd",
                   accs=[((1, 128), F32), ((1, 128), F32)])


def dn_qkv_post(j, y):
    xs = _silu(y)
    sc = jnp.where(j == 0, DN_D ** -0.5, 1.0)
    outs = []
    for h in range(DN_H):
        xh = xs[:, h * DN_D:(h + 1) * DN_D]
        r = lax.rsqrt(jnp.sum(xh * xh, axis=1, keepdims=True) + EPS)
        outs.append(jnp.where(j < 2, xh * r * sc, xh))
    return jnp.concatenate(outs, axis=1), y


def dn_qkv_bwd(cq, dq, dk, dv):
    rows = cq.shape[0]
    tr = _pick(rows, (384, 128))

    def fn(i, c0, c1, c2, g0, g1, g2):
        pieces = []
        for kind, (cv, g) in enumerate(((c0, g0), (c1, g1), (c2, g2))):
            xs = _silu(cv)
            if kind < 2:
                sc = DN_D ** -0.5 if kind == 0 else 1.0
                ds = []
                for h in range(DN_H):
                    sl = slice(h * DN_D, (h + 1) * DN_D)
                    xh, gh = xs[:, sl], g[:, sl]
                    r = lax.rsqrt(jnp.sum(xh * xh, axis=1, keepdims=True) + EPS)
                    xn = xh * r
                    ds.append(sc * r * (gh - xn * jnp.sum(gh * xn, axis=1, keepdims=True)))
                dxs = jnp.concatenate(ds, axis=1)
            else:
                dxs = g
            pieces.append(dxs * _dsilu(cv))
        return jnp.concatenate(pieces, axis=1)

    ins = [cols(cq, tr, DN_DIM, k) for k in range(3)] + [cols(t, tr) for t in (dq, dk, dv)]
    return rowwise(fn, ins, [out2d(rows, 3 * DN_DIM, F32, tr)], steps=rows // tr, name="dn_qkv_bwd")[0]


def dn_out_fwd(o, p, nw):
    rows = o.shape[0]
    tr = _pick(rows, (384, 128))

    def fn(i, ov, z, w):
        xs, _ = _head_rms(ov, w)
        return jnp.concatenate(xs, axis=1) * jnp.concatenate([w] * DN_H, axis=1) * _silu(z)

    return rowwise(fn, [cols(o, tr), cols(p, tr, DN_DIM, 6), whole(nw)], [out2d(rows, DN_DIM, BF16, tr)],
                   steps=rows // tr, name="dn_out_fwd")[0]


def dn_out_bwd(o, p, nw, dymix):
    rows = o.shape[0]
    tr = _pick(rows, (384, 128))

    def fn(i, ov, z, w, dy):
        xs, rs = _head_rms(ov, w)
        sz = _silu(z)
        dn = dy * sz
        dos, dw = [], jnp.zeros((1, DN_D), F32)
        for h in range(DN_H):
            sl = slice(h * DN_D, (h + 1) * DN_D)
            gw = dn[:, sl] * w
            dos.append(rs[h] * (gw - xs[h] * jnp.mean(gw * xs[h], axis=1, keepdims=True)))
            dw = dw + jnp.sum(dn[:, sl] * xs[h], axis=0, keepdims=True)
        n = jnp.concatenate(xs, axis=1) * jnp.concatenate([w] * DN_H, axis=1)
        return jnp.concatenate(dos, axis=1), dy * n * _dsilu(z), dw

    return rowwise(fn, [cols(o, tr), cols(p, tr, DN_DIM, 6), whole(nw), cols(dymix, tr, DN_DIM, 1)],
                   [out2d(rows, DN_DIM, F32, tr), out2d(rows, DN_DIM, BF16, tr)], steps=rows // tr,
                   name="dn_out_bwd", accs=[((1, DN_D), F32)])


def conv_a_pre_bwd(dymix, cv, p):
    rows = cv.shape[0]
    tr = _pick(rows, (384, 128))

    def fn(i, dy, c, go):
        return dy * c, dy * go

    return rowwise(fn, [cols(dymix, tr, D_CONV, 0), cols(cv, tr), cols(p, tr, D_CONV, 1)],
                   [out2d(rows, D_CONV, BF16, tr), out2d(rows, D_CONV, F32, tr)], steps=rows // tr,
                   name="conv_a_pre_bwd")


def _act_bwd_epi(row0, da, gc, val):
    c, val = gc.astype(F32), val.astype(F32)
    return da * _silu(c), da * val * _dsilu(c)


def _rows8(w):
    return jnp.pad(w.astype(F32), ((0, 8 - w.shape[0]), (0, 0)))


def _lanes(v, at):
    return jnp.pad(v.astype(F32), (at, 128 - at - v.shape[0]))[None]


def add_norm(a, w, h, next_nw, *, name):
    if next_nw is None:
        return mm(a, w, add=h, name=name), None
    return mm(a, w, name=name, epi=_add_norm_epi, epi_ins=[(h, lambda j: 0)], epi_consts=[next_nw],
              epi_outs=[F32, BF16])


def _add_norm_epi(row0, t, h, nw):
    x = t + h
    return x, x * lax.rsqrt(jnp.mean(x * x, axis=1, keepdims=True) + EPS) * nw


def ffn_up_conv(hn, w_up, cw8, *, name):
    rows = hn.shape[0]
    tn = w_up.shape[2]
    tm = _pick(rows, (384, 128))
    nr = rows // tm

    def body(x_ref, wg_ref, wv_ref, w_ref, ug_ref, uv_ref, gc_ref, a_ref, carry, scr):
        i = pl.program_id(1)
        x = x_ref[...]
        gate = _dot(x, wg_ref[...])
        val = _dot(x, wv_ref[...])
        ug_ref[...] = gate.astype(BF16)
        uv_ref[...] = val.astype(BF16)
        scr[0:8, :] = jnp.where(i > 0, carry[...], 0.0)
        scr[8:8 + tm, :] = gate
        carry[...] = gate[tm - 8:tm]
        y = jnp.zeros((tm, tn), F32)
        for q in range(3):
            sh = 2 - q
            y = y + w_ref[q:q + 1, :] * scr[8 - sh:8 - sh + tm, :]
        gc_ref[...] = y.astype(BF16)
        a_ref[...] = (_silu(y) * val).astype(BF16)

    half = pl.BlockSpec((tm, tn), lambda j, i: (i, j))
    return pl.pallas_call(
        body, name=name, interpret=False,
        out_shape=[jax.ShapeDtypeStruct((rows, D_FF), BF16)] * 4,
        grid=(D_FF // tn, nr),
        in_specs=[pl.BlockSpec((tm, D), lambda j, i: (i, 0)),
                  pl.BlockSpec((None, D, tn), lambda j, i: (j, 0, 0)),
                  pl.BlockSpec((None, D, tn), lambda j, i: (j + D_FF // tn, 0, 0)),
                  pl.BlockSpec((8, tn), lambda j, i: (0, j))],
        out_specs=[half] * 4,
        scratch_shapes=[pltpu.VMEM((8, tn), F32), pltpu.VMEM((tm + 8, tn), F32)],
        compiler_params=_params(("arbitrary", "arbitrary")),
    )(hn, w_up, w_up, cw8)


def ffn_down_bwd(dh, w_down, gc, uv, ug, cw8, *, name):
    rows = dh.shape[0]
    tn = D_FF // 2
    tm = _pick(rows, (384, 128))
    nr = rows // tm
    r8 = tm // 8

    def body(dh_ref, w_ref, gc_ref, uv_ref, ug_ref, halo_ref, cw_ref, du_ref, dw_ref, carry, gscr, xscr):
        ip = pl.program_id(1)
        i = nr - 1 - ip
        da = _dot(dh_ref[...].astype(BF16), w_ref[...], 1, 1)
        c, val = gc_ref[...].astype(F32), uv_ref[...].astype(F32)
        dgc = da * val * _dsilu(c)
        du_ref[:, tn:] = (da * _silu(c)).astype(BF16)
        gscr[0:tm, :] = dgc
        gscr[tm:tm + 8, :] = jnp.where(ip > 0, carry[...], 0.0)
        carry[...] = dgc[0:8]
        xscr[0:8, :] = jnp.where(i > 0, halo_ref[...].astype(F32), 0.0)
        xscr[8:8 + tm, :] = ug_ref[...].astype(F32)
        dx = jnp.zeros((tm, tn), F32)
        dws = []
        for q in range(3):
            sh = 2 - q
            dx = dx + cw_ref[q:q + 1, :] * gscr[sh:sh + tm, :]
            dws.append(jnp.sum(dgc * xscr[8 - sh:8 - sh + tm, :], axis=0, keepdims=True))
        du_ref[:, :tn] = dx.astype(BF16)

        @pl.when(ip == 0)
        def _():
            dw_ref[...] = jnp.zeros((8, tn), F32)

        dw_ref[...] += jnp.concatenate(dws + [jnp.zeros((5, tn), F32)], axis=0)

    rev = lambda ip: nr - 1 - ip
    tile = lambda arr: pl.BlockSpec((tm, tn), lambda j, ip: (rev(ip), j))
    return pl.pallas_call(
        body, name=name, interpret=False,
        out_shape=[jax.ShapeDtypeStruct((rows, 2 * D_FF), BF16), jax.ShapeDtypeStruct((8, D_FF), F32)],
        grid=(2, nr),
        in_specs=[pl.BlockSpec((tm, D), lambda j, ip: (rev(ip), 0)),
                  pl.BlockSpec((tn, D), lambda j, ip: (j, 0)),
                  tile(gc), tile(uv), tile(ug),
                  pl.BlockSpec((8, tn), lambda j, ip: (jnp.maximum(rev(ip) * r8 - 1, 0), j)),
                  pl.BlockSpec((8, tn), lambda j, ip: (0, j))],
        out_specs=[pl.BlockSpec((tm, 2 * tn), lambda j, ip: (rev(ip), j)),
                   pl.BlockSpec((8, tn), lambda j, ip: (0, j))],
        scratch_shapes=[pltpu.VMEM((8, tn), F32), pltpu.VMEM((tm + 8, tn), F32), pltpu.VMEM((tm + 8, tn), F32)],
        compiler_params=_params(("arbitrary", "arbitrary")),
    )(dh, w_down, gc, uv, ug, ug, cw8)


def ffn_fwd(h, hn, w_up, cw8, w_down, tag, next_nw):
    ug, uv, gc, a = ffn_up_conv(hn, w_up, cw8, name=f"ffn{tag}_up")
    out, hn_next = add_norm(a, w_down, h, next_nw, name=f"ffn{tag}_down")
    return out, hn_next, (hn, ug, uv, a, gc)


def ffn_bwd(h, nw, w_up, cw8, w_down, saved, dh, tag):
    hn, ug, uv, a, gc = saved
    du, d_cw = ffn_down_bwd(dh, w_down, gc, uv, ug, cw8, name=f"ffn{tag}_down_dx")
    d_w_down = mm(a, dh, ta=True, out_dtype=BF16, name=f"ffn{tag}_down_dw")
    dh_new, d_nw = dx_rms_bwd(du, w_up, h, nw, dh, name=f"ffn{tag}_up_dx", b_chip=True, swap_mid=True)
    d_w_up = mm(hn, du, ta=True, out_dtype=BF16, out_chip=True, swap_mid=True, name=f"ffn{tag}_up_dw")
    return dh_new, d_nw, d_w_up, d_cw, d_w_down


def mixer_fwd(h, nw, w_in, ca8, dc8, alog, dtb, dnw, w_out, tie=None, next_nw=None):
    rows = h.shape[0]
    tr = _pick(rows, (384, 128))
    hn = rms_fwd(h, nw, name="mix_norm")
    p = mm(hn, w_in, name="mix_in")
    y_a, cv = conv_fwd([(p, 0), (p, 2)], ca8, 3, rows=rows, c=D_CONV, tc=D_CONV, tr=tr, name="conv_a",
                       pre=lambda gi, ah: gi * ah, post=lambda j, y, go: (go * y, y), extras=[(p, 1)],
                       outs=[BF16, F32])
    qkv_n, cq = conv_fwd([(p, 3)], dc8, 4, rows=rows, c=3 * DN_DIM, tc=DN_DIM, tr=tr, name="dn_conv",
                         post=dn_qkv_post, outs=[F32, F32], strip=tr)
    bgcol = bg_fwd(p, alog, dtb)
    if tie is not None:
        bgcol = tie(bgcol)
    bgrow = bgcol[:, :8].reshape(rows // CH, CH, 8).transpose(0, 2, 1)
    o, s_all, ti_all = dn_fwd(qkv_n, bgcol, bgrow)
    y_b = dn_out_fwd(o, p, dnw)
    ymix = jnp.concatenate([y_a, y_b], axis=1)
    out, hn_next = add_norm(ymix, w_out, h, next_nw, name="mix_out")
    return out, hn_next, (hn, p, cv, qkv_n, cq, bgcol, bgrow, o, s_all, ti_all, ymix)


def mixer_bwd(h, nw, w_in, ca8, dc8, alog, dtb, dnw, w_out, saved, dh):
    hn, p, cv, qkv_n, cq, bgcol, bgrow, o, s_all, ti_all, ymix = saved
    rows = h.shape[0]
    tr = _pick(rows, (384, 128))
    dymix = mm(dh, w_out, tb=True, name="mix_out_dx")
    d_w_out = mm(ymix, dh, ta=True, out_dtype=BF16, name="mix_out_dw")
    do, dz, d_dnw = dn_out_bwd(o, p, dnw, dymix)
    dq, dk, dv, dbg = dn_bwd(qkv_n, bgcol, bgrow, s_all, ti_all, do)
    dbg_p, d_alog, d_dtb = bg_bwd(p, alog, dtb, dbg)
    dcq = dn_qkv_bwd(cq, dq, dk, dv)
    dqkv, d_dc = conv_bwd([(p, 3)], dc8, 4, dcq, rows=rows, c=3 * DN_DIM, tc=DN_DIM, tr=tr, name="dn_conv_bwd",
                          post=lambda dx: dx, outs=[BF16])
    dgo, dcv = conv_a_pre_bwd(dymix, cv, p)
    dgi, dah, d_ca = conv_bwd([(p, 0), (p, 2)], ca8, 3, dcv, rows=rows, c=D_CONV, tc=D_CONV, tr=tr,
                              name="conv_a_bwd", pre=lambda gi, ah: gi * ah,
                              post=lambda dm, gi, ah: (dm * ah, dm * gi), extras=[(p, 0), (p, 2)], outs=[BF16, BF16])
    dp = jnp.concatenate([dgi, dgo, dah, dqkv, dz, dbg_p], axis=1)
    dh_new, d_nw = dx_rms_bwd(dp, w_in, h, nw, dh, name="mix_in_dx")
    d_w_in = mm(hn, dp, ta=True, out_dtype=BF16, name="mix_in_dw")
    return dh_new, d_nw, d_w_in, d_ca, d_dc, d_alog, d_dtb, d_dnw, d_w_out


def swa_layer_fwd(h, hn, wqkv, qw, kw, sinks, wo, next_nw):
    qkv = mm(hn, wqkv, name="swa_qkv")
    qh, kh, vh = qknorm_fwd(qkv, qw, kw)
    att = swa_fwd(qh, kh, vh, sinks)
    out, hn_next = add_norm(att, wo, h, next_nw, name="swa_out")
    return out, hn_next, (hn, qkv, qh, kh, vh, att)


def swa_layer_bwd(h, nw, wqkv, qw, kw, sinks, wo, saved, dh):
    hn, qkv, qh, kh, vh, att = saved
    datt = mm(dh, wo, tb=True, out_dtype=BF16, name="swa_out_dx")
    d_wo = mm(att, dh, ta=True, out_dtype=BF16, name="swa_out_dw")
    dqh, dkh, dvh, dsk = swa_bwd(qh, kh, vh, sinks, datt)
    dqkv, d_qw, d_kw = qknorm_bwd(qkv, qw, kw, dqh, dkh, dvh)
    dh_new, d_nw = dx_rms_bwd(dqkv, wqkv, h, nw, dh, name="swa_qkv_dx")
    d_wqkv = mm(hn, dqkv, ta=True, out_dtype=BF16, name="swa_qkv_dw")
    d_sinks = jnp.sum(dsk[:, :, 0], axis=0)
    return dh_new, d_nw, d_wqkv, d_qw, d_kw, d_sinks, d_wo


BIG = ("mix_w_in", "mix_w_out", "swa_wq", "swa_wk", "swa_wv", "swa_wo", "ffn_w_up", "ffn_w_down")


def _flat_pad(parts, rows):
    v = jnp.concatenate([t.astype(F32).reshape(-1) for t in parts])
    return jnp.pad(v, (0, rows * 1024 - v.shape[0])).reshape(rows, 1024)


def _split_flat(flat, shapes):
    v = flat.reshape(-1)
    out, o = [], 0
    for s in shapes:
        n = 1
        for d_ in s:
            n *= d_
        out.append(v[o:o + n].reshape(s))
        o += n
    return out


def local_step(x0, target0, meta_full, anw, fnw, w_in, ca8, dc8, alog, dtb, dnw, w_out, qw, kw, sinks, fc8, late,
               begin=None, tie=None):
    begin = begin or (lambda tag, names, grads: None)
    h0 = jnp.concatenate([jnp.zeros((PAD, D), F32), meta_full, x0], axis=0)
    h1, hn1, s_mix = mixer_fwd(h0, anw[0], w_in, ca8, dc8, alog, dtb, dnw, w_out, tie, fnw[0])
    wqkv, wo, w_up, w_down = late()
    h2, hn2, s_f0 = ffn_fwd(h1, hn1, w_up[0], fc8[0], w_down[0], 0, anw[1])
    h3, hn3, s_swa = swa_layer_fwd(h2, hn2, wqkv, qw, kw, sinks, wo, fnw[1])
    h4, _, s_f1 = ffn_fwd(h3, hn3, w_up[1], fc8[1], w_down[1], 1, None)
    dh, loss_l = loss_grad(h4, target0)
    dh, d_fnw1, d_up1, d_fc1, d_down1 = ffn_bwd(h3, fnw[1], w_up[1], fc8[1], w_down[1], s_f1, dh, 1)
    begin("ffn1", ("up1", "down1"), [d_up1, d_down1.reshape(4, 704, D)])
    dh, d_anw1, d_wqkv, d_qw, d_kw, d_sinks, d_wo = swa_layer_bwd(h2, anw[1], wqkv, qw, kw, sinks, wo, s_swa, dh)
    begin("swa", ("wq", "wk", "wv", "wo"),
          [d_wqkv[:, :D].reshape(4, 256, D), d_wqkv[:, D:D + 256].reshape(4, 256, 256),
           d_wqkv[:, D + 256:].reshape(4, 256, 256), d_wo.reshape(4, 256, D)])
    dh, d_fnw0, d_up0, d_fc0, d_down0 = ffn_bwd(h1, fnw[0], w_up[0], fc8[0], w_down[0], s_f0, dh, 0)
    begin("ffn0", ("up0", "down0"), [d_up0, d_down0.reshape(4, 704, D)])
    dh, d_anw0, d_w_in, d_ca, d_dc, d_alog, d_dtb, d_dnw, d_w_out = mixer_bwd(
        h0, anw[0], w_in, ca8, dc8, alog, dtb, dnw, w_out, s_mix, dh)
    begin("mix", ("w_in", "w_out"),
          [d_w_in[:, :IN_DIM].reshape(D, 4, 898).transpose(1, 0, 2), d_w_out.reshape(4, 256, D)])
    return (dh, loss_l, d_anw0, d_anw1, d_fnw0, d_fnw1, d_w_in, d_ca, d_dc, d_alog, d_dtb, d_dnw, d_w_out, d_wqkv,
            d_qw, d_kw, d_sinks, d_wo, d_up0, d_up1, d_fc0, d_fc1, d_down0, d_down1)


def kernel(x, meta_tokens, attn_norm_w, ffn_norm_w, mix_w_in, conv_a_w, dn_conv_w, dn_a_log, dn_dt_bias, dn_norm_w, mix_w_out, swa_wq, swa_wk, swa_wv, swa_q_norm_w, swa_k_norm_w, swa_sinks, swa_wo, ffn_w_up, ffn_conv_w, ffn_w_down, loss_target, m_meta_tokens, m_attn_norm_w, m_ffn_norm_w, m_mix_w_in, m_conv_a_w, m_dn_conv_w, m_dn_a_log, m_dn_dt_bias, m_dn_norm_w, m_mix_w_out, m_swa_wq, m_swa_wk, m_swa_wv, m_swa_q_norm_w, m_swa_k_norm_w, m_swa_sinks, m_swa_wo, m_ffn_w_up, m_ffn_conv_w, m_ffn_w_down, v_meta_tokens, v_attn_norm_w, v_ffn_norm_w, v_mix_w_in, v_conv_a_w, v_dn_conv_w, v_dn_a_log, v_dn_dt_bias, v_dn_norm_w, v_mix_w_out, v_swa_wq, v_swa_wk, v_swa_wv, v_swa_q_norm_w, v_swa_k_norm_w, v_swa_sinks, v_swa_wo, v_ffn_w_up, v_ffn_conv_w, v_ffn_w_down):
    ix, iy, ic = lax.axis_index("x"), lax.axis_index("y"), lax.axis_index("c")
    chip = 2 * ix + iy
    seq = x.shape[1]
    rows = HEAD0 + seq

    small_sharded = (conv_a_w, dn_conv_w, ffn_conv_w, meta_tokens)
    up_b, down_b = ffn_w_up.astype(BF16), ffn_w_down.astype(BF16)
    own = [mix_w_in[0].astype(BF16), mix_w_out[0].astype(BF16), swa_wq[0].astype(BF16), swa_wk[0].astype(BF16),
           swa_wv[0].astype(BF16), swa_wo[0].astype(BF16), up_b[0], up_b[1], down_b[0], down_b[1]]
    fill = lambda gathered, mine: [lax.dynamic_update_slice_in_dim(g, t[None], chip, axis=0)
                                   for g, t in zip(gathered, mine)]
    first, g_small = gather_weights(own[:2], _flat_pad(small_sharded, SW_ROWS))
    g_in, g_out = fill(first, own[:2])
    w_in = jnp.pad(g_in.transpose(1, 0, 2).reshape(D, IN_DIM), ((0, 0), (0, P_W - IN_DIM)))
    w_out = g_out.reshape(D, D)
    rest = {}

    def tie(t):
        t, *mine = lax.optimization_barrier((t, *own[2:]))
        rest["w"] = fill(gather_weights_beside(mine), mine)
        return t

    def late():
        g_q, g_k, g_v, g_o, g_up0, g_up1, g_dn0, g_dn1 = rest["w"]
        wqkv = jnp.concatenate([g_q.reshape(D, D), g_k.reshape(D, 256), g_v.reshape(D, 256)], axis=1)
        return wqkv, g_o.reshape(D, D), [g_up0, g_up1], [g_dn0.reshape(D_FF, D), g_dn1.reshape(D_FF, D)]

    gs = g_small.reshape(4, -1)
    ca_full = gs[:, 0:384].reshape(4, 3, 128).transpose(1, 0, 2).reshape(3, D_CONV)
    dc_full = gs[:, 384:1920].reshape(4, 4, 384).transpose(1, 0, 2).reshape(4, 3 * DN_DIM)
    fc_full = gs[:, 1920:6144].reshape(4, 2, 3, 704).transpose(1, 2, 0, 3).reshape(2, 3, D_FF)
    meta_full = gs[:, 6144:10240].reshape(4, N_META, 256).transpose(1, 0, 2).reshape(N_META, D)
    ca8, dc8 = _rows8(ca_full), _rows8(dc_full)
    fc8 = [_rows8(fc_full[0]), _rows8(fc_full[1])]
    alog, dtb = _lanes(dn_a_log[0], 4), _lanes(dn_dt_bias[0], 4)
    dnw = dn_norm_w.astype(F32)
    qw, kw = swa_q_norm_w.astype(F32), swa_k_norm_w.astype(F32)
    sinks = swa_sinks[0].astype(F32)
    anw = [attn_norm_w[0:1], attn_norm_w[1:2]]
    fnw = [ffn_norm_w[0:1], ffn_norm_w[1:2]]

    c_idx = jnp.reshape(ic, (1,)).astype(jnp.int32)
    chip_idx = jnp.stack([chip, ic]).astype(jnp.int32)
    begun = []

    def begin(tag, names, grads):
        pairs, gots = reduce_begin(grads, names, c_idx, 2 + len(begun), tag)
        begun.append((names, pairs, gots))

    (dh, loss_l, d_anw0, d_anw1, d_fnw0, d_fnw1, d_w_in, d_ca, d_dc, d_alog, d_dtb, d_dnw, d_w_out, d_wqkv, d_qw,
     d_kw, d_sinks, d_wo, d_up0, d_up1, d_fc0, d_fc1, d_down0, d_down1) = local_step(
        x[0], loss_target[0], meta_full, anw, fnw, w_in, ca8, dc8, alog, dtb, dnw, w_out, qw, kw, sinks, fc8, late,
        begin, tie)
    grad_x = dh[HEAD0:][None]

    small_parts = [jnp.concatenate([d_anw0, d_anw1], axis=0), jnp.concatenate([d_fnw0, d_fnw1], axis=0),
                   d_alog[0, 4:8], d_dtb[0, 4:8], d_dnw, d_qw, d_kw, d_sinks,
                   d_ca[:3], d_dc[:4], jnp.stack([d_fc0[:3], d_fc1[:3]]), dh[PAD:HEAD0], loss_l[0, 0:1]]
    small_shapes = [(2, D), (2, D), (1, 4), (1, 4), (1, DN_D), (1, SWA_D), (1, SWA_D), (1, SWA_H),
                    (1, 3, D_CONV), (1, 4, 3 * DN_DIM), (2, 3, D_FF), (N_META, D), ()]
    gathered_small = gather_small(_flat_pad(small_parts, SV_ROWS))

    red_big = {}
    for part in (begun[:-1], begun[-1:]):
        part_names = [n for names, _, _ in part for n in names]
        red_big.update(zip(part_names, reduce_end([p for _, ps, _ in part for p in ps],
                                                  [g for _, _, gs_ in part for g in gs_], part_names, chip_idx)))
    g_w_in, g_w_out, g_wq, g_wk, g_wv, g_wo, g_up0, g_up1, g_dn0, g_dn1 = [
        red_big[n] for n in ("w_in", "w_out", "wq", "wk", "wv", "wo", "up0", "up1", "down0", "down1")]

    grads = dict(mix_w_in=g_w_in, mix_w_out=g_w_out, swa_wq=g_wq, swa_wk=g_wk, swa_wv=g_wv, swa_wo=g_wo,
                 ffn_w_up=[g_up0, g_up1], ffn_w_down=[g_dn0, g_dn1])
    weights = dict(meta_tokens=meta_tokens, attn_norm_w=attn_norm_w, ffn_norm_w=ffn_norm_w, mix_w_in=mix_w_in,
                   conv_a_w=conv_a_w, dn_conv_w=dn_conv_w, dn_a_log=dn_a_log, dn_dt_bias=dn_dt_bias,
                   dn_norm_w=dn_norm_w, mix_w_out=mix_w_out, swa_wq=swa_wq, swa_wk=swa_wk, swa_wv=swa_wv,
                   swa_q_norm_w=swa_q_norm_w, swa_k_norm_w=swa_k_norm_w, swa_sinks=swa_sinks, swa_wo=swa_wo,
                   ffn_w_up=ffn_w_up, ffn_conv_w=ffn_conv_w, ffn_w_down=ffn_w_down)
    m_in = dict(meta_tokens=m_meta_tokens, attn_norm_w=m_attn_norm_w, ffn_norm_w=m_ffn_norm_w, mix_w_in=m_mix_w_in,
                conv_a_w=m_conv_a_w, dn_conv_w=m_dn_conv_w, dn_a_log=m_dn_a_log, dn_dt_bias=m_dn_dt_bias,
                dn_norm_w=m_dn_norm_w, mix_w_out=m_mix_w_out, swa_wq=m_swa_wq, swa_wk=m_swa_wk, swa_wv=m_swa_wv,
                swa_q_norm_w=m_swa_q_norm_w, swa_k_norm_w=m_swa_k_norm_w, swa_sinks=m_swa_sinks, swa_wo=m_swa_wo,
                ffn_w_up=m_ffn_w_up, ffn_conv_w=m_ffn_conv_w, ffn_w_down=m_ffn_w_down)
    v_in = dict(meta_tokens=v_meta_tokens, attn_norm_w=v_attn_norm_w, ffn_norm_w=v_ffn_norm_w, mix_w_in=v_mix_w_in,
                conv_a_w=v_conv_a_w, dn_conv_w=v_dn_conv_w, dn_a_log=v_dn_a_log, dn_dt_bias=v_dn_dt_bias,
                dn_norm_w=v_dn_norm_w, mix_w_out=v_mix_w_out, swa_wq=v_swa_wq, swa_wk=v_swa_wk, swa_wv=v_swa_wv,
                swa_q_norm_w=v_swa_q_norm_w, swa_k_norm_w=v_swa_k_norm_w, swa_sinks=v_swa_sinks, swa_wo=v_swa_wo,
                ffn_w_up=v_ffn_w_up, ffn_conv_w=v_ffn_conv_w, ffn_w_down=v_ffn_w_down)
    names = list(weights)
    small = [n for n in names if n not in BIG]
    delta, new_m, new_v = {}, {}, {}
    for n in BIG:
        delta[n], new_m[n], new_v[n], grads[n] = adamw(weights[n], grads[n], m_in[n], v_in[n], name=f"adamw_{n}")
    gathered_small, _ = lax.optimization_barrier((gathered_small, new_v["ffn_w_down"]))
    (g_anw, g_fnw, g_alog, g_dtb, g_dnw, g_qw, g_kw, g_sinks, g_ca_f, g_dc_f, g_fc_f, g_meta_f,
     loss) = _split_flat(sum_slots(gathered_small), small_shapes)
    grads.update(meta_tokens=lax.dynamic_slice_in_dim(g_meta_f, chip * 256, 256, axis=1), attn_norm_w=g_anw,
                 ffn_norm_w=g_fnw, conv_a_w=lax.dynamic_slice_in_dim(g_ca_f, chip * 128, 128, axis=2),
                 dn_conv_w=lax.dynamic_slice_in_dim(g_dc_f, chip * 384, 384, axis=2), dn_a_log=g_alog,
                 dn_dt_bias=g_dtb, dn_norm_w=g_dnw, swa_q_norm_w=g_qw, swa_k_norm_w=g_kw, swa_sinks=g_sinks,
                 ffn_conv_w=lax.dynamic_slice_in_dim(g_fc_f, chip * 704, 704, axis=2))
    grads = {n: grads[n].reshape(weights[n].shape) for n in names}
    shapes = [weights[n].shape for n in small]
    packed = [_flat_pad([t[n] for n in small], SW_ROWS) for t in (weights, grads, m_in, v_in)]
    for store, flat in zip((delta, new_m, new_v), adamw(*packed, name="adamw_small")):
        for n, t in zip(small, _split_flat(flat, shapes)):
            store[n] = t
    return (loss, grad_x, *[grads[n] for n in names], *[delta[n] for n in names],
            *[new_m[n] for n in names], *[new_v[n] for n in names])
```

```python
import functools

import jax
import jax.numpy as jnp
from jax import lax
from jax.experimental import pallas as pl
from jax.experimental.pallas import tpu as pltpu
from jax.experimental.pallas import tpu_sc as plsc

F32 = jnp.float32
BF16 = jnp.bfloat16
HI = lax.Precision.HIGHEST
MESH = pl.DeviceIdType.MESH

D = 1024
N_META = 16
PAD = 112
HEAD0 = PAD + N_META
D_CONV = 512
DN_H = 4
DN_D = 128
DN_DIM = 512
CH = 64
IN_DIM = 3592
P_W = 3840
BG0 = 3584
SWA_H = 16
SWA_KV = 4
SWA_D = 64
BLK = 128
NKEY = N_META + 2 * BLK
D_FF = 2816
EPS = 1e-6
LR, B1, B2, AEPS, WD, STEP = 0.001, 0.9, 0.999, 1e-08, 0.01, 10
VMEM_LIMIT = 48 * 1024 * 1024
MM_VMEM_BUDGET = 34 * 1024 * 1024
R_BIG = 6144
R_HALF = R_BIG // 2
SV_ROWS = 48
SW_ROWS = 16


def _pick(n, cands):
    for c in cands:
        if n % c == 0:
            return c
    return n


def _params(sem=None):
    return pltpu.CompilerParams(dimension_semantics=sem, vmem_limit_bytes=VMEM_LIMIT)


def _dot(a, b, ca=1, cb=0, prec=None):
    return lax.dot_general(a, b, (((ca,), (cb,)), ((), ())), precision=prec,
                           preferred_element_type=F32)


def _sigmoid(x):
    return 1.0 / (1.0 + jnp.exp(-x))


def _silu(x):
    return x * _sigmoid(x)


def _dsilu(x):
    s = _sigmoid(x)
    return s * (1.0 + x * (1.0 - s))


def _softplus(x):
    return jnp.maximum(x, 0.0) + jnp.log(1.0 + jnp.exp(-jnp.abs(x)))


def mm(a, b, *, name, ta=False, tb=False, out_dtype=F32, add=None, tm=None, tn=None, tk=None,
       b_chip=False, out_chip=False, swap_mid=False, epi=None, epi_ins=(), epi_consts=(), epi_outs=(), epi_accs=()):
    if epi is not None:
        return _mm_epi(a, b, name=name, tb=tb, tn=tn, b_chip=b_chip, swap_mid=swap_mid, epi=epi, epi_ins=epi_ins,
                       epi_consts=epi_consts, epi_outs=epi_outs, epi_accs=epi_accs)
    chip_of = _chip_order(swap_mid)
    m, k = (a.shape[1], a.shape[0]) if ta else a.shape
    if b_chip:
        n = b.shape[1] if tb else 4 * b.shape[2]
        if tb:
            tk = b.shape[2]
        else:
            tn = b.shape[2]
    else:
        n = b.shape[0] if tb else b.shape[1]
    if out_chip:
        tn = n // 4
    tn = tn or _pick(n, (1408, 1024, 768, 512, 256, 128))
    tk = tk or (_pick(k, (1408, 704, 384, 128)) if ta else _pick(k, (1024, 1408, 768, 512, 128)))
    nk = k // tk
    if tm is None:
        isz = lambda t: jnp.dtype(t.dtype).itemsize
        osz = jnp.dtype(out_dtype).itemsize
        for tm in ((1408, 1024, 512, 384, 256, 128) if ta else (1408, 704, 512, 384, 256, 128)):
            need = 2 * (tm * tk * isz(a) + tk * tn * isz(b) + tm * tn * osz + (tm * tn * 4 if add is not None else 0))
            need += tm * tn * 4 if nk > 1 else 0
            if m % tm == 0 and need <= MM_VMEM_BUDGET:
                break
        else:
            tm = m
    dims = (((0 if ta else 1,), (1 if tb else 0,)), ((), ()))

    def body(*refs):
        if add is None:
            a_ref, b_ref, o_ref, acc_ref = refs
            add_ref = None
        else:
            a_ref, b_ref, add_ref, o_ref, acc_ref = refs
        part = lax.dot_general(a_ref[...].astype(BF16), b_ref[...].astype(BF16), dims,
                               preferred_element_type=F32)

        def finish(total):
            if add_ref is not None:
                total = total + add_ref[...]
            o_ref[...] = total.astype(out_dtype)

        if nk == 1:
            finish(part)
        else:
            kk = pl.program_id(2)

            @pl.when(kk == 0)
            def _():
                acc_ref[...] = part

            @pl.when(kk > 0)
            def _():
                acc_ref[...] += part

            @pl.when(kk == nk - 1)
            def _():
                finish(acc_ref[...])

    a_spec = pl.BlockSpec((tk, tm), lambda i, j, kk: (kk, i)) if ta else pl.BlockSpec((tm, tk), lambda i, j, kk: (i, kk))
    if b_chip and tb:
        b_spec = pl.BlockSpec((None, tn, tk), lambda i, j, kk: (chip_of(kk), j, 0))
    elif b_chip:
        b_spec = pl.BlockSpec((None, tk, tn), lambda i, j, kk: (j, kk, 0))
    elif tb:
        b_spec = pl.BlockSpec((tn, tk), lambda i, j, kk: (j, kk))
    else:
        b_spec = pl.BlockSpec((tk, tn), lambda i, j, kk: (kk, j))
    o_spec = pl.BlockSpec((tm, tn), lambda i, j, kk: (i, j))
    in_specs = [a_spec, b_spec] + ([o_spec] if add is not None else [])
    args = [a, b] + ([add] if add is not None else [])
    out_spec = pl.BlockSpec((None, tm, tn), lambda i, j, kk: (chip_of(j), i, 0)) if out_chip else o_spec
    return pl.pallas_call(
        body, name=name, interpret=False,
        out_shape=jax.ShapeDtypeStruct((4, m, tn) if out_chip else (m, n), out_dtype),
        grid=(m // tm, n // tn, nk), in_specs=in_specs, out_specs=out_spec,
        scratch_shapes=[pltpu.VMEM((tm, tn) if nk > 1 else (8, 128), F32)],
        compiler_params=_params(("parallel", "parallel", "arbitrary")),
    )(*args)


def _chip_order(swap_mid):
    return (lambda k: (k % 2) * 2 + k // 2) if swap_mid else (lambda k: k)


def _mm_epi(a, b, *, name, tb, tn, b_chip, epi, epi_ins, epi_consts, epi_outs, epi_accs, swap_mid=False):
    chip_of = _chip_order(swap_mid)
    m, k = a.shape
    if b_chip:
        n = b.shape[1] if tb else 4 * b.shape[2]
        tk = b.shape[2] if tb else None
        tn = tn if tb else b.shape[2]
    else:
        n = b.shape[0] if tb else b.shape[1]
        tk = None
    tn = tn or _pick(n, (1408, 1024, 768, 512, 256, 128))
    tk = tk or _pick(k, (1024, 1408, 1280, 768, 512, 128))
    nk, nj = k // tk, n // tn
    isz = lambda t: jnp.dtype(t.dtype if hasattr(t, "dtype") else t).itemsize
    outs3 = [t if isinstance(t, tuple) else (t, n, lambda j: j) for t in epi_outs]
    side = sum(isz(t) for t, _ in epi_ins) + sum(isz(dt) for dt, _, _ in outs3)
    for tm in (1408, 704, 512, 384, 256, 128):
        need = 2 * (tm * tk * isz(a) + tk * tn * isz(b) + tm * tn * side) + (tm * tn * 4 if nk > 1 else 0)
        if m % tm == 0 and need <= MM_VMEM_BUDGET:
            break
    else:
        tm = m
    dims = (((1,), (1 if tb else 0,)), ((), ()))
    n_in, n_c, n_out, n_acc = len(epi_ins), len(epi_consts), len(epi_outs), len(epi_accs)

    def body(*refs):
        a_ref, b_ref = refs[:2]
        in_refs = refs[2:2 + n_in + n_c]
        out_refs = refs[2 + n_in + n_c:2 + n_in + n_c + n_out]
        acc_out = refs[2 + n_in + n_c + n_out:2 + n_in + n_c + n_out + n_acc]
        acc_ref = refs[-1]
        i, j, kk = pl.program_id(0), pl.program_id(1), pl.program_id(2)
        part = lax.dot_general(a_ref[...].astype(BF16), b_ref[...].astype(BF16), dims,
                               preferred_element_type=F32)

        def finish(total):
            res = epi(i * tm, total, *[r[...] for r in in_refs])
            if not isinstance(res, (tuple, list)):
                res = (res,)
            for r, v in zip(out_refs, res[:n_out]):
                r[...] = v.astype(r.dtype)
            if n_acc:
                @pl.when(jnp.logical_and(i == 0, j == 0))
                def _():
                    for r in acc_out:
                        r[...] = jnp.zeros(r.shape, r.dtype)

                for r, v in zip(acc_out, res[n_out:]):
                    r[...] += jnp.broadcast_to(v, r.shape).astype(r.dtype)

        if nk == 1:
            finish(part)
        else:
            @pl.when(kk == 0)
            def _():
                acc_ref[...] = part

            @pl.when(kk > 0)
            def _():
                acc_ref[...] += part

            @pl.when(kk == nk - 1)
            def _():
                finish(acc_ref[...])

    a_spec = pl.BlockSpec((tm, tk), lambda i, j, kk: (i, kk))
    if b_chip and tb:
        b_spec = pl.BlockSpec((None, tn, tk), lambda i, j, kk: (chip_of(kk), j, 0))
    elif b_chip:
        b_spec = pl.BlockSpec((None, tk, tn), lambda i, j, kk: (j, kk, 0))
    elif tb:
        b_spec = pl.BlockSpec((tn, tk), lambda i, j, kk: (j, kk))
    else:
        b_spec = pl.BlockSpec((tk, tn), lambda i, j, kk: (kk, j))
    in_specs = [a_spec, b_spec]
    in_specs += [pl.BlockSpec((tm, tn), lambda i, j, kk, col=col: (i, col(j))) for _, col in epi_ins]
    in_specs += [pl.BlockSpec(t.shape, lambda i, j, kk, nd=t.ndim: (0,) * nd) for t in epi_consts]
    out_specs = [pl.BlockSpec((tm, tn), lambda i, j, kk, col=col: (i, col(j))) for _, _, col in outs3]
    out_specs += [pl.BlockSpec(s, lambda i, j, kk, nd=len(s): (0,) * nd) for s, _ in epi_accs]
    out_shape = [jax.ShapeDtypeStruct((m, width), dt) for dt, width, _ in outs3]
    out_shape += [jax.ShapeDtypeStruct(s, dt) for s, dt in epi_accs]
    sem = ("arbitrary", "arbitrary", "arbitrary") if n_acc else ("parallel", "parallel", "arbitrary")
    return pl.pallas_call(
        body, name=name, interpret=False, out_shape=out_shape,
        grid=(m // tm, nj, nk), in_specs=in_specs, out_specs=out_specs,
        scratch_shapes=[pltpu.VMEM((tm, tn) if nk > 1 else (8, 128), F32)],
        compiler_params=_params(sem),
    )(a, b, *[t for t, _ in epi_ins], *epi_consts)


def cols(arr, tr, width=None, cb=0):
    width = width or arr.shape[1]
    return (arr, (tr, width), lambda i: (i, cb), "r2")


def heads(arr, tr):
    return (arr, (arr.shape[0], tr, arr.shape[2]), lambda i: (0, i, 0), "r3")


def whole(arr):
    nd = arr.ndim
    return (arr, arr.shape, lambda i: (0,) * nd, "w")


STRIP = 16


def _rows_of(ref, kind, r0, n):
    if kind == "r2":
        return ref[pl.ds(r0, n), :]
    if kind == "r3":
        return ref[:, pl.ds(r0, n), :]
    return ref[...]


def _set_rows(ref, kind, r0, n, v):
    if kind == "r2":
        ref[pl.ds(r0, n), :] = v.astype(ref.dtype)
    elif kind == "r3":
        ref[:, pl.ds(r0, n), :] = v.astype(ref.dtype)
    else:
        ref[...] = v.astype(ref.dtype)


def rowwise(fn, ins, outs, *, steps, name, accs=(), strip=None):
    n_in, n_out, n_acc = len(ins), len(outs), len(accs)
    kin = [t[3] for t in ins]
    kout = [t[4] for t in outs]
    tr = next((t[1][-2] for t in ins if t[3] != "w"), 0)

    def body(*refs):
        i = pl.program_id(0)
        in_refs, out_refs, acc_refs = refs[:n_in], refs[n_in:n_in + n_out], refs[n_in + n_out:]
        if n_acc:
            @pl.when(i == 0)
            def _():
                for r in acc_refs:
                    r[...] = jnp.zeros(r.shape, r.dtype)

        def run(r0, n):
            res = fn(i * tr + r0, *[_rows_of(r, k, r0, n) for r, k in zip(in_refs, kin)])
            if not isinstance(res, (tuple, list)):
                res = (res,)
            for r, k, v in zip(out_refs, kout, res[:n_out]):
                _set_rows(r, k, r0, n, v)
            for r, v in zip(acc_refs, res[n_out:]):
                r[...] += jnp.broadcast_to(v, r.shape).astype(r.dtype)

        if strip is None or tr <= strip:
            run(0, tr)
        else:
            def step(s, carry):
                run(pl.multiple_of(s * strip, strip), strip)
                return carry
            lax.fori_loop(0, tr // strip, step, 0)

    def zmap(nd):
        return lambda i: (0,) * nd

    in_specs = [pl.BlockSpec(t[1], t[2]) for t in ins]
    out_specs = [pl.BlockSpec(t[2], t[3]) for t in outs]
    out_specs += [pl.BlockSpec(s, zmap(len(s))) for s, _ in accs]
    out_shape = [jax.ShapeDtypeStruct(t[0], t[1]) for t in outs]
    out_shape += [jax.ShapeDtypeStruct(s, d) for s, d in accs]
    res = pl.pallas_call(
        body, name=name, interpret=False, out_shape=out_shape, grid=(steps,),
        in_specs=in_specs, out_specs=out_specs,
        compiler_params=_params(("arbitrary",)),
    )(*[t[0] for t in ins])
    return res


def out2d(rows, width, dtype, tr):
    return ((rows, width), dtype, (tr, width), lambda i: (i, 0), "r2")


def conv_fwd(xs, w8, kw, *, rows, c, tc, tr, name, post, extras=(), outs=(), pre=None, strip=STRIP):
    nx, ne, no = len(xs), len(extras), len(outs)
    nr, nc = rows // tr, c // tc
    r8 = tr // 8
    st = strip

    def body(*refs):
        x_refs = refs[:2 * nx]
        w_ref = refs[2 * nx]
        e_refs = refs[2 * nx + 1:2 * nx + 1 + ne]
        o_refs = refs[2 * nx + 1 + ne:2 * nx + 1 + ne + no]
        scr = refs[-1]
        j, i = pl.program_id(0), pl.program_id(1)
        halo = [x_refs[2 * q + 1][...].astype(F32) for q in range(nx)]
        scr[0:8, :] = jnp.where(i > 0, pre(*halo) if pre else halo[0], 0.0)

        def fill(s, carry):
            r0 = pl.multiple_of(s * st, st)
            cur = [x_refs[2 * q][pl.ds(r0, st), :].astype(F32) for q in range(nx)]
            scr[pl.ds(8 + r0, st), :] = pre(*cur) if pre else cur[0]
            return carry

        def comp(s, carry):
            r0 = pl.multiple_of(s * st, st)
            win = scr[pl.ds(r0, st + 8), :]
            y = jnp.zeros((st, tc), F32)
            for q in range(kw):
                sh = kw - 1 - q
                y = y + w_ref[q:q + 1, :] * win[8 - sh:8 - sh + st]
            res = post(j, y, *[e[pl.ds(r0, st), :] for e in e_refs])
            if not isinstance(res, (tuple, list)):
                res = (res,)
            for r, v in zip(o_refs, res):
                r[pl.ds(r0, st), :] = v.astype(r.dtype)
            return carry

        lax.fori_loop(0, tr // st, fill, 0)
        lax.fori_loop(0, tr // st, comp, 0)

    in_specs, args = [], []
    for arr, cb0 in xs:
        in_specs.append(pl.BlockSpec((tr, tc), lambda j, i, cb0=cb0: (i, cb0 + j)))
        in_specs.append(pl.BlockSpec((8, tc), lambda j, i, cb0=cb0: (jnp.maximum(i * r8 - 1, 0), cb0 + j)))
        args += [arr, arr]
    in_specs.append(pl.BlockSpec((8, tc), lambda j, i: (0, j)))
    args.append(w8)
    for arr, cb0 in extras:
        in_specs.append(pl.BlockSpec((tr, tc), lambda j, i, cb0=cb0: (i, cb0 + j)))
        args.append(arr)
    return pl.pallas_call(
        body, name=name, interpret=False,
        out_shape=[jax.ShapeDtypeStruct((rows, c), dt) for dt in outs],
        grid=(nc, nr), in_specs=in_specs,
        out_specs=[pl.BlockSpec((tr, tc), lambda j, i: (i, j)) for _ in outs],
        scratch_shapes=[pltpu.VMEM((tr + 8, tc), F32)],
        compiler_params=_params(("parallel", "arbitrary")),
    )(*args)


def conv_bwd(xs, w8, kw, dy, *, rows, c, tc, tr, name, post, extras=(), outs=(), pre=None):
    nx, ne, no = len(xs), len(extras), len(outs)
    nr, nc = rows // tr, c // tc
    r8 = tr // 8

    def body(*refs):
        x_refs = refs[:2 * nx]
        w_ref, dy_ref, dyn_ref = refs[2 * nx:2 * nx + 3]
        e_refs = refs[2 * nx + 3:2 * nx + 3 + ne]
        first_out = 2 * nx + 3 + ne
        o_refs = refs[first_out:first_out + no]
        dw_ref = refs[first_out + no]
        xscr, gscr = refs[-2], refs[-1]
        i = pl.program_id(1)
        halo = [x_refs[2 * q + 1][...].astype(F32) for q in range(nx)]
        xscr[0:8, :] = jnp.where(i > 0, pre(*halo) if pre else halo[0], 0.0)
        gscr[tr:tr + 8, :] = jnp.where(i < nr - 1, dyn_ref[...].astype(F32), 0.0)

        def fill(s, carry):
            r0 = pl.multiple_of(s * STRIP, STRIP)
            cur = [x_refs[2 * q][pl.ds(r0, STRIP), :].astype(F32) for q in range(nx)]
            xscr[pl.ds(8 + r0, STRIP), :] = pre(*cur) if pre else cur[0]
            gscr[pl.ds(r0, STRIP), :] = dy_ref[pl.ds(r0, STRIP), :].astype(F32)
            return carry

        def comp(s, dws):
            r0 = pl.multiple_of(s * STRIP, STRIP)
            gwin = gscr[pl.ds(r0, STRIP + 8), :]
            xwin = xscr[pl.ds(r0, STRIP + 8), :]
            g = gwin[0:STRIP]
            dx = jnp.zeros((STRIP, tc), F32)
            new = []
            for q in range(kw):
                sh = kw - 1 - q
                dx = dx + w_ref[q:q + 1, :] * gwin[sh:sh + STRIP]
                part = g * xwin[8 - sh:8 - sh + STRIP]
                new.append(dws[q] + part[0:8] + part[8:16])
            res = post(dx, *[e[pl.ds(r0, STRIP), :] for e in e_refs])
            if not isinstance(res, (tuple, list)):
                res = (res,)
            for r, v in zip(o_refs, res):
                r[pl.ds(r0, STRIP), :] = v.astype(r.dtype)
            return tuple(new)

        lax.fori_loop(0, tr // STRIP, fill, 0)
        dws = lax.fori_loop(0, tr // STRIP, comp, tuple(jnp.zeros((8, tc), F32) for _ in range(kw)))

        @pl.when(i == 0)
        def _():
            dw_ref[...] = jnp.zeros((8, tc), F32)

        dw_ref[...] += jnp.concatenate([jnp.sum(t, axis=0, keepdims=True) for t in dws]
                                       + [jnp.zeros((8 - kw, tc), F32)], axis=0)

    in_specs, args = [], []
    for arr, cb0 in xs:
        in_specs.append(pl.BlockSpec((tr, tc), lambda j, i, cb0=cb0: (i, cb0 + j)))
        in_specs.append(pl.BlockSpec((8, tc), lambda j, i, cb0=cb0: (jnp.maximum(i * r8 - 1, 0), cb0 + j)))
        args += [arr, arr]
    in_specs.append(pl.BlockSpec((8, tc), lambda j, i: (0, j)))
    in_specs.append(pl.BlockSpec((tr, tc), lambda j, i: (i, j)))
    in_specs.append(pl.BlockSpec((8, tc), lambda j, i: (jnp.minimum((i + 1) * r8, nr * r8 - 1), j)))
    args += [w8, dy, dy]
    for arr, cb0 in extras:
        in_specs.append(pl.BlockSpec((tr, tc), lambda j, i, cb0=cb0: (i, cb0 + j)))
        args.append(arr)
    return pl.pallas_call(
        body, name=name, interpret=False,
        out_shape=[jax.ShapeDtypeStruct((rows, c), dt) for dt in outs] + [jax.ShapeDtypeStruct((8, c), F32)],
        grid=(nc, nr), in_specs=in_specs,
        out_specs=[pl.BlockSpec((tr, tc), lambda j, i: (i, j)) for _ in outs] + [pl.BlockSpec((8, tc), lambda j, i: (0, j))],
        scratch_shapes=[pltpu.VMEM((tr + 8, tc), F32), pltpu.VMEM((tr + 8, tc), F32)],
        compiler_params=_params(("parallel", "arbitrary")),
    )(*args)


def rms_fwd(h, w, *, name):
    rows = h.shape[0]
    tr = _pick(rows, (384, 128))

    def fn(i, x, wv):
        r = lax.rsqrt(jnp.mean(x * x, axis=1, keepdims=True) + EPS)
        return x * r * wv

    return rowwise(fn, [cols(h, tr), whole(w)], [out2d(rows, D, BF16, tr)], steps=rows // tr, name=name)[0]


def _rms_bwd_epi(row0, g, x, dr, wv):
    r = lax.rsqrt(jnp.mean(x * x, axis=1, keepdims=True) + EPS)
    xh = x * r
    gw = g * wv
    dx = r * (gw - xh * jnp.mean(gw * xh, axis=1, keepdims=True))
    row = row0 + lax.broadcasted_iota(jnp.int32, (x.shape[0], 1), 0)
    return jnp.where(row >= PAD, dr + dx, 0.0), jnp.sum(g * xh, axis=0, keepdims=True)


def dx_rms_bwd(dy, w, h, nw, dres, *, name, b_chip=False, swap_mid=False):
    return mm(dy, w, tb=True, b_chip=b_chip, swap_mid=swap_mid, tn=D, name=name, epi=_rms_bwd_epi,
              epi_ins=[(h, lambda j: 0), (dres, lambda j: 0)], epi_consts=[nw], epi_outs=[F32],
              epi_accs=[((1, D), F32)])


def loss_grad(h, target):
    rows = h.shape[0]

    def fn(i, y, t):
        diff = jnp.where(i >= HEAD0, y - t, 0.0)
        part = jnp.sum(jnp.sum(diff * diff, axis=1, keepdims=True), axis=0, keepdims=True)
        return diff * (1.0 / D), part * (0.5 / D)

    tgt = (target, (BLK, D), lambda i: (jnp.maximum(i - 1, 0), 0), "r2")
    return rowwise(fn, [cols(h, BLK), tgt], [out2d(rows, D, F32, BLK)], steps=rows // BLK,
                   name="loss_grad", accs=[((1, 128), F32)])


def adamw(w, g, m, v, *, name):
    shape = w.shape
    gs = list(g) if isinstance(g, (list, tuple)) else [g]
    nl = len(gs)
    width = shape[-1]
    rows = w.size // width
    rl = rows // nl
    tr = _pick(rl, (256, 176, 128, 64, 16, 8))
    nr = rl // tr
    if w.ndim == 3 and shape[1] % tr == 0:
        per = shape[1] // tr
        view = lambda t: (t, (None, tr, width), lambda i: (i // per, i % per, 0), "r2")
        out = (shape, F32, (None, tr, width), lambda i: (i // per, i % per, 0), "r2")
    else:
        view = lambda t: cols(t.reshape(rows, width), tr)
        out = out2d(rows, width, F32, tr)

    def fn(i, wv, mv, vv, *gvs):
        gv = gvs[0]
        for layer in range(1, nl):
            gv = jnp.where(i >= layer * rl, gvs[layer], gv)
        mn = B1 * mv + (1.0 - B1) * gv
        vn = B2 * vv + (1.0 - B2) * gv * gv
        mh = mn / (1.0 - B1 ** STEP)
        vh = vn / (1.0 - B2 ** STEP)
        return -LR * (mh / (jnp.sqrt(vh) + AEPS) + WD * wv), mn, vn, gv

    g_ins = [(t.reshape(rl, width), (tr, width), lambda i, layer=layer: (jnp.clip(i - layer * nr, 0, nr - 1), 0), "r2")
             for layer, t in enumerate(gs)]
    res = rowwise(fn, [view(t) for t in (w, m, v)] + g_ins, [out] * 4, steps=rows // tr, name=name)
    return [r.reshape(shape) for r in res]


HB = DN_H * CH
PAIR = 2


def _split(a):
    hi = a.astype(BF16)
    return hi, (a - hi.astype(F32)).astype(BF16)


def _dot1(a, b, ca=1, cb=0):
    return _dot(a.astype(BF16), b.astype(BF16), ca, cb)


def _dot3(a, b, ca=1, cb=0):
    ah, al = _split(a)
    bh, bl = _split(b)
    return _dot(ah, bh, ca, cb) + (_dot(ah, bl, ca, cb) + _dot(al, bh, ca, cb))


def _dot01(m01, b, ca=1, cb=0):
    bh, bl = _split(b)
    m = m01.astype(BF16)
    return _dot(m, bh, ca, cb) + _dot(m, bl, ca, cb)


def _stack(x):
    return jnp.concatenate([x[:, h * DN_D:(h + 1) * DN_D] for h in range(DN_H)], axis=0)


def _unstack(x):
    return jnp.concatenate([x[h * CH:(h + 1) * CH] for h in range(DN_H)], axis=1)


def _tri_inv(a, blk, eye):
    ad = jnp.where(blk, a, 0.0)
    lo = a - ad
    a2 = _dot3(ad, ad)
    a4 = _dot3(a2, a2)
    a8 = _dot3(a4, a4)
    dgi = _dot3(_dot3(_dot3(eye - ad, eye + a2), eye + a4), eye + a8)
    n = _dot3(dgi, lo)
    return _dot3(_dot3(eye - n, eye + _dot3(n, n)), dgi)


def _dn_masks():
    row = lax.broadcasted_iota(jnp.int32, (HB, HB), 0)
    col = lax.broadcasted_iota(jnp.int32, (HB, HB), 1)
    same = (row // CH) == (col // CH)
    incl = jnp.logical_and(same, row >= col)
    strict = jnp.logical_and(same, row > col)
    upper = jnp.logical_and(same, row <= col)
    blk = (row // 16) == (col // 16)
    eye = (row == col).astype(F32)
    return incl, strict, upper, blk, eye


def _dn_chunk(qv, kv, vv, bc, br, incl, strict):
    r64 = lax.broadcasted_iota(jnp.int32, (CH, CH), 0)
    c64 = lax.broadcasted_iota(jnp.int32, (CH, CH), 1)
    dcol = _dot01((r64 >= c64).astype(F32), bc)
    drow = _dot3(br, (r64 <= c64).astype(F32))
    col = lambda m, l0: jnp.concatenate([m[:, l0 + h:l0 + h + 1] for h in range(DN_H)], axis=0)
    b_c = col(bc, 0)
    d_c = col(dcol, 4)
    d_r = jnp.concatenate([drow[4 + h:5 + h, :] for h in range(DN_H)], axis=1)
    d_last_h = [dcol[CH - 1:CH, 4 + h:5 + h] for h in range(DN_H)]
    d_last = jnp.concatenate([jnp.broadcast_to(t, (CH, 1)) for t in d_last_h], axis=0)
    q, k, v = _stack(qv), _stack(kv), _stack(vv)
    dm = jnp.where(incl, jnp.exp(jnp.where(incl, d_c - d_r, 0.0)), 0.0)
    kk = _dot1(k, k, 1, 1)
    a = jnp.where(strict, b_c * kk * dm, 0.0)
    ed = jnp.exp(d_c)
    rhs = jnp.concatenate([v * b_c, k * (b_c * ed)], axis=1)
    qk = _dot1(q, k, 1, 1) * dm
    ekd = jnp.exp(d_last - d_c)
    gl = [jnp.exp(t) for t in d_last_h]
    return q, k, v, b_c, dm, kk, a, ed, rhs, qk, ekd, gl


def dn_fwd(qkv_n, bgcol, bgrow):
    rows = qkv_n.shape[0]
    nch = rows // CH

    def body(q_ref, k_ref, v_ref, bc_ref, br_ref, o_ref, s_out, ti_out, s_scr, prep, prep_qk, prep_gl):
        n = pl.program_id(0)

        @pl.when(n == 0)
        def _():
            s_scr[...] = jnp.zeros(s_scr.shape, F32)
            prep[...] = jnp.zeros(prep.shape, F32)
            prep_qk[...] = jnp.zeros(prep_qk.shape, F32)
            prep_gl[...] = jnp.zeros(prep_gl.shape, F32)

        live = n > 0
        for c in range(PAIR):
            u, w, qd, kd = prep[c, 0], prep[c, 1], prep[c, 2], prep[c, 3]
            v_new, o_state = [], []
            for h in range(DN_H):
                rs = slice(h * CH, (h + 1) * CH)
                s = s_scr[h]
                s_out[c, h] = s
                vn = u[rs] - _dot1(w[rs], s)
                v_new.append(vn)
                o_state.append(_dot1(qd[rs], s))
                s_scr[h] = jnp.where(live, prep_gl[c, h:h + 1, 0:1] * s + _dot1(kd[rs], vn, 0, 0), s)
            o = jnp.concatenate(o_state, axis=0) + _dot1(prep_qk[c], jnp.concatenate(v_new, axis=0))
            o_ref[c * CH:(c + 1) * CH, :] = _unstack(o)

        incl, strict, _, blk, eye = _dn_masks()
        for c in range(PAIR):
            rows_c = slice(c * CH, (c + 1) * CH)
            q, k, v, b_c, dm, kk, a, ed, rhs, qk_n, ekd, gl = _dn_chunk(
                q_ref[rows_c, :], k_ref[rows_c, :], v_ref[rows_c, :], bc_ref[rows_c, :], br_ref[c], incl, strict)
            tinv = _tri_inv(a, blk, eye)
            ti_out[c] = tinv
            sol = _dot3(tinv, rhs)
            prep[c, 0] = sol[:, :DN_D]
            prep[c, 1] = sol[:, DN_D:]
            prep[c, 2] = q * ed
            prep[c, 3] = k * ekd
            prep_qk[c] = qk_n
            prep_gl[c] = jnp.concatenate([jnp.broadcast_to(t, (1, 128)) for t in gl]
                                         + [jnp.zeros((8 - DN_H, 128), F32)], axis=0)

    assert nch % PAIR == 0
    npair = nch // PAIR
    last = npair - 1
    return pl.pallas_call(
        body, name="dn_fwd", interpret=False,
        out_shape=[jax.ShapeDtypeStruct((rows, DN_DIM), F32),
                   jax.ShapeDtypeStruct((nch, DN_H, DN_D, DN_D), F32),
                   jax.ShapeDtypeStruct((nch, HB, HB), F32)],
        grid=(npair + 1,),
        in_specs=[pl.BlockSpec((PAIR * CH, DN_DIM), lambda n: (jnp.minimum(n, last), 0)),
                  pl.BlockSpec((PAIR * CH, DN_DIM), lambda n: (jnp.minimum(n, last), 1)),
                  pl.BlockSpec((PAIR * CH, DN_DIM), lambda n: (jnp.minimum(n, last), 2)),
                  pl.BlockSpec((PAIR * CH, 128), lambda n: (jnp.minimum(n, last), 0)),
                  pl.BlockSpec((PAIR, 8, CH), lambda n: (jnp.minimum(n, last), 0, 0))],
        out_specs=[pl.BlockSpec((PAIR * CH, DN_DIM), lambda n: (jnp.maximum(n - 1, 0), 0)),
                   pl.BlockSpec((PAIR, DN_H, DN_D, DN_D), lambda n: (jnp.maximum(n - 1, 0), 0, 0, 0)),
                   pl.BlockSpec((PAIR, HB, HB), lambda n: (jnp.minimum(n, last), 0, 0))],
        scratch_shapes=[pltpu.VMEM((DN_H, DN_D, DN_D), F32), pltpu.VMEM((PAIR, 4, HB, DN_D), F32),
                        pltpu.VMEM((PAIR, HB, HB), F32), pltpu.VMEM((PAIR, 8, 128), F32)],
        compiler_params=_params(("arbitrary",)),
    )(qkv_n, qkv_n, qkv_n, bgcol, bgrow)


def dn_bwd(qkv_n, bgcol, bgrow, s_all, ti_all, do):
    rows = qkv_n.shape[0]
    nch = rows // CH

    def body(q_ref, k_ref, v_ref, bc_ref, br_ref, s_ref, ti_ref, do_ref, dq_ref, dk_ref, dv_ref, dbg_ref, ds_scr):
        n = pl.program_id(0)

        @pl.when(n == 0)
        def _():
            ds_scr[...] = jnp.zeros(ds_scr.shape, F32)

        incl, strict, upper, _, _ = _dn_masks()
        q, k, v, b_c, dm, kk, a, ed, rhs, qk, ekd, gl = _dn_chunk(q_ref[...], k_ref[...], v_ref[...], bc_ref[...],
                                                                  br_ref[0], incl, strict)
        tinv = ti_ref[0]
        g_o = _stack(do_ref[...])
        sol = _dot3(tinv, rhs)
        u, w = sol[:, :DN_D], sol[:, DN_D:]
        qd, kd = q * ed, k * ekd
        rsum = lambda t: jnp.sum(t, axis=1, keepdims=True)
        rows_of = [slice(h * CH, (h + 1) * CH) for h in range(DN_H)]
        s_h = [s_ref[0, h] for h in range(DN_H)]
        ds_h = [ds_scr[h] for h in range(DN_H)]
        v_new = jnp.concatenate([u[rs] - _dot1(w[rs], s) for rs, s in zip(rows_of, s_h)], axis=0)
        dv_new = _dot1(qk, g_o, 0, 0) + jnp.concatenate([_dot1(kd[rs], t) for rs, t in zip(rows_of, ds_h)], axis=0)
        dqd = jnp.concatenate([_dot1(g_o[rs], s, 1, 1) for rs, s in zip(rows_of, s_h)], axis=0)
        dkd = jnp.concatenate([_dot1(v_new[rs], t, 1, 1) for rs, t in zip(rows_of, ds_h)], axis=0)
        for h, rs in enumerate(rows_of):
            ds_scr[h] = _dot1(qd[rs], g_o[rs], 0, 0) + gl[h] * ds_h[h] - _dot1(w[rs], dv_new[rs], 0, 0)
        dw = jnp.concatenate([-_dot1(dv_new[rs], s, 1, 1) for rs, s in zip(rows_of, s_h)], axis=0)
        dqk = _dot1(g_o, v_new, 1, 1)
        drhs = _dot3(tinv, jnp.concatenate([dv_new, dw], axis=1), 0, 0)
        da = jnp.where(strict, -_dot1(drhs, sol, 1, 1), 0.0)
        drhs_u, drhs_w = drhs[:, :DN_D], drhs[:, DN_D:]
        s2 = rsum(drhs_w * k)
        dbeta = rsum(drhs_u * v) + s2 * ed + rsum(da * kk * dm)
        dkk = da * b_c * dm
        dqkr = dqk * dm
        mmat = da * a + dqk * qk
        tmp = rsum(dkd * kd)
        dd = (s2 * b_c * ed + rsum(mmat) - _dot3(mmat, jnp.ones((HB, 128), F32), 0, 0)[:, :1] + rsum(dqd * qd) - tmp)
        rowi = lax.broadcasted_iota(jnp.int32, (CH, 1), 0)
        last = []
        for h, rs in enumerate(rows_of):
            dgl = jnp.sum(rsum(s_h[h] * ds_h[h]), axis=0, keepdims=True)
            dd_last = jnp.sum(tmp[rs], axis=0, keepdims=True) + dgl * gl[h]
            last.append(jnp.where(rowi == CH - 1, dd_last, 0.0))
        dd = dd + jnp.concatenate(last, axis=0)
        dq_ref[...] = _unstack(_dot1(dqkr, k) + dqd * ed)
        dk_ref[...] = _unstack(drhs_w * (b_c * ed) + _dot1(dkk, k) + _dot1(dkk, k, 0, 0) + _dot1(dqkr, q, 0, 0)
                               + dkd * ekd)
        dv_ref[...] = _unstack(drhs_u * b_c)
        dg = _dot01(upper.astype(F32), jnp.broadcast_to(dd, (HB, 128)))[:, :1]
        lane = lax.broadcasted_iota(jnp.int32, (CH, 128), 1)
        out = jnp.zeros((CH, 128), F32)
        for h, rs in enumerate(rows_of):
            out = out + jnp.where(lane == h, dbeta[rs], 0.0) + jnp.where(lane == 4 + h, dg[rs], 0.0)
        dbg_ref[...] = out

    rev = lambda n: nch - 1 - n
    return pl.pallas_call(
        body, name="dn_bwd", interpret=False,
        out_shape=[jax.ShapeDtypeStruct((rows, DN_DIM), F32)] * 3 + [jax.ShapeDtypeStruct((rows, 128), F32)],
        grid=(nch,),
        in_specs=[pl.BlockSpec((CH, DN_DIM), lambda n: (rev(n), 0)),
                  pl.BlockSpec((CH, DN_DIM), lambda n: (rev(n), 1)),
                  pl.BlockSpec((CH, DN_DIM), lambda n: (rev(n), 2)),
                  pl.BlockSpec((CH, 128), lambda n: (rev(n), 0)),
                  pl.BlockSpec((1, 8, CH), lambda n: (rev(n), 0, 0)),
                  pl.BlockSpec((1, DN_H, DN_D, DN_D), lambda n: (rev(n), 0, 0, 0)),
                  pl.BlockSpec((1, HB, HB), lambda n: (rev(n), 0, 0)),
                  pl.BlockSpec((CH, DN_DIM), lambda n: (rev(n), 0))],
        out_specs=[pl.BlockSpec((CH, DN_DIM), lambda n: (rev(n), 0))] * 3 + [pl.BlockSpec((CH, 128), lambda n: (rev(n), 0))],
        scratch_shapes=[pltpu.VMEM((DN_H, DN_D, DN_D), F32)],
        compiler_params=_params(("arbitrary",)),
    )(qkv_n, qkv_n, qkv_n, bgcol, bgrow, s_all, ti_all, do)


def _swa_valid(n):
    c3 = lax.broadcasted_iota(jnp.int32, (NKEY, 4 * BLK), 0)
    r = lax.broadcasted_iota(jnp.int32, (NKEY, 4 * BLK), 1) % BLK
    prev0 = N_META + BLK
    c = jnp.where(c3 < N_META, PAD + c3, jnp.where(c3 < prev0, c3 - N_META, c3 - prev0))
    lo = jnp.where(c3 < N_META, 0, jnp.where(c3 < prev0, r + 1 + jnp.where(n >= 2, 0, BLK), 0))
    hi = jnp.where(c3 < N_META, r + jnp.where(n >= 1, BLK, 0),
                   jnp.where(c3 < prev0, BLK, r - jnp.where(n >= 1, 0, BLK)))
    return jnp.logical_and(c >= lo, c <= hi)


def _swa_probs(q, kcat, valid, sink):
    s = jnp.where(valid, _dot(kcat, q, 1, 1), -1e30)
    m = jnp.maximum(jnp.max(s, axis=0, keepdims=True), sink)
    e = jnp.where(valid, jnp.exp(s - m), 0.0)
    es = jnp.exp(sink - m)
    inv = 1.0 / (jnp.sum(e, axis=0, keepdims=True) + es)
    return e * inv, es * inv


def _swa_group(q_ref, sk_ref, h):
    q4 = jnp.concatenate([q_ref[4 * h + g] for g in range(4)], axis=0)
    sink4 = jnp.concatenate([jnp.full((1, BLK), sk_ref[4 * h + g], F32) for g in range(4)], axis=1)
    return q4, sink4


def _swa_specs():
    q = pl.BlockSpec((SWA_H, BLK, SWA_D), lambda n: (0, n, 0))
    km = pl.BlockSpec((SWA_KV, N_META, SWA_D), lambda n: (0, PAD // N_META, 0))
    kp = pl.BlockSpec((SWA_KV, BLK, SWA_D), lambda n: (0, jnp.maximum(n - 1, 0), 0))
    kc = pl.BlockSpec((SWA_KV, BLK, SWA_D), lambda n: (0, n, 0))
    return [q, km, kp, kc, km, kp, kc]


def swa_fwd(qh, kh, vh, sinks):
    rows = qh.shape[1]
    nb = rows // BLK

    def body(q_ref, km, kp, kc, vm, vp, vc, sk_ref, o_ref):
        n = pl.program_id(0)
        valid = _swa_valid(n)
        outs = []
        for h in range(SWA_KV):
            kcat = jnp.concatenate([km[h], kp[h], kc[h]], axis=0)
            vcat = jnp.concatenate([vm[h], vp[h], vc[h]], axis=0)
            q4, sink4 = _swa_group(q_ref, sk_ref, h)
            p, _ = _swa_probs(q4, kcat, valid, sink4)
            o4 = _dot(p.astype(BF16), vcat, 0, 0)
            outs += [o4[g * BLK:(g + 1) * BLK] for g in range(4)]
        o_ref[...] = jnp.concatenate(outs, axis=1).astype(BF16)

    return pl.pallas_call(
        body, name="swa_fwd", interpret=False,
        out_shape=jax.ShapeDtypeStruct((rows, SWA_H * SWA_D), BF16),
        grid=(nb,),
        in_specs=_swa_specs() + [pl.BlockSpec(memory_space=pltpu.SMEM)],
        out_specs=pl.BlockSpec((BLK, SWA_H * SWA_D), lambda n: (n, 0)),
        compiler_params=_params(("parallel",)),
    )(qh, kh, kh, kh, vh, vh, vh, sinks)


def swa_bwd(qh, kh, vh, sinks, do):
    rows = qh.shape[1]
    nb = rows // BLK

    def body(q_ref, km, kp, kc, vm, vp, vc, do_ref, sk_ref, dq_ref, dk_ref, dv_ref, dsk_ref):
        n = pl.program_id(0)

        @pl.when(n == 0)
        def _():
            dk_ref[...] = jnp.zeros(dk_ref.shape, F32)
            dv_ref[...] = jnp.zeros(dv_ref.shape, F32)

        valid = _swa_valid(n)
        g_all = do_ref[...]
        rowi = lax.broadcasted_iota(jnp.int32, (SWA_H, 128), 0)
        dsk = jnp.zeros((SWA_H, 128), F32)
        pm = pl.multiple_of(jnp.maximum(n - 1, 0) * BLK, BLK)
        pc = pl.multiple_of(n * BLK, BLK)
        for h in range(SWA_KV):
            kcat = jnp.concatenate([km[h], kp[h], kc[h]], axis=0)
            vcat = jnp.concatenate([vm[h], vp[h], vc[h]], axis=0)
            q4, sink4 = _swa_group(q_ref, sk_ref, h)
            p, ps = _swa_probs(q4, kcat, valid, sink4)
            g4 = jnp.concatenate([g_all[:, (4 * h + g) * SWA_D:(4 * h + g + 1) * SWA_D] for g in range(4)], axis=0)
            dp = _dot(vcat, g4, 1, 1)
            delta = jnp.sum(p * dp, axis=0, keepdims=True)
            ds = (p * (dp - delta)).astype(BF16)
            dq4 = _dot(ds, kcat, 0, 0)
            dkc = _dot(ds, q4)
            dvc = _dot(p.astype(BF16), g4)
            t = ps * delta
            for g in range(4):
                dq_ref[4 * h + g] = dq4[g * BLK:(g + 1) * BLK]
                part = -jnp.sum(t[:, g * BLK:(g + 1) * BLK], axis=1, keepdims=True)
                dsk = dsk + jnp.where(rowi == 4 * h + g, part, 0.0)
            lanes = slice(h * SWA_D, (h + 1) * SWA_D)
            for ref, val in ((dk_ref, dkc), (dv_ref, dvc)):
                ref[PAD:BLK, lanes] += val[0:N_META]
                ref[pl.ds(pm, BLK), lanes] += val[N_META:N_META + BLK]
                ref[pl.ds(pc, BLK), lanes] += val[N_META + BLK:]
        dsk_ref[0] = dsk

    return pl.pallas_call(
        body, name="swa_bwd", interpret=False,
        out_shape=[jax.ShapeDtypeStruct((SWA_H, rows, SWA_D), F32),
                   jax.ShapeDtypeStruct((rows, SWA_KV * SWA_D), F32),
                   jax.ShapeDtypeStruct((rows, SWA_KV * SWA_D), F32),
                   jax.ShapeDtypeStruct((nb, SWA_H, 128), F32)],
        grid=(nb,),
        in_specs=_swa_specs() + [pl.BlockSpec((BLK, SWA_H * SWA_D), lambda n: (n, 0)),
                                 pl.BlockSpec(memory_space=pltpu.SMEM)],
        out_specs=[pl.BlockSpec((SWA_H, BLK, SWA_D), lambda n: (0, n, 0)),
                   pl.BlockSpec((rows, SWA_KV * SWA_D), lambda n: (0, 0)),
                   pl.BlockSpec((rows, SWA_KV * SWA_D), lambda n: (0, 0)),
                   pl.BlockSpec((1, SWA_H, 128), lambda n: (n, 0, 0))],
        compiler_params=_params(("arbitrary",)),
    )(qh, kh, kh, kh, vh, vh, vh, do, sinks)


QK_W = (SWA_H + SWA_KV) * SWA_D


def _head_mean(t):
    r = lax.broadcasted_iota(jnp.int32, (128, 128), 0) // SWA_D
    c = lax.broadcasted_iota(jnp.int32, (128, 128), 1) // SWA_D
    blk = jnp.where(r == c, 1.0 / SWA_D, 0.0).astype(BF16)
    out = []
    for i in range(t.shape[1] // 128):
        hi, lo = _split(t[:, 128 * i:128 * (i + 1)])
        out.append(_dot(hi, blk) + _dot(lo, blk))
    return jnp.concatenate(out, axis=1)


def _qk_scales(qw, kw):
    scale = SWA_D ** -0.5
    wt = jnp.concatenate([jnp.tile(qw.astype(F32) * scale, (1, SWA_H)), jnp.tile(kw.astype(F32), (1, SWA_KV))], axis=1)
    st = jnp.concatenate([jnp.full((1, SWA_H * SWA_D), scale, F32), jnp.ones((1, SWA_KV * SWA_D), F32)], axis=1)
    return wt, st


def qknorm_fwd(qkv, qw, kw):
    rows = qkv.shape[0]
    tr = _pick(rows, (384, 128))
    wt, _ = _qk_scales(qw, kw)

    def fn(i, x, w):
        xq = x[:, :QK_W]
        y = xq * lax.rsqrt(_head_mean(xq * xq) + EPS) * w
        head = lambda t, j: t[:, j * SWA_D:(j + 1) * SWA_D][None]
        qo = jnp.concatenate([head(y, j) for j in range(SWA_H)], axis=0)
        ko = jnp.concatenate([head(y, SWA_H + j) for j in range(SWA_KV)], axis=0)
        vo = jnp.concatenate([head(x, SWA_H + SWA_KV + j) for j in range(SWA_KV)], axis=0)
        return qo, ko, vo

    hm = lambda nh: ((nh, rows, SWA_D), BF16, (nh, tr, SWA_D), lambda i: (0, i, 0), "r3")
    return rowwise(fn, [cols(qkv, tr), whole(wt)], [hm(SWA_H), hm(SWA_KV), hm(SWA_KV)],
                   steps=rows // tr, name="qknorm_fwd")


def qknorm_bwd(qkv, qw, kw, dqh, dk, dv):
    rows = qkv.shape[0]
    tr = _pick(rows, (384, 128))
    wt, st = _qk_scales(qw, kw)

    def fn(i, x, w, sc, dq, dkv, dvv):
        xq = x[:, :QK_W]
        dy = jnp.concatenate([dq[j] for j in range(SWA_H)] + [dkv], axis=1)
        r = lax.rsqrt(_head_mean(xq * xq) + EPS)
        xh = xq * r
        gw = dy * w
        dx = r * (gw - xh * _head_mean(gw * xh))
        return jnp.concatenate([dx, dvv], axis=1), jnp.sum(dy * sc * xh, axis=0, keepdims=True)

    dqkv, dw = rowwise(fn, [cols(qkv, tr), whole(wt), whole(st), heads(dqh, tr), cols(dk, tr), cols(dv, tr)],
                       [out2d(rows, 1536, BF16, tr)], steps=rows // tr, name="qknorm_bwd", accs=[((1, QK_W), F32)])
    dw = dw.reshape(SWA_H + SWA_KV, SWA_D)
    return dqkv, jnp.sum(dw[:SWA_H], axis=0, keepdims=True), jnp.sum(dw[SWA_H:], axis=0, keepdims=True)


def _place():
    return lax.axis_index("x"), lax.axis_index("y"), lax.axis_index("c")


ANY = pl.BlockSpec(memory_space=pl.ANY)


def _rcopy(ssem, rsem, k, src, dst, to):
    return pltpu.make_async_remote_copy(src_ref=src, dst_ref=dst, send_sem=ssem.at[k], recv_sem=rsem.at[k],
                                        device_id=to, device_id_type=MESH)


def gather_weights(shards, small):
    n = len(shards)
    halves = [t.shape[0] // 2 for t in shards]

    def body(*refs):
        s_refs, small_ref = refs[:n], refs[n]
        o_refs, osmall = refs[n + 1:2 * n + 1], refs[2 * n + 1]
        ssem, rsem, lsem = refs[2 * n + 2:]
        x, y, c = _place()
        me = 2 * x + y
        chips = [(1 - x, y), (x, 1 - y), (1 - x, 1 - y)]

        def half(k, s, hh):
            return o_refs[k].at[s, pl.ds(hh * halves[k], halves[k]), :]

        loc = pltpu.make_async_copy(small_ref, osmall.at[me], lsem)
        loc.start()
        sends = []
        for k in range(n):
            for j, (px, py) in enumerate(chips):
                sends.append(_rcopy(ssem, rsem, 6 * k + j, s_refs[k].at[pl.ds(c * halves[k], halves[k]), :],
                                    half(k, me, c), (px, py, c)))
        for j, (px, py) in enumerate(chips):
            sends.append(_rcopy(ssem, rsem, 6 * n + j, small_ref, osmall.at[me], (px, py, c)))
        for cp in sends:
            cp.start()
        for k in range(n):
            for j, (px, py) in enumerate(chips):
                s = 2 * px + py
                _rcopy(ssem, rsem, 6 * k + j, half(k, s, c), half(k, s, c), (x, y, c)).wait_recv()
                fwd = _rcopy(ssem, rsem, 6 * k + 3 + j, half(k, s, c), half(k, s, c), (x, y, 1 - c))
                fwd.start()
                sends.append(fwd)
        for k in range(n):
            for j, (px, py) in enumerate(chips):
                s = 2 * px + py
                _rcopy(ssem, rsem, 6 * k + 3 + j, half(k, s, 1 - c), half(k, s, 1 - c), (x, y, c)).wait_recv()
        for j, (px, py) in enumerate(chips):
            s = 2 * px + py
            _rcopy(ssem, rsem, 6 * n + j, osmall.at[s], osmall.at[s], (x, y, c)).wait_recv()
        for cp in sends:
            cp.wait_send()
        loc.wait()

    res = pl.pallas_call(
        body, name="gather_weights", interpret=False,
        out_shape=[jax.ShapeDtypeStruct((4,) + t.shape, t.dtype) for t in shards]
        + [jax.ShapeDtypeStruct((4, SW_ROWS, 1024), F32)],
        in_specs=[ANY] * (n + 1), out_specs=[ANY] * (n + 1),
        scratch_shapes=[pltpu.SemaphoreType.DMA((6 * n + 3,)), pltpu.SemaphoreType.DMA((6 * n + 3,)),
                        pltpu.SemaphoreType.DMA],
    )(*shards, small)
    return res[:n], res[n]


def _handshake(peers):
    barrier = pltpu.get_barrier_semaphore()
    for peer in peers:
        pl.semaphore_signal(barrier, inc=1, device_id=peer, device_id_type=MESH)
    pl.semaphore_wait(barrier, len(peers))


def gather_weights_beside(shards):
    n = len(shards)
    halves = [t.shape[0] // 2 for t in shards]

    def body(*refs):
        s_refs, o_refs, ssem, rsem = refs[:n], refs[n:2 * n], refs[2 * n], refs[2 * n + 1]
        x, y, c = _place()
        me = 2 * x + y
        chips = [(1 - x, y), (x, 1 - y), (1 - x, 1 - y)]
        _handshake([(px, py, c) for px, py in chips] + [(x, y, 1 - c)])

        def half(k, s, hh):
            return o_refs[k].at[s, pl.ds(hh * halves[k], halves[k]), :]

        sends = []
        for k in range(n):
            for j, (px, py) in enumerate(chips):
                sends.append(_rcopy(ssem, rsem, 6 * k + j, s_refs[k].at[pl.ds(c * halves[k], halves[k]), :],
                                    half(k, me, c), (px, py, c)))
        for cp in sends:
            cp.start()
        for k in range(n):
            for j, (px, py) in enumerate(chips):
                s = 2 * px + py
                _rcopy(ssem, rsem, 6 * k + j, half(k, s, c), half(k, s, c), (x, y, c)).wait_recv()
                fwd = _rcopy(ssem, rsem, 6 * k + 3 + j, half(k, s, c), half(k, s, c), (x, y, 1 - c))
                fwd.start()
                sends.append(fwd)
        for k in range(n):
            for j, (px, py) in enumerate(chips):
                s = 2 * px + py
                _rcopy(ssem, rsem, 6 * k + 3 + j, half(k, s, 1 - c), half(k, s, 1 - c), (x, y, c)).wait_recv()
        for cp in sends:
            cp.wait_send()

    return pl.kernel(
        body, name="gather_weights_beside",
        out_type=[jax.ShapeDtypeStruct((4,) + t.shape, t.dtype) for t in shards],
        mesh=plsc.ScalarSubcoreMesh(axis_name="sequencer", num_cores=1),
        scratch_types=[pltpu.SemaphoreType.DMA((6 * n,)), pltpu.SemaphoreType.DMA((6 * n,))],
        compiler_params=pltpu.CompilerParams(collective_id=1),
    )(*shards)


def swap_halves(gs, *, name):
    n = len(gs)

    def body(*refs):
        g_refs, o_refs, ssem, rsem = refs[:n], refs[n:2 * n], refs[2 * n], refs[2 * n + 1]
        x, y, c = _place()
        cps = []
        for k in range(n):
            hk = g_refs[k].shape[1] // 2
            cps.append(_rcopy(ssem, rsem, k, g_refs[k].at[:, pl.ds((1 - c) * hk, hk), :], o_refs[k], (x, y, 1 - c)))
        for cp in cps:
            cp.start()
        for cp in cps:
            cp.wait()

    return pl.pallas_call(
        body, name=name, interpret=False,
        out_shape=[jax.ShapeDtypeStruct((4, t.shape[1] // 2, t.shape[2]), t.dtype) for t in gs],
        in_specs=[ANY] * n, out_specs=[ANY] * n,
        scratch_shapes=[pltpu.SemaphoreType.DMA((n,)), pltpu.SemaphoreType.DMA((n,))],
    )(*gs)


def _sum_rows(hk):
    return _pick(hk, (512, 352, 256, 128))


def pair_sum(g, other, c_idx, *, name):
    _, hk, width = other.shape
    tr = _sum_rows(hk)
    nbk = hk // tr

    def body(c_ref, g_ref, o_ref, out_ref):
        out_ref[...] = (g_ref[...].astype(F32) + o_ref[...].astype(F32)).astype(BF16)

    return pl.pallas_call(
        body, name=name, interpret=False,
        out_shape=jax.ShapeDtypeStruct((4, hk, width), BF16),
        grid_spec=pltpu.PrefetchScalarGridSpec(
            num_scalar_prefetch=1, grid=(4, nbk),
            in_specs=[pl.BlockSpec((1, tr, width), lambda s, i, c_ref: (s, c_ref[0] * nbk + i, 0)),
                      pl.BlockSpec((1, tr, width), lambda s, i, c_ref: (s, i, 0))],
            out_specs=pl.BlockSpec((1, tr, width), lambda s, i, c_ref: (s, i, 0))),
        compiler_params=_params(("parallel", "parallel")),
    )(c_idx, g, other)


def chip_sum(p, got, idx, *, name):
    _, hk, width = got.shape
    tr = _sum_rows(hk)
    nbk = hk // tr

    def body(idx_ref, p_ref, g_ref, out_ref):
        acc = p_ref[0].astype(F32)
        for j in range(3):
            acc = acc + g_ref[j].astype(F32)
        out_ref[0] = acc

    return pl.pallas_call(
        body, name=name, interpret=False,
        out_shape=jax.ShapeDtypeStruct((2, hk, width), F32),
        grid_spec=pltpu.PrefetchScalarGridSpec(
            num_scalar_prefetch=1, grid=(nbk,),
            in_specs=[pl.BlockSpec((1, tr, width), lambda i, idx_ref: (idx_ref[0], i, 0)),
                      pl.BlockSpec((3, tr, width), lambda i, idx_ref: (0, i, 0))],
            out_specs=pl.BlockSpec((1, tr, width), lambda i, idx_ref: (idx_ref[1], i, 0))),
        compiler_params=_params(("parallel",)),
    )(idx, p, got)


def join_halves(qs):
    n = len(qs)

    def body(*refs):
        q_refs, o_refs, ssem, rsem = refs[:n], refs[n:2 * n], refs[2 * n], refs[2 * n + 1]
        x, y, c = _place()
        cps = [_rcopy(ssem, rsem, k, q_refs[k].at[c], o_refs[k].at[c], (x, y, 1 - c)) for k in range(n)]
        for cp in cps:
            cp.start()
        for k in range(n):
            _rcopy(ssem, rsem, k, q_refs[k].at[c], o_refs[k].at[1 - c], (x, y, 1 - c)).wait_recv()
        for cp in cps:
            cp.wait_send()

    return pl.pallas_call(
        body, name="join_halves", interpret=False,
        out_shape=[jax.ShapeDtypeStruct(t.shape, t.dtype) for t in qs],
        in_specs=[ANY] * n, out_specs=[ANY] * n, input_output_aliases={k: k for k in range(n)},
        scratch_shapes=[pltpu.SemaphoreType.DMA((n,)), pltpu.SemaphoreType.DMA((n,))],
    )(*qs)


def scatter_chips_beside(ps, cid, name):
    n = len(ps)

    def body(*refs):
        p_refs, o_refs, ssem, rsem = refs[:n], refs[n:2 * n], refs[2 * n], refs[2 * n + 1]
        x, y, c = _place()
        chips = [(1 - x, y), (x, 1 - y), (1 - x, 1 - y)]
        _handshake([(px, py, c) for px, py in chips])
        cps = [_rcopy(ssem, rsem, 3 * k + j, p_refs[k].at[2 * px + py], o_refs[k].at[j], (px, py, c))
               for k in range(n) for j, (px, py) in enumerate(chips)]
        for cp in cps:
            cp.start()
        for cp in cps:
            cp.wait()

    return pl.kernel(
        body, name=name, out_type=[jax.ShapeDtypeStruct((3,) + t.shape[1:], t.dtype) for t in ps],
        mesh=plsc.ScalarSubcoreMesh(axis_name="sequencer", num_cores=1),
        scratch_types=[pltpu.SemaphoreType.DMA((3 * n,)), pltpu.SemaphoreType.DMA((3 * n,))],
        compiler_params=pltpu.CompilerParams(collective_id=cid),
    )(*ps)


def reduce_begin(gs, names, c_idx, cid, tag):
    others = swap_halves(gs, name=f"swap_halves_{tag}")
    pairs = [pair_sum(g, o, c_idx, name=f"pair_sum_{nm}") for g, o, nm in zip(gs, others, names)]
    return pairs, scatter_chips_beside(pairs, cid, f"scatter_chips_{tag}")


def reduce_end(pairs, gots, names, idx):
    mine = [chip_sum(p, g, idx, name=f"chip_sum_{nm}") for p, g, nm in zip(pairs, gots, names)]
    return [q.reshape(2 * q.shape[1], q.shape[2]) for q in join_halves(mine)]


def gather_small(v):
    def body(v_ref, o_ref, ssem, rsem, lsem):
        x, y, c = _place()
        peers = []
        for k in range(1, 8):
            fx, fy, fc = (k >> 2) & 1, (k >> 1) & 1, k & 1
            peers.append((1 - x if fx else x, 1 - y if fy else y, 1 - c if fc else c))
        _handshake(peers)
        loc = pltpu.make_async_copy(v_ref, o_ref.at[4 * x + 2 * y + c], lsem)
        loc.start()
        cps = []
        for k, (px, py, pc) in enumerate(peers):
            cps.append((pltpu.make_async_remote_copy(
                src_ref=v_ref, dst_ref=o_ref.at[4 * x + 2 * y + c], send_sem=ssem.at[k], recv_sem=rsem.at[k],
                device_id=(px, py, pc), device_id_type=MESH), 4 * px + 2 * py + pc))
        for cp, _ in cps:
            cp.start()
        for k, (cp, peer) in enumerate(cps):
            pltpu.make_async_remote_copy(
                src_ref=v_ref, dst_ref=o_ref.at[peer], send_sem=ssem.at[k], recv_sem=rsem.at[k],
                device_id=(x, y, c), device_id_type=MESH).wait_recv()
        for cp, _ in cps:
            cp.wait_send()
        loc.wait()

    return pl.kernel(
        body, name="gather_small", out_type=jax.ShapeDtypeStruct((8, SV_ROWS, 1024), F32),
        mesh=plsc.ScalarSubcoreMesh(axis_name="sequencer", num_cores=1),
        scratch_types=[pltpu.SemaphoreType.DMA((7,)), pltpu.SemaphoreType.DMA((7,)), pltpu.SemaphoreType.DMA],
        compiler_params=pltpu.CompilerParams(collective_id=6),
    )(v)


def sum_slots(a):
    def fn(i, t):
        acc = t[0]
        for k in range(1, 8):
            acc = acc + t[k]
        return acc

    return rowwise(fn, [whole(a)], [((SV_ROWS, 1024), F32, (SV_ROWS, 1024), lambda i: (0, 0), "w")], steps=1,
                   name="sum_slots")[0]


def _head_rms(x, nw):
    xs, rs = [], []
    for h in range(DN_H):
        xh = x[:, h * DN_D:(h + 1) * DN_D]
        r = lax.rsqrt(jnp.mean(xh * xh, axis=1, keepdims=True) + EPS)
        xs.append(xh * r)
        rs.append(r)
    return xs, rs


def bg_fwd(p, alog, dtb):
    rows = p.shape[0]
    tr = _pick(rows, (384, 128))

    def fn(i, x, al, dt):
        lane = lax.broadcasted_iota(jnp.int32, x.shape, 1)
        row = i + lax.broadcasted_iota(jnp.int32, x.shape, 0)
        g = -jnp.exp(al) * _softplus(x + dt)
        out = jnp.where(lane < 4, _sigmoid(x), jnp.where(lane < 8, g, 0.0))
        return jnp.where(row >= PAD, out, 0.0)

    return rowwise(fn, [cols(p, tr, 128, BG0 // 128), whole(alog), whole(dtb)], [out2d(rows, 128, F32, tr)],
                   steps=rows // tr, name="bg_fwd")[0]


def bg_bwd(p, alog, dtb, dbg):
    rows = p.shape[0]
    tr = _pick(rows, (384, 128))

    def fn(i, x, al, dt, g_in):
        lane = lax.broadcasted_iota(jnp.int32, x.shape, 1)
        row = i + lax.broadcasted_iota(jnp.int32, x.shape, 0)
        live = row >= PAD
        is_b = jnp.logical_and(live, lane < 4)
        is_g = jnp.logical_and(live, jnp.logical_and(lane >= 4, lane < 8))
        beta = _sigmoid(x)
        ea = jnp.exp(al)
        g = -ea * _softplus(x + dt)
        dalpha = jnp.where(is_g, g_in * (-ea) * _sigmoid(x + dt), 0.0)
        dx = jnp.where(is_b, g_in * beta * (1.0 - beta), dalpha)
        dal = jnp.sum(jnp.where(is_g, g_in * g, 0.0), axis=0, keepdims=True)
        return jnp.concatenate([dx, jnp.zeros(x.shape, F32)], axis=1), dal, jnp.sum(dalpha, axis=0, keepdims=True)

    return rowwise(fn, [cols(p, tr, 128, BG0 // 128), whole(alog), whole(dtb), cols(dbg, tr)],
                   [out2d(rows, 256, BF16, tr)], steps=rows // tr, name="bg_bwd",
                   accs=[((1, 128), F32), ((1, 128), F32)])


def dn_qkv_post(j, y):
    xs = _silu(y)
    sc = jnp.where(j == 0, DN_D ** -0.5, 1.0)
    outs = []
    for h in range(DN_H):
        xh = xs[:, h * DN_D:(h + 1) * DN_D]
        r = lax.rsqrt(jnp.sum(xh * xh, axis=1, keepdims=True) + EPS)
        outs.append(jnp.where(j < 2, xh * r * sc, xh))
    return jnp.concatenate(outs, axis=1), y


def dn_qkv_bwd(cq, dq, dk, dv):
    rows = cq.shape[0]
    tr = _pick(rows, (384, 128))

    def fn(i, c0, c1, c2, g0, g1, g2):
        pieces = []
        for kind, (cv, g) in enumerate(((c0, g0), (c1, g1), (c2, g2))):
            xs = _silu(cv)
            if kind < 2:
                sc = DN_D ** -0.5 if kind == 0 else 1.0
                ds = []
                for h in range(DN_H):
                    sl = slice(h * DN_D, (h + 1) * DN_D)
                    xh, gh = xs[:, sl], g[:, sl]
                    r = lax.rsqrt(jnp.sum(xh * xh, axis=1, keepdims=True) + EPS)
                    xn = xh * r
                    ds.append(sc * r * (gh - xn * jnp.sum(gh * xn, axis=1, keepdims=True)))
                dxs = jnp.concatenate(ds, axis=1)
            else:
                dxs = g
            pieces.append(dxs * _dsilu(cv))
        return jnp.concatenate(pieces, axis=1)

    ins = [cols(cq, tr, DN_DIM, k) for k in range(3)] + [cols(t, tr) for t in (dq, dk, dv)]
    return rowwise(fn, ins, [out2d(rows, 3 * DN_DIM, F32, tr)], steps=rows // tr, name="dn_qkv_bwd")[0]


def dn_out_fwd(o, p, nw):
    rows = o.shape[0]
    tr = _pick(rows, (384, 128))

    def fn(i, ov, z, w):
        xs, _ = _head_rms(ov, w)
        return jnp.concatenate(xs, axis=1) * jnp.concatenate([w] * DN_H, axis=1) * _silu(z)

    return rowwise(fn, [cols(o, tr), cols(p, tr, DN_DIM, 6), whole(nw)], [out2d(rows, DN_DIM, BF16, tr)],
                   steps=rows // tr, name="dn_out_fwd")[0]


def dn_out_bwd(o, p, nw, dymix):
    rows = o.shape[0]
    tr = _pick(rows, (384, 128))

    def fn(i, ov, z, w, dy):
        xs, rs = _head_rms(ov, w)
        sz = _silu(z)
        dn = dy * sz
        dos, dw = [], jnp.zeros((1, DN_D), F32)
        for h in range(DN_H):
            sl = slice(h * DN_D, (h + 1) * DN_D)
            gw = dn[:, sl] * w
            dos.append(rs[h] * (gw - xs[h] * jnp.mean(gw * xs[h], axis=1, keepdims=True)))
            dw = dw + jnp.sum(dn[:, sl] * xs[h], axis=0, keepdims=True)
        n = jnp.concatenate(xs, axis=1) * jnp.concatenate([w] * DN_H, axis=1)
        return jnp.concatenate(dos, axis=1), dy * n * _dsilu(z), dw

    return rowwise(fn, [cols(o, tr), cols(p, tr, DN_DIM, 6), whole(nw), cols(dymix, tr, DN_DIM, 1)],
                   [out2d(rows, DN_DIM, F32, tr), out2d(rows, DN_DIM, BF16, tr)], steps=rows // tr,
                   name="dn_out_bwd", accs=[((1, DN_D), F32)])


def conv_a_pre_bwd(dymix, cv, p):
    rows = cv.shape[0]
    tr = _pick(rows, (384, 128))

    def fn(i, dy, c, go):
        return dy * c, dy * go

    return rowwise(fn, [cols(dymix, tr, D_CONV, 0), cols(cv, tr), cols(p, tr, D_CONV, 1)],
                   [out2d(rows, D_CONV, BF16, tr), out2d(rows, D_CONV, F32, tr)], steps=rows // tr,
                   name="conv_a_pre_bwd")


def _rows8(w):
    return jnp.pad(w.astype(F32), ((0, 8 - w.shape[0]), (0, 0)))


def _lanes(v, at):
    return jnp.pad(v.astype(F32), (at, 128 - at - v.shape[0]))[None]


def add_norm(a, w, h, next_nw, *, name):
    if next_nw is None:
        return mm(a, w, add=h, name=name), None
    return mm(a, w, name=name, epi=_add_norm_epi, epi_ins=[(h, lambda j: 0)], epi_consts=[next_nw],
              epi_outs=[F32, BF16])


def _add_norm_epi(row0, t, h, nw):
    x = t + h
    return x, x * lax.rsqrt(jnp.mean(x * x, axis=1, keepdims=True) + EPS) * nw


def ffn_up_conv(hn, w_up, cw8, *, name):
    rows = hn.shape[0]
    tn = w_up.shape[2]
    tm = _pick(rows, (384, 128))
    nr = rows // tm

    def body(x_ref, wg_ref, wv_ref, w_ref, ug_ref, uv_ref, gc_ref, a_ref, carry, scr):
        i = pl.program_id(1)
        x = x_ref[...]
        gate = _dot(x, wg_ref[...])
        val = _dot(x, wv_ref[...])
        ug_ref[...] = gate.astype(BF16)
        uv_ref[...] = val.astype(BF16)
        scr[0:8, :] = jnp.where(i > 0, carry[...], 0.0)
        scr[8:8 + tm, :] = gate
        carry[...] = gate[tm - 8:tm]
        y = jnp.zeros((tm, tn), F32)
        for q in range(3):
            sh = 2 - q
            y = y + w_ref[q:q + 1, :] * scr[8 - sh:8 - sh + tm, :]
        gc_ref[...] = y.astype(BF16)
        a_ref[...] = (_silu(y) * val).astype(BF16)

    half = pl.BlockSpec((tm, tn), lambda j, i: (i, j))
    return pl.pallas_call(
        body, name=name, interpret=False,
        out_shape=[jax.ShapeDtypeStruct((rows, D_FF), BF16)] * 4,
        grid=(D_FF // tn, nr),
        in_specs=[pl.BlockSpec((tm, D), lambda j, i: (i, 0)),
                  pl.BlockSpec((None, D, tn), lambda j, i: (j, 0, 0)),
                  pl.BlockSpec((None, D, tn), lambda j, i: (j + D_FF // tn, 0, 0)),
                  pl.BlockSpec((8, tn), lambda j, i: (0, j))],
        out_specs=[half] * 4,
        scratch_shapes=[pltpu.VMEM((8, tn), F32), pltpu.VMEM((tm + 8, tn), F32)],
        compiler_params=_params(("arbitrary", "arbitrary")),
    )(hn, w_up, w_up, cw8)


def ffn_down_bwd(dh, w_down, gc, uv, ug, cw8, *, name):
    rows = dh.shape[0]
    tn = D_FF // 2
    tm = _pick(rows, (384, 128))
    nr = rows // tm
    r8 = tm // 8

    def body(dh_ref, w_ref, gc_ref, uv_ref, ug_ref, halo_ref, cw_ref, du_ref, dw_ref, carry, gscr, xscr):
        ip = pl.program_id(1)
        i = nr - 1 - ip
        da = _dot(dh_ref[...].astype(BF16), w_ref[...], 1, 1)
        c, val = gc_ref[...].astype(F32), uv_ref[...].astype(F32)
        dgc = da * val * _dsilu(c)
        du_ref[:, tn:] = (da * _silu(c)).astype(BF16)
        gscr[0:tm, :] = dgc
        gscr[tm:tm + 8, :] = jnp.where(ip > 0, carry[...], 0.0)
        carry[...] = dgc[0:8]
        xscr[0:8, :] = jnp.where(i > 0, halo_ref[...].astype(F32), 0.0)
        xscr[8:8 + tm, :] = ug_ref[...].astype(F32)
        dx = jnp.zeros((tm, tn), F32)
        dws = []
        for q in range(3):
            sh = 2 - q
            dx = dx + cw_ref[q:q + 1, :] * gscr[sh:sh + tm, :]
            dws.append(jnp.sum(dgc * xscr[8 - sh:8 - sh + tm, :], axis=0, keepdims=True))
        du_ref[:, :tn] = dx.astype(BF16)

        @pl.when(ip == 0)
        def _():
            dw_ref[...] = jnp.zeros((8, tn), F32)

        dw_ref[...] += jnp.concatenate(dws + [jnp.zeros((5, tn), F32)], axis=0)

    rev = lambda ip: nr - 1 - ip
    tile = lambda arr: pl.BlockSpec((tm, tn), lambda j, ip: (rev(ip), j))
    return pl.pallas_call(
        body, name=name, interpret=False,
        out_shape=[jax.ShapeDtypeStruct((rows, 2 * D_FF), BF16), jax.ShapeDtypeStruct((8, D_FF), F32)],
        grid=(2, nr),
        in_specs=[pl.BlockSpec((tm, D), lambda j, ip: (rev(ip), 0)),
                  pl.BlockSpec((tn, D), lambda j, ip: (j, 0)),
                  tile(gc), tile(uv), tile(ug),
                  pl.BlockSpec((8, tn), lambda j, ip: (jnp.maximum(rev(ip) * r8 - 1, 0), j)),
                  pl.BlockSpec((8, tn), lambda j, ip: (0, j))],
        out_specs=[pl.BlockSpec((tm, 2 * tn), lambda j, ip: (rev(ip), j)),
                   pl.BlockSpec((8, tn), lambda j, ip: (0, j))],
        scratch_shapes=[pltpu.VMEM((8, tn), F32), pltpu.VMEM((tm + 8, tn), F32), pltpu.VMEM((tm + 8, tn), F32)],
        compiler_params=_params(("arbitrary", "arbitrary")),
    )(dh, w_down, gc, uv, ug, ug, cw8)


def ffn_fwd(h, hn, w_up, cw8, w_down, tag, next_nw):
    ug, uv, gc, a = ffn_up_conv(hn, w_up, cw8, name=f"ffn{tag}_up")
    out, hn_next = add_norm(a, w_down, h, next_nw, name=f"ffn{tag}_down")
    return out, hn_next, (hn, ug, uv, a, gc)


def ffn_bwd(h, nw, w_up, cw8, w_down, saved, dh, tag):
    hn, ug, uv, a, gc = saved
    du, d_cw = ffn_down_bwd(dh, w_down, gc, uv, ug, cw8, name=f"ffn{tag}_down_dx")
    d_w_down = mm(a, dh, ta=True, out_dtype=BF16, name=f"ffn{tag}_down_dw")
    dh_new, d_nw = dx_rms_bwd(du, w_up, h, nw, dh, name=f"ffn{tag}_up_dx", b_chip=True, swap_mid=True)
    d_w_up = mm(hn, du, ta=True, out_dtype=BF16, out_chip=True, swap_mid=True, name=f"ffn{tag}_up_dw")
    return dh_new, d_nw, d_w_up, d_cw, d_w_down


def mixer_fwd(h, nw, w_in, ca8, dc8, alog, dtb, dnw, w_out, tie=None, next_nw=None):
    rows = h.shape[0]
    tr = _pick(rows, (384, 128))
    hn = rms_fwd(h, nw, name="mix_norm")
    p = mm(hn, w_in, name="mix_in")
    y_a, cv = conv_fwd([(p, 0), (p, 2)], ca8, 3, rows=rows, c=D_CONV, tc=D_CONV, tr=tr, name="conv_a",
                       pre=lambda gi, ah: gi * ah, post=lambda j, y, go: (go * y, y), extras=[(p, 1)],
                       outs=[BF16, F32])
    qkv_n, cq = conv_fwd([(p, 3)], dc8, 4, rows=rows, c=3 * DN_DIM, tc=DN_DIM, tr=tr, name="dn_conv",
                         post=dn_qkv_post, outs=[F32, F32], strip=tr)
    bgcol = bg_fwd(p, alog, dtb)
    if tie is not None:
        bgcol = tie(bgcol)
    bgrow = bgcol[:, :8].reshape(rows // CH, CH, 8).transpose(0, 2, 1)
    o, s_all, ti_all = dn_fwd(qkv_n, bgcol, bgrow)
    y_b = dn_out_fwd(o, p, dnw)
    ymix = jnp.concatenate([y_a, y_b], axis=1)
    w_out = w_out() if callable(w_out) else w_out
    out, hn_next = add_norm(ymix, w_out, h, next_nw, name="mix_out")
    return out, hn_next, (hn, p, cv, qkv_n, cq, bgcol, bgrow, o, s_all, ti_all, ymix)


def mixer_bwd(h, nw, w_in, ca8, dc8, alog, dtb, dnw, w_out, saved, dh):
    hn, p, cv, qkv_n, cq, bgcol, bgrow, o, s_all, ti_all, ymix = saved
    rows = h.shape[0]
    tr = _pick(rows, (384, 128))
    dymix = mm(dh, w_out, tb=True, name="mix_out_dx")
    d_w_out = mm(ymix, dh, ta=True, out_dtype=BF16, name="mix_out_dw")
    do, dz, d_dnw = dn_out_bwd(o, p, dnw, dymix)
    dq, dk, dv, dbg = dn_bwd(qkv_n, bgcol, bgrow, s_all, ti_all, do)
    dbg_p, d_alog, d_dtb = bg_bwd(p, alog, dtb, dbg)
    dcq = dn_qkv_bwd(cq, dq, dk, dv)
    dqkv, d_dc = conv_bwd([(p, 3)], dc8, 4, dcq, rows=rows, c=3 * DN_DIM, tc=DN_DIM, tr=tr, name="dn_conv_bwd",
                          post=lambda dx: dx, outs=[BF16])
    dgo, dcv = conv_a_pre_bwd(dymix, cv, p)
    dgi, dah, d_ca = conv_bwd([(p, 0), (p, 2)], ca8, 3, dcv, rows=rows, c=D_CONV, tc=D_CONV, tr=tr,
                              name="conv_a_bwd", pre=lambda gi, ah: gi * ah,
                              post=lambda dm, gi, ah: (dm * ah, dm * gi), extras=[(p, 0), (p, 2)], outs=[BF16, BF16])
    dp = jnp.concatenate([dgi, dgo, dah, dqkv, dz, dbg_p], axis=1)
    dh_new, d_nw = dx_rms_bwd(dp, w_in, h, nw, dh, name="mix_in_dx")
    d_w_in = mm(hn, dp, ta=True, out_dtype=BF16, name="mix_in_dw")
    return dh_new, d_nw, d_w_in, d_ca, d_dc, d_alog, d_dtb, d_dnw, d_w_out


def swa_layer_fwd(h, hn, wqkv, qw, kw, sinks, wo, next_nw):
    qkv = mm(hn, wqkv, name="swa_qkv")
    qh, kh, vh = qknorm_fwd(qkv, qw, kw)
    att = swa_fwd(qh, kh, vh, sinks)
    out, hn_next = add_norm(att, wo, h, next_nw, name="swa_out")
    return out, hn_next, (hn, qkv, qh, kh, vh, att)


def swa_layer_bwd(h, nw, wqkv, qw, kw, sinks, wo, saved, dh):
    hn, qkv, qh, kh, vh, att = saved
    datt = mm(dh, wo, tb=True, out_dtype=BF16, name="swa_out_dx")
    d_wo = mm(att, dh, ta=True, out_dtype=BF16, name="swa_out_dw")
    dqh, dkh, dvh, dsk = swa_bwd(qh, kh, vh, sinks, datt)
    dqkv, d_qw, d_kw = qknorm_bwd(qkv, qw, kw, dqh, dkh, dvh)
    dh_new, d_nw = dx_rms_bwd(dqkv, wqkv, h, nw, dh, name="swa_qkv_dx")
    d_wqkv = mm(hn, dqkv, ta=True, out_dtype=BF16, name="swa_qkv_dw")
    d_sinks = jnp.sum(dsk[:, :, 0], axis=0)
    return dh_new, d_nw, d_wqkv, d_qw, d_kw, d_sinks, d_wo


BIG = ("mix_w_in", "mix_w_out", "swa_wq", "swa_wk", "swa_wv", "swa_wo", "ffn_w_up", "ffn_w_down")


def _flat_pad(parts, rows):
    v = jnp.concatenate([t.astype(F32).reshape(-1) for t in parts])
    return jnp.pad(v, (0, rows * 1024 - v.shape[0])).reshape(rows, 1024)


def _split_flat(flat, shapes):
    v = flat.reshape(-1)
    out, o = [], 0
    for s in shapes:
        n = 1
        for d_ in s:
            n *= d_
        out.append(v[o:o + n].reshape(s))
        o += n
    return out


def local_step(x0, target0, meta_full, anw, fnw, w_in, ca8, dc8, alog, dtb, dnw, qw, kw, sinks, fc8, late,
               begin=None, tie=None):
    begin = begin or (lambda tag, names, grads: None)
    h0 = jnp.concatenate([jnp.zeros((PAD, D), F32), meta_full, x0], axis=0)
    h1, hn1, s_mix = mixer_fwd(h0, anw[0], w_in, ca8, dc8, alog, dtb, dnw, lambda: late()[0], tie, fnw[0])
    w_out, wqkv, wo, w_up, w_down = late()
    h2, hn2, s_f0 = ffn_fwd(h1, hn1, w_up[0], fc8[0], w_down[0], 0, anw[1])
    h3, hn3, s_swa = swa_layer_fwd(h2, hn2, wqkv, qw, kw, sinks, wo, fnw[1])
    h4, _, s_f1 = ffn_fwd(h3, hn3, w_up[1], fc8[1], w_down[1], 1, None)
    dh, loss_l = loss_grad(h4, target0)
    dh, d_fnw1, d_up1, d_fc1, d_down1 = ffn_bwd(h3, fnw[1], w_up[1], fc8[1], w_down[1], s_f1, dh, 1)
    begin("ffn1", ("up1", "down1"), [d_up1, d_down1.reshape(4, 704, D)])
    dh, d_anw1, d_wqkv, d_qw, d_kw, d_sinks, d_wo = swa_layer_bwd(h2, anw[1], wqkv, qw, kw, sinks, wo, s_swa, dh)
    begin("swa", ("wq", "wk", "wv", "wo"),
          [d_wqkv[:, :D].reshape(4, 256, D), d_wqkv[:, D:D + 256].reshape(4, 256, 256),
           d_wqkv[:, D + 256:].reshape(4, 256, 256), d_wo.reshape(4, 256, D)])
    dh, d_fnw0, d_up0, d_fc0, d_down0 = ffn_bwd(h1, fnw[0], w_up[0], fc8[0], w_down[0], s_f0, dh, 0)
    begin("ffn0", ("up0", "down0"), [d_up0, d_down0.reshape(4, 704, D)])
    dh, d_anw0, d_w_in, d_ca, d_dc, d_alog, d_dtb, d_dnw, d_w_out = mixer_bwd(
        h0, anw[0], w_in, ca8, dc8, alog, dtb, dnw, w_out, s_mix, dh)
    begin("mix", ("w_in", "w_out"),
          [d_w_in[:, :IN_DIM].reshape(D, 4, 898).transpose(1, 0, 2), d_w_out.reshape(4, 256, D)])
    return (dh, loss_l, d_anw0, d_anw1, d_fnw0, d_fnw1, d_w_in, d_ca, d_dc, d_alog, d_dtb, d_dnw, d_w_out, d_wqkv,
            d_qw, d_kw, d_sinks, d_wo, d_up0, d_up1, d_fc0, d_fc1, d_down0, d_down1)


def kernel(x, meta_tokens, attn_norm_w, ffn_norm_w, mix_w_in, conv_a_w, dn_conv_w, dn_a_log, dn_dt_bias, dn_norm_w, mix_w_out, swa_wq, swa_wk, swa_wv, swa_q_norm_w, swa_k_norm_w, swa_sinks, swa_wo, ffn_w_up, ffn_conv_w, ffn_w_down, loss_target, m_meta_tokens, m_attn_norm_w, m_ffn_norm_w, m_mix_w_in, m_conv_a_w, m_dn_conv_w, m_dn_a_log, m_dn_dt_bias, m_dn_norm_w, m_mix_w_out, m_swa_wq, m_swa_wk, m_swa_wv, m_swa_q_norm_w, m_swa_k_norm_w, m_swa_sinks, m_swa_wo, m_ffn_w_up, m_ffn_conv_w, m_ffn_w_down, v_meta_tokens, v_attn_norm_w, v_ffn_norm_w, v_mix_w_in, v_conv_a_w, v_dn_conv_w, v_dn_a_log, v_dn_dt_bias, v_dn_norm_w, v_mix_w_out, v_swa_wq, v_swa_wk, v_swa_wv, v_swa_q_norm_w, v_swa_k_norm_w, v_swa_sinks, v_swa_wo, v_ffn_w_up, v_ffn_conv_w, v_ffn_w_down):
    ix, iy, ic = lax.axis_index("x"), lax.axis_index("y"), lax.axis_index("c")
    chip = 2 * ix + iy
    seq = x.shape[1]
    rows = HEAD0 + seq

    small_sharded = (conv_a_w, dn_conv_w, ffn_conv_w, meta_tokens)
    up_b, down_b = ffn_w_up.astype(BF16), ffn_w_down.astype(BF16)
    own = [mix_w_in[0].astype(BF16), mix_w_out[0].astype(BF16), swa_wq[0].astype(BF16), swa_wk[0].astype(BF16),
           swa_wv[0].astype(BF16), swa_wo[0].astype(BF16), up_b[0], up_b[1], down_b[0], down_b[1]]
    fill = lambda gathered, mine: [lax.dynamic_update_slice_in_dim(g, t[None], chip, axis=0)
                                   for g, t in zip(gathered, mine)]
    first, g_small = gather_weights(own[:1], _flat_pad(small_sharded, SW_ROWS))
    g_in, = fill(first, own[:1])
    w_in = jnp.pad(g_in.transpose(1, 0, 2).reshape(D, IN_DIM), ((0, 0), (0, P_W - IN_DIM)))
    rest = {}

    def tie(t):
        t, *mine = lax.optimization_barrier((t, *own[1:]))
        rest["w"] = fill(gather_weights_beside(mine), mine)
        return t

    def late():
        g_out, g_q, g_k, g_v, g_o, g_up0, g_up1, g_dn0, g_dn1 = rest["w"]
        wqkv = jnp.concatenate([g_q.reshape(D, D), g_k.reshape(D, 256), g_v.reshape(D, 256)], axis=1)
        return (g_out.reshape(D, D), wqkv, g_o.reshape(D, D), [g_up0, g_up1],
                [g_dn0.reshape(D_FF, D), g_dn1.reshape(D_FF, D)])

    gs = g_small.reshape(4, -1)
    ca_full = gs[:, 0:384].reshape(4, 3, 128).transpose(1, 0, 2).reshape(3, D_CONV)
    dc_full = gs[:, 384:1920].reshape(4, 4, 384).transpose(1, 0, 2).reshape(4, 3 * DN_DIM)
    fc_full = gs[:, 1920:6144].reshape(4, 2, 3, 704).transpose(1, 2, 0, 3).reshape(2, 3, D_FF)
    meta_full = gs[:, 6144:10240].reshape(4, N_META, 256).transpose(1, 0, 2).reshape(N_META, D)
    ca8, dc8 = _rows8(ca_full), _rows8(dc_full)
    fc8 = [_rows8(fc_full[0]), _rows8(fc_full[1])]
    alog, dtb = _lanes(dn_a_log[0], 4), _lanes(dn_dt_bias[0], 4)
    dnw = dn_norm_w.astype(F32)
    qw, kw = swa_q_norm_w.astype(F32), swa_k_norm_w.astype(F32)
    sinks = swa_sinks[0].astype(F32)
    anw = [attn_norm_w[0:1], attn_norm_w[1:2]]
    fnw = [ffn_norm_w[0:1], ffn_norm_w[1:2]]

    c_idx = jnp.reshape(ic, (1,)).astype(jnp.int32)
    chip_idx = jnp.stack([chip, ic]).astype(jnp.int32)
    begun = []

    def begin(tag, names, grads):
        pairs, gots = reduce_begin(grads, names, c_idx, 2 + len(begun), tag)
        begun.append((names, pairs, gots))

    (dh, loss_l, d_anw0, d_anw1, d_fnw0, d_fnw1, d_w_in, d_ca, d_dc, d_alog, d_dtb, d_dnw, d_w_out, d_wqkv, d_qw,
     d_kw, d_sinks, d_wo, d_up0, d_up1, d_fc0, d_fc1, d_down0, d_down1) = local_step(
        x[0], loss_target[0], meta_full, anw, fnw, w_in, ca8, dc8, alog, dtb, dnw, qw, kw, sinks, fc8, late,
        begin, tie)
    grad_x = dh[HEAD0:][None]

    small_parts = [jnp.concatenate([d_anw0, d_anw1], axis=0), jnp.concatenate([d_fnw0, d_fnw1], axis=0),
                   d_alog[0, 4:8], d_dtb[0, 4:8], d_dnw, d_qw, d_kw, d_sinks,
                   d_ca[:3], d_dc[:4], jnp.stack([d_fc0[:3], d_fc1[:3]]), dh[PAD:HEAD0], loss_l[0, 0:1]]
    small_shapes = [(2, D), (2, D), (1, 4), (1, 4), (1, DN_D), (1, SWA_D), (1, SWA_D), (1, SWA_H),
                    (1, 3, D_CONV), (1, 4, 3 * DN_DIM), (2, 3, D_FF), (N_META, D), ()]
    gathered_small = gather_small(_flat_pad(small_parts, SV_ROWS))

    red_big = {}
    for part in (begun[:-1], begun[-1:]):
        part_names = [n for names, _, _ in part for n in names]
        red_big.update(zip(part_names, reduce_end([p for _, ps, _ in part for p in ps],
                                                  [g for _, _, gs_ in part for g in gs_], part_names, chip_idx)))
    g_w_in, g_w_out, g_wq, g_wk, g_wv, g_wo, g_up0, g_up1, g_dn0, g_dn1 = [
        red_big[n] for n in ("w_in", "w_out", "wq", "wk", "wv", "wo", "up0", "up1", "down0", "down1")]

    grads = dict(mix_w_in=g_w_in, mix_w_out=g_w_out, swa_wq=g_wq, swa_wk=g_wk, swa_wv=g_wv, swa_wo=g_wo,
                 ffn_w_up=[g_up0, g_up1], ffn_w_down=[g_dn0, g_dn1])
    weights = dict(meta_tokens=meta_tokens, attn_norm_w=attn_norm_w, ffn_norm_w=ffn_norm_w, mix_w_in=mix_w_in,
                   conv_a_w=conv_a_w, dn_conv_w=dn_conv_w, dn_a_log=dn_a_log, dn_dt_bias=dn_dt_bias,
                   dn_norm_w=dn_norm_w, mix_w_out=mix_w_out, swa_wq=swa_wq, swa_wk=swa_wk, swa_wv=swa_wv,
                   swa_q_norm_w=swa_q_norm_w, swa_k_norm_w=swa_k_norm_w, swa_sinks=swa_sinks, swa_wo=swa_wo,
                   ffn_w_up=ffn_w_up, ffn_conv_w=ffn_conv_w, ffn_w_down=ffn_w_down)
    m_in = dict(meta_tokens=m_meta_tokens, attn_norm_w=m_attn_norm_w, ffn_norm_w=m_ffn_norm_w, mix_w_in=m_mix_w_in,
                conv_a_w=m_conv_a_w, dn_conv_w=m_dn_conv_w, dn_a_log=m_dn_a_log, dn_dt_bias=m_dn_dt_bias,
                dn_norm_w=m_dn_norm_w, mix_w_out=m_mix_w_out, swa_wq=m_swa_wq, swa_wk=m_swa_wk, swa_wv=m_swa_wv,
                swa_q_norm_w=m_swa_q_norm_w, swa_k_norm_w=m_swa_k_norm_w, swa_sinks=m_swa_sinks, swa_wo=m_swa_wo,
                ffn_w_up=m_ffn_w_up, ffn_conv_w=m_ffn_conv_w, ffn_w_down=m_ffn_w_down)
    v_in = dict(meta_tokens=v_meta_tokens, attn_norm_w=v_attn_norm_w, ffn_norm_w=v_ffn_norm_w, mix_w_in=v_mix_w_in,
                conv_a_w=v_conv_a_w, dn_conv_w=v_dn_conv_w, dn_a_log=v_dn_a_log, dn_dt_bias=v_dn_dt_bias,
                dn_norm_w=v_dn_norm_w, mix_w_out=v_mix_w_out, swa_wq=v_swa_wq, swa_wk=v_swa_wk, swa_wv=v_swa_wv,
                swa_q_norm_w=v_swa_q_norm_w, swa_k_norm_w=v_swa_k_norm_w, swa_sinks=v_swa_sinks, swa_wo=v_swa_wo,
                ffn_w_up=v_ffn_w_up, ffn_conv_w=v_ffn_conv_w, ffn_w_down=v_ffn_w_down)
    names = list(weights)
    small = [n for n in names if n not in BIG]
    delta, new_m, new_v = {}, {}, {}
    for n in BIG:
        delta[n], new_m[n], new_v[n], grads[n] = adamw(weights[n], grads[n], m_in[n], v_in[n], name=f"adamw_{n}")
    gathered_small, _ = lax.optimization_barrier((gathered_small, new_v["ffn_w_down"]))
    (g_anw, g_fnw, g_alog, g_dtb, g_dnw, g_qw, g_kw, g_sinks, g_ca_f, g_dc_f, g_fc_f, g_meta_f,
     loss) = _split_flat(sum_slots(gathered_small), small_shapes)
    grads.update(meta_tokens=lax.dynamic_slice_in_dim(g_meta_f, chip * 256, 256, axis=1), attn_norm_w=g_anw,
                 ffn_norm_w=g_fnw, conv_a_w=lax.dynamic_slice_in_dim(g_ca_f, chip * 128, 128, axis=2),
                 dn_conv_w=lax.dynamic_slice_in_dim(g_dc_f, chip * 384, 384, axis=2), dn_a_log=g_alog,
                 dn_dt_bias=g_dtb, dn_norm_w=g_dnw, swa_q_norm_w=g_qw, swa_k_norm_w=g_kw, swa_sinks=g_sinks,
                 ffn_conv_w=lax.dynamic_slice_in_dim(g_fc_f, chip * 704, 704, axis=2))
    grads = {n: grads[n].reshape(weights[n].shape) for n in names}
    shapes = [weights[n].shape for n in small]
    packed = [_flat_pad([t[n] for n in small], SW_ROWS) for t in (weights, grads, m_in, v_in)]
    for store, flat in zip((delta, new_m, new_v), adamw(*packed, name="adamw_small")):
        for n, t in zip(small, _split_flat(flat, shapes)):
            store[n] = t
    return (loss, grad_x, *[grads[n] for n in names], *[delta[n] for n in names],
            *[new_m[n] for n in names], *[new_v[n] for n in names])
```

```python
import functools

import jax
import jax.numpy as jnp
from jax import lax
from jax.experimental import pallas as pl
from jax.experimental.pallas import tpu as pltpu
from jax.experimental.pallas import tpu_sc as plsc

F32 = jnp.float32
BF16 = jnp.bfloat16
HI = lax.Precision.HIGHEST
MESH = pl.DeviceIdType.MESH

D = 1024
N_META = 16
PAD = 112
HEAD0 = PAD + N_META
D_CONV = 512
DN_H = 4
DN_D = 128
DN_DIM = 512
CH = 64
IN_DIM = 3592
P_W = 3840
BG0 = 3584
SWA_H = 16
SWA_KV = 4
SWA_D = 64
BLK = 128
NKEY = N_META + 2 * BLK
D_FF = 2816
EPS = 1e-6
LR, B1, B2, AEPS, WD, STEP = 0.001, 0.9, 0.999, 1e-08, 0.01, 10
VMEM_LIMIT = 48 * 1024 * 1024
MM_VMEM_BUDGET = 34 * 1024 * 1024
R_BIG = 6144
R_HALF = R_BIG // 2
SV_ROWS = 48
SW_ROWS = 16


def _pick(n, cands):
    for c in cands:
        if n % c == 0:
            return c
    return n


def _params(sem=None):
    return pltpu.CompilerParams(dimension_semantics=sem, vmem_limit_bytes=VMEM_LIMIT)


def _dot(a, b, ca=1, cb=0, prec=None):
    return lax.dot_general(a, b, (((ca,), (cb,)), ((), ())), precision=prec,
                           preferred_element_type=F32)


def _sigmoid(x):
    return 1.0 / (1.0 + jnp.exp(-x))


def _silu(x):
    return x * _sigmoid(x)


def _dsilu(x):
    s = _sigmoid(x)
    return s * (1.0 + x * (1.0 - s))


def _softplus(x):
    return jnp.maximum(x, 0.0) + jnp.log(1.0 + jnp.exp(-jnp.abs(x)))


def mm(a, b, *, name, ta=False, tb=False, out_dtype=F32, add=None, tm=None, tn=None, tk=None,
       b_chip=False, out_chip=False, swap_mid=False, epi=None, epi_ins=(), epi_consts=(), epi_outs=(), epi_accs=()):
    if epi is not None:
        return _mm_epi(a, b, name=name, tb=tb, tn=tn, b_chip=b_chip, swap_mid=swap_mid, epi=epi, epi_ins=epi_ins,
                       epi_consts=epi_consts, epi_outs=epi_outs, epi_accs=epi_accs)
    chip_of = _chip_order(swap_mid)
    m, k = (a.shape[1], a.shape[0]) if ta else a.shape
    if b_chip:
        n = b.shape[1] if tb else 4 * b.shape[2]
        if tb:
            tk = b.shape[2]
        else:
            tn = b.shape[2]
    else:
        n = b.shape[0] if tb else b.shape[1]
    if out_chip:
        tn = n // 4
    tn = tn or _pick(n, (1408, 1024, 768, 512, 256, 128))
    tk = tk or (_pick(k, (1408, 704, 384, 128)) if ta else _pick(k, (1024, 1408, 768, 512, 128)))
    nk = k // tk
    if tm is None:
        isz = lambda t: jnp.dtype(t.dtype).itemsize
        osz = jnp.dtype(out_dtype).itemsize
        for tm in ((1408, 1024, 512, 384, 256, 128) if ta else (1408, 704, 512, 384, 256, 128)):
            need = 2 * (tm * tk * isz(a) + tk * tn * isz(b) + tm * tn * osz + (tm * tn * 4 if add is not None else 0))
            need += tm * tn * 4 if nk > 1 else 0
            if m % tm == 0 and need <= MM_VMEM_BUDGET:
                break
        else:
            tm = m
    dims = (((0 if ta else 1,), (1 if tb else 0,)), ((), ()))

    def body(*refs):
        if add is None:
            a_ref, b_ref, o_ref, acc_ref = refs
            add_ref = None
        else:
            a_ref, b_ref, add_ref, o_ref, acc_ref = refs
        part = lax.dot_general(a_ref[...].astype(BF16), b_ref[...].astype(BF16), dims,
                               preferred_element_type=F32)

        def finish(total):
            if add_ref is not None:
                total = total + add_ref[...]
            o_ref[...] = total.astype(out_dtype)

        if nk == 1:
            finish(part)
        else:
            kk = pl.program_id(2)

            @pl.when(kk == 0)
            def _():
                acc_ref[...] = part

            @pl.when(kk > 0)
            def _():
                acc_ref[...] += part

            @pl.when(kk == nk - 1)
            def _():
                finish(acc_ref[...])

    a_spec = pl.BlockSpec((tk, tm), lambda i, j, kk: (kk, i)) if ta else pl.BlockSpec((tm, tk), lambda i, j, kk: (i, kk))
    if b_chip and tb:
        b_spec = pl.BlockSpec((None, tn, tk), lambda i, j, kk: (chip_of(kk), j, 0))
    elif b_chip:
        b_spec = pl.BlockSpec((None, tk, tn), lambda i, j, kk: (j, kk, 0))
    elif tb:
        b_spec = pl.BlockSpec((tn, tk), lambda i, j, kk: (j, kk))
    else:
        b_spec = pl.BlockSpec((tk, tn), lambda i, j, kk: (kk, j))
    o_spec = pl.BlockSpec((tm, tn), lambda i, j, kk: (i, j))
    in_specs = [a_spec, b_spec] + ([o_spec] if add is not None else [])
    args = [a, b] + ([add] if add is not None else [])
    out_spec = pl.BlockSpec((None, tm, tn), lambda i, j, kk: (chip_of(j), i, 0)) if out_chip else o_spec
    return pl.pallas_call(
        body, name=name, interpret=False,
        out_shape=jax.ShapeDtypeStruct((4, m, tn) if out_chip else (m, n), out_dtype),
        grid=(m // tm, n // tn, nk), in_specs=in_specs, out_specs=out_spec,
        scratch_shapes=[pltpu.VMEM((tm, tn) if nk > 1 else (8, 128), F32)],
        compiler_params=_params(("parallel", "parallel", "arbitrary")),
    )(*args)


def _chip_order(swap_mid):
    return (lambda k: (k % 2) * 2 + k // 2) if swap_mid else (lambda k: k)


def _mm_epi(a, b, *, name, tb, tn, b_chip, epi, epi_ins, epi_consts, epi_outs, epi_accs, swap_mid=False):
    chip_of = _chip_order(swap_mid)
    m, k = a.shape
    if b_chip:
        n = b.shape[1] if tb else 4 * b.shape[2]
        tk = b.shape[2] if tb else None
        tn = tn if tb else b.shape[2]
    else:
        n = b.shape[0] if tb else b.shape[1]
        tk = None
    tn = tn or _pick(n, (1408, 1024, 768, 512, 256, 128))
    tk = tk or _pick(k, (1024, 1408, 1280, 768, 512, 128))
    nk, nj = k // tk, n // tn
    isz = lambda t: jnp.dtype(t.dtype if hasattr(t, "dtype") else t).itemsize
    outs3 = [t if isinstance(t, tuple) else (t, n, lambda j: j) for t in epi_outs]
    side = sum(isz(t) for t, _ in epi_ins) + sum(isz(dt) for dt, _, _ in outs3)
    for tm in (1408, 704, 512, 384, 256, 128):
        need = 2 * (tm * tk * isz(a) + tk * tn * isz(b) + tm * tn * side) + (tm * tn * 4 if nk > 1 else 0)
        if m % tm == 0 and need <= MM_VMEM_BUDGET:
            break
    else:
        tm = m
    dims = (((1,), (1 if tb else 0,)), ((), ()))
    n_in, n_c, n_out, n_acc = len(epi_ins), len(epi_consts), len(epi_outs), len(epi_accs)

    def body(*refs):
        a_ref, b_ref = refs[:2]
        in_refs = refs[2:2 + n_in + n_c]
        out_refs = refs[2 + n_in + n_c:2 + n_in + n_c + n_out]
        acc_out = refs[2 + n_in + n_c + n_out:2 + n_in + n_c + n_out + n_acc]
        acc_ref = refs[-1]
        i, j, kk = pl.program_id(0), pl.program_id(1), pl.program_id(2)
        part = lax.dot_general(a_ref[...].astype(BF16), b_ref[...].astype(BF16), dims,
                               preferred_element_type=F32)

        def finish(total):
            res = epi(i * tm, total, *[r[...] for r in in_refs])
            if not isinstance(res, (tuple, list)):
                res = (res,)
            for r, v in zip(out_refs, res[:n_out]):
                r[...] = v.astype(r.dtype)
            if n_acc:
                @pl.when(jnp.logical_and(i == 0, j == 0))
                def _():
                    for r in acc_out:
                        r[...] = jnp.zeros(r.shape, r.dtype)

                for r, v in zip(acc_out, res[n_out:]):
                    r[...] += jnp.broadcast_to(v, r.shape).astype(r.dtype)

        if nk == 1:
            finish(part)
        else:
            @pl.when(kk == 0)
            def _():
                acc_ref[...] = part

            @pl.when(kk > 0)
            def _():
                acc_ref[...] += part

            @pl.when(kk == nk - 1)
            def _():
                finish(acc_ref[...])

    a_spec = pl.BlockSpec((tm, tk), lambda i, j, kk: (i, kk))
    if b_chip and tb:
        b_spec = pl.BlockSpec((None, tn, tk), lambda i, j, kk: (chip_of(kk), j, 0))
    elif b_chip:
        b_spec = pl.BlockSpec((None, tk, tn), lambda i, j, kk: (j, kk, 0))
    elif tb:
        b_spec = pl.BlockSpec((tn, tk), lambda i, j, kk: (j, kk))
    else:
        b_spec = pl.BlockSpec((tk, tn), lambda i, j, kk: (kk, j))
    in_specs = [a_spec, b_spec]
    in_specs += [pl.BlockSpec((tm, tn), lambda i, j, kk, col=col: (i, col(j))) for _, col in epi_ins]
    in_specs += [pl.BlockSpec(t.shape, lambda i, j, kk, nd=t.ndim: (0,) * nd) for t in epi_consts]
    out_specs = [pl.BlockSpec((tm, tn), lambda i, j, kk, col=col: (i, col(j))) for _, _, col in outs3]
    out_specs += [pl.BlockSpec(s, lambda i, j, kk, nd=len(s): (0,) * nd) for s, _ in epi_accs]
    out_shape = [jax.ShapeDtypeStruct((m, width), dt) for dt, width, _ in outs3]
    out_shape += [jax.ShapeDtypeStruct(s, dt) for s, dt in epi_accs]
    sem = ("arbitrary", "arbitrary", "arbitrary") if n_acc else ("parallel", "parallel", "arbitrary")
    return pl.pallas_call(
        body, name=name, interpret=False, out_shape=out_shape,
        grid=(m // tm, nj, nk), in_specs=in_specs, out_specs=out_specs,
        scratch_shapes=[pltpu.VMEM((tm, tn) if nk > 1 else (8, 128), F32)],
        compiler_params=_params(sem),
    )(a, b, *[t for t, _ in epi_ins], *epi_consts)


def cols(arr, tr, width=None, cb=0):
    width = width or arr.shape[1]
    return (arr, (tr, width), lambda i: (i, cb), "r2")


def heads(arr, tr):
    return (arr, (arr.shape[0], tr, arr.shape[2]), lambda i: (0, i, 0), "r3")


def whole(arr):
    nd = arr.ndim
    return (arr, arr.shape, lambda i: (0,) * nd, "w")


STRIP = 16


def _rows_of(ref, kind, r0, n):
    if kind == "r2":
        return ref[pl.ds(r0, n), :]
    if kind == "r3":
        return ref[:, pl.ds(r0, n), :]
    return ref[...]


def _set_rows(ref, kind, r0, n, v):
    if kind == "r2":
        ref[pl.ds(r0, n), :] = v.astype(ref.dtype)
    elif kind == "r3":
        ref[:, pl.ds(r0, n), :] = v.astype(ref.dtype)
    else:
        ref[...] = v.astype(ref.dtype)


def rowwise(fn, ins, outs, *, steps, name, accs=(), strip=None):
    n_in, n_out, n_acc = len(ins), len(outs), len(accs)
    kin = [t[3] for t in ins]
    kout = [t[4] for t in outs]
    tr = next((t[1][-2] for t in ins if t[3] != "w"), 0)

    def body(*refs):
        i = pl.program_id(0)
        in_refs, out_refs, acc_refs = refs[:n_in], refs[n_in:n_in + n_out], refs[n_in + n_out:]
        if n_acc:
            @pl.when(i == 0)
            def _():
                for r in acc_refs:
                    r[...] = jnp.zeros(r.shape, r.dtype)

        def run(r0, n):
            res = fn(i * tr + r0, *[_rows_of(r, k, r0, n) for r, k in zip(in_refs, kin)])
            if not isinstance(res, (tuple, list)):
                res = (res,)
            for r, k, v in zip(out_refs, kout, res[:n_out]):
                _set_rows(r, k, r0, n, v)
            for r, v in zip(acc_refs, res[n_out:]):
                r[...] += jnp.broadcast_to(v, r.shape).astype(r.dtype)

        if strip is None or tr <= strip:
            run(0, tr)
        else:
            def step(s, carry):
                run(pl.multiple_of(s * strip, strip), strip)
                return carry
            lax.fori_loop(0, tr // strip, step, 0)

    def zmap(nd):
        return lambda i: (0,) * nd

    in_specs = [pl.BlockSpec(t[1], t[2]) for t in ins]
    out_specs = [pl.BlockSpec(t[2], t[3]) for t in outs]
    out_specs += [pl.BlockSpec(s, zmap(len(s))) for s, _ in accs]
    out_shape = [jax.ShapeDtypeStruct(t[0], t[1]) for t in outs]
    out_shape += [jax.ShapeDtypeStruct(s, d) for s, d in accs]
    res = pl.pallas_call(
        body, name=name, interpret=False, out_shape=out_shape, grid=(steps,),
        in_specs=in_specs, out_specs=out_specs,
        compiler_params=_params(("arbitrary",)),
    )(*[t[0] for t in ins])
    return res


def out2d(rows, width, dtype, tr):
    return ((rows, width), dtype, (tr, width), lambda i: (i, 0), "r2")


def conv_fwd(xs, w8, kw, *, rows, c, tc, tr, name, post, extras=(), outs=(), pre=None, strip=STRIP):
    nx, ne, no = len(xs), len(extras), len(outs)
    nr, nc = rows // tr, c // tc
    r8 = tr // 8
    st = strip

    def body(*refs):
        x_refs = refs[:2 * nx]
        w_ref = refs[2 * nx]
        e_refs = refs[2 * nx + 1:2 * nx + 1 + ne]
        o_refs = refs[2 * nx + 1 + ne:2 * nx + 1 + ne + no]
        scr = refs[-1]
        j, i = pl.program_id(0), pl.program_id(1)
        halo = [x_refs[2 * q + 1][...].astype(F32) for q in range(nx)]
        scr[0:8, :] = jnp.where(i > 0, pre(*halo) if pre else halo[0], 0.0)

        def fill(s, carry):
            r0 = pl.multiple_of(s * st, st)
            cur = [x_refs[2 * q][pl.ds(r0, st), :].astype(F32) for q in range(nx)]
            scr[pl.ds(8 + r0, st), :] = pre(*cur) if pre else cur[0]
            return carry

        def comp(s, carry):
            r0 = pl.multiple_of(s * st, st)
            win = scr[pl.ds(r0, st + 8), :]
            y = jnp.zeros((st, tc), F32)
            for q in range(kw):
                sh = kw - 1 - q
                y = y + w_ref[q:q + 1, :] * win[8 - sh:8 - sh + st]
            res = post(j, y, *[e[pl.ds(r0, st), :] for e in e_refs])
            if not isinstance(res, (tuple, list)):
                res = (res,)
            for r, v in zip(o_refs, res):
                r[pl.ds(r0, st), :] = v.astype(r.dtype)
            return carry

        lax.fori_loop(0, tr // st, fill, 0)
        lax.fori_loop(0, tr // st, comp, 0)

    in_specs, args = [], []
    for arr, cb0 in xs:
        in_specs.append(pl.BlockSpec((tr, tc), lambda j, i, cb0=cb0: (i, cb0 + j)))
        in_specs.append(pl.BlockSpec((8, tc), lambda j, i, cb0=cb0: (jnp.maximum(i * r8 - 1, 0), cb0 + j)))
        args += [arr, arr]
    in_specs.append(pl.BlockSpec((8, tc), lambda j, i: (0, j)))
    args.append(w8)
    for arr, cb0 in extras:
        in_specs.append(pl.BlockSpec((tr, tc), lambda j, i, cb0=cb0: (i, cb0 + j)))
        args.append(arr)
    return pl.pallas_call(
        body, name=name, interpret=False,
        out_shape=[jax.ShapeDtypeStruct((rows, c), dt) for dt in outs],
        grid=(nc, nr), in_specs=in_specs,
        out_specs=[pl.BlockSpec((tr, tc), lambda j, i: (i, j)) for _ in outs],
        scratch_shapes=[pltpu.VMEM((tr + 8, tc), F32)],
        compiler_params=_params(("parallel", "arbitrary")),
    )(*args)


def conv_bwd(xs, w8, kw, dy, *, rows, c, tc, tr, name, post, extras=(), outs=(), pre=None):
    nx, ne, no = len(xs), len(extras), len(outs)
    nr, nc = rows // tr, c // tc
    r8 = tr // 8

    def body(*refs):
        x_refs = refs[:2 * nx]
        w_ref, dy_ref, dyn_ref = refs[2 * nx:2 * nx + 3]
        e_refs = refs[2 * nx + 3:2 * nx + 3 + ne]
        first_out = 2 * nx + 3 + ne
        o_refs = refs[first_out:first_out + no]
        dw_ref = refs[first_out + no]
        xscr, gscr = refs[-2], refs[-1]
        i = pl.program_id(1)
        halo = [x_refs[2 * q + 1][...].astype(F32) for q in range(nx)]
        xscr[0:8, :] = jnp.where(i > 0, pre(*halo) if pre else halo[0], 0.0)
        gscr[tr:tr + 8, :] = jnp.where(i < nr - 1, dyn_ref[...].astype(F32), 0.0)

        def fill(s, carry):
            r0 = pl.multiple_of(s * STRIP, STRIP)
            cur = [x_refs[2 * q][pl.ds(r0, STRIP), :].astype(F32) for q in range(nx)]
            xscr[pl.ds(8 + r0, STRIP), :] = pre(*cur) if pre else cur[0]
            gscr[pl.ds(r0, STRIP), :] = dy_ref[pl.ds(r0, STRIP), :].astype(F32)
            return carry

        def comp(s, dws):
            r0 = pl.multiple_of(s * STRIP, STRIP)
            gwin = gscr[pl.ds(r0, STRIP + 8), :]
            xwin = xscr[pl.ds(r0, STRIP + 8), :]
            g = gwin[0:STRIP]
            dx = jnp.zeros((STRIP, tc), F32)
            new = []
            for q in range(kw):
                sh = kw - 1 - q
                dx = dx + w_ref[q:q + 1, :] * gwin[sh:sh + STRIP]
                part = g * xwin[8 - sh:8 - sh + STRIP]
                new.append(dws[q] + part[0:8] + part[8:16])
            res = post(dx, *[e[pl.ds(r0, STRIP), :] for e in e_refs])
            if not isinstance(res, (tuple, list)):
                res = (res,)
            for r, v in zip(o_refs, res):
                r[pl.ds(r0, STRIP), :] = v.astype(r.dtype)
            return tuple(new)

        lax.fori_loop(0, tr // STRIP, fill, 0)
        dws = lax.fori_loop(0, tr // STRIP, comp, tuple(jnp.zeros((8, tc), F32) for _ in range(kw)))

        @pl.when(i == 0)
        def _():
            dw_ref[...] = jnp.zeros((8, tc), F32)

        dw_ref[...] += jnp.concatenate([jnp.sum(t, axis=0, keepdims=True) for t in dws]
                                       + [jnp.zeros((8 - kw, tc), F32)], axis=0)

    in_specs, args = [], []
    for arr, cb0 in xs:
        in_specs.append(pl.BlockSpec((tr, tc), lambda j, i, cb0=cb0: (i, cb0 + j)))
        in_specs.append(pl.BlockSpec((8, tc), lambda j, i, cb0=cb0: (jnp.maximum(i * r8 - 1, 0), cb0 + j)))
        args += [arr, arr]
    in_specs.append(pl.BlockSpec((8, tc), lambda j, i: (0, j)))
    in_specs.append(pl.BlockSpec((tr, tc), lambda j, i: (i, j)))
    in_specs.append(pl.BlockSpec((8, tc), lambda j, i: (jnp.minimum((i + 1) * r8, nr * r8 - 1), j)))
    args += [w8, dy, dy]
    for arr, cb0 in extras:
        in_specs.append(pl.BlockSpec((tr, tc), lambda j, i, cb0=cb0: (i, cb0 + j)))
        args.append(arr)
    return pl.pallas_call(
        body, name=name, interpret=False,
        out_shape=[jax.ShapeDtypeStruct((rows, c), dt) for dt in outs] + [jax.ShapeDtypeStruct((8, c), F32)],
        grid=(nc, nr), in_specs=in_specs,
        out_specs=[pl.BlockSpec((tr, tc), lambda j, i: (i, j)) for _ in outs] + [pl.BlockSpec((8, tc), lambda j, i: (0, j))],
        scratch_shapes=[pltpu.VMEM((tr + 8, tc), F32), pltpu.VMEM((tr + 8, tc), F32)],
        compiler_params=_params(("parallel", "arbitrary")),
    )(*args)


def rms_fwd(h, w, *, name):
    rows = h.shape[0]
    tr = _pick(rows, (384, 128))

    def fn(i, x, wv):
        r = lax.rsqrt(jnp.mean(x * x, axis=1, keepdims=True) + EPS)
        return x * r * wv

    return rowwise(fn, [cols(h, tr), whole(w)], [out2d(rows, D, BF16, tr)], steps=rows // tr, name=name)[0]


def _rms_bwd_epi(row0, g, x, dr, wv):
    r = lax.rsqrt(jnp.mean(x * x, axis=1, keepdims=True) + EPS)
    xh = x * r
    gw = g * wv
    dx = r * (gw - xh * jnp.mean(gw * xh, axis=1, keepdims=True))
    row = row0 + lax.broadcasted_iota(jnp.int32, (x.shape[0], 1), 0)
    return jnp.where(row >= PAD, dr + dx, 0.0), jnp.sum(g * xh, axis=0, keepdims=True)


def dx_rms_bwd(dy, w, h, nw, dres, *, name, b_chip=False, swap_mid=False):
    return mm(dy, w, tb=True, b_chip=b_chip, swap_mid=swap_mid, tn=D, name=name, epi=_rms_bwd_epi,
              epi_ins=[(h, lambda j: 0), (dres, lambda j: 0)], epi_consts=[nw], epi_outs=[F32],
              epi_accs=[((1, D), F32)])


def loss_grad(h, target):
    rows = h.shape[0]

    def fn(i, y, t):
        diff = jnp.where(i >= HEAD0, y - t, 0.0)
        part = jnp.sum(jnp.sum(diff * diff, axis=1, keepdims=True), axis=0, keepdims=True)
        return diff * (1.0 / D), part * (0.5 / D)

    tgt = (target, (BLK, D), lambda i: (jnp.maximum(i - 1, 0), 0), "r2")
    return rowwise(fn, [cols(h, BLK), tgt], [out2d(rows, D, F32, BLK)], steps=rows // BLK,
                   name="loss_grad", accs=[((1, 128), F32)])


def adamw(w, g, m, v, *, name):
    shape = w.shape
    gs = list(g) if isinstance(g, (list, tuple)) else [g]
    nl = len(gs)
    width = shape[-1]
    rows = w.size // width
    rl = rows // nl
    tr = _pick(rl, (256, 176, 128, 64, 16, 8))
    nr = rl // tr
    if w.ndim == 3 and shape[1] % tr == 0:
        per = shape[1] // tr
        view = lambda t: (t, (None, tr, width), lambda i: (i // per, i % per, 0), "r2")
        out = (shape, F32, (None, tr, width), lambda i: (i // per, i % per, 0), "r2")
    else:
        view = lambda t: cols(t.reshape(rows, width), tr)
        out = out2d(rows, width, F32, tr)

    def fn(i, wv, mv, vv, *gvs):
        gv = gvs[0]
        for layer in range(1, nl):
            gv = jnp.where(i >= layer * rl, gvs[layer], gv)
        mn = B1 * mv + (1.0 - B1) * gv
        vn = B2 * vv + (1.0 - B2) * gv * gv
        mh = mn / (1.0 - B1 ** STEP)
        vh = vn / (1.0 - B2 ** STEP)
        return -LR * (mh / (jnp.sqrt(vh) + AEPS) + WD * wv), mn, vn, gv

    g_ins = [(t.reshape(rl, width), (tr, width), lambda i, layer=layer: (jnp.clip(i - layer * nr, 0, nr - 1), 0), "r2")
             for layer, t in enumerate(gs)]
    res = rowwise(fn, [view(t) for t in (w, m, v)] + g_ins, [out] * 4, steps=rows // tr, name=name)
    return [r.reshape(shape) for r in res]


HB = DN_H * CH
PAIR = 2


def _split(a):
    hi = a.astype(BF16)
    return hi, (a - hi.astype(F32)).astype(BF16)


def _dot1(a, b, ca=1, cb=0):
    return _dot(a.astype(BF16), b.astype(BF16), ca, cb)


def _dot3(a, b, ca=1, cb=0):
    ah, al = _split(a)
    bh, bl = _split(b)
    return _dot(ah, bh, ca, cb) + (_dot(ah, bl, ca, cb) + _dot(al, bh, ca, cb))


def _dot01(m01, b, ca=1, cb=0):
    bh, bl = _split(b)
    m = m01.astype(BF16)
    return _dot(m, bh, ca, cb) + _dot(m, bl, ca, cb)


def _stack(x):
    return jnp.concatenate([x[:, h * DN_D:(h + 1) * DN_D] for h in range(DN_H)], axis=0)


def _unstack(x):
    return jnp.concatenate([x[h * CH:(h + 1) * CH] for h in range(DN_H)], axis=1)


def _tri_inv(a, blk, eye):
    ad = jnp.where(blk, a, 0.0)
    lo = a - ad
    a2 = _dot3(ad, ad)
    a4 = _dot3(a2, a2)
    a8 = _dot3(a4, a4)
    dgi = _dot3(_dot3(_dot3(eye - ad, eye + a2), eye + a4), eye + a8)
    n = _dot3(dgi, lo)
    return _dot3(_dot3(eye - n, eye + _dot3(n, n)), dgi)


def _dn_masks():
    row = lax.broadcasted_iota(jnp.int32, (HB, HB), 0)
    col = lax.broadcasted_iota(jnp.int32, (HB, HB), 1)
    same = (row // CH) == (col // CH)
    incl = jnp.logical_and(same, row >= col)
    strict = jnp.logical_and(same, row > col)
    upper = jnp.logical_and(same, row <= col)
    blk = (row // 16) == (col // 16)
    eye = (row == col).astype(F32)
    return incl, strict, upper, blk, eye


def _dn_chunk(qv, kv, vv, bc, br, incl, strict):
    r64 = lax.broadcasted_iota(jnp.int32, (CH, CH), 0)
    c64 = lax.broadcasted_iota(jnp.int32, (CH, CH), 1)
    dcol = _dot01((r64 >= c64).astype(F32), bc)
    drow = _dot3(br, (r64 <= c64).astype(F32))
    col = lambda m, l0: jnp.concatenate([m[:, l0 + h:l0 + h + 1] for h in range(DN_H)], axis=0)
    b_c = col(bc, 0)
    d_c = col(dcol, 4)
    d_r = jnp.concatenate([drow[4 + h:5 + h, :] for h in range(DN_H)], axis=1)
    d_last_h = [dcol[CH - 1:CH, 4 + h:5 + h] for h in range(DN_H)]
    d_last = jnp.concatenate([jnp.broadcast_to(t, (CH, 1)) for t in d_last_h], axis=0)
    q, k, v = _stack(qv), _stack(kv), _stack(vv)
    dm = jnp.where(incl, jnp.exp(jnp.where(incl, d_c - d_r, 0.0)), 0.0)
    kk = _dot1(k, k, 1, 1)
    a = jnp.where(strict, b_c * kk * dm, 0.0)
    ed = jnp.exp(d_c)
    rhs = jnp.concatenate([v * b_c, k * (b_c * ed)], axis=1)
    qk = _dot1(q, k, 1, 1) * dm
    ekd = jnp.exp(d_last - d_c)
    gl = [jnp.exp(t) for t in d_last_h]
    return q, k, v, b_c, dm, kk, a, ed, rhs, qk, ekd, gl


def dn_fwd(qkv_n, bgcol, bgrow):
    rows = qkv_n.shape[0]
    nch = rows // CH

    def body(q_ref, k_ref, v_ref, bc_ref, br_ref, o_ref, s_out, ti_out, s_scr, prep, prep_qk, prep_gl):
        n = pl.program_id(0)

        @pl.when(n == 0)
        def _():
            s_scr[...] = jnp.zeros(s_scr.shape, F32)
            prep[...] = jnp.zeros(prep.shape, F32)
            prep_qk[...] = jnp.zeros(prep_qk.shape, F32)
            prep_gl[...] = jnp.zeros(prep_gl.shape, F32)

        live = n > 0
        for c in range(PAIR):
            u, w, qd, kd = prep[c, 0], prep[c, 1], prep[c, 2], prep[c, 3]
            v_new, o_state = [], []
            for h in range(DN_H):
                rs = slice(h * CH, (h + 1) * CH)
                s = s_scr[h]
                s_out[c, h] = s
                vn = u[rs] - _dot1(w[rs], s)
                v_new.append(vn)
                o_state.append(_dot1(qd[rs], s))
                s_scr[h] = jnp.where(live, prep_gl[c, h:h + 1, 0:1] * s + _dot1(kd[rs], vn, 0, 0), s)
            o = jnp.concatenate(o_state, axis=0) + _dot1(prep_qk[c], jnp.concatenate(v_new, axis=0))
            o_ref[c * CH:(c + 1) * CH, :] = _unstack(o)

        incl, strict, _, blk, eye = _dn_masks()
        for c in range(PAIR):
            rows_c = slice(c * CH, (c + 1) * CH)
            q, k, v, b_c, dm, kk, a, ed, rhs, qk_n, ekd, gl = _dn_chunk(
                q_ref[rows_c, :], k_ref[rows_c, :], v_ref[rows_c, :], bc_ref[rows_c, :], br_ref[c], incl, strict)
            tinv = _tri_inv(a, blk, eye)
            ti_out[c] = tinv
            sol = _dot3(tinv, rhs)
            prep[c, 0] = sol[:, :DN_D]
            prep[c, 1] = sol[:, DN_D:]
            prep[c, 2] = q * ed
            prep[c, 3] = k * ekd
            prep_qk[c] = qk_n
            prep_gl[c] = jnp.concatenate([jnp.broadcast_to(t, (1, 128)) for t in gl]
                                         + [jnp.zeros((8 - DN_H, 128), F32)], axis=0)

    assert nch % PAIR == 0
    npair = nch // PAIR
    last = npair - 1
    return pl.pallas_call(
        body, name="dn_fwd", interpret=False,
        out_shape=[jax.ShapeDtypeStruct((rows, DN_DIM), F32),
                   jax.ShapeDtypeStruct((nch, DN_H, DN_D, DN_D), F32),
                   jax.ShapeDtypeStruct((nch, HB, HB), F32)],
        grid=(npair + 1,),
        in_specs=[pl.BlockSpec((PAIR * CH, DN_DIM), lambda n: (jnp.minimum(n, last), 0)),
                  pl.BlockSpec((PAIR * CH, DN_DIM), lambda n: (jnp.minimum(n, last), 1)),
                  pl.BlockSpec((PAIR * CH, DN_DIM), lambda n: (jnp.minimum(n, last), 2)),
                  pl.BlockSpec((PAIR * CH, 128), lambda n: (jnp.minimum(n, last), 0)),
                  pl.BlockSpec((PAIR, 8, CH), lambda n: (jnp.minimum(n, last), 0, 0))],
        out_specs=[pl.BlockSpec((PAIR * CH, DN_DIM), lambda n: (jnp.maximum(n - 1, 0), 0)),
                   pl.BlockSpec((PAIR, DN_H, DN_D, DN_D), lambda n: (jnp.maximum(n - 1, 0), 0, 0, 0)),
                   pl.BlockSpec((PAIR, HB, HB), lambda n: (jnp.minimum(n, last), 0, 0))],
        scratch_shapes=[pltpu.VMEM((DN_H, DN_D, DN_D), F32), pltpu.VMEM((PAIR, 4, HB, DN_D), F32),
                        pltpu.VMEM((PAIR, HB, HB), F32), pltpu.VMEM((PAIR, 8, 128), F32)],
        compiler_params=_params(("arbitrary",)),
    )(qkv_n, qkv_n, qkv_n, bgcol, bgrow)


def dn_bwd(qkv_n, bgcol, bgrow, s_all, ti_all, do):
    rows = qkv_n.shape[0]
    nch = rows // CH

    def body(q_ref, k_ref, v_ref, bc_ref, br_ref, s_ref, ti_ref, do_ref, dq_ref, dk_ref, dv_ref, dbg_ref, ds_scr):
        n = pl.program_id(0)

        @pl.when(n == 0)
        def _():
            ds_scr[...] = jnp.zeros(ds_scr.shape, F32)

        incl, strict, upper, _, _ = _dn_masks()
        q, k, v, b_c, dm, kk, a, ed, rhs, qk, ekd, gl = _dn_chunk(q_ref[...], k_ref[...], v_ref[...], bc_ref[...],
                                                                  br_ref[0], incl, strict)
        tinv = ti_ref[0]
        g_o = _stack(do_ref[...])
        sol = _dot3(tinv, rhs)
        u, w = sol[:, :DN_D], sol[:, DN_D:]
        qd, kd = q * ed, k * ekd
        rsum = lambda t: jnp.sum(t, axis=1, keepdims=True)
        rows_of = [slice(h * CH, (h + 1) * CH) for h in range(DN_H)]
        s_h = [s_ref[0, h] for h in range(DN_H)]
        ds_h = [ds_scr[h] for h in range(DN_H)]
        v_new = jnp.concatenate([u[rs] - _dot1(w[rs], s) for rs, s in zip(rows_of, s_h)], axis=0)
        dv_new = _dot1(qk, g_o, 0, 0) + jnp.concatenate([_dot1(kd[rs], t) for rs, t in zip(rows_of, ds_h)], axis=0)
        dqd = jnp.concatenate([_dot1(g_o[rs], s, 1, 1) for rs, s in zip(rows_of, s_h)], axis=0)
        dkd = jnp.concatenate([_dot1(v_new[rs], t, 1, 1) for rs, t in zip(rows_of, ds_h)], axis=0)
        for h, rs in enumerate(rows_of):
            ds_scr[h] = _dot1(qd[rs], g_o[rs], 0, 0) + gl[h] * ds_h[h] - _dot1(w[rs], dv_new[rs], 0, 0)
        dw = jnp.concatenate([-_dot1(dv_new[rs], s, 1, 1) for rs, s in zip(rows_of, s_h)], axis=0)
        dqk = _dot1(g_o, v_new, 1, 1)
        drhs = _dot3(tinv, jnp.concatenate([dv_new, dw], axis=1), 0, 0)
        da = jnp.where(strict, -_dot1(drhs, sol, 1, 1), 0.0)
        drhs_u, drhs_w = drhs[:, :DN_D], drhs[:, DN_D:]
        s2 = rsum(drhs_w * k)
        dbeta = rsum(drhs_u * v) + s2 * ed + rsum(da * kk * dm)
        dkk = da * b_c * dm
        dqkr = dqk * dm
        mmat = da * a + dqk * qk
        tmp = rsum(dkd * kd)
        dd = (s2 * b_c * ed + rsum(mmat) - _dot3(mmat, jnp.ones((HB, 128), F32), 0, 0)[:, :1] + rsum(dqd * qd) - tmp)
        rowi = lax.broadcasted_iota(jnp.int32, (CH, 1), 0)
        last = []
        for h, rs in enumerate(rows_of):
            dgl = jnp.sum(rsum(s_h[h] * ds_h[h]), axis=0, keepdims=True)
            dd_last = jnp.sum(tmp[rs], axis=0, keepdims=True) + dgl * gl[h]
            last.append(jnp.where(rowi == CH - 1, dd_last, 0.0))
        dd = dd + jnp.concatenate(last, axis=0)
        dq_ref[...] = _unstack(_dot1(dqkr, k) + dqd * ed)
        dk_ref[...] = _unstack(drhs_w * (b_c * ed) + _dot1(dkk, k) + _dot1(dkk, k, 0, 0) + _dot1(dqkr, q, 0, 0)
                               + dkd * ekd)
        dv_ref[...] = _unstack(drhs_u * b_c)
        dg = _dot01(upper.astype(F32), jnp.broadcast_to(dd, (HB, 128)))[:, :1]
        lane = lax.broadcasted_iota(jnp.int32, (CH, 128), 1)
        out = jnp.zeros((CH, 128), F32)
        for h, rs in enumerate(rows_of):
            out = out + jnp.where(lane == h, dbeta[rs], 0.0) + jnp.where(lane == 4 + h, dg[rs], 0.0)
        dbg_ref[...] = out

    rev = lambda n: nch - 1 - n
    return pl.pallas_call(
        body, name="dn_bwd", interpret=False,
        out_shape=[jax.ShapeDtypeStruct((rows, DN_DIM), F32)] * 3 + [jax.ShapeDtypeStruct((rows, 128), F32)],
        grid=(nch,),
        in_specs=[pl.BlockSpec((CH, DN_DIM), lambda n: (rev(n), 0)),
                  pl.BlockSpec((CH, DN_DIM), lambda n: (rev(n), 1)),
                  pl.BlockSpec((CH, DN_DIM), lambda n: (rev(n), 2)),
                  pl.BlockSpec((CH, 128), lambda n: (rev(n), 0)),
                  pl.BlockSpec((1, 8, CH), lambda n: (rev(n), 0, 0)),
                  pl.BlockSpec((1, DN_H, DN_D, DN_D), lambda n: (rev(n), 0, 0, 0)),
                  pl.BlockSpec((1, HB, HB), lambda n: (rev(n), 0, 0)),
                  pl.BlockSpec((CH, DN_DIM), lambda n: (rev(n), 0))],
        out_specs=[pl.BlockSpec((CH, DN_DIM), lambda n: (rev(n), 0))] * 3 + [pl.BlockSpec((CH, 128), lambda n: (rev(n), 0))],
        scratch_shapes=[pltpu.VMEM((DN_H, DN_D, DN_D), F32)],
        compiler_params=_params(("arbitrary",)),
    )(qkv_n, qkv_n, qkv_n, bgcol, bgrow, s_all, ti_all, do)


def _swa_valid(n):
    c3 = lax.broadcasted_iota(jnp.int32, (NKEY, 4 * BLK), 0)
    r = lax.broadcasted_iota(jnp.int32, (NKEY, 4 * BLK), 1) % BLK
    prev0 = N_META + BLK
    c = jnp.where(c3 < N_META, PAD + c3, jnp.where(c3 < prev0, c3 - N_META, c3 - prev0))
    lo = jnp.where(c3 < N_META, 0, jnp.where(c3 < prev0, r + 1 + jnp.where(n >= 2, 0, BLK), 0))
    hi = jnp.where(c3 < N_META, r + jnp.where(n >= 1, BLK, 0),
                   jnp.where(c3 < prev0, BLK, r - jnp.where(n >= 1, 0, BLK)))
    return jnp.logical_and(c >= lo, c <= hi)


def _swa_probs(q, kcat, valid, sink):
    s = jnp.where(valid, _dot(kcat, q, 1, 1), -1e30)
    m = jnp.maximum(jnp.max(s, axis=0, keepdims=True), sink)
    e = jnp.where(valid, jnp.exp(s - m), 0.0)
    es = jnp.exp(sink - m)
    inv = 1.0 / (jnp.sum(e, axis=0, keepdims=True) + es)
    return e * inv, es * inv


def _swa_group(q_ref, sk_ref, h):
    q4 = jnp.concatenate([q_ref[4 * h + g] for g in range(4)], axis=0)
    sink4 = jnp.concatenate([jnp.full((1, BLK), sk_ref[4 * h + g], F32) for g in range(4)], axis=1)
    return q4, sink4


def _swa_specs():
    q = pl.BlockSpec((SWA_H, BLK, SWA_D), lambda n: (0, n, 0))
    km = pl.BlockSpec((SWA_KV, N_META, SWA_D), lambda n: (0, PAD // N_META, 0))
    kp = pl.BlockSpec((SWA_KV, BLK, SWA_D), lambda n: (0, jnp.maximum(n - 1, 0), 0))
    kc = pl.BlockSpec((SWA_KV, BLK, SWA_D), lambda n: (0, n, 0))
    return [q, km, kp, kc, km, kp, kc]


def swa_fwd(qh, kh, vh, sinks):
    rows = qh.shape[1]
    nb = rows // BLK

    def body(q_ref, km, kp, kc, vm, vp, vc, sk_ref, o_ref):
        n = pl.program_id(0)
        valid = _swa_valid(n)
        outs = []
        for h in range(SWA_KV):
            kcat = jnp.concatenate([km[h], kp[h], kc[h]], axis=0)
            vcat = jnp.concatenate([vm[h], vp[h], vc[h]], axis=0)
            q4, sink4 = _swa_group(q_ref, sk_ref, h)
            p, _ = _swa_probs(q4, kcat, valid, sink4)
            o4 = _dot(p.astype(BF16), vcat, 0, 0)
            outs += [o4[g * BLK:(g + 1) * BLK] for g in range(4)]
        o_ref[...] = jnp.concatenate(outs, axis=1).astype(BF16)

    return pl.pallas_call(
        body, name="swa_fwd", interpret=False,
        out_shape=jax.ShapeDtypeStruct((rows, SWA_H * SWA_D), BF16),
        grid=(nb,),
        in_specs=_swa_specs() + [pl.BlockSpec(memory_space=pltpu.SMEM)],
        out_specs=pl.BlockSpec((BLK, SWA_H * SWA_D), lambda n: (n, 0)),
        compiler_params=_params(("parallel",)),
    )(qh, kh, kh, kh, vh, vh, vh, sinks)


def swa_bwd(qh, kh, vh, sinks, do):
    rows = qh.shape[1]
    nb = rows // BLK

    def body(q_ref, km, kp, kc, vm, vp, vc, do_ref, sk_ref, dq_ref, dk_ref, dv_ref, dsk_ref):
        n = pl.program_id(0)

        @pl.when(n == 0)
        def _():
            dk_ref[...] = jnp.zeros(dk_ref.shape, F32)
            dv_ref[...] = jnp.zeros(dv_ref.shape, F32)

        valid = _swa_valid(n)
        g_all = do_ref[...]
        rowi = lax.broadcasted_iota(jnp.int32, (SWA_H, 128), 0)
        dsk = jnp.zeros((SWA_H, 128), F32)
        pm = pl.multiple_of(jnp.maximum(n - 1, 0) * BLK, BLK)
        pc = pl.multiple_of(n * BLK, BLK)
        for h in range(SWA_KV):
            kcat = jnp.concatenate([km[h], kp[h], kc[h]], axis=0)
            vcat = jnp.concatenate([vm[h], vp[h], vc[h]], axis=0)
            q4, sink4 = _swa_group(q_ref, sk_ref, h)
            p, ps = _swa_probs(q4, kcat, valid, sink4)
            g4 = jnp.concatenate([g_all[:, (4 * h + g) * SWA_D:(4 * h + g + 1) * SWA_D] for g in range(4)], axis=0)
            dp = _dot(vcat, g4, 1, 1)
            delta = jnp.sum(p * dp, axis=0, keepdims=True)
            ds = (p * (dp - delta)).astype(BF16)
            dq4 = _dot(ds, kcat, 0, 0)
            dkc = _dot(ds, q4)
            dvc = _dot(p.astype(BF16), g4)
            t = ps * delta
            for g in range(4):
                dq_ref[4 * h + g] = dq4[g * BLK:(g + 1) * BLK]
                part = -jnp.sum(t[:, g * BLK:(g + 1) * BLK], axis=1, keepdims=True)
                dsk = dsk + jnp.where(rowi == 4 * h + g, part, 0.0)
            lanes = slice(h * SWA_D, (h + 1) * SWA_D)
            for ref, val in ((dk_ref, dkc), (dv_ref, dvc)):
                ref[PAD:BLK, lanes] += val[0:N_META]
                ref[pl.ds(pm, BLK), lanes] += val[N_META:N_META + BLK]
                ref[pl.ds(pc, BLK), lanes] += val[N_META + BLK:]
        dsk_ref[0] = dsk

    return pl.pallas_call(
        body, name="swa_bwd", interpret=False,
        out_shape=[jax.ShapeDtypeStruct((SWA_H, rows, SWA_D), F32),
                   jax.ShapeDtypeStruct((rows, SWA_KV * SWA_D), F32),
                   jax.ShapeDtypeStruct((rows, SWA_KV * SWA_D), F32),
                   jax.ShapeDtypeStruct((nb, SWA_H, 128), F32)],
        grid=(nb,),
        in_specs=_swa_specs() + [pl.BlockSpec((BLK, SWA_H * SWA_D), lambda n: (n, 0)),
                                 pl.BlockSpec(memory_space=pltpu.SMEM)],
        out_specs=[pl.BlockSpec((SWA_H, BLK, SWA_D), lambda n: (0, n, 0)),
                   pl.BlockSpec((rows, SWA_KV * SWA_D), lambda n: (0, 0)),
                   pl.BlockSpec((rows, SWA_KV * SWA_D), lambda n: (0, 0)),
                   pl.BlockSpec((1, SWA_H, 128), lambda n: (n, 0, 0))],
        compiler_params=_params(("arbitrary",)),
    )(qh, kh, kh, kh, vh, vh, vh, do, sinks)


QK_W = (SWA_H + SWA_KV) * SWA_D


def _head_mean(t):
    r = lax.broadcasted_iota(jnp.int32, (128, 128), 0) // SWA_D
    c = lax.broadcasted_iota(jnp.int32, (128, 128), 1) // SWA_D
    blk = jnp.where(r == c, 1.0 / SWA_D, 0.0).astype(BF16)
    out = []
    for i in range(t.shape[1] // 128):
        hi, lo = _split(t[:, 128 * i:128 * (i + 1)])
        out.append(_dot(hi, blk) + _dot(lo, blk))
    return jnp.concatenate(out, axis=1)


def _qk_scales(qw, kw):
    scale = SWA_D ** -0.5
    wt = jnp.concatenate([jnp.tile(qw.astype(F32) * scale, (1, SWA_H)), jnp.tile(kw.astype(F32), (1, SWA_KV))], axis=1)
    st = jnp.concatenate([jnp.full((1, SWA_H * SWA_D), scale, F32), jnp.ones((1, SWA_KV * SWA_D), F32)], axis=1)
    return wt, st


def qknorm_fwd(qkv, qw, kw):
    rows = qkv.shape[0]
    tr = _pick(rows, (384, 128))
    wt, _ = _qk_scales(qw, kw)

    def fn(i, x, w):
        xq = x[:, :QK_W]
        y = xq * lax.rsqrt(_head_mean(xq * xq) + EPS) * w
        head = lambda t, j: t[:, j * SWA_D:(j + 1) * SWA_D][None]
        qo = jnp.concatenate([head(y, j) for j in range(SWA_H)], axis=0)
        ko = jnp.concatenate([head(y, SWA_H + j) for j in range(SWA_KV)], axis=0)
        vo = jnp.concatenate([head(x, SWA_H + SWA_KV + j) for j in range(SWA_KV)], axis=0)
        return qo, ko, vo

    hm = lambda nh: ((nh, rows, SWA_D), BF16, (nh, tr, SWA_D), lambda i: (0, i, 0), "r3")
    return rowwise(fn, [cols(qkv, tr), whole(wt)], [hm(SWA_H), hm(SWA_KV), hm(SWA_KV)],
                   steps=rows // tr, name="qknorm_fwd")


def qknorm_bwd(qkv, qw, kw, dqh, dk, dv):
    rows = qkv.shape[0]
    tr = _pick(rows, (384, 128))
    wt, st = _qk_scales(qw, kw)

    def fn(i, x, w, sc, dq, dkv, dvv):
        xq = x[:, :QK_W]
        dy = jnp.concatenate([dq[j] for j in range(SWA_H)] + [dkv], axis=1)
        r = lax.rsqrt(_head_mean(xq * xq) + EPS)
        xh = xq * r
        gw = dy * w
        dx = r * (gw - xh * _head_mean(gw * xh))
        return jnp.concatenate([dx, dvv], axis=1), jnp.sum(dy * sc * xh, axis=0, keepdims=True)

    dqkv, dw = rowwise(fn, [cols(qkv, tr), whole(wt), whole(st), heads(dqh, tr), cols(dk, tr), cols(dv, tr)],
                       [out2d(rows, 1536, BF16, tr)], steps=rows // tr, name="qknorm_bwd", accs=[((1, QK_W), F32)])
    dw = dw.reshape(SWA_H + SWA_KV, SWA_D)
    return dqkv, jnp.sum(dw[:SWA_H], axis=0, keepdims=True), jnp.sum(dw[SWA_H:], axis=0, keepdims=True)


def _place():
    return lax.axis_index("x"), lax.axis_index("y"), lax.axis_index("c")


ANY = pl.BlockSpec(memory_space=pl.ANY)


def _rcopy(ssem, rsem, k, src, dst, to):
    return pltpu.make_async_remote_copy(src_ref=src, dst_ref=dst, send_sem=ssem.at[k], recv_sem=rsem.at[k],
                                        device_id=to, device_id_type=MESH)


def gather_weights(shards, small):
    n = len(shards)
    halves = [t.shape[0] // 2 for t in shards]

    def body(*refs):
        s_refs, small_ref = refs[:n], refs[n]
        o_refs, osmall = refs[n + 1:2 * n + 1], refs[2 * n + 1]
        ssem, rsem, lsem = refs[2 * n + 2:]
        x, y, c = _place()
        me = 2 * x + y
        chips = [(1 - x, y), (x, 1 - y), (1 - x, 1 - y)]

        def half(k, s, hh):
            return o_refs[k].at[s, pl.ds(hh * halves[k], halves[k]), :]

        loc = pltpu.make_async_copy(small_ref, osmall.at[me], lsem)
        loc.start()
        sends = []
        for k in range(n):
            for j, (px, py) in enumerate(chips):
                sends.append(_rcopy(ssem, rsem, 6 * k + j, s_refs[k].at[pl.ds(c * halves[k], halves[k]), :],
                                    half(k, me, c), (px, py, c)))
        for j, (px, py) in enumerate(chips):
            sends.append(_rcopy(ssem, rsem, 6 * n + j, small_ref, osmall.at[me], (px, py, c)))
        for cp in sends:
            cp.start()
        for k in range(n):
            for j, (px, py) in enumerate(chips):
                s = 2 * px + py
                _rcopy(ssem, rsem, 6 * k + j, half(k, s, c), half(k, s, c), (x, y, c)).wait_recv()
                fwd = _rcopy(ssem, rsem, 6 * k + 3 + j, half(k, s, c), half(k, s, c), (x, y, 1 - c))
                fwd.start()
                sends.append(fwd)
        for k in range(n):
            for j, (px, py) in enumerate(chips):
                s = 2 * px + py
                _rcopy(ssem, rsem, 6 * k + 3 + j, half(k, s, 1 - c), half(k, s, 1 - c), (x, y, c)).wait_recv()
        for j, (px, py) in enumerate(chips):
            s = 2 * px + py
            _rcopy(ssem, rsem, 6 * n + j, osmall.at[s], osmall.at[s], (x, y, c)).wait_recv()
        for cp in sends:
            cp.wait_send()
        loc.wait()

    res = pl.pallas_call(
        body, name="gather_weights", interpret=False,
        out_shape=[jax.ShapeDtypeStruct((4,) + t.shape, t.dtype) for t in shards]
        + [jax.ShapeDtypeStruct((4, SW_ROWS, 1024), F32)],
        in_specs=[ANY] * (n + 1), out_specs=[ANY] * (n + 1),
        scratch_shapes=[pltpu.SemaphoreType.DMA((6 * n + 3,)), pltpu.SemaphoreType.DMA((6 * n + 3,)),
                        pltpu.SemaphoreType.DMA],
    )(*shards, small)
    return res[:n], res[n]


def _handshake(peers):
    barrier = pltpu.get_barrier_semaphore()
    for peer in peers:
        pl.semaphore_signal(barrier, inc=1, device_id=peer, device_id_type=MESH)
    pl.semaphore_wait(barrier, len(peers))


def gather_weights_beside(shards):
    n = len(shards)
    halves = [t.shape[0] // 2 for t in shards]

    def body(*refs):
        s_refs, o_refs, ssem, rsem = refs[:n], refs[n:2 * n], refs[2 * n], refs[2 * n + 1]
        x, y, c = _place()
        me = 2 * x + y
        chips = [(1 - x, y), (x, 1 - y), (1 - x, 1 - y)]
        _handshake([(px, py, c) for px, py in chips] + [(x, y, 1 - c)])

        def half(k, s, hh):
            return o_refs[k].at[s, pl.ds(hh * halves[k], halves[k]), :]

        sends = []
        for k in range(n):
            for j, (px, py) in enumerate(chips):
                sends.append(_rcopy(ssem, rsem, 6 * k + j, s_refs[k].at[pl.ds(c * halves[k], halves[k]), :],
                                    half(k, me, c), (px, py, c)))
        for cp in sends:
            cp.start()
        for k in range(n):
            for j, (px, py) in enumerate(chips):
                s = 2 * px + py
                _rcopy(ssem, rsem, 6 * k + j, half(k, s, c), half(k, s, c), (x, y, c)).wait_recv()
                fwd = _rcopy(ssem, rsem, 6 * k + 3 + j, half(k, s, c), half(k, s, c), (x, y, 1 - c))
                fwd.start()
                sends.append(fwd)
        for k in range(n):
            for j, (px, py) in enumerate(chips):
                s = 2 * px + py
                _rcopy(ssem, rsem, 6 * k + 3 + j, half(k, s, 1 - c), half(k, s, 1 - c), (x, y, c)).wait_recv()
        for cp in sends:
            cp.wait_send()

    return pl.kernel(
        body, name="gather_weights_beside",
        out_type=[jax.ShapeDtypeStruct((4,) + t.shape, t.dtype) for t in shards],
        mesh=plsc.ScalarSubcoreMesh(axis_name="sequencer", num_cores=1),
        scratch_types=[pltpu.SemaphoreType.DMA((6 * n,)), pltpu.SemaphoreType.DMA((6 * n,))],
        compiler_params=pltpu.CompilerParams(collective_id=1),
    )(*shards)


def swap_halves(gs, *, name):
    n = len(gs)

    def body(*refs):
        g_refs, o_refs, ssem, rsem = refs[:n], refs[n:2 * n], refs[2 * n], refs[2 * n + 1]
        x, y, c = _place()
        cps = []
        for k in range(n):
            hk = g_refs[k].shape[1] // 2
            cps.append(_rcopy(ssem, rsem, k, g_refs[k].at[:, pl.ds((1 - c) * hk, hk), :], o_refs[k], (x, y, 1 - c)))
        for cp in cps:
            cp.start()
        for cp in cps:
            cp.wait()

    return pl.pallas_call(
        body, name=name, interpret=False,
        out_shape=[jax.ShapeDtypeStruct((4, t.shape[1] // 2, t.shape[2]), t.dtype) for t in gs],
        in_specs=[ANY] * n, out_specs=[ANY] * n,
        scratch_shapes=[pltpu.SemaphoreType.DMA((n,)), pltpu.SemaphoreType.DMA((n,))],
    )(*gs)


def _sum_rows(hk):
    return _pick(hk, (512, 352, 256, 128))


def pair_sum(g, other, c_idx, *, name):
    _, hk, width = other.shape
    tr = _sum_rows(hk)
    nbk = hk // tr

    def body(c_ref, g_ref, o_ref, out_ref):
        out_ref[...] = (g_ref[...].astype(F32) + o_ref[...].astype(F32)).astype(BF16)

    return pl.pallas_call(
        body, name=name, interpret=False,
        out_shape=jax.ShapeDtypeStruct((4, hk, width), BF16),
        grid_spec=pltpu.PrefetchScalarGridSpec(
            num_scalar_prefetch=1, grid=(4, nbk),
            in_specs=[pl.BlockSpec((1, tr, width), lambda s, i, c_ref: (s, c_ref[0] * nbk + i, 0)),
                      pl.BlockSpec((1, tr, width), lambda s, i, c_ref: (s, i, 0))],
            out_specs=pl.BlockSpec((1, tr, width), lambda s, i, c_ref: (s, i, 0))),
        compiler_params=_params(("parallel", "parallel")),
    )(c_idx, g, other)


def chip_sum(p, got, idx, *, name, own_half=False):
    pieces, hk, width = got.shape
    tr = _pick(hk, (256, 176, 128)) if pieces > 3 else _sum_rows(hk)
    nbk = hk // tr

    def body(idx_ref, p_ref, g_ref, out_ref):
        acc = p_ref[0].astype(F32)
        for j in range(pieces):
            acc = acc + g_ref[j].astype(F32)
        out_ref[0] = acc

    own = ((lambda i, idx_ref: (idx_ref[0], idx_ref[1] * nbk + i, 0)) if own_half
           else (lambda i, idx_ref: (idx_ref[0], i, 0)))
    return pl.pallas_call(
        body, name=name, interpret=False,
        out_shape=jax.ShapeDtypeStruct((2, hk, width), F32),
        grid_spec=pltpu.PrefetchScalarGridSpec(
            num_scalar_prefetch=1, grid=(nbk,),
            in_specs=[pl.BlockSpec((1, tr, width), own),
                      pl.BlockSpec((pieces, tr, width), lambda i, idx_ref: (0, i, 0))],
            out_specs=pl.BlockSpec((1, tr, width), lambda i, idx_ref: (idx_ref[1], i, 0))),
        compiler_params=_params(("parallel",)),
    )(idx, p, got)


def join_halves(qs):
    n = len(qs)

    def body(*refs):
        q_refs, o_refs, ssem, rsem = refs[:n], refs[n:2 * n], refs[2 * n], refs[2 * n + 1]
        x, y, c = _place()
        cps = [_rcopy(ssem, rsem, k, q_refs[k].at[c], o_refs[k].at[c], (x, y, 1 - c)) for k in range(n)]
        for cp in cps:
            cp.start()
        for k in range(n):
            _rcopy(ssem, rsem, k, q_refs[k].at[c], o_refs[k].at[1 - c], (x, y, 1 - c)).wait_recv()
        for cp in cps:
            cp.wait_send()

    return pl.pallas_call(
        body, name="join_halves", interpret=False,
        out_shape=[jax.ShapeDtypeStruct(t.shape, t.dtype) for t in qs],
        in_specs=[ANY] * n, out_specs=[ANY] * n, input_output_aliases={k: k for k in range(n)},
        scratch_shapes=[pltpu.SemaphoreType.DMA((n,)), pltpu.SemaphoreType.DMA((n,))],
    )(*qs)


def scatter_chips_beside(ps, cid, name):
    n = len(ps)

    def body(*refs):
        p_refs, o_refs, ssem, rsem = refs[:n], refs[n:2 * n], refs[2 * n], refs[2 * n + 1]
        x, y, c = _place()
        chips = [(1 - x, y), (x, 1 - y), (1 - x, 1 - y)]
        _handshake([(px, py, c) for px, py in chips])
        cps = [_rcopy(ssem, rsem, 3 * k + j, p_refs[k].at[2 * px + py], o_refs[k].at[j], (px, py, c))
               for k in range(n) for j, (px, py) in enumerate(chips)]
        for cp in cps:
            cp.start()
        for cp in cps:
            cp.wait()

    return pl.kernel(
        body, name=name, out_type=[jax.ShapeDtypeStruct((3,) + t.shape[1:], t.dtype) for t in ps],
        mesh=plsc.ScalarSubcoreMesh(axis_name="sequencer", num_cores=1),
        scratch_types=[pltpu.SemaphoreType.DMA((3 * n,)), pltpu.SemaphoreType.DMA((3 * n,))],
        compiler_params=pltpu.CompilerParams(collective_id=cid),
    )(*ps)


def scatter_all_beside(gs, cid, name):
    n = len(gs)

    def body(*refs):
        g_refs, o_refs, ssem, rsem = refs[:n], refs[n:2 * n], refs[2 * n], refs[2 * n + 1]
        x, y, c = _place()
        peers = []
        for k in range(1, 8):
            fx, fy, fc = (k >> 2) & 1, (k >> 1) & 1, k & 1
            peers.append((1 - x if fx else x, 1 - y if fy else y, 1 - c if fc else c))
        _handshake(peers)
        cps = []
        for a in range(n):
            hk = g_refs[a].shape[1] // 2
            for k, (px, py, pc) in enumerate(peers):
                cps.append(_rcopy(ssem, rsem, 7 * a + k, g_refs[a].at[2 * px + py, pl.ds(pc * hk, hk), :],
                                  o_refs[a].at[k], (px, py, pc)))
        for cp in cps:
            cp.start()
        for cp in cps:
            cp.wait()

    return pl.kernel(
        body, name=name, out_type=[jax.ShapeDtypeStruct((7, t.shape[1] // 2, t.shape[2]), t.dtype) for t in gs],
        mesh=plsc.ScalarSubcoreMesh(axis_name="sequencer", num_cores=1),
        scratch_types=[pltpu.SemaphoreType.DMA((7 * n,)), pltpu.SemaphoreType.DMA((7 * n,))],
        compiler_params=pltpu.CompilerParams(collective_id=cid),
    )(*gs)


def reduce_begin(gs, names, c_idx, cid, tag, presum):
    if not presum:
        return gs, scatter_all_beside(gs, cid, f"scatter_all_{tag}"), False
    others = swap_halves(gs, name=f"swap_halves_{tag}")
    pairs = [pair_sum(g, o, c_idx, name=f"pair_sum_{nm}") for g, o, nm in zip(gs, others, names)]
    return pairs, scatter_chips_beside(pairs, cid, f"scatter_chips_{tag}"), True


def reduce_end(begun, idx):
    names = [nm for g in begun for nm in g[0]]
    mine = [chip_sum(p, got, idx, name=f"chip_sum_{nm}", own_half=not presum)
            for nms, ps, gots, presum in begun for nm, p, got in zip(nms, ps, gots)]
    return dict(zip(names, [q.reshape(2 * q.shape[1], q.shape[2]) for q in join_halves(mine)]))


def gather_small(v):
    def body(v_ref, o_ref, ssem, rsem, lsem):
        x, y, c = _place()
        peers = []
        for k in range(1, 8):
            fx, fy, fc = (k >> 2) & 1, (k >> 1) & 1, k & 1
            peers.append((1 - x if fx else x, 1 - y if fy else y, 1 - c if fc else c))
        _handshake(peers)
        loc = pltpu.make_async_copy(v_ref, o_ref.at[4 * x + 2 * y + c], lsem)
        loc.start()
        cps = []
        for k, (px, py, pc) in enumerate(peers):
            cps.append((pltpu.make_async_remote_copy(
                src_ref=v_ref, dst_ref=o_ref.at[4 * x + 2 * y + c], send_sem=ssem.at[k], recv_sem=rsem.at[k],
                device_id=(px, py, pc), device_id_type=MESH), 4 * px + 2 * py + pc))
        for cp, _ in cps:
            cp.start()
        for k, (cp, peer) in enumerate(cps):
            pltpu.make_async_remote_copy(
                src_ref=v_ref, dst_ref=o_ref.at[peer], send_sem=ssem.at[k], recv_sem=rsem.at[k],
                device_id=(x, y, c), device_id_type=MESH).wait_recv()
        for cp, _ in cps:
            cp.wait_send()
        loc.wait()

    return pl.kernel(
        body, name="gather_small", out_type=jax.ShapeDtypeStruct((8, SV_ROWS, 1024), F32),
        mesh=plsc.ScalarSubcoreMesh(axis_name="sequencer", num_cores=1),
        scratch_types=[pltpu.SemaphoreType.DMA((7,)), pltpu.SemaphoreType.DMA((7,)), pltpu.SemaphoreType.DMA],
        compiler_params=pltpu.CompilerParams(collective_id=6),
    )(v)


def sum_slots(a):
    def fn(i, t):
        acc = t[0]
        for k in range(1, 8):
            acc = acc + t[k]
        return acc

    return rowwise(fn, [whole(a)], [((SV_ROWS, 1024), F32, (SV_ROWS, 1024), lambda i: (0, 0), "w")], steps=1,
                   name="sum_slots")[0]


def _head_rms(x, nw):
    xs, rs = [], []
    for h in range(DN_H):
        xh = x[:, h * DN_D:(h + 1) * DN_D]
        r = lax.rsqrt(jnp.mean(xh * xh, axis=1, keepdims=True) + EPS)
        xs.append(xh * r)
        rs.append(r)
    return xs, rs


def bg_fwd(p, alog, dtb):
    rows = p.shape[0]
    tr = _pick(rows, (384, 128))

    def fn(i, x, al, dt):
        lane = lax.broadcasted_iota(jnp.int32, x.shape, 1)
        row = i + lax.broadcasted_iota(jnp.int32, x.shape, 0)
        g = -jnp.exp(al) * _softplus(x + dt)
        out = jnp.where(lane < 4, _sigmoid(x), jnp.where(lane < 8, g, 0.0))
        return jnp.where(row >= PAD, out, 0.0)

    return rowwise(fn, [cols(p, tr, 128, BG0 // 128), whole(alog), whole(dtb)], [out2d(rows, 128, F32, tr)],
                   steps=rows // tr, name="bg_fwd")[0]


def bg_bwd(p, alog, dtb, dbg):
    rows = p.shape[0]
    tr = _pick(rows, (384, 128))

    def fn(i, x, al, dt, g_in):
        lane = lax.broadcasted_iota(jnp.int32, x.shape, 1)
        row = i + lax.broadcasted_iota(jnp.int32, x.shape, 0)
        live = row >= PAD
        is_b = jnp.logical_and(live, lane < 4)
        is_g = jnp.logical_and(live, jnp.logical_and(lane >= 4, lane < 8))
        beta = _sigmoid(x)
        ea = jnp.exp(al)
        g = -ea * _softplus(x + dt)
        dalpha = jnp.where(is_g, g_in * (-ea) * _sigmoid(x + dt), 0.0)
        dx = jnp.where(is_b, g_in * beta * (1.0 - beta), dalpha)
        dal = jnp.sum(jnp.where(is_g, g_in * g, 0.0), axis=0, keepdims=True)
        return jnp.concatenate([dx, jnp.zeros(x.shape, F32)], axis=1), dal, jnp.sum(dalpha, axis=0, keepdims=True)

    return rowwise(fn, [cols(p, tr, 128, BG0 // 128), whole(alog), whole(dtb), cols(dbg, tr)],
                   [out2d(rows, 256, BF16, tr)], steps=rows // tr, name="bg_bwd",
                   accs=[((1, 128), F32), ((1, 128), F32)])


def dn_qkv_post(j, y):
    xs = _silu(y)
    sc = jnp.where(j == 0, DN_D ** -0.5, 1.0)
    outs = []
    for h in range(DN_H):
        xh = xs[:, h * DN_D:(h + 1) * DN_D]
        r = lax.rsqrt(jnp.sum(xh * xh, axis=1, keepdims=True) + EPS)
        outs.append(jnp.where(j < 2, xh * r * sc, xh))
    return jnp.concatenate(outs, axis=1), y


def dn_qkv_bwd(cq, dq, dk, dv):
    rows = cq.shape[0]
    tr = _pick(rows, (384, 128))

    def fn(i, c0, c1, c2, g0, g1, g2):
        pieces = []
        for kind, (cv, g) in enumerate(((c0, g0), (c1, g1), (c2, g2))):
            xs = _silu(cv)
            if kind < 2:
                sc = DN_D ** -0.5 if kind == 0 else 1.0
                ds = []
                for h in range(DN_H):
                    sl = slice(h * DN_D, (h + 1) * DN_D)
                    xh, gh = xs[:, sl], g[:, sl]
                    r = lax.rsqrt(jnp.sum(xh * xh, axis=1, keepdims=True) + EPS)
                    xn = xh * r
                    ds.append(sc * r * (gh - xn * jnp.sum(gh * xn, axis=1, keepdims=True)))
                dxs = jnp.concatenate(ds, axis=1)
            else:
                dxs = g
            pieces.append(dxs * _dsilu(cv))
        return jnp.concatenate(pieces, axis=1)

    ins = [cols(cq, tr, DN_DIM, k) for k in range(3)] + [cols(t, tr) for t in (dq, dk, dv)]
    return rowwise(fn, ins, [out2d(rows, 3 * DN_DIM, F32, tr)], steps=rows // tr, name="dn_qkv_bwd")[0]


def dn_out_fwd(o, p, nw):
    rows = o.shape[0]
    tr = _pick(rows, (384, 128))

    def fn(i, ov, z, w):
        xs, _ = _head_rms(ov, w)
        return jnp.concatenate(xs, axis=1) * jnp.concatenate([w] * DN_H, axis=1) * _silu(z)

    return rowwise(fn, [cols(o, tr), cols(p, tr, DN_DIM, 6), whole(nw)], [out2d(rows, DN_DIM, BF16, tr)],
                   steps=rows // tr, name="dn_out_fwd")[0]


def dn_out_bwd(o, p, nw, dymix):
    rows = o.shape[0]
    tr = _pick(rows, (384, 128))

    def fn(i, ov, z, w, dy):
        xs, rs = _head_rms(ov, w)
        sz = _silu(z)
        dn = dy * sz
        dos, dw = [], jnp.zeros((1, DN_D), F32)
        for h in range(DN_H):
            sl = slice(h * DN_D, (h + 1) * DN_D)
            gw = dn[:, sl] * w
            dos.append(rs[h] * (gw - xs[h] * jnp.mean(gw * xs[h], axis=1, keepdims=True)))
            dw = dw + jnp.sum(dn[:, sl] * xs[h], axis=0, keepdims=True)
        n = jnp.concatenate(xs, axis=1) * jnp.concatenate([w] * DN_H, axis=1)
        return jnp.concatenate(dos, axis=1), dy * n * _dsilu(z), dw

    return rowwise(fn, [cols(o, tr), cols(p, tr, DN_DIM, 6), whole(nw), cols(dymix, tr, DN_DIM, 1)],
                   [out2d(rows, DN_DIM, F32, tr), out2d(rows, DN_DIM, BF16, tr)], steps=rows // tr,
                   name="dn_out_bwd", accs=[((1, DN_D), F32)])


def conv_a_pre_bwd(dymix, cv, p):
    rows = cv.shape[0]
    tr = _pick(rows, (384, 128))

    def fn(i, dy, c, go):
        return dy * c, dy * go

    return rowwise(fn, [cols(dymix, tr, D_CONV, 0), cols(cv, tr), cols(p, tr, D_CONV, 1)],
                   [out2d(rows, D_CONV, BF16, tr), out2d(rows, D_CONV, F32, tr)], steps=rows // tr,
                   name="conv_a_pre_bwd")


def _rows8(w):
    return jnp.pad(w.astype(F32), ((0, 8 - w.shape[0]), (0, 0)))


def _lanes(v, at):
    return jnp.pad(v.astype(F32), (at, 128 - at - v.shape[0]))[None]


def add_norm(a, w, h, next_nw, *, name):
    if next_nw is None:
        return mm(a, w, add=h, name=name), None
    return mm(a, w, name=name, epi=_add_norm_epi, epi_ins=[(h, lambda j: 0)], epi_consts=[next_nw],
              epi_outs=[F32, BF16])


def _add_norm_epi(row0, t, h, nw):
    x = t + h
    return x, x * lax.rsqrt(jnp.mean(x * x, axis=1, keepdims=True) + EPS) * nw


def ffn_up_conv(hn, w_up, cw8, *, name):
    rows = hn.shape[0]
    tn = w_up.shape[2]
    tm = _pick(rows, (384, 128))
    nr = rows // tm

    def body(x_ref, wg_ref, wv_ref, w_ref, ug_ref, uv_ref, gc_ref, a_ref, carry, scr):
        i = pl.program_id(1)
        x = x_ref[...]
        gate = _dot(x, wg_ref[...])
        val = _dot(x, wv_ref[...])
        ug_ref[...] = gate.astype(BF16)
        uv_ref[...] = val.astype(BF16)
        scr[0:8, :] = jnp.where(i > 0, carry[...], 0.0)
        scr[8:8 + tm, :] = gate
        carry[...] = gate[tm - 8:tm]
        y = jnp.zeros((tm, tn), F32)
        for q in range(3):
            sh = 2 - q
            y = y + w_ref[q:q + 1, :] * scr[8 - sh:8 - sh + tm, :]
        gc_ref[...] = y.astype(BF16)
        a_ref[...] = (_silu(y) * val).astype(BF16)

    half = pl.BlockSpec((tm, tn), lambda j, i: (i, j))
    return pl.pallas_call(
        body, name=name, interpret=False,
        out_shape=[jax.ShapeDtypeStruct((rows, D_FF), BF16)] * 4,
        grid=(D_FF // tn, nr),
        in_specs=[pl.BlockSpec((tm, D), lambda j, i: (i, 0)),
                  pl.BlockSpec((None, D, tn), lambda j, i: (j, 0, 0)),
                  pl.BlockSpec((None, D, tn), lambda j, i: (j + D_FF // tn, 0, 0)),
                  pl.BlockSpec((8, tn), lambda j, i: (0, j))],
        out_specs=[half] * 4,
        scratch_shapes=[pltpu.VMEM((8, tn), F32), pltpu.VMEM((tm + 8, tn), F32)],
        compiler_params=_params(("arbitrary", "arbitrary")),
    )(hn, w_up, w_up, cw8)


def ffn_down_bwd(dh, w_down, gc, uv, ug, cw8, *, name):
    rows = dh.shape[0]
    tn = D_FF // 2
    tm = _pick(rows, (384, 128))
    nr = rows // tm
    r8 = tm // 8

    def body(dh_ref, w_ref, gc_ref, uv_ref, ug_ref, halo_ref, cw_ref, du_ref, dw_ref, carry, gscr, xscr):
        ip = pl.program_id(1)
        i = nr - 1 - ip
        da = _dot(dh_ref[...].astype(BF16), w_ref[...], 1, 1)
        c, val = gc_ref[...].astype(F32), uv_ref[...].astype(F32)
        dgc = da * val * _dsilu(c)
        du_ref[:, tn:] = (da * _silu(c)).astype(BF16)
        gscr[0:tm, :] = dgc
        gscr[tm:tm + 8, :] = jnp.where(ip > 0, carry[...], 0.0)
        carry[...] = dgc[0:8]
        xscr[0:8, :] = jnp.where(i > 0, halo_ref[...].astype(F32), 0.0)
        xscr[8:8 + tm, :] = ug_ref[...].astype(F32)
        dx = jnp.zeros((tm, tn), F32)
        dws = []
        for q in range(3):
            sh = 2 - q
            dx = dx + cw_ref[q:q + 1, :] * gscr[sh:sh + tm, :]
            dws.append(jnp.sum(dgc * xscr[8 - sh:8 - sh + tm, :], axis=0, keepdims=True))
        du_ref[:, :tn] = dx.astype(BF16)

        @pl.when(ip == 0)
        def _():
            dw_ref[...] = jnp.zeros((8, tn), F32)

        dw_ref[...] += jnp.concatenate(dws + [jnp.zeros((5, tn), F32)], axis=0)

    rev = lambda ip: nr - 1 - ip
    tile = lambda arr: pl.BlockSpec((tm, tn), lambda j, ip: (rev(ip), j))
    return pl.pallas_call(
        body, name=name, interpret=False,
        out_shape=[jax.ShapeDtypeStruct((rows, 2 * D_FF), BF16), jax.ShapeDtypeStruct((8, D_FF), F32)],
        grid=(2, nr),
        in_specs=[pl.BlockSpec((tm, D), lambda j, ip: (rev(ip), 0)),
                  pl.BlockSpec((tn, D), lambda j, ip: (j, 0)),
                  tile(gc), tile(uv), tile(ug),
                  pl.BlockSpec((8, tn), lambda j, ip: (jnp.maximum(rev(ip) * r8 - 1, 0), j)),
                  pl.BlockSpec((8, tn), lambda j, ip: (0, j))],
        out_specs=[pl.BlockSpec((tm, 2 * tn), lambda j, ip: (rev(ip), j)),
                   pl.BlockSpec((8, tn), lambda j, ip: (0, j))],
        scratch_shapes=[pltpu.VMEM((8, tn), F32), pltpu.VMEM((tm + 8, tn), F32), pltpu.VMEM((tm + 8, tn), F32)],
        compiler_params=_params(("arbitrary", "arbitrary")),
    )(dh, w_down, gc, uv, ug, ug, cw8)


def ffn_fwd(h, hn, w_up, cw8, w_down, tag, next_nw):
    ug, uv, gc, a = ffn_up_conv(hn, w_up, cw8, name=f"ffn{tag}_up")
    out, hn_next = add_norm(a, w_down, h, next_nw, name=f"ffn{tag}_down")
    return out, hn_next, (hn, ug, uv, a, gc)


def ffn_bwd(h, nw, w_up, cw8, w_down, saved, dh, tag):
    hn, ug, uv, a, gc = saved
    du, d_cw = ffn_down_bwd(dh, w_down, gc, uv, ug, cw8, name=f"ffn{tag}_down_dx")
    d_w_down = mm(a, dh, ta=True, out_dtype=BF16, name=f"ffn{tag}_down_dw")
    dh_new, d_nw = dx_rms_bwd(du, w_up, h, nw, dh, name=f"ffn{tag}_up_dx", b_chip=True, swap_mid=True)
    d_w_up = mm(hn, du, ta=True, out_dtype=BF16, out_chip=True, swap_mid=True, name=f"ffn{tag}_up_dw")
    return dh_new, d_nw, d_w_up, d_cw, d_w_down


def mixer_fwd(h, nw, w_in, ca8, dc8, alog, dtb, dnw, w_out, tie=None, next_nw=None):
    rows = h.shape[0]
    tr = _pick(rows, (384, 128))
    hn = rms_fwd(h, nw, name="mix_norm")
    p = mm(hn, w_in, name="mix_in")
    y_a, cv = conv_fwd([(p, 0), (p, 2)], ca8, 3, rows=rows, c=D_CONV, tc=D_CONV, tr=tr, name="conv_a",
                       pre=lambda gi, ah: gi * ah, post=lambda j, y, go: (go * y, y), extras=[(p, 1)],
                       outs=[BF16, F32])
    qkv_n, cq = conv_fwd([(p, 3)], dc8, 4, rows=rows, c=3 * DN_DIM, tc=DN_DIM, tr=tr, name="dn_conv",
                         post=dn_qkv_post, outs=[F32, F32], strip=tr)
    bgcol = bg_fwd(p, alog, dtb)
    if tie is not None:
        bgcol = tie(bgcol)
    bgrow = bgcol[:, :8].reshape(rows // CH, CH, 8).transpose(0, 2, 1)
    o, s_all, ti_all = dn_fwd(qkv_n, bgcol, bgrow)
    y_b = dn_out_fwd(o, p, dnw)
    ymix = jnp.concatenate([y_a, y_b], axis=1)
    w_out = w_out() if callable(w_out) else w_out
    out, hn_next = add_norm(ymix, w_out, h, next_nw, name="mix_out")
    return out, hn_next, (hn, p, cv, qkv_n, cq, bgcol, bgrow, o, s_all, ti_all, ymix)


def mixer_bwd(h, nw, w_in, ca8, dc8, alog, dtb, dnw, w_out, saved, dh):
    hn, p, cv, qkv_n, cq, bgcol, bgrow, o, s_all, ti_all, ymix = saved
    rows = h.shape[0]
    tr = _pick(rows, (384, 128))
    dymix = mm(dh, w_out, tb=True, name="mix_out_dx")
    d_w_out = mm(ymix, dh, ta=True, out_dtype=BF16, name="mix_out_dw")
    do, dz, d_dnw = dn_out_bwd(o, p, dnw, dymix)
    dq, dk, dv, dbg = dn_bwd(qkv_n, bgcol, bgrow, s_all, ti_all, do)
    dbg_p, d_alog, d_dtb = bg_bwd(p, alog, dtb, dbg)
    dcq = dn_qkv_bwd(cq, dq, dk, dv)
    dqkv, d_dc = conv_bwd([(p, 3)], dc8, 4, dcq, rows=rows, c=3 * DN_DIM, tc=DN_DIM, tr=tr, name="dn_conv_bwd",
                          post=lambda dx: dx, outs=[BF16])
    dgo, dcv = conv_a_pre_bwd(dymix, cv, p)
    dgi, dah, d_ca = conv_bwd([(p, 0), (p, 2)], ca8, 3, dcv, rows=rows, c=D_CONV, tc=D_CONV, tr=tr,
                              name="conv_a_bwd", pre=lambda gi, ah: gi * ah,
                              post=lambda dm, gi, ah: (dm * ah, dm * gi), extras=[(p, 0), (p, 2)], outs=[BF16, BF16])
    dp = jnp.concatenate([dgi, dgo, dah, dqkv, dz, dbg_p], axis=1)
    dh_new, d_nw = dx_rms_bwd(dp, w_in, h, nw, dh, name="mix_in_dx")
    d_w_in = mm(hn, dp, ta=True, out_dtype=BF16, name="mix_in_dw")
    return dh_new, d_nw, d_w_in, d_ca, d_dc, d_alog, d_dtb, d_dnw, d_w_out


def swa_layer_fwd(h, hn, wqkv, qw, kw, sinks, wo, next_nw):
    qkv = mm(hn, wqkv, name="swa_qkv")
    qh, kh, vh = qknorm_fwd(qkv, qw, kw)
    att = swa_fwd(qh, kh, vh, sinks)
    out, hn_next = add_norm(att, wo, h, next_nw, name="swa_out")
    return out, hn_next, (hn, qkv, qh, kh, vh, att)


def swa_layer_bwd(h, nw, wqkv, qw, kw, sinks, wo, saved, dh):
    hn, qkv, qh, kh, vh, att = saved
    datt = mm(dh, wo, tb=True, out_dtype=BF16, name="swa_out_dx")
    d_wo = mm(att, dh, ta=True, out_dtype=BF16, name="swa_out_dw")
    dqh, dkh, dvh, dsk = swa_bwd(qh, kh, vh, sinks, datt)
    dqkv, d_qw, d_kw = qknorm_bwd(qkv, qw, kw, dqh, dkh, dvh)
    dh_new, d_nw = dx_rms_bwd(dqkv, wqkv, h, nw, dh, name="swa_qkv_dx")
    d_wqkv = mm(hn, dqkv, ta=True, out_dtype=BF16, name="swa_qkv_dw")
    d_sinks = jnp.sum(dsk[:, :, 0], axis=0)
    return dh_new, d_nw, d_wqkv, d_qw, d_kw, d_sinks, d_wo


BIG = ("mix_w_in", "mix_w_out", "swa_wq", "swa_wk", "swa_wv", "swa_wo", "ffn_w_up", "ffn_w_down")


def _flat_pad(parts, rows):
    v = jnp.concatenate([t.astype(F32).reshape(-1) for t in parts])
    return jnp.pad(v, (0, rows * 1024 - v.shape[0])).reshape(rows, 1024)


def _split_flat(flat, shapes):
    v = flat.reshape(-1)
    out, o = [], 0
    for s in shapes:
        n = 1
        for d_ in s:
            n *= d_
        out.append(v[o:o + n].reshape(s))
        o += n
    return out


def local_step(x0, target0, meta_full, anw, fnw, w_in, ca8, dc8, alog, dtb, dnw, qw, kw, sinks, fc8, late,
               begin=None, tie=None):
    begin = begin or (lambda tag, names, grads: None)
    h0 = jnp.concatenate([jnp.zeros((PAD, D), F32), meta_full, x0], axis=0)
    h1, hn1, s_mix = mixer_fwd(h0, anw[0], w_in, ca8, dc8, alog, dtb, dnw, lambda: late()[0], tie, fnw[0])
    w_out, wqkv, wo, w_up, w_down = late()
    h2, hn2, s_f0 = ffn_fwd(h1, hn1, w_up[0], fc8[0], w_down[0], 0, anw[1])
    h3, hn3, s_swa = swa_layer_fwd(h2, hn2, wqkv, qw, kw, sinks, wo, fnw[1])
    h4, _, s_f1 = ffn_fwd(h3, hn3, w_up[1], fc8[1], w_down[1], 1, None)
    dh, loss_l = loss_grad(h4, target0)
    dh, d_fnw1, d_up1, d_fc1, d_down1 = ffn_bwd(h3, fnw[1], w_up[1], fc8[1], w_down[1], s_f1, dh, 1)
    begin("ffn1", ("up1", "down1"), [d_up1, d_down1.reshape(4, 704, D)])
    dh, d_anw1, d_wqkv, d_qw, d_kw, d_sinks, d_wo = swa_layer_bwd(h2, anw[1], wqkv, qw, kw, sinks, wo, s_swa, dh)
    begin("swa", ("wq", "wk", "wv", "wo"),
          [d_wqkv[:, :D].reshape(4, 256, D), d_wqkv[:, D:D + 256].reshape(4, 256, 256),
           d_wqkv[:, D + 256:].reshape(4, 256, 256), d_wo.reshape(4, 256, D)])
    dh, d_fnw0, d_up0, d_fc0, d_down0 = ffn_bwd(h1, fnw[0], w_up[0], fc8[0], w_down[0], s_f0, dh, 0)
    begin("ffn0", ("up0", "down0"), [d_up0, d_down0.reshape(4, 704, D)])
    dh, d_anw0, d_w_in, d_ca, d_dc, d_alog, d_dtb, d_dnw, d_w_out = mixer_bwd(
        h0, anw[0], w_in, ca8, dc8, alog, dtb, dnw, w_out, s_mix, dh)
    begin("mix", ("w_in", "w_out"),
          [d_w_in[:, :IN_DIM].reshape(D, 4, 898).transpose(1, 0, 2), d_w_out.reshape(4, 256, D)])
    return (dh, loss_l, d_anw0, d_anw1, d_fnw0, d_fnw1, d_w_in, d_ca, d_dc, d_alog, d_dtb, d_dnw, d_w_out, d_wqkv,
            d_qw, d_kw, d_sinks, d_wo, d_up0, d_up1, d_fc0, d_fc1, d_down0, d_down1)


def kernel(x, meta_tokens, attn_norm_w, ffn_norm_w, mix_w_in, conv_a_w, dn_conv_w, dn_a_log, dn_dt_bias, dn_norm_w, mix_w_out, swa_wq, swa_wk, swa_wv, swa_q_norm_w, swa_k_norm_w, swa_sinks, swa_wo, ffn_w_up, ffn_conv_w, ffn_w_down, loss_target, m_meta_tokens, m_attn_norm_w, m_ffn_norm_w, m_mix_w_in, m_conv_a_w, m_dn_conv_w, m_dn_a_log, m_dn_dt_bias, m_dn_norm_w, m_mix_w_out, m_swa_wq, m_swa_wk, m_swa_wv, m_swa_q_norm_w, m_swa_k_norm_w, m_swa_sinks, m_swa_wo, m_ffn_w_up, m_ffn_conv_w, m_ffn_w_down, v_meta_tokens, v_attn_norm_w, v_ffn_norm_w, v_mix_w_in, v_conv_a_w, v_dn_conv_w, v_dn_a_log, v_dn_dt_bias, v_dn_norm_w, v_mix_w_out, v_swa_wq, v_swa_wk, v_swa_wv, v_swa_q_norm_w, v_swa_k_norm_w, v_swa_sinks, v_swa_wo, v_ffn_w_up, v_ffn_conv_w, v_ffn_w_down):
    ix, iy, ic = lax.axis_index("x"), lax.axis_index("y"), lax.axis_index("c")
    chip = 2 * ix + iy
    seq = x.shape[1]
    rows = HEAD0 + seq

    small_sharded = (conv_a_w, dn_conv_w, ffn_conv_w, meta_tokens)
    up_b, down_b = ffn_w_up.astype(BF16), ffn_w_down.astype(BF16)
    own = [mix_w_in[0].astype(BF16), mix_w_out[0].astype(BF16), swa_wq[0].astype(BF16), swa_wk[0].astype(BF16),
           swa_wv[0].astype(BF16), swa_wo[0].astype(BF16), up_b[0], up_b[1], down_b[0], down_b[1]]
    fill = lambda gathered, mine: [lax.dynamic_update_slice_in_dim(g, t[None], chip, axis=0)
                                   for g, t in zip(gathered, mine)]
    first, g_small = gather_weights(own[:1], _flat_pad(small_sharded, SW_ROWS))
    g_in, = fill(first, own[:1])
    w_in = jnp.pad(g_in.transpose(1, 0, 2).reshape(D, IN_DIM), ((0, 0), (0, P_W - IN_DIM)))
    rest = {}

    def tie(t):
        t, *mine = lax.optimization_barrier((t, *own[1:]))
        rest["w"] = fill(gather_weights_beside(mine), mine)
        return t

    def late():
        g_out, g_q, g_k, g_v, g_o, g_up0, g_up1, g_dn0, g_dn1 = rest["w"]
        wqkv = jnp.concatenate([g_q.reshape(D, D), g_k.reshape(D, 256), g_v.reshape(D, 256)], axis=1)
        return (g_out.reshape(D, D), wqkv, g_o.reshape(D, D), [g_up0, g_up1],
                [g_dn0.reshape(D_FF, D), g_dn1.reshape(D_FF, D)])

    gs = g_small.reshape(4, -1)
    ca_full = gs[:, 0:384].reshape(4, 3, 128).transpose(1, 0, 2).reshape(3, D_CONV)
    dc_full = gs[:, 384:1920].reshape(4, 4, 384).transpose(1, 0, 2).reshape(4, 3 * DN_DIM)
    fc_full = gs[:, 1920:6144].reshape(4, 2, 3, 704).transpose(1, 2, 0, 3).reshape(2, 3, D_FF)
    meta_full = gs[:, 6144:10240].reshape(4, N_META, 256).transpose(1, 0, 2).reshape(N_META, D)
    ca8, dc8 = _rows8(ca_full), _rows8(dc_full)
    fc8 = [_rows8(fc_full[0]), _rows8(fc_full[1])]
    alog, dtb = _lanes(dn_a_log[0], 4), _lanes(dn_dt_bias[0], 4)
    dnw = dn_norm_w.astype(F32)
    qw, kw = swa_q_norm_w.astype(F32), swa_k_norm_w.astype(F32)
    sinks = swa_sinks[0].astype(F32)
    anw = [attn_norm_w[0:1], attn_norm_w[1:2]]
    fnw = [ffn_norm_w[0:1], ffn_norm_w[1:2]]

    c_idx = jnp.reshape(ic, (1,)).astype(jnp.int32)
    chip_idx = jnp.stack([chip, ic]).astype(jnp.int32)
    begun = []

    def begin(tag, names, grads):
        begun.append((names, *reduce_begin(grads, names, c_idx, 2 + len(begun), tag, presum=tag == "mix")))

    (dh, loss_l, d_anw0, d_anw1, d_fnw0, d_fnw1, d_w_in, d_ca, d_dc, d_alog, d_dtb, d_dnw, d_w_out, d_wqkv, d_qw,
     d_kw, d_sinks, d_wo, d_up0, d_up1, d_fc0, d_fc1, d_down0, d_down1) = local_step(
        x[0], loss_target[0], meta_full, anw, fnw, w_in, ca8, dc8, alog, dtb, dnw, qw, kw, sinks, fc8, late,
        begin, tie)
    grad_x = dh[HEAD0:][None]

    small_parts = [jnp.concatenate([d_anw0, d_anw1], axis=0), jnp.concatenate([d_fnw0, d_fnw1], axis=0),
                   d_alog[0, 4:8], d_dtb[0, 4:8], d_dnw, d_qw, d_kw, d_sinks,
                   d_ca[:3], d_dc[:4], jnp.stack([d_fc0[:3], d_fc1[:3]]), dh[PAD:HEAD0], loss_l[0, 0:1]]
    small_shapes = [(2, D), (2, D), (1, 4), (1, 4), (1, DN_D), (1, SWA_D), (1, SWA_D), (1, SWA_H),
                    (1, 3, D_CONV), (1, 4, 3 * DN_DIM), (2, 3, D_FF), (N_META, D), ()]
    gathered_small = gather_small(_flat_pad(small_parts, SV_ROWS))

    red_big = {**reduce_end(begun[:-1], chip_idx), **reduce_end(begun[-1:], chip_idx)}
    g_w_in, g_w_out, g_wq, g_wk, g_wv, g_wo, g_up0, g_up1, g_dn0, g_dn1 = [
        red_big[n] for n in ("w_in", "w_out", "wq", "wk", "wv", "wo", "up0", "up1", "down0", "down1")]

    grads = dict(mix_w_in=g_w_in, mix_w_out=g_w_out, swa_wq=g_wq, swa_wk=g_wk, swa_wv=g_wv, swa_wo=g_wo,
                 ffn_w_up=[g_up0, g_up1], ffn_w_down=[g_dn0, g_dn1])
    weights = dict(meta_tokens=meta_tokens, attn_norm_w=attn_norm_w, ffn_norm_w=ffn_norm_w, mix_w_in=mix_w_in,
                   conv_a_w=conv_a_w, dn_conv_w=dn_conv_w, dn_a_log=dn_a_log, dn_dt_bias=dn_dt_bias,
                   dn_norm_w=dn_norm_w, mix_w_out=mix_w_out, swa_wq=swa_wq, swa_wk=swa_wk, swa_wv=swa_wv,
                   swa_q_norm_w=swa_q_norm_w, swa_k_norm_w=swa_k_norm_w, swa_sinks=swa_sinks, swa_wo=swa_wo,
                   ffn_w_up=ffn_w_up, ffn_conv_w=ffn_conv_w, ffn_w_down=ffn_w_down)
    m_in = dict(meta_tokens=m_meta_tokens, attn_norm_w=m_attn_norm_w, ffn_norm_w=m_ffn_norm_w, mix_w_in=m_mix_w_in,
                conv_a_w=m_conv_a_w, dn_conv_w=m_dn_conv_w, dn_a_log=m_dn_a_log, dn_dt_bias=m_dn_dt_bias,
                dn_norm_w=m_dn_norm_w, mix_w_out=m_mix_w_out, swa_wq=m_swa_wq, swa_wk=m_swa_wk, swa_wv=m_swa_wv,
                swa_q_norm_w=m_swa_q_norm_w, swa_k_norm_w=m_swa_k_norm_w, swa_sinks=m_swa_sinks, swa_wo=m_swa_wo,
                ffn_w_up=m_ffn_w_up, ffn_conv_w=m_ffn_conv_w, ffn_w_down=m_ffn_w_down)
    v_in = dict(meta_tokens=v_meta_tokens, attn_norm_w=v_attn_norm_w, ffn_norm_w=v_ffn_norm_w, mix_w_in=v_mix_w_in,
                conv_a_w=v_conv_a_w, dn_conv_w=v_dn_conv_w, dn_a_log=v_dn_a_log, dn_dt_bias=v_dn_dt_bias,
                dn_norm_w=v_dn_norm_w, mix_w_out=v_mix_w_out, swa_wq=v_swa_wq, swa_wk=v_swa_wk, swa_wv=v_swa_wv,
                swa_q_norm_w=v_swa_q_norm_w, swa_k_norm_w=v_swa_k_norm_w, swa_sinks=v_swa_sinks, swa_wo=v_swa_wo,
                ffn_w_up=v_ffn_w_up, ffn_conv_w=v_ffn_conv_w, ffn_w_down=v_ffn_w_down)
    names = list(weights)
    small = [n for n in names if n not in BIG]
    delta, new_m, new_v = {}, {}, {}
    for n in BIG:
        delta[n], new_m[n], new_v[n], grads[n] = adamw(weights[n], grads[n], m_in[n], v_in[n], name=f"adamw_{n}")
    gathered_small, _ = lax.optimization_barrier((gathered_small, new_v["ffn_w_down"]))
    (g_anw, g_fnw, g_alog, g_dtb, g_dnw, g_qw, g_kw, g_sinks, g_ca_f, g_dc_f, g_fc_f, g_meta_f,
     loss) = _split_flat(sum_slots(gathered_small), small_shapes)
    grads.update(meta_tokens=lax.dynamic_slice_in_dim(g_meta_f, chip * 256, 256, axis=1), attn_norm_w=g_anw,
                 ffn_norm_w=g_fnw, conv_a_w=lax.dynamic_slice_in_dim(g_ca_f, chip * 128, 128, axis=2),
                 dn_conv_w=lax.dynamic_slice_in_dim(g_dc_f, chip * 384, 384, axis=2), dn_a_log=g_alog,
                 dn_dt_bias=g_dtb, dn_norm_w=g_dnw, swa_q_norm_w=g_qw, swa_k_norm_w=g_kw, swa_sinks=g_sinks,
                 ffn_conv_w=lax.dynamic_slice_in_dim(g_fc_f, chip * 704, 704, axis=2))
    grads = {n: grads[n].reshape(weights[n].shape) for n in names}
    shapes = [weights[n].shape for n in small]
    packed = [_flat_pad([t[n] for n in small], SW_ROWS) for t in (weights, grads, m_in, v_in)]
    for store, flat in zip((delta, new_m, new_v), adamw(*packed, name="adamw_small")):
        for n, t in zip(small, _split_flat(flat, shapes)):
            store[n] = t
    return (loss, grad_x, *[grads[n] for n in names], *[delta[n] for n in names],
            *[new_m[n] for n in names], *[new_v[n] for n in names])
```

```python
import functools

import jax
import jax.numpy as jnp
from jax import lax
from jax.experimental import pallas as pl
from jax.experimental.pallas import tpu as pltpu
from jax.experimental.pallas import tpu_sc as plsc

F32 = jnp.float32
BF16 = jnp.bfloat16
HI = lax.Precision.HIGHEST
MESH = pl.DeviceIdType.MESH

D = 1024
N_META = 16
PAD = 112
HEAD0 = PAD + N_META
D_CONV = 512
DN_H = 4
DN_D = 128
DN_DIM = 512
CH = 64
IN_DIM = 3592
P_W = 3840
BG0 = 3584
SWA_H = 16
SWA_KV = 4
SWA_D = 64
BLK = 128
NKEY = N_META + 2 * BLK
D_FF = 2816
EPS = 1e-6
LR, B1, B2, AEPS, WD, STEP = 0.001, 0.9, 0.999, 1e-08, 0.01, 10
VMEM_LIMIT = 48 * 1024 * 1024
MM_VMEM_BUDGET = 34 * 1024 * 1024
R_BIG = 6144
R_HALF = R_BIG // 2
SV_ROWS = 48
SW_ROWS = 16


def _pick(n, cands):
    for c in cands:
        if n % c == 0:
            return c
    return n


def _params(sem=None):
    return pltpu.CompilerParams(dimension_semantics=sem, vmem_limit_bytes=VMEM_LIMIT)


def _dot(a, b, ca=1, cb=0, prec=None):
    return lax.dot_general(a, b, (((ca,), (cb,)), ((), ())), precision=prec,
                           preferred_element_type=F32)


def _sigmoid(x):
    return 1.0 / (1.0 + jnp.exp(-x))


def _silu(x):
    return x * _sigmoid(x)


def _dsilu(x):
    s = _sigmoid(x)
    return s * (1.0 + x * (1.0 - s))


def _softplus(x):
    return jnp.maximum(x, 0.0) + jnp.log(1.0 + jnp.exp(-jnp.abs(x)))


def mm(a, b, *, name, ta=False, tb=False, out_dtype=F32, add=None, tm=None, tn=None, tk=None,
       b_chip=False, out_chip=False, swap_mid=False, epi=None, epi_ins=(), epi_consts=(), epi_outs=(), epi_accs=()):
    if epi is not None:
        return _mm_epi(a, b, name=name, tb=tb, tn=tn, b_chip=b_chip, swap_mid=swap_mid, epi=epi, epi_ins=epi_ins,
                       epi_consts=epi_consts, epi_outs=epi_outs, epi_accs=epi_accs)
    chip_of = _chip_order(swap_mid)
    m, k = (a.shape[1], a.shape[0]) if ta else a.shape
    if b_chip:
        n = b.shape[1] if tb else 4 * b.shape[2]
        if tb:
            tk = b.shape[2]
        else:
            tn = b.shape[2]
    else:
        n = b.shape[0] if tb else b.shape[1]
    if out_chip:
        tn = n // 4
    tn = tn or _pick(n, (1408, 1024, 768, 512, 256, 128))
    tk = tk or (_pick(k, (1408, 704, 384, 128)) if ta else _pick(k, (1024, 1408, 768, 512, 128)))
    nk = k // tk
    if tm is None:
        isz = lambda t: jnp.dtype(t.dtype).itemsize
        osz = jnp.dtype(out_dtype).itemsize
        for tm in ((1408, 1024, 512, 384, 256, 128) if ta else (1408, 704, 512, 384, 256, 128)):
            need = 2 * (tm * tk * isz(a) + tk * tn * isz(b) + tm * tn * osz + (tm * tn * 4 if add is not None else 0))
            need += tm * tn * 4 if nk > 1 else 0
            if m % tm == 0 and need <= MM_VMEM_BUDGET:
                break
        else:
            tm = m
    dims = (((0 if ta else 1,), (1 if tb else 0,)), ((), ()))

    def body(*refs):
        if add is None:
            a_ref, b_ref, o_ref, acc_ref = refs
            add_ref = None
        else:
            a_ref, b_ref, add_ref, o_ref, acc_ref = refs
        part = lax.dot_general(a_ref[...].astype(BF16), b_ref[...].astype(BF16), dims,
                               preferred_element_type=F32)

        def finish(total):
            if add_ref is not None:
                total = total + add_ref[...]
            o_ref[...] = total.astype(out_dtype)

        if nk == 1:
            finish(part)
        else:
            kk = pl.program_id(2)

            @pl.when(kk == 0)
            def _():
                acc_ref[...] = part

            @pl.when(kk > 0)
            def _():
                acc_ref[...] += part

            @pl.when(kk == nk - 1)
            def _():
                finish(acc_ref[...])

    a_spec = pl.BlockSpec((tk, tm), lambda i, j, kk: (kk, i)) if ta else pl.BlockSpec((tm, tk), lambda i, j, kk: (i, kk))
    if b_chip and tb:
        b_spec = pl.BlockSpec((None, tn, tk), lambda i, j, kk: (chip_of(kk), j, 0))
    elif b_chip:
        b_spec = pl.BlockSpec((None, tk, tn), lambda i, j, kk: (j, kk, 0))
    elif tb:
        b_spec = pl.BlockSpec((tn, tk), lambda i, j, kk: (j, kk))
    else:
        b_spec = pl.BlockSpec((tk, tn), lambda i, j, kk: (kk, j))
    o_spec = pl.BlockSpec((tm, tn), lambda i, j, kk: (i, j))
    in_specs = [a_spec, b_spec] + ([o_spec] if add is not None else [])
    args = [a, b] + ([add] if add is not None else [])
    out_spec = pl.BlockSpec((None, tm, tn), lambda i, j, kk: (chip_of(j), i, 0)) if out_chip else o_spec
    return pl.pallas_call(
        body, name=name, interpret=False,
        out_shape=jax.ShapeDtypeStruct((4, m, tn) if out_chip else (m, n), out_dtype),
        grid=(m // tm, n // tn, nk), in_specs=in_specs, out_specs=out_spec,
        scratch_shapes=[pltpu.VMEM((tm, tn) if nk > 1 else (8, 128), F32)],
        compiler_params=_params(("parallel", "parallel", "arbitrary")),
    )(*args)


def _chip_order(swap_mid):
    return (lambda k: (k % 2) * 2 + k // 2) if swap_mid else (lambda k: k)


def _mm_epi(a, b, *, name, tb, tn, b_chip, epi, epi_ins, epi_consts, epi_outs, epi_accs, swap_mid=False):
    chip_of = _chip_order(swap_mid)
    m, k = a.shape
    if b_chip:
        n = b.shape[1] if tb else 4 * b.shape[2]
        tk = b.shape[2] if tb else None
        tn = tn if tb else b.shape[2]
    else:
        n = b.shape[0] if tb else b.shape[1]
        tk = None
    tn = tn or _pick(n, (1408, 1024, 768, 512, 256, 128))
    tk = tk or _pick(k, (1024, 1408, 1280, 768, 512, 128))
    nk, nj = k // tk, n // tn
    isz = lambda t: jnp.dtype(t.dtype if hasattr(t, "dtype") else t).itemsize
    outs3 = [t if isinstance(t, tuple) else (t, n, lambda j: j) for t in epi_outs]
    side = sum(isz(t) for t, _ in epi_ins) + sum(isz(dt) for dt, _, _ in outs3)
    for tm in (1408, 704, 512, 384, 256, 128):
        need = 2 * (tm * tk * isz(a) + tk * tn * isz(b) + tm * tn * side) + (tm * tn * 4 if nk > 1 else 0)
        if m % tm == 0 and need <= MM_VMEM_BUDGET:
            break
    else:
        tm = m
    dims = (((1,), (1 if tb else 0,)), ((), ()))
    n_in, n_c, n_out, n_acc = len(epi_ins), len(epi_consts), len(epi_outs), len(epi_accs)

    def body(*refs):
        a_ref, b_ref = refs[:2]
        in_refs = refs[2:2 + n_in + n_c]
        out_refs = refs[2 + n_in + n_c:2 + n_in + n_c + n_out]
        acc_out = refs[2 + n_in + n_c + n_out:2 + n_in + n_c + n_out + n_acc]
        acc_ref = refs[-1]
        i, j, kk = pl.program_id(0), pl.program_id(1), pl.program_id(2)
        part = lax.dot_general(a_ref[...].astype(BF16), b_ref[...].astype(BF16), dims,
                               preferred_element_type=F32)

        def finish(total):
            res = epi(i * tm, total, *[r[...] for r in in_refs])
            if not isinstance(res, (tuple, list)):
                res = (res,)
            for r, v in zip(out_refs, res[:n_out]):
                r[...] = v.astype(r.dtype)
            if n_acc:
                @pl.when(jnp.logical_and(i == 0, j == 0))
                def _():
                    for r in acc_out:
                        r[...] = jnp.zeros(r.shape, r.dtype)

                for r, v in zip(acc_out, res[n_out:]):
                    r[...] += jnp.broadcast_to(v, r.shape).astype(r.dtype)

        if nk == 1:
            finish(part)
        else:
            @pl.when(kk == 0)
            def _():
                acc_ref[...] = part

            @pl.when(kk > 0)
            def _():
                acc_ref[...] += part

            @pl.when(kk == nk - 1)
            def _():
                finish(acc_ref[...])

    a_spec = pl.BlockSpec((tm, tk), lambda i, j, kk: (i, kk))
    if b_chip and tb:
        b_spec = pl.BlockSpec((None, tn, tk), lambda i, j, kk: (chip_of(kk), j, 0))
    elif b_chip:
        b_spec = pl.BlockSpec((None, tk, tn), lambda i, j, kk: (j, kk, 0))
    elif tb:
        b_spec = pl.BlockSpec((tn, tk), lambda i, j, kk: (j, kk))
    else:
        b_spec = pl.BlockSpec((tk, tn), lambda i, j, kk: (kk, j))
    in_specs = [a_spec, b_spec]
    in_specs += [pl.BlockSpec((tm, tn), lambda i, j, kk, col=col: (i, col(j))) for _, col in epi_ins]
    in_specs += [pl.BlockSpec(t.shape, lambda i, j, kk, nd=t.ndim: (0,) * nd) for t in epi_consts]
    out_specs = [pl.BlockSpec((tm, tn), lambda i, j, kk, col=col: (i, col(j))) for _, _, col in outs3]
    out_specs += [pl.BlockSpec(s, lambda i, j, kk, nd=len(s): (0,) * nd) for s, _ in epi_accs]
    out_shape = [jax.ShapeDtypeStruct((m, width), dt) for dt, width, _ in outs3]
    out_shape += [jax.ShapeDtypeStruct(s, dt) for s, dt in epi_accs]
    sem = ("arbitrary", "arbitrary", "arbitrary") if n_acc else ("parallel", "parallel", "arbitrary")
    return pl.pallas_call(
        body, name=name, interpret=False, out_shape=out_shape,
        grid=(m // tm, nj, nk), in_specs=in_specs, out_specs=out_specs,
        scratch_shapes=[pltpu.VMEM((tm, tn) if nk > 1 else (8, 128), F32)],
        compiler_params=_params(sem),
    )(a, b, *[t for t, _ in epi_ins], *epi_consts)


def cols(arr, tr, width=None, cb=0):
    width = width or arr.shape[1]
    return (arr, (tr, width), lambda i: (i, cb), "r2")


def heads(arr, tr):
    return (arr, (arr.shape[0], tr, arr.shape[2]), lambda i: (0, i, 0), "r3")


def whole(arr):
    nd = arr.ndim
    return (arr, arr.shape, lambda i: (0,) * nd, "w")


STRIP = 16


def _rows_of(ref, kind, r0, n):
    if kind == "r2":
        return ref[pl.ds(r0, n), :]
    if kind == "r3":
        return ref[:, pl.ds(r0, n), :]
    return ref[...]


def _set_rows(ref, kind, r0, n, v):
    if kind == "r2":
        ref[pl.ds(r0, n), :] = v.astype(ref.dtype)
    elif kind == "r3":
        ref[:, pl.ds(r0, n), :] = v.astype(ref.dtype)
    else:
        ref[...] = v.astype(ref.dtype)


def rowwise(fn, ins, outs, *, steps, name, accs=(), strip=None):
    n_in, n_out, n_acc = len(ins), len(outs), len(accs)
    kin = [t[3] for t in ins]
    kout = [t[4] for t in outs]
    tr = next((t[1][-2] for t in ins if t[3] != "w"), 0)

    def body(*refs):
        i = pl.program_id(0)
        in_refs, out_refs, acc_refs = refs[:n_in], refs[n_in:n_in + n_out], refs[n_in + n_out:]
        if n_acc:
            @pl.when(i == 0)
            def _():
                for r in acc_refs:
                    r[...] = jnp.zeros(r.shape, r.dtype)

        def run(r0, n):
            res = fn(i * tr + r0, *[_rows_of(r, k, r0, n) for r, k in zip(in_refs, kin)])
            if not isinstance(res, (tuple, list)):
                res = (res,)
            for r, k, v in zip(out_refs, kout, res[:n_out]):
                _set_rows(r, k, r0, n, v)
            for r, v in zip(acc_refs, res[n_out:]):
                r[...] += jnp.broadcast_to(v, r.shape).astype(r.dtype)

        if strip is None or tr <= strip:
            run(0, tr)
        else:
            def step(s, carry):
                run(pl.multiple_of(s * strip, strip), strip)
                return carry
            lax.fori_loop(0, tr // strip, step, 0)

    def zmap(nd):
        return lambda i: (0,) * nd

    in_specs = [pl.BlockSpec(t[1], t[2]) for t in ins]
    out_specs = [pl.BlockSpec(t[2], t[3]) for t in outs]
    out_specs += [pl.BlockSpec(s, zmap(len(s))) for s, _ in accs]
    out_shape = [jax.ShapeDtypeStruct(t[0], t[1]) for t in outs]
    out_shape += [jax.ShapeDtypeStruct(s, d) for s, d in accs]
    res = pl.pallas_call(
        body, name=name, interpret=False, out_shape=out_shape, grid=(steps,),
        in_specs=in_specs, out_specs=out_specs,
        compiler_params=_params(("arbitrary",)),
    )(*[t[0] for t in ins])
    return res


def out2d(rows, width, dtype, tr):
    return ((rows, width), dtype, (tr, width), lambda i: (i, 0), "r2")


def conv_fwd(xs, w8, kw, *, rows, c, tc, tr, name, post, extras=(), outs=(), pre=None, strip=STRIP):
    nx, ne, no = len(xs), len(extras), len(outs)
    nr, nc = rows // tr, c // tc
    r8 = tr // 8
    st = strip

    def body(*refs):
        x_refs = refs[:2 * nx]
        w_ref = refs[2 * nx]
        e_refs = refs[2 * nx + 1:2 * nx + 1 + ne]
        o_refs = refs[2 * nx + 1 + ne:2 * nx + 1 + ne + no]
        scr = refs[-1]
        j, i = pl.program_id(0), pl.program_id(1)
        halo = [x_refs[2 * q + 1][...].astype(F32) for q in range(nx)]
        scr[0:8, :] = jnp.where(i > 0, pre(*halo) if pre else halo[0], 0.0)

        def fill(s, carry):
            r0 = pl.multiple_of(s * st, st)
            cur = [x_refs[2 * q][pl.ds(r0, st), :].astype(F32) for q in range(nx)]
            scr[pl.ds(8 + r0, st), :] = pre(*cur) if pre else cur[0]
            return carry

        def comp(s, carry):
            r0 = pl.multiple_of(s * st, st)
            win = scr[pl.ds(r0, st + 8), :]
            y = jnp.zeros((st, tc), F32)
            for q in range(kw):
                sh = kw - 1 - q
                y = y + w_ref[q:q + 1, :] * win[8 - sh:8 - sh + st]
            res = post(j, y, *[e[pl.ds(r0, st), :] for e in e_refs])
            if not isinstance(res, (tuple, list)):
                res = (res,)
            for r, v in zip(o_refs, res):
                r[pl.ds(r0, st), :] = v.astype(r.dtype)
            return carry

        lax.fori_loop(0, tr // st, fill, 0)
        lax.fori_loop(0, tr // st, comp, 0)

    in_specs, args = [], []
    for arr, cb0 in xs:
        in_specs.append(pl.BlockSpec((tr, tc), lambda j, i, cb0=cb0: (i, cb0 + j)))
        in_specs.append(pl.BlockSpec((8, tc), lambda j, i, cb0=cb0: (jnp.maximum(i * r8 - 1, 0), cb0 + j)))
        args += [arr, arr]
    in_specs.append(pl.BlockSpec((8, tc), lambda j, i: (0, j)))
    args.append(w8)
    for arr, cb0 in extras:
        in_specs.append(pl.BlockSpec((tr, tc), lambda j, i, cb0=cb0: (i, cb0 + j)))
        args.append(arr)
    return pl.pallas_call(
        body, name=name, interpret=False,
        out_shape=[jax.ShapeDtypeStruct((rows, c), dt) for dt in outs],
        grid=(nc, nr), in_specs=in_specs,
        out_specs=[pl.BlockSpec((tr, tc), lambda j, i: (i, j)) for _ in outs],
        scratch_shapes=[pltpu.VMEM((tr + 8, tc), F32)],
        compiler_params=_params(("parallel", "arbitrary")),
    )(*args)


def conv_bwd(xs, w8, kw, dy, *, rows, c, tc, tr, name, post, extras=(), outs=(), pre=None):
    nx, ne, no = len(xs), len(extras), len(outs)
    nr, nc = rows // tr, c // tc
    r8 = tr // 8

    def body(*refs):
        x_refs = refs[:2 * nx]
        w_ref, dy_ref, dyn_ref = refs[2 * nx:2 * nx + 3]
        e_refs = refs[2 * nx + 3:2 * nx + 3 + ne]
        first_out = 2 * nx + 3 + ne
        o_refs = refs[first_out:first_out + no]
        dw_ref = refs[first_out + no]
        xscr, gscr = refs[-2], refs[-1]
        i = pl.program_id(1)
        halo = [x_refs[2 * q + 1][...].astype(F32) for q in range(nx)]
        xscr[0:8, :] = jnp.where(i > 0, pre(*halo) if pre else halo[0], 0.0)
        gscr[tr:tr + 8, :] = jnp.where(i < nr - 1, dyn_ref[...].astype(F32), 0.0)

        def fill(s, carry):
            r0 = pl.multiple_of(s * STRIP, STRIP)
            cur = [x_refs[2 * q][pl.ds(r0, STRIP), :].astype(F32) for q in range(nx)]
            xscr[pl.ds(8 + r0, STRIP), :] = pre(*cur) if pre else cur[0]
            gscr[pl.ds(r0, STRIP), :] = dy_ref[pl.ds(r0, STRIP), :].astype(F32)
            return carry

        def comp(s, dws):
            r0 = pl.multiple_of(s * STRIP, STRIP)
            gwin = gscr[pl.ds(r0, STRIP + 8), :]
            xwin = xscr[pl.ds(r0, STRIP + 8), :]
            g = gwin[0:STRIP]
            dx = jnp.zeros((STRIP, tc), F32)
            new = []
            for q in range(kw):
                sh = kw - 1 - q
                dx = dx + w_ref[q:q + 1, :] * gwin[sh:sh + STRIP]
                part = g * xwin[8 - sh:8 - sh + STRIP]
                new.append(dws[q] + part[0:8] + part[8:16])
            res = post(dx, *[e[pl.ds(r0, STRIP), :] for e in e_refs])
            if not isinstance(res, (tuple, list)):
                res = (res,)
            for r, v in zip(o_refs, res):
                r[pl.ds(r0, STRIP), :] = v.astype(r.dtype)
            return tuple(new)

        lax.fori_loop(0, tr // STRIP, fill, 0)
        dws = lax.fori_loop(0, tr // STRIP, comp, tuple(jnp.zeros((8, tc), F32) for _ in range(kw)))

        @pl.when(i == 0)
        def _():
            dw_ref[...] = jnp.zeros((8, tc), F32)

        dw_ref[...] += jnp.concatenate([jnp.sum(t, axis=0, keepdims=True) for t in dws]
                                       + [jnp.zeros((8 - kw, tc), F32)], axis=0)

    in_specs, args = [], []
    for arr, cb0 in xs:
        in_specs.append(pl.BlockSpec((tr, tc), lambda j, i, cb0=cb0: (i, cb0 + j)))
        in_specs.append(pl.BlockSpec((8, tc), lambda j, i, cb0=cb0: (jnp.maximum(i * r8 - 1, 0), cb0 + j)))
        args += [arr, arr]
    in_specs.append(pl.BlockSpec((8, tc), lambda j, i: (0, j)))
    in_specs.append(pl.BlockSpec((tr, tc), lambda j, i: (i, j)))
    in_specs.append(pl.BlockSpec((8, tc), lambda j, i: (jnp.minimum((i + 1) * r8, nr * r8 - 1), j)))
    args += [w8, dy, dy]
    for arr, cb0 in extras:
        in_specs.append(pl.BlockSpec((tr, tc), lambda j, i, cb0=cb0: (i, cb0 + j)))
        args.append(arr)
    return pl.pallas_call(
        body, name=name, interpret=False,
        out_shape=[jax.ShapeDtypeStruct((rows, c), dt) for dt in outs] + [jax.ShapeDtypeStruct((8, c), F32)],
        grid=(nc, nr), in_specs=in_specs,
        out_specs=[pl.BlockSpec((tr, tc), lambda j, i: (i, j)) for _ in outs] + [pl.BlockSpec((8, tc), lambda j, i: (0, j))],
        scratch_shapes=[pltpu.VMEM((tr + 8, tc), F32), pltpu.VMEM((tr + 8, tc), F32)],
        compiler_params=_params(("parallel", "arbitrary")),
    )(*args)


def rms_fwd(h, w, *, name):
    rows = h.shape[0]
    tr = _pick(rows, (384, 128))

    def fn(i, x, wv):
        r = lax.rsqrt(jnp.mean(x * x, axis=1, keepdims=True) + EPS)
        return x * r * wv

    return rowwise(fn, [cols(h, tr), whole(w)], [out2d(rows, D, BF16, tr)], steps=rows // tr, name=name)[0]


def _rms_bwd_epi(row0, g, x, dr, wv):
    r = lax.rsqrt(jnp.mean(x * x, axis=1, keepdims=True) + EPS)
    xh = x * r
    gw = g * wv
    dx = r * (gw - xh * jnp.mean(gw * xh, axis=1, keepdims=True))
    row = row0 + lax.broadcasted_iota(jnp.int32, (x.shape[0], 1), 0)
    return jnp.where(row >= PAD, dr + dx, 0.0), jnp.sum(g * xh, axis=0, keepdims=True)


def dx_rms_bwd(dy, w, h, nw, dres, *, name, b_chip=False, swap_mid=False):
    return mm(dy, w, tb=True, b_chip=b_chip, swap_mid=swap_mid, tn=D, name=name, epi=_rms_bwd_epi,
              epi_ins=[(h, lambda j: 0), (dres, lambda j: 0)], epi_consts=[nw], epi_outs=[F32],
              epi_accs=[((1, D), F32)])


def loss_grad(h, target):
    rows = h.shape[0]

    def fn(i, y, t):
        diff = jnp.where(i >= HEAD0, y - t, 0.0)
        part = jnp.sum(jnp.sum(diff * diff, axis=1, keepdims=True), axis=0, keepdims=True)
        return diff * (1.0 / D), part * (0.5 / D)

    tgt = (target, (BLK, D), lambda i: (jnp.maximum(i - 1, 0), 0), "r2")
    return rowwise(fn, [cols(h, BLK), tgt], [out2d(rows, D, F32, BLK)], steps=rows // BLK,
                   name="loss_grad", accs=[((1, 128), F32)])


def adamw(w, g, m, v, *, name):
    shape = w.shape
    gs = list(g) if isinstance(g, (list, tuple)) else [g]
    nl = len(gs)
    width = shape[-1]
    rows = w.size // width
    rl = rows // nl
    tr = _pick(rl, (256, 176, 128, 64, 16, 8))
    nr = rl // tr
    if w.ndim == 3 and shape[1] % tr == 0:
        per = shape[1] // tr
        view = lambda t: (t, (None, tr, width), lambda i: (i // per, i % per, 0), "r2")
        out = (shape, F32, (None, tr, width), lambda i: (i // per, i % per, 0), "r2")
    else:
        view = lambda t: cols(t.reshape(rows, width), tr)
        out = out2d(rows, width, F32, tr)

    def fn(i, wv, mv, vv, *gvs):
        gv = gvs[0]
        for layer in range(1, nl):
            gv = jnp.where(i >= layer * rl, gvs[layer], gv)
        mn = B1 * mv + (1.0 - B1) * gv
        vn = B2 * vv + (1.0 - B2) * gv * gv
        mh = mn / (1.0 - B1 ** STEP)
        vh = vn / (1.0 - B2 ** STEP)
        return -LR * (mh / (jnp.sqrt(vh) + AEPS) + WD * wv), mn, vn, gv

    g_ins = [(t.reshape(rl, width), (tr, width), lambda i, layer=layer: (jnp.clip(i - layer * nr, 0, nr - 1), 0), "r2")
             for layer, t in enumerate(gs)]
    res = rowwise(fn, [view(t) for t in (w, m, v)] + g_ins, [out] * 4, steps=rows // tr, name=name)
    return [r.reshape(shape) for r in res]


HB = DN_H * CH
PAIR = 3


def _split(a):
    hi = a.astype(BF16)
    return hi, (a - hi.astype(F32)).astype(BF16)


def _dot1(a, b, ca=1, cb=0):
    return _dot(a.astype(BF16), b.astype(BF16), ca, cb)


def _dot3(a, b, ca=1, cb=0):
    ah, al = _split(a)
    bh, bl = _split(b)
    return _dot(ah, bh, ca, cb) + (_dot(ah, bl, ca, cb) + _dot(al, bh, ca, cb))


def _dot01(m01, b, ca=1, cb=0):
    bh, bl = _split(b)
    m = m01.astype(BF16)
    return _dot(m, bh, ca, cb) + _dot(m, bl, ca, cb)


def _stack(x):
    return jnp.concatenate([x[:, h * DN_D:(h + 1) * DN_D] for h in range(DN_H)], axis=0)


def _unstack(x):
    return jnp.concatenate([x[h * CH:(h + 1) * CH] for h in range(DN_H)], axis=1)


def _tri_inv(mats, blk, eye):
    each = lambda f, *lists: [f(*t) for t in zip(*lists)]
    ad = [jnp.where(blk, a, 0.0) for a in mats]
    lo = each(lambda a, d: a - d, mats, ad)
    a2 = each(_dot3, ad, ad)
    a4 = each(_dot3, a2, a2)
    a8 = each(_dot3, a4, a4)
    dgi = each(lambda d, s: _dot3(eye - d, eye + s), ad, a2)
    dgi = each(lambda p, s: _dot3(p, eye + s), dgi, a4)
    dgi = each(lambda p, s: _dot3(p, eye + s), dgi, a8)
    n = each(_dot3, dgi, lo)
    n2 = each(_dot3, n, n)
    return each(_dot3, each(lambda u, v: _dot3(eye - u, eye + v), n, n2), dgi)


def _dn_masks():
    row = lax.broadcasted_iota(jnp.int32, (HB, HB), 0)
    col = lax.broadcasted_iota(jnp.int32, (HB, HB), 1)
    same = (row // CH) == (col // CH)
    incl = jnp.logical_and(same, row >= col)
    strict = jnp.logical_and(same, row > col)
    upper = jnp.logical_and(same, row <= col)
    blk = (row // 16) == (col // 16)
    eye = (row == col).astype(F32)
    return incl, strict, upper, blk, eye


def _dn_chunk(qv, kv, vv, bc, br, incl, strict):
    r64 = lax.broadcasted_iota(jnp.int32, (CH, CH), 0)
    c64 = lax.broadcasted_iota(jnp.int32, (CH, CH), 1)
    dcol = _dot01((r64 >= c64).astype(F32), bc)
    drow = _dot3(br, (r64 <= c64).astype(F32))
    col = lambda m, l0: jnp.concatenate([m[:, l0 + h:l0 + h + 1] for h in range(DN_H)], axis=0)
    b_c = col(bc, 0)
    d_c = col(dcol, 4)
    d_r = jnp.concatenate([drow[4 + h:5 + h, :] for h in range(DN_H)], axis=1)
    d_last_h = [dcol[CH - 1:CH, 4 + h:5 + h] for h in range(DN_H)]
    d_last = jnp.concatenate([jnp.broadcast_to(t, (CH, 1)) for t in d_last_h], axis=0)
    q, k, v = _stack(qv), _stack(kv), _stack(vv)
    dm = jnp.where(incl, jnp.exp(jnp.where(incl, d_c - d_r, 0.0)), 0.0)
    kk = _dot1(k, k, 1, 1)
    a = jnp.where(strict, b_c * kk * dm, 0.0)
    ed = jnp.exp(d_c)
    rhs = jnp.concatenate([v * b_c, k * (b_c * ed)], axis=1)
    qk = _dot1(q, k, 1, 1) * dm
    ekd = jnp.exp(d_last - d_c)
    gl = [jnp.exp(t) for t in d_last_h]
    return q, k, v, b_c, dm, kk, a, ed, rhs, qk, ekd, gl


def dn_fwd(qkv_n, bgcol, bgrow):
    rows = qkv_n.shape[0]
    nch = rows // CH

    def body(q_ref, k_ref, v_ref, bc_ref, br_ref, o_ref, s_out, ti_out, s_scr, prep, prep_qk, prep_gl):
        n = pl.program_id(0)

        @pl.when(n == 0)
        def _():
            s_scr[...] = jnp.zeros(s_scr.shape, F32)
            prep[...] = jnp.zeros(prep.shape, F32)
            prep_qk[...] = jnp.zeros(prep_qk.shape, F32)
            prep_gl[...] = jnp.zeros(prep_gl.shape, F32)

        live = n > 0
        for c in range(PAIR):
            u, w, qd, kd = prep[c, 0], prep[c, 1], prep[c, 2], prep[c, 3]
            v_new, o_state = [], []
            for h in range(DN_H):
                rs = slice(h * CH, (h + 1) * CH)
                s = s_scr[h]
                s_out[c, h] = s
                vn = u[rs] - _dot1(w[rs], s)
                v_new.append(vn)
                o_state.append(_dot1(qd[rs], s))
                s_scr[h] = jnp.where(live, prep_gl[c, h:h + 1, 0:1] * s + _dot1(kd[rs], vn, 0, 0), s)
            o = jnp.concatenate(o_state, axis=0) + _dot1(prep_qk[c], jnp.concatenate(v_new, axis=0))
            o_ref[c * CH:(c + 1) * CH, :] = _unstack(o)

        incl, strict, _, blk, eye = _dn_masks()
        parts = []
        for c in range(PAIR):
            rows_c = slice(c * CH, (c + 1) * CH)
            parts.append(_dn_chunk(q_ref[rows_c, :], k_ref[rows_c, :], v_ref[rows_c, :], bc_ref[rows_c, :],
                                   br_ref[c], incl, strict))
        tinvs = _tri_inv([p[6] for p in parts], blk, eye)
        for c, (q, k, v, b_c, dm, kk, a, ed, rhs, qk_n, ekd, gl) in enumerate(parts):
            tinv = tinvs[c]
            ti_out[c] = tinv
            sol = _dot3(tinv, rhs)
            prep[c, 0] = sol[:, :DN_D]
            prep[c, 1] = sol[:, DN_D:]
            prep[c, 2] = q * ed
            prep[c, 3] = k * ekd
            prep_qk[c] = qk_n
            prep_gl[c] = jnp.concatenate([jnp.broadcast_to(t, (1, 128)) for t in gl]
                                         + [jnp.zeros((8 - DN_H, 128), F32)], axis=0)

    assert nch % PAIR == 0
    npair = nch // PAIR
    last = npair - 1
    return pl.pallas_call(
        body, name="dn_fwd", interpret=False,
        out_shape=[jax.ShapeDtypeStruct((rows, DN_DIM), F32),
                   jax.ShapeDtypeStruct((nch, DN_H, DN_D, DN_D), F32),
                   jax.ShapeDtypeStruct((nch, HB, HB), F32)],
        grid=(npair + 1,),
        in_specs=[pl.BlockSpec((PAIR * CH, DN_DIM), lambda n: (jnp.minimum(n, last), 0)),
                  pl.BlockSpec((PAIR * CH, DN_DIM), lambda n: (jnp.minimum(n, last), 1)),
                  pl.BlockSpec((PAIR * CH, DN_DIM), lambda n: (jnp.minimum(n, last), 2)),
                  pl.BlockSpec((PAIR * CH, 128), lambda n: (jnp.minimum(n, last), 0)),
                  pl.BlockSpec((PAIR, 8, CH), lambda n: (jnp.minimum(n, last), 0, 0))],
        out_specs=[pl.BlockSpec((PAIR * CH, DN_DIM), lambda n: (jnp.maximum(n - 1, 0), 0)),
                   pl.BlockSpec((PAIR, DN_H, DN_D, DN_D), lambda n: (jnp.maximum(n - 1, 0), 0, 0, 0)),
                   pl.BlockSpec((PAIR, HB, HB), lambda n: (jnp.minimum(n, last), 0, 0))],
        scratch_shapes=[pltpu.VMEM((DN_H, DN_D, DN_D), F32), pltpu.VMEM((PAIR, 4, HB, DN_D), F32),
                        pltpu.VMEM((PAIR, HB, HB), F32), pltpu.VMEM((PAIR, 8, 128), F32)],
        compiler_params=_params(("arbitrary",)),
    )(qkv_n, qkv_n, qkv_n, bgcol, bgrow)


def dn_bwd(qkv_n, bgcol, bgrow, s_all, ti_all, do):
    rows = qkv_n.shape[0]
    nch = rows // CH

    def body(q_ref, k_ref, v_ref, bc_ref, br_ref, s_ref, ti_ref, do_ref, dq_ref, dk_ref, dv_ref, dbg_ref, ds_scr):
        n = pl.program_id(0)

        @pl.when(n == 0)
        def _():
            ds_scr[...] = jnp.zeros(ds_scr.shape, F32)

        incl, strict, upper, _, _ = _dn_masks()
        rsum = lambda t: jnp.sum(t, axis=1, keepdims=True)
        rows_of = [slice(h * CH, (h + 1) * CH) for h in range(DN_H)]
        heads_of = lambda f: jnp.concatenate([f(h, rs) for h, rs in enumerate(rows_of)], axis=0)
        cs = []
        for c in reversed(range(PAIR)):
            rc = slice(c * CH, (c + 1) * CH)
            q, k, v, b_c, dm, kk, a, ed, rhs, qk, ekd, gl = _dn_chunk(
                q_ref[rc, :], k_ref[rc, :], v_ref[rc, :], bc_ref[rc, :], br_ref[c], incl, strict)
            cs.append(dict(rc=rc, q=q, k=k, v=v, b_c=b_c, dm=dm, kk=kk, a=a, ed=ed, rhs=rhs, qk=qk, ekd=ekd, gl=gl,
                           tinv=ti_ref[c], g=_stack(do_ref[rc, :]), s=[s_ref[c, h] for h in range(DN_H)]))
        for t in cs:
            t["sol"] = _dot3(t["tinv"], t["rhs"])
        for t in cs:
            t["u"], t["w"] = t["sol"][:, :DN_D], t["sol"][:, DN_D:]
            t["qd"], t["kd"] = t["q"] * t["ed"], t["k"] * t["ekd"]
            t["v_new"] = heads_of(lambda h, rs: t["u"][rs] - _dot1(t["w"][rs], t["s"][h]))
            t["dv0"] = _dot1(t["qk"], t["g"], 0, 0)
            t["ds0"] = [_dot1(t["qd"][rs], t["g"][rs], 0, 0) for rs in rows_of]
            t["dqd"] = heads_of(lambda h, rs: _dot1(t["g"][rs], t["s"][h], 1, 1))
        for t in cs:
            t["dqk"] = _dot1(t["g"], t["v_new"], 1, 1)
        ds = [ds_scr[h] for h in range(DN_H)]
        for t in cs:
            t["ds"] = ds
            t["dv_new"] = t["dv0"] + heads_of(lambda h, rs: _dot1(t["kd"][rs], ds[h]))
            ds = [t["ds0"][h] + t["gl"][h] * ds[h] - _dot1(t["w"][rs], t["dv_new"][rs], 0, 0)
                  for h, rs in enumerate(rows_of)]
        for h in range(DN_H):
            ds_scr[h] = ds[h]
        for t in cs:
            t["dkd"] = heads_of(lambda h, rs: _dot1(t["v_new"][rs], t["ds"][h], 1, 1))
            dw = heads_of(lambda h, rs: -_dot1(t["dv_new"][rs], t["s"][h], 1, 1))
            t["dsol"] = jnp.concatenate([t["dv_new"], dw], axis=1)
        for t in cs:
            t["drhs"] = _dot3(t["tinv"], t["dsol"], 0, 0)
        for t in cs:
            t["da"] = jnp.where(strict, -_dot1(t["drhs"], t["sol"], 1, 1), 0.0)
        rowi = lax.broadcasted_iota(jnp.int32, (CH, 1), 0)
        lane = lax.broadcasted_iota(jnp.int32, (CH, 128), 1)
        for t in cs:
            q, k, v, b_c, dm, ed, da, dqk = t["q"], t["k"], t["v"], t["b_c"], t["dm"], t["ed"], t["da"], t["dqk"]
            drhs_u, drhs_w = t["drhs"][:, :DN_D], t["drhs"][:, DN_D:]
            s2 = rsum(drhs_w * k)
            dbeta = rsum(drhs_u * v) + s2 * ed + rsum(da * t["kk"] * dm)
            dkk = da * b_c * dm
            dqkr = dqk * dm
            mmat = da * t["a"] + dqk * t["qk"]
            tmp = rsum(t["dkd"] * t["kd"])
            dd = (s2 * b_c * ed + rsum(mmat) - _dot3(mmat, jnp.ones((HB, 128), F32), 0, 0)[:, :1]
                  + rsum(t["dqd"] * t["qd"]) - tmp)
            last = []
            for h, rs in enumerate(rows_of):
                dgl = jnp.sum(rsum(t["s"][h] * t["ds"][h]), axis=0, keepdims=True)
                dd_last = jnp.sum(tmp[rs], axis=0, keepdims=True) + dgl * t["gl"][h]
                last.append(jnp.where(rowi == CH - 1, dd_last, 0.0))
            dd = dd + jnp.concatenate(last, axis=0)
            rc = t["rc"]
            dq_ref[rc, :] = _unstack(_dot1(dqkr, k) + t["dqd"] * ed)
            dk_ref[rc, :] = _unstack(drhs_w * (b_c * ed) + _dot1(dkk, k) + _dot1(dkk, k, 0, 0) + _dot1(dqkr, q, 0, 0)
                                     + t["dkd"] * t["ekd"])
            dv_ref[rc, :] = _unstack(drhs_u * b_c)
            dg = _dot01(upper.astype(F32), jnp.broadcast_to(dd, (HB, 128)))[:, :1]
            out = jnp.zeros((CH, 128), F32)
            for h, rs in enumerate(rows_of):
                out = out + jnp.where(lane == h, dbeta[rs], 0.0) + jnp.where(lane == 4 + h, dg[rs], 0.0)
            dbg_ref[rc, :] = out

    assert nch % PAIR == 0
    npair = nch // PAIR
    rev = lambda n: npair - 1 - n
    blk = PAIR * CH
    return pl.pallas_call(
        body, name="dn_bwd", interpret=False,
        out_shape=[jax.ShapeDtypeStruct((rows, DN_DIM), F32)] * 3 + [jax.ShapeDtypeStruct((rows, 128), F32)],
        grid=(npair,),
        in_specs=[pl.BlockSpec((blk, DN_DIM), lambda n: (rev(n), 0)),
                  pl.BlockSpec((blk, DN_DIM), lambda n: (rev(n), 1)),
                  pl.BlockSpec((blk, DN_DIM), lambda n: (rev(n), 2)),
                  pl.BlockSpec((blk, 128), lambda n: (rev(n), 0)),
                  pl.BlockSpec((PAIR, 8, CH), lambda n: (rev(n), 0, 0)),
                  pl.BlockSpec((PAIR, DN_H, DN_D, DN_D), lambda n: (rev(n), 0, 0, 0)),
                  pl.BlockSpec((PAIR, HB, HB), lambda n: (rev(n), 0, 0)),
                  pl.BlockSpec((blk, DN_DIM), lambda n: (rev(n), 0))],
        out_specs=[pl.BlockSpec((blk, DN_DIM), lambda n: (rev(n), 0))] * 3 + [pl.BlockSpec((blk, 128), lambda n: (rev(n), 0))],
        scratch_shapes=[pltpu.VMEM((DN_H, DN_D, DN_D), F32)],
        compiler_params=_params(("arbitrary",)),
    )(qkv_n, qkv_n, qkv_n, bgcol, bgrow, s_all, ti_all, do)


def _swa_valid(n):
    c3 = lax.broadcasted_iota(jnp.int32, (NKEY, 4 * BLK), 0)
    r = lax.broadcasted_iota(jnp.int32, (NKEY, 4 * BLK), 1) % BLK
    prev0 = N_META + BLK
    c = jnp.where(c3 < N_META, PAD + c3, jnp.where(c3 < prev0, c3 - N_META, c3 - prev0))
    lo = jnp.where(c3 < N_META, 0, jnp.where(c3 < prev0, r + 1 + jnp.where(n >= 2, 0, BLK), 0))
    hi = jnp.where(c3 < N_META, r + jnp.where(n >= 1, BLK, 0),
                   jnp.where(c3 < prev0, BLK, r - jnp.where(n >= 1, 0, BLK)))
    return jnp.logical_and(c >= lo, c <= hi)


def _swa_probs(q, kcat, valid, sink):
    s = jnp.where(valid, _dot(kcat, q, 1, 1), -1e30)
    m = jnp.maximum(jnp.max(s, axis=0, keepdims=True), sink)
    e = jnp.where(valid, jnp.exp(s - m), 0.0)
    es = jnp.exp(sink - m)
    inv = 1.0 / (jnp.sum(e, axis=0, keepdims=True) + es)
    return e * inv, es * inv


def _swa_group(q_ref, sk_ref, h):
    q4 = jnp.concatenate([q_ref[4 * h + g] for g in range(4)], axis=0)
    sink4 = jnp.concatenate([jnp.full((1, BLK), sk_ref[4 * h + g], F32) for g in range(4)], axis=1)
    return q4, sink4


def _swa_specs():
    q = pl.BlockSpec((SWA_H, BLK, SWA_D), lambda n: (0, n, 0))
    km = pl.BlockSpec((SWA_KV, N_META, SWA_D), lambda n: (0, PAD // N_META, 0))
    kp = pl.BlockSpec((SWA_KV, BLK, SWA_D), lambda n: (0, jnp.maximum(n - 1, 0), 0))
    kc = pl.BlockSpec((SWA_KV, BLK, SWA_D), lambda n: (0, n, 0))
    return [q, km, kp, kc, km, kp, kc]


def swa_fwd(qh, kh, vh, sinks):
    rows = qh.shape[1]
    nb = rows // BLK

    def body(q_ref, km, kp, kc, vm, vp, vc, sk_ref, o_ref):
        n = pl.program_id(0)
        valid = _swa_valid(n)
        outs = []
        for h in range(SWA_KV):
            kcat = jnp.concatenate([km[h], kp[h], kc[h]], axis=0)
            vcat = jnp.concatenate([vm[h], vp[h], vc[h]], axis=0)
            q4, sink4 = _swa_group(q_ref, sk_ref, h)
            p, _ = _swa_probs(q4, kcat, valid, sink4)
            o4 = _dot(p.astype(BF16), vcat, 0, 0)
            outs += [o4[g * BLK:(g + 1) * BLK] for g in range(4)]
        o_ref[...] = jnp.concatenate(outs, axis=1).astype(BF16)

    return pl.pallas_call(
        body, name="swa_fwd", interpret=False,
        out_shape=jax.ShapeDtypeStruct((rows, SWA_H * SWA_D), BF16),
        grid=(nb,),
        in_specs=_swa_specs() + [pl.BlockSpec(memory_space=pltpu.SMEM)],
        out_specs=pl.BlockSpec((BLK, SWA_H * SWA_D), lambda n: (n, 0)),
        compiler_params=_params(("parallel",)),
    )(qh, kh, kh, kh, vh, vh, vh, sinks)


def swa_bwd(qh, kh, vh, sinks, do):
    rows = qh.shape[1]
    nb = rows // BLK

    def body(q_ref, km, kp, kc, vm, vp, vc, do_ref, sk_ref, dq_ref, dk_ref, dv_ref, dsk_ref):
        n = pl.program_id(0)

        @pl.when(n == 0)
        def _():
            dk_ref[...] = jnp.zeros(dk_ref.shape, F32)
            dv_ref[...] = jnp.zeros(dv_ref.shape, F32)

        valid = _swa_valid(n)
        g_all = do_ref[...]
        rowi = lax.broadcasted_iota(jnp.int32, (SWA_H, 128), 0)
        dsk = jnp.zeros((SWA_H, 128), F32)
        pm = pl.multiple_of(jnp.maximum(n - 1, 0) * BLK, BLK)
        pc = pl.multiple_of(n * BLK, BLK)
        for h in range(SWA_KV):
            kcat = jnp.concatenate([km[h], kp[h], kc[h]], axis=0)
            vcat = jnp.concatenate([vm[h], vp[h], vc[h]], axis=0)
            q4, sink4 = _swa_group(q_ref, sk_ref, h)
            p, ps = _swa_probs(q4, kcat, valid, sink4)
            g4 = jnp.concatenate([g_all[:, (4 * h + g) * SWA_D:(4 * h + g + 1) * SWA_D] for g in range(4)], axis=0)
            dp = _dot(vcat, g4, 1, 1)
            delta = jnp.sum(p * dp, axis=0, keepdims=True)
            ds = (p * (dp - delta)).astype(BF16)
            dq4 = _dot(ds, kcat, 0, 0)
            dkc = _dot(ds, q4)
            dvc = _dot(p.astype(BF16), g4)
            t = ps * delta
            for g in range(4):
                dq_ref[4 * h + g] = dq4[g * BLK:(g + 1) * BLK]
                part = -jnp.sum(t[:, g * BLK:(g + 1) * BLK], axis=1, keepdims=True)
                dsk = dsk + jnp.where(rowi == 4 * h + g, part, 0.0)
            lanes = slice(h * SWA_D, (h + 1) * SWA_D)
            for ref, val in ((dk_ref, dkc), (dv_ref, dvc)):
                ref[PAD:BLK, lanes] += val[0:N_META]
                ref[pl.ds(pm, BLK), lanes] += val[N_META:N_META + BLK]
                ref[pl.ds(pc, BLK), lanes] += val[N_META + BLK:]
        dsk_ref[0] = dsk

    return pl.pallas_call(
        body, name="swa_bwd", interpret=False,
        out_shape=[jax.ShapeDtypeStruct((SWA_H, rows, SWA_D), F32),
                   jax.ShapeDtypeStruct((rows, SWA_KV * SWA_D), F32),
                   jax.ShapeDtypeStruct((rows, SWA_KV * SWA_D), F32),
                   jax.ShapeDtypeStruct((nb, SWA_H, 128), F32)],
        grid=(nb,),
        in_specs=_swa_specs() + [pl.BlockSpec((BLK, SWA_H * SWA_D), lambda n: (n, 0)),
                                 pl.BlockSpec(memory_space=pltpu.SMEM)],
        out_specs=[pl.BlockSpec((SWA_H, BLK, SWA_D), lambda n: (0, n, 0)),
                   pl.BlockSpec((rows, SWA_KV * SWA_D), lambda n: (0, 0)),
                   pl.BlockSpec((rows, SWA_KV * SWA_D), lambda n: (0, 0)),
                   pl.BlockSpec((1, SWA_H, 128), lambda n: (n, 0, 0))],
        compiler_params=_params(("arbitrary",)),
    )(qh, kh, kh, kh, vh, vh, vh, do, sinks)


QK_W = (SWA_H + SWA_KV) * SWA_D


def _head_mean(t):
    r = lax.broadcasted_iota(jnp.int32, (128, 128), 0) // SWA_D
    c = lax.broadcasted_iota(jnp.int32, (128, 128), 1) // SWA_D
    blk = jnp.where(r == c, 1.0 / SWA_D, 0.0).astype(BF16)
    out = []
    for i in range(t.shape[1] // 128):
        hi, lo = _split(t[:, 128 * i:128 * (i + 1)])
        out.append(_dot(hi, blk) + _dot(lo, blk))
    return jnp.concatenate(out, axis=1)


def _qk_scales(qw, kw):
    scale = SWA_D ** -0.5
    wt = jnp.concatenate([jnp.tile(qw.astype(F32) * scale, (1, SWA_H)), jnp.tile(kw.astype(F32), (1, SWA_KV))], axis=1)
    st = jnp.concatenate([jnp.full((1, SWA_H * SWA_D), scale, F32), jnp.ones((1, SWA_KV * SWA_D), F32)], axis=1)
    return wt, st


def qknorm_fwd(qkv, qw, kw):
    rows = qkv.shape[0]
    tr = _pick(rows, (384, 128))
    wt, _ = _qk_scales(qw, kw)

    def fn(i, x, w):
        xq = x[:, :QK_W]
        y = xq * lax.rsqrt(_head_mean(xq * xq) + EPS) * w
        head = lambda t, j: t[:, j * SWA_D:(j + 1) * SWA_D][None]
        qo = jnp.concatenate([head(y, j) for j in range(SWA_H)], axis=0)
        ko = jnp.concatenate([head(y, SWA_H + j) for j in range(SWA_KV)], axis=0)
        vo = jnp.concatenate([head(x, SWA_H + SWA_KV + j) for j in range(SWA_KV)], axis=0)
        return qo, ko, vo

    hm = lambda nh: ((nh, rows, SWA_D), BF16, (nh, tr, SWA_D), lambda i: (0, i, 0), "r3")
    return rowwise(fn, [cols(qkv, tr), whole(wt)], [hm(SWA_H), hm(SWA_KV), hm(SWA_KV)],
                   steps=rows // tr, name="qknorm_fwd")


def qknorm_bwd(qkv, qw, kw, dqh, dk, dv):
    rows = qkv.shape[0]
    tr = _pick(rows, (384, 128))
    wt, st = _qk_scales(qw, kw)

    def fn(i, x, w, sc, dq, dkv, dvv):
        xq = x[:, :QK_W]
        dy = jnp.concatenate([dq[j] for j in range(SWA_H)] + [dkv], axis=1)
        r = lax.rsqrt(_head_mean(xq * xq) + EPS)
        xh = xq * r
        gw = dy * w
        dx = r * (gw - xh * _head_mean(gw * xh))
        return jnp.concatenate([dx, dvv], axis=1), jnp.sum(dy * sc * xh, axis=0, keepdims=True)

    dqkv, dw = rowwise(fn, [cols(qkv, tr), whole(wt), whole(st), heads(dqh, tr), cols(dk, tr), cols(dv, tr)],
                       [out2d(rows, 1536, BF16, tr)], steps=rows // tr, name="qknorm_bwd", accs=[((1, QK_W), F32)])
    dw = dw.reshape(SWA_H + SWA_KV, SWA_D)
    return dqkv, jnp.sum(dw[:SWA_H], axis=0, keepdims=True), jnp.sum(dw[SWA_H:], axis=0, keepdims=True)


def _place():
    return lax.axis_index("x"), lax.axis_index("y"), lax.axis_index("c")


ANY = pl.BlockSpec(memory_space=pl.ANY)


def _rcopy(ssem, rsem, k, src, dst, to):
    return pltpu.make_async_remote_copy(src_ref=src, dst_ref=dst, send_sem=ssem.at[k], recv_sem=rsem.at[k],
                                        device_id=to, device_id_type=MESH)


def gather_weights(shards, small):
    n = len(shards)
    halves = [t.shape[0] // 2 for t in shards]

    def body(*refs):
        s_refs, small_ref = refs[:n], refs[n]
        o_refs, osmall = refs[n + 1:2 * n + 1], refs[2 * n + 1]
        ssem, rsem, lsem = refs[2 * n + 2:]
        x, y, c = _place()
        me = 2 * x + y
        chips = [(1 - x, y), (x, 1 - y), (1 - x, 1 - y)]

        def half(k, s, hh):
            return o_refs[k].at[s, pl.ds(hh * halves[k], halves[k]), :]

        loc = pltpu.make_async_copy(small_ref, osmall.at[me], lsem)
        loc.start()
        sends = []
        for k in range(n):
            for j, (px, py) in enumerate(chips):
                sends.append(_rcopy(ssem, rsem, 6 * k + j, s_refs[k].at[pl.ds(c * halves[k], halves[k]), :],
                                    half(k, me, c), (px, py, c)))
        for j, (px, py) in enumerate(chips):
            sends.append(_rcopy(ssem, rsem, 6 * n + j, small_ref, osmall.at[me], (px, py, c)))
        for cp in sends:
            cp.start()
        for k in range(n):
            for j, (px, py) in enumerate(chips):
                s = 2 * px + py
                _rcopy(ssem, rsem, 6 * k + j, half(k, s, c), half(k, s, c), (x, y, c)).wait_recv()
                fwd = _rcopy(ssem, rsem, 6 * k + 3 + j, half(k, s, c), half(k, s, c), (x, y, 1 - c))
                fwd.start()
                sends.append(fwd)
        for k in range(n):
            for j, (px, py) in enumerate(chips):
                s = 2 * px + py
                _rcopy(ssem, rsem, 6 * k + 3 + j, half(k, s, 1 - c), half(k, s, 1 - c), (x, y, c)).wait_recv()
        for j, (px, py) in enumerate(chips):
            s = 2 * px + py
            _rcopy(ssem, rsem, 6 * n + j, osmall.at[s], osmall.at[s], (x, y, c)).wait_recv()
        for cp in sends:
            cp.wait_send()
        loc.wait()

    res = pl.pallas_call(
        body, name="gather_weights", interpret=False,
        out_shape=[jax.ShapeDtypeStruct((4,) + t.shape, t.dtype) for t in shards]
        + [jax.ShapeDtypeStruct((4, SW_ROWS, 1024), F32)],
        in_specs=[ANY] * (n + 1), out_specs=[ANY] * (n + 1),
        scratch_shapes=[pltpu.SemaphoreType.DMA((6 * n + 3,)), pltpu.SemaphoreType.DMA((6 * n + 3,)),
                        pltpu.SemaphoreType.DMA],
    )(*shards, small)
    return res[:n], res[n]


def _handshake(peers):
    barrier = pltpu.get_barrier_semaphore()
    for peer in peers:
        pl.semaphore_signal(barrier, inc=1, device_id=peer, device_id_type=MESH)
    pl.semaphore_wait(barrier, len(peers))


def gather_weights_beside(shards):
    n = len(shards)
    halves = [t.shape[0] // 2 for t in shards]

    def body(*refs):
        s_refs, o_refs, ssem, rsem = refs[:n], refs[n:2 * n], refs[2 * n], refs[2 * n + 1]
        x, y, c = _place()
        me = 2 * x + y
        chips = [(1 - x, y), (x, 1 - y), (1 - x, 1 - y)]
        _handshake([(px, py, c) for px, py in chips] + [(x, y, 1 - c)])

        def half(k, s, hh):
            return o_refs[k].at[s, pl.ds(hh * halves[k], halves[k]), :]

        sends = []
        for k in range(n):
            for j, (px, py) in enumerate(chips):
                sends.append(_rcopy(ssem, rsem, 6 * k + j, s_refs[k].at[pl.ds(c * halves[k], halves[k]), :],
                                    half(k, me, c), (px, py, c)))
        for cp in sends:
            cp.start()
        for k in range(n):
            for j, (px, py) in enumerate(chips):
                s = 2 * px + py
                _rcopy(ssem, rsem, 6 * k + j, half(k, s, c), half(k, s, c), (x, y, c)).wait_recv()
                fwd = _rcopy(ssem, rsem, 6 * k + 3 + j, half(k, s, c), half(k, s, c), (x, y, 1 - c))
                fwd.start()
                sends.append(fwd)
        for k in range(n):
            for j, (px, py) in enumerate(chips):
                s = 2 * px + py
                _rcopy(ssem, rsem, 6 * k + 3 + j, half(k, s, 1 - c), half(k, s, 1 - c), (x, y, c)).wait_recv()
        for cp in sends:
            cp.wait_send()

    return pl.kernel(
        body, name="gather_weights_beside",
        out_type=[jax.ShapeDtypeStruct((4,) + t.shape, t.dtype) for t in shards],
        mesh=plsc.ScalarSubcoreMesh(axis_name="sequencer", num_cores=1),
        scratch_types=[pltpu.SemaphoreType.DMA((6 * n,)), pltpu.SemaphoreType.DMA((6 * n,))],
        compiler_params=pltpu.CompilerParams(collective_id=1),
    )(*shards)


def swap_halves(gs, *, name):
    n = len(gs)

    def body(*refs):
        g_refs, o_refs, ssem, rsem = refs[:n], refs[n:2 * n], refs[2 * n], refs[2 * n + 1]
        x, y, c = _place()
        cps = []
        for k in range(n):
            hk = g_refs[k].shape[1] // 2
            cps.append(_rcopy(ssem, rsem, k, g_refs[k].at[:, pl.ds((1 - c) * hk, hk), :], o_refs[k], (x, y, 1 - c)))
        for cp in cps:
            cp.start()
        for cp in cps:
            cp.wait()

    return pl.pallas_call(
        body, name=name, interpret=False,
        out_shape=[jax.ShapeDtypeStruct((4, t.shape[1] // 2, t.shape[2]), t.dtype) for t in gs],
        in_specs=[ANY] * n, out_specs=[ANY] * n,
        scratch_shapes=[pltpu.SemaphoreType.DMA((n,)), pltpu.SemaphoreType.DMA((n,))],
    )(*gs)


def _sum_rows(hk):
    return _pick(hk, (512, 352, 256, 128))


def pair_sum(g, other, c_idx, *, name):
    _, hk, width = other.shape
    tr = _sum_rows(hk)
    nbk = hk // tr

    def body(c_ref, g_ref, o_ref, out_ref):
        out_ref[...] = (g_ref[...].astype(F32) + o_ref[...].astype(F32)).astype(BF16)

    return pl.pallas_call(
        body, name=name, interpret=False,
        out_shape=jax.ShapeDtypeStruct((4, hk, width), BF16),
        grid_spec=pltpu.PrefetchScalarGridSpec(
            num_scalar_prefetch=1, grid=(4, nbk),
            in_specs=[pl.BlockSpec((1, tr, width), lambda s, i, c_ref: (s, c_ref[0] * nbk + i, 0)),
                      pl.BlockSpec((1, tr, width), lambda s, i, c_ref: (s, i, 0))],
            out_specs=pl.BlockSpec((1, tr, width), lambda s, i, c_ref: (s, i, 0))),
        compiler_params=_params(("parallel", "parallel")),
    )(c_idx, g, other)


def chip_sum(p, got, idx, *, name):
    _, hk, width = got.shape
    tr = _sum_rows(hk)
    nbk = hk // tr

    def body(idx_ref, p_ref, g_ref, out_ref):
        acc = p_ref[0].astype(F32)
        for j in range(3):
            acc = acc + g_ref[j].astype(F32)
        out_ref[0] = acc

    return pl.pallas_call(
        body, name=name, interpret=False,
        out_shape=jax.ShapeDtypeStruct((2, hk, width), F32),
        grid_spec=pltpu.PrefetchScalarGridSpec(
            num_scalar_prefetch=1, grid=(nbk,),
            in_specs=[pl.BlockSpec((1, tr, width), lambda i, idx_ref: (idx_ref[0], i, 0)),
                      pl.BlockSpec((3, tr, width), lambda i, idx_ref: (0, i, 0))],
            out_specs=pl.BlockSpec((1, tr, width), lambda i, idx_ref: (idx_ref[1], i, 0))),
        compiler_params=_params(("parallel",)),
    )(idx, p, got)


def join_halves(qs):
    n = len(qs)

    def body(*refs):
        q_refs, o_refs, ssem, rsem = refs[:n], refs[n:2 * n], refs[2 * n], refs[2 * n + 1]
        x, y, c = _place()
        cps = [_rcopy(ssem, rsem, k, q_refs[k].at[c], o_refs[k].at[c], (x, y, 1 - c)) for k in range(n)]
        for cp in cps:
            cp.start()
        for k in range(n):
            _rcopy(ssem, rsem, k, q_refs[k].at[c], o_refs[k].at[1 - c], (x, y, 1 - c)).wait_recv()
        for cp in cps:
            cp.wait_send()

    return pl.pallas_call(
        body, name="join_halves", interpret=False,
        out_shape=[jax.ShapeDtypeStruct(t.shape, t.dtype) for t in qs],
        in_specs=[ANY] * n, out_specs=[ANY] * n, input_output_aliases={k: k for k in range(n)},
        scratch_shapes=[pltpu.SemaphoreType.DMA((n,)), pltpu.SemaphoreType.DMA((n,))],
    )(*qs)


def scatter_chips_beside(ps, cid, name):
    n = len(ps)

    def body(*refs):
        p_refs, o_refs, ssem, rsem = refs[:n], refs[n:2 * n], refs[2 * n], refs[2 * n + 1]
        x, y, c = _place()
        chips = [(1 - x, y), (x, 1 - y), (1 - x, 1 - y)]
        _handshake([(px, py, c) for px, py in chips])
        cps = [_rcopy(ssem, rsem, 3 * k + j, p_refs[k].at[2 * px + py], o_refs[k].at[j], (px, py, c))
               for k in range(n) for j, (px, py) in enumerate(chips)]
        for cp in cps:
            cp.start()
        for cp in cps:
            cp.wait()

    return pl.kernel(
        body, name=name, out_type=[jax.ShapeDtypeStruct((3,) + t.shape[1:], t.dtype) for t in ps],
        mesh=plsc.ScalarSubcoreMesh(axis_name="sequencer", num_cores=1),
        scratch_types=[pltpu.SemaphoreType.DMA((3 * n,)), pltpu.SemaphoreType.DMA((3 * n,))],
        compiler_params=pltpu.CompilerParams(collective_id=cid),
    )(*ps)


def reduce_begin(gs, names, c_idx, cid, tag):
    others = swap_halves(gs, name=f"swap_halves_{tag}")
    pairs = [pair_sum(g, o, c_idx, name=f"pair_sum_{nm}") for g, o, nm in zip(gs, others, names)]
    return pairs, scatter_chips_beside(pairs, cid, f"scatter_chips_{tag}")


def reduce_end(pairs, gots, names, idx):
    mine = [chip_sum(p, g, idx, name=f"chip_sum_{nm}") for p, g, nm in zip(pairs, gots, names)]
    return [q.reshape(2 * q.shape[1], q.shape[2]) for q in join_halves(mine)]


def gather_small(v):
    def body(v_ref, o_ref, ssem, rsem, lsem):
        x, y, c = _place()
        peers = []
        for k in range(1, 8):
            fx, fy, fc = (k >> 2) & 1, (k >> 1) & 1, k & 1
            peers.append((1 - x if fx else x, 1 - y if fy else y, 1 - c if fc else c))
        _handshake(peers)
        loc = pltpu.make_async_copy(v_ref, o_ref.at[4 * x + 2 * y + c], lsem)
        loc.start()
        cps = []
        for k, (px, py, pc) in enumerate(peers):
            cps.append((pltpu.make_async_remote_copy(
                src_ref=v_ref, dst_ref=o_ref.at[4 * x + 2 * y + c], send_sem=ssem.at[k], recv_sem=rsem.at[k],
                device_id=(px, py, pc), device_id_type=MESH), 4 * px + 2 * py + pc))
        for cp, _ in cps:
            cp.start()
        for k, (cp, peer) in enumerate(cps):
            pltpu.make_async_remote_copy(
                src_ref=v_ref, dst_ref=o_ref.at[peer], send_sem=ssem.at[k], recv_sem=rsem.at[k],
                device_id=(x, y, c), device_id_type=MESH).wait_recv()
        for cp, _ in cps:
            cp.wait_send()
        loc.wait()

    return pl.kernel(
        body, name="gather_small", out_type=jax.ShapeDtypeStruct((8, SV_ROWS, 1024), F32),
        mesh=plsc.ScalarSubcoreMesh(axis_name="sequencer", num_cores=1),
        scratch_types=[pltpu.SemaphoreType.DMA((7,)), pltpu.SemaphoreType.DMA((7,)), pltpu.SemaphoreType.DMA],
        compiler_params=pltpu.CompilerParams(collective_id=6),
    )(v)


def sum_slots(a):
    def fn(i, t):
        acc = t[0]
        for k in range(1, 8):
            acc = acc + t[k]
        return acc

    return rowwise(fn, [whole(a)], [((SV_ROWS, 1024), F32, (SV_ROWS, 1024), lambda i: (0, 0), "w")], steps=1,
                   name="sum_slots")[0]


def _head_rms(x, nw):
    xs, rs = [], []
    for h in range(DN_H):
        xh = x[:, h * DN_D:(h + 1) * DN_D]
        r = lax.rsqrt(jnp.mean(xh * xh, axis=1, keepdims=True) + EPS)
        xs.append(xh * r)
        rs.append(r)
    return xs, rs


def bg_fwd(p, alog, dtb):
    rows = p.shape[0]
    tr = _pick(rows, (384, 128))

    def fn(i, x, al, dt):
        lane = lax.broadcasted_iota(jnp.int32, x.shape, 1)
        row = i + lax.broadcasted_iota(jnp.int32, x.shape, 0)
        g = -jnp.exp(al) * _softplus(x + dt)
        out = jnp.where(lane < 4, _sigmoid(x), jnp.where(lane < 8, g, 0.0))
        return jnp.where(row >= PAD, out, 0.0)

    return rowwise(fn, [cols(p, tr, 128, BG0 // 128), whole(alog), whole(dtb)], [out2d(rows, 128, F32, tr)],
                   steps=rows // tr, name="bg_fwd")[0]


def bg_bwd(p, alog, dtb, dbg):
    rows = p.shape[0]
    tr = _pick(rows, (384, 128))

    def fn(i, x, al, dt, g_in):
        lane = lax.broadcasted_iota(jnp.int32, x.shape, 1)
        row = i + lax.broadcasted_iota(jnp.int32, x.shape, 0)
        live = row >= PAD
        is_b = jnp.logical_and(live, lane < 4)
        is_g = jnp.logical_and(live, jnp.logical_and(lane >= 4, lane < 8))
        beta = _sigmoid(x)
        ea = jnp.exp(al)
        g = -ea * _softplus(x + dt)
        dalpha = jnp.where(is_g, g_in * (-ea) * _sigmoid(x + dt), 0.0)
        dx = jnp.where(is_b, g_in * beta * (1.0 - beta), dalpha)
        dal = jnp.sum(jnp.where(is_g, g_in * g, 0.0), axis=0, keepdims=True)
        return jnp.concatenate([dx, jnp.zeros(x.shape, F32)], axis=1), dal, jnp.sum(dalpha, axis=0, keepdims=True)

    return rowwise(fn, [cols(p, tr, 128, BG0 // 128), whole(alog), whole(dtb), cols(dbg, tr)],
                   [out2d(rows, 256, BF16, tr)], steps=rows // tr, name="bg_bwd",
                   accs=[((1, 128), F32), ((1, 128), F32)])


def dn_qkv_post(j, y):
    xs = _silu(y)
    sc = jnp.where(j == 0, DN_D ** -0.5, 1.0)
    outs = []
    for h in range(DN_H):
        xh = xs[:, h * DN_D:(h + 1) * DN_D]
        r = lax.rsqrt(jnp.sum(xh * xh, axis=1, keepdims=True) + EPS)
        outs.append(jnp.where(j < 2, xh * r * sc, xh))
    return jnp.concatenate(outs, axis=1), y


def dn_qkv_bwd(cq, dq, dk, dv):
    rows = cq.shape[0]
    tr = _pick(rows, (384, 128))

    def fn(i, c0, c1, c2, g0, g1, g2):
        pieces = []
        for kind, (cv, g) in enumerate(((c0, g0), (c1, g1), (c2, g2))):
            xs = _silu(cv)
            if kind < 2:
                sc = DN_D ** -0.5 if kind == 0 else 1.0
                ds = []
                for h in range(DN_H):
                    sl = slice(h * DN_D, (h + 1) * DN_D)
                    xh, gh = xs[:, sl], g[:, sl]
                    r = lax.rsqrt(jnp.sum(xh * xh, axis=1, keepdims=True) + EPS)
                    xn = xh * r
                    ds.append(sc * r * (gh - xn * jnp.sum(gh * xn, axis=1, keepdims=True)))
                dxs = jnp.concatenate(ds, axis=1)
            else:
                dxs = g
            pieces.append(dxs * _dsilu(cv))
        return jnp.concatenate(pieces, axis=1)

    ins = [cols(cq, tr, DN_DIM, k) for k in range(3)] + [cols(t, tr) for t in (dq, dk, dv)]
    return rowwise(fn, ins, [out2d(rows, 3 * DN_DIM, F32, tr)], steps=rows // tr, name="dn_qkv_bwd")[0]


def dn_out_fwd(o, p, nw):
    rows = o.shape[0]
    tr = _pick(rows, (384, 128))

    def fn(i, ov, z, w):
        xs, _ = _head_rms(ov, w)
        return jnp.concatenate(xs, axis=1) * jnp.concatenate([w] * DN_H, axis=1) * _silu(z)

    return rowwise(fn, [cols(o, tr), cols(p, tr, DN_DIM, 6), whole(nw)], [out2d(rows, DN_DIM, BF16, tr)],
                   steps=rows // tr, name="dn_out_fwd")[0]


def dn_out_bwd(o, p, nw, dymix):
    rows = o.shape[0]
    tr = _pick(rows, (384, 128))

    def fn(i, ov, z, w, dy):
        xs, rs = _head_rms(ov, w)
        sz = _silu(z)
        dn = dy * sz
        dos, dw = [], jnp.zeros((1, DN_D), F32)
        for h in range(DN_H):
            sl = slice(h * DN_D, (h + 1) * DN_D)
            gw = dn[:, sl] * w
            dos.append(rs[h] * (gw - xs[h] * jnp.mean(gw * xs[h], axis=1, keepdims=True)))
            dw = dw + jnp.sum(dn[:, sl] * xs[h], axis=0, keepdims=True)
        n = jnp.concatenate(xs, axis=1) * jnp.concatenate([w] * DN_H, axis=1)
        return jnp.concatenate(dos, axis=1), dy * n * _dsilu(z), dw

    return rowwise(fn, [cols(o, tr), cols(p, tr, DN_DIM, 6), whole(nw), cols(dymix, tr, DN_DIM, 1)],
                   [out2d(rows, DN_DIM, F32, tr), out2d(rows, DN_DIM, BF16, tr)], steps=rows // tr,
                   name="dn_out_bwd", accs=[((1, DN_D), F32)])


def conv_a_pre_bwd(dymix, cv, p):
    rows = cv.shape[0]
    tr = _pick(rows, (384, 128))

    def fn(i, dy, c, go):
        return dy * c, dy * go

    return rowwise(fn, [cols(dymix, tr, D_CONV, 0), cols(cv, tr), cols(p, tr, D_CONV, 1)],
                   [out2d(rows, D_CONV, BF16, tr), out2d(rows, D_CONV, F32, tr)], steps=rows // tr,
                   name="conv_a_pre_bwd")


def _rows8(w):
    return jnp.pad(w.astype(F32), ((0, 8 - w.shape[0]), (0, 0)))


def _lanes(v, at):
    return jnp.pad(v.astype(F32), (at, 128 - at - v.shape[0]))[None]


def add_norm(a, w, h, next_nw, *, name):
    if next_nw is None:
        return mm(a, w, add=h, name=name), None
    return mm(a, w, name=name, epi=_add_norm_epi, epi_ins=[(h, lambda j: 0)], epi_consts=[next_nw],
              epi_outs=[F32, BF16])


def _add_norm_epi(row0, t, h, nw):
    x = t + h
    return x, x * lax.rsqrt(jnp.mean(x * x, axis=1, keepdims=True) + EPS) * nw


def ffn_up_conv(hn, w_up, cw8, *, name):
    rows = hn.shape[0]
    tn = w_up.shape[2]
    tm = _pick(rows, (384, 128))
    nr = rows // tm

    def body(x_ref, wg_ref, wv_ref, w_ref, ug_ref, uv_ref, gc_ref, a_ref, carry, scr):
        i = pl.program_id(1)
        x = x_ref[...]
        gate = _dot(x, wg_ref[...])
        val = _dot(x, wv_ref[...])
        ug_ref[...] = gate.astype(BF16)
        uv_ref[...] = val.astype(BF16)
        scr[0:8, :] = jnp.where(i > 0, carry[...], 0.0)
        scr[8:8 + tm, :] = gate
        carry[...] = gate[tm - 8:tm]
        y = jnp.zeros((tm, tn), F32)
        for q in range(3):
            sh = 2 - q
            y = y + w_ref[q:q + 1, :] * scr[8 - sh:8 - sh + tm, :]
        gc_ref[...] = y.astype(BF16)
        a_ref[...] = (_silu(y) * val).astype(BF16)

    half = pl.BlockSpec((tm, tn), lambda j, i: (i, j))
    return pl.pallas_call(
        body, name=name, interpret=False,
        out_shape=[jax.ShapeDtypeStruct((rows, D_FF), BF16)] * 4,
        grid=(D_FF // tn, nr),
        in_specs=[pl.BlockSpec((tm, D), lambda j, i: (i, 0)),
                  pl.BlockSpec((None, D, tn), lambda j, i: (j, 0, 0)),
                  pl.BlockSpec((None, D, tn), lambda j, i: (j + D_FF // tn, 0, 0)),
                  pl.BlockSpec((8, tn), lambda j, i: (0, j))],
        out_specs=[half] * 4,
        scratch_shapes=[pltpu.VMEM((8, tn), F32), pltpu.VMEM((tm + 8, tn), F32)],
        compiler_params=_params(("arbitrary", "arbitrary")),
    )(hn, w_up, w_up, cw8)


def ffn_down_bwd(dh, w_down, gc, uv, ug, cw8, *, name):
    rows = dh.shape[0]
    tn = D_FF // 2
    tm = _pick(rows, (384, 128))
    nr = rows // tm
    r8 = tm // 8

    def body(dh_ref, w_ref, gc_ref, uv_ref, ug_ref, halo_ref, cw_ref, du_ref, dw_ref, carry, gscr, xscr):
        ip = pl.program_id(1)
        i = nr - 1 - ip
        da = _dot(dh_ref[...].astype(BF16), w_ref[...], 1, 1)
        c, val = gc_ref[...].astype(F32), uv_ref[...].astype(F32)
        dgc = da * val * _dsilu(c)
        du_ref[:, tn:] = (da * _silu(c)).astype(BF16)
        gscr[0:tm, :] = dgc
        gscr[tm:tm + 8, :] = jnp.where(ip > 0, carry[...], 0.0)
        carry[...] = dgc[0:8]
        xscr[0:8, :] = jnp.where(i > 0, halo_ref[...].astype(F32), 0.0)
        xscr[8:8 + tm, :] = ug_ref[...].astype(F32)
        dx = jnp.zeros((tm, tn), F32)
        dws = []
        for q in range(3):
            sh = 2 - q
            dx = dx + cw_ref[q:q + 1, :] * gscr[sh:sh + tm, :]
            dws.append(jnp.sum(dgc * xscr[8 - sh:8 - sh + tm, :], axis=0, keepdims=True))
        du_ref[:, :tn] = dx.astype(BF16)

        @pl.when(ip == 0)
        def _():
            dw_ref[...] = jnp.zeros((8, tn), F32)

        dw_ref[...] += jnp.concatenate(dws + [jnp.zeros((5, tn), F32)], axis=0)

    rev = lambda ip: nr - 1 - ip
    tile = lambda arr: pl.BlockSpec((tm, tn), lambda j, ip: (rev(ip), j))
    return pl.pallas_call(
        body, name=name, interpret=False,
        out_shape=[jax.ShapeDtypeStruct((rows, 2 * D_FF), BF16), jax.ShapeDtypeStruct((8, D_FF), F32)],
        grid=(2, nr),
        in_specs=[pl.BlockSpec((tm, D), lambda j, ip: (rev(ip), 0)),
                  pl.BlockSpec((tn, D), lambda j, ip: (j, 0)),
                  tile(gc), tile(uv), tile(ug),
                  pl.BlockSpec((8, tn), lambda j, ip: (jnp.maximum(rev(ip) * r8 - 1, 0), j)),
                  pl.BlockSpec((8, tn), lambda j, ip: (0, j))],
        out_specs=[pl.BlockSpec((tm, 2 * tn), lambda j, ip: (rev(ip), j)),
                   pl.BlockSpec((8, tn), lambda j, ip: (0, j))],
        scratch_shapes=[pltpu.VMEM((8, tn), F32), pltpu.VMEM((tm + 8, tn), F32), pltpu.VMEM((tm + 8, tn), F32)],
        compiler_params=_params(("arbitrary", "arbitrary")),
    )(dh, w_down, gc, uv, ug, ug, cw8)


def ffn_fwd(h, hn, w_up, cw8, w_down, tag, next_nw):
    ug, uv, gc, a = ffn_up_conv(hn, w_up, cw8, name=f"ffn{tag}_up")
    out, hn_next = add_norm(a, w_down, h, next_nw, name=f"ffn{tag}_down")
    return out, hn_next, (hn, ug, uv, a, gc)


def ffn_bwd(h, nw, w_up, cw8, w_down, saved, dh, tag):
    hn, ug, uv, a, gc = saved
    du, d_cw = ffn_down_bwd(dh, w_down, gc, uv, ug, cw8, name=f"ffn{tag}_down_dx")
    d_w_down = mm(a, dh, ta=True, out_dtype=BF16, name=f"ffn{tag}_down_dw")
    dh_new, d_nw = dx_rms_bwd(du, w_up, h, nw, dh, name=f"ffn{tag}_up_dx", b_chip=True, swap_mid=True)
    d_w_up = mm(hn, du, ta=True, out_dtype=BF16, out_chip=True, swap_mid=True, name=f"ffn{tag}_up_dw")
    return dh_new, d_nw, d_w_up, d_cw, d_w_down


def mixer_fwd(h, nw, w_in, ca8, dc8, alog, dtb, dnw, w_out, tie=None, next_nw=None):
    rows = h.shape[0]
    tr = _pick(rows, (384, 128))
    hn = rms_fwd(h, nw, name="mix_norm")
    p = mm(hn, w_in, name="mix_in")
    y_a, cv = conv_fwd([(p, 0), (p, 2)], ca8, 3, rows=rows, c=D_CONV, tc=D_CONV, tr=tr, name="conv_a",
                       pre=lambda gi, ah: gi * ah, post=lambda j, y, go: (go * y, y), extras=[(p, 1)],
                       outs=[BF16, F32])
    qkv_n, cq = conv_fwd([(p, 3)], dc8, 4, rows=rows, c=3 * DN_DIM, tc=DN_DIM, tr=tr, name="dn_conv",
                         post=dn_qkv_post, outs=[F32, F32], strip=tr)
    bgcol = bg_fwd(p, alog, dtb)
    if tie is not None:
        bgcol = tie(bgcol)
    bgrow = bgcol[:, :8].reshape(rows // CH, CH, 8).transpose(0, 2, 1)
    o, s_all, ti_all = dn_fwd(qkv_n, bgcol, bgrow)
    y_b = dn_out_fwd(o, p, dnw)
    ymix = jnp.concatenate([y_a, y_b], axis=1)
    w_out = w_out() if callable(w_out) else w_out
    out, hn_next = add_norm(ymix, w_out, h, next_nw, name="mix_out")
    return out, hn_next, (hn, p, cv, qkv_n, cq, bgcol, bgrow, o, s_all, ti_all, ymix)


def mixer_bwd(h, nw, w_in, ca8, dc8, alog, dtb, dnw, w_out, saved, dh):
    hn, p, cv, qkv_n, cq, bgcol, bgrow, o, s_all, ti_all, ymix = saved
    rows = h.shape[0]
    tr = _pick(rows, (384, 128))
    dymix = mm(dh, w_out, tb=True, name="mix_out_dx")
    d_w_out = mm(ymix, dh, ta=True, out_dtype=BF16, name="mix_out_dw")
    do, dz, d_dnw = dn_out_bwd(o, p, dnw, dymix)
    dq, dk, dv, dbg = dn_bwd(qkv_n, bgcol, bgrow, s_all, ti_all, do)
    dbg_p, d_alog, d_dtb = bg_bwd(p, alog, dtb, dbg)
    dcq = dn_qkv_bwd(cq, dq, dk, dv)
    dqkv, d_dc = conv_bwd([(p, 3)], dc8, 4, dcq, rows=rows, c=3 * DN_DIM, tc=DN_DIM, tr=tr, name="dn_conv_bwd",
                          post=lambda dx: dx, outs=[BF16])
    dgo, dcv = conv_a_pre_bwd(dymix, cv, p)
    dgi, dah, d_ca = conv_bwd([(p, 0), (p, 2)], ca8, 3, dcv, rows=rows, c=D_CONV, tc=D_CONV, tr=tr,
                              name="conv_a_bwd", pre=lambda gi, ah: gi * ah,
                              post=lambda dm, gi, ah: (dm * ah, dm * gi), extras=[(p, 0), (p, 2)], outs=[BF16, BF16])
    dp = jnp.concatenate([dgi, dgo, dah, dqkv, dz, dbg_p], axis=1)
    dh_new, d_nw = dx_rms_bwd(dp, w_in, h, nw, dh, name="mix_in_dx")
    d_w_in = mm(hn, dp, ta=True, out_dtype=BF16, name="mix_in_dw")
    return dh_new, d_nw, d_w_in, d_ca, d_dc, d_alog, d_dtb, d_dnw, d_w_out


def swa_layer_fwd(h, hn, wqkv, qw, kw, sinks, wo, next_nw):
    qkv = mm(hn, wqkv, name="swa_qkv")
    qh, kh, vh = qknorm_fwd(qkv, qw, kw)
    att = swa_fwd(qh, kh, vh, sinks)
    out, hn_next = add_norm(att, wo, h, next_nw, name="swa_out")
    return out, hn_next, (hn, qkv, qh, kh, vh, att)


def swa_layer_bwd(h, nw, wqkv, qw, kw, sinks, wo, saved, dh):
    hn, qkv, qh, kh, vh, att = saved
    datt = mm(dh, wo, tb=True, out_dtype=BF16, name="swa_out_dx")
    d_wo = mm(att, dh, ta=True, out_dtype=BF16, name="swa_out_dw")
    dqh, dkh, dvh, dsk = swa_bwd(qh, kh, vh, sinks, datt)
    dqkv, d_qw, d_kw = qknorm_bwd(qkv, qw, kw, dqh, dkh, dvh)
    dh_new, d_nw = dx_rms_bwd(dqkv, wqkv, h, nw, dh, name="swa_qkv_dx")
    d_wqkv = mm(hn, dqkv, ta=True, out_dtype=BF16, name="swa_qkv_dw")
    d_sinks = jnp.sum(dsk[:, :, 0], axis=0)
    return dh_new, d_nw, d_wqkv, d_qw, d_kw, d_sinks, d_wo


BIG = ("mix_w_in", "mix_w_out", "swa_wq", "swa_wk", "swa_wv", "swa_wo", "ffn_w_up", "ffn_w_down")


def _flat_pad(parts, rows):
    v = jnp.concatenate([t.astype(F32).reshape(-1) for t in parts])
    return jnp.pad(v, (0, rows * 1024 - v.shape[0])).reshape(rows, 1024)


def _split_flat(flat, shapes):
    v = flat.reshape(-1)
    out, o = [], 0
    for s in shapes:
        n = 1
        for d_ in s:
            n *= d_
        out.append(v[o:o + n].reshape(s))
        o += n
    return out


def local_step(x0, target0, meta_full, anw, fnw, w_in, ca8, dc8, alog, dtb, dnw, qw, kw, sinks, fc8, late,
               begin=None, tie=None):
    begin = begin or (lambda tag, names, grads: None)
    h0 = jnp.concatenate([jnp.zeros((PAD, D), F32), meta_full, x0], axis=0)
    h1, hn1, s_mix = mixer_fwd(h0, anw[0], w_in, ca8, dc8, alog, dtb, dnw, lambda: late()[0], tie, fnw[0])
    w_out, wqkv, wo, w_up, w_down = late()
    h2, hn2, s_f0 = ffn_fwd(h1, hn1, w_up[0], fc8[0], w_down[0], 0, anw[1])
    h3, hn3, s_swa = swa_layer_fwd(h2, hn2, wqkv, qw, kw, sinks, wo, fnw[1])
    h4, _, s_f1 = ffn_fwd(h3, hn3, w_up[1], fc8[1], w_down[1], 1, None)
    dh, loss_l = loss_grad(h4, target0)
    dh, d_fnw1, d_up1, d_fc1, d_down1 = ffn_bwd(h3, fnw[1], w_up[1], fc8[1], w_down[1], s_f1, dh, 1)
    begin("ffn1", ("up1", "down1"), [d_up1, d_down1.reshape(4, 704, D)])
    dh, d_anw1, d_wqkv, d_qw, d_kw, d_sinks, d_wo = swa_layer_bwd(h2, anw[1], wqkv, qw, kw, sinks, wo, s_swa, dh)
    begin("swa", ("wq", "wk", "wv", "wo"),
          [d_wqkv[:, :D].reshape(4, 256, D), d_wqkv[:, D:D + 256].reshape(4, 256, 256),
           d_wqkv[:, D + 256:].reshape(4, 256, 256), d_wo.reshape(4, 256, D)])
    dh, d_fnw0, d_up0, d_fc0, d_down0 = ffn_bwd(h1, fnw[0], w_up[0], fc8[0], w_down[0], s_f0, dh, 0)
    begin("ffn0", ("up0", "down0"), [d_up0, d_down0.reshape(4, 704, D)])
    dh, d_anw0, d_w_in, d_ca, d_dc, d_alog, d_dtb, d_dnw, d_w_out = mixer_bwd(
        h0, anw[0], w_in, ca8, dc8, alog, dtb, dnw, w_out, s_mix, dh)
    begin("mix", ("w_in", "w_out"),
          [d_w_in[:, :IN_DIM].reshape(D, 4, 898).transpose(1, 0, 2), d_w_out.reshape(4, 256, D)])
    return (dh, loss_l, d_anw0, d_anw1, d_fnw0, d_fnw1, d_w_in, d_ca, d_dc, d_alog, d_dtb, d_dnw, d_w_out, d_wqkv,
            d_qw, d_kw, d_sinks, d_wo, d_up0, d_up1, d_fc0, d_fc1, d_down0, d_down1)


def kernel(x, meta_tokens, attn_norm_w, ffn_norm_w, mix_w_in, conv_a_w, dn_conv_w, dn_a_log, dn_dt_bias, dn_norm_w, mix_w_out, swa_wq, swa_wk, swa_wv, swa_q_norm_w, swa_k_norm_w, swa_sinks, swa_wo, ffn_w_up, ffn_conv_w, ffn_w_down, loss_target, m_meta_tokens, m_attn_norm_w, m_ffn_norm_w, m_mix_w_in, m_conv_a_w, m_dn_conv_w, m_dn_a_log, m_dn_dt_bias, m_dn_norm_w, m_mix_w_out, m_swa_wq, m_swa_wk, m_swa_wv, m_swa_q_norm_w, m_swa_k_norm_w, m_swa_sinks, m_swa_wo, m_ffn_w_up, m_ffn_conv_w, m_ffn_w_down, v_meta_tokens, v_attn_norm_w, v_ffn_norm_w, v_mix_w_in, v_conv_a_w, v_dn_conv_w, v_dn_a_log, v_dn_dt_bias, v_dn_norm_w, v_mix_w_out, v_swa_wq, v_swa_wk, v_swa_wv, v_swa_q_norm_w, v_swa_k_norm_w, v_swa_sinks, v_swa_wo, v_ffn_w_up, v_ffn_conv_w, v_ffn_w_down):
    ix, iy, ic = lax.axis_index("x"), lax.axis_index("y"), lax.axis_index("c")
    chip = 2 * ix + iy
    seq = x.shape[1]
    rows = HEAD0 + seq

    small_sharded = (conv_a_w, dn_conv_w, ffn_conv_w, meta_tokens)
    up_b, down_b = ffn_w_up.astype(BF16), ffn_w_down.astype(BF16)
    own = [mix_w_in[0].astype(BF16), mix_w_out[0].astype(BF16), swa_wq[0].astype(BF16), swa_wk[0].astype(BF16),
           swa_wv[0].astype(BF16), swa_wo[0].astype(BF16), up_b[0], up_b[1], down_b[0], down_b[1]]
    fill = lambda gathered, mine: [lax.dynamic_update_slice_in_dim(g, t[None], chip, axis=0)
                                   for g, t in zip(gathered, mine)]
    first, g_small = gather_weights(own[:1], _flat_pad(small_sharded, SW_ROWS))
    g_in, = fill(first, own[:1])
    w_in = jnp.pad(g_in.transpose(1, 0, 2).reshape(D, IN_DIM), ((0, 0), (0, P_W - IN_DIM)))
    rest = {}

    def tie(t):
        t, *mine = lax.optimization_barrier((t, *own[1:]))
        rest["w"] = fill(gather_weights_beside(mine), mine)
        return t

    def late():
        g_out, g_q, g_k, g_v, g_o, g_up0, g_up1, g_dn0, g_dn1 = rest["w"]
        wqkv = jnp.concatenate([g_q.reshape(D, D), g_k.reshape(D, 256), g_v.reshape(D, 256)], axis=1)
        return (g_out.reshape(D, D), wqkv, g_o.reshape(D, D), [g_up0, g_up1],
                [g_dn0.reshape(D_FF, D), g_dn1.reshape(D_FF, D)])

    gs = g_small.reshape(4, -1)
    ca_full = gs[:, 0:384].reshape(4, 3, 128).transpose(1, 0, 2).reshape(3, D_CONV)
    dc_full = gs[:, 384:1920].reshape(4, 4, 384).transpose(1, 0, 2).reshape(4, 3 * DN_DIM)
    fc_full = gs[:, 1920:6144].reshape(4, 2, 3, 704).transpose(1, 2, 0, 3).reshape(2, 3, D_FF)
    meta_full = gs[:, 6144:10240].reshape(4, N_META, 256).transpose(1, 0, 2).reshape(N_META, D)
    ca8, dc8 = _rows8(ca_full), _rows8(dc_full)
    fc8 = [_rows8(fc_full[0]), _rows8(fc_full[1])]
    alog, dtb = _lanes(dn_a_log[0], 4), _lanes(dn_dt_bias[0], 4)
    dnw = dn_norm_w.astype(F32)
    qw, kw = swa_q_norm_w.astype(F32), swa_k_norm_w.astype(F32)
    sinks = swa_sinks[0].astype(F32)
    anw = [attn_norm_w[0:1], attn_norm_w[1:2]]
    fnw = [ffn_norm_w[0:1], ffn_norm_w[1:2]]

    c_idx = jnp.reshape(ic, (1,)).astype(jnp.int32)
    chip_idx = jnp.stack([chip, ic]).astype(jnp.int32)
    begun = []

    def begin(tag, names, grads):
        pairs, gots = reduce_begin(grads, names, c_idx, 2 + len(begun), tag)
        begun.append((names, pairs, gots))

    (dh, loss_l, d_anw0, d_anw1, d_fnw0, d_fnw1, d_w_in, d_ca, d_dc, d_alog, d_dtb, d_dnw, d_w_out, d_wqkv, d_qw,
     d_kw, d_sinks, d_wo, d_up0, d_up1, d_fc0, d_fc1, d_down0, d_down1) = local_step(
        x[0], loss_target[0], meta_full, anw, fnw, w_in, ca8, dc8, alog, dtb, dnw, qw, kw, sinks, fc8, late,
        begin, tie)
    grad_x = dh[HEAD0:][None]

    small_parts = [jnp.concatenate([d_anw0, d_anw1], axis=0), jnp.concatenate([d_fnw0, d_fnw1], axis=0),
                   d_alog[0, 4:8], d_dtb[0, 4:8], d_dnw, d_qw, d_kw, d_sinks,
                   d_ca[:3], d_dc[:4], jnp.stack([d_fc0[:3], d_fc1[:3]]), dh[PAD:HEAD0], loss_l[0, 0:1]]
    small_shapes = [(2, D), (2, D), (1, 4), (1, 4), (1, DN_D), (1, SWA_D), (1, SWA_D), (1, SWA_H),
                    (1, 3, D_CONV), (1, 4, 3 * DN_DIM), (2, 3, D_FF), (N_META, D), ()]
    gathered_small = gather_small(_flat_pad(small_parts, SV_ROWS))

    red_big = {}
    for part in (begun[:-1], begun[-1:]):
        part_names = [n for names, _, _ in part for n in names]
        red_big.update(zip(part_names, reduce_end([p for _, ps, _ in part for p in ps],
                                                  [g for _, _, gs_ in part for g in gs_], part_names, chip_idx)))
    g_w_in, g_w_out, g_wq, g_wk, g_wv, g_wo, g_up0, g_up1, g_dn0, g_dn1 = [
        red_big[n] for n in ("w_in", "w_out", "wq", "wk", "wv", "wo", "up0", "up1", "down0", "down1")]

    grads = dict(mix_w_in=g_w_in, mix_w_out=g_w_out, swa_wq=g_wq, swa_wk=g_wk, swa_wv=g_wv, swa_wo=g_wo,
                 ffn_w_up=[g_up0, g_up1], ffn_w_down=[g_dn0, g_dn1])
    weights = dict(meta_tokens=meta_tokens, attn_norm_w=attn_norm_w, ffn_norm_w=ffn_norm_w, mix_w_in=mix_w_in,
                   conv_a_w=conv_a_w, dn_conv_w=dn_conv_w, dn_a_log=dn_a_log, dn_dt_bias=dn_dt_bias,
                   dn_norm_w=dn_norm_w, mix_w_out=mix_w_out, swa_wq=swa_wq, swa_wk=swa_wk, swa_wv=swa_wv,
                   swa_q_norm_w=swa_q_norm_w, swa_k_norm_w=swa_k_norm_w, swa_sinks=swa_sinks, swa_wo=swa_wo,
                   ffn_w_up=ffn_w_up, ffn_conv_w=ffn_conv_w, ffn_w_down=ffn_w_down)
    m_in = dict(meta_tokens=m_meta_tokens, attn_norm_w=m_attn_norm_w, ffn_norm_w=m_ffn_norm_w, mix_w_in=m_mix_w_in,
                conv_a_w=m_conv_a_w, dn_conv_w=m_dn_conv_w, dn_a_log=m_dn_a_log, dn_dt_bias=m_dn_dt_bias,
                dn_norm_w=m_dn_norm_w, mix_w_out=m_mix_w_out, swa_wq=m_swa_wq, swa_wk=m_swa_wk, swa_wv=m_swa_wv,
                swa_q_norm_w=m_swa_q_norm_w, swa_k_norm_w=m_swa_k_norm_w, swa_sinks=m_swa_sinks, swa_wo=m_swa_wo,
                ffn_w_up=m_ffn_w_up, ffn_conv_w=m_ffn_conv_w, ffn_w_down=m_ffn_w_down)
    v_in = dict(meta_tokens=v_meta_tokens, attn_norm_w=v_attn_norm_w, ffn_norm_w=v_ffn_norm_w, mix_w_in=v_mix_w_in,
                conv_a_w=v_conv_a_w, dn_conv_w=v_dn_conv_w, dn_a_log=v_dn_a_log, dn_dt_bias=v_dn_dt_bias,
                dn_norm_w=v_dn_norm_w, mix_w_out=v_mix_w_out, swa_wq=v_swa_wq, swa_wk=v_swa_wk, swa_wv=v_swa_wv,
                swa_q_norm_w=v_swa_q_norm_w, swa_k_norm_w=v_swa_k_norm_w, swa_sinks=v_swa_sinks, swa_wo=v_swa_wo,
                ffn_w_up=v_ffn_w_up, ffn_conv_w=v_ffn_conv_w, ffn_w_down=v_ffn_w_down)
    names = list(weights)
    small = [n for n in names if n not in BIG]
    delta, new_m, new_v = {}, {}, {}
    for n in BIG:
        delta[n], new_m[n], new_v[n], grads[n] = adamw(weights[n], grads[n], m_in[n], v_in[n], name=f"adamw_{n}")
    gathered_small, _ = lax.optimization_barrier((gathered_small, new_v["ffn_w_down"]))
    (g_anw, g_fnw, g_alog, g_dtb, g_dnw, g_qw, g_kw, g_sinks, g_ca_f, g_dc_f, g_fc_f, g_meta_f,
     loss) = _split_flat(sum_slots(gathered_small), small_shapes)
    grads.update(meta_tokens=lax.dynamic_slice_in_dim(g_meta_f, chip * 256, 256, axis=1), attn_norm_w=g_anw,
                 ffn_norm_w=g_fnw, conv_a_w=lax.dynamic_slice_in_dim(g_ca_f, chip * 128, 128, axis=2),
                 dn_conv_w=lax.dynamic_slice_in_dim(g_dc_f, chip * 384, 384, axis=2), dn_a_log=g_alog,
                 dn_dt_bias=g_dtb, dn_norm_w=g_dnw, swa_q_norm_w=g_qw, swa_k_norm_w=g_kw, swa_sinks=g_sinks,
                 ffn_conv_w=lax.dynamic_slice_in_dim(g_fc_f, chip * 704, 704, axis=2))
    grads = {n: grads[n].reshape(weights[n].shape) for n in names}
    shapes = [weights[n].shape for n in small]
    packed = [_flat_pad([t[n] for n in small], SW_ROWS) for t in (weights, grads, m_in, v_in)]
    for store, flat in zip((delta, new_m, new_v), adamw(*packed, name="adamw_small")):
        for n, t in zip(small, _split_flat(flat, shapes)):
            store[n] = t
    return (loss, grad_x, *[grads[n] for n in names], *[delta[n] for n in names],
            *[new_m[n] for n in names], *[new_v[n] for n in names])
```

```python
import functools

import jax
import jax.numpy as jnp
from jax import lax
from jax.experimental import pallas as pl
from jax.experimental.pallas import tpu as pltpu
from jax.experimental.pallas import tpu_sc as plsc

F32 = jnp.float32
BF16 = jnp.bfloat16
HI = lax.Precision.HIGHEST
MESH = pl.DeviceIdType.MESH

D = 1024
N_META = 16
PAD = 112
HEAD0 = PAD + N_META
D_CONV = 512
DN_H = 4
DN_D = 128
DN_DIM = 512
CH = 64
IN_DIM = 3592
P_W = 3840
BG0 = 3584
SWA_H = 16
SWA_KV = 4
SWA_D = 64
BLK = 128
NKEY = N_META + 2 * BLK
D_FF = 2816
EPS = 1e-6
LR, B1, B2, AEPS, WD, STEP = 0.001, 0.9, 0.999, 1e-08, 0.01, 10
VMEM_LIMIT = 48 * 1024 * 1024
MM_VMEM_BUDGET = 34 * 1024 * 1024
R_BIG = 6144
R_HALF = R_BIG // 2
SV_ROWS = 48
SW_ROWS = 16


def _pick(n, cands):
    for c in cands:
        if n % c == 0:
            return c
    return n


def _params(sem=None):
    return pltpu.CompilerParams(dimension_semantics=sem, vmem_limit_bytes=VMEM_LIMIT)


def _dot(a, b, ca=1, cb=0, prec=None):
    return lax.dot_general(a, b, (((ca,), (cb,)), ((), ())), precision=prec,
                           preferred_element_type=F32)


def _sigmoid(x):
    return 1.0 / (1.0 + jnp.exp(-x))


def _silu(x):
    return x * _sigmoid(x)


def _dsilu(x):
    s = _sigmoid(x)
    return s * (1.0 + x * (1.0 - s))


def _softplus(x):
    return jnp.maximum(x, 0.0) + jnp.log(1.0 + jnp.exp(-jnp.abs(x)))


def mm(a, b, *, name, ta=False, tb=False, out_dtype=F32, add=None, tm=None, tn=None, tk=None,
       b_chip=False, out_chip=False, swap_mid=False, epi=None, epi_ins=(), epi_consts=(), epi_outs=(), epi_accs=()):
    if epi is not None:
        return _mm_epi(a, b, name=name, tb=tb, tn=tn, b_chip=b_chip, swap_mid=swap_mid, epi=epi, epi_ins=epi_ins,
                       epi_consts=epi_consts, epi_outs=epi_outs, epi_accs=epi_accs)
    chip_of = _chip_order(swap_mid)
    m, k = (a.shape[1], a.shape[0]) if ta else a.shape
    if b_chip:
        n = b.shape[1] if tb else 4 * b.shape[2]
        if tb:
            tk = b.shape[2]
        else:
            tn = b.shape[2]
    else:
        n = b.shape[0] if tb else b.shape[1]
    if out_chip:
        tn = n // 4
    tn = tn or _pick(n, (1408, 1024, 768, 512, 256, 128))
    tk = tk or (_pick(k, (1408, 704, 384, 128)) if ta else _pick(k, (1024, 1408, 768, 512, 128)))
    nk = k // tk
    if tm is None:
        isz = lambda t: jnp.dtype(t.dtype).itemsize
        osz = jnp.dtype(out_dtype).itemsize
        for tm in ((1408, 1024, 512, 384, 256, 128) if ta else (1408, 704, 512, 384, 256, 128)):
            need = 2 * (tm * tk * isz(a) + tk * tn * isz(b) + tm * tn * osz + (tm * tn * 4 if add is not None else 0))
            need += tm * tn * 4 if nk > 1 else 0
            if m % tm == 0 and need <= MM_VMEM_BUDGET:
                break
        else:
            tm = m
    dims = (((0 if ta else 1,), (1 if tb else 0,)), ((), ()))

    def body(*refs):
        if add is None:
            a_ref, b_ref, o_ref, acc_ref = refs
            add_ref = None
        else:
            a_ref, b_ref, add_ref, o_ref, acc_ref = refs
        part = lax.dot_general(a_ref[...].astype(BF16), b_ref[...].astype(BF16), dims,
                               preferred_element_type=F32)

        def finish(total):
            if add_ref is not None:
                total = total + add_ref[...]
            o_ref[...] = total.astype(out_dtype)

        if nk == 1:
            finish(part)
        else:
            kk = pl.program_id(2)

            @pl.when(kk == 0)
            def _():
                acc_ref[...] = part

            @pl.when(kk > 0)
            def _():
                acc_ref[...] += part

            @pl.when(kk == nk - 1)
            def _():
                finish(acc_ref[...])

    a_spec = pl.BlockSpec((tk, tm), lambda i, j, kk: (kk, i)) if ta else pl.BlockSpec((tm, tk), lambda i, j, kk: (i, kk))
    if b_chip and tb:
        b_spec = pl.BlockSpec((None, tn, tk), lambda i, j, kk: (chip_of(kk), j, 0))
    elif b_chip:
        b_spec = pl.BlockSpec((None, tk, tn), lambda i, j, kk: (j, kk, 0))
    elif tb:
        b_spec = pl.BlockSpec((tn, tk), lambda i, j, kk: (j, kk))
    else:
        b_spec = pl.BlockSpec((tk, tn), lambda i, j, kk: (kk, j))
    o_spec = pl.BlockSpec((tm, tn), lambda i, j, kk: (i, j))
    in_specs = [a_spec, b_spec] + ([o_spec] if add is not None else [])
    args = [a, b] + ([add] if add is not None else [])
    out_spec = pl.BlockSpec((None, tm, tn), lambda i, j, kk: (chip_of(j), i, 0)) if out_chip else o_spec
    return pl.pallas_call(
        body, name=name, interpret=False,
        out_shape=jax.ShapeDtypeStruct((4, m, tn) if out_chip else (m, n), out_dtype),
        grid=(m // tm, n // tn, nk), in_specs=in_specs, out_specs=out_spec,
        scratch_shapes=[pltpu.VMEM((tm, tn) if nk > 1 else (8, 128), F32)],
        compiler_params=_params(("parallel", "parallel", "arbitrary")),
    )(*args)


def _chip_order(swap_mid):
    return (lambda k: (k % 2) * 2 + k // 2) if swap_mid else (lambda k: k)


def _mm_epi(a, b, *, name, tb, tn, b_chip, epi, epi_ins, epi_consts, epi_outs, epi_accs, swap_mid=False):
    chip_of = _chip_order(swap_mid)
    m, k = a.shape
    if b_chip:
        n = b.shape[1] if tb else 4 * b.shape[2]
        tk = b.shape[2] if tb else None
        tn = tn if tb else b.shape[2]
    else:
        n = b.shape[0] if tb else b.shape[1]
        tk = None
    tn = tn or _pick(n, (1408, 1024, 768, 512, 256, 128))
    tk = tk or _pick(k, (1024, 1408, 1280, 768, 512, 128))
    nk, nj = k // tk, n // tn
    isz = lambda t: jnp.dtype(t.dtype if hasattr(t, "dtype") else t).itemsize
    outs3 = [t if isinstance(t, tuple) else (t, n, lambda j: j) for t in epi_outs]
    side = sum(isz(t) for t, _ in epi_ins) + sum(isz(dt) for dt, _, _ in outs3)
    for tm in (1408, 704, 512, 384, 256, 128):
        need = 2 * (tm * tk * isz(a) + tk * tn * isz(b) + tm * tn * side) + (tm * tn * 4 if nk > 1 else 0)
        if m % tm == 0 and need <= MM_VMEM_BUDGET:
            break
    else:
        tm = m
    dims = (((1,), (1 if tb else 0,)), ((), ()))
    n_in, n_c, n_out, n_acc = len(epi_ins), len(epi_consts), len(epi_outs), len(epi_accs)

    def body(*refs):
        a_ref, b_ref = refs[:2]
        in_refs = refs[2:2 + n_in + n_c]
        out_refs = refs[2 + n_in + n_c:2 + n_in + n_c + n_out]
        acc_out = refs[2 + n_in + n_c + n_out:2 + n_in + n_c + n_out + n_acc]
        acc_ref = refs[-1]
        i, j, kk = pl.program_id(0), pl.program_id(1), pl.program_id(2)
        part = lax.dot_general(a_ref[...].astype(BF16), b_ref[...].astype(BF16), dims,
                               preferred_element_type=F32)

        def finish(total):
            res = epi(i * tm, total, *[r[...] for r in in_refs])
            if not isinstance(res, (tuple, list)):
                res = (res,)
            for r, v in zip(out_refs, res[:n_out]):
                r[...] = v.astype(r.dtype)
            if n_acc:
                @pl.when(jnp.logical_and(i == 0, j == 0))
                def _():
                    for r in acc_out:
                        r[...] = jnp.zeros(r.shape, r.dtype)

                for r, v in zip(acc_out, res[n_out:]):
                    r[...] += jnp.broadcast_to(v, r.shape).astype(r.dtype)

        if nk == 1:
            finish(part)
        else:
            @pl.when(kk == 0)
            def _():
                acc_ref[...] = part

            @pl.when(kk > 0)
            def _():
                acc_ref[...] += part

            @pl.when(kk == nk - 1)
            def _():
                finish(acc_ref[...])

    a_spec = pl.BlockSpec((tm, tk), lambda i, j, kk: (i, kk))
    if b_chip and tb:
        b_spec = pl.BlockSpec((None, tn, tk), lambda i, j, kk: (chip_of(kk), j, 0))
    elif b_chip:
        b_spec = pl.BlockSpec((None, tk, tn), lambda i, j, kk: (j, kk, 0))
    elif tb:
        b_spec = pl.BlockSpec((tn, tk), lambda i, j, kk: (j, kk))
    else:
        b_spec = pl.BlockSpec((tk, tn), lambda i, j, kk: (kk, j))
    in_specs = [a_spec, b_spec]
    in_specs += [pl.BlockSpec((tm, tn), lambda i, j, kk, col=col: (i, col(j))) for _, col in epi_ins]
    in_specs += [pl.BlockSpec(t.shape, lambda i, j, kk, nd=t.ndim: (0,) * nd) for t in epi_consts]
    out_specs = [pl.BlockSpec((tm, tn), lambda i, j, kk, col=col: (i, col(j))) for _, _, col in outs3]
    out_specs += [pl.BlockSpec(s, lambda i, j, kk, nd=len(s): (0,) * nd) for s, _ in epi_accs]
    out_shape = [jax.ShapeDtypeStruct((m, width), dt) for dt, width, _ in outs3]
    out_shape += [jax.ShapeDtypeStruct(s, dt) for s, dt in epi_accs]
    sem = ("arbitrary", "arbitrary", "arbitrary") if n_acc else ("parallel", "parallel", "arbitrary")
    return pl.pallas_call(
        body, name=name, interpret=False, out_shape=out_shape,
        grid=(m // tm, nj, nk), in_specs=in_specs, out_specs=out_specs,
        scratch_shapes=[pltpu.VMEM((tm, tn) if nk > 1 else (8, 128), F32)],
        compiler_params=_params(sem),
    )(a, b, *[t for t, _ in epi_ins], *epi_consts)


def cols(arr, tr, width=None, cb=0):
    width = width or arr.shape[1]
    return (arr, (tr, width), lambda i: (i, cb), "r2")


def heads(arr, tr):
    return (arr, (arr.shape[0], tr, arr.shape[2]), lambda i: (0, i, 0), "r3")


def whole(arr):
    nd = arr.ndim
    return (arr, arr.shape, lambda i: (0,) * nd, "w")


STRIP = 16


def _rows_of(ref, kind, r0, n):
    if kind == "r2":
        return ref[pl.ds(r0, n), :]
    if kind == "r3":
        return ref[:, pl.ds(r0, n), :]
    return ref[...]


def _set_rows(ref, kind, r0, n, v):
    if kind == "r2":
        ref[pl.ds(r0, n), :] = v.astype(ref.dtype)
    elif kind == "r3":
        ref[:, pl.ds(r0, n), :] = v.astype(ref.dtype)
    else:
        ref[...] = v.astype(ref.dtype)


def rowwise(fn, ins, outs, *, steps, name, accs=(), strip=None):
    n_in, n_out, n_acc = len(ins), len(outs), len(accs)
    kin = [t[3] for t in ins]
    kout = [t[4] for t in outs]
    tr = next((t[1][-2] for t in ins if t[3] != "w"), 0)

    def body(*refs):
        i = pl.program_id(0)
        in_refs, out_refs, acc_refs = refs[:n_in], refs[n_in:n_in + n_out], refs[n_in + n_out:]
        if n_acc:
            @pl.when(i == 0)
            def _():
                for r in acc_refs:
                    r[...] = jnp.zeros(r.shape, r.dtype)

        def run(r0, n):
            res = fn(i * tr + r0, *[_rows_of(r, k, r0, n) for r, k in zip(in_refs, kin)])
            if not isinstance(res, (tuple, list)):
                res = (res,)
            for r, k, v in zip(out_refs, kout, res[:n_out]):
                _set_rows(r, k, r0, n, v)
            for r, v in zip(acc_refs, res[n_out:]):
                r[...] += jnp.broadcast_to(v, r.shape).astype(r.dtype)

        if strip is None or tr <= strip:
            run(0, tr)
        else:
            def step(s, carry):
                run(pl.multiple_of(s * strip, strip), strip)
                return carry
            lax.fori_loop(0, tr // strip, step, 0)

    def zmap(nd):
        return lambda i: (0,) * nd

    in_specs = [pl.BlockSpec(t[1], t[2]) for t in ins]
    out_specs = [pl.BlockSpec(t[2], t[3]) for t in outs]
    out_specs += [pl.BlockSpec(s, zmap(len(s))) for s, _ in accs]
    out_shape = [jax.ShapeDtypeStruct(t[0], t[1]) for t in outs]
    out_shape += [jax.ShapeDtypeStruct(s, d) for s, d in accs]
    res = pl.pallas_call(
        body, name=name, interpret=False, out_shape=out_shape, grid=(steps,),
        in_specs=in_specs, out_specs=out_specs,
        compiler_params=_params(("arbitrary",)),
    )(*[t[0] for t in ins])
    return res


def out2d(rows, width, dtype, tr):
    return ((rows, width), dtype, (tr, width), lambda i: (i, 0), "r2")


def conv_fwd(xs, w8, kw, *, rows, c, tc, tr, name, post, extras=(), outs=(), pre=None, strip=STRIP):
    nx, ne, no = len(xs), len(extras), len(outs)
    nr, nc = rows // tr, c // tc
    r8 = tr // 8
    st = strip

    def body(*refs):
        x_refs = refs[:2 * nx]
        w_ref = refs[2 * nx]
        e_refs = refs[2 * nx + 1:2 * nx + 1 + ne]
        o_refs = refs[2 * nx + 1 + ne:2 * nx + 1 + ne + no]
        scr = refs[-1]
        j, i = pl.program_id(0), pl.program_id(1)
        halo = [x_refs[2 * q + 1][...].astype(F32) for q in range(nx)]
        scr[0:8, :] = jnp.where(i > 0, pre(*halo) if pre else halo[0], 0.0)

        def fill(s, carry):
            r0 = pl.multiple_of(s * st, st)
            cur = [x_refs[2 * q][pl.ds(r0, st), :].astype(F32) for q in range(nx)]
            scr[pl.ds(8 + r0, st), :] = pre(*cur) if pre else cur[0]
            return carry

        def comp(s, carry):
            r0 = pl.multiple_of(s * st, st)
            win = scr[pl.ds(r0, st + 8), :]
            y = jnp.zeros((st, tc), F32)
            for q in range(kw):
                sh = kw - 1 - q
                y = y + w_ref[q:q + 1, :] * win[8 - sh:8 - sh + st]
            res = post(j, y, *[e[pl.ds(r0, st), :] for e in e_refs])
            if not isinstance(res, (tuple, list)):
                res = (res,)
            for r, v in zip(o_refs, res):
                r[pl.ds(r0, st), :] = v.astype(r.dtype)
            return carry

        lax.fori_loop(0, tr // st, fill, 0)
        lax.fori_loop(0, tr // st, comp, 0)

    in_specs, args = [], []
    for arr, cb0 in xs:
        in_specs.append(pl.BlockSpec((tr, tc), lambda j, i, cb0=cb0: (i, cb0 + j)))
        in_specs.append(pl.BlockSpec((8, tc), lambda j, i, cb0=cb0: (jnp.maximum(i * r8 - 1, 0), cb0 + j)))
        args += [arr, arr]
    in_specs.append(pl.BlockSpec((8, tc), lambda j, i: (0, j)))
    args.append(w8)
    for arr, cb0 in extras:
        in_specs.append(pl.BlockSpec((tr, tc), lambda j, i, cb0=cb0: (i, cb0 + j)))
        args.append(arr)
    return pl.pallas_call(
        body, name=name, interpret=False,
        out_shape=[jax.ShapeDtypeStruct((rows, c), dt) for dt in outs],
        grid=(nc, nr), in_specs=in_specs,
        out_specs=[pl.BlockSpec((tr, tc), lambda j, i: (i, j)) for _ in outs],
        scratch_shapes=[pltpu.VMEM((tr + 8, tc), F32)],
        compiler_params=_params(("parallel", "arbitrary")),
    )(*args)


def conv_bwd(xs, w8, kw, dy, *, rows, c, tc, tr, name, post, extras=(), outs=(), pre=None):
    nx, ne, no = len(xs), len(extras), len(outs)
    nr, nc = rows // tr, c // tc
    r8 = tr // 8

    def body(*refs):
        x_refs = refs[:2 * nx]
        w_ref, dy_ref, dyn_ref = refs[2 * nx:2 * nx + 3]
        e_refs = refs[2 * nx + 3:2 * nx + 3 + ne]
        first_out = 2 * nx + 3 + ne
        o_refs = refs[first_out:first_out + no]
        dw_ref = refs[first_out + no]
        xscr, gscr = refs[-2], refs[-1]
        i = pl.program_id(1)
        halo = [x_refs[2 * q + 1][...].astype(F32) for q in range(nx)]
        xscr[0:8, :] = jnp.where(i > 0, pre(*halo) if pre else halo[0], 0.0)
        gscr[tr:tr + 8, :] = jnp.where(i < nr - 1, dyn_ref[...].astype(F32), 0.0)

        def fill(s, carry):
            r0 = pl.multiple_of(s * STRIP, STRIP)
            cur = [x_refs[2 * q][pl.ds(r0, STRIP), :].astype(F32) for q in range(nx)]
            xscr[pl.ds(8 + r0, STRIP), :] = pre(*cur) if pre else cur[0]
            gscr[pl.ds(r0, STRIP), :] = dy_ref[pl.ds(r0, STRIP), :].astype(F32)
            return carry

        def comp(s, dws):
            r0 = pl.multiple_of(s * STRIP, STRIP)
            gwin = gscr[pl.ds(r0, STRIP + 8), :]
            xwin = xscr[pl.ds(r0, STRIP + 8), :]
            g = gwin[0:STRIP]
            dx = jnp.zeros((STRIP, tc), F32)
            new = []
            for q in range(kw):
                sh = kw - 1 - q
                dx = dx + w_ref[q:q + 1, :] * gwin[sh:sh + STRIP]
                part = g * xwin[8 - sh:8 - sh + STRIP]
                new.append(dws[q] + part[0:8] + part[8:16])
            res = post(dx, *[e[pl.ds(r0, STRIP), :] for e in e_refs])
            if not isinstance(res, (tuple, list)):
                res = (res,)
            for r, v in zip(o_refs, res):
                r[pl.ds(r0, STRIP), :] = v.astype(r.dtype)
            return tuple(new)

        lax.fori_loop(0, tr // STRIP, fill, 0)
        dws = lax.fori_loop(0, tr // STRIP, comp, tuple(jnp.zeros((8, tc), F32) for _ in range(kw)))

        @pl.when(i == 0)
        def _():
            dw_ref[...] = jnp.zeros((8, tc), F32)

        dw_ref[...] += jnp.concatenate([jnp.sum(t, axis=0, keepdims=True) for t in dws]
                                       + [jnp.zeros((8 - kw, tc), F32)], axis=0)

    in_specs, args = [], []
    for arr, cb0 in xs:
        in_specs.append(pl.BlockSpec((tr, tc), lambda j, i, cb0=cb0: (i, cb0 + j)))
        in_specs.append(pl.BlockSpec((8, tc), lambda j, i, cb0=cb0: (jnp.maximum(i * r8 - 1, 0), cb0 + j)))
        args += [arr, arr]
    in_specs.append(pl.BlockSpec((8, tc), lambda j, i: (0, j)))
    in_specs.append(pl.BlockSpec((tr, tc), lambda j, i: (i, j)))
    in_specs.append(pl.BlockSpec((8, tc), lambda j, i: (jnp.minimum((i + 1) * r8, nr * r8 - 1), j)))
    args += [w8, dy, dy]
    for arr, cb0 in extras:
        in_specs.append(pl.BlockSpec((tr, tc), lambda j, i, cb0=cb0: (i, cb0 + j)))
        args.append(arr)
    return pl.pallas_call(
        body, name=name, interpret=False,
        out_shape=[jax.ShapeDtypeStruct((rows, c), dt) for dt in outs] + [jax.ShapeDtypeStruct((8, c), F32)],
        grid=(nc, nr), in_specs=in_specs,
        out_specs=[pl.BlockSpec((tr, tc), lambda j, i: (i, j)) for _ in outs] + [pl.BlockSpec((8, tc), lambda j, i: (0, j))],
        scratch_shapes=[pltpu.VMEM((tr + 8, tc), F32), pltpu.VMEM((tr + 8, tc), F32)],
        compiler_params=_params(("parallel", "arbitrary")),
    )(*args)


def rms_fwd(h, w, *, name):
    rows = h.shape[0]
    tr = _pick(rows, (384, 128))

    def fn(i, x, wv):
        r = lax.rsqrt(jnp.mean(x * x, axis=1, keepdims=True) + EPS)
        return x * r * wv

    return rowwise(fn, [cols(h, tr), whole(w)], [out2d(rows, D, BF16, tr)], steps=rows // tr, name=name)[0]


def _rms_bwd_epi(row0, g, x, dr, wv):
    r = lax.rsqrt(jnp.mean(x * x, axis=1, keepdims=True) + EPS)
    xh = x * r
    gw = g * wv
    dx = r * (gw - xh * jnp.mean(gw * xh, axis=1, keepdims=True))
    row = row0 + lax.broadcasted_iota(jnp.int32, (x.shape[0], 1), 0)
    return jnp.where(row >= PAD, dr + dx, 0.0), jnp.sum(g * xh, axis=0, keepdims=True)


def dx_rms_bwd(dy, w, h, nw, dres, *, name, b_chip=False, swap_mid=False):
    return mm(dy, w, tb=True, b_chip=b_chip, swap_mid=swap_mid, tn=D, name=name, epi=_rms_bwd_epi,
              epi_ins=[(h, lambda j: 0), (dres, lambda j: 0)], epi_consts=[nw], epi_outs=[F32],
              epi_accs=[((1, D), F32)])


def loss_grad(h, target):
    rows = h.shape[0]

    def fn(i, y, t):
        diff = jnp.where(i >= HEAD0, y - t, 0.0)
        part = jnp.sum(jnp.sum(diff * diff, axis=1, keepdims=True), axis=0, keepdims=True)
        return diff * (1.0 / D), part * (0.5 / D)

    tgt = (target, (BLK, D), lambda i: (jnp.maximum(i - 1, 0), 0), "r2")
    return rowwise(fn, [cols(h, BLK), tgt], [out2d(rows, D, F32, BLK)], steps=rows // BLK,
                   name="loss_grad", accs=[((1, 128), F32)])


def adamw(w, g, m, v, *, name):
    shape = w.shape
    gs = list(g) if isinstance(g, (list, tuple)) else [g]
    nl = len(gs)
    width = shape[-1]
    rows = w.size // width
    rl = rows // nl
    tr = _pick(rl, (256, 176, 128, 64, 16, 8))
    nr = rl // tr
    if w.ndim == 3 and shape[1] % tr == 0:
        per = shape[1] // tr
        view = lambda t: (t, (None, tr, width), lambda i: (i // per, i % per, 0), "r2")
        out = (shape, F32, (None, tr, width), lambda i: (i // per, i % per, 0), "r2")
    else:
        view = lambda t: cols(t.reshape(rows, width), tr)
        out = out2d(rows, width, F32, tr)

    def fn(i, wv, mv, vv, *gvs):
        gv = gvs[0]
        for layer in range(1, nl):
            gv = jnp.where(i >= layer * rl, gvs[layer], gv)
        mn = B1 * mv + (1.0 - B1) * gv
        vn = B2 * vv + (1.0 - B2) * gv * gv
        mh = mn / (1.0 - B1 ** STEP)
        vh = vn / (1.0 - B2 ** STEP)
        return -LR * (mh / (jnp.sqrt(vh) + AEPS) + WD * wv), mn, vn, gv

    g_ins = [(t.reshape(rl, width), (tr, width), lambda i, layer=layer: (jnp.clip(i - layer * nr, 0, nr - 1), 0), "r2")
             for layer, t in enumerate(gs)]
    res = rowwise(fn, [view(t) for t in (w, m, v)] + g_ins, [out] * 4, steps=rows // tr, name=name)
    return [r.reshape(shape) for r in res]


HB = DN_H * CH
PAIR = 3


def _split(a):
    hi = a.astype(BF16)
    return hi, (a - hi.astype(F32)).astype(BF16)


def _dot1(a, b, ca=1, cb=0):
    return _dot(a.astype(BF16), b.astype(BF16), ca, cb)


def _dot3(a, b, ca=1, cb=0):
    ah, al = _split(a)
    bh, bl = _split(b)
    return _dot(ah, bh, ca, cb) + (_dot(ah, bl, ca, cb) + _dot(al, bh, ca, cb))


def _dot01(m01, b, ca=1, cb=0):
    bh, bl = _split(b)
    m = m01.astype(BF16)
    return _dot(m, bh, ca, cb) + _dot(m, bl, ca, cb)


def _stack(x):
    return jnp.concatenate([x[:, h * DN_D:(h + 1) * DN_D] for h in range(DN_H)], axis=0)


def _unstack(x):
    return jnp.concatenate([x[h * CH:(h + 1) * CH] for h in range(DN_H)], axis=1)


def _tri_inv(mats, blk, eye):
    each = lambda f, *lists: [f(*t) for t in zip(*lists)]
    ad = [jnp.where(blk, a, 0.0) for a in mats]
    lo = each(lambda a, d: a - d, mats, ad)
    a2 = each(_dot3, ad, ad)
    a4 = each(_dot3, a2, a2)
    a8 = each(_dot3, a4, a4)
    dgi = each(lambda d, s: _dot3(eye - d, eye + s), ad, a2)
    dgi = each(lambda p, s: _dot3(p, eye + s), dgi, a4)
    dgi = each(lambda p, s: _dot3(p, eye + s), dgi, a8)
    n = each(_dot3, dgi, lo)
    n2 = each(_dot3, n, n)
    return each(_dot3, each(lambda u, v: _dot3(eye - u, eye + v), n, n2), dgi)


def _dn_masks():
    row = lax.broadcasted_iota(jnp.int32, (HB, HB), 0)
    col = lax.broadcasted_iota(jnp.int32, (HB, HB), 1)
    same = (row // CH) == (col // CH)
    incl = jnp.logical_and(same, row >= col)
    strict = jnp.logical_and(same, row > col)
    upper = jnp.logical_and(same, row <= col)
    blk = (row // 16) == (col // 16)
    eye = (row == col).astype(F32)
    return incl, strict, upper, blk, eye


def _dn_chunk(qv, kv, vv, bc, br, incl, strict):
    r64 = lax.broadcasted_iota(jnp.int32, (CH, CH), 0)
    c64 = lax.broadcasted_iota(jnp.int32, (CH, CH), 1)
    dcol = _dot01((r64 >= c64).astype(F32), bc)
    drow = _dot3(br, (r64 <= c64).astype(F32))
    col = lambda m, l0: jnp.concatenate([m[:, l0 + h:l0 + h + 1] for h in range(DN_H)], axis=0)
    b_c = col(bc, 0)
    d_c = col(dcol, 4)
    d_r = jnp.concatenate([drow[4 + h:5 + h, :] for h in range(DN_H)], axis=1)
    d_last_h = [dcol[CH - 1:CH, 4 + h:5 + h] for h in range(DN_H)]
    d_last = jnp.concatenate([jnp.broadcast_to(t, (CH, 1)) for t in d_last_h], axis=0)
    q, k, v = _stack(qv), _stack(kv), _stack(vv)
    dm = jnp.where(incl, jnp.exp(jnp.where(incl, d_c - d_r, 0.0)), 0.0)
    kk = _dot1(k, k, 1, 1)
    a = jnp.where(strict, b_c * kk * dm, 0.0)
    ed = jnp.exp(d_c)
    rhs = jnp.concatenate([v * b_c, k * (b_c * ed)], axis=1)
    qk = _dot1(q, k, 1, 1) * dm
    ekd = jnp.exp(d_last - d_c)
    gl = [jnp.exp(t) for t in d_last_h]
    return q, k, v, b_c, dm, kk, a, ed, rhs, qk, ekd, gl


def dn_fwd(qkv_n, bgcol, bgrow):
    rows = qkv_n.shape[0]
    nch = rows // CH

    def body(q_ref, k_ref, v_ref, bc_ref, br_ref, o_ref, s_out, ti_out, s_scr, prep, prep_qk, prep_gl):
        n = pl.program_id(0)

        @pl.when(n == 0)
        def _():
            s_scr[...] = jnp.zeros(s_scr.shape, F32)
            prep[...] = jnp.zeros(prep.shape, F32)
            prep_qk[...] = jnp.zeros(prep_qk.shape, F32)
            prep_gl[...] = jnp.zeros(prep_gl.shape, F32)

        live = n > 0
        rows_of = [slice(h * CH, (h + 1) * CH) for h in range(DN_H)]
        s = [s_scr[h] for h in range(DN_H)]
        for c in range(PAIR):
            u, w, qd, kd = prep[c, 0], prep[c, 1], prep[c, 2], prep[c, 3]
            for h in range(DN_H):
                s_out[c, h] = s[h]
            v_new = [u[rs] - _dot1(w[rs], s[h]) for h, rs in enumerate(rows_of)]
            o_state = [_dot1(qd[rs], s[h]) for h, rs in enumerate(rows_of)]
            s = [jnp.where(live, prep_gl[c, h:h + 1, 0:1] * s[h] + _dot1(kd[rs], v_new[h], 0, 0), s[h])
                 for h, rs in enumerate(rows_of)]
            o = jnp.concatenate(o_state, axis=0) + _dot1(prep_qk[c], jnp.concatenate(v_new, axis=0))
            o_ref[c * CH:(c + 1) * CH, :] = _unstack(o)
        for h in range(DN_H):
            s_scr[h] = s[h]

        incl, strict, _, blk, eye = _dn_masks()
        parts = []
        for c in range(PAIR):
            rows_c = slice(c * CH, (c + 1) * CH)
            parts.append(_dn_chunk(q_ref[rows_c, :], k_ref[rows_c, :], v_ref[rows_c, :], bc_ref[rows_c, :],
                                   br_ref[c], incl, strict))
        tinvs = _tri_inv([p[6] for p in parts], blk, eye)
        for c, (q, k, v, b_c, dm, kk, a, ed, rhs, qk_n, ekd, gl) in enumerate(parts):
            tinv = tinvs[c]
            ti_out[c] = tinv
            sol = _dot3(tinv, rhs)
            prep[c, 0] = sol[:, :DN_D]
            prep[c, 1] = sol[:, DN_D:]
            prep[c, 2] = q * ed
            prep[c, 3] = k * ekd
            prep_qk[c] = qk_n
            prep_gl[c] = jnp.concatenate([jnp.broadcast_to(t, (1, 128)) for t in gl]
                                         + [jnp.zeros((8 - DN_H, 128), F32)], axis=0)

    assert nch % PAIR == 0
    npair = nch // PAIR
    last = npair - 1
    return pl.pallas_call(
        body, name="dn_fwd", interpret=False,
        out_shape=[jax.ShapeDtypeStruct((rows, DN_DIM), F32),
                   jax.ShapeDtypeStruct((nch, DN_H, DN_D, DN_D), F32),
                   jax.ShapeDtypeStruct((nch, HB, HB), F32)],
        grid=(npair + 1,),
        in_specs=[pl.BlockSpec((PAIR * CH, DN_DIM), lambda n: (jnp.minimum(n, last), 0)),
                  pl.BlockSpec((PAIR * CH, DN_DIM), lambda n: (jnp.minimum(n, last), 1)),
                  pl.BlockSpec((PAIR * CH, DN_DIM), lambda n: (jnp.minimum(n, last), 2)),
                  pl.BlockSpec((PAIR * CH, 128), lambda n: (jnp.minimum(n, last), 0)),
                  pl.BlockSpec((PAIR, 8, CH), lambda n: (jnp.minimum(n, last), 0, 0))],
        out_specs=[pl.BlockSpec((PAIR * CH, DN_DIM), lambda n: (jnp.maximum(n - 1, 0), 0)),
                   pl.BlockSpec((PAIR, DN_H, DN_D, DN_D), lambda n: (jnp.maximum(n - 1, 0), 0, 0, 0)),
                   pl.BlockSpec((PAIR, HB, HB), lambda n: (jnp.minimum(n, last), 0, 0))],
        scratch_shapes=[pltpu.VMEM((DN_H, DN_D, DN_D), F32), pltpu.VMEM((PAIR, 4, HB, DN_D), F32),
                        pltpu.VMEM((PAIR, HB, HB), F32), pltpu.VMEM((PAIR, 8, 128), F32)],
        compiler_params=_params(("arbitrary",)),
    )(qkv_n, qkv_n, qkv_n, bgcol, bgrow)


def dn_bwd(qkv_n, bgcol, bgrow, s_all, ti_all, do):
    rows = qkv_n.shape[0]
    nch = rows // CH

    def body(q_ref, k_ref, v_ref, bc_ref, br_ref, s_ref, ti_ref, do_ref, dq_ref, dk_ref, dv_ref, dbg_ref, ds_scr):
        n = pl.program_id(0)

        @pl.when(n == 0)
        def _():
            ds_scr[...] = jnp.zeros(ds_scr.shape, F32)

        incl, strict, upper, _, _ = _dn_masks()
        rsum = lambda t: jnp.sum(t, axis=1, keepdims=True)
        rows_of = [slice(h * CH, (h + 1) * CH) for h in range(DN_H)]
        heads_of = lambda f: jnp.concatenate([f(h, rs) for h, rs in enumerate(rows_of)], axis=0)
        cs = []
        for c in reversed(range(PAIR)):
            rc = slice(c * CH, (c + 1) * CH)
            q, k, v, b_c, dm, kk, a, ed, rhs, qk, ekd, gl = _dn_chunk(
                q_ref[rc, :], k_ref[rc, :], v_ref[rc, :], bc_ref[rc, :], br_ref[c], incl, strict)
            cs.append(dict(rc=rc, q=q, k=k, v=v, b_c=b_c, dm=dm, kk=kk, a=a, ed=ed, rhs=rhs, qk=qk, ekd=ekd, gl=gl,
                           tinv=ti_ref[c], g=_stack(do_ref[rc, :]), s=[s_ref[c, h] for h in range(DN_H)]))
        for t in cs:
            t["sol"] = _dot3(t["tinv"], t["rhs"])
        for t in cs:
            t["u"], t["w"] = t["sol"][:, :DN_D], t["sol"][:, DN_D:]
            t["qd"], t["kd"] = t["q"] * t["ed"], t["k"] * t["ekd"]
            t["v_new"] = heads_of(lambda h, rs: t["u"][rs] - _dot1(t["w"][rs], t["s"][h]))
            t["dv0"] = _dot1(t["qk"], t["g"], 0, 0)
            t["ds0"] = [_dot1(t["qd"][rs], t["g"][rs], 0, 0) for rs in rows_of]
            t["dqd"] = heads_of(lambda h, rs: _dot1(t["g"][rs], t["s"][h], 1, 1))
        for t in cs:
            t["dqk"] = _dot1(t["g"], t["v_new"], 1, 1)
        ds = [ds_scr[h] for h in range(DN_H)]
        for t in cs:
            t["ds"] = ds
            t["dv_new"] = t["dv0"] + heads_of(lambda h, rs: _dot1(t["kd"][rs], ds[h]))
            ds = [t["ds0"][h] + t["gl"][h] * ds[h] - _dot1(t["w"][rs], t["dv_new"][rs], 0, 0)
                  for h, rs in enumerate(rows_of)]
        for h in range(DN_H):
            ds_scr[h] = ds[h]
        for t in cs:
            t["dkd"] = heads_of(lambda h, rs: _dot1(t["v_new"][rs], t["ds"][h], 1, 1))
            dw = heads_of(lambda h, rs: -_dot1(t["dv_new"][rs], t["s"][h], 1, 1))
            t["dsol"] = jnp.concatenate([t["dv_new"], dw], axis=1)
        for t in cs:
            t["drhs"] = _dot3(t["tinv"], t["dsol"], 0, 0)
        for t in cs:
            t["da"] = jnp.where(strict, -_dot1(t["drhs"], t["sol"], 1, 1), 0.0)
        rowi = lax.broadcasted_iota(jnp.int32, (CH, 1), 0)
        lane = lax.broadcasted_iota(jnp.int32, (CH, 128), 1)
        for t in cs:
            q, k, v, b_c, dm, ed, da, dqk = t["q"], t["k"], t["v"], t["b_c"], t["dm"], t["ed"], t["da"], t["dqk"]
            drhs_u, drhs_w = t["drhs"][:, :DN_D], t["drhs"][:, DN_D:]
            s2 = rsum(drhs_w * k)
            dbeta = rsum(drhs_u * v) + s2 * ed + rsum(da * t["kk"] * dm)
            dkk = da * b_c * dm
            dqkr = dqk * dm
            mmat = da * t["a"] + dqk * t["qk"]
            tmp = rsum(t["dkd"] * t["kd"])
            dd = (s2 * b_c * ed + rsum(mmat) - _dot3(mmat, jnp.ones((HB, 128), F32), 0, 0)[:, :1]
                  + rsum(t["dqd"] * t["qd"]) - tmp)
            last = []
            for h, rs in enumerate(rows_of):
                dgl = jnp.sum(rsum(t["s"][h] * t["ds"][h]), axis=0, keepdims=True)
                dd_last = jnp.sum(tmp[rs], axis=0, keepdims=True) + dgl * t["gl"][h]
                last.append(jnp.where(rowi == CH - 1, dd_last, 0.0))
            dd = dd + jnp.concatenate(last, axis=0)
            rc = t["rc"]
            dq_ref[rc, :] = _unstack(_dot1(dqkr, k) + t["dqd"] * ed)
            dk_ref[rc, :] = _unstack(drhs_w * (b_c * ed) + _dot1(dkk, k) + _dot1(dkk, k, 0, 0) + _dot1(dqkr, q, 0, 0)
                                     + t["dkd"] * t["ekd"])
            dv_ref[rc, :] = _unstack(drhs_u * b_c)
            dg = _dot01(upper.astype(F32), jnp.broadcast_to(dd, (HB, 128)))[:, :1]
            out = jnp.zeros((CH, 128), F32)
            for h, rs in enumerate(rows_of):
                out = out + jnp.where(lane == h, dbeta[rs], 0.0) + jnp.where(lane == 4 + h, dg[rs], 0.0)
            dbg_ref[rc, :] = out

    assert nch % PAIR == 0
    npair = nch // PAIR
    rev = lambda n: npair - 1 - n
    blk = PAIR * CH
    return pl.pallas_call(
        body, name="dn_bwd", interpret=False,
        out_shape=[jax.ShapeDtypeStruct((rows, DN_DIM), F32)] * 3 + [jax.ShapeDtypeStruct((rows, 128), F32)],
        grid=(npair,),
        in_specs=[pl.BlockSpec((blk, DN_DIM), lambda n: (rev(n), 0)),
                  pl.BlockSpec((blk, DN_DIM), lambda n: (rev(n), 1)),
                  pl.BlockSpec((blk, DN_DIM), lambda n: (rev(n), 2)),
                  pl.BlockSpec((blk, 128), lambda n: (rev(n), 0)),
                  pl.BlockSpec((PAIR, 8, CH), lambda n: (rev(n), 0, 0)),
                  pl.BlockSpec((PAIR, DN_H, DN_D, DN_D), lambda n: (rev(n), 0, 0, 0)),
                  pl.BlockSpec((PAIR, HB, HB), lambda n: (rev(n), 0, 0)),
                  pl.BlockSpec((blk, DN_DIM), lambda n: (rev(n), 0))],
        out_specs=[pl.BlockSpec((blk, DN_DIM), lambda n: (rev(n), 0))] * 3 + [pl.BlockSpec((blk, 128), lambda n: (rev(n), 0))],
        scratch_shapes=[pltpu.VMEM((DN_H, DN_D, DN_D), F32)],
        compiler_params=_params(("arbitrary",)),
    )(qkv_n, qkv_n, qkv_n, bgcol, bgrow, s_all, ti_all, do)


def _swa_valid(n):
    c3 = lax.broadcasted_iota(jnp.int32, (NKEY, 4 * BLK), 0)
    r = lax.broadcasted_iota(jnp.int32, (NKEY, 4 * BLK), 1) % BLK
    prev0 = N_META + BLK
    c = jnp.where(c3 < N_META, PAD + c3, jnp.where(c3 < prev0, c3 - N_META, c3 - prev0))
    lo = jnp.where(c3 < N_META, 0, jnp.where(c3 < prev0, r + 1 + jnp.where(n >= 2, 0, BLK), 0))
    hi = jnp.where(c3 < N_META, r + jnp.where(n >= 1, BLK, 0),
                   jnp.where(c3 < prev0, BLK, r - jnp.where(n >= 1, 0, BLK)))
    return jnp.logical_and(c >= lo, c <= hi)


def _swa_probs(qs, kcats, valid, sinks):
    s = [jnp.where(valid, _dot(kc, q, 1, 1), -1e30) for q, kc in zip(qs, kcats)]
    m = [jnp.maximum(jnp.max(t, axis=0, keepdims=True), sk) for t, sk in zip(s, sinks)]
    e = [jnp.where(valid, jnp.exp(t - mx), 0.0) for t, mx in zip(s, m)]
    es = [jnp.exp(sk - mx) for sk, mx in zip(sinks, m)]
    inv = [1.0 / (jnp.sum(t, axis=0, keepdims=True) + u) for t, u in zip(e, es)]
    return [t * i for t, i in zip(e, inv)], [u * i for u, i in zip(es, inv)]


def _swa_group(q_ref, sk_ref, h):
    q4 = jnp.concatenate([q_ref[4 * h + g] for g in range(4)], axis=0)
    sink4 = jnp.concatenate([jnp.full((1, BLK), sk_ref[4 * h + g], F32) for g in range(4)], axis=1)
    return q4, sink4


def _swa_specs():
    q = pl.BlockSpec((SWA_H, BLK, SWA_D), lambda n: (0, n, 0))
    km = pl.BlockSpec((SWA_KV, N_META, SWA_D), lambda n: (0, PAD // N_META, 0))
    kp = pl.BlockSpec((SWA_KV, BLK, SWA_D), lambda n: (0, jnp.maximum(n - 1, 0), 0))
    kc = pl.BlockSpec((SWA_KV, BLK, SWA_D), lambda n: (0, n, 0))
    return [q, km, kp, kc, km, kp, kc]


def swa_fwd(qh, kh, vh, sinks):
    rows = qh.shape[1]
    nb = rows // BLK

    def body(q_ref, km, kp, kc, vm, vp, vc, sk_ref, o_ref):
        n = pl.program_id(0)
        valid = _swa_valid(n)
        kcats = [jnp.concatenate([km[h], kp[h], kc[h]], axis=0) for h in range(SWA_KV)]
        vcats = [jnp.concatenate([vm[h], vp[h], vc[h]], axis=0) for h in range(SWA_KV)]
        qs, sinks4 = zip(*[_swa_group(q_ref, sk_ref, h) for h in range(SWA_KV)])
        ps, _ = _swa_probs(qs, kcats, valid, sinks4)
        o4s = [_dot(p.astype(BF16), vc_, 0, 0) for p, vc_ in zip(ps, vcats)]
        o_ref[...] = jnp.concatenate([o4[g * BLK:(g + 1) * BLK] for o4 in o4s for g in range(4)],
                                     axis=1).astype(BF16)

    return pl.pallas_call(
        body, name="swa_fwd", interpret=False,
        out_shape=jax.ShapeDtypeStruct((rows, SWA_H * SWA_D), BF16),
        grid=(nb,),
        in_specs=_swa_specs() + [pl.BlockSpec(memory_space=pltpu.SMEM)],
        out_specs=pl.BlockSpec((BLK, SWA_H * SWA_D), lambda n: (n, 0)),
        compiler_params=_params(("parallel",)),
    )(qh, kh, kh, kh, vh, vh, vh, sinks)


def swa_bwd(qh, kh, vh, sinks, do):
    rows = qh.shape[1]
    nb = rows // BLK

    def body(q_ref, km, kp, kc, vm, vp, vc, do_ref, sk_ref, dq_ref, dk_ref, dv_ref, dsk_ref):
        n = pl.program_id(0)

        @pl.when(n == 0)
        def _():
            dk_ref[...] = jnp.zeros(dk_ref.shape, F32)
            dv_ref[...] = jnp.zeros(dv_ref.shape, F32)

        valid = _swa_valid(n)
        g_all = do_ref[...]
        rowi = lax.broadcasted_iota(jnp.int32, (SWA_H, 128), 0)
        dsk = jnp.zeros((SWA_H, 128), F32)
        pm = pl.multiple_of(jnp.maximum(n - 1, 0) * BLK, BLK)
        pc = pl.multiple_of(n * BLK, BLK)
        hs = range(SWA_KV)
        kcats = [jnp.concatenate([km[h], kp[h], kc[h]], axis=0) for h in hs]
        vcats = [jnp.concatenate([vm[h], vp[h], vc[h]], axis=0) for h in hs]
        qs, sinks4 = zip(*[_swa_group(q_ref, sk_ref, h) for h in hs])
        g4s = [jnp.concatenate([g_all[:, (4 * h + g) * SWA_D:(4 * h + g + 1) * SWA_D] for g in range(4)], axis=0)
               for h in hs]
        ps, pss = _swa_probs(qs, kcats, valid, sinks4)
        dps = [_dot(vc_, g4, 1, 1) for vc_, g4 in zip(vcats, g4s)]
        deltas = [jnp.sum(p * dp, axis=0, keepdims=True) for p, dp in zip(ps, dps)]
        dss = [(p * (dp - dl)).astype(BF16) for p, dp, dl in zip(ps, dps, deltas)]
        dq4s = [_dot(ds, kc_, 0, 0) for ds, kc_ in zip(dss, kcats)]
        dkcs = [_dot(ds, q4) for ds, q4 in zip(dss, qs)]
        dvcs = [_dot(p.astype(BF16), g4) for p, g4 in zip(ps, g4s)]
        for h in hs:
            t = pss[h] * deltas[h]
            for g in range(4):
                dq_ref[4 * h + g] = dq4s[h][g * BLK:(g + 1) * BLK]
                part = -jnp.sum(t[:, g * BLK:(g + 1) * BLK], axis=1, keepdims=True)
                dsk = dsk + jnp.where(rowi == 4 * h + g, part, 0.0)
            lanes = slice(h * SWA_D, (h + 1) * SWA_D)
            for ref, val in ((dk_ref, dkcs[h]), (dv_ref, dvcs[h])):
                ref[PAD:BLK, lanes] += val[0:N_META]
                ref[pl.ds(pm, BLK), lanes] += val[N_META:N_META + BLK]
                ref[pl.ds(pc, BLK), lanes] += val[N_META + BLK:]
        dsk_ref[0] = dsk

    return pl.pallas_call(
        body, name="swa_bwd", interpret=False,
        out_shape=[jax.ShapeDtypeStruct((SWA_H, rows, SWA_D), F32),
                   jax.ShapeDtypeStruct((rows, SWA_KV * SWA_D), F32),
                   jax.ShapeDtypeStruct((rows, SWA_KV * SWA_D), F32),
                   jax.ShapeDtypeStruct((nb, SWA_H, 128), F32)],
        grid=(nb,),
        in_specs=_swa_specs() + [pl.BlockSpec((BLK, SWA_H * SWA_D), lambda n: (n, 0)),
                                 pl.BlockSpec(memory_space=pltpu.SMEM)],
        out_specs=[pl.BlockSpec((SWA_H, BLK, SWA_D), lambda n: (0, n, 0)),
                   pl.BlockSpec((rows, SWA_KV * SWA_D), lambda n: (0, 0)),
                   pl.BlockSpec((rows, SWA_KV * SWA_D), lambda n: (0, 0)),
                   pl.BlockSpec((1, SWA_H, 128), lambda n: (n, 0, 0))],
        compiler_params=_params(("arbitrary",)),
    )(qh, kh, kh, kh, vh, vh, vh, do, sinks)


QK_W = (SWA_H + SWA_KV) * SWA_D


def _head_mean(t):
    r = lax.broadcasted_iota(jnp.int32, (128, 128), 0) // SWA_D
    c = lax.broadcasted_iota(jnp.int32, (128, 128), 1) // SWA_D
    blk = jnp.where(r == c, 1.0 / SWA_D, 0.0).astype(BF16)
    out = []
    for i in range(t.shape[1] // 128):
        hi, lo = _split(t[:, 128 * i:128 * (i + 1)])
        out.append(_dot(hi, blk) + _dot(lo, blk))
    return jnp.concatenate(out, axis=1)


def _qk_scales(qw, kw):
    scale = SWA_D ** -0.5
    wt = jnp.concatenate([jnp.tile(qw.astype(F32) * scale, (1, SWA_H)), jnp.tile(kw.astype(F32), (1, SWA_KV))], axis=1)
    st = jnp.concatenate([jnp.full((1, SWA_H * SWA_D), scale, F32), jnp.ones((1, SWA_KV * SWA_D), F32)], axis=1)
    return wt, st


def qknorm_fwd(qkv, qw, kw):
    rows = qkv.shape[0]
    tr = _pick(rows, (384, 128))
    wt, _ = _qk_scales(qw, kw)

    def fn(i, x, w):
        xq = x[:, :QK_W]
        y = xq * lax.rsqrt(_head_mean(xq * xq) + EPS) * w
        head = lambda t, j: t[:, j * SWA_D:(j + 1) * SWA_D][None]
        qo = jnp.concatenate([head(y, j) for j in range(SWA_H)], axis=0)
        ko = jnp.concatenate([head(y, SWA_H + j) for j in range(SWA_KV)], axis=0)
        vo = jnp.concatenate([head(x, SWA_H + SWA_KV + j) for j in range(SWA_KV)], axis=0)
        return qo, ko, vo

    hm = lambda nh: ((nh, rows, SWA_D), BF16, (nh, tr, SWA_D), lambda i: (0, i, 0), "r3")
    return rowwise(fn, [cols(qkv, tr), whole(wt)], [hm(SWA_H), hm(SWA_KV), hm(SWA_KV)],
                   steps=rows // tr, name="qknorm_fwd")


def qknorm_bwd(qkv, qw, kw, dqh, dk, dv):
    rows = qkv.shape[0]
    tr = _pick(rows, (384, 128))
    wt, st = _qk_scales(qw, kw)

    def fn(i, x, w, sc, dq, dkv, dvv):
        xq = x[:, :QK_W]
        dy = jnp.concatenate([dq[j] for j in range(SWA_H)] + [dkv], axis=1)
        r = lax.rsqrt(_head_mean(xq * xq) + EPS)
        xh = xq * r
        gw = dy * w
        dx = r * (gw - xh * _head_mean(gw * xh))
        return jnp.concatenate([dx, dvv], axis=1), jnp.sum(dy * sc * xh, axis=0, keepdims=True)

    dqkv, dw = rowwise(fn, [cols(qkv, tr), whole(wt), whole(st), heads(dqh, tr), cols(dk, tr), cols(dv, tr)],
                       [out2d(rows, 1536, BF16, tr)], steps=rows // tr, name="qknorm_bwd", accs=[((1, QK_W), F32)])
    dw = dw.reshape(SWA_H + SWA_KV, SWA_D)
    return dqkv, jnp.sum(dw[:SWA_H], axis=0, keepdims=True), jnp.sum(dw[SWA_H:], axis=0, keepdims=True)


def _place():
    return lax.axis_index("x"), lax.axis_index("y"), lax.axis_index("c")


ANY = pl.BlockSpec(memory_space=pl.ANY)


def _rcopy(ssem, rsem, k, src, dst, to):
    return pltpu.make_async_remote_copy(src_ref=src, dst_ref=dst, send_sem=ssem.at[k], recv_sem=rsem.at[k],
                                        device_id=to, device_id_type=MESH)


def gather_weights(shards, small):
    n = len(shards)
    halves = [t.shape[0] // 2 for t in shards]

    def body(*refs):
        s_refs, small_ref = refs[:n], refs[n]
        o_refs, osmall = refs[n + 1:2 * n + 1], refs[2 * n + 1]
        ssem, rsem, lsem = refs[2 * n + 2:]
        x, y, c = _place()
        me = 2 * x + y
        chips = [(1 - x, y), (x, 1 - y), (1 - x, 1 - y)]

        def half(k, s, hh):
            return o_refs[k].at[s, pl.ds(hh * halves[k], halves[k]), :]

        loc = pltpu.make_async_copy(small_ref, osmall.at[me], lsem)
        loc.start()
        sends = []
        for k in range(n):
            for j, (px, py) in enumerate(chips):
                sends.append(_rcopy(ssem, rsem, 6 * k + j, s_refs[k].at[pl.ds(c * halves[k], halves[k]), :],
                                    half(k, me, c), (px, py, c)))
        for j, (px, py) in enumerate(chips):
            sends.append(_rcopy(ssem, rsem, 6 * n + j, small_ref, osmall.at[me], (px, py, c)))
        for cp in sends:
            cp.start()
        for k in range(n):
            for j, (px, py) in enumerate(chips):
                s = 2 * px + py
                _rcopy(ssem, rsem, 6 * k + j, half(k, s, c), half(k, s, c), (x, y, c)).wait_recv()
                fwd = _rcopy(ssem, rsem, 6 * k + 3 + j, half(k, s, c), half(k, s, c), (x, y, 1 - c))
                fwd.start()
                sends.append(fwd)
        for k in range(n):
            for j, (px, py) in enumerate(chips):
                s = 2 * px + py
                _rcopy(ssem, rsem, 6 * k + 3 + j, half(k, s, 1 - c), half(k, s, 1 - c), (x, y, c)).wait_recv()
        for j, (px, py) in enumerate(chips):
            s = 2 * px + py
            _rcopy(ssem, rsem, 6 * n + j, osmall.at[s], osmall.at[s], (x, y, c)).wait_recv()
        for cp in sends:
            cp.wait_send()
        loc.wait()

    res = pl.pallas_call(
        body, name="gather_weights", interpret=False,
        out_shape=[jax.ShapeDtypeStruct((4,) + t.shape, t.dtype) for t in shards]
        + [jax.ShapeDtypeStruct((4, SW_ROWS, 1024), F32)],
        in_specs=[ANY] * (n + 1), out_specs=[ANY] * (n + 1),
        scratch_shapes=[pltpu.SemaphoreType.DMA((6 * n + 3,)), pltpu.SemaphoreType.DMA((6 * n + 3,)),
                        pltpu.SemaphoreType.DMA],
    )(*shards, small)
    return res[:n], res[n]


def _handshake(peers):
    barrier = pltpu.get_barrier_semaphore()
    for peer in peers:
        pl.semaphore_signal(barrier, inc=1, device_id=peer, device_id_type=MESH)
    pl.semaphore_wait(barrier, len(peers))


def gather_weights_beside(shards):
    n = len(shards)
    halves = [t.shape[0] // 2 for t in shards]

    def body(*refs):
        s_refs, o_refs, ssem, rsem = refs[:n], refs[n:2 * n], refs[2 * n], refs[2 * n + 1]
        x, y, c = _place()
        me = 2 * x + y
        chips = [(1 - x, y), (x, 1 - y), (1 - x, 1 - y)]
        _handshake([(px, py, c) for px, py in chips] + [(x, y, 1 - c)])

        def half(k, s, hh):
            return o_refs[k].at[s, pl.ds(hh * halves[k], halves[k]), :]

        sends = []
        for k in range(n):
            for j, (px, py) in enumerate(chips):
                sends.append(_rcopy(ssem, rsem, 6 * k + j, s_refs[k].at[pl.ds(c * halves[k], halves[k]), :],
                                    half(k, me, c), (px, py, c)))
        for cp in sends:
            cp.start()
        for k in range(n):
            for j, (px, py) in enumerate(chips):
                s = 2 * px + py
                _rcopy(ssem, rsem, 6 * k + j, half(k, s, c), half(k, s, c), (x, y, c)).wait_recv()
                fwd = _rcopy(ssem, rsem, 6 * k + 3 + j, half(k, s, c), half(k, s, c), (x, y, 1 - c))
                fwd.start()
                sends.append(fwd)
        for k in range(n):
            for j, (px, py) in enumerate(chips):
                s = 2 * px + py
                _rcopy(ssem, rsem, 6 * k + 3 + j, half(k, s, 1 - c), half(k, s, 1 - c), (x, y, c)).wait_recv()
        for cp in sends:
            cp.wait_send()

    return pl.kernel(
        body, name="gather_weights_beside",
        out_type=[jax.ShapeDtypeStruct((4,) + t.shape, t.dtype) for t in shards],
        mesh=plsc.ScalarSubcoreMesh(axis_name="sequencer", num_cores=1),
        scratch_types=[pltpu.SemaphoreType.DMA((6 * n,)), pltpu.SemaphoreType.DMA((6 * n,))],
        compiler_params=pltpu.CompilerParams(collective_id=1),
    )(*shards)


def swap_halves(gs, *, name):
    n = len(gs)

    def body(*refs):
        g_refs, o_refs, ssem, rsem = refs[:n], refs[n:2 * n], refs[2 * n], refs[2 * n + 1]
        x, y, c = _place()
        cps = []
        for k in range(n):
            hk = g_refs[k].shape[1] // 2
            cps.append(_rcopy(ssem, rsem, k, g_refs[k].at[:, pl.ds((1 - c) * hk, hk), :], o_refs[k], (x, y, 1 - c)))
        for cp in cps:
            cp.start()
        for cp in cps:
            cp.wait()

    return pl.pallas_call(
        body, name=name, interpret=False,
        out_shape=[jax.ShapeDtypeStruct((4, t.shape[1] // 2, t.shape[2]), t.dtype) for t in gs],
        in_specs=[ANY] * n, out_specs=[ANY] * n,
        scratch_shapes=[pltpu.SemaphoreType.DMA((n,)), pltpu.SemaphoreType.DMA((n,))],
    )(*gs)


def _sum_rows(hk):
    return _pick(hk, (512, 352, 256, 128))


def pair_sum(g, other, c_idx, *, name):
    _, hk, width = other.shape
    tr = _sum_rows(hk)
    nbk = hk // tr

    def body(c_ref, g_ref, o_ref, out_ref):
        out_ref[...] = (g_ref[...].astype(F32) + o_ref[...].astype(F32)).astype(BF16)

    return pl.pallas_call(
        body, name=name, interpret=False,
        out_shape=jax.ShapeDtypeStruct((4, hk, width), BF16),
        grid_spec=pltpu.PrefetchScalarGridSpec(
            num_scalar_prefetch=1, grid=(4, nbk),
            in_specs=[pl.BlockSpec((1, tr, width), lambda s, i, c_ref: (s, c_ref[0] * nbk + i, 0)),
                      pl.BlockSpec((1, tr, width), lambda s, i, c_ref: (s, i, 0))],
            out_specs=pl.BlockSpec((1, tr, width), lambda s, i, c_ref: (s, i, 0))),
        compiler_params=_params(("parallel", "parallel")),
    )(c_idx, g, other)


def chip_sum(p, got, idx, *, name):
    _, hk, width = got.shape
    tr = _sum_rows(hk)
    nbk = hk // tr

    def body(idx_ref, p_ref, g_ref, out_ref):
        acc = p_ref[0].astype(F32)
        for j in range(3):
            acc = acc + g_ref[j].astype(F32)
        out_ref[0] = acc

    return pl.pallas_call(
        body, name=name, interpret=False,
        out_shape=jax.ShapeDtypeStruct((2, hk, width), F32),
        grid_spec=pltpu.PrefetchScalarGridSpec(
            num_scalar_prefetch=1, grid=(nbk,),
            in_specs=[pl.BlockSpec((1, tr, width), lambda i, idx_ref: (idx_ref[0], i, 0)),
                      pl.BlockSpec((3, tr, width), lambda i, idx_ref: (0, i, 0))],
            out_specs=pl.BlockSpec((1, tr, width), lambda i, idx_ref: (idx_ref[1], i, 0))),
        compiler_params=_params(("parallel",)),
    )(idx, p, got)


def join_halves(qs):
    n = len(qs)

    def body(*refs):
        q_refs, o_refs, ssem, rsem = refs[:n], refs[n:2 * n], refs[2 * n], refs[2 * n + 1]
        x, y, c = _place()
        cps = [_rcopy(ssem, rsem, k, q_refs[k].at[c], o_refs[k].at[c], (x, y, 1 - c)) for k in range(n)]
        for cp in cps:
            cp.start()
        for k in range(n):
            _rcopy(ssem, rsem, k, q_refs[k].at[c], o_refs[k].at[1 - c], (x, y, 1 - c)).wait_recv()
        for cp in cps:
            cp.wait_send()

    return pl.pallas_call(
        body, name="join_halves", interpret=False,
        out_shape=[jax.ShapeDtypeStruct(t.shape, t.dtype) for t in qs],
        in_specs=[ANY] * n, out_specs=[ANY] * n, input_output_aliases={k: k for k in range(n)},
        scratch_shapes=[pltpu.SemaphoreType.DMA((n,)), pltpu.SemaphoreType.DMA((n,))],
    )(*qs)


def scatter_chips_beside(ps, cid, name):
    n = len(ps)

    def body(*refs):
        p_refs, o_refs, ssem, rsem = refs[:n], refs[n:2 * n], refs[2 * n], refs[2 * n + 1]
        x, y, c = _place()
        chips = [(1 - x, y), (x, 1 - y), (1 - x, 1 - y)]
        _handshake([(px, py, c) for px, py in chips])
        cps = [_rcopy(ssem, rsem, 3 * k + j, p_refs[k].at[2 * px + py], o_refs[k].at[j], (px, py, c))
               for k in range(n) for j, (px, py) in enumerate(chips)]
        for cp in cps:
            cp.start()
        for cp in cps:
            cp.wait()

    return pl.kernel(
        body, name=name, out_type=[jax.ShapeDtypeStruct((3,) + t.shape[1:], t.dtype) for t in ps],
        mesh=plsc.ScalarSubcoreMesh(axis_name="sequencer", num_cores=1),
        scratch_types=[pltpu.SemaphoreType.DMA((3 * n,)), pltpu.SemaphoreType.DMA((3 * n,))],
        compiler_params=pltpu.CompilerParams(collective_id=cid),
    )(*ps)


def reduce_begin(gs, names, c_idx, cid, tag):
    others = swap_halves(gs, name=f"swap_halves_{tag}")
    pairs = [pair_sum(g, o, c_idx, name=f"pair_sum_{nm}") for g, o, nm in zip(gs, others, names)]
    return pairs, scatter_chips_beside(pairs, cid, f"scatter_chips_{tag}")


def reduce_end(pairs, gots, names, idx):
    mine = [chip_sum(p, g, idx, name=f"chip_sum_{nm}") for p, g, nm in zip(pairs, gots, names)]
    return [q.reshape(2 * q.shape[1], q.shape[2]) for q in join_halves(mine)]


def gather_small(v):
    def body(v_ref, o_ref, ssem, rsem, lsem):
        x, y, c = _place()
        peers = []
        for k in range(1, 8):
            fx, fy, fc = (k >> 2) & 1, (k >> 1) & 1, k & 1
            peers.append((1 - x if fx else x, 1 - y if fy else y, 1 - c if fc else c))
        _handshake(peers)
        loc = pltpu.make_async_copy(v_ref, o_ref.at[4 * x + 2 * y + c], lsem)
        loc.start()
        cps = []
        for k, (px, py, pc) in enumerate(peers):
            cps.append((pltpu.make_async_remote_copy(
                src_ref=v_ref, dst_ref=o_ref.at[4 * x + 2 * y + c], send_sem=ssem.at[k], recv_sem=rsem.at[k],
                device_id=(px, py, pc), device_id_type=MESH), 4 * px + 2 * py + pc))
        for cp, _ in cps:
            cp.start()
        for k, (cp, peer) in enumerate(cps):
            pltpu.make_async_remote_copy(
                src_ref=v_ref, dst_ref=o_ref.at[peer], send_sem=ssem.at[k], recv_sem=rsem.at[k],
                device_id=(x, y, c), device_id_type=MESH).wait_recv()
        for cp, _ in cps:
            cp.wait_send()
        loc.wait()

    return pl.kernel(
        body, name="gather_small", out_type=jax.ShapeDtypeStruct((8, SV_ROWS, 1024), F32),
        mesh=plsc.ScalarSubcoreMesh(axis_name="sequencer", num_cores=1),
        scratch_types=[pltpu.SemaphoreType.DMA((7,)), pltpu.SemaphoreType.DMA((7,)), pltpu.SemaphoreType.DMA],
        compiler_params=pltpu.CompilerParams(collective_id=6),
    )(v)


def sum_slots(a):
    def fn(i, t):
        acc = t[0]
        for k in range(1, 8):
            acc = acc + t[k]
        return acc

    return rowwise(fn, [whole(a)], [((SV_ROWS, 1024), F32, (SV_ROWS, 1024), lambda i: (0, 0), "w")], steps=1,
                   name="sum_slots")[0]


def _head_rms(x, nw):
    xs, rs = [], []
    for h in range(DN_H):
        xh = x[:, h * DN_D:(h + 1) * DN_D]
        r = lax.rsqrt(jnp.mean(xh * xh, axis=1, keepdims=True) + EPS)
        xs.append(xh * r)
        rs.append(r)
    return xs, rs


def bg_fwd(p, alog, dtb):
    rows = p.shape[0]
    tr = _pick(rows, (384, 128))

    def fn(i, x, al, dt):
        lane = lax.broadcasted_iota(jnp.int32, x.shape, 1)
        row = i + lax.broadcasted_iota(jnp.int32, x.shape, 0)
        g = -jnp.exp(al) * _softplus(x + dt)
        out = jnp.where(lane < 4, _sigmoid(x), jnp.where(lane < 8, g, 0.0))
        return jnp.where(row >= PAD, out, 0.0)

    return rowwise(fn, [cols(p, tr, 128, BG0 // 128), whole(alog), whole(dtb)], [out2d(rows, 128, F32, tr)],
                   steps=rows // tr, name="bg_fwd")[0]


def bg_bwd(p, alog, dtb, dbg):
    rows = p.shape[0]
    tr = _pick(rows, (384, 128))

    def fn(i, x, al, dt, g_in):
        lane = lax.broadcasted_iota(jnp.int32, x.shape, 1)
        row = i + lax.broadcasted_iota(jnp.int32, x.shape, 0)
        live = row >= PAD
        is_b = jnp.logical_and(live, lane < 4)
        is_g = jnp.logical_and(live, jnp.logical_and(lane >= 4, lane < 8))
        beta = _sigmoid(x)
        ea = jnp.exp(al)
        g = -ea * _softplus(x + dt)
        dalpha = jnp.where(is_g, g_in * (-ea) * _sigmoid(x + dt), 0.0)
        dx = jnp.where(is_b, g_in * beta * (1.0 - beta), dalpha)
        dal = jnp.sum(jnp.where(is_g, g_in * g, 0.0), axis=0, keepdims=True)
        return jnp.concatenate([dx, jnp.zeros(x.shape, F32)], axis=1), dal, jnp.sum(dalpha, axis=0, keepdims=True)

    return rowwise(fn, [cols(p, tr, 128, BG0 // 128), whole(alog), whole(dtb), cols(dbg, tr)],
                   [out2d(rows, 256, BF16, tr)], steps=rows // tr, name="bg_bwd",
                   accs=[((1, 128), F32), ((1, 128), F32)])


def dn_qkv_post(j, y):
    xs = _silu(y)
    sc = jnp.where(j == 0, DN_D ** -0.5, 1.0)
    outs = []
    for h in range(DN_H):
        xh = xs[:, h * DN_D:(h + 1) * DN_D]
        r = lax.rsqrt(jnp.sum(xh * xh, axis=1, keepdims=True) + EPS)
        outs.append(jnp.where(j < 2, xh * r * sc, xh))
    return jnp.concatenate(outs, axis=1), y


def dn_qkv_bwd(cq, dq, dk, dv):
    rows = cq.shape[0]
    tr = _pick(rows, (384, 128))

    def fn(i, c0, c1, c2, g0, g1, g2):
        pieces = []
        for kind, (cv, g) in enumerate(((c0, g0), (c1, g1), (c2, g2))):
            xs = _silu(cv)
            if kind < 2:
                sc = DN_D ** -0.5 if kind == 0 else 1.0
                ds = []
                for h in range(DN_H):
                    sl = slice(h * DN_D, (h + 1) * DN_D)
                    xh, gh = xs[:, sl], g[:, sl]
                    r = lax.rsqrt(jnp.sum(xh * xh, axis=1, keepdims=True) + EPS)
                    xn = xh * r
                    ds.append(sc * r * (gh - xn * jnp.sum(gh * xn, axis=1, keepdims=True)))
                dxs = jnp.concatenate(ds, axis=1)
            else:
                dxs = g
            pieces.append(dxs * _dsilu(cv))
        return jnp.concatenate(pieces, axis=1)

    ins = [cols(cq, tr, DN_DIM, k) for k in range(3)] + [cols(t, tr) for t in (dq, dk, dv)]
    return rowwise(fn, ins, [out2d(rows, 3 * DN_DIM, F32, tr)], steps=rows // tr, name="dn_qkv_bwd")[0]


def dn_out_fwd(o, p, nw):
    rows = o.shape[0]
    tr = _pick(rows, (384, 128))

    def fn(i, ov, z, w):
        xs, _ = _head_rms(ov, w)
        return jnp.concatenate(xs, axis=1) * jnp.concatenate([w] * DN_H, axis=1) * _silu(z)

    return rowwise(fn, [cols(o, tr), cols(p, tr, DN_DIM, 6), whole(nw)], [out2d(rows, DN_DIM, BF16, tr)],
                   steps=rows // tr, name="dn_out_fwd")[0]


def dn_out_bwd(o, p, nw, dymix):
    rows = o.shape[0]
    tr = _pick(rows, (384, 128))

    def fn(i, ov, z, w, dy):
        xs, rs = _head_rms(ov, w)
        sz = _silu(z)
        dn = dy * sz
        dos, dw = [], jnp.zeros((1, DN_D), F32)
        for h in range(DN_H):
            sl = slice(h * DN_D, (h + 1) * DN_D)
            gw = dn[:, sl] * w
            dos.append(rs[h] * (gw - xs[h] * jnp.mean(gw * xs[h], axis=1, keepdims=True)))
            dw = dw + jnp.sum(dn[:, sl] * xs[h], axis=0, keepdims=True)
        n = jnp.concatenate(xs, axis=1) * jnp.concatenate([w] * DN_H, axis=1)
        return jnp.concatenate(dos, axis=1), dy * n * _dsilu(z), dw

    return rowwise(fn, [cols(o, tr), cols(p, tr, DN_DIM, 6), whole(nw), cols(dymix, tr, DN_DIM, 1)],
                   [out2d(rows, DN_DIM, F32, tr), out2d(rows, DN_DIM, BF16, tr)], steps=rows // tr,
                   name="dn_out_bwd", accs=[((1, DN_D), F32)])


def conv_a_pre_bwd(dymix, cv, p):
    rows = cv.shape[0]
    tr = _pick(rows, (384, 128))

    def fn(i, dy, c, go):
        return dy * c, dy * go

    return rowwise(fn, [cols(dymix, tr, D_CONV, 0), cols(cv, tr), cols(p, tr, D_CONV, 1)],
                   [out2d(rows, D_CONV, BF16, tr), out2d(rows, D_CONV, F32, tr)], steps=rows // tr,
                   name="conv_a_pre_bwd")


def _rows8(w):
    return jnp.pad(w.astype(F32), ((0, 8 - w.shape[0]), (0, 0)))


def _lanes(v, at):
    return jnp.pad(v.astype(F32), (at, 128 - at - v.shape[0]))[None]


def add_norm(a, w, h, next_nw, *, name):
    if next_nw is None:
        return mm(a, w, add=h, name=name), None
    return mm(a, w, name=name, epi=_add_norm_epi, epi_ins=[(h, lambda j: 0)], epi_consts=[next_nw],
              epi_outs=[F32, BF16])


def _add_norm_epi(row0, t, h, nw):
    x = t + h
    return x, x * lax.rsqrt(jnp.mean(x * x, axis=1, keepdims=True) + EPS) * nw


def ffn_up_conv(hn, w_up, cw8, *, name):
    rows = hn.shape[0]
    tn = w_up.shape[2]
    tm = _pick(rows, (384, 128))
    nr = rows // tm

    def body(x_ref, wg_ref, wv_ref, w_ref, ug_ref, uv_ref, gc_ref, a_ref, carry, scr):
        i = pl.program_id(1)
        x = x_ref[...]
        gate = _dot(x, wg_ref[...])
        val = _dot(x, wv_ref[...])
        ug_ref[...] = gate.astype(BF16)
        uv_ref[...] = val.astype(BF16)
        scr[0:8, :] = jnp.where(i > 0, carry[...], 0.0)
        scr[8:8 + tm, :] = gate
        carry[...] = gate[tm - 8:tm]
        y = jnp.zeros((tm, tn), F32)
        for q in range(3):
            sh = 2 - q
            y = y + w_ref[q:q + 1, :] * scr[8 - sh:8 - sh + tm, :]
        gc_ref[...] = y.astype(BF16)
        a_ref[...] = (_silu(y) * val).astype(BF16)

    half = pl.BlockSpec((tm, tn), lambda j, i: (i, j))
    return pl.pallas_call(
        body, name=name, interpret=False,
        out_shape=[jax.ShapeDtypeStruct((rows, D_FF), BF16)] * 4,
        grid=(D_FF // tn, nr),
        in_specs=[pl.BlockSpec((tm, D), lambda j, i: (i, 0)),
                  pl.BlockSpec((None, D, tn), lambda j, i: (j, 0, 0)),
                  pl.BlockSpec((None, D, tn), lambda j, i: (j + D_FF // tn, 0, 0)),
                  pl.BlockSpec((8, tn), lambda j, i: (0, j))],
        out_specs=[half] * 4,
        scratch_shapes=[pltpu.VMEM((8, tn), F32), pltpu.VMEM((tm + 8, tn), F32)],
        compiler_params=_params(("arbitrary", "arbitrary")),
    )(hn, w_up, w_up, cw8)


def ffn_down_bwd(dh, w_down, gc, uv, ug, cw8, *, name):
    rows = dh.shape[0]
    tn = D_FF // 2
    tm = _pick(rows, (384, 128))
    nr = rows // tm
    r8 = tm // 8

    def body(dh_ref, w_ref, gc_ref, uv_ref, ug_ref, halo_ref, cw_ref, du_ref, dw_ref, carry, gscr, xscr):
        ip = pl.program_id(1)
        i = nr - 1 - ip
        da = _dot(dh_ref[...].astype(BF16), w_ref[...], 1, 1)
        c, val = gc_ref[...].astype(F32), uv_ref[...].astype(F32)
        dgc = da * val * _dsilu(c)
        du_ref[:, tn:] = (da * _silu(c)).astype(BF16)
        gscr[0:tm, :] = dgc
        gscr[tm:tm + 8, :] = jnp.where(ip > 0, carry[...], 0.0)
        carry[...] = dgc[0:8]
        xscr[0:8, :] = jnp.where(i > 0, halo_ref[...].astype(F32), 0.0)
        xscr[8:8 + tm, :] = ug_ref[...].astype(F32)
        dx = jnp.zeros((tm, tn), F32)
        dws = []
        for q in range(3):
            sh = 2 - q
            dx = dx + cw_ref[q:q + 1, :] * gscr[sh:sh + tm, :]
            dws.append(jnp.sum(dgc * xscr[8 - sh:8 - sh + tm, :], axis=0, keepdims=True))
        du_ref[:, :tn] = dx.astype(BF16)

        @pl.when(ip == 0)
        def _():
            dw_ref[...] = jnp.zeros((8, tn), F32)

        dw_ref[...] += jnp.concatenate(dws + [jnp.zeros((5, tn), F32)], axis=0)

    rev = lambda ip: nr - 1 - ip
    tile = lambda arr: pl.BlockSpec((tm, tn), lambda j, ip: (rev(ip), j))
    return pl.pallas_call(
        body, name=name, interpret=False,
        out_shape=[jax.ShapeDtypeStruct((rows, 2 * D_FF), BF16), jax.ShapeDtypeStruct((8, D_FF), F32)],
        grid=(2, nr),
        in_specs=[pl.BlockSpec((tm, D), lambda j, ip: (rev(ip), 0)),
                  pl.BlockSpec((tn, D), lambda j, ip: (j, 0)),
                  tile(gc), tile(uv), tile(ug),
                  pl.BlockSpec((8, tn), lambda j, ip: (jnp.maximum(rev(ip) * r8 - 1, 0), j)),
                  pl.BlockSpec((8, tn), lambda j, ip: (0, j))],
        out_specs=[pl.BlockSpec((tm, 2 * tn), lambda j, ip: (rev(ip), j)),
                   pl.BlockSpec((8, tn), lambda j, ip: (0, j))],
        scratch_shapes=[pltpu.VMEM((8, tn), F32), pltpu.VMEM((tm + 8, tn), F32), pltpu.VMEM((tm + 8, tn), F32)],
        compiler_params=_params(("arbitrary", "arbitrary")),
    )(dh, w_down, gc, uv, ug, ug, cw8)


def ffn_fwd(h, hn, w_up, cw8, w_down, tag, next_nw):
    ug, uv, gc, a = ffn_up_conv(hn, w_up, cw8, name=f"ffn{tag}_up")
    out, hn_next = add_norm(a, w_down, h, next_nw, name=f"ffn{tag}_down")
    return out, hn_next, (hn, ug, uv, a, gc)


def ffn_bwd(h, nw, w_up, cw8, w_down, saved, dh, tag):
    hn, ug, uv, a, gc = saved
    du, d_cw = ffn_down_bwd(dh, w_down, gc, uv, ug, cw8, name=f"ffn{tag}_down_dx")
    d_w_down = mm(a, dh, ta=True, out_dtype=BF16, name=f"ffn{tag}_down_dw")
    dh_new, d_nw = dx_rms_bwd(du, w_up, h, nw, dh, name=f"ffn{tag}_up_dx", b_chip=True, swap_mid=True)
    d_w_up = mm(hn, du, ta=True, out_dtype=BF16, out_chip=True, swap_mid=True, name=f"ffn{tag}_up_dw")
    return dh_new, d_nw, d_w_up, d_cw, d_w_down


def mixer_fwd(h, nw, w_in, ca8, dc8, alog, dtb, dnw, w_out, tie=None, next_nw=None):
    rows = h.shape[0]
    tr = _pick(rows, (384, 128))
    hn = rms_fwd(h, nw, name="mix_norm")
    p = mm(hn, w_in, name="mix_in")
    y_a, cv = conv_fwd([(p, 0), (p, 2)], ca8, 3, rows=rows, c=D_CONV, tc=D_CONV, tr=tr, name="conv_a",
                       pre=lambda gi, ah: gi * ah, post=lambda j, y, go: (go * y, y), extras=[(p, 1)],
                       outs=[BF16, F32])
    qkv_n, cq = conv_fwd([(p, 3)], dc8, 4, rows=rows, c=3 * DN_DIM, tc=DN_DIM, tr=tr, name="dn_conv",
                         post=dn_qkv_post, outs=[F32, F32], strip=tr)
    bgcol = bg_fwd(p, alog, dtb)
    if tie is not None:
        bgcol = tie(bgcol)
    bgrow = bgcol[:, :8].reshape(rows // CH, CH, 8).transpose(0, 2, 1)
    o, s_all, ti_all = dn_fwd(qkv_n, bgcol, bgrow)
    y_b = dn_out_fwd(o, p, dnw)
    ymix = jnp.concatenate([y_a, y_b], axis=1)
    w_out = w_out() if callable(w_out) else w_out
    out, hn_next = add_norm(ymix, w_out, h, next_nw, name="mix_out")
    return out, hn_next, (hn, p, cv, qkv_n, cq, bgcol, bgrow, o, s_all, ti_all, ymix)


def mixer_bwd(h, nw, w_in, ca8, dc8, alog, dtb, dnw, w_out, saved, dh):
    hn, p, cv, qkv_n, cq, bgcol, bgrow, o, s_all, ti_all, ymix = saved
    rows = h.shape[0]
    tr = _pick(rows, (384, 128))
    dymix = mm(dh, w_out, tb=True, name="mix_out_dx")
    d_w_out = mm(ymix, dh, ta=True, out_dtype=BF16, name="mix_out_dw")
    do, dz, d_dnw = dn_out_bwd(o, p, dnw, dymix)
    dq, dk, dv, dbg = dn_bwd(qkv_n, bgcol, bgrow, s_all, ti_all, do)
    dbg_p, d_alog, d_dtb = bg_bwd(p, alog, dtb, dbg)
    dcq = dn_qkv_bwd(cq, dq, dk, dv)
    dqkv, d_dc = conv_bwd([(p, 3)], dc8, 4, dcq, rows=rows, c=3 * DN_DIM, tc=DN_DIM, tr=tr, name="dn_conv_bwd",
                          post=lambda dx: dx, outs=[BF16])
    dgo, dcv = conv_a_pre_bwd(dymix, cv, p)
    dgi, dah, d_ca = conv_bwd([(p, 0), (p, 2)], ca8, 3, dcv, rows=rows, c=D_CONV, tc=D_CONV, tr=tr,
                              name="conv_a_bwd", pre=lambda gi, ah: gi * ah,
                              post=lambda dm, gi, ah: (dm * ah, dm * gi), extras=[(p, 0), (p, 2)], outs=[BF16, BF16])
    dp = jnp.concatenate([dgi, dgo, dah, dqkv, dz, dbg_p], axis=1)
    dh_new, d_nw = dx_rms_bwd(dp, w_in, h, nw, dh, name="mix_in_dx")
    d_w_in = mm(hn, dp, ta=True, out_dtype=BF16, name="mix_in_dw")
    return dh_new, d_nw, d_w_in, d_ca, d_dc, d_alog, d_dtb, d_dnw, d_w_out


def swa_layer_fwd(h, hn, wqkv, qw, kw, sinks, wo, next_nw):
    qkv = mm(hn, wqkv, name="swa_qkv")
    qh, kh, vh = qknorm_fwd(qkv, qw, kw)
    att = swa_fwd(qh, kh, vh, sinks)
    out, hn_next = add_norm(att, wo, h, next_nw, name="swa_out")
    return out, hn_next, (hn, qkv, qh, kh, vh, att)


def swa_layer_bwd(h, nw, wqkv, qw, kw, sinks, wo, saved, dh):
    hn, qkv, qh, kh, vh, att = saved
    datt = mm(dh, wo, tb=True, out_dtype=BF16, name="swa_out_dx")
    d_wo = mm(att, dh, ta=True, out_dtype=BF16, name="swa_out_dw")
    dqh, dkh, dvh, dsk = swa_bwd(qh, kh, vh, sinks, datt)
    dqkv, d_qw, d_kw = qknorm_bwd(qkv, qw, kw, dqh, dkh, dvh)
    dh_new, d_nw = dx_rms_bwd(dqkv, wqkv, h, nw, dh, name="swa_qkv_dx")
    d_wqkv = mm(hn, dqkv, ta=True, out_dtype=BF16, name="swa_qkv_dw")
    d_sinks = jnp.sum(dsk[:, :, 0], axis=0)
    return dh_new, d_nw, d_wqkv, d_qw, d_kw, d_sinks, d_wo


BIG = ("mix_w_in", "mix_w_out", "swa_wq", "swa_wk", "swa_wv", "swa_wo", "ffn_w_up", "ffn_w_down")


def _flat_pad(parts, rows):
    v = jnp.concatenate([t.astype(F32).reshape(-1) for t in parts])
    return jnp.pad(v, (0, rows * 1024 - v.shape[0])).reshape(rows, 1024)


def _split_flat(flat, shapes):
    v = flat.reshape(-1)
    out, o = [], 0
    for s in shapes:
        n = 1
        for d_ in s:
            n *= d_
        out.append(v[o:o + n].reshape(s))
        o += n
    return out


def local_step(x0, target0, meta_full, anw, fnw, w_in, ca8, dc8, alog, dtb, dnw, qw, kw, sinks, fc8, late,
               begin=None, tie=None):
    begin = begin or (lambda tag, names, grads: None)
    h0 = jnp.concatenate([jnp.zeros((PAD, D), F32), meta_full, x0], axis=0)
    h1, hn1, s_mix = mixer_fwd(h0, anw[0], w_in, ca8, dc8, alog, dtb, dnw, lambda: late()[0], tie, fnw[0])
    w_out, wqkv, wo, w_up, w_down = late()
    h2, hn2, s_f0 = ffn_fwd(h1, hn1, w_up[0], fc8[0], w_down[0], 0, anw[1])
    h3, hn3, s_swa = swa_layer_fwd(h2, hn2, wqkv, qw, kw, sinks, wo, fnw[1])
    h4, _, s_f1 = ffn_fwd(h3, hn3, w_up[1], fc8[1], w_down[1], 1, None)
    dh, loss_l = loss_grad(h4, target0)
    dh, d_fnw1, d_up1, d_fc1, d_down1 = ffn_bwd(h3, fnw[1], w_up[1], fc8[1], w_down[1], s_f1, dh, 1)
    begin("ffn1", ("up1", "down1"), [d_up1, d_down1.reshape(4, 704, D)])
    dh, d_anw1, d_wqkv, d_qw, d_kw, d_sinks, d_wo = swa_layer_bwd(h2, anw[1], wqkv, qw, kw, sinks, wo, s_swa, dh)
    begin("swa", ("wq", "wk", "wv", "wo"),
          [d_wqkv[:, :D].reshape(4, 256, D), d_wqkv[:, D:D + 256].reshape(4, 256, 256),
           d_wqkv[:, D + 256:].reshape(4, 256, 256), d_wo.reshape(4, 256, D)])
    dh, d_fnw0, d_up0, d_fc0, d_down0 = ffn_bwd(h1, fnw[0], w_up[0], fc8[0], w_down[0], s_f0, dh, 0)
    begin("ffn0", ("up0", "down0"), [d_up0, d_down0.reshape(4, 704, D)])
    dh, d_anw0, d_w_in, d_ca, d_dc, d_alog, d_dtb, d_dnw, d_w_out = mixer_bwd(
        h0, anw[0], w_in, ca8, dc8, alog, dtb, dnw, w_out, s_mix, dh)
    begin("mix", ("w_in", "w_out"),
          [d_w_in[:, :IN_DIM].reshape(D, 4, 898).transpose(1, 0, 2), d_w_out.reshape(4, 256, D)])
    return (dh, loss_l, d_anw0, d_anw1, d_fnw0, d_fnw1, d_w_in, d_ca, d_dc, d_alog, d_dtb, d_dnw, d_w_out, d_wqkv,
            d_qw, d_kw, d_sinks, d_wo, d_up0, d_up1, d_fc0, d_fc1, d_down0, d_down1)


def kernel(x, meta_tokens, attn_norm_w, ffn_norm_w, mix_w_in, conv_a_w, dn_conv_w, dn_a_log, dn_dt_bias, dn_norm_w, mix_w_out, swa_wq, swa_wk, swa_wv, swa_q_norm_w, swa_k_norm_w, swa_sinks, swa_wo, ffn_w_up, ffn_conv_w, ffn_w_down, loss_target, m_meta_tokens, m_attn_norm_w, m_ffn_norm_w, m_mix_w_in, m_conv_a_w, m_dn_conv_w, m_dn_a_log, m_dn_dt_bias, m_dn_norm_w, m_mix_w_out, m_swa_wq, m_swa_wk, m_swa_wv, m_swa_q_norm_w, m_swa_k_norm_w, m_swa_sinks, m_swa_wo, m_ffn_w_up, m_ffn_conv_w, m_ffn_w_down, v_meta_tokens, v_attn_norm_w, v_ffn_norm_w, v_mix_w_in, v_conv_a_w, v_dn_conv_w, v_dn_a_log, v_dn_dt_bias, v_dn_norm_w, v_mix_w_out, v_swa_wq, v_swa_wk, v_swa_wv, v_swa_q_norm_w, v_swa_k_norm_w, v_swa_sinks, v_swa_wo, v_ffn_w_up, v_ffn_conv_w, v_ffn_w_down):
    ix, iy, ic = lax.axis_index("x"), lax.axis_index("y"), lax.axis_index("c")
    chip = 2 * ix + iy
    seq = x.shape[1]
    rows = HEAD0 + seq

    small_sharded = (conv_a_w, dn_conv_w, ffn_conv_w, meta_tokens)
    up_b, down_b = ffn_w_up.astype(BF16), ffn_w_down.astype(BF16)
    own = [mix_w_in[0].astype(BF16), mix_w_out[0].astype(BF16), swa_wq[0].astype(BF16), swa_wk[0].astype(BF16),
           swa_wv[0].astype(BF16), swa_wo[0].astype(BF16), up_b[0], up_b[1], down_b[0], down_b[1]]
    fill = lambda gathered, mine: [lax.dynamic_update_slice_in_dim(g, t[None], chip, axis=0)
                                   for g, t in zip(gathered, mine)]
    first, g_small = gather_weights(own[:1], _flat_pad(small_sharded, SW_ROWS))
    g_in, = fill(first, own[:1])
    w_in = jnp.pad(g_in.transpose(1, 0, 2).reshape(D, IN_DIM), ((0, 0), (0, P_W - IN_DIM)))
    rest = {}

    def tie(t):
        t, *mine = lax.optimization_barrier((t, *own[1:]))
        rest["w"] = fill(gather_weights_beside(mine), mine)
        return t

    def late():
        g_out, g_q, g_k, g_v, g_o, g_up0, g_up1, g_dn0, g_dn1 = rest["w"]
        wqkv = jnp.concatenate([g_q.reshape(D, D), g_k.reshape(D, 256), g_v.reshape(D, 256)], axis=1)
        return (g_out.reshape(D, D), wqkv, g_o.reshape(D, D), [g_up0, g_up1],
                [g_dn0.reshape(D_FF, D), g_dn1.reshape(D_FF, D)])

    gs = g_small.reshape(4, -1)
    ca_full = gs[:, 0:384].reshape(4, 3, 128).transpose(1, 0, 2).reshape(3, D_CONV)
    dc_full = gs[:, 384:1920].reshape(4, 4, 384).transpose(1, 0, 2).reshape(4, 3 * DN_DIM)
    fc_full = gs[:, 1920:6144].reshape(4, 2, 3, 704).transpose(1, 2, 0, 3).reshape(2, 3, D_FF)
    meta_full = gs[:, 6144:10240].reshape(4, N_META, 256).transpose(1, 0, 2).reshape(N_META, D)
    ca8, dc8 = _rows8(ca_full), _rows8(dc_full)
    fc8 = [_rows8(fc_full[0]), _rows8(fc_full[1])]
    alog, dtb = _lanes(dn_a_log[0], 4), _lanes(dn_dt_bias[0], 4)
    dnw = dn_norm_w.astype(F32)
    qw, kw = swa_q_norm_w.astype(F32), swa_k_norm_w.astype(F32)
    sinks = swa_sinks[0].astype(F32)
    anw = [attn_norm_w[0:1], attn_norm_w[1:2]]
    fnw = [ffn_norm_w[0:1], ffn_norm_w[1:2]]

    c_idx = jnp.reshape(ic, (1,)).astype(jnp.int32)
    chip_idx = jnp.stack([chip, ic]).astype(jnp.int32)
    begun = []

    def begin(tag, names, grads):
        pairs, gots = reduce_begin(grads, names, c_idx, 2 + len(begun), tag)
        begun.append((names, pairs, gots))

    (dh, loss_l, d_anw0, d_anw1, d_fnw0, d_fnw1, d_w_in, d_ca, d_dc, d_alog, d_dtb, d_dnw, d_w_out, d_wqkv, d_qw,
     d_kw, d_sinks, d_wo, d_up0, d_up1, d_fc0, d_fc1, d_down0, d_down1) = local_step(
        x[0], loss_target[0], meta_full, anw, fnw, w_in, ca8, dc8, alog, dtb, dnw, qw, kw, sinks, fc8, late,
        begin, tie)
    grad_x = dh[HEAD0:][None]

    small_parts = [jnp.concatenate([d_anw0, d_anw1], axis=0), jnp.concatenate([d_fnw0, d_fnw1], axis=0),
                   d_alog[0, 4:8], d_dtb[0, 4:8], d_dnw, d_qw, d_kw, d_sinks,
                   d_ca[:3], d_dc[:4], jnp.stack([d_fc0[:3], d_fc1[:3]]), dh[PAD:HEAD0], loss_l[0, 0:1]]
    small_shapes = [(2, D), (2, D), (1, 4), (1, 4), (1, DN_D), (1, SWA_D), (1, SWA_D), (1, SWA_H),
                    (1, 3, D_CONV), (1, 4, 3 * DN_DIM), (2, 3, D_FF), (N_META, D), ()]
    gathered_small = gather_small(_flat_pad(small_parts, SV_ROWS))

    red_big = {}
    for part in (begun[:-1], begun[-1:]):
        part_names = [n for names, _, _ in part for n in names]
        red_big.update(zip(part_names, reduce_end([p for _, ps, _ in part for p in ps],
                                                  [g for _, _, gs_ in part for g in gs_], part_names, chip_idx)))
    g_w_in, g_w_out, g_wq, g_wk, g_wv, g_wo, g_up0, g_up1, g_dn0, g_dn1 = [
        red_big[n] for n in ("w_in", "w_out", "wq", "wk", "wv", "wo", "up0", "up1", "down0", "down1")]

    grads = dict(mix_w_in=g_w_in, mix_w_out=g_w_out, swa_wq=g_wq, swa_wk=g_wk, swa_wv=g_wv, swa_wo=g_wo,
                 ffn_w_up=[g_up0, g_up1], ffn_w_down=[g_dn0, g_dn1])
    weights = dict(meta_tokens=meta_tokens, attn_norm_w=attn_norm_w, ffn_norm_w=ffn_norm_w, mix_w_in=mix_w_in,
                   conv_a_w=conv_a_w, dn_conv_w=dn_conv_w, dn_a_log=dn_a_log, dn_dt_bias=dn_dt_bias,
                   dn_norm_w=dn_norm_w, mix_w_out=mix_w_out, swa_wq=swa_wq, swa_wk=swa_wk, swa_wv=swa_wv,
                   swa_q_norm_w=swa_q_norm_w, swa_k_norm_w=swa_k_norm_w, swa_sinks=swa_sinks, swa_wo=swa_wo,
                   ffn_w_up=ffn_w_up, ffn_conv_w=ffn_conv_w, ffn_w_down=ffn_w_down)
    m_in = dict(meta_tokens=m_meta_tokens, attn_norm_w=m_attn_norm_w, ffn_norm_w=m_ffn_norm_w, mix_w_in=m_mix_w_in,
                conv_a_w=m_conv_a_w, dn_conv_w=m_dn_conv_w, dn_a_log=m_dn_a_log, dn_dt_bias=m_dn_dt_bias,
                dn_norm_w=m_dn_norm_w, mix_w_out=m_mix_w_out, swa_wq=m_swa_wq, swa_wk=m_swa_wk, swa_wv=m_swa_wv,
                swa_q_norm_w=m_swa_q_norm_w, swa_k_norm_w=m_swa_k_norm_w, swa_sinks=m_swa_sinks, swa_wo=m_swa_wo,
                ffn_w_up=m_ffn_w_up, ffn_conv_w=m_ffn_conv_w, ffn_w_down=m_ffn_w_down)
    v_in = dict(meta_tokens=v_meta_tokens, attn_norm_w=v_attn_norm_w, ffn_norm_w=v_ffn_norm_w, mix_w_in=v_mix_w_in,
                conv_a_w=v_conv_a_w, dn_conv_w=v_dn_conv_w, dn_a_log=v_dn_a_log, dn_dt_bias=v_dn_dt_bias,
                dn_norm_w=v_dn_norm_w, mix_w_out=v_mix_w_out, swa_wq=v_swa_wq, swa_wk=v_swa_wk, swa_wv=v_swa_wv,
                swa_q_norm_w=v_swa_q_norm_w, swa_k_norm_w=v_swa_k_norm_w, swa_sinks=v_swa_sinks, swa_wo=v_swa_wo,
                ffn_w_up=v_ffn_w_up, ffn_conv_w=v_ffn_conv_w, ffn_w_down=v_ffn_w_down)
    names = list(weights)
    small = [n for n in names if n not in BIG]
    delta, new_m, new_v = {}, {}, {}
    for n in BIG:
        delta[n], new_m[n], new_v[n], grads[n] = adamw(weights[n], grads[n], m_in[n], v_in[n], name=f"adamw_{n}")
    gathered_small, _ = lax.optimization_barrier((gathered_small, new_v["ffn_w_down"]))
    (g_anw, g_fnw, g_alog, g_dtb, g_dnw, g_qw, g_kw, g_sinks, g_ca_f, g_dc_f, g_fc_f, g_meta_f,
     loss) = _split_flat(sum_slots(gathered_small), small_shapes)
    grads.update(meta_tokens=lax.dynamic_slice_in_dim(g_meta_f, chip * 256, 256, axis=1), attn_norm_w=g_anw,
                 ffn_norm_w=g_fnw, conv_a_w=lax.dynamic_slice_in_dim(g_ca_f, chip * 128, 128, axis=2),
                 dn_conv_w=lax.dynamic_slice_in_dim(g_dc_f, chip * 384, 384, axis=2), dn_a_log=g_alog,
                 dn_dt_bias=g_dtb, dn_norm_w=g_dnw, swa_q_norm_w=g_qw, swa_k_norm_w=g_kw, swa_sinks=g_sinks,
                 ffn_conv_w=lax.dynamic_slice_in_dim(g_fc_f, chip * 704, 704, axis=2))
    grads = {n: grads[n].reshape(weights[n].shape) for n in names}
    shapes = [weights[n].shape for n in small]
    packed = [_flat_pad([t[n] for n in small], SW_ROWS) for t in (weights, grads, m_in, v_in)]
    for store, flat in zip((delta, new_m, new_v), adamw(*packed, name="adamw_small")):
        for n, t in zip(small, _split_flat(flat, shapes)):
            store[n] = t
    return (loss, grad_x, *[grads[n] for n in names], *[delta[n] for n in names],
            *[new_m[n] for n in names], *[new_v[n] for n in names])
```

```python
import functools

import jax
import jax.numpy as jnp
from jax import lax
from jax.experimental import pallas as pl
from jax.experimental.pallas import tpu as pltpu
from jax.experimental.pallas import tpu_sc as plsc

F32 = jnp.float32
BF16 = jnp.bfloat16
HI = lax.Precision.HIGHEST
MESH = pl.DeviceIdType.MESH

D = 1024
N_META = 16
PAD = 112
HEAD0 = PAD + N_META
D_CONV = 512
DN_H = 4
DN_D = 128
DN_DIM = 512
CH = 64
IN_DIM = 3592
P_W = 3840
BG0 = 3584
SWA_H = 16
SWA_KV = 4
SWA_D = 64
BLK = 128
NKEY = N_META + 2 * BLK
D_FF = 2816
EPS = 1e-6
LR, B1, B2, AEPS, WD, STEP = 0.001, 0.9, 0.999, 1e-08, 0.01, 10
VMEM_LIMIT = 48 * 1024 * 1024
MM_VMEM_BUDGET = 34 * 1024 * 1024
R_BIG = 6144
R_HALF = R_BIG // 2
SV_ROWS = 48
SW_ROWS = 16


def _pick(n, cands):
    for c in cands:
        if n % c == 0:
            return c
    return n


def _params(sem=None):
    return pltpu.CompilerParams(dimension_semantics=sem, vmem_limit_bytes=VMEM_LIMIT)


def _dot(a, b, ca=1, cb=0, prec=None):
    return lax.dot_general(a, b, (((ca,), (cb,)), ((), ())), precision=prec,
                           preferred_element_type=F32)


def _sigmoid(x):
    return 1.0 / (1.0 + jnp.exp(-x))


def _silu(x):
    return x * _sigmoid(x)


def _dsilu(x):
    s = _sigmoid(x)
    return s * (1.0 + x * (1.0 - s))


def _softplus(x):
    return jnp.maximum(x, 0.0) + jnp.log(1.0 + jnp.exp(-jnp.abs(x)))


def mm(a, b, *, name, ta=False, tb=False, out_dtype=F32, add=None, tm=None, tn=None, tk=None,
       b_chip=False, out_chip=False, swap_mid=False, epi=None, epi_ins=(), epi_consts=(), epi_outs=(), epi_accs=()):
    if epi is not None:
        return _mm_epi(a, b, name=name, tb=tb, tn=tn, b_chip=b_chip, swap_mid=swap_mid, epi=epi, epi_ins=epi_ins,
                       epi_consts=epi_consts, epi_outs=epi_outs, epi_accs=epi_accs)
    chip_of = _chip_order(swap_mid)
    m, k = (a.shape[1], a.shape[0]) if ta else a.shape
    if b_chip:
        n = b.shape[1] if tb else 4 * b.shape[2]
        if tb:
            tk = b.shape[2]
        else:
            tn = b.shape[2]
    else:
        n = b.shape[0] if tb else b.shape[1]
    if out_chip:
        tn = n // 4
    tn = tn or _pick(n, (1408, 1024, 768, 512, 256, 128))
    tk = tk or (_pick(k, (1408, 704, 384, 128)) if ta else _pick(k, (1024, 1408, 768, 512, 128)))
    nk = k // tk
    if tm is None:
        isz = lambda t: jnp.dtype(t.dtype).itemsize
        osz = jnp.dtype(out_dtype).itemsize
        for tm in ((1408, 1024, 512, 384, 256, 128) if ta else (1408, 704, 512, 384, 256, 128)):
            need = 2 * (tm * tk * isz(a) + tk * tn * isz(b) + tm * tn * osz + (tm * tn * 4 if add is not None else 0))
            need += tm * tn * 4 if nk > 1 else 0
            if m % tm == 0 and need <= MM_VMEM_BUDGET:
                break
        else:
            tm = m
    dims = (((0 if ta else 1,), (1 if tb else 0,)), ((), ()))

    def body(*refs):
        if add is None:
            a_ref, b_ref, o_ref, acc_ref = refs
            add_ref = None
        else:
            a_ref, b_ref, add_ref, o_ref, acc_ref = refs
        part = lax.dot_general(a_ref[...].astype(BF16), b_ref[...].astype(BF16), dims,
                               preferred_element_type=F32)

        def finish(total):
            if add_ref is not None:
                total = total + add_ref[...]
            o_ref[...] = total.astype(out_dtype)

        if nk == 1:
            finish(part)
        else:
            kk = pl.program_id(2)

            @pl.when(kk == 0)
            def _():
                acc_ref[...] = part

            @pl.when(kk > 0)
            def _():
                acc_ref[...] += part

            @pl.when(kk == nk - 1)
            def _():
                finish(acc_ref[...])

    a_spec = pl.BlockSpec((tk, tm), lambda i, j, kk: (kk, i)) if ta else pl.BlockSpec((tm, tk), lambda i, j, kk: (i, kk))
    if b_chip and tb:
        b_spec = pl.BlockSpec((None, tn, tk), lambda i, j, kk: (chip_of(kk), j, 0))
    elif b_chip:
        b_spec = pl.BlockSpec((None, tk, tn), lambda i, j, kk: (j, kk, 0))
    elif tb:
        b_spec = pl.BlockSpec((tn, tk), lambda i, j, kk: (j, kk))
    else:
        b_spec = pl.BlockSpec((tk, tn), lambda i, j, kk: (kk, j))
    o_spec = pl.BlockSpec((tm, tn), lambda i, j, kk: (i, j))
    in_specs = [a_spec, b_spec] + ([o_spec] if add is not None else [])
    args = [a, b] + ([add] if add is not None else [])
    out_spec = pl.BlockSpec((None, tm, tn), lambda i, j, kk: (chip_of(j), i, 0)) if out_chip else o_spec
    return pl.pallas_call(
        body, name=name, interpret=False,
        out_shape=jax.ShapeDtypeStruct((4, m, tn) if out_chip else (m, n), out_dtype),
        grid=(m // tm, n // tn, nk), in_specs=in_specs, out_specs=out_spec,
        scratch_shapes=[pltpu.VMEM((tm, tn) if nk > 1 else (8, 128), F32)],
        compiler_params=_params(("parallel", "parallel", "arbitrary")),
    )(*args)


def _chip_order(swap_mid):
    return (lambda k: (k % 2) * 2 + k // 2) if swap_mid else (lambda k: k)


def _mm_epi(a, b, *, name, tb, tn, b_chip, epi, epi_ins, epi_consts, epi_outs, epi_accs, swap_mid=False):
    chip_of = _chip_order(swap_mid)
    m, k = a.shape
    if b_chip:
        n = b.shape[1] if tb else 4 * b.shape[2]
        tk = b.shape[2] if tb else None
        tn = tn if tb else b.shape[2]
    else:
        n = b.shape[0] if tb else b.shape[1]
        tk = None
    tn = tn or _pick(n, (1408, 1024, 768, 512, 256, 128))
    tk = tk or _pick(k, (1024, 1408, 1280, 768, 512, 128))
    nk, nj = k // tk, n // tn
    isz = lambda t: jnp.dtype(t.dtype if hasattr(t, "dtype") else t).itemsize
    outs3 = [t if isinstance(t, tuple) else (t, n, lambda j: j) for t in epi_outs]
    side = sum(isz(t) for t, _ in epi_ins) + sum(isz(dt) for dt, _, _ in outs3)
    for tm in (1408, 704, 512, 384, 256, 128):
        need = 2 * (tm * tk * isz(a) + tk * tn * isz(b) + tm * tn * side) + (tm * tn * 4 if nk > 1 else 0)
        if m % tm == 0 and need <= MM_VMEM_BUDGET:
            break
    else:
        tm = m
    dims = (((1,), (1 if tb else 0,)), ((), ()))
    n_in, n_c, n_out, n_acc = len(epi_ins), len(epi_consts), len(epi_outs), len(epi_accs)

    def body(*refs):
        a_ref, b_ref = refs[:2]
        in_refs = refs[2:2 + n_in + n_c]
        out_refs = refs[2 + n_in + n_c:2 + n_in + n_c + n_out]
        acc_out = refs[2 + n_in + n_c + n_out:2 + n_in + n_c + n_out + n_acc]
        acc_ref = refs[-1]
        i, j, kk = pl.program_id(0), pl.program_id(1), pl.program_id(2)
        part = lax.dot_general(a_ref[...].astype(BF16), b_ref[...].astype(BF16), dims,
                               preferred_element_type=F32)

        def finish(total):
            res = epi(i * tm, total, *[r[...] for r in in_refs])
            if not isinstance(res, (tuple, list)):
                res = (res,)
            for r, v in zip(out_refs, res[:n_out]):
                r[...] = v.astype(r.dtype)
            if n_acc:
                @pl.when(jnp.logical_and(i == 0, j == 0))
                def _():
                    for r in acc_out:
                        r[...] = jnp.zeros(r.shape, r.dtype)

                for r, v in zip(acc_out, res[n_out:]):
                    r[...] += jnp.broadcast_to(v, r.shape).astype(r.dtype)

        if nk == 1:
            finish(part)
        else:
            @pl.when(kk == 0)
            def _():
                acc_ref[...] = part

            @pl.when(kk > 0)
            def _():
                acc_ref[...] += part

            @pl.when(kk == nk - 1)
            def _():
                finish(acc_ref[...])

    a_spec = pl.BlockSpec((tm, tk), lambda i, j, kk: (i, kk))
    if b_chip and tb:
        b_spec = pl.BlockSpec((None, tn, tk), lambda i, j, kk: (chip_of(kk), j, 0))
    elif b_chip:
        b_spec = pl.BlockSpec((None, tk, tn), lambda i, j, kk: (j, kk, 0))
    elif tb:
        b_spec = pl.BlockSpec((tn, tk), lambda i, j, kk: (j, kk))
    else:
        b_spec = pl.BlockSpec((tk, tn), lambda i, j, kk: (kk, j))
    in_specs = [a_spec, b_spec]
    in_specs += [pl.BlockSpec((tm, tn), lambda i, j, kk, col=col: (i, col(j))) for _, col in epi_ins]
    in_specs += [pl.BlockSpec(t.shape, lambda i, j, kk, nd=t.ndim: (0,) * nd) for t in epi_consts]
    out_specs = [pl.BlockSpec((tm, tn), lambda i, j, kk, col=col: (i, col(j))) for _, _, col in outs3]
    out_specs += [pl.BlockSpec(s, lambda i, j, kk, nd=len(s): (0,) * nd) for s, _ in epi_accs]
    out_shape = [jax.ShapeDtypeStruct((m, width), dt) for dt, width, _ in outs3]
    out_shape += [jax.ShapeDtypeStruct(s, dt) for s, dt in epi_accs]
    sem = ("arbitrary", "arbitrary", "arbitrary") if n_acc else ("parallel", "parallel", "arbitrary")
    return pl.pallas_call(
        body, name=name, interpret=False, out_shape=out_shape,
        grid=(m // tm, nj, nk), in_specs=in_specs, out_specs=out_specs,
        scratch_shapes=[pltpu.VMEM((tm, tn) if nk > 1 else (8, 128), F32)],
        compiler_params=_params(sem),
    )(a, b, *[t for t, _ in epi_ins], *epi_consts)


def cols(arr, tr, width=None, cb=0):
    width = width or arr.shape[1]
    return (arr, (tr, width), lambda i: (i, cb), "r2")


def heads(arr, tr):
    return (arr, (arr.shape[0], tr, arr.shape[2]), lambda i: (0, i, 0), "r3")


def whole(arr):
    nd = arr.ndim
    return (arr, arr.shape, lambda i: (0,) * nd, "w")


STRIP = 16


def _rows_of(ref, kind, r0, n):
    if kind == "r2":
        return ref[pl.ds(r0, n), :]
    if kind == "r3":
        return ref[:, pl.ds(r0, n), :]
    return ref[...]


def _set_rows(ref, kind, r0, n, v):
    if kind == "r2":
        ref[pl.ds(r0, n), :] = v.astype(ref.dtype)
    elif kind == "r3":
        ref[:, pl.ds(r0, n), :] = v.astype(ref.dtype)
    else:
        ref[...] = v.astype(ref.dtype)


def rowwise(fn, ins, outs, *, steps, name, accs=(), strip=None):
    n_in, n_out, n_acc = len(ins), len(outs), len(accs)
    kin = [t[3] for t in ins]
    kout = [t[4] for t in outs]
    tr = next((t[1][-2] for t in ins if t[3] != "w"), 0)

    def body(*refs):
        i = pl.program_id(0)
        in_refs, out_refs, acc_refs = refs[:n_in], refs[n_in:n_in + n_out], refs[n_in + n_out:]
        if n_acc:
            @pl.when(i == 0)
            def _():
                for r in acc_refs:
                    r[...] = jnp.zeros(r.shape, r.dtype)

        def run(r0, n):
            res = fn(i * tr + r0, *[_rows_of(r, k, r0, n) for r, k in zip(in_refs, kin)])
            if not isinstance(res, (tuple, list)):
                res = (res,)
            for r, k, v in zip(out_refs, kout, res[:n_out]):
                _set_rows(r, k, r0, n, v)
            for r, v in zip(acc_refs, res[n_out:]):
                r[...] += jnp.broadcast_to(v, r.shape).astype(r.dtype)

        if strip is None or tr <= strip:
            run(0, tr)
        else:
            def step(s, carry):
                run(pl.multiple_of(s * strip, strip), strip)
                return carry
            lax.fori_loop(0, tr // strip, step, 0)

    def zmap(nd):
        return lambda i: (0,) * nd

    in_specs = [pl.BlockSpec(t[1], t[2]) for t in ins]
    out_specs = [pl.BlockSpec(t[2], t[3]) for t in outs]
    out_specs += [pl.BlockSpec(s, zmap(len(s))) for s, _ in accs]
    out_shape = [jax.ShapeDtypeStruct(t[0], t[1]) for t in outs]
    out_shape += [jax.ShapeDtypeStruct(s, d) for s, d in accs]
    res = pl.pallas_call(
        body, name=name, interpret=False, out_shape=out_shape, grid=(steps,),
        in_specs=in_specs, out_specs=out_specs,
        compiler_params=_params(("arbitrary",)),
    )(*[t[0] for t in ins])
    return res


def out2d(rows, width, dtype, tr):
    return ((rows, width), dtype, (tr, width), lambda i: (i, 0), "r2")


def conv_fwd(xs, w8, kw, *, rows, c, tc, tr, name, post, extras=(), outs=(), pre=None, strip=STRIP):
    nx, ne, no = len(xs), len(extras), len(outs)
    nr, nc = rows // tr, c // tc
    r8 = tr // 8
    st = strip

    def body(*refs):
        x_refs = refs[:2 * nx]
        w_ref = refs[2 * nx]
        e_refs = refs[2 * nx + 1:2 * nx + 1 + ne]
        o_refs = refs[2 * nx + 1 + ne:2 * nx + 1 + ne + no]
        scr = refs[-1]
        j, i = pl.program_id(0), pl.program_id(1)
        halo = [x_refs[2 * q + 1][...].astype(F32) for q in range(nx)]
        scr[0:8, :] = jnp.where(i > 0, pre(*halo) if pre else halo[0], 0.0)

        def fill(s, carry):
            r0 = pl.multiple_of(s * st, st)
            cur = [x_refs[2 * q][pl.ds(r0, st), :].astype(F32) for q in range(nx)]
            scr[pl.ds(8 + r0, st), :] = pre(*cur) if pre else cur[0]
            return carry

        def comp(s, carry):
            r0 = pl.multiple_of(s * st, st)
            win = scr[pl.ds(r0, st + 8), :]
            y = jnp.zeros((st, tc), F32)
            for q in range(kw):
                sh = kw - 1 - q
                y = y + w_ref[q:q + 1, :] * win[8 - sh:8 - sh + st]
            res = post(j, y, *[e[pl.ds(r0, st), :] for e in e_refs])
            if not isinstance(res, (tuple, list)):
                res = (res,)
            for r, v in zip(o_refs, res):
                r[pl.ds(r0, st), :] = v.astype(r.dtype)
            return carry

        lax.fori_loop(0, tr // st, fill, 0)
        lax.fori_loop(0, tr // st, comp, 0)

    in_specs, args = [], []
    for arr, cb0 in xs:
        in_specs.append(pl.BlockSpec((tr, tc), lambda j, i, cb0=cb0: (i, cb0 + j)))
        in_specs.append(pl.BlockSpec((8, tc), lambda j, i, cb0=cb0: (jnp.maximum(i * r8 - 1, 0), cb0 + j)))
        args += [arr, arr]
    in_specs.append(pl.BlockSpec((8, tc), lambda j, i: (0, j)))
    args.append(w8)
    for arr, cb0 in extras:
        in_specs.append(pl.BlockSpec((tr, tc), lambda j, i, cb0=cb0: (i, cb0 + j)))
        args.append(arr)
    return pl.pallas_call(
        body, name=name, interpret=False,
        out_shape=[jax.ShapeDtypeStruct((rows, c), dt) for dt in outs],
        grid=(nc, nr), in_specs=in_specs,
        out_specs=[pl.BlockSpec((tr, tc), lambda j, i: (i, j)) for _ in outs],
        scratch_shapes=[pltpu.VMEM((tr + 8, tc), F32)],
        compiler_params=_params(("parallel", "arbitrary")),
    )(*args)


def conv_bwd(xs, w8, kw, dy, *, rows, c, tc, tr, name, post, extras=(), outs=(), pre=None):
    nx, ne, no = len(xs), len(extras), len(outs)
    nr, nc = rows // tr, c // tc
    r8 = tr // 8

    def body(*refs):
        x_refs = refs[:2 * nx]
        w_ref, dy_ref, dyn_ref = refs[2 * nx:2 * nx + 3]
        e_refs = refs[2 * nx + 3:2 * nx + 3 + ne]
        first_out = 2 * nx + 3 + ne
        o_refs = refs[first_out:first_out + no]
        dw_ref = refs[first_out + no]
        xscr, gscr = refs[-2], refs[-1]
        i = pl.program_id(1)
        halo = [x_refs[2 * q + 1][...].astype(F32) for q in range(nx)]
        xscr[0:8, :] = jnp.where(i > 0, pre(*halo) if pre else halo[0], 0.0)
        gscr[tr:tr + 8, :] = jnp.where(i < nr - 1, dyn_ref[...].astype(F32), 0.0)

        def fill(s, carry):
            r0 = pl.multiple_of(s * STRIP, STRIP)
            cur = [x_refs[2 * q][pl.ds(r0, STRIP), :].astype(F32) for q in range(nx)]
            xscr[pl.ds(8 + r0, STRIP), :] = pre(*cur) if pre else cur[0]
            gscr[pl.ds(r0, STRIP), :] = dy_ref[pl.ds(r0, STRIP), :].astype(F32)
            return carry

        def comp(s, dws):
            r0 = pl.multiple_of(s * STRIP, STRIP)
            gwin = gscr[pl.ds(r0, STRIP + 8), :]
            xwin = xscr[pl.ds(r0, STRIP + 8), :]
            g = gwin[0:STRIP]
            dx = jnp.zeros((STRIP, tc), F32)
            new = []
            for q in range(kw):
                sh = kw - 1 - q
                dx = dx + w_ref[q:q + 1, :] * gwin[sh:sh + STRIP]
                part = g * xwin[8 - sh:8 - sh + STRIP]
                new.append(dws[q] + part[0:8] + part[8:16])
            res = post(dx, *[e[pl.ds(r0, STRIP), :] for e in e_refs])
            if not isinstance(res, (tuple, list)):
                res = (res,)
            for r, v in zip(o_refs, res):
                r[pl.ds(r0, STRIP), :] = v.astype(r.dtype)
            return tuple(new)

        lax.fori_loop(0, tr // STRIP, fill, 0)
        dws = lax.fori_loop(0, tr // STRIP, comp, tuple(jnp.zeros((8, tc), F32) for _ in range(kw)))

        @pl.when(i == 0)
        def _():
            dw_ref[...] = jnp.zeros((8, tc), F32)

        dw_ref[...] += jnp.concatenate([jnp.sum(t, axis=0, keepdims=True) for t in dws]
                                       + [jnp.zeros((8 - kw, tc), F32)], axis=0)

    in_specs, args = [], []
    for arr, cb0 in xs:
        in_specs.append(pl.BlockSpec((tr, tc), lambda j, i, cb0=cb0: (i, cb0 + j)))
        in_specs.append(pl.BlockSpec((8, tc), lambda j, i, cb0=cb0: (jnp.maximum(i * r8 - 1, 0), cb0 + j)))
        args += [arr, arr]
    in_specs.append(pl.BlockSpec((8, tc), lambda j, i: (0, j)))
    in_specs.append(pl.BlockSpec((tr, tc), lambda j, i: (i, j)))
    in_specs.append(pl.BlockSpec((8, tc), lambda j, i: (jnp.minimum((i + 1) * r8, nr * r8 - 1), j)))
    args += [w8, dy, dy]
    for arr, cb0 in extras:
        in_specs.append(pl.BlockSpec((tr, tc), lambda j, i, cb0=cb0: (i, cb0 + j)))
        args.append(arr)
    return pl.pallas_call(
        body, name=name, interpret=False,
        out_shape=[jax.ShapeDtypeStruct((rows, c), dt) for dt in outs] + [jax.ShapeDtypeStruct((8, c), F32)],
        grid=(nc, nr), in_specs=in_specs,
        out_specs=[pl.BlockSpec((tr, tc), lambda j, i: (i, j)) for _ in outs] + [pl.BlockSpec((8, tc), lambda j, i: (0, j))],
        scratch_shapes=[pltpu.VMEM((tr + 8, tc), F32), pltpu.VMEM((tr + 8, tc), F32)],
        compiler_params=_params(("parallel", "arbitrary")),
    )(*args)


def rms_fwd(h, w, *, name):
    rows = h.shape[0]
    tr = _pick(rows, (384, 128))

    def fn(i, x, wv):
        r = lax.rsqrt(jnp.mean(x * x, axis=1, keepdims=True) + EPS)
        return x * r * wv

    return rowwise(fn, [cols(h, tr), whole(w)], [out2d(rows, D, BF16, tr)], steps=rows // tr, name=name)[0]


def _rms_bwd_epi(row0, g, x, dr, wv):
    r = lax.rsqrt(jnp.mean(x * x, axis=1, keepdims=True) + EPS)
    xh = x * r
    gw = g * wv
    dx = r * (gw - xh * jnp.mean(gw * xh, axis=1, keepdims=True))
    row = row0 + lax.broadcasted_iota(jnp.int32, (x.shape[0], 1), 0)
    return jnp.where(row >= PAD, dr + dx, 0.0), jnp.sum(g * xh, axis=0, keepdims=True)


def dx_rms_bwd(dy, w, h, nw, dres, *, name, b_chip=False, swap_mid=False):
    return mm(dy, w, tb=True, b_chip=b_chip, swap_mid=swap_mid, tn=D, name=name, epi=_rms_bwd_epi,
              epi_ins=[(h, lambda j: 0), (dres, lambda j: 0)], epi_consts=[nw], epi_outs=[F32],
              epi_accs=[((1, D), F32)])


def _add_loss_epi(row0, t, h, tgt):
    row = row0 + lax.broadcasted_iota(jnp.int32, (t.shape[0], 1), 0)
    diff = jnp.where(row >= HEAD0, t + h - tgt, 0.0)
    part = jnp.sum(jnp.sum(diff * diff, axis=1, keepdims=True), axis=0, keepdims=True)
    return diff * (1.0 / D), part * (0.5 / D)


def add_loss(a, w, h, target, *, name):
    return mm(a, w, name=name, epi=_add_loss_epi, epi_ins=[(h, lambda j: 0), (target, lambda j: 0)],
              epi_outs=[F32], epi_accs=[((1, 128), F32)])


def adamw(w, g, m, v, *, name):
    shape = w.shape
    gs = list(g) if isinstance(g, (list, tuple)) else [g]
    nl = len(gs)
    width = shape[-1]
    rows = w.size // width
    rl = rows // nl
    tr = _pick(rl, (256, 176, 128, 64, 16, 8))
    nr = rl // tr
    if w.ndim == 3 and shape[1] % tr == 0:
        per = shape[1] // tr
        view = lambda t: (t, (None, tr, width), lambda i: (i // per, i % per, 0), "r2")
        out = (shape, F32, (None, tr, width), lambda i: (i // per, i % per, 0), "r2")
    else:
        view = lambda t: cols(t.reshape(rows, width), tr)
        out = out2d(rows, width, F32, tr)

    def fn(i, wv, mv, vv, *gvs):
        gv = gvs[0]
        for layer in range(1, nl):
            gv = jnp.where(i >= layer * rl, gvs[layer], gv)
        mn = B1 * mv + (1.0 - B1) * gv
        vn = B2 * vv + (1.0 - B2) * gv * gv
        mh = mn / (1.0 - B1 ** STEP)
        vh = vn / (1.0 - B2 ** STEP)
        return -LR * (mh / (jnp.sqrt(vh) + AEPS) + WD * wv), mn, vn, gv

    g_ins = [(t.reshape(rl, width), (tr, width), lambda i, layer=layer: (jnp.clip(i - layer * nr, 0, nr - 1), 0), "r2")
             for layer, t in enumerate(gs)]
    res = rowwise(fn, [view(t) for t in (w, m, v)] + g_ins, [out] * 4, steps=rows // tr, name=name)
    return [r.reshape(shape) for r in res]


HB = DN_H * CH
PAIR = 3


def _split(a):
    hi = a.astype(BF16)
    return hi, (a - hi.astype(F32)).astype(BF16)


def _dot1(a, b, ca=1, cb=0):
    return _dot(a.astype(BF16), b.astype(BF16), ca, cb)


def _dot3(a, b, ca=1, cb=0):
    ah, al = _split(a)
    bh, bl = _split(b)
    return _dot(ah, bh, ca, cb) + (_dot(ah, bl, ca, cb) + _dot(al, bh, ca, cb))


def _dot01(m01, b, ca=1, cb=0):
    bh, bl = _split(b)
    m = m01.astype(BF16)
    return _dot(m, bh, ca, cb) + _dot(m, bl, ca, cb)


def _stack(x):
    return jnp.concatenate([x[:, h * DN_D:(h + 1) * DN_D] for h in range(DN_H)], axis=0)


def _unstack(x):
    return jnp.concatenate([x[h * CH:(h + 1) * CH] for h in range(DN_H)], axis=1)


def _tri_inv(mats, blk, eye):
    each = lambda f, *lists: [f(*t) for t in zip(*lists)]
    ad = [jnp.where(blk, a, 0.0) for a in mats]
    lo = each(lambda a, d: a - d, mats, ad)
    a2 = each(_dot3, ad, ad)
    a4 = each(_dot3, a2, a2)
    a8 = each(_dot3, a4, a4)
    dgi = each(lambda d, s: _dot3(eye - d, eye + s), ad, a2)
    dgi = each(lambda p, s: _dot3(p, eye + s), dgi, a4)
    dgi = each(lambda p, s: _dot3(p, eye + s), dgi, a8)
    n = each(_dot3, dgi, lo)
    n2 = each(_dot3, n, n)
    return each(_dot3, each(lambda u, v: _dot3(eye - u, eye + v), n, n2), dgi)


def _dn_masks():
    row = lax.broadcasted_iota(jnp.int32, (HB, HB), 0)
    col = lax.broadcasted_iota(jnp.int32, (HB, HB), 1)
    same = (row // CH) == (col // CH)
    incl = jnp.logical_and(same, row >= col)
    strict = jnp.logical_and(same, row > col)
    upper = jnp.logical_and(same, row <= col)
    blk = (row // 16) == (col // 16)
    eye = (row == col).astype(F32)
    return incl, strict, upper, blk, eye


def _dn_chunk(qv, kv, vv, bc, br, incl, strict):
    r64 = lax.broadcasted_iota(jnp.int32, (CH, CH), 0)
    c64 = lax.broadcasted_iota(jnp.int32, (CH, CH), 1)
    dcol = _dot01((r64 >= c64).astype(F32), bc)
    drow = _dot3(br, (r64 <= c64).astype(F32))
    col = lambda m, l0: jnp.concatenate([m[:, l0 + h:l0 + h + 1] for h in range(DN_H)], axis=0)
    b_c = col(bc, 0)
    d_c = col(dcol, 4)
    d_r = jnp.concatenate([drow[4 + h:5 + h, :] for h in range(DN_H)], axis=1)
    d_last_h = [dcol[CH - 1:CH, 4 + h:5 + h] for h in range(DN_H)]
    d_last = jnp.concatenate([jnp.broadcast_to(t, (CH, 1)) for t in d_last_h], axis=0)
    q, k, v = _stack(qv), _stack(kv), _stack(vv)
    dm = jnp.where(incl, jnp.exp(jnp.where(incl, d_c - d_r, 0.0)), 0.0)
    kk = _dot1(k, k, 1, 1)
    a = jnp.where(strict, b_c * kk * dm, 0.0)
    ed = jnp.exp(d_c)
    rhs = jnp.concatenate([v * b_c, k * (b_c * ed)], axis=1)
    qk = _dot1(q, k, 1, 1) * dm
    ekd = jnp.exp(d_last - d_c)
    gl = [jnp.exp(t) for t in d_last_h]
    return q, k, v, b_c, dm, kk, a, ed, rhs, qk, ekd, gl


def dn_fwd(qkv_n, bgcol, bgrow):
    rows = qkv_n.shape[0]
    nch = rows // CH

    def body(q_ref, k_ref, v_ref, bc_ref, br_ref, o_ref, s_out, ti_out, s_scr, prep, prep_qk, prep_gl):
        n = pl.program_id(0)

        @pl.when(n == 0)
        def _():
            s_scr[...] = jnp.zeros(s_scr.shape, F32)
            prep[...] = jnp.zeros(prep.shape, F32)
            prep_qk[...] = jnp.zeros(prep_qk.shape, F32)
            prep_gl[...] = jnp.zeros(prep_gl.shape, F32)

        live = n > 0
        rows_of = [slice(h * CH, (h + 1) * CH) for h in range(DN_H)]
        s = [s_scr[h] for h in range(DN_H)]
        for c in range(PAIR):
            u, w, qd, kd = prep[c, 0], prep[c, 1], prep[c, 2], prep[c, 3]
            for h in range(DN_H):
                s_out[c, h] = s[h]
            v_new = [u[rs] - _dot1(w[rs], s[h]) for h, rs in enumerate(rows_of)]
            o_state = [_dot1(qd[rs], s[h]) for h, rs in enumerate(rows_of)]
            s = [jnp.where(live, prep_gl[c, h:h + 1, 0:1] * s[h] + _dot1(kd[rs], v_new[h], 0, 0), s[h])
                 for h, rs in enumerate(rows_of)]
            o = jnp.concatenate(o_state, axis=0) + _dot1(prep_qk[c], jnp.concatenate(v_new, axis=0))
            o_ref[c * CH:(c + 1) * CH, :] = _unstack(o)
        for h in range(DN_H):
            s_scr[h] = s[h]

        incl, strict, _, blk, eye = _dn_masks()
        parts = []
        for c in range(PAIR):
            rows_c = slice(c * CH, (c + 1) * CH)
            parts.append(_dn_chunk(q_ref[rows_c, :], k_ref[rows_c, :], v_ref[rows_c, :], bc_ref[rows_c, :],
                                   br_ref[c], incl, strict))
        tinvs = _tri_inv([p[6] for p in parts], blk, eye)
        for c, (q, k, v, b_c, dm, kk, a, ed, rhs, qk_n, ekd, gl) in enumerate(parts):
            tinv = tinvs[c]
            ti_out[c] = tinv
            sol = _dot3(tinv, rhs)
            prep[c, 0] = sol[:, :DN_D]
            prep[c, 1] = sol[:, DN_D:]
            prep[c, 2] = q * ed
            prep[c, 3] = k * ekd
            prep_qk[c] = qk_n
            prep_gl[c] = jnp.concatenate([jnp.broadcast_to(t, (1, 128)) for t in gl]
                                         + [jnp.zeros((8 - DN_H, 128), F32)], axis=0)

    assert nch % PAIR == 0
    npair = nch // PAIR
    last = npair - 1
    return pl.pallas_call(
        body, name="dn_fwd", interpret=False,
        out_shape=[jax.ShapeDtypeStruct((rows, DN_DIM), F32),
                   jax.ShapeDtypeStruct((nch, DN_H, DN_D, DN_D), F32),
                   jax.ShapeDtypeStruct((nch, HB, HB), F32)],
        grid=(npair + 1,),
        in_specs=[pl.BlockSpec((PAIR * CH, DN_DIM), lambda n: (jnp.minimum(n, last), 0)),
                  pl.BlockSpec((PAIR * CH, DN_DIM), lambda n: (jnp.minimum(n, last), 1)),
                  pl.BlockSpec((PAIR * CH, DN_DIM), lambda n: (jnp.minimum(n, last), 2)),
                  pl.BlockSpec((PAIR * CH, 128), lambda n: (jnp.minimum(n, last), 0)),
                  pl.BlockSpec((PAIR, 8, CH), lambda n: (jnp.minimum(n, last), 0, 0))],
        out_specs=[pl.BlockSpec((PAIR * CH, DN_DIM), lambda n: (jnp.maximum(n - 1, 0), 0)),
                   pl.BlockSpec((PAIR, DN_H, DN_D, DN_D), lambda n: (jnp.maximum(n - 1, 0), 0, 0, 0)),
                   pl.BlockSpec((PAIR, HB, HB), lambda n: (jnp.minimum(n, last), 0, 0))],
        scratch_shapes=[pltpu.VMEM((DN_H, DN_D, DN_D), F32), pltpu.VMEM((PAIR, 4, HB, DN_D), F32),
                        pltpu.VMEM((PAIR, HB, HB), F32), pltpu.VMEM((PAIR, 8, 128), F32)],
        compiler_params=_params(("arbitrary",)),
    )(qkv_n, qkv_n, qkv_n, bgcol, bgrow)


def dn_bwd(qkv_n, bgcol, bgrow, s_all, ti_all, do):
    rows = qkv_n.shape[0]
    nch = rows // CH

    def body(q_ref, k_ref, v_ref, bc_ref, br_ref, s_ref, ti_ref, do_ref, dq_ref, dk_ref, dv_ref, dbg_ref, ds_scr):
        n = pl.program_id(0)

        @pl.when(n == 0)
        def _():
            ds_scr[...] = jnp.zeros(ds_scr.shape, F32)

        incl, strict, upper, _, _ = _dn_masks()
        rsum = lambda t: jnp.sum(t, axis=1, keepdims=True)
        rows_of = [slice(h * CH, (h + 1) * CH) for h in range(DN_H)]
        heads_of = lambda f: jnp.concatenate([f(h, rs) for h, rs in enumerate(rows_of)], axis=0)
        cs = []
        for c in reversed(range(PAIR)):
            rc = slice(c * CH, (c + 1) * CH)
            q, k, v, b_c, dm, kk, a, ed, rhs, qk, ekd, gl = _dn_chunk(
                q_ref[rc, :], k_ref[rc, :], v_ref[rc, :], bc_ref[rc, :], br_ref[c], incl, strict)
            cs.append(dict(rc=rc, q=q, k=k, v=v, b_c=b_c, dm=dm, kk=kk, a=a, ed=ed, rhs=rhs, qk=qk, ekd=ekd, gl=gl,
                           tinv=ti_ref[c], g=_stack(do_ref[rc, :]), s=[s_ref[c, h] for h in range(DN_H)]))
        for t in cs:
            t["sol"] = _dot3(t["tinv"], t["rhs"])
        for t in cs:
            t["u"], t["w"] = t["sol"][:, :DN_D], t["sol"][:, DN_D:]
            t["qd"], t["kd"] = t["q"] * t["ed"], t["k"] * t["ekd"]
            t["v_new"] = heads_of(lambda h, rs: t["u"][rs] - _dot1(t["w"][rs], t["s"][h]))
            t["dv0"] = _dot1(t["qk"], t["g"], 0, 0)
            t["ds0"] = [_dot1(t["qd"][rs], t["g"][rs], 0, 0) for rs in rows_of]
            t["dqd"] = heads_of(lambda h, rs: _dot1(t["g"][rs], t["s"][h], 1, 1))
        for t in cs:
            t["dqk"] = _dot1(t["g"], t["v_new"], 1, 1)
        ds = [ds_scr[h] for h in range(DN_H)]
        for t in cs:
            t["ds"] = ds
            t["dv_new"] = t["dv0"] + heads_of(lambda h, rs: _dot1(t["kd"][rs], ds[h]))
            ds = [t["ds0"][h] + t["gl"][h] * ds[h] - _dot1(t["w"][rs], t["dv_new"][rs], 0, 0)
                  for h, rs in enumerate(rows_of)]
        for h in range(DN_H):
            ds_scr[h] = ds[h]
        for t in cs:
            t["dkd"] = heads_of(lambda h, rs: _dot1(t["v_new"][rs], t["ds"][h], 1, 1))
            dw = heads_of(lambda h, rs: -_dot1(t["dv_new"][rs], t["s"][h], 1, 1))
            t["dsol"] = jnp.concatenate([t["dv_new"], dw], axis=1)
        for t in cs:
            t["drhs"] = _dot3(t["tinv"], t["dsol"], 0, 0)
        for t in cs:
            t["da"] = jnp.where(strict, -_dot1(t["drhs"], t["sol"], 1, 1), 0.0)
        rowi = lax.broadcasted_iota(jnp.int32, (CH, 1), 0)
        lane = lax.broadcasted_iota(jnp.int32, (CH, 128), 1)
        for t in cs:
            q, k, v, b_c, dm, ed, da, dqk = t["q"], t["k"], t["v"], t["b_c"], t["dm"], t["ed"], t["da"], t["dqk"]
            drhs_u, drhs_w = t["drhs"][:, :DN_D], t["drhs"][:, DN_D:]
            s2 = rsum(drhs_w * k)
            dbeta = rsum(drhs_u * v) + s2 * ed + rsum(da * t["kk"] * dm)
            dkk = da * b_c * dm
            dqkr = dqk * dm
            mmat = da * t["a"] + dqk * t["qk"]
            tmp = rsum(t["dkd"] * t["kd"])
            dd = (s2 * b_c * ed + rsum(mmat) - _dot3(mmat, jnp.ones((HB, 128), F32), 0, 0)[:, :1]
                  + rsum(t["dqd"] * t["qd"]) - tmp)
            last = []
            for h, rs in enumerate(rows_of):
                dgl = jnp.sum(rsum(t["s"][h] * t["ds"][h]), axis=0, keepdims=True)
                dd_last = jnp.sum(tmp[rs], axis=0, keepdims=True) + dgl * t["gl"][h]
                last.append(jnp.where(rowi == CH - 1, dd_last, 0.0))
            dd = dd + jnp.concatenate(last, axis=0)
            rc = t["rc"]
            dq_ref[rc, :] = _unstack(_dot1(dqkr, k) + t["dqd"] * ed)
            dk_ref[rc, :] = _unstack(drhs_w * (b_c * ed) + _dot1(dkk, k) + _dot1(dkk, k, 0, 0) + _dot1(dqkr, q, 0, 0)
                                     + t["dkd"] * t["ekd"])
            dv_ref[rc, :] = _unstack(drhs_u * b_c)
            dg = _dot01(upper.astype(F32), jnp.broadcast_to(dd, (HB, 128)))[:, :1]
            out = jnp.zeros((CH, 128), F32)
            for h, rs in enumerate(rows_of):
                out = out + jnp.where(lane == h, dbeta[rs], 0.0) + jnp.where(lane == 4 + h, dg[rs], 0.0)
            dbg_ref[rc, :] = out

    assert nch % PAIR == 0
    npair = nch // PAIR
    rev = lambda n: npair - 1 - n
    blk = PAIR * CH
    return pl.pallas_call(
        body, name="dn_bwd", interpret=False,
        out_shape=[jax.ShapeDtypeStruct((rows, DN_DIM), F32)] * 3 + [jax.ShapeDtypeStruct((rows, 128), F32)],
        grid=(npair,),
        in_specs=[pl.BlockSpec((blk, DN_DIM), lambda n: (rev(n), 0)),
                  pl.BlockSpec((blk, DN_DIM), lambda n: (rev(n), 1)),
                  pl.BlockSpec((blk, DN_DIM), lambda n: (rev(n), 2)),
                  pl.BlockSpec((blk, 128), lambda n: (rev(n), 0)),
                  pl.BlockSpec((PAIR, 8, CH), lambda n: (rev(n), 0, 0)),
                  pl.BlockSpec((PAIR, DN_H, DN_D, DN_D), lambda n: (rev(n), 0, 0, 0)),
                  pl.BlockSpec((PAIR, HB, HB), lambda n: (rev(n), 0, 0)),
                  pl.BlockSpec((blk, DN_DIM), lambda n: (rev(n), 0))],
        out_specs=[pl.BlockSpec((blk, DN_DIM), lambda n: (rev(n), 0))] * 3 + [pl.BlockSpec((blk, 128), lambda n: (rev(n), 0))],
        scratch_shapes=[pltpu.VMEM((DN_H, DN_D, DN_D), F32)],
        compiler_params=_params(("arbitrary",)),
    )(qkv_n, qkv_n, qkv_n, bgcol, bgrow, s_all, ti_all, do)


def _swa_valid(n):
    c3 = lax.broadcasted_iota(jnp.int32, (NKEY, 4 * BLK), 0)
    r = lax.broadcasted_iota(jnp.int32, (NKEY, 4 * BLK), 1) % BLK
    prev0 = N_META + BLK
    c = jnp.where(c3 < N_META, PAD + c3, jnp.where(c3 < prev0, c3 - N_META, c3 - prev0))
    lo = jnp.where(c3 < N_META, 0, jnp.where(c3 < prev0, r + 1 + jnp.where(n >= 2, 0, BLK), 0))
    hi = jnp.where(c3 < N_META, r + jnp.where(n >= 1, BLK, 0),
                   jnp.where(c3 < prev0, BLK, r - jnp.where(n >= 1, 0, BLK)))
    return jnp.logical_and(c >= lo, c <= hi)


def _swa_probs(qs, kcats, valid, sinks):
    s = [jnp.where(valid, _dot(kc, q, 1, 1), -1e30) for q, kc in zip(qs, kcats)]
    m = [jnp.maximum(jnp.max(t, axis=0, keepdims=True), sk) for t, sk in zip(s, sinks)]
    e = [jnp.where(valid, jnp.exp(t - mx), 0.0) for t, mx in zip(s, m)]
    es = [jnp.exp(sk - mx) for sk, mx in zip(sinks, m)]
    inv = [1.0 / (jnp.sum(t, axis=0, keepdims=True) + u) for t, u in zip(e, es)]
    return [t * i for t, i in zip(e, inv)], [u * i for u, i in zip(es, inv)]


def _swa_group(q_ref, sk_ref, h):
    q4 = jnp.concatenate([q_ref[4 * h + g] for g in range(4)], axis=0)
    sink4 = jnp.concatenate([jnp.full((1, BLK), sk_ref[4 * h + g], F32) for g in range(4)], axis=1)
    return q4, sink4


def _swa_specs():
    q = pl.BlockSpec((SWA_H, BLK, SWA_D), lambda n: (0, n, 0))
    km = pl.BlockSpec((SWA_KV, N_META, SWA_D), lambda n: (0, PAD // N_META, 0))
    kp = pl.BlockSpec((SWA_KV, BLK, SWA_D), lambda n: (0, jnp.maximum(n - 1, 0), 0))
    kc = pl.BlockSpec((SWA_KV, BLK, SWA_D), lambda n: (0, n, 0))
    return [q, km, kp, kc, km, kp, kc]


def swa_fwd(qh, kh, vh, sinks):
    rows = qh.shape[1]
    nb = rows // BLK

    def body(q_ref, km, kp, kc, vm, vp, vc, sk_ref, o_ref):
        n = pl.program_id(0)
        valid = _swa_valid(n)
        kcats = [jnp.concatenate([km[h], kp[h], kc[h]], axis=0) for h in range(SWA_KV)]
        vcats = [jnp.concatenate([vm[h], vp[h], vc[h]], axis=0) for h in range(SWA_KV)]
        qs, sinks4 = zip(*[_swa_group(q_ref, sk_ref, h) for h in range(SWA_KV)])
        ps, _ = _swa_probs(qs, kcats, valid, sinks4)
        o4s = [_dot(p.astype(BF16), vc_, 0, 0) for p, vc_ in zip(ps, vcats)]
        o_ref[...] = jnp.concatenate([o4[g * BLK:(g + 1) * BLK] for o4 in o4s for g in range(4)],
                                     axis=1).astype(BF16)

    return pl.pallas_call(
        body, name="swa_fwd", interpret=False,
        out_shape=jax.ShapeDtypeStruct((rows, SWA_H * SWA_D), BF16),
        grid=(nb,),
        in_specs=_swa_specs() + [pl.BlockSpec(memory_space=pltpu.SMEM)],
        out_specs=pl.BlockSpec((BLK, SWA_H * SWA_D), lambda n: (n, 0)),
        compiler_params=_params(("parallel",)),
    )(qh, kh, kh, kh, vh, vh, vh, sinks)


def swa_bwd(qh, kh, vh, sinks, do):
    rows = qh.shape[1]
    nb = rows // BLK

    def body(q_ref, km, kp, kc, vm, vp, vc, do_ref, sk_ref, dq_ref, dk_ref, dv_ref, dsk_ref):
        n = pl.program_id(0)

        @pl.when(n == 0)
        def _():
            dk_ref[...] = jnp.zeros(dk_ref.shape, F32)
            dv_ref[...] = jnp.zeros(dv_ref.shape, F32)

        valid = _swa_valid(n)
        g_all = do_ref[...]
        rowi = lax.broadcasted_iota(jnp.int32, (SWA_H, 128), 0)
        dsk = jnp.zeros((SWA_H, 128), F32)
        pm = pl.multiple_of(jnp.maximum(n - 1, 0) * BLK, BLK)
        pc = pl.multiple_of(n * BLK, BLK)
        hs = range(SWA_KV)
        kcats = [jnp.concatenate([km[h], kp[h], kc[h]], axis=0) for h in hs]
        vcats = [jnp.concatenate([vm[h], vp[h], vc[h]], axis=0) for h in hs]
        qs, sinks4 = zip(*[_swa_group(q_ref, sk_ref, h) for h in hs])
        g4s = [jnp.concatenate([g_all[:, (4 * h + g) * SWA_D:(4 * h + g + 1) * SWA_D] for g in range(4)], axis=0)
               for h in hs]
        ps, pss = _swa_probs(qs, kcats, valid, sinks4)
        dps = [_dot(vc_, g4, 1, 1) for vc_, g4 in zip(vcats, g4s)]
        deltas = [jnp.sum(p * dp, axis=0, keepdims=True) for p, dp in zip(ps, dps)]
        dss = [(p * (dp - dl)).astype(BF16) for p, dp, dl in zip(ps, dps, deltas)]
        dq4s = [_dot(ds, kc_, 0, 0) for ds, kc_ in zip(dss, kcats)]
        dkcs = [_dot(ds, q4) for ds, q4 in zip(dss, qs)]
        dvcs = [_dot(p.astype(BF16), g4) for p, g4 in zip(ps, g4s)]
        for h in hs:
            t = pss[h] * deltas[h]
            for g in range(4):
                dq_ref[4 * h + g] = dq4s[h][g * BLK:(g + 1) * BLK]
                part = -jnp.sum(t[:, g * BLK:(g + 1) * BLK], axis=1, keepdims=True)
                dsk = dsk + jnp.where(rowi == 4 * h + g, part, 0.0)
            lanes = slice(h * SWA_D, (h + 1) * SWA_D)
            for ref, val in ((dk_ref, dkcs[h]), (dv_ref, dvcs[h])):
                ref[PAD:BLK, lanes] += val[0:N_META]
                ref[pl.ds(pm, BLK), lanes] += val[N_META:N_META + BLK]
                ref[pl.ds(pc, BLK), lanes] += val[N_META + BLK:]
        dsk_ref[0] = dsk

    return pl.pallas_call(
        body, name="swa_bwd", interpret=False,
        out_shape=[jax.ShapeDtypeStruct((SWA_H, rows, SWA_D), F32),
                   jax.ShapeDtypeStruct((rows, SWA_KV * SWA_D), F32),
                   jax.ShapeDtypeStruct((rows, SWA_KV * SWA_D), F32),
                   jax.ShapeDtypeStruct((nb, SWA_H, 128), F32)],
        grid=(nb,),
        in_specs=_swa_specs() + [pl.BlockSpec((BLK, SWA_H * SWA_D), lambda n: (n, 0)),
                                 pl.BlockSpec(memory_space=pltpu.SMEM)],
        out_specs=[pl.BlockSpec((SWA_H, BLK, SWA_D), lambda n: (0, n, 0)),
                   pl.BlockSpec((rows, SWA_KV * SWA_D), lambda n: (0, 0)),
                   pl.BlockSpec((rows, SWA_KV * SWA_D), lambda n: (0, 0)),
                   pl.BlockSpec((1, SWA_H, 128), lambda n: (n, 0, 0))],
        compiler_params=_params(("arbitrary",)),
    )(qh, kh, kh, kh, vh, vh, vh, do, sinks)


QK_W = (SWA_H + SWA_KV) * SWA_D


def _head_mean(t):
    r = lax.broadcasted_iota(jnp.int32, (128, 128), 0) // SWA_D
    c = lax.broadcasted_iota(jnp.int32, (128, 128), 1) // SWA_D
    blk = jnp.where(r == c, 1.0 / SWA_D, 0.0).astype(BF16)
    out = []
    for i in range(t.shape[1] // 128):
        hi, lo = _split(t[:, 128 * i:128 * (i + 1)])
        out.append(_dot(hi, blk) + _dot(lo, blk))
    return jnp.concatenate(out, axis=1)


def _qk_scales(qw, kw):
    scale = SWA_D ** -0.5
    wt = jnp.concatenate([jnp.tile(qw.astype(F32) * scale, (1, SWA_H)), jnp.tile(kw.astype(F32), (1, SWA_KV))], axis=1)
    st = jnp.concatenate([jnp.full((1, SWA_H * SWA_D), scale, F32), jnp.ones((1, SWA_KV * SWA_D), F32)], axis=1)
    return wt, st


def qknorm_fwd(qkv, qw, kw):
    rows = qkv.shape[0]
    tr = _pick(rows, (384, 128))
    wt, _ = _qk_scales(qw, kw)

    def fn(i, x, w):
        xq = x[:, :QK_W]
        y = xq * lax.rsqrt(_head_mean(xq * xq) + EPS) * w
        head = lambda t, j: t[:, j * SWA_D:(j + 1) * SWA_D][None]
        qo = jnp.concatenate([head(y, j) for j in range(SWA_H)], axis=0)
        ko = jnp.concatenate([head(y, SWA_H + j) for j in range(SWA_KV)], axis=0)
        vo = jnp.concatenate([head(x, SWA_H + SWA_KV + j) for j in range(SWA_KV)], axis=0)
        return qo, ko, vo

    hm = lambda nh: ((nh, rows, SWA_D), BF16, (nh, tr, SWA_D), lambda i: (0, i, 0), "r3")
    return rowwise(fn, [cols(qkv, tr), whole(wt)], [hm(SWA_H), hm(SWA_KV), hm(SWA_KV)],
                   steps=rows // tr, name="qknorm_fwd")


def qknorm_bwd(qkv, qw, kw, dqh, dk, dv):
    rows = qkv.shape[0]
    tr = _pick(rows, (384, 128))
    wt, st = _qk_scales(qw, kw)

    def fn(i, x, w, sc, dq, dkv, dvv):
        xq = x[:, :QK_W]
        dy = jnp.concatenate([dq[j] for j in range(SWA_H)] + [dkv], axis=1)
        r = lax.rsqrt(_head_mean(xq * xq) + EPS)
        xh = xq * r
        gw = dy * w
        dx = r * (gw - xh * _head_mean(gw * xh))
        return jnp.concatenate([dx, dvv], axis=1), jnp.sum(dy * sc * xh, axis=0, keepdims=True)

    dqkv, dw = rowwise(fn, [cols(qkv, tr), whole(wt), whole(st), heads(dqh, tr), cols(dk, tr), cols(dv, tr)],
                       [out2d(rows, 1536, BF16, tr)], steps=rows // tr, name="qknorm_bwd", accs=[((1, QK_W), F32)])
    dw = dw.reshape(SWA_H + SWA_KV, SWA_D)
    return dqkv, jnp.sum(dw[:SWA_H], axis=0, keepdims=True), jnp.sum(dw[SWA_H:], axis=0, keepdims=True)


def _place():
    return lax.axis_index("x"), lax.axis_index("y"), lax.axis_index("c")


ANY = pl.BlockSpec(memory_space=pl.ANY)


def _rcopy(ssem, rsem, k, src, dst, to):
    return pltpu.make_async_remote_copy(src_ref=src, dst_ref=dst, send_sem=ssem.at[k], recv_sem=rsem.at[k],
                                        device_id=to, device_id_type=MESH)


def gather_weights(shards, small):
    n = len(shards)
    halves = [t.shape[0] // 2 for t in shards]

    def body(*refs):
        s_refs, small_ref = refs[:n], refs[n]
        o_refs, osmall = refs[n + 1:2 * n + 1], refs[2 * n + 1]
        ssem, rsem, lsem = refs[2 * n + 2:]
        x, y, c = _place()
        me = 2 * x + y
        chips = [(1 - x, y), (x, 1 - y), (1 - x, 1 - y)]

        def half(k, s, hh):
            return o_refs[k].at[s, pl.ds(hh * halves[k], halves[k]), :]

        loc = pltpu.make_async_copy(small_ref, osmall.at[me], lsem)
        loc.start()
        sends = []
        for k in range(n):
            for j, (px, py) in enumerate(chips):
                sends.append(_rcopy(ssem, rsem, 6 * k + j, s_refs[k].at[pl.ds(c * halves[k], halves[k]), :],
                                    half(k, me, c), (px, py, c)))
        for j, (px, py) in enumerate(chips):
            sends.append(_rcopy(ssem, rsem, 6 * n + j, small_ref, osmall.at[me], (px, py, c)))
        for cp in sends:
            cp.start()
        for k in range(n):
            for j, (px, py) in enumerate(chips):
                s = 2 * px + py
                _rcopy(ssem, rsem, 6 * k + j, half(k, s, c), half(k, s, c), (x, y, c)).wait_recv()
                fwd = _rcopy(ssem, rsem, 6 * k + 3 + j, half(k, s, c), half(k, s, c), (x, y, 1 - c))
                fwd.start()
                sends.append(fwd)
        for k in range(n):
            for j, (px, py) in enumerate(chips):
                s = 2 * px + py
                _rcopy(ssem, rsem, 6 * k + 3 + j, half(k, s, 1 - c), half(k, s, 1 - c), (x, y, c)).wait_recv()
        for j, (px, py) in enumerate(chips):
            s = 2 * px + py
            _rcopy(ssem, rsem, 6 * n + j, osmall.at[s], osmall.at[s], (x, y, c)).wait_recv()
        for cp in sends:
            cp.wait_send()
        loc.wait()

    res = pl.pallas_call(
        body, name="gather_weights", interpret=False,
        out_shape=[jax.ShapeDtypeStruct((4,) + t.shape, t.dtype) for t in shards]
        + [jax.ShapeDtypeStruct((4, SW_ROWS, 1024), F32)],
        in_specs=[ANY] * (n + 1), out_specs=[ANY] * (n + 1),
        scratch_shapes=[pltpu.SemaphoreType.DMA((6 * n + 3,)), pltpu.SemaphoreType.DMA((6 * n + 3,)),
                        pltpu.SemaphoreType.DMA],
    )(*shards, small)
    return res[:n], res[n]


def _handshake(peers):
    barrier = pltpu.get_barrier_semaphore()
    for peer in peers:
        pl.semaphore_signal(barrier, inc=1, device_id=peer, device_id_type=MESH)
    pl.semaphore_wait(barrier, len(peers))


def gather_weights_beside(shards, cid, name):
    n = len(shards)
    halves = [t.shape[0] // 2 for t in shards]

    def body(*refs):
        s_refs, o_refs, ssem, rsem = refs[:n], refs[n:2 * n], refs[2 * n], refs[2 * n + 1]
        x, y, c = _place()
        me = 2 * x + y
        chips = [(1 - x, y), (x, 1 - y), (1 - x, 1 - y)]
        _handshake([(px, py, c) for px, py in chips] + [(x, y, 1 - c)])

        def half(k, s, hh):
            return o_refs[k].at[s, pl.ds(hh * halves[k], halves[k]), :]

        sends = []
        for k in range(n):
            for j, (px, py) in enumerate(chips):
                sends.append(_rcopy(ssem, rsem, 6 * k + j, s_refs[k].at[pl.ds(c * halves[k], halves[k]), :],
                                    half(k, me, c), (px, py, c)))
        for cp in sends:
            cp.start()
        for k in range(n):
            for j, (px, py) in enumerate(chips):
                s = 2 * px + py
                _rcopy(ssem, rsem, 6 * k + j, half(k, s, c), half(k, s, c), (x, y, c)).wait_recv()
                fwd = _rcopy(ssem, rsem, 6 * k + 3 + j, half(k, s, c), half(k, s, c), (x, y, 1 - c))
                fwd.start()
                sends.append(fwd)
        for k in range(n):
            for j, (px, py) in enumerate(chips):
                s = 2 * px + py
                _rcopy(ssem, rsem, 6 * k + 3 + j, half(k, s, 1 - c), half(k, s, 1 - c), (x, y, c)).wait_recv()
        for cp in sends:
            cp.wait_send()

    return pl.kernel(
        body, name=name,
        out_type=[jax.ShapeDtypeStruct((4,) + t.shape, t.dtype) for t in shards],
        mesh=plsc.ScalarSubcoreMesh(axis_name="sequencer", num_cores=1),
        scratch_types=[pltpu.SemaphoreType.DMA((6 * n,)), pltpu.SemaphoreType.DMA((6 * n,))],
        compiler_params=pltpu.CompilerParams(collective_id=cid),
    )(*shards)


def swap_halves(gs, *, name):
    n = len(gs)

    def body(*refs):
        g_refs, o_refs, ssem, rsem = refs[:n], refs[n:2 * n], refs[2 * n], refs[2 * n + 1]
        x, y, c = _place()
        cps = []
        for k in range(n):
            hk = g_refs[k].shape[1] // 2
            cps.append(_rcopy(ssem, rsem, k, g_refs[k].at[:, pl.ds((1 - c) * hk, hk), :], o_refs[k], (x, y, 1 - c)))
        for cp in cps:
            cp.start()
        for cp in cps:
            cp.wait()

    return pl.pallas_call(
        body, name=name, interpret=False,
        out_shape=[jax.ShapeDtypeStruct((4, t.shape[1] // 2, t.shape[2]), t.dtype) for t in gs],
        in_specs=[ANY] * n, out_specs=[ANY] * n,
        scratch_shapes=[pltpu.SemaphoreType.DMA((n,)), pltpu.SemaphoreType.DMA((n,))],
    )(*gs)


def _sum_rows(hk):
    return _pick(hk, (512, 352, 256, 128))


def pair_sum(g, other, c_idx, *, name):
    _, hk, width = other.shape
    tr = _sum_rows(hk)
    nbk = hk // tr

    def body(c_ref, g_ref, o_ref, out_ref):
        out_ref[...] = (g_ref[...].astype(F32) + o_ref[...].astype(F32)).astype(BF16)

    return pl.pallas_call(
        body, name=name, interpret=False,
        out_shape=jax.ShapeDtypeStruct((4, hk, width), BF16),
        grid_spec=pltpu.PrefetchScalarGridSpec(
            num_scalar_prefetch=1, grid=(4, nbk),
            in_specs=[pl.BlockSpec((1, tr, width), lambda s, i, c_ref: (s, c_ref[0] * nbk + i, 0)),
                      pl.BlockSpec((1, tr, width), lambda s, i, c_ref: (s, i, 0))],
            out_specs=pl.BlockSpec((1, tr, width), lambda s, i, c_ref: (s, i, 0))),
        compiler_params=_params(("parallel", "parallel")),
    )(c_idx, g, other)


def chip_sum(p, got, idx, *, name):
    _, hk, width = got.shape
    tr = _sum_rows(hk)
    nbk = hk // tr

    def body(idx_ref, p_ref, g_ref, out_ref):
        acc = p_ref[0].astype(F32)
        for j in range(3):
            acc = acc + g_ref[j].astype(F32)
        out_ref[0] = acc

    return pl.pallas_call(
        body, name=name, interpret=False,
        out_shape=jax.ShapeDtypeStruct((2, hk, width), F32),
        grid_spec=pltpu.PrefetchScalarGridSpec(
            num_scalar_prefetch=1, grid=(nbk,),
            in_specs=[pl.BlockSpec((1, tr, width), lambda i, idx_ref: (idx_ref[0], i, 0)),
                      pl.BlockSpec((3, tr, width), lambda i, idx_ref: (0, i, 0))],
            out_specs=pl.BlockSpec((1, tr, width), lambda i, idx_ref: (idx_ref[1], i, 0))),
        compiler_params=_params(("parallel",)),
    )(idx, p, got)


def join_halves(qs):
    n = len(qs)

    def body(*refs):
        q_refs, o_refs, ssem, rsem = refs[:n], refs[n:2 * n], refs[2 * n], refs[2 * n + 1]
        x, y, c = _place()
        cps = [_rcopy(ssem, rsem, k, q_refs[k].at[c], o_refs[k].at[c], (x, y, 1 - c)) for k in range(n)]
        for cp in cps:
            cp.start()
        for k in range(n):
            _rcopy(ssem, rsem, k, q_refs[k].at[c], o_refs[k].at[1 - c], (x, y, 1 - c)).wait_recv()
        for cp in cps:
            cp.wait_send()

    return pl.pallas_call(
        body, name="join_halves", interpret=False,
        out_shape=[jax.ShapeDtypeStruct(t.shape, t.dtype) for t in qs],
        in_specs=[ANY] * n, out_specs=[ANY] * n, input_output_aliases={k: k for k in range(n)},
        scratch_shapes=[pltpu.SemaphoreType.DMA((n,)), pltpu.SemaphoreType.DMA((n,))],
    )(*qs)


def scatter_chips_beside(ps, cid, name):
    n = len(ps)

    def body(*refs):
        p_refs, o_refs, ssem, rsem = refs[:n], refs[n:2 * n], refs[2 * n], refs[2 * n + 1]
        x, y, c = _place()
        chips = [(1 - x, y), (x, 1 - y), (1 - x, 1 - y)]
        _handshake([(px, py, c) for px, py in chips])
        cps = [_rcopy(ssem, rsem, 3 * k + j, p_refs[k].at[2 * px + py], o_refs[k].at[j], (px, py, c))
               for k in range(n) for j, (px, py) in enumerate(chips)]
        for cp in cps:
            cp.start()
        for cp in cps:
            cp.wait()

    return pl.kernel(
        body, name=name, out_type=[jax.ShapeDtypeStruct((3,) + t.shape[1:], t.dtype) for t in ps],
        mesh=plsc.ScalarSubcoreMesh(axis_name="sequencer", num_cores=1),
        scratch_types=[pltpu.SemaphoreType.DMA((3 * n,)), pltpu.SemaphoreType.DMA((3 * n,))],
        compiler_params=pltpu.CompilerParams(collective_id=cid),
    )(*ps)


def reduce_begin(gs, names, c_idx, cid, tag):
    others = swap_halves(gs, name=f"swap_halves_{tag}")
    pairs = [pair_sum(g, o, c_idx, name=f"pair_sum_{nm}") for g, o, nm in zip(gs, others, names)]
    return pairs, scatter_chips_beside(pairs, cid, f"scatter_chips_{tag}")


def reduce_end(pairs, gots, names, idx):
    mine = [chip_sum(p, g, idx, name=f"chip_sum_{nm}") for p, g, nm in zip(pairs, gots, names)]
    return [q.reshape(2 * q.shape[1], q.shape[2]) for q in join_halves(mine)]


def gather_small(v):
    def body(v_ref, o_ref, ssem, rsem, lsem):
        x, y, c = _place()
        peers = []
        for k in range(1, 8):
            fx, fy, fc = (k >> 2) & 1, (k >> 1) & 1, k & 1
            peers.append((1 - x if fx else x, 1 - y if fy else y, 1 - c if fc else c))
        _handshake(peers)
        loc = pltpu.make_async_copy(v_ref, o_ref.at[4 * x + 2 * y + c], lsem)
        loc.start()
        cps = []
        for k, (px, py, pc) in enumerate(peers):
            cps.append((pltpu.make_async_remote_copy(
                src_ref=v_ref, dst_ref=o_ref.at[4 * x + 2 * y + c], send_sem=ssem.at[k], recv_sem=rsem.at[k],
                device_id=(px, py, pc), device_id_type=MESH), 4 * px + 2 * py + pc))
        for cp, _ in cps:
            cp.start()
        for k, (cp, peer) in enumerate(cps):
            pltpu.make_async_remote_copy(
                src_ref=v_ref, dst_ref=o_ref.at[peer], send_sem=ssem.at[k], recv_sem=rsem.at[k],
                device_id=(x, y, c), device_id_type=MESH).wait_recv()
        for cp, _ in cps:
            cp.wait_send()
        loc.wait()

    return pl.kernel(
        body, name="gather_small", out_type=jax.ShapeDtypeStruct((8, SV_ROWS, 1024), F32),
        mesh=plsc.ScalarSubcoreMesh(axis_name="sequencer", num_cores=1),
        scratch_types=[pltpu.SemaphoreType.DMA((7,)), pltpu.SemaphoreType.DMA((7,)), pltpu.SemaphoreType.DMA],
        compiler_params=pltpu.CompilerParams(collective_id=6),
    )(v)


def sum_slots(a):
    def fn(i, t):
        acc = t[0]
        for k in range(1, 8):
            acc = acc + t[k]
        return acc

    return rowwise(fn, [whole(a)], [((SV_ROWS, 1024), F32, (SV_ROWS, 1024), lambda i: (0, 0), "w")], steps=1,
                   name="sum_slots")[0]


def _head_rms(x, nw):
    xs, rs = [], []
    for h in range(DN_H):
        xh = x[:, h * DN_D:(h + 1) * DN_D]
        r = lax.rsqrt(jnp.mean(xh * xh, axis=1, keepdims=True) + EPS)
        xs.append(xh * r)
        rs.append(r)
    return xs, rs


def bg_fwd(p, alog, dtb):
    rows = p.shape[0]
    tr = _pick(rows, (384, 128))

    def fn(i, x, al, dt):
        lane = lax.broadcasted_iota(jnp.int32, x.shape, 1)
        row = i + lax.broadcasted_iota(jnp.int32, x.shape, 0)
        g = -jnp.exp(al) * _softplus(x + dt)
        out = jnp.where(lane < 4, _sigmoid(x), jnp.where(lane < 8, g, 0.0))
        return jnp.where(row >= PAD, out, 0.0)

    return rowwise(fn, [cols(p, tr, 128, BG0 // 128), whole(alog), whole(dtb)], [out2d(rows, 128, F32, tr)],
                   steps=rows // tr, name="bg_fwd")[0]


def bg_bwd(p, alog, dtb, dbg):
    rows = p.shape[0]
    tr = _pick(rows, (384, 128))

    def fn(i, x, al, dt, g_in):
        lane = lax.broadcasted_iota(jnp.int32, x.shape, 1)
        row = i + lax.broadcasted_iota(jnp.int32, x.shape, 0)
        live = row >= PAD
        is_b = jnp.logical_and(live, lane < 4)
        is_g = jnp.logical_and(live, jnp.logical_and(lane >= 4, lane < 8))
        beta = _sigmoid(x)
        ea = jnp.exp(al)
        g = -ea * _softplus(x + dt)
        dalpha = jnp.where(is_g, g_in * (-ea) * _sigmoid(x + dt), 0.0)
        dx = jnp.where(is_b, g_in * beta * (1.0 - beta), dalpha)
        dal = jnp.sum(jnp.where(is_g, g_in * g, 0.0), axis=0, keepdims=True)
        return jnp.concatenate([dx, jnp.zeros(x.shape, F32)], axis=1), dal, jnp.sum(dalpha, axis=0, keepdims=True)

    return rowwise(fn, [cols(p, tr, 128, BG0 // 128), whole(alog), whole(dtb), cols(dbg, tr)],
                   [out2d(rows, 256, BF16, tr)], steps=rows // tr, name="bg_bwd",
                   accs=[((1, 128), F32), ((1, 128), F32)])


def dn_qkv_post(j, y):
    xs = _silu(y)
    sc = jnp.where(j == 0, DN_D ** -0.5, 1.0)
    outs = []
    for h in range(DN_H):
        xh = xs[:, h * DN_D:(h + 1) * DN_D]
        r = lax.rsqrt(jnp.sum(xh * xh, axis=1, keepdims=True) + EPS)
        outs.append(jnp.where(j < 2, xh * r * sc, xh))
    return jnp.concatenate(outs, axis=1), y


def dn_qkv_bwd(cq, dq, dk, dv):
    rows = cq.shape[0]
    tr = _pick(rows, (384, 128))

    def fn(i, c0, c1, c2, g0, g1, g2):
        pieces = []
        for kind, (cv, g) in enumerate(((c0, g0), (c1, g1), (c2, g2))):
            xs = _silu(cv)
            if kind < 2:
                sc = DN_D ** -0.5 if kind == 0 else 1.0
                ds = []
                for h in range(DN_H):
                    sl = slice(h * DN_D, (h + 1) * DN_D)
                    xh, gh = xs[:, sl], g[:, sl]
                    r = lax.rsqrt(jnp.sum(xh * xh, axis=1, keepdims=True) + EPS)
                    xn = xh * r
                    ds.append(sc * r * (gh - xn * jnp.sum(gh * xn, axis=1, keepdims=True)))
                dxs = jnp.concatenate(ds, axis=1)
            else:
                dxs = g
            pieces.append(dxs * _dsilu(cv))
        return jnp.concatenate(pieces, axis=1)

    ins = [cols(cq, tr, DN_DIM, k) for k in range(3)] + [cols(t, tr) for t in (dq, dk, dv)]
    return rowwise(fn, ins, [out2d(rows, 3 * DN_DIM, F32, tr)], steps=rows // tr, name="dn_qkv_bwd")[0]


def dn_out_fwd(o, p, nw):
    rows = o.shape[0]
    tr = _pick(rows, (384, 128))

    def fn(i, ov, z, w):
        xs, _ = _head_rms(ov, w)
        return jnp.concatenate(xs, axis=1) * jnp.concatenate([w] * DN_H, axis=1) * _silu(z)

    return rowwise(fn, [cols(o, tr), cols(p, tr, DN_DIM, 6), whole(nw)], [out2d(rows, DN_DIM, BF16, tr)],
                   steps=rows // tr, name="dn_out_fwd")[0]


def dn_out_bwd(o, p, nw, dymix):
    rows = o.shape[0]
    tr = _pick(rows, (384, 128))

    def fn(i, ov, z, w, dy):
        xs, rs = _head_rms(ov, w)
        sz = _silu(z)
        dn = dy * sz
        dos, dw = [], jnp.zeros((1, DN_D), F32)
        for h in range(DN_H):
            sl = slice(h * DN_D, (h + 1) * DN_D)
            gw = dn[:, sl] * w
            dos.append(rs[h] * (gw - xs[h] * jnp.mean(gw * xs[h], axis=1, keepdims=True)))
            dw = dw + jnp.sum(dn[:, sl] * xs[h], axis=0, keepdims=True)
        n = jnp.concatenate(xs, axis=1) * jnp.concatenate([w] * DN_H, axis=1)
        return jnp.concatenate(dos, axis=1), dy * n * _dsilu(z), dw

    return rowwise(fn, [cols(o, tr), cols(p, tr, DN_DIM, 6), whole(nw), cols(dymix, tr, DN_DIM, 1)],
                   [out2d(rows, DN_DIM, F32, tr), out2d(rows, DN_DIM, BF16, tr)], steps=rows // tr,
                   name="dn_out_bwd", accs=[((1, DN_D), F32)])


def conv_a_pre_bwd(dymix, cv, p):
    rows = cv.shape[0]
    tr = _pick(rows, (384, 128))

    def fn(i, dy, c, go):
        return dy * c, dy * go

    return rowwise(fn, [cols(dymix, tr, D_CONV, 0), cols(cv, tr), cols(p, tr, D_CONV, 1)],
                   [out2d(rows, D_CONV, BF16, tr), out2d(rows, D_CONV, F32, tr)], steps=rows // tr,
                   name="conv_a_pre_bwd")


def _rows8(w):
    return jnp.pad(w.astype(F32), ((0, 8 - w.shape[0]), (0, 0)))


def _lanes(v, at):
    return jnp.pad(v.astype(F32), (at, 128 - at - v.shape[0]))[None]


def add_norm(a, w, h, next_nw, *, name):
    return mm(a, w, name=name, epi=_add_norm_epi, epi_ins=[(h, lambda j: 0)], epi_consts=[next_nw],
              epi_outs=[F32, BF16])


def _add_norm_epi(row0, t, h, nw):
    x = t + h
    return x, x * lax.rsqrt(jnp.mean(x * x, axis=1, keepdims=True) + EPS) * nw


def ffn_up_conv(hn, w_up, cw8, *, name):
    rows = hn.shape[0]
    tn = w_up.shape[2]
    tm = _pick(rows, (384, 128))
    nr = rows // tm

    def body(x_ref, wg_ref, wv_ref, w_ref, ug_ref, uv_ref, gc_ref, a_ref, carry, scr):
        i = pl.program_id(1)
        x = x_ref[...]
        gate = _dot(x, wg_ref[...])
        val = _dot(x, wv_ref[...])
        ug_ref[...] = gate.astype(BF16)
        uv_ref[...] = val.astype(BF16)
        scr[0:8, :] = jnp.where(i > 0, carry[...], 0.0)
        scr[8:8 + tm, :] = gate
        carry[...] = gate[tm - 8:tm]
        y = jnp.zeros((tm, tn), F32)
        for q in range(3):
            sh = 2 - q
            y = y + w_ref[q:q + 1, :] * scr[8 - sh:8 - sh + tm, :]
        gc_ref[...] = y.astype(BF16)
        a_ref[...] = (_silu(y) * val).astype(BF16)

    half = pl.BlockSpec((tm, tn), lambda j, i: (i, j))
    return pl.pallas_call(
        body, name=name, interpret=False,
        out_shape=[jax.ShapeDtypeStruct((rows, D_FF), BF16)] * 4,
        grid=(D_FF // tn, nr),
        in_specs=[pl.BlockSpec((tm, D), lambda j, i: (i, 0)),
                  pl.BlockSpec((None, D, tn), lambda j, i: (j, 0, 0)),
                  pl.BlockSpec((None, D, tn), lambda j, i: (j + D_FF // tn, 0, 0)),
                  pl.BlockSpec((8, tn), lambda j, i: (0, j))],
        out_specs=[half] * 4,
        scratch_shapes=[pltpu.VMEM((8, tn), F32), pltpu.VMEM((tm + 8, tn), F32)],
        compiler_params=_params(("arbitrary", "arbitrary")),
    )(hn, w_up, w_up, cw8)


def ffn_down_bwd(dh, w_down, gc, uv, ug, cw8, *, name):
    rows = dh.shape[0]
    tn = D_FF // 2
    tm = _pick(rows, (384, 128))
    nr = rows // tm
    r8 = tm // 8

    def body(dh_ref, w_ref, gc_ref, uv_ref, ug_ref, halo_ref, cw_ref, du_ref, dw_ref, carry, gscr, xscr):
        ip = pl.program_id(1)
        i = nr - 1 - ip
        da = _dot(dh_ref[...].astype(BF16), w_ref[...], 1, 1)
        c, val = gc_ref[...].astype(F32), uv_ref[...].astype(F32)
        dgc = da * val * _dsilu(c)
        du_ref[:, tn:] = (da * _silu(c)).astype(BF16)
        gscr[0:tm, :] = dgc
        gscr[tm:tm + 8, :] = jnp.where(ip > 0, carry[...], 0.0)
        carry[...] = dgc[0:8]
        xscr[0:8, :] = jnp.where(i > 0, halo_ref[...].astype(F32), 0.0)
        xscr[8:8 + tm, :] = ug_ref[...].astype(F32)
        dx = jnp.zeros((tm, tn), F32)
        dws = []
        for q in range(3):
            sh = 2 - q
            dx = dx + cw_ref[q:q + 1, :] * gscr[sh:sh + tm, :]
            dws.append(jnp.sum(dgc * xscr[8 - sh:8 - sh + tm, :], axis=0, keepdims=True))
        du_ref[:, :tn] = dx.astype(BF16)

        @pl.when(ip == 0)
        def _():
            dw_ref[...] = jnp.zeros((8, tn), F32)

        dw_ref[...] += jnp.concatenate(dws + [jnp.zeros((5, tn), F32)], axis=0)

    rev = lambda ip: nr - 1 - ip
    tile = lambda arr: pl.BlockSpec((tm, tn), lambda j, ip: (rev(ip), j))
    return pl.pallas_call(
        body, name=name, interpret=False,
        out_shape=[jax.ShapeDtypeStruct((rows, 2 * D_FF), BF16), jax.ShapeDtypeStruct((8, D_FF), F32)],
        grid=(2, nr),
        in_specs=[pl.BlockSpec((tm, D), lambda j, ip: (rev(ip), 0)),
                  pl.BlockSpec((tn, D), lambda j, ip: (j, 0)),
                  tile(gc), tile(uv), tile(ug),
                  pl.BlockSpec((8, tn), lambda j, ip: (jnp.maximum(rev(ip) * r8 - 1, 0), j)),
                  pl.BlockSpec((8, tn), lambda j, ip: (0, j))],
        out_specs=[pl.BlockSpec((tm, 2 * tn), lambda j, ip: (rev(ip), j)),
                   pl.BlockSpec((8, tn), lambda j, ip: (0, j))],
        scratch_shapes=[pltpu.VMEM((8, tn), F32), pltpu.VMEM((tm + 8, tn), F32), pltpu.VMEM((tm + 8, tn), F32)],
        compiler_params=_params(("arbitrary", "arbitrary")),
    )(dh, w_down, gc, uv, ug, ug, cw8)


def ffn_fwd(h, hn, w_up, cw8, w_down, tag, next_nw=None, target=None):
    ug, uv, gc, a = ffn_up_conv(hn, w_up, cw8, name=f"ffn{tag}_up")
    if target is not None:
        out, hn_next = add_loss(a, w_down, h, target, name=f"ffn{tag}_down")
    else:
        out, hn_next = add_norm(a, w_down, h, next_nw, name=f"ffn{tag}_down")
    return out, hn_next, (hn, ug, uv, a, gc)


def ffn_bwd(h, nw, w_up, cw8, w_down, saved, dh, tag):
    hn, ug, uv, a, gc = saved
    du, d_cw = ffn_down_bwd(dh, w_down, gc, uv, ug, cw8, name=f"ffn{tag}_down_dx")
    d_w_down = mm(a, dh, ta=True, out_dtype=BF16, name=f"ffn{tag}_down_dw")
    dh_new, d_nw = dx_rms_bwd(du, w_up, h, nw, dh, name=f"ffn{tag}_up_dx", b_chip=True, swap_mid=True)
    d_w_up = mm(hn, du, ta=True, out_dtype=BF16, out_chip=True, swap_mid=True, name=f"ffn{tag}_up_dw")
    return dh_new, d_nw, d_w_up, d_cw, d_w_down


def mixer_fwd(h, nw, w_in, ca8, dc8, alog, dtb, dnw, w_out, tie=None, next_nw=None):
    rows = h.shape[0]
    tr = _pick(rows, (384, 128))
    hn = rms_fwd(h, nw, name="mix_norm")
    p = mm(hn, w_in, name="mix_in")
    y_a, cv = conv_fwd([(p, 0), (p, 2)], ca8, 3, rows=rows, c=D_CONV, tc=D_CONV, tr=tr, name="conv_a",
                       pre=lambda gi, ah: gi * ah, post=lambda j, y, go: (go * y, y), extras=[(p, 1)],
                       outs=[BF16, F32])
    qkv_n, cq = conv_fwd([(p, 3)], dc8, 4, rows=rows, c=3 * DN_DIM, tc=DN_DIM, tr=tr, name="dn_conv",
                         post=dn_qkv_post, outs=[F32, F32], strip=tr)
    bgcol = bg_fwd(p, alog, dtb)
    if tie is not None:
        bgcol = tie(bgcol)
    bgrow = bgcol[:, :8].reshape(rows // CH, CH, 8).transpose(0, 2, 1)
    o, s_all, ti_all = dn_fwd(qkv_n, bgcol, bgrow)
    y_b = dn_out_fwd(o, p, dnw)
    ymix = jnp.concatenate([y_a, y_b], axis=1)
    w_out = w_out() if callable(w_out) else w_out
    out, hn_next = add_norm(ymix, w_out, h, next_nw, name="mix_out")
    return out, hn_next, (hn, p, cv, qkv_n, cq, bgcol, bgrow, o, s_all, ti_all, ymix)


def mixer_bwd(h, nw, w_in, ca8, dc8, alog, dtb, dnw, w_out, saved, dh):
    hn, p, cv, qkv_n, cq, bgcol, bgrow, o, s_all, ti_all, ymix = saved
    rows = h.shape[0]
    tr = _pick(rows, (384, 128))
    dymix = mm(dh, w_out, tb=True, name="mix_out_dx")
    d_w_out = mm(ymix, dh, ta=True, out_dtype=BF16, name="mix_out_dw")
    do, dz, d_dnw = dn_out_bwd(o, p, dnw, dymix)
    dq, dk, dv, dbg = dn_bwd(qkv_n, bgcol, bgrow, s_all, ti_all, do)
    dbg_p, d_alog, d_dtb = bg_bwd(p, alog, dtb, dbg)
    dcq = dn_qkv_bwd(cq, dq, dk, dv)
    dqkv, d_dc = conv_bwd([(p, 3)], dc8, 4, dcq, rows=rows, c=3 * DN_DIM, tc=DN_DIM, tr=tr, name="dn_conv_bwd",
                          post=lambda dx: dx, outs=[BF16])
    dgo, dcv = conv_a_pre_bwd(dymix, cv, p)
    dgi, dah, d_ca = conv_bwd([(p, 0), (p, 2)], ca8, 3, dcv, rows=rows, c=D_CONV, tc=D_CONV, tr=tr,
                              name="conv_a_bwd", pre=lambda gi, ah: gi * ah,
                              post=lambda dm, gi, ah: (dm * ah, dm * gi), extras=[(p, 0), (p, 2)], outs=[BF16, BF16])
    dp = jnp.concatenate([dgi, dgo, dah, dqkv, dz, dbg_p], axis=1)
    dh_new, d_nw = dx_rms_bwd(dp, w_in, h, nw, dh, name="mix_in_dx")
    d_w_in = mm(hn, dp, ta=True, out_dtype=BF16, name="mix_in_dw")
    return dh_new, d_nw, d_w_in, d_ca, d_dc, d_alog, d_dtb, d_dnw, d_w_out


def swa_layer_fwd(h, hn, wqkv, qw, kw, sinks, wo, next_nw):
    qkv = mm(hn, wqkv, name="swa_qkv")
    qh, kh, vh = qknorm_fwd(qkv, qw, kw)
    att = swa_fwd(qh, kh, vh, sinks)
    out, hn_next = add_norm(att, wo, h, next_nw, name="swa_out")
    return out, hn_next, (hn, qkv, qh, kh, vh, att)


def swa_layer_bwd(h, nw, wqkv, qw, kw, sinks, wo, saved, dh):
    hn, qkv, qh, kh, vh, att = saved
    datt = mm(dh, wo, tb=True, out_dtype=BF16, name="swa_out_dx")
    d_wo = mm(att, dh, ta=True, out_dtype=BF16, name="swa_out_dw")
    dqh, dkh, dvh, dsk = swa_bwd(qh, kh, vh, sinks, datt)
    dqkv, d_qw, d_kw = qknorm_bwd(qkv, qw, kw, dqh, dkh, dvh)
    dh_new, d_nw = dx_rms_bwd(dqkv, wqkv, h, nw, dh, name="swa_qkv_dx")
    d_wqkv = mm(hn, dqkv, ta=True, out_dtype=BF16, name="swa_qkv_dw")
    d_sinks = jnp.sum(dsk[:, :, 0], axis=0)
    return dh_new, d_nw, d_wqkv, d_qw, d_kw, d_sinks, d_wo


BIG = ("mix_w_in", "mix_w_out", "swa_wq", "swa_wk", "swa_wv", "swa_wo", "ffn_w_up", "ffn_w_down")


def _flat_pad(parts, rows):
    v = jnp.concatenate([t.astype(F32).reshape(-1) for t in parts])
    return jnp.pad(v, (0, rows * 1024 - v.shape[0])).reshape(rows, 1024)


def _split_flat(flat, shapes):
    v = flat.reshape(-1)
    out, o = [], 0
    for s in shapes:
        n = 1
        for d_ in s:
            n *= d_
        out.append(v[o:o + n].reshape(s))
        o += n
    return out


def local_step(x0, target0, meta_full, anw, fnw, w_in, ca8, dc8, alog, dtb, dnw, qw, kw, sinks, fc8, late,
               begin=None, tie=None):
    begin = begin or (lambda tag, names, grads: None)
    h0 = jnp.concatenate([jnp.zeros((PAD, D), F32), meta_full, x0], axis=0)
    h1, hn1, s_mix = mixer_fwd(h0, anw[0], w_in, ca8, dc8, alog, dtb, dnw, lambda: late()[0], tie, fnw[0])
    w_out, wqkv, wo, w_up, w_down = late()
    h2, hn2, s_f0 = ffn_fwd(h1, hn1, w_up[0], fc8[0], w_down[0], 0, anw[1])
    h3, hn3, s_swa = swa_layer_fwd(h2, hn2, wqkv, qw, kw, sinks, wo, fnw[1])
    dh, loss_l, s_f1 = ffn_fwd(h3, hn3, w_up[1], fc8[1], w_down[1], 1,
                               target=jnp.pad(target0, ((HEAD0, 0), (0, 0))))
    dh, d_fnw1, d_up1, d_fc1, d_down1 = ffn_bwd(h3, fnw[1], w_up[1], fc8[1], w_down[1], s_f1, dh, 1)
    begin("ffn1", ("up1", "down1"), [d_up1, d_down1.reshape(4, 704, D)])
    dh, d_anw1, d_wqkv, d_qw, d_kw, d_sinks, d_wo = swa_layer_bwd(h2, anw[1], wqkv, qw, kw, sinks, wo, s_swa, dh)
    begin("swa", ("wq", "wk", "wv", "wo"),
          [d_wqkv[:, :D].reshape(4, 256, D), d_wqkv[:, D:D + 256].reshape(4, 256, 256),
           d_wqkv[:, D + 256:].reshape(4, 256, 256), d_wo.reshape(4, 256, D)])
    dh, d_fnw0, d_up0, d_fc0, d_down0 = ffn_bwd(h1, fnw[0], w_up[0], fc8[0], w_down[0], s_f0, dh, 0)
    begin("ffn0", ("up0", "down0"), [d_up0, d_down0.reshape(4, 704, D)])
    dh, d_anw0, d_w_in, d_ca, d_dc, d_alog, d_dtb, d_dnw, d_w_out = mixer_bwd(
        h0, anw[0], w_in, ca8, dc8, alog, dtb, dnw, w_out, s_mix, dh)
    begin("mix", ("w_in", "w_out"),
          [d_w_in[:, :IN_DIM].reshape(D, 4, 898).transpose(1, 0, 2), d_w_out.reshape(4, 256, D)])
    return (dh, loss_l, d_anw0, d_anw1, d_fnw0, d_fnw1, d_w_in, d_ca, d_dc, d_alog, d_dtb, d_dnw, d_w_out, d_wqkv,
            d_qw, d_kw, d_sinks, d_wo, d_up0, d_up1, d_fc0, d_fc1, d_down0, d_down1)


def kernel(x, meta_tokens, attn_norm_w, ffn_norm_w, mix_w_in, conv_a_w, dn_conv_w, dn_a_log, dn_dt_bias, dn_norm_w, mix_w_out, swa_wq, swa_wk, swa_wv, swa_q_norm_w, swa_k_norm_w, swa_sinks, swa_wo, ffn_w_up, ffn_conv_w, ffn_w_down, loss_target, m_meta_tokens, m_attn_norm_w, m_ffn_norm_w, m_mix_w_in, m_conv_a_w, m_dn_conv_w, m_dn_a_log, m_dn_dt_bias, m_dn_norm_w, m_mix_w_out, m_swa_wq, m_swa_wk, m_swa_wv, m_swa_q_norm_w, m_swa_k_norm_w, m_swa_sinks, m_swa_wo, m_ffn_w_up, m_ffn_conv_w, m_ffn_w_down, v_meta_tokens, v_attn_norm_w, v_ffn_norm_w, v_mix_w_in, v_conv_a_w, v_dn_conv_w, v_dn_a_log, v_dn_dt_bias, v_dn_norm_w, v_mix_w_out, v_swa_wq, v_swa_wk, v_swa_wv, v_swa_q_norm_w, v_swa_k_norm_w, v_swa_sinks, v_swa_wo, v_ffn_w_up, v_ffn_conv_w, v_ffn_w_down):
    ix, iy, ic = lax.axis_index("x"), lax.axis_index("y"), lax.axis_index("c")
    chip = 2 * ix + iy
    seq = x.shape[1]
    rows = HEAD0 + seq

    small_sharded = (conv_a_w, dn_conv_w, ffn_conv_w, meta_tokens)
    up_b, down_b = ffn_w_up.astype(BF16), ffn_w_down.astype(BF16)
    own = [mix_w_in[0].astype(BF16), mix_w_out[0].astype(BF16), swa_wq[0].astype(BF16), swa_wk[0].astype(BF16),
           swa_wv[0].astype(BF16), swa_wo[0].astype(BF16), up_b[0], up_b[1], down_b[0], down_b[1]]
    fill = lambda gathered, mine: [lax.dynamic_update_slice_in_dim(g, t[None], chip, axis=0)
                                   for g, t in zip(gathered, mine)]
    first, g_small = gather_weights(own[:1], _flat_pad(small_sharded, SW_ROWS))
    g_in, = fill(first, own[:1])
    w_in = jnp.pad(g_in.transpose(1, 0, 2).reshape(D, IN_DIM), ((0, 0), (0, P_W - IN_DIM)))
    _, own_out = lax.optimization_barrier((g_in, own[1]))
    rest = {"w_out": fill(gather_weights_beside([own_out], 1, "gather_w_out"), [own_out])}

    def tie(t):
        t, *mine = lax.optimization_barrier((t, *own[2:]))
        g_q, g_k, g_v, g_o, g_up0, g_up1, g_dn0, g_dn1 = mine
        soon, last = [g_up0, g_dn0, g_q, g_k, g_v, g_o], [g_up1, g_dn1]
        rest["soon"] = fill(gather_weights_beside(soon, 7, "gather_layers_12"), soon)
        rest["last"] = fill(gather_weights_beside(last, 8, "gather_layer_3"), last)
        return t

    def late():
        (g_out,), (g_up0, g_dn0, g_q, g_k, g_v, g_o), (g_up1, g_dn1) = rest["w_out"], rest["soon"], rest["last"]
        wqkv = jnp.concatenate([g_q.reshape(D, D), g_k.reshape(D, 256), g_v.reshape(D, 256)], axis=1)
        return (g_out.reshape(D, D), wqkv, g_o.reshape(D, D), [g_up0, g_up1],
                [g_dn0.reshape(D_FF, D), g_dn1.reshape(D_FF, D)])

    gs = g_small.reshape(4, -1)
    ca_full = gs[:, 0:384].reshape(4, 3, 128).transpose(1, 0, 2).reshape(3, D_CONV)
    dc_full = gs[:, 384:1920].reshape(4, 4, 384).transpose(1, 0, 2).reshape(4, 3 * DN_DIM)
    fc_full = gs[:, 1920:6144].reshape(4, 2, 3, 704).transpose(1, 2, 0, 3).reshape(2, 3, D_FF)
    meta_full = gs[:, 6144:10240].reshape(4, N_META, 256).transpose(1, 0, 2).reshape(N_META, D)
    ca8, dc8 = _rows8(ca_full), _rows8(dc_full)
    fc8 = [_rows8(fc_full[0]), _rows8(fc_full[1])]
    alog, dtb = _lanes(dn_a_log[0], 4), _lanes(dn_dt_bias[0], 4)
    dnw = dn_norm_w.astype(F32)
    qw, kw = swa_q_norm_w.astype(F32), swa_k_norm_w.astype(F32)
    sinks = swa_sinks[0].astype(F32)
    anw = [attn_norm_w[0:1], attn_norm_w[1:2]]
    fnw = [ffn_norm_w[0:1], ffn_norm_w[1:2]]

    c_idx = jnp.reshape(ic, (1,)).astype(jnp.int32)
    chip_idx = jnp.stack([chip, ic]).astype(jnp.int32)
    begun = []

    def begin(tag, names, grads):
        pairs, gots = reduce_begin(grads, names, c_idx, 2 + len(begun), tag)
        begun.append((names, pairs, gots))

    (dh, loss_l, d_anw0, d_anw1, d_fnw0, d_fnw1, d_w_in, d_ca, d_dc, d_alog, d_dtb, d_dnw, d_w_out, d_wqkv, d_qw,
     d_kw, d_sinks, d_wo, d_up0, d_up1, d_fc0, d_fc1, d_down0, d_down1) = local_step(
        x[0], loss_target[0], meta_full, anw, fnw, w_in, ca8, dc8, alog, dtb, dnw, qw, kw, sinks, fc8, late,
        begin, tie)
    grad_x = dh[HEAD0:][None]

    small_parts = [jnp.concatenate([d_anw0, d_anw1], axis=0), jnp.concatenate([d_fnw0, d_fnw1], axis=0),
                   d_alog[0, 4:8], d_dtb[0, 4:8], d_dnw, d_qw, d_kw, d_sinks,
                   d_ca[:3], d_dc[:4], jnp.stack([d_fc0[:3], d_fc1[:3]]), dh[PAD:HEAD0], loss_l[0, 0:1]]
    small_shapes = [(2, D), (2, D), (1, 4), (1, 4), (1, DN_D), (1, SWA_D), (1, SWA_D), (1, SWA_H),
                    (1, 3, D_CONV), (1, 4, 3 * DN_DIM), (2, 3, D_FF), (N_META, D), ()]
    gathered_small = gather_small(_flat_pad(small_parts, SV_ROWS))

    red_big = {}
    for part in (begun[:-1], begun[-1:]):
        part_names = [n for names, _, _ in part for n in names]
        red_big.update(zip(part_names, reduce_end([p for _, ps, _ in part for p in ps],
                                                  [g for _, _, gs_ in part for g in gs_], part_names, chip_idx)))
    g_w_in, g_w_out, g_wq, g_wk, g_wv, g_wo, g_up0, g_up1, g_dn0, g_dn1 = [
        red_big[n] for n in ("w_in", "w_out", "wq", "wk", "wv", "wo", "up0", "up1", "down0", "down1")]

    grads = dict(mix_w_in=g_w_in, mix_w_out=g_w_out, swa_wq=g_wq, swa_wk=g_wk, swa_wv=g_wv, swa_wo=g_wo,
                 ffn_w_up=[g_up0, g_up1], ffn_w_down=[g_dn0, g_dn1])
    weights = dict(meta_tokens=meta_tokens, attn_norm_w=attn_norm_w, ffn_norm_w=ffn_norm_w, mix_w_in=mix_w_in,
                   conv_a_w=conv_a_w, dn_conv_w=dn_conv_w, dn_a_log=dn_a_log, dn_dt_bias=dn_dt_bias,
                   dn_norm_w=dn_norm_w, mix_w_out=mix_w_out, swa_wq=swa_wq, swa_wk=swa_wk, swa_wv=swa_wv,
                   swa_q_norm_w=swa_q_norm_w, swa_k_norm_w=swa_k_norm_w, swa_sinks=swa_sinks, swa_wo=swa_wo,
                   ffn_w_up=ffn_w_up, ffn_conv_w=ffn_conv_w, ffn_w_down=ffn_w_down)
    m_in = dict(meta_tokens=m_meta_tokens, attn_norm_w=m_attn_norm_w, ffn_norm_w=m_ffn_norm_w, mix_w_in=m_mix_w_in,
                conv_a_w=m_conv_a_w, dn_conv_w=m_dn_conv_w, dn_a_log=m_dn_a_log, dn_dt_bias=m_dn_dt_bias,
                dn_norm_w=m_dn_norm_w, mix_w_out=m_mix_w_out, swa_wq=m_swa_wq, swa_wk=m_swa_wk, swa_wv=m_swa_wv,
                swa_q_norm_w=m_swa_q_norm_w, swa_k_norm_w=m_swa_k_norm_w, swa_sinks=m_swa_sinks, swa_wo=m_swa_wo,
                ffn_w_up=m_ffn_w_up, ffn_conv_w=m_ffn_conv_w, ffn_w_down=m_ffn_w_down)
    v_in = dict(meta_tokens=v_meta_tokens, attn_norm_w=v_attn_norm_w, ffn_norm_w=v_ffn_norm_w, mix_w_in=v_mix_w_in,
                conv_a_w=v_conv_a_w, dn_conv_w=v_dn_conv_w, dn_a_log=v_dn_a_log, dn_dt_bias=v_dn_dt_bias,
                dn_norm_w=v_dn_norm_w, mix_w_out=v_mix_w_out, swa_wq=v_swa_wq, swa_wk=v_swa_wk, swa_wv=v_swa_wv,
                swa_q_norm_w=v_swa_q_norm_w, swa_k_norm_w=v_swa_k_norm_w, swa_sinks=v_swa_sinks, swa_wo=v_swa_wo,
                ffn_w_up=v_ffn_w_up, ffn_conv_w=v_ffn_conv_w, ffn_w_down=v_ffn_w_down)
    names = list(weights)
    small = [n for n in names if n not in BIG]
    delta, new_m, new_v = {}, {}, {}
    for n in BIG:
        delta[n], new_m[n], new_v[n], grads[n] = adamw(weights[n], grads[n], m_in[n], v_in[n], name=f"adamw_{n}")
    gathered_small, _ = lax.optimization_barrier((gathered_small, new_v["ffn_w_down"]))
    (g_anw, g_fnw, g_alog, g_dtb, g_dnw, g_qw, g_kw, g_sinks, g_ca_f, g_dc_f, g_fc_f, g_meta_f,
     loss) = _split_flat(sum_slots(gathered_small), small_shapes)
    grads.update(meta_tokens=lax.dynamic_slice_in_dim(g_meta_f, chip * 256, 256, axis=1), attn_norm_w=g_anw,
                 ffn_norm_w=g_fnw, conv_a_w=lax.dynamic_slice_in_dim(g_ca_f, chip * 128, 128, axis=2),
                 dn_conv_w=lax.dynamic_slice_in_dim(g_dc_f, chip * 384, 384, axis=2), dn_a_log=g_alog,
                 dn_dt_bias=g_dtb, dn_norm_w=g_dnw, swa_q_norm_w=g_qw, swa_k_norm_w=g_kw, swa_sinks=g_sinks,
                 ffn_conv_w=lax.dynamic_slice_in_dim(g_fc_f, chip * 704, 704, axis=2))
    grads = {n: grads[n].reshape(weights[n].shape) for n in names}
    shapes = [weights[n].shape for n in small]
    packed = [_flat_pad([t[n] for n in small], SW_ROWS) for t in (weights, grads, m_in, v_in)]
    for store, flat in zip((delta, new_m, new_v), adamw(*packed, name="adamw_small")):
        for n, t in zip(small, _split_flat(flat, shapes)):
            store[n] = t
    return (loss, grad_x, *[grads[n] for n in names], *[delta[n] for n in names],
            *[new_m[n] for n in names], *[new_v[n] for n in names])
```

```python
import functools

import jax
import jax.numpy as jnp
from jax import lax
from jax.experimental import pallas as pl
from jax.experimental.pallas import tpu as pltpu
from jax.experimental.pallas import tpu_sc as plsc

F32 = jnp.float32
BF16 = jnp.bfloat16
HI = lax.Precision.HIGHEST
MESH = pl.DeviceIdType.MESH

D = 1024
N_META = 16
PAD = 112
HEAD0 = PAD + N_META
D_CONV = 512
DN_H = 4
DN_D = 128
DN_DIM = 512
CH = 64
IN_DIM = 3592
P_W = 3840
BG0 = 3584
SWA_H = 16
SWA_KV = 4
SWA_D = 64
BLK = 128
NKEY = N_META + 2 * BLK
D_FF = 2816
EPS = 1e-6
LR, B1, B2, AEPS, WD, STEP = 0.001, 0.9, 0.999, 1e-08, 0.01, 10
VMEM_LIMIT = 48 * 1024 * 1024
MM_VMEM_BUDGET = 34 * 1024 * 1024
R_BIG = 6144
R_HALF = R_BIG // 2
SV_ROWS = 48
SW_ROWS = 16


def _pick(n, cands):
    for c in cands:
        if n % c == 0:
            return c
    return n


def _params(sem=None):
    return pltpu.CompilerParams(dimension_semantics=sem, vmem_limit_bytes=VMEM_LIMIT)


def _dot(a, b, ca=1, cb=0, prec=None):
    return lax.dot_general(a, b, (((ca,), (cb,)), ((), ())), precision=prec,
                           preferred_element_type=F32)


def _sigmoid(x):
    return 1.0 / (1.0 + jnp.exp(-x))


def _silu(x):
    return x * _sigmoid(x)


def _dsilu(x):
    s = _sigmoid(x)
    return s * (1.0 + x * (1.0 - s))


def _softplus(x):
    return jnp.maximum(x, 0.0) + jnp.log(1.0 + jnp.exp(-jnp.abs(x)))


def mm(a, b, *, name, ta=False, tb=False, out_dtype=F32, add=None, tm=None, tn=None, tk=None,
       b_chip=False, out_chip=False, swap_mid=False, epi=None, epi_ins=(), epi_consts=(), epi_outs=(), epi_accs=()):
    if epi is not None:
        return _mm_epi(a, b, name=name, tb=tb, tn=tn, b_chip=b_chip, swap_mid=swap_mid, epi=epi, epi_ins=epi_ins,
                       epi_consts=epi_consts, epi_outs=epi_outs, epi_accs=epi_accs)
    chip_of = _chip_order(swap_mid)
    m, k = (a.shape[1], a.shape[0]) if ta else a.shape
    if b_chip:
        n = b.shape[1] if tb else 4 * b.shape[2]
        if tb:
            tk = b.shape[2]
        else:
            tn = b.shape[2]
    else:
        n = b.shape[0] if tb else b.shape[1]
    if out_chip:
        tn = n // 4
    tn = tn or _pick(n, (1408, 1024, 768, 512, 256, 128))
    tk = tk or (_pick(k, (1408, 704, 384, 128)) if ta else _pick(k, (1024, 1408, 768, 512, 128)))
    nk = k // tk
    if tm is None:
        isz = lambda t: jnp.dtype(t.dtype).itemsize
        osz = jnp.dtype(out_dtype).itemsize
        for tm in ((1408, 1024, 512, 384, 256, 128) if ta else (1408, 704, 512, 384, 256, 128)):
            need = 2 * (tm * tk * isz(a) + tk * tn * isz(b) + tm * tn * osz + (tm * tn * 4 if add is not None else 0))
            need += tm * tn * 4 if nk > 1 else 0
            if m % tm == 0 and need <= MM_VMEM_BUDGET:
                break
        else:
            tm = m
    dims = (((0 if ta else 1,), (1 if tb else 0,)), ((), ()))

    def body(*refs):
        if add is None:
            a_ref, b_ref, o_ref, acc_ref = refs
            add_ref = None
        else:
            a_ref, b_ref, add_ref, o_ref, acc_ref = refs
        part = lax.dot_general(a_ref[...].astype(BF16), b_ref[...].astype(BF16), dims,
                               preferred_element_type=F32)

        def finish(total):
            if add_ref is not None:
                total = total + add_ref[...]
            o_ref[...] = total.astype(out_dtype)

        if nk == 1:
            finish(part)
        else:
            kk = pl.program_id(2)

            @pl.when(kk == 0)
            def _():
                acc_ref[...] = part

            @pl.when(kk > 0)
            def _():
                acc_ref[...] += part

            @pl.when(kk == nk - 1)
            def _():
                finish(acc_ref[...])

    a_spec = pl.BlockSpec((tk, tm), lambda i, j, kk: (kk, i)) if ta else pl.BlockSpec((tm, tk), lambda i, j, kk: (i, kk))
    if b_chip and tb:
        b_spec = pl.BlockSpec((None, tn, tk), lambda i, j, kk: (chip_of(kk), j, 0))
    elif b_chip:
        b_spec = pl.BlockSpec((None, tk, tn), lambda i, j, kk: (j, kk, 0))
    elif tb:
        b_spec = pl.BlockSpec((tn, tk), lambda i, j, kk: (j, kk))
    else:
        b_spec = pl.BlockSpec((tk, tn), lambda i, j, kk: (kk, j))
    o_spec = pl.BlockSpec((tm, tn), lambda i, j, kk: (i, j))
    in_specs = [a_spec, b_spec] + ([o_spec] if add is not None else [])
    args = [a, b] + ([add] if add is not None else [])
    out_spec = pl.BlockSpec((None, tm, tn), lambda i, j, kk: (chip_of(j), i, 0)) if out_chip else o_spec
    return pl.pallas_call(
        body, name=name, interpret=False,
        out_shape=jax.ShapeDtypeStruct((4, m, tn) if out_chip else (m, n), out_dtype),
        grid=(m // tm, n // tn, nk), in_specs=in_specs, out_specs=out_spec,
        scratch_shapes=[pltpu.VMEM((tm, tn) if nk > 1 else (8, 128), F32)],
        compiler_params=_params(("parallel", "parallel", "arbitrary")),
    )(*args)


def _chip_order(swap_mid):
    return (lambda k: (k % 2) * 2 + k // 2) if swap_mid else (lambda k: k)


def _mm_epi(a, b, *, name, tb, tn, b_chip, epi, epi_ins, epi_consts, epi_outs, epi_accs, swap_mid=False):
    chip_of = _chip_order(swap_mid)
    m, k = a.shape
    if b_chip:
        n = b.shape[1] if tb else 4 * b.shape[2]
        tk = b.shape[2] if tb else None
        tn = tn if tb else b.shape[2]
    else:
        n = b.shape[0] if tb else b.shape[1]
        tk = None
    tn = tn or _pick(n, (1408, 1024, 768, 512, 256, 128))
    tk = tk or _pick(k, (1024, 1408, 1280, 768, 512, 128))
    nk, nj = k // tk, n // tn
    isz = lambda t: jnp.dtype(t.dtype if hasattr(t, "dtype") else t).itemsize
    outs3 = [t if isinstance(t, tuple) else (t, n, lambda j: j) for t in epi_outs]
    side = sum(isz(t) for t, _ in epi_ins) + sum(isz(dt) for dt, _, _ in outs3)
    for tm in (1408, 704, 512, 384, 256, 128):
        need = 2 * (tm * tk * isz(a) + tk * tn * isz(b) + tm * tn * side) + (tm * tn * 4 if nk > 1 else 0)
        if m % tm == 0 and need <= MM_VMEM_BUDGET:
            break
    else:
        tm = m
    dims = (((1,), (1 if tb else 0,)), ((), ()))
    n_in, n_c, n_out, n_acc = len(epi_ins), len(epi_consts), len(epi_outs), len(epi_accs)

    def body(*refs):
        a_ref, b_ref = refs[:2]
        in_refs = refs[2:2 + n_in + n_c]
        out_refs = refs[2 + n_in + n_c:2 + n_in + n_c + n_out]
        acc_out = refs[2 + n_in + n_c + n_out:2 + n_in + n_c + n_out + n_acc]
        acc_ref = refs[-1]
        i, j, kk = pl.program_id(0), pl.program_id(1), pl.program_id(2)
        part = lax.dot_general(a_ref[...].astype(BF16), b_ref[...].astype(BF16), dims,
                               preferred_element_type=F32)

        def finish(total):
            res = epi(i * tm, total, *[r[...] for r in in_refs])
            if not isinstance(res, (tuple, list)):
                res = (res,)
            for r, v in zip(out_refs, res[:n_out]):
                r[...] = v.astype(r.dtype)
            if n_acc:
                @pl.when(jnp.logical_and(i == 0, j == 0))
                def _():
                    for r in acc_out:
                        r[...] = jnp.zeros(r.shape, r.dtype)

                for r, v in zip(acc_out, res[n_out:]):
                    r[...] += jnp.broadcast_to(v, r.shape).astype(r.dtype)

        if nk == 1:
            finish(part)
        else:
            @pl.when(kk == 0)
            def _():
                acc_ref[...] = part

            @pl.when(kk > 0)
            def _():
                acc_ref[...] += part

            @pl.when(kk == nk - 1)
            def _():
                finish(acc_ref[...])

    a_spec = pl.BlockSpec((tm, tk), lambda i, j, kk: (i, kk))
    if b_chip and tb:
        b_spec = pl.BlockSpec((None, tn, tk), lambda i, j, kk: (chip_of(kk), j, 0))
    elif b_chip:
        b_spec = pl.BlockSpec((None, tk, tn), lambda i, j, kk: (j, kk, 0))
    elif tb:
        b_spec = pl.BlockSpec((tn, tk), lambda i, j, kk: (j, kk))
    else:
        b_spec = pl.BlockSpec((tk, tn), lambda i, j, kk: (kk, j))
    in_specs = [a_spec, b_spec]
    in_specs += [pl.BlockSpec((tm, tn), lambda i, j, kk, col=col: (i, col(j))) for _, col in epi_ins]
    in_specs += [pl.BlockSpec(t.shape, lambda i, j, kk, nd=t.ndim: (0,) * nd) for t in epi_consts]
    out_specs = [pl.BlockSpec((tm, tn), lambda i, j, kk, col=col: (i, col(j))) for _, _, col in outs3]
    out_specs += [pl.BlockSpec(s, lambda i, j, kk, nd=len(s): (0,) * nd) for s, _ in epi_accs]
    out_shape = [jax.ShapeDtypeStruct((m, width), dt) for dt, width, _ in outs3]
    out_shape += [jax.ShapeDtypeStruct(s, dt) for s, dt in epi_accs]
    sem = ("arbitrary", "arbitrary", "arbitrary") if n_acc else ("parallel", "parallel", "arbitrary")
    return pl.pallas_call(
        body, name=name, interpret=False, out_shape=out_shape,
        grid=(m // tm, nj, nk), in_specs=in_specs, out_specs=out_specs,
        scratch_shapes=[pltpu.VMEM((tm, tn) if nk > 1 else (8, 128), F32)],
        compiler_params=_params(sem),
    )(a, b, *[t for t, _ in epi_ins], *epi_consts)


def cols(arr, tr, width=None, cb=0):
    width = width or arr.shape[1]
    return (arr, (tr, width), lambda i: (i, cb), "r2")


def heads(arr, tr):
    return (arr, (arr.shape[0], tr, arr.shape[2]), lambda i: (0, i, 0), "r3")


def whole(arr):
    nd = arr.ndim
    return (arr, arr.shape, lambda i: (0,) * nd, "w")


STRIP = 16


def _rows_of(ref, kind, r0, n):
    if kind == "r2":
        return ref[pl.ds(r0, n), :]
    if kind == "r3":
        return ref[:, pl.ds(r0, n), :]
    return ref[...]


def _set_rows(ref, kind, r0, n, v):
    if kind == "r2":
        ref[pl.ds(r0, n), :] = v.astype(ref.dtype)
    elif kind == "r3":
        ref[:, pl.ds(r0, n), :] = v.astype(ref.dtype)
    else:
        ref[...] = v.astype(ref.dtype)


def rowwise(fn, ins, outs, *, steps, name, accs=(), strip=None):
    n_in, n_out, n_acc = len(ins), len(outs), len(accs)
    kin = [t[3] for t in ins]
    kout = [t[4] for t in outs]
    tr = next((t[1][-2] for t in ins if t[3] != "w"), 0)

    def body(*refs):
        i = pl.program_id(0)
        in_refs, out_refs, acc_refs = refs[:n_in], refs[n_in:n_in + n_out], refs[n_in + n_out:]
        if n_acc:
            @pl.when(i == 0)
            def _():
                for r in acc_refs:
                    r[...] = jnp.zeros(r.shape, r.dtype)

        def run(r0, n):
            res = fn(i * tr + r0, *[_rows_of(r, k, r0, n) for r, k in zip(in_refs, kin)])
            if not isinstance(res, (tuple, list)):
                res = (res,)
            for r, k, v in zip(out_refs, kout, res[:n_out]):
                _set_rows(r, k, r0, n, v)
            for r, v in zip(acc_refs, res[n_out:]):
                r[...] += jnp.broadcast_to(v, r.shape).astype(r.dtype)

        if strip is None or tr <= strip:
            run(0, tr)
        else:
            def step(s, carry):
                run(pl.multiple_of(s * strip, strip), strip)
                return carry
            lax.fori_loop(0, tr // strip, step, 0)

    def zmap(nd):
        return lambda i: (0,) * nd

    in_specs = [pl.BlockSpec(t[1], t[2]) for t in ins]
    out_specs = [pl.BlockSpec(t[2], t[3]) for t in outs]
    out_specs += [pl.BlockSpec(s, zmap(len(s))) for s, _ in accs]
    out_shape = [jax.ShapeDtypeStruct(t[0], t[1]) for t in outs]
    out_shape += [jax.ShapeDtypeStruct(s, d) for s, d in accs]
    res = pl.pallas_call(
        body, name=name, interpret=False, out_shape=out_shape, grid=(steps,),
        in_specs=in_specs, out_specs=out_specs,
        compiler_params=_params(("arbitrary",)),
    )(*[t[0] for t in ins])
    return res


def out2d(rows, width, dtype, tr):
    return ((rows, width), dtype, (tr, width), lambda i: (i, 0), "r2")


def conv_fwd(xs, w8, kw, *, rows, c, tc, tr, name, post, extras=(), outs=(), pre=None, strip=STRIP):
    nx, ne, no = len(xs), len(extras), len(outs)
    nr, nc = rows // tr, c // tc
    r8 = tr // 8
    st = strip

    def body(*refs):
        x_refs = refs[:2 * nx]
        w_ref = refs[2 * nx]
        e_refs = refs[2 * nx + 1:2 * nx + 1 + ne]
        o_refs = refs[2 * nx + 1 + ne:2 * nx + 1 + ne + no]
        scr = refs[-1]
        j, i = pl.program_id(0), pl.program_id(1)
        halo = [x_refs[2 * q + 1][...].astype(F32) for q in range(nx)]
        scr[0:8, :] = jnp.where(i > 0, pre(*halo) if pre else halo[0], 0.0)

        def fill(s, carry):
            r0 = pl.multiple_of(s * st, st)
            cur = [x_refs[2 * q][pl.ds(r0, st), :].astype(F32) for q in range(nx)]
            scr[pl.ds(8 + r0, st), :] = pre(*cur) if pre else cur[0]
            return carry

        def comp(s, carry):
            r0 = pl.multiple_of(s * st, st)
            win = scr[pl.ds(r0, st + 8), :]
            y = jnp.zeros((st, tc), F32)
            for q in range(kw):
                sh = kw - 1 - q
                y = y + w_ref[q:q + 1, :] * win[8 - sh:8 - sh + st]
            res = post(j, y, *[e[pl.ds(r0, st), :] for e in e_refs])
            if not isinstance(res, (tuple, list)):
                res = (res,)
            for r, v in zip(o_refs, res):
                r[pl.ds(r0, st), :] = v.astype(r.dtype)
            return carry

        lax.fori_loop(0, tr // st, fill, 0)
        lax.fori_loop(0, tr // st, comp, 0)

    in_specs, args = [], []
    for arr, cb0 in xs:
        in_specs.append(pl.BlockSpec((tr, tc), lambda j, i, cb0=cb0: (i, cb0 + j)))
        in_specs.append(pl.BlockSpec((8, tc), lambda j, i, cb0=cb0: (jnp.maximum(i * r8 - 1, 0), cb0 + j)))
        args += [arr, arr]
    in_specs.append(pl.BlockSpec((8, tc), lambda j, i: (0, j)))
    args.append(w8)
    for arr, cb0 in extras:
        in_specs.append(pl.BlockSpec((tr, tc), lambda j, i, cb0=cb0: (i, cb0 + j)))
        args.append(arr)
    return pl.pallas_call(
        body, name=name, interpret=False,
        out_shape=[jax.ShapeDtypeStruct((rows, c), dt) for dt in outs],
        grid=(nc, nr), in_specs=in_specs,
        out_specs=[pl.BlockSpec((tr, tc), lambda j, i: (i, j)) for _ in outs],
        scratch_shapes=[pltpu.VMEM((tr + 8, tc), F32)],
        compiler_params=_params(("parallel", "arbitrary")),
    )(*args)


def conv_bwd(xs, w8, kw, dy, *, rows, c, tc, tr, name, post, extras=(), outs=(), pre=None):
    nx, ne, no = len(xs), len(extras), len(outs)
    nr, nc = rows // tr, c // tc
    r8 = tr // 8

    def body(*refs):
        x_refs = refs[:nx]
        w_ref, dy_ref, dyn_ref = refs[nx:nx + 3]
        e_refs = refs[nx + 3:nx + 3 + ne]
        first_out = nx + 3 + ne
        o_refs = refs[first_out:first_out + no]
        dw_ref = refs[first_out + no]
        gscr = refs[-1]
        i = pl.program_id(1)
        gscr[tr:tr + 8, :] = jnp.where(i < nr - 1, dyn_ref[...].astype(F32), 0.0)

        def fill(s, carry):
            r0 = pl.multiple_of(s * STRIP, STRIP)
            gscr[pl.ds(r0, STRIP), :] = dy_ref[pl.ds(r0, STRIP), :].astype(F32)
            return carry

        def comp(s, dws):
            r0 = pl.multiple_of(s * STRIP, STRIP)
            gwin = gscr[pl.ds(r0, STRIP + 8), :]
            cur = [x_refs[q][pl.ds(r0, STRIP), :].astype(F32) for q in range(nx)]
            x = pre(*cur) if pre else cur[0]
            dx = jnp.zeros((STRIP, tc), F32)
            new = []
            for q in range(kw):
                sh = kw - 1 - q
                ahead = gwin[sh:sh + STRIP]
                dx = dx + w_ref[q:q + 1, :] * ahead
                part = ahead * x
                new.append(dws[q] + part[0:8] + part[8:16])
            res = post(dx, *[e[pl.ds(r0, STRIP), :] for e in e_refs])
            if not isinstance(res, (tuple, list)):
                res = (res,)
            for r, v in zip(o_refs, res):
                r[pl.ds(r0, STRIP), :] = v.astype(r.dtype)
            return tuple(new)

        lax.fori_loop(0, tr // STRIP, fill, 0)
        dws = lax.fori_loop(0, tr // STRIP, comp, tuple(jnp.zeros((8, tc), F32) for _ in range(kw)))

        @pl.when(i == 0)
        def _():
            dw_ref[...] = jnp.zeros((8, tc), F32)

        dw_ref[...] += jnp.concatenate([jnp.sum(t, axis=0, keepdims=True) for t in dws]
                                       + [jnp.zeros((8 - kw, tc), F32)], axis=0)

    in_specs, args = [], []
    for arr, cb0 in xs:
        in_specs.append(pl.BlockSpec((tr, tc), lambda j, i, cb0=cb0: (i, cb0 + j)))
        args.append(arr)
    in_specs.append(pl.BlockSpec((8, tc), lambda j, i: (0, j)))
    in_specs.append(pl.BlockSpec((tr, tc), lambda j, i: (i, j)))
    in_specs.append(pl.BlockSpec((8, tc), lambda j, i: (jnp.minimum((i + 1) * r8, nr * r8 - 1), j)))
    args += [w8, dy, dy]
    for arr, cb0 in extras:
        in_specs.append(pl.BlockSpec((tr, tc), lambda j, i, cb0=cb0: (i, cb0 + j)))
        args.append(arr)
    return pl.pallas_call(
        body, name=name, interpret=False,
        out_shape=[jax.ShapeDtypeStruct((rows, c), dt) for dt in outs] + [jax.ShapeDtypeStruct((8, c), F32)],
        grid=(nc, nr), in_specs=in_specs,
        out_specs=[pl.BlockSpec((tr, tc), lambda j, i: (i, j)) for _ in outs] + [pl.BlockSpec((8, tc), lambda j, i: (0, j))],
        scratch_shapes=[pltpu.VMEM((tr + 8, tc), F32)],
        compiler_params=_params(("parallel", "arbitrary")),
    )(*args)


def rms_fwd(h, w, *, name):
    rows = h.shape[0]
    tr = _pick(rows, (384, 128))

    def fn(i, x, wv):
        r = lax.rsqrt(jnp.mean(x * x, axis=1, keepdims=True) + EPS)
        return x * r * wv

    return rowwise(fn, [cols(h, tr), whole(w)], [out2d(rows, D, BF16, tr)], steps=rows // tr, name=name)[0]


def _rms_bwd_epi(row0, g, x, dr, wv):
    r = lax.rsqrt(jnp.mean(x * x, axis=1, keepdims=True) + EPS)
    xh = x * r
    gw = g * wv
    dx = r * (gw - xh * jnp.mean(gw * xh, axis=1, keepdims=True))
    row = row0 + lax.broadcasted_iota(jnp.int32, (x.shape[0], 1), 0)
    return jnp.where(row >= PAD, dr + dx, 0.0), jnp.sum(g * xh, axis=0, keepdims=True)


def dx_rms_bwd(dy, w, h, nw, dres, *, name, b_chip=False, swap_mid=False):
    return mm(dy, w, tb=True, b_chip=b_chip, swap_mid=swap_mid, tn=D, name=name, epi=_rms_bwd_epi,
              epi_ins=[(h, lambda j: 0), (dres, lambda j: 0)], epi_consts=[nw], epi_outs=[F32],
              epi_accs=[((1, D), F32)])


def _add_loss_epi(row0, t, h, tgt):
    row = row0 + lax.broadcasted_iota(jnp.int32, (t.shape[0], 1), 0)
    diff = jnp.where(row >= HEAD0, t + h - tgt, 0.0)
    part = jnp.sum(jnp.sum(diff * diff, axis=1, keepdims=True), axis=0, keepdims=True)
    return diff * (1.0 / D), part * (0.5 / D)


def add_loss(a, w, h, target, *, name):
    return mm(a, w, name=name, epi=_add_loss_epi, epi_ins=[(h, lambda j: 0), (target, lambda j: 0)],
              epi_outs=[F32], epi_accs=[((1, 128), F32)])


def adamw(w, g, m, v, *, name):
    shape = w.shape
    gs = list(g) if isinstance(g, (list, tuple)) else [g]
    nl = len(gs)
    width = shape[-1]
    rows = w.size // width
    rl = rows // nl
    tr = _pick(rl, (256, 176, 128, 64, 16, 8))
    nr = rl // tr
    if w.ndim == 3 and shape[1] % tr == 0:
        per = shape[1] // tr
        view = lambda t: (t, (None, tr, width), lambda i: (i // per, i % per, 0), "r2")
        out = (shape, F32, (None, tr, width), lambda i: (i // per, i % per, 0), "r2")
    else:
        view = lambda t: cols(t.reshape(rows, width), tr)
        out = out2d(rows, width, F32, tr)

    def fn(i, wv, mv, vv, *gvs):
        gv = gvs[0]
        for layer in range(1, nl):
            gv = jnp.where(i >= layer * rl, gvs[layer], gv)
        mn = B1 * mv + (1.0 - B1) * gv
        vn = B2 * vv + (1.0 - B2) * gv * gv
        mh = mn / (1.0 - B1 ** STEP)
        vh = vn / (1.0 - B2 ** STEP)
        return -LR * (mh / (jnp.sqrt(vh) + AEPS) + WD * wv), mn, vn, gv

    g_ins = [(t.reshape(rl, width), (tr, width), lambda i, layer=layer: (jnp.clip(i - layer * nr, 0, nr - 1), 0), "r2")
             for layer, t in enumerate(gs)]
    res = rowwise(fn, [view(t) for t in (w, m, v)] + g_ins, [out] * 4, steps=rows // tr, name=name)
    return [r.reshape(shape) for r in res]


HB = DN_H * CH
PAIR = 3


def _split(a):
    hi = a.astype(BF16)
    return hi, (a - hi.astype(F32)).astype(BF16)


def _dot1(a, b, ca=1, cb=0):
    return _dot(a.astype(BF16), b.astype(BF16), ca, cb)


def _dot3(a, b, ca=1, cb=0):
    ah, al = _split(a)
    bh, bl = _split(b)
    return _dot(ah, bh, ca, cb) + (_dot(ah, bl, ca, cb) + _dot(al, bh, ca, cb))


def _dot01(m01, b, ca=1, cb=0):
    bh, bl = _split(b)
    m = m01.astype(BF16)
    return _dot(m, bh, ca, cb) + _dot(m, bl, ca, cb)


def _stack(x):
    return jnp.concatenate([x[:, h * DN_D:(h + 1) * DN_D] for h in range(DN_H)], axis=0)


def _unstack(x):
    return jnp.concatenate([x[h * CH:(h + 1) * CH] for h in range(DN_H)], axis=1)


def _tri_inv(mats, blk, eye):
    each = lambda f, *lists: [f(*t) for t in zip(*lists)]
    ad = [jnp.where(blk, a, 0.0) for a in mats]
    lo = each(lambda a, d: a - d, mats, ad)
    a2 = each(_dot3, ad, ad)
    a4 = each(_dot3, a2, a2)
    a8 = each(_dot3, a4, a4)
    dgi = each(lambda d, s: _dot3(eye - d, eye + s), ad, a2)
    dgi = each(lambda p, s: _dot3(p, eye + s), dgi, a4)
    dgi = each(lambda p, s: _dot3(p, eye + s), dgi, a8)
    n = each(_dot3, dgi, lo)
    n2 = each(_dot3, n, n)
    return each(_dot3, each(lambda u, v: _dot3(eye - u, eye + v), n, n2), dgi)


def _dn_masks():
    row = lax.broadcasted_iota(jnp.int32, (HB, HB), 0)
    col = lax.broadcasted_iota(jnp.int32, (HB, HB), 1)
    same = (row // CH) == (col // CH)
    incl = jnp.logical_and(same, row >= col)
    strict = jnp.logical_and(same, row > col)
    upper = jnp.logical_and(same, row <= col)
    blk = (row // 16) == (col // 16)
    eye = (row == col).astype(F32)
    return incl, strict, upper, blk, eye


def _dn_chunk(qv, kv, vv, bc, br, incl, strict):
    r64 = lax.broadcasted_iota(jnp.int32, (CH, CH), 0)
    c64 = lax.broadcasted_iota(jnp.int32, (CH, CH), 1)
    dcol = _dot01((r64 >= c64).astype(F32), bc)
    drow = _dot3(br, (r64 <= c64).astype(F32))
    col = lambda m, l0: jnp.concatenate([m[:, l0 + h:l0 + h + 1] for h in range(DN_H)], axis=0)
    b_c = col(bc, 0)
    d_c = col(dcol, 4)
    d_r = jnp.concatenate([drow[4 + h:5 + h, :] for h in range(DN_H)], axis=1)
    d_last_h = [dcol[CH - 1:CH, 4 + h:5 + h] for h in range(DN_H)]
    d_last = jnp.concatenate([jnp.broadcast_to(t, (CH, 1)) for t in d_last_h], axis=0)
    q, k, v = _stack(qv), _stack(kv), _stack(vv)
    dm = jnp.where(incl, jnp.exp(jnp.where(incl, d_c - d_r, 0.0)), 0.0)
    kk = _dot1(k, k, 1, 1)
    a = jnp.where(strict, b_c * kk * dm, 0.0)
    ed = jnp.exp(d_c)
    rhs = jnp.concatenate([v * b_c, k * (b_c * ed)], axis=1)
    qk = _dot1(q, k, 1, 1) * dm
    ekd = jnp.exp(d_last - d_c)
    gl = [jnp.exp(t) for t in d_last_h]
    return q, k, v, b_c, dm, kk, a, ed, rhs, qk, ekd, gl


def dn_fwd(qkv_n, bgcol, bgrow):
    rows = qkv_n.shape[0]
    nch = rows // CH

    def body(q_ref, k_ref, v_ref, bc_ref, br_ref, o_ref, s_out, ti_out, s_scr, prep, prep_qk, prep_gl):
        n = pl.program_id(0)

        @pl.when(n == 0)
        def _():
            s_scr[...] = jnp.zeros(s_scr.shape, F32)
            prep[...] = jnp.zeros(prep.shape, F32)
            prep_qk[...] = jnp.zeros(prep_qk.shape, F32)
            prep_gl[...] = jnp.zeros(prep_gl.shape, F32)

        live = n > 0
        rows_of = [slice(h * CH, (h + 1) * CH) for h in range(DN_H)]
        s = [s_scr[h] for h in range(DN_H)]
        for c in range(PAIR):
            u, w, qd, kd = prep[c, 0], prep[c, 1], prep[c, 2], prep[c, 3]
            for h in range(DN_H):
                s_out[c, h] = s[h]
            v_new = [u[rs] - _dot1(w[rs], s[h]) for h, rs in enumerate(rows_of)]
            o_state = [_dot1(qd[rs], s[h]) for h, rs in enumerate(rows_of)]
            s = [jnp.where(live, prep_gl[c, h:h + 1, 0:1] * s[h] + _dot1(kd[rs], v_new[h], 0, 0), s[h])
                 for h, rs in enumerate(rows_of)]
            o = jnp.concatenate(o_state, axis=0) + _dot1(prep_qk[c], jnp.concatenate(v_new, axis=0))
            o_ref[c * CH:(c + 1) * CH, :] = _unstack(o)
        for h in range(DN_H):
            s_scr[h] = s[h]

        incl, strict, _, blk, eye = _dn_masks()
        parts = []
        for c in range(PAIR):
            rows_c = slice(c * CH, (c + 1) * CH)
            parts.append(_dn_chunk(q_ref[rows_c, :], k_ref[rows_c, :], v_ref[rows_c, :], bc_ref[rows_c, :],
                                   br_ref[c], incl, strict))
        tinvs = _tri_inv([p[6] for p in parts], blk, eye)
        for c, (q, k, v, b_c, dm, kk, a, ed, rhs, qk_n, ekd, gl) in enumerate(parts):
            tinv = tinvs[c]
            ti_out[c] = tinv
            sol = _dot3(tinv, rhs)
            prep[c, 0] = sol[:, :DN_D]
            prep[c, 1] = sol[:, DN_D:]
            prep[c, 2] = q * ed
            prep[c, 3] = k * ekd
            prep_qk[c] = qk_n
            prep_gl[c] = jnp.concatenate([jnp.broadcast_to(t, (1, 128)) for t in gl]
                                         + [jnp.zeros((8 - DN_H, 128), F32)], axis=0)

    assert nch % PAIR == 0
    npair = nch // PAIR
    last = npair - 1
    return pl.pallas_call(
        body, name="dn_fwd", interpret=False,
        out_shape=[jax.ShapeDtypeStruct((rows, DN_DIM), F32),
                   jax.ShapeDtypeStruct((nch, DN_H, DN_D, DN_D), F32),
                   jax.ShapeDtypeStruct((nch, HB, HB), F32)],
        grid=(npair + 1,),
        in_specs=[pl.BlockSpec((PAIR * CH, DN_DIM), lambda n: (jnp.minimum(n, last), 0)),
                  pl.BlockSpec((PAIR * CH, DN_DIM), lambda n: (jnp.minimum(n, last), 1)),
                  pl.BlockSpec((PAIR * CH, DN_DIM), lambda n: (jnp.minimum(n, last), 2)),
                  pl.BlockSpec((PAIR * CH, 128), lambda n: (jnp.minimum(n, last), 0)),
                  pl.BlockSpec((PAIR, 8, CH), lambda n: (jnp.minimum(n, last), 0, 0))],
        out_specs=[pl.BlockSpec((PAIR * CH, DN_DIM), lambda n: (jnp.maximum(n - 1, 0), 0)),
                   pl.BlockSpec((PAIR, DN_H, DN_D, DN_D), lambda n: (jnp.maximum(n - 1, 0), 0, 0, 0)),
                   pl.BlockSpec((PAIR, HB, HB), lambda n: (jnp.minimum(n, last), 0, 0))],
        scratch_shapes=[pltpu.VMEM((DN_H, DN_D, DN_D), F32), pltpu.VMEM((PAIR, 4, HB, DN_D), F32),
                        pltpu.VMEM((PAIR, HB, HB), F32), pltpu.VMEM((PAIR, 8, 128), F32)],
        compiler_params=_params(("arbitrary",)),
    )(qkv_n, qkv_n, qkv_n, bgcol, bgrow)


def dn_bwd(qkv_n, bgcol, bgrow, s_all, ti_all, do):
    rows = qkv_n.shape[0]
    nch = rows // CH

    def body(q_ref, k_ref, v_ref, bc_ref, br_ref, s_ref, ti_ref, do_ref, dq_ref, dk_ref, dv_ref, dbg_ref, ds_scr):
        n = pl.program_id(0)

        @pl.when(n == 0)
        def _():
            ds_scr[...] = jnp.zeros(ds_scr.shape, F32)

        incl, strict, upper, _, _ = _dn_masks()
        rsum = lambda t: jnp.sum(t, axis=1, keepdims=True)
        rows_of = [slice(h * CH, (h + 1) * CH) for h in range(DN_H)]
        heads_of = lambda f: jnp.concatenate([f(h, rs) for h, rs in enumerate(rows_of)], axis=0)
        cs = []
        for c in reversed(range(PAIR)):
            rc = slice(c * CH, (c + 1) * CH)
            q, k, v, b_c, dm, kk, a, ed, rhs, qk, ekd, gl = _dn_chunk(
                q_ref[rc, :], k_ref[rc, :], v_ref[rc, :], bc_ref[rc, :], br_ref[c], incl, strict)
            cs.append(dict(rc=rc, q=q, k=k, v=v, b_c=b_c, dm=dm, kk=kk, a=a, ed=ed, rhs=rhs, qk=qk, ekd=ekd, gl=gl,
                           tinv=ti_ref[c], g=_stack(do_ref[rc, :]), s=[s_ref[c, h] for h in range(DN_H)]))
        for t in cs:
            t["sol"] = _dot3(t["tinv"], t["rhs"])
        for t in cs:
            t["u"], t["w"] = t["sol"][:, :DN_D], t["sol"][:, DN_D:]
            t["qd"], t["kd"] = t["q"] * t["ed"], t["k"] * t["ekd"]
            t["v_new"] = heads_of(lambda h, rs: t["u"][rs] - _dot1(t["w"][rs], t["s"][h]))
            t["dv0"] = _dot1(t["qk"], t["g"], 0, 0)
            t["ds0"] = [_dot1(t["qd"][rs], t["g"][rs], 0, 0) for rs in rows_of]
            t["dqd"] = heads_of(lambda h, rs: _dot1(t["g"][rs], t["s"][h], 1, 1))
        for t in cs:
            t["dqk"] = _dot1(t["g"], t["v_new"], 1, 1)
        ds = [ds_scr[h] for h in range(DN_H)]
        for t in cs:
            t["ds"] = ds
            t["dv_new"] = t["dv0"] + heads_of(lambda h, rs: _dot1(t["kd"][rs], ds[h]))
            ds = [t["ds0"][h] + t["gl"][h] * ds[h] - _dot1(t["w"][rs], t["dv_new"][rs], 0, 0)
                  for h, rs in enumerate(rows_of)]
        for h in range(DN_H):
            ds_scr[h] = ds[h]
        for t in cs:
            t["dkd"] = heads_of(lambda h, rs: _dot1(t["v_new"][rs], t["ds"][h], 1, 1))
            dw = heads_of(lambda h, rs: -_dot1(t["dv_new"][rs], t["s"][h], 1, 1))
            t["dsol"] = jnp.concatenate([t["dv_new"], dw], axis=1)
        for t in cs:
            t["drhs"] = _dot3(t["tinv"], t["dsol"], 0, 0)
        for t in cs:
            t["da"] = jnp.where(strict, -_dot1(t["drhs"], t["sol"], 1, 1), 0.0)
        rowi = lax.broadcasted_iota(jnp.int32, (CH, 1), 0)
        lane = lax.broadcasted_iota(jnp.int32, (CH, 128), 1)
        for t in cs:
            q, k, v, b_c, dm, ed, da, dqk = t["q"], t["k"], t["v"], t["b_c"], t["dm"], t["ed"], t["da"], t["dqk"]
            drhs_u, drhs_w = t["drhs"][:, :DN_D], t["drhs"][:, DN_D:]
            s2 = rsum(drhs_w * k)
            dbeta = rsum(drhs_u * v) + s2 * ed + rsum(da * t["kk"] * dm)
            dkk = da * b_c * dm
            dqkr = dqk * dm
            mmat = da * t["a"] + dqk * t["qk"]
            tmp = rsum(t["dkd"] * t["kd"])
            dd = (s2 * b_c * ed + rsum(mmat) - _dot3(mmat, jnp.ones((HB, 128), F32), 0, 0)[:, :1]
                  + rsum(t["dqd"] * t["qd"]) - tmp)
            last = []
            for h, rs in enumerate(rows_of):
                dgl = jnp.sum(rsum(t["s"][h] * t["ds"][h]), axis=0, keepdims=True)
                dd_last = jnp.sum(tmp[rs], axis=0, keepdims=True) + dgl * t["gl"][h]
                last.append(jnp.where(rowi == CH - 1, dd_last, 0.0))
            dd = dd + jnp.concatenate(last, axis=0)
            rc = t["rc"]
            dq_ref[rc, :] = _unstack(_dot1(dqkr, k) + t["dqd"] * ed)
            dk_ref[rc, :] = _unstack(drhs_w * (b_c * ed) + _dot1(dkk, k) + _dot1(dkk, k, 0, 0) + _dot1(dqkr, q, 0, 0)
                                     + t["dkd"] * t["ekd"])
            dv_ref[rc, :] = _unstack(drhs_u * b_c)
            dg = _dot01(upper.astype(F32), jnp.broadcast_to(dd, (HB, 128)))[:, :1]
            out = jnp.zeros((CH, 128), F32)
            for h, rs in enumerate(rows_of):
                out = out + jnp.where(lane == h, dbeta[rs], 0.0) + jnp.where(lane == 4 + h, dg[rs], 0.0)
            dbg_ref[rc, :] = out

    assert nch % PAIR == 0
    npair = nch // PAIR
    rev = lambda n: npair - 1 - n
    blk = PAIR * CH
    return pl.pallas_call(
        body, name="dn_bwd", interpret=False,
        out_shape=[jax.ShapeDtypeStruct((rows, DN_DIM), F32)] * 3 + [jax.ShapeDtypeStruct((rows, 128), F32)],
        grid=(npair,),
        in_specs=[pl.BlockSpec((blk, DN_DIM), lambda n: (rev(n), 0)),
                  pl.BlockSpec((blk, DN_DIM), lambda n: (rev(n), 1)),
                  pl.BlockSpec((blk, DN_DIM), lambda n: (rev(n), 2)),
                  pl.BlockSpec((blk, 128), lambda n: (rev(n), 0)),
                  pl.BlockSpec((PAIR, 8, CH), lambda n: (rev(n), 0, 0)),
                  pl.BlockSpec((PAIR, DN_H, DN_D, DN_D), lambda n: (rev(n), 0, 0, 0)),
                  pl.BlockSpec((PAIR, HB, HB), lambda n: (rev(n), 0, 0)),
                  pl.BlockSpec((blk, DN_DIM), lambda n: (rev(n), 0))],
        out_specs=[pl.BlockSpec((blk, DN_DIM), lambda n: (rev(n), 0))] * 3 + [pl.BlockSpec((blk, 128), lambda n: (rev(n), 0))],
        scratch_shapes=[pltpu.VMEM((DN_H, DN_D, DN_D), F32)],
        compiler_params=_params(("arbitrary",)),
    )(qkv_n, qkv_n, qkv_n, bgcol, bgrow, s_all, ti_all, do)


def _swa_valid(n):
    c3 = lax.broadcasted_iota(jnp.int32, (NKEY, 4 * BLK), 0)
    r = lax.broadcasted_iota(jnp.int32, (NKEY, 4 * BLK), 1) % BLK
    prev0 = N_META + BLK
    c = jnp.where(c3 < N_META, PAD + c3, jnp.where(c3 < prev0, c3 - N_META, c3 - prev0))
    lo = jnp.where(c3 < N_META, 0, jnp.where(c3 < prev0, r + 1 + jnp.where(n >= 2, 0, BLK), 0))
    hi = jnp.where(c3 < N_META, r + jnp.where(n >= 1, BLK, 0),
                   jnp.where(c3 < prev0, BLK, r - jnp.where(n >= 1, 0, BLK)))
    return jnp.logical_and(c >= lo, c <= hi)


def _swa_probs(qs, kcats, valid, sinks):
    s = [jnp.where(valid, _dot(kc, q, 1, 1), -1e30) for q, kc in zip(qs, kcats)]
    m = [jnp.maximum(jnp.max(t, axis=0, keepdims=True), sk) for t, sk in zip(s, sinks)]
    e = [jnp.where(valid, jnp.exp(t - mx), 0.0) for t, mx in zip(s, m)]
    es = [jnp.exp(sk - mx) for sk, mx in zip(sinks, m)]
    inv = [1.0 / (jnp.sum(t, axis=0, keepdims=True) + u) for t, u in zip(e, es)]
    return [t * i for t, i in zip(e, inv)], [u * i for u, i in zip(es, inv)]


def _swa_group(q_ref, sk_ref, h):
    q4 = jnp.concatenate([q_ref[4 * h + g] for g in range(4)], axis=0)
    sink4 = jnp.concatenate([jnp.full((1, BLK), sk_ref[4 * h + g], F32) for g in range(4)], axis=1)
    return q4, sink4


def _swa_specs():
    q = pl.BlockSpec((SWA_H, BLK, SWA_D), lambda n: (0, n, 0))
    km = pl.BlockSpec((SWA_KV, N_META, SWA_D), lambda n: (0, PAD // N_META, 0))
    kp = pl.BlockSpec((SWA_KV, BLK, SWA_D), lambda n: (0, jnp.maximum(n - 1, 0), 0))
    kc = pl.BlockSpec((SWA_KV, BLK, SWA_D), lambda n: (0, n, 0))
    return [q, km, kp, kc, km, kp, kc]


def swa_fwd(qh, kh, vh, sinks):
    rows = qh.shape[1]
    nb = rows // BLK

    def body(q_ref, km, kp, kc, vm, vp, vc, sk_ref, o_ref):
        n = pl.program_id(0)
        valid = _swa_valid(n)
        kcats = [jnp.concatenate([km[h], kp[h], kc[h]], axis=0) for h in range(SWA_KV)]
        vcats = [jnp.concatenate([vm[h], vp[h], vc[h]], axis=0) for h in range(SWA_KV)]
        qs, sinks4 = zip(*[_swa_group(q_ref, sk_ref, h) for h in range(SWA_KV)])
        ps, _ = _swa_probs(qs, kcats, valid, sinks4)
        o4s = [_dot(p.astype(BF16), vc_, 0, 0) for p, vc_ in zip(ps, vcats)]
        o_ref[...] = jnp.concatenate([o4[g * BLK:(g + 1) * BLK] for o4 in o4s for g in range(4)],
                                     axis=1).astype(BF16)

    return pl.pallas_call(
        body, name="swa_fwd", interpret=False,
        out_shape=jax.ShapeDtypeStruct((rows, SWA_H * SWA_D), BF16),
        grid=(nb,),
        in_specs=_swa_specs() + [pl.BlockSpec(memory_space=pltpu.SMEM)],
        out_specs=pl.BlockSpec((BLK, SWA_H * SWA_D), lambda n: (n, 0)),
        compiler_params=_params(("parallel",)),
    )(qh, kh, kh, kh, vh, vh, vh, sinks)


def swa_bwd(qh, kh, vh, sinks, do):
    rows = qh.shape[1]
    nb = rows // BLK

    def body(q_ref, km, kp, kc, vm, vp, vc, do_ref, sk_ref, dq_ref, dk_ref, dv_ref, dsk_ref):
        n = pl.program_id(0)

        @pl.when(n == 0)
        def _():
            dk_ref[...] = jnp.zeros(dk_ref.shape, F32)
            dv_ref[...] = jnp.zeros(dv_ref.shape, F32)

        valid = _swa_valid(n)
        g_all = do_ref[...]
        rowi = lax.broadcasted_iota(jnp.int32, (SWA_H, 128), 0)
        dsk = jnp.zeros((SWA_H, 128), F32)
        pm = pl.multiple_of(jnp.maximum(n - 1, 0) * BLK, BLK)
        pc = pl.multiple_of(n * BLK, BLK)
        hs = range(SWA_KV)
        kcats = [jnp.concatenate([km[h], kp[h], kc[h]], axis=0) for h in hs]
        vcats = [jnp.concatenate([vm[h], vp[h], vc[h]], axis=0) for h in hs]
        qs, sinks4 = zip(*[_swa_group(q_ref, sk_ref, h) for h in hs])
        g4s = [jnp.concatenate([g_all[:, (4 * h + g) * SWA_D:(4 * h + g + 1) * SWA_D] for g in range(4)], axis=0)
               for h in hs]
        ps, pss = _swa_probs(qs, kcats, valid, sinks4)
        dps = [_dot(vc_, g4, 1, 1) for vc_, g4 in zip(vcats, g4s)]
        deltas = [jnp.sum(p * dp, axis=0, keepdims=True) for p, dp in zip(ps, dps)]
        dss = [(p * (dp - dl)).astype(BF16) for p, dp, dl in zip(ps, dps, deltas)]
        dq4s = [_dot(ds, kc_, 0, 0) for ds, kc_ in zip(dss, kcats)]
        dkcs = [_dot(ds, q4) for ds, q4 in zip(dss, qs)]
        dvcs = [_dot(p.astype(BF16), g4) for p, g4 in zip(ps, g4s)]
        for h in hs:
            t = pss[h] * deltas[h]
            for g in range(4):
                dq_ref[4 * h + g] = dq4s[h][g * BLK:(g + 1) * BLK]
                part = -jnp.sum(t[:, g * BLK:(g + 1) * BLK], axis=1, keepdims=True)
                dsk = dsk + jnp.where(rowi == 4 * h + g, part, 0.0)
            lanes = slice(h * SWA_D, (h + 1) * SWA_D)
            for ref, val in ((dk_ref, dkcs[h]), (dv_ref, dvcs[h])):
                ref[PAD:BLK, lanes] += val[0:N_META]
                ref[pl.ds(pm, BLK), lanes] += val[N_META:N_META + BLK]
                ref[pl.ds(pc, BLK), lanes] += val[N_META + BLK:]
        dsk_ref[0] = dsk

    return pl.pallas_call(
        body, name="swa_bwd", interpret=False,
        out_shape=[jax.ShapeDtypeStruct((SWA_H, rows, SWA_D), F32),
                   jax.ShapeDtypeStruct((rows, SWA_KV * SWA_D), F32),
                   jax.ShapeDtypeStruct((rows, SWA_KV * SWA_D), F32),
                   jax.ShapeDtypeStruct((nb, SWA_H, 128), F32)],
        grid=(nb,),
        in_specs=_swa_specs() + [pl.BlockSpec((BLK, SWA_H * SWA_D), lambda n: (n, 0)),
                                 pl.BlockSpec(memory_space=pltpu.SMEM)],
        out_specs=[pl.BlockSpec((SWA_H, BLK, SWA_D), lambda n: (0, n, 0)),
                   pl.BlockSpec((rows, SWA_KV * SWA_D), lambda n: (0, 0)),
                   pl.BlockSpec((rows, SWA_KV * SWA_D), lambda n: (0, 0)),
                   pl.BlockSpec((1, SWA_H, 128), lambda n: (n, 0, 0))],
        compiler_params=_params(("arbitrary",)),
    )(qh, kh, kh, kh, vh, vh, vh, do, sinks)


QK_W = (SWA_H + SWA_KV) * SWA_D


def _head_mean(t):
    r = lax.broadcasted_iota(jnp.int32, (128, 128), 0) // SWA_D
    c = lax.broadcasted_iota(jnp.int32, (128, 128), 1) // SWA_D
    blk = jnp.where(r == c, 1.0 / SWA_D, 0.0).astype(BF16)
    out = []
    for i in range(t.shape[1] // 128):
        hi, lo = _split(t[:, 128 * i:128 * (i + 1)])
        out.append(_dot(hi, blk) + _dot(lo, blk))
    return jnp.concatenate(out, axis=1)


def _qk_scales(qw, kw):
    scale = SWA_D ** -0.5
    wt = jnp.concatenate([jnp.tile(qw.astype(F32) * scale, (1, SWA_H)), jnp.tile(kw.astype(F32), (1, SWA_KV))], axis=1)
    st = jnp.concatenate([jnp.full((1, SWA_H * SWA_D), scale, F32), jnp.ones((1, SWA_KV * SWA_D), F32)], axis=1)
    return wt, st


def qknorm_fwd(qkv, qw, kw):
    rows = qkv.shape[0]
    tr = _pick(rows, (384, 128))
    wt, _ = _qk_scales(qw, kw)

    def fn(i, x, w):
        xq = x[:, :QK_W]
        y = xq * lax.rsqrt(_head_mean(xq * xq) + EPS) * w
        head = lambda t, j: t[:, j * SWA_D:(j + 1) * SWA_D][None]
        qo = jnp.concatenate([head(y, j) for j in range(SWA_H)], axis=0)
        ko = jnp.concatenate([head(y, SWA_H + j) for j in range(SWA_KV)], axis=0)
        vo = jnp.concatenate([head(x, SWA_H + SWA_KV + j) for j in range(SWA_KV)], axis=0)
        return qo, ko, vo

    hm = lambda nh: ((nh, rows, SWA_D), BF16, (nh, tr, SWA_D), lambda i: (0, i, 0), "r3")
    return rowwise(fn, [cols(qkv, tr), whole(wt)], [hm(SWA_H), hm(SWA_KV), hm(SWA_KV)],
                   steps=rows // tr, name="qknorm_fwd")


def qknorm_bwd(qkv, qw, kw, dqh, dk, dv):
    rows = qkv.shape[0]
    tr = _pick(rows, (384, 128))
    wt, st = _qk_scales(qw, kw)

    def fn(i, x, w, sc, dq, dkv, dvv):
        xq = x[:, :QK_W]
        dy = jnp.concatenate([dq[j] for j in range(SWA_H)] + [dkv], axis=1)
        r = lax.rsqrt(_head_mean(xq * xq) + EPS)
        xh = xq * r
        gw = dy * w
        dx = r * (gw - xh * _head_mean(gw * xh))
        return jnp.concatenate([dx, dvv], axis=1), jnp.sum(dy * sc * xh, axis=0, keepdims=True)

    dqkv, dw = rowwise(fn, [cols(qkv, tr), whole(wt), whole(st), heads(dqh, tr), cols(dk, tr), cols(dv, tr)],
                       [out2d(rows, 1536, BF16, tr)], steps=rows // tr, name="qknorm_bwd", accs=[((1, QK_W), F32)])
    dw = dw.reshape(SWA_H + SWA_KV, SWA_D)
    return dqkv, jnp.sum(dw[:SWA_H], axis=0, keepdims=True), jnp.sum(dw[SWA_H:], axis=0, keepdims=True)


def _place():
    return lax.axis_index("x"), lax.axis_index("y"), lax.axis_index("c")


ANY = pl.BlockSpec(memory_space=pl.ANY)


def _rcopy(ssem, rsem, k, src, dst, to):
    return pltpu.make_async_remote_copy(src_ref=src, dst_ref=dst, send_sem=ssem.at[k], recv_sem=rsem.at[k],
                                        device_id=to, device_id_type=MESH)


def gather_weights(shards, small):
    n = len(shards)
    halves = [t.shape[0] // 2 for t in shards]

    def body(*refs):
        s_refs, small_ref = refs[:n], refs[n]
        o_refs, osmall = refs[n + 1:2 * n + 1], refs[2 * n + 1]
        ssem, rsem, lsem = refs[2 * n + 2:]
        x, y, c = _place()
        me = 2 * x + y
        chips = [(1 - x, y), (x, 1 - y), (1 - x, 1 - y)]

        def half(k, s, hh):
            return o_refs[k].at[s, pl.ds(hh * halves[k], halves[k]), :]

        loc = pltpu.make_async_copy(small_ref, osmall.at[me], lsem)
        loc.start()
        sends = []
        for k in range(n):
            for j, (px, py) in enumerate(chips):
                sends.append(_rcopy(ssem, rsem, 6 * k + j, s_refs[k].at[pl.ds(c * halves[k], halves[k]), :],
                                    half(k, me, c), (px, py, c)))
        for j, (px, py) in enumerate(chips):
            sends.append(_rcopy(ssem, rsem, 6 * n + j, small_ref, osmall.at[me], (px, py, c)))
        for cp in sends:
            cp.start()
        for k in range(n):
            for j, (px, py) in enumerate(chips):
                s = 2 * px + py
                _rcopy(ssem, rsem, 6 * k + j, half(k, s, c), half(k, s, c), (x, y, c)).wait_recv()
                fwd = _rcopy(ssem, rsem, 6 * k + 3 + j, half(k, s, c), half(k, s, c), (x, y, 1 - c))
                fwd.start()
                sends.append(fwd)
        for k in range(n):
            for j, (px, py) in enumerate(chips):
                s = 2 * px + py
                _rcopy(ssem, rsem, 6 * k + 3 + j, half(k, s, 1 - c), half(k, s, 1 - c), (x, y, c)).wait_recv()
        for j, (px, py) in enumerate(chips):
            s = 2 * px + py
            _rcopy(ssem, rsem, 6 * n + j, osmall.at[s], osmall.at[s], (x, y, c)).wait_recv()
        for cp in sends:
            cp.wait_send()
        loc.wait()

    res = pl.pallas_call(
        body, name="gather_weights", interpret=False,
        out_shape=[jax.ShapeDtypeStruct((4,) + t.shape, t.dtype) for t in shards]
        + [jax.ShapeDtypeStruct((4, SW_ROWS, 1024), F32)],
        in_specs=[ANY] * (n + 1), out_specs=[ANY] * (n + 1),
        scratch_shapes=[pltpu.SemaphoreType.DMA((6 * n + 3,)), pltpu.SemaphoreType.DMA((6 * n + 3,)),
                        pltpu.SemaphoreType.DMA],
    )(*shards, small)
    return res[:n], res[n]


def _handshake(peers):
    barrier = pltpu.get_barrier_semaphore()
    for peer in peers:
        pl.semaphore_signal(barrier, inc=1, device_id=peer, device_id_type=MESH)
    pl.semaphore_wait(barrier, len(peers))


def gather_weights_beside(shards, cid, name):
    n = len(shards)
    halves = [t.shape[0] // 2 for t in shards]

    def body(*refs):
        s_refs, o_refs, ssem, rsem = refs[:n], refs[n:2 * n], refs[2 * n], refs[2 * n + 1]
        x, y, c = _place()
        me = 2 * x + y
        chips = [(1 - x, y), (x, 1 - y), (1 - x, 1 - y)]
        _handshake([(px, py, c) for px, py in chips] + [(x, y, 1 - c)])

        def half(k, s, hh):
            return o_refs[k].at[s, pl.ds(hh * halves[k], halves[k]), :]

        sends = []
        for k in range(n):
            for j, (px, py) in enumerate(chips):
                sends.append(_rcopy(ssem, rsem, 6 * k + j, s_refs[k].at[pl.ds(c * halves[k], halves[k]), :],
                                    half(k, me, c), (px, py, c)))
        for cp in sends:
            cp.start()
        for k in range(n):
            for j, (px, py) in enumerate(chips):
                s = 2 * px + py
                _rcopy(ssem, rsem, 6 * k + j, half(k, s, c), half(k, s, c), (x, y, c)).wait_recv()
                fwd = _rcopy(ssem, rsem, 6 * k + 3 + j, half(k, s, c), half(k, s, c), (x, y, 1 - c))
                fwd.start()
                sends.append(fwd)
        for k in range(n):
            for j, (px, py) in enumerate(chips):
                s = 2 * px + py
                _rcopy(ssem, rsem, 6 * k + 3 + j, half(k, s, 1 - c), half(k, s, 1 - c), (x, y, c)).wait_recv()
        for cp in sends:
            cp.wait_send()

    return pl.kernel(
        body, name=name,
        out_type=[jax.ShapeDtypeStruct((4,) + t.shape, t.dtype) for t in shards],
        mesh=plsc.ScalarSubcoreMesh(axis_name="sequencer", num_cores=1),
        scratch_types=[pltpu.SemaphoreType.DMA((6 * n,)), pltpu.SemaphoreType.DMA((6 * n,))],
        compiler_params=pltpu.CompilerParams(collective_id=cid),
    )(*shards)


def swap_halves(gs, *, name):
    n = len(gs)

    def body(*refs):
        g_refs, o_refs, ssem, rsem = refs[:n], refs[n:2 * n], refs[2 * n], refs[2 * n + 1]
        x, y, c = _place()
        cps = []
        for k in range(n):
            hk = g_refs[k].shape[1] // 2
            cps.append(_rcopy(ssem, rsem, k, g_refs[k].at[:, pl.ds((1 - c) * hk, hk), :], o_refs[k], (x, y, 1 - c)))
        for cp in cps:
            cp.start()
        for cp in cps:
            cp.wait()

    return pl.pallas_call(
        body, name=name, interpret=False,
        out_shape=[jax.ShapeDtypeStruct((4, t.shape[1] // 2, t.shape[2]), t.dtype) for t in gs],
        in_specs=[ANY] * n, out_specs=[ANY] * n,
        scratch_shapes=[pltpu.SemaphoreType.DMA((n,)), pltpu.SemaphoreType.DMA((n,))],
    )(*gs)


def _sum_rows(hk):
    return _pick(hk, (512, 352, 256, 128))


def pair_sum(g, other, c_idx, *, name):
    _, hk, width = other.shape
    tr = _sum_rows(hk)
    nbk = hk // tr

    def body(c_ref, g_ref, o_ref, out_ref):
        out_ref[...] = (g_ref[...].astype(F32) + o_ref[...].astype(F32)).astype(BF16)

    return pl.pallas_call(
        body, name=name, interpret=False,
        out_shape=jax.ShapeDtypeStruct((4, hk, width), BF16),
        grid_spec=pltpu.PrefetchScalarGridSpec(
            num_scalar_prefetch=1, grid=(4, nbk),
            in_specs=[pl.BlockSpec((1, tr, width), lambda s, i, c_ref: (s, c_ref[0] * nbk + i, 0)),
                      pl.BlockSpec((1, tr, width), lambda s, i, c_ref: (s, i, 0))],
            out_specs=pl.BlockSpec((1, tr, width), lambda s, i, c_ref: (s, i, 0))),
        compiler_params=_params(("parallel", "parallel")),
    )(c_idx, g, other)


def chip_sum(p, got, idx, *, name):
    _, hk, width = got.shape
    tr = _sum_rows(hk)
    nbk = hk // tr

    def body(idx_ref, p_ref, g_ref, out_ref):
        acc = p_ref[0].astype(F32)
        for j in range(3):
            acc = acc + g_ref[j].astype(F32)
        out_ref[0] = acc

    return pl.pallas_call(
        body, name=name, interpret=False,
        out_shape=jax.ShapeDtypeStruct((2, hk, width), F32),
        grid_spec=pltpu.PrefetchScalarGridSpec(
            num_scalar_prefetch=1, grid=(nbk,),
            in_specs=[pl.BlockSpec((1, tr, width), lambda i, idx_ref: (idx_ref[0], i, 0)),
                      pl.BlockSpec((3, tr, width), lambda i, idx_ref: (0, i, 0))],
            out_specs=pl.BlockSpec((1, tr, width), lambda i, idx_ref: (idx_ref[1], i, 0))),
        compiler_params=_params(("parallel",)),
    )(idx, p, got)


def join_halves(qs):
    n = len(qs)

    def body(*refs):
        q_refs, o_refs, ssem, rsem = refs[:n], refs[n:2 * n], refs[2 * n], refs[2 * n + 1]
        x, y, c = _place()
        cps = [_rcopy(ssem, rsem, k, q_refs[k].at[c], o_refs[k].at[c], (x, y, 1 - c)) for k in range(n)]
        for cp in cps:
            cp.start()
        for k in range(n):
            _rcopy(ssem, rsem, k, q_refs[k].at[c], o_refs[k].at[1 - c], (x, y, 1 - c)).wait_recv()
        for cp in cps:
            cp.wait_send()

    return pl.pallas_call(
        body, name="join_halves", interpret=False,
        out_shape=[jax.ShapeDtypeStruct(t.shape, t.dtype) for t in qs],
        in_specs=[ANY] * n, out_specs=[ANY] * n, input_output_aliases={k: k for k in range(n)},
        scratch_shapes=[pltpu.SemaphoreType.DMA((n,)), pltpu.SemaphoreType.DMA((n,))],
    )(*qs)


def scatter_chips_beside(ps, cid, name):
    n = len(ps)

    def body(*refs):
        p_refs, o_refs, ssem, rsem = refs[:n], refs[n:2 * n], refs[2 * n], refs[2 * n + 1]
        x, y, c = _place()
        chips = [(1 - x, y), (x, 1 - y), (1 - x, 1 - y)]
        _handshake([(px, py, c) for px, py in chips])
        cps = [_rcopy(ssem, rsem, 3 * k + j, p_refs[k].at[2 * px + py], o_refs[k].at[j], (px, py, c))
               for k in range(n) for j, (px, py) in enumerate(chips)]
        for cp in cps:
            cp.start()
        for cp in cps:
            cp.wait()

    return pl.kernel(
        body, name=name, out_type=[jax.ShapeDtypeStruct((3,) + t.shape[1:], t.dtype) for t in ps],
        mesh=plsc.ScalarSubcoreMesh(axis_name="sequencer", num_cores=1),
        scratch_types=[pltpu.SemaphoreType.DMA((3 * n,)), pltpu.SemaphoreType.DMA((3 * n,))],
        compiler_params=pltpu.CompilerParams(collective_id=cid),
    )(*ps)


def reduce_begin(gs, names, c_idx, cid, tag):
    others = swap_halves(gs, name=f"swap_halves_{tag}")
    pairs = [pair_sum(g, o, c_idx, name=f"pair_sum_{nm}") for g, o, nm in zip(gs, others, names)]
    return pairs, scatter_chips_beside(pairs, cid, f"scatter_chips_{tag}")


def reduce_end(pairs, gots, names, idx):
    mine = [chip_sum(p, g, idx, name=f"chip_sum_{nm}") for p, g, nm in zip(pairs, gots, names)]
    return [q.reshape(2 * q.shape[1], q.shape[2]) for q in join_halves(mine)]


def gather_small(v):
    def body(v_ref, o_ref, ssem, rsem, lsem):
        x, y, c = _place()
        peers = []
        for k in range(1, 8):
            fx, fy, fc = (k >> 2) & 1, (k >> 1) & 1, k & 1
            peers.append((1 - x if fx else x, 1 - y if fy else y, 1 - c if fc else c))
        _handshake(peers)
        loc = pltpu.make_async_copy(v_ref, o_ref.at[4 * x + 2 * y + c], lsem)
        loc.start()
        cps = []
        for k, (px, py, pc) in enumerate(peers):
            cps.append((pltpu.make_async_remote_copy(
                src_ref=v_ref, dst_ref=o_ref.at[4 * x + 2 * y + c], send_sem=ssem.at[k], recv_sem=rsem.at[k],
                device_id=(px, py, pc), device_id_type=MESH), 4 * px + 2 * py + pc))
        for cp, _ in cps:
            cp.start()
        for k, (cp, peer) in enumerate(cps):
            pltpu.make_async_remote_copy(
                src_ref=v_ref, dst_ref=o_ref.at[peer], send_sem=ssem.at[k], recv_sem=rsem.at[k],
                device_id=(x, y, c), device_id_type=MESH).wait_recv()
        for cp, _ in cps:
            cp.wait_send()
        loc.wait()

    return pl.kernel(
        body, name="gather_small", out_type=jax.ShapeDtypeStruct((8, SV_ROWS, 1024), F32),
        mesh=plsc.ScalarSubcoreMesh(axis_name="sequencer", num_cores=1),
        scratch_types=[pltpu.SemaphoreType.DMA((7,)), pltpu.SemaphoreType.DMA((7,)), pltpu.SemaphoreType.DMA],
        compiler_params=pltpu.CompilerParams(collective_id=6),
    )(v)


def sum_slots(a):
    def fn(i, t):
        acc = t[0]
        for k in range(1, 8):
            acc = acc + t[k]
        return acc

    return rowwise(fn, [whole(a)], [((SV_ROWS, 1024), F32, (SV_ROWS, 1024), lambda i: (0, 0), "w")], steps=1,
                   name="sum_slots")[0]


def _head_rms(x, nw):
    xs, rs = [], []
    for h in range(DN_H):
        xh = x[:, h * DN_D:(h + 1) * DN_D]
        r = lax.rsqrt(jnp.mean(xh * xh, axis=1, keepdims=True) + EPS)
        xs.append(xh * r)
        rs.append(r)
    return xs, rs


def bg_fwd(p, alog, dtb):
    rows = p.shape[0]
    tr = _pick(rows, (384, 128))

    def fn(i, x, al, dt):
        lane = lax.broadcasted_iota(jnp.int32, x.shape, 1)
        row = i + lax.broadcasted_iota(jnp.int32, x.shape, 0)
        g = -jnp.exp(al) * _softplus(x + dt)
        out = jnp.where(lane < 4, _sigmoid(x), jnp.where(lane < 8, g, 0.0))
        return jnp.where(row >= PAD, out, 0.0)

    return rowwise(fn, [cols(p, tr, 128, BG0 // 128), whole(alog), whole(dtb)], [out2d(rows, 128, F32, tr)],
                   steps=rows // tr, name="bg_fwd")[0]


def bg_bwd(p, alog, dtb, dbg):
    rows = p.shape[0]
    tr = _pick(rows, (384, 128))

    def fn(i, x, al, dt, g_in):
        lane = lax.broadcasted_iota(jnp.int32, x.shape, 1)
        row = i + lax.broadcasted_iota(jnp.int32, x.shape, 0)
        live = row >= PAD
        is_b = jnp.logical_and(live, lane < 4)
        is_g = jnp.logical_and(live, jnp.logical_and(lane >= 4, lane < 8))
        beta = _sigmoid(x)
        ea = jnp.exp(al)
        g = -ea * _softplus(x + dt)
        dalpha = jnp.where(is_g, g_in * (-ea) * _sigmoid(x + dt), 0.0)
        dx = jnp.where(is_b, g_in * beta * (1.0 - beta), dalpha)
        dal = jnp.sum(jnp.where(is_g, g_in * g, 0.0), axis=0, keepdims=True)
        return jnp.concatenate([dx, jnp.zeros(x.shape, F32)], axis=1), dal, jnp.sum(dalpha, axis=0, keepdims=True)

    return rowwise(fn, [cols(p, tr, 128, BG0 // 128), whole(alog), whole(dtb), cols(dbg, tr)],
                   [out2d(rows, 256, BF16, tr)], steps=rows // tr, name="bg_bwd",
                   accs=[((1, 128), F32), ((1, 128), F32)])


def dn_qkv_post(j, y):
    xs = _silu(y)
    sc = jnp.where(j == 0, DN_D ** -0.5, 1.0)
    outs = []
    for h in range(DN_H):
        xh = xs[:, h * DN_D:(h + 1) * DN_D]
        r = lax.rsqrt(jnp.sum(xh * xh, axis=1, keepdims=True) + EPS)
        outs.append(jnp.where(j < 2, xh * r * sc, xh))
    return jnp.concatenate(outs, axis=1), y


def dn_qkv_bwd(cq, dq, dk, dv):
    rows = cq.shape[0]
    tr = _pick(rows, (384, 128))

    def fn(i, c0, c1, c2, g0, g1, g2):
        pieces = []
        for kind, (cv, g) in enumerate(((c0, g0), (c1, g1), (c2, g2))):
            xs = _silu(cv)
            if kind < 2:
                sc = DN_D ** -0.5 if kind == 0 else 1.0
                ds = []
                for h in range(DN_H):
                    sl = slice(h * DN_D, (h + 1) * DN_D)
                    xh, gh = xs[:, sl], g[:, sl]
                    r = lax.rsqrt(jnp.sum(xh * xh, axis=1, keepdims=True) + EPS)
                    xn = xh * r
                    ds.append(sc * r * (gh - xn * jnp.sum(gh * xn, axis=1, keepdims=True)))
                dxs = jnp.concatenate(ds, axis=1)
            else:
                dxs = g
            pieces.append(dxs * _dsilu(cv))
        return jnp.concatenate(pieces, axis=1)

    ins = [cols(cq, tr, DN_DIM, k) for k in range(3)] + [cols(t, tr) for t in (dq, dk, dv)]
    return rowwise(fn, ins, [out2d(rows, 3 * DN_DIM, F32, tr)], steps=rows // tr, name="dn_qkv_bwd")[0]


def dn_out_fwd(o, p, nw):
    rows = o.shape[0]
    tr = _pick(rows, (384, 128))

    def fn(i, ov, z, w):
        xs, _ = _head_rms(ov, w)
        return jnp.concatenate(xs, axis=1) * jnp.concatenate([w] * DN_H, axis=1) * _silu(z)

    return rowwise(fn, [cols(o, tr), cols(p, tr, DN_DIM, 6), whole(nw)], [out2d(rows, DN_DIM, BF16, tr)],
                   steps=rows // tr, name="dn_out_fwd")[0]


def dn_out_bwd(o, p, nw, dymix):
    rows = o.shape[0]
    tr = _pick(rows, (384, 128))

    def fn(i, ov, z, w, dy):
        xs, rs = _head_rms(ov, w)
        sz = _silu(z)
        dn = dy * sz
        dos, dw = [], jnp.zeros((1, DN_D), F32)
        for h in range(DN_H):
            sl = slice(h * DN_D, (h + 1) * DN_D)
            gw = dn[:, sl] * w
            dos.append(rs[h] * (gw - xs[h] * jnp.mean(gw * xs[h], axis=1, keepdims=True)))
            dw = dw + jnp.sum(dn[:, sl] * xs[h], axis=0, keepdims=True)
        n = jnp.concatenate(xs, axis=1) * jnp.concatenate([w] * DN_H, axis=1)
        return jnp.concatenate(dos, axis=1), dy * n * _dsilu(z), dw

    return rowwise(fn, [cols(o, tr), cols(p, tr, DN_DIM, 6), whole(nw), cols(dymix, tr, DN_DIM, 1)],
                   [out2d(rows, DN_DIM, F32, tr), out2d(rows, DN_DIM, BF16, tr)], steps=rows // tr,
                   name="dn_out_bwd", accs=[((1, DN_D), F32)])


def conv_a_pre_bwd(dymix, cv, p):
    rows = cv.shape[0]
    tr = _pick(rows, (384, 128))

    def fn(i, dy, c, go):
        return dy * c, dy * go

    return rowwise(fn, [cols(dymix, tr, D_CONV, 0), cols(cv, tr), cols(p, tr, D_CONV, 1)],
                   [out2d(rows, D_CONV, BF16, tr), out2d(rows, D_CONV, F32, tr)], steps=rows // tr,
                   name="conv_a_pre_bwd")


def _rows8(w):
    return jnp.pad(w.astype(F32), ((0, 8 - w.shape[0]), (0, 0)))


def _lanes(v, at):
    return jnp.pad(v.astype(F32), (at, 128 - at - v.shape[0]))[None]


def add_norm(a, w, h, next_nw, *, name):
    return mm(a, w, name=name, epi=_add_norm_epi, epi_ins=[(h, lambda j: 0)], epi_consts=[next_nw],
              epi_outs=[F32, BF16])


def _add_norm_epi(row0, t, h, nw):
    x = t + h
    return x, x * lax.rsqrt(jnp.mean(x * x, axis=1, keepdims=True) + EPS) * nw


def ffn_up_conv(hn, w_up, cw8, *, name):
    rows = hn.shape[0]
    tn = w_up.shape[2]
    tm = _pick(rows, (384, 128))
    nr = rows // tm

    def body(x_ref, wg_ref, wv_ref, w_ref, ug_ref, uv_ref, gc_ref, a_ref, carry, scr):
        i = pl.program_id(1)
        x = x_ref[...]
        gate = _dot(x, wg_ref[...])
        val = _dot(x, wv_ref[...])
        ug_ref[...] = gate.astype(BF16)
        uv_ref[...] = val.astype(BF16)
        scr[0:8, :] = jnp.where(i > 0, carry[...], 0.0)
        scr[8:8 + tm, :] = gate
        carry[...] = gate[tm - 8:tm]
        y = jnp.zeros((tm, tn), F32)
        for q in range(3):
            sh = 2 - q
            y = y + w_ref[q:q + 1, :] * scr[8 - sh:8 - sh + tm, :]
        gc_ref[...] = y.astype(BF16)
        a_ref[...] = (_silu(y) * val).astype(BF16)

    half = pl.BlockSpec((tm, tn), lambda j, i: (i, j))
    return pl.pallas_call(
        body, name=name, interpret=False,
        out_shape=[jax.ShapeDtypeStruct((rows, D_FF), BF16)] * 4,
        grid=(D_FF // tn, nr),
        in_specs=[pl.BlockSpec((tm, D), lambda j, i: (i, 0)),
                  pl.BlockSpec((None, D, tn), lambda j, i: (j, 0, 0)),
                  pl.BlockSpec((None, D, tn), lambda j, i: (j + D_FF // tn, 0, 0)),
                  pl.BlockSpec((8, tn), lambda j, i: (0, j))],
        out_specs=[half] * 4,
        scratch_shapes=[pltpu.VMEM((8, tn), F32), pltpu.VMEM((tm + 8, tn), F32)],
        compiler_params=_params(("arbitrary", "arbitrary")),
    )(hn, w_up, w_up, cw8)


def ffn_down_bwd(dh, w_down, gc, uv, ug, cw8, *, name):
    rows = dh.shape[0]
    tn = D_FF // 2
    tm = _pick(rows, (384, 128))
    nr = rows // tm
    r8 = tm // 8

    def body(dh_ref, w_ref, gc_ref, uv_ref, ug_ref, halo_ref, cw_ref, du_ref, dw_ref, carry, gscr, xscr):
        ip = pl.program_id(1)
        i = nr - 1 - ip
        da = _dot(dh_ref[...].astype(BF16), w_ref[...], 1, 1)
        c, val = gc_ref[...].astype(F32), uv_ref[...].astype(F32)
        dgc = da * val * _dsilu(c)
        du_ref[:, tn:] = (da * _silu(c)).astype(BF16)
        gscr[0:tm, :] = dgc
        gscr[tm:tm + 8, :] = jnp.where(ip > 0, carry[...], 0.0)
        carry[...] = dgc[0:8]
        xscr[0:8, :] = jnp.where(i > 0, halo_ref[...].astype(F32), 0.0)
        xscr[8:8 + tm, :] = ug_ref[...].astype(F32)
        dx = jnp.zeros((tm, tn), F32)
        dws = []
        for q in range(3):
            sh = 2 - q
            dx = dx + cw_ref[q:q + 1, :] * gscr[sh:sh + tm, :]
            dws.append(jnp.sum(dgc * xscr[8 - sh:8 - sh + tm, :], axis=0, keepdims=True))
        du_ref[:, :tn] = dx.astype(BF16)

        @pl.when(ip == 0)
        def _():
            dw_ref[...] = jnp.zeros((8, tn), F32)

        dw_ref[...] += jnp.concatenate(dws + [jnp.zeros((5, tn), F32)], axis=0)

    rev = lambda ip: nr - 1 - ip
    tile = lambda arr: pl.BlockSpec((tm, tn), lambda j, ip: (rev(ip), j))
    return pl.pallas_call(
        body, name=name, interpret=False,
        out_shape=[jax.ShapeDtypeStruct((rows, 2 * D_FF), BF16), jax.ShapeDtypeStruct((8, D_FF), F32)],
        grid=(2, nr),
        in_specs=[pl.BlockSpec((tm, D), lambda j, ip: (rev(ip), 0)),
                  pl.BlockSpec((tn, D), lambda j, ip: (j, 0)),
                  tile(gc), tile(uv), tile(ug),
                  pl.BlockSpec((8, tn), lambda j, ip: (jnp.maximum(rev(ip) * r8 - 1, 0), j)),
                  pl.BlockSpec((8, tn), lambda j, ip: (0, j))],
        out_specs=[pl.BlockSpec((tm, 2 * tn), lambda j, ip: (rev(ip), j)),
                   pl.BlockSpec((8, tn), lambda j, ip: (0, j))],
        scratch_shapes=[pltpu.VMEM((8, tn), F32), pltpu.VMEM((tm + 8, tn), F32), pltpu.VMEM((tm + 8, tn), F32)],
        compiler_params=_params(("arbitrary", "arbitrary")),
    )(dh, w_down, gc, uv, ug, ug, cw8)


def ffn_fwd(h, hn, w_up, cw8, w_down, tag, next_nw=None, target=None):
    ug, uv, gc, a = ffn_up_conv(hn, w_up, cw8, name=f"ffn{tag}_up")
    if target is not None:
        out, hn_next = add_loss(a, w_down, h, target, name=f"ffn{tag}_down")
    else:
        out, hn_next = add_norm(a, w_down, h, next_nw, name=f"ffn{tag}_down")
    return out, hn_next, (hn, ug, uv, a, gc)


def ffn_bwd(h, nw, w_up, cw8, w_down, saved, dh, tag):
    hn, ug, uv, a, gc = saved
    du, d_cw = ffn_down_bwd(dh, w_down, gc, uv, ug, cw8, name=f"ffn{tag}_down_dx")
    d_w_down = mm(a, dh, ta=True, out_dtype=BF16, name=f"ffn{tag}_down_dw")
    dh_new, d_nw = dx_rms_bwd(du, w_up, h, nw, dh, name=f"ffn{tag}_up_dx", b_chip=True, swap_mid=True)
    d_w_up = mm(hn, du, ta=True, out_dtype=BF16, out_chip=True, swap_mid=True, name=f"ffn{tag}_up_dw")
    return dh_new, d_nw, d_w_up, d_cw, d_w_down


def mixer_fwd(h, nw, w_in, ca8, dc8, alog, dtb, dnw, w_out, tie=None, next_nw=None):
    rows = h.shape[0]
    tr = _pick(rows, (384, 128))
    hn = rms_fwd(h, nw, name="mix_norm")
    if callable(w_in):
        hn, w_in = w_in(hn)
    p = mm(hn, w_in, name="mix_in")
    y_a, cv = conv_fwd([(p, 0), (p, 2)], ca8, 3, rows=rows, c=D_CONV, tc=D_CONV, tr=tr, name="conv_a",
                       pre=lambda gi, ah: gi * ah, post=lambda j, y, go: (go * y, y), extras=[(p, 1)],
                       outs=[BF16, F32])
    qkv_n, cq = conv_fwd([(p, 3)], dc8, 4, rows=rows, c=3 * DN_DIM, tc=DN_DIM, tr=tr, name="dn_conv",
                         post=dn_qkv_post, outs=[F32, F32], strip=tr)
    bgcol = bg_fwd(p, alog, dtb)
    if tie is not None:
        bgcol = tie(bgcol)
    bgrow = bgcol[:, :8].reshape(rows // CH, CH, 8).transpose(0, 2, 1)
    o, s_all, ti_all = dn_fwd(qkv_n, bgcol, bgrow)
    y_b = dn_out_fwd(o, p, dnw)
    ymix = jnp.concatenate([y_a, y_b], axis=1)
    w_out = w_out() if callable(w_out) else w_out
    out, hn_next = add_norm(ymix, w_out, h, next_nw, name="mix_out")
    return out, hn_next, (hn, p, cv, qkv_n, cq, bgcol, bgrow, o, s_all, ti_all, ymix, w_in)


def mixer_bwd(h, nw, ca8, dc8, alog, dtb, dnw, w_out, saved, dh):
    hn, p, cv, qkv_n, cq, bgcol, bgrow, o, s_all, ti_all, ymix, w_in = saved
    rows = h.shape[0]
    tr = _pick(rows, (384, 128))
    dymix = mm(dh, w_out, tb=True, name="mix_out_dx")
    d_w_out = mm(ymix, dh, ta=True, out_dtype=BF16, name="mix_out_dw")
    do, dz, d_dnw = dn_out_bwd(o, p, dnw, dymix)
    dq, dk, dv, dbg = dn_bwd(qkv_n, bgcol, bgrow, s_all, ti_all, do)
    dbg_p, d_alog, d_dtb = bg_bwd(p, alog, dtb, dbg)
    dcq = dn_qkv_bwd(cq, dq, dk, dv)
    dqkv, d_dc = conv_bwd([(p, 3)], dc8, 4, dcq, rows=rows, c=3 * DN_DIM, tc=DN_DIM, tr=tr, name="dn_conv_bwd",
                          post=lambda dx: dx, outs=[BF16])
    dgo, dcv = conv_a_pre_bwd(dymix, cv, p)
    dgi, dah, d_ca = conv_bwd([(p, 0), (p, 2)], ca8, 3, dcv, rows=rows, c=D_CONV, tc=D_CONV, tr=tr,
                              name="conv_a_bwd", pre=lambda gi, ah: gi * ah,
                              post=lambda dm, gi, ah: (dm * ah, dm * gi), extras=[(p, 0), (p, 2)], outs=[BF16, BF16])
    dp = jnp.concatenate([dgi, dgo, dah, dqkv, dz, dbg_p], axis=1)
    dh_new, d_nw = dx_rms_bwd(dp, w_in, h, nw, dh, name="mix_in_dx")
    d_w_in = mm(hn, dp, ta=True, out_dtype=BF16, name="mix_in_dw")
    return dh_new, d_nw, d_w_in, d_ca, d_dc, d_alog, d_dtb, d_dnw, d_w_out


def swa_layer_fwd(h, hn, wqkv, qw, kw, sinks, wo, next_nw):
    qkv = mm(hn, wqkv, name="swa_qkv")
    qh, kh, vh = qknorm_fwd(qkv, qw, kw)
    att = swa_fwd(qh, kh, vh, sinks)
    out, hn_next = add_norm(att, wo, h, next_nw, name="swa_out")
    return out, hn_next, (hn, qkv, qh, kh, vh, att)


def swa_layer_bwd(h, nw, wqkv, qw, kw, sinks, wo, saved, dh):
    hn, qkv, qh, kh, vh, att = saved
    datt = mm(dh, wo, tb=True, out_dtype=BF16, name="swa_out_dx")
    d_wo = mm(att, dh, ta=True, out_dtype=BF16, name="swa_out_dw")
    dqh, dkh, dvh, dsk = swa_bwd(qh, kh, vh, sinks, datt)
    dqkv, d_qw, d_kw = qknorm_bwd(qkv, qw, kw, dqh, dkh, dvh)
    dh_new, d_nw = dx_rms_bwd(dqkv, wqkv, h, nw, dh, name="swa_qkv_dx")
    d_wqkv = mm(hn, dqkv, ta=True, out_dtype=BF16, name="swa_qkv_dw")
    d_sinks = jnp.sum(dsk[:, :, 0], axis=0)
    return dh_new, d_nw, d_wqkv, d_qw, d_kw, d_sinks, d_wo


BIG = ("mix_w_in", "mix_w_out", "swa_wq", "swa_wk", "swa_wv", "swa_wo", "ffn_w_up", "ffn_w_down")


def _flat_pad(parts, rows):
    v = jnp.concatenate([t.astype(F32).reshape(-1) for t in parts])
    return jnp.pad(v, (0, rows * 1024 - v.shape[0])).reshape(rows, 1024)


def _split_flat(flat, shapes):
    v = flat.reshape(-1)
    out, o = [], 0
    for s in shapes:
        n = 1
        for d_ in s:
            n *= d_
        out.append(v[o:o + n].reshape(s))
        o += n
    return out


def local_step(x0, target0, meta_full, anw, fnw, w_in, ca8, dc8, alog, dtb, dnw, qw, kw, sinks, fc8, late,
               begin=None, tie=None):
    begin = begin or (lambda tag, names, grads: None)
    h0 = jnp.concatenate([jnp.zeros((PAD, D), F32), meta_full, x0], axis=0)
    h1, hn1, s_mix = mixer_fwd(h0, anw[0], w_in, ca8, dc8, alog, dtb, dnw, lambda: late()[0], tie, fnw[0])
    w_out, wqkv, wo, w_up, w_down = late()
    h2, hn2, s_f0 = ffn_fwd(h1, hn1, w_up[0], fc8[0], w_down[0], 0, anw[1])
    h3, hn3, s_swa = swa_layer_fwd(h2, hn2, wqkv, qw, kw, sinks, wo, fnw[1])
    dh, loss_l, s_f1 = ffn_fwd(h3, hn3, w_up[1], fc8[1], w_down[1], 1,
                               target=jnp.pad(target0, ((HEAD0, 0), (0, 0))))
    dh, d_fnw1, d_up1, d_fc1, d_down1 = ffn_bwd(h3, fnw[1], w_up[1], fc8[1], w_down[1], s_f1, dh, 1)
    begin("ffn1", ("up1", "down1"), [d_up1, d_down1.reshape(4, 704, D)])
    dh, d_anw1, d_wqkv, d_qw, d_kw, d_sinks, d_wo = swa_layer_bwd(h2, anw[1], wqkv, qw, kw, sinks, wo, s_swa, dh)
    begin("swa", ("wq", "wk", "wv", "wo"),
          [d_wqkv[:, :D].reshape(4, 256, D), d_wqkv[:, D:D + 256].reshape(4, 256, 256),
           d_wqkv[:, D + 256:].reshape(4, 256, 256), d_wo.reshape(4, 256, D)])
    dh, d_fnw0, d_up0, d_fc0, d_down0 = ffn_bwd(h1, fnw[0], w_up[0], fc8[0], w_down[0], s_f0, dh, 0)
    begin("ffn0", ("up0", "down0"), [d_up0, d_down0.reshape(4, 704, D)])
    dh, d_anw0, d_w_in, d_ca, d_dc, d_alog, d_dtb, d_dnw, d_w_out = mixer_bwd(
        h0, anw[0], ca8, dc8, alog, dtb, dnw, w_out, s_mix, dh)
    begin("mix", ("w_in", "w_out"),
          [d_w_in[:, :IN_DIM].reshape(D, 4, 898).transpose(1, 0, 2), d_w_out.reshape(4, 256, D)])
    return (dh, loss_l, d_anw0, d_anw1, d_fnw0, d_fnw1, d_w_in, d_ca, d_dc, d_alog, d_dtb, d_dnw, d_w_out, d_wqkv,
            d_qw, d_kw, d_sinks, d_wo, d_up0, d_up1, d_fc0, d_fc1, d_down0, d_down1)


def kernel(x, meta_tokens, attn_norm_w, ffn_norm_w, mix_w_in, conv_a_w, dn_conv_w, dn_a_log, dn_dt_bias, dn_norm_w, mix_w_out, swa_wq, swa_wk, swa_wv, swa_q_norm_w, swa_k_norm_w, swa_sinks, swa_wo, ffn_w_up, ffn_conv_w, ffn_w_down, loss_target, m_meta_tokens, m_attn_norm_w, m_ffn_norm_w, m_mix_w_in, m_conv_a_w, m_dn_conv_w, m_dn_a_log, m_dn_dt_bias, m_dn_norm_w, m_mix_w_out, m_swa_wq, m_swa_wk, m_swa_wv, m_swa_q_norm_w, m_swa_k_norm_w, m_swa_sinks, m_swa_wo, m_ffn_w_up, m_ffn_conv_w, m_ffn_w_down, v_meta_tokens, v_attn_norm_w, v_ffn_norm_w, v_mix_w_in, v_conv_a_w, v_dn_conv_w, v_dn_a_log, v_dn_dt_bias, v_dn_norm_w, v_mix_w_out, v_swa_wq, v_swa_wk, v_swa_wv, v_swa_q_norm_w, v_swa_k_norm_w, v_swa_sinks, v_swa_wo, v_ffn_w_up, v_ffn_conv_w, v_ffn_w_down):
    ix, iy, ic = lax.axis_index("x"), lax.axis_index("y"), lax.axis_index("c")
    chip = 2 * ix + iy
    seq = x.shape[1]
    rows = HEAD0 + seq

    small_sharded = (conv_a_w, dn_conv_w, ffn_conv_w, meta_tokens)
    up_b, down_b = ffn_w_up.astype(BF16), ffn_w_down.astype(BF16)
    own = [mix_w_in[0].astype(BF16), mix_w_out[0].astype(BF16), swa_wq[0].astype(BF16), swa_wk[0].astype(BF16),
           swa_wv[0].astype(BF16), swa_wo[0].astype(BF16), up_b[0], up_b[1], down_b[0], down_b[1]]
    fill = lambda gathered, mine: [lax.dynamic_update_slice_in_dim(g, t[None], chip, axis=0)
                                   for g, t in zip(gathered, mine)]
    on_its_way, = gather_weights_beside(own[:1], 9, "gather_w_in")
    _, g_small = gather_weights([], _flat_pad(small_sharded, SW_ROWS))

    def w_in(hn):
        hn, got = lax.optimization_barrier((hn, on_its_way))
        g_in, = fill([got], own[:1])
        return hn, jnp.pad(g_in.transpose(1, 0, 2).reshape(D, IN_DIM), ((0, 0), (0, P_W - IN_DIM)))
    rest = {}

    def tie(t):
        t, *mine = lax.optimization_barrier((t, *own[1:]))
        g_out, g_q, g_k, g_v, g_o, g_up0, g_up1, g_dn0, g_dn1 = mine
        soon, last = [g_up0, g_dn0, g_q, g_k, g_v, g_o], [g_up1, g_dn1]
        rest["w_out"] = fill(gather_weights_beside([g_out], 1, "gather_w_out"), [g_out])
        rest["soon"] = fill(gather_weights_beside(soon, 7, "gather_layers_12"), soon)
        rest["last"] = fill(gather_weights_beside(last, 8, "gather_layer_3"), last)
        return t

    def late():
        (g_out,), (g_up0, g_dn0, g_q, g_k, g_v, g_o), (g_up1, g_dn1) = rest["w_out"], rest["soon"], rest["last"]
        wqkv = jnp.concatenate([g_q.reshape(D, D), g_k.reshape(D, 256), g_v.reshape(D, 256)], axis=1)
        return (g_out.reshape(D, D), wqkv, g_o.reshape(D, D), [g_up0, g_up1],
                [g_dn0.reshape(D_FF, D), g_dn1.reshape(D_FF, D)])

    gs = g_small.reshape(4, -1)
    ca_full = gs[:, 0:384].reshape(4, 3, 128).transpose(1, 0, 2).reshape(3, D_CONV)
    dc_full = gs[:, 384:1920].reshape(4, 4, 384).transpose(1, 0, 2).reshape(4, 3 * DN_DIM)
    fc_full = gs[:, 1920:6144].reshape(4, 2, 3, 704).transpose(1, 2, 0, 3).reshape(2, 3, D_FF)
    meta_full = gs[:, 6144:10240].reshape(4, N_META, 256).transpose(1, 0, 2).reshape(N_META, D)
    ca8, dc8 = _rows8(ca_full), _rows8(dc_full)
    fc8 = [_rows8(fc_full[0]), _rows8(fc_full[1])]
    alog, dtb = _lanes(dn_a_log[0], 4), _lanes(dn_dt_bias[0], 4)
    dnw = dn_norm_w.astype(F32)
    qw, kw = swa_q_norm_w.astype(F32), swa_k_norm_w.astype(F32)
    sinks = swa_sinks[0].astype(F32)
    anw = [attn_norm_w[0:1], attn_norm_w[1:2]]
    fnw = [ffn_norm_w[0:1], ffn_norm_w[1:2]]

    c_idx = jnp.reshape(ic, (1,)).astype(jnp.int32)
    chip_idx = jnp.stack([chip, ic]).astype(jnp.int32)
    begun = []

    def begin(tag, names, grads):
        pairs, gots = reduce_begin(grads, names, c_idx, 2 + len(begun), tag)
        begun.append((names, pairs, gots))

    (dh, loss_l, d_anw0, d_anw1, d_fnw0, d_fnw1, d_w_in, d_ca, d_dc, d_alog, d_dtb, d_dnw, d_w_out, d_wqkv, d_qw,
     d_kw, d_sinks, d_wo, d_up0, d_up1, d_fc0, d_fc1, d_down0, d_down1) = local_step(
        x[0], loss_target[0], meta_full, anw, fnw, w_in, ca8, dc8, alog, dtb, dnw, qw, kw, sinks, fc8, late,
        begin, tie)
    grad_x = dh[HEAD0:][None]

    small_parts = [jnp.concatenate([d_anw0, d_anw1], axis=0), jnp.concatenate([d_fnw0, d_fnw1], axis=0),
                   d_alog[0, 4:8], d_dtb[0, 4:8], d_dnw, d_qw, d_kw, d_sinks,
                   d_ca[:3], d_dc[:4], jnp.stack([d_fc0[:3], d_fc1[:3]]), dh[PAD:HEAD0], loss_l[0, 0:1]]
    small_shapes = [(2, D), (2, D), (1, 4), (1, 4), (1, DN_D), (1, SWA_D), (1, SWA_D), (1, SWA_H),
                    (1, 3, D_CONV), (1, 4, 3 * DN_DIM), (2, 3, D_FF), (N_META, D), ()]
    gathered_small = gather_small(_flat_pad(small_parts, SV_ROWS))

    red_big = {}
    for part in (begun[:-1], begun[-1:]):
        part_names = [n for names, _, _ in part for n in names]
        red_big.update(zip(part_names, reduce_end([p for _, ps, _ in part for p in ps],
                                                  [g for _, _, gs_ in part for g in gs_], part_names, chip_idx)))
    g_w_in, g_w_out, g_wq, g_wk, g_wv, g_wo, g_up0, g_up1, g_dn0, g_dn1 = [
        red_big[n] for n in ("w_in", "w_out", "wq", "wk", "wv", "wo", "up0", "up1", "down0", "down1")]

    grads = dict(mix_w_in=g_w_in, mix_w_out=g_w_out, swa_wq=g_wq, swa_wk=g_wk, swa_wv=g_wv, swa_wo=g_wo,
                 ffn_w_up=[g_up0, g_up1], ffn_w_down=[g_dn0, g_dn1])
    weights = dict(meta_tokens=meta_tokens, attn_norm_w=attn_norm_w, ffn_norm_w=ffn_norm_w, mix_w_in=mix_w_in,
                   conv_a_w=conv_a_w, dn_conv_w=dn_conv_w, dn_a_log=dn_a_log, dn_dt_bias=dn_dt_bias,
                   dn_norm_w=dn_norm_w, mix_w_out=mix_w_out, swa_wq=swa_wq, swa_wk=swa_wk, swa_wv=swa_wv,
                   swa_q_norm_w=swa_q_norm_w, swa_k_norm_w=swa_k_norm_w, swa_sinks=swa_sinks, swa_wo=swa_wo,
                   ffn_w_up=ffn_w_up, ffn_conv_w=ffn_conv_w, ffn_w_down=ffn_w_down)
    m_in = dict(meta_tokens=m_meta_tokens, attn_norm_w=m_attn_norm_w, ffn_norm_w=m_ffn_norm_w, mix_w_in=m_mix_w_in,
                conv_a_w=m_conv_a_w, dn_conv_w=m_dn_conv_w, dn_a_log=m_dn_a_log, dn_dt_bias=m_dn_dt_bias,
                dn_norm_w=m_dn_norm_w, mix_w_out=m_mix_w_out, swa_wq=m_swa_wq, swa_wk=m_swa_wk, swa_wv=m_swa_wv,
                swa_q_norm_w=m_swa_q_norm_w, swa_k_norm_w=m_swa_k_norm_w, swa_sinks=m_swa_sinks, swa_wo=m_swa_wo,
                ffn_w_up=m_ffn_w_up, ffn_conv_w=m_ffn_conv_w, ffn_w_down=m_ffn_w_down)
    v_in = dict(meta_tokens=v_meta_tokens, attn_norm_w=v_attn_norm_w, ffn_norm_w=v_ffn_norm_w, mix_w_in=v_mix_w_in,
                conv_a_w=v_conv_a_w, dn_conv_w=v_dn_conv_w, dn_a_log=v_dn_a_log, dn_dt_bias=v_dn_dt_bias,
                dn_norm_w=v_dn_norm_w, mix_w_out=v_mix_w_out, swa_wq=v_swa_wq, swa_wk=v_swa_wk, swa_wv=v_swa_wv,
                swa_q_norm_w=v_swa_q_norm_w, swa_k_norm_w=v_swa_k_norm_w, swa_sinks=v_swa_sinks, swa_wo=v_swa_wo,
                ffn_w_up=v_ffn_w_up, ffn_conv_w=v_ffn_conv_w, ffn_w_down=v_ffn_w_down)
    names = list(weights)
    small = [n for n in names if n not in BIG]
    delta, new_m, new_v = {}, {}, {}
    for n in BIG:
        delta[n], new_m[n], new_v[n], grads[n] = adamw(weights[n], grads[n], m_in[n], v_in[n], name=f"adamw_{n}")
    gathered_small, _ = lax.optimization_barrier((gathered_small, new_v["ffn_w_down"]))
    (g_anw, g_fnw, g_alog, g_dtb, g_dnw, g_qw, g_kw, g_sinks, g_ca_f, g_dc_f, g_fc_f, g_meta_f,
     loss) = _split_flat(sum_slots(gathered_small), small_shapes)
    grads.update(meta_tokens=lax.dynamic_slice_in_dim(g_meta_f, chip * 256, 256, axis=1), attn_norm_w=g_anw,
                 ffn_norm_w=g_fnw, conv_a_w=lax.dynamic_slice_in_dim(g_ca_f, chip * 128, 128, axis=2),
                 dn_conv_w=lax.dynamic_slice_in_dim(g_dc_f, chip * 384, 384, axis=2), dn_a_log=g_alog,
                 dn_dt_bias=g_dtb, dn_norm_w=g_dnw, swa_q_norm_w=g_qw, swa_k_norm_w=g_kw, swa_sinks=g_sinks,
                 ffn_conv_w=lax.dynamic_slice_in_dim(g_fc_f, chip * 704, 704, axis=2))
    grads = {n: grads[n].reshape(weights[n].shape) for n in names}
    shapes = [weights[n].shape for n in small]
    packed = [_flat_pad([t[n] for n in small], SW_ROWS) for t in (weights, grads, m_in, v_in)]
    for store, flat in zip((delta, new_m, new_v), adamw(*packed, name="adamw_small")):
        for n, t in zip(small, _split_flat(flat, shapes)):
            store[n] = t
    return (loss, grad_x, *[grads[n] for n in names], *[delta[n] for n in names],
            *[new_m[n] for n in names], *[new_v[n] for n in names])
```

```python
import functools

import jax
import jax.numpy as jnp
from jax import lax
from jax.experimental import pallas as pl
from jax.experimental.pallas import tpu as pltpu
from jax.experimental.pallas import tpu_sc as plsc

F32 = jnp.float32
BF16 = jnp.bfloat16
HI = lax.Precision.HIGHEST
MESH = pl.DeviceIdType.MESH

D = 1024
N_META = 16
PAD = 112
HEAD0 = PAD + N_META
D_CONV = 512
DN_H = 4
DN_D = 128
DN_DIM = 512
CH = 64
IN_DIM = 3592
P_W = 3840
BG0 = 3584
SWA_H = 16
SWA_KV = 4
SWA_D = 64
BLK = 128
NKEY = N_META + 2 * BLK
D_FF = 2816
EPS = 1e-6
LR, B1, B2, AEPS, WD, STEP = 0.001, 0.9, 0.999, 1e-08, 0.01, 10
VMEM_LIMIT = 48 * 1024 * 1024
MM_VMEM_BUDGET = 34 * 1024 * 1024
R_BIG = 6144
R_HALF = R_BIG // 2
SV_ROWS = 48
SW_ROWS = 16


def _pick(n, cands):
    for c in cands:
        if n % c == 0:
            return c
    return n


def _params(sem=None):
    return pltpu.CompilerParams(dimension_semantics=sem, vmem_limit_bytes=VMEM_LIMIT)


def _dot(a, b, ca=1, cb=0, prec=None):
    return lax.dot_general(a, b, (((ca,), (cb,)), ((), ())), precision=prec,
                           preferred_element_type=F32)


def _sigmoid(x):
    return 1.0 / (1.0 + jnp.exp(-x))


def _silu(x):
    return x * _sigmoid(x)


def _dsilu(x):
    s = _sigmoid(x)
    return s * (1.0 + x * (1.0 - s))


def _softplus(x):
    return jnp.maximum(x, 0.0) + jnp.log(1.0 + jnp.exp(-jnp.abs(x)))


def mm(a, b, *, name, ta=False, tb=False, out_dtype=F32, add=None, tm=None, tn=None, tk=None,
       b_chip=False, out_chip=False, swap_mid=False, epi=None, epi_ins=(), epi_consts=(), epi_outs=(), epi_accs=()):
    if epi is not None:
        return _mm_epi(a, b, name=name, tb=tb, tn=tn, b_chip=b_chip, swap_mid=swap_mid, epi=epi, epi_ins=epi_ins,
                       epi_consts=epi_consts, epi_outs=epi_outs, epi_accs=epi_accs)
    chip_of = _chip_order(swap_mid)
    m, k = (a.shape[1], a.shape[0]) if ta else a.shape
    if b_chip:
        n = b.shape[1] if tb else 4 * b.shape[2]
        if tb:
            tk = b.shape[2]
        else:
            tn = b.shape[2]
    else:
        n = b.shape[0] if tb else b.shape[1]
    if out_chip:
        tn = n // 4
    tn = tn or _pick(n, (1408, 1024, 768, 512, 256, 128))
    tk = tk or (_pick(k, (1408, 704, 384, 128)) if ta else _pick(k, (1024, 1408, 768, 512, 128)))
    nk = k // tk
    if tm is None:
        isz = lambda t: jnp.dtype(t.dtype).itemsize
        osz = jnp.dtype(out_dtype).itemsize
        for tm in ((1408, 1024, 512, 384, 256, 128) if ta else (1408, 704, 512, 384, 256, 128)):
            need = 2 * (tm * tk * isz(a) + tk * tn * isz(b) + tm * tn * osz + (tm * tn * 4 if add is not None else 0))
            need += tm * tn * 4 if nk > 1 else 0
            if m % tm == 0 and need <= MM_VMEM_BUDGET:
                break
        else:
            tm = m
    dims = (((0 if ta else 1,), (1 if tb else 0,)), ((), ()))

    def body(*refs):
        if add is None:
            a_ref, b_ref, o_ref, acc_ref = refs
            add_ref = None
        else:
            a_ref, b_ref, add_ref, o_ref, acc_ref = refs
        part = lax.dot_general(a_ref[...].astype(BF16), b_ref[...].astype(BF16), dims,
                               preferred_element_type=F32)

        def finish(total):
            if add_ref is not None:
                total = total + add_ref[...]
            o_ref[...] = total.astype(out_dtype)

        if nk == 1:
            finish(part)
        else:
            kk = pl.program_id(2)

            @pl.when(kk == 0)
            def _():
                acc_ref[...] = part

            @pl.when(kk > 0)
            def _():
                acc_ref[...] += part

            @pl.when(kk == nk - 1)
            def _():
                finish(acc_ref[...])

    a_spec = pl.BlockSpec((tk, tm), lambda i, j, kk: (kk, i)) if ta else pl.BlockSpec((tm, tk), lambda i, j, kk: (i, kk))
    if b_chip and tb:
        b_spec = pl.BlockSpec((None, tn, tk), lambda i, j, kk: (chip_of(kk), j, 0))
    elif b_chip:
        b_spec = pl.BlockSpec((None, tk, tn), lambda i, j, kk: (j, kk, 0))
    elif tb:
        b_spec = pl.BlockSpec((tn, tk), lambda i, j, kk: (j, kk))
    else:
        b_spec = pl.BlockSpec((tk, tn), lambda i, j, kk: (kk, j))
    o_spec = pl.BlockSpec((tm, tn), lambda i, j, kk: (i, j))
    in_specs = [a_spec, b_spec] + ([o_spec] if add is not None else [])
    args = [a, b] + ([add] if add is not None else [])
    out_spec = pl.BlockSpec((None, tm, tn), lambda i, j, kk: (chip_of(j), i, 0)) if out_chip else o_spec
    return pl.pallas_call(
        body, name=name, interpret=False,
        out_shape=jax.ShapeDtypeStruct((4, m, tn) if out_chip else (m, n), out_dtype),
        grid=(m // tm, n // tn, nk), in_specs=in_specs, out_specs=out_spec,
        scratch_shapes=[pltpu.VMEM((tm, tn) if nk > 1 else (8, 128), F32)],
        compiler_params=_params(("parallel", "parallel", "arbitrary")),
    )(*args)


def _chip_order(swap_mid):
    return (lambda k: (k % 2) * 2 + k // 2) if swap_mid else (lambda k: k)


def _mm_epi(a, b, *, name, tb, tn, b_chip, epi, epi_ins, epi_consts, epi_outs, epi_accs, swap_mid=False):
    chip_of = _chip_order(swap_mid)
    m, k = a.shape
    if b_chip:
        n = b.shape[1] if tb else 4 * b.shape[2]
        tk = b.shape[2] if tb else None
        tn = tn if tb else b.shape[2]
    else:
        n = b.shape[0] if tb else b.shape[1]
        tk = None
    tn = tn or _pick(n, (1408, 1024, 768, 512, 256, 128))
    tk = tk or _pick(k, (1024, 1408, 1280, 768, 512, 128))
    nk, nj = k // tk, n // tn
    isz = lambda t: jnp.dtype(t.dtype if hasattr(t, "dtype") else t).itemsize
    outs3 = [t if isinstance(t, tuple) else (t, n, lambda j: j) for t in epi_outs]
    side = sum(isz(t) for t, _ in epi_ins) + sum(isz(dt) for dt, _, _ in outs3)
    for tm in (1408, 704, 512, 384, 256, 128):
        need = 2 * (tm * tk * isz(a) + tk * tn * isz(b) + tm * tn * side) + (tm * tn * 4 if nk > 1 else 0)
        if m % tm == 0 and need <= MM_VMEM_BUDGET:
            break
    else:
        tm = m
    dims = (((1,), (1 if tb else 0,)), ((), ()))
    n_in, n_c, n_out, n_acc = len(epi_ins), len(epi_consts), len(epi_outs), len(epi_accs)

    def body(*refs):
        a_ref, b_ref = refs[:2]
        in_refs = refs[2:2 + n_in + n_c]
        out_refs = refs[2 + n_in + n_c:2 + n_in + n_c + n_out]
        acc_out = refs[2 + n_in + n_c + n_out:2 + n_in + n_c + n_out + n_acc]
        acc_ref = refs[-1]
        i, j, kk = pl.program_id(0), pl.program_id(1), pl.program_id(2)
        part = lax.dot_general(a_ref[...].astype(BF16), b_ref[...].astype(BF16), dims,
                               preferred_element_type=F32)

        def finish(total):
            res = epi(i * tm, total, *[r[...] for r in in_refs])
            if not isinstance(res, (tuple, list)):
                res = (res,)
            for r, v in zip(out_refs, res[:n_out]):
                r[...] = v.astype(r.dtype)
            if n_acc:
                @pl.when(jnp.logical_and(i == 0, j == 0))
                def _():
                    for r in acc_out:
                        r[...] = jnp.zeros(r.shape, r.dtype)

                for r, v in zip(acc_out, res[n_out:]):
                    r[...] += jnp.broadcast_to(v, r.shape).astype(r.dtype)

        if nk == 1:
            finish(part)
        else:
            @pl.when(kk == 0)
            def _():
                acc_ref[...] = part

            @pl.when(kk > 0)
            def _():
                acc_ref[...] += part

            @pl.when(kk == nk - 1)
            def _():
                finish(acc_ref[...])

    a_spec = pl.BlockSpec((tm, tk), lambda i, j, kk: (i, kk))
    if b_chip and tb:
        b_spec = pl.BlockSpec((None, tn, tk), lambda i, j, kk: (chip_of(kk), j, 0))
    elif b_chip:
        b_spec = pl.BlockSpec((None, tk, tn), lambda i, j, kk: (j, kk, 0))
    elif tb:
        b_spec = pl.BlockSpec((tn, tk), lambda i, j, kk: (j, kk))
    else:
        b_spec = pl.BlockSpec((tk, tn), lambda i, j, kk: (kk, j))
    in_specs = [a_spec, b_spec]

    def in_spec(t, col):
        front = m - t.shape[0]
        if not front:
            return pl.BlockSpec((tm, tn), lambda i, j, kk: (i, col(j)))
        return pl.BlockSpec((pl.Element(tm), pl.Element(tn)),
                            lambda i, j, kk: (pl.multiple_of(jnp.maximum(i * tm - front, 0), 8), col(j) * tn))

    in_specs += [in_spec(t, col) for t, col in epi_ins]
    in_specs += [pl.BlockSpec(t.shape, lambda i, j, kk, nd=t.ndim: (0,) * nd) for t in epi_consts]
    out_specs = [pl.BlockSpec((tm, tn), lambda i, j, kk, col=col: (i, col(j))) for _, _, col in outs3]
    out_specs += [pl.BlockSpec(s, lambda i, j, kk, nd=len(s): (0,) * nd) for s, _ in epi_accs]
    out_shape = [jax.ShapeDtypeStruct((m, width), dt) for dt, width, _ in outs3]
    out_shape += [jax.ShapeDtypeStruct(s, dt) for s, dt in epi_accs]
    sem = ("arbitrary", "arbitrary", "arbitrary") if n_acc else ("parallel", "parallel", "arbitrary")
    return pl.pallas_call(
        body, name=name, interpret=False, out_shape=out_shape,
        grid=(m // tm, nj, nk), in_specs=in_specs, out_specs=out_specs,
        scratch_shapes=[pltpu.VMEM((tm, tn) if nk > 1 else (8, 128), F32)],
        compiler_params=_params(sem),
    )(a, b, *[t for t, _ in epi_ins], *epi_consts)


def cols(arr, tr, width=None, cb=0):
    width = width or arr.shape[1]
    return (arr, (tr, width), lambda i: (i, cb), "r2")


def heads(arr, tr):
    return (arr, (arr.shape[0], tr, arr.shape[2]), lambda i: (0, i, 0), "r3")


def whole(arr):
    nd = arr.ndim
    return (arr, arr.shape, lambda i: (0,) * nd, "w")


STRIP = 16


def _rows_of(ref, kind, r0, n):
    if kind == "r2":
        return ref[pl.ds(r0, n), :]
    if kind == "r3":
        return ref[:, pl.ds(r0, n), :]
    return ref[...]


def _set_rows(ref, kind, r0, n, v):
    if kind == "r2":
        ref[pl.ds(r0, n), :] = v.astype(ref.dtype)
    elif kind == "r3":
        ref[:, pl.ds(r0, n), :] = v.astype(ref.dtype)
    else:
        ref[...] = v.astype(ref.dtype)


def rowwise(fn, ins, outs, *, steps, name, accs=(), strip=None):
    n_in, n_out, n_acc = len(ins), len(outs), len(accs)
    kin = [t[3] for t in ins]
    kout = [t[4] for t in outs]
    tr = next((t[1][-2] for t in ins if t[3] != "w"), 0)

    def body(*refs):
        i = pl.program_id(0)
        in_refs, out_refs, acc_refs = refs[:n_in], refs[n_in:n_in + n_out], refs[n_in + n_out:]
        if n_acc:
            @pl.when(i == 0)
            def _():
                for r in acc_refs:
                    r[...] = jnp.zeros(r.shape, r.dtype)

        def run(r0, n):
            res = fn(i * tr + r0, *[_rows_of(r, k, r0, n) for r, k in zip(in_refs, kin)])
            if not isinstance(res, (tuple, list)):
                res = (res,)
            for r, k, v in zip(out_refs, kout, res[:n_out]):
                _set_rows(r, k, r0, n, v)
            for r, v in zip(acc_refs, res[n_out:]):
                r[...] += jnp.broadcast_to(v, r.shape).astype(r.dtype)

        if strip is None or tr <= strip:
            run(0, tr)
        else:
            def step(s, carry):
                run(pl.multiple_of(s * strip, strip), strip)
                return carry
            lax.fori_loop(0, tr // strip, step, 0)

    def zmap(nd):
        return lambda i: (0,) * nd

    in_specs = [pl.BlockSpec(t[1], t[2]) for t in ins]
    out_specs = [pl.BlockSpec(t[2], t[3]) for t in outs]
    out_specs += [pl.BlockSpec(s, zmap(len(s))) for s, _ in accs]
    out_shape = [jax.ShapeDtypeStruct(t[0], t[1]) for t in outs]
    out_shape += [jax.ShapeDtypeStruct(s, d) for s, d in accs]
    res = pl.pallas_call(
        body, name=name, interpret=False, out_shape=out_shape, grid=(steps,),
        in_specs=in_specs, out_specs=out_specs,
        compiler_params=_params(("arbitrary",)),
    )(*[t[0] for t in ins])
    return res


def out2d(rows, width, dtype, tr):
    return ((rows, width), dtype, (tr, width), lambda i: (i, 0), "r2")


def conv_fwd(xs, w8, kw, *, rows, c, tc, tr, name, post, extras=(), outs=(), pre=None, strip=STRIP):
    nx, ne, no = len(xs), len(extras), len(outs)
    nr, nc = rows // tr, c // tc
    r8 = tr // 8
    st = strip

    def body(*refs):
        x_refs = refs[:2 * nx]
        w_ref = refs[2 * nx]
        e_refs = refs[2 * nx + 1:2 * nx + 1 + ne]
        o_refs = refs[2 * nx + 1 + ne:2 * nx + 1 + ne + no]
        scr = refs[-1]
        j, i = pl.program_id(0), pl.program_id(1)
        halo = [x_refs[2 * q + 1][...].astype(F32) for q in range(nx)]
        scr[0:8, :] = jnp.where(i > 0, pre(*halo) if pre else halo[0], 0.0)

        def fill(s, carry):
            r0 = pl.multiple_of(s * st, st)
            cur = [x_refs[2 * q][pl.ds(r0, st), :].astype(F32) for q in range(nx)]
            scr[pl.ds(8 + r0, st), :] = pre(*cur) if pre else cur[0]
            return carry

        def comp(s, carry):
            r0 = pl.multiple_of(s * st, st)
            win = scr[pl.ds(r0, st + 8), :]
            y = jnp.zeros((st, tc), F32)
            for q in range(kw):
                sh = kw - 1 - q
                y = y + w_ref[q:q + 1, :] * win[8 - sh:8 - sh + st]
            res = post(j, y, *[e[pl.ds(r0, st), :] for e in e_refs])
            if not isinstance(res, (tuple, list)):
                res = (res,)
            for r, v in zip(o_refs, res):
                r[pl.ds(r0, st), :] = v.astype(r.dtype)
            return carry

        lax.fori_loop(0, tr // st, fill, 0)
        lax.fori_loop(0, tr // st, comp, 0)

    in_specs, args = [], []
    for arr, cb0 in xs:
        in_specs.append(pl.BlockSpec((tr, tc), lambda j, i, cb0=cb0: (i, cb0 + j)))
        in_specs.append(pl.BlockSpec((8, tc), lambda j, i, cb0=cb0: (jnp.maximum(i * r8 - 1, 0), cb0 + j)))
        args += [arr, arr]
    in_specs.append(pl.BlockSpec((8, tc), lambda j, i: (0, j)))
    args.append(w8)
    for arr, cb0 in extras:
        in_specs.append(pl.BlockSpec((tr, tc), lambda j, i, cb0=cb0: (i, cb0 + j)))
        args.append(arr)
    return pl.pallas_call(
        body, name=name, interpret=False,
        out_shape=[jax.ShapeDtypeStruct((rows, c), dt) for dt in outs],
        grid=(nc, nr), in_specs=in_specs,
        out_specs=[pl.BlockSpec((tr, tc), lambda j, i: (i, j)) for _ in outs],
        scratch_shapes=[pltpu.VMEM((tr + 8, tc), F32)],
        compiler_params=_params(("parallel", "arbitrary")),
    )(*args)


def conv_bwd(xs, w8, kw, dy, *, rows, c, tc, tr, name, post, extras=(), outs=(), pre=None):
    nx, ne, no = len(xs), len(extras), len(outs)
    nr, nc = rows // tr, c // tc
    r8 = tr // 8

    def body(*refs):
        x_refs = refs[:nx]
        w_ref, dy_ref, dyn_ref = refs[nx:nx + 3]
        e_refs = refs[nx + 3:nx + 3 + ne]
        first_out = nx + 3 + ne
        o_refs = refs[first_out:first_out + no]
        dw_ref = refs[first_out + no]
        gscr = refs[-1]
        i = pl.program_id(1)
        gscr[tr:tr + 8, :] = jnp.where(i < nr - 1, dyn_ref[...].astype(F32), 0.0)

        def fill(s, carry):
            r0 = pl.multiple_of(s * STRIP, STRIP)
            gscr[pl.ds(r0, STRIP), :] = dy_ref[pl.ds(r0, STRIP), :].astype(F32)
            return carry

        def comp(s, dws):
            r0 = pl.multiple_of(s * STRIP, STRIP)
            gwin = gscr[pl.ds(r0, STRIP + 8), :]
            cur = [x_refs[q][pl.ds(r0, STRIP), :].astype(F32) for q in range(nx)]
            x = pre(*cur) if pre else cur[0]
            dx = jnp.zeros((STRIP, tc), F32)
            new = []
            for q in range(kw):
                sh = kw - 1 - q
                ahead = gwin[sh:sh + STRIP]
                dx = dx + w_ref[q:q + 1, :] * ahead
                part = ahead * x
                new.append(dws[q] + part[0:8] + part[8:16])
            res = post(dx, *[e[pl.ds(r0, STRIP), :] for e in e_refs])
            if not isinstance(res, (tuple, list)):
                res = (res,)
            for r, v in zip(o_refs, res):
                r[pl.ds(r0, STRIP), :] = v.astype(r.dtype)
            return tuple(new)

        lax.fori_loop(0, tr // STRIP, fill, 0)
        dws = lax.fori_loop(0, tr // STRIP, comp, tuple(jnp.zeros((8, tc), F32) for _ in range(kw)))

        @pl.when(i == 0)
        def _():
            dw_ref[...] = jnp.zeros((8, tc), F32)

        dw_ref[...] += jnp.concatenate([jnp.sum(t, axis=0, keepdims=True) for t in dws]
                                       + [jnp.zeros((8 - kw, tc), F32)], axis=0)

    in_specs, args = [], []
    for arr, cb0 in xs:
        in_specs.append(pl.BlockSpec((tr, tc), lambda j, i, cb0=cb0: (i, cb0 + j)))
        args.append(arr)
    in_specs.append(pl.BlockSpec((8, tc), lambda j, i: (0, j)))
    in_specs.append(pl.BlockSpec((tr, tc), lambda j, i: (i, j)))
    in_specs.append(pl.BlockSpec((8, tc), lambda j, i: (jnp.minimum((i + 1) * r8, nr * r8 - 1), j)))
    args += [w8, dy, dy]
    for arr, cb0 in extras:
        in_specs.append(pl.BlockSpec((tr, tc), lambda j, i, cb0=cb0: (i, cb0 + j)))
        args.append(arr)
    return pl.pallas_call(
        body, name=name, interpret=False,
        out_shape=[jax.ShapeDtypeStruct((rows, c), dt) for dt in outs] + [jax.ShapeDtypeStruct((8, c), F32)],
        grid=(nc, nr), in_specs=in_specs,
        out_specs=[pl.BlockSpec((tr, tc), lambda j, i: (i, j)) for _ in outs] + [pl.BlockSpec((8, tc), lambda j, i: (0, j))],
        scratch_shapes=[pltpu.VMEM((tr + 8, tc), F32)],
        compiler_params=_params(("parallel", "arbitrary")),
    )(*args)


def rms_fwd(h, w, *, name):
    rows = h.shape[0]
    tr = _pick(rows, (384, 128))

    def fn(i, x, wv):
        r = lax.rsqrt(jnp.mean(x * x, axis=1, keepdims=True) + EPS)
        return x * r * wv

    return rowwise(fn, [cols(h, tr), whole(w)], [out2d(rows, D, BF16, tr)], steps=rows // tr, name=name)[0]


def _rms_bwd_epi(row0, g, x, dr, wv):
    r = lax.rsqrt(jnp.mean(x * x, axis=1, keepdims=True) + EPS)
    xh = x * r
    gw = g * wv
    dx = r * (gw - xh * jnp.mean(gw * xh, axis=1, keepdims=True))
    row = row0 + lax.broadcasted_iota(jnp.int32, (x.shape[0], 1), 0)
    return jnp.where(row >= PAD, dr + dx, 0.0), jnp.sum(g * xh, axis=0, keepdims=True)


def dx_rms_bwd(dy, w, h, nw, dres, *, name, b_chip=False, swap_mid=False):
    return mm(dy, w, tb=True, b_chip=b_chip, swap_mid=swap_mid, tn=D, name=name, epi=_rms_bwd_epi,
              epi_ins=[(h, lambda j: 0), (dres, lambda j: 0)], epi_consts=[nw], epi_outs=[F32],
              epi_accs=[((1, D), F32)])


def _add_loss_epi(row0, t, h, tgt):
    row = row0 + lax.broadcasted_iota(jnp.int32, (t.shape[0], 1), 0)
    tgt = jnp.where(row0 == 0, jnp.concatenate([tgt[-HEAD0:], tgt[:-HEAD0]], axis=0), tgt)
    diff = jnp.where(row >= HEAD0, t + h - tgt, 0.0)
    part = jnp.sum(jnp.sum(diff * diff, axis=1, keepdims=True), axis=0, keepdims=True)
    return diff * (1.0 / D), part * (0.5 / D)


def add_loss(a, w, h, target, *, name):
    return mm(a, w, name=name, epi=_add_loss_epi, epi_ins=[(h, lambda j: 0), (target, lambda j: 0)],
              epi_outs=[F32], epi_accs=[((1, 128), F32)])


def adamw(w, g, m, v, *, name):
    shape = w.shape
    gs = list(g) if isinstance(g, (list, tuple)) else [g]
    nl = len(gs)
    width = shape[-1]
    rows = w.size // width
    rl = rows // nl
    tr = _pick(rl, (256, 176, 128, 64, 16, 8))
    nr = rl // tr
    if w.ndim == 3 and shape[1] % tr == 0:
        per = shape[1] // tr
        view = lambda t: (t, (None, tr, width), lambda i: (i // per, i % per, 0), "r2")
        out = (shape, F32, (None, tr, width), lambda i: (i // per, i % per, 0), "r2")
    else:
        view = lambda t: cols(t.reshape(rows, width), tr)
        out = out2d(rows, width, F32, tr)

    def fn(i, wv, mv, vv, *gvs):
        gv = gvs[0]
        for layer in range(1, nl):
            gv = jnp.where(i >= layer * rl, gvs[layer], gv)
        mn = B1 * mv + (1.0 - B1) * gv
        vn = B2 * vv + (1.0 - B2) * gv * gv
        mh = mn / (1.0 - B1 ** STEP)
        vh = vn / (1.0 - B2 ** STEP)
        return -LR * (mh / (jnp.sqrt(vh) + AEPS) + WD * wv), mn, vn, gv

    g_ins = [(t.reshape(rl, width), (tr, width), lambda i, layer=layer: (jnp.clip(i - layer * nr, 0, nr - 1), 0), "r2")
             for layer, t in enumerate(gs)]
    res = rowwise(fn, [view(t) for t in (w, m, v)] + g_ins, [out] * 4, steps=rows // tr, name=name)
    return [r.reshape(shape) for r in res]


HB = DN_H * CH
PAIR = 3


def _split(a):
    hi = a.astype(BF16)
    return hi, (a - hi.astype(F32)).astype(BF16)


def _dot1(a, b, ca=1, cb=0):
    return _dot(a.astype(BF16), b.astype(BF16), ca, cb)


def _dot3(a, b, ca=1, cb=0):
    ah, al = _split(a)
    bh, bl = _split(b)
    return _dot(ah, bh, ca, cb) + (_dot(ah, bl, ca, cb) + _dot(al, bh, ca, cb))


def _dot01(m01, b, ca=1, cb=0):
    bh, bl = _split(b)
    m = m01.astype(BF16)
    return _dot(m, bh, ca, cb) + _dot(m, bl, ca, cb)


def _stack(x):
    return jnp.concatenate([x[:, h * DN_D:(h + 1) * DN_D] for h in range(DN_H)], axis=0)


def _unstack(x):
    return jnp.concatenate([x[h * CH:(h + 1) * CH] for h in range(DN_H)], axis=1)


def _tri_inv(mats, blk, eye):
    each = lambda f, *lists: [f(*t) for t in zip(*lists)]
    ad = [jnp.where(blk, a, 0.0) for a in mats]
    lo = each(lambda a, d: a - d, mats, ad)
    a2 = each(_dot3, ad, ad)
    a4 = each(_dot3, a2, a2)
    a8 = each(_dot3, a4, a4)
    dgi = each(lambda d, s: _dot3(eye - d, eye + s), ad, a2)
    dgi = each(lambda p, s: _dot3(p, eye + s), dgi, a4)
    dgi = each(lambda p, s: _dot3(p, eye + s), dgi, a8)
    n = each(_dot3, dgi, lo)
    n2 = each(_dot3, n, n)
    return each(_dot3, each(lambda u, v: _dot3(eye - u, eye + v), n, n2), dgi)


def _dn_masks():
    row = lax.broadcasted_iota(jnp.int32, (HB, HB), 0)
    col = lax.broadcasted_iota(jnp.int32, (HB, HB), 1)
    same = (row // CH) == (col // CH)
    incl = jnp.logical_and(same, row >= col)
    strict = jnp.logical_and(same, row > col)
    upper = jnp.logical_and(same, row <= col)
    blk = (row // 16) == (col // 16)
    eye = (row == col).astype(F32)
    return incl, strict, upper, blk, eye


def _dn_chunk(qv, kv, vv, bc, br, incl, strict):
    r64 = lax.broadcasted_iota(jnp.int32, (CH, CH), 0)
    c64 = lax.broadcasted_iota(jnp.int32, (CH, CH), 1)
    dcol = _dot01((r64 >= c64).astype(F32), bc)
    drow = _dot3(br, (r64 <= c64).astype(F32))
    col = lambda m, l0: jnp.concatenate([m[:, l0 + h:l0 + h + 1] for h in range(DN_H)], axis=0)
    b_c = col(bc, 0)
    d_c = col(dcol, 4)
    d_r = jnp.concatenate([drow[4 + h:5 + h, :] for h in range(DN_H)], axis=1)
    d_last_h = [dcol[CH - 1:CH, 4 + h:5 + h] for h in range(DN_H)]
    d_last = jnp.concatenate([jnp.broadcast_to(t, (CH, 1)) for t in d_last_h], axis=0)
    q, k, v = _stack(qv), _stack(kv), _stack(vv)
    dm = jnp.where(incl, jnp.exp(jnp.where(incl, d_c - d_r, 0.0)), 0.0)
    kk = _dot1(k, k, 1, 1)
    a = jnp.where(strict, b_c * kk * dm, 0.0)
    ed = jnp.exp(d_c)
    rhs = jnp.concatenate([v * b_c, k * (b_c * ed)], axis=1)
    qk = _dot1(q, k, 1, 1) * dm
    ekd = jnp.exp(d_last - d_c)
    gl = [jnp.exp(t) for t in d_last_h]
    return q, k, v, b_c, dm, kk, a, ed, rhs, qk, ekd, gl


def dn_fwd(qkv_n, bgcol, bgrow):
    rows = qkv_n.shape[0]
    nch = rows // CH

    def body(q_ref, k_ref, v_ref, bc_ref, br_ref, o_ref, s_out, ti_out, s_scr, prep, prep_qk, prep_gl):
        n = pl.program_id(0)

        @pl.when(n == 0)
        def _():
            s_scr[...] = jnp.zeros(s_scr.shape, F32)
            prep[...] = jnp.zeros(prep.shape, F32)
            prep_qk[...] = jnp.zeros(prep_qk.shape, F32)
            prep_gl[...] = jnp.zeros(prep_gl.shape, F32)

        live = n > 0
        rows_of = [slice(h * CH, (h + 1) * CH) for h in range(DN_H)]
        s = [s_scr[h] for h in range(DN_H)]
        for c in range(PAIR):
            u, w, qd, kd = prep[c, 0], prep[c, 1], prep[c, 2], prep[c, 3]
            for h in range(DN_H):
                s_out[c, h] = s[h]
            v_new = [u[rs] - _dot1(w[rs], s[h]) for h, rs in enumerate(rows_of)]
            o_state = [_dot1(qd[rs], s[h]) for h, rs in enumerate(rows_of)]
            s = [jnp.where(live, prep_gl[c, h:h + 1, 0:1] * s[h] + _dot1(kd[rs], v_new[h], 0, 0), s[h])
                 for h, rs in enumerate(rows_of)]
            o = jnp.concatenate(o_state, axis=0) + _dot1(prep_qk[c], jnp.concatenate(v_new, axis=0))
            o_ref[c * CH:(c + 1) * CH, :] = _unstack(o)
        for h in range(DN_H):
            s_scr[h] = s[h]

        incl, strict, _, blk, eye = _dn_masks()
        parts = []
        for c in range(PAIR):
            rows_c = slice(c * CH, (c + 1) * CH)
            parts.append(_dn_chunk(q_ref[rows_c, :], k_ref[rows_c, :], v_ref[rows_c, :], bc_ref[rows_c, :],
                                   br_ref[c], incl, strict))
        tinvs = _tri_inv([p[6] for p in parts], blk, eye)
        for c, (q, k, v, b_c, dm, kk, a, ed, rhs, qk_n, ekd, gl) in enumerate(parts):
            tinv = tinvs[c]
            ti_out[c] = tinv
            sol = _dot3(tinv, rhs)
            prep[c, 0] = sol[:, :DN_D]
            prep[c, 1] = sol[:, DN_D:]
            prep[c, 2] = q * ed
            prep[c, 3] = k * ekd
            prep_qk[c] = qk_n
            prep_gl[c] = jnp.concatenate([jnp.broadcast_to(t, (1, 128)) for t in gl]
                                         + [jnp.zeros((8 - DN_H, 128), F32)], axis=0)

    assert nch % PAIR == 0
    npair = nch // PAIR
    last = npair - 1
    return pl.pallas_call(
        body, name="dn_fwd", interpret=False,
        out_shape=[jax.ShapeDtypeStruct((rows, DN_DIM), F32),
                   jax.ShapeDtypeStruct((nch, DN_H, DN_D, DN_D), F32),
                   jax.ShapeDtypeStruct((nch, HB, HB), F32)],
        grid=(npair + 1,),
        in_specs=[pl.BlockSpec((PAIR * CH, DN_DIM), lambda n: (jnp.minimum(n, last), 0)),
                  pl.BlockSpec((PAIR * CH, DN_DIM), lambda n: (jnp.minimum(n, last), 1)),
                  pl.BlockSpec((PAIR * CH, DN_DIM), lambda n: (jnp.minimum(n, last), 2)),
                  pl.BlockSpec((PAIR * CH, 128), lambda n: (jnp.minimum(n, last), 0)),
                  pl.BlockSpec((PAIR, 8, CH), lambda n: (jnp.minimum(n, last), 0, 0))],
        out_specs=[pl.BlockSpec((PAIR * CH, DN_DIM), lambda n: (jnp.maximum(n - 1, 0), 0)),
                   pl.BlockSpec((PAIR, DN_H, DN_D, DN_D), lambda n: (jnp.maximum(n - 1, 0), 0, 0, 0)),
                   pl.BlockSpec((PAIR, HB, HB), lambda n: (jnp.minimum(n, last), 0, 0))],
        scratch_shapes=[pltpu.VMEM((DN_H, DN_D, DN_D), F32), pltpu.VMEM((PAIR, 4, HB, DN_D), F32),
                        pltpu.VMEM((PAIR, HB, HB), F32), pltpu.VMEM((PAIR, 8, 128), F32)],
        compiler_params=_params(("arbitrary",)),
    )(qkv_n, qkv_n, qkv_n, bgcol, bgrow)


def dn_bwd(qkv_n, bgcol, bgrow, s_all, ti_all, do):
    rows = qkv_n.shape[0]
    nch = rows // CH

    def body(q_ref, k_ref, v_ref, bc_ref, br_ref, s_ref, ti_ref, do_ref, dq_ref, dk_ref, dv_ref, dbg_ref, ds_scr):
        n = pl.program_id(0)

        @pl.when(n == 0)
        def _():
            ds_scr[...] = jnp.zeros(ds_scr.shape, F32)

        incl, strict, upper, _, _ = _dn_masks()
        rsum = lambda t: jnp.sum(t, axis=1, keepdims=True)
        rows_of = [slice(h * CH, (h + 1) * CH) for h in range(DN_H)]
        heads_of = lambda f: jnp.concatenate([f(h, rs) for h, rs in enumerate(rows_of)], axis=0)
        cs = []
        for c in reversed(range(PAIR)):
            rc = slice(c * CH, (c + 1) * CH)
            q, k, v, b_c, dm, kk, a, ed, rhs, qk, ekd, gl = _dn_chunk(
                q_ref[rc, :], k_ref[rc, :], v_ref[rc, :], bc_ref[rc, :], br_ref[c], incl, strict)
            cs.append(dict(rc=rc, q=q, k=k, v=v, b_c=b_c, dm=dm, kk=kk, a=a, ed=ed, rhs=rhs, qk=qk, ekd=ekd, gl=gl,
                           tinv=ti_ref[c], g=_stack(do_ref[rc, :]), s=[s_ref[c, h] for h in range(DN_H)]))
        for t in cs:
            t["sol"] = _dot3(t["tinv"], t["rhs"])
        for t in cs:
            t["u"], t["w"] = t["sol"][:, :DN_D], t["sol"][:, DN_D:]
            t["qd"], t["kd"] = t["q"] * t["ed"], t["k"] * t["ekd"]
            t["v_new"] = heads_of(lambda h, rs: t["u"][rs] - _dot1(t["w"][rs], t["s"][h]))
            t["dv0"] = _dot1(t["qk"], t["g"], 0, 0)
            t["ds0"] = [_dot1(t["qd"][rs], t["g"][rs], 0, 0) for rs in rows_of]
            t["dqd"] = heads_of(lambda h, rs: _dot1(t["g"][rs], t["s"][h], 1, 1))
        for t in cs:
            t["dqk"] = _dot1(t["g"], t["v_new"], 1, 1)
        ds = [ds_scr[h] for h in range(DN_H)]
        for t in cs:
            t["ds"] = ds
            t["dv_new"] = t["dv0"] + heads_of(lambda h, rs: _dot1(t["kd"][rs], ds[h]))
            ds = [t["ds0"][h] + t["gl"][h] * ds[h] - _dot1(t["w"][rs], t["dv_new"][rs], 0, 0)
                  for h, rs in enumerate(rows_of)]
        for h in range(DN_H):
            ds_scr[h] = ds[h]
        for t in cs:
            t["dkd"] = heads_of(lambda h, rs: _dot1(t["v_new"][rs], t["ds"][h], 1, 1))
            dw = heads_of(lambda h, rs: -_dot1(t["dv_new"][rs], t["s"][h], 1, 1))
            t["dsol"] = jnp.concatenate([t["dv_new"], dw], axis=1)
        for t in cs:
            t["drhs"] = _dot3(t["tinv"], t["dsol"], 0, 0)
        for t in cs:
            t["da"] = jnp.where(strict, -_dot1(t["drhs"], t["sol"], 1, 1), 0.0)
        rowi = lax.broadcasted_iota(jnp.int32, (CH, 1), 0)
        lane = lax.broadcasted_iota(jnp.int32, (CH, 128), 1)
        for t in cs:
            q, k, v, b_c, dm, ed, da, dqk = t["q"], t["k"], t["v"], t["b_c"], t["dm"], t["ed"], t["da"], t["dqk"]
            drhs_u, drhs_w = t["drhs"][:, :DN_D], t["drhs"][:, DN_D:]
            s2 = rsum(drhs_w * k)
            dbeta = rsum(drhs_u * v) + s2 * ed + rsum(da * t["kk"] * dm)
            dkk = da * b_c * dm
            dqkr = dqk * dm
            mmat = da * t["a"] + dqk * t["qk"]
            tmp = rsum(t["dkd"] * t["kd"])
            dd = (s2 * b_c * ed + rsum(mmat) - _dot3(mmat, jnp.ones((HB, 128), F32), 0, 0)[:, :1]
                  + rsum(t["dqd"] * t["qd"]) - tmp)
            last = []
            for h, rs in enumerate(rows_of):
                dgl = jnp.sum(rsum(t["s"][h] * t["ds"][h]), axis=0, keepdims=True)
                dd_last = jnp.sum(tmp[rs], axis=0, keepdims=True) + dgl * t["gl"][h]
                last.append(jnp.where(rowi == CH - 1, dd_last, 0.0))
            dd = dd + jnp.concatenate(last, axis=0)
            rc = t["rc"]
            dq_ref[rc, :] = _unstack(_dot1(dqkr, k) + t["dqd"] * ed)
            dk_ref[rc, :] = _unstack(drhs_w * (b_c * ed) + _dot1(dkk, k) + _dot1(dkk, k, 0, 0) + _dot1(dqkr, q, 0, 0)
                                     + t["dkd"] * t["ekd"])
            dv_ref[rc, :] = _unstack(drhs_u * b_c)
            dg = _dot01(upper.astype(F32), jnp.broadcast_to(dd, (HB, 128)))[:, :1]
            out = jnp.zeros((CH, 128), F32)
            for h, rs in enumerate(rows_of):
                out = out + jnp.where(lane == h, dbeta[rs], 0.0) + jnp.where(lane == 4 + h, dg[rs], 0.0)
            dbg_ref[rc, :] = out

    assert nch % PAIR == 0
    npair = nch // PAIR
    rev = lambda n: npair - 1 - n
    blk = PAIR * CH
    return pl.pallas_call(
        body, name="dn_bwd", interpret=False,
        out_shape=[jax.ShapeDtypeStruct((rows, DN_DIM), F32)] * 3 + [jax.ShapeDtypeStruct((rows, 128), F32)],
        grid=(npair,),
        in_specs=[pl.BlockSpec((blk, DN_DIM), lambda n: (rev(n), 0)),
                  pl.BlockSpec((blk, DN_DIM), lambda n: (rev(n), 1)),
                  pl.BlockSpec((blk, DN_DIM), lambda n: (rev(n), 2)),
                  pl.BlockSpec((blk, 128), lambda n: (rev(n), 0)),
                  pl.BlockSpec((PAIR, 8, CH), lambda n: (rev(n), 0, 0)),
                  pl.BlockSpec((PAIR, DN_H, DN_D, DN_D), lambda n: (rev(n), 0, 0, 0)),
                  pl.BlockSpec((PAIR, HB, HB), lambda n: (rev(n), 0, 0)),
                  pl.BlockSpec((blk, DN_DIM), lambda n: (rev(n), 0))],
        out_specs=[pl.BlockSpec((blk, DN_DIM), lambda n: (rev(n), 0))] * 3 + [pl.BlockSpec((blk, 128), lambda n: (rev(n), 0))],
        scratch_shapes=[pltpu.VMEM((DN_H, DN_D, DN_D), F32)],
        compiler_params=_params(("arbitrary",)),
    )(qkv_n, qkv_n, qkv_n, bgcol, bgrow, s_all, ti_all, do)


def _swa_valid(n):
    c3 = lax.broadcasted_iota(jnp.int32, (NKEY, 4 * BLK), 0)
    r = lax.broadcasted_iota(jnp.int32, (NKEY, 4 * BLK), 1) % BLK
    prev0 = N_META + BLK
    c = jnp.where(c3 < N_META, PAD + c3, jnp.where(c3 < prev0, c3 - N_META, c3 - prev0))
    lo = jnp.where(c3 < N_META, 0, jnp.where(c3 < prev0, r + 1 + jnp.where(n >= 2, 0, BLK), 0))
    hi = jnp.where(c3 < N_META, r + jnp.where(n >= 1, BLK, 0),
                   jnp.where(c3 < prev0, BLK, r - jnp.where(n >= 1, 0, BLK)))
    return jnp.logical_and(c >= lo, c <= hi)


def _swa_probs(qs, kcats, valid, sinks):
    s = [jnp.where(valid, _dot(kc, q, 1, 1), -1e30) for q, kc in zip(qs, kcats)]
    m = [jnp.maximum(jnp.max(t, axis=0, keepdims=True), sk) for t, sk in zip(s, sinks)]
    e = [jnp.where(valid, jnp.exp(t - mx), 0.0) for t, mx in zip(s, m)]
    es = [jnp.exp(sk - mx) for sk, mx in zip(sinks, m)]
    inv = [1.0 / (jnp.sum(t, axis=0, keepdims=True) + u) for t, u in zip(e, es)]
    return [t * i for t, i in zip(e, inv)], [u * i for u, i in zip(es, inv)]


def _swa_group(q_ref, sk_ref, h):
    q4 = jnp.concatenate([q_ref[4 * h + g] for g in range(4)], axis=0)
    sink4 = jnp.concatenate([jnp.full((1, BLK), sk_ref[4 * h + g], F32) for g in range(4)], axis=1)
    return q4, sink4


def _swa_specs():
    q = pl.BlockSpec((SWA_H, BLK, SWA_D), lambda n: (0, n, 0))
    km = pl.BlockSpec((SWA_KV, N_META, SWA_D), lambda n: (0, PAD // N_META, 0))
    kp = pl.BlockSpec((SWA_KV, BLK, SWA_D), lambda n: (0, jnp.maximum(n - 1, 0), 0))
    kc = pl.BlockSpec((SWA_KV, BLK, SWA_D), lambda n: (0, n, 0))
    return [q, km, kp, kc, km, kp, kc]


def swa_fwd(qh, kh, vh, sinks):
    rows = qh.shape[1]
    nb = rows // BLK

    def body(q_ref, km, kp, kc, vm, vp, vc, sk_ref, o_ref):
        n = pl.program_id(0)
        valid = _swa_valid(n)
        kcats = [jnp.concatenate([km[h], kp[h], kc[h]], axis=0) for h in range(SWA_KV)]
        vcats = [jnp.concatenate([vm[h], vp[h], vc[h]], axis=0) for h in range(SWA_KV)]
        qs, sinks4 = zip(*[_swa_group(q_ref, sk_ref, h) for h in range(SWA_KV)])
        ps, _ = _swa_probs(qs, kcats, valid, sinks4)
        o4s = [_dot(p.astype(BF16), vc_, 0, 0) for p, vc_ in zip(ps, vcats)]
        o_ref[...] = jnp.concatenate([o4[g * BLK:(g + 1) * BLK] for o4 in o4s for g in range(4)],
                                     axis=1).astype(BF16)

    return pl.pallas_call(
        body, name="swa_fwd", interpret=False,
        out_shape=jax.ShapeDtypeStruct((rows, SWA_H * SWA_D), BF16),
        grid=(nb,),
        in_specs=_swa_specs() + [pl.BlockSpec(memory_space=pltpu.SMEM)],
        out_specs=pl.BlockSpec((BLK, SWA_H * SWA_D), lambda n: (n, 0)),
        compiler_params=_params(("parallel",)),
    )(qh, kh, kh, kh, vh, vh, vh, sinks)


def swa_bwd(qh, kh, vh, sinks, do):
    rows = qh.shape[1]
    nb = rows // BLK

    def body(q_ref, km, kp, kc, vm, vp, vc, do_ref, sk_ref, dq_ref, dk_ref, dv_ref, dsk_ref):
        n = pl.program_id(0)

        @pl.when(n == 0)
        def _():
            dk_ref[...] = jnp.zeros(dk_ref.shape, F32)
            dv_ref[...] = jnp.zeros(dv_ref.shape, F32)

        valid = _swa_valid(n)
        g_all = do_ref[...]
        rowi = lax.broadcasted_iota(jnp.int32, (SWA_H, 128), 0)
        dsk = jnp.zeros((SWA_H, 128), F32)
        pm = pl.multiple_of(jnp.maximum(n - 1, 0) * BLK, BLK)
        pc = pl.multiple_of(n * BLK, BLK)
        hs = range(SWA_KV)
        kcats = [jnp.concatenate([km[h], kp[h], kc[h]], axis=0) for h in hs]
        vcats = [jnp.concatenate([vm[h], vp[h], vc[h]], axis=0) for h in hs]
        qs, sinks4 = zip(*[_swa_group(q_ref, sk_ref, h) for h in hs])
        g4s = [jnp.concatenate([g_all[:, (4 * h + g) * SWA_D:(4 * h + g + 1) * SWA_D] for g in range(4)], axis=0)
               for h in hs]
        ps, pss = _swa_probs(qs, kcats, valid, sinks4)
        dps = [_dot(vc_, g4, 1, 1) for vc_, g4 in zip(vcats, g4s)]
        deltas = [jnp.sum(p * dp, axis=0, keepdims=True) for p, dp in zip(ps, dps)]
        dss = [(p * (dp - dl)).astype(BF16) for p, dp, dl in zip(ps, dps, deltas)]
        dq4s = [_dot(ds, kc_, 0, 0) for ds, kc_ in zip(dss, kcats)]
        dkcs = [_dot(ds, q4) for ds, q4 in zip(dss, qs)]
        dvcs = [_dot(p.astype(BF16), g4) for p, g4 in zip(ps, g4s)]
        for h in hs:
            t = pss[h] * deltas[h]
            for g in range(4):
                dq_ref[4 * h + g] = dq4s[h][g * BLK:(g + 1) * BLK]
                part = -jnp.sum(t[:, g * BLK:(g + 1) * BLK], axis=1, keepdims=True)
                dsk = dsk + jnp.where(rowi == 4 * h + g, part, 0.0)
            lanes = slice(h * SWA_D, (h + 1) * SWA_D)
            for ref, val in ((dk_ref, dkcs[h]), (dv_ref, dvcs[h])):
                ref[PAD:BLK, lanes] += val[0:N_META]
                ref[pl.ds(pm, BLK), lanes] += val[N_META:N_META + BLK]
                ref[pl.ds(pc, BLK), lanes] += val[N_META + BLK:]
        dsk_ref[0] = dsk

    return pl.pallas_call(
        body, name="swa_bwd", interpret=False,
        out_shape=[jax.ShapeDtypeStruct((SWA_H, rows, SWA_D), F32),
                   jax.ShapeDtypeStruct((rows, SWA_KV * SWA_D), F32),
                   jax.ShapeDtypeStruct((rows, SWA_KV * SWA_D), F32),
                   jax.ShapeDtypeStruct((nb, SWA_H, 128), F32)],
        grid=(nb,),
        in_specs=_swa_specs() + [pl.BlockSpec((BLK, SWA_H * SWA_D), lambda n: (n, 0)),
                                 pl.BlockSpec(memory_space=pltpu.SMEM)],
        out_specs=[pl.BlockSpec((SWA_H, BLK, SWA_D), lambda n: (0, n, 0)),
                   pl.BlockSpec((rows, SWA_KV * SWA_D), lambda n: (0, 0)),
                   pl.BlockSpec((rows, SWA_KV * SWA_D), lambda n: (0, 0)),
                   pl.BlockSpec((1, SWA_H, 128), lambda n: (n, 0, 0))],
        compiler_params=_params(("arbitrary",)),
    )(qh, kh, kh, kh, vh, vh, vh, do, sinks)


QK_W = (SWA_H + SWA_KV) * SWA_D


def _head_mean(t):
    r = lax.broadcasted_iota(jnp.int32, (128, 128), 0) // SWA_D
    c = lax.broadcasted_iota(jnp.int32, (128, 128), 1) // SWA_D
    blk = jnp.where(r == c, 1.0 / SWA_D, 0.0).astype(BF16)
    out = []
    for i in range(t.shape[1] // 128):
        hi, lo = _split(t[:, 128 * i:128 * (i + 1)])
        out.append(_dot(hi, blk) + _dot(lo, blk))
    return jnp.concatenate(out, axis=1)


def _qk_scales(qw, kw):
    scale = SWA_D ** -0.5
    wt = jnp.concatenate([jnp.tile(qw.astype(F32) * scale, (1, SWA_H)), jnp.tile(kw.astype(F32), (1, SWA_KV))], axis=1)
    st = jnp.concatenate([jnp.full((1, SWA_H * SWA_D), scale, F32), jnp.ones((1, SWA_KV * SWA_D), F32)], axis=1)
    return wt, st


def qknorm_fwd(qkv, qw, kw):
    rows = qkv.shape[0]
    tr = _pick(rows, (384, 128))
    wt, _ = _qk_scales(qw, kw)

    def fn(i, x, w):
        xq = x[:, :QK_W]
        y = xq * lax.rsqrt(_head_mean(xq * xq) + EPS) * w
        head = lambda t, j: t[:, j * SWA_D:(j + 1) * SWA_D][None]
        qo = jnp.concatenate([head(y, j) for j in range(SWA_H)], axis=0)
        ko = jnp.concatenate([head(y, SWA_H + j) for j in range(SWA_KV)], axis=0)
        vo = jnp.concatenate([head(x, SWA_H + SWA_KV + j) for j in range(SWA_KV)], axis=0)
        return qo, ko, vo

    hm = lambda nh: ((nh, rows, SWA_D), BF16, (nh, tr, SWA_D), lambda i: (0, i, 0), "r3")
    return rowwise(fn, [cols(qkv, tr), whole(wt)], [hm(SWA_H), hm(SWA_KV), hm(SWA_KV)],
                   steps=rows // tr, name="qknorm_fwd")


def qknorm_bwd(qkv, qw, kw, dqh, dk, dv):
    rows = qkv.shape[0]
    tr = _pick(rows, (384, 128))
    wt, st = _qk_scales(qw, kw)

    def fn(i, x, w, sc, dq, dkv, dvv):
        xq = x[:, :QK_W]
        dy = jnp.concatenate([dq[j] for j in range(SWA_H)] + [dkv], axis=1)
        r = lax.rsqrt(_head_mean(xq * xq) + EPS)
        xh = xq * r
        gw = dy * w
        dx = r * (gw - xh * _head_mean(gw * xh))
        return jnp.concatenate([dx, dvv], axis=1), jnp.sum(dy * sc * xh, axis=0, keepdims=True)

    dqkv, dw = rowwise(fn, [cols(qkv, tr), whole(wt), whole(st), heads(dqh, tr), cols(dk, tr), cols(dv, tr)],
                       [out2d(rows, 1536, BF16, tr)], steps=rows // tr, name="qknorm_bwd", accs=[((1, QK_W), F32)])
    dw = dw.reshape(SWA_H + SWA_KV, SWA_D)
    return dqkv, jnp.sum(dw[:SWA_H], axis=0, keepdims=True), jnp.sum(dw[SWA_H:], axis=0, keepdims=True)


def _place():
    return lax.axis_index("x"), lax.axis_index("y"), lax.axis_index("c")


ANY = pl.BlockSpec(memory_space=pl.ANY)


def _rcopy(ssem, rsem, k, src, dst, to):
    return pltpu.make_async_remote_copy(src_ref=src, dst_ref=dst, send_sem=ssem.at[k], recv_sem=rsem.at[k],
                                        device_id=to, device_id_type=MESH)


def gather_weights(shards, small):
    n = len(shards)
    halves = [t.shape[0] // 2 for t in shards]

    def body(*refs):
        s_refs, small_ref = refs[:n], refs[n]
        o_refs, osmall = refs[n + 1:2 * n + 1], refs[2 * n + 1]
        ssem, rsem, lsem = refs[2 * n + 2:]
        x, y, c = _place()
        me = 2 * x + y
        chips = [(1 - x, y), (x, 1 - y), (1 - x, 1 - y)]

        def half(k, s, hh):
            return o_refs[k].at[s, pl.ds(hh * halves[k], halves[k]), :]

        loc = pltpu.make_async_copy(small_ref, osmall.at[me], lsem)
        loc.start()
        sends = []
        for k in range(n):
            for j, (px, py) in enumerate(chips):
                sends.append(_rcopy(ssem, rsem, 6 * k + j, s_refs[k].at[pl.ds(c * halves[k], halves[k]), :],
                                    half(k, me, c), (px, py, c)))
        for j, (px, py) in enumerate(chips):
            sends.append(_rcopy(ssem, rsem, 6 * n + j, small_ref, osmall.at[me], (px, py, c)))
        for cp in sends:
            cp.start()
        for k in range(n):
            for j, (px, py) in enumerate(chips):
                s = 2 * px + py
                _rcopy(ssem, rsem, 6 * k + j, half(k, s, c), half(k, s, c), (x, y, c)).wait_recv()
                fwd = _rcopy(ssem, rsem, 6 * k + 3 + j, half(k, s, c), half(k, s, c), (x, y, 1 - c))
                fwd.start()
                sends.append(fwd)
        for k in range(n):
            for j, (px, py) in enumerate(chips):
                s = 2 * px + py
                _rcopy(ssem, rsem, 6 * k + 3 + j, half(k, s, 1 - c), half(k, s, 1 - c), (x, y, c)).wait_recv()
        for j, (px, py) in enumerate(chips):
            s = 2 * px + py
            _rcopy(ssem, rsem, 6 * n + j, osmall.at[s], osmall.at[s], (x, y, c)).wait_recv()
        for cp in sends:
            cp.wait_send()
        loc.wait()

    res = pl.pallas_call(
        body, name="gather_weights", interpret=False,
        out_shape=[jax.ShapeDtypeStruct((4,) + t.shape, t.dtype) for t in shards]
        + [jax.ShapeDtypeStruct((4, SW_ROWS, 1024), F32)],
        in_specs=[ANY] * (n + 1), out_specs=[ANY] * (n + 1),
        scratch_shapes=[pltpu.SemaphoreType.DMA((6 * n + 3,)), pltpu.SemaphoreType.DMA((6 * n + 3,)),
                        pltpu.SemaphoreType.DMA],
    )(*shards, small)
    return res[:n], res[n]


def _handshake(peers):
    barrier = pltpu.get_barrier_semaphore()
    for peer in peers:
        pl.semaphore_signal(barrier, inc=1, device_id=peer, device_id_type=MESH)
    pl.semaphore_wait(barrier, len(peers))


def gather_weights_beside(shards, cid, name):
    n = len(shards)
    halves = [t.shape[0] // 2 for t in shards]

    def body(*refs):
        s_refs, o_refs, ssem, rsem = refs[:n], refs[n:2 * n], refs[2 * n], refs[2 * n + 1]
        x, y, c = _place()
        me = 2 * x + y
        chips = [(1 - x, y), (x, 1 - y), (1 - x, 1 - y)]
        _handshake([(px, py, c) for px, py in chips] + [(x, y, 1 - c)])

        def half(k, s, hh):
            return o_refs[k].at[s, pl.ds(hh * halves[k], halves[k]), :]

        sends = []
        for k in range(n):
            for j, (px, py) in enumerate(chips):
                sends.append(_rcopy(ssem, rsem, 6 * k + j, s_refs[k].at[pl.ds(c * halves[k], halves[k]), :],
                                    half(k, me, c), (px, py, c)))
        for cp in sends:
            cp.start()
        for k in range(n):
            for j, (px, py) in enumerate(chips):
                s = 2 * px + py
                _rcopy(ssem, rsem, 6 * k + j, half(k, s, c), half(k, s, c), (x, y, c)).wait_recv()
                fwd = _rcopy(ssem, rsem, 6 * k + 3 + j, half(k, s, c), half(k, s, c), (x, y, 1 - c))
                fwd.start()
                sends.append(fwd)
        for k in range(n):
            for j, (px, py) in enumerate(chips):
                s = 2 * px + py
                _rcopy(ssem, rsem, 6 * k + 3 + j, half(k, s, 1 - c), half(k, s, 1 - c), (x, y, c)).wait_recv()
        for cp in sends:
            cp.wait_send()

    return pl.kernel(
        body, name=name,
        out_type=[jax.ShapeDtypeStruct((4,) + t.shape, t.dtype) for t in shards],
        mesh=plsc.ScalarSubcoreMesh(axis_name="sequencer", num_cores=1),
        scratch_types=[pltpu.SemaphoreType.DMA((6 * n,)), pltpu.SemaphoreType.DMA((6 * n,))],
        compiler_params=pltpu.CompilerParams(collective_id=cid),
    )(*shards)


def swap_halves(gs, *, name):
    n = len(gs)

    def body(*refs):
        g_refs, o_refs, ssem, rsem = refs[:n], refs[n:2 * n], refs[2 * n], refs[2 * n + 1]
        x, y, c = _place()
        cps = []
        for k in range(n):
            hk = g_refs[k].shape[1] // 2
            cps.append(_rcopy(ssem, rsem, k, g_refs[k].at[:, pl.ds((1 - c) * hk, hk), :], o_refs[k], (x, y, 1 - c)))
        for cp in cps:
            cp.start()
        for cp in cps:
            cp.wait()

    return pl.pallas_call(
        body, name=name, interpret=False,
        out_shape=[jax.ShapeDtypeStruct((4, t.shape[1] // 2, t.shape[2]), t.dtype) for t in gs],
        in_specs=[ANY] * n, out_specs=[ANY] * n,
        scratch_shapes=[pltpu.SemaphoreType.DMA((n,)), pltpu.SemaphoreType.DMA((n,))],
    )(*gs)


def _sum_rows(hk):
    return _pick(hk, (512, 352, 256, 128))


def pair_sum(g, other, c_idx, *, name):
    _, hk, width = other.shape
    tr = _sum_rows(hk)
    nbk = hk // tr

    def body(c_ref, g_ref, o_ref, out_ref):
        out_ref[...] = (g_ref[...].astype(F32) + o_ref[...].astype(F32)).astype(BF16)

    return pl.pallas_call(
        body, name=name, interpret=False,
        out_shape=jax.ShapeDtypeStruct((4, hk, width), BF16),
        grid_spec=pltpu.PrefetchScalarGridSpec(
            num_scalar_prefetch=1, grid=(4, nbk),
            in_specs=[pl.BlockSpec((1, tr, width), lambda s, i, c_ref: (s, c_ref[0] * nbk + i, 0)),
                      pl.BlockSpec((1, tr, width), lambda s, i, c_ref: (s, i, 0))],
            out_specs=pl.BlockSpec((1, tr, width), lambda s, i, c_ref: (s, i, 0))),
        compiler_params=_params(("parallel", "parallel")),
    )(c_idx, g, other)


def chip_sum(p, got, idx, *, name):
    _, hk, width = got.shape
    tr = _sum_rows(hk)
    nbk = hk // tr

    def body(idx_ref, p_ref, g_ref, out_ref):
        acc = p_ref[0].astype(F32)
        for j in range(3):
            acc = acc + g_ref[j].astype(F32)
        out_ref[0] = acc

    return pl.pallas_call(
        body, name=name, interpret=False,
        out_shape=jax.ShapeDtypeStruct((2, hk, width), F32),
        grid_spec=pltpu.PrefetchScalarGridSpec(
            num_scalar_prefetch=1, grid=(nbk,),
            in_specs=[pl.BlockSpec((1, tr, width), lambda i, idx_ref: (idx_ref[0], i, 0)),
                      pl.BlockSpec((3, tr, width), lambda i, idx_ref: (0, i, 0))],
            out_specs=pl.BlockSpec((1, tr, width), lambda i, idx_ref: (idx_ref[1], i, 0))),
        compiler_params=_params(("parallel",)),
    )(idx, p, got)


def join_halves(qs):
    n = len(qs)

    def body(*refs):
        q_refs, o_refs, ssem, rsem = refs[:n], refs[n:2 * n], refs[2 * n], refs[2 * n + 1]
        x, y, c = _place()
        cps = [_rcopy(ssem, rsem, k, q_refs[k].at[c], o_refs[k].at[c], (x, y, 1 - c)) for k in range(n)]
        for cp in cps:
            cp.start()
        for k in range(n):
            _rcopy(ssem, rsem, k, q_refs[k].at[c], o_refs[k].at[1 - c], (x, y, 1 - c)).wait_recv()
        for cp in cps:
            cp.wait_send()

    return pl.pallas_call(
        body, name="join_halves", interpret=False,
        out_shape=[jax.ShapeDtypeStruct(t.shape, t.dtype) for t in qs],
        in_specs=[ANY] * n, out_specs=[ANY] * n, input_output_aliases={k: k for k in range(n)},
        scratch_shapes=[pltpu.SemaphoreType.DMA((n,)), pltpu.SemaphoreType.DMA((n,))],
    )(*qs)


def scatter_chips_beside(ps, cid, name):
    n = len(ps)

    def body(*refs):
        p_refs, o_refs, ssem, rsem = refs[:n], refs[n:2 * n], refs[2 * n], refs[2 * n + 1]
        x, y, c = _place()
        chips = [(1 - x, y), (x, 1 - y), (1 - x, 1 - y)]
        _handshake([(px, py, c) for px, py in chips])
        cps = [_rcopy(ssem, rsem, 3 * k + j, p_refs[k].at[2 * px + py], o_refs[k].at[j], (px, py, c))
               for k in range(n) for j, (px, py) in enumerate(chips)]
        for cp in cps:
            cp.start()
        for cp in cps:
            cp.wait()

    return pl.kernel(
        body, name=name, out_type=[jax.ShapeDtypeStruct((3,) + t.shape[1:], t.dtype) for t in ps],
        mesh=plsc.ScalarSubcoreMesh(axis_name="sequencer", num_cores=1),
        scratch_types=[pltpu.SemaphoreType.DMA((3 * n,)), pltpu.SemaphoreType.DMA((3 * n,))],
        compiler_params=pltpu.CompilerParams(collective_id=cid),
    )(*ps)


def reduce_begin(gs, names, c_idx, cid, tag):
    others = swap_halves(gs, name=f"swap_halves_{tag}")
    pairs = [pair_sum(g, o, c_idx, name=f"pair_sum_{nm}") for g, o, nm in zip(gs, others, names)]
    return pairs, scatter_chips_beside(pairs, cid, f"scatter_chips_{tag}")


def reduce_end(pairs, gots, names, idx):
    mine = [chip_sum(p, g, idx, name=f"chip_sum_{nm}") for p, g, nm in zip(pairs, gots, names)]
    return [q.reshape(2 * q.shape[1], q.shape[2]) for q in join_halves(mine)]


def gather_small(v):
    def body(v_ref, o_ref, ssem, rsem, lsem):
        x, y, c = _place()
        peers = []
        for k in range(1, 8):
            fx, fy, fc = (k >> 2) & 1, (k >> 1) & 1, k & 1
            peers.append((1 - x if fx else x, 1 - y if fy else y, 1 - c if fc else c))
        _handshake(peers)
        loc = pltpu.make_async_copy(v_ref, o_ref.at[4 * x + 2 * y + c], lsem)
        loc.start()
        cps = []
        for k, (px, py, pc) in enumerate(peers):
            cps.append((pltpu.make_async_remote_copy(
                src_ref=v_ref, dst_ref=o_ref.at[4 * x + 2 * y + c], send_sem=ssem.at[k], recv_sem=rsem.at[k],
                device_id=(px, py, pc), device_id_type=MESH), 4 * px + 2 * py + pc))
        for cp, _ in cps:
            cp.start()
        for k, (cp, peer) in enumerate(cps):
            pltpu.make_async_remote_copy(
                src_ref=v_ref, dst_ref=o_ref.at[peer], send_sem=ssem.at[k], recv_sem=rsem.at[k],
                device_id=(x, y, c), device_id_type=MESH).wait_recv()
        for cp, _ in cps:
            cp.wait_send()
        loc.wait()

    return pl.kernel(
        body, name="gather_small", out_type=jax.ShapeDtypeStruct((8, SV_ROWS, 1024), F32),
        mesh=plsc.ScalarSubcoreMesh(axis_name="sequencer", num_cores=1),
        scratch_types=[pltpu.SemaphoreType.DMA((7,)), pltpu.SemaphoreType.DMA((7,)), pltpu.SemaphoreType.DMA],
        compiler_params=pltpu.CompilerParams(collective_id=6),
    )(v)


def sum_slots(a):
    def fn(i, t):
        acc = t[0]
        for k in range(1, 8):
            acc = acc + t[k]
        return acc

    return rowwise(fn, [whole(a)], [((SV_ROWS, 1024), F32, (SV_ROWS, 1024), lambda i: (0, 0), "w")], steps=1,
                   name="sum_slots")[0]


def _head_rms(x, nw):
    xs, rs = [], []
    for h in range(DN_H):
        xh = x[:, h * DN_D:(h + 1) * DN_D]
        r = lax.rsqrt(jnp.mean(xh * xh, axis=1, keepdims=True) + EPS)
        xs.append(xh * r)
        rs.append(r)
    return xs, rs


def bg_fwd(p, alog, dtb):
    rows = p.shape[0]
    tr = _pick(rows, (384, 128))

    def fn(i, x, al, dt):
        lane = lax.broadcasted_iota(jnp.int32, x.shape, 1)
        row = i + lax.broadcasted_iota(jnp.int32, x.shape, 0)
        g = -jnp.exp(al) * _softplus(x + dt)
        out = jnp.where(lane < 4, _sigmoid(x), jnp.where(lane < 8, g, 0.0))
        return jnp.where(row >= PAD, out, 0.0)

    return rowwise(fn, [cols(p, tr, 128, BG0 // 128), whole(alog), whole(dtb)], [out2d(rows, 128, F32, tr)],
                   steps=rows // tr, name="bg_fwd")[0]


def bg_bwd(p, alog, dtb, dbg):
    rows = p.shape[0]
    tr = _pick(rows, (384, 128))

    def fn(i, x, al, dt, g_in):
        lane = lax.broadcasted_iota(jnp.int32, x.shape, 1)
        row = i + lax.broadcasted_iota(jnp.int32, x.shape, 0)
        live = row >= PAD
        is_b = jnp.logical_and(live, lane < 4)
        is_g = jnp.logical_and(live, jnp.logical_and(lane >= 4, lane < 8))
        beta = _sigmoid(x)
        ea = jnp.exp(al)
        g = -ea * _softplus(x + dt)
        dalpha = jnp.where(is_g, g_in * (-ea) * _sigmoid(x + dt), 0.0)
        dx = jnp.where(is_b, g_in * beta * (1.0 - beta), dalpha)
        dal = jnp.sum(jnp.where(is_g, g_in * g, 0.0), axis=0, keepdims=True)
        return jnp.concatenate([dx, jnp.zeros(x.shape, F32)], axis=1), dal, jnp.sum(dalpha, axis=0, keepdims=True)

    return rowwise(fn, [cols(p, tr, 128, BG0 // 128), whole(alog), whole(dtb), cols(dbg, tr)],
                   [out2d(rows, 256, BF16, tr)], steps=rows // tr, name="bg_bwd",
                   accs=[((1, 128), F32), ((1, 128), F32)])


def dn_qkv_post(j, y):
    xs = _silu(y)
    sc = jnp.where(j == 0, DN_D ** -0.5, 1.0)
    outs = []
    for h in range(DN_H):
        xh = xs[:, h * DN_D:(h + 1) * DN_D]
        r = lax.rsqrt(jnp.sum(xh * xh, axis=1, keepdims=True) + EPS)
        outs.append(jnp.where(j < 2, xh * r * sc, xh))
    return jnp.concatenate(outs, axis=1), y


def dn_qkv_bwd(cq, dq, dk, dv):
    rows = cq.shape[0]
    tr = _pick(rows, (384, 128))

    def fn(i, c0, c1, c2, g0, g1, g2):
        pieces = []
        for kind, (cv, g) in enumerate(((c0, g0), (c1, g1), (c2, g2))):
            xs = _silu(cv)
            if kind < 2:
                sc = DN_D ** -0.5 if kind == 0 else 1.0
                ds = []
                for h in range(DN_H):
                    sl = slice(h * DN_D, (h + 1) * DN_D)
                    xh, gh = xs[:, sl], g[:, sl]
                    r = lax.rsqrt(jnp.sum(xh * xh, axis=1, keepdims=True) + EPS)
                    xn = xh * r
                    ds.append(sc * r * (gh - xn * jnp.sum(gh * xn, axis=1, keepdims=True)))
                dxs = jnp.concatenate(ds, axis=1)
            else:
                dxs = g
            pieces.append(dxs * _dsilu(cv))
        return jnp.concatenate(pieces, axis=1)

    ins = [cols(cq, tr, DN_DIM, k) for k in range(3)] + [cols(t, tr) for t in (dq, dk, dv)]
    return rowwise(fn, ins, [out2d(rows, 3 * DN_DIM, F32, tr)], steps=rows // tr, name="dn_qkv_bwd")[0]


def dn_out_fwd(o, p, nw):
    rows = o.shape[0]
    tr = _pick(rows, (384, 128))

    def fn(i, ov, z, w):
        xs, _ = _head_rms(ov, w)
        return jnp.concatenate(xs, axis=1) * jnp.concatenate([w] * DN_H, axis=1) * _silu(z)

    return rowwise(fn, [cols(o, tr), cols(p, tr, DN_DIM, 6), whole(nw)], [out2d(rows, DN_DIM, BF16, tr)],
                   steps=rows // tr, name="dn_out_fwd")[0]


def dn_out_bwd(o, p, nw, dymix):
    rows = o.shape[0]
    tr = _pick(rows, (384, 128))

    def fn(i, ov, z, w, dy):
        xs, rs = _head_rms(ov, w)
        sz = _silu(z)
        dn = dy * sz
        dos, dw = [], jnp.zeros((1, DN_D), F32)
        for h in range(DN_H):
            sl = slice(h * DN_D, (h + 1) * DN_D)
            gw = dn[:, sl] * w
            dos.append(rs[h] * (gw - xs[h] * jnp.mean(gw * xs[h], axis=1, keepdims=True)))
            dw = dw + jnp.sum(dn[:, sl] * xs[h], axis=0, keepdims=True)
        n = jnp.concatenate(xs, axis=1) * jnp.concatenate([w] * DN_H, axis=1)
        return jnp.concatenate(dos, axis=1), dy * n * _dsilu(z), dw

    return rowwise(fn, [cols(o, tr), cols(p, tr, DN_DIM, 6), whole(nw), cols(dymix, tr, DN_DIM, 1)],
                   [out2d(rows, DN_DIM, F32, tr), out2d(rows, DN_DIM, BF16, tr)], steps=rows // tr,
                   name="dn_out_bwd", accs=[((1, DN_D), F32)])


def conv_a_pre_bwd(dymix, cv, p):
    rows = cv.shape[0]
    tr = _pick(rows, (384, 128))

    def fn(i, dy, c, go):
        return dy * c, dy * go

    return rowwise(fn, [cols(dymix, tr, D_CONV, 0), cols(cv, tr), cols(p, tr, D_CONV, 1)],
                   [out2d(rows, D_CONV, BF16, tr), out2d(rows, D_CONV, F32, tr)], steps=rows // tr,
                   name="conv_a_pre_bwd")


def _rows8(w):
    return jnp.pad(w.astype(F32), ((0, 8 - w.shape[0]), (0, 0)))


def _lanes(v, at):
    return jnp.pad(v.astype(F32), (at, 128 - at - v.shape[0]))[None]


def add_norm(a, w, h, next_nw, *, name):
    return mm(a, w, name=name, epi=_add_norm_epi, epi_ins=[(h, lambda j: 0)], epi_consts=[next_nw],
              epi_outs=[F32, BF16])


def _add_norm_epi(row0, t, h, nw):
    x = t + h
    return x, x * lax.rsqrt(jnp.mean(x * x, axis=1, keepdims=True) + EPS) * nw


def ffn_up_conv(hn, w_up, cw8, *, name):
    rows = hn.shape[0]
    tn = w_up.shape[2]
    tm = _pick(rows, (384, 128))
    nr = rows // tm

    def body(x_ref, wg_ref, wv_ref, w_ref, ug_ref, uv_ref, gc_ref, a_ref, carry, scr):
        i = pl.program_id(1)
        x = x_ref[...]
        gate = _dot(x, wg_ref[...])
        val = _dot(x, wv_ref[...])
        ug_ref[...] = gate.astype(BF16)
        uv_ref[...] = val.astype(BF16)
        scr[0:8, :] = jnp.where(i > 0, carry[...], 0.0)
        scr[8:8 + tm, :] = gate
        carry[...] = gate[tm - 8:tm]
        y = jnp.zeros((tm, tn), F32)
        for q in range(3):
            sh = 2 - q
            y = y + w_ref[q:q + 1, :] * scr[8 - sh:8 - sh + tm, :]
        gc_ref[...] = y.astype(BF16)
        a_ref[...] = (_silu(y) * val).astype(BF16)

    half = pl.BlockSpec((tm, tn), lambda j, i: (i, j))
    return pl.pallas_call(
        body, name=name, interpret=False,
        out_shape=[jax.ShapeDtypeStruct((rows, D_FF), BF16)] * 4,
        grid=(D_FF // tn, nr),
        in_specs=[pl.BlockSpec((tm, D), lambda j, i: (i, 0)),
                  pl.BlockSpec((None, D, tn), lambda j, i: (j, 0, 0)),
                  pl.BlockSpec((None, D, tn), lambda j, i: (j + D_FF // tn, 0, 0)),
                  pl.BlockSpec((8, tn), lambda j, i: (0, j))],
        out_specs=[half] * 4,
        scratch_shapes=[pltpu.VMEM((8, tn), F32), pltpu.VMEM((tm + 8, tn), F32)],
        compiler_params=_params(("arbitrary", "arbitrary")),
    )(hn, w_up, w_up, cw8)


def ffn_down_bwd(dh, w_down, gc, uv, ug, cw8, *, name):
    rows = dh.shape[0]
    tn = D_FF // 2
    tm = _pick(rows, (384, 128))
    nr = rows // tm
    r8 = tm // 8

    def body(dh_ref, w_ref, gc_ref, uv_ref, ug_ref, halo_ref, cw_ref, du_ref, dw_ref, carry, gscr, xscr):
        ip = pl.program_id(1)
        i = nr - 1 - ip
        da = _dot(dh_ref[...].astype(BF16), w_ref[...], 1, 1)
        c, val = gc_ref[...].astype(F32), uv_ref[...].astype(F32)
        dgc = da * val * _dsilu(c)
        du_ref[:, tn:] = (da * _silu(c)).astype(BF16)
        gscr[0:tm, :] = dgc
        gscr[tm:tm + 8, :] = jnp.where(ip > 0, carry[...], 0.0)
        carry[...] = dgc[0:8]
        xscr[0:8, :] = jnp.where(i > 0, halo_ref[...].astype(F32), 0.0)
        xscr[8:8 + tm, :] = ug_ref[...].astype(F32)
        dx = jnp.zeros((tm, tn), F32)
        dws = []
        for q in range(3):
            sh = 2 - q
            dx = dx + cw_ref[q:q + 1, :] * gscr[sh:sh + tm, :]
            dws.append(jnp.sum(dgc * xscr[8 - sh:8 - sh + tm, :], axis=0, keepdims=True))
        du_ref[:, :tn] = dx.astype(BF16)

        @pl.when(ip == 0)
        def _():
            dw_ref[...] = jnp.zeros((8, tn), F32)

        dw_ref[...] += jnp.concatenate(dws + [jnp.zeros((5, tn), F32)], axis=0)

    rev = lambda ip: nr - 1 - ip
    tile = lambda arr: pl.BlockSpec((tm, tn), lambda j, ip: (rev(ip), j))
    return pl.pallas_call(
        body, name=name, interpret=False,
        out_shape=[jax.ShapeDtypeStruct((rows, 2 * D_FF), BF16), jax.ShapeDtypeStruct((8, D_FF), F32)],
        grid=(2, nr),
        in_specs=[pl.BlockSpec((tm, D), lambda j, ip: (rev(ip), 0)),
                  pl.BlockSpec((tn, D), lambda j, ip: (j, 0)),
                  tile(gc), tile(uv), tile(ug),
                  pl.BlockSpec((8, tn), lambda j, ip: (jnp.maximum(rev(ip) * r8 - 1, 0), j)),
                  pl.BlockSpec((8, tn), lambda j, ip: (0, j))],
        out_specs=[pl.BlockSpec((tm, 2 * tn), lambda j, ip: (rev(ip), j)),
                   pl.BlockSpec((8, tn), lambda j, ip: (0, j))],
        scratch_shapes=[pltpu.VMEM((8, tn), F32), pltpu.VMEM((tm + 8, tn), F32), pltpu.VMEM((tm + 8, tn), F32)],
        compiler_params=_params(("arbitrary", "arbitrary")),
    )(dh, w_down, gc, uv, ug, ug, cw8)


def ffn_fwd(h, hn, w_up, cw8, w_down, tag, next_nw=None, target=None):
    ug, uv, gc, a = ffn_up_conv(hn, w_up, cw8, name=f"ffn{tag}_up")
    if target is not None:
        out, hn_next = add_loss(a, w_down, h, target, name=f"ffn{tag}_down")
    else:
        out, hn_next = add_norm(a, w_down, h, next_nw, name=f"ffn{tag}_down")
    return out, hn_next, (hn, ug, uv, a, gc)


def ffn_bwd(h, nw, w_up, cw8, w_down, saved, dh, tag):
    hn, ug, uv, a, gc = saved
    du, d_cw = ffn_down_bwd(dh, w_down, gc, uv, ug, cw8, name=f"ffn{tag}_down_dx")
    d_w_down = mm(a, dh, ta=True, out_dtype=BF16, name=f"ffn{tag}_down_dw")
    dh_new, d_nw = dx_rms_bwd(du, w_up, h, nw, dh, name=f"ffn{tag}_up_dx", b_chip=True, swap_mid=True)
    d_w_up = mm(hn, du, ta=True, out_dtype=BF16, out_chip=True, swap_mid=True, name=f"ffn{tag}_up_dw")
    return dh_new, d_nw, d_w_up, d_cw, d_w_down


def mixer_fwd(h, nw, w_in, ca8, dc8, alog, dtb, dnw, w_out, tie=None, next_nw=None):
    rows = h.shape[0]
    tr = _pick(rows, (384, 128))
    hn = rms_fwd(h, nw, name="mix_norm")
    if callable(w_in):
        hn, w_in = w_in(hn)
    p = mm(hn, w_in, name="mix_in")
    y_a, cv = conv_fwd([(p, 0), (p, 2)], ca8, 3, rows=rows, c=D_CONV, tc=D_CONV, tr=tr, name="conv_a",
                       pre=lambda gi, ah: gi * ah, post=lambda j, y, go: (go * y, y), extras=[(p, 1)],
                       outs=[BF16, F32])
    qkv_n, cq = conv_fwd([(p, 3)], dc8, 4, rows=rows, c=3 * DN_DIM, tc=DN_DIM, tr=tr, name="dn_conv",
                         post=dn_qkv_post, outs=[F32, F32], strip=tr)
    bgcol = bg_fwd(p, alog, dtb)
    if tie is not None:
        bgcol = tie(bgcol)
    bgrow = bgcol[:, :8].reshape(rows // CH, CH, 8).transpose(0, 2, 1)
    o, s_all, ti_all = dn_fwd(qkv_n, bgcol, bgrow)
    y_b = dn_out_fwd(o, p, dnw)
    ymix = jnp.concatenate([y_a, y_b], axis=1)
    w_out = w_out() if callable(w_out) else w_out
    out, hn_next = add_norm(ymix, w_out, h, next_nw, name="mix_out")
    return out, hn_next, (hn, p, cv, qkv_n, cq, bgcol, bgrow, o, s_all, ti_all, ymix, w_in)


def mixer_bwd(h, nw, ca8, dc8, alog, dtb, dnw, w_out, saved, dh):
    hn, p, cv, qkv_n, cq, bgcol, bgrow, o, s_all, ti_all, ymix, w_in = saved
    rows = h.shape[0]
    tr = _pick(rows, (384, 128))
    dymix = mm(dh, w_out, tb=True, name="mix_out_dx")
    d_w_out = mm(ymix, dh, ta=True, out_dtype=BF16, name="mix_out_dw")
    do, dz, d_dnw = dn_out_bwd(o, p, dnw, dymix)
    dq, dk, dv, dbg = dn_bwd(qkv_n, bgcol, bgrow, s_all, ti_all, do)
    dbg_p, d_alog, d_dtb = bg_bwd(p, alog, dtb, dbg)
    dcq = dn_qkv_bwd(cq, dq, dk, dv)
    dqkv, d_dc = conv_bwd([(p, 3)], dc8, 4, dcq, rows=rows, c=3 * DN_DIM, tc=DN_DIM, tr=tr, name="dn_conv_bwd",
                          post=lambda dx: dx, outs=[BF16])
    dgo, dcv = conv_a_pre_bwd(dymix, cv, p)
    dgi, dah, d_ca = conv_bwd([(p, 0), (p, 2)], ca8, 3, dcv, rows=rows, c=D_CONV, tc=D_CONV, tr=tr,
                              name="conv_a_bwd", pre=lambda gi, ah: gi * ah,
                              post=lambda dm, gi, ah: (dm * ah, dm * gi), extras=[(p, 0), (p, 2)], outs=[BF16, BF16])
    dp = jnp.concatenate([dgi, dgo, dah, dqkv, dz, dbg_p], axis=1)
    dh_new, d_nw = dx_rms_bwd(dp, w_in, h, nw, dh, name="mix_in_dx")
    d_w_in = mm(hn, dp, ta=True, out_dtype=BF16, name="mix_in_dw")
    return dh_new, d_nw, d_w_in, d_ca, d_dc, d_alog, d_dtb, d_dnw, d_w_out


def swa_layer_fwd(h, hn, wqkv, qw, kw, sinks, wo, next_nw):
    qkv = mm(hn, wqkv, name="swa_qkv")
    qh, kh, vh = qknorm_fwd(qkv, qw, kw)
    att = swa_fwd(qh, kh, vh, sinks)
    out, hn_next = add_norm(att, wo, h, next_nw, name="swa_out")
    return out, hn_next, (hn, qkv, qh, kh, vh, att)


def swa_layer_bwd(h, nw, wqkv, qw, kw, sinks, wo, saved, dh):
    hn, qkv, qh, kh, vh, att = saved
    datt = mm(dh, wo, tb=True, out_dtype=BF16, name="swa_out_dx")
    d_wo = mm(att, dh, ta=True, out_dtype=BF16, name="swa_out_dw")
    dqh, dkh, dvh, dsk = swa_bwd(qh, kh, vh, sinks, datt)
    dqkv, d_qw, d_kw = qknorm_bwd(qkv, qw, kw, dqh, dkh, dvh)
    dh_new, d_nw = dx_rms_bwd(dqkv, wqkv, h, nw, dh, name="swa_qkv_dx")
    d_wqkv = mm(hn, dqkv, ta=True, out_dtype=BF16, name="swa_qkv_dw")
    d_sinks = jnp.sum(dsk[:, :, 0], axis=0)
    return dh_new, d_nw, d_wqkv, d_qw, d_kw, d_sinks, d_wo


BIG = ("mix_w_in", "mix_w_out", "swa_wq", "swa_wk", "swa_wv", "swa_wo", "ffn_w_up", "ffn_w_down")


def _flat_pad(parts, rows):
    v = jnp.concatenate([t.astype(F32).reshape(-1) for t in parts])
    return jnp.pad(v, (0, rows * 1024 - v.shape[0])).reshape(rows, 1024)


def _split_flat(flat, shapes):
    v = flat.reshape(-1)
    out, o = [], 0
    for s in shapes:
        n = 1
        for d_ in s:
            n *= d_
        out.append(v[o:o + n].reshape(s))
        o += n
    return out


def local_step(x0, target0, meta_full, anw, fnw, w_in, ca8, dc8, alog, dtb, dnw, qw, kw, sinks, fc8, late,
               begin=None, tie=None):
    begin = begin or (lambda tag, names, grads: None)
    h0 = jnp.concatenate([jnp.zeros((PAD, D), F32), meta_full, x0], axis=0)
    h1, hn1, s_mix = mixer_fwd(h0, anw[0], w_in, ca8, dc8, alog, dtb, dnw, lambda: late()[0], tie, fnw[0])
    w_out, wqkv, wo, w_up, w_down = late()
    h2, hn2, s_f0 = ffn_fwd(h1, hn1, w_up[0], fc8[0], w_down[0], 0, anw[1])
    h3, hn3, s_swa = swa_layer_fwd(h2, hn2, wqkv, qw, kw, sinks, wo, fnw[1])
    dh, loss_l, s_f1 = ffn_fwd(h3, hn3, w_up[1], fc8[1], w_down[1], 1, target=target0)
    dh, d_fnw1, d_up1, d_fc1, d_down1 = ffn_bwd(h3, fnw[1], w_up[1], fc8[1], w_down[1], s_f1, dh, 1)
    begin("ffn1", ("up1", "down1"), [d_up1, d_down1.reshape(4, 704, D)])
    dh, d_anw1, d_wqkv, d_qw, d_kw, d_sinks, d_wo = swa_layer_bwd(h2, anw[1], wqkv, qw, kw, sinks, wo, s_swa, dh)
    begin("swa", ("wq", "wk", "wv", "wo"),
          [d_wqkv[:, :D].reshape(4, 256, D), d_wqkv[:, D:D + 256].reshape(4, 256, 256),
           d_wqkv[:, D + 256:].reshape(4, 256, 256), d_wo.reshape(4, 256, D)])
    dh, d_fnw0, d_up0, d_fc0, d_down0 = ffn_bwd(h1, fnw[0], w_up[0], fc8[0], w_down[0], s_f0, dh, 0)
    begin("ffn0", ("up0", "down0"), [d_up0, d_down0.reshape(4, 704, D)])
    dh, d_anw0, d_w_in, d_ca, d_dc, d_alog, d_dtb, d_dnw, d_w_out = mixer_bwd(
        h0, anw[0], ca8, dc8, alog, dtb, dnw, w_out, s_mix, dh)
    begin("mix", ("w_in", "w_out"),
          [d_w_in[:, :IN_DIM].reshape(D, 4, 898).transpose(1, 0, 2), d_w_out.reshape(4, 256, D)])
    return (dh, loss_l, d_anw0, d_anw1, d_fnw0, d_fnw1, d_w_in, d_ca, d_dc, d_alog, d_dtb, d_dnw, d_w_out, d_wqkv,
            d_qw, d_kw, d_sinks, d_wo, d_up0, d_up1, d_fc0, d_fc1, d_down0, d_down1)


def kernel(x, meta_tokens, attn_norm_w, ffn_norm_w, mix_w_in, conv_a_w, dn_conv_w, dn_a_log, dn_dt_bias, dn_norm_w, mix_w_out, swa_wq, swa_wk, swa_wv, swa_q_norm_w, swa_k_norm_w, swa_sinks, swa_wo, ffn_w_up, ffn_conv_w, ffn_w_down, loss_target, m_meta_tokens, m_attn_norm_w, m_ffn_norm_w, m_mix_w_in, m_conv_a_w, m_dn_conv_w, m_dn_a_log, m_dn_dt_bias, m_dn_norm_w, m_mix_w_out, m_swa_wq, m_swa_wk, m_swa_wv, m_swa_q_norm_w, m_swa_k_norm_w, m_swa_sinks, m_swa_wo, m_ffn_w_up, m_ffn_conv_w, m_ffn_w_down, v_meta_tokens, v_attn_norm_w, v_ffn_norm_w, v_mix_w_in, v_conv_a_w, v_dn_conv_w, v_dn_a_log, v_dn_dt_bias, v_dn_norm_w, v_mix_w_out, v_swa_wq, v_swa_wk, v_swa_wv, v_swa_q_norm_w, v_swa_k_norm_w, v_swa_sinks, v_swa_wo, v_ffn_w_up, v_ffn_conv_w, v_ffn_w_down):
    ix, iy, ic = lax.axis_index("x"), lax.axis_index("y"), lax.axis_index("c")
    chip = 2 * ix + iy
    seq = x.shape[1]
    rows = HEAD0 + seq

    small_sharded = (conv_a_w, dn_conv_w, ffn_conv_w, meta_tokens)
    up_b, down_b = ffn_w_up.astype(BF16), ffn_w_down.astype(BF16)
    own = [mix_w_in[0].astype(BF16), mix_w_out[0].astype(BF16), swa_wq[0].astype(BF16), swa_wk[0].astype(BF16),
           swa_wv[0].astype(BF16), swa_wo[0].astype(BF16), up_b[0], up_b[1], down_b[0], down_b[1]]
    fill = lambda gathered, mine: [lax.dynamic_update_slice_in_dim(g, t[None], chip, axis=0)
                                   for g, t in zip(gathered, mine)]
    on_its_way, = gather_weights_beside(own[:1], 9, "gather_w_in")
    _, g_small = gather_weights([], _flat_pad(small_sharded, SW_ROWS))

    def w_in(hn):
        hn, got = lax.optimization_barrier((hn, on_its_way))
        g_in, = fill([got], own[:1])
        return hn, jnp.pad(g_in.transpose(1, 0, 2).reshape(D, IN_DIM), ((0, 0), (0, P_W - IN_DIM)))
    rest = {}

    def tie(t):
        t, *mine = lax.optimization_barrier((t, *own[1:]))
        g_out, g_q, g_k, g_v, g_o, g_up0, g_up1, g_dn0, g_dn1 = mine
        soon, last = [g_up0, g_dn0, g_q, g_k, g_v, g_o], [g_up1, g_dn1]
        rest["w_out"] = fill(gather_weights_beside([g_out], 1, "gather_w_out"), [g_out])
        rest["soon"] = fill(gather_weights_beside(soon, 7, "gather_layers_12"), soon)
        rest["last"] = fill(gather_weights_beside(last, 8, "gather_layer_3"), last)
        return t

    def late():
        (g_out,), (g_up0, g_dn0, g_q, g_k, g_v, g_o), (g_up1, g_dn1) = rest["w_out"], rest["soon"], rest["last"]
        wqkv = jnp.concatenate([g_q.reshape(D, D), g_k.reshape(D, 256), g_v.reshape(D, 256)], axis=1)
        return (g_out.reshape(D, D), wqkv, g_o.reshape(D, D), [g_up0, g_up1],
                [g_dn0.reshape(D_FF, D), g_dn1.reshape(D_FF, D)])

    gs = g_small.reshape(4, -1)
    ca_full = gs[:, 0:384].reshape(4, 3, 128).transpose(1, 0, 2).reshape(3, D_CONV)
    dc_full = gs[:, 384:1920].reshape(4, 4, 384).transpose(1, 0, 2).reshape(4, 3 * DN_DIM)
    fc_full = gs[:, 1920:6144].reshape(4, 2, 3, 704).transpose(1, 2, 0, 3).reshape(2, 3, D_FF)
    meta_full = gs[:, 6144:10240].reshape(4, N_META, 256).transpose(1, 0, 2).reshape(N_META, D)
    ca8, dc8 = _rows8(ca_full), _rows8(dc_full)
    fc8 = [_rows8(fc_full[0]), _rows8(fc_full[1])]
    alog, dtb = _lanes(dn_a_log[0], 4), _lanes(dn_dt_bias[0], 4)
    dnw = dn_norm_w.astype(F32)
    qw, kw = swa_q_norm_w.astype(F32), swa_k_norm_w.astype(F32)
    sinks = swa_sinks[0].astype(F32)
    anw = [attn_norm_w[0:1], attn_norm_w[1:2]]
    fnw = [ffn_norm_w[0:1], ffn_norm_w[1:2]]

    c_idx = jnp.reshape(ic, (1,)).astype(jnp.int32)
    chip_idx = jnp.stack([chip, ic]).astype(jnp.int32)
    begun = []

    def begin(tag, names, grads):
        pairs, gots = reduce_begin(grads, names, c_idx, 2 + len(begun), tag)
        begun.append((names, pairs, gots))

    (dh, loss_l, d_anw0, d_anw1, d_fnw0, d_fnw1, d_w_in, d_ca, d_dc, d_alog, d_dtb, d_dnw, d_w_out, d_wqkv, d_qw,
     d_kw, d_sinks, d_wo, d_up0, d_up1, d_fc0, d_fc1, d_down0, d_down1) = local_step(
        x[0], loss_target[0], meta_full, anw, fnw, w_in, ca8, dc8, alog, dtb, dnw, qw, kw, sinks, fc8, late,
        begin, tie)
    grad_x = dh[HEAD0:][None]

    small_parts = [jnp.concatenate([d_anw0, d_anw1], axis=0), jnp.concatenate([d_fnw0, d_fnw1], axis=0),
                   d_alog[0, 4:8], d_dtb[0, 4:8], d_dnw, d_qw, d_kw, d_sinks,
                   d_ca[:3], d_dc[:4], jnp.stack([d_fc0[:3], d_fc1[:3]]), dh[PAD:HEAD0], loss_l[0, 0:1]]
    small_shapes = [(2, D), (2, D), (1, 4), (1, 4), (1, DN_D), (1, SWA_D), (1, SWA_D), (1, SWA_H),
                    (1, 3, D_CONV), (1, 4, 3 * DN_DIM), (2, 3, D_FF), (N_META, D), ()]
    gathered_small = gather_small(_flat_pad(small_parts, SV_ROWS))

    red_big = {}
    for part in (begun[:-1], begun[-1:]):
        part_names = [n for names, _, _ in part for n in names]
        red_big.update(zip(part_names, reduce_end([p for _, ps, _ in part for p in ps],
                                                  [g for _, _, gs_ in part for g in gs_], part_names, chip_idx)))
    g_w_in, g_w_out, g_wq, g_wk, g_wv, g_wo, g_up0, g_up1, g_dn0, g_dn1 = [
        red_big[n] for n in ("w_in", "w_out", "wq", "wk", "wv", "wo", "up0", "up1", "down0", "down1")]

    grads = dict(mix_w_in=g_w_in, mix_w_out=g_w_out, swa_wq=g_wq, swa_wk=g_wk, swa_wv=g_wv, swa_wo=g_wo,
                 ffn_w_up=[g_up0, g_up1], ffn_w_down=[g_dn0, g_dn1])
    weights = dict(meta_tokens=meta_tokens, attn_norm_w=attn_norm_w, ffn_norm_w=ffn_norm_w, mix_w_in=mix_w_in,
                   conv_a_w=conv_a_w, dn_conv_w=dn_conv_w, dn_a_log=dn_a_log, dn_dt_bias=dn_dt_bias,
                   dn_norm_w=dn_norm_w, mix_w_out=mix_w_out, swa_wq=swa_wq, swa_wk=swa_wk, swa_wv=swa_wv,
                   swa_q_norm_w=swa_q_norm_w, swa_k_norm_w=swa_k_norm_w, swa_sinks=swa_sinks, swa_wo=swa_wo,
                   ffn_w_up=ffn_w_up, ffn_conv_w=ffn_conv_w, ffn_w_down=ffn_w_down)
    m_in = dict(meta_tokens=m_meta_tokens, attn_norm_w=m_attn_norm_w, ffn_norm_w=m_ffn_norm_w, mix_w_in=m_mix_w_in,
                conv_a_w=m_conv_a_w, dn_conv_w=m_dn_conv_w, dn_a_log=m_dn_a_log, dn_dt_bias=m_dn_dt_bias,
                dn_norm_w=m_dn_norm_w, mix_w_out=m_mix_w_out, swa_wq=m_swa_wq, swa_wk=m_swa_wk, swa_wv=m_swa_wv,
                swa_q_norm_w=m_swa_q_norm_w, swa_k_norm_w=m_swa_k_norm_w, swa_sinks=m_swa_sinks, swa_wo=m_swa_wo,
                ffn_w_up=m_ffn_w_up, ffn_conv_w=m_ffn_conv_w, ffn_w_down=m_ffn_w_down)
    v_in = dict(meta_tokens=v_meta_tokens, attn_norm_w=v_attn_norm_w, ffn_norm_w=v_ffn_norm_w, mix_w_in=v_mix_w_in,
                conv_a_w=v_conv_a_w, dn_conv_w=v_dn_conv_w, dn_a_log=v_dn_a_log, dn_dt_bias=v_dn_dt_bias,
                dn_norm_w=v_dn_norm_w, mix_w_out=v_mix_w_out, swa_wq=v_swa_wq, swa_wk=v_swa_wk, swa_wv=v_swa_wv,
                swa_q_norm_w=v_swa_q_norm_w, swa_k_norm_w=v_swa_k_norm_w, swa_sinks=v_swa_sinks, swa_wo=v_swa_wo,
                ffn_w_up=v_ffn_w_up, ffn_conv_w=v_ffn_conv_w, ffn_w_down=v_ffn_w_down)
    names = list(weights)
    small = [n for n in names if n not in BIG]
    delta, new_m, new_v = {}, {}, {}
    for n in BIG:
        delta[n], new_m[n], new_v[n], grads[n] = adamw(weights[n], grads[n], m_in[n], v_in[n], name=f"adamw_{n}")
    gathered_small, _ = lax.optimization_barrier((gathered_small, new_v["ffn_w_down"]))
    (g_anw, g_fnw, g_alog, g_dtb, g_dnw, g_qw, g_kw, g_sinks, g_ca_f, g_dc_f, g_fc_f, g_meta_f,
     loss) = _split_flat(sum_slots(gathered_small), small_shapes)
    grads.update(meta_tokens=lax.dynamic_slice_in_dim(g_meta_f, chip * 256, 256, axis=1), attn_norm_w=g_anw,
                 ffn_norm_w=g_fnw, conv_a_w=lax.dynamic_slice_in_dim(g_ca_f, chip * 128, 128, axis=2),
                 dn_conv_w=lax.dynamic_slice_in_dim(g_dc_f, chip * 384, 384, axis=2), dn_a_log=g_alog,
                 dn_dt_bias=g_dtb, dn_norm_w=g_dnw, swa_q_norm_w=g_qw, swa_k_norm_w=g_kw, swa_sinks=g_sinks,
                 ffn_conv_w=lax.dynamic_slice_in_dim(g_fc_f, chip * 704, 704, axis=2))
    grads = {n: grads[n].reshape(weights[n].shape) for n in names}
    shapes = [weights[n].shape for n in small]
    packed = [_flat_pad([t[n] for n in small], SW_ROWS) for t in (weights, grads, m_in, v_in)]
    for store, flat in zip((delta, new_m, new_v), adamw(*packed, name="adamw_small")):
        for n, t in zip(small, _split_flat(flat, shapes)):
            store[n] = t
    return (loss, grad_x, *[grads[n] for n in names], *[delta[n] for n in names],
            *[new_m[n] for n in names], *[new_v[n] for n in names])
```

```python
import functools

import jax
import jax.numpy as jnp
from jax import lax
from jax.experimental import pallas as pl
from jax.experimental.pallas import tpu as pltpu
from jax.experimental.pallas import tpu_sc as plsc

F32 = jnp.float32
BF16 = jnp.bfloat16
HI = lax.Precision.HIGHEST
MESH = pl.DeviceIdType.MESH

D = 1024
N_META = 16
PAD = 112
HEAD0 = PAD + N_META
D_CONV = 512
DN_H = 4
DN_D = 128
DN_DIM = 512
CH = 64
IN_DIM = 3592
P_W = 3840
BG0 = 3584
SWA_H = 16
SWA_KV = 4
SWA_D = 64
BLK = 128
NKEY = N_META + 2 * BLK
D_FF = 2816
EPS = 1e-6
LR, B1, B2, AEPS, WD, STEP = 0.001, 0.9, 0.999, 1e-08, 0.01, 10
VMEM_LIMIT = 48 * 1024 * 1024
MM_VMEM_BUDGET = 34 * 1024 * 1024
R_BIG = 6144
R_HALF = R_BIG // 2
SV_ROWS = 48
SW_ROWS = 16


def _pick(n, cands):
    for c in cands:
        if n % c == 0:
            return c
    return n


def _params(sem=None):
    return pltpu.CompilerParams(dimension_semantics=sem, vmem_limit_bytes=VMEM_LIMIT)


def _dot(a, b, ca=1, cb=0, prec=None):
    return lax.dot_general(a, b, (((ca,), (cb,)), ((), ())), precision=prec,
                           preferred_element_type=F32)


def _sigmoid(x):
    return 1.0 / (1.0 + jnp.exp(-x))


def _silu(x):
    return x * _sigmoid(x)


def _dsilu(x):
    s = _sigmoid(x)
    return s * (1.0 + x * (1.0 - s))


def _softplus(x):
    return jnp.maximum(x, 0.0) + jnp.log(1.0 + jnp.exp(-jnp.abs(x)))


def mm(a, b, *, name, ta=False, tb=False, out_dtype=F32, add=None, tm=None, tn=None, tk=None,
       b_chip=False, out_chip=False, swap_mid=False, epi=None, epi_ins=(), epi_consts=(), epi_outs=(), epi_accs=()):
    if epi is not None:
        return _mm_epi(a, b, name=name, tb=tb, tn=tn, b_chip=b_chip, swap_mid=swap_mid, epi=epi, epi_ins=epi_ins,
                       epi_consts=epi_consts, epi_outs=epi_outs, epi_accs=epi_accs)
    chip_of = _chip_order(swap_mid)
    m, k = (a.shape[1], a.shape[0]) if ta else a.shape
    if b_chip:
        n = b.shape[1] if tb else 4 * b.shape[2]
        if tb:
            tk = b.shape[2]
        else:
            tn = b.shape[2]
    else:
        n = b.shape[0] if tb else b.shape[1]
    if out_chip:
        tn = n // 4
    tn = tn or _pick(n, (1408, 1024, 768, 512, 256, 128))
    tk = tk or (_pick(k, (1408, 704, 384, 128)) if ta else _pick(k, (1024, 1408, 768, 512, 128)))
    nk = k // tk
    if tm is None:
        isz = lambda t: jnp.dtype(t.dtype).itemsize
        osz = jnp.dtype(out_dtype).itemsize
        for tm in ((1408, 1024, 512, 384, 256, 128) if ta else (1408, 704, 512, 384, 256, 128)):
            need = 2 * (tm * tk * isz(a) + tk * tn * isz(b) + tm * tn * osz + (tm * tn * 4 if add is not None else 0))
            need += tm * tn * 4 if nk > 1 else 0
            if m % tm == 0 and need <= MM_VMEM_BUDGET:
                break
        else:
            tm = m
    dims = (((0 if ta else 1,), (1 if tb else 0,)), ((), ()))

    def body(*refs):
        if add is None:
            a_ref, b_ref, o_ref, acc_ref = refs
            add_ref = None
        else:
            a_ref, b_ref, add_ref, o_ref, acc_ref = refs
        part = lax.dot_general(a_ref[...].astype(BF16), b_ref[...].astype(BF16), dims,
                               preferred_element_type=F32)

        def finish(total):
            if add_ref is not None:
                total = total + add_ref[...]
            o_ref[...] = total.astype(out_dtype)

        if nk == 1:
            finish(part)
        else:
            kk = pl.program_id(2)

            @pl.when(kk == 0)
            def _():
                acc_ref[...] = part

            @pl.when(kk > 0)
            def _():
                acc_ref[...] += part

            @pl.when(kk == nk - 1)
            def _():
                finish(acc_ref[...])

    a_spec = pl.BlockSpec((tk, tm), lambda i, j, kk: (kk, i)) if ta else pl.BlockSpec((tm, tk), lambda i, j, kk: (i, kk))
    if b_chip and tb:
        b_spec = pl.BlockSpec((None, tn, tk), lambda i, j, kk: (chip_of(kk), j, 0))
    elif b_chip:
        b_spec = pl.BlockSpec((None, tk, tn), lambda i, j, kk: (j, kk, 0))
    elif tb:
        b_spec = pl.BlockSpec((tn, tk), lambda i, j, kk: (j, kk))
    else:
        b_spec = pl.BlockSpec((tk, tn), lambda i, j, kk: (kk, j))
    o_spec = pl.BlockSpec((tm, tn), lambda i, j, kk: (i, j))
    in_specs = [a_spec, b_spec] + ([o_spec] if add is not None else [])
    args = [a, b] + ([add] if add is not None else [])
    out_spec = pl.BlockSpec((None, tm, tn), lambda i, j, kk: (chip_of(j), i, 0)) if out_chip else o_spec
    return pl.pallas_call(
        body, name=name, interpret=False,
        out_shape=jax.ShapeDtypeStruct((4, m, tn) if out_chip else (m, n), out_dtype),
        grid=(m // tm, n // tn, nk), in_specs=in_specs, out_specs=out_spec,
        scratch_shapes=[pltpu.VMEM((tm, tn) if nk > 1 else (8, 128), F32)],
        compiler_params=_params(("parallel", "parallel", "arbitrary")),
    )(*args)


def _chip_order(swap_mid):
    return (lambda k: (k % 2) * 2 + k // 2) if swap_mid else (lambda k: k)


def _mm_epi(a, b, *, name, tb, tn, b_chip, epi, epi_ins, epi_consts, epi_outs, epi_accs, swap_mid=False):
    chip_of = _chip_order(swap_mid)
    m, k = a.shape
    if b_chip:
        n = b.shape[1] if tb else 4 * b.shape[2]
        tk = b.shape[2] if tb else None
        tn = tn if tb else b.shape[2]
    else:
        n = b.shape[0] if tb else b.shape[1]
        tk = None
    tn = tn or _pick(n, (1408, 1024, 768, 512, 256, 128))
    tk = tk or _pick(k, (1024, 1408, 1280, 768, 512, 128))
    nk, nj = k // tk, n // tn
    isz = lambda t: jnp.dtype(t.dtype if hasattr(t, "dtype") else t).itemsize
    outs3 = [t if isinstance(t, tuple) else (t, n, lambda j: j) for t in epi_outs]
    side = sum(isz(t) for t, _ in epi_ins) + sum(isz(dt) for dt, _, _ in outs3)
    for tm in (1408, 704, 512, 384, 256, 128):
        need = 2 * (tm * tk * isz(a) + tk * tn * isz(b) + tm * tn * side) + (tm * tn * 4 if nk > 1 else 0)
        if m % tm == 0 and need <= MM_VMEM_BUDGET:
            break
    else:
        tm = m
    dims = (((1,), (1 if tb else 0,)), ((), ()))
    n_in, n_c, n_out, n_acc = len(epi_ins), len(epi_consts), len(epi_outs), len(epi_accs)

    def body(*refs):
        a_ref, b_ref = refs[:2]
        in_refs = refs[2:2 + n_in + n_c]
        out_refs = refs[2 + n_in + n_c:2 + n_in + n_c + n_out]
        acc_out = refs[2 + n_in + n_c + n_out:2 + n_in + n_c + n_out + n_acc]
        acc_ref = refs[-1]
        i, j, kk = pl.program_id(0), pl.program_id(1), pl.program_id(2)
        part = lax.dot_general(a_ref[...].astype(BF16), b_ref[...].astype(BF16), dims,
                               preferred_element_type=F32)

        def finish(total):
            res = epi(i * tm, total, *[r[...] for r in in_refs])
            if not isinstance(res, (tuple, list)):
                res = (res,)
            for r, v in zip(out_refs, res[:n_out]):
                r[...] = v.astype(r.dtype)
            if n_acc:
                @pl.when(jnp.logical_and(i == 0, j == 0))
                def _():
                    for r in acc_out:
                        r[...] = jnp.zeros(r.shape, r.dtype)

                for r, v in zip(acc_out, res[n_out:]):
                    r[...] += jnp.broadcast_to(v, r.shape).astype(r.dtype)

        if nk == 1:
            finish(part)
        else:
            @pl.when(kk == 0)
            def _():
                acc_ref[...] = part

            @pl.when(kk > 0)
            def _():
                acc_ref[...] += part

            @pl.when(kk == nk - 1)
            def _():
                finish(acc_ref[...])

    a_spec = pl.BlockSpec((tm, tk), lambda i, j, kk: (i, kk))
    if b_chip and tb:
        b_spec = pl.BlockSpec((None, tn, tk), lambda i, j, kk: (chip_of(kk), j, 0))
    elif b_chip:
        b_spec = pl.BlockSpec((None, tk, tn), lambda i, j, kk: (j, kk, 0))
    elif tb:
        b_spec = pl.BlockSpec((tn, tk), lambda i, j, kk: (j, kk))
    else:
        b_spec = pl.BlockSpec((tk, tn), lambda i, j, kk: (kk, j))
    in_specs = [a_spec, b_spec]

    def in_spec(t, col):
        front = m - t.shape[0]
        if not front:
            return pl.BlockSpec((tm, tn), lambda i, j, kk: (i, col(j)))
        return pl.BlockSpec((pl.Element(tm), pl.Element(tn)),
                            lambda i, j, kk: (pl.multiple_of(jnp.maximum(i * tm - front, 0), 8), col(j) * tn))

    in_specs += [in_spec(t, col) for t, col in epi_ins]
    in_specs += [pl.BlockSpec(t.shape, lambda i, j, kk, nd=t.ndim: (0,) * nd) for t in epi_consts]
    out_specs = [pl.BlockSpec((tm, tn), lambda i, j, kk, col=col: (i, col(j))) for _, _, col in outs3]
    out_specs += [pl.BlockSpec(s, lambda i, j, kk, nd=len(s): (0,) * nd) for s, _ in epi_accs]
    out_shape = [jax.ShapeDtypeStruct((m, width), dt) for dt, width, _ in outs3]
    out_shape += [jax.ShapeDtypeStruct(s, dt) for s, dt in epi_accs]
    sem = ("arbitrary", "arbitrary", "arbitrary") if n_acc else ("parallel", "parallel", "arbitrary")
    return pl.pallas_call(
        body, name=name, interpret=False, out_shape=out_shape,
        grid=(m // tm, nj, nk), in_specs=in_specs, out_specs=out_specs,
        scratch_shapes=[pltpu.VMEM((tm, tn) if nk > 1 else (8, 128), F32)],
        compiler_params=_params(sem),
    )(a, b, *[t for t, _ in epi_ins], *epi_consts)


def cols(arr, tr, width=None, cb=0):
    width = width or arr.shape[1]
    return (arr, (tr, width), lambda i: (i, cb), "r2")


def heads(arr, tr):
    return (arr, (arr.shape[0], tr, arr.shape[2]), lambda i: (0, i, 0), "r3")


def whole(arr):
    nd = arr.ndim
    return (arr, arr.shape, lambda i: (0,) * nd, "w")


STRIP = 16


def _rows_of(ref, kind, r0, n):
    if kind == "r2":
        return ref[pl.ds(r0, n), :]
    if kind == "r3":
        return ref[:, pl.ds(r0, n), :]
    return ref[...]


def _set_rows(ref, kind, r0, n, v):
    if kind == "r2":
        ref[pl.ds(r0, n), :] = v.astype(ref.dtype)
    elif kind == "r3":
        ref[:, pl.ds(r0, n), :] = v.astype(ref.dtype)
    else:
        ref[...] = v.astype(ref.dtype)


def rowwise(fn, ins, outs, *, steps, name, accs=(), strip=None):
    n_in, n_out, n_acc = len(ins), len(outs), len(accs)
    kin = [t[3] for t in ins]
    kout = [t[4] for t in outs]
    tr = next((t[1][-2] for t in ins if t[3] != "w"), 0)

    def body(*refs):
        i = pl.program_id(0)
        in_refs, out_refs, acc_refs = refs[:n_in], refs[n_in:n_in + n_out], refs[n_in + n_out:]
        if n_acc:
            @pl.when(i == 0)
            def _():
                for r in acc_refs:
                    r[...] = jnp.zeros(r.shape, r.dtype)

        def run(r0, n):
            res = fn(i * tr + r0, *[_rows_of(r, k, r0, n) for r, k in zip(in_refs, kin)])
            if not isinstance(res, (tuple, list)):
                res = (res,)
            for r, k, v in zip(out_refs, kout, res[:n_out]):
                _set_rows(r, k, r0, n, v)
            for r, v in zip(acc_refs, res[n_out:]):
                r[...] += jnp.broadcast_to(v, r.shape).astype(r.dtype)

        if strip is None or tr <= strip:
            run(0, tr)
        else:
            def step(s, carry):
                run(pl.multiple_of(s * strip, strip), strip)
                return carry
            lax.fori_loop(0, tr // strip, step, 0)

    def zmap(nd):
        return lambda i: (0,) * nd

    in_specs = [pl.BlockSpec(t[1], t[2]) for t in ins]
    out_specs = [pl.BlockSpec(t[2], t[3]) for t in outs]
    out_specs += [pl.BlockSpec(s, zmap(len(s))) for s, _ in accs]
    out_shape = [jax.ShapeDtypeStruct(t[0], t[1]) for t in outs]
    out_shape += [jax.ShapeDtypeStruct(s, d) for s, d in accs]
    res = pl.pallas_call(
        body, name=name, interpret=False, out_shape=out_shape, grid=(steps,),
        in_specs=in_specs, out_specs=out_specs,
        compiler_params=_params(("arbitrary",)),
    )(*[t[0] for t in ins])
    return res


def out2d(rows, width, dtype, tr):
    return ((rows, width), dtype, (tr, width), lambda i: (i, 0), "r2")


def conv_fwd(xs, w8, kw, *, rows, c, tc, tr, name, post, extras=(), outs=(), pre=None, strip=STRIP):
    nx, ne, no = len(xs), len(extras), len(outs)
    nr, nc = rows // tr, c // tc
    r8 = tr // 8
    st = strip

    def body(*refs):
        x_refs = refs[:2 * nx]
        w_ref = refs[2 * nx]
        e_refs = refs[2 * nx + 1:2 * nx + 1 + ne]
        o_refs = refs[2 * nx + 1 + ne:2 * nx + 1 + ne + no]
        scr = refs[-1]
        j, i = pl.program_id(0), pl.program_id(1)
        halo = [x_refs[2 * q + 1][...].astype(F32) for q in range(nx)]
        scr[0:8, :] = jnp.where(i > 0, pre(*halo) if pre else halo[0], 0.0)

        def fill(s, carry):
            r0 = pl.multiple_of(s * st, st)
            cur = [x_refs[2 * q][pl.ds(r0, st), :].astype(F32) for q in range(nx)]
            scr[pl.ds(8 + r0, st), :] = pre(*cur) if pre else cur[0]
            return carry

        def comp(s, carry):
            r0 = pl.multiple_of(s * st, st)
            win = scr[pl.ds(r0, st + 8), :]
            y = jnp.zeros((st, tc), F32)
            for q in range(kw):
                sh = kw - 1 - q
                y = y + w_ref[q:q + 1, :] * win[8 - sh:8 - sh + st]
            res = post(j, y, *[e[pl.ds(r0, st), :] for e in e_refs])
            if not isinstance(res, (tuple, list)):
                res = (res,)
            for r, v in zip(o_refs, res):
                r[pl.ds(r0, st), :] = v.astype(r.dtype)
            return carry

        lax.fori_loop(0, tr // st, fill, 0)
        lax.fori_loop(0, tr // st, comp, 0)

    in_specs, args = [], []
    for arr, cb0 in xs:
        in_specs.append(pl.BlockSpec((tr, tc), lambda j, i, cb0=cb0: (i, cb0 + j)))
        in_specs.append(pl.BlockSpec((8, tc), lambda j, i, cb0=cb0: (jnp.maximum(i * r8 - 1, 0), cb0 + j)))
        args += [arr, arr]
    in_specs.append(pl.BlockSpec((8, tc), lambda j, i: (0, j)))
    args.append(w8)
    for arr, cb0 in extras:
        in_specs.append(pl.BlockSpec((tr, tc), lambda j, i, cb0=cb0: (i, cb0 + j)))
        args.append(arr)
    return pl.pallas_call(
        body, name=name, interpret=False,
        out_shape=[jax.ShapeDtypeStruct((rows, c), dt) for dt in outs],
        grid=(nc, nr), in_specs=in_specs,
        out_specs=[pl.BlockSpec((tr, tc), lambda j, i: (i, j)) for _ in outs],
        scratch_shapes=[pltpu.VMEM((tr + 8, tc), F32)],
        compiler_params=_params(("parallel", "arbitrary")),
    )(*args)


def conv_bwd(xs, w8, kw, dy, *, rows, c, tc, tr, name, post, extras=(), outs=(), pre=None):
    nx, ne, no = len(xs), len(extras), len(outs)
    nr, nc = rows // tr, c // tc
    r8 = tr // 8

    def body(*refs):
        x_refs = refs[:nx]
        w_ref, dy_ref, dyn_ref = refs[nx:nx + 3]
        e_refs = refs[nx + 3:nx + 3 + ne]
        first_out = nx + 3 + ne
        o_refs = refs[first_out:first_out + no]
        dw_ref = refs[first_out + no]
        gscr = refs[-1]
        i = pl.program_id(1)
        gscr[tr:tr + 8, :] = jnp.where(i < nr - 1, dyn_ref[...].astype(F32), 0.0)

        def fill(s, carry):
            r0 = pl.multiple_of(s * STRIP, STRIP)
            gscr[pl.ds(r0, STRIP), :] = dy_ref[pl.ds(r0, STRIP), :].astype(F32)
            return carry

        def comp(s, dws):
            r0 = pl.multiple_of(s * STRIP, STRIP)
            gwin = gscr[pl.ds(r0, STRIP + 8), :]
            cur = [x_refs[q][pl.ds(r0, STRIP), :].astype(F32) for q in range(nx)]
            x = pre(*cur) if pre else cur[0]
            dx = jnp.zeros((STRIP, tc), F32)
            new = []
            for q in range(kw):
                sh = kw - 1 - q
                ahead = gwin[sh:sh + STRIP]
                dx = dx + w_ref[q:q + 1, :] * ahead
                part = ahead * x
                new.append(dws[q] + part[0:8] + part[8:16])
            res = post(dx, *[e[pl.ds(r0, STRIP), :] for e in e_refs])
            if not isinstance(res, (tuple, list)):
                res = (res,)
            for r, v in zip(o_refs, res):
                r[pl.ds(r0, STRIP), :] = v.astype(r.dtype)
            return tuple(new)

        lax.fori_loop(0, tr // STRIP, fill, 0)
        dws = lax.fori_loop(0, tr // STRIP, comp, tuple(jnp.zeros((8, tc), F32) for _ in range(kw)))

        @pl.when(i == 0)
        def _():
            dw_ref[...] = jnp.zeros((8, tc), F32)

        dw_ref[...] += jnp.concatenate([jnp.sum(t, axis=0, keepdims=True) for t in dws]
                                       + [jnp.zeros((8 - kw, tc), F32)], axis=0)

    in_specs, args = [], []
    for arr, cb0 in xs:
        in_specs.append(pl.BlockSpec((tr, tc), lambda j, i, cb0=cb0: (i, cb0 + j)))
        args.append(arr)
    in_specs.append(pl.BlockSpec((8, tc), lambda j, i: (0, j)))
    in_specs.append(pl.BlockSpec((tr, tc), lambda j, i: (i, j)))
    in_specs.append(pl.BlockSpec((8, tc), lambda j, i: (jnp.minimum((i + 1) * r8, nr * r8 - 1), j)))
    args += [w8, dy, dy]
    for arr, cb0 in extras:
        in_specs.append(pl.BlockSpec((tr, tc), lambda j, i, cb0=cb0: (i, cb0 + j)))
        args.append(arr)
    return pl.pallas_call(
        body, name=name, interpret=False,
        out_shape=[jax.ShapeDtypeStruct((rows, c), dt) for dt in outs] + [jax.ShapeDtypeStruct((8, c), F32)],
        grid=(nc, nr), in_specs=in_specs,
        out_specs=[pl.BlockSpec((tr, tc), lambda j, i: (i, j)) for _ in outs] + [pl.BlockSpec((8, tc), lambda j, i: (0, j))],
        scratch_shapes=[pltpu.VMEM((tr + 8, tc), F32)],
        compiler_params=_params(("parallel", "arbitrary")),
    )(*args)


def rms_fwd(h, w, *, name):
    rows = h.shape[0]
    tr = _pick(rows, (384, 128))

    def fn(i, x, wv):
        r = lax.rsqrt(jnp.mean(x * x, axis=1, keepdims=True) + EPS)
        return x * r * wv

    return rowwise(fn, [cols(h, tr), whole(w)], [out2d(rows, D, BF16, tr)], steps=rows // tr, name=name)[0]


def _rms_bwd_epi(row0, g, x, dr, wv):
    r = lax.rsqrt(jnp.mean(x * x, axis=1, keepdims=True) + EPS)
    xh = x * r
    gw = g * wv
    dx = r * (gw - xh * jnp.mean(gw * xh, axis=1, keepdims=True))
    row = row0 + lax.broadcasted_iota(jnp.int32, (x.shape[0], 1), 0)
    return jnp.where(row >= PAD, dr + dx, 0.0), jnp.sum(g * xh, axis=0, keepdims=True)


def dx_rms_bwd(dy, w, h, nw, dres, *, name, b_chip=False, swap_mid=False):
    return mm(dy, w, tb=True, b_chip=b_chip, swap_mid=swap_mid, tn=D, name=name, epi=_rms_bwd_epi,
              epi_ins=[(h, lambda j: 0), (dres, lambda j: 0)], epi_consts=[nw], epi_outs=[F32],
              epi_accs=[((1, D), F32)])


def _add_loss_epi(row0, t, h, tgt):
    row = row0 + lax.broadcasted_iota(jnp.int32, (t.shape[0], 1), 0)
    tgt = jnp.where(row0 == 0, jnp.concatenate([tgt[-HEAD0:], tgt[:-HEAD0]], axis=0), tgt)
    diff = jnp.where(row >= HEAD0, t + h - tgt, 0.0)
    part = jnp.sum(jnp.sum(diff * diff, axis=1, keepdims=True), axis=0, keepdims=True)
    return diff * (1.0 / D), part * (0.5 / D)


def add_loss(a, w, h, target, *, name):
    return mm(a, w, name=name, epi=_add_loss_epi, epi_ins=[(h, lambda j: 0), (target, lambda j: 0)],
              epi_outs=[F32], epi_accs=[((1, 128), F32)])


def adamw(w, g, m, v, *, name):
    shape = w.shape
    gs = list(g) if isinstance(g, (list, tuple)) else [g]
    nl = len(gs)
    width = shape[-1]
    rows = w.size // width
    rl = rows // nl
    tr = _pick(rl, (256, 176, 128, 64, 16, 8))
    nr = rl // tr
    if w.ndim == 3 and shape[1] % tr == 0:
        per = shape[1] // tr
        view = lambda t: (t, (None, tr, width), lambda i: (i // per, i % per, 0), "r2")
        out = (shape, F32, (None, tr, width), lambda i: (i // per, i % per, 0), "r2")
    else:
        view = lambda t: cols(t.reshape(rows, width), tr)
        out = out2d(rows, width, F32, tr)

    def fn(i, wv, mv, vv, *gvs):
        gv = gvs[0]
        for layer in range(1, nl):
            gv = jnp.where(i >= layer * rl, gvs[layer], gv)
        mn = B1 * mv + (1.0 - B1) * gv
        vn = B2 * vv + (1.0 - B2) * gv * gv
        mh = mn / (1.0 - B1 ** STEP)
        vh = vn / (1.0 - B2 ** STEP)
        return -LR * (mh / (jnp.sqrt(vh) + AEPS) + WD * wv), mn, vn, gv

    g_ins = [(t.reshape(rl, width), (tr, width), lambda i, layer=layer: (jnp.clip(i - layer * nr, 0, nr - 1), 0), "r2")
             for layer, t in enumerate(gs)]
    res = rowwise(fn, [view(t) for t in (w, m, v)] + g_ins, [out] * 4, steps=rows // tr, name=name)
    return [r.reshape(shape) for r in res]


HB = DN_H * CH
PAIR = 3


def _split(a):
    hi = a.astype(BF16)
    return hi, (a - hi.astype(F32)).astype(BF16)


def _dot1(a, b, ca=1, cb=0):
    return _dot(a.astype(BF16), b.astype(BF16), ca, cb)


def _dot3(a, b, ca=1, cb=0):
    ah, al = _split(a)
    bh, bl = _split(b)
    return _dot(ah, bh, ca, cb) + (_dot(ah, bl, ca, cb) + _dot(al, bh, ca, cb))


def _dot01(m01, b, ca=1, cb=0):
    bh, bl = _split(b)
    m = m01.astype(BF16)
    return _dot(m, bh, ca, cb) + _dot(m, bl, ca, cb)


def _stack(x):
    return jnp.concatenate([x[:, h * DN_D:(h + 1) * DN_D] for h in range(DN_H)], axis=0)


def _unstack(x):
    return jnp.concatenate([x[h * CH:(h + 1) * CH] for h in range(DN_H)], axis=1)


def _tri_inv(mats, blk, eye):
    each = lambda f, *lists: [f(*t) for t in zip(*lists)]
    ad = [jnp.where(blk, a, 0.0) for a in mats]
    lo = each(lambda a, d: a - d, mats, ad)
    a2 = each(_dot3, ad, ad)
    a4 = each(_dot3, a2, a2)
    a8 = each(_dot3, a4, a4)
    dgi = each(lambda d, s: _dot3(eye - d, eye + s), ad, a2)
    dgi = each(lambda p, s: _dot3(p, eye + s), dgi, a4)
    dgi = each(lambda p, s: _dot3(p, eye + s), dgi, a8)
    n = each(_dot3, dgi, lo)
    n2 = each(_dot3, n, n)
    return each(_dot3, each(lambda u, v: _dot3(eye - u, eye + v), n, n2), dgi)


def _dn_masks():
    row = lax.broadcasted_iota(jnp.int32, (HB, HB), 0)
    col = lax.broadcasted_iota(jnp.int32, (HB, HB), 1)
    same = (row // CH) == (col // CH)
    incl = jnp.logical_and(same, row >= col)
    strict = jnp.logical_and(same, row > col)
    upper = jnp.logical_and(same, row <= col)
    blk = (row // 16) == (col // 16)
    eye = (row == col).astype(F32)
    return incl, strict, upper, blk, eye


def _dn_chunk(qv, kv, vv, bc, br, incl, strict):
    r64 = lax.broadcasted_iota(jnp.int32, (CH, CH), 0)
    c64 = lax.broadcasted_iota(jnp.int32, (CH, CH), 1)
    dcol = _dot01((r64 >= c64).astype(F32), bc)
    drow = _dot3(br, (r64 <= c64).astype(F32))
    col = lambda m, l0: jnp.concatenate([m[:, l0 + h:l0 + h + 1] for h in range(DN_H)], axis=0)
    b_c = col(bc, 0)
    d_c = col(dcol, 4)
    d_r = jnp.concatenate([drow[4 + h:5 + h, :] for h in range(DN_H)], axis=1)
    d_last_h = [dcol[CH - 1:CH, 4 + h:5 + h] for h in range(DN_H)]
    d_last = jnp.concatenate([jnp.broadcast_to(t, (CH, 1)) for t in d_last_h], axis=0)
    q, k, v = _stack(qv), _stack(kv), _stack(vv)
    dm = jnp.where(incl, jnp.exp(jnp.where(incl, d_c - d_r, 0.0)), 0.0)
    kk = _dot1(k, k, 1, 1)
    a = jnp.where(strict, b_c * kk * dm, 0.0)
    ed = jnp.exp(d_c)
    rhs = jnp.concatenate([v * b_c, k * (b_c * ed)], axis=1)
    qk = _dot1(q, k, 1, 1) * dm
    ekd = jnp.exp(d_last - d_c)
    gl = [jnp.exp(t) for t in d_last_h]
    return q, k, v, b_c, dm, kk, a, ed, rhs, qk, ekd, gl


def dn_fwd(qkv_n, bgcol, bgrow):
    rows = qkv_n.shape[0]
    nch = rows // CH

    def body(q_ref, k_ref, v_ref, bc_ref, br_ref, o_ref, s_out, ti_out, s_scr, prep, prep_qk, prep_gl):
        n = pl.program_id(0)

        @pl.when(n == 0)
        def _():
            s_scr[...] = jnp.zeros(s_scr.shape, F32)
            prep[...] = jnp.zeros(prep.shape, F32)
            prep_qk[...] = jnp.zeros(prep_qk.shape, F32)
            prep_gl[...] = jnp.zeros(prep_gl.shape, F32)

        live = n > 0
        rows_of = [slice(h * CH, (h + 1) * CH) for h in range(DN_H)]
        s = [s_scr[h] for h in range(DN_H)]
        for c in range(PAIR):
            u, w, qd, kd = prep[c, 0], prep[c, 1], prep[c, 2], prep[c, 3]
            for h in range(DN_H):
                s_out[c, h] = s[h]
            v_new = [u[rs] - _dot1(w[rs], s[h]) for h, rs in enumerate(rows_of)]
            o_state = [_dot1(qd[rs], s[h]) for h, rs in enumerate(rows_of)]
            s = [jnp.where(live, prep_gl[c, h:h + 1, 0:1] * s[h] + _dot1(kd[rs], v_new[h], 0, 0), s[h])
                 for h, rs in enumerate(rows_of)]
            o = jnp.concatenate(o_state, axis=0) + _dot1(prep_qk[c], jnp.concatenate(v_new, axis=0))
            o_ref[c * CH:(c + 1) * CH, :] = _unstack(o)
        for h in range(DN_H):
            s_scr[h] = s[h]

        incl, strict, _, blk, eye = _dn_masks()
        parts = []
        for c in range(PAIR):
            rows_c = slice(c * CH, (c + 1) * CH)
            parts.append(_dn_chunk(q_ref[rows_c, :], k_ref[rows_c, :], v_ref[rows_c, :], bc_ref[rows_c, :],
                                   br_ref[c], incl, strict))
        tinvs = _tri_inv([p[6] for p in parts], blk, eye)
        for c, (q, k, v, b_c, dm, kk, a, ed, rhs, qk_n, ekd, gl) in enumerate(parts):
            tinv = tinvs[c]
            ti_out[c] = tinv
            sol = _dot3(tinv, rhs)
            prep[c, 0] = sol[:, :DN_D]
            prep[c, 1] = sol[:, DN_D:]
            prep[c, 2] = q * ed
            prep[c, 3] = k * ekd
            prep_qk[c] = qk_n
            prep_gl[c] = jnp.concatenate([jnp.broadcast_to(t, (1, 128)) for t in gl]
                                         + [jnp.zeros((8 - DN_H, 128), F32)], axis=0)

    assert nch % PAIR == 0
    npair = nch // PAIR
    last = npair - 1
    return pl.pallas_call(
        body, name="dn_fwd", interpret=False,
        out_shape=[jax.ShapeDtypeStruct((rows, DN_DIM), F32),
                   jax.ShapeDtypeStruct((nch, DN_H, DN_D, DN_D), F32),
                   jax.ShapeDtypeStruct((nch, HB, HB), F32)],
        grid=(npair + 1,),
        in_specs=[pl.BlockSpec((PAIR * CH, DN_DIM), lambda n: (jnp.minimum(n, last), 0)),
                  pl.BlockSpec((PAIR * CH, DN_DIM), lambda n: (jnp.minimum(n, last), 1)),
                  pl.BlockSpec((PAIR * CH, DN_DIM), lambda n: (jnp.minimum(n, last), 2)),
                  pl.BlockSpec((PAIR * CH, 128), lambda n: (jnp.minimum(n, last), 0)),
                  pl.BlockSpec((PAIR, 8, CH), lambda n: (jnp.minimum(n, last), 0, 0))],
        out_specs=[pl.BlockSpec((PAIR * CH, DN_DIM), lambda n: (jnp.maximum(n - 1, 0), 0)),
                   pl.BlockSpec((PAIR, DN_H, DN_D, DN_D), lambda n: (jnp.maximum(n - 1, 0), 0, 0, 0)),
                   pl.BlockSpec((PAIR, HB, HB), lambda n: (jnp.minimum(n, last), 0, 0))],
        scratch_shapes=[pltpu.VMEM((DN_H, DN_D, DN_D), F32), pltpu.VMEM((PAIR, 4, HB, DN_D), F32),
                        pltpu.VMEM((PAIR, HB, HB), F32), pltpu.VMEM((PAIR, 8, 128), F32)],
        compiler_params=_params(("arbitrary",)),
    )(qkv_n, qkv_n, qkv_n, bgcol, bgrow)


def dn_bwd(qkv_n, bgcol, bgrow, s_all, ti_all, do):
    rows = qkv_n.shape[0]
    nch = rows // CH

    def body(q_ref, k_ref, v_ref, bc_ref, br_ref, s_ref, ti_ref, do_ref, dq_ref, dk_ref, dv_ref, dbg_ref, ds_scr):
        n = pl.program_id(0)

        @pl.when(n == 0)
        def _():
            ds_scr[...] = jnp.zeros(ds_scr.shape, F32)

        incl, strict, upper, _, _ = _dn_masks()
        rsum = lambda t: jnp.sum(t, axis=1, keepdims=True)
        rows_of = [slice(h * CH, (h + 1) * CH) for h in range(DN_H)]
        heads_of = lambda f: jnp.concatenate([f(h, rs) for h, rs in enumerate(rows_of)], axis=0)
        cs = []
        for c in reversed(range(PAIR)):
            rc = slice(c * CH, (c + 1) * CH)
            q, k, v, b_c, dm, kk, a, ed, rhs, qk, ekd, gl = _dn_chunk(
                q_ref[rc, :], k_ref[rc, :], v_ref[rc, :], bc_ref[rc, :], br_ref[c], incl, strict)
            cs.append(dict(rc=rc, q=q, k=k, v=v, b_c=b_c, dm=dm, kk=kk, a=a, ed=ed, rhs=rhs, qk=qk, ekd=ekd, gl=gl,
                           tinv=ti_ref[c], g=_stack(do_ref[rc, :]), s=[s_ref[c, h] for h in range(DN_H)]))
        for t in cs:
            t["sol"] = _dot3(t["tinv"], t["rhs"])
        for t in cs:
            t["u"], t["w"] = t["sol"][:, :DN_D], t["sol"][:, DN_D:]
            t["qd"], t["kd"] = t["q"] * t["ed"], t["k"] * t["ekd"]
            t["v_new"] = heads_of(lambda h, rs: t["u"][rs] - _dot1(t["w"][rs], t["s"][h]))
            t["dv0"] = _dot1(t["qk"], t["g"], 0, 0)
            t["ds0"] = [_dot1(t["qd"][rs], t["g"][rs], 0, 0) for rs in rows_of]
            t["dqd"] = heads_of(lambda h, rs: _dot1(t["g"][rs], t["s"][h], 1, 1))
        for t in cs:
            t["dqk"] = _dot1(t["g"], t["v_new"], 1, 1)
        ds = [ds_scr[h] for h in range(DN_H)]
        for t in cs:
            t["ds"] = ds
            t["dv_new"] = t["dv0"] + heads_of(lambda h, rs: _dot1(t["kd"][rs], ds[h]))
            ds = [t["ds0"][h] + t["gl"][h] * ds[h] - _dot1(t["w"][rs], t["dv_new"][rs], 0, 0)
                  for h, rs in enumerate(rows_of)]
        for h in range(DN_H):
            ds_scr[h] = ds[h]
        for t in cs:
            t["dkd"] = heads_of(lambda h, rs: _dot1(t["v_new"][rs], t["ds"][h], 1, 1))
            dw = heads_of(lambda h, rs: -_dot1(t["dv_new"][rs], t["s"][h], 1, 1))
            t["dsol"] = jnp.concatenate([t["dv_new"], dw], axis=1)
        for t in cs:
            t["drhs"] = _dot3(t["tinv"], t["dsol"], 0, 0)
        for t in cs:
            t["da"] = jnp.where(strict, -_dot1(t["drhs"], t["sol"], 1, 1), 0.0)
        rowi = lax.broadcasted_iota(jnp.int32, (CH, 1), 0)
        lane = lax.broadcasted_iota(jnp.int32, (CH, 128), 1)
        for t in cs:
            q, k, v, b_c, dm, ed, da, dqk = t["q"], t["k"], t["v"], t["b_c"], t["dm"], t["ed"], t["da"], t["dqk"]
            drhs_u, drhs_w = t["drhs"][:, :DN_D], t["drhs"][:, DN_D:]
            s2 = rsum(drhs_w * k)
            dbeta = rsum(drhs_u * v) + s2 * ed + rsum(da * t["kk"] * dm)
            dkk = da * b_c * dm
            dqkr = dqk * dm
            mmat = da * t["a"] + dqk * t["qk"]
            tmp = rsum(t["dkd"] * t["kd"])
            dd = (s2 * b_c * ed + rsum(mmat) - _dot3(mmat, jnp.ones((HB, 128), F32), 0, 0)[:, :1]
                  + rsum(t["dqd"] * t["qd"]) - tmp)
            last = []
            for h, rs in enumerate(rows_of):
                dgl = jnp.sum(rsum(t["s"][h] * t["ds"][h]), axis=0, keepdims=True)
                dd_last = jnp.sum(tmp[rs], axis=0, keepdims=True) + dgl * t["gl"][h]
                last.append(jnp.where(rowi == CH - 1, dd_last, 0.0))
            dd = dd + jnp.concatenate(last, axis=0)
            rc = t["rc"]
            dq_ref[rc, :] = _unstack(_dot1(dqkr, k) + t["dqd"] * ed)
            dk_ref[rc, :] = _unstack(drhs_w * (b_c * ed) + _dot1(dkk, k) + _dot1(dkk, k, 0, 0) + _dot1(dqkr, q, 0, 0)
                                     + t["dkd"] * t["ekd"])
            dv_ref[rc, :] = _unstack(drhs_u * b_c)
            dg = _dot01(upper.astype(F32), jnp.broadcast_to(dd, (HB, 128)))[:, :1]
            out = jnp.zeros((CH, 128), F32)
            for h, rs in enumerate(rows_of):
                out = out + jnp.where(lane == h, dbeta[rs], 0.0) + jnp.where(lane == 4 + h, dg[rs], 0.0)
            dbg_ref[rc, :] = out

    assert nch % PAIR == 0
    npair = nch // PAIR
    rev = lambda n: npair - 1 - n
    blk = PAIR * CH
    return pl.pallas_call(
        body, name="dn_bwd", interpret=False,
        out_shape=[jax.ShapeDtypeStruct((rows, DN_DIM), F32)] * 3 + [jax.ShapeDtypeStruct((rows, 128), F32)],
        grid=(npair,),
        in_specs=[pl.BlockSpec((blk, DN_DIM), lambda n: (rev(n), 0)),
                  pl.BlockSpec((blk, DN_DIM), lambda n: (rev(n), 1)),
                  pl.BlockSpec((blk, DN_DIM), lambda n: (rev(n), 2)),
                  pl.BlockSpec((blk, 128), lambda n: (rev(n), 0)),
                  pl.BlockSpec((PAIR, 8, CH), lambda n: (rev(n), 0, 0)),
                  pl.BlockSpec((PAIR, DN_H, DN_D, DN_D), lambda n: (rev(n), 0, 0, 0)),
                  pl.BlockSpec((PAIR, HB, HB), lambda n: (rev(n), 0, 0)),
                  pl.BlockSpec((blk, DN_DIM), lambda n: (rev(n), 0))],
        out_specs=[pl.BlockSpec((blk, DN_DIM), lambda n: (rev(n), 0))] * 3 + [pl.BlockSpec((blk, 128), lambda n: (rev(n), 0))],
        scratch_shapes=[pltpu.VMEM((DN_H, DN_D, DN_D), F32)],
        compiler_params=_params(("arbitrary",)),
    )(qkv_n, qkv_n, qkv_n, bgcol, bgrow, s_all, ti_all, do)


def _swa_valid(n):
    c3 = lax.broadcasted_iota(jnp.int32, (NKEY, 4 * BLK), 0)
    r = lax.broadcasted_iota(jnp.int32, (NKEY, 4 * BLK), 1) % BLK
    prev0 = N_META + BLK
    c = jnp.where(c3 < N_META, PAD + c3, jnp.where(c3 < prev0, c3 - N_META, c3 - prev0))
    lo = jnp.where(c3 < N_META, 0, jnp.where(c3 < prev0, r + 1 + jnp.where(n >= 2, 0, BLK), 0))
    hi = jnp.where(c3 < N_META, r + jnp.where(n >= 1, BLK, 0),
                   jnp.where(c3 < prev0, BLK, r - jnp.where(n >= 1, 0, BLK)))
    return jnp.logical_and(c >= lo, c <= hi)


def _swa_probs(qs, kcats, valid, sinks):
    s = [jnp.where(valid, _dot(kc, q, 1, 1), -1e30) for q, kc in zip(qs, kcats)]
    m = [jnp.maximum(jnp.max(t, axis=0, keepdims=True), sk) for t, sk in zip(s, sinks)]
    e = [jnp.where(valid, jnp.exp(t - mx), 0.0) for t, mx in zip(s, m)]
    es = [jnp.exp(sk - mx) for sk, mx in zip(sinks, m)]
    inv = [1.0 / (jnp.sum(t, axis=0, keepdims=True) + u) for t, u in zip(e, es)]
    return [t * i for t, i in zip(e, inv)], [u * i for u, i in zip(es, inv)]


def _swa_group(q_ref, sk_ref, h):
    q4 = jnp.concatenate([q_ref[4 * h + g] for g in range(4)], axis=0)
    sink4 = jnp.concatenate([jnp.full((1, BLK), sk_ref[4 * h + g], F32) for g in range(4)], axis=1)
    return q4, sink4


def _swa_specs():
    q = pl.BlockSpec((SWA_H, BLK, SWA_D), lambda n: (0, n, 0))
    km = pl.BlockSpec((SWA_KV, N_META, SWA_D), lambda n: (0, PAD // N_META, 0))
    kp = pl.BlockSpec((SWA_KV, BLK, SWA_D), lambda n: (0, jnp.maximum(n - 1, 0), 0))
    kc = pl.BlockSpec((SWA_KV, BLK, SWA_D), lambda n: (0, n, 0))
    return [q, km, kp, kc, km, kp, kc]


def swa_fwd(qh, kh, vh, sinks):
    rows = qh.shape[1]
    nb = rows // BLK

    def body(q_ref, km, kp, kc, vm, vp, vc, sk_ref, o_ref):
        n = pl.program_id(0)
        valid = _swa_valid(n)
        kcats = [jnp.concatenate([km[h], kp[h], kc[h]], axis=0) for h in range(SWA_KV)]
        vcats = [jnp.concatenate([vm[h], vp[h], vc[h]], axis=0) for h in range(SWA_KV)]
        qs, sinks4 = zip(*[_swa_group(q_ref, sk_ref, h) for h in range(SWA_KV)])
        ps, _ = _swa_probs(qs, kcats, valid, sinks4)
        o4s = [_dot(p.astype(BF16), vc_, 0, 0) for p, vc_ in zip(ps, vcats)]
        o_ref[...] = jnp.concatenate([o4[g * BLK:(g + 1) * BLK] for o4 in o4s for g in range(4)],
                                     axis=1).astype(BF16)

    return pl.pallas_call(
        body, name="swa_fwd", interpret=False,
        out_shape=jax.ShapeDtypeStruct((rows, SWA_H * SWA_D), BF16),
        grid=(nb,),
        in_specs=_swa_specs() + [pl.BlockSpec(memory_space=pltpu.SMEM)],
        out_specs=pl.BlockSpec((BLK, SWA_H * SWA_D), lambda n: (n, 0)),
        compiler_params=_params(("parallel",)),
    )(qh, kh, kh, kh, vh, vh, vh, sinks)


def swa_bwd(qh, kh, vh, sinks, do):
    rows = qh.shape[1]
    nb = rows // BLK

    def body(q_ref, km, kp, kc, vm, vp, vc, do_ref, sk_ref, dq_ref, dk_ref, dv_ref, dsk_ref):
        n = pl.program_id(0)

        @pl.when(n == 0)
        def _():
            dk_ref[...] = jnp.zeros(dk_ref.shape, F32)
            dv_ref[...] = jnp.zeros(dv_ref.shape, F32)

        valid = _swa_valid(n)
        g_all = do_ref[...]
        rowi = lax.broadcasted_iota(jnp.int32, (SWA_H, 128), 0)
        dsk = jnp.zeros((SWA_H, 128), F32)
        pm = pl.multiple_of(jnp.maximum(n - 1, 0) * BLK, BLK)
        pc = pl.multiple_of(n * BLK, BLK)
        hs = range(SWA_KV)
        kcats = [jnp.concatenate([km[h], kp[h], kc[h]], axis=0) for h in hs]
        vcats = [jnp.concatenate([vm[h], vp[h], vc[h]], axis=0) for h in hs]
        qs, sinks4 = zip(*[_swa_group(q_ref, sk_ref, h) for h in hs])
        g4s = [jnp.concatenate([g_all[:, (4 * h + g) * SWA_D:(4 * h + g + 1) * SWA_D] for g in range(4)], axis=0)
               for h in hs]
        ps, pss = _swa_probs(qs, kcats, valid, sinks4)
        dps = [_dot(vc_, g4, 1, 1) for vc_, g4 in zip(vcats, g4s)]
        deltas = [jnp.sum(p * dp, axis=0, keepdims=True) for p, dp in zip(ps, dps)]
        dss = [(p * (dp - dl)).astype(BF16) for p, dp, dl in zip(ps, dps, deltas)]
        dq4s = [_dot(ds, kc_, 0, 0) for ds, kc_ in zip(dss, kcats)]
        dkcs = [_dot(ds, q4) for ds, q4 in zip(dss, qs)]
        dvcs = [_dot(p.astype(BF16), g4) for p, g4 in zip(ps, g4s)]
        for h in hs:
            t = pss[h] * deltas[h]
            for g in range(4):
                dq_ref[4 * h + g] = dq4s[h][g * BLK:(g + 1) * BLK]
                part = -jnp.sum(t[:, g * BLK:(g + 1) * BLK], axis=1, keepdims=True)
                dsk = dsk + jnp.where(rowi == 4 * h + g, part, 0.0)
            lanes = slice(h * SWA_D, (h + 1) * SWA_D)
            for ref, val in ((dk_ref, dkcs[h]), (dv_ref, dvcs[h])):
                ref[PAD:BLK, lanes] += val[0:N_META]
                ref[pl.ds(pm, BLK), lanes] += val[N_META:N_META + BLK]
                ref[pl.ds(pc, BLK), lanes] += val[N_META + BLK:]
        dsk_ref[0] = dsk

    return pl.pallas_call(
        body, name="swa_bwd", interpret=False,
        out_shape=[jax.ShapeDtypeStruct((SWA_H, rows, SWA_D), F32),
                   jax.ShapeDtypeStruct((rows, SWA_KV * SWA_D), F32),
                   jax.ShapeDtypeStruct((rows, SWA_KV * SWA_D), F32),
                   jax.ShapeDtypeStruct((nb, SWA_H, 128), F32)],
        grid=(nb,),
        in_specs=_swa_specs() + [pl.BlockSpec((BLK, SWA_H * SWA_D), lambda n: (n, 0)),
                                 pl.BlockSpec(memory_space=pltpu.SMEM)],
        out_specs=[pl.BlockSpec((SWA_H, BLK, SWA_D), lambda n: (0, n, 0)),
                   pl.BlockSpec((rows, SWA_KV * SWA_D), lambda n: (0, 0)),
                   pl.BlockSpec((rows, SWA_KV * SWA_D), lambda n: (0, 0)),
                   pl.BlockSpec((1, SWA_H, 128), lambda n: (n, 0, 0))],
        compiler_params=_params(("arbitrary",)),
    )(qh, kh, kh, kh, vh, vh, vh, do, sinks)


QK_W = (SWA_H + SWA_KV) * SWA_D


def _head_mean(t):
    r = lax.broadcasted_iota(jnp.int32, (128, 128), 0) // SWA_D
    c = lax.broadcasted_iota(jnp.int32, (128, 128), 1) // SWA_D
    blk = jnp.where(r == c, 1.0 / SWA_D, 0.0).astype(BF16)
    out = []
    for i in range(t.shape[1] // 128):
        hi, lo = _split(t[:, 128 * i:128 * (i + 1)])
        out.append(_dot(hi, blk) + _dot(lo, blk))
    return jnp.concatenate(out, axis=1)


def _qk_scales(qw, kw):
    scale = SWA_D ** -0.5
    wt = jnp.concatenate([jnp.tile(qw.astype(F32) * scale, (1, SWA_H)), jnp.tile(kw.astype(F32), (1, SWA_KV))], axis=1)
    st = jnp.concatenate([jnp.full((1, SWA_H * SWA_D), scale, F32), jnp.ones((1, SWA_KV * SWA_D), F32)], axis=1)
    return wt, st


def qknorm_fwd(qkv, qw, kw):
    rows = qkv.shape[0]
    tr = _pick(rows, (384, 128))
    wt, _ = _qk_scales(qw, kw)

    def fn(i, x, w):
        xq = x[:, :QK_W]
        y = xq * lax.rsqrt(_head_mean(xq * xq) + EPS) * w
        head = lambda t, j: t[:, j * SWA_D:(j + 1) * SWA_D][None]
        qo = jnp.concatenate([head(y, j) for j in range(SWA_H)], axis=0)
        ko = jnp.concatenate([head(y, SWA_H + j) for j in range(SWA_KV)], axis=0)
        vo = jnp.concatenate([head(x, SWA_H + SWA_KV + j) for j in range(SWA_KV)], axis=0)
        return qo, ko, vo

    hm = lambda nh: ((nh, rows, SWA_D), BF16, (nh, tr, SWA_D), lambda i: (0, i, 0), "r3")
    return rowwise(fn, [cols(qkv, tr), whole(wt)], [hm(SWA_H), hm(SWA_KV), hm(SWA_KV)],
                   steps=rows // tr, name="qknorm_fwd")


def qknorm_bwd(qkv, qw, kw, dqh, dk, dv):
    rows = qkv.shape[0]
    tr = _pick(rows, (384, 128))
    wt, st = _qk_scales(qw, kw)

    def fn(i, x, w, sc, dq, dkv, dvv):
        xq = x[:, :QK_W]
        dy = jnp.concatenate([dq[j] for j in range(SWA_H)] + [dkv], axis=1)
        r = lax.rsqrt(_head_mean(xq * xq) + EPS)
        xh = xq * r
        gw = dy * w
        dx = r * (gw - xh * _head_mean(gw * xh))
        return jnp.concatenate([dx, dvv], axis=1), jnp.sum(dy * sc * xh, axis=0, keepdims=True)

    dqkv, dw = rowwise(fn, [cols(qkv, tr), whole(wt), whole(st), heads(dqh, tr), cols(dk, tr), cols(dv, tr)],
                       [out2d(rows, 1536, BF16, tr)], steps=rows // tr, name="qknorm_bwd", accs=[((1, QK_W), F32)])
    dw = dw.reshape(SWA_H + SWA_KV, SWA_D)
    return dqkv, jnp.sum(dw[:SWA_H], axis=0, keepdims=True), jnp.sum(dw[SWA_H:], axis=0, keepdims=True)


def _place():
    return lax.axis_index("x"), lax.axis_index("y"), lax.axis_index("c")


ANY = pl.BlockSpec(memory_space=pl.ANY)


def _rcopy(ssem, rsem, k, src, dst, to):
    return pltpu.make_async_remote_copy(src_ref=src, dst_ref=dst, send_sem=ssem.at[k], recv_sem=rsem.at[k],
                                        device_id=to, device_id_type=MESH)


def gather_weights(shards, small):
    n = len(shards)
    halves = [t.shape[0] // 2 for t in shards]

    def body(*refs):
        s_refs, small_ref = refs[:n], refs[n]
        o_refs, osmall = refs[n + 1:2 * n + 1], refs[2 * n + 1]
        ssem, rsem, lsem = refs[2 * n + 2:]
        x, y, c = _place()
        me = 2 * x + y
        chips = [(1 - x, y), (x, 1 - y), (1 - x, 1 - y)]

        def half(k, s, hh):
            return o_refs[k].at[s, pl.ds(hh * halves[k], halves[k]), :]

        loc = pltpu.make_async_copy(small_ref, osmall.at[me], lsem)
        loc.start()
        sends = []
        for k in range(n):
            for j, (px, py) in enumerate(chips):
                sends.append(_rcopy(ssem, rsem, 6 * k + j, s_refs[k].at[pl.ds(c * halves[k], halves[k]), :],
                                    half(k, me, c), (px, py, c)))
        for j, (px, py) in enumerate(chips):
            sends.append(_rcopy(ssem, rsem, 6 * n + j, small_ref, osmall.at[me], (px, py, c)))
        for cp in sends:
            cp.start()
        for k in range(n):
            for j, (px, py) in enumerate(chips):
                s = 2 * px + py
                _rcopy(ssem, rsem, 6 * k + j, half(k, s, c), half(k, s, c), (x, y, c)).wait_recv()
                fwd = _rcopy(ssem, rsem, 6 * k + 3 + j, half(k, s, c), half(k, s, c), (x, y, 1 - c))
                fwd.start()
                sends.append(fwd)
        for k in range(n):
            for j, (px, py) in enumerate(chips):
                s = 2 * px + py
                _rcopy(ssem, rsem, 6 * k + 3 + j, half(k, s, 1 - c), half(k, s, 1 - c), (x, y, c)).wait_recv()
        for j, (px, py) in enumerate(chips):
            s = 2 * px + py
            _rcopy(ssem, rsem, 6 * n + j, osmall.at[s], osmall.at[s], (x, y, c)).wait_recv()
        for cp in sends:
            cp.wait_send()
        loc.wait()

    res = pl.pallas_call(
        body, name="gather_weights", interpret=False,
        out_shape=[jax.ShapeDtypeStruct((4,) + t.shape, t.dtype) for t in shards]
        + [jax.ShapeDtypeStruct((4, SW_ROWS, 1024), F32)],
        in_specs=[ANY] * (n + 1), out_specs=[ANY] * (n + 1),
        scratch_shapes=[pltpu.SemaphoreType.DMA((6 * n + 3,)), pltpu.SemaphoreType.DMA((6 * n + 3,)),
                        pltpu.SemaphoreType.DMA],
    )(*shards, small)
    return res[:n], res[n]


def _handshake(peers):
    barrier = pltpu.get_barrier_semaphore()
    for peer in peers:
        pl.semaphore_signal(barrier, inc=1, device_id=peer, device_id_type=MESH)
    pl.semaphore_wait(barrier, len(peers))


def gather_weights_beside(shards, cid, name):
    n = len(shards)
    halves = [t.shape[0] // 2 for t in shards]

    def body(*refs):
        s_refs, o_refs, ssem, rsem = refs[:n], refs[n:2 * n], refs[2 * n], refs[2 * n + 1]
        x, y, c = _place()
        me = 2 * x + y
        chips = [(1 - x, y), (x, 1 - y), (1 - x, 1 - y)]
        _handshake([(px, py, c) for px, py in chips] + [(x, y, 1 - c)])

        def half(k, s, hh):
            return o_refs[k].at[s, pl.ds(hh * halves[k], halves[k]), :]

        sends = []
        for k in range(n):
            for j, (px, py) in enumerate(chips):
                sends.append(_rcopy(ssem, rsem, 6 * k + j, s_refs[k].at[pl.ds(c * halves[k], halves[k]), :],
                                    half(k, me, c), (px, py, c)))
        for cp in sends:
            cp.start()
        for k in range(n):
            for j, (px, py) in enumerate(chips):
                s = 2 * px + py
                _rcopy(ssem, rsem, 6 * k + j, half(k, s, c), half(k, s, c), (x, y, c)).wait_recv()
                fwd = _rcopy(ssem, rsem, 6 * k + 3 + j, half(k, s, c), half(k, s, c), (x, y, 1 - c))
                fwd.start()
                sends.append(fwd)
        for k in range(n):
            for j, (px, py) in enumerate(chips):
                s = 2 * px + py
                _rcopy(ssem, rsem, 6 * k + 3 + j, half(k, s, 1 - c), half(k, s, 1 - c), (x, y, c)).wait_recv()
        for cp in sends:
            cp.wait_send()

    return pl.kernel(
        body, name=name,
        out_type=[jax.ShapeDtypeStruct((4,) + t.shape, t.dtype) for t in shards],
        mesh=plsc.ScalarSubcoreMesh(axis_name="sequencer", num_cores=1),
        scratch_types=[pltpu.SemaphoreType.DMA((6 * n,)), pltpu.SemaphoreType.DMA((6 * n,))],
        compiler_params=pltpu.CompilerParams(collective_id=cid),
    )(*shards)


def swap_halves(gs, *, name):
    n = len(gs)

    def body(*refs):
        g_refs, o_refs, ssem, rsem = refs[:n], refs[n:2 * n], refs[2 * n], refs[2 * n + 1]
        x, y, c = _place()
        cps = []
        for k in range(n):
            hk = g_refs[k].shape[1] // 2
            cps.append(_rcopy(ssem, rsem, k, g_refs[k].at[:, pl.ds((1 - c) * hk, hk), :], o_refs[k], (x, y, 1 - c)))
        for cp in cps:
            cp.start()
        for cp in cps:
            cp.wait()

    return pl.pallas_call(
        body, name=name, interpret=False,
        out_shape=[jax.ShapeDtypeStruct((4, t.shape[1] // 2, t.shape[2]), t.dtype) for t in gs],
        in_specs=[ANY] * n, out_specs=[ANY] * n,
        scratch_shapes=[pltpu.SemaphoreType.DMA((n,)), pltpu.SemaphoreType.DMA((n,))],
    )(*gs)


def _sum_rows(hk):
    return _pick(hk, (512, 352, 256, 128))


def pair_sum(g, other, c_idx, *, name):
    _, hk, width = other.shape
    tr = _sum_rows(hk)
    nbk = hk // tr

    def body(c_ref, g_ref, o_ref, out_ref):
        out_ref[...] = (g_ref[...].astype(F32) + o_ref[...].astype(F32)).astype(BF16)

    return pl.pallas_call(
        body, name=name, interpret=False,
        out_shape=jax.ShapeDtypeStruct((4, hk, width), BF16),
        grid_spec=pltpu.PrefetchScalarGridSpec(
            num_scalar_prefetch=1, grid=(4, nbk),
            in_specs=[pl.BlockSpec((1, tr, width), lambda s, i, c_ref: (s, c_ref[0] * nbk + i, 0)),
                      pl.BlockSpec((1, tr, width), lambda s, i, c_ref: (s, i, 0))],
            out_specs=pl.BlockSpec((1, tr, width), lambda s, i, c_ref: (s, i, 0))),
        compiler_params=_params(("parallel", "parallel")),
    )(c_idx, g, other)


def chip_sum(p, got, idx, *, name):
    _, hk, width = got.shape
    tr = _sum_rows(hk)
    nbk = hk // tr

    def body(idx_ref, p_ref, g_ref, out_ref):
        acc = p_ref[0].astype(F32)
        for j in range(3):
            acc = acc + g_ref[j].astype(F32)
        out_ref[0] = acc

    return pl.pallas_call(
        body, name=name, interpret=False,
        out_shape=jax.ShapeDtypeStruct((2, hk, width), F32),
        grid_spec=pltpu.PrefetchScalarGridSpec(
            num_scalar_prefetch=1, grid=(nbk,),
            in_specs=[pl.BlockSpec((1, tr, width), lambda i, idx_ref: (idx_ref[0], i, 0)),
                      pl.BlockSpec((3, tr, width), lambda i, idx_ref: (0, i, 0))],
            out_specs=pl.BlockSpec((1, tr, width), lambda i, idx_ref: (idx_ref[1], i, 0))),
        compiler_params=_params(("parallel",)),
    )(idx, p, got)


def join_halves(qs):
    n = len(qs)

    def body(*refs):
        q_refs, o_refs, ssem, rsem = refs[:n], refs[n:2 * n], refs[2 * n], refs[2 * n + 1]
        x, y, c = _place()
        cps = [_rcopy(ssem, rsem, k, q_refs[k].at[c], o_refs[k].at[c], (x, y, 1 - c)) for k in range(n)]
        for cp in cps:
            cp.start()
        for k in range(n):
            _rcopy(ssem, rsem, k, q_refs[k].at[c], o_refs[k].at[1 - c], (x, y, 1 - c)).wait_recv()
        for cp in cps:
            cp.wait_send()

    return pl.pallas_call(
        body, name="join_halves", interpret=False,
        out_shape=[jax.ShapeDtypeStruct(t.shape, t.dtype) for t in qs],
        in_specs=[ANY] * n, out_specs=[ANY] * n, input_output_aliases={k: k for k in range(n)},
        scratch_shapes=[pltpu.SemaphoreType.DMA((n,)), pltpu.SemaphoreType.DMA((n,))],
    )(*qs)


def scatter_chips_beside(ps, cid, name):
    n = len(ps)

    def body(*refs):
        p_refs, o_refs, ssem, rsem = refs[:n], refs[n:2 * n], refs[2 * n], refs[2 * n + 1]
        x, y, c = _place()
        chips = [(1 - x, y), (x, 1 - y), (1 - x, 1 - y)]
        _handshake([(px, py, c) for px, py in chips])
        cps = [_rcopy(ssem, rsem, 3 * k + j, p_refs[k].at[2 * px + py], o_refs[k].at[j], (px, py, c))
               for k in range(n) for j, (px, py) in enumerate(chips)]
        for cp in cps:
            cp.start()
        for cp in cps:
            cp.wait()

    return pl.kernel(
        body, name=name, out_type=[jax.ShapeDtypeStruct((3,) + t.shape[1:], t.dtype) for t in ps],
        mesh=plsc.ScalarSubcoreMesh(axis_name="sequencer", num_cores=1),
        scratch_types=[pltpu.SemaphoreType.DMA((3 * n,)), pltpu.SemaphoreType.DMA((3 * n,))],
        compiler_params=pltpu.CompilerParams(collective_id=cid),
    )(*ps)


def reduce_begin(gs, names, c_idx, cid, tag):
    others = swap_halves(gs, name=f"swap_halves_{tag}")
    pairs = [pair_sum(g, o, c_idx, name=f"pair_sum_{nm}") for g, o, nm in zip(gs, others, names)]
    return pairs, scatter_chips_beside(pairs, cid, f"scatter_chips_{tag}")


def reduce_end(pairs, gots, names, idx):
    mine = [chip_sum(p, g, idx, name=f"chip_sum_{nm}") for p, g, nm in zip(pairs, gots, names)]
    return [q.reshape(2 * q.shape[1], q.shape[2]) for q in join_halves(mine)]


def gather_small(v):
    def body(v_ref, o_ref, ssem, rsem, lsem):
        x, y, c = _place()
        peers = []
        for k in range(1, 8):
            fx, fy, fc = (k >> 2) & 1, (k >> 1) & 1, k & 1
            peers.append((1 - x if fx else x, 1 - y if fy else y, 1 - c if fc else c))
        _handshake(peers)
        loc = pltpu.make_async_copy(v_ref, o_ref.at[4 * x + 2 * y + c], lsem)
        loc.start()
        cps = []
        for k, (px, py, pc) in enumerate(peers):
            cps.append((pltpu.make_async_remote_copy(
                src_ref=v_ref, dst_ref=o_ref.at[4 * x + 2 * y + c], send_sem=ssem.at[k], recv_sem=rsem.at[k],
                device_id=(px, py, pc), device_id_type=MESH), 4 * px + 2 * py + pc))
        for cp, _ in cps:
            cp.start()
        for k, (cp, peer) in enumerate(cps):
            pltpu.make_async_remote_copy(
                src_ref=v_ref, dst_ref=o_ref.at[peer], send_sem=ssem.at[k], recv_sem=rsem.at[k],
                device_id=(x, y, c), device_id_type=MESH).wait_recv()
        for cp, _ in cps:
            cp.wait_send()
        loc.wait()

    return pl.kernel(
        body, name="gather_small", out_type=jax.ShapeDtypeStruct((8, SV_ROWS, 1024), F32),
        mesh=plsc.ScalarSubcoreMesh(axis_name="sequencer", num_cores=1),
        scratch_types=[pltpu.SemaphoreType.DMA((7,)), pltpu.SemaphoreType.DMA((7,)), pltpu.SemaphoreType.DMA],
        compiler_params=pltpu.CompilerParams(collective_id=6),
    )(v)


def sum_slots(a):
    def fn(i, t):
        acc = t[0]
        for k in range(1, 8):
            acc = acc + t[k]
        return acc

    return rowwise(fn, [whole(a)], [((SV_ROWS, 1024), F32, (SV_ROWS, 1024), lambda i: (0, 0), "w")], steps=1,
                   name="sum_slots")[0]


def _head_rms(x, nw):
    xs, rs = [], []
    for h in range(DN_H):
        xh = x[:, h * DN_D:(h + 1) * DN_D]
        r = lax.rsqrt(jnp.mean(xh * xh, axis=1, keepdims=True) + EPS)
        xs.append(xh * r)
        rs.append(r)
    return xs, rs


def bg_fwd(p, alog, dtb):
    rows = p.shape[0]
    tr = _pick(rows, (384, 128))

    def fn(i, x, al, dt):
        lane = lax.broadcasted_iota(jnp.int32, x.shape, 1)
        row = i + lax.broadcasted_iota(jnp.int32, x.shape, 0)
        g = -jnp.exp(al) * _softplus(x + dt)
        out = jnp.where(lane < 4, _sigmoid(x), jnp.where(lane < 8, g, 0.0))
        return jnp.where(row >= PAD, out, 0.0)

    return rowwise(fn, [cols(p, tr, 128, BG0 // 128), whole(alog), whole(dtb)], [out2d(rows, 128, F32, tr)],
                   steps=rows // tr, name="bg_fwd")[0]


def bg_bwd(p, alog, dtb, dbg):
    rows = p.shape[0]
    tr = _pick(rows, (384, 128))

    def fn(i, x, al, dt, g_in):
        lane = lax.broadcasted_iota(jnp.int32, x.shape, 1)
        row = i + lax.broadcasted_iota(jnp.int32, x.shape, 0)
        live = row >= PAD
        is_b = jnp.logical_and(live, lane < 4)
        is_g = jnp.logical_and(live, jnp.logical_and(lane >= 4, lane < 8))
        beta = _sigmoid(x)
        ea = jnp.exp(al)
        g = -ea * _softplus(x + dt)
        dalpha = jnp.where(is_g, g_in * (-ea) * _sigmoid(x + dt), 0.0)
        dx = jnp.where(is_b, g_in * beta * (1.0 - beta), dalpha)
        dal = jnp.sum(jnp.where(is_g, g_in * g, 0.0), axis=0, keepdims=True)
        return jnp.concatenate([dx, jnp.zeros(x.shape, F32)], axis=1), dal, jnp.sum(dalpha, axis=0, keepdims=True)

    return rowwise(fn, [cols(p, tr, 128, BG0 // 128), whole(alog), whole(dtb), cols(dbg, tr)],
                   [out2d(rows, 256, BF16, tr)], steps=rows // tr, name="bg_bwd",
                   accs=[((1, 128), F32), ((1, 128), F32)])


def dn_qkv_post(j, y):
    xs = _silu(y)
    sc = jnp.where(j == 0, DN_D ** -0.5, 1.0)
    outs = []
    for h in range(DN_H):
        xh = xs[:, h * DN_D:(h + 1) * DN_D]
        r = lax.rsqrt(jnp.sum(xh * xh, axis=1, keepdims=True) + EPS)
        outs.append(jnp.where(j < 2, xh * r * sc, xh))
    return jnp.concatenate(outs, axis=1), y


def dn_qkv_bwd(cq, dq, dk, dv):
    rows = cq.shape[0]
    tr = _pick(rows, (384, 128))

    def fn(i, c0, c1, c2, g0, g1, g2):
        pieces = []
        for kind, (cv, g) in enumerate(((c0, g0), (c1, g1), (c2, g2))):
            xs = _silu(cv)
            if kind < 2:
                sc = DN_D ** -0.5 if kind == 0 else 1.0
                ds = []
                for h in range(DN_H):
                    sl = slice(h * DN_D, (h + 1) * DN_D)
                    xh, gh = xs[:, sl], g[:, sl]
                    r = lax.rsqrt(jnp.sum(xh * xh, axis=1, keepdims=True) + EPS)
                    xn = xh * r
                    ds.append(sc * r * (gh - xn * jnp.sum(gh * xn, axis=1, keepdims=True)))
                dxs = jnp.concatenate(ds, axis=1)
            else:
                dxs = g
            pieces.append(dxs * _dsilu(cv))
        return jnp.concatenate(pieces, axis=1)

    ins = [cols(cq, tr, DN_DIM, k) for k in range(3)] + [cols(t, tr) for t in (dq, dk, dv)]
    return rowwise(fn, ins, [out2d(rows, 3 * DN_DIM, F32, tr)], steps=rows // tr, name="dn_qkv_bwd")[0]


def dn_out_fwd(o, p, nw):
    rows = o.shape[0]
    tr = _pick(rows, (384, 128))

    def fn(i, ov, z, w):
        xs, _ = _head_rms(ov, w)
        return jnp.concatenate(xs, axis=1) * jnp.concatenate([w] * DN_H, axis=1) * _silu(z)

    return rowwise(fn, [cols(o, tr), cols(p, tr, DN_DIM, 6), whole(nw)], [out2d(rows, DN_DIM, BF16, tr)],
                   steps=rows // tr, name="dn_out_fwd")[0]


def dn_out_bwd(o, p, nw, dymix):
    rows = o.shape[0]
    tr = _pick(rows, (384, 128))

    def fn(i, ov, z, w, dy):
        xs, rs = _head_rms(ov, w)
        sz = _silu(z)
        dn = dy * sz
        dos, dw = [], jnp.zeros((1, DN_D), F32)
        for h in range(DN_H):
            sl = slice(h * DN_D, (h + 1) * DN_D)
            gw = dn[:, sl] * w
            dos.append(rs[h] * (gw - xs[h] * jnp.mean(gw * xs[h], axis=1, keepdims=True)))
            dw = dw + jnp.sum(dn[:, sl] * xs[h], axis=0, keepdims=True)
        n = jnp.concatenate(xs, axis=1) * jnp.concatenate([w] * DN_H, axis=1)
        return jnp.concatenate(dos, axis=1), dy * n * _dsilu(z), dw

    return rowwise(fn, [cols(o, tr), cols(p, tr, DN_DIM, 6), whole(nw), cols(dymix, tr, DN_DIM, 1)],
                   [out2d(rows, DN_DIM, F32, tr), out2d(rows, DN_DIM, BF16, tr)], steps=rows // tr,
                   name="dn_out_bwd", accs=[((1, DN_D), F32)])


def conv_a_pre_bwd(dymix, cv, p):
    rows = cv.shape[0]
    tr = _pick(rows, (384, 128))

    def fn(i, dy, c, go):
        return dy * c, dy * go

    return rowwise(fn, [cols(dymix, tr, D_CONV, 0), cols(cv, tr), cols(p, tr, D_CONV, 1)],
                   [out2d(rows, D_CONV, BF16, tr), out2d(rows, D_CONV, F32, tr)], steps=rows // tr,
                   name="conv_a_pre_bwd")


def _rows8(w):
    return jnp.pad(w.astype(F32), ((0, 8 - w.shape[0]), (0, 0)))


def _lanes(v, at):
    return jnp.pad(v.astype(F32), (at, 128 - at - v.shape[0]))[None]


def add_norm(a, w, h, next_nw, *, name):
    return mm(a, w, name=name, epi=_add_norm_epi, epi_ins=[(h, lambda j: 0)], epi_consts=[next_nw],
              epi_outs=[F32, BF16])


def _add_norm_epi(row0, t, h, nw):
    x = t + h
    return x, x * lax.rsqrt(jnp.mean(x * x, axis=1, keepdims=True) + EPS) * nw


def ffn_up_conv(hn, w_up, cw8, *, name):
    rows = hn.shape[0]
    tn = w_up.shape[2]
    tm = _pick(rows, (384, 128))
    nr = rows // tm

    def body(x_ref, wg_ref, wv_ref, w_ref, ug_ref, uv_ref, gc_ref, a_ref, carry, scr):
        i = pl.program_id(1)
        x = x_ref[...]
        gate = _dot(x, wg_ref[...])
        val = _dot(x, wv_ref[...])
        ug_ref[...] = gate.astype(BF16)
        uv_ref[...] = val.astype(BF16)
        scr[0:8, :] = jnp.where(i > 0, carry[...], 0.0)
        scr[8:8 + tm, :] = gate
        carry[...] = gate[tm - 8:tm]
        y = jnp.zeros((tm, tn), F32)
        for q in range(3):
            sh = 2 - q
            y = y + w_ref[q:q + 1, :] * scr[8 - sh:8 - sh + tm, :]
        gc_ref[...] = y.astype(BF16)
        a_ref[...] = (_silu(y) * val).astype(BF16)

    half = pl.BlockSpec((tm, tn), lambda j, i: (i, j))
    return pl.pallas_call(
        body, name=name, interpret=False,
        out_shape=[jax.ShapeDtypeStruct((rows, D_FF), BF16)] * 4,
        grid=(D_FF // tn, nr),
        in_specs=[pl.BlockSpec((tm, D), lambda j, i: (i, 0)),
                  pl.BlockSpec((None, D, tn), lambda j, i: (j, 0, 0)),
                  pl.BlockSpec((None, D, tn), lambda j, i: (j + D_FF // tn, 0, 0)),
                  pl.BlockSpec((8, tn), lambda j, i: (0, j))],
        out_specs=[half] * 4,
        scratch_shapes=[pltpu.VMEM((8, tn), F32), pltpu.VMEM((tm + 8, tn), F32)],
        compiler_params=_params(("arbitrary", "arbitrary")),
    )(hn, w_up, w_up, cw8)


def ffn_down_bwd(dh, w_down, gc, uv, ug, cw8, *, name):
    rows = dh.shape[0]
    tn = D_FF // 2
    tm = _pick(rows, (384, 128))
    nr = rows // tm
    r8 = tm // 8

    def body(dh_ref, w_ref, gc_ref, uv_ref, ug_ref, halo_ref, cw_ref, du_ref, dw_ref, carry, gscr, xscr):
        ip = pl.program_id(1)
        i = nr - 1 - ip
        da = _dot(dh_ref[...].astype(BF16), w_ref[...], 1, 1)
        c, val = gc_ref[...].astype(F32), uv_ref[...].astype(F32)
        dgc = da * val * _dsilu(c)
        du_ref[:, tn:] = (da * _silu(c)).astype(BF16)
        gscr[0:tm, :] = dgc
        gscr[tm:tm + 8, :] = jnp.where(ip > 0, carry[...], 0.0)
        carry[...] = dgc[0:8]
        xscr[0:8, :] = jnp.where(i > 0, halo_ref[...].astype(F32), 0.0)
        xscr[8:8 + tm, :] = ug_ref[...].astype(F32)
        dx = jnp.zeros((tm, tn), F32)
        dws = []
        for q in range(3):
            sh = 2 - q
            dx = dx + cw_ref[q:q + 1, :] * gscr[sh:sh + tm, :]
            dws.append(jnp.sum(dgc * xscr[8 - sh:8 - sh + tm, :], axis=0, keepdims=True))
        du_ref[:, :tn] = dx.astype(BF16)

        @pl.when(ip == 0)
        def _():
            dw_ref[...] = jnp.zeros((8, tn), F32)

        dw_ref[...] += jnp.concatenate(dws + [jnp.zeros((5, tn), F32)], axis=0)

    rev = lambda ip: nr - 1 - ip
    tile = lambda arr: pl.BlockSpec((tm, tn), lambda j, ip: (rev(ip), j))
    return pl.pallas_call(
        body, name=name, interpret=False,
        out_shape=[jax.ShapeDtypeStruct((rows, 2 * D_FF), BF16), jax.ShapeDtypeStruct((8, D_FF), F32)],
        grid=(2, nr),
        in_specs=[pl.BlockSpec((tm, D), lambda j, ip: (rev(ip), 0)),
                  pl.BlockSpec((tn, D), lambda j, ip: (j, 0)),
                  tile(gc), tile(uv), tile(ug),
                  pl.BlockSpec((8, tn), lambda j, ip: (jnp.maximum(rev(ip) * r8 - 1, 0), j)),
                  pl.BlockSpec((8, tn), lambda j, ip: (0, j))],
        out_specs=[pl.BlockSpec((tm, 2 * tn), lambda j, ip: (rev(ip), j)),
                   pl.BlockSpec((8, tn), lambda j, ip: (0, j))],
        scratch_shapes=[pltpu.VMEM((8, tn), F32), pltpu.VMEM((tm + 8, tn), F32), pltpu.VMEM((tm + 8, tn), F32)],
        compiler_params=_params(("arbitrary", "arbitrary")),
    )(dh, w_down, gc, uv, ug, ug, cw8)


def ffn_fwd(h, hn, w_up, cw8, w_down, tag, next_nw=None, target=None):
    ug, uv, gc, a = ffn_up_conv(hn, w_up, cw8, name=f"ffn{tag}_up")
    if target is not None:
        out, hn_next = add_loss(a, w_down, h, target, name=f"ffn{tag}_down")
    else:
        out, hn_next = add_norm(a, w_down, h, next_nw, name=f"ffn{tag}_down")
    return out, hn_next, (hn, ug, uv, a, gc)


def ffn_bwd(h, nw, w_up, cw8, w_down, saved, dh, tag):
    hn, ug, uv, a, gc = saved
    du, d_cw = ffn_down_bwd(dh, w_down, gc, uv, ug, cw8, name=f"ffn{tag}_down_dx")
    d_w_down = mm(a, dh, ta=True, out_dtype=BF16, name=f"ffn{tag}_down_dw")
    dh_new, d_nw = dx_rms_bwd(du, w_up, h, nw, dh, name=f"ffn{tag}_up_dx", b_chip=True, swap_mid=True)
    d_w_up = mm(hn, du, ta=True, out_dtype=BF16, out_chip=True, swap_mid=True, name=f"ffn{tag}_up_dw")
    return dh_new, d_nw, d_w_up, d_cw, d_w_down


def mixer_fwd(h, nw, w_in, ca8, dc8, alog, dtb, dnw, w_out, tie=None, next_nw=None):
    rows = h.shape[0]
    tr = _pick(rows, (384, 128))
    hn = rms_fwd(h, nw, name="mix_norm")
    if callable(w_in):
        hn, w_in = w_in(hn)
    p = mm(hn, w_in, name="mix_in")
    y_a, cv = conv_fwd([(p, 0), (p, 2)], ca8, 3, rows=rows, c=D_CONV, tc=D_CONV, tr=tr, name="conv_a",
                       pre=lambda gi, ah: gi * ah, post=lambda j, y, go: (go * y, y), extras=[(p, 1)],
                       outs=[BF16, F32])
    qkv_n, cq = conv_fwd([(p, 3)], dc8, 4, rows=rows, c=3 * DN_DIM, tc=DN_DIM, tr=tr, name="dn_conv",
                         post=dn_qkv_post, outs=[F32, F32], strip=tr)
    bgcol = bg_fwd(p, alog, dtb)
    if tie is not None:
        bgcol = tie(bgcol)
    bgrow = bgcol[:, :8].reshape(rows // CH, CH, 8).transpose(0, 2, 1)
    o, s_all, ti_all = dn_fwd(qkv_n, bgcol, bgrow)
    y_b = dn_out_fwd(o, p, dnw)
    ymix = jnp.concatenate([y_a, y_b], axis=1)
    w_out = w_out() if callable(w_out) else w_out
    out, hn_next = add_norm(ymix, w_out, h, next_nw, name="mix_out")
    return out, hn_next, (hn, p, cv, qkv_n, cq, bgcol, bgrow, o, s_all, ti_all, ymix, w_in)


def mixer_bwd(h, nw, ca8, dc8, alog, dtb, dnw, w_out, saved, dh):
    hn, p, cv, qkv_n, cq, bgcol, bgrow, o, s_all, ti_all, ymix, w_in = saved
    rows = h.shape[0]
    tr = _pick(rows, (384, 128))
    dymix = mm(dh, w_out, tb=True, name="mix_out_dx")
    d_w_out = mm(ymix, dh, ta=True, out_dtype=BF16, name="mix_out_dw")
    do, dz, d_dnw = dn_out_bwd(o, p, dnw, dymix)
    dq, dk, dv, dbg = dn_bwd(qkv_n, bgcol, bgrow, s_all, ti_all, do)
    dbg_p, d_alog, d_dtb = bg_bwd(p, alog, dtb, dbg)
    dcq = dn_qkv_bwd(cq, dq, dk, dv)
    dqkv, d_dc = conv_bwd([(p, 3)], dc8, 4, dcq, rows=rows, c=3 * DN_DIM, tc=DN_DIM, tr=tr, name="dn_conv_bwd",
                          post=lambda dx: dx, outs=[BF16])
    dgo, dcv = conv_a_pre_bwd(dymix, cv, p)
    dgi, dah, d_ca = conv_bwd([(p, 0), (p, 2)], ca8, 3, dcv, rows=rows, c=D_CONV, tc=D_CONV, tr=tr,
                              name="conv_a_bwd", pre=lambda gi, ah: gi * ah,
                              post=lambda dm, gi, ah: (dm * ah, dm * gi), extras=[(p, 0), (p, 2)], outs=[BF16, BF16])
    dp = jnp.concatenate([dgi, dgo, dah, dqkv, dz, dbg_p], axis=1)
    dh_new, d_nw = dx_rms_bwd(dp, w_in, h, nw, dh, name="mix_in_dx")
    d_w_in = mm(hn, dp, ta=True, out_dtype=BF16, name="mix_in_dw")
    return dh_new, d_nw, d_w_in, d_ca, d_dc, d_alog, d_dtb, d_dnw, d_w_out


def swa_layer_fwd(h, hn, wqkv, qw, kw, sinks, wo, next_nw):
    qkv = mm(hn, wqkv, name="swa_qkv")
    qh, kh, vh = qknorm_fwd(qkv, qw, kw)
    att = swa_fwd(qh, kh, vh, sinks)
    out, hn_next = add_norm(att, wo, h, next_nw, name="swa_out")
    return out, hn_next, (hn, qkv, qh, kh, vh, att)


def swa_layer_bwd(h, nw, wqkv, qw, kw, sinks, wo, saved, dh):
    hn, qkv, qh, kh, vh, att = saved
    datt = mm(dh, wo, tb=True, out_dtype=BF16, name="swa_out_dx")
    d_wo = mm(att, dh, ta=True, out_dtype=BF16, name="swa_out_dw")
    dqh, dkh, dvh, dsk = swa_bwd(qh, kh, vh, sinks, datt)
    dqkv, d_qw, d_kw = qknorm_bwd(qkv, qw, kw, dqh, dkh, dvh)
    dh_new, d_nw = dx_rms_bwd(dqkv, wqkv, h, nw, dh, name="swa_qkv_dx")
    d_wqkv = mm(hn, dqkv, ta=True, out_dtype=BF16, name="swa_qkv_dw")
    d_sinks = jnp.sum(dsk[:, :, 0], axis=0)
    return dh_new, d_nw, d_wqkv, d_qw, d_kw, d_sinks, d_wo


BIG = ("mix_w_in", "mix_w_out", "swa_wq", "swa_wk", "swa_wv", "swa_wo", "ffn_w_up", "ffn_w_down")


def _flat_pad(parts, rows):
    v = jnp.concatenate([t.astype(F32).reshape(-1) for t in parts])
    return jnp.pad(v, (0, rows * 1024 - v.shape[0])).reshape(rows, 1024)


def _split_flat(flat, shapes):
    v = flat.reshape(-1)
    out, o = [], 0
    for s in shapes:
        n = 1
        for d_ in s:
            n *= d_
        out.append(v[o:o + n].reshape(s))
        o += n
    return out


def local_step(x0, target0, meta_full, anw, fnw, w_in, ca8, dc8, alog, dtb, dnw, qw, kw, sinks, fc8, late,
               begin=None, tie=None):
    begin = begin or (lambda tag, names, grads: None)
    h0 = jnp.concatenate([jnp.zeros((PAD, D), F32), meta_full, x0], axis=0)
    h1, hn1, s_mix = mixer_fwd(h0, anw[0], w_in, ca8, dc8, alog, dtb, dnw, lambda: late()[0], tie, fnw[0])
    w_out, wqkv, wo, w_up, w_down = late()
    h2, hn2, s_f0 = ffn_fwd(h1, hn1, w_up[0], fc8[0], w_down[0], 0, anw[1])
    h3, hn3, s_swa = swa_layer_fwd(h2, hn2, wqkv, qw, kw, sinks, wo, fnw[1])
    dh, loss_l, s_f1 = ffn_fwd(h3, hn3, w_up[1], fc8[1], w_down[1], 1, target=target0)
    dh, d_fnw1, d_up1, d_fc1, d_down1 = ffn_bwd(h3, fnw[1], w_up[1], fc8[1], w_down[1], s_f1, dh, 1)
    begin("ffn1", ("up1", "down1"), [d_up1, d_down1.reshape(4, 704, D)])
    dh, d_anw1, d_wqkv, d_qw, d_kw, d_sinks, d_wo = swa_layer_bwd(h2, anw[1], wqkv, qw, kw, sinks, wo, s_swa, dh)
    begin("swa", ("wq", "wk", "wv", "wo"),
          [d_wqkv[:, :D].reshape(4, 256, D), d_wqkv[:, D:D + 256].reshape(4, 256, 256),
           d_wqkv[:, D + 256:].reshape(4, 256, 256), d_wo.reshape(4, 256, D)])
    dh, d_fnw0, d_up0, d_fc0, d_down0 = ffn_bwd(h1, fnw[0], w_up[0], fc8[0], w_down[0], s_f0, dh, 0)
    begin("ffn0", ("up0", "down0"), [d_up0, d_down0.reshape(4, 704, D)])
    dh, d_anw0, d_w_in, d_ca, d_dc, d_alog, d_dtb, d_dnw, d_w_out = mixer_bwd(
        h0, anw[0], ca8, dc8, alog, dtb, dnw, w_out, s_mix, dh)
    begin("mix", ("w_in", "w_out"),
          [d_w_in[:, :IN_DIM].reshape(D, 4, 898).transpose(1, 0, 2), d_w_out.reshape(4, 256, D)])
    return (dh, loss_l, d_anw0, d_anw1, d_fnw0, d_fnw1, d_w_in, d_ca, d_dc, d_alog, d_dtb, d_dnw, d_w_out, d_wqkv,
            d_qw, d_kw, d_sinks, d_wo, d_up0, d_up1, d_fc0, d_fc1, d_down0, d_down1)


def kernel(x, meta_tokens, attn_norm_w, ffn_norm_w, mix_w_in, conv_a_w, dn_conv_w, dn_a_log, dn_dt_bias, dn_norm_w, mix_w_out, swa_wq, swa_wk, swa_wv, swa_q_norm_w, swa_k_norm_w, swa_sinks, swa_wo, ffn_w_up, ffn_conv_w, ffn_w_down, loss_target, m_meta_tokens, m_attn_norm_w, m_ffn_norm_w, m_mix_w_in, m_conv_a_w, m_dn_conv_w, m_dn_a_log, m_dn_dt_bias, m_dn_norm_w, m_mix_w_out, m_swa_wq, m_swa_wk, m_swa_wv, m_swa_q_norm_w, m_swa_k_norm_w, m_swa_sinks, m_swa_wo, m_ffn_w_up, m_ffn_conv_w, m_ffn_w_down, v_meta_tokens, v_attn_norm_w, v_ffn_norm_w, v_mix_w_in, v_conv_a_w, v_dn_conv_w, v_dn_a_log, v_dn_dt_bias, v_dn_norm_w, v_mix_w_out, v_swa_wq, v_swa_wk, v_swa_wv, v_swa_q_norm_w, v_swa_k_norm_w, v_swa_sinks, v_swa_wo, v_ffn_w_up, v_ffn_conv_w, v_ffn_w_down):
    ix, iy, ic = lax.axis_index("x"), lax.axis_index("y"), lax.axis_index("c")
    chip = 2 * ix + iy
    seq = x.shape[1]
    rows = HEAD0 + seq

    small_sharded = (conv_a_w, dn_conv_w, ffn_conv_w, meta_tokens)
    up_b, down_b = ffn_w_up.astype(BF16), ffn_w_down.astype(BF16)
    own = [mix_w_in[0].astype(BF16), mix_w_out[0].astype(BF16), swa_wq[0].astype(BF16), swa_wk[0].astype(BF16),
           swa_wv[0].astype(BF16), swa_wo[0].astype(BF16), up_b[0], up_b[1], down_b[0], down_b[1]]
    fill = lambda gathered, mine: [lax.dynamic_update_slice_in_dim(g, t[None], chip, axis=0)
                                   for g, t in zip(gathered, mine)]
    on_its_way, = gather_weights_beside(own[:1], 9, "gather_w_in")
    _, g_small = gather_weights([], _flat_pad(small_sharded, SW_ROWS))

    rest = {}

    def w_in(hn):
        hn, got, g_out = lax.optimization_barrier((hn, on_its_way, own[1]))
        rest["w_out"] = fill(gather_weights_beside([g_out], 1, "gather_w_out"), [g_out])
        g_in, = fill([got], own[:1])
        return hn, jnp.pad(g_in.transpose(1, 0, 2).reshape(D, IN_DIM), ((0, 0), (0, P_W - IN_DIM)))

    def tie(t):
        t, *mine = lax.optimization_barrier((t, *own[2:]))
        g_q, g_k, g_v, g_o, g_up0, g_up1, g_dn0, g_dn1 = mine
        soon, last = [g_up0, g_dn0, g_q, g_k, g_v, g_o], [g_up1, g_dn1]
        rest["soon"] = fill(gather_weights_beside(soon, 7, "gather_layers_12"), soon)
        rest["last"] = fill(gather_weights_beside(last, 8, "gather_layer_3"), last)
        return t

    def late():
        (g_out,), (g_up0, g_dn0, g_q, g_k, g_v, g_o), (g_up1, g_dn1) = rest["w_out"], rest["soon"], rest["last"]
        wqkv = jnp.concatenate([g_q.reshape(D, D), g_k.reshape(D, 256), g_v.reshape(D, 256)], axis=1)
        return (g_out.reshape(D, D), wqkv, g_o.reshape(D, D), [g_up0, g_up1],
                [g_dn0.reshape(D_FF, D), g_dn1.reshape(D_FF, D)])

    gs = g_small.reshape(4, -1)
    ca_full = gs[:, 0:384].reshape(4, 3, 128).transpose(1, 0, 2).reshape(3, D_CONV)
    dc_full = gs[:, 384:1920].reshape(4, 4, 384).transpose(1, 0, 2).reshape(4, 3 * DN_DIM)
    fc_full = gs[:, 1920:6144].reshape(4, 2, 3, 704).transpose(1, 2, 0, 3).reshape(2, 3, D_FF)
    meta_full = gs[:, 6144:10240].reshape(4, N_META, 256).transpose(1, 0, 2).reshape(N_META, D)
    ca8, dc8 = _rows8(ca_full), _rows8(dc_full)
    fc8 = [_rows8(fc_full[0]), _rows8(fc_full[1])]
    alog, dtb = _lanes(dn_a_log[0], 4), _lanes(dn_dt_bias[0], 4)
    dnw = dn_norm_w.astype(F32)
    qw, kw = swa_q_norm_w.astype(F32), swa_k_norm_w.astype(F32)
    sinks = swa_sinks[0].astype(F32)
    anw = [attn_norm_w[0:1], attn_norm_w[1:2]]
    fnw = [ffn_norm_w[0:1], ffn_norm_w[1:2]]

    c_idx = jnp.reshape(ic, (1,)).astype(jnp.int32)
    chip_idx = jnp.stack([chip, ic]).astype(jnp.int32)
    begun = []

    def begin(tag, names, grads):
        pairs, gots = reduce_begin(grads, names, c_idx, 2 + len(begun), tag)
        begun.append((names, pairs, gots))

    (dh, loss_l, d_anw0, d_anw1, d_fnw0, d_fnw1, d_w_in, d_ca, d_dc, d_alog, d_dtb, d_dnw, d_w_out, d_wqkv, d_qw,
     d_kw, d_sinks, d_wo, d_up0, d_up1, d_fc0, d_fc1, d_down0, d_down1) = local_step(
        x[0], loss_target[0], meta_full, anw, fnw, w_in, ca8, dc8, alog, dtb, dnw, qw, kw, sinks, fc8, late,
        begin, tie)
    grad_x = dh[HEAD0:][None]

    small_parts = [jnp.concatenate([d_anw0, d_anw1], axis=0), jnp.concatenate([d_fnw0, d_fnw1], axis=0),
                   d_alog[0, 4:8], d_dtb[0, 4:8], d_dnw, d_qw, d_kw, d_sinks,
                   d_ca[:3], d_dc[:4], jnp.stack([d_fc0[:3], d_fc1[:3]]), dh[PAD:HEAD0], loss_l[0, 0:1]]
    small_shapes = [(2, D), (2, D), (1, 4), (1, 4), (1, DN_D), (1, SWA_D), (1, SWA_D), (1, SWA_H),
                    (1, 3, D_CONV), (1, 4, 3 * DN_DIM), (2, 3, D_FF), (N_META, D), ()]
    gathered_small = gather_small(_flat_pad(small_parts, SV_ROWS))

    red_big = {}
    for part in (begun[:-1], begun[-1:]):
        part_names = [n for names, _, _ in part for n in names]
        red_big.update(zip(part_names, reduce_end([p for _, ps, _ in part for p in ps],
                                                  [g for _, _, gs_ in part for g in gs_], part_names, chip_idx)))
    g_w_in, g_w_out, g_wq, g_wk, g_wv, g_wo, g_up0, g_up1, g_dn0, g_dn1 = [
        red_big[n] for n in ("w_in", "w_out", "wq", "wk", "wv", "wo", "up0", "up1", "down0", "down1")]

    grads = dict(mix_w_in=g_w_in, mix_w_out=g_w_out, swa_wq=g_wq, swa_wk=g_wk, swa_wv=g_wv, swa_wo=g_wo,
                 ffn_w_up=[g_up0, g_up1], ffn_w_down=[g_dn0, g_dn1])
    weights = dict(meta_tokens=meta_tokens, attn_norm_w=attn_norm_w, ffn_norm_w=ffn_norm_w, mix_w_in=mix_w_in,
                   conv_a_w=conv_a_w, dn_conv_w=dn_conv_w, dn_a_log=dn_a_log, dn_dt_bias=dn_dt_bias,
                   dn_norm_w=dn_norm_w, mix_w_out=mix_w_out, swa_wq=swa_wq, swa_wk=swa_wk, swa_wv=swa_wv,
                   swa_q_norm_w=swa_q_norm_w, swa_k_norm_w=swa_k_norm_w, swa_sinks=swa_sinks, swa_wo=swa_wo,
                   ffn_w_up=ffn_w_up, ffn_conv_w=ffn_conv_w, ffn_w_down=ffn_w_down)
    m_in = dict(meta_tokens=m_meta_tokens, attn_norm_w=m_attn_norm_w, ffn_norm_w=m_ffn_norm_w, mix_w_in=m_mix_w_in,
                conv_a_w=m_conv_a_w, dn_conv_w=m_dn_conv_w, dn_a_log=m_dn_a_log, dn_dt_bias=m_dn_dt_bias,
                dn_norm_w=m_dn_norm_w, mix_w_out=m_mix_w_out, swa_wq=m_swa_wq, swa_wk=m_swa_wk, swa_wv=m_swa_wv,
                swa_q_norm_w=m_swa_q_norm_w, swa_k_norm_w=m_swa_k_norm_w, swa_sinks=m_swa_sinks, swa_wo=m_swa_wo,
                ffn_w_up=m_ffn_w_up, ffn_conv_w=m_ffn_conv_w, ffn_w_down=m_ffn_w_down)
    v_in = dict(meta_tokens=v_meta_tokens, attn_norm_w=v_attn_norm_w, ffn_norm_w=v_ffn_norm_w, mix_w_in=v_mix_w_in,
                conv_a_w=v_conv_a_w, dn_conv_w=v_dn_conv_w, dn_a_log=v_dn_a_log, dn_dt_bias=v_dn_dt_bias,
                dn_norm_w=v_dn_norm_w, mix_w_out=v_mix_w_out, swa_wq=v_swa_wq, swa_wk=v_swa_wk, swa_wv=v_swa_wv,
                swa_q_norm_w=v_swa_q_norm_w, swa_k_norm_w=v_swa_k_norm_w, swa_sinks=v_swa_sinks, swa_wo=v_swa_wo,
                ffn_w_up=v_ffn_w_up, ffn_conv_w=v_ffn_conv_w, ffn_w_down=v_ffn_w_down)
    names = list(weights)
    small = [n for n in names if n not in BIG]
    delta, new_m, new_v = {}, {}, {}
    for n in BIG:
        delta[n], new_m[n], new_v[n], grads[n] = adamw(weights[n], grads[n], m_in[n], v_in[n], name=f"adamw_{n}")
    gathered_small, _ = lax.optimization_barrier((gathered_small, new_v["ffn_w_down"]))
    (g_anw, g_fnw, g_alog, g_dtb, g_dnw, g_qw, g_kw, g_sinks, g_ca_f, g_dc_f, g_fc_f, g_meta_f,
     loss) = _split_flat(sum_slots(gathered_small), small_shapes)
    grads.update(meta_tokens=lax.dynamic_slice_in_dim(g_meta_f, chip * 256, 256, axis=1), attn_norm_w=g_anw,
                 ffn_norm_w=g_fnw, conv_a_w=lax.dynamic_slice_in_dim(g_ca_f, chip * 128, 128, axis=2),
                 dn_conv_w=lax.dynamic_slice_in_dim(g_dc_f, chip * 384, 384, axis=2), dn_a_log=g_alog,
                 dn_dt_bias=g_dtb, dn_norm_w=g_dnw, swa_q_norm_w=g_qw, swa_k_norm_w=g_kw, swa_sinks=g_sinks,
                 ffn_conv_w=lax.dynamic_slice_in_dim(g_fc_f, chip * 704, 704, axis=2))
    grads = {n: grads[n].reshape(weights[n].shape) for n in names}
    shapes = [weights[n].shape for n in small]
    packed = [_flat_pad([t[n] for n in small], SW_ROWS) for t in (weights, grads, m_in, v_in)]
    for store, flat in zip((delta, new_m, new_v), adamw(*packed, name="adamw_small")):
        for n, t in zip(small, _split_flat(flat, shapes)):
            store[n] = t
    return (loss, grad_x, *[grads[n] for n in names], *[delta[n] for n in names],
            *[new_m[n] for n in names], *[new_v[n] for n in names])
```

```python
import functools

import jax
import jax.numpy as jnp
from jax import lax
from jax.experimental import pallas as pl
from jax.experimental.pallas import tpu as pltpu
from jax.experimental.pallas import tpu_sc as plsc

F32 = jnp.float32
BF16 = jnp.bfloat16
HI = lax.Precision.HIGHEST
MESH = pl.DeviceIdType.MESH

D = 1024
N_META = 16
PAD = 112
HEAD0 = PAD + N_META
D_CONV = 512
DN_H = 4
DN_D = 128
DN_DIM = 512
CH = 64
IN_DIM = 3592
P_W = 3840
BG0 = 3584
SWA_H = 16
SWA_KV = 4
SWA_D = 64
BLK = 128
NKEY = N_META + 2 * BLK
D_FF = 2816
EPS = 1e-6
LR, B1, B2, AEPS, WD, STEP = 0.001, 0.9, 0.999, 1e-08, 0.01, 10
VMEM_LIMIT = 48 * 1024 * 1024
MM_VMEM_BUDGET = 34 * 1024 * 1024
R_BIG = 6144
R_HALF = R_BIG // 2
SV_ROWS = 48
SW_ROWS = 16


def _pick(n, cands):
    for c in cands:
        if n % c == 0:
            return c
    return n


def _params(sem=None):
    return pltpu.CompilerParams(dimension_semantics=sem, vmem_limit_bytes=VMEM_LIMIT)


def _dot(a, b, ca=1, cb=0, prec=None):
    return lax.dot_general(a, b, (((ca,), (cb,)), ((), ())), precision=prec,
                           preferred_element_type=F32)


def _sigmoid(x):
    return 1.0 / (1.0 + jnp.exp(-x))


def _silu(x):
    return x * _sigmoid(x)


def _dsilu(x):
    s = _sigmoid(x)
    return s * (1.0 + x * (1.0 - s))


def _softplus(x):
    return jnp.maximum(x, 0.0) + jnp.log(1.0 + jnp.exp(-jnp.abs(x)))


def mm(a, b, *, name, ta=False, tb=False, out_dtype=F32, add=None, tm=None, tn=None, tk=None,
       b_chip=False, out_chip=False, swap_mid=False, epi=None, epi_ins=(), epi_consts=(), epi_outs=(), epi_accs=()):
    if epi is not None:
        return _mm_epi(a, b, name=name, tb=tb, tn=tn, b_chip=b_chip, swap_mid=swap_mid, epi=epi, epi_ins=epi_ins,
                       epi_consts=epi_consts, epi_outs=epi_outs, epi_accs=epi_accs)
    chip_of = _chip_order(swap_mid)
    m, k = (a.shape[1], a.shape[0]) if ta else a.shape
    if b_chip:
        n = b.shape[1] if tb else 4 * b.shape[2]
        if tb:
            tk = b.shape[2]
        else:
            tn = b.shape[2]
    else:
        n = b.shape[0] if tb else b.shape[1]
    if out_chip:
        tn = n // 4
    tn = tn or _pick(n, (1408, 1024, 768, 512, 256, 128))
    tk = tk or (_pick(k, (1408, 704, 384, 128)) if ta else _pick(k, (1024, 1408, 768, 512, 128)))
    nk = k // tk
    if tm is None:
        isz = lambda t: jnp.dtype(t.dtype).itemsize
        osz = jnp.dtype(out_dtype).itemsize
        for tm in ((1408, 1024, 512, 384, 256, 128) if ta else (1408, 704, 512, 384, 256, 128)):
            need = 2 * (tm * tk * isz(a) + tk * tn * isz(b) + tm * tn * osz + (tm * tn * 4 if add is not None else 0))
            need += tm * tn * 4 if nk > 1 else 0
            if m % tm == 0 and need <= MM_VMEM_BUDGET:
                break
        else:
            tm = m
    dims = (((0 if ta else 1,), (1 if tb else 0,)), ((), ()))

    def body(*refs):
        if add is None:
            a_ref, b_ref, o_ref, acc_ref = refs
            add_ref = None
        else:
            a_ref, b_ref, add_ref, o_ref, acc_ref = refs
        def part():
            return lax.dot_general(a_ref[...].astype(BF16), b_ref[...].astype(BF16), dims,
                                   preferred_element_type=F32)

        def finish(total):
            if add_ref is not None:
                total = total + add_ref[...]
            o_ref[...] = total.astype(out_dtype)

        if nk == 1:
            finish(part())
        else:
            kk = pl.program_id(2)

            @pl.when(kk == 0)
            def _():
                acc_ref[...] = jnp.zeros(acc_ref.shape, F32)

            acc_ref[...] += part()

            @pl.when(kk == nk - 1)
            def _():
                finish(acc_ref[...])

    a_spec = pl.BlockSpec((tk, tm), lambda i, j, kk: (kk, i)) if ta else pl.BlockSpec((tm, tk), lambda i, j, kk: (i, kk))
    if b_chip and tb:
        b_spec = pl.BlockSpec((None, tn, tk), lambda i, j, kk: (chip_of(kk), j, 0))
    elif b_chip:
        b_spec = pl.BlockSpec((None, tk, tn), lambda i, j, kk: (j, kk, 0))
    elif tb:
        b_spec = pl.BlockSpec((tn, tk), lambda i, j, kk: (j, kk))
    else:
        b_spec = pl.BlockSpec((tk, tn), lambda i, j, kk: (kk, j))
    o_spec = pl.BlockSpec((tm, tn), lambda i, j, kk: (i, j))
    in_specs = [a_spec, b_spec] + ([o_spec] if add is not None else [])
    args = [a, b] + ([add] if add is not None else [])
    out_spec = pl.BlockSpec((None, tm, tn), lambda i, j, kk: (chip_of(j), i, 0)) if out_chip else o_spec
    return pl.pallas_call(
        body, name=name, interpret=False,
        out_shape=jax.ShapeDtypeStruct((4, m, tn) if out_chip else (m, n), out_dtype),
        grid=(m // tm, n // tn, nk), in_specs=in_specs, out_specs=out_spec,
        scratch_shapes=[pltpu.VMEM((tm, tn) if nk > 1 else (8, 128), F32)],
        compiler_params=_params(("parallel", "parallel", "arbitrary")),
    )(*args)


def _chip_order(swap_mid):
    return (lambda k: (k % 2) * 2 + k // 2) if swap_mid else (lambda k: k)


def _mm_epi(a, b, *, name, tb, tn, b_chip, epi, epi_ins, epi_consts, epi_outs, epi_accs, swap_mid=False):
    chip_of = _chip_order(swap_mid)
    m, k = a.shape
    if b_chip:
        n = b.shape[1] if tb else 4 * b.shape[2]
        tk = b.shape[2] if tb else None
        tn = tn if tb else b.shape[2]
    else:
        n = b.shape[0] if tb else b.shape[1]
        tk = None
    tn = tn or _pick(n, (1408, 1024, 768, 512, 256, 128))
    tk = tk or _pick(k, (1024, 1408, 1280, 768, 512, 128))
    nk, nj = k // tk, n // tn
    isz = lambda t: jnp.dtype(t.dtype if hasattr(t, "dtype") else t).itemsize
    outs3 = [t if isinstance(t, tuple) else (t, n, lambda j: j) for t in epi_outs]
    side = sum(isz(t) for t, _ in epi_ins) + sum(isz(dt) for dt, _, _ in outs3)
    for tm in (1408, 704, 512, 384, 256, 128):
        need = 2 * (tm * tk * isz(a) + tk * tn * isz(b) + tm * tn * side) + (tm * tn * 4 if nk > 1 else 0)
        if m % tm == 0 and need <= MM_VMEM_BUDGET:
            break
    else:
        tm = m
    dims = (((1,), (1 if tb else 0,)), ((), ()))
    n_in, n_c, n_out, n_acc = len(epi_ins), len(epi_consts), len(epi_outs), len(epi_accs)

    def body(*refs):
        a_ref, b_ref = refs[:2]
        in_refs = refs[2:2 + n_in + n_c]
        out_refs = refs[2 + n_in + n_c:2 + n_in + n_c + n_out]
        acc_out = refs[2 + n_in + n_c + n_out:2 + n_in + n_c + n_out + n_acc]
        acc_ref = refs[-1]
        i, j, kk = pl.program_id(0), pl.program_id(1), pl.program_id(2)
        def part():
            return lax.dot_general(a_ref[...].astype(BF16), b_ref[...].astype(BF16), dims,
                                   preferred_element_type=F32)

        def finish(total):
            res = epi(i * tm, total, *[r[...] for r in in_refs])
            if not isinstance(res, (tuple, list)):
                res = (res,)
            for r, v in zip(out_refs, res[:n_out]):
                r[...] = v.astype(r.dtype)
            if n_acc:
                @pl.when(jnp.logical_and(i == 0, j == 0))
                def _():
                    for r in acc_out:
                        r[...] = jnp.zeros(r.shape, r.dtype)

                for r, v in zip(acc_out, res[n_out:]):
                    r[...] += jnp.broadcast_to(v, r.shape).astype(r.dtype)

        if nk == 1:
            finish(part())
        else:
            @pl.when(kk == 0)
            def _():
                acc_ref[...] = jnp.zeros(acc_ref.shape, F32)

            acc_ref[...] += part()

            @pl.when(kk == nk - 1)
            def _():
                finish(acc_ref[...])

    a_spec = pl.BlockSpec((tm, tk), lambda i, j, kk: (i, kk))
    if b_chip and tb:
        b_spec = pl.BlockSpec((None, tn, tk), lambda i, j, kk: (chip_of(kk), j, 0))
    elif b_chip:
        b_spec = pl.BlockSpec((None, tk, tn), lambda i, j, kk: (j, kk, 0))
    elif tb:
        b_spec = pl.BlockSpec((tn, tk), lambda i, j, kk: (j, kk))
    else:
        b_spec = pl.BlockSpec((tk, tn), lambda i, j, kk: (kk, j))
    in_specs = [a_spec, b_spec]

    def in_spec(t, col):
        front = m - t.shape[0]
        if not front:
            return pl.BlockSpec((tm, tn), lambda i, j, kk: (i, col(j)))
        return pl.BlockSpec((pl.Element(tm), pl.Element(tn)),
                            lambda i, j, kk: (pl.multiple_of(jnp.maximum(i * tm - front, 0), 8), col(j) * tn))

    in_specs += [in_spec(t, col) for t, col in epi_ins]
    in_specs += [pl.BlockSpec(t.shape, lambda i, j, kk, nd=t.ndim: (0,) * nd) for t in epi_consts]
    out_specs = [pl.BlockSpec((tm, tn), lambda i, j, kk, col=col: (i, col(j))) for _, _, col in outs3]
    out_specs += [pl.BlockSpec(s, lambda i, j, kk, nd=len(s): (0,) * nd) for s, _ in epi_accs]
    out_shape = [jax.ShapeDtypeStruct((m, width), dt) for dt, width, _ in outs3]
    out_shape += [jax.ShapeDtypeStruct(s, dt) for s, dt in epi_accs]
    sem = ("arbitrary", "arbitrary", "arbitrary") if n_acc else ("parallel", "parallel", "arbitrary")
    return pl.pallas_call(
        body, name=name, interpret=False, out_shape=out_shape,
        grid=(m // tm, nj, nk), in_specs=in_specs, out_specs=out_specs,
        scratch_shapes=[pltpu.VMEM((tm, tn) if nk > 1 else (8, 128), F32)],
        compiler_params=_params(sem),
    )(a, b, *[t for t, _ in epi_ins], *epi_consts)


def cols(arr, tr, width=None, cb=0):
    width = width or arr.shape[1]
    return (arr, (tr, width), lambda i: (i, cb), "r2")


def heads(arr, tr):
    return (arr, (arr.shape[0], tr, arr.shape[2]), lambda i: (0, i, 0), "r3")


def whole(arr):
    nd = arr.ndim
    return (arr, arr.shape, lambda i: (0,) * nd, "w")


STRIP = 16


def _rows_of(ref, kind, r0, n):
    if kind == "r2":
        return ref[pl.ds(r0, n), :]
    if kind == "r3":
        return ref[:, pl.ds(r0, n), :]
    return ref[...]


def _set_rows(ref, kind, r0, n, v):
    if kind == "r2":
        ref[pl.ds(r0, n), :] = v.astype(ref.dtype)
    elif kind == "r3":
        ref[:, pl.ds(r0, n), :] = v.astype(ref.dtype)
    else:
        ref[...] = v.astype(ref.dtype)


def rowwise(fn, ins, outs, *, steps, name, accs=(), strip=None):
    n_in, n_out, n_acc = len(ins), len(outs), len(accs)
    kin = [t[3] for t in ins]
    kout = [t[4] for t in outs]
    tr = next((t[1][-2] for t in ins if t[3] != "w"), 0)

    def body(*refs):
        i = pl.program_id(0)
        in_refs, out_refs, acc_refs = refs[:n_in], refs[n_in:n_in + n_out], refs[n_in + n_out:]
        if n_acc:
            @pl.when(i == 0)
            def _():
                for r in acc_refs:
                    r[...] = jnp.zeros(r.shape, r.dtype)

        def run(r0, n):
            res = fn(i * tr + r0, *[_rows_of(r, k, r0, n) for r, k in zip(in_refs, kin)])
            if not isinstance(res, (tuple, list)):
                res = (res,)
            for r, k, v in zip(out_refs, kout, res[:n_out]):
                _set_rows(r, k, r0, n, v)
            for r, v in zip(acc_refs, res[n_out:]):
                r[...] += jnp.broadcast_to(v, r.shape).astype(r.dtype)

        if strip is None or tr <= strip:
            run(0, tr)
        else:
            def step(s, carry):
                run(pl.multiple_of(s * strip, strip), strip)
                return carry
            lax.fori_loop(0, tr // strip, step, 0)

    def zmap(nd):
        return lambda i: (0,) * nd

    in_specs = [pl.BlockSpec(t[1], t[2]) for t in ins]
    out_specs = [pl.BlockSpec(t[2], t[3]) for t in outs]
    out_specs += [pl.BlockSpec(s, zmap(len(s))) for s, _ in accs]
    out_shape = [jax.ShapeDtypeStruct(t[0], t[1]) for t in outs]
    out_shape += [jax.ShapeDtypeStruct(s, d) for s, d in accs]
    res = pl.pallas_call(
        body, name=name, interpret=False, out_shape=out_shape, grid=(steps,),
        in_specs=in_specs, out_specs=out_specs,
        compiler_params=_params(("arbitrary",)),
    )(*[t[0] for t in ins])
    return res


def out2d(rows, width, dtype, tr):
    return ((rows, width), dtype, (tr, width), lambda i: (i, 0), "r2")


def conv_fwd(xs, w8, kw, *, rows, c, tc, tr, name, post, extras=(), outs=(), pre=None, strip=STRIP):
    nx, ne, no = len(xs), len(extras), len(outs)
    nr, nc = rows // tr, c // tc
    r8 = tr // 8
    st = strip

    def body(*refs):
        x_refs = refs[:2 * nx]
        w_ref = refs[2 * nx]
        e_refs = refs[2 * nx + 1:2 * nx + 1 + ne]
        o_refs = refs[2 * nx + 1 + ne:2 * nx + 1 + ne + no]
        scr = refs[-1]
        j, i = pl.program_id(0), pl.program_id(1)
        halo = [x_refs[2 * q + 1][...].astype(F32) for q in range(nx)]
        scr[0:8, :] = jnp.where(i > 0, pre(*halo) if pre else halo[0], 0.0)

        def fill(s, carry):
            r0 = pl.multiple_of(s * st, st)
            cur = [x_refs[2 * q][pl.ds(r0, st), :].astype(F32) for q in range(nx)]
            scr[pl.ds(8 + r0, st), :] = pre(*cur) if pre else cur[0]
            return carry

        def comp(s, carry):
            r0 = pl.multiple_of(s * st, st)
            win = scr[pl.ds(r0, st + 8), :]
            y = jnp.zeros((st, tc), F32)
            for q in range(kw):
                sh = kw - 1 - q
                y = y + w_ref[q:q + 1, :] * win[8 - sh:8 - sh + st]
            res = post(j, y, *[e[pl.ds(r0, st), :] for e in e_refs])
            if not isinstance(res, (tuple, list)):
                res = (res,)
            for r, v in zip(o_refs, res):
                r[pl.ds(r0, st), :] = v.astype(r.dtype)
            return carry

        lax.fori_loop(0, tr // st, fill, 0)
        lax.fori_loop(0, tr // st, comp, 0)

    in_specs, args = [], []
    for arr, cb0 in xs:
        in_specs.append(pl.BlockSpec((tr, tc), lambda j, i, cb0=cb0: (i, cb0 + j)))
        in_specs.append(pl.BlockSpec((8, tc), lambda j, i, cb0=cb0: (jnp.maximum(i * r8 - 1, 0), cb0 + j)))
        args += [arr, arr]
    in_specs.append(pl.BlockSpec((8, tc), lambda j, i: (0, j)))
    args.append(w8)
    for arr, cb0 in extras:
        in_specs.append(pl.BlockSpec((tr, tc), lambda j, i, cb0=cb0: (i, cb0 + j)))
        args.append(arr)
    return pl.pallas_call(
        body, name=name, interpret=False,
        out_shape=[jax.ShapeDtypeStruct((rows, c), dt) for dt in outs],
        grid=(nc, nr), in_specs=in_specs,
        out_specs=[pl.BlockSpec((tr, tc), lambda j, i: (i, j)) for _ in outs],
        scratch_shapes=[pltpu.VMEM((tr + 8, tc), F32)],
        compiler_params=_params(("parallel", "arbitrary")),
    )(*args)


def conv_bwd(xs, w8, kw, dy, *, rows, c, tc, tr, name, post, extras=(), outs=(), pre=None):
    nx, ne, no = len(xs), len(extras), len(outs)
    nr, nc = rows // tr, c // tc
    r8 = tr // 8

    def body(*refs):
        x_refs = refs[:nx]
        w_ref, dy_ref, dyn_ref = refs[nx:nx + 3]
        e_refs = refs[nx + 3:nx + 3 + ne]
        first_out = nx + 3 + ne
        o_refs = refs[first_out:first_out + no]
        dw_ref = refs[first_out + no]
        gscr = refs[-1]
        i = pl.program_id(1)
        gscr[tr:tr + 8, :] = jnp.where(i < nr - 1, dyn_ref[...].astype(F32), 0.0)

        def fill(s, carry):
            r0 = pl.multiple_of(s * STRIP, STRIP)
            gscr[pl.ds(r0, STRIP), :] = dy_ref[pl.ds(r0, STRIP), :].astype(F32)
            return carry

        def comp(s, dws):
            r0 = pl.multiple_of(s * STRIP, STRIP)
            gwin = gscr[pl.ds(r0, STRIP + 8), :]
            cur = [x_refs[q][pl.ds(r0, STRIP), :].astype(F32) for q in range(nx)]
            x = pre(*cur) if pre else cur[0]
            dx = jnp.zeros((STRIP, tc), F32)
            new = []
            for q in range(kw):
                sh = kw - 1 - q
                ahead = gwin[sh:sh + STRIP]
                dx = dx + w_ref[q:q + 1, :] * ahead
                part = ahead * x
                new.append(dws[q] + part[0:8] + part[8:16])
            res = post(dx, *[e[pl.ds(r0, STRIP), :] for e in e_refs])
            if not isinstance(res, (tuple, list)):
                res = (res,)
            for r, v in zip(o_refs, res):
                r[pl.ds(r0, STRIP), :] = v.astype(r.dtype)
            return tuple(new)

        lax.fori_loop(0, tr // STRIP, fill, 0)
        dws = lax.fori_loop(0, tr // STRIP, comp, tuple(jnp.zeros((8, tc), F32) for _ in range(kw)))

        @pl.when(i == 0)
        def _():
            dw_ref[...] = jnp.zeros((8, tc), F32)

        dw_ref[...] += jnp.concatenate([jnp.sum(t, axis=0, keepdims=True) for t in dws]
                                       + [jnp.zeros((8 - kw, tc), F32)], axis=0)

    in_specs, args = [], []
    for arr, cb0 in xs:
        in_specs.append(pl.BlockSpec((tr, tc), lambda j, i, cb0=cb0: (i, cb0 + j)))
        args.append(arr)
    in_specs.append(pl.BlockSpec((8, tc), lambda j, i: (0, j)))
    in_specs.append(pl.BlockSpec((tr, tc), lambda j, i: (i, j)))
    in_specs.append(pl.BlockSpec((8, tc), lambda j, i: (jnp.minimum((i + 1) * r8, nr * r8 - 1), j)))
    args += [w8, dy, dy]
    for arr, cb0 in extras:
        in_specs.append(pl.BlockSpec((tr, tc), lambda j, i, cb0=cb0: (i, cb0 + j)))
        args.append(arr)
    return pl.pallas_call(
        body, name=name, interpret=False,
        out_shape=[jax.ShapeDtypeStruct((rows, c), dt) for dt in outs] + [jax.ShapeDtypeStruct((8, c), F32)],
        grid=(nc, nr), in_specs=in_specs,
        out_specs=[pl.BlockSpec((tr, tc), lambda j, i: (i, j)) for _ in outs] + [pl.BlockSpec((8, tc), lambda j, i: (0, j))],
        scratch_shapes=[pltpu.VMEM((tr + 8, tc), F32)],
        compiler_params=_params(("parallel", "arbitrary")),
    )(*args)


def rms_fwd(h, w, *, name):
    rows = h.shape[0]
    tr = _pick(rows, (384, 128))

    def fn(i, x, wv):
        r = lax.rsqrt(jnp.mean(x * x, axis=1, keepdims=True) + EPS)
        return x * r * wv

    return rowwise(fn, [cols(h, tr), whole(w)], [out2d(rows, D, BF16, tr)], steps=rows // tr, name=name)[0]


def _rms_bwd_epi(row0, g, x, dr, wv):
    r = lax.rsqrt(jnp.mean(x * x, axis=1, keepdims=True) + EPS)
    xh = x * r
    gw = g * wv
    dx = r * (gw - xh * jnp.mean(gw * xh, axis=1, keepdims=True))
    row = row0 + lax.broadcasted_iota(jnp.int32, (x.shape[0], 1), 0)
    return jnp.where(row >= PAD, dr + dx, 0.0), jnp.sum(g * xh, axis=0, keepdims=True)


def dx_rms_bwd(dy, w, h, nw, dres, *, name, b_chip=False, swap_mid=False):
    return mm(dy, w, tb=True, b_chip=b_chip, swap_mid=swap_mid, tn=D, name=name, epi=_rms_bwd_epi,
              epi_ins=[(h, lambda j: 0), (dres, lambda j: 0)], epi_consts=[nw], epi_outs=[F32],
              epi_accs=[((1, D), F32)])


def _add_loss_epi(row0, t, h, tgt):
    row = row0 + lax.broadcasted_iota(jnp.int32, (t.shape[0], 1), 0)
    tgt = jnp.where(row0 == 0, jnp.concatenate([tgt[-HEAD0:], tgt[:-HEAD0]], axis=0), tgt)
    diff = jnp.where(row >= HEAD0, t + h - tgt, 0.0)
    part = jnp.sum(jnp.sum(diff * diff, axis=1, keepdims=True), axis=0, keepdims=True)
    return diff * (1.0 / D), part * (0.5 / D)


def add_loss(a, w, h, target, *, name):
    return mm(a, w, name=name, epi=_add_loss_epi, epi_ins=[(h, lambda j: 0), (target, lambda j: 0)],
              epi_outs=[F32], epi_accs=[((1, 128), F32)])


def adamw(w, g, m, v, *, name):
    shape = w.shape
    gs = list(g) if isinstance(g, (list, tuple)) else [g]
    nl = len(gs)
    width = shape[-1]
    rows = w.size // width
    rl = rows // nl
    tr = _pick(rl, (256, 176, 128, 64, 16, 8))
    nr = rl // tr
    if w.ndim == 3 and shape[1] % tr == 0:
        per = shape[1] // tr
        view = lambda t: (t, (None, tr, width), lambda i: (i // per, i % per, 0), "r2")
        out = (shape, F32, (None, tr, width), lambda i: (i // per, i % per, 0), "r2")
    else:
        view = lambda t: cols(t.reshape(rows, width), tr)
        out = out2d(rows, width, F32, tr)

    def fn(i, wv, mv, vv, *gvs):
        gv = gvs[0]
        for layer in range(1, nl):
            gv = jnp.where(i >= layer * rl, gvs[layer], gv)
        mn = B1 * mv + (1.0 - B1) * gv
        vn = B2 * vv + (1.0 - B2) * gv * gv
        mh = mn / (1.0 - B1 ** STEP)
        vh = vn / (1.0 - B2 ** STEP)
        return -LR * (mh / (jnp.sqrt(vh) + AEPS) + WD * wv), mn, vn, gv

    g_ins = [(t.reshape(rl, width), (tr, width), lambda i, layer=layer: (jnp.clip(i - layer * nr, 0, nr - 1), 0), "r2")
             for layer, t in enumerate(gs)]
    res = rowwise(fn, [view(t) for t in (w, m, v)] + g_ins, [out] * 4, steps=rows // tr, name=name)
    return [r.reshape(shape) for r in res]


HB = DN_H * CH
PAIR = 3


def _split(a):
    hi = a.astype(BF16)
    return hi, (a - hi.astype(F32)).astype(BF16)


def _dot1(a, b, ca=1, cb=0):
    return _dot(a.astype(BF16), b.astype(BF16), ca, cb)


def _dot3(a, b, ca=1, cb=0):
    ah, al = _split(a)
    bh, bl = _split(b)
    return _dot(ah, bh, ca, cb) + (_dot(ah, bl, ca, cb) + _dot(al, bh, ca, cb))


def _dot01(m01, b, ca=1, cb=0):
    bh, bl = _split(b)
    m = m01.astype(BF16)
    return _dot(m, bh, ca, cb) + _dot(m, bl, ca, cb)


def _stack(x):
    return jnp.concatenate([x[:, h * DN_D:(h + 1) * DN_D] for h in range(DN_H)], axis=0)


def _unstack(x):
    return jnp.concatenate([x[h * CH:(h + 1) * CH] for h in range(DN_H)], axis=1)


def _tri_inv(mats, blk, eye):
    each = lambda f, *lists: [f(*t) for t in zip(*lists)]
    ad = [jnp.where(blk, a, 0.0) for a in mats]
    lo = each(lambda a, d: a - d, mats, ad)
    a2 = each(_dot3, ad, ad)
    a4 = each(_dot3, a2, a2)
    a8 = each(_dot3, a4, a4)
    dgi = each(lambda d, s: _dot3(eye - d, eye + s), ad, a2)
    dgi = each(lambda p, s: _dot3(p, eye + s), dgi, a4)
    dgi = each(lambda p, s: _dot3(p, eye + s), dgi, a8)
    n = each(_dot3, dgi, lo)
    n2 = each(_dot3, n, n)
    return each(_dot3, each(lambda u, v: _dot3(eye - u, eye + v), n, n2), dgi)


def _dn_masks():
    row = lax.broadcasted_iota(jnp.int32, (HB, HB), 0)
    col = lax.broadcasted_iota(jnp.int32, (HB, HB), 1)
    same = (row // CH) == (col // CH)
    incl = jnp.logical_and(same, row >= col)
    strict = jnp.logical_and(same, row > col)
    upper = jnp.logical_and(same, row <= col)
    blk = (row // 16) == (col // 16)
    eye = (row == col).astype(F32)
    return incl, strict, upper, blk, eye


def _dn_chunk(qv, kv, vv, bc, br, incl, strict):
    r64 = lax.broadcasted_iota(jnp.int32, (CH, CH), 0)
    c64 = lax.broadcasted_iota(jnp.int32, (CH, CH), 1)
    dcol = _dot01((r64 >= c64).astype(F32), bc)
    drow = _dot3(br, (r64 <= c64).astype(F32))
    col = lambda m, l0: jnp.concatenate([m[:, l0 + h:l0 + h + 1] for h in range(DN_H)], axis=0)
    b_c = col(bc, 0)
    d_c = col(dcol, 4)
    d_r = jnp.concatenate([drow[4 + h:5 + h, :] for h in range(DN_H)], axis=1)
    d_last_h = [dcol[CH - 1:CH, 4 + h:5 + h] for h in range(DN_H)]
    d_last = jnp.concatenate([jnp.broadcast_to(t, (CH, 1)) for t in d_last_h], axis=0)
    q, k, v = _stack(qv), _stack(kv), _stack(vv)
    dm = jnp.where(incl, jnp.exp(jnp.where(incl, d_c - d_r, 0.0)), 0.0)
    kk = _dot1(k, k, 1, 1)
    a = jnp.where(strict, b_c * kk * dm, 0.0)
    ed = jnp.exp(d_c)
    rhs = jnp.concatenate([v * b_c, k * (b_c * ed)], axis=1)
    qk = _dot1(q, k, 1, 1) * dm
    ekd = jnp.exp(d_last - d_c)
    gl = [jnp.exp(t) for t in d_last_h]
    return q, k, v, b_c, dm, kk, a, ed, rhs, qk, ekd, gl


def dn_fwd(qkv_n, bgcol, bgrow):
    rows = qkv_n.shape[0]
    nch = rows // CH

    def body(q_ref, k_ref, v_ref, bc_ref, br_ref, o_ref, s_out, ti_out, s_scr, prep, prep_qk, prep_gl):
        n = pl.program_id(0)

        @pl.when(n == 0)
        def _():
            s_scr[...] = jnp.zeros(s_scr.shape, F32)
            prep[...] = jnp.zeros(prep.shape, F32)
            prep_qk[...] = jnp.zeros(prep_qk.shape, F32)
            prep_gl[...] = jnp.zeros(prep_gl.shape, F32)

        live = n > 0
        rows_of = [slice(h * CH, (h + 1) * CH) for h in range(DN_H)]
        s = [s_scr[h] for h in range(DN_H)]
        for c in range(PAIR):
            u, w, qd, kd = prep[c, 0], prep[c, 1], prep[c, 2], prep[c, 3]
            for h in range(DN_H):
                s_out[c, h] = s[h]
            v_new = [u[rs] - _dot1(w[rs], s[h]) for h, rs in enumerate(rows_of)]
            o_state = [_dot1(qd[rs], s[h]) for h, rs in enumerate(rows_of)]
            s = [jnp.where(live, prep_gl[c, h:h + 1, 0:1] * s[h] + _dot1(kd[rs], v_new[h], 0, 0), s[h])
                 for h, rs in enumerate(rows_of)]
            o = jnp.concatenate(o_state, axis=0) + _dot1(prep_qk[c], jnp.concatenate(v_new, axis=0))
            o_ref[c * CH:(c + 1) * CH, :] = _unstack(o)
        for h in range(DN_H):
            s_scr[h] = s[h]

        incl, strict, _, blk, eye = _dn_masks()
        parts = []
        for c in range(PAIR):
            rows_c = slice(c * CH, (c + 1) * CH)
            parts.append(_dn_chunk(q_ref[rows_c, :], k_ref[rows_c, :], v_ref[rows_c, :], bc_ref[rows_c, :],
                                   br_ref[c], incl, strict))
        tinvs = _tri_inv([p[6] for p in parts], blk, eye)
        for c, (q, k, v, b_c, dm, kk, a, ed, rhs, qk_n, ekd, gl) in enumerate(parts):
            tinv = tinvs[c]
            ti_out[c] = tinv
            sol = _dot3(tinv, rhs)
            prep[c, 0] = sol[:, :DN_D]
            prep[c, 1] = sol[:, DN_D:]
            prep[c, 2] = q * ed
            prep[c, 3] = k * ekd
            prep_qk[c] = qk_n
            prep_gl[c] = jnp.concatenate([jnp.broadcast_to(t, (1, 128)) for t in gl]
                                         + [jnp.zeros((8 - DN_H, 128), F32)], axis=0)

    assert nch % PAIR == 0
    npair = nch // PAIR
    last = npair - 1
    return pl.pallas_call(
        body, name="dn_fwd", interpret=False,
        out_shape=[jax.ShapeDtypeStruct((rows, DN_DIM), F32),
                   jax.ShapeDtypeStruct((nch, DN_H, DN_D, DN_D), F32),
                   jax.ShapeDtypeStruct((nch, HB, HB), F32)],
        grid=(npair + 1,),
        in_specs=[pl.BlockSpec((PAIR * CH, DN_DIM), lambda n: (jnp.minimum(n, last), 0)),
                  pl.BlockSpec((PAIR * CH, DN_DIM), lambda n: (jnp.minimum(n, last), 1)),
                  pl.BlockSpec((PAIR * CH, DN_DIM), lambda n: (jnp.minimum(n, last), 2)),
                  pl.BlockSpec((PAIR * CH, 128), lambda n: (jnp.minimum(n, last), 0)),
                  pl.BlockSpec((PAIR, 8, CH), lambda n: (jnp.minimum(n, last), 0, 0))],
        out_specs=[pl.BlockSpec((PAIR * CH, DN_DIM), lambda n: (jnp.maximum(n - 1, 0), 0)),
                   pl.BlockSpec((PAIR, DN_H, DN_D, DN_D), lambda n: (jnp.maximum(n - 1, 0), 0, 0, 0)),
                   pl.BlockSpec((PAIR, HB, HB), lambda n: (jnp.minimum(n, last), 0, 0))],
        scratch_shapes=[pltpu.VMEM((DN_H, DN_D, DN_D), F32), pltpu.VMEM((PAIR, 4, HB, DN_D), F32),
                        pltpu.VMEM((PAIR, HB, HB), F32), pltpu.VMEM((PAIR, 8, 128), F32)],
        compiler_params=_params(("arbitrary",)),
    )(qkv_n, qkv_n, qkv_n, bgcol, bgrow)


def dn_bwd(qkv_n, bgcol, bgrow, s_all, ti_all, do):
    rows = qkv_n.shape[0]
    nch = rows // CH

    def body(q_ref, k_ref, v_ref, bc_ref, br_ref, s_ref, ti_ref, do_ref, dq_ref, dk_ref, dv_ref, dbg_ref, ds_scr):
        n = pl.program_id(0)

        @pl.when(n == 0)
        def _():
            ds_scr[...] = jnp.zeros(ds_scr.shape, F32)

        incl, strict, upper, _, _ = _dn_masks()
        rsum = lambda t: jnp.sum(t, axis=1, keepdims=True)
        rows_of = [slice(h * CH, (h + 1) * CH) for h in range(DN_H)]
        heads_of = lambda f: jnp.concatenate([f(h, rs) for h, rs in enumerate(rows_of)], axis=0)
        cs = []
        for c in reversed(range(PAIR)):
            rc = slice(c * CH, (c + 1) * CH)
            q, k, v, b_c, dm, kk, a, ed, rhs, qk, ekd, gl = _dn_chunk(
                q_ref[rc, :], k_ref[rc, :], v_ref[rc, :], bc_ref[rc, :], br_ref[c], incl, strict)
            cs.append(dict(rc=rc, q=q, k=k, v=v, b_c=b_c, dm=dm, kk=kk, a=a, ed=ed, rhs=rhs, qk=qk, ekd=ekd, gl=gl,
                           tinv=ti_ref[c], g=_stack(do_ref[rc, :]), s=[s_ref[c, h] for h in range(DN_H)]))
        for t in cs:
            t["sol"] = _dot3(t["tinv"], t["rhs"])
        for t in cs:
            t["u"], t["w"] = t["sol"][:, :DN_D], t["sol"][:, DN_D:]
            t["qd"], t["kd"] = t["q"] * t["ed"], t["k"] * t["ekd"]
            t["v_new"] = heads_of(lambda h, rs: t["u"][rs] - _dot1(t["w"][rs], t["s"][h]))
            t["dv0"] = _dot1(t["qk"], t["g"], 0, 0)
            t["ds0"] = [_dot1(t["qd"][rs], t["g"][rs], 0, 0) for rs in rows_of]
            t["dqd"] = heads_of(lambda h, rs: _dot1(t["g"][rs], t["s"][h], 1, 1))
        for t in cs:
            t["dqk"] = _dot1(t["g"], t["v_new"], 1, 1)
        ds = [ds_scr[h] for h in range(DN_H)]
        for t in cs:
            t["ds"] = ds
            t["dv_new"] = t["dv0"] + heads_of(lambda h, rs: _dot1(t["kd"][rs], ds[h]))
            ds = [t["ds0"][h] + t["gl"][h] * ds[h] - _dot1(t["w"][rs], t["dv_new"][rs], 0, 0)
                  for h, rs in enumerate(rows_of)]
        for h in range(DN_H):
            ds_scr[h] = ds[h]
        for t in cs:
            t["dkd"] = heads_of(lambda h, rs: _dot1(t["v_new"][rs], t["ds"][h], 1, 1))
            dw = heads_of(lambda h, rs: -_dot1(t["dv_new"][rs], t["s"][h], 1, 1))
            t["dsol"] = jnp.concatenate([t["dv_new"], dw], axis=1)
        for t in cs:
            t["drhs"] = _dot3(t["tinv"], t["dsol"], 0, 0)
        for t in cs:
            t["da"] = jnp.where(strict, -_dot1(t["drhs"], t["sol"], 1, 1), 0.0)
        rowi = lax.broadcasted_iota(jnp.int32, (CH, 1), 0)
        lane = lax.broadcasted_iota(jnp.int32, (CH, 128), 1)
        for t in cs:
            q, k, v, b_c, dm, ed, da, dqk = t["q"], t["k"], t["v"], t["b_c"], t["dm"], t["ed"], t["da"], t["dqk"]
            drhs_u, drhs_w = t["drhs"][:, :DN_D], t["drhs"][:, DN_D:]
            s2 = rsum(drhs_w * k)
            dbeta = rsum(drhs_u * v) + s2 * ed + rsum(da * t["kk"] * dm)
            dkk = da * b_c * dm
            dqkr = dqk * dm
            mmat = da * t["a"] + dqk * t["qk"]
            tmp = rsum(t["dkd"] * t["kd"])
            dd = (s2 * b_c * ed + rsum(mmat) - _dot3(mmat, jnp.ones((HB, 128), F32), 0, 0)[:, :1]
                  + rsum(t["dqd"] * t["qd"]) - tmp)
            last = []
            for h, rs in enumerate(rows_of):
                dgl = jnp.sum(rsum(t["s"][h] * t["ds"][h]), axis=0, keepdims=True)
                dd_last = jnp.sum(tmp[rs], axis=0, keepdims=True) + dgl * t["gl"][h]
                last.append(jnp.where(rowi == CH - 1, dd_last, 0.0))
            dd = dd + jnp.concatenate(last, axis=0)
            rc = t["rc"]
            dq_ref[rc, :] = _unstack(_dot1(dqkr, k) + t["dqd"] * ed)
            dk_ref[rc, :] = _unstack(drhs_w * (b_c * ed) + _dot1(dkk, k) + _dot1(dkk, k, 0, 0) + _dot1(dqkr, q, 0, 0)
                                     + t["dkd"] * t["ekd"])
            dv_ref[rc, :] = _unstack(drhs_u * b_c)
            dg = _dot01(upper.astype(F32), jnp.broadcast_to(dd, (HB, 128)))[:, :1]
            out = jnp.zeros((CH, 128), F32)
            for h, rs in enumerate(rows_of):
                out = out + jnp.where(lane == h, dbeta[rs], 0.0) + jnp.where(lane == 4 + h, dg[rs], 0.0)
            dbg_ref[rc, :] = out

    assert nch % PAIR == 0
    npair = nch // PAIR
    rev = lambda n: npair - 1 - n
    blk = PAIR * CH
    return pl.pallas_call(
        body, name="dn_bwd", interpret=False,
        out_shape=[jax.ShapeDtypeStruct((rows, DN_DIM), F32)] * 3 + [jax.ShapeDtypeStruct((rows, 128), F32)],
        grid=(npair,),
        in_specs=[pl.BlockSpec((blk, DN_DIM), lambda n: (rev(n), 0)),
                  pl.BlockSpec((blk, DN_DIM), lambda n: (rev(n), 1)),
                  pl.BlockSpec((blk, DN_DIM), lambda n: (rev(n), 2)),
                  pl.BlockSpec((blk, 128), lambda n: (rev(n), 0)),
                  pl.BlockSpec((PAIR, 8, CH), lambda n: (rev(n), 0, 0)),
                  pl.BlockSpec((PAIR, DN_H, DN_D, DN_D), lambda n: (rev(n), 0, 0, 0)),
                  pl.BlockSpec((PAIR, HB, HB), lambda n: (rev(n), 0, 0)),
                  pl.BlockSpec((blk, DN_DIM), lambda n: (rev(n), 0))],
        out_specs=[pl.BlockSpec((blk, DN_DIM), lambda n: (rev(n), 0))] * 3 + [pl.BlockSpec((blk, 128), lambda n: (rev(n), 0))],
        scratch_shapes=[pltpu.VMEM((DN_H, DN_D, DN_D), F32)],
        compiler_params=_params(("arbitrary",)),
    )(qkv_n, qkv_n, qkv_n, bgcol, bgrow, s_all, ti_all, do)


def _swa_valid(n):
    c3 = lax.broadcasted_iota(jnp.int32, (NKEY, 4 * BLK), 0)
    r = lax.broadcasted_iota(jnp.int32, (NKEY, 4 * BLK), 1) % BLK
    prev0 = N_META + BLK
    c = jnp.where(c3 < N_META, PAD + c3, jnp.where(c3 < prev0, c3 - N_META, c3 - prev0))
    lo = jnp.where(c3 < N_META, 0, jnp.where(c3 < prev0, r + 1 + jnp.where(n >= 2, 0, BLK), 0))
    hi = jnp.where(c3 < N_META, r + jnp.where(n >= 1, BLK, 0),
                   jnp.where(c3 < prev0, BLK, r - jnp.where(n >= 1, 0, BLK)))
    return jnp.logical_and(c >= lo, c <= hi)


def _swa_probs(qs, kcats, valid, sinks):
    s = [jnp.where(valid, _dot(kc, q, 1, 1), -1e30) for q, kc in zip(qs, kcats)]
    m = [jnp.maximum(jnp.max(t, axis=0, keepdims=True), sk) for t, sk in zip(s, sinks)]
    e = [jnp.where(valid, jnp.exp(t - mx), 0.0) for t, mx in zip(s, m)]
    es = [jnp.exp(sk - mx) for sk, mx in zip(sinks, m)]
    inv = [1.0 / (jnp.sum(t, axis=0, keepdims=True) + u) for t, u in zip(e, es)]
    return [t * i for t, i in zip(e, inv)], [u * i for u, i in zip(es, inv)]


def _swa_group(q_ref, sk_ref, h):
    q4 = jnp.concatenate([q_ref[4 * h + g] for g in range(4)], axis=0)
    sink4 = jnp.concatenate([jnp.full((1, BLK), sk_ref[4 * h + g], F32) for g in range(4)], axis=1)
    return q4, sink4


def _swa_specs():
    q = pl.BlockSpec((SWA_H, BLK, SWA_D), lambda n: (0, n, 0))
    km = pl.BlockSpec((SWA_KV, N_META, SWA_D), lambda n: (0, PAD // N_META, 0))
    kp = pl.BlockSpec((SWA_KV, BLK, SWA_D), lambda n: (0, jnp.maximum(n - 1, 0), 0))
    kc = pl.BlockSpec((SWA_KV, BLK, SWA_D), lambda n: (0, n, 0))
    return [q, km, kp, kc, km, kp, kc]


def swa_fwd(qh, kh, vh, sinks):
    rows = qh.shape[1]
    nb = rows // BLK

    def body(q_ref, km, kp, kc, vm, vp, vc, sk_ref, o_ref):
        n = pl.program_id(0)
        valid = _swa_valid(n)
        kcats = [jnp.concatenate([km[h], kp[h], kc[h]], axis=0) for h in range(SWA_KV)]
        vcats = [jnp.concatenate([vm[h], vp[h], vc[h]], axis=0) for h in range(SWA_KV)]
        qs, sinks4 = zip(*[_swa_group(q_ref, sk_ref, h) for h in range(SWA_KV)])
        ps, _ = _swa_probs(qs, kcats, valid, sinks4)
        o4s = [_dot(p.astype(BF16), vc_, 0, 0) for p, vc_ in zip(ps, vcats)]
        o_ref[...] = jnp.concatenate([o4[g * BLK:(g + 1) * BLK] for o4 in o4s for g in range(4)],
                                     axis=1).astype(BF16)

    return pl.pallas_call(
        body, name="swa_fwd", interpret=False,
        out_shape=jax.ShapeDtypeStruct((rows, SWA_H * SWA_D), BF16),
        grid=(nb,),
        in_specs=_swa_specs() + [pl.BlockSpec(memory_space=pltpu.SMEM)],
        out_specs=pl.BlockSpec((BLK, SWA_H * SWA_D), lambda n: (n, 0)),
        compiler_params=_params(("parallel",)),
    )(qh, kh, kh, kh, vh, vh, vh, sinks)


def swa_bwd(qh, kh, vh, sinks, do):
    rows = qh.shape[1]
    nb = rows // BLK

    def body(q_ref, km, kp, kc, vm, vp, vc, do_ref, sk_ref, dq_ref, dk_ref, dv_ref, dsk_ref):
        n = pl.program_id(0)

        @pl.when(n == 0)
        def _():
            dk_ref[...] = jnp.zeros(dk_ref.shape, F32)
            dv_ref[...] = jnp.zeros(dv_ref.shape, F32)

        valid = _swa_valid(n)
        g_all = do_ref[...]
        rowi = lax.broadcasted_iota(jnp.int32, (SWA_H, 128), 0)
        dsk = jnp.zeros((SWA_H, 128), F32)
        pm = pl.multiple_of(jnp.maximum(n - 1, 0) * BLK, BLK)
        pc = pl.multiple_of(n * BLK, BLK)
        hs = range(SWA_KV)
        kcats = [jnp.concatenate([km[h], kp[h], kc[h]], axis=0) for h in hs]
        vcats = [jnp.concatenate([vm[h], vp[h], vc[h]], axis=0) for h in hs]
        qs, sinks4 = zip(*[_swa_group(q_ref, sk_ref, h) for h in hs])
        g4s = [jnp.concatenate([g_all[:, (4 * h + g) * SWA_D:(4 * h + g + 1) * SWA_D] for g in range(4)], axis=0)
               for h in hs]
        ps, pss = _swa_probs(qs, kcats, valid, sinks4)
        dps = [_dot(vc_, g4, 1, 1) for vc_, g4 in zip(vcats, g4s)]
        deltas = [jnp.sum(p * dp, axis=0, keepdims=True) for p, dp in zip(ps, dps)]
        dss = [(p * (dp - dl)).astype(BF16) for p, dp, dl in zip(ps, dps, deltas)]
        dq4s = [_dot(ds, kc_, 0, 0) for ds, kc_ in zip(dss, kcats)]
        dkcs = [_dot(ds, q4) for ds, q4 in zip(dss, qs)]
        dvcs = [_dot(p.astype(BF16), g4) for p, g4 in zip(ps, g4s)]
        for h in hs:
            t = pss[h] * deltas[h]
            for g in range(4):
                dq_ref[4 * h + g] = dq4s[h][g * BLK:(g + 1) * BLK]
                part = -jnp.sum(t[:, g * BLK:(g + 1) * BLK], axis=1, keepdims=True)
                dsk = dsk + jnp.where(rowi == 4 * h + g, part, 0.0)
            lanes = slice(h * SWA_D, (h + 1) * SWA_D)
            for ref, val in ((dk_ref, dkcs[h]), (dv_ref, dvcs[h])):
                ref[PAD:BLK, lanes] += val[0:N_META]
                ref[pl.ds(pm, BLK), lanes] += val[N_META:N_META + BLK]
                ref[pl.ds(pc, BLK), lanes] += val[N_META + BLK:]
        dsk_ref[0] = dsk

    return pl.pallas_call(
        body, name="swa_bwd", interpret=False,
        out_shape=[jax.ShapeDtypeStruct((SWA_H, rows, SWA_D), F32),
                   jax.ShapeDtypeStruct((rows, SWA_KV * SWA_D), F32),
                   jax.ShapeDtypeStruct((rows, SWA_KV * SWA_D), F32),
                   jax.ShapeDtypeStruct((nb, SWA_H, 128), F32)],
        grid=(nb,),
        in_specs=_swa_specs() + [pl.BlockSpec((BLK, SWA_H * SWA_D), lambda n: (n, 0)),
                                 pl.BlockSpec(memory_space=pltpu.SMEM)],
        out_specs=[pl.BlockSpec((SWA_H, BLK, SWA_D), lambda n: (0, n, 0)),
                   pl.BlockSpec((rows, SWA_KV * SWA_D), lambda n: (0, 0)),
                   pl.BlockSpec((rows, SWA_KV * SWA_D), lambda n: (0, 0)),
                   pl.BlockSpec((1, SWA_H, 128), lambda n: (n, 0, 0))],
        compiler_params=_params(("arbitrary",)),
    )(qh, kh, kh, kh, vh, vh, vh, do, sinks)


QK_W = (SWA_H + SWA_KV) * SWA_D


def _head_mean(t):
    r = lax.broadcasted_iota(jnp.int32, (128, 128), 0) // SWA_D
    c = lax.broadcasted_iota(jnp.int32, (128, 128), 1) // SWA_D
    blk = jnp.where(r == c, 1.0 / SWA_D, 0.0).astype(BF16)
    out = []
    for i in range(t.shape[1] // 128):
        hi, lo = _split(t[:, 128 * i:128 * (i + 1)])
        out.append(_dot(hi, blk) + _dot(lo, blk))
    return jnp.concatenate(out, axis=1)


def _qk_scales(qw, kw):
    scale = SWA_D ** -0.5
    wt = jnp.concatenate([jnp.tile(qw.astype(F32) * scale, (1, SWA_H)), jnp.tile(kw.astype(F32), (1, SWA_KV))], axis=1)
    st = jnp.concatenate([jnp.full((1, SWA_H * SWA_D), scale, F32), jnp.ones((1, SWA_KV * SWA_D), F32)], axis=1)
    return wt, st


def qknorm_fwd(qkv, qw, kw):
    rows = qkv.shape[0]
    tr = _pick(rows, (384, 128))
    wt, _ = _qk_scales(qw, kw)

    def fn(i, x, w):
        xq = x[:, :QK_W]
        y = xq * lax.rsqrt(_head_mean(xq * xq) + EPS) * w
        head = lambda t, j: t[:, j * SWA_D:(j + 1) * SWA_D][None]
        qo = jnp.concatenate([head(y, j) for j in range(SWA_H)], axis=0)
        ko = jnp.concatenate([head(y, SWA_H + j) for j in range(SWA_KV)], axis=0)
        vo = jnp.concatenate([head(x, SWA_H + SWA_KV + j) for j in range(SWA_KV)], axis=0)
        return qo, ko, vo

    hm = lambda nh: ((nh, rows, SWA_D), BF16, (nh, tr, SWA_D), lambda i: (0, i, 0), "r3")
    return rowwise(fn, [cols(qkv, tr), whole(wt)], [hm(SWA_H), hm(SWA_KV), hm(SWA_KV)],
                   steps=rows // tr, name="qknorm_fwd")


def qknorm_bwd(qkv, qw, kw, dqh, dk, dv):
    rows = qkv.shape[0]
    tr = _pick(rows, (384, 128))
    wt, st = _qk_scales(qw, kw)

    def fn(i, x, w, sc, dq, dkv, dvv):
        xq = x[:, :QK_W]
        dy = jnp.concatenate([dq[j] for j in range(SWA_H)] + [dkv], axis=1)
        r = lax.rsqrt(_head_mean(xq * xq) + EPS)
        xh = xq * r
        gw = dy * w
        dx = r * (gw - xh * _head_mean(gw * xh))
        return jnp.concatenate([dx, dvv], axis=1), jnp.sum(dy * sc * xh, axis=0, keepdims=True)

    dqkv, dw = rowwise(fn, [cols(qkv, tr), whole(wt), whole(st), heads(dqh, tr), cols(dk, tr), cols(dv, tr)],
                       [out2d(rows, 1536, BF16, tr)], steps=rows // tr, name="qknorm_bwd", accs=[((1, QK_W), F32)])
    dw = dw.reshape(SWA_H + SWA_KV, SWA_D)
    return dqkv, jnp.sum(dw[:SWA_H], axis=0, keepdims=True), jnp.sum(dw[SWA_H:], axis=0, keepdims=True)


def _place():
    return lax.axis_index("x"), lax.axis_index("y"), lax.axis_index("c")


ANY = pl.BlockSpec(memory_space=pl.ANY)


def _rcopy(ssem, rsem, k, src, dst, to):
    return pltpu.make_async_remote_copy(src_ref=src, dst_ref=dst, send_sem=ssem.at[k], recv_sem=rsem.at[k],
                                        device_id=to, device_id_type=MESH)


def gather_weights(shards, small):
    n = len(shards)
    halves = [t.shape[0] // 2 for t in shards]

    def body(*refs):
        s_refs, small_ref = refs[:n], refs[n]
        o_refs, osmall = refs[n + 1:2 * n + 1], refs[2 * n + 1]
        ssem, rsem, lsem = refs[2 * n + 2:]
        x, y, c = _place()
        me = 2 * x + y
        chips = [(1 - x, y), (x, 1 - y), (1 - x, 1 - y)]

        def half(k, s, hh):
            return o_refs[k].at[s, pl.ds(hh * halves[k], halves[k]), :]

        loc = pltpu.make_async_copy(small_ref, osmall.at[me], lsem)
        loc.start()
        sends = []
        for k in range(n):
            for j, (px, py) in enumerate(chips):
                sends.append(_rcopy(ssem, rsem, 6 * k + j, s_refs[k].at[pl.ds(c * halves[k], halves[k]), :],
                                    half(k, me, c), (px, py, c)))
        for j, (px, py) in enumerate(chips):
            sends.append(_rcopy(ssem, rsem, 6 * n + j, small_ref, osmall.at[me], (px, py, c)))
        for cp in sends:
            cp.start()
        for k in range(n):
            for j, (px, py) in enumerate(chips):
                s = 2 * px + py
                _rcopy(ssem, rsem, 6 * k + j, half(k, s, c), half(k, s, c), (x, y, c)).wait_recv()
                fwd = _rcopy(ssem, rsem, 6 * k + 3 + j, half(k, s, c), half(k, s, c), (x, y, 1 - c))
                fwd.start()
                sends.append(fwd)
        for k in range(n):
            for j, (px, py) in enumerate(chips):
                s = 2 * px + py
                _rcopy(ssem, rsem, 6 * k + 3 + j, half(k, s, 1 - c), half(k, s, 1 - c), (x, y, c)).wait_recv()
        for j, (px, py) in enumerate(chips):
            s = 2 * px + py
            _rcopy(ssem, rsem, 6 * n + j, osmall.at[s], osmall.at[s], (x, y, c)).wait_recv()
        for cp in sends:
            cp.wait_send()
        loc.wait()

    res = pl.pallas_call(
        body, name="gather_weights", interpret=False,
        out_shape=[jax.ShapeDtypeStruct((4,) + t.shape, t.dtype) for t in shards]
        + [jax.ShapeDtypeStruct((4, SW_ROWS, 1024), F32)],
        in_specs=[ANY] * (n + 1), out_specs=[ANY] * (n + 1),
        scratch_shapes=[pltpu.SemaphoreType.DMA((6 * n + 3,)), pltpu.SemaphoreType.DMA((6 * n + 3,)),
                        pltpu.SemaphoreType.DMA],
    )(*shards, small)
    return res[:n], res[n]


def _handshake(peers):
    barrier = pltpu.get_barrier_semaphore()
    for peer in peers:
        pl.semaphore_signal(barrier, inc=1, device_id=peer, device_id_type=MESH)
    pl.semaphore_wait(barrier, len(peers))


def gather_weights_beside(shards, cid, name):
    n = len(shards)
    halves = [t.shape[0] // 2 for t in shards]

    def body(*refs):
        s_refs, o_refs, ssem, rsem = refs[:n], refs[n:2 * n], refs[2 * n], refs[2 * n + 1]
        x, y, c = _place()
        me = 2 * x + y
        chips = [(1 - x, y), (x, 1 - y), (1 - x, 1 - y)]
        _handshake([(px, py, c) for px, py in chips] + [(x, y, 1 - c)])

        def half(k, s, hh):
            return o_refs[k].at[s, pl.ds(hh * halves[k], halves[k]), :]

        sends = []
        for k in range(n):
            for j, (px, py) in enumerate(chips):
                sends.append(_rcopy(ssem, rsem, 6 * k + j, s_refs[k].at[pl.ds(c * halves[k], halves[k]), :],
                                    half(k, me, c), (px, py, c)))
        for cp in sends:
            cp.start()
        for k in range(n):
            for j, (px, py) in enumerate(chips):
                s = 2 * px + py
                _rcopy(ssem, rsem, 6 * k + j, half(k, s, c), half(k, s, c), (x, y, c)).wait_recv()
                fwd = _rcopy(ssem, rsem, 6 * k + 3 + j, half(k, s, c), half(k, s, c), (x, y, 1 - c))
                fwd.start()
                sends.append(fwd)
        for k in range(n):
            for j, (px, py) in enumerate(chips):
                s = 2 * px + py
                _rcopy(ssem, rsem, 6 * k + 3 + j, half(k, s, 1 - c), half(k, s, 1 - c), (x, y, c)).wait_recv()
        for cp in sends:
            cp.wait_send()

    return pl.kernel(
        body, name=name,
        out_type=[jax.ShapeDtypeStruct((4,) + t.shape, t.dtype) for t in shards],
        mesh=plsc.ScalarSubcoreMesh(axis_name="sequencer", num_cores=1),
        scratch_types=[pltpu.SemaphoreType.DMA((6 * n,)), pltpu.SemaphoreType.DMA((6 * n,))],
        compiler_params=pltpu.CompilerParams(collective_id=cid),
    )(*shards)


def swap_halves(gs, *, name):
    n = len(gs)

    def body(*refs):
        g_refs, o_refs, ssem, rsem = refs[:n], refs[n:2 * n], refs[2 * n], refs[2 * n + 1]
        x, y, c = _place()
        cps = []
        for k in range(n):
            hk = g_refs[k].shape[1] // 2
            cps.append(_rcopy(ssem, rsem, k, g_refs[k].at[:, pl.ds((1 - c) * hk, hk), :], o_refs[k], (x, y, 1 - c)))
        for cp in cps:
            cp.start()
        for cp in cps:
            cp.wait()

    return pl.pallas_call(
        body, name=name, interpret=False,
        out_shape=[jax.ShapeDtypeStruct((4, t.shape[1] // 2, t.shape[2]), t.dtype) for t in gs],
        in_specs=[ANY] * n, out_specs=[ANY] * n,
        scratch_shapes=[pltpu.SemaphoreType.DMA((n,)), pltpu.SemaphoreType.DMA((n,))],
    )(*gs)


def _sum_rows(hk):
    return _pick(hk, (512, 352, 256, 128))


def pair_sum(g, other, c_idx, *, name):
    _, hk, width = other.shape
    tr = _sum_rows(hk)
    nbk = hk // tr

    def body(c_ref, g_ref, o_ref, out_ref):
        out_ref[...] = (g_ref[...].astype(F32) + o_ref[...].astype(F32)).astype(BF16)

    return pl.pallas_call(
        body, name=name, interpret=False,
        out_shape=jax.ShapeDtypeStruct((4, hk, width), BF16),
        grid_spec=pltpu.PrefetchScalarGridSpec(
            num_scalar_prefetch=1, grid=(4, nbk),
            in_specs=[pl.BlockSpec((1, tr, width), lambda s, i, c_ref: (s, c_ref[0] * nbk + i, 0)),
                      pl.BlockSpec((1, tr, width), lambda s, i, c_ref: (s, i, 0))],
            out_specs=pl.BlockSpec((1, tr, width), lambda s, i, c_ref: (s, i, 0))),
        compiler_params=_params(("parallel", "parallel")),
    )(c_idx, g, other)


def chip_sum(p, got, idx, *, name):
    _, hk, width = got.shape
    tr = _sum_rows(hk)
    nbk = hk // tr

    def body(idx_ref, p_ref, g_ref, out_ref):
        acc = p_ref[0].astype(F32)
        for j in range(3):
            acc = acc + g_ref[j].astype(F32)
        out_ref[0] = acc

    return pl.pallas_call(
        body, name=name, interpret=False,
        out_shape=jax.ShapeDtypeStruct((2, hk, width), F32),
        grid_spec=pltpu.PrefetchScalarGridSpec(
            num_scalar_prefetch=1, grid=(nbk,),
            in_specs=[pl.BlockSpec((1, tr, width), lambda i, idx_ref: (idx_ref[0], i, 0)),
                      pl.BlockSpec((3, tr, width), lambda i, idx_ref: (0, i, 0))],
            out_specs=pl.BlockSpec((1, tr, width), lambda i, idx_ref: (idx_ref[1], i, 0))),
        compiler_params=_params(("parallel",)),
    )(idx, p, got)


def join_halves(qs):
    n = len(qs)

    def body(*refs):
        q_refs, o_refs, ssem, rsem = refs[:n], refs[n:2 * n], refs[2 * n], refs[2 * n + 1]
        x, y, c = _place()
        cps = [_rcopy(ssem, rsem, k, q_refs[k].at[c], o_refs[k].at[c], (x, y, 1 - c)) for k in range(n)]
        for cp in cps:
            cp.start()
        for k in range(n):
            _rcopy(ssem, rsem, k, q_refs[k].at[c], o_refs[k].at[1 - c], (x, y, 1 - c)).wait_recv()
        for cp in cps:
            cp.wait_send()

    return pl.pallas_call(
        body, name="join_halves", interpret=False,
        out_shape=[jax.ShapeDtypeStruct(t.shape, t.dtype) for t in qs],
        in_specs=[ANY] * n, out_specs=[ANY] * n, input_output_aliases={k: k for k in range(n)},
        scratch_shapes=[pltpu.SemaphoreType.DMA((n,)), pltpu.SemaphoreType.DMA((n,))],
    )(*qs)


def scatter_chips_beside(ps, cid, name):
    n = len(ps)

    def body(*refs):
        p_refs, o_refs, ssem, rsem = refs[:n], refs[n:2 * n], refs[2 * n], refs[2 * n + 1]
        x, y, c = _place()
        chips = [(1 - x, y), (x, 1 - y), (1 - x, 1 - y)]
        _handshake([(px, py, c) for px, py in chips])
        cps = [_rcopy(ssem, rsem, 3 * k + j, p_refs[k].at[2 * px + py], o_refs[k].at[j], (px, py, c))
               for k in range(n) for j, (px, py) in enumerate(chips)]
        for cp in cps:
            cp.start()
        for cp in cps:
            cp.wait()

    return pl.kernel(
        body, name=name, out_type=[jax.ShapeDtypeStruct((3,) + t.shape[1:], t.dtype) for t in ps],
        mesh=plsc.ScalarSubcoreMesh(axis_name="sequencer", num_cores=1),
        scratch_types=[pltpu.SemaphoreType.DMA((3 * n,)), pltpu.SemaphoreType.DMA((3 * n,))],
        compiler_params=pltpu.CompilerParams(collective_id=cid),
    )(*ps)


def reduce_begin(gs, names, c_idx, cid, tag):
    others = swap_halves(gs, name=f"swap_halves_{tag}")
    pairs = [pair_sum(g, o, c_idx, name=f"pair_sum_{nm}") for g, o, nm in zip(gs, others, names)]
    return pairs, scatter_chips_beside(pairs, cid, f"scatter_chips_{tag}")


def reduce_end(pairs, gots, names, idx):
    mine = [chip_sum(p, g, idx, name=f"chip_sum_{nm}") for p, g, nm in zip(pairs, gots, names)]
    return [q.reshape(2 * q.shape[1], q.shape[2]) for q in join_halves(mine)]


def gather_small(v):
    def body(v_ref, o_ref, ssem, rsem, lsem):
        x, y, c = _place()
        peers = []
        for k in range(1, 8):
            fx, fy, fc = (k >> 2) & 1, (k >> 1) & 1, k & 1
            peers.append((1 - x if fx else x, 1 - y if fy else y, 1 - c if fc else c))
        _handshake(peers)
        loc = pltpu.make_async_copy(v_ref, o_ref.at[4 * x + 2 * y + c], lsem)
        loc.start()
        cps = []
        for k, (px, py, pc) in enumerate(peers):
            cps.append((pltpu.make_async_remote_copy(
                src_ref=v_ref, dst_ref=o_ref.at[4 * x + 2 * y + c], send_sem=ssem.at[k], recv_sem=rsem.at[k],
                device_id=(px, py, pc), device_id_type=MESH), 4 * px + 2 * py + pc))
        for cp, _ in cps:
            cp.start()
        for k, (cp, peer) in enumerate(cps):
            pltpu.make_async_remote_copy(
                src_ref=v_ref, dst_ref=o_ref.at[peer], send_sem=ssem.at[k], recv_sem=rsem.at[k],
                device_id=(x, y, c), device_id_type=MESH).wait_recv()
        for cp, _ in cps:
            cp.wait_send()
        loc.wait()

    return pl.kernel(
        body, name="gather_small", out_type=jax.ShapeDtypeStruct((8, SV_ROWS, 1024), F32),
        mesh=plsc.ScalarSubcoreMesh(axis_name="sequencer", num_cores=1),
        scratch_types=[pltpu.SemaphoreType.DMA((7,)), pltpu.SemaphoreType.DMA((7,)), pltpu.SemaphoreType.DMA],
        compiler_params=pltpu.CompilerParams(collective_id=6),
    )(v)


def sum_slots(a):
    def fn(i, t):
        acc = t[0]
        for k in range(1, 8):
            acc = acc + t[k]
        return acc

    return rowwise(fn, [whole(a)], [((SV_ROWS, 1024), F32, (SV_ROWS, 1024), lambda i: (0, 0), "w")], steps=1,
                   name="sum_slots")[0]


def _head_rms(x, nw):
    xs, rs = [], []
    for h in range(DN_H):
        xh = x[:, h * DN_D:(h + 1) * DN_D]
        r = lax.rsqrt(jnp.mean(xh * xh, axis=1, keepdims=True) + EPS)
        xs.append(xh * r)
        rs.append(r)
    return xs, rs


def bg_fwd(p, alog, dtb):
    rows = p.shape[0]
    tr = _pick(rows, (384, 128))

    def fn(i, x, al, dt):
        lane = lax.broadcasted_iota(jnp.int32, x.shape, 1)
        row = i + lax.broadcasted_iota(jnp.int32, x.shape, 0)
        g = -jnp.exp(al) * _softplus(x + dt)
        out = jnp.where(lane < 4, _sigmoid(x), jnp.where(lane < 8, g, 0.0))
        return jnp.where(row >= PAD, out, 0.0)

    return rowwise(fn, [cols(p, tr, 128, BG0 // 128), whole(alog), whole(dtb)], [out2d(rows, 128, F32, tr)],
                   steps=rows // tr, name="bg_fwd")[0]


def bg_bwd(p, alog, dtb, dbg):
    rows = p.shape[0]
    tr = _pick(rows, (384, 128))

    def fn(i, x, al, dt, g_in):
        lane = lax.broadcasted_iota(jnp.int32, x.shape, 1)
        row = i + lax.broadcasted_iota(jnp.int32, x.shape, 0)
        live = row >= PAD
        is_b = jnp.logical_and(live, lane < 4)
        is_g = jnp.logical_and(live, jnp.logical_and(lane >= 4, lane < 8))
        beta = _sigmoid(x)
        ea = jnp.exp(al)
        g = -ea * _softplus(x + dt)
        dalpha = jnp.where(is_g, g_in * (-ea) * _sigmoid(x + dt), 0.0)
        dx = jnp.where(is_b, g_in * beta * (1.0 - beta), dalpha)
        dal = jnp.sum(jnp.where(is_g, g_in * g, 0.0), axis=0, keepdims=True)
        return jnp.concatenate([dx, jnp.zeros(x.shape, F32)], axis=1), dal, jnp.sum(dalpha, axis=0, keepdims=True)

    return rowwise(fn, [cols(p, tr, 128, BG0 // 128), whole(alog), whole(dtb), cols(dbg, tr)],
                   [out2d(rows, 256, BF16, tr)], steps=rows // tr, name="bg_bwd",
                   accs=[((1, 128), F32), ((1, 128), F32)])


def dn_qkv_post(j, y):
    xs = _silu(y)
    sc = jnp.where(j == 0, DN_D ** -0.5, 1.0)
    outs = []
    for h in range(DN_H):
        xh = xs[:, h * DN_D:(h + 1) * DN_D]
        r = lax.rsqrt(jnp.sum(xh * xh, axis=1, keepdims=True) + EPS)
        outs.append(jnp.where(j < 2, xh * r * sc, xh))
    return jnp.concatenate(outs, axis=1), y


def dn_qkv_bwd(cq, dq, dk, dv):
    rows = cq.shape[0]
    tr = _pick(rows, (384, 128))

    def fn(i, c0, c1, c2, g0, g1, g2):
        pieces = []
        for kind, (cv, g) in enumerate(((c0, g0), (c1, g1), (c2, g2))):
            xs = _silu(cv)
            if kind < 2:
                sc = DN_D ** -0.5 if kind == 0 else 1.0
                ds = []
                for h in range(DN_H):
                    sl = slice(h * DN_D, (h + 1) * DN_D)
                    xh, gh = xs[:, sl], g[:, sl]
                    r = lax.rsqrt(jnp.sum(xh * xh, axis=1, keepdims=True) + EPS)
                    xn = xh * r
                    ds.append(sc * r * (gh - xn * jnp.sum(gh * xn, axis=1, keepdims=True)))
                dxs = jnp.concatenate(ds, axis=1)
            else:
                dxs = g
            pieces.append(dxs * _dsilu(cv))
        return jnp.concatenate(pieces, axis=1)

    ins = [cols(cq, tr, DN_DIM, k) for k in range(3)] + [cols(t, tr) for t in (dq, dk, dv)]
    return rowwise(fn, ins, [out2d(rows, 3 * DN_DIM, F32, tr)], steps=rows // tr, name="dn_qkv_bwd")[0]


def dn_out_fwd(o, p, nw):
    rows = o.shape[0]
    tr = _pick(rows, (384, 128))

    def fn(i, ov, z, w):
        xs, _ = _head_rms(ov, w)
        return jnp.concatenate(xs, axis=1) * jnp.concatenate([w] * DN_H, axis=1) * _silu(z)

    return rowwise(fn, [cols(o, tr), cols(p, tr, DN_DIM, 6), whole(nw)], [out2d(rows, DN_DIM, BF16, tr)],
                   steps=rows // tr, name="dn_out_fwd")[0]


def dn_out_bwd(o, p, nw, dymix):
    rows = o.shape[0]
    tr = _pick(rows, (384, 128))

    def fn(i, ov, z, w, dy):
        xs, rs = _head_rms(ov, w)
        sz = _silu(z)
        dn = dy * sz
        dos, dw = [], jnp.zeros((1, DN_D), F32)
        for h in range(DN_H):
            sl = slice(h * DN_D, (h + 1) * DN_D)
            gw = dn[:, sl] * w
            dos.append(rs[h] * (gw - xs[h] * jnp.mean(gw * xs[h], axis=1, keepdims=True)))
            dw = dw + jnp.sum(dn[:, sl] * xs[h], axis=0, keepdims=True)
        n = jnp.concatenate(xs, axis=1) * jnp.concatenate([w] * DN_H, axis=1)
        return jnp.concatenate(dos, axis=1), dy * n * _dsilu(z), dw

    return rowwise(fn, [cols(o, tr), cols(p, tr, DN_DIM, 6), whole(nw), cols(dymix, tr, DN_DIM, 1)],
                   [out2d(rows, DN_DIM, F32, tr), out2d(rows, DN_DIM, BF16, tr)], steps=rows // tr,
                   name="dn_out_bwd", accs=[((1, DN_D), F32)])


def conv_a_pre_bwd(dymix, cv, p):
    rows = cv.shape[0]
    tr = _pick(rows, (384, 128))

    def fn(i, dy, c, go):
        return dy * c, dy * go

    return rowwise(fn, [cols(dymix, tr, D_CONV, 0), cols(cv, tr), cols(p, tr, D_CONV, 1)],
                   [out2d(rows, D_CONV, BF16, tr), out2d(rows, D_CONV, F32, tr)], steps=rows // tr,
                   name="conv_a_pre_bwd")


def _rows8(w):
    return jnp.pad(w.astype(F32), ((0, 8 - w.shape[0]), (0, 0)))


def _lanes(v, at):
    return jnp.pad(v.astype(F32), (at, 128 - at - v.shape[0]))[None]


def add_norm(a, w, h, next_nw, *, name):
    return mm(a, w, name=name, epi=_add_norm_epi, epi_ins=[(h, lambda j: 0)], epi_consts=[next_nw],
              epi_outs=[F32, BF16])


def _add_norm_epi(row0, t, h, nw):
    x = t + h
    return x, x * lax.rsqrt(jnp.mean(x * x, axis=1, keepdims=True) + EPS) * nw


def ffn_up_conv(hn, w_up, cw8, *, name):
    rows = hn.shape[0]
    tn = w_up.shape[2]
    tm = _pick(rows, (384, 128))
    nr = rows // tm

    def body(x_ref, wg_ref, wv_ref, w_ref, ug_ref, uv_ref, gc_ref, a_ref, carry, scr):
        i = pl.program_id(1)
        x = x_ref[...]
        gate = _dot(x, wg_ref[...])
        val = _dot(x, wv_ref[...])
        ug_ref[...] = gate.astype(BF16)
        uv_ref[...] = val.astype(BF16)
        scr[0:8, :] = jnp.where(i > 0, carry[...], 0.0)
        scr[8:8 + tm, :] = gate
        carry[...] = gate[tm - 8:tm]
        y = jnp.zeros((tm, tn), F32)
        for q in range(3):
            sh = 2 - q
            y = y + w_ref[q:q + 1, :] * scr[8 - sh:8 - sh + tm, :]
        gc_ref[...] = y.astype(BF16)
        a_ref[...] = (_silu(y) * val).astype(BF16)

    half = pl.BlockSpec((tm, tn), lambda j, i: (i, j))
    return pl.pallas_call(
        body, name=name, interpret=False,
        out_shape=[jax.ShapeDtypeStruct((rows, D_FF), BF16)] * 4,
        grid=(D_FF // tn, nr),
        in_specs=[pl.BlockSpec((tm, D), lambda j, i: (i, 0)),
                  pl.BlockSpec((None, D, tn), lambda j, i: (j, 0, 0)),
                  pl.BlockSpec((None, D, tn), lambda j, i: (j + D_FF // tn, 0, 0)),
                  pl.BlockSpec((8, tn), lambda j, i: (0, j))],
        out_specs=[half] * 4,
        scratch_shapes=[pltpu.VMEM((8, tn), F32), pltpu.VMEM((tm + 8, tn), F32)],
        compiler_params=_params(("arbitrary", "arbitrary")),
    )(hn, w_up, w_up, cw8)


def ffn_down_bwd(dh, w_down, gc, uv, ug, cw8, *, name):
    rows = dh.shape[0]
    tn = D_FF // 2
    tm = _pick(rows, (384, 128))
    nr = rows // tm
    r8 = tm // 8

    def body(dh_ref, w_ref, gc_ref, uv_ref, ug_ref, halo_ref, cw_ref, du_ref, dw_ref, carry, gscr, xscr):
        ip = pl.program_id(1)
        i = nr - 1 - ip
        da = _dot(dh_ref[...].astype(BF16), w_ref[...], 1, 1)
        c, val = gc_ref[...].astype(F32), uv_ref[...].astype(F32)
        dgc = da * val * _dsilu(c)
        du_ref[:, tn:] = (da * _silu(c)).astype(BF16)
        gscr[0:tm, :] = dgc
        gscr[tm:tm + 8, :] = jnp.where(ip > 0, carry[...], 0.0)
        carry[...] = dgc[0:8]
        xscr[0:8, :] = jnp.where(i > 0, halo_ref[...].astype(F32), 0.0)
        xscr[8:8 + tm, :] = ug_ref[...].astype(F32)
        dx = jnp.zeros((tm, tn), F32)
        dws = []
        for q in range(3):
            sh = 2 - q
            dx = dx + cw_ref[q:q + 1, :] * gscr[sh:sh + tm, :]
            dws.append(jnp.sum(dgc * xscr[8 - sh:8 - sh + tm, :], axis=0, keepdims=True))
        du_ref[:, :tn] = dx.astype(BF16)

        @pl.when(ip == 0)
        def _():
            dw_ref[...] = jnp.zeros((8, tn), F32)

        dw_ref[...] += jnp.concatenate(dws + [jnp.zeros((5, tn), F32)], axis=0)

    rev = lambda ip: nr - 1 - ip
    tile = lambda arr: pl.BlockSpec((tm, tn), lambda j, ip: (rev(ip), j))
    return pl.pallas_call(
        body, name=name, interpret=False,
        out_shape=[jax.ShapeDtypeStruct((rows, 2 * D_FF), BF16), jax.ShapeDtypeStruct((8, D_FF), F32)],
        grid=(2, nr),
        in_specs=[pl.BlockSpec((tm, D), lambda j, ip: (rev(ip), 0)),
                  pl.BlockSpec((tn, D), lambda j, ip: (j, 0)),
                  tile(gc), tile(uv), tile(ug),
                  pl.BlockSpec((8, tn), lambda j, ip: (jnp.maximum(rev(ip) * r8 - 1, 0), j)),
                  pl.BlockSpec((8, tn), lambda j, ip: (0, j))],
        out_specs=[pl.BlockSpec((tm, 2 * tn), lambda j, ip: (rev(ip), j)),
                   pl.BlockSpec((8, tn), lambda j, ip: (0, j))],
        scratch_shapes=[pltpu.VMEM((8, tn), F32), pltpu.VMEM((tm + 8, tn), F32), pltpu.VMEM((tm + 8, tn), F32)],
        compiler_params=_params(("arbitrary", "arbitrary")),
    )(dh, w_down, gc, uv, ug, ug, cw8)


def ffn_fwd(h, hn, w_up, cw8, w_down, tag, next_nw=None, target=None):
    ug, uv, gc, a = ffn_up_conv(hn, w_up, cw8, name=f"ffn{tag}_up")
    if target is not None:
        out, hn_next = add_loss(a, w_down, h, target, name=f"ffn{tag}_down")
    else:
        out, hn_next = add_norm(a, w_down, h, next_nw, name=f"ffn{tag}_down")
    return out, hn_next, (hn, ug, uv, a, gc)


def ffn_bwd(h, nw, w_up, cw8, w_down, saved, dh, tag):
    hn, ug, uv, a, gc = saved
    du, d_cw = ffn_down_bwd(dh, w_down, gc, uv, ug, cw8, name=f"ffn{tag}_down_dx")
    d_w_down = mm(a, dh, ta=True, out_dtype=BF16, name=f"ffn{tag}_down_dw")
    dh_new, d_nw = dx_rms_bwd(du, w_up, h, nw, dh, name=f"ffn{tag}_up_dx", b_chip=True, swap_mid=True)
    d_w_up = mm(hn, du, ta=True, out_dtype=BF16, out_chip=True, swap_mid=True, name=f"ffn{tag}_up_dw")
    return dh_new, d_nw, d_w_up, d_cw, d_w_down


def mixer_fwd(h, nw, w_in, ca8, dc8, alog, dtb, dnw, w_out, tie=None, next_nw=None):
    rows = h.shape[0]
    tr = _pick(rows, (384, 128))
    hn = rms_fwd(h, nw, name="mix_norm")
    if callable(w_in):
        hn, w_in = w_in(hn)
    p = mm(hn, w_in, name="mix_in")
    y_a, cv = conv_fwd([(p, 0), (p, 2)], ca8, 3, rows=rows, c=D_CONV, tc=D_CONV, tr=tr, name="conv_a",
                       pre=lambda gi, ah: gi * ah, post=lambda j, y, go: (go * y, y), extras=[(p, 1)],
                       outs=[BF16, F32])
    qkv_n, cq = conv_fwd([(p, 3)], dc8, 4, rows=rows, c=3 * DN_DIM, tc=DN_DIM, tr=tr, name="dn_conv",
                         post=dn_qkv_post, outs=[F32, F32], strip=tr)
    bgcol = bg_fwd(p, alog, dtb)
    if tie is not None:
        bgcol = tie(bgcol)
    bgrow = bgcol[:, :8].reshape(rows // CH, CH, 8).transpose(0, 2, 1)
    o, s_all, ti_all = dn_fwd(qkv_n, bgcol, bgrow)
    y_b = dn_out_fwd(o, p, dnw)
    ymix = jnp.concatenate([y_a, y_b], axis=1)
    w_out = w_out() if callable(w_out) else w_out
    out, hn_next = add_norm(ymix, w_out, h, next_nw, name="mix_out")
    return out, hn_next, (hn, p, cv, qkv_n, cq, bgcol, bgrow, o, s_all, ti_all, ymix, w_in)


def mixer_bwd(h, nw, ca8, dc8, alog, dtb, dnw, w_out, saved, dh):
    hn, p, cv, qkv_n, cq, bgcol, bgrow, o, s_all, ti_all, ymix, w_in = saved
    rows = h.shape[0]
    tr = _pick(rows, (384, 128))
    dymix = mm(dh, w_out, tb=True, name="mix_out_dx")
    d_w_out = mm(ymix, dh, ta=True, out_dtype=BF16, name="mix_out_dw")
    do, dz, d_dnw = dn_out_bwd(o, p, dnw, dymix)
    dq, dk, dv, dbg = dn_bwd(qkv_n, bgcol, bgrow, s_all, ti_all, do)
    dbg_p, d_alog, d_dtb = bg_bwd(p, alog, dtb, dbg)
    dcq = dn_qkv_bwd(cq, dq, dk, dv)
    dqkv, d_dc = conv_bwd([(p, 3)], dc8, 4, dcq, rows=rows, c=3 * DN_DIM, tc=DN_DIM, tr=tr, name="dn_conv_bwd",
                          post=lambda dx: dx, outs=[BF16])
    dgo, dcv = conv_a_pre_bwd(dymix, cv, p)
    dgi, dah, d_ca = conv_bwd([(p, 0), (p, 2)], ca8, 3, dcv, rows=rows, c=D_CONV, tc=D_CONV, tr=tr,
                              name="conv_a_bwd", pre=lambda gi, ah: gi * ah,
                              post=lambda dm, gi, ah: (dm * ah, dm * gi), extras=[(p, 0), (p, 2)], outs=[BF16, BF16])
    dp = jnp.concatenate([dgi, dgo, dah, dqkv, dz, dbg_p], axis=1)
    dh_new, d_nw = dx_rms_bwd(dp, w_in, h, nw, dh, name="mix_in_dx")
    d_w_in = mm(hn, dp, ta=True, out_dtype=BF16, name="mix_in_dw")
    return dh_new, d_nw, d_w_in, d_ca, d_dc, d_alog, d_dtb, d_dnw, d_w_out


def swa_layer_fwd(h, hn, wqkv, qw, kw, sinks, wo, next_nw):
    qkv = mm(hn, wqkv, name="swa_qkv")
    qh, kh, vh = qknorm_fwd(qkv, qw, kw)
    att = swa_fwd(qh, kh, vh, sinks)
    out, hn_next = add_norm(att, wo, h, next_nw, name="swa_out")
    return out, hn_next, (hn, qkv, qh, kh, vh, att)


def swa_layer_bwd(h, nw, wqkv, qw, kw, sinks, wo, saved, dh):
    hn, qkv, qh, kh, vh, att = saved
    datt = mm(dh, wo, tb=True, out_dtype=BF16, name="swa_out_dx")
    d_wo = mm(att, dh, ta=True, out_dtype=BF16, name="swa_out_dw")
    dqh, dkh, dvh, dsk = swa_bwd(qh, kh, vh, sinks, datt)
    dqkv, d_qw, d_kw = qknorm_bwd(qkv, qw, kw, dqh, dkh, dvh)
    dh_new, d_nw = dx_rms_bwd(dqkv, wqkv, h, nw, dh, name="swa_qkv_dx")
    d_wqkv = mm(hn, dqkv, ta=True, out_dtype=BF16, name="swa_qkv_dw")
    d_sinks = jnp.sum(dsk[:, :, 0], axis=0)
    return dh_new, d_nw, d_wqkv, d_qw, d_kw, d_sinks, d_wo


BIG = ("mix_w_in", "mix_w_out", "swa_wq", "swa_wk", "swa_wv", "swa_wo", "ffn_w_up", "ffn_w_down")


def _flat_pad(parts, rows):
    v = jnp.concatenate([t.astype(F32).reshape(-1) for t in parts])
    return jnp.pad(v, (0, rows * 1024 - v.shape[0])).reshape(rows, 1024)


def _split_flat(flat, shapes):
    v = flat.reshape(-1)
    out, o = [], 0
    for s in shapes:
        n = 1
        for d_ in s:
            n *= d_
        out.append(v[o:o + n].reshape(s))
        o += n
    return out


def local_step(x0, target0, meta_full, anw, fnw, w_in, ca8, dc8, alog, dtb, dnw, qw, kw, sinks, fc8, late,
               begin=None, tie=None):
    begin = begin or (lambda tag, names, grads: None)
    h0 = jnp.concatenate([jnp.zeros((PAD, D), F32), meta_full, x0], axis=0)
    h1, hn1, s_mix = mixer_fwd(h0, anw[0], w_in, ca8, dc8, alog, dtb, dnw, lambda: late()[0], tie, fnw[0])
    w_out, wqkv, wo, w_up, w_down = late()
    h2, hn2, s_f0 = ffn_fwd(h1, hn1, w_up[0], fc8[0], w_down[0], 0, anw[1])
    h3, hn3, s_swa = swa_layer_fwd(h2, hn2, wqkv, qw, kw, sinks, wo, fnw[1])
    dh, loss_l, s_f1 = ffn_fwd(h3, hn3, w_up[1], fc8[1], w_down[1], 1, target=target0)
    dh, d_fnw1, d_up1, d_fc1, d_down1 = ffn_bwd(h3, fnw[1], w_up[1], fc8[1], w_down[1], s_f1, dh, 1)
    begin("ffn1", ("up1", "down1"), [d_up1, d_down1.reshape(4, 704, D)])
    dh, d_anw1, d_wqkv, d_qw, d_kw, d_sinks, d_wo = swa_layer_bwd(h2, anw[1], wqkv, qw, kw, sinks, wo, s_swa, dh)
    begin("swa", ("wq", "wk", "wv", "wo"),
          [d_wqkv[:, :D].reshape(4, 256, D), d_wqkv[:, D:D + 256].reshape(4, 256, 256),
           d_wqkv[:, D + 256:].reshape(4, 256, 256), d_wo.reshape(4, 256, D)])
    dh, d_fnw0, d_up0, d_fc0, d_down0 = ffn_bwd(h1, fnw[0], w_up[0], fc8[0], w_down[0], s_f0, dh, 0)
    begin("ffn0", ("up0", "down0"), [d_up0, d_down0.reshape(4, 704, D)])
    dh, d_anw0, d_w_in, d_ca, d_dc, d_alog, d_dtb, d_dnw, d_w_out = mixer_bwd(
        h0, anw[0], ca8, dc8, alog, dtb, dnw, w_out, s_mix, dh)
    begin("mix", ("w_in", "w_out"),
          [d_w_in[:, :IN_DIM].reshape(D, 4, 898).transpose(1, 0, 2), d_w_out.reshape(4, 256, D)])
    return (dh, loss_l, d_anw0, d_anw1, d_fnw0, d_fnw1, d_w_in, d_ca, d_dc, d_alog, d_dtb, d_dnw, d_w_out, d_wqkv,
            d_qw, d_kw, d_sinks, d_wo, d_up0, d_up1, d_fc0, d_fc1, d_down0, d_down1)


def kernel(x, meta_tokens, attn_norm_w, ffn_norm_w, mix_w_in, conv_a_w, dn_conv_w, dn_a_log, dn_dt_bias, dn_norm_w, mix_w_out, swa_wq, swa_wk, swa_wv, swa_q_norm_w, swa_k_norm_w, swa_sinks, swa_wo, ffn_w_up, ffn_conv_w, ffn_w_down, loss_target, m_meta_tokens, m_attn_norm_w, m_ffn_norm_w, m_mix_w_in, m_conv_a_w, m_dn_conv_w, m_dn_a_log, m_dn_dt_bias, m_dn_norm_w, m_mix_w_out, m_swa_wq, m_swa_wk, m_swa_wv, m_swa_q_norm_w, m_swa_k_norm_w, m_swa_sinks, m_swa_wo, m_ffn_w_up, m_ffn_conv_w, m_ffn_w_down, v_meta_tokens, v_attn_norm_w, v_ffn_norm_w, v_mix_w_in, v_conv_a_w, v_dn_conv_w, v_dn_a_log, v_dn_dt_bias, v_dn_norm_w, v_mix_w_out, v_swa_wq, v_swa_wk, v_swa_wv, v_swa_q_norm_w, v_swa_k_norm_w, v_swa_sinks, v_swa_wo, v_ffn_w_up, v_ffn_conv_w, v_ffn_w_down):
    ix, iy, ic = lax.axis_index("x"), lax.axis_index("y"), lax.axis_index("c")
    chip = 2 * ix + iy
    seq = x.shape[1]
    rows = HEAD0 + seq

    small_sharded = (conv_a_w, dn_conv_w, ffn_conv_w, meta_tokens)
    up_b, down_b = ffn_w_up.astype(BF16), ffn_w_down.astype(BF16)
    own = [mix_w_in[0].astype(BF16), mix_w_out[0].astype(BF16), swa_wq[0].astype(BF16), swa_wk[0].astype(BF16),
           swa_wv[0].astype(BF16), swa_wo[0].astype(BF16), up_b[0], up_b[1], down_b[0], down_b[1]]
    fill = lambda gathered, mine: [lax.dynamic_update_slice_in_dim(g, t[None], chip, axis=0)
                                   for g, t in zip(gathered, mine)]
    on_its_way, = gather_weights_beside(own[:1], 9, "gather_w_in")
    _, g_small = gather_weights([], _flat_pad(small_sharded, SW_ROWS))

    rest = {}

    def w_in(hn):
        hn, got, g_out = lax.optimization_barrier((hn, on_its_way, own[1]))
        rest["w_out"] = fill(gather_weights_beside([g_out], 1, "gather_w_out"), [g_out])
        g_in, = fill([got], own[:1])
        return hn, jnp.pad(g_in.transpose(1, 0, 2).reshape(D, IN_DIM), ((0, 0), (0, P_W - IN_DIM)))

    def tie(t):
        t, *mine = lax.optimization_barrier((t, *own[2:]))
        g_q, g_k, g_v, g_o, g_up0, g_up1, g_dn0, g_dn1 = mine
        soon, last = [g_up0, g_dn0, g_q, g_k, g_v, g_o], [g_up1, g_dn1]
        rest["soon"] = fill(gather_weights_beside(soon, 7, "gather_layers_12"), soon)
        rest["last"] = fill(gather_weights_beside(last, 8, "gather_layer_3"), last)
        return t

    def late():
        (g_out,), (g_up0, g_dn0, g_q, g_k, g_v, g_o), (g_up1, g_dn1) = rest["w_out"], rest["soon"], rest["last"]
        wqkv = jnp.concatenate([g_q.reshape(D, D), g_k.reshape(D, 256), g_v.reshape(D, 256)], axis=1)
        return (g_out.reshape(D, D), wqkv, g_o.reshape(D, D), [g_up0, g_up1],
                [g_dn0.reshape(D_FF, D), g_dn1.reshape(D_FF, D)])

    gs = g_small.reshape(4, -1)
    ca_full = gs[:, 0:384].reshape(4, 3, 128).transpose(1, 0, 2).reshape(3, D_CONV)
    dc_full = gs[:, 384:1920].reshape(4, 4, 384).transpose(1, 0, 2).reshape(4, 3 * DN_DIM)
    fc_full = gs[:, 1920:6144].reshape(4, 2, 3, 704).transpose(1, 2, 0, 3).reshape(2, 3, D_FF)
    meta_full = gs[:, 6144:10240].reshape(4, N_META, 256).transpose(1, 0, 2).reshape(N_META, D)
    ca8, dc8 = _rows8(ca_full), _rows8(dc_full)
    fc8 = [_rows8(fc_full[0]), _rows8(fc_full[1])]
    alog, dtb = _lanes(dn_a_log[0], 4), _lanes(dn_dt_bias[0], 4)
    dnw = dn_norm_w.astype(F32)
    qw, kw = swa_q_norm_w.astype(F32), swa_k_norm_w.astype(F32)
    sinks = swa_sinks[0].astype(F32)
    anw = [attn_norm_w[0:1], attn_norm_w[1:2]]
    fnw = [ffn_norm_w[0:1], ffn_norm_w[1:2]]

    c_idx = jnp.reshape(ic, (1,)).astype(jnp.int32)
    chip_idx = jnp.stack([chip, ic]).astype(jnp.int32)
    begun = []

    def begin(tag, names, grads):
        pairs, gots = reduce_begin(grads, names, c_idx, 2 + len(begun), tag)
        begun.append((names, pairs, gots))

    (dh, loss_l, d_anw0, d_anw1, d_fnw0, d_fnw1, d_w_in, d_ca, d_dc, d_alog, d_dtb, d_dnw, d_w_out, d_wqkv, d_qw,
     d_kw, d_sinks, d_wo, d_up0, d_up1, d_fc0, d_fc1, d_down0, d_down1) = local_step(
        x[0], loss_target[0], meta_full, anw, fnw, w_in, ca8, dc8, alog, dtb, dnw, qw, kw, sinks, fc8, late,
        begin, tie)
    grad_x = dh[HEAD0:][None]

    small_parts = [jnp.concatenate([d_anw0, d_anw1], axis=0), jnp.concatenate([d_fnw0, d_fnw1], axis=0),
                   d_alog[0, 4:8], d_dtb[0, 4:8], d_dnw, d_qw, d_kw, d_sinks,
                   d_ca[:3], d_dc[:4], jnp.stack([d_fc0[:3], d_fc1[:3]]), dh[PAD:HEAD0], loss_l[0, 0:1]]
    small_shapes = [(2, D), (2, D), (1, 4), (1, 4), (1, DN_D), (1, SWA_D), (1, SWA_D), (1, SWA_H),
                    (1, 3, D_CONV), (1, 4, 3 * DN_DIM), (2, 3, D_FF), (N_META, D), ()]
    gathered_small = gather_small(_flat_pad(small_parts, SV_ROWS))

    red_big = {}
    for part in (begun[:-1], begun[-1:]):
        part_names = [n for names, _, _ in part for n in names]
        red_big.update(zip(part_names, reduce_end([p for _, ps, _ in part for p in ps],
                                                  [g for _, _, gs_ in part for g in gs_], part_names, chip_idx)))
    g_w_in, g_w_out, g_wq, g_wk, g_wv, g_wo, g_up0, g_up1, g_dn0, g_dn1 = [
        red_big[n] for n in ("w_in", "w_out", "wq", "wk", "wv", "wo", "up0", "up1", "down0", "down1")]

    grads = dict(mix_w_in=g_w_in, mix_w_out=g_w_out, swa_wq=g_wq, swa_wk=g_wk, swa_wv=g_wv, swa_wo=g_wo,
                 ffn_w_up=[g_up0, g_up1], ffn_w_down=[g_dn0, g_dn1])
    weights = dict(meta_tokens=meta_tokens, attn_norm_w=attn_norm_w, ffn_norm_w=ffn_norm_w, mix_w_in=mix_w_in,
                   conv_a_w=conv_a_w, dn_conv_w=dn_conv_w, dn_a_log=dn_a_log, dn_dt_bias=dn_dt_bias,
                   dn_norm_w=dn_norm_w, mix_w_out=mix_w_out, swa_wq=swa_wq, swa_wk=swa_wk, swa_wv=swa_wv,
                   swa_q_norm_w=swa_q_norm_w, swa_k_norm_w=swa_k_norm_w, swa_sinks=swa_sinks, swa_wo=swa_wo,
                   ffn_w_up=ffn_w_up, ffn_conv_w=ffn_conv_w, ffn_w_down=ffn_w_down)
    m_in = dict(meta_tokens=m_meta_tokens, attn_norm_w=m_attn_norm_w, ffn_norm_w=m_ffn_norm_w, mix_w_in=m_mix_w_in,
                conv_a_w=m_conv_a_w, dn_conv_w=m_dn_conv_w, dn_a_log=m_dn_a_log, dn_dt_bias=m_dn_dt_bias,
                dn_norm_w=m_dn_norm_w, mix_w_out=m_mix_w_out, swa_wq=m_swa_wq, swa_wk=m_swa_wk, swa_wv=m_swa_wv,
                swa_q_norm_w=m_swa_q_norm_w, swa_k_norm_w=m_swa_k_norm_w, swa_sinks=m_swa_sinks, swa_wo=m_swa_wo,
                ffn_w_up=m_ffn_w_up, ffn_conv_w=m_ffn_conv_w, ffn_w_down=m_ffn_w_down)
    v_in = dict(meta_tokens=v_meta_tokens, attn_norm_w=v_attn_norm_w, ffn_norm_w=v_ffn_norm_w, mix_w_in=v_mix_w_in,
                conv_a_w=v_conv_a_w, dn_conv_w=v_dn_conv_w, dn_a_log=v_dn_a_log, dn_dt_bias=v_dn_dt_bias,
                dn_norm_w=v_dn_norm_w, mix_w_out=v_mix_w_out, swa_wq=v_swa_wq, swa_wk=v_swa_wk, swa_wv=v_swa_wv,
                swa_q_norm_w=v_swa_q_norm_w, swa_k_norm_w=v_swa_k_norm_w, swa_sinks=v_swa_sinks, swa_wo=v_swa_wo,
                ffn_w_up=v_ffn_w_up, ffn_conv_w=v_ffn_conv_w, ffn_w_down=v_ffn_w_down)
    names = list(weights)
    small = [n for n in names if n not in BIG]
    delta, new_m, new_v = {}, {}, {}
    for n in BIG:
        delta[n], new_m[n], new_v[n], grads[n] = adamw(weights[n], grads[n], m_in[n], v_in[n], name=f"adamw_{n}")
    gathered_small, _ = lax.optimization_barrier((gathered_small, new_v["ffn_w_down"]))
    (g_anw, g_fnw, g_alog, g_dtb, g_dnw, g_qw, g_kw, g_sinks, g_ca_f, g_dc_f, g_fc_f, g_meta_f,
     loss) = _split_flat(sum_slots(gathered_small), small_shapes)
    grads.update(meta_tokens=lax.dynamic_slice_in_dim(g_meta_f, chip * 256, 256, axis=1), attn_norm_w=g_anw,
                 ffn_norm_w=g_fnw, conv_a_w=lax.dynamic_slice_in_dim(g_ca_f, chip * 128, 128, axis=2),
                 dn_conv_w=lax.dynamic_slice_in_dim(g_dc_f, chip * 384, 384, axis=2), dn_a_log=g_alog,
                 dn_dt_bias=g_dtb, dn_norm_w=g_dnw, swa_q_norm_w=g_qw, swa_k_norm_w=g_kw, swa_sinks=g_sinks,
                 ffn_conv_w=lax.dynamic_slice_in_dim(g_fc_f, chip * 704, 704, axis=2))
    grads = {n: grads[n].reshape(weights[n].shape) for n in names}
    shapes = [weights[n].shape for n in small]
    packed = [_flat_pad([t[n] for n in small], SW_ROWS) for t in (weights, grads, m_in, v_in)]
    for store, flat in zip((delta, new_m, new_v), adamw(*packed, name="adamw_small")):
        for n, t in zip(small, _split_flat(flat, shapes)):
            store[n] = t
    return (loss, grad_x, *[grads[n] for n in names], *[delta[n] for n in names],
            *[new_m[n] for n in names], *[new_v[n] for n in names])
```

```python
import functools

import jax
import jax.numpy as jnp
from jax import lax
from jax.experimental import pallas as pl
from jax.experimental.pallas import tpu as pltpu
from jax.experimental.pallas import tpu_sc as plsc

F32 = jnp.float32
BF16 = jnp.bfloat16
HI = lax.Precision.HIGHEST
MESH = pl.DeviceIdType.MESH

D = 1024
N_META = 16
PAD = 112
HEAD0 = PAD + N_META
D_CONV = 512
DN_H = 4
DN_D = 128
DN_DIM = 512
CH = 64
IN_DIM = 3592
P_W = 3840
BG0 = 3584
SWA_H = 16
SWA_KV = 4
SWA_D = 64
BLK = 128
NKEY = N_META + 2 * BLK
D_FF = 2816
EPS = 1e-6
LR, B1, B2, AEPS, WD, STEP = 0.001, 0.9, 0.999, 1e-08, 0.01, 10
VMEM_LIMIT = 48 * 1024 * 1024
MM_VMEM_BUDGET = 34 * 1024 * 1024
R_BIG = 6144
R_HALF = R_BIG // 2
SV_ROWS = 48
SW_ROWS = 16


def _pick(n, cands):
    for c in cands:
        if n % c == 0:
            return c
    return n


def _params(sem=None):
    return pltpu.CompilerParams(dimension_semantics=sem, vmem_limit_bytes=VMEM_LIMIT)


def _dot(a, b, ca=1, cb=0, prec=None):
    return lax.dot_general(a, b, (((ca,), (cb,)), ((), ())), precision=prec,
                           preferred_element_type=F32)


def _sigmoid(x):
    return 1.0 / (1.0 + jnp.exp(-x))


def _silu(x):
    return x * _sigmoid(x)


def _dsilu(x):
    s = _sigmoid(x)
    return s * (1.0 + x * (1.0 - s))


def _softplus(x):
    return jnp.maximum(x, 0.0) + jnp.log(1.0 + jnp.exp(-jnp.abs(x)))


def mm(a, b, *, name, ta=False, tb=False, out_dtype=F32, add=None, tm=None, tn=None, tk=None,
       b_chip=False, out_chip=False, swap_mid=False, epi=None, epi_ins=(), epi_consts=(), epi_outs=(), epi_accs=()):
    if epi is not None:
        return _mm_epi(a, b, name=name, tb=tb, tn=tn, b_chip=b_chip, swap_mid=swap_mid, epi=epi, epi_ins=epi_ins,
                       epi_consts=epi_consts, epi_outs=epi_outs, epi_accs=epi_accs)
    chip_of = _chip_order(swap_mid)
    m, k = (a.shape[1], a.shape[0]) if ta else a.shape
    if b_chip:
        n = b.shape[1] if tb else 4 * b.shape[2]
        if tb:
            tk = b.shape[2]
        else:
            tn = b.shape[2]
    else:
        n = b.shape[0] if tb else b.shape[1]
    if out_chip:
        tn = n // 4
    tn = tn or _pick(n, (1408, 1024, 768, 512, 256, 128))
    tk = tk or (_pick(k, (1408, 704, 384, 128)) if ta else _pick(k, (1024, 1408, 768, 512, 128)))
    nk = k // tk
    if tm is None:
        isz = lambda t: jnp.dtype(t.dtype).itemsize
        osz = jnp.dtype(out_dtype).itemsize
        for tm in ((1408, 1024, 512, 384, 256, 128) if ta else (1408, 704, 512, 384, 256, 128)):
            need = 2 * (tm * tk * isz(a) + tk * tn * isz(b) + tm * tn * osz + (tm * tn * 4 if add is not None else 0))
            need += tm * tn * 4 if nk > 1 else 0
            if m % tm == 0 and need <= MM_VMEM_BUDGET:
                break
        else:
            tm = m
    dims = (((0 if ta else 1,), (1 if tb else 0,)), ((), ()))

    def body(*refs):
        if add is None:
            a_ref, b_ref, o_ref, acc_ref = refs
            add_ref = None
        else:
            a_ref, b_ref, add_ref, o_ref, acc_ref = refs
        def part():
            return lax.dot_general(a_ref[...].astype(BF16), b_ref[...].astype(BF16), dims,
                                   preferred_element_type=F32)

        def finish(total):
            if add_ref is not None:
                total = total + add_ref[...]
            o_ref[...] = total.astype(out_dtype)

        if nk == 1:
            finish(part())
        else:
            kk = pl.program_id(2)

            @pl.when(kk == 0)
            def _():
                acc_ref[...] = jnp.zeros(acc_ref.shape, F32)

            @pl.when(kk < nk - 1)
            def _():
                acc_ref[...] += part()

            @pl.when(kk == nk - 1)
            def _():
                finish(acc_ref[...] + part())

    a_spec = pl.BlockSpec((tk, tm), lambda i, j, kk: (kk, i)) if ta else pl.BlockSpec((tm, tk), lambda i, j, kk: (i, kk))
    if b_chip and tb:
        b_spec = pl.BlockSpec((None, tn, tk), lambda i, j, kk: (chip_of(kk), j, 0))
    elif b_chip:
        b_spec = pl.BlockSpec((None, tk, tn), lambda i, j, kk: (j, kk, 0))
    elif tb:
        b_spec = pl.BlockSpec((tn, tk), lambda i, j, kk: (j, kk))
    else:
        b_spec = pl.BlockSpec((tk, tn), lambda i, j, kk: (kk, j))
    o_spec = pl.BlockSpec((tm, tn), lambda i, j, kk: (i, j))
    in_specs = [a_spec, b_spec] + ([o_spec] if add is not None else [])
    args = [a, b] + ([add] if add is not None else [])
    out_spec = pl.BlockSpec((None, tm, tn), lambda i, j, kk: (chip_of(j), i, 0)) if out_chip else o_spec
    return pl.pallas_call(
        body, name=name, interpret=False,
        out_shape=jax.ShapeDtypeStruct((4, m, tn) if out_chip else (m, n), out_dtype),
        grid=(m // tm, n // tn, nk), in_specs=in_specs, out_specs=out_spec,
        scratch_shapes=[pltpu.VMEM((tm, tn) if nk > 1 else (8, 128), F32)],
        compiler_params=_params(("parallel", "parallel", "arbitrary")),
    )(*args)


def _chip_order(swap_mid):
    return (lambda k: (k % 2) * 2 + k // 2) if swap_mid else (lambda k: k)


def _mm_epi(a, b, *, name, tb, tn, b_chip, epi, epi_ins, epi_consts, epi_outs, epi_accs, swap_mid=False):
    chip_of = _chip_order(swap_mid)
    m, k = a.shape
    if b_chip:
        n = b.shape[1] if tb else 4 * b.shape[2]
        tk = b.shape[2] if tb else None
        tn = tn if tb else b.shape[2]
    else:
        n = b.shape[0] if tb else b.shape[1]
        tk = None
    tn = tn or _pick(n, (1408, 1024, 768, 512, 256, 128))
    tk = tk or _pick(k, (1024, 1408, 1280, 768, 512, 128))
    nk, nj = k // tk, n // tn
    isz = lambda t: jnp.dtype(t.dtype if hasattr(t, "dtype") else t).itemsize
    outs3 = [t if isinstance(t, tuple) else (t, n, lambda j: j) for t in epi_outs]
    side = sum(isz(t) for t, _ in epi_ins) + sum(isz(dt) for dt, _, _ in outs3)
    for tm in (1408, 704, 512, 384, 256, 128):
        need = 2 * (tm * tk * isz(a) + tk * tn * isz(b) + tm * tn * side) + (tm * tn * 4 if nk > 1 else 0)
        if m % tm == 0 and need <= MM_VMEM_BUDGET:
            break
    else:
        tm = m
    dims = (((1,), (1 if tb else 0,)), ((), ()))
    n_in, n_c, n_out, n_acc = len(epi_ins), len(epi_consts), len(epi_outs), len(epi_accs)

    def body(*refs):
        a_ref, b_ref = refs[:2]
        in_refs = refs[2:2 + n_in + n_c]
        out_refs = refs[2 + n_in + n_c:2 + n_in + n_c + n_out]
        acc_out = refs[2 + n_in + n_c + n_out:2 + n_in + n_c + n_out + n_acc]
        acc_ref = refs[-1]
        i, j, kk = pl.program_id(0), pl.program_id(1), pl.program_id(2)
        def part():
            return lax.dot_general(a_ref[...].astype(BF16), b_ref[...].astype(BF16), dims,
                                   preferred_element_type=F32)

        def finish(total):
            res = epi(i * tm, total, *[r[...] for r in in_refs])
            if not isinstance(res, (tuple, list)):
                res = (res,)
            for r, v in zip(out_refs, res[:n_out]):
                r[...] = v.astype(r.dtype)
            if n_acc:
                @pl.when(jnp.logical_and(i == 0, j == 0))
                def _():
                    for r in acc_out:
                        r[...] = jnp.zeros(r.shape, r.dtype)

                for r, v in zip(acc_out, res[n_out:]):
                    r[...] += jnp.broadcast_to(v, r.shape).astype(r.dtype)

        if nk == 1:
            finish(part())
        else:
            @pl.when(kk == 0)
            def _():
                acc_ref[...] = jnp.zeros(acc_ref.shape, F32)

            @pl.when(kk < nk - 1)
            def _():
                acc_ref[...] += part()

            @pl.when(kk == nk - 1)
            def _():
                finish(acc_ref[...] + part())

    a_spec = pl.BlockSpec((tm, tk), lambda i, j, kk: (i, kk))
    if b_chip and tb:
        b_spec = pl.BlockSpec((None, tn, tk), lambda i, j, kk: (chip_of(kk), j, 0))
    elif b_chip:
        b_spec = pl.BlockSpec((None, tk, tn), lambda i, j, kk: (j, kk, 0))
    elif tb:
        b_spec = pl.BlockSpec((tn, tk), lambda i, j, kk: (j, kk))
    else:
        b_spec = pl.BlockSpec((tk, tn), lambda i, j, kk: (kk, j))
    in_specs = [a_spec, b_spec]

    def in_spec(t, col):
        front = m - t.shape[0]
        if not front:
            return pl.BlockSpec((tm, tn), lambda i, j, kk: (i, col(j)))
        return pl.BlockSpec((pl.Element(tm), pl.Element(tn)),
                            lambda i, j, kk: (pl.multiple_of(jnp.maximum(i * tm - front, 0), 8), col(j) * tn))

    in_specs += [in_spec(t, col) for t, col in epi_ins]
    in_specs += [pl.BlockSpec(t.shape, lambda i, j, kk, nd=t.ndim: (0,) * nd) for t in epi_consts]
    out_specs = [pl.BlockSpec((tm, tn), lambda i, j, kk, col=col: (i, col(j))) for _, _, col in outs3]
    out_specs += [pl.BlockSpec(s, lambda i, j, kk, nd=len(s): (0,) * nd) for s, _ in epi_accs]
    out_shape = [jax.ShapeDtypeStruct((m, width), dt) for dt, width, _ in outs3]
    out_shape += [jax.ShapeDtypeStruct(s, dt) for s, dt in epi_accs]
    sem = ("arbitrary", "arbitrary", "arbitrary") if n_acc else ("parallel", "parallel", "arbitrary")
    return pl.pallas_call(
        body, name=name, interpret=False, out_shape=out_shape,
        grid=(m // tm, nj, nk), in_specs=in_specs, out_specs=out_specs,
        scratch_shapes=[pltpu.VMEM((tm, tn) if nk > 1 else (8, 128), F32)],
        compiler_params=_params(sem),
    )(a, b, *[t for t, _ in epi_ins], *epi_consts)


def cols(arr, tr, width=None, cb=0):
    width = width or arr.shape[1]
    return (arr, (tr, width), lambda i: (i, cb), "r2")


def heads(arr, tr):
    return (arr, (arr.shape[0], tr, arr.shape[2]), lambda i: (0, i, 0), "r3")


def whole(arr):
    nd = arr.ndim
    return (arr, arr.shape, lambda i: (0,) * nd, "w")


STRIP = 16


def _rows_of(ref, kind, r0, n):
    if kind == "r2":
        return ref[pl.ds(r0, n), :]
    if kind == "r3":
        return ref[:, pl.ds(r0, n), :]
    return ref[...]


def _set_rows(ref, kind, r0, n, v):
    if kind == "r2":
        ref[pl.ds(r0, n), :] = v.astype(ref.dtype)
    elif kind == "r3":
        ref[:, pl.ds(r0, n), :] = v.astype(ref.dtype)
    else:
        ref[...] = v.astype(ref.dtype)


def rowwise(fn, ins, outs, *, steps, name, accs=(), strip=None):
    n_in, n_out, n_acc = len(ins), len(outs), len(accs)
    kin = [t[3] for t in ins]
    kout = [t[4] for t in outs]
    tr = next((t[1][-2] for t in ins if t[3] != "w"), 0)

    def body(*refs):
        i = pl.program_id(0)
        in_refs, out_refs, acc_refs = refs[:n_in], refs[n_in:n_in + n_out], refs[n_in + n_out:]
        if n_acc:
            @pl.when(i == 0)
            def _():
                for r in acc_refs:
                    r[...] = jnp.zeros(r.shape, r.dtype)

        def run(r0, n):
            res = fn(i * tr + r0, *[_rows_of(r, k, r0, n) for r, k in zip(in_refs, kin)])
            if not isinstance(res, (tuple, list)):
                res = (res,)
            for r, k, v in zip(out_refs, kout, res[:n_out]):
                _set_rows(r, k, r0, n, v)
            for r, v in zip(acc_refs, res[n_out:]):
                r[...] += jnp.broadcast_to(v, r.shape).astype(r.dtype)

        if strip is None or tr <= strip:
            run(0, tr)
        else:
            def step(s, carry):
                run(pl.multiple_of(s * strip, strip), strip)
                return carry
            lax.fori_loop(0, tr // strip, step, 0)

    def zmap(nd):
        return lambda i: (0,) * nd

    in_specs = [pl.BlockSpec(t[1], t[2]) for t in ins]
    out_specs = [pl.BlockSpec(t[2], t[3]) for t in outs]
    out_specs += [pl.BlockSpec(s, zmap(len(s))) for s, _ in accs]
    out_shape = [jax.ShapeDtypeStruct(t[0], t[1]) for t in outs]
    out_shape += [jax.ShapeDtypeStruct(s, d) for s, d in accs]
    res = pl.pallas_call(
        body, name=name, interpret=False, out_shape=out_shape, grid=(steps,),
        in_specs=in_specs, out_specs=out_specs,
        compiler_params=_params(("arbitrary",)),
    )(*[t[0] for t in ins])
    return res


def out2d(rows, width, dtype, tr):
    return ((rows, width), dtype, (tr, width), lambda i: (i, 0), "r2")


def conv_fwd(xs, w8, kw, *, rows, c, tc, tr, name, post, extras=(), outs=(), pre=None, strip=STRIP):
    nx, ne, no = len(xs), len(extras), len(outs)
    nr, nc = rows // tr, c // tc
    r8 = tr // 8
    st = strip

    def body(*refs):
        x_refs = refs[:2 * nx]
        w_ref = refs[2 * nx]
        e_refs = refs[2 * nx + 1:2 * nx + 1 + ne]
        o_refs = refs[2 * nx + 1 + ne:2 * nx + 1 + ne + no]
        scr = refs[-1]
        j, i = pl.program_id(0), pl.program_id(1)
        halo = [x_refs[2 * q + 1][...].astype(F32) for q in range(nx)]
        scr[0:8, :] = jnp.where(i > 0, pre(*halo) if pre else halo[0], 0.0)

        def fill(s, carry):
            r0 = pl.multiple_of(s * st, st)
            cur = [x_refs[2 * q][pl.ds(r0, st), :].astype(F32) for q in range(nx)]
            scr[pl.ds(8 + r0, st), :] = pre(*cur) if pre else cur[0]
            return carry

        def comp(s, carry):
            r0 = pl.multiple_of(s * st, st)
            win = scr[pl.ds(r0, st + 8), :]
            y = jnp.zeros((st, tc), F32)
            for q in range(kw):
                sh = kw - 1 - q
                y = y + w_ref[q:q + 1, :] * win[8 - sh:8 - sh + st]
            res = post(j, y, *[e[pl.ds(r0, st), :] for e in e_refs])
            if not isinstance(res, (tuple, list)):
                res = (res,)
            for r, v in zip(o_refs, res):
                r[pl.ds(r0, st), :] = v.astype(r.dtype)
            return carry

        lax.fori_loop(0, tr // st, fill, 0)
        lax.fori_loop(0, tr // st, comp, 0)

    in_specs, args = [], []
    for arr, cb0 in xs:
        in_specs.append(pl.BlockSpec((tr, tc), lambda j, i, cb0=cb0: (i, cb0 + j)))
        in_specs.append(pl.BlockSpec((8, tc), lambda j, i, cb0=cb0: (jnp.maximum(i * r8 - 1, 0), cb0 + j)))
        args += [arr, arr]
    in_specs.append(pl.BlockSpec((8, tc), lambda j, i: (0, j)))
    args.append(w8)
    for arr, cb0 in extras:
        in_specs.append(pl.BlockSpec((tr, tc), lambda j, i, cb0=cb0: (i, cb0 + j)))
        args.append(arr)
    return pl.pallas_call(
        body, name=name, interpret=False,
        out_shape=[jax.ShapeDtypeStruct((rows, c), dt) for dt in outs],
        grid=(nc, nr), in_specs=in_specs,
        out_specs=[pl.BlockSpec((tr, tc), lambda j, i: (i, j)) for _ in outs],
        scratch_shapes=[pltpu.VMEM((tr + 8, tc), F32)],
        compiler_params=_params(("parallel", "arbitrary")),
    )(*args)


def conv_bwd(xs, w8, kw, dy, *, rows, c, tc, tr, name, post, extras=(), outs=(), pre=None):
    nx, ne, no = len(xs), len(extras), len(outs)
    nr, nc = rows // tr, c // tc
    r8 = tr // 8

    def body(*refs):
        x_refs = refs[:nx]
        w_ref, dy_ref, dyn_ref = refs[nx:nx + 3]
        e_refs = refs[nx + 3:nx + 3 + ne]
        first_out = nx + 3 + ne
        o_refs = refs[first_out:first_out + no]
        dw_ref = refs[first_out + no]
        gscr = refs[-1]
        i = pl.program_id(1)
        gscr[tr:tr + 8, :] = jnp.where(i < nr - 1, dyn_ref[...].astype(F32), 0.0)

        def fill(s, carry):
            r0 = pl.multiple_of(s * STRIP, STRIP)
            gscr[pl.ds(r0, STRIP), :] = dy_ref[pl.ds(r0, STRIP), :].astype(F32)
            return carry

        def comp(s, dws):
            r0 = pl.multiple_of(s * STRIP, STRIP)
            gwin = gscr[pl.ds(r0, STRIP + 8), :]
            cur = [x_refs[q][pl.ds(r0, STRIP), :].astype(F32) for q in range(nx)]
            x = pre(*cur) if pre else cur[0]
            dx = jnp.zeros((STRIP, tc), F32)
            new = []
            for q in range(kw):
                sh = kw - 1 - q
                ahead = gwin[sh:sh + STRIP]
                dx = dx + w_ref[q:q + 1, :] * ahead
                part = ahead * x
                new.append(dws[q] + part[0:8] + part[8:16])
            res = post(dx, *[e[pl.ds(r0, STRIP), :] for e in e_refs])
            if not isinstance(res, (tuple, list)):
                res = (res,)
            for r, v in zip(o_refs, res):
                r[pl.ds(r0, STRIP), :] = v.astype(r.dtype)
            return tuple(new)

        lax.fori_loop(0, tr // STRIP, fill, 0)
        dws = lax.fori_loop(0, tr // STRIP, comp, tuple(jnp.zeros((8, tc), F32) for _ in range(kw)))

        @pl.when(i == 0)
        def _():
            dw_ref[...] = jnp.zeros((8, tc), F32)

        dw_ref[...] += jnp.concatenate([jnp.sum(t, axis=0, keepdims=True) for t in dws]
                                       + [jnp.zeros((8 - kw, tc), F32)], axis=0)

    in_specs, args = [], []
    for arr, cb0 in xs:
        in_specs.append(pl.BlockSpec((tr, tc), lambda j, i, cb0=cb0: (i, cb0 + j)))
        args.append(arr)
    in_specs.append(pl.BlockSpec((8, tc), lambda j, i: (0, j)))
    in_specs.append(pl.BlockSpec((tr, tc), lambda j, i: (i, j)))
    in_specs.append(pl.BlockSpec((8, tc), lambda j, i: (jnp.minimum((i + 1) * r8, nr * r8 - 1), j)))
    args += [w8, dy, dy]
    for arr, cb0 in extras:
        in_specs.append(pl.BlockSpec((tr, tc), lambda j, i, cb0=cb0: (i, cb0 + j)))
        args.append(arr)
    return pl.pallas_call(
        body, name=name, interpret=False,
        out_shape=[jax.ShapeDtypeStruct((rows, c), dt) for dt in outs] + [jax.ShapeDtypeStruct((8, c), F32)],
        grid=(nc, nr), in_specs=in_specs,
        out_specs=[pl.BlockSpec((tr, tc), lambda j, i: (i, j)) for _ in outs] + [pl.BlockSpec((8, tc), lambda j, i: (0, j))],
        scratch_shapes=[pltpu.VMEM((tr + 8, tc), F32)],
        compiler_params=_params(("parallel", "arbitrary")),
    )(*args)


def rms_fwd(h, w, *, name):
    rows = h.shape[0]
    tr = _pick(rows, (384, 128))

    def fn(i, x, wv):
        r = lax.rsqrt(jnp.mean(x * x, axis=1, keepdims=True) + EPS)
        return x * r * wv

    return rowwise(fn, [cols(h, tr), whole(w)], [out2d(rows, D, BF16, tr)], steps=rows // tr, name=name)[0]


def _rms_bwd_epi(row0, g, x, dr, wv):
    r = lax.rsqrt(jnp.mean(x * x, axis=1, keepdims=True) + EPS)
    xh = x * r
    gw = g * wv
    dx = r * (gw - xh * jnp.mean(gw * xh, axis=1, keepdims=True))
    row = row0 + lax.broadcasted_iota(jnp.int32, (x.shape[0], 1), 0)
    return jnp.where(row >= PAD, dr + dx, 0.0), jnp.sum(g * xh, axis=0, keepdims=True)


def dx_rms_bwd(dy, w, h, nw, dres, *, name, b_chip=False, swap_mid=False):
    return mm(dy, w, tb=True, b_chip=b_chip, swap_mid=swap_mid, tn=D, name=name, epi=_rms_bwd_epi,
              epi_ins=[(h, lambda j: 0), (dres, lambda j: 0)], epi_consts=[nw], epi_outs=[F32],
              epi_accs=[((1, D), F32)])


def _add_loss_epi(row0, t, h, tgt):
    row = row0 + lax.broadcasted_iota(jnp.int32, (t.shape[0], 1), 0)
    tgt = jnp.where(row0 == 0, jnp.concatenate([tgt[-HEAD0:], tgt[:-HEAD0]], axis=0), tgt)
    diff = jnp.where(row >= HEAD0, t + h - tgt, 0.0)
    part = jnp.sum(jnp.sum(diff * diff, axis=1, keepdims=True), axis=0, keepdims=True)
    return diff * (1.0 / D), part * (0.5 / D)


def add_loss(a, w, h, target, *, name):
    return mm(a, w, name=name, epi=_add_loss_epi, epi_ins=[(h, lambda j: 0), (target, lambda j: 0)],
              epi_outs=[F32], epi_accs=[((1, 128), F32)])


def adamw(w, g, m, v, *, name):
    shape = w.shape
    gs = list(g) if isinstance(g, (list, tuple)) else [g]
    nl = len(gs)
    width = shape[-1]
    rows = w.size // width
    rl = rows // nl
    tr = _pick(rl, (256, 176, 128, 64, 16, 8))
    nr = rl // tr
    if w.ndim == 3 and shape[1] % tr == 0:
        per = shape[1] // tr
        view = lambda t: (t, (None, tr, width), lambda i: (i // per, i % per, 0), "r2")
        out = (shape, F32, (None, tr, width), lambda i: (i // per, i % per, 0), "r2")
    else:
        view = lambda t: cols(t.reshape(rows, width), tr)
        out = out2d(rows, width, F32, tr)

    def fn(i, wv, mv, vv, *gvs):
        gv = gvs[0]
        for layer in range(1, nl):
            gv = jnp.where(i >= layer * rl, gvs[layer], gv)
        mn = B1 * mv + (1.0 - B1) * gv
        vn = B2 * vv + (1.0 - B2) * gv * gv
        mh = mn / (1.0 - B1 ** STEP)
        vh = vn / (1.0 - B2 ** STEP)
        return -LR * (mh / (jnp.sqrt(vh) + AEPS) + WD * wv), mn, vn, gv

    g_ins = [(t.reshape(rl, width), (tr, width), lambda i, layer=layer: (jnp.clip(i - layer * nr, 0, nr - 1), 0), "r2")
             for layer, t in enumerate(gs)]
    res = rowwise(fn, [view(t) for t in (w, m, v)] + g_ins, [out] * 4, steps=rows // tr, name=name)
    return [r.reshape(shape) for r in res]


HB = DN_H * CH
PAIR = 3


def _split(a):
    hi = a.astype(BF16)
    return hi, (a - hi.astype(F32)).astype(BF16)


def _dot1(a, b, ca=1, cb=0):
    return _dot(a.astype(BF16), b.astype(BF16), ca, cb)


def _dot3(a, b, ca=1, cb=0):
    ah, al = _split(a)
    bh, bl = _split(b)
    return _dot(ah, bh, ca, cb) + (_dot(ah, bl, ca, cb) + _dot(al, bh, ca, cb))


def _dot01(m01, b, ca=1, cb=0):
    bh, bl = _split(b)
    m = m01.astype(BF16)
    return _dot(m, bh, ca, cb) + _dot(m, bl, ca, cb)


def _stack(x):
    return jnp.concatenate([x[:, h * DN_D:(h + 1) * DN_D] for h in range(DN_H)], axis=0)


def _unstack(x):
    return jnp.concatenate([x[h * CH:(h + 1) * CH] for h in range(DN_H)], axis=1)


def _tri_inv(mats, blk, eye):
    each = lambda f, *lists: [f(*t) for t in zip(*lists)]
    ad = [jnp.where(blk, a, 0.0) for a in mats]
    lo = each(lambda a, d: a - d, mats, ad)
    a2 = each(_dot3, ad, ad)
    a4 = each(_dot3, a2, a2)
    a8 = each(_dot3, a4, a4)
    dgi = each(lambda d, s: _dot3(eye - d, eye + s), ad, a2)
    dgi = each(lambda p, s: _dot3(p, eye + s), dgi, a4)
    dgi = each(lambda p, s: _dot3(p, eye + s), dgi, a8)
    n = each(_dot3, dgi, lo)
    n2 = each(_dot3, n, n)
    return each(_dot3, each(lambda u, v: _dot3(eye - u, eye + v), n, n2), dgi)


def _dn_masks():
    row = lax.broadcasted_iota(jnp.int32, (HB, HB), 0)
    col = lax.broadcasted_iota(jnp.int32, (HB, HB), 1)
    same = (row // CH) == (col // CH)
    incl = jnp.logical_and(same, row >= col)
    strict = jnp.logical_and(same, row > col)
    upper = jnp.logical_and(same, row <= col)
    blk = (row // 16) == (col // 16)
    eye = (row == col).astype(F32)
    return incl, strict, upper, blk, eye


def _dn_chunk(qv, kv, vv, bc, br, incl, strict):
    r64 = lax.broadcasted_iota(jnp.int32, (CH, CH), 0)
    c64 = lax.broadcasted_iota(jnp.int32, (CH, CH), 1)
    dcol = _dot01((r64 >= c64).astype(F32), bc)
    drow = _dot3(br, (r64 <= c64).astype(F32))
    col = lambda m, l0: jnp.concatenate([m[:, l0 + h:l0 + h + 1] for h in range(DN_H)], axis=0)
    b_c = col(bc, 0)
    d_c = col(dcol, 4)
    d_r = jnp.concatenate([drow[4 + h:5 + h, :] for h in range(DN_H)], axis=1)
    d_last_h = [dcol[CH - 1:CH, 4 + h:5 + h] for h in range(DN_H)]
    d_last = jnp.concatenate([jnp.broadcast_to(t, (CH, 1)) for t in d_last_h], axis=0)
    q, k, v = _stack(qv), _stack(kv), _stack(vv)
    dm = jnp.where(incl, jnp.exp(jnp.where(incl, d_c - d_r, 0.0)), 0.0)
    kk = _dot1(k, k, 1, 1)
    a = jnp.where(strict, b_c * kk * dm, 0.0)
    ed = jnp.exp(d_c)
    rhs = jnp.concatenate([v * b_c, k * (b_c * ed)], axis=1)
    qk = _dot1(q, k, 1, 1) * dm
    ekd = jnp.exp(d_last - d_c)
    gl = [jnp.exp(t) for t in d_last_h]
    return q, k, v, b_c, dm, kk, a, ed, rhs, qk, ekd, gl


def dn_fwd(qkv_n, bgcol, bgrow):
    rows = qkv_n.shape[0]
    nch = rows // CH

    def body(q_ref, k_ref, v_ref, bc_ref, br_ref, o_ref, s_out, ti_out, s_scr, prep, prep_qk, prep_gl):
        n = pl.program_id(0)

        @pl.when(n == 0)
        def _():
            s_scr[...] = jnp.zeros(s_scr.shape, F32)
            prep[...] = jnp.zeros(prep.shape, F32)
            prep_qk[...] = jnp.zeros(prep_qk.shape, F32)
            prep_gl[...] = jnp.zeros(prep_gl.shape, F32)

        live = n > 0
        rows_of = [slice(h * CH, (h + 1) * CH) for h in range(DN_H)]
        s = [s_scr[h] for h in range(DN_H)]
        for c in range(PAIR):
            u, w, qd, kd = prep[c, 0], prep[c, 1], prep[c, 2], prep[c, 3]
            for h in range(DN_H):
                s_out[c, h] = s[h]
            v_new = [u[rs] - _dot1(w[rs], s[h]) for h, rs in enumerate(rows_of)]
            o_state = [_dot1(qd[rs], s[h]) for h, rs in enumerate(rows_of)]
            s = [jnp.where(live, prep_gl[c, h:h + 1, 0:1] * s[h] + _dot1(kd[rs], v_new[h], 0, 0), s[h])
                 for h, rs in enumerate(rows_of)]
            o = jnp.concatenate(o_state, axis=0) + _dot1(prep_qk[c], jnp.concatenate(v_new, axis=0))
            o_ref[c * CH:(c + 1) * CH, :] = _unstack(o)
        for h in range(DN_H):
            s_scr[h] = s[h]

        incl, strict, _, blk, eye = _dn_masks()
        parts = []
        for c in range(PAIR):
            rows_c = slice(c * CH, (c + 1) * CH)
            parts.append(_dn_chunk(q_ref[rows_c, :], k_ref[rows_c, :], v_ref[rows_c, :], bc_ref[rows_c, :],
                                   br_ref[c], incl, strict))
        tinvs = _tri_inv([p[6] for p in parts], blk, eye)
        for c, (q, k, v, b_c, dm, kk, a, ed, rhs, qk_n, ekd, gl) in enumerate(parts):
            tinv = tinvs[c]
            ti_out[c] = tinv
            sol = _dot3(tinv, rhs)
            prep[c, 0] = sol[:, :DN_D]
            prep[c, 1] = sol[:, DN_D:]
            prep[c, 2] = q * ed
            prep[c, 3] = k * ekd
            prep_qk[c] = qk_n
            prep_gl[c] = jnp.concatenate([jnp.broadcast_to(t, (1, 128)) for t in gl]
                                         + [jnp.zeros((8 - DN_H, 128), F32)], axis=0)

    assert nch % PAIR == 0
    npair = nch // PAIR
    last = npair - 1
    return pl.pallas_call(
        body, name="dn_fwd", interpret=False,
        out_shape=[jax.ShapeDtypeStruct((rows, DN_DIM), F32),
                   jax.ShapeDtypeStruct((nch, DN_H, DN_D, DN_D), F32),
                   jax.ShapeDtypeStruct((nch, HB, HB), F32)],
        grid=(npair + 1,),
        in_specs=[pl.BlockSpec((PAIR * CH, DN_DIM), lambda n: (jnp.minimum(n, last), 0)),
                  pl.BlockSpec((PAIR * CH, DN_DIM), lambda n: (jnp.minimum(n, last), 1)),
                  pl.BlockSpec((PAIR * CH, DN_DIM), lambda n: (jnp.minimum(n, last), 2)),
                  pl.BlockSpec((PAIR * CH, 128), lambda n: (jnp.minimum(n, last), 0)),
                  pl.BlockSpec((PAIR, 8, CH), lambda n: (jnp.minimum(n, last), 0, 0))],
        out_specs=[pl.BlockSpec((PAIR * CH, DN_DIM), lambda n: (jnp.maximum(n - 1, 0), 0)),
                   pl.BlockSpec((PAIR, DN_H, DN_D, DN_D), lambda n: (jnp.maximum(n - 1, 0), 0, 0, 0)),
                   pl.BlockSpec((PAIR, HB, HB), lambda n: (jnp.minimum(n, last), 0, 0))],
        scratch_shapes=[pltpu.VMEM((DN_H, DN_D, DN_D), F32), pltpu.VMEM((PAIR, 4, HB, DN_D), F32),
                        pltpu.VMEM((PAIR, HB, HB), F32), pltpu.VMEM((PAIR, 8, 128), F32)],
        compiler_params=_params(("arbitrary",)),
    )(qkv_n, qkv_n, qkv_n, bgcol, bgrow)


def dn_bwd(qkv_n, bgcol, bgrow, s_all, ti_all, do):
    rows = qkv_n.shape[0]
    nch = rows // CH

    def body(q_ref, k_ref, v_ref, bc_ref, br_ref, s_ref, ti_ref, do_ref, dq_ref, dk_ref, dv_ref, dbg_ref, ds_scr):
        n = pl.program_id(0)

        @pl.when(n == 0)
        def _():
            ds_scr[...] = jnp.zeros(ds_scr.shape, F32)

        incl, strict, upper, _, _ = _dn_masks()
        rsum = lambda t: jnp.sum(t, axis=1, keepdims=True)
        rows_of = [slice(h * CH, (h + 1) * CH) for h in range(DN_H)]
        heads_of = lambda f: jnp.concatenate([f(h, rs) for h, rs in enumerate(rows_of)], axis=0)
        cs = []
        for c in reversed(range(PAIR)):
            rc = slice(c * CH, (c + 1) * CH)
            q, k, v, b_c, dm, kk, a, ed, rhs, qk, ekd, gl = _dn_chunk(
                q_ref[rc, :], k_ref[rc, :], v_ref[rc, :], bc_ref[rc, :], br_ref[c], incl, strict)
            cs.append(dict(rc=rc, q=q, k=k, v=v, b_c=b_c, dm=dm, kk=kk, a=a, ed=ed, rhs=rhs, qk=qk, ekd=ekd, gl=gl,
                           tinv=ti_ref[c], g=_stack(do_ref[rc, :]), s=[s_ref[c, h] for h in range(DN_H)]))
        for t in cs:
            t["sol"] = _dot3(t["tinv"], t["rhs"])
        for t in cs:
            t["u"], t["w"] = t["sol"][:, :DN_D], t["sol"][:, DN_D:]
            t["qd"], t["kd"] = t["q"] * t["ed"], t["k"] * t["ekd"]
            t["v_new"] = heads_of(lambda h, rs: t["u"][rs] - _dot1(t["w"][rs], t["s"][h]))
            t["dv0"] = _dot1(t["qk"], t["g"], 0, 0)
            t["ds0"] = [_dot1(t["qd"][rs], t["g"][rs], 0, 0) for rs in rows_of]
            t["dqd"] = heads_of(lambda h, rs: _dot1(t["g"][rs], t["s"][h], 1, 1))
        for t in cs:
            t["dqk"] = _dot1(t["g"], t["v_new"], 1, 1)
        ds = [ds_scr[h] for h in range(DN_H)]
        for t in cs:
            t["ds"] = ds
            t["dv_new"] = t["dv0"] + heads_of(lambda h, rs: _dot1(t["kd"][rs], ds[h]))
            ds = [t["ds0"][h] + t["gl"][h] * ds[h] - _dot1(t["w"][rs], t["dv_new"][rs], 0, 0)
                  for h, rs in enumerate(rows_of)]
        for h in range(DN_H):
            ds_scr[h] = ds[h]
        for t in cs:
            t["dkd"] = heads_of(lambda h, rs: _dot1(t["v_new"][rs], t["ds"][h], 1, 1))
            dw = heads_of(lambda h, rs: -_dot1(t["dv_new"][rs], t["s"][h], 1, 1))
            t["dsol"] = jnp.concatenate([t["dv_new"], dw], axis=1)
        for t in cs:
            t["drhs"] = _dot3(t["tinv"], t["dsol"], 0, 0)
        for t in cs:
            t["da"] = jnp.where(strict, -_dot1(t["drhs"], t["sol"], 1, 1), 0.0)
        rowi = lax.broadcasted_iota(jnp.int32, (CH, 1), 0)
        lane = lax.broadcasted_iota(jnp.int32, (CH, 128), 1)
        for t in cs:
            q, k, v, b_c, dm, ed, da, dqk = t["q"], t["k"], t["v"], t["b_c"], t["dm"], t["ed"], t["da"], t["dqk"]
            drhs_u, drhs_w = t["drhs"][:, :DN_D], t["drhs"][:, DN_D:]
            s2 = rsum(drhs_w * k)
            dbeta = rsum(drhs_u * v) + s2 * ed + rsum(da * t["kk"] * dm)
            dkk = da * b_c * dm
            dqkr = dqk * dm
            mmat = da * t["a"] + dqk * t["qk"]
            tmp = rsum(t["dkd"] * t["kd"])
            dd = (s2 * b_c * ed + rsum(mmat) - _dot3(mmat, jnp.ones((HB, 128), F32), 0, 0)[:, :1]
                  + rsum(t["dqd"] * t["qd"]) - tmp)
            last = []
            for h, rs in enumerate(rows_of):
                dgl = jnp.sum(rsum(t["s"][h] * t["ds"][h]), axis=0, keepdims=True)
                dd_last = jnp.sum(tmp[rs], axis=0, keepdims=True) + dgl * t["gl"][h]
                last.append(jnp.where(rowi == CH - 1, dd_last, 0.0))
            dd = dd + jnp.concatenate(last, axis=0)
            rc = t["rc"]
            dq_ref[rc, :] = _unstack(_dot1(dqkr, k) + t["dqd"] * ed)
            dk_ref[rc, :] = _unstack(drhs_w * (b_c * ed) + _dot1(dkk, k) + _dot1(dkk, k, 0, 0) + _dot1(dqkr, q, 0, 0)
                                     + t["dkd"] * t["ekd"])
            dv_ref[rc, :] = _unstack(drhs_u * b_c)
            dg = _dot01(upper.astype(F32), jnp.broadcast_to(dd, (HB, 128)))[:, :1]
            out = jnp.zeros((CH, 128), F32)
            for h, rs in enumerate(rows_of):
                out = out + jnp.where(lane == h, dbeta[rs], 0.0) + jnp.where(lane == 4 + h, dg[rs], 0.0)
            dbg_ref[rc, :] = out

    assert nch % PAIR == 0
    npair = nch // PAIR
    rev = lambda n: npair - 1 - n
    blk = PAIR * CH
    return pl.pallas_call(
        body, name="dn_bwd", interpret=False,
        out_shape=[jax.ShapeDtypeStruct((rows, DN_DIM), F32)] * 3 + [jax.ShapeDtypeStruct((rows, 128), F32)],
        grid=(npair,),
        in_specs=[pl.BlockSpec((blk, DN_DIM), lambda n: (rev(n), 0)),
                  pl.BlockSpec((blk, DN_DIM), lambda n: (rev(n), 1)),
                  pl.BlockSpec((blk, DN_DIM), lambda n: (rev(n), 2)),
                  pl.BlockSpec((blk, 128), lambda n: (rev(n), 0)),
                  pl.BlockSpec((PAIR, 8, CH), lambda n: (rev(n), 0, 0)),
                  pl.BlockSpec((PAIR, DN_H, DN_D, DN_D), lambda n: (rev(n), 0, 0, 0)),
                  pl.BlockSpec((PAIR, HB, HB), lambda n: (rev(n), 0, 0)),
                  pl.BlockSpec((blk, DN_DIM), lambda n: (rev(n), 0))],
        out_specs=[pl.BlockSpec((blk, DN_DIM), lambda n: (rev(n), 0))] * 3 + [pl.BlockSpec((blk, 128), lambda n: (rev(n), 0))],
        scratch_shapes=[pltpu.VMEM((DN_H, DN_D, DN_D), F32)],
        compiler_params=_params(("arbitrary",)),
    )(qkv_n, qkv_n, qkv_n, bgcol, bgrow, s_all, ti_all, do)


def _swa_valid(n):
    c3 = lax.broadcasted_iota(jnp.int32, (NKEY, 4 * BLK), 0)
    r = lax.broadcasted_iota(jnp.int32, (NKEY, 4 * BLK), 1) % BLK
    prev0 = N_META + BLK
    c = jnp.where(c3 < N_META, PAD + c3, jnp.where(c3 < prev0, c3 - N_META, c3 - prev0))
    lo = jnp.where(c3 < N_META, 0, jnp.where(c3 < prev0, r + 1 + jnp.where(n >= 2, 0, BLK), 0))
    hi = jnp.where(c3 < N_META, r + jnp.where(n >= 1, BLK, 0),
                   jnp.where(c3 < prev0, BLK, r - jnp.where(n >= 1, 0, BLK)))
    return jnp.logical_and(c >= lo, c <= hi)


def _swa_probs(qs, kcats, valid, sinks):
    s = [jnp.where(valid, _dot(kc, q, 1, 1), -1e30) for q, kc in zip(qs, kcats)]
    m = [jnp.maximum(jnp.max(t, axis=0, keepdims=True), sk) for t, sk in zip(s, sinks)]
    e = [jnp.where(valid, jnp.exp(t - mx), 0.0) for t, mx in zip(s, m)]
    es = [jnp.exp(sk - mx) for sk, mx in zip(sinks, m)]
    inv = [1.0 / (jnp.sum(t, axis=0, keepdims=True) + u) for t, u in zip(e, es)]
    return [t * i for t, i in zip(e, inv)], [u * i for u, i in zip(es, inv)]


def _swa_group(q_ref, sk_ref, h):
    q4 = jnp.concatenate([q_ref[4 * h + g] for g in range(4)], axis=0)
    sink4 = jnp.concatenate([jnp.full((1, BLK), sk_ref[4 * h + g], F32) for g in range(4)], axis=1)
    return q4, sink4


def _swa_specs():
    q = pl.BlockSpec((SWA_H, BLK, SWA_D), lambda n: (0, n, 0))
    km = pl.BlockSpec((SWA_KV, N_META, SWA_D), lambda n: (0, PAD // N_META, 0))
    kp = pl.BlockSpec((SWA_KV, BLK, SWA_D), lambda n: (0, jnp.maximum(n - 1, 0), 0))
    kc = pl.BlockSpec((SWA_KV, BLK, SWA_D), lambda n: (0, n, 0))
    return [q, km, kp, kc, km, kp, kc]


def swa_fwd(qh, kh, vh, sinks):
    rows = qh.shape[1]
    nb = rows // BLK

    def body(q_ref, km, kp, kc, vm, vp, vc, sk_ref, o_ref):
        n = pl.program_id(0)
        valid = _swa_valid(n)
        kcats = [jnp.concatenate([km[h], kp[h], kc[h]], axis=0) for h in range(SWA_KV)]
        vcats = [jnp.concatenate([vm[h], vp[h], vc[h]], axis=0) for h in range(SWA_KV)]
        qs, sinks4 = zip(*[_swa_group(q_ref, sk_ref, h) for h in range(SWA_KV)])
        ps, _ = _swa_probs(qs, kcats, valid, sinks4)
        o4s = [_dot(p.astype(BF16), vc_, 0, 0) for p, vc_ in zip(ps, vcats)]
        o_ref[...] = jnp.concatenate([o4[g * BLK:(g + 1) * BLK] for o4 in o4s for g in range(4)],
                                     axis=1).astype(BF16)

    return pl.pallas_call(
        body, name="swa_fwd", interpret=False,
        out_shape=jax.ShapeDtypeStruct((rows, SWA_H * SWA_D), BF16),
        grid=(nb,),
        in_specs=_swa_specs() + [pl.BlockSpec(memory_space=pltpu.SMEM)],
        out_specs=pl.BlockSpec((BLK, SWA_H * SWA_D), lambda n: (n, 0)),
        compiler_params=_params(("parallel",)),
    )(qh, kh, kh, kh, vh, vh, vh, sinks)


def swa_bwd(qh, kh, vh, sinks, do):
    rows = qh.shape[1]
    nb = rows // BLK

    def body(q_ref, km, kp, kc, vm, vp, vc, do_ref, sk_ref, dq_ref, dk_ref, dv_ref, dsk_ref):
        n = pl.program_id(0)

        @pl.when(n == 0)
        def _():
            dk_ref[...] = jnp.zeros(dk_ref.shape, F32)
            dv_ref[...] = jnp.zeros(dv_ref.shape, F32)

        valid = _swa_valid(n)
        g_all = do_ref[...]
        rowi = lax.broadcasted_iota(jnp.int32, (SWA_H, 128), 0)
        dsk = jnp.zeros((SWA_H, 128), F32)
        pm = pl.multiple_of(jnp.maximum(n - 1, 0) * BLK, BLK)
        pc = pl.multiple_of(n * BLK, BLK)
        hs = range(SWA_KV)
        kcats = [jnp.concatenate([km[h], kp[h], kc[h]], axis=0) for h in hs]
        vcats = [jnp.concatenate([vm[h], vp[h], vc[h]], axis=0) for h in hs]
        qs, sinks4 = zip(*[_swa_group(q_ref, sk_ref, h) for h in hs])
        g4s = [jnp.concatenate([g_all[:, (4 * h + g) * SWA_D:(4 * h + g + 1) * SWA_D] for g in range(4)], axis=0)
               for h in hs]
        ps, pss = _swa_probs(qs, kcats, valid, sinks4)
        dps = [_dot(vc_, g4, 1, 1) for vc_, g4 in zip(vcats, g4s)]
        deltas = [jnp.sum(p * dp, axis=0, keepdims=True) for p, dp in zip(ps, dps)]
        dss = [(p * (dp - dl)).astype(BF16) for p, dp, dl in zip(ps, dps, deltas)]
        dq4s = [_dot(ds, kc_, 0, 0) for ds, kc_ in zip(dss, kcats)]
        dkcs = [_dot(ds, q4) for ds, q4 in zip(dss, qs)]
        dvcs = [_dot(p.astype(BF16), g4) for p, g4 in zip(ps, g4s)]
        for h in hs:
            t = pss[h] * deltas[h]
            for g in range(4):
                dq_ref[4 * h + g] = dq4s[h][g * BLK:(g + 1) * BLK]
                part = -jnp.sum(t[:, g * BLK:(g + 1) * BLK], axis=1, keepdims=True)
                dsk = dsk + jnp.where(rowi == 4 * h + g, part, 0.0)
            lanes = slice(h * SWA_D, (h + 1) * SWA_D)
            for ref, val in ((dk_ref, dkcs[h]), (dv_ref, dvcs[h])):
                ref[PAD:BLK, lanes] += val[0:N_META]
                ref[pl.ds(pm, BLK), lanes] += val[N_META:N_META + BLK]
                ref[pl.ds(pc, BLK), lanes] += val[N_META + BLK:]
        dsk_ref[0] = dsk

    return pl.pallas_call(
        body, name="swa_bwd", interpret=False,
        out_shape=[jax.ShapeDtypeStruct((SWA_H, rows, SWA_D), F32),
                   jax.ShapeDtypeStruct((rows, SWA_KV * SWA_D), F32),
                   jax.ShapeDtypeStruct((rows, SWA_KV * SWA_D), F32),
                   jax.ShapeDtypeStruct((nb, SWA_H, 128), F32)],
        grid=(nb,),
        in_specs=_swa_specs() + [pl.BlockSpec((BLK, SWA_H * SWA_D), lambda n: (n, 0)),
                                 pl.BlockSpec(memory_space=pltpu.SMEM)],
        out_specs=[pl.BlockSpec((SWA_H, BLK, SWA_D), lambda n: (0, n, 0)),
                   pl.BlockSpec((rows, SWA_KV * SWA_D), lambda n: (0, 0)),
                   pl.BlockSpec((rows, SWA_KV * SWA_D), lambda n: (0, 0)),
                   pl.BlockSpec((1, SWA_H, 128), lambda n: (n, 0, 0))],
        compiler_params=_params(("arbitrary",)),
    )(qh, kh, kh, kh, vh, vh, vh, do, sinks)


QK_W = (SWA_H + SWA_KV) * SWA_D


def _head_mean(t):
    r = lax.broadcasted_iota(jnp.int32, (128, 128), 0) // SWA_D
    c = lax.broadcasted_iota(jnp.int32, (128, 128), 1) // SWA_D
    blk = jnp.where(r == c, 1.0 / SWA_D, 0.0).astype(BF16)
    out = []
    for i in range(t.shape[1] // 128):
        hi, lo = _split(t[:, 128 * i:128 * (i + 1)])
        out.append(_dot(hi, blk) + _dot(lo, blk))
    return jnp.concatenate(out, axis=1)


def _qk_scales(qw, kw):
    scale = SWA_D ** -0.5
    wt = jnp.concatenate([jnp.tile(qw.astype(F32) * scale, (1, SWA_H)), jnp.tile(kw.astype(F32), (1, SWA_KV))], axis=1)
    st = jnp.concatenate([jnp.full((1, SWA_H * SWA_D), scale, F32), jnp.ones((1, SWA_KV * SWA_D), F32)], axis=1)
    return wt, st


def qknorm_fwd(qkv, qw, kw):
    rows = qkv.shape[0]
    tr = _pick(rows, (384, 128))
    wt, _ = _qk_scales(qw, kw)

    def fn(i, x, w):
        xq = x[:, :QK_W]
        y = xq * lax.rsqrt(_head_mean(xq * xq) + EPS) * w
        head = lambda t, j: t[:, j * SWA_D:(j + 1) * SWA_D][None]
        qo = jnp.concatenate([head(y, j) for j in range(SWA_H)], axis=0)
        ko = jnp.concatenate([head(y, SWA_H + j) for j in range(SWA_KV)], axis=0)
        vo = jnp.concatenate([head(x, SWA_H + SWA_KV + j) for j in range(SWA_KV)], axis=0)
        return qo, ko, vo

    hm = lambda nh: ((nh, rows, SWA_D), BF16, (nh, tr, SWA_D), lambda i: (0, i, 0), "r3")
    return rowwise(fn, [cols(qkv, tr), whole(wt)], [hm(SWA_H), hm(SWA_KV), hm(SWA_KV)],
                   steps=rows // tr, name="qknorm_fwd")


def qknorm_bwd(qkv, qw, kw, dqh, dk, dv):
    rows = qkv.shape[0]
    tr = _pick(rows, (384, 128))
    wt, st = _qk_scales(qw, kw)

    def fn(i, x, w, sc, dq, dkv, dvv):
        xq = x[:, :QK_W]
        dy = jnp.concatenate([dq[j] for j in range(SWA_H)] + [dkv], axis=1)
        r = lax.rsqrt(_head_mean(xq * xq) + EPS)
        xh = xq * r
        gw = dy * w
        dx = r * (gw - xh * _head_mean(gw * xh))
        return jnp.concatenate([dx, dvv], axis=1), jnp.sum(dy * sc * xh, axis=0, keepdims=True)

    dqkv, dw = rowwise(fn, [cols(qkv, tr), whole(wt), whole(st), heads(dqh, tr), cols(dk, tr), cols(dv, tr)],
                       [out2d(rows, 1536, BF16, tr)], steps=rows // tr, name="qknorm_bwd", accs=[((1, QK_W), F32)])
    dw = dw.reshape(SWA_H + SWA_KV, SWA_D)
    return dqkv, jnp.sum(dw[:SWA_H], axis=0, keepdims=True), jnp.sum(dw[SWA_H:], axis=0, keepdims=True)


def _place():
    return lax.axis_index("x"), lax.axis_index("y"), lax.axis_index("c")


ANY = pl.BlockSpec(memory_space=pl.ANY)


def _rcopy(ssem, rsem, k, src, dst, to):
    return pltpu.make_async_remote_copy(src_ref=src, dst_ref=dst, send_sem=ssem.at[k], recv_sem=rsem.at[k],
                                        device_id=to, device_id_type=MESH)


def gather_weights(shards, small):
    n = len(shards)
    halves = [t.shape[0] // 2 for t in shards]

    def body(*refs):
        s_refs, small_ref = refs[:n], refs[n]
        o_refs, osmall = refs[n + 1:2 * n + 1], refs[2 * n + 1]
        ssem, rsem, lsem = refs[2 * n + 2:]
        x, y, c = _place()
        me = 2 * x + y
        chips = [(1 - x, y), (x, 1 - y), (1 - x, 1 - y)]

        def half(k, s, hh):
            return o_refs[k].at[s, pl.ds(hh * halves[k], halves[k]), :]

        loc = pltpu.make_async_copy(small_ref, osmall.at[me], lsem)
        loc.start()
        sends = []
        for k in range(n):
            for j, (px, py) in enumerate(chips):
                sends.append(_rcopy(ssem, rsem, 6 * k + j, s_refs[k].at[pl.ds(c * halves[k], halves[k]), :],
                                    half(k, me, c), (px, py, c)))
        for j, (px, py) in enumerate(chips):
            sends.append(_rcopy(ssem, rsem, 6 * n + j, small_ref, osmall.at[me], (px, py, c)))
        for cp in sends:
            cp.start()
        for k in range(n):
            for j, (px, py) in enumerate(chips):
                s = 2 * px + py
                _rcopy(ssem, rsem, 6 * k + j, half(k, s, c), half(k, s, c), (x, y, c)).wait_recv()
                fwd = _rcopy(ssem, rsem, 6 * k + 3 + j, half(k, s, c), half(k, s, c), (x, y, 1 - c))
                fwd.start()
                sends.append(fwd)
        for k in range(n):
            for j, (px, py) in enumerate(chips):
                s = 2 * px + py
                _rcopy(ssem, rsem, 6 * k + 3 + j, half(k, s, 1 - c), half(k, s, 1 - c), (x, y, c)).wait_recv()
        for j, (px, py) in enumerate(chips):
            s = 2 * px + py
            _rcopy(ssem, rsem, 6 * n + j, osmall.at[s], osmall.at[s], (x, y, c)).wait_recv()
        for cp in sends:
            cp.wait_send()
        loc.wait()

    res = pl.pallas_call(
        body, name="gather_weights", interpret=False,
        out_shape=[jax.ShapeDtypeStruct((4,) + t.shape, t.dtype) for t in shards]
        + [jax.ShapeDtypeStruct((4, SW_ROWS, 1024), F32)],
        in_specs=[ANY] * (n + 1), out_specs=[ANY] * (n + 1),
        scratch_shapes=[pltpu.SemaphoreType.DMA((6 * n + 3,)), pltpu.SemaphoreType.DMA((6 * n + 3,)),
                        pltpu.SemaphoreType.DMA],
    )(*shards, small)
    return res[:n], res[n]


def _handshake(peers):
    barrier = pltpu.get_barrier_semaphore()
    for peer in peers:
        pl.semaphore_signal(barrier, inc=1, device_id=peer, device_id_type=MESH)
    pl.semaphore_wait(barrier, len(peers))


def gather_weights_beside(shards, cid, name):
    n = len(shards)
    halves = [t.shape[0] // 2 for t in shards]

    def body(*refs):
        s_refs, o_refs, ssem, rsem = refs[:n], refs[n:2 * n], refs[2 * n], refs[2 * n + 1]
        x, y, c = _place()
        me = 2 * x + y
        chips = [(1 - x, y), (x, 1 - y), (1 - x, 1 - y)]
        _handshake([(px, py, c) for px, py in chips] + [(x, y, 1 - c)])

        def half(k, s, hh):
            return o_refs[k].at[s, pl.ds(hh * halves[k], halves[k]), :]

        sends = []
        for k in range(n):
            for j, (px, py) in enumerate(chips):
                sends.append(_rcopy(ssem, rsem, 6 * k + j, s_refs[k].at[pl.ds(c * halves[k], halves[k]), :],
                                    half(k, me, c), (px, py, c)))
        for cp in sends:
            cp.start()
        for k in range(n):
            for j, (px, py) in enumerate(chips):
                s = 2 * px + py
                _rcopy(ssem, rsem, 6 * k + j, half(k, s, c), half(k, s, c), (x, y, c)).wait_recv()
                fwd = _rcopy(ssem, rsem, 6 * k + 3 + j, half(k, s, c), half(k, s, c), (x, y, 1 - c))
                fwd.start()
                sends.append(fwd)
        for k in range(n):
            for j, (px, py) in enumerate(chips):
                s = 2 * px + py
                _rcopy(ssem, rsem, 6 * k + 3 + j, half(k, s, 1 - c), half(k, s, 1 - c), (x, y, c)).wait_recv()
        for cp in sends:
            cp.wait_send()

    return pl.kernel(
        body, name=name,
        out_type=[jax.ShapeDtypeStruct((4,) + t.shape, t.dtype) for t in shards],
        mesh=plsc.ScalarSubcoreMesh(axis_name="sequencer", num_cores=1),
        scratch_types=[pltpu.SemaphoreType.DMA((6 * n,)), pltpu.SemaphoreType.DMA((6 * n,))],
        compiler_params=pltpu.CompilerParams(collective_id=cid),
    )(*shards)


def swap_halves(gs, *, name):
    n = len(gs)

    def body(*refs):
        g_refs, o_refs, ssem, rsem = refs[:n], refs[n:2 * n], refs[2 * n], refs[2 * n + 1]
        x, y, c = _place()
        cps = []
        for k in range(n):
            hk = g_refs[k].shape[1] // 2
            cps.append(_rcopy(ssem, rsem, k, g_refs[k].at[:, pl.ds((1 - c) * hk, hk), :], o_refs[k], (x, y, 1 - c)))
        for cp in cps:
            cp.start()
        for cp in cps:
            cp.wait()

    return pl.pallas_call(
        body, name=name, interpret=False,
        out_shape=[jax.ShapeDtypeStruct((4, t.shape[1] // 2, t.shape[2]), t.dtype) for t in gs],
        in_specs=[ANY] * n, out_specs=[ANY] * n,
        scratch_shapes=[pltpu.SemaphoreType.DMA((n,)), pltpu.SemaphoreType.DMA((n,))],
    )(*gs)


def _sum_rows(hk):
    return _pick(hk, (512, 352, 256, 128))


def pair_sum(g, other, c_idx, *, name):
    _, hk, width = other.shape
    tr = _sum_rows(hk)
    nbk = hk // tr

    def body(c_ref, g_ref, o_ref, out_ref):
        out_ref[...] = (g_ref[...].astype(F32) + o_ref[...].astype(F32)).astype(BF16)

    return pl.pallas_call(
        body, name=name, interpret=False,
        out_shape=jax.ShapeDtypeStruct((4, hk, width), BF16),
        grid_spec=pltpu.PrefetchScalarGridSpec(
            num_scalar_prefetch=1, grid=(4, nbk),
            in_specs=[pl.BlockSpec((1, tr, width), lambda s, i, c_ref: (s, c_ref[0] * nbk + i, 0)),
                      pl.BlockSpec((1, tr, width), lambda s, i, c_ref: (s, i, 0))],
            out_specs=pl.BlockSpec((1, tr, width), lambda s, i, c_ref: (s, i, 0))),
        compiler_params=_params(("parallel", "parallel")),
    )(c_idx, g, other)


def chip_sum(p, got, idx, *, name):
    _, hk, width = got.shape
    tr = _sum_rows(hk)
    nbk = hk // tr

    def body(idx_ref, p_ref, g_ref, out_ref):
        acc = p_ref[0].astype(F32)
        for j in range(3):
            acc = acc + g_ref[j].astype(F32)
        out_ref[0] = acc

    return pl.pallas_call(
        body, name=name, interpret=False,
        out_shape=jax.ShapeDtypeStruct((2, hk, width), F32),
        grid_spec=pltpu.PrefetchScalarGridSpec(
            num_scalar_prefetch=1, grid=(nbk,),
            in_specs=[pl.BlockSpec((1, tr, width), lambda i, idx_ref: (idx_ref[0], i, 0)),
                      pl.BlockSpec((3, tr, width), lambda i, idx_ref: (0, i, 0))],
            out_specs=pl.BlockSpec((1, tr, width), lambda i, idx_ref: (idx_ref[1], i, 0))),
        compiler_params=_params(("parallel",)),
    )(idx, p, got)


def join_halves(qs):
    n = len(qs)

    def body(*refs):
        q_refs, o_refs, ssem, rsem = refs[:n], refs[n:2 * n], refs[2 * n], refs[2 * n + 1]
        x, y, c = _place()
        cps = [_rcopy(ssem, rsem, k, q_refs[k].at[c], o_refs[k].at[c], (x, y, 1 - c)) for k in range(n)]
        for cp in cps:
            cp.start()
        for k in range(n):
            _rcopy(ssem, rsem, k, q_refs[k].at[c], o_refs[k].at[1 - c], (x, y, 1 - c)).wait_recv()
        for cp in cps:
            cp.wait_send()

    return pl.pallas_call(
        body, name="join_halves", interpret=False,
        out_shape=[jax.ShapeDtypeStruct(t.shape, t.dtype) for t in qs],
        in_specs=[ANY] * n, out_specs=[ANY] * n, input_output_aliases={k: k for k in range(n)},
        scratch_shapes=[pltpu.SemaphoreType.DMA((n,)), pltpu.SemaphoreType.DMA((n,))],
    )(*qs)


def scatter_chips_beside(ps, cid, name):
    n = len(ps)

    def body(*refs):
        p_refs, o_refs, ssem, rsem = refs[:n], refs[n:2 * n], refs[2 * n], refs[2 * n + 1]
        x, y, c = _place()
        chips = [(1 - x, y), (x, 1 - y), (1 - x, 1 - y)]
        _handshake([(px, py, c) for px, py in chips])
        cps = [_rcopy(ssem, rsem, 3 * k + j, p_refs[k].at[2 * px + py], o_refs[k].at[j], (px, py, c))
               for k in range(n) for j, (px, py) in enumerate(chips)]
        for cp in cps:
            cp.start()
        for cp in cps:
            cp.wait()

    return pl.kernel(
        body, name=name, out_type=[jax.ShapeDtypeStruct((3,) + t.shape[1:], t.dtype) for t in ps],
        mesh=plsc.ScalarSubcoreMesh(axis_name="sequencer", num_cores=1),
        scratch_types=[pltpu.SemaphoreType.DMA((3 * n,)), pltpu.SemaphoreType.DMA((3 * n,))],
        compiler_params=pltpu.CompilerParams(collective_id=cid),
    )(*ps)


def reduce_begin(gs, names, c_idx, cid, tag):
    others = swap_halves(gs, name=f"swap_halves_{tag}")
    pairs = [pair_sum(g, o, c_idx, name=f"pair_sum_{nm}") for g, o, nm in zip(gs, others, names)]
    return pairs, scatter_chips_beside(pairs, cid, f"scatter_chips_{tag}")


def reduce_end(pairs, gots, names, idx):
    mine = [chip_sum(p, g, idx, name=f"chip_sum_{nm}") for p, g, nm in zip(pairs, gots, names)]
    return [q.reshape(2 * q.shape[1], q.shape[2]) for q in join_halves(mine)]


def gather_small(v):
    def body(v_ref, o_ref, ssem, rsem, lsem):
        x, y, c = _place()
        peers = []
        for k in range(1, 8):
            fx, fy, fc = (k >> 2) & 1, (k >> 1) & 1, k & 1
            peers.append((1 - x if fx else x, 1 - y if fy else y, 1 - c if fc else c))
        _handshake(peers)
        loc = pltpu.make_async_copy(v_ref, o_ref.at[4 * x + 2 * y + c], lsem)
        loc.start()
        cps = []
        for k, (px, py, pc) in enumerate(peers):
            cps.append((pltpu.make_async_remote_copy(
                src_ref=v_ref, dst_ref=o_ref.at[4 * x + 2 * y + c], send_sem=ssem.at[k], recv_sem=rsem.at[k],
                device_id=(px, py, pc), device_id_type=MESH), 4 * px + 2 * py + pc))
        for cp, _ in cps:
            cp.start()
        for k, (cp, peer) in enumerate(cps):
            pltpu.make_async_remote_copy(
                src_ref=v_ref, dst_ref=o_ref.at[peer], send_sem=ssem.at[k], recv_sem=rsem.at[k],
                device_id=(x, y, c), device_id_type=MESH).wait_recv()
        for cp, _ in cps:
            cp.wait_send()
        loc.wait()

    return pl.kernel(
        body, name="gather_small", out_type=jax.ShapeDtypeStruct((8, SV_ROWS, 1024), F32),
        mesh=plsc.ScalarSubcoreMesh(axis_name="sequencer", num_cores=1),
        scratch_types=[pltpu.SemaphoreType.DMA((7,)), pltpu.SemaphoreType.DMA((7,)), pltpu.SemaphoreType.DMA],
        compiler_params=pltpu.CompilerParams(collective_id=6),
    )(v)


def sum_slots(a):
    def fn(i, t):
        acc = t[0]
        for k in range(1, 8):
            acc = acc + t[k]
        return acc

    return rowwise(fn, [whole(a)], [((SV_ROWS, 1024), F32, (SV_ROWS, 1024), lambda i: (0, 0), "w")], steps=1,
                   name="sum_slots")[0]


def _head_rms(x, nw):
    xs, rs = [], []
    for h in range(DN_H):
        xh = x[:, h * DN_D:(h + 1) * DN_D]
        r = lax.rsqrt(jnp.mean(xh * xh, axis=1, keepdims=True) + EPS)
        xs.append(xh * r)
        rs.append(r)
    return xs, rs


def bg_fwd(p, alog, dtb):
    rows = p.shape[0]
    tr = _pick(rows, (384, 128))

    def fn(i, x, al, dt):
        lane = lax.broadcasted_iota(jnp.int32, x.shape, 1)
        row = i + lax.broadcasted_iota(jnp.int32, x.shape, 0)
        g = -jnp.exp(al) * _softplus(x + dt)
        out = jnp.where(lane < 4, _sigmoid(x), jnp.where(lane < 8, g, 0.0))
        return jnp.where(row >= PAD, out, 0.0)

    return rowwise(fn, [cols(p, tr, 128, BG0 // 128), whole(alog), whole(dtb)], [out2d(rows, 128, F32, tr)],
                   steps=rows // tr, name="bg_fwd")[0]


def bg_bwd(p, alog, dtb, dbg):
    rows = p.shape[0]
    tr = _pick(rows, (384, 128))

    def fn(i, x, al, dt, g_in):
        lane = lax.broadcasted_iota(jnp.int32, x.shape, 1)
        row = i + lax.broadcasted_iota(jnp.int32, x.shape, 0)
        live = row >= PAD
        is_b = jnp.logical_and(live, lane < 4)
        is_g = jnp.logical_and(live, jnp.logical_and(lane >= 4, lane < 8))
        beta = _sigmoid(x)
        ea = jnp.exp(al)
        g = -ea * _softplus(x + dt)
        dalpha = jnp.where(is_g, g_in * (-ea) * _sigmoid(x + dt), 0.0)
        dx = jnp.where(is_b, g_in * beta * (1.0 - beta), dalpha)
        dal = jnp.sum(jnp.where(is_g, g_in * g, 0.0), axis=0, keepdims=True)
        return jnp.concatenate([dx, jnp.zeros(x.shape, F32)], axis=1), dal, jnp.sum(dalpha, axis=0, keepdims=True)

    return rowwise(fn, [cols(p, tr, 128, BG0 // 128), whole(alog), whole(dtb), cols(dbg, tr)],
                   [out2d(rows, 256, BF16, tr)], steps=rows // tr, name="bg_bwd",
                   accs=[((1, 128), F32), ((1, 128), F32)])


def dn_qkv_post(j, y):
    xs = _silu(y)
    sc = jnp.where(j == 0, DN_D ** -0.5, 1.0)
    outs = []
    for h in range(DN_H):
        xh = xs[:, h * DN_D:(h + 1) * DN_D]
        r = lax.rsqrt(jnp.sum(xh * xh, axis=1, keepdims=True) + EPS)
        outs.append(jnp.where(j < 2, xh * r * sc, xh))
    return jnp.concatenate(outs, axis=1), y


def dn_qkv_bwd(cq, dq, dk, dv):
    rows = cq.shape[0]
    tr = _pick(rows, (384, 128))

    def fn(i, c0, c1, c2, g0, g1, g2):
        pieces = []
        for kind, (cv, g) in enumerate(((c0, g0), (c1, g1), (c2, g2))):
            xs = _silu(cv)
            if kind < 2:
                sc = DN_D ** -0.5 if kind == 0 else 1.0
                ds = []
                for h in range(DN_H):
                    sl = slice(h * DN_D, (h + 1) * DN_D)
                    xh, gh = xs[:, sl], g[:, sl]
                    r = lax.rsqrt(jnp.sum(xh * xh, axis=1, keepdims=True) + EPS)
                    xn = xh * r
                    ds.append(sc * r * (gh - xn * jnp.sum(gh * xn, axis=1, keepdims=True)))
                dxs = jnp.concatenate(ds, axis=1)
            else:
                dxs = g
            pieces.append(dxs * _dsilu(cv))
        return jnp.concatenate(pieces, axis=1)

    ins = [cols(cq, tr, DN_DIM, k) for k in range(3)] + [cols(t, tr) for t in (dq, dk, dv)]
    return rowwise(fn, ins, [out2d(rows, 3 * DN_DIM, F32, tr)], steps=rows // tr, name="dn_qkv_bwd")[0]


def dn_out_fwd(o, p, nw):
    rows = o.shape[0]
    tr = _pick(rows, (384, 128))

    def fn(i, ov, z, w):
        xs, _ = _head_rms(ov, w)
        return jnp.concatenate(xs, axis=1) * jnp.concatenate([w] * DN_H, axis=1) * _silu(z)

    return rowwise(fn, [cols(o, tr), cols(p, tr, DN_DIM, 6), whole(nw)], [out2d(rows, DN_DIM, BF16, tr)],
                   steps=rows // tr, name="dn_out_fwd")[0]


def dn_out_bwd(o, p, nw, dymix):
    rows = o.shape[0]
    tr = _pick(rows, (384, 128))

    def fn(i, ov, z, w, dy):
        xs, rs = _head_rms(ov, w)
        sz = _silu(z)
        dn = dy * sz
        dos, dw = [], jnp.zeros((1, DN_D), F32)
        for h in range(DN_H):
            sl = slice(h * DN_D, (h + 1) * DN_D)
            gw = dn[:, sl] * w
            dos.append(rs[h] * (gw - xs[h] * jnp.mean(gw * xs[h], axis=1, keepdims=True)))
            dw = dw + jnp.sum(dn[:, sl] * xs[h], axis=0, keepdims=True)
        n = jnp.concatenate(xs, axis=1) * jnp.concatenate([w] * DN_H, axis=1)
        return jnp.concatenate(dos, axis=1), dy * n * _dsilu(z), dw

    return rowwise(fn, [cols(o, tr), cols(p, tr, DN_DIM, 6), whole(nw), cols(dymix, tr, DN_DIM, 1)],
                   [out2d(rows, DN_DIM, F32, tr), out2d(rows, DN_DIM, BF16, tr)], steps=rows // tr,
                   name="dn_out_bwd", accs=[((1, DN_D), F32)])


def conv_a_pre_bwd(dymix, cv, p):
    rows = cv.shape[0]
    tr = _pick(rows, (384, 128))

    def fn(i, dy, c, go):
        return dy * c, dy * go

    return rowwise(fn, [cols(dymix, tr, D_CONV, 0), cols(cv, tr), cols(p, tr, D_CONV, 1)],
                   [out2d(rows, D_CONV, BF16, tr), out2d(rows, D_CONV, F32, tr)], steps=rows // tr,
                   name="conv_a_pre_bwd")


def _rows8(w):
    return jnp.pad(w.astype(F32), ((0, 8 - w.shape[0]), (0, 0)))


def _lanes(v, at):
    return jnp.pad(v.astype(F32), (at, 128 - at - v.shape[0]))[None]


def add_norm(a, w, h, next_nw, *, name):
    return mm(a, w, name=name, epi=_add_norm_epi, epi_ins=[(h, lambda j: 0)], epi_consts=[next_nw],
              epi_outs=[F32, BF16])


def _add_norm_epi(row0, t, h, nw):
    x = t + h
    return x, x * lax.rsqrt(jnp.mean(x * x, axis=1, keepdims=True) + EPS) * nw


def ffn_up_conv(hn, w_up, cw8, *, name):
    rows = hn.shape[0]
    tn = w_up.shape[2]
    tm = _pick(rows, (384, 128))
    nr = rows // tm

    def body(x_ref, wg_ref, wv_ref, w_ref, ug_ref, uv_ref, gc_ref, a_ref, carry, scr):
        i = pl.program_id(1)
        x = x_ref[...]
        gate = _dot(x, wg_ref[...])
        val = _dot(x, wv_ref[...])
        ug_ref[...] = gate.astype(BF16)
        uv_ref[...] = val.astype(BF16)
        scr[0:8, :] = jnp.where(i > 0, carry[...], 0.0)
        scr[8:8 + tm, :] = gate
        carry[...] = gate[tm - 8:tm]
        y = jnp.zeros((tm, tn), F32)
        for q in range(3):
            sh = 2 - q
            y = y + w_ref[q:q + 1, :] * scr[8 - sh:8 - sh + tm, :]
        gc_ref[...] = y.astype(BF16)
        a_ref[...] = (_silu(y) * val).astype(BF16)

    half = pl.BlockSpec((tm, tn), lambda j, i: (i, j))
    return pl.pallas_call(
        body, name=name, interpret=False,
        out_shape=[jax.ShapeDtypeStruct((rows, D_FF), BF16)] * 4,
        grid=(D_FF // tn, nr),
        in_specs=[pl.BlockSpec((tm, D), lambda j, i: (i, 0)),
                  pl.BlockSpec((None, D, tn), lambda j, i: (j, 0, 0)),
                  pl.BlockSpec((None, D, tn), lambda j, i: (j + D_FF // tn, 0, 0)),
                  pl.BlockSpec((8, tn), lambda j, i: (0, j))],
        out_specs=[half] * 4,
        scratch_shapes=[pltpu.VMEM((8, tn), F32), pltpu.VMEM((tm + 8, tn), F32)],
        compiler_params=_params(("arbitrary", "arbitrary")),
    )(hn, w_up, w_up, cw8)


def ffn_down_bwd(dh, w_down, gc, uv, ug, cw8, *, name):
    rows = dh.shape[0]
    tn = D_FF // 2
    tm = _pick(rows, (384, 128))
    nr = rows // tm
    r8 = tm // 8

    def body(dh_ref, w_ref, gc_ref, uv_ref, ug_ref, halo_ref, cw_ref, du_ref, dw_ref, carry, gscr, xscr):
        ip = pl.program_id(1)
        i = nr - 1 - ip
        da = _dot(dh_ref[...].astype(BF16), w_ref[...], 1, 1)
        c, val = gc_ref[...].astype(F32), uv_ref[...].astype(F32)
        dgc = da * val * _dsilu(c)
        du_ref[:, tn:] = (da * _silu(c)).astype(BF16)
        gscr[0:tm, :] = dgc
        gscr[tm:tm + 8, :] = jnp.where(ip > 0, carry[...], 0.0)
        carry[...] = dgc[0:8]
        xscr[0:8, :] = jnp.where(i > 0, halo_ref[...].astype(F32), 0.0)
        xscr[8:8 + tm, :] = ug_ref[...].astype(F32)
        dx = jnp.zeros((tm, tn), F32)
        dws = []
        for q in range(3):
            sh = 2 - q
            dx = dx + cw_ref[q:q + 1, :] * gscr[sh:sh + tm, :]
            dws.append(jnp.sum(dgc * xscr[8 - sh:8 - sh + tm, :], axis=0, keepdims=True))
        du_ref[:, :tn] = dx.astype(BF16)

        @pl.when(ip == 0)
        def _():
            dw_ref[...] = jnp.zeros((8, tn), F32)

        dw_ref[...] += jnp.concatenate(dws + [jnp.zeros((5, tn), F32)], axis=0)

    rev = lambda ip: nr - 1 - ip
    tile = lambda arr: pl.BlockSpec((tm, tn), lambda j, ip: (rev(ip), j))
    return pl.pallas_call(
        body, name=name, interpret=False,
        out_shape=[jax.ShapeDtypeStruct((rows, 2 * D_FF), BF16), jax.ShapeDtypeStruct((8, D_FF), F32)],
        grid=(2, nr),
        in_specs=[pl.BlockSpec((tm, D), lambda j, ip: (rev(ip), 0)),
                  pl.BlockSpec((tn, D), lambda j, ip: (j, 0)),
                  tile(gc), tile(uv), tile(ug),
                  pl.BlockSpec((8, tn), lambda j, ip: (jnp.maximum(rev(ip) * r8 - 1, 0), j)),
                  pl.BlockSpec((8, tn), lambda j, ip: (0, j))],
        out_specs=[pl.BlockSpec((tm, 2 * tn), lambda j, ip: (rev(ip), j)),
                   pl.BlockSpec((8, tn), lambda j, ip: (0, j))],
        scratch_shapes=[pltpu.VMEM((8, tn), F32), pltpu.VMEM((tm + 8, tn), F32), pltpu.VMEM((tm + 8, tn), F32)],
        compiler_params=_params(("arbitrary", "arbitrary")),
    )(dh, w_down, gc, uv, ug, ug, cw8)


def ffn_fwd(h, hn, w_up, cw8, w_down, tag, next_nw=None, target=None):
    ug, uv, gc, a = ffn_up_conv(hn, w_up, cw8, name=f"ffn{tag}_up")
    if target is not None:
        out, hn_next = add_loss(a, w_down, h, target, name=f"ffn{tag}_down")
    else:
        out, hn_next = add_norm(a, w_down, h, next_nw, name=f"ffn{tag}_down")
    return out, hn_next, (hn, ug, uv, a, gc)


def ffn_bwd(h, nw, w_up, cw8, w_down, saved, dh, tag):
    hn, ug, uv, a, gc = saved
    du, d_cw = ffn_down_bwd(dh, w_down, gc, uv, ug, cw8, name=f"ffn{tag}_down_dx")
    d_w_down = mm(a, dh, ta=True, out_dtype=BF16, name=f"ffn{tag}_down_dw")
    dh_new, d_nw = dx_rms_bwd(du, w_up, h, nw, dh, name=f"ffn{tag}_up_dx", b_chip=True, swap_mid=True)
    d_w_up = mm(hn, du, ta=True, out_dtype=BF16, out_chip=True, swap_mid=True, name=f"ffn{tag}_up_dw")
    return dh_new, d_nw, d_w_up, d_cw, d_w_down


def mixer_fwd(h, nw, w_in, ca8, dc8, alog, dtb, dnw, w_out, tie=None, next_nw=None):
    rows = h.shape[0]
    tr = _pick(rows, (384, 128))
    hn = rms_fwd(h, nw, name="mix_norm")
    if callable(w_in):
        hn, w_in = w_in(hn)
    p = mm(hn, w_in, name="mix_in")
    y_a, cv = conv_fwd([(p, 0), (p, 2)], ca8, 3, rows=rows, c=D_CONV, tc=D_CONV, tr=tr, name="conv_a",
                       pre=lambda gi, ah: gi * ah, post=lambda j, y, go: (go * y, y), extras=[(p, 1)],
                       outs=[BF16, F32])
    qkv_n, cq = conv_fwd([(p, 3)], dc8, 4, rows=rows, c=3 * DN_DIM, tc=DN_DIM, tr=tr, name="dn_conv",
                         post=dn_qkv_post, outs=[F32, F32], strip=tr)
    bgcol = bg_fwd(p, alog, dtb)
    if tie is not None:
        bgcol = tie(bgcol)
    bgrow = bgcol[:, :8].reshape(rows // CH, CH, 8).transpose(0, 2, 1)
    o, s_all, ti_all = dn_fwd(qkv_n, bgcol, bgrow)
    y_b = dn_out_fwd(o, p, dnw)
    ymix = jnp.concatenate([y_a, y_b], axis=1)
    w_out = w_out() if callable(w_out) else w_out
    out, hn_next = add_norm(ymix, w_out, h, next_nw, name="mix_out")
    return out, hn_next, (hn, p, cv, qkv_n, cq, bgcol, bgrow, o, s_all, ti_all, ymix, w_in)


def mixer_bwd(h, nw, ca8, dc8, alog, dtb, dnw, w_out, saved, dh):
    hn, p, cv, qkv_n, cq, bgcol, bgrow, o, s_all, ti_all, ymix, w_in = saved
    rows = h.shape[0]
    tr = _pick(rows, (384, 128))
    dymix = mm(dh, w_out, tb=True, name="mix_out_dx")
    d_w_out = mm(ymix, dh, ta=True, out_dtype=BF16, name="mix_out_dw")
    do, dz, d_dnw = dn_out_bwd(o, p, dnw, dymix)
    dq, dk, dv, dbg = dn_bwd(qkv_n, bgcol, bgrow, s_all, ti_all, do)
    dbg_p, d_alog, d_dtb = bg_bwd(p, alog, dtb, dbg)
    dcq = dn_qkv_bwd(cq, dq, dk, dv)
    dqkv, d_dc = conv_bwd([(p, 3)], dc8, 4, dcq, rows=rows, c=3 * DN_DIM, tc=DN_DIM, tr=tr, name="dn_conv_bwd",
                          post=lambda dx: dx, outs=[BF16])
    dgo, dcv = conv_a_pre_bwd(dymix, cv, p)
    dgi, dah, d_ca = conv_bwd([(p, 0), (p, 2)], ca8, 3, dcv, rows=rows, c=D_CONV, tc=D_CONV, tr=tr,
                              name="conv_a_bwd", pre=lambda gi, ah: gi * ah,
                              post=lambda dm, gi, ah: (dm * ah, dm * gi), extras=[(p, 0), (p, 2)], outs=[BF16, BF16])
    dp = jnp.concatenate([dgi, dgo, dah, dqkv, dz, dbg_p], axis=1)
    dh_new, d_nw = dx_rms_bwd(dp, w_in, h, nw, dh, name="mix_in_dx")
    d_w_in = mm(hn, dp, ta=True, out_dtype=BF16, name="mix_in_dw")
    return dh_new, d_nw, d_w_in, d_ca, d_dc, d_alog, d_dtb, d_dnw, d_w_out


def swa_layer_fwd(h, hn, wqkv, qw, kw, sinks, wo, next_nw):
    qkv = mm(hn, wqkv, name="swa_qkv")
    qh, kh, vh = qknorm_fwd(qkv, qw, kw)
    att = swa_fwd(qh, kh, vh, sinks)
    out, hn_next = add_norm(att, wo, h, next_nw, name="swa_out")
    return out, hn_next, (hn, qkv, qh, kh, vh, att)


def swa_layer_bwd(h, nw, wqkv, qw, kw, sinks, wo, saved, dh):
    hn, qkv, qh, kh, vh, att = saved
    datt = mm(dh, wo, tb=True, out_dtype=BF16, name="swa_out_dx")
    d_wo = mm(att, dh, ta=True, out_dtype=BF16, name="swa_out_dw")
    dqh, dkh, dvh, dsk = swa_bwd(qh, kh, vh, sinks, datt)
    dqkv, d_qw, d_kw = qknorm_bwd(qkv, qw, kw, dqh, dkh, dvh)
    dh_new, d_nw = dx_rms_bwd(dqkv, wqkv, h, nw, dh, name="swa_qkv_dx")
    d_wqkv = mm(hn, dqkv, ta=True, out_dtype=BF16, name="swa_qkv_dw")
    d_sinks = jnp.sum(dsk[:, :, 0], axis=0)
    return dh_new, d_nw, d_wqkv, d_qw, d_kw, d_sinks, d_wo


BIG = ("mix_w_in", "mix_w_out", "swa_wq", "swa_wk", "swa_wv", "swa_wo", "ffn_w_up", "ffn_w_down")


def _flat_pad(parts, rows):
    v = jnp.concatenate([t.astype(F32).reshape(-1) for t in parts])
    return jnp.pad(v, (0, rows * 1024 - v.shape[0])).reshape(rows, 1024)


def _split_flat(flat, shapes):
    v = flat.reshape(-1)
    out, o = [], 0
    for s in shapes:
        n = 1
        for d_ in s:
            n *= d_
        out.append(v[o:o + n].reshape(s))
        o += n
    return out


def local_step(x0, target0, meta_full, anw, fnw, w_in, ca8, dc8, alog, dtb, dnw, qw, kw, sinks, fc8, late,
               begin=None, tie=None):
    begin = begin or (lambda tag, names, grads: None)
    h0 = jnp.concatenate([jnp.zeros((PAD, D), F32), meta_full, x0], axis=0)
    h1, hn1, s_mix = mixer_fwd(h0, anw[0], w_in, ca8, dc8, alog, dtb, dnw, lambda: late()[0], tie, fnw[0])
    w_out, wqkv, wo, w_up, w_down = late()
    h2, hn2, s_f0 = ffn_fwd(h1, hn1, w_up[0], fc8[0], w_down[0], 0, anw[1])
    h3, hn3, s_swa = swa_layer_fwd(h2, hn2, wqkv, qw, kw, sinks, wo, fnw[1])
    dh, loss_l, s_f1 = ffn_fwd(h3, hn3, w_up[1], fc8[1], w_down[1], 1, target=target0)
    dh, d_fnw1, d_up1, d_fc1, d_down1 = ffn_bwd(h3, fnw[1], w_up[1], fc8[1], w_down[1], s_f1, dh, 1)
    begin("ffn1", ("up1", "down1"), [d_up1, d_down1.reshape(4, 704, D)])
    dh, d_anw1, d_wqkv, d_qw, d_kw, d_sinks, d_wo = swa_layer_bwd(h2, anw[1], wqkv, qw, kw, sinks, wo, s_swa, dh)
    begin("swa", ("wq", "wk", "wv", "wo"),
          [d_wqkv[:, :D].reshape(4, 256, D), d_wqkv[:, D:D + 256].reshape(4, 256, 256),
           d_wqkv[:, D + 256:].reshape(4, 256, 256), d_wo.reshape(4, 256, D)])
    dh, d_fnw0, d_up0, d_fc0, d_down0 = ffn_bwd(h1, fnw[0], w_up[0], fc8[0], w_down[0], s_f0, dh, 0)
    begin("ffn0", ("up0", "down0"), [d_up0, d_down0.reshape(4, 704, D)])
    dh, d_anw0, d_w_in, d_ca, d_dc, d_alog, d_dtb, d_dnw, d_w_out = mixer_bwd(
        h0, anw[0], ca8, dc8, alog, dtb, dnw, w_out, s_mix, dh)
    begin("mix", ("w_in", "w_out"),
          [d_w_in[:, :IN_DIM].reshape(D, 4, 898).transpose(1, 0, 2), d_w_out.reshape(4, 256, D)])
    return (dh, loss_l, d_anw0, d_anw1, d_fnw0, d_fnw1, d_w_in, d_ca, d_dc, d_alog, d_dtb, d_dnw, d_w_out, d_wqkv,
            d_qw, d_kw, d_sinks, d_wo, d_up0, d_up1, d_fc0, d_fc1, d_down0, d_down1)


def kernel(x, meta_tokens, attn_norm_w, ffn_norm_w, mix_w_in, conv_a_w, dn_conv_w, dn_a_log, dn_dt_bias, dn_norm_w, mix_w_out, swa_wq, swa_wk, swa_wv, swa_q_norm_w, swa_k_norm_w, swa_sinks, swa_wo, ffn_w_up, ffn_conv_w, ffn_w_down, loss_target, m_meta_tokens, m_attn_norm_w, m_ffn_norm_w, m_mix_w_in, m_conv_a_w, m_dn_conv_w, m_dn_a_log, m_dn_dt_bias, m_dn_norm_w, m_mix_w_out, m_swa_wq, m_swa_wk, m_swa_wv, m_swa_q_norm_w, m_swa_k_norm_w, m_swa_sinks, m_swa_wo, m_ffn_w_up, m_ffn_conv_w, m_ffn_w_down, v_meta_tokens, v_attn_norm_w, v_ffn_norm_w, v_mix_w_in, v_conv_a_w, v_dn_conv_w, v_dn_a_log, v_dn_dt_bias, v_dn_norm_w, v_mix_w_out, v_swa_wq, v_swa_wk, v_swa_wv, v_swa_q_norm_w, v_swa_k_norm_w, v_swa_sinks, v_swa_wo, v_ffn_w_up, v_ffn_conv_w, v_ffn_w_down):
    ix, iy, ic = lax.axis_index("x"), lax.axis_index("y"), lax.axis_index("c")
    chip = 2 * ix + iy
    seq = x.shape[1]
    rows = HEAD0 + seq

    small_sharded = (conv_a_w, dn_conv_w, ffn_conv_w, meta_tokens)
    up_b, down_b = ffn_w_up.astype(BF16), ffn_w_down.astype(BF16)
    own = [mix_w_in[0].astype(BF16), mix_w_out[0].astype(BF16), swa_wq[0].astype(BF16), swa_wk[0].astype(BF16),
           swa_wv[0].astype(BF16), swa_wo[0].astype(BF16), up_b[0], up_b[1], down_b[0], down_b[1]]
    fill = lambda gathered, mine: [lax.dynamic_update_slice_in_dim(g, t[None], chip, axis=0)
                                   for g, t in zip(gathered, mine)]
    on_its_way, = gather_weights_beside(own[:1], 9, "gather_w_in")
    _, g_small = gather_weights([], _flat_pad(small_sharded, SW_ROWS))

    rest = {}

    def w_in(hn):
        hn, got, g_out = lax.optimization_barrier((hn, on_its_way, own[1]))
        rest["w_out"] = fill(gather_weights_beside([g_out], 1, "gather_w_out"), [g_out])
        g_in, = fill([got], own[:1])
        return hn, jnp.pad(g_in.transpose(1, 0, 2).reshape(D, IN_DIM), ((0, 0), (0, P_W - IN_DIM)))

    def tie(t):
        t, *mine = lax.optimization_barrier((t, *own[2:]))
        g_q, g_k, g_v, g_o, g_up0, g_up1, g_dn0, g_dn1 = mine
        soon, last = [g_up0, g_dn0, g_q, g_k, g_v, g_o], [g_up1, g_dn1]
        rest["soon"] = fill(gather_weights_beside(soon, 7, "gather_layers_12"), soon)
        rest["last"] = fill(gather_weights_beside(last, 8, "gather_layer_3"), last)
        return t

    def late():
        (g_out,), (g_up0, g_dn0, g_q, g_k, g_v, g_o), (g_up1, g_dn1) = rest["w_out"], rest["soon"], rest["last"]
        wqkv = jnp.concatenate([g_q.reshape(D, D), g_k.reshape(D, 256), g_v.reshape(D, 256)], axis=1)
        return (g_out.reshape(D, D), wqkv, g_o.reshape(D, D), [g_up0, g_up1],
                [g_dn0.reshape(D_FF, D), g_dn1.reshape(D_FF, D)])

    gs = g_small.reshape(4, -1)
    ca_full = gs[:, 0:384].reshape(4, 3, 128).transpose(1, 0, 2).reshape(3, D_CONV)
    dc_full = gs[:, 384:1920].reshape(4, 4, 384).transpose(1, 0, 2).reshape(4, 3 * DN_DIM)
    fc_full = gs[:, 1920:6144].reshape(4, 2, 3, 704).transpose(1, 2, 0, 3).reshape(2, 3, D_FF)
    meta_full = gs[:, 6144:10240].reshape(4, N_META, 256).transpose(1, 0, 2).reshape(N_META, D)
    ca8, dc8 = _rows8(ca_full), _rows8(dc_full)
    fc8 = [_rows8(fc_full[0]), _rows8(fc_full[1])]
    alog, dtb = _lanes(dn_a_log[0], 4), _lanes(dn_dt_bias[0], 4)
    dnw = dn_norm_w.astype(F32)
    qw, kw = swa_q_norm_w.astype(F32), swa_k_norm_w.astype(F32)
    sinks = swa_sinks[0].astype(F32)
    anw = [attn_norm_w[0:1], attn_norm_w[1:2]]
    fnw = [ffn_norm_w[0:1], ffn_norm_w[1:2]]

    c_idx = jnp.reshape(ic, (1,)).astype(jnp.int32)
    chip_idx = jnp.stack([chip, ic]).astype(jnp.int32)
    begun = []

    def begin(tag, names, grads):
        pairs, gots = reduce_begin(grads, names, c_idx, 2 + len(begun), tag)
        begun.append((names, pairs, gots))

    (dh, loss_l, d_anw0, d_anw1, d_fnw0, d_fnw1, d_w_in, d_ca, d_dc, d_alog, d_dtb, d_dnw, d_w_out, d_wqkv, d_qw,
     d_kw, d_sinks, d_wo, d_up0, d_up1, d_fc0, d_fc1, d_down0, d_down1) = local_step(
        x[0], loss_target[0], meta_full, anw, fnw, w_in, ca8, dc8, alog, dtb, dnw, qw, kw, sinks, fc8, late,
        begin, tie)
    grad_x = dh[HEAD0:][None]

    small_parts = [jnp.concatenate([d_anw0, d_anw1], axis=0), jnp.concatenate([d_fnw0, d_fnw1], axis=0),
                   d_alog[0, 4:8], d_dtb[0, 4:8], d_dnw, d_qw, d_kw, d_sinks,
                   d_ca[:3], d_dc[:4], jnp.stack([d_fc0[:3], d_fc1[:3]]), dh[PAD:HEAD0], loss_l[0, 0:1]]
    small_shapes = [(2, D), (2, D), (1, 4), (1, 4), (1, DN_D), (1, SWA_D), (1, SWA_D), (1, SWA_H),
                    (1, 3, D_CONV), (1, 4, 3 * DN_DIM), (2, 3, D_FF), (N_META, D), ()]
    gathered_small = gather_small(_flat_pad(small_parts, SV_ROWS))

    red_big = {}
    for part in (begun[:-1], begun[-1:]):
        part_names = [n for names, _, _ in part for n in names]
        red_big.update(zip(part_names, reduce_end([p for _, ps, _ in part for p in ps],
                                                  [g for _, _, gs_ in part for g in gs_], part_names, chip_idx)))
    g_w_in, g_w_out, g_wq, g_wk, g_wv, g_wo, g_up0, g_up1, g_dn0, g_dn1 = [
        red_big[n] for n in ("w_in", "w_out", "wq", "wk", "wv", "wo", "up0", "up1", "down0", "down1")]

    grads = dict(mix_w_in=g_w_in, mix_w_out=g_w_out, swa_wq=g_wq, swa_wk=g_wk, swa_wv=g_wv, swa_wo=g_wo,
                 ffn_w_up=[g_up0, g_up1], ffn_w_down=[g_dn0, g_dn1])
    weights = dict(meta_tokens=meta_tokens, attn_norm_w=attn_norm_w, ffn_norm_w=ffn_norm_w, mix_w_in=mix_w_in,
                   conv_a_w=conv_a_w, dn_conv_w=dn_conv_w, dn_a_log=dn_a_log, dn_dt_bias=dn_dt_bias,
                   dn_norm_w=dn_norm_w, mix_w_out=mix_w_out, swa_wq=swa_wq, swa_wk=swa_wk, swa_wv=swa_wv,
                   swa_q_norm_w=swa_q_norm_w, swa_k_norm_w=swa_k_norm_w, swa_sinks=swa_sinks, swa_wo=swa_wo,
                   ffn_w_up=ffn_w_up, ffn_conv_w=ffn_conv_w, ffn_w_down=ffn_w_down)
    m_in = dict(meta_tokens=m_meta_tokens, attn_norm_w=m_attn_norm_w, ffn_norm_w=m_ffn_norm_w, mix_w_in=m_mix_w_in,
                conv_a_w=m_conv_a_w, dn_conv_w=m_dn_conv_w, dn_a_log=m_dn_a_log, dn_dt_bias=m_dn_dt_bias,
                dn_norm_w=m_dn_norm_w, mix_w_out=m_mix_w_out, swa_wq=m_swa_wq, swa_wk=m_swa_wk, swa_wv=m_swa_wv,
                swa_q_norm_w=m_swa_q_norm_w, swa_k_norm_w=m_swa_k_norm_w, swa_sinks=m_swa_sinks, swa_wo=m_swa_wo,
                ffn_w_up=m_ffn_w_up, ffn_conv_w=m_ffn_conv_w, ffn_w_down=m_ffn_w_down)
    v_in = dict(meta_tokens=v_meta_tokens, attn_norm_w=v_attn_norm_w, ffn_norm_w=v_ffn_norm_w, mix_w_in=v_mix_w_in,
                conv_a_w=v_conv_a_w, dn_conv_w=v_dn_conv_w, dn_a_log=v_dn_a_log, dn_dt_bias=v_dn_dt_bias,
                dn_norm_w=v_dn_norm_w, mix_w_out=v_mix_w_out, swa_wq=v_swa_wq, swa_wk=v_swa_wk, swa_wv=v_swa_wv,
                swa_q_norm_w=v_swa_q_norm_w, swa_k_norm_w=v_swa_k_norm_w, swa_sinks=v_swa_sinks, swa_wo=v_swa_wo,
                ffn_w_up=v_ffn_w_up, ffn_conv_w=v_ffn_conv_w, ffn_w_down=v_ffn_w_down)
    names = list(weights)
    small = [n for n in names if n not in BIG]
    delta, new_m, new_v = {}, {}, {}
    for n in BIG:
        delta[n], new_m[n], new_v[n], grads[n] = adamw(weights[n], grads[n], m_in[n], v_in[n], name=f"adamw_{n}")
    gathered_small, _ = lax.optimization_barrier((gathered_small, new_v["ffn_w_down"]))
    (g_anw, g_fnw, g_alog, g_dtb, g_dnw, g_qw, g_kw, g_sinks, g_ca_f, g_dc_f, g_fc_f, g_meta_f,
     loss) = _split_flat(sum_slots(gathered_small), small_shapes)
    grads.update(meta_tokens=lax.dynamic_slice_in_dim(g_meta_f, chip * 256, 256, axis=1), attn_norm_w=g_anw,
                 ffn_norm_w=g_fnw, conv_a_w=lax.dynamic_slice_in_dim(g_ca_f, chip * 128, 128, axis=2),
                 dn_conv_w=lax.dynamic_slice_in_dim(g_dc_f, chip * 384, 384, axis=2), dn_a_log=g_alog,
                 dn_dt_bias=g_dtb, dn_norm_w=g_dnw, swa_q_norm_w=g_qw, swa_k_norm_w=g_kw, swa_sinks=g_sinks,
                 ffn_conv_w=lax.dynamic_slice_in_dim(g_fc_f, chip * 704, 704, axis=2))
    grads = {n: grads[n].reshape(weights[n].shape) for n in names}
    shapes = [weights[n].shape for n in small]
    packed = [_flat_pad([t[n] for n in small], SW_ROWS) for t in (weights, grads, m_in, v_in)]
    for store, flat in zip((delta, new_m, new_v), adamw(*packed, name="adamw_small")):
        for n, t in zip(small, _split_flat(flat, shapes)):
            store[n] = t
    return (loss, grad_x, *[grads[n] for n in names], *[delta[n] for n in names],
            *[new_m[n] for n in names], *[new_v[n] for n in names])
```

```python
import functools

import jax
import jax.numpy as jnp
from jax import lax
from jax.experimental import pallas as pl
from jax.experimental.pallas import tpu as pltpu
from jax.experimental.pallas import tpu_sc as plsc

F32 = jnp.float32
BF16 = jnp.bfloat16
HI = lax.Precision.HIGHEST
MESH = pl.DeviceIdType.MESH

D = 1024
N_META = 16
PAD = 112
HEAD0 = PAD + N_META
D_CONV = 512
DN_H = 4
DN_D = 128
DN_DIM = 512
CH = 64
IN_DIM = 3592
P_W = 3840
BG0 = 3584
SWA_H = 16
SWA_KV = 4
SWA_D = 64
BLK = 128
NKEY = N_META + 2 * BLK
D_FF = 2816
EPS = 1e-6
LR, B1, B2, AEPS, WD, STEP = 0.001, 0.9, 0.999, 1e-08, 0.01, 10
VMEM_LIMIT = 48 * 1024 * 1024
MM_VMEM_BUDGET = 34 * 1024 * 1024
R_BIG = 6144
R_HALF = R_BIG // 2
SV_ROWS = 48
SW_ROWS = 16


def _pick(n, cands):
    for c in cands:
        if n % c == 0:
            return c
    return n


def _params(sem=None):
    return pltpu.CompilerParams(dimension_semantics=sem, vmem_limit_bytes=VMEM_LIMIT)


def _dot(a, b, ca=1, cb=0, prec=None):
    return lax.dot_general(a, b, (((ca,), (cb,)), ((), ())), precision=prec,
                           preferred_element_type=F32)


def _sigmoid(x):
    return 1.0 / (1.0 + jnp.exp(-x))


def _silu(x):
    return x * _sigmoid(x)


def _dsilu(x):
    s = _sigmoid(x)
    return s * (1.0 + x * (1.0 - s))


def _softplus(x):
    return jnp.maximum(x, 0.0) + jnp.log(1.0 + jnp.exp(-jnp.abs(x)))


def mm(a, b, *, name, ta=False, tb=False, out_dtype=F32, add=None, tm=None, tn=None, tk=None,
       b_chip=False, out_chip=False, swap_mid=False, epi=None, epi_ins=(), epi_consts=(), epi_outs=(), epi_accs=()):
    if epi is not None:
        return _mm_epi(a, b, name=name, tb=tb, tn=tn, b_chip=b_chip, swap_mid=swap_mid, epi=epi, epi_ins=epi_ins,
                       epi_consts=epi_consts, epi_outs=epi_outs, epi_accs=epi_accs)
    chip_of = _chip_order(swap_mid)
    m, k = (a.shape[1], a.shape[0]) if ta else a.shape
    if b_chip:
        n = b.shape[1] if tb else 4 * b.shape[2]
        if tb:
            tk = b.shape[2]
        else:
            tn = b.shape[2]
    else:
        n = b.shape[0] if tb else b.shape[1]
    if out_chip:
        tn = n // 4
    tn = tn or _pick(n, (1408, 1024, 768, 512, 256, 128))
    tk = tk or (_pick(k, (1408, 704, 384, 128)) if ta else _pick(k, (1024, 1408, 768, 512, 128)))
    nk = k // tk
    if tm is None:
        isz = lambda t: jnp.dtype(t.dtype).itemsize
        osz = jnp.dtype(out_dtype).itemsize
        for tm in ((1408, 1024, 512, 384, 256, 128) if ta else (1408, 704, 512, 384, 256, 128)):
            need = 2 * (tm * tk * isz(a) + tk * tn * isz(b) + tm * tn * osz + (tm * tn * 4 if add is not None else 0))
            need += tm * tn * 4 if nk > 1 else 0
            if m % tm == 0 and need <= MM_VMEM_BUDGET:
                break
        else:
            tm = m
    dims = (((0 if ta else 1,), (1 if tb else 0,)), ((), ()))

    def body(*refs):
        if add is None:
            a_ref, b_ref, o_ref, acc_ref = refs
            add_ref = None
        else:
            a_ref, b_ref, add_ref, o_ref, acc_ref = refs
        def part():
            return lax.dot_general(a_ref[...].astype(BF16), b_ref[...].astype(BF16), dims,
                                   preferred_element_type=F32)

        def finish(total):
            if add_ref is not None:
                total = total + add_ref[...]
            o_ref[...] = total.astype(out_dtype)

        if nk == 1:
            finish(part())
        else:
            kk = pl.program_id(2)

            @pl.when(kk == 0)
            def _():
                acc_ref[...] = part()

            @pl.when(jnp.logical_and(kk > 0, kk < nk - 1))
            def _():
                acc_ref[...] += part()

            @pl.when(kk == nk - 1)
            def _():
                finish(acc_ref[...] + part())

    a_spec = pl.BlockSpec((tk, tm), lambda i, j, kk: (kk, i)) if ta else pl.BlockSpec((tm, tk), lambda i, j, kk: (i, kk))
    if b_chip and tb:
        b_spec = pl.BlockSpec((None, tn, tk), lambda i, j, kk: (chip_of(kk), j, 0))
    elif b_chip:
        b_spec = pl.BlockSpec((None, tk, tn), lambda i, j, kk: (j, kk, 0))
    elif tb:
        b_spec = pl.BlockSpec((tn, tk), lambda i, j, kk: (j, kk))
    else:
        b_spec = pl.BlockSpec((tk, tn), lambda i, j, kk: (kk, j))
    o_spec = pl.BlockSpec((tm, tn), lambda i, j, kk: (i, j))
    in_specs = [a_spec, b_spec] + ([o_spec] if add is not None else [])
    args = [a, b] + ([add] if add is not None else [])
    out_spec = pl.BlockSpec((None, tm, tn), lambda i, j, kk: (chip_of(j), i, 0)) if out_chip else o_spec
    return pl.pallas_call(
        body, name=name, interpret=False,
        out_shape=jax.ShapeDtypeStruct((4, m, tn) if out_chip else (m, n), out_dtype),
        grid=(m // tm, n // tn, nk), in_specs=in_specs, out_specs=out_spec,
        scratch_shapes=[pltpu.VMEM((tm, tn) if nk > 1 else (8, 128), F32)],
        compiler_params=_params(("parallel", "parallel", "arbitrary")),
    )(*args)


def _chip_order(swap_mid):
    return (lambda k: (k % 2) * 2 + k // 2) if swap_mid else (lambda k: k)


def _mm_epi(a, b, *, name, tb, tn, b_chip, epi, epi_ins, epi_consts, epi_outs, epi_accs, swap_mid=False):
    chip_of = _chip_order(swap_mid)
    m, k = a.shape
    if b_chip:
        n = b.shape[1] if tb else 4 * b.shape[2]
        tk = b.shape[2] if tb else None
        tn = tn if tb else b.shape[2]
    else:
        n = b.shape[0] if tb else b.shape[1]
        tk = None
    tn = tn or _pick(n, (1408, 1024, 768, 512, 256, 128))
    tk = tk or _pick(k, (1024, 1408, 1280, 768, 512, 128))
    nk, nj = k // tk, n // tn
    isz = lambda t: jnp.dtype(t.dtype if hasattr(t, "dtype") else t).itemsize
    outs3 = [t if isinstance(t, tuple) else (t, n, lambda j: j) for t in epi_outs]
    side = sum(isz(t) for t, _ in epi_ins) + sum(isz(dt) for dt, _, _ in outs3)
    for tm in (1408, 704, 512, 384, 256, 128):
        need = 2 * (tm * tk * isz(a) + tk * tn * isz(b) + tm * tn * side) + (tm * tn * 4 if nk > 1 else 0)
        if m % tm == 0 and need <= MM_VMEM_BUDGET:
            break
    else:
        tm = m
    dims = (((1,), (1 if tb else 0,)), ((), ()))
    n_in, n_c, n_out, n_acc = len(epi_ins), len(epi_consts), len(epi_outs), len(epi_accs)

    def body(*refs):
        a_ref, b_ref = refs[:2]
        in_refs = refs[2:2 + n_in + n_c]
        out_refs = refs[2 + n_in + n_c:2 + n_in + n_c + n_out]
        acc_out = refs[2 + n_in + n_c + n_out:2 + n_in + n_c + n_out + n_acc]
        acc_ref = refs[-1]
        i, j, kk = pl.program_id(0), pl.program_id(1), pl.program_id(2)
        def part():
            return lax.dot_general(a_ref[...].astype(BF16), b_ref[...].astype(BF16), dims,
                                   preferred_element_type=F32)

        def finish(total):
            res = epi(i * tm, total, *[r[...] for r in in_refs])
            if not isinstance(res, (tuple, list)):
                res = (res,)
            for r, v in zip(out_refs, res[:n_out]):
                r[...] = v.astype(r.dtype)
            if n_acc:
                @pl.when(jnp.logical_and(i == 0, j == 0))
                def _():
                    for r in acc_out:
                        r[...] = jnp.zeros(r.shape, r.dtype)

                for r, v in zip(acc_out, res[n_out:]):
                    r[...] += jnp.broadcast_to(v, r.shape).astype(r.dtype)

        if nk == 1:
            finish(part())
        else:
            @pl.when(kk == 0)
            def _():
                acc_ref[...] = part()

            @pl.when(jnp.logical_and(kk > 0, kk < nk - 1))
            def _():
                acc_ref[...] += part()

            @pl.when(kk == nk - 1)
            def _():
                finish(acc_ref[...] + part())

    a_spec = pl.BlockSpec((tm, tk), lambda i, j, kk: (i, kk))
    if b_chip and tb:
        b_spec = pl.BlockSpec((None, tn, tk), lambda i, j, kk: (chip_of(kk), j, 0))
    elif b_chip:
        b_spec = pl.BlockSpec((None, tk, tn), lambda i, j, kk: (j, kk, 0))
    elif tb:
        b_spec = pl.BlockSpec((tn, tk), lambda i, j, kk: (j, kk))
    else:
        b_spec = pl.BlockSpec((tk, tn), lambda i, j, kk: (kk, j))
    in_specs = [a_spec, b_spec]

    def in_spec(t, col):
        front = m - t.shape[0]
        if not front:
            return pl.BlockSpec((tm, tn), lambda i, j, kk: (i, col(j)))
        return pl.BlockSpec((pl.Element(tm), pl.Element(tn)),
                            lambda i, j, kk: (pl.multiple_of(jnp.maximum(i * tm - front, 0), 8), col(j) * tn))

    in_specs += [in_spec(t, col) for t, col in epi_ins]
    in_specs += [pl.BlockSpec(t.shape, lambda i, j, kk, nd=t.ndim: (0,) * nd) for t in epi_consts]
    out_specs = [pl.BlockSpec((tm, tn), lambda i, j, kk, col=col: (i, col(j))) for _, _, col in outs3]
    out_specs += [pl.BlockSpec(s, lambda i, j, kk, nd=len(s): (0,) * nd) for s, _ in epi_accs]
    out_shape = [jax.ShapeDtypeStruct((m, width), dt) for dt, width, _ in outs3]
    out_shape += [jax.ShapeDtypeStruct(s, dt) for s, dt in epi_accs]
    sem = ("arbitrary", "arbitrary", "arbitrary") if n_acc else ("parallel", "parallel", "arbitrary")
    return pl.pallas_call(
        body, name=name, interpret=False, out_shape=out_shape,
        grid=(m // tm, nj, nk), in_specs=in_specs, out_specs=out_specs,
        scratch_shapes=[pltpu.VMEM((tm, tn) if nk > 1 else (8, 128), F32)],
        compiler_params=_params(sem),
    )(a, b, *[t for t, _ in epi_ins], *epi_consts)


def cols(arr, tr, width=None, cb=0):
    width = width or arr.shape[1]
    return (arr, (tr, width), lambda i: (i, cb), "r2")


def heads(arr, tr):
    return (arr, (arr.shape[0], tr, arr.shape[2]), lambda i: (0, i, 0), "r3")


def whole(arr):
    nd = arr.ndim
    return (arr, arr.shape, lambda i: (0,) * nd, "w")


STRIP = 16


def _rows_of(ref, kind, r0, n):
    if kind == "r2":
        return ref[pl.ds(r0, n), :]
    if kind == "r3":
        return ref[:, pl.ds(r0, n), :]
    return ref[...]


def _set_rows(ref, kind, r0, n, v):
    if kind == "r2":
        ref[pl.ds(r0, n), :] = v.astype(ref.dtype)
    elif kind == "r3":
        ref[:, pl.ds(r0, n), :] = v.astype(ref.dtype)
    else:
        ref[...] = v.astype(ref.dtype)


def rowwise(fn, ins, outs, *, steps, name, accs=(), strip=None):
    n_in, n_out, n_acc = len(ins), len(outs), len(accs)
    kin = [t[3] for t in ins]
    kout = [t[4] for t in outs]
    tr = next((t[1][-2] for t in ins if t[3] != "w"), 0)

    def body(*refs):
        i = pl.program_id(0)
        in_refs, out_refs, acc_refs = refs[:n_in], refs[n_in:n_in + n_out], refs[n_in + n_out:]
        if n_acc:
            @pl.when(i == 0)
            def _():
                for r in acc_refs:
                    r[...] = jnp.zeros(r.shape, r.dtype)

        def run(r0, n):
            res = fn(i * tr + r0, *[_rows_of(r, k, r0, n) for r, k in zip(in_refs, kin)])
            if not isinstance(res, (tuple, list)):
                res = (res,)
            for r, k, v in zip(out_refs, kout, res[:n_out]):
                _set_rows(r, k, r0, n, v)
            for r, v in zip(acc_refs, res[n_out:]):
                r[...] += jnp.broadcast_to(v, r.shape).astype(r.dtype)

        if strip is None or tr <= strip:
            run(0, tr)
        else:
            def step(s, carry):
                run(pl.multiple_of(s * strip, strip), strip)
                return carry
            lax.fori_loop(0, tr // strip, step, 0)

    def zmap(nd):
        return lambda i: (0,) * nd

    in_specs = [pl.BlockSpec(t[1], t[2]) for t in ins]
    out_specs = [pl.BlockSpec(t[2], t[3]) for t in outs]
    out_specs += [pl.BlockSpec(s, zmap(len(s))) for s, _ in accs]
    out_shape = [jax.ShapeDtypeStruct(t[0], t[1]) for t in outs]
    out_shape += [jax.ShapeDtypeStruct(s, d) for s, d in accs]
    res = pl.pallas_call(
        body, name=name, interpret=False, out_shape=out_shape, grid=(steps,),
        in_specs=in_specs, out_specs=out_specs,
        compiler_params=_params(("arbitrary",)),
    )(*[t[0] for t in ins])
    return res


def out2d(rows, width, dtype, tr):
    return ((rows, width), dtype, (tr, width), lambda i: (i, 0), "r2")


def conv_fwd(xs, w8, kw, *, rows, c, tc, tr, name, post, extras=(), outs=(), pre=None, strip=STRIP):
    nx, ne, no = len(xs), len(extras), len(outs)
    nr, nc = rows // tr, c // tc
    r8 = tr // 8
    st = strip

    def body(*refs):
        x_refs = refs[:2 * nx]
        w_ref = refs[2 * nx]
        e_refs = refs[2 * nx + 1:2 * nx + 1 + ne]
        o_refs = refs[2 * nx + 1 + ne:2 * nx + 1 + ne + no]
        scr = refs[-1]
        j, i = pl.program_id(0), pl.program_id(1)
        halo = [x_refs[2 * q + 1][...].astype(F32) for q in range(nx)]
        scr[0:8, :] = jnp.where(i > 0, pre(*halo) if pre else halo[0], 0.0)

        def fill(s, carry):
            r0 = pl.multiple_of(s * st, st)
            cur = [x_refs[2 * q][pl.ds(r0, st), :].astype(F32) for q in range(nx)]
            scr[pl.ds(8 + r0, st), :] = pre(*cur) if pre else cur[0]
            return carry

        def comp(s, carry):
            r0 = pl.multiple_of(s * st, st)
            win = scr[pl.ds(r0, st + 8), :]
            y = jnp.zeros((st, tc), F32)
            for q in range(kw):
                sh = kw - 1 - q
                y = y + w_ref[q:q + 1, :] * win[8 - sh:8 - sh + st]
            res = post(j, y, *[e[pl.ds(r0, st), :] for e in e_refs])
            if not isinstance(res, (tuple, list)):
                res = (res,)
            for r, v in zip(o_refs, res):
                r[pl.ds(r0, st), :] = v.astype(r.dtype)
            return carry

        lax.fori_loop(0, tr // st, fill, 0)
        lax.fori_loop(0, tr // st, comp, 0)

    in_specs, args = [], []
    for arr, cb0 in xs:
        in_specs.append(pl.BlockSpec((tr, tc), lambda j, i, cb0=cb0: (i, cb0 + j)))
        in_specs.append(pl.BlockSpec((8, tc), lambda j, i, cb0=cb0: (jnp.maximum(i * r8 - 1, 0), cb0 + j)))
        args += [arr, arr]
    in_specs.append(pl.BlockSpec((8, tc), lambda j, i: (0, j)))
    args.append(w8)
    for arr, cb0 in extras:
        in_specs.append(pl.BlockSpec((tr, tc), lambda j, i, cb0=cb0: (i, cb0 + j)))
        args.append(arr)
    return pl.pallas_call(
        body, name=name, interpret=False,
        out_shape=[jax.ShapeDtypeStruct((rows, c), dt) for dt in outs],
        grid=(nc, nr), in_specs=in_specs,
        out_specs=[pl.BlockSpec((tr, tc), lambda j, i: (i, j)) for _ in outs],
        scratch_shapes=[pltpu.VMEM((tr + 8, tc), F32)],
        compiler_params=_params(("parallel", "arbitrary")),
    )(*args)


def conv_bwd(xs, w8, kw, dy, *, rows, c, tc, tr, name, post, extras=(), outs=(), pre=None):
    nx, ne, no = len(xs), len(extras), len(outs)
    nr, nc = rows // tr, c // tc
    r8 = tr // 8

    def body(*refs):
        x_refs = refs[:nx]
        w_ref, dy_ref, dyn_ref = refs[nx:nx + 3]
        e_refs = refs[nx + 3:nx + 3 + ne]
        first_out = nx + 3 + ne
        o_refs = refs[first_out:first_out + no]
        dw_ref = refs[first_out + no]
        gscr = refs[-1]
        i = pl.program_id(1)
        gscr[tr:tr + 8, :] = jnp.where(i < nr - 1, dyn_ref[...].astype(F32), 0.0)

        def fill(s, carry):
            r0 = pl.multiple_of(s * STRIP, STRIP)
            gscr[pl.ds(r0, STRIP), :] = dy_ref[pl.ds(r0, STRIP), :].astype(F32)
            return carry

        def comp(s, dws):
            r0 = pl.multiple_of(s * STRIP, STRIP)
            gwin = gscr[pl.ds(r0, STRIP + 8), :]
            cur = [x_refs[q][pl.ds(r0, STRIP), :].astype(F32) for q in range(nx)]
            x = pre(*cur) if pre else cur[0]
            dx = jnp.zeros((STRIP, tc), F32)
            new = []
            for q in range(kw):
                sh = kw - 1 - q
                ahead = gwin[sh:sh + STRIP]
                dx = dx + w_ref[q:q + 1, :] * ahead
                part = ahead * x
                new.append(dws[q] + part[0:8] + part[8:16])
            res = post(dx, *[e[pl.ds(r0, STRIP), :] for e in e_refs])
            if not isinstance(res, (tuple, list)):
                res = (res,)
            for r, v in zip(o_refs, res):
                r[pl.ds(r0, STRIP), :] = v.astype(r.dtype)
            return tuple(new)

        lax.fori_loop(0, tr // STRIP, fill, 0)
        dws = lax.fori_loop(0, tr // STRIP, comp, tuple(jnp.zeros((8, tc), F32) for _ in range(kw)))

        @pl.when(i == 0)
        def _():
            dw_ref[...] = jnp.zeros((8, tc), F32)

        dw_ref[...] += jnp.concatenate([jnp.sum(t, axis=0, keepdims=True) for t in dws]
                                       + [jnp.zeros((8 - kw, tc), F32)], axis=0)

    in_specs, args = [], []
    for arr, cb0 in xs:
        in_specs.append(pl.BlockSpec((tr, tc), lambda j, i, cb0=cb0: (i, cb0 + j)))
        args.append(arr)
    in_specs.append(pl.BlockSpec((8, tc), lambda j, i: (0, j)))
    in_specs.append(pl.BlockSpec((tr, tc), lambda j, i: (i, j)))
    in_specs.append(pl.BlockSpec((8, tc), lambda j, i: (jnp.minimum((i + 1) * r8, nr * r8 - 1), j)))
    args += [w8, dy, dy]
    for arr, cb0 in extras:
        in_specs.append(pl.BlockSpec((tr, tc), lambda j, i, cb0=cb0: (i, cb0 + j)))
        args.append(arr)
    return pl.pallas_call(
        body, name=name, interpret=False,
        out_shape=[jax.ShapeDtypeStruct((rows, c), dt) for dt in outs] + [jax.ShapeDtypeStruct((8, c), F32)],
        grid=(nc, nr), in_specs=in_specs,
        out_specs=[pl.BlockSpec((tr, tc), lambda j, i: (i, j)) for _ in outs] + [pl.BlockSpec((8, tc), lambda j, i: (0, j))],
        scratch_shapes=[pltpu.VMEM((tr + 8, tc), F32)],
        compiler_params=_params(("parallel", "arbitrary")),
    )(*args)


def rms_fwd(h, w, *, name):
    rows = h.shape[0]
    tr = _pick(rows, (384, 128))

    def fn(i, x, wv):
        r = lax.rsqrt(jnp.mean(x * x, axis=1, keepdims=True) + EPS)
        return x * r * wv

    return rowwise(fn, [cols(h, tr), whole(w)], [out2d(rows, D, BF16, tr)], steps=rows // tr, name=name)[0]


def _rms_bwd_epi(row0, g, x, dr, wv):
    r = lax.rsqrt(jnp.mean(x * x, axis=1, keepdims=True) + EPS)
    xh = x * r
    gw = g * wv
    dx = r * (gw - xh * jnp.mean(gw * xh, axis=1, keepdims=True))
    row = row0 + lax.broadcasted_iota(jnp.int32, (x.shape[0], 1), 0)
    return jnp.where(row >= PAD, dr + dx, 0.0), jnp.sum(g * xh, axis=0, keepdims=True)


def dx_rms_bwd(dy, w, h, nw, dres, *, name, b_chip=False, swap_mid=False):
    return mm(dy, w, tb=True, b_chip=b_chip, swap_mid=swap_mid, tn=D, name=name, epi=_rms_bwd_epi,
              epi_ins=[(h, lambda j: 0), (dres, lambda j: 0)], epi_consts=[nw], epi_outs=[F32],
              epi_accs=[((1, D), F32)])


def _add_loss_epi(row0, t, h, tgt):
    row = row0 + lax.broadcasted_iota(jnp.int32, (t.shape[0], 1), 0)
    tgt = jnp.where(row0 == 0, jnp.concatenate([tgt[-HEAD0:], tgt[:-HEAD0]], axis=0), tgt)
    diff = jnp.where(row >= HEAD0, t + h - tgt, 0.0)
    part = jnp.sum(jnp.sum(diff * diff, axis=1, keepdims=True), axis=0, keepdims=True)
    return diff * (1.0 / D), part * (0.5 / D)


def add_loss(a, w, h, target, *, name):
    return mm(a, w, name=name, epi=_add_loss_epi, epi_ins=[(h, lambda j: 0), (target, lambda j: 0)],
              epi_outs=[F32], epi_accs=[((1, 128), F32)])


def adamw(w, g, m, v, *, name):
    shape = w.shape
    gs = list(g) if isinstance(g, (list, tuple)) else [g]
    nl = len(gs)
    width = shape[-1]
    rows = w.size // width
    rl = rows // nl
    tr = _pick(rl, (256, 176, 128, 64, 16, 8))
    nr = rl // tr
    if w.ndim == 3 and shape[1] % tr == 0:
        per = shape[1] // tr
        view = lambda t: (t, (None, tr, width), lambda i: (i // per, i % per, 0), "r2")
        out = (shape, F32, (None, tr, width), lambda i: (i // per, i % per, 0), "r2")
    else:
        view = lambda t: cols(t.reshape(rows, width), tr)
        out = out2d(rows, width, F32, tr)

    def fn(i, wv, mv, vv, *gvs):
        gv = gvs[0]
        for layer in range(1, nl):
            gv = jnp.where(i >= layer * rl, gvs[layer], gv)
        mn = B1 * mv + (1.0 - B1) * gv
        vn = B2 * vv + (1.0 - B2) * gv * gv
        mh = mn / (1.0 - B1 ** STEP)
        vh = vn / (1.0 - B2 ** STEP)
        return -LR * (mh / (jnp.sqrt(vh) + AEPS) + WD * wv), mn, vn, gv

    g_ins = [(t.reshape(rl, width), (tr, width), lambda i, layer=layer: (jnp.clip(i - layer * nr, 0, nr - 1), 0), "r2")
             for layer, t in enumerate(gs)]
    res = rowwise(fn, [view(t) for t in (w, m, v)] + g_ins, [out] * 4, steps=rows // tr, name=name)
    return [r.reshape(shape) for r in res]


HB = DN_H * CH
PAIR = 3


def _split(a):
    hi = a.astype(BF16)
    return hi, (a - hi.astype(F32)).astype(BF16)


def _dot1(a, b, ca=1, cb=0):
    return _dot(a.astype(BF16), b.astype(BF16), ca, cb)


def _dot3(a, b, ca=1, cb=0):
    ah, al = _split(a)
    bh, bl = _split(b)
    return _dot(ah, bh, ca, cb) + (_dot(ah, bl, ca, cb) + _dot(al, bh, ca, cb))


def _dot01(m01, b, ca=1, cb=0):
    bh, bl = _split(b)
    m = m01.astype(BF16)
    return _dot(m, bh, ca, cb) + _dot(m, bl, ca, cb)


def _stack(x):
    return jnp.concatenate([x[:, h * DN_D:(h + 1) * DN_D] for h in range(DN_H)], axis=0)


def _unstack(x):
    return jnp.concatenate([x[h * CH:(h + 1) * CH] for h in range(DN_H)], axis=1)


def _tri_inv(mats, blk, eye):
    each = lambda f, *lists: [f(*t) for t in zip(*lists)]
    ad = [jnp.where(blk, a, 0.0) for a in mats]
    lo = each(lambda a, d: a - d, mats, ad)
    a2 = each(_dot3, ad, ad)
    a4 = each(_dot3, a2, a2)
    a8 = each(_dot3, a4, a4)
    dgi = each(lambda d, s: _dot3(eye - d, eye + s), ad, a2)
    dgi = each(lambda p, s: _dot3(p, eye + s), dgi, a4)
    dgi = each(lambda p, s: _dot3(p, eye + s), dgi, a8)
    n = each(_dot3, dgi, lo)
    n2 = each(_dot3, n, n)
    return each(_dot3, each(lambda u, v: _dot3(eye - u, eye + v), n, n2), dgi)


def _dn_masks():
    row = lax.broadcasted_iota(jnp.int32, (HB, HB), 0)
    col = lax.broadcasted_iota(jnp.int32, (HB, HB), 1)
    same = (row // CH) == (col // CH)
    incl = jnp.logical_and(same, row >= col)
    strict = jnp.logical_and(same, row > col)
    upper = jnp.logical_and(same, row <= col)
    blk = (row // 16) == (col // 16)
    eye = (row == col).astype(F32)
    return incl, strict, upper, blk, eye


def _dn_chunk(qv, kv, vv, bc, br, incl, strict):
    r64 = lax.broadcasted_iota(jnp.int32, (CH, CH), 0)
    c64 = lax.broadcasted_iota(jnp.int32, (CH, CH), 1)
    dcol = _dot01((r64 >= c64).astype(F32), bc)
    drow = _dot3(br, (r64 <= c64).astype(F32))
    col = lambda m, l0: jnp.concatenate([m[:, l0 + h:l0 + h + 1] for h in range(DN_H)], axis=0)
    b_c = col(bc, 0)
    d_c = col(dcol, 4)
    d_r = jnp.concatenate([drow[4 + h:5 + h, :] for h in range(DN_H)], axis=1)
    d_last_h = [dcol[CH - 1:CH, 4 + h:5 + h] for h in range(DN_H)]
    d_last = jnp.concatenate([jnp.broadcast_to(t, (CH, 1)) for t in d_last_h], axis=0)
    q, k, v = _stack(qv), _stack(kv), _stack(vv)
    dm = jnp.where(incl, jnp.exp(jnp.where(incl, d_c - d_r, 0.0)), 0.0)
    kk = _dot1(k, k, 1, 1)
    a = jnp.where(strict, b_c * kk * dm, 0.0)
    ed = jnp.exp(d_c)
    rhs = jnp.concatenate([v * b_c, k * (b_c * ed)], axis=1)
    qk = _dot1(q, k, 1, 1) * dm
    ekd = jnp.exp(d_last - d_c)
    gl = [jnp.exp(t) for t in d_last_h]
    return q, k, v, b_c, dm, kk, a, ed, rhs, qk, ekd, gl


def dn_fwd(qkv_n, bgcol, bgrow):
    rows = qkv_n.shape[0]
    nch = rows // CH

    def body(q_ref, k_ref, v_ref, bc_ref, br_ref, o_ref, s_out, ti_out, s_scr, prep, prep_qk, prep_gl):
        n = pl.program_id(0)

        @pl.when(n == 0)
        def _():
            s_scr[...] = jnp.zeros(s_scr.shape, F32)
            prep[...] = jnp.zeros(prep.shape, F32)
            prep_qk[...] = jnp.zeros(prep_qk.shape, F32)
            prep_gl[...] = jnp.zeros(prep_gl.shape, F32)

        live = n > 0
        rows_of = [slice(h * CH, (h + 1) * CH) for h in range(DN_H)]
        s = [s_scr[h] for h in range(DN_H)]
        for c in range(PAIR):
            u, w, qd, kd = prep[c, 0], prep[c, 1], prep[c, 2], prep[c, 3]
            for h in range(DN_H):
                s_out[c, h] = s[h]
            v_new = [u[rs] - _dot1(w[rs], s[h]) for h, rs in enumerate(rows_of)]
            o_state = [_dot1(qd[rs], s[h]) for h, rs in enumerate(rows_of)]
            s = [jnp.where(live, prep_gl[c, h:h + 1, 0:1] * s[h] + _dot1(kd[rs], v_new[h], 0, 0), s[h])
                 for h, rs in enumerate(rows_of)]
            o = jnp.concatenate(o_state, axis=0) + _dot1(prep_qk[c], jnp.concatenate(v_new, axis=0))
            o_ref[c * CH:(c + 1) * CH, :] = _unstack(o)
        for h in range(DN_H):
            s_scr[h] = s[h]

        incl, strict, _, blk, eye = _dn_masks()
        parts = []
        for c in range(PAIR):
            rows_c = slice(c * CH, (c + 1) * CH)
            parts.append(_dn_chunk(q_ref[rows_c, :], k_ref[rows_c, :], v_ref[rows_c, :], bc_ref[rows_c, :],
                                   br_ref[c], incl, strict))
        tinvs = _tri_inv([p[6] for p in parts], blk, eye)
        for c, (q, k, v, b_c, dm, kk, a, ed, rhs, qk_n, ekd, gl) in enumerate(parts):
            tinv = tinvs[c]
            ti_out[c] = tinv
            sol = _dot3(tinv, rhs)
            prep[c, 0] = sol[:, :DN_D]
            prep[c, 1] = sol[:, DN_D:]
            prep[c, 2] = q * ed
            prep[c, 3] = k * ekd
            prep_qk[c] = qk_n
            prep_gl[c] = jnp.concatenate([jnp.broadcast_to(t, (1, 128)) for t in gl]
                                         + [jnp.zeros((8 - DN_H, 128), F32)], axis=0)

    assert nch % PAIR == 0
    npair = nch // PAIR
    last = npair - 1
    return pl.pallas_call(
        body, name="dn_fwd", interpret=False,
        out_shape=[jax.ShapeDtypeStruct((rows, DN_DIM), F32),
                   jax.ShapeDtypeStruct((nch, DN_H, DN_D, DN_D), F32),
                   jax.ShapeDtypeStruct((nch, HB, HB), F32)],
        grid=(npair + 1,),
        in_specs=[pl.BlockSpec((PAIR * CH, DN_DIM), lambda n: (jnp.minimum(n, last), 0)),
                  pl.BlockSpec((PAIR * CH, DN_DIM), lambda n: (jnp.minimum(n, last), 1)),
                  pl.BlockSpec((PAIR * CH, DN_DIM), lambda n: (jnp.minimum(n, last), 2)),
                  pl.BlockSpec((PAIR * CH, 128), lambda n: (jnp.minimum(n, last), 0)),
                  pl.BlockSpec((PAIR, 8, CH), lambda n: (jnp.minimum(n, last), 0, 0))],
        out_specs=[pl.BlockSpec((PAIR * CH, DN_DIM), lambda n: (jnp.maximum(n - 1, 0), 0)),
                   pl.BlockSpec((PAIR, DN_H, DN_D, DN_D), lambda n: (jnp.maximum(n - 1, 0), 0, 0, 0)),
                   pl.BlockSpec((PAIR, HB, HB), lambda n: (jnp.minimum(n, last), 0, 0))],
        scratch_shapes=[pltpu.VMEM((DN_H, DN_D, DN_D), F32), pltpu.VMEM((PAIR, 4, HB, DN_D), F32),
                        pltpu.VMEM((PAIR, HB, HB), F32), pltpu.VMEM((PAIR, 8, 128), F32)],
        compiler_params=_params(("arbitrary",)),
    )(qkv_n, qkv_n, qkv_n, bgcol, bgrow)


def dn_bwd(qkv_n, bgcol, bgrow, s_all, ti_all, do):
    rows = qkv_n.shape[0]
    nch = rows // CH

    def body(q_ref, k_ref, v_ref, bc_ref, br_ref, s_ref, ti_ref, do_ref, dq_ref, dk_ref, dv_ref, dbg_ref, ds_scr):
        n = pl.program_id(0)

        @pl.when(n == 0)
        def _():
            ds_scr[...] = jnp.zeros(ds_scr.shape, F32)

        incl, strict, upper, _, _ = _dn_masks()
        rsum = lambda t: jnp.sum(t, axis=1, keepdims=True)
        rows_of = [slice(h * CH, (h + 1) * CH) for h in range(DN_H)]
        heads_of = lambda f: jnp.concatenate([f(h, rs) for h, rs in enumerate(rows_of)], axis=0)
        cs = []
        for c in reversed(range(PAIR)):
            rc = slice(c * CH, (c + 1) * CH)
            q, k, v, b_c, dm, kk, a, ed, rhs, qk, ekd, gl = _dn_chunk(
                q_ref[rc, :], k_ref[rc, :], v_ref[rc, :], bc_ref[rc, :], br_ref[c], incl, strict)
            cs.append(dict(rc=rc, q=q, k=k, v=v, b_c=b_c, dm=dm, kk=kk, a=a, ed=ed, rhs=rhs, qk=qk, ekd=ekd, gl=gl,
                           tinv=ti_ref[c], g=_stack(do_ref[rc, :]), s=[s_ref[c, h] for h in range(DN_H)]))
        for t in cs:
            t["sol"] = _dot3(t["tinv"], t["rhs"])
        for t in cs:
            t["u"], t["w"] = t["sol"][:, :DN_D], t["sol"][:, DN_D:]
            t["qd"], t["kd"] = t["q"] * t["ed"], t["k"] * t["ekd"]
            t["v_new"] = heads_of(lambda h, rs: t["u"][rs] - _dot1(t["w"][rs], t["s"][h]))
            t["dv0"] = _dot1(t["qk"], t["g"], 0, 0)
            t["ds0"] = [_dot1(t["qd"][rs], t["g"][rs], 0, 0) for rs in rows_of]
            t["dqd"] = heads_of(lambda h, rs: _dot1(t["g"][rs], t["s"][h], 1, 1))
        for t in cs:
            t["dqk"] = _dot1(t["g"], t["v_new"], 1, 1)
        ds = [ds_scr[h] for h in range(DN_H)]
        for t in cs:
            t["ds"] = ds
            t["dv_new"] = t["dv0"] + heads_of(lambda h, rs: _dot1(t["kd"][rs], ds[h]))
            ds = [t["ds0"][h] + t["gl"][h] * ds[h] - _dot1(t["w"][rs], t["dv_new"][rs], 0, 0)
                  for h, rs in enumerate(rows_of)]
        for h in range(DN_H):
            ds_scr[h] = ds[h]
        for t in cs:
            t["dkd"] = heads_of(lambda h, rs: _dot1(t["v_new"][rs], t["ds"][h], 1, 1))
            dw = heads_of(lambda h, rs: -_dot1(t["dv_new"][rs], t["s"][h], 1, 1))
            t["dsol"] = jnp.concatenate([t["dv_new"], dw], axis=1)
        for t in cs:
            t["drhs"] = _dot3(t["tinv"], t["dsol"], 0, 0)
        for t in cs:
            t["da"] = jnp.where(strict, -_dot1(t["drhs"], t["sol"], 1, 1), 0.0)
        rowi = lax.broadcasted_iota(jnp.int32, (CH, 1), 0)
        lane = lax.broadcasted_iota(jnp.int32, (CH, 128), 1)
        for t in cs:
            q, k, v, b_c, dm, ed, da, dqk = t["q"], t["k"], t["v"], t["b_c"], t["dm"], t["ed"], t["da"], t["dqk"]
            drhs_u, drhs_w = t["drhs"][:, :DN_D], t["drhs"][:, DN_D:]
            s2 = rsum(drhs_w * k)
            dbeta = rsum(drhs_u * v) + s2 * ed + rsum(da * t["kk"] * dm)
            dkk = da * b_c * dm
            dqkr = dqk * dm
            mmat = da * t["a"] + dqk * t["qk"]
            tmp = rsum(t["dkd"] * t["kd"])
            dd = (s2 * b_c * ed + rsum(mmat) - _dot3(mmat, jnp.ones((HB, 128), F32), 0, 0)[:, :1]
                  + rsum(t["dqd"] * t["qd"]) - tmp)
            last = []
            for h, rs in enumerate(rows_of):
                dgl = jnp.sum(rsum(t["s"][h] * t["ds"][h]), axis=0, keepdims=True)
                dd_last = jnp.sum(tmp[rs], axis=0, keepdims=True) + dgl * t["gl"][h]
                last.append(jnp.where(rowi == CH - 1, dd_last, 0.0))
            dd = dd + jnp.concatenate(last, axis=0)
            rc = t["rc"]
            dq_ref[rc, :] = _unstack(_dot1(dqkr, k) + t["dqd"] * ed)
            dk_ref[rc, :] = _unstack(drhs_w * (b_c * ed) + _dot1(dkk, k) + _dot1(dkk, k, 0, 0) + _dot1(dqkr, q, 0, 0)
                                     + t["dkd"] * t["ekd"])
            dv_ref[rc, :] = _unstack(drhs_u * b_c)
            dg = _dot01(upper.astype(F32), jnp.broadcast_to(dd, (HB, 128)))[:, :1]
            out = jnp.zeros((CH, 128), F32)
            for h, rs in enumerate(rows_of):
                out = out + jnp.where(lane == h, dbeta[rs], 0.0) + jnp.where(lane == 4 + h, dg[rs], 0.0)
            dbg_ref[rc, :] = out

    assert nch % PAIR == 0
    npair = nch // PAIR
    rev = lambda n: npair - 1 - n
    blk = PAIR * CH
    return pl.pallas_call(
        body, name="dn_bwd", interpret=False,
        out_shape=[jax.ShapeDtypeStruct((rows, DN_DIM), F32)] * 3 + [jax.ShapeDtypeStruct((rows, 128), F32)],
        grid=(npair,),
        in_specs=[pl.BlockSpec((blk, DN_DIM), lambda n: (rev(n), 0)),
                  pl.BlockSpec((blk, DN_DIM), lambda n: (rev(n), 1)),
                  pl.BlockSpec((blk, DN_DIM), lambda n: (rev(n), 2)),
                  pl.BlockSpec((blk, 128), lambda n: (rev(n), 0)),
                  pl.BlockSpec((PAIR, 8, CH), lambda n: (rev(n), 0, 0)),
                  pl.BlockSpec((PAIR, DN_H, DN_D, DN_D), lambda n: (rev(n), 0, 0, 0)),
                  pl.BlockSpec((PAIR, HB, HB), lambda n: (rev(n), 0, 0)),
                  pl.BlockSpec((blk, DN_DIM), lambda n: (rev(n), 0))],
        out_specs=[pl.BlockSpec((blk, DN_DIM), lambda n: (rev(n), 0))] * 3 + [pl.BlockSpec((blk, 128), lambda n: (rev(n), 0))],
        scratch_shapes=[pltpu.VMEM((DN_H, DN_D, DN_D), F32)],
        compiler_params=_params(("arbitrary",)),
    )(qkv_n, qkv_n, qkv_n, bgcol, bgrow, s_all, ti_all, do)


def _swa_valid(n):
    c3 = lax.broadcasted_iota(jnp.int32, (NKEY, 4 * BLK), 0)
    r = lax.broadcasted_iota(jnp.int32, (NKEY, 4 * BLK), 1) % BLK
    prev0 = N_META + BLK
    c = jnp.where(c3 < N_META, PAD + c3, jnp.where(c3 < prev0, c3 - N_META, c3 - prev0))
    lo = jnp.where(c3 < N_META, 0, jnp.where(c3 < prev0, r + 1 + jnp.where(n >= 2, 0, BLK), 0))
    hi = jnp.where(c3 < N_META, r + jnp.where(n >= 1, BLK, 0),
                   jnp.where(c3 < prev0, BLK, r - jnp.where(n >= 1, 0, BLK)))
    return jnp.logical_and(c >= lo, c <= hi)


def _swa_probs(qs, kcats, valid, sinks):
    s = [jnp.where(valid, _dot(kc, q, 1, 1), -1e30) for q, kc in zip(qs, kcats)]
    m = [jnp.maximum(jnp.max(t, axis=0, keepdims=True), sk) for t, sk in zip(s, sinks)]
    e = [jnp.where(valid, jnp.exp(t - mx), 0.0) for t, mx in zip(s, m)]
    es = [jnp.exp(sk - mx) for sk, mx in zip(sinks, m)]
    inv = [1.0 / (jnp.sum(t, axis=0, keepdims=True) + u) for t, u in zip(e, es)]
    return [t * i for t, i in zip(e, inv)], [u * i for u, i in zip(es, inv)]


def _swa_group(q_ref, sk_ref, h):
    q4 = jnp.concatenate([q_ref[4 * h + g] for g in range(4)], axis=0)
    sink4 = jnp.concatenate([jnp.full((1, BLK), sk_ref[4 * h + g], F32) for g in range(4)], axis=1)
    return q4, sink4


def _swa_specs():
    q = pl.BlockSpec((SWA_H, BLK, SWA_D), lambda n: (0, n, 0))
    km = pl.BlockSpec((SWA_KV, N_META, SWA_D), lambda n: (0, PAD // N_META, 0))
    kp = pl.BlockSpec((SWA_KV, BLK, SWA_D), lambda n: (0, jnp.maximum(n - 1, 0), 0))
    kc = pl.BlockSpec((SWA_KV, BLK, SWA_D), lambda n: (0, n, 0))
    return [q, km, kp, kc, km, kp, kc]


def swa_fwd(qh, kh, vh, sinks):
    rows = qh.shape[1]
    nb = rows // BLK

    def body(q_ref, km, kp, kc, vm, vp, vc, sk_ref, o_ref):
        n = pl.program_id(0)
        valid = _swa_valid(n)
        kcats = [jnp.concatenate([km[h], kp[h], kc[h]], axis=0) for h in range(SWA_KV)]
        vcats = [jnp.concatenate([vm[h], vp[h], vc[h]], axis=0) for h in range(SWA_KV)]
        qs, sinks4 = zip(*[_swa_group(q_ref, sk_ref, h) for h in range(SWA_KV)])
        ps, _ = _swa_probs(qs, kcats, valid, sinks4)
        o4s = [_dot(p.astype(BF16), vc_, 0, 0) for p, vc_ in zip(ps, vcats)]
        o_ref[...] = jnp.concatenate([o4[g * BLK:(g + 1) * BLK] for o4 in o4s for g in range(4)],
                                     axis=1).astype(BF16)

    return pl.pallas_call(
        body, name="swa_fwd", interpret=False,
        out_shape=jax.ShapeDtypeStruct((rows, SWA_H * SWA_D), BF16),
        grid=(nb,),
        in_specs=_swa_specs() + [pl.BlockSpec(memory_space=pltpu.SMEM)],
        out_specs=pl.BlockSpec((BLK, SWA_H * SWA_D), lambda n: (n, 0)),
        compiler_params=_params(("parallel",)),
    )(qh, kh, kh, kh, vh, vh, vh, sinks)


def swa_bwd(qh, kh, vh, sinks, do):
    rows = qh.shape[1]
    nb = rows // BLK

    def body(q_ref, km, kp, kc, vm, vp, vc, do_ref, sk_ref, dq_ref, dk_ref, dv_ref, dsk_ref):
        n = pl.program_id(0)

        @pl.when(n == 0)
        def _():
            dk_ref[...] = jnp.zeros(dk_ref.shape, F32)
            dv_ref[...] = jnp.zeros(dv_ref.shape, F32)

        valid = _swa_valid(n)
        g_all = do_ref[...]
        rowi = lax.broadcasted_iota(jnp.int32, (SWA_H, 128), 0)
        dsk = jnp.zeros((SWA_H, 128), F32)
        pm = pl.multiple_of(jnp.maximum(n - 1, 0) * BLK, BLK)
        pc = pl.multiple_of(n * BLK, BLK)
        hs = range(SWA_KV)
        kcats = [jnp.concatenate([km[h], kp[h], kc[h]], axis=0) for h in hs]
        vcats = [jnp.concatenate([vm[h], vp[h], vc[h]], axis=0) for h in hs]
        qs, sinks4 = zip(*[_swa_group(q_ref, sk_ref, h) for h in hs])
        g4s = [jnp.concatenate([g_all[:, (4 * h + g) * SWA_D:(4 * h + g + 1) * SWA_D] for g in range(4)], axis=0)
               for h in hs]
        ps, pss = _swa_probs(qs, kcats, valid, sinks4)
        dps = [_dot(vc_, g4, 1, 1) for vc_, g4 in zip(vcats, g4s)]
        deltas = [jnp.sum(p * dp, axis=0, keepdims=True) for p, dp in zip(ps, dps)]
        dss = [(p * (dp - dl)).astype(BF16) for p, dp, dl in zip(ps, dps, deltas)]
        dq4s = [_dot(ds, kc_, 0, 0) for ds, kc_ in zip(dss, kcats)]
        dkcs = [_dot(ds, q4) for ds, q4 in zip(dss, qs)]
        dvcs = [_dot(p.astype(BF16), g4) for p, g4 in zip(ps, g4s)]
        for h in hs:
            t = pss[h] * deltas[h]
            for g in range(4):
                dq_ref[4 * h + g] = dq4s[h][g * BLK:(g + 1) * BLK]
                part = -jnp.sum(t[:, g * BLK:(g + 1) * BLK], axis=1, keepdims=True)
                dsk = dsk + jnp.where(rowi == 4 * h + g, part, 0.0)
            lanes = slice(h * SWA_D, (h + 1) * SWA_D)
            for ref, val in ((dk_ref, dkcs[h]), (dv_ref, dvcs[h])):
                ref[PAD:BLK, lanes] += val[0:N_META]
                ref[pl.ds(pm, BLK), lanes] += val[N_META:N_META + BLK]
                ref[pl.ds(pc, BLK), lanes] += val[N_META + BLK:]
        dsk_ref[0] = dsk

    return pl.pallas_call(
        body, name="swa_bwd", interpret=False,
        out_shape=[jax.ShapeDtypeStruct((SWA_H, rows, SWA_D), F32),
                   jax.ShapeDtypeStruct((rows, SWA_KV * SWA_D), F32),
                   jax.ShapeDtypeStruct((rows, SWA_KV * SWA_D), F32),
                   jax.ShapeDtypeStruct((nb, SWA_H, 128), F32)],
        grid=(nb,),
        in_specs=_swa_specs() + [pl.BlockSpec((BLK, SWA_H * SWA_D), lambda n: (n, 0)),
                                 pl.BlockSpec(memory_space=pltpu.SMEM)],
        out_specs=[pl.BlockSpec((SWA_H, BLK, SWA_D), lambda n: (0, n, 0)),
                   pl.BlockSpec((rows, SWA_KV * SWA_D), lambda n: (0, 0)),
                   pl.BlockSpec((rows, SWA_KV * SWA_D), lambda n: (0, 0)),
                   pl.BlockSpec((1, SWA_H, 128), lambda n: (n, 0, 0))],
        compiler_params=_params(("arbitrary",)),
    )(qh, kh, kh, kh, vh, vh, vh, do, sinks)


QK_W = (SWA_H + SWA_KV) * SWA_D


def _head_mean(t):
    r = lax.broadcasted_iota(jnp.int32, (128, 128), 0) // SWA_D
    c = lax.broadcasted_iota(jnp.int32, (128, 128), 1) // SWA_D
    blk = jnp.where(r == c, 1.0 / SWA_D, 0.0).astype(BF16)
    out = []
    for i in range(t.shape[1] // 128):
        hi, lo = _split(t[:, 128 * i:128 * (i + 1)])
        out.append(_dot(hi, blk) + _dot(lo, blk))
    return jnp.concatenate(out, axis=1)


def _qk_scales(qw, kw):
    scale = SWA_D ** -0.5
    wt = jnp.concatenate([jnp.tile(qw.astype(F32) * scale, (1, SWA_H)), jnp.tile(kw.astype(F32), (1, SWA_KV))], axis=1)
    st = jnp.concatenate([jnp.full((1, SWA_H * SWA_D), scale, F32), jnp.ones((1, SWA_KV * SWA_D), F32)], axis=1)
    return wt, st


def qknorm_fwd(qkv, qw, kw):
    rows = qkv.shape[0]
    tr = _pick(rows, (384, 128))
    wt, _ = _qk_scales(qw, kw)

    def fn(i, x, w):
        xq = x[:, :QK_W]
        y = xq * lax.rsqrt(_head_mean(xq * xq) + EPS) * w
        head = lambda t, j: t[:, j * SWA_D:(j + 1) * SWA_D][None]
        qo = jnp.concatenate([head(y, j) for j in range(SWA_H)], axis=0)
        ko = jnp.concatenate([head(y, SWA_H + j) for j in range(SWA_KV)], axis=0)
        vo = jnp.concatenate([head(x, SWA_H + SWA_KV + j) for j in range(SWA_KV)], axis=0)
        return qo, ko, vo

    hm = lambda nh: ((nh, rows, SWA_D), BF16, (nh, tr, SWA_D), lambda i: (0, i, 0), "r3")
    return rowwise(fn, [cols(qkv, tr), whole(wt)], [hm(SWA_H), hm(SWA_KV), hm(SWA_KV)],
                   steps=rows // tr, name="qknorm_fwd")


def qknorm_bwd(qkv, qw, kw, dqh, dk, dv):
    rows = qkv.shape[0]
    tr = _pick(rows, (384, 128))
    wt, st = _qk_scales(qw, kw)

    def fn(i, x, w, sc, dq, dkv, dvv):
        xq = x[:, :QK_W]
        dy = jnp.concatenate([dq[j] for j in range(SWA_H)] + [dkv], axis=1)
        r = lax.rsqrt(_head_mean(xq * xq) + EPS)
        xh = xq * r
        gw = dy * w
        dx = r * (gw - xh * _head_mean(gw * xh))
        return jnp.concatenate([dx, dvv], axis=1), jnp.sum(dy * sc * xh, axis=0, keepdims=True)

    dqkv, dw = rowwise(fn, [cols(qkv, tr), whole(wt), whole(st), heads(dqh, tr), cols(dk, tr), cols(dv, tr)],
                       [out2d(rows, 1536, BF16, tr)], steps=rows // tr, name="qknorm_bwd", accs=[((1, QK_W), F32)])
    dw = dw.reshape(SWA_H + SWA_KV, SWA_D)
    return dqkv, jnp.sum(dw[:SWA_H], axis=0, keepdims=True), jnp.sum(dw[SWA_H:], axis=0, keepdims=True)


def _place():
    return lax.axis_index("x"), lax.axis_index("y"), lax.axis_index("c")


ANY = pl.BlockSpec(memory_space=pl.ANY)


def _rcopy(ssem, rsem, k, src, dst, to):
    return pltpu.make_async_remote_copy(src_ref=src, dst_ref=dst, send_sem=ssem.at[k], recv_sem=rsem.at[k],
                                        device_id=to, device_id_type=MESH)


def gather_weights(shards, small):
    n = len(shards)
    halves = [t.shape[0] // 2 for t in shards]

    def body(*refs):
        s_refs, small_ref = refs[:n], refs[n]
        o_refs, osmall = refs[n + 1:2 * n + 1], refs[2 * n + 1]
        ssem, rsem, lsem = refs[2 * n + 2:]
        x, y, c = _place()
        me = 2 * x + y
        chips = [(1 - x, y), (x, 1 - y), (1 - x, 1 - y)]

        def half(k, s, hh):
            return o_refs[k].at[s, pl.ds(hh * halves[k], halves[k]), :]

        loc = pltpu.make_async_copy(small_ref, osmall.at[me], lsem)
        loc.start()
        sends = []
        for k in range(n):
            for j, (px, py) in enumerate(chips):
                sends.append(_rcopy(ssem, rsem, 6 * k + j, s_refs[k].at[pl.ds(c * halves[k], halves[k]), :],
                                    half(k, me, c), (px, py, c)))
        for j, (px, py) in enumerate(chips):
            sends.append(_rcopy(ssem, rsem, 6 * n + j, small_ref, osmall.at[me], (px, py, c)))
        for cp in sends:
            cp.start()
        for k in range(n):
            for j, (px, py) in enumerate(chips):
                s = 2 * px + py
                _rcopy(ssem, rsem, 6 * k + j, half(k, s, c), half(k, s, c), (x, y, c)).wait_recv()
                fwd = _rcopy(ssem, rsem, 6 * k + 3 + j, half(k, s, c), half(k, s, c), (x, y, 1 - c))
                fwd.start()
                sends.append(fwd)
        for k in range(n):
            for j, (px, py) in enumerate(chips):
                s = 2 * px + py
                _rcopy(ssem, rsem, 6 * k + 3 + j, half(k, s, 1 - c), half(k, s, 1 - c), (x, y, c)).wait_recv()
        for j, (px, py) in enumerate(chips):
            s = 2 * px + py
            _rcopy(ssem, rsem, 6 * n + j, osmall.at[s], osmall.at[s], (x, y, c)).wait_recv()
        for cp in sends:
            cp.wait_send()
        loc.wait()

    res = pl.pallas_call(
        body, name="gather_weights", interpret=False,
        out_shape=[jax.ShapeDtypeStruct((4,) + t.shape, t.dtype) for t in shards]
        + [jax.ShapeDtypeStruct((4, SW_ROWS, 1024), F32)],
        in_specs=[ANY] * (n + 1), out_specs=[ANY] * (n + 1),
        scratch_shapes=[pltpu.SemaphoreType.DMA((6 * n + 3,)), pltpu.SemaphoreType.DMA((6 * n + 3,)),
                        pltpu.SemaphoreType.DMA],
    )(*shards, small)
    return res[:n], res[n]


def _handshake(peers):
    barrier = pltpu.get_barrier_semaphore()
    for peer in peers:
        pl.semaphore_signal(barrier, inc=1, device_id=peer, device_id_type=MESH)
    pl.semaphore_wait(barrier, len(peers))


def gather_weights_beside(shards, cid, name):
    n = len(shards)
    halves = [t.shape[0] // 2 for t in shards]

    def body(*refs):
        s_refs, o_refs, ssem, rsem = refs[:n], refs[n:2 * n], refs[2 * n], refs[2 * n + 1]
        x, y, c = _place()
        me = 2 * x + y
        chips = [(1 - x, y), (x, 1 - y), (1 - x, 1 - y)]
        _handshake([(px, py, c) for px, py in chips] + [(x, y, 1 - c)])

        def half(k, s, hh):
            return o_refs[k].at[s, pl.ds(hh * halves[k], halves[k]), :]

        sends = []
        for k in range(n):
            for j, (px, py) in enumerate(chips):
                sends.append(_rcopy(ssem, rsem, 6 * k + j, s_refs[k].at[pl.ds(c * halves[k], halves[k]), :],
                                    half(k, me, c), (px, py, c)))
        for cp in sends:
            cp.start()
        for k in range(n):
            for j, (px, py) in enumerate(chips):
                s = 2 * px + py
                _rcopy(ssem, rsem, 6 * k + j, half(k, s, c), half(k, s, c), (x, y, c)).wait_recv()
                fwd = _rcopy(ssem, rsem, 6 * k + 3 + j, half(k, s, c), half(k, s, c), (x, y, 1 - c))
                fwd.start()
                sends.append(fwd)
        for k in range(n):
            for j, (px, py) in enumerate(chips):
                s = 2 * px + py
                _rcopy(ssem, rsem, 6 * k + 3 + j, half(k, s, 1 - c), half(k, s, 1 - c), (x, y, c)).wait_recv()
        for cp in sends:
            cp.wait_send()

    return pl.kernel(
        body, name=name,
        out_type=[jax.ShapeDtypeStruct((4,) + t.shape, t.dtype) for t in shards],
        mesh=plsc.ScalarSubcoreMesh(axis_name="sequencer", num_cores=1),
        scratch_types=[pltpu.SemaphoreType.DMA((6 * n,)), pltpu.SemaphoreType.DMA((6 * n,))],
        compiler_params=pltpu.CompilerParams(collective_id=cid),
    )(*shards)


def swap_halves(gs, *, name):
    n = len(gs)

    def body(*refs):
        g_refs, o_refs, ssem, rsem = refs[:n], refs[n:2 * n], refs[2 * n], refs[2 * n + 1]
        x, y, c = _place()
        cps = []
        for k in range(n):
            hk = g_refs[k].shape[1] // 2
            cps.append(_rcopy(ssem, rsem, k, g_refs[k].at[:, pl.ds((1 - c) * hk, hk), :], o_refs[k], (x, y, 1 - c)))
        for cp in cps:
            cp.start()
        for cp in cps:
            cp.wait()

    return pl.pallas_call(
        body, name=name, interpret=False,
        out_shape=[jax.ShapeDtypeStruct((4, t.shape[1] // 2, t.shape[2]), t.dtype) for t in gs],
        in_specs=[ANY] * n, out_specs=[ANY] * n,
        scratch_shapes=[pltpu.SemaphoreType.DMA((n,)), pltpu.SemaphoreType.DMA((n,))],
    )(*gs)


def _sum_rows(hk):
    return _pick(hk, (512, 352, 256, 128))


def pair_sum(g, other, c_idx, *, name):
    _, hk, width = other.shape
    tr = _sum_rows(hk)
    nbk = hk // tr

    def body(c_ref, g_ref, o_ref, out_ref):
        out_ref[...] = (g_ref[...].astype(F32) + o_ref[...].astype(F32)).astype(BF16)

    return pl.pallas_call(
        body, name=name, interpret=False,
        out_shape=jax.ShapeDtypeStruct((4, hk, width), BF16),
        grid_spec=pltpu.PrefetchScalarGridSpec(
            num_scalar_prefetch=1, grid=(4, nbk),
            in_specs=[pl.BlockSpec((1, tr, width), lambda s, i, c_ref: (s, c_ref[0] * nbk + i, 0)),
                      pl.BlockSpec((1, tr, width), lambda s, i, c_ref: (s, i, 0))],
            out_specs=pl.BlockSpec((1, tr, width), lambda s, i, c_ref: (s, i, 0))),
        compiler_params=_params(("parallel", "parallel")),
    )(c_idx, g, other)


def chip_sum(p, got, idx, *, name):
    _, hk, width = got.shape
    tr = _sum_rows(hk)
    nbk = hk // tr

    def body(idx_ref, p_ref, g_ref, out_ref):
        acc = p_ref[0].astype(F32)
        for j in range(3):
            acc = acc + g_ref[j].astype(F32)
        out_ref[0] = acc

    return pl.pallas_call(
        body, name=name, interpret=False,
        out_shape=jax.ShapeDtypeStruct((2, hk, width), F32),
        grid_spec=pltpu.PrefetchScalarGridSpec(
            num_scalar_prefetch=1, grid=(nbk,),
            in_specs=[pl.BlockSpec((1, tr, width), lambda i, idx_ref: (idx_ref[0], i, 0)),
                      pl.BlockSpec((3, tr, width), lambda i, idx_ref: (0, i, 0))],
            out_specs=pl.BlockSpec((1, tr, width), lambda i, idx_ref: (idx_ref[1], i, 0))),
        compiler_params=_params(("parallel",)),
    )(idx, p, got)


def join_halves(qs):
    n = len(qs)

    def body(*refs):
        q_refs, o_refs, ssem, rsem = refs[:n], refs[n:2 * n], refs[2 * n], refs[2 * n + 1]
        x, y, c = _place()
        cps = [_rcopy(ssem, rsem, k, q_refs[k].at[c], o_refs[k].at[c], (x, y, 1 - c)) for k in range(n)]
        for cp in cps:
            cp.start()
        for k in range(n):
            _rcopy(ssem, rsem, k, q_refs[k].at[c], o_refs[k].at[1 - c], (x, y, 1 - c)).wait_recv()
        for cp in cps:
            cp.wait_send()

    return pl.pallas_call(
        body, name="join_halves", interpret=False,
        out_shape=[jax.ShapeDtypeStruct(t.shape, t.dtype) for t in qs],
        in_specs=[ANY] * n, out_specs=[ANY] * n, input_output_aliases={k: k for k in range(n)},
        scratch_shapes=[pltpu.SemaphoreType.DMA((n,)), pltpu.SemaphoreType.DMA((n,))],
    )(*qs)


def scatter_chips_beside(ps, cid, name):
    n = len(ps)

    def body(*refs):
        p_refs, o_refs, ssem, rsem = refs[:n], refs[n:2 * n], refs[2 * n], refs[2 * n + 1]
        x, y, c = _place()
        chips = [(1 - x, y), (x, 1 - y), (1 - x, 1 - y)]
        _handshake([(px, py, c) for px, py in chips])
        cps = [_rcopy(ssem, rsem, 3 * k + j, p_refs[k].at[2 * px + py], o_refs[k].at[j], (px, py, c))
               for k in range(n) for j, (px, py) in enumerate(chips)]
        for cp in cps:
            cp.start()
        for cp in cps:
            cp.wait()

    return pl.kernel(
        body, name=name, out_type=[jax.ShapeDtypeStruct((3,) + t.shape[1:], t.dtype) for t in ps],
        mesh=plsc.ScalarSubcoreMesh(axis_name="sequencer", num_cores=1),
        scratch_types=[pltpu.SemaphoreType.DMA((3 * n,)), pltpu.SemaphoreType.DMA((3 * n,))],
        compiler_params=pltpu.CompilerParams(collective_id=cid),
    )(*ps)


def reduce_begin(gs, names, c_idx, cid, tag):
    others = swap_halves(gs, name=f"swap_halves_{tag}")
    pairs = [pair_sum(g, o, c_idx, name=f"pair_sum_{nm}") for g, o, nm in zip(gs, others, names)]
    return pairs, scatter_chips_beside(pairs, cid, f"scatter_chips_{tag}")


def reduce_end(pairs, gots, names, idx):
    mine = [chip_sum(p, g, idx, name=f"chip_sum_{nm}") for p, g, nm in zip(pairs, gots, names)]
    return [q.reshape(2 * q.shape[1], q.shape[2]) for q in join_halves(mine)]


def gather_small(v):
    def body(v_ref, o_ref, ssem, rsem, lsem):
        x, y, c = _place()
        peers = []
        for k in range(1, 8):
            fx, fy, fc = (k >> 2) & 1, (k >> 1) & 1, k & 1
            peers.append((1 - x if fx else x, 1 - y if fy else y, 1 - c if fc else c))
        _handshake(peers)
        loc = pltpu.make_async_copy(v_ref, o_ref.at[4 * x + 2 * y + c], lsem)
        loc.start()
        cps = []
        for k, (px, py, pc) in enumerate(peers):
            cps.append((pltpu.make_async_remote_copy(
                src_ref=v_ref, dst_ref=o_ref.at[4 * x + 2 * y + c], send_sem=ssem.at[k], recv_sem=rsem.at[k],
                device_id=(px, py, pc), device_id_type=MESH), 4 * px + 2 * py + pc))
        for cp, _ in cps:
            cp.start()
        for k, (cp, peer) in enumerate(cps):
            pltpu.make_async_remote_copy(
                src_ref=v_ref, dst_ref=o_ref.at[peer], send_sem=ssem.at[k], recv_sem=rsem.at[k],
                device_id=(x, y, c), device_id_type=MESH).wait_recv()
        for cp, _ in cps:
            cp.wait_send()
        loc.wait()

    return pl.kernel(
        body, name="gather_small", out_type=jax.ShapeDtypeStruct((8, SV_ROWS, 1024), F32),
        mesh=plsc.ScalarSubcoreMesh(axis_name="sequencer", num_cores=1),
        scratch_types=[pltpu.SemaphoreType.DMA((7,)), pltpu.SemaphoreType.DMA((7,)), pltpu.SemaphoreType.DMA],
        compiler_params=pltpu.CompilerParams(collective_id=6),
    )(v)


def sum_slots(a):
    def fn(i, t):
        acc = t[0]
        for k in range(1, 8):
            acc = acc + t[k]
        return acc

    return rowwise(fn, [whole(a)], [((SV_ROWS, 1024), F32, (SV_ROWS, 1024), lambda i: (0, 0), "w")], steps=1,
                   name="sum_slots")[0]


def _head_rms(x, nw):
    xs, rs = [], []
    for h in range(DN_H):
        xh = x[:, h * DN_D:(h + 1) * DN_D]
        r = lax.rsqrt(jnp.mean(xh * xh, axis=1, keepdims=True) + EPS)
        xs.append(xh * r)
        rs.append(r)
    return xs, rs


def bg_fwd(p, alog, dtb):
    rows = p.shape[0]
    tr = _pick(rows, (384, 128))

    def fn(i, x, al, dt):
        lane = lax.broadcasted_iota(jnp.int32, x.shape, 1)
        row = i + lax.broadcasted_iota(jnp.int32, x.shape, 0)
        g = -jnp.exp(al) * _softplus(x + dt)
        out = jnp.where(lane < 4, _sigmoid(x), jnp.where(lane < 8, g, 0.0))
        return jnp.where(row >= PAD, out, 0.0)

    return rowwise(fn, [cols(p, tr, 128, BG0 // 128), whole(alog), whole(dtb)], [out2d(rows, 128, F32, tr)],
                   steps=rows // tr, name="bg_fwd")[0]


def bg_bwd(p, alog, dtb, dbg):
    rows = p.shape[0]
    tr = _pick(rows, (384, 128))

    def fn(i, x, al, dt, g_in):
        lane = lax.broadcasted_iota(jnp.int32, x.shape, 1)
        row = i + lax.broadcasted_iota(jnp.int32, x.shape, 0)
        live = row >= PAD
        is_b = jnp.logical_and(live, lane < 4)
        is_g = jnp.logical_and(live, jnp.logical_and(lane >= 4, lane < 8))
        beta = _sigmoid(x)
        ea = jnp.exp(al)
        g = -ea * _softplus(x + dt)
        dalpha = jnp.where(is_g, g_in * (-ea) * _sigmoid(x + dt), 0.0)
        dx = jnp.where(is_b, g_in * beta * (1.0 - beta), dalpha)
        dal = jnp.sum(jnp.where(is_g, g_in * g, 0.0), axis=0, keepdims=True)
        return jnp.concatenate([dx, jnp.zeros(x.shape, F32)], axis=1), dal, jnp.sum(dalpha, axis=0, keepdims=True)

    return rowwise(fn, [cols(p, tr, 128, BG0 // 128), whole(alog), whole(dtb), cols(dbg, tr)],
                   [out2d(rows, 256, BF16, tr)], steps=rows // tr, name="bg_bwd",
                   accs=[((1, 128), F32), ((1, 128), F32)])


def dn_qkv_post(j, y):
    xs = _silu(y)
    sc = jnp.where(j == 0, DN_D ** -0.5, 1.0)
    outs = []
    for h in range(DN_H):
        xh = xs[:, h * DN_D:(h + 1) * DN_D]
        r = lax.rsqrt(jnp.sum(xh * xh, axis=1, keepdims=True) + EPS)
        outs.append(jnp.where(j < 2, xh * r * sc, xh))
    return jnp.concatenate(outs, axis=1), y


def dn_qkv_bwd(cq, dq, dk, dv):
    rows = cq.shape[0]
    tr = _pick(rows, (384, 128))

    def fn(i, c0, c1, c2, g0, g1, g2):
        pieces = []
        for kind, (cv, g) in enumerate(((c0, g0), (c1, g1), (c2, g2))):
            xs = _silu(cv)
            if kind < 2:
                sc = DN_D ** -0.5 if kind == 0 else 1.0
                ds = []
                for h in range(DN_H):
                    sl = slice(h * DN_D, (h + 1) * DN_D)
                    xh, gh = xs[:, sl], g[:, sl]
                    r = lax.rsqrt(jnp.sum(xh * xh, axis=1, keepdims=True) + EPS)
                    xn = xh * r
                    ds.append(sc * r * (gh - xn * jnp.sum(gh * xn, axis=1, keepdims=True)))
                dxs = jnp.concatenate(ds, axis=1)
            else:
                dxs = g
            pieces.append(dxs * _dsilu(cv))
        return jnp.concatenate(pieces, axis=1)

    ins = [cols(cq, tr, DN_DIM, k) for k in range(3)] + [cols(t, tr) for t in (dq, dk, dv)]
    return rowwise(fn, ins, [out2d(rows, 3 * DN_DIM, F32, tr)], steps=rows // tr, name="dn_qkv_bwd")[0]


def dn_out_fwd(o, p, nw):
    rows = o.shape[0]
    tr = _pick(rows, (384, 128))

    def fn(i, ov, z, w):
        xs, _ = _head_rms(ov, w)
        return jnp.concatenate(xs, axis=1) * jnp.concatenate([w] * DN_H, axis=1) * _silu(z)

    return rowwise(fn, [cols(o, tr), cols(p, tr, DN_DIM, 6), whole(nw)], [out2d(rows, DN_DIM, BF16, tr)],
                   steps=rows // tr, name="dn_out_fwd")[0]


def dn_out_bwd(o, p, nw, dymix):
    rows = o.shape[0]
    tr = _pick(rows, (384, 128))

    def fn(i, ov, z, w, dy):
        xs, rs = _head_rms(ov, w)
        sz = _silu(z)
        dn = dy * sz
        dos, dw = [], jnp.zeros((1, DN_D), F32)
        for h in range(DN_H):
            sl = slice(h * DN_D, (h + 1) * DN_D)
            gw = dn[:, sl] * w
            dos.append(rs[h] * (gw - xs[h] * jnp.mean(gw * xs[h], axis=1, keepdims=True)))
            dw = dw + jnp.sum(dn[:, sl] * xs[h], axis=0, keepdims=True)
        n = jnp.concatenate(xs, axis=1) * jnp.concatenate([w] * DN_H, axis=1)
        return jnp.concatenate(dos, axis=1), dy * n * _dsilu(z), dw

    return rowwise(fn, [cols(o, tr), cols(p, tr, DN_DIM, 6), whole(nw), cols(dymix, tr, DN_DIM, 1)],
                   [out2d(rows, DN_DIM, F32, tr), out2d(rows, DN_DIM, BF16, tr)], steps=rows // tr,
                   name="dn_out_bwd", accs=[((1, DN_D), F32)])


def conv_a_pre_bwd(dymix, cv, p):
    rows = cv.shape[0]
    tr = _pick(rows, (384, 128))

    def fn(i, dy, c, go):
        return dy * c, dy * go

    return rowwise(fn, [cols(dymix, tr, D_CONV, 0), cols(cv, tr), cols(p, tr, D_CONV, 1)],
                   [out2d(rows, D_CONV, BF16, tr), out2d(rows, D_CONV, F32, tr)], steps=rows // tr,
                   name="conv_a_pre_bwd")


def _rows8(w):
    return jnp.pad(w.astype(F32), ((0, 8 - w.shape[0]), (0, 0)))


def _lanes(v, at):
    return jnp.pad(v.astype(F32), (at, 128 - at - v.shape[0]))[None]


def add_norm(a, w, h, next_nw, *, name):
    return mm(a, w, name=name, epi=_add_norm_epi, epi_ins=[(h, lambda j: 0)], epi_consts=[next_nw],
              epi_outs=[F32, BF16])


def _add_norm_epi(row0, t, h, nw):
    x = t + h
    return x, x * lax.rsqrt(jnp.mean(x * x, axis=1, keepdims=True) + EPS) * nw


def ffn_up_conv(hn, w_up, cw8, *, name):
    rows = hn.shape[0]
    tn = w_up.shape[2]
    tm = _pick(rows, (384, 128))
    nr = rows // tm

    def body(x_ref, wg_ref, wv_ref, w_ref, ug_ref, uv_ref, gc_ref, a_ref, carry, scr):
        i = pl.program_id(1)
        x = x_ref[...]
        gate = _dot(x, wg_ref[...])
        val = _dot(x, wv_ref[...])
        ug_ref[...] = gate.astype(BF16)
        uv_ref[...] = val.astype(BF16)
        scr[0:8, :] = jnp.where(i > 0, carry[...], 0.0)
        scr[8:8 + tm, :] = gate
        carry[...] = gate[tm - 8:tm]
        y = jnp.zeros((tm, tn), F32)
        for q in range(3):
            sh = 2 - q
            y = y + w_ref[q:q + 1, :] * scr[8 - sh:8 - sh + tm, :]
        gc_ref[...] = y.astype(BF16)
        a_ref[...] = (_silu(y) * val).astype(BF16)

    half = pl.BlockSpec((tm, tn), lambda j, i: (i, j))
    return pl.pallas_call(
        body, name=name, interpret=False,
        out_shape=[jax.ShapeDtypeStruct((rows, D_FF), BF16)] * 4,
        grid=(D_FF // tn, nr),
        in_specs=[pl.BlockSpec((tm, D), lambda j, i: (i, 0)),
                  pl.BlockSpec((None, D, tn), lambda j, i: (j, 0, 0)),
                  pl.BlockSpec((None, D, tn), lambda j, i: (j + D_FF // tn, 0, 0)),
                  pl.BlockSpec((8, tn), lambda j, i: (0, j))],
        out_specs=[half] * 4,
        scratch_shapes=[pltpu.VMEM((8, tn), F32), pltpu.VMEM((tm + 8, tn), F32)],
        compiler_params=_params(("arbitrary", "arbitrary")),
    )(hn, w_up, w_up, cw8)


def ffn_down_bwd(dh, w_down, gc, uv, ug, cw8, *, name):
    rows = dh.shape[0]
    tn = D_FF // 2
    tm = _pick(rows, (384, 128))
    nr = rows // tm
    r8 = tm // 8

    def body(dh_ref, w_ref, gc_ref, uv_ref, ug_ref, halo_ref, cw_ref, du_ref, dw_ref, carry, gscr, xscr):
        ip = pl.program_id(1)
        i = nr - 1 - ip
        da = _dot(dh_ref[...].astype(BF16), w_ref[...], 1, 1)
        c, val = gc_ref[...].astype(F32), uv_ref[...].astype(F32)
        dgc = da * val * _dsilu(c)
        du_ref[:, tn:] = (da * _silu(c)).astype(BF16)
        gscr[0:tm, :] = dgc
        gscr[tm:tm + 8, :] = jnp.where(ip > 0, carry[...], 0.0)
        carry[...] = dgc[0:8]
        xscr[0:8, :] = jnp.where(i > 0, halo_ref[...].astype(F32), 0.0)
        xscr[8:8 + tm, :] = ug_ref[...].astype(F32)
        dx = jnp.zeros((tm, tn), F32)
        dws = []
        for q in range(3):
            sh = 2 - q
            dx = dx + cw_ref[q:q + 1, :] * gscr[sh:sh + tm, :]
            dws.append(jnp.sum(dgc * xscr[8 - sh:8 - sh + tm, :], axis=0, keepdims=True))
        du_ref[:, :tn] = dx.astype(BF16)

        @pl.when(ip == 0)
        def _():
            dw_ref[...] = jnp.zeros((8, tn), F32)

        dw_ref[...] += jnp.concatenate(dws + [jnp.zeros((5, tn), F32)], axis=0)

    rev = lambda ip: nr - 1 - ip
    tile = lambda arr: pl.BlockSpec((tm, tn), lambda j, ip: (rev(ip), j))
    return pl.pallas_call(
        body, name=name, interpret=False,
        out_shape=[jax.ShapeDtypeStruct((rows, 2 * D_FF), BF16), jax.ShapeDtypeStruct((8, D_FF), F32)],
        grid=(2, nr),
        in_specs=[pl.BlockSpec((tm, D), lambda j, ip: (rev(ip), 0)),
                  pl.BlockSpec((tn, D), lambda j, ip: (j, 0)),
                  tile(gc), tile(uv), tile(ug),
                  pl.BlockSpec((8, tn), lambda j, ip: (jnp.maximum(rev(ip) * r8 - 1, 0), j)),
                  pl.BlockSpec((8, tn), lambda j, ip: (0, j))],
        out_specs=[pl.BlockSpec((tm, 2 * tn), lambda j, ip: (rev(ip), j)),
                   pl.BlockSpec((8, tn), lambda j, ip: (0, j))],
        scratch_shapes=[pltpu.VMEM((8, tn), F32), pltpu.VMEM((tm + 8, tn), F32), pltpu.VMEM((tm + 8, tn), F32)],
        compiler_params=_params(("arbitrary", "arbitrary")),
    )(dh, w_down, gc, uv, ug, ug, cw8)


def ffn_fwd(h, hn, w_up, cw8, w_down, tag, next_nw=None, target=None):
    ug, uv, gc, a = ffn_up_conv(hn, w_up, cw8, name=f"ffn{tag}_up")
    if target is not None:
        out, hn_next = add_loss(a, w_down, h, target, name=f"ffn{tag}_down")
    else:
        out, hn_next = add_norm(a, w_down, h, next_nw, name=f"ffn{tag}_down")
    return out, hn_next, (hn, ug, uv, a, gc)


def ffn_bwd(h, nw, w_up, cw8, w_down, saved, dh, tag):
    hn, ug, uv, a, gc = saved
    du, d_cw = ffn_down_bwd(dh, w_down, gc, uv, ug, cw8, name=f"ffn{tag}_down_dx")
    d_w_down = mm(a, dh, ta=True, out_dtype=BF16, name=f"ffn{tag}_down_dw")
    dh_new, d_nw = dx_rms_bwd(du, w_up, h, nw, dh, name=f"ffn{tag}_up_dx", b_chip=True, swap_mid=True)
    d_w_up = mm(hn, du, ta=True, out_dtype=BF16, out_chip=True, swap_mid=True, name=f"ffn{tag}_up_dw")
    return dh_new, d_nw, d_w_up, d_cw, d_w_down


def mixer_fwd(h, nw, w_in, ca8, dc8, alog, dtb, dnw, w_out, tie=None, next_nw=None):
    rows = h.shape[0]
    tr = _pick(rows, (384, 128))
    hn = rms_fwd(h, nw, name="mix_norm")
    if callable(w_in):
        hn, w_in = w_in(hn)
    p = mm(hn, w_in, name="mix_in")
    y_a, cv = conv_fwd([(p, 0), (p, 2)], ca8, 3, rows=rows, c=D_CONV, tc=D_CONV, tr=tr, name="conv_a",
                       pre=lambda gi, ah: gi * ah, post=lambda j, y, go: (go * y, y), extras=[(p, 1)],
                       outs=[BF16, F32])
    qkv_n, cq = conv_fwd([(p, 3)], dc8, 4, rows=rows, c=3 * DN_DIM, tc=DN_DIM, tr=tr, name="dn_conv",
                         post=dn_qkv_post, outs=[F32, F32], strip=tr)
    bgcol = bg_fwd(p, alog, dtb)
    if tie is not None:
        bgcol = tie(bgcol)
    bgrow = bgcol[:, :8].reshape(rows // CH, CH, 8).transpose(0, 2, 1)
    o, s_all, ti_all = dn_fwd(qkv_n, bgcol, bgrow)
    y_b = dn_out_fwd(o, p, dnw)
    ymix = jnp.concatenate([y_a, y_b], axis=1)
    w_out = w_out() if callable(w_out) else w_out
    out, hn_next = add_norm(ymix, w_out, h, next_nw, name="mix_out")
    return out, hn_next, (hn, p, cv, qkv_n, cq, bgcol, bgrow, o, s_all, ti_all, ymix, w_in)


def mixer_bwd(h, nw, ca8, dc8, alog, dtb, dnw, w_out, saved, dh):
    hn, p, cv, qkv_n, cq, bgcol, bgrow, o, s_all, ti_all, ymix, w_in = saved
    rows = h.shape[0]
    tr = _pick(rows, (384, 128))
    dymix = mm(dh, w_out, tb=True, name="mix_out_dx")
    d_w_out = mm(ymix, dh, ta=True, out_dtype=BF16, name="mix_out_dw")
    do, dz, d_dnw = dn_out_bwd(o, p, dnw, dymix)
    dq, dk, dv, dbg = dn_bwd(qkv_n, bgcol, bgrow, s_all, ti_all, do)
    dbg_p, d_alog, d_dtb = bg_bwd(p, alog, dtb, dbg)
    dcq = dn_qkv_bwd(cq, dq, dk, dv)
    dqkv, d_dc = conv_bwd([(p, 3)], dc8, 4, dcq, rows=rows, c=3 * DN_DIM, tc=DN_DIM, tr=tr, name="dn_conv_bwd",
                          post=lambda dx: dx, outs=[BF16])
    dgo, dcv = conv_a_pre_bwd(dymix, cv, p)
    dgi, dah, d_ca = conv_bwd([(p, 0), (p, 2)], ca8, 3, dcv, rows=rows, c=D_CONV, tc=D_CONV, tr=tr,
                              name="conv_a_bwd", pre=lambda gi, ah: gi * ah,
                              post=lambda dm, gi, ah: (dm * ah, dm * gi), extras=[(p, 0), (p, 2)], outs=[BF16, BF16])
    dp = jnp.concatenate([dgi, dgo, dah, dqkv, dz, dbg_p], axis=1)
    dh_new, d_nw = dx_rms_bwd(dp, w_in, h, nw, dh, name="mix_in_dx")
    d_w_in = mm(hn, dp, ta=True, out_dtype=BF16, name="mix_in_dw")
    return dh_new, d_nw, d_w_in, d_ca, d_dc, d_alog, d_dtb, d_dnw, d_w_out


def swa_layer_fwd(h, hn, wqkv, qw, kw, sinks, wo, next_nw):
    qkv = mm(hn, wqkv, name="swa_qkv")
    qh, kh, vh = qknorm_fwd(qkv, qw, kw)
    att = swa_fwd(qh, kh, vh, sinks)
    out, hn_next = add_norm(att, wo, h, next_nw, name="swa_out")
    return out, hn_next, (hn, qkv, qh, kh, vh, att)


def swa_layer_bwd(h, nw, wqkv, qw, kw, sinks, wo, saved, dh):
    hn, qkv, qh, kh, vh, att = saved
    datt = mm(dh, wo, tb=True, out_dtype=BF16, name="swa_out_dx")
    d_wo = mm(att, dh, ta=True, out_dtype=BF16, name="swa_out_dw")
    dqh, dkh, dvh, dsk = swa_bwd(qh, kh, vh, sinks, datt)
    dqkv, d_qw, d_kw = qknorm_bwd(qkv, qw, kw, dqh, dkh, dvh)
    dh_new, d_nw = dx_rms_bwd(dqkv, wqkv, h, nw, dh, name="swa_qkv_dx")
    d_wqkv = mm(hn, dqkv, ta=True, out_dtype=BF16, name="swa_qkv_dw")
    d_sinks = jnp.sum(dsk[:, :, 0], axis=0)
    return dh_new, d_nw, d_wqkv, d_qw, d_kw, d_sinks, d_wo


BIG = ("mix_w_in", "mix_w_out", "swa_wq", "swa_wk", "swa_wv", "swa_wo", "ffn_w_up", "ffn_w_down")


def _flat_pad(parts, rows):
    v = jnp.concatenate([t.astype(F32).reshape(-1) for t in parts])
    return jnp.pad(v, (0, rows * 1024 - v.shape[0])).reshape(rows, 1024)


def _split_flat(flat, shapes):
    v = flat.reshape(-1)
    out, o = [], 0
    for s in shapes:
        n = 1
        for d_ in s:
            n *= d_
        out.append(v[o:o + n].reshape(s))
        o += n
    return out


def local_step(x0, target0, meta_full, anw, fnw, w_in, ca8, dc8, alog, dtb, dnw, qw, kw, sinks, fc8, late,
               begin=None, tie=None):
    begin = begin or (lambda tag, names, grads: None)
    h0 = jnp.concatenate([jnp.zeros((PAD, D), F32), meta_full, x0], axis=0)
    h1, hn1, s_mix = mixer_fwd(h0, anw[0], w_in, ca8, dc8, alog, dtb, dnw, lambda: late()[0], tie, fnw[0])
    w_out, wqkv, wo, w_up, w_down = late()
    h2, hn2, s_f0 = ffn_fwd(h1, hn1, w_up[0], fc8[0], w_down[0], 0, anw[1])
    h3, hn3, s_swa = swa_layer_fwd(h2, hn2, wqkv, qw, kw, sinks, wo, fnw[1])
    dh, loss_l, s_f1 = ffn_fwd(h3, hn3, w_up[1], fc8[1], w_down[1], 1, target=target0)
    dh, d_fnw1, d_up1, d_fc1, d_down1 = ffn_bwd(h3, fnw[1], w_up[1], fc8[1], w_down[1], s_f1, dh, 1)
    begin("ffn1", ("up1", "down1"), [d_up1, d_down1.reshape(4, 704, D)])
    dh, d_anw1, d_wqkv, d_qw, d_kw, d_sinks, d_wo = swa_layer_bwd(h2, anw[1], wqkv, qw, kw, sinks, wo, s_swa, dh)
    begin("swa", ("wq", "wk", "wv", "wo"),
          [d_wqkv[:, :D].reshape(4, 256, D), d_wqkv[:, D:D + 256].reshape(4, 256, 256),
           d_wqkv[:, D + 256:].reshape(4, 256, 256), d_wo.reshape(4, 256, D)])
    dh, d_fnw0, d_up0, d_fc0, d_down0 = ffn_bwd(h1, fnw[0], w_up[0], fc8[0], w_down[0], s_f0, dh, 0)
    begin("ffn0", ("up0", "down0"), [d_up0, d_down0.reshape(4, 704, D)])
    dh, d_anw0, d_w_in, d_ca, d_dc, d_alog, d_dtb, d_dnw, d_w_out = mixer_bwd(
        h0, anw[0], ca8, dc8, alog, dtb, dnw, w_out, s_mix, dh)
    begin("mix", ("w_in", "w_out"),
          [d_w_in[:, :IN_DIM].reshape(D, 4, 898).transpose(1, 0, 2), d_w_out.reshape(4, 256, D)])
    return (dh, loss_l, d_anw0, d_anw1, d_fnw0, d_fnw1, d_w_in, d_ca, d_dc, d_alog, d_dtb, d_dnw, d_w_out, d_wqkv,
            d_qw, d_kw, d_sinks, d_wo, d_up0, d_up1, d_fc0, d_fc1, d_down0, d_down1)


def kernel(x, meta_tokens, attn_norm_w, ffn_norm_w, mix_w_in, conv_a_w, dn_conv_w, dn_a_log, dn_dt_bias, dn_norm_w, mix_w_out, swa_wq, swa_wk, swa_wv, swa_q_norm_w, swa_k_norm_w, swa_sinks, swa_wo, ffn_w_up, ffn_conv_w, ffn_w_down, loss_target, m_meta_tokens, m_attn_norm_w, m_ffn_norm_w, m_mix_w_in, m_conv_a_w, m_dn_conv_w, m_dn_a_log, m_dn_dt_bias, m_dn_norm_w, m_mix_w_out, m_swa_wq, m_swa_wk, m_swa_wv, m_swa_q_norm_w, m_swa_k_norm_w, m_swa_sinks, m_swa_wo, m_ffn_w_up, m_ffn_conv_w, m_ffn_w_down, v_meta_tokens, v_attn_norm_w, v_ffn_norm_w, v_mix_w_in, v_conv_a_w, v_dn_conv_w, v_dn_a_log, v_dn_dt_bias, v_dn_norm_w, v_mix_w_out, v_swa_wq, v_swa_wk, v_swa_wv, v_swa_q_norm_w, v_swa_k_norm_w, v_swa_sinks, v_swa_wo, v_ffn_w_up, v_ffn_conv_w, v_ffn_w_down):
    ix, iy, ic = lax.axis_index("x"), lax.axis_index("y"), lax.axis_index("c")
    chip = 2 * ix + iy
    seq = x.shape[1]
    rows = HEAD0 + seq

    small_sharded = (conv_a_w, dn_conv_w, ffn_conv_w, meta_tokens)
    up_b, down_b = ffn_w_up.astype(BF16), ffn_w_down.astype(BF16)
    own = [mix_w_in[0].astype(BF16), mix_w_out[0].astype(BF16), swa_wq[0].astype(BF16), swa_wk[0].astype(BF16),
           swa_wv[0].astype(BF16), swa_wo[0].astype(BF16), up_b[0], up_b[1], down_b[0], down_b[1]]
    fill = lambda gathered, mine: [lax.dynamic_update_slice_in_dim(g, t[None], chip, axis=0)
                                   for g, t in zip(gathered, mine)]
    on_its_way, = gather_weights_beside(own[:1], 9, "gather_w_in")
    _, g_small = gather_weights([], _flat_pad(small_sharded, SW_ROWS))

    rest = {}

    def w_in(hn):
        hn, got, g_out = lax.optimization_barrier((hn, on_its_way, own[1]))
        rest["w_out"] = fill(gather_weights_beside([g_out], 1, "gather_w_out"), [g_out])
        g_in, = fill([got], own[:1])
        return hn, jnp.pad(g_in.transpose(1, 0, 2).reshape(D, IN_DIM), ((0, 0), (0, P_W - IN_DIM)))

    def tie(t):
        t, *mine = lax.optimization_barrier((t, *own[2:]))
        g_q, g_k, g_v, g_o, g_up0, g_up1, g_dn0, g_dn1 = mine
        soon, last = [g_up0, g_dn0, g_q, g_k, g_v, g_o], [g_up1, g_dn1]
        rest["soon"] = fill(gather_weights_beside(soon, 7, "gather_layers_12"), soon)
        rest["last"] = fill(gather_weights_beside(last, 8, "gather_layer_3"), last)
        return t

    def late():
        (g_out,), (g_up0, g_dn0, g_q, g_k, g_v, g_o), (g_up1, g_dn1) = rest["w_out"], rest["soon"], rest["last"]
        wqkv = jnp.concatenate([g_q.reshape(D, D), g_k.reshape(D, 256), g_v.reshape(D, 256)], axis=1)
        return (g_out.reshape(D, D), wqkv, g_o.reshape(D, D), [g_up0, g_up1],
                [g_dn0.reshape(D_FF, D), g_dn1.reshape(D_FF, D)])

    gs = g_small.reshape(4, -1)
    ca_full = gs[:, 0:384].reshape(4, 3, 128).transpose(1, 0, 2).reshape(3, D_CONV)
    dc_full = gs[:, 384:1920].reshape(4, 4, 384).transpose(1, 0, 2).reshape(4, 3 * DN_DIM)
    fc_full = gs[:, 1920:6144].reshape(4, 2, 3, 704).transpose(1, 2, 0, 3).reshape(2, 3, D_FF)
    meta_full = gs[:, 6144:10240].reshape(4, N_META, 256).transpose(1, 0, 2).reshape(N_META, D)
    ca8, dc8 = _rows8(ca_full), _rows8(dc_full)
    fc8 = [_rows8(fc_full[0]), _rows8(fc_full[1])]
    alog, dtb = _lanes(dn_a_log[0], 4), _lanes(dn_dt_bias[0], 4)
    dnw = dn_norm_w.astype(F32)
    qw, kw = swa_q_norm_w.astype(F32), swa_k_norm_w.astype(F32)
    sinks = swa_sinks[0].astype(F32)
    anw = [attn_norm_w[0:1], attn_norm_w[1:2]]
    fnw = [ffn_norm_w[0:1], ffn_norm_w[1:2]]

    c_idx = jnp.reshape(ic, (1,)).astype(jnp.int32)
    chip_idx = jnp.stack([chip, ic]).astype(jnp.int32)
    begun = []

    def begin(tag, names, grads):
        pairs, gots = reduce_begin(grads, names, c_idx, 2 + len(begun), tag)
        begun.append((names, pairs, gots))

    (dh, loss_l, d_anw0, d_anw1, d_fnw0, d_fnw1, d_w_in, d_ca, d_dc, d_alog, d_dtb, d_dnw, d_w_out, d_wqkv, d_qw,
     d_kw, d_sinks, d_wo, d_up0, d_up1, d_fc0, d_fc1, d_down0, d_down1) = local_step(
        x[0], loss_target[0], meta_full, anw, fnw, w_in, ca8, dc8, alog, dtb, dnw, qw, kw, sinks, fc8, late,
        begin, tie)
    grad_x = dh[HEAD0:][None]

    small_parts = [jnp.concatenate([d_anw0, d_anw1], axis=0), jnp.concatenate([d_fnw0, d_fnw1], axis=0),
                   d_alog[0, 4:8], d_dtb[0, 4:8], d_dnw, d_qw, d_kw, d_sinks,
                   d_ca[:3], d_dc[:4], jnp.stack([d_fc0[:3], d_fc1[:3]]), dh[PAD:HEAD0], loss_l[0, 0:1]]
    small_shapes = [(2, D), (2, D), (1, 4), (1, 4), (1, DN_D), (1, SWA_D), (1, SWA_D), (1, SWA_H),
                    (1, 3, D_CONV), (1, 4, 3 * DN_DIM), (2, 3, D_FF), (N_META, D), ()]
    gathered_small = gather_small(_flat_pad(small_parts, SV_ROWS))

    red_big = {}
    for part in (begun[:-1], begun[-1:]):
        part_names = [n for names, _, _ in part for n in names]
        red_big.update(zip(part_names, reduce_end([p for _, ps, _ in part for p in ps],
                                                  [g for _, _, gs_ in part for g in gs_], part_names, chip_idx)))
    g_w_in, g_w_out, g_wq, g_wk, g_wv, g_wo, g_up0, g_up1, g_dn0, g_dn1 = [
        red_big[n] for n in ("w_in", "w_out", "wq", "wk", "wv", "wo", "up0", "up1", "down0", "down1")]

    grads = dict(mix_w_in=g_w_in, mix_w_out=g_w_out, swa_wq=g_wq, swa_wk=g_wk, swa_wv=g_wv, swa_wo=g_wo,
                 ffn_w_up=[g_up0, g_up1], ffn_w_down=[g_dn0, g_dn1])
    weights = dict(meta_tokens=meta_tokens, attn_norm_w=attn_norm_w, ffn_norm_w=ffn_norm_w, mix_w_in=mix_w_in,
                   conv_a_w=conv_a_w, dn_conv_w=dn_conv_w, dn_a_log=dn_a_log, dn_dt_bias=dn_dt_bias,
                   dn_norm_w=dn_norm_w, mix_w_out=mix_w_out, swa_wq=swa_wq, swa_wk=swa_wk, swa_wv=swa_wv,
                   swa_q_norm_w=swa_q_norm_w, swa_k_norm_w=swa_k_norm_w, swa_sinks=swa_sinks, swa_wo=swa_wo,
                   ffn_w_up=ffn_w_up, ffn_conv_w=ffn_conv_w, ffn_w_down=ffn_w_down)
    m_in = dict(meta_tokens=m_meta_tokens, attn_norm_w=m_attn_norm_w, ffn_norm_w=m_ffn_norm_w, mix_w_in=m_mix_w_in,
                conv_a_w=m_conv_a_w, dn_conv_w=m_dn_conv_w, dn_a_log=m_dn_a_log, dn_dt_bias=m_dn_dt_bias,
                dn_norm_w=m_dn_norm_w, mix_w_out=m_mix_w_out, swa_wq=m_swa_wq, swa_wk=m_swa_wk, swa_wv=m_swa_wv,
                swa_q_norm_w=m_swa_q_norm_w, swa_k_norm_w=m_swa_k_norm_w, swa_sinks=m_swa_sinks, swa_wo=m_swa_wo,
                ffn_w_up=m_ffn_w_up, ffn_conv_w=m_ffn_conv_w, ffn_w_down=m_ffn_w_down)
    v_in = dict(meta_tokens=v_meta_tokens, attn_norm_w=v_attn_norm_w, ffn_norm_w=v_ffn_norm_w, mix_w_in=v_mix_w_in,
                conv_a_w=v_conv_a_w, dn_conv_w=v_dn_conv_w, dn_a_log=v_dn_a_log, dn_dt_bias=v_dn_dt_bias,
                dn_norm_w=v_dn_norm_w, mix_w_out=v_mix_w_out, swa_wq=v_swa_wq, swa_wk=v_swa_wk, swa_wv=v_swa_wv,
                swa_q_norm_w=v_swa_q_norm_w, swa_k_norm_w=v_swa_k_norm_w, swa_sinks=v_swa_sinks, swa_wo=v_swa_wo,
                ffn_w_up=v_ffn_w_up, ffn_conv_w=v_ffn_conv_w, ffn_w_down=v_ffn_w_down)
    names = list(weights)
    small = [n for n in names if n not in BIG]
    delta, new_m, new_v = {}, {}, {}
    for n in BIG:
        delta[n], new_m[n], new_v[n], grads[n] = adamw(weights[n], grads[n], m_in[n], v_in[n], name=f"adamw_{n}")
    gathered_small, _ = lax.optimization_barrier((gathered_small, new_v["ffn_w_down"]))
    (g_anw, g_fnw, g_alog, g_dtb, g_dnw, g_qw, g_kw, g_sinks, g_ca_f, g_dc_f, g_fc_f, g_meta_f,
     loss) = _split_flat(sum_slots(gathered_small), small_shapes)
    grads.update(meta_tokens=lax.dynamic_slice_in_dim(g_meta_f, chip * 256, 256, axis=1), attn_norm_w=g_anw,
                 ffn_norm_w=g_fnw, conv_a_w=lax.dynamic_slice_in_dim(g_ca_f, chip * 128, 128, axis=2),
                 dn_conv_w=lax.dynamic_slice_in_dim(g_dc_f, chip * 384, 384, axis=2), dn_a_log=g_alog,
                 dn_dt_bias=g_dtb, dn_norm_w=g_dnw, swa_q_norm_w=g_qw, swa_k_norm_w=g_kw, swa_sinks=g_sinks,
                 ffn_conv_w=lax.dynamic_slice_in_dim(g_fc_f, chip * 704, 704, axis=2))
    grads = {n: grads[n].reshape(weights[n].shape) for n in names}
    shapes = [weights[n].shape for n in small]
    packed = [_flat_pad([t[n] for n in small], SW_ROWS) for t in (weights, grads, m_in, v_in)]
    for store, flat in zip((delta, new_m, new_v), adamw(*packed, name="adamw_small")):
        for n, t in zip(small, _split_flat(flat, shapes)):
            store[n] = t
    return (loss, grad_x, *[grads[n] for n in names], *[delta[n] for n in names],
            *[new_m[n] for n in names], *[new_v[n] for n in names])
```

```python
import functools

import jax
import jax.numpy as jnp
from jax import lax
from jax.experimental import pallas as pl
from jax.experimental.pallas import tpu as pltpu
from jax.experimental.pallas import tpu_sc as plsc

F32 = jnp.float32
BF16 = jnp.bfloat16
HI = lax.Precision.HIGHEST
MESH = pl.DeviceIdType.MESH

D = 1024
N_META = 16
PAD = 112
HEAD0 = PAD + N_META
D_CONV = 512
DN_H = 4
DN_D = 128
DN_DIM = 512
CH = 64
IN_DIM = 3592
P_W = 3840
BG0 = 3584
SWA_H = 16
SWA_KV = 4
SWA_D = 64
BLK = 128
NKEY = N_META + 2 * BLK
D_FF = 2816
EPS = 1e-6
LR, B1, B2, AEPS, WD, STEP = 0.001, 0.9, 0.999, 1e-08, 0.01, 10
VMEM_LIMIT = 48 * 1024 * 1024
MM_VMEM_BUDGET = 34 * 1024 * 1024
R_BIG = 6144
R_HALF = R_BIG // 2
SV_ROWS = 48
SW_ROWS = 16


def _pick(n, cands):
    for c in cands:
        if n % c == 0:
            return c
    return n


def _params(sem=None):
    return pltpu.CompilerParams(dimension_semantics=sem, vmem_limit_bytes=VMEM_LIMIT)


def _dot(a, b, ca=1, cb=0, prec=None):
    return lax.dot_general(a, b, (((ca,), (cb,)), ((), ())), precision=prec,
                           preferred_element_type=F32)


def _sigmoid(x):
    return 1.0 / (1.0 + jnp.exp(-x))


def _silu(x):
    return x * _sigmoid(x)


def _dsilu(x):
    s = _sigmoid(x)
    return s * (1.0 + x * (1.0 - s))


def _softplus(x):
    return jnp.maximum(x, 0.0) + jnp.log(1.0 + jnp.exp(-jnp.abs(x)))


def mm(a, b, *, name, ta=False, tb=False, out_dtype=F32, add=None, tm=None, tn=None, tk=None,
       b_chip=False, out_chip=False, swap_mid=False, epi=None, epi_ins=(), epi_consts=(), epi_outs=(), epi_accs=()):
    if epi is not None:
        return _mm_epi(a, b, name=name, tb=tb, tn=tn, b_chip=b_chip, swap_mid=swap_mid, epi=epi, epi_ins=epi_ins,
                       epi_consts=epi_consts, epi_outs=epi_outs, epi_accs=epi_accs)
    chip_of = _chip_order(swap_mid)
    m, k = (a.shape[1], a.shape[0]) if ta else a.shape
    if b_chip:
        n = b.shape[1] if tb else 4 * b.shape[2]
        if tb:
            tk = b.shape[2]
        else:
            tn = b.shape[2]
    else:
        n = b.shape[0] if tb else b.shape[1]
    if out_chip:
        tn = n // 4
    tn = tn or _pick(n, (1408, 1024, 768, 512, 256, 128))
    tk = tk or (_pick(k, (1408, 704, 384, 128)) if ta else _pick(k, (1024, 1408, 768, 512, 128)))
    nk = k // tk
    if tm is None:
        isz = lambda t: jnp.dtype(t.dtype).itemsize
        osz = jnp.dtype(out_dtype).itemsize
        for tm in ((1408, 1024, 512, 384, 256, 128) if ta else (1408, 704, 512, 384, 256, 128)):
            need = 2 * (tm * tk * isz(a) + tk * tn * isz(b) + tm * tn * osz + (tm * tn * 4 if add is not None else 0))
            need += tm * tn * 4 if nk > 1 else 0
            if m % tm == 0 and need <= MM_VMEM_BUDGET:
                break
        else:
            tm = m
    dims = (((0 if ta else 1,), (1 if tb else 0,)), ((), ()))

    def body(*refs):
        if add is None:
            a_ref, b_ref, o_ref, acc_ref = refs
            add_ref = None
        else:
            a_ref, b_ref, add_ref, o_ref, acc_ref = refs
        def part():
            return lax.dot_general(a_ref[...].astype(BF16), b_ref[...].astype(BF16), dims,
                                   preferred_element_type=F32)

        def finish(total):
            if add_ref is not None:
                total = total + add_ref[...]
            o_ref[...] = total.astype(out_dtype)

        if nk == 1:
            finish(part())
        else:
            kk = pl.program_id(2)

            @pl.when(kk == 0)
            def _():
                acc_ref[...] = part()

            @pl.when(jnp.logical_and(kk > 0, kk < nk - 1))
            def _():
                acc_ref[...] += part()

            @pl.when(kk == nk - 1)
            def _():
                finish(acc_ref[...] + part())

    a_spec = pl.BlockSpec((tk, tm), lambda i, j, kk: (kk, i)) if ta else pl.BlockSpec((tm, tk), lambda i, j, kk: (i, kk))
    if b_chip and tb:
        b_spec = pl.BlockSpec((None, tn, tk), lambda i, j, kk: (chip_of(kk), j, 0))
    elif b_chip:
        b_spec = pl.BlockSpec((None, tk, tn), lambda i, j, kk: (j, kk, 0))
    elif tb:
        b_spec = pl.BlockSpec((tn, tk), lambda i, j, kk: (j, kk))
    else:
        b_spec = pl.BlockSpec((tk, tn), lambda i, j, kk: (kk, j))
    o_spec = pl.BlockSpec((tm, tn), lambda i, j, kk: (i, j))
    in_specs = [a_spec, b_spec] + ([o_spec] if add is not None else [])
    args = [a, b] + ([add] if add is not None else [])
    out_spec = pl.BlockSpec((None, tm, tn), lambda i, j, kk: (chip_of(j), i, 0)) if out_chip else o_spec
    return pl.pallas_call(
        body, name=name, interpret=False,
        out_shape=jax.ShapeDtypeStruct((4, m, tn) if out_chip else (m, n), out_dtype),
        grid=(m // tm, n // tn, nk), in_specs=in_specs, out_specs=out_spec,
        scratch_shapes=[pltpu.VMEM((tm, tn) if nk > 1 else (8, 128), F32)],
        compiler_params=_params(("parallel", "parallel", "arbitrary")),
    )(*args)


def _chip_order(swap_mid):
    return (lambda k: (k % 2) * 2 + k // 2) if swap_mid else (lambda k: k)


def _mm_epi(a, b, *, name, tb, tn, b_chip, epi, epi_ins, epi_consts, epi_outs, epi_accs, swap_mid=False):
    chip_of = _chip_order(swap_mid)
    m, k = a.shape
    if b_chip:
        n = b.shape[1] if tb else 4 * b.shape[2]
        tk = b.shape[2] if tb else None
        tn = tn if tb else b.shape[2]
    else:
        n = b.shape[0] if tb else b.shape[1]
        tk = None
    tn = tn or _pick(n, (1408, 1024, 768, 512, 256, 128))
    tk = tk or _pick(k, (1024, 1408, 1280, 768, 512, 128))
    nk, nj = k // tk, n // tn
    isz = lambda t: jnp.dtype(t.dtype if hasattr(t, "dtype") else t).itemsize
    outs3 = [t if isinstance(t, tuple) else (t, n, lambda j: j) for t in epi_outs]
    side = sum(isz(t) for t, _ in epi_ins) + sum(isz(dt) for dt, _, _ in outs3)
    for tm in (1408, 704, 512, 384, 256, 128):
        need = 2 * (tm * tk * isz(a) + tk * tn * isz(b) + tm * tn * side) + (tm * tn * 4 if nk > 1 else 0)
        if m % tm == 0 and need <= MM_VMEM_BUDGET:
            break
    else:
        tm = m
    dims = (((1,), (1 if tb else 0,)), ((), ()))
    n_in, n_c, n_out, n_acc = len(epi_ins), len(epi_consts), len(epi_outs), len(epi_accs)

    def body(*refs):
        a_ref, b_ref = refs[:2]
        in_refs = refs[2:2 + n_in + n_c]
        out_refs = refs[2 + n_in + n_c:2 + n_in + n_c + n_out]
        acc_out = refs[2 + n_in + n_c + n_out:2 + n_in + n_c + n_out + n_acc]
        acc_ref = refs[-1]
        i, j, kk = pl.program_id(0), pl.program_id(1), pl.program_id(2)
        def part():
            return lax.dot_general(a_ref[...].astype(BF16), b_ref[...].astype(BF16), dims,
                                   preferred_element_type=F32)

        def finish(total):
            res = epi(i * tm, total, *[r[...] for r in in_refs])
            if not isinstance(res, (tuple, list)):
                res = (res,)
            for r, v in zip(out_refs, res[:n_out]):
                r[...] = v.astype(r.dtype)
            if n_acc:
                @pl.when(jnp.logical_and(i == 0, j == 0))
                def _():
                    for r in acc_out:
                        r[...] = jnp.zeros(r.shape, r.dtype)

                for r, v in zip(acc_out, res[n_out:]):
                    r[...] += jnp.broadcast_to(v, r.shape).astype(r.dtype)

        if nk == 1:
            finish(part())
        else:
            @pl.when(kk == 0)
            def _():
                acc_ref[...] = part()

            @pl.when(jnp.logical_and(kk > 0, kk < nk - 1))
            def _():
                acc_ref[...] += part()

            @pl.when(kk == nk - 1)
            def _():
                finish(acc_ref[...] + part())

    a_spec = pl.BlockSpec((tm, tk), lambda i, j, kk: (i, kk))
    if b_chip and tb:
        b_spec = pl.BlockSpec((None, tn, tk), lambda i, j, kk: (chip_of(kk), j, 0))
    elif b_chip:
        b_spec = pl.BlockSpec((None, tk, tn), lambda i, j, kk: (j, kk, 0))
    elif tb:
        b_spec = pl.BlockSpec((tn, tk), lambda i, j, kk: (j, kk))
    else:
        b_spec = pl.BlockSpec((tk, tn), lambda i, j, kk: (kk, j))
    in_specs = [a_spec, b_spec]

    def in_spec(t, col):
        front = m - t.shape[0]
        if not front:
            return pl.BlockSpec((tm, tn), lambda i, j, kk: (i, col(j)))
        return pl.BlockSpec((pl.Element(tm), pl.Element(tn)),
                            lambda i, j, kk: (pl.multiple_of(jnp.maximum(i * tm - front, 0), 8), col(j) * tn))

    in_specs += [in_spec(t, col) for t, col in epi_ins]
    in_specs += [pl.BlockSpec(t.shape, lambda i, j, kk, nd=t.ndim: (0,) * nd) for t in epi_consts]
    out_specs = [pl.BlockSpec((tm, tn), lambda i, j, kk, col=col: (i, col(j))) for _, _, col in outs3]
    out_specs += [pl.BlockSpec(s, lambda i, j, kk, nd=len(s): (0,) * nd) for s, _ in epi_accs]
    out_shape = [jax.ShapeDtypeStruct((m, width), dt) for dt, width, _ in outs3]
    out_shape += [jax.ShapeDtypeStruct(s, dt) for s, dt in epi_accs]
    sem = ("arbitrary", "arbitrary", "arbitrary") if n_acc else ("parallel", "parallel", "arbitrary")
    return pl.pallas_call(
        body, name=name, interpret=False, out_shape=out_shape,
        grid=(m // tm, nj, nk), in_specs=in_specs, out_specs=out_specs,
        scratch_shapes=[pltpu.VMEM((tm, tn) if nk > 1 else (8, 128), F32)],
        compiler_params=_params(sem),
    )(a, b, *[t for t, _ in epi_ins], *epi_consts)


def cols(arr, tr, width=None, cb=0):
    width = width or arr.shape[1]
    return (arr, (tr, width), lambda i: (i, cb), "r2")


def heads(arr, tr):
    return (arr, (arr.shape[0], tr, arr.shape[2]), lambda i: (0, i, 0), "r3")


def whole(arr):
    nd = arr.ndim
    return (arr, arr.shape, lambda i: (0,) * nd, "w")


STRIP = 16


def _rows_of(ref, kind, r0, n):
    if kind == "r2":
        return ref[pl.ds(r0, n), :]
    if kind == "r3":
        return ref[:, pl.ds(r0, n), :]
    return ref[...]


def _set_rows(ref, kind, r0, n, v):
    if kind == "r2":
        ref[pl.ds(r0, n), :] = v.astype(ref.dtype)
    elif kind == "r3":
        ref[:, pl.ds(r0, n), :] = v.astype(ref.dtype)
    else:
        ref[...] = v.astype(ref.dtype)


def rowwise(fn, ins, outs, *, steps, name, accs=(), strip=None):
    n_in, n_out, n_acc = len(ins), len(outs), len(accs)
    kin = [t[3] for t in ins]
    kout = [t[4] for t in outs]
    tr = next((t[1][-2] for t in ins if t[3] != "w"), 0)

    def body(*refs):
        i = pl.program_id(0)
        in_refs, out_refs, acc_refs = refs[:n_in], refs[n_in:n_in + n_out], refs[n_in + n_out:]
        if n_acc:
            @pl.when(i == 0)
            def _():
                for r in acc_refs:
                    r[...] = jnp.zeros(r.shape, r.dtype)

        def run(r0, n):
            res = fn(i * tr + r0, *[_rows_of(r, k, r0, n) for r, k in zip(in_refs, kin)])
            if not isinstance(res, (tuple, list)):
                res = (res,)
            for r, k, v in zip(out_refs, kout, res[:n_out]):
                _set_rows(r, k, r0, n, v)
            for r, v in zip(acc_refs, res[n_out:]):
                r[...] += jnp.broadcast_to(v, r.shape).astype(r.dtype)

        if strip is None or tr <= strip:
            run(0, tr)
        else:
            def step(s, carry):
                run(pl.multiple_of(s * strip, strip), strip)
                return carry
            lax.fori_loop(0, tr // strip, step, 0)

    def zmap(nd):
        return lambda i: (0,) * nd

    in_specs = [pl.BlockSpec(t[1], t[2]) for t in ins]
    out_specs = [pl.BlockSpec(t[2], t[3]) for t in outs]
    out_specs += [pl.BlockSpec(s, zmap(len(s))) for s, _ in accs]
    out_shape = [jax.ShapeDtypeStruct(t[0], t[1]) for t in outs]
    out_shape += [jax.ShapeDtypeStruct(s, d) for s, d in accs]
    res = pl.pallas_call(
        body, name=name, interpret=False, out_shape=out_shape, grid=(steps,),
        in_specs=in_specs, out_specs=out_specs,
        compiler_params=_params(("arbitrary",)),
    )(*[t[0] for t in ins])
    return res


def out2d(rows, width, dtype, tr):
    return ((rows, width), dtype, (tr, width), lambda i: (i, 0), "r2")


def conv_fwd(xs, w8, kw, *, rows, c, tc, tr, name, post, extras=(), outs=(), pre=None, strip=STRIP):
    nx, ne, no = len(xs), len(extras), len(outs)
    nr, nc = rows // tr, c // tc
    r8 = tr // 8
    st = strip

    def body(*refs):
        x_refs = refs[:2 * nx]
        w_ref = refs[2 * nx]
        e_refs = refs[2 * nx + 1:2 * nx + 1 + ne]
        o_refs = refs[2 * nx + 1 + ne:2 * nx + 1 + ne + no]
        scr = refs[-1]
        j, i = pl.program_id(0), pl.program_id(1)
        halo = [x_refs[2 * q + 1][...].astype(F32) for q in range(nx)]
        scr[0:8, :] = jnp.where(i > 0, pre(*halo) if pre else halo[0], 0.0)

        def fill(s, carry):
            r0 = pl.multiple_of(s * st, st)
            cur = [x_refs[2 * q][pl.ds(r0, st), :].astype(F32) for q in range(nx)]
            scr[pl.ds(8 + r0, st), :] = pre(*cur) if pre else cur[0]
            return carry

        def comp(s, carry):
            r0 = pl.multiple_of(s * st, st)
            win = scr[pl.ds(r0, st + 8), :]
            y = jnp.zeros((st, tc), F32)
            for q in range(kw):
                sh = kw - 1 - q
                y = y + w_ref[q:q + 1, :] * win[8 - sh:8 - sh + st]
            res = post(j, y, *[e[pl.ds(r0, st), :] for e in e_refs])
            if not isinstance(res, (tuple, list)):
                res = (res,)
            for r, v in zip(o_refs, res):
                r[pl.ds(r0, st), :] = v.astype(r.dtype)
            return carry

        lax.fori_loop(0, tr // st, fill, 0)
        lax.fori_loop(0, tr // st, comp, 0)

    in_specs, args = [], []
    for arr, cb0 in xs:
        in_specs.append(pl.BlockSpec((tr, tc), lambda j, i, cb0=cb0: (i, cb0 + j)))
        in_specs.append(pl.BlockSpec((8, tc), lambda j, i, cb0=cb0: (jnp.maximum(i * r8 - 1, 0), cb0 + j)))
        args += [arr, arr]
    in_specs.append(pl.BlockSpec((8, tc), lambda j, i: (0, j)))
    args.append(w8)
    for arr, cb0 in extras:
        in_specs.append(pl.BlockSpec((tr, tc), lambda j, i, cb0=cb0: (i, cb0 + j)))
        args.append(arr)
    return pl.pallas_call(
        body, name=name, interpret=False,
        out_shape=[jax.ShapeDtypeStruct((rows, c), dt) for dt in outs],
        grid=(nc, nr), in_specs=in_specs,
        out_specs=[pl.BlockSpec((tr, tc), lambda j, i: (i, j)) for _ in outs],
        scratch_shapes=[pltpu.VMEM((tr + 8, tc), F32)],
        compiler_params=_params(("parallel", "arbitrary")),
    )(*args)


def conv_bwd(xs, w8, kw, dy, *, rows, c, tc, tr, name, post, extras=(), outs=(), pre=None):
    nx, ne, no = len(xs), len(extras), len(outs)
    nr, nc = rows // tr, c // tc
    r8 = tr // 8

    def body(*refs):
        x_refs = refs[:nx]
        w_ref, dy_ref, dyn_ref = refs[nx:nx + 3]
        e_refs = refs[nx + 3:nx + 3 + ne]
        first_out = nx + 3 + ne
        o_refs = refs[first_out:first_out + no]
        dw_ref = refs[first_out + no]
        gscr = refs[-1]
        i = pl.program_id(1)
        gscr[tr:tr + 8, :] = jnp.where(i < nr - 1, dyn_ref[...].astype(F32), 0.0)

        def fill(s, carry):
            r0 = pl.multiple_of(s * STRIP, STRIP)
            gscr[pl.ds(r0, STRIP), :] = dy_ref[pl.ds(r0, STRIP), :].astype(F32)
            return carry

        def comp(s, dws):
            r0 = pl.multiple_of(s * STRIP, STRIP)
            gwin = gscr[pl.ds(r0, STRIP + 8), :]
            cur = [x_refs[q][pl.ds(r0, STRIP), :].astype(F32) for q in range(nx)]
            x = pre(*cur) if pre else cur[0]
            dx = jnp.zeros((STRIP, tc), F32)
            new = []
            for q in range(kw):
                sh = kw - 1 - q
                ahead = gwin[sh:sh + STRIP]
                dx = dx + w_ref[q:q + 1, :] * ahead
                part = ahead * x
                new.append(dws[q] + part[0:8] + part[8:16])
            res = post(dx, *[e[pl.ds(r0, STRIP), :] for e in e_refs])
            if not isinstance(res, (tuple, list)):
                res = (res,)
            for r, v in zip(o_refs, res):
                r[pl.ds(r0, STRIP), :] = v.astype(r.dtype)
            return tuple(new)

        lax.fori_loop(0, tr // STRIP, fill, 0)
        dws = lax.fori_loop(0, tr // STRIP, comp, tuple(jnp.zeros((8, tc), F32) for _ in range(kw)))

        @pl.when(i == 0)
        def _():
            dw_ref[...] = jnp.zeros((8, tc), F32)

        dw_ref[...] += jnp.concatenate([jnp.sum(t, axis=0, keepdims=True) for t in dws]
                                       + [jnp.zeros((8 - kw, tc), F32)], axis=0)

    in_specs, args = [], []
    for arr, cb0 in xs:
        in_specs.append(pl.BlockSpec((tr, tc), lambda j, i, cb0=cb0: (i, cb0 + j)))
        args.append(arr)
    in_specs.append(pl.BlockSpec((8, tc), lambda j, i: (0, j)))
    in_specs.append(pl.BlockSpec((tr, tc), lambda j, i: (i, j)))
    in_specs.append(pl.BlockSpec((8, tc), lambda j, i: (jnp.minimum((i + 1) * r8, nr * r8 - 1), j)))
    args += [w8, dy, dy]
    for arr, cb0 in extras:
        in_specs.append(pl.BlockSpec((tr, tc), lambda j, i, cb0=cb0: (i, cb0 + j)))
        args.append(arr)
    return pl.pallas_call(
        body, name=name, interpret=False,
        out_shape=[jax.ShapeDtypeStruct((rows, c), dt) for dt in outs] + [jax.ShapeDtypeStruct((8, c), F32)],
        grid=(nc, nr), in_specs=in_specs,
        out_specs=[pl.BlockSpec((tr, tc), lambda j, i: (i, j)) for _ in outs] + [pl.BlockSpec((8, tc), lambda j, i: (0, j))],
        scratch_shapes=[pltpu.VMEM((tr + 8, tc), F32)],
        compiler_params=_params(("parallel", "arbitrary")),
    )(*args)


def rms_fwd(h, w, *, name):
    rows = h.shape[0]
    tr = _pick(rows, (384, 128))

    def fn(i, x, wv):
        r = lax.rsqrt(jnp.mean(x * x, axis=1, keepdims=True) + EPS)
        return x * r * wv

    return rowwise(fn, [cols(h, tr), whole(w)], [out2d(rows, D, BF16, tr)], steps=rows // tr, name=name)[0]


def _rms_bwd_epi(row0, g, x, dr, wv):
    r = lax.rsqrt(jnp.mean(x * x, axis=1, keepdims=True) + EPS)
    xh = x * r
    gw = g * wv
    dx = r * (gw - xh * jnp.mean(gw * xh, axis=1, keepdims=True))
    row = row0 + lax.broadcasted_iota(jnp.int32, (x.shape[0], 1), 0)
    return jnp.where(row >= PAD, dr + dx, 0.0), jnp.sum(g * xh, axis=0, keepdims=True)


def dx_rms_bwd(dy, w, h, nw, dres, *, name, b_chip=False, swap_mid=False):
    return mm(dy, w, tb=True, b_chip=b_chip, swap_mid=swap_mid, tn=D, name=name, epi=_rms_bwd_epi,
              epi_ins=[(h, lambda j: 0), (dres, lambda j: 0)], epi_consts=[nw], epi_outs=[F32],
              epi_accs=[((1, D), F32)])


def _add_loss_epi(row0, t, h, tgt):
    row = row0 + lax.broadcasted_iota(jnp.int32, (t.shape[0], 1), 0)
    tgt = jnp.where(row0 == 0, jnp.concatenate([tgt[-HEAD0:], tgt[:-HEAD0]], axis=0), tgt)
    diff = jnp.where(row >= HEAD0, t + h - tgt, 0.0)
    part = jnp.sum(jnp.sum(diff * diff, axis=1, keepdims=True), axis=0, keepdims=True)
    return diff * (1.0 / D), part * (0.5 / D)


def add_loss(a, w, h, target, *, name):
    return mm(a, w, name=name, epi=_add_loss_epi, epi_ins=[(h, lambda j: 0), (target, lambda j: 0)],
              epi_outs=[F32], epi_accs=[((1, 128), F32)])


def adamw(w, g, m, v, *, name):
    shape = w.shape
    gs = list(g) if isinstance(g, (list, tuple)) else [g]
    nl = len(gs)
    width = shape[-1]
    rows = w.size // width
    rl = rows // nl
    tr = _pick(rl, (256, 176, 128, 64, 16, 8))
    nr = rl // tr
    if w.ndim == 3 and shape[1] % tr == 0:
        per = shape[1] // tr
        view = lambda t: (t, (None, tr, width), lambda i: (i // per, i % per, 0), "r2")
        out = (shape, F32, (None, tr, width), lambda i: (i // per, i % per, 0), "r2")
    else:
        view = lambda t: cols(t.reshape(rows, width), tr)
        out = out2d(rows, width, F32, tr)

    def fn(i, wv, mv, vv, *gvs):
        gv = gvs[0]
        for layer in range(1, nl):
            gv = jnp.where(i >= layer * rl, gvs[layer], gv)
        mn = B1 * mv + (1.0 - B1) * gv
        vn = B2 * vv + (1.0 - B2) * gv * gv
        mh = mn / (1.0 - B1 ** STEP)
        vh = vn / (1.0 - B2 ** STEP)
        return -LR * (mh / (jnp.sqrt(vh) + AEPS) + WD * wv), mn, vn, gv

    g_ins = [(t.reshape(rl, width), (tr, width), lambda i, layer=layer: (jnp.clip(i - layer * nr, 0, nr - 1), 0), "r2")
             for layer, t in enumerate(gs)]
    res = rowwise(fn, [view(t) for t in (w, m, v)] + g_ins, [out] * 4, steps=rows // tr, name=name)
    return [r.reshape(shape) for r in res]


HB = DN_H * CH
PAIR = 3
PAIR_BWD = 6


def _split(a):
    hi = a.astype(BF16)
    return hi, (a - hi.astype(F32)).astype(BF16)


def _dot1(a, b, ca=1, cb=0):
    return _dot(a.astype(BF16), b.astype(BF16), ca, cb)


def _dot3(a, b, ca=1, cb=0):
    ah, al = _split(a)
    bh, bl = _split(b)
    return _dot(ah, bh, ca, cb) + (_dot(ah, bl, ca, cb) + _dot(al, bh, ca, cb))


def _dot01(m01, b, ca=1, cb=0):
    bh, bl = _split(b)
    m = m01.astype(BF16)
    return _dot(m, bh, ca, cb) + _dot(m, bl, ca, cb)


def _stack(x):
    return jnp.concatenate([x[:, h * DN_D:(h + 1) * DN_D] for h in range(DN_H)], axis=0)


def _unstack(x):
    return jnp.concatenate([x[h * CH:(h + 1) * CH] for h in range(DN_H)], axis=1)


def _tri_inv(mats, blk, eye):
    each = lambda f, *lists: [f(*t) for t in zip(*lists)]
    ad = [jnp.where(blk, a, 0.0) for a in mats]
    lo = each(lambda a, d: a - d, mats, ad)
    a2 = each(_dot3, ad, ad)
    a4 = each(_dot3, a2, a2)
    a8 = each(_dot3, a4, a4)
    dgi = each(lambda d, s: _dot3(eye - d, eye + s), ad, a2)
    dgi = each(lambda p, s: _dot3(p, eye + s), dgi, a4)
    dgi = each(lambda p, s: _dot3(p, eye + s), dgi, a8)
    n = each(_dot3, dgi, lo)
    n2 = each(_dot3, n, n)
    return each(_dot3, each(lambda u, v: _dot3(eye - u, eye + v), n, n2), dgi)


def _dn_masks():
    row = lax.broadcasted_iota(jnp.int32, (HB, HB), 0)
    col = lax.broadcasted_iota(jnp.int32, (HB, HB), 1)
    same = (row // CH) == (col // CH)
    incl = jnp.logical_and(same, row >= col)
    strict = jnp.logical_and(same, row > col)
    upper = jnp.logical_and(same, row <= col)
    blk = (row // 16) == (col // 16)
    eye = (row == col).astype(F32)
    return incl, strict, upper, blk, eye


def _dn_chunk(qv, kv, vv, bc, br, incl, strict):
    r64 = lax.broadcasted_iota(jnp.int32, (CH, CH), 0)
    c64 = lax.broadcasted_iota(jnp.int32, (CH, CH), 1)
    dcol = _dot01((r64 >= c64).astype(F32), bc)
    drow = _dot3(br, (r64 <= c64).astype(F32))
    col = lambda m, l0: jnp.concatenate([m[:, l0 + h:l0 + h + 1] for h in range(DN_H)], axis=0)
    b_c = col(bc, 0)
    d_c = col(dcol, 4)
    d_r = jnp.concatenate([drow[4 + h:5 + h, :] for h in range(DN_H)], axis=1)
    d_last_h = [dcol[CH - 1:CH, 4 + h:5 + h] for h in range(DN_H)]
    d_last = jnp.concatenate([jnp.broadcast_to(t, (CH, 1)) for t in d_last_h], axis=0)
    q, k, v = _stack(qv), _stack(kv), _stack(vv)
    dm = jnp.where(incl, jnp.exp(jnp.where(incl, d_c - d_r, 0.0)), 0.0)
    kk = _dot1(k, k, 1, 1)
    a = jnp.where(strict, b_c * kk * dm, 0.0)
    ed = jnp.exp(d_c)
    rhs = jnp.concatenate([v * b_c, k * (b_c * ed)], axis=1)
    qk = _dot1(q, k, 1, 1) * dm
    ekd = jnp.exp(d_last - d_c)
    gl = [jnp.exp(t) for t in d_last_h]
    return q, k, v, b_c, dm, kk, a, ed, rhs, qk, ekd, gl


def dn_fwd(qkv_n, bgcol, bgrow):
    rows = qkv_n.shape[0]
    nch = rows // CH

    def body(q_ref, k_ref, v_ref, bc_ref, br_ref, o_ref, s_out, ti_out, s_scr, prep, prep_qk, prep_gl):
        n = pl.program_id(0)

        @pl.when(n == 0)
        def _():
            s_scr[...] = jnp.zeros(s_scr.shape, F32)
            prep[...] = jnp.zeros(prep.shape, F32)
            prep_qk[...] = jnp.zeros(prep_qk.shape, F32)
            prep_gl[...] = jnp.zeros(prep_gl.shape, F32)

        live = n > 0
        rows_of = [slice(h * CH, (h + 1) * CH) for h in range(DN_H)]
        s = [s_scr[h] for h in range(DN_H)]
        for c in range(PAIR):
            u, w, qd, kd = prep[c, 0], prep[c, 1], prep[c, 2], prep[c, 3]
            for h in range(DN_H):
                s_out[c, h] = s[h]
            v_new = [u[rs] - _dot1(w[rs], s[h]) for h, rs in enumerate(rows_of)]
            o_state = [_dot1(qd[rs], s[h]) for h, rs in enumerate(rows_of)]
            s = [jnp.where(live, prep_gl[c, h:h + 1, 0:1] * s[h] + _dot1(kd[rs], v_new[h], 0, 0), s[h])
                 for h, rs in enumerate(rows_of)]
            o = jnp.concatenate(o_state, axis=0) + _dot1(prep_qk[c], jnp.concatenate(v_new, axis=0))
            o_ref[c * CH:(c + 1) * CH, :] = _unstack(o)
        for h in range(DN_H):
            s_scr[h] = s[h]

        incl, strict, _, blk, eye = _dn_masks()
        parts = []
        for c in range(PAIR):
            rows_c = slice(c * CH, (c + 1) * CH)
            parts.append(_dn_chunk(q_ref[rows_c, :], k_ref[rows_c, :], v_ref[rows_c, :], bc_ref[rows_c, :],
                                   br_ref[c], incl, strict))
        tinvs = _tri_inv([p[6] for p in parts], blk, eye)
        for c, (q, k, v, b_c, dm, kk, a, ed, rhs, qk_n, ekd, gl) in enumerate(parts):
            tinv = tinvs[c]
            ti_out[c] = tinv
            sol = _dot3(tinv, rhs)
            prep[c, 0] = sol[:, :DN_D]
            prep[c, 1] = sol[:, DN_D:]
            prep[c, 2] = q * ed
            prep[c, 3] = k * ekd
            prep_qk[c] = qk_n
            prep_gl[c] = jnp.concatenate([jnp.broadcast_to(t, (1, 128)) for t in gl]
                                         + [jnp.zeros((8 - DN_H, 128), F32)], axis=0)

    assert nch % PAIR == 0
    npair = nch // PAIR
    last = npair - 1
    return pl.pallas_call(
        body, name="dn_fwd", interpret=False,
        out_shape=[jax.ShapeDtypeStruct((rows, DN_DIM), F32),
                   jax.ShapeDtypeStruct((nch, DN_H, DN_D, DN_D), F32),
                   jax.ShapeDtypeStruct((nch, HB, HB), F32)],
        grid=(npair + 1,),
        in_specs=[pl.BlockSpec((PAIR * CH, DN_DIM), lambda n: (jnp.minimum(n, last), 0)),
                  pl.BlockSpec((PAIR * CH, DN_DIM), lambda n: (jnp.minimum(n, last), 1)),
                  pl.BlockSpec((PAIR * CH, DN_DIM), lambda n: (jnp.minimum(n, last), 2)),
                  pl.BlockSpec((PAIR * CH, 128), lambda n: (jnp.minimum(n, last), 0)),
                  pl.BlockSpec((PAIR, 8, CH), lambda n: (jnp.minimum(n, last), 0, 0))],
        out_specs=[pl.BlockSpec((PAIR * CH, DN_DIM), lambda n: (jnp.maximum(n - 1, 0), 0)),
                   pl.BlockSpec((PAIR, DN_H, DN_D, DN_D), lambda n: (jnp.maximum(n - 1, 0), 0, 0, 0)),
                   pl.BlockSpec((PAIR, HB, HB), lambda n: (jnp.minimum(n, last), 0, 0))],
        scratch_shapes=[pltpu.VMEM((DN_H, DN_D, DN_D), F32), pltpu.VMEM((PAIR, 4, HB, DN_D), F32),
                        pltpu.VMEM((PAIR, HB, HB), F32), pltpu.VMEM((PAIR, 8, 128), F32)],
        compiler_params=_params(("arbitrary",)),
    )(qkv_n, qkv_n, qkv_n, bgcol, bgrow)


def dn_bwd(qkv_n, bgcol, bgrow, s_all, ti_all, do):
    rows = qkv_n.shape[0]
    nch = rows // CH

    def body(q_ref, k_ref, v_ref, bc_ref, br_ref, s_ref, ti_ref, do_ref, dq_ref, dk_ref, dv_ref, dbg_ref, ds_scr):
        n = pl.program_id(0)

        @pl.when(n == 0)
        def _():
            ds_scr[...] = jnp.zeros(ds_scr.shape, F32)

        incl, strict, upper, _, _ = _dn_masks()
        rsum = lambda t: jnp.sum(t, axis=1, keepdims=True)
        rows_of = [slice(h * CH, (h + 1) * CH) for h in range(DN_H)]
        heads_of = lambda f: jnp.concatenate([f(h, rs) for h, rs in enumerate(rows_of)], axis=0)
        cs = []
        for c in reversed(range(PAIR_BWD)):
            rc = slice(c * CH, (c + 1) * CH)
            q, k, v, b_c, dm, kk, a, ed, rhs, qk, ekd, gl = _dn_chunk(
                q_ref[rc, :], k_ref[rc, :], v_ref[rc, :], bc_ref[rc, :], br_ref[c], incl, strict)
            cs.append(dict(rc=rc, q=q, k=k, v=v, b_c=b_c, dm=dm, kk=kk, a=a, ed=ed, rhs=rhs, qk=qk, ekd=ekd, gl=gl,
                           tinv=ti_ref[c], g=_stack(do_ref[rc, :]), s=[s_ref[c, h] for h in range(DN_H)]))
        for t in cs:
            t["sol"] = _dot3(t["tinv"], t["rhs"])
        for t in cs:
            t["u"], t["w"] = t["sol"][:, :DN_D], t["sol"][:, DN_D:]
            t["qd"], t["kd"] = t["q"] * t["ed"], t["k"] * t["ekd"]
            t["v_new"] = heads_of(lambda h, rs: t["u"][rs] - _dot1(t["w"][rs], t["s"][h]))
            t["dv0"] = _dot1(t["qk"], t["g"], 0, 0)
            t["ds0"] = [_dot1(t["qd"][rs], t["g"][rs], 0, 0) for rs in rows_of]
            t["dqd"] = heads_of(lambda h, rs: _dot1(t["g"][rs], t["s"][h], 1, 1))
        for t in cs:
            t["dqk"] = _dot1(t["g"], t["v_new"], 1, 1)
        ds = [ds_scr[h] for h in range(DN_H)]
        for t in cs:
            t["ds"] = ds
            t["dv_new"] = t["dv0"] + heads_of(lambda h, rs: _dot1(t["kd"][rs], ds[h]))
            ds = [t["ds0"][h] + t["gl"][h] * ds[h] - _dot1(t["w"][rs], t["dv_new"][rs], 0, 0)
                  for h, rs in enumerate(rows_of)]
        for h in range(DN_H):
            ds_scr[h] = ds[h]
        for t in cs:
            t["dkd"] = heads_of(lambda h, rs: _dot1(t["v_new"][rs], t["ds"][h], 1, 1))
            dw = heads_of(lambda h, rs: -_dot1(t["dv_new"][rs], t["s"][h], 1, 1))
            t["dsol"] = jnp.concatenate([t["dv_new"], dw], axis=1)
        for t in cs:
            t["drhs"] = _dot3(t["tinv"], t["dsol"], 0, 0)
        for t in cs:
            t["da"] = jnp.where(strict, -_dot1(t["drhs"], t["sol"], 1, 1), 0.0)
        rowi = lax.broadcasted_iota(jnp.int32, (CH, 1), 0)
        lane = lax.broadcasted_iota(jnp.int32, (CH, 128), 1)
        for t in cs:
            q, k, v, b_c, dm, ed, da, dqk = t["q"], t["k"], t["v"], t["b_c"], t["dm"], t["ed"], t["da"], t["dqk"]
            drhs_u, drhs_w = t["drhs"][:, :DN_D], t["drhs"][:, DN_D:]
            s2 = rsum(drhs_w * k)
            dbeta = rsum(drhs_u * v) + s2 * ed + rsum(da * t["kk"] * dm)
            dkk = da * b_c * dm
            dqkr = dqk * dm
            mmat = da * t["a"] + dqk * t["qk"]
            tmp = rsum(t["dkd"] * t["kd"])
            dd = (s2 * b_c * ed + rsum(mmat) - _dot3(mmat, jnp.ones((HB, 128), F32), 0, 0)[:, :1]
                  + rsum(t["dqd"] * t["qd"]) - tmp)
            last = []
            for h, rs in enumerate(rows_of):
                dgl = jnp.sum(rsum(t["s"][h] * t["ds"][h]), axis=0, keepdims=True)
                dd_last = jnp.sum(tmp[rs], axis=0, keepdims=True) + dgl * t["gl"][h]
                last.append(jnp.where(rowi == CH - 1, dd_last, 0.0))
            dd = dd + jnp.concatenate(last, axis=0)
            rc = t["rc"]
            dq_ref[rc, :] = _unstack(_dot1(dqkr, k) + t["dqd"] * ed)
            dk_ref[rc, :] = _unstack(drhs_w * (b_c * ed) + _dot1(dkk, k) + _dot1(dkk, k, 0, 0) + _dot1(dqkr, q, 0, 0)
                                     + t["dkd"] * t["ekd"])
            dv_ref[rc, :] = _unstack(drhs_u * b_c)
            dg = _dot01(upper.astype(F32), jnp.broadcast_to(dd, (HB, 128)))[:, :1]
            out = jnp.zeros((CH, 128), F32)
            for h, rs in enumerate(rows_of):
                out = out + jnp.where(lane == h, dbeta[rs], 0.0) + jnp.where(lane == 4 + h, dg[rs], 0.0)
            dbg_ref[rc, :] = out

    assert nch % PAIR_BWD == 0
    npair = nch // PAIR_BWD
    rev = lambda n: npair - 1 - n
    blk = PAIR_BWD * CH
    return pl.pallas_call(
        body, name="dn_bwd", interpret=False,
        out_shape=[jax.ShapeDtypeStruct((rows, DN_DIM), F32)] * 3 + [jax.ShapeDtypeStruct((rows, 128), F32)],
        grid=(npair,),
        in_specs=[pl.BlockSpec((blk, DN_DIM), lambda n: (rev(n), 0)),
                  pl.BlockSpec((blk, DN_DIM), lambda n: (rev(n), 1)),
                  pl.BlockSpec((blk, DN_DIM), lambda n: (rev(n), 2)),
                  pl.BlockSpec((blk, 128), lambda n: (rev(n), 0)),
                  pl.BlockSpec((PAIR_BWD, 8, CH), lambda n: (rev(n), 0, 0)),
                  pl.BlockSpec((PAIR_BWD, DN_H, DN_D, DN_D), lambda n: (rev(n), 0, 0, 0)),
                  pl.BlockSpec((PAIR_BWD, HB, HB), lambda n: (rev(n), 0, 0)),
                  pl.BlockSpec((blk, DN_DIM), lambda n: (rev(n), 0))],
        out_specs=[pl.BlockSpec((blk, DN_DIM), lambda n: (rev(n), 0))] * 3 + [pl.BlockSpec((blk, 128), lambda n: (rev(n), 0))],
        scratch_shapes=[pltpu.VMEM((DN_H, DN_D, DN_D), F32)],
        compiler_params=_params(("arbitrary",)),
    )(qkv_n, qkv_n, qkv_n, bgcol, bgrow, s_all, ti_all, do)


def _swa_valid(n):
    c3 = lax.broadcasted_iota(jnp.int32, (NKEY, 4 * BLK), 0)
    r = lax.broadcasted_iota(jnp.int32, (NKEY, 4 * BLK), 1) % BLK
    prev0 = N_META + BLK
    c = jnp.where(c3 < N_META, PAD + c3, jnp.where(c3 < prev0, c3 - N_META, c3 - prev0))
    lo = jnp.where(c3 < N_META, 0, jnp.where(c3 < prev0, r + 1 + jnp.where(n >= 2, 0, BLK), 0))
    hi = jnp.where(c3 < N_META, r + jnp.where(n >= 1, BLK, 0),
                   jnp.where(c3 < prev0, BLK, r - jnp.where(n >= 1, 0, BLK)))
    return jnp.logical_and(c >= lo, c <= hi)


def _swa_probs(qs, kcats, valid, sinks):
    s = [jnp.where(valid, _dot(kc, q, 1, 1), -1e30) for q, kc in zip(qs, kcats)]
    m = [jnp.maximum(jnp.max(t, axis=0, keepdims=True), sk) for t, sk in zip(s, sinks)]
    e = [jnp.where(valid, jnp.exp(t - mx), 0.0) for t, mx in zip(s, m)]
    es = [jnp.exp(sk - mx) for sk, mx in zip(sinks, m)]
    inv = [1.0 / (jnp.sum(t, axis=0, keepdims=True) + u) for t, u in zip(e, es)]
    return [t * i for t, i in zip(e, inv)], [u * i for u, i in zip(es, inv)]


def _swa_group(q_ref, sk_ref, h):
    q4 = jnp.concatenate([q_ref[4 * h + g] for g in range(4)], axis=0)
    sink4 = jnp.concatenate([jnp.full((1, BLK), sk_ref[4 * h + g], F32) for g in range(4)], axis=1)
    return q4, sink4


def _swa_specs():
    q = pl.BlockSpec((SWA_H, BLK, SWA_D), lambda n: (0, n, 0))
    km = pl.BlockSpec((SWA_KV, N_META, SWA_D), lambda n: (0, PAD // N_META, 0))
    kp = pl.BlockSpec((SWA_KV, BLK, SWA_D), lambda n: (0, jnp.maximum(n - 1, 0), 0))
    kc = pl.BlockSpec((SWA_KV, BLK, SWA_D), lambda n: (0, n, 0))
    return [q, km, kp, kc, km, kp, kc]


def swa_fwd(qh, kh, vh, sinks):
    rows = qh.shape[1]
    nb = rows // BLK

    def body(q_ref, km, kp, kc, vm, vp, vc, sk_ref, o_ref):
        n = pl.program_id(0)
        valid = _swa_valid(n)
        kcats = [jnp.concatenate([km[h], kp[h], kc[h]], axis=0) for h in range(SWA_KV)]
        vcats = [jnp.concatenate([vm[h], vp[h], vc[h]], axis=0) for h in range(SWA_KV)]
        qs, sinks4 = zip(*[_swa_group(q_ref, sk_ref, h) for h in range(SWA_KV)])
        ps, _ = _swa_probs(qs, kcats, valid, sinks4)
        o4s = [_dot(p.astype(BF16), vc_, 0, 0) for p, vc_ in zip(ps, vcats)]
        o_ref[...] = jnp.concatenate([o4[g * BLK:(g + 1) * BLK] for o4 in o4s for g in range(4)],
                                     axis=1).astype(BF16)

    return pl.pallas_call(
        body, name="swa_fwd", interpret=False,
        out_shape=jax.ShapeDtypeStruct((rows, SWA_H * SWA_D), BF16),
        grid=(nb,),
        in_specs=_swa_specs() + [pl.BlockSpec(memory_space=pltpu.SMEM)],
        out_specs=pl.BlockSpec((BLK, SWA_H * SWA_D), lambda n: (n, 0)),
        compiler_params=_params(("parallel",)),
    )(qh, kh, kh, kh, vh, vh, vh, sinks)


def swa_bwd(qh, kh, vh, sinks, do):
    rows = qh.shape[1]
    nb = rows // BLK

    def body(q_ref, km, kp, kc, vm, vp, vc, do_ref, sk_ref, dq_ref, dk_ref, dv_ref, dsk_ref):
        n = pl.program_id(0)

        @pl.when(n == 0)
        def _():
            dk_ref[...] = jnp.zeros(dk_ref.shape, F32)
            dv_ref[...] = jnp.zeros(dv_ref.shape, F32)

        valid = _swa_valid(n)
        g_all = do_ref[...]
        rowi = lax.broadcasted_iota(jnp.int32, (SWA_H, 128), 0)
        dsk = jnp.zeros((SWA_H, 128), F32)
        pm = pl.multiple_of(jnp.maximum(n - 1, 0) * BLK, BLK)
        pc = pl.multiple_of(n * BLK, BLK)
        hs = range(SWA_KV)
        kcats = [jnp.concatenate([km[h], kp[h], kc[h]], axis=0) for h in hs]
        vcats = [jnp.concatenate([vm[h], vp[h], vc[h]], axis=0) for h in hs]
        qs, sinks4 = zip(*[_swa_group(q_ref, sk_ref, h) for h in hs])
        g4s = [jnp.concatenate([g_all[:, (4 * h + g) * SWA_D:(4 * h + g + 1) * SWA_D] for g in range(4)], axis=0)
               for h in hs]
        ps, pss = _swa_probs(qs, kcats, valid, sinks4)
        dps = [_dot(vc_, g4, 1, 1) for vc_, g4 in zip(vcats, g4s)]
        deltas = [jnp.sum(p * dp, axis=0, keepdims=True) for p, dp in zip(ps, dps)]
        dss = [(p * (dp - dl)).astype(BF16) for p, dp, dl in zip(ps, dps, deltas)]
        dq4s = [_dot(ds, kc_, 0, 0) for ds, kc_ in zip(dss, kcats)]
        dkcs = [_dot(ds, q4) for ds, q4 in zip(dss, qs)]
        dvcs = [_dot(p.astype(BF16), g4) for p, g4 in zip(ps, g4s)]
        for h in hs:
            t = pss[h] * deltas[h]
            for g in range(4):
                dq_ref[4 * h + g] = dq4s[h][g * BLK:(g + 1) * BLK]
                part = -jnp.sum(t[:, g * BLK:(g + 1) * BLK], axis=1, keepdims=True)
                dsk = dsk + jnp.where(rowi == 4 * h + g, part, 0.0)
            lanes = slice(h * SWA_D, (h + 1) * SWA_D)
            for ref, val in ((dk_ref, dkcs[h]), (dv_ref, dvcs[h])):
                ref[PAD:BLK, lanes] += val[0:N_META]
                ref[pl.ds(pm, BLK), lanes] += val[N_META:N_META + BLK]
                ref[pl.ds(pc, BLK), lanes] += val[N_META + BLK:]
        dsk_ref[0] = dsk

    return pl.pallas_call(
        body, name="swa_bwd", interpret=False,
        out_shape=[jax.ShapeDtypeStruct((SWA_H, rows, SWA_D), F32),
                   jax.ShapeDtypeStruct((rows, SWA_KV * SWA_D), F32),
                   jax.ShapeDtypeStruct((rows, SWA_KV * SWA_D), F32),
                   jax.ShapeDtypeStruct((nb, SWA_H, 128), F32)],
        grid=(nb,),
        in_specs=_swa_specs() + [pl.BlockSpec((BLK, SWA_H * SWA_D), lambda n: (n, 0)),
                                 pl.BlockSpec(memory_space=pltpu.SMEM)],
        out_specs=[pl.BlockSpec((SWA_H, BLK, SWA_D), lambda n: (0, n, 0)),
                   pl.BlockSpec((rows, SWA_KV * SWA_D), lambda n: (0, 0)),
                   pl.BlockSpec((rows, SWA_KV * SWA_D), lambda n: (0, 0)),
                   pl.BlockSpec((1, SWA_H, 128), lambda n: (n, 0, 0))],
        compiler_params=_params(("arbitrary",)),
    )(qh, kh, kh, kh, vh, vh, vh, do, sinks)


QK_W = (SWA_H + SWA_KV) * SWA_D


def _head_mean(t):
    r = lax.broadcasted_iota(jnp.int32, (128, 128), 0) // SWA_D
    c = lax.broadcasted_iota(jnp.int32, (128, 128), 1) // SWA_D
    blk = jnp.where(r == c, 1.0 / SWA_D, 0.0).astype(BF16)
    out = []
    for i in range(t.shape[1] // 128):
        hi, lo = _split(t[:, 128 * i:128 * (i + 1)])
        out.append(_dot(hi, blk) + _dot(lo, blk))
    return jnp.concatenate(out, axis=1)


def _qk_scales(qw, kw):
    scale = SWA_D ** -0.5
    wt = jnp.concatenate([jnp.tile(qw.astype(F32) * scale, (1, SWA_H)), jnp.tile(kw.astype(F32), (1, SWA_KV))], axis=1)
    st = jnp.concatenate([jnp.full((1, SWA_H * SWA_D), scale, F32), jnp.ones((1, SWA_KV * SWA_D), F32)], axis=1)
    return wt, st


def qknorm_fwd(qkv, qw, kw):
    rows = qkv.shape[0]
    tr = _pick(rows, (384, 128))
    wt, _ = _qk_scales(qw, kw)

    def fn(i, x, w):
        xq = x[:, :QK_W]
        y = xq * lax.rsqrt(_head_mean(xq * xq) + EPS) * w
        head = lambda t, j: t[:, j * SWA_D:(j + 1) * SWA_D][None]
        qo = jnp.concatenate([head(y, j) for j in range(SWA_H)], axis=0)
        ko = jnp.concatenate([head(y, SWA_H + j) for j in range(SWA_KV)], axis=0)
        vo = jnp.concatenate([head(x, SWA_H + SWA_KV + j) for j in range(SWA_KV)], axis=0)
        return qo, ko, vo

    hm = lambda nh: ((nh, rows, SWA_D), BF16, (nh, tr, SWA_D), lambda i: (0, i, 0), "r3")
    return rowwise(fn, [cols(qkv, tr), whole(wt)], [hm(SWA_H), hm(SWA_KV), hm(SWA_KV)],
                   steps=rows // tr, name="qknorm_fwd")


def qknorm_bwd(qkv, qw, kw, dqh, dk, dv):
    rows = qkv.shape[0]
    tr = _pick(rows, (384, 128))
    wt, st = _qk_scales(qw, kw)

    def fn(i, x, w, sc, dq, dkv, dvv):
        xq = x[:, :QK_W]
        dy = jnp.concatenate([dq[j] for j in range(SWA_H)] + [dkv], axis=1)
        r = lax.rsqrt(_head_mean(xq * xq) + EPS)
        xh = xq * r
        gw = dy * w
        dx = r * (gw - xh * _head_mean(gw * xh))
        return jnp.concatenate([dx, dvv], axis=1), jnp.sum(dy * sc * xh, axis=0, keepdims=True)

    dqkv, dw = rowwise(fn, [cols(qkv, tr), whole(wt), whole(st), heads(dqh, tr), cols(dk, tr), cols(dv, tr)],
                       [out2d(rows, 1536, BF16, tr)], steps=rows // tr, name="qknorm_bwd", accs=[((1, QK_W), F32)])
    dw = dw.reshape(SWA_H + SWA_KV, SWA_D)
    return dqkv, jnp.sum(dw[:SWA_H], axis=0, keepdims=True), jnp.sum(dw[SWA_H:], axis=0, keepdims=True)


def _place():
    return lax.axis_index("x"), lax.axis_index("y"), lax.axis_index("c")


ANY = pl.BlockSpec(memory_space=pl.ANY)


def _rcopy(ssem, rsem, k, src, dst, to):
    return pltpu.make_async_remote_copy(src_ref=src, dst_ref=dst, send_sem=ssem.at[k], recv_sem=rsem.at[k],
                                        device_id=to, device_id_type=MESH)


def gather_weights(shards, small):
    n = len(shards)
    halves = [t.shape[0] // 2 for t in shards]

    def body(*refs):
        s_refs, small_ref = refs[:n], refs[n]
        o_refs, osmall = refs[n + 1:2 * n + 1], refs[2 * n + 1]
        ssem, rsem, lsem = refs[2 * n + 2:]
        x, y, c = _place()
        me = 2 * x + y
        chips = [(1 - x, y), (x, 1 - y), (1 - x, 1 - y)]

        def half(k, s, hh):
            return o_refs[k].at[s, pl.ds(hh * halves[k], halves[k]), :]

        loc = pltpu.make_async_copy(small_ref, osmall.at[me], lsem)
        loc.start()
        sends = []
        for k in range(n):
            for j, (px, py) in enumerate(chips):
                sends.append(_rcopy(ssem, rsem, 6 * k + j, s_refs[k].at[pl.ds(c * halves[k], halves[k]), :],
                                    half(k, me, c), (px, py, c)))
        for j, (px, py) in enumerate(chips):
            sends.append(_rcopy(ssem, rsem, 6 * n + j, small_ref, osmall.at[me], (px, py, c)))
        for cp in sends:
            cp.start()
        for k in range(n):
            for j, (px, py) in enumerate(chips):
                s = 2 * px + py
                _rcopy(ssem, rsem, 6 * k + j, half(k, s, c), half(k, s, c), (x, y, c)).wait_recv()
                fwd = _rcopy(ssem, rsem, 6 * k + 3 + j, half(k, s, c), half(k, s, c), (x, y, 1 - c))
                fwd.start()
                sends.append(fwd)
        for k in range(n):
            for j, (px, py) in enumerate(chips):
                s = 2 * px + py
                _rcopy(ssem, rsem, 6 * k + 3 + j, half(k, s, 1 - c), half(k, s, 1 - c), (x, y, c)).wait_recv()
        for j, (px, py) in enumerate(chips):
            s = 2 * px + py
            _rcopy(ssem, rsem, 6 * n + j, osmall.at[s], osmall.at[s], (x, y, c)).wait_recv()
        for cp in sends:
            cp.wait_send()
        loc.wait()

    res = pl.pallas_call(
        body, name="gather_weights", interpret=False,
        out_shape=[jax.ShapeDtypeStruct((4,) + t.shape, t.dtype) for t in shards]
        + [jax.ShapeDtypeStruct((4, SW_ROWS, 1024), F32)],
        in_specs=[ANY] * (n + 1), out_specs=[ANY] * (n + 1),
        scratch_shapes=[pltpu.SemaphoreType.DMA((6 * n + 3,)), pltpu.SemaphoreType.DMA((6 * n + 3,)),
                        pltpu.SemaphoreType.DMA],
    )(*shards, small)
    return res[:n], res[n]


def _handshake(peers):
    barrier = pltpu.get_barrier_semaphore()
    for peer in peers:
        pl.semaphore_signal(barrier, inc=1, device_id=peer, device_id_type=MESH)
    pl.semaphore_wait(barrier, len(peers))


def gather_weights_beside(shards, cid, name):
    n = len(shards)
    halves = [t.shape[0] // 2 for t in shards]

    def body(*refs):
        s_refs, o_refs, ssem, rsem = refs[:n], refs[n:2 * n], refs[2 * n], refs[2 * n + 1]
        x, y, c = _place()
        me = 2 * x + y
        chips = [(1 - x, y), (x, 1 - y), (1 - x, 1 - y)]
        _handshake([(px, py, c) for px, py in chips] + [(x, y, 1 - c)])

        def half(k, s, hh):
            return o_refs[k].at[s, pl.ds(hh * halves[k], halves[k]), :]

        sends = []
        for k in range(n):
            for j, (px, py) in enumerate(chips):
                sends.append(_rcopy(ssem, rsem, 6 * k + j, s_refs[k].at[pl.ds(c * halves[k], halves[k]), :],
                                    half(k, me, c), (px, py, c)))
        for cp in sends:
            cp.start()
        for k in range(n):
            for j, (px, py) in enumerate(chips):
                s = 2 * px + py
                _rcopy(ssem, rsem, 6 * k + j, half(k, s, c), half(k, s, c), (x, y, c)).wait_recv()
                fwd = _rcopy(ssem, rsem, 6 * k + 3 + j, half(k, s, c), half(k, s, c), (x, y, 1 - c))
                fwd.start()
                sends.append(fwd)
        for k in range(n):
            for j, (px, py) in enumerate(chips):
                s = 2 * px + py
                _rcopy(ssem, rsem, 6 * k + 3 + j, half(k, s, 1 - c), half(k, s, 1 - c), (x, y, c)).wait_recv()
        for cp in sends:
            cp.wait_send()

    return pl.kernel(
        body, name=name,
        out_type=[jax.ShapeDtypeStruct((4,) + t.shape, t.dtype) for t in shards],
        mesh=plsc.ScalarSubcoreMesh(axis_name="sequencer", num_cores=1),
        scratch_types=[pltpu.SemaphoreType.DMA((6 * n,)), pltpu.SemaphoreType.DMA((6 * n,))],
        compiler_params=pltpu.CompilerParams(collective_id=cid),
    )(*shards)


def swap_halves(gs, *, name):
    n = len(gs)

    def body(*refs):
        g_refs, o_refs, ssem, rsem = refs[:n], refs[n:2 * n], refs[2 * n], refs[2 * n + 1]
        x, y, c = _place()
        cps = []
        for k in range(n):
            hk = g_refs[k].shape[1] // 2
            cps.append(_rcopy(ssem, rsem, k, g_refs[k].at[:, pl.ds((1 - c) * hk, hk), :], o_refs[k], (x, y, 1 - c)))
        for cp in cps:
            cp.start()
        for cp in cps:
            cp.wait()

    return pl.pallas_call(
        body, name=name, interpret=False,
        out_shape=[jax.ShapeDtypeStruct((4, t.shape[1] // 2, t.shape[2]), t.dtype) for t in gs],
        in_specs=[ANY] * n, out_specs=[ANY] * n,
        scratch_shapes=[pltpu.SemaphoreType.DMA((n,)), pltpu.SemaphoreType.DMA((n,))],
    )(*gs)


def _sum_rows(hk):
    return _pick(hk, (512, 352, 256, 128))


def pair_sum(g, other, c_idx, *, name):
    _, hk, width = other.shape
    tr = _sum_rows(hk)
    nbk = hk // tr

    def body(c_ref, g_ref, o_ref, out_ref):
        out_ref[...] = (g_ref[...].astype(F32) + o_ref[...].astype(F32)).astype(BF16)

    return pl.pallas_call(
        body, name=name, interpret=False,
        out_shape=jax.ShapeDtypeStruct((4, hk, width), BF16),
        grid_spec=pltpu.PrefetchScalarGridSpec(
            num_scalar_prefetch=1, grid=(4, nbk),
            in_specs=[pl.BlockSpec((1, tr, width), lambda s, i, c_ref: (s, c_ref[0] * nbk + i, 0)),
                      pl.BlockSpec((1, tr, width), lambda s, i, c_ref: (s, i, 0))],
            out_specs=pl.BlockSpec((1, tr, width), lambda s, i, c_ref: (s, i, 0))),
        compiler_params=_params(("parallel", "parallel")),
    )(c_idx, g, other)


def chip_sum(p, got, idx, *, name):
    _, hk, width = got.shape
    tr = _sum_rows(hk)
    nbk = hk // tr

    def body(idx_ref, p_ref, g_ref, out_ref):
        acc = p_ref[0].astype(F32)
        for j in range(3):
            acc = acc + g_ref[j].astype(F32)
        out_ref[0] = acc

    return pl.pallas_call(
        body, name=name, interpret=False,
        out_shape=jax.ShapeDtypeStruct((2, hk, width), F32),
        grid_spec=pltpu.PrefetchScalarGridSpec(
            num_scalar_prefetch=1, grid=(nbk,),
            in_specs=[pl.BlockSpec((1, tr, width), lambda i, idx_ref: (idx_ref[0], i, 0)),
                      pl.BlockSpec((3, tr, width), lambda i, idx_ref: (0, i, 0))],
            out_specs=pl.BlockSpec((1, tr, width), lambda i, idx_ref: (idx_ref[1], i, 0))),
        compiler_params=_params(("parallel",)),
    )(idx, p, got)


def join_halves(qs):
    n = len(qs)

    def body(*refs):
        q_refs, o_refs, ssem, rsem = refs[:n], refs[n:2 * n], refs[2 * n], refs[2 * n + 1]
        x, y, c = _place()
        cps = [_rcopy(ssem, rsem, k, q_refs[k].at[c], o_refs[k].at[c], (x, y, 1 - c)) for k in range(n)]
        for cp in cps:
            cp.start()
        for k in range(n):
            _rcopy(ssem, rsem, k, q_refs[k].at[c], o_refs[k].at[1 - c], (x, y, 1 - c)).wait_recv()
        for cp in cps:
            cp.wait_send()

    return pl.pallas_call(
        body, name="join_halves", interpret=False,
        out_shape=[jax.ShapeDtypeStruct(t.shape, t.dtype) for t in qs],
        in_specs=[ANY] * n, out_specs=[ANY] * n, input_output_aliases={k: k for k in range(n)},
        scratch_shapes=[pltpu.SemaphoreType.DMA((n,)), pltpu.SemaphoreType.DMA((n,))],
    )(*qs)


def scatter_chips_beside(ps, cid, name):
    n = len(ps)

    def body(*refs):
        p_refs, o_refs, ssem, rsem = refs[:n], refs[n:2 * n], refs[2 * n], refs[2 * n + 1]
        x, y, c = _place()
        chips = [(1 - x, y), (x, 1 - y), (1 - x, 1 - y)]
        _handshake([(px, py, c) for px, py in chips])
        cps = [_rcopy(ssem, rsem, 3 * k + j, p_refs[k].at[2 * px + py], o_refs[k].at[j], (px, py, c))
               for k in range(n) for j, (px, py) in enumerate(chips)]
        for cp in cps:
            cp.start()
        for cp in cps:
            cp.wait()

    return pl.kernel(
        body, name=name, out_type=[jax.ShapeDtypeStruct((3,) + t.shape[1:], t.dtype) for t in ps],
        mesh=plsc.ScalarSubcoreMesh(axis_name="sequencer", num_cores=1),
        scratch_types=[pltpu.SemaphoreType.DMA((3 * n,)), pltpu.SemaphoreType.DMA((3 * n,))],
        compiler_params=pltpu.CompilerParams(collective_id=cid),
    )(*ps)


def reduce_begin(gs, names, c_idx, cid, tag):
    others = swap_halves(gs, name=f"swap_halves_{tag}")
    pairs = [pair_sum(g, o, c_idx, name=f"pair_sum_{nm}") for g, o, nm in zip(gs, others, names)]
    return pairs, scatter_chips_beside(pairs, cid, f"scatter_chips_{tag}")


def reduce_end(pairs, gots, names, idx):
    mine = [chip_sum(p, g, idx, name=f"chip_sum_{nm}") for p, g, nm in zip(pairs, gots, names)]
    return [q.reshape(2 * q.shape[1], q.shape[2]) for q in join_halves(mine)]


def gather_small(v):
    def body(v_ref, o_ref, ssem, rsem, lsem):
        x, y, c = _place()
        peers = []
        for k in range(1, 8):
            fx, fy, fc = (k >> 2) & 1, (k >> 1) & 1, k & 1
            peers.append((1 - x if fx else x, 1 - y if fy else y, 1 - c if fc else c))
        _handshake(peers)
        loc = pltpu.make_async_copy(v_ref, o_ref.at[4 * x + 2 * y + c], lsem)
        loc.start()
        cps = []
        for k, (px, py, pc) in enumerate(peers):
            cps.append((pltpu.make_async_remote_copy(
                src_ref=v_ref, dst_ref=o_ref.at[4 * x + 2 * y + c], send_sem=ssem.at[k], recv_sem=rsem.at[k],
                device_id=(px, py, pc), device_id_type=MESH), 4 * px + 2 * py + pc))
        for cp, _ in cps:
            cp.start()
        for k, (cp, peer) in enumerate(cps):
            pltpu.make_async_remote_copy(
                src_ref=v_ref, dst_ref=o_ref.at[peer], send_sem=ssem.at[k], recv_sem=rsem.at[k],
                device_id=(x, y, c), device_id_type=MESH).wait_recv()
        for cp, _ in cps:
            cp.wait_send()
        loc.wait()

    return pl.kernel(
        body, name="gather_small", out_type=jax.ShapeDtypeStruct((8, SV_ROWS, 1024), F32),
        mesh=plsc.ScalarSubcoreMesh(axis_name="sequencer", num_cores=1),
        scratch_types=[pltpu.SemaphoreType.DMA((7,)), pltpu.SemaphoreType.DMA((7,)), pltpu.SemaphoreType.DMA],
        compiler_params=pltpu.CompilerParams(collective_id=6),
    )(v)


def sum_slots(a):
    def fn(i, t):
        acc = t[0]
        for k in range(1, 8):
            acc = acc + t[k]
        return acc

    return rowwise(fn, [whole(a)], [((SV_ROWS, 1024), F32, (SV_ROWS, 1024), lambda i: (0, 0), "w")], steps=1,
                   name="sum_slots")[0]


def _head_rms(x, nw):
    xs, rs = [], []
    for h in range(DN_H):
        xh = x[:, h * DN_D:(h + 1) * DN_D]
        r = lax.rsqrt(jnp.mean(xh * xh, axis=1, keepdims=True) + EPS)
        xs.append(xh * r)
        rs.append(r)
    return xs, rs


def bg_fwd(p, alog, dtb):
    rows = p.shape[0]
    tr = _pick(rows, (384, 128))

    def fn(i, x, al, dt):
        lane = lax.broadcasted_iota(jnp.int32, x.shape, 1)
        row = i + lax.broadcasted_iota(jnp.int32, x.shape, 0)
        g = -jnp.exp(al) * _softplus(x + dt)
        out = jnp.where(lane < 4, _sigmoid(x), jnp.where(lane < 8, g, 0.0))
        return jnp.where(row >= PAD, out, 0.0)

    return rowwise(fn, [cols(p, tr, 128, BG0 // 128), whole(alog), whole(dtb)], [out2d(rows, 128, F32, tr)],
                   steps=rows // tr, name="bg_fwd")[0]


def bg_bwd(p, alog, dtb, dbg):
    rows = p.shape[0]
    tr = _pick(rows, (384, 128))

    def fn(i, x, al, dt, g_in):
        lane = lax.broadcasted_iota(jnp.int32, x.shape, 1)
        row = i + lax.broadcasted_iota(jnp.int32, x.shape, 0)
        live = row >= PAD
        is_b = jnp.logical_and(live, lane < 4)
        is_g = jnp.logical_and(live, jnp.logical_and(lane >= 4, lane < 8))
        beta = _sigmoid(x)
        ea = jnp.exp(al)
        g = -ea * _softplus(x + dt)
        dalpha = jnp.where(is_g, g_in * (-ea) * _sigmoid(x + dt), 0.0)
        dx = jnp.where(is_b, g_in * beta * (1.0 - beta), dalpha)
        dal = jnp.sum(jnp.where(is_g, g_in * g, 0.0), axis=0, keepdims=True)
        return jnp.concatenate([dx, jnp.zeros(x.shape, F32)], axis=1), dal, jnp.sum(dalpha, axis=0, keepdims=True)

    return rowwise(fn, [cols(p, tr, 128, BG0 // 128), whole(alog), whole(dtb), cols(dbg, tr)],
                   [out2d(rows, 256, BF16, tr)], steps=rows // tr, name="bg_bwd",
                   accs=[((1, 128), F32), ((1, 128), F32)])


def dn_qkv_post(j, y):
    xs = _silu(y)
    sc = jnp.where(j == 0, DN_D ** -0.5, 1.0)
    outs = []
    for h in range(DN_H):
        xh = xs[:, h * DN_D:(h + 1) * DN_D]
        r = lax.rsqrt(jnp.sum(xh * xh, axis=1, keepdims=True) + EPS)
        outs.append(jnp.where(j < 2, xh * r * sc, xh))
    return jnp.concatenate(outs, axis=1), y


def dn_qkv_bwd(cq, dq, dk, dv):
    rows = cq.shape[0]
    tr = _pick(rows, (384, 128))

    def fn(i, c0, c1, c2, g0, g1, g2):
        pieces = []
        for kind, (cv, g) in enumerate(((c0, g0), (c1, g1), (c2, g2))):
            xs = _silu(cv)
            if kind < 2:
                sc = DN_D ** -0.5 if kind == 0 else 1.0
                ds = []
                for h in range(DN_H):
                    sl = slice(h * DN_D, (h + 1) * DN_D)
                    xh, gh = xs[:, sl], g[:, sl]
                    r = lax.rsqrt(jnp.sum(xh * xh, axis=1, keepdims=True) + EPS)
                    xn = xh * r
                    ds.append(sc * r * (gh - xn * jnp.sum(gh * xn, axis=1, keepdims=True)))
                dxs = jnp.concatenate(ds, axis=1)
            else:
                dxs = g
            pieces.append(dxs * _dsilu(cv))
        return jnp.concatenate(pieces, axis=1)

    ins = [cols(cq, tr, DN_DIM, k) for k in range(3)] + [cols(t, tr) for t in (dq, dk, dv)]
    return rowwise(fn, ins, [out2d(rows, 3 * DN_DIM, F32, tr)], steps=rows // tr, name="dn_qkv_bwd")[0]


def dn_out_fwd(o, p, nw):
    rows = o.shape[0]
    tr = _pick(rows, (384, 128))

    def fn(i, ov, z, w):
        xs, _ = _head_rms(ov, w)
        return jnp.concatenate(xs, axis=1) * jnp.concatenate([w] * DN_H, axis=1) * _silu(z)

    return rowwise(fn, [cols(o, tr), cols(p, tr, DN_DIM, 6), whole(nw)], [out2d(rows, DN_DIM, BF16, tr)],
                   steps=rows // tr, name="dn_out_fwd")[0]


def dn_out_bwd(o, p, nw, dymix):
    rows = o.shape[0]
    tr = _pick(rows, (384, 128))

    def fn(i, ov, z, w, dy):
        xs, rs = _head_rms(ov, w)
        sz = _silu(z)
        dn = dy * sz
        dos, dw = [], jnp.zeros((1, DN_D), F32)
        for h in range(DN_H):
            sl = slice(h * DN_D, (h + 1) * DN_D)
            gw = dn[:, sl] * w
            dos.append(rs[h] * (gw - xs[h] * jnp.mean(gw * xs[h], axis=1, keepdims=True)))
            dw = dw + jnp.sum(dn[:, sl] * xs[h], axis=0, keepdims=True)
        n = jnp.concatenate(xs, axis=1) * jnp.concatenate([w] * DN_H, axis=1)
        return jnp.concatenate(dos, axis=1), dy * n * _dsilu(z), dw

    return rowwise(fn, [cols(o, tr), cols(p, tr, DN_DIM, 6), whole(nw), cols(dymix, tr, DN_DIM, 1)],
                   [out2d(rows, DN_DIM, F32, tr), out2d(rows, DN_DIM, BF16, tr)], steps=rows // tr,
                   name="dn_out_bwd", accs=[((1, DN_D), F32)])


def conv_a_pre_bwd(dymix, cv, p):
    rows = cv.shape[0]
    tr = _pick(rows, (384, 128))

    def fn(i, dy, c, go):
        return dy * c, dy * go

    return rowwise(fn, [cols(dymix, tr, D_CONV, 0), cols(cv, tr), cols(p, tr, D_CONV, 1)],
                   [out2d(rows, D_CONV, BF16, tr), out2d(rows, D_CONV, F32, tr)], steps=rows // tr,
                   name="conv_a_pre_bwd")


def _rows8(w):
    return jnp.pad(w.astype(F32), ((0, 8 - w.shape[0]), (0, 0)))


def _lanes(v, at):
    return jnp.pad(v.astype(F32), (at, 128 - at - v.shape[0]))[None]


def add_norm(a, w, h, next_nw, *, name):
    return mm(a, w, name=name, epi=_add_norm_epi, epi_ins=[(h, lambda j: 0)], epi_consts=[next_nw],
              epi_outs=[F32, BF16])


def _add_norm_epi(row0, t, h, nw):
    x = t + h
    return x, x * lax.rsqrt(jnp.mean(x * x, axis=1, keepdims=True) + EPS) * nw


def ffn_up_conv(hn, w_up, cw8, *, name):
    rows = hn.shape[0]
    tn = w_up.shape[2]
    tm = _pick(rows, (384, 128))
    nr = rows // tm

    def body(x_ref, wg_ref, wv_ref, w_ref, ug_ref, uv_ref, gc_ref, a_ref, carry, scr):
        i = pl.program_id(1)
        x = x_ref[...]
        gate = _dot(x, wg_ref[...])
        val = _dot(x, wv_ref[...])
        ug_ref[...] = gate.astype(BF16)
        uv_ref[...] = val.astype(BF16)
        scr[0:8, :] = jnp.where(i > 0, carry[...], 0.0)
        scr[8:8 + tm, :] = gate
        carry[...] = gate[tm - 8:tm]
        y = jnp.zeros((tm, tn), F32)
        for q in range(3):
            sh = 2 - q
            y = y + w_ref[q:q + 1, :] * scr[8 - sh:8 - sh + tm, :]
        gc_ref[...] = y.astype(BF16)
        a_ref[...] = (_silu(y) * val).astype(BF16)

    half = pl.BlockSpec((tm, tn), lambda j, i: (i, j))
    return pl.pallas_call(
        body, name=name, interpret=False,
        out_shape=[jax.ShapeDtypeStruct((rows, D_FF), BF16)] * 4,
        grid=(D_FF // tn, nr),
        in_specs=[pl.BlockSpec((tm, D), lambda j, i: (i, 0)),
                  pl.BlockSpec((None, D, tn), lambda j, i: (j, 0, 0)),
                  pl.BlockSpec((None, D, tn), lambda j, i: (j + D_FF // tn, 0, 0)),
                  pl.BlockSpec((8, tn), lambda j, i: (0, j))],
        out_specs=[half] * 4,
        scratch_shapes=[pltpu.VMEM((8, tn), F32), pltpu.VMEM((tm + 8, tn), F32)],
        compiler_params=_params(("arbitrary", "arbitrary")),
    )(hn, w_up, w_up, cw8)


def ffn_down_bwd(dh, w_down, gc, uv, ug, cw8, *, name):
    rows = dh.shape[0]
    tn = D_FF // 2
    tm = _pick(rows, (384, 128))
    nr = rows // tm
    r8 = tm // 8

    def body(dh_ref, w_ref, gc_ref, uv_ref, ug_ref, halo_ref, cw_ref, du_ref, dw_ref, carry, gscr, xscr):
        ip = pl.program_id(1)
        i = nr - 1 - ip
        da = _dot(dh_ref[...].astype(BF16), w_ref[...], 1, 1)
        c, val = gc_ref[...].astype(F32), uv_ref[...].astype(F32)
        dgc = da * val * _dsilu(c)
        du_ref[:, tn:] = (da * _silu(c)).astype(BF16)
        gscr[0:tm, :] = dgc
        gscr[tm:tm + 8, :] = jnp.where(ip > 0, carry[...], 0.0)
        carry[...] = dgc[0:8]
        xscr[0:8, :] = jnp.where(i > 0, halo_ref[...].astype(F32), 0.0)
        xscr[8:8 + tm, :] = ug_ref[...].astype(F32)
        dx = jnp.zeros((tm, tn), F32)
        dws = []
        for q in range(3):
            sh = 2 - q
            dx = dx + cw_ref[q:q + 1, :] * gscr[sh:sh + tm, :]
            dws.append(jnp.sum(dgc * xscr[8 - sh:8 - sh + tm, :], axis=0, keepdims=True))
        du_ref[:, :tn] = dx.astype(BF16)

        @pl.when(ip == 0)
        def _():
            dw_ref[...] = jnp.zeros((8, tn), F32)

        dw_ref[...] += jnp.concatenate(dws + [jnp.zeros((5, tn), F32)], axis=0)

    rev = lambda ip: nr - 1 - ip
    tile = lambda arr: pl.BlockSpec((tm, tn), lambda j, ip: (rev(ip), j))
    return pl.pallas_call(
        body, name=name, interpret=False,
        out_shape=[jax.ShapeDtypeStruct((rows, 2 * D_FF), BF16), jax.ShapeDtypeStruct((8, D_FF), F32)],
        grid=(2, nr),
        in_specs=[pl.BlockSpec((tm, D), lambda j, ip: (rev(ip), 0)),
                  pl.BlockSpec((tn, D), lambda j, ip: (j, 0)),
                  tile(gc), tile(uv), tile(ug),
                  pl.BlockSpec((8, tn), lambda j, ip: (jnp.maximum(rev(ip) * r8 - 1, 0), j)),
                  pl.BlockSpec((8, tn), lambda j, ip: (0, j))],
        out_specs=[pl.BlockSpec((tm, 2 * tn), lambda j, ip: (rev(ip), j)),
                   pl.BlockSpec((8, tn), lambda j, ip: (0, j))],
        scratch_shapes=[pltpu.VMEM((8, tn), F32), pltpu.VMEM((tm + 8, tn), F32), pltpu.VMEM((tm + 8, tn), F32)],
        compiler_params=_params(("arbitrary", "arbitrary")),
    )(dh, w_down, gc, uv, ug, ug, cw8)


def ffn_fwd(h, hn, w_up, cw8, w_down, tag, next_nw=None, target=None):
    ug, uv, gc, a = ffn_up_conv(hn, w_up, cw8, name=f"ffn{tag}_up")
    if target is not None:
        out, hn_next = add_loss(a, w_down, h, target, name=f"ffn{tag}_down")
    else:
        out, hn_next = add_norm(a, w_down, h, next_nw, name=f"ffn{tag}_down")
    return out, hn_next, (hn, ug, uv, a, gc)


def ffn_bwd(h, nw, w_up, cw8, w_down, saved, dh, tag):
    hn, ug, uv, a, gc = saved
    du, d_cw = ffn_down_bwd(dh, w_down, gc, uv, ug, cw8, name=f"ffn{tag}_down_dx")
    d_w_down = mm(a, dh, ta=True, out_dtype=BF16, name=f"ffn{tag}_down_dw")
    dh_new, d_nw = dx_rms_bwd(du, w_up, h, nw, dh, name=f"ffn{tag}_up_dx", b_chip=True, swap_mid=True)
    d_w_up = mm(hn, du, ta=True, out_dtype=BF16, out_chip=True, swap_mid=True, name=f"ffn{tag}_up_dw")
    return dh_new, d_nw, d_w_up, d_cw, d_w_down


def mixer_fwd(h, nw, w_in, ca8, dc8, alog, dtb, dnw, w_out, tie=None, next_nw=None):
    rows = h.shape[0]
    tr = _pick(rows, (384, 128))
    hn = rms_fwd(h, nw, name="mix_norm")
    if callable(w_in):
        hn, w_in = w_in(hn)
    p = mm(hn, w_in, name="mix_in")
    y_a, cv = conv_fwd([(p, 0), (p, 2)], ca8, 3, rows=rows, c=D_CONV, tc=D_CONV, tr=tr, name="conv_a",
                       pre=lambda gi, ah: gi * ah, post=lambda j, y, go: (go * y, y), extras=[(p, 1)],
                       outs=[BF16, F32])
    qkv_n, cq = conv_fwd([(p, 3)], dc8, 4, rows=rows, c=3 * DN_DIM, tc=DN_DIM, tr=tr, name="dn_conv",
                         post=dn_qkv_post, outs=[F32, F32], strip=tr)
    bgcol = bg_fwd(p, alog, dtb)
    if tie is not None:
        bgcol = tie(bgcol)
    bgrow = bgcol[:, :8].reshape(rows // CH, CH, 8).transpose(0, 2, 1)
    o, s_all, ti_all = dn_fwd(qkv_n, bgcol, bgrow)
    y_b = dn_out_fwd(o, p, dnw)
    ymix = jnp.concatenate([y_a, y_b], axis=1)
    w_out = w_out() if callable(w_out) else w_out
    out, hn_next = add_norm(ymix, w_out, h, next_nw, name="mix_out")
    return out, hn_next, (hn, p, cv, qkv_n, cq, bgcol, bgrow, o, s_all, ti_all, ymix, w_in)


def mixer_bwd(h, nw, ca8, dc8, alog, dtb, dnw, w_out, saved, dh):
    hn, p, cv, qkv_n, cq, bgcol, bgrow, o, s_all, ti_all, ymix, w_in = saved
    rows = h.shape[0]
    tr = _pick(rows, (384, 128))
    dymix = mm(dh, w_out, tb=True, name="mix_out_dx")
    d_w_out = mm(ymix, dh, ta=True, out_dtype=BF16, name="mix_out_dw")
    do, dz, d_dnw = dn_out_bwd(o, p, dnw, dymix)
    dq, dk, dv, dbg = dn_bwd(qkv_n, bgcol, bgrow, s_all, ti_all, do)
    dbg_p, d_alog, d_dtb = bg_bwd(p, alog, dtb, dbg)
    dcq = dn_qkv_bwd(cq, dq, dk, dv)
    dqkv, d_dc = conv_bwd([(p, 3)], dc8, 4, dcq, rows=rows, c=3 * DN_DIM, tc=DN_DIM, tr=tr, name="dn_conv_bwd",
                          post=lambda dx: dx, outs=[BF16])
    dgo, dcv = conv_a_pre_bwd(dymix, cv, p)
    dgi, dah, d_ca = conv_bwd([(p, 0), (p, 2)], ca8, 3, dcv, rows=rows, c=D_CONV, tc=D_CONV, tr=tr,
                              name="conv_a_bwd", pre=lambda gi, ah: gi * ah,
                              post=lambda dm, gi, ah: (dm * ah, dm * gi), extras=[(p, 0), (p, 2)], outs=[BF16, BF16])
    dp = jnp.concatenate([dgi, dgo, dah, dqkv, dz, dbg_p], axis=1)
    dh_new, d_nw = dx_rms_bwd(dp, w_in, h, nw, dh, name="mix_in_dx")
    d_w_in = mm(hn, dp, ta=True, out_dtype=BF16, name="mix_in_dw")
    return dh_new, d_nw, d_w_in, d_ca, d_dc, d_alog, d_dtb, d_dnw, d_w_out


def swa_layer_fwd(h, hn, wqkv, qw, kw, sinks, wo, next_nw):
    qkv = mm(hn, wqkv, name="swa_qkv")
    qh, kh, vh = qknorm_fwd(qkv, qw, kw)
    att = swa_fwd(qh, kh, vh, sinks)
    out, hn_next = add_norm(att, wo, h, next_nw, name="swa_out")
    return out, hn_next, (hn, qkv, qh, kh, vh, att)


def swa_layer_bwd(h, nw, wqkv, qw, kw, sinks, wo, saved, dh):
    hn, qkv, qh, kh, vh, att = saved
    datt = mm(dh, wo, tb=True, out_dtype=BF16, name="swa_out_dx")
    d_wo = mm(att, dh, ta=True, out_dtype=BF16, name="swa_out_dw")
    dqh, dkh, dvh, dsk = swa_bwd(qh, kh, vh, sinks, datt)
    dqkv, d_qw, d_kw = qknorm_bwd(qkv, qw, kw, dqh, dkh, dvh)
    dh_new, d_nw = dx_rms_bwd(dqkv, wqkv, h, nw, dh, name="swa_qkv_dx")
    d_wqkv = mm(hn, dqkv, ta=True, out_dtype=BF16, name="swa_qkv_dw")
    d_sinks = jnp.sum(dsk[:, :, 0], axis=0)
    return dh_new, d_nw, d_wqkv, d_qw, d_kw, d_sinks, d_wo


BIG = ("mix_w_in", "mix_w_out", "swa_wq", "swa_wk", "swa_wv", "swa_wo", "ffn_w_up", "ffn_w_down")


def _flat_pad(parts, rows):
    v = jnp.concatenate([t.astype(F32).reshape(-1) for t in parts])
    return jnp.pad(v, (0, rows * 1024 - v.shape[0])).reshape(rows, 1024)


def _split_flat(flat, shapes):
    v = flat.reshape(-1)
    out, o = [], 0
    for s in shapes:
        n = 1
        for d_ in s:
            n *= d_
        out.append(v[o:o + n].reshape(s))
        o += n
    return out


def local_step(x0, target0, meta_full, anw, fnw, w_in, ca8, dc8, alog, dtb, dnw, qw, kw, sinks, fc8, late,
               begin=None, tie=None):
    begin = begin or (lambda tag, names, grads: None)
    h0 = jnp.concatenate([jnp.zeros((PAD, D), F32), meta_full, x0], axis=0)
    h1, hn1, s_mix = mixer_fwd(h0, anw[0], w_in, ca8, dc8, alog, dtb, dnw, lambda: late()[0], tie, fnw[0])
    w_out, wqkv, wo, w_up, w_down = late()
    h2, hn2, s_f0 = ffn_fwd(h1, hn1, w_up[0], fc8[0], w_down[0], 0, anw[1])
    h3, hn3, s_swa = swa_layer_fwd(h2, hn2, wqkv, qw, kw, sinks, wo, fnw[1])
    dh, loss_l, s_f1 = ffn_fwd(h3, hn3, w_up[1], fc8[1], w_down[1], 1, target=target0)
    dh, d_fnw1, d_up1, d_fc1, d_down1 = ffn_bwd(h3, fnw[1], w_up[1], fc8[1], w_down[1], s_f1, dh, 1)
    begin("ffn1", ("up1", "down1"), [d_up1, d_down1.reshape(4, 704, D)])
    dh, d_anw1, d_wqkv, d_qw, d_kw, d_sinks, d_wo = swa_layer_bwd(h2, anw[1], wqkv, qw, kw, sinks, wo, s_swa, dh)
    begin("swa", ("wq", "wk", "wv", "wo"),
          [d_wqkv[:, :D].reshape(4, 256, D), d_wqkv[:, D:D + 256].reshape(4, 256, 256),
           d_wqkv[:, D + 256:].reshape(4, 256, 256), d_wo.reshape(4, 256, D)])
    dh, d_fnw0, d_up0, d_fc0, d_down0 = ffn_bwd(h1, fnw[0], w_up[0], fc8[0], w_down[0], s_f0, dh, 0)
    begin("ffn0", ("up0", "down0"), [d_up0, d_down0.reshape(4, 704, D)])
    dh, d_anw0, d_w_in, d_ca, d_dc, d_alog, d_dtb, d_dnw, d_w_out = mixer_bwd(
        h0, anw[0], ca8, dc8, alog, dtb, dnw, w_out, s_mix, dh)
    begin("mix", ("w_in", "w_out"),
          [d_w_in[:, :IN_DIM].reshape(D, 4, 898).transpose(1, 0, 2), d_w_out.reshape(4, 256, D)])
    return (dh, loss_l, d_anw0, d_anw1, d_fnw0, d_fnw1, d_w_in, d_ca, d_dc, d_alog, d_dtb, d_dnw, d_w_out, d_wqkv,
            d_qw, d_kw, d_sinks, d_wo, d_up0, d_up1, d_fc0, d_fc1, d_down0, d_down1)


def kernel(x, meta_tokens, attn_norm_w, ffn_norm_w, mix_w_in, conv_a_w, dn_conv_w, dn_a_log, dn_dt_bias, dn_norm_w, mix_w_out, swa_wq, swa_wk, swa_wv, swa_q_norm_w, swa_k_norm_w, swa_sinks, swa_wo, ffn_w_up, ffn_conv_w, ffn_w_down, loss_target, m_meta_tokens, m_attn_norm_w, m_ffn_norm_w, m_mix_w_in, m_conv_a_w, m_dn_conv_w, m_dn_a_log, m_dn_dt_bias, m_dn_norm_w, m_mix_w_out, m_swa_wq, m_swa_wk, m_swa_wv, m_swa_q_norm_w, m_swa_k_norm_w, m_swa_sinks, m_swa_wo, m_ffn_w_up, m_ffn_conv_w, m_ffn_w_down, v_meta_tokens, v_attn_norm_w, v_ffn_norm_w, v_mix_w_in, v_conv_a_w, v_dn_conv_w, v_dn_a_log, v_dn_dt_bias, v_dn_norm_w, v_mix_w_out, v_swa_wq, v_swa_wk, v_swa_wv, v_swa_q_norm_w, v_swa_k_norm_w, v_swa_sinks, v_swa_wo, v_ffn_w_up, v_ffn_conv_w, v_ffn_w_down):
    ix, iy, ic = lax.axis_index("x"), lax.axis_index("y"), lax.axis_index("c")
    chip = 2 * ix + iy
    seq = x.shape[1]
    rows = HEAD0 + seq

    small_sharded = (conv_a_w, dn_conv_w, ffn_conv_w, meta_tokens)
    up_b, down_b = ffn_w_up.astype(BF16), ffn_w_down.astype(BF16)
    own = [mix_w_in[0].astype(BF16), mix_w_out[0].astype(BF16), swa_wq[0].astype(BF16), swa_wk[0].astype(BF16),
           swa_wv[0].astype(BF16), swa_wo[0].astype(BF16), up_b[0], up_b[1], down_b[0], down_b[1]]
    fill = lambda gathered, mine: [lax.dynamic_update_slice_in_dim(g, t[None], chip, axis=0)
                                   for g, t in zip(gathered, mine)]
    on_its_way, = gather_weights_beside(own[:1], 9, "gather_w_in")
    _, g_small = gather_weights([], _flat_pad(small_sharded, SW_ROWS))

    rest = {}

    def w_in(hn):
        hn, got, g_out = lax.optimization_barrier((hn, on_its_way, own[1]))
        rest["w_out"] = fill(gather_weights_beside([g_out], 1, "gather_w_out"), [g_out])
        g_in, = fill([got], own[:1])
        return hn, jnp.pad(g_in.transpose(1, 0, 2).reshape(D, IN_DIM), ((0, 0), (0, P_W - IN_DIM)))

    def tie(t):
        t, *mine = lax.optimization_barrier((t, *own[2:]))
        g_q, g_k, g_v, g_o, g_up0, g_up1, g_dn0, g_dn1 = mine
        soon, last = [g_up0, g_dn0, g_q, g_k, g_v, g_o], [g_up1, g_dn1]
        rest["soon"] = fill(gather_weights_beside(soon, 7, "gather_layers_12"), soon)
        rest["last"] = fill(gather_weights_beside(last, 8, "gather_layer_3"), last)
        return t

    def late():
        (g_out,), (g_up0, g_dn0, g_q, g_k, g_v, g_o), (g_up1, g_dn1) = rest["w_out"], rest["soon"], rest["last"]
        wqkv = jnp.concatenate([g_q.reshape(D, D), g_k.reshape(D, 256), g_v.reshape(D, 256)], axis=1)
        return (g_out.reshape(D, D), wqkv, g_o.reshape(D, D), [g_up0, g_up1],
                [g_dn0.reshape(D_FF, D), g_dn1.reshape(D_FF, D)])

    gs = g_small.reshape(4, -1)
    ca_full = gs[:, 0:384].reshape(4, 3, 128).transpose(1, 0, 2).reshape(3, D_CONV)
    dc_full = gs[:, 384:1920].reshape(4, 4, 384).transpose(1, 0, 2).reshape(4, 3 * DN_DIM)
    fc_full = gs[:, 1920:6144].reshape(4, 2, 3, 704).transpose(1, 2, 0, 3).reshape(2, 3, D_FF)
    meta_full = gs[:, 6144:10240].reshape(4, N_META, 256).transpose(1, 0, 2).reshape(N_META, D)
    ca8, dc8 = _rows8(ca_full), _rows8(dc_full)
    fc8 = [_rows8(fc_full[0]), _rows8(fc_full[1])]
    alog, dtb = _lanes(dn_a_log[0], 4), _lanes(dn_dt_bias[0], 4)
    dnw = dn_norm_w.astype(F32)
    qw, kw = swa_q_norm_w.astype(F32), swa_k_norm_w.astype(F32)
    sinks = swa_sinks[0].astype(F32)
    anw = [attn_norm_w[0:1], attn_norm_w[1:2]]
    fnw = [ffn_norm_w[0:1], ffn_norm_w[1:2]]

    c_idx = jnp.reshape(ic, (1,)).astype(jnp.int32)
    chip_idx = jnp.stack([chip, ic]).astype(jnp.int32)
    begun = []

    def begin(tag, names, grads):
        pairs, gots = reduce_begin(grads, names, c_idx, 2 + len(begun), tag)
        begun.append((names, pairs, gots))

    (dh, loss_l, d_anw0, d_anw1, d_fnw0, d_fnw1, d_w_in, d_ca, d_dc, d_alog, d_dtb, d_dnw, d_w_out, d_wqkv, d_qw,
     d_kw, d_sinks, d_wo, d_up0, d_up1, d_fc0, d_fc1, d_down0, d_down1) = local_step(
        x[0], loss_target[0], meta_full, anw, fnw, w_in, ca8, dc8, alog, dtb, dnw, qw, kw, sinks, fc8, late,
        begin, tie)
    grad_x = dh[HEAD0:][None]

    small_parts = [jnp.concatenate([d_anw0, d_anw1], axis=0), jnp.concatenate([d_fnw0, d_fnw1], axis=0),
                   d_alog[0, 4:8], d_dtb[0, 4:8], d_dnw, d_qw, d_kw, d_sinks,
                   d_ca[:3], d_dc[:4], jnp.stack([d_fc0[:3], d_fc1[:3]]), dh[PAD:HEAD0], loss_l[0, 0:1]]
    small_shapes = [(2, D), (2, D), (1, 4), (1, 4), (1, DN_D), (1, SWA_D), (1, SWA_D), (1, SWA_H),
                    (1, 3, D_CONV), (1, 4, 3 * DN_DIM), (2, 3, D_FF), (N_META, D), ()]
    gathered_small = gather_small(_flat_pad(small_parts, SV_ROWS))

    red_big = {}
    for part in (begun[:-1], begun[-1:]):
        part_names = [n for names, _, _ in part for n in names]
        red_big.update(zip(part_names, reduce_end([p for _, ps, _ in part for p in ps],
                                                  [g for _, _, gs_ in part for g in gs_], part_names, chip_idx)))
    g_w_in, g_w_out, g_wq, g_wk, g_wv, g_wo, g_up0, g_up1, g_dn0, g_dn1 = [
        red_big[n] for n in ("w_in", "w_out", "wq", "wk", "wv", "wo", "up0", "up1", "down0", "down1")]

    grads = dict(mix_w_in=g_w_in, mix_w_out=g_w_out, swa_wq=g_wq, swa_wk=g_wk, swa_wv=g_wv, swa_wo=g_wo,
                 ffn_w_up=[g_up0, g_up1], ffn_w_down=[g_dn0, g_dn1])
    weights = dict(meta_tokens=meta_tokens, attn_norm_w=attn_norm_w, ffn_norm_w=ffn_norm_w, mix_w_in=mix_w_in,
                   conv_a_w=conv_a_w, dn_conv_w=dn_conv_w, dn_a_log=dn_a_log, dn_dt_bias=dn_dt_bias,
                   dn_norm_w=dn_norm_w, mix_w_out=mix_w_out, swa_wq=swa_wq, swa_wk=swa_wk, swa_wv=swa_wv,
                   swa_q_norm_w=swa_q_norm_w, swa_k_norm_w=swa_k_norm_w, swa_sinks=swa_sinks, swa_wo=swa_wo,
                   ffn_w_up=ffn_w_up, ffn_conv_w=ffn_conv_w, ffn_w_down=ffn_w_down)
    m_in = dict(meta_tokens=m_meta_tokens, attn_norm_w=m_attn_norm_w, ffn_norm_w=m_ffn_norm_w, mix_w_in=m_mix_w_in,
                conv_a_w=m_conv_a_w, dn_conv_w=m_dn_conv_w, dn_a_log=m_dn_a_log, dn_dt_bias=m_dn_dt_bias,
                dn_norm_w=m_dn_norm_w, mix_w_out=m_mix_w_out, swa_wq=m_swa_wq, swa_wk=m_swa_wk, swa_wv=m_swa_wv,
                swa_q_norm_w=m_swa_q_norm_w, swa_k_norm_w=m_swa_k_norm_w, swa_sinks=m_swa_sinks, swa_wo=m_swa_wo,
                ffn_w_up=m_ffn_w_up, ffn_conv_w=m_ffn_conv_w, ffn_w_down=m_ffn_w_down)
    v_in = dict(meta_tokens=v_meta_tokens, attn_norm_w=v_attn_norm_w, ffn_norm_w=v_ffn_norm_w, mix_w_in=v_mix_w_in,
                conv_a_w=v_conv_a_w, dn_conv_w=v_dn_conv_w, dn_a_log=v_dn_a_log, dn_dt_bias=v_dn_dt_bias,
                dn_norm_w=v_dn_norm_w, mix_w_out=v_mix_w_out, swa_wq=v_swa_wq, swa_wk=v_swa_wk, swa_wv=v_swa_wv,
                swa_q_norm_w=v_swa_q_norm_w, swa_k_norm_w=v_swa_k_norm_w, swa_sinks=v_swa_sinks, swa_wo=v_swa_wo,
                ffn_w_up=v_ffn_w_up, ffn_conv_w=v_ffn_conv_w, ffn_w_down=v_ffn_w_down)
    names = list(weights)
    small = [n for n in names if n not in BIG]
    delta, new_m, new_v = {}, {}, {}
    for n in BIG:
        delta[n], new_m[n], new_v[n], grads[n] = adamw(weights[n], grads[n], m_in[n], v_in[n], name=f"adamw_{n}")
    gathered_small, _ = lax.optimization_barrier((gathered_small, new_v["ffn_w_down"]))
    (g_anw, g_fnw, g_alog, g_dtb, g_dnw, g_qw, g_kw, g_sinks, g_ca_f, g_dc_f, g_fc_f, g_meta_f,
     loss) = _split_flat(sum_slots(gathered_small), small_shapes)
    grads.update(meta_tokens=lax.dynamic_slice_in_dim(g_meta_f, chip * 256, 256, axis=1), attn_norm_w=g_anw,
                 ffn_norm_w=g_fnw, conv_a_w=lax.dynamic_slice_in_dim(g_ca_f, chip * 128, 128, axis=2),
                 dn_conv_w=lax.dynamic_slice_in_dim(g_dc_f, chip * 384, 384, axis=2), dn_a_log=g_alog,
                 dn_dt_bias=g_dtb, dn_norm_w=g_dnw, swa_q_norm_w=g_qw, swa_k_norm_w=g_kw, swa_sinks=g_sinks,
                 ffn_conv_w=lax.dynamic_slice_in_dim(g_fc_f, chip * 704, 704, axis=2))
    grads = {n: grads[n].reshape(weights[n].shape) for n in names}
    shapes = [weights[n].shape for n in small]
    packed = [_flat_pad([t[n] for n in small], SW_ROWS) for t in (weights, grads, m_in, v_in)]
    for store, flat in zip((delta, new_m, new_v), adamw(*packed, name="adamw_small")):
        for n, t in zip(small, _split_flat(flat, shapes)):
            store[n] = t
    return (loss, grad_x, *[grads[n] for n in names], *[delta[n] for n in names],
            *[new_m[n] for n in names], *[new_v[n] for n in names])
```

```python
import functools

import jax
import jax.numpy as jnp
from jax import lax
from jax.experimental import pallas as pl
from jax.experimental.pallas import tpu as pltpu
from jax.experimental.pallas import tpu_sc as plsc

F32 = jnp.float32
BF16 = jnp.bfloat16
HI = lax.Precision.HIGHEST
MESH = pl.DeviceIdType.MESH

D = 1024
N_META = 16
PAD = 112
HEAD0 = PAD + N_META
D_CONV = 512
DN_H = 4
DN_D = 128
DN_DIM = 512
CH = 64
IN_DIM = 3592
P_W = 3840
BG0 = 3584
SWA_H = 16
SWA_KV = 4
SWA_D = 64
BLK = 128
NKEY = N_META + 2 * BLK
D_FF = 2816
EPS = 1e-6
LR, B1, B2, AEPS, WD, STEP = 0.001, 0.9, 0.999, 1e-08, 0.01, 10
VMEM_LIMIT = 48 * 1024 * 1024
MM_VMEM_BUDGET = 34 * 1024 * 1024
R_BIG = 6144
R_HALF = R_BIG // 2
SV_ROWS = 48
SW_ROWS = 16


def _pick(n, cands):
    for c in cands:
        if n % c == 0:
            return c
    return n


def _params(sem=None):
    return pltpu.CompilerParams(dimension_semantics=sem, vmem_limit_bytes=VMEM_LIMIT)


def _dot(a, b, ca=1, cb=0, prec=None):
    return lax.dot_general(a, b, (((ca,), (cb,)), ((), ())), precision=prec,
                           preferred_element_type=F32)


def _sigmoid(x):
    return 1.0 / (1.0 + jnp.exp(-x))


def _silu(x):
    return x * _sigmoid(x)


def _dsilu(x):
    s = _sigmoid(x)
    return s * (1.0 + x * (1.0 - s))


def _softplus(x):
    return jnp.maximum(x, 0.0) + jnp.log(1.0 + jnp.exp(-jnp.abs(x)))


def mm(a, b, *, name, ta=False, tb=False, out_dtype=F32, add=None, tm=None, tn=None, tk=None,
       b_chip=False, out_chip=False, swap_mid=False, epi=None, epi_ins=(), epi_consts=(), epi_outs=(), epi_accs=()):
    if epi is not None:
        return _mm_epi(a, b, name=name, tb=tb, tn=tn, b_chip=b_chip, swap_mid=swap_mid, epi=epi, epi_ins=epi_ins,
                       epi_consts=epi_consts, epi_outs=epi_outs, epi_accs=epi_accs)
    chip_of = _chip_order(swap_mid)
    m, k = (a.shape[1], a.shape[0]) if ta else a.shape
    if b_chip:
        n = b.shape[1] if tb else 4 * b.shape[2]
        if tb:
            tk = b.shape[2]
        else:
            tn = b.shape[2]
    else:
        n = b.shape[0] if tb else b.shape[1]
    if out_chip:
        tn = n // 4
    tn = tn or _pick(n, (1408, 1024, 768, 512, 256, 128))
    tk = tk or (_pick(k, (1408, 704, 384, 128)) if ta else _pick(k, (1024, 1408, 768, 512, 128)))
    nk = k // tk
    if tm is None:
        isz = lambda t: jnp.dtype(t.dtype).itemsize
        osz = jnp.dtype(out_dtype).itemsize
        for tm in ((1408, 1024, 512, 384, 256, 128) if ta else (1408, 704, 512, 384, 256, 128)):
            need = 2 * (tm * tk * isz(a) + tk * tn * isz(b) + tm * tn * osz + (tm * tn * 4 if add is not None else 0))
            need += tm * tn * 4 if nk > 1 else 0
            if m % tm == 0 and need <= MM_VMEM_BUDGET:
                break
        else:
            tm = m
    dims = (((0 if ta else 1,), (1 if tb else 0,)), ((), ()))

    def body(*refs):
        if add is None:
            a_ref, b_ref, o_ref, acc_ref = refs
            add_ref = None
        else:
            a_ref, b_ref, add_ref, o_ref, acc_ref = refs
        def part():
            return lax.dot_general(a_ref[...].astype(BF16), b_ref[...].astype(BF16), dims,
                                   preferred_element_type=F32)

        def finish(total):
            if add_ref is not None:
                total = total + add_ref[...]
            o_ref[...] = total.astype(out_dtype)

        if nk == 1:
            finish(part())
        else:
            kk = pl.program_id(2)

            @pl.when(kk == 0)
            def _():
                acc_ref[...] = part()

            @pl.when(jnp.logical_and(kk > 0, kk < nk - 1))
            def _():
                acc_ref[...] += part()

            @pl.when(kk == nk - 1)
            def _():
                finish(acc_ref[...] + part())

    a_spec = pl.BlockSpec((tk, tm), lambda i, j, kk: (kk, i)) if ta else pl.BlockSpec((tm, tk), lambda i, j, kk: (i, kk))
    if b_chip and tb:
        b_spec = pl.BlockSpec((None, tn, tk), lambda i, j, kk: (chip_of(kk), j, 0))
    elif b_chip:
        b_spec = pl.BlockSpec((None, tk, tn), lambda i, j, kk: (j, kk, 0))
    elif tb:
        b_spec = pl.BlockSpec((tn, tk), lambda i, j, kk: (j, kk))
    else:
        b_spec = pl.BlockSpec((tk, tn), lambda i, j, kk: (kk, j))
    o_spec = pl.BlockSpec((tm, tn), lambda i, j, kk: (i, j))
    in_specs = [a_spec, b_spec] + ([o_spec] if add is not None else [])
    args = [a, b] + ([add] if add is not None else [])
    out_spec = pl.BlockSpec((None, tm, tn), lambda i, j, kk: (chip_of(j), i, 0)) if out_chip else o_spec
    return pl.pallas_call(
        body, name=name, interpret=False,
        out_shape=jax.ShapeDtypeStruct((4, m, tn) if out_chip else (m, n), out_dtype),
        grid=(m // tm, n // tn, nk), in_specs=in_specs, out_specs=out_spec,
        scratch_shapes=[pltpu.VMEM((tm, tn) if nk > 1 else (8, 128), F32)],
        compiler_params=_params(("parallel", "parallel", "arbitrary")),
    )(*args)


def _chip_order(swap_mid):
    return (lambda k: (k % 2) * 2 + k // 2) if swap_mid else (lambda k: k)


def _mm_epi(a, b, *, name, tb, tn, b_chip, epi, epi_ins, epi_consts, epi_outs, epi_accs, swap_mid=False):
    chip_of = _chip_order(swap_mid)
    m, k = a.shape
    if b_chip:
        n = b.shape[1] if tb else 4 * b.shape[2]
        tk = b.shape[2] if tb else None
        tn = tn if tb else b.shape[2]
    else:
        n = b.shape[0] if tb else b.shape[1]
        tk = None
    tn = tn or _pick(n, (1408, 1024, 768, 512, 256, 128))
    tk = tk or _pick(k, (1024, 1408, 1280, 768, 512, 128))
    nk, nj = k // tk, n // tn
    isz = lambda t: jnp.dtype(t.dtype if hasattr(t, "dtype") else t).itemsize
    outs3 = [t if isinstance(t, tuple) else (t, n, lambda j: j) for t in epi_outs]
    side = sum(isz(t) for t, _ in epi_ins) + sum(isz(dt) for dt, _, _ in outs3)
    for tm in (1408, 704, 512, 384, 256, 128):
        need = 2 * (tm * tk * isz(a) + tk * tn * isz(b) + tm * tn * side) + (tm * tn * 4 if nk > 1 else 0)
        if m % tm == 0 and need <= MM_VMEM_BUDGET:
            break
    else:
        tm = m
    dims = (((1,), (1 if tb else 0,)), ((), ()))
    n_in, n_c, n_out, n_acc = len(epi_ins), len(epi_consts), len(epi_outs), len(epi_accs)
    ni = m // tm
    lag = nk > 1 and nj == 1
    ti = (lambda i: jnp.maximum(i - 1, 0)) if lag else (lambda i: i)
    pi = (lambda i: jnp.minimum(i, ni - 1)) if lag else (lambda i: i)
    pk = (lambda i, kk: jnp.where(i == ni, nk - 1, kk)) if lag else (lambda i, kk: kk)

    def body(*refs):
        a_ref, b_ref = refs[:2]
        in_refs = refs[2:2 + n_in + n_c]
        out_refs = refs[2 + n_in + n_c:2 + n_in + n_c + n_out]
        acc_out = refs[2 + n_in + n_c + n_out:2 + n_in + n_c + n_out + n_acc]
        accs = refs[2 + n_in + n_c + n_out + n_acc:]
        acc_ref = accs[0]
        i, j, kk = pl.program_id(0), pl.program_id(1), pl.program_id(2)
        both = jnp.logical_and

        def part():
            return lax.dot_general(a_ref[...].astype(BF16), b_ref[...].astype(BF16), dims,
                                   preferred_element_type=F32)

        def finish(total):
            res = epi(ti(i) * tm, total, *[r[...] for r in in_refs])
            if not isinstance(res, (tuple, list)):
                res = (res,)
            for r, v in zip(out_refs, res[:n_out]):
                r[...] = v.astype(r.dtype)
            if n_acc:
                @pl.when(both(i == (1 if lag else 0), j == 0))
                def _():
                    for r in acc_out:
                        r[...] = jnp.zeros(r.shape, r.dtype)

                for r, v in zip(acc_out, res[n_out:]):
                    r[...] += jnp.broadcast_to(v, r.shape).astype(r.dtype)

        if nk == 1:
            finish(part())
        elif lag:
            @pl.when(both(kk == 0, i == 0))
            def _():
                accs[0][...] = part()

            for par in range(2):
                mine, other = accs[par], accs[1 - par]
                here = i % 2 == par

                @pl.when(both(both(kk == 0, here), both(i > 0, i < ni)))
                def _():
                    mine[...] = part()
                    finish(other[...])

                @pl.when(both(both(kk == 0, here), i == ni))
                def _():
                    finish(other[...])

                @pl.when(both(both(kk > 0, here), i < ni))
                def _():
                    mine[...] += part()
        else:
            @pl.when(kk == 0)
            def _():
                acc_ref[...] = part()

            @pl.when(both(kk > 0, kk < nk - 1))
            def _():
                acc_ref[...] += part()

            @pl.when(kk == nk - 1)
            def _():
                finish(acc_ref[...] + part())

    a_spec = pl.BlockSpec((tm, tk), lambda i, j, kk: (pi(i), pk(i, kk)))
    if b_chip and tb:
        b_spec = pl.BlockSpec((None, tn, tk), lambda i, j, kk: (chip_of(pk(i, kk)), j, 0))
    elif b_chip:
        b_spec = pl.BlockSpec((None, tk, tn), lambda i, j, kk: (j, pk(i, kk), 0))
    elif tb:
        b_spec = pl.BlockSpec((tn, tk), lambda i, j, kk: (j, pk(i, kk)))
    else:
        b_spec = pl.BlockSpec((tk, tn), lambda i, j, kk: (pk(i, kk), j))
    in_specs = [a_spec, b_spec]

    def in_spec(t, col):
        front = m - t.shape[0]
        if not front:
            return pl.BlockSpec((tm, tn), lambda i, j, kk: (ti(i), col(j)))
        return pl.BlockSpec((pl.Element(tm), pl.Element(tn)),
                            lambda i, j, kk: (pl.multiple_of(jnp.maximum(ti(i) * tm - front, 0), 8), col(j) * tn))

    in_specs += [in_spec(t, col) for t, col in epi_ins]
    in_specs += [pl.BlockSpec(t.shape, lambda i, j, kk, nd=t.ndim: (0,) * nd) for t in epi_consts]
    out_specs = [pl.BlockSpec((tm, tn), lambda i, j, kk, col=col: (ti(i), col(j))) for _, _, col in outs3]
    out_specs += [pl.BlockSpec(s, lambda i, j, kk, nd=len(s): (0,) * nd) for s, _ in epi_accs]
    out_shape = [jax.ShapeDtypeStruct((m, width), dt) for dt, width, _ in outs3]
    out_shape += [jax.ShapeDtypeStruct(s, dt) for s, dt in epi_accs]
    sem = ("arbitrary", "arbitrary", "arbitrary") if n_acc or lag else ("parallel", "parallel", "arbitrary")
    return pl.pallas_call(
        body, name=name, interpret=False, out_shape=out_shape,
        grid=(ni + 1 if lag else ni, nj, nk), in_specs=in_specs, out_specs=out_specs,
        scratch_shapes=[pltpu.VMEM((tm, tn) if nk > 1 else (8, 128), F32)] * (2 if lag else 1),
        compiler_params=_params(sem),
    )(a, b, *[t for t, _ in epi_ins], *epi_consts)


def cols(arr, tr, width=None, cb=0):
    width = width or arr.shape[1]
    return (arr, (tr, width), lambda i: (i, cb), "r2")


def heads(arr, tr):
    return (arr, (arr.shape[0], tr, arr.shape[2]), lambda i: (0, i, 0), "r3")


def whole(arr):
    nd = arr.ndim
    return (arr, arr.shape, lambda i: (0,) * nd, "w")


STRIP = 16


def _rows_of(ref, kind, r0, n):
    if kind == "r2":
        return ref[pl.ds(r0, n), :]
    if kind == "r3":
        return ref[:, pl.ds(r0, n), :]
    return ref[...]


def _set_rows(ref, kind, r0, n, v):
    if kind == "r2":
        ref[pl.ds(r0, n), :] = v.astype(ref.dtype)
    elif kind == "r3":
        ref[:, pl.ds(r0, n), :] = v.astype(ref.dtype)
    else:
        ref[...] = v.astype(ref.dtype)


def rowwise(fn, ins, outs, *, steps, name, accs=(), strip=None):
    n_in, n_out, n_acc = len(ins), len(outs), len(accs)
    kin = [t[3] for t in ins]
    kout = [t[4] for t in outs]
    tr = next((t[1][-2] for t in ins if t[3] != "w"), 0)

    def body(*refs):
        i = pl.program_id(0)
        in_refs, out_refs, acc_refs = refs[:n_in], refs[n_in:n_in + n_out], refs[n_in + n_out:]
        if n_acc:
            @pl.when(i == 0)
            def _():
                for r in acc_refs:
                    r[...] = jnp.zeros(r.shape, r.dtype)

        def run(r0, n):
            res = fn(i * tr + r0, *[_rows_of(r, k, r0, n) for r, k in zip(in_refs, kin)])
            if not isinstance(res, (tuple, list)):
                res = (res,)
            for r, k, v in zip(out_refs, kout, res[:n_out]):
                _set_rows(r, k, r0, n, v)
            for r, v in zip(acc_refs, res[n_out:]):
                r[...] += jnp.broadcast_to(v, r.shape).astype(r.dtype)

        if strip is None or tr <= strip:
            run(0, tr)
        else:
            def step(s, carry):
                run(pl.multiple_of(s * strip, strip), strip)
                return carry
            lax.fori_loop(0, tr // strip, step, 0)

    def zmap(nd):
        return lambda i: (0,) * nd

    in_specs = [pl.BlockSpec(t[1], t[2]) for t in ins]
    out_specs = [pl.BlockSpec(t[2], t[3]) for t in outs]
    out_specs += [pl.BlockSpec(s, zmap(len(s))) for s, _ in accs]
    out_shape = [jax.ShapeDtypeStruct(t[0], t[1]) for t in outs]
    out_shape += [jax.ShapeDtypeStruct(s, d) for s, d in accs]
    res = pl.pallas_call(
        body, name=name, interpret=False, out_shape=out_shape, grid=(steps,),
        in_specs=in_specs, out_specs=out_specs,
        compiler_params=_params(("arbitrary",)),
    )(*[t[0] for t in ins])
    return res


def out2d(rows, width, dtype, tr):
    return ((rows, width), dtype, (tr, width), lambda i: (i, 0), "r2")


def conv_fwd(xs, w8, kw, *, rows, c, tc, tr, name, post, extras=(), outs=(), pre=None, strip=STRIP):
    nx, ne, no = len(xs), len(extras), len(outs)
    nr, nc = rows // tr, c // tc
    r8 = tr // 8
    st = strip

    def body(*refs):
        x_refs = refs[:2 * nx]
        w_ref = refs[2 * nx]
        e_refs = refs[2 * nx + 1:2 * nx + 1 + ne]
        o_refs = refs[2 * nx + 1 + ne:2 * nx + 1 + ne + no]
        scr = refs[-1]
        j, i = pl.program_id(0), pl.program_id(1)
        halo = [x_refs[2 * q + 1][...].astype(F32) for q in range(nx)]
        scr[0:8, :] = jnp.where(i > 0, pre(*halo) if pre else halo[0], 0.0)

        def fill(s, carry):
            r0 = pl.multiple_of(s * st, st)
            cur = [x_refs[2 * q][pl.ds(r0, st), :].astype(F32) for q in range(nx)]
            scr[pl.ds(8 + r0, st), :] = pre(*cur) if pre else cur[0]
            return carry

        def comp(s, carry):
            r0 = pl.multiple_of(s * st, st)
            win = scr[pl.ds(r0, st + 8), :]
            y = jnp.zeros((st, tc), F32)
            for q in range(kw):
                sh = kw - 1 - q
                y = y + w_ref[q:q + 1, :] * win[8 - sh:8 - sh + st]
            res = post(j, y, *[e[pl.ds(r0, st), :] for e in e_refs])
            if not isinstance(res, (tuple, list)):
                res = (res,)
            for r, v in zip(o_refs, res):
                r[pl.ds(r0, st), :] = v.astype(r.dtype)
            return carry

        lax.fori_loop(0, tr // st, fill, 0)
        lax.fori_loop(0, tr // st, comp, 0)

    in_specs, args = [], []
    for arr, cb0 in xs:
        in_specs.append(pl.BlockSpec((tr, tc), lambda j, i, cb0=cb0: (i, cb0 + j)))
        in_specs.append(pl.BlockSpec((8, tc), lambda j, i, cb0=cb0: (jnp.maximum(i * r8 - 1, 0), cb0 + j)))
        args += [arr, arr]
    in_specs.append(pl.BlockSpec((8, tc), lambda j, i: (0, j)))
    args.append(w8)
    for arr, cb0 in extras:
        in_specs.append(pl.BlockSpec((tr, tc), lambda j, i, cb0=cb0: (i, cb0 + j)))
        args.append(arr)
    return pl.pallas_call(
        body, name=name, interpret=False,
        out_shape=[jax.ShapeDtypeStruct((rows, c), dt) for dt in outs],
        grid=(nc, nr), in_specs=in_specs,
        out_specs=[pl.BlockSpec((tr, tc), lambda j, i: (i, j)) for _ in outs],
        scratch_shapes=[pltpu.VMEM((tr + 8, tc), F32)],
        compiler_params=_params(("parallel", "arbitrary")),
    )(*args)


def conv_bwd(xs, w8, kw, dy, *, rows, c, tc, tr, name, post, extras=(), outs=(), pre=None):
    nx, ne, no = len(xs), len(extras), len(outs)
    nr, nc = rows // tr, c // tc
    r8 = tr // 8

    def body(*refs):
        x_refs = refs[:nx]
        w_ref, dy_ref, dyn_ref = refs[nx:nx + 3]
        e_refs = refs[nx + 3:nx + 3 + ne]
        first_out = nx + 3 + ne
        o_refs = refs[first_out:first_out + no]
        dw_ref = refs[first_out + no]
        gscr = refs[-1]
        i = pl.program_id(1)
        gscr[tr:tr + 8, :] = jnp.where(i < nr - 1, dyn_ref[...].astype(F32), 0.0)

        def fill(s, carry):
            r0 = pl.multiple_of(s * STRIP, STRIP)
            gscr[pl.ds(r0, STRIP), :] = dy_ref[pl.ds(r0, STRIP), :].astype(F32)
            return carry

        def comp(s, dws):
            r0 = pl.multiple_of(s * STRIP, STRIP)
            gwin = gscr[pl.ds(r0, STRIP + 8), :]
            cur = [x_refs[q][pl.ds(r0, STRIP), :].astype(F32) for q in range(nx)]
            x = pre(*cur) if pre else cur[0]
            dx = jnp.zeros((STRIP, tc), F32)
            new = []
            for q in range(kw):
                sh = kw - 1 - q
                ahead = gwin[sh:sh + STRIP]
                dx = dx + w_ref[q:q + 1, :] * ahead
                part = ahead * x
                new.append(dws[q] + part[0:8] + part[8:16])
            res = post(dx, *[e[pl.ds(r0, STRIP), :] for e in e_refs])
            if not isinstance(res, (tuple, list)):
                res = (res,)
            for r, v in zip(o_refs, res):
                r[pl.ds(r0, STRIP), :] = v.astype(r.dtype)
            return tuple(new)

        lax.fori_loop(0, tr // STRIP, fill, 0)
        dws = lax.fori_loop(0, tr // STRIP, comp, tuple(jnp.zeros((8, tc), F32) for _ in range(kw)))

        @pl.when(i == 0)
        def _():
            dw_ref[...] = jnp.zeros((8, tc), F32)

        dw_ref[...] += jnp.concatenate([jnp.sum(t, axis=0, keepdims=True) for t in dws]
                                       + [jnp.zeros((8 - kw, tc), F32)], axis=0)

    in_specs, args = [], []
    for arr, cb0 in xs:
        in_specs.append(pl.BlockSpec((tr, tc), lambda j, i, cb0=cb0: (i, cb0 + j)))
        args.append(arr)
    in_specs.append(pl.BlockSpec((8, tc), lambda j, i: (0, j)))
    in_specs.append(pl.BlockSpec((tr, tc), lambda j, i: (i, j)))
    in_specs.append(pl.BlockSpec((8, tc), lambda j, i: (jnp.minimum((i + 1) * r8, nr * r8 - 1), j)))
    args += [w8, dy, dy]
    for arr, cb0 in extras:
        in_specs.append(pl.BlockSpec((tr, tc), lambda j, i, cb0=cb0: (i, cb0 + j)))
        args.append(arr)
    return pl.pallas_call(
        body, name=name, interpret=False,
        out_shape=[jax.ShapeDtypeStruct((rows, c), dt) for dt in outs] + [jax.ShapeDtypeStruct((8, c), F32)],
        grid=(nc, nr), in_specs=in_specs,
        out_specs=[pl.BlockSpec((tr, tc), lambda j, i: (i, j)) for _ in outs] + [pl.BlockSpec((8, tc), lambda j, i: (0, j))],
        scratch_shapes=[pltpu.VMEM((tr + 8, tc), F32)],
        compiler_params=_params(("parallel", "arbitrary")),
    )(*args)


def rms_fwd(h, w, *, name):
    rows = h.shape[0]
    tr = _pick(rows, (384, 128))

    def fn(i, x, wv):
        r = lax.rsqrt(jnp.mean(x * x, axis=1, keepdims=True) + EPS)
        return x * r * wv

    return rowwise(fn, [cols(h, tr), whole(w)], [out2d(rows, D, BF16, tr)], steps=rows // tr, name=name)[0]


def _rms_bwd_epi(row0, g, x, dr, wv):
    r = lax.rsqrt(jnp.mean(x * x, axis=1, keepdims=True) + EPS)
    xh = x * r
    gw = g * wv
    dx = r * (gw - xh * jnp.mean(gw * xh, axis=1, keepdims=True))
    row = row0 + lax.broadcasted_iota(jnp.int32, (x.shape[0], 1), 0)
    return jnp.where(row >= PAD, dr + dx, 0.0), jnp.sum(g * xh, axis=0, keepdims=True)


def dx_rms_bwd(dy, w, h, nw, dres, *, name, b_chip=False, swap_mid=False):
    return mm(dy, w, tb=True, b_chip=b_chip, swap_mid=swap_mid, tn=D, name=name, epi=_rms_bwd_epi,
              epi_ins=[(h, lambda j: 0), (dres, lambda j: 0)], epi_consts=[nw], epi_outs=[F32],
              epi_accs=[((1, D), F32)])


def _add_loss_epi(row0, t, h, tgt):
    row = row0 + lax.broadcasted_iota(jnp.int32, (t.shape[0], 1), 0)
    tgt = jnp.where(row0 == 0, jnp.concatenate([tgt[-HEAD0:], tgt[:-HEAD0]], axis=0), tgt)
    diff = jnp.where(row >= HEAD0, t + h - tgt, 0.0)
    part = jnp.sum(jnp.sum(diff * diff, axis=1, keepdims=True), axis=0, keepdims=True)
    return diff * (1.0 / D), part * (0.5 / D)


def add_loss(a, w, h, target, *, name):
    return mm(a, w, name=name, epi=_add_loss_epi, epi_ins=[(h, lambda j: 0), (target, lambda j: 0)],
              epi_outs=[F32], epi_accs=[((1, 128), F32)])


def adamw(w, g, m, v, *, name):
    shape = w.shape
    gs = list(g) if isinstance(g, (list, tuple)) else [g]
    nl = len(gs)
    width = shape[-1]
    rows = w.size // width
    rl = rows // nl
    tr = _pick(rl, (256, 176, 128, 64, 16, 8))
    nr = rl // tr
    if w.ndim == 3 and shape[1] % tr == 0:
        per = shape[1] // tr
        view = lambda t: (t, (None, tr, width), lambda i: (i // per, i % per, 0), "r2")
        out = (shape, F32, (None, tr, width), lambda i: (i // per, i % per, 0), "r2")
    else:
        view = lambda t: cols(t.reshape(rows, width), tr)
        out = out2d(rows, width, F32, tr)

    def fn(i, wv, mv, vv, *gvs):
        gv = gvs[0]
        for layer in range(1, nl):
            gv = jnp.where(i >= layer * rl, gvs[layer], gv)
        mn = B1 * mv + (1.0 - B1) * gv
        vn = B2 * vv + (1.0 - B2) * gv * gv
        mh = mn / (1.0 - B1 ** STEP)
        vh = vn / (1.0 - B2 ** STEP)
        return -LR * (mh / (jnp.sqrt(vh) + AEPS) + WD * wv), mn, vn, gv

    g_ins = [(t.reshape(rl, width), (tr, width), lambda i, layer=layer: (jnp.clip(i - layer * nr, 0, nr - 1), 0), "r2")
             for layer, t in enumerate(gs)]
    res = rowwise(fn, [view(t) for t in (w, m, v)] + g_ins, [out] * 4, steps=rows // tr, name=name)
    return [r.reshape(shape) for r in res]


HB = DN_H * CH
PAIR = 3
PAIR_BWD = 6


def _split(a):
    hi = a.astype(BF16)
    return hi, (a - hi.astype(F32)).astype(BF16)


def _dot1(a, b, ca=1, cb=0):
    return _dot(a.astype(BF16), b.astype(BF16), ca, cb)


def _dot3(a, b, ca=1, cb=0):
    ah, al = _split(a)
    bh, bl = _split(b)
    return _dot(ah, bh, ca, cb) + (_dot(ah, bl, ca, cb) + _dot(al, bh, ca, cb))


def _dot01(m01, b, ca=1, cb=0):
    bh, bl = _split(b)
    m = m01.astype(BF16)
    return _dot(m, bh, ca, cb) + _dot(m, bl, ca, cb)


def _stack(x):
    return jnp.concatenate([x[:, h * DN_D:(h + 1) * DN_D] for h in range(DN_H)], axis=0)


def _unstack(x):
    return jnp.concatenate([x[h * CH:(h + 1) * CH] for h in range(DN_H)], axis=1)


def _tri_inv(mats, blk, eye):
    each = lambda f, *lists: [f(*t) for t in zip(*lists)]
    ad = [jnp.where(blk, a, 0.0) for a in mats]
    lo = each(lambda a, d: a - d, mats, ad)
    a2 = each(_dot3, ad, ad)
    a4 = each(_dot3, a2, a2)
    a8 = each(_dot3, a4, a4)
    dgi = each(lambda d, s: _dot3(eye - d, eye + s), ad, a2)
    dgi = each(lambda p, s: _dot3(p, eye + s), dgi, a4)
    dgi = each(lambda p, s: _dot3(p, eye + s), dgi, a8)
    n = each(_dot3, dgi, lo)
    n2 = each(_dot3, n, n)
    return each(_dot3, each(lambda u, v: _dot3(eye - u, eye + v), n, n2), dgi)


def _dn_masks():
    row = lax.broadcasted_iota(jnp.int32, (HB, HB), 0)
    col = lax.broadcasted_iota(jnp.int32, (HB, HB), 1)
    same = (row // CH) == (col // CH)
    incl = jnp.logical_and(same, row >= col)
    strict = jnp.logical_and(same, row > col)
    upper = jnp.logical_and(same, row <= col)
    blk = (row // 16) == (col // 16)
    eye = (row == col).astype(F32)
    return incl, strict, upper, blk, eye


def _dn_chunk(qv, kv, vv, bc, br, incl, strict):
    r64 = lax.broadcasted_iota(jnp.int32, (CH, CH), 0)
    c64 = lax.broadcasted_iota(jnp.int32, (CH, CH), 1)
    dcol = _dot01((r64 >= c64).astype(F32), bc)
    drow = _dot3(br, (r64 <= c64).astype(F32))
    col = lambda m, l0: jnp.concatenate([m[:, l0 + h:l0 + h + 1] for h in range(DN_H)], axis=0)
    b_c = col(bc, 0)
    d_c = col(dcol, 4)
    d_r = jnp.concatenate([drow[4 + h:5 + h, :] for h in range(DN_H)], axis=1)
    d_last_h = [dcol[CH - 1:CH, 4 + h:5 + h] for h in range(DN_H)]
    d_last = jnp.concatenate([jnp.broadcast_to(t, (CH, 1)) for t in d_last_h], axis=0)
    q, k, v = _stack(qv), _stack(kv), _stack(vv)
    dm = jnp.where(incl, jnp.exp(jnp.where(incl, d_c - d_r, 0.0)), 0.0)
    kk = _dot1(k, k, 1, 1)
    a = jnp.where(strict, b_c * kk * dm, 0.0)
    ed = jnp.exp(d_c)
    rhs = jnp.concatenate([v * b_c, k * (b_c * ed)], axis=1)
    qk = _dot1(q, k, 1, 1) * dm
    ekd = jnp.exp(d_last - d_c)
    gl = [jnp.exp(t) for t in d_last_h]
    return q, k, v, b_c, dm, kk, a, ed, rhs, qk, ekd, gl


def dn_fwd(qkv_n, bgcol, bgrow):
    rows = qkv_n.shape[0]
    nch = rows // CH

    def body(q_ref, k_ref, v_ref, bc_ref, br_ref, o_ref, s_out, ti_out, s_scr, prep, prep_qk, prep_gl):
        n = pl.program_id(0)

        @pl.when(n == 0)
        def _():
            s_scr[...] = jnp.zeros(s_scr.shape, F32)
            prep[...] = jnp.zeros(prep.shape, F32)
            prep_qk[...] = jnp.zeros(prep_qk.shape, F32)
            prep_gl[...] = jnp.zeros(prep_gl.shape, F32)

        live = n > 0
        rows_of = [slice(h * CH, (h + 1) * CH) for h in range(DN_H)]
        s = [s_scr[h] for h in range(DN_H)]
        for c in range(PAIR):
            u, w, qd, kd = prep[c, 0], prep[c, 1], prep[c, 2], prep[c, 3]
            for h in range(DN_H):
                s_out[c, h] = s[h]
            v_new = [u[rs] - _dot1(w[rs], s[h]) for h, rs in enumerate(rows_of)]
            o_state = [_dot1(qd[rs], s[h]) for h, rs in enumerate(rows_of)]
            s = [jnp.where(live, prep_gl[c, h:h + 1, 0:1] * s[h] + _dot1(kd[rs], v_new[h], 0, 0), s[h])
                 for h, rs in enumerate(rows_of)]
            o = jnp.concatenate(o_state, axis=0) + _dot1(prep_qk[c], jnp.concatenate(v_new, axis=0))
            o_ref[c * CH:(c + 1) * CH, :] = _unstack(o)
        for h in range(DN_H):
            s_scr[h] = s[h]

        incl, strict, _, blk, eye = _dn_masks()
        parts = []
        for c in range(PAIR):
            rows_c = slice(c * CH, (c + 1) * CH)
            parts.append(_dn_chunk(q_ref[rows_c, :], k_ref[rows_c, :], v_ref[rows_c, :], bc_ref[rows_c, :],
                                   br_ref[c], incl, strict))
        tinvs = _tri_inv([p[6] for p in parts], blk, eye)
        for c, (q, k, v, b_c, dm, kk, a, ed, rhs, qk_n, ekd, gl) in enumerate(parts):
            tinv = tinvs[c]
            ti_out[c] = tinv
            sol = _dot3(tinv, rhs)
            prep[c, 0] = sol[:, :DN_D]
            prep[c, 1] = sol[:, DN_D:]
            prep[c, 2] = q * ed
            prep[c, 3] = k * ekd
            prep_qk[c] = qk_n
            prep_gl[c] = jnp.concatenate([jnp.broadcast_to(t, (1, 128)) for t in gl]
                                         + [jnp.zeros((8 - DN_H, 128), F32)], axis=0)

    assert nch % PAIR == 0
    npair = nch // PAIR
    last = npair - 1
    return pl.pallas_call(
        body, name="dn_fwd", interpret=False,
        out_shape=[jax.ShapeDtypeStruct((rows, DN_DIM), F32),
                   jax.ShapeDtypeStruct((nch, DN_H, DN_D, DN_D), F32),
                   jax.ShapeDtypeStruct((nch, HB, HB), F32)],
        grid=(npair + 1,),
        in_specs=[pl.BlockSpec((PAIR * CH, DN_DIM), lambda n: (jnp.minimum(n, last), 0)),
                  pl.BlockSpec((PAIR * CH, DN_DIM), lambda n: (jnp.minimum(n, last), 1)),
                  pl.BlockSpec((PAIR * CH, DN_DIM), lambda n: (jnp.minimum(n, last), 2)),
                  pl.BlockSpec((PAIR * CH, 128), lambda n: (jnp.minimum(n, last), 0)),
                  pl.BlockSpec((PAIR, 8, CH), lambda n: (jnp.minimum(n, last), 0, 0))],
        out_specs=[pl.BlockSpec((PAIR * CH, DN_DIM), lambda n: (jnp.maximum(n - 1, 0), 0)),
                   pl.BlockSpec((PAIR, DN_H, DN_D, DN_D), lambda n: (jnp.maximum(n - 1, 0), 0, 0, 0)),
                   pl.BlockSpec((PAIR, HB, HB), lambda n: (jnp.minimum(n, last), 0, 0))],
        scratch_shapes=[pltpu.VMEM((DN_H, DN_D, DN_D), F32), pltpu.VMEM((PAIR, 4, HB, DN_D), F32),
                        pltpu.VMEM((PAIR, HB, HB), F32), pltpu.VMEM((PAIR, 8, 128), F32)],
        compiler_params=_params(("arbitrary",)),
    )(qkv_n, qkv_n, qkv_n, bgcol, bgrow)


def dn_bwd(qkv_n, bgcol, bgrow, s_all, ti_all, do):
    rows = qkv_n.shape[0]
    nch = rows // CH

    def body(q_ref, k_ref, v_ref, bc_ref, br_ref, s_ref, ti_ref, do_ref, dq_ref, dk_ref, dv_ref, dbg_ref, ds_scr):
        n = pl.program_id(0)

        @pl.when(n == 0)
        def _():
            ds_scr[...] = jnp.zeros(ds_scr.shape, F32)

        incl, strict, upper, _, _ = _dn_masks()
        rsum = lambda t: jnp.sum(t, axis=1, keepdims=True)
        rows_of = [slice(h * CH, (h + 1) * CH) for h in range(DN_H)]
        heads_of = lambda f: jnp.concatenate([f(h, rs) for h, rs in enumerate(rows_of)], axis=0)
        cs = []
        for c in reversed(range(PAIR_BWD)):
            rc = slice(c * CH, (c + 1) * CH)
            q, k, v, b_c, dm, kk, a, ed, rhs, qk, ekd, gl = _dn_chunk(
                q_ref[rc, :], k_ref[rc, :], v_ref[rc, :], bc_ref[rc, :], br_ref[c], incl, strict)
            cs.append(dict(rc=rc, q=q, k=k, v=v, b_c=b_c, dm=dm, kk=kk, a=a, ed=ed, rhs=rhs, qk=qk, ekd=ekd, gl=gl,
                           tinv=ti_ref[c], g=_stack(do_ref[rc, :]), s=[s_ref[c, h] for h in range(DN_H)]))
        for t in cs:
            t["sol"] = _dot3(t["tinv"], t["rhs"])
        for t in cs:
            t["u"], t["w"] = t["sol"][:, :DN_D], t["sol"][:, DN_D:]
            t["qd"], t["kd"] = t["q"] * t["ed"], t["k"] * t["ekd"]
            t["v_new"] = heads_of(lambda h, rs: t["u"][rs] - _dot1(t["w"][rs], t["s"][h]))
            t["dv0"] = _dot1(t["qk"], t["g"], 0, 0)
            t["ds0"] = [_dot1(t["qd"][rs], t["g"][rs], 0, 0) for rs in rows_of]
            t["dqd"] = heads_of(lambda h, rs: _dot1(t["g"][rs], t["s"][h], 1, 1))
        for t in cs:
            t["dqk"] = _dot1(t["g"], t["v_new"], 1, 1)
        ds = [ds_scr[h] for h in range(DN_H)]
        for t in cs:
            t["ds"] = ds
            t["dv_new"] = t["dv0"] + heads_of(lambda h, rs: _dot1(t["kd"][rs], ds[h]))
            ds = [t["ds0"][h] + t["gl"][h] * ds[h] - _dot1(t["w"][rs], t["dv_new"][rs], 0, 0)
                  for h, rs in enumerate(rows_of)]
        for h in range(DN_H):
            ds_scr[h] = ds[h]
        for t in cs:
            t["dkd"] = heads_of(lambda h, rs: _dot1(t["v_new"][rs], t["ds"][h], 1, 1))
            dw = heads_of(lambda h, rs: -_dot1(t["dv_new"][rs], t["s"][h], 1, 1))
            t["dsol"] = jnp.concatenate([t["dv_new"], dw], axis=1)
        for t in cs:
            t["drhs"] = _dot3(t["tinv"], t["dsol"], 0, 0)
        for t in cs:
            t["da"] = jnp.where(strict, -_dot1(t["drhs"], t["sol"], 1, 1), 0.0)
        rowi = lax.broadcasted_iota(jnp.int32, (CH, 1), 0)
        lane = lax.broadcasted_iota(jnp.int32, (CH, 128), 1)
        for t in cs:
            q, k, v, b_c, dm, ed, da, dqk = t["q"], t["k"], t["v"], t["b_c"], t["dm"], t["ed"], t["da"], t["dqk"]
            drhs_u, drhs_w = t["drhs"][:, :DN_D], t["drhs"][:, DN_D:]
            s2 = rsum(drhs_w * k)
            dbeta = rsum(drhs_u * v) + s2 * ed + rsum(da * t["kk"] * dm)
            dkk = da * b_c * dm
            dqkr = dqk * dm
            mmat = da * t["a"] + dqk * t["qk"]
            tmp = rsum(t["dkd"] * t["kd"])
            dd = (s2 * b_c * ed + rsum(mmat) - _dot3(mmat, jnp.ones((HB, 128), F32), 0, 0)[:, :1]
                  + rsum(t["dqd"] * t["qd"]) - tmp)
            last = []
            for h, rs in enumerate(rows_of):
                dgl = jnp.sum(rsum(t["s"][h] * t["ds"][h]), axis=0, keepdims=True)
                dd_last = jnp.sum(tmp[rs], axis=0, keepdims=True) + dgl * t["gl"][h]
                last.append(jnp.where(rowi == CH - 1, dd_last, 0.0))
            dd = dd + jnp.concatenate(last, axis=0)
            rc = t["rc"]
            dq_ref[rc, :] = _unstack(_dot1(dqkr, k) + t["dqd"] * ed)
            dk_ref[rc, :] = _unstack(drhs_w * (b_c * ed) + _dot1(dkk, k) + _dot1(dkk, k, 0, 0) + _dot1(dqkr, q, 0, 0)
                                     + t["dkd"] * t["ekd"])
            dv_ref[rc, :] = _unstack(drhs_u * b_c)
            dg = _dot01(upper.astype(F32), jnp.broadcast_to(dd, (HB, 128)))[:, :1]
            out = jnp.zeros((CH, 128), F32)
            for h, rs in enumerate(rows_of):
                out = out + jnp.where(lane == h, dbeta[rs], 0.0) + jnp.where(lane == 4 + h, dg[rs], 0.0)
            dbg_ref[rc, :] = out

    assert nch % PAIR_BWD == 0
    npair = nch // PAIR_BWD
    rev = lambda n: npair - 1 - n
    blk = PAIR_BWD * CH
    return pl.pallas_call(
        body, name="dn_bwd", interpret=False,
        out_shape=[jax.ShapeDtypeStruct((rows, DN_DIM), F32)] * 3 + [jax.ShapeDtypeStruct((rows, 128), F32)],
        grid=(npair,),
        in_specs=[pl.BlockSpec((blk, DN_DIM), lambda n: (rev(n), 0)),
                  pl.BlockSpec((blk, DN_DIM), lambda n: (rev(n), 1)),
                  pl.BlockSpec((blk, DN_DIM), lambda n: (rev(n), 2)),
                  pl.BlockSpec((blk, 128), lambda n: (rev(n), 0)),
                  pl.BlockSpec((PAIR_BWD, 8, CH), lambda n: (rev(n), 0, 0)),
                  pl.BlockSpec((PAIR_BWD, DN_H, DN_D, DN_D), lambda n: (rev(n), 0, 0, 0)),
                  pl.BlockSpec((PAIR_BWD, HB, HB), lambda n: (rev(n), 0, 0)),
                  pl.BlockSpec((blk, DN_DIM), lambda n: (rev(n), 0))],
        out_specs=[pl.BlockSpec((blk, DN_DIM), lambda n: (rev(n), 0))] * 3 + [pl.BlockSpec((blk, 128), lambda n: (rev(n), 0))],
        scratch_shapes=[pltpu.VMEM((DN_H, DN_D, DN_D), F32)],
        compiler_params=_params(("arbitrary",)),
    )(qkv_n, qkv_n, qkv_n, bgcol, bgrow, s_all, ti_all, do)


def _swa_valid(n):
    c3 = lax.broadcasted_iota(jnp.int32, (NKEY, 4 * BLK), 0)
    r = lax.broadcasted_iota(jnp.int32, (NKEY, 4 * BLK), 1) % BLK
    prev0 = N_META + BLK
    c = jnp.where(c3 < N_META, PAD + c3, jnp.where(c3 < prev0, c3 - N_META, c3 - prev0))
    lo = jnp.where(c3 < N_META, 0, jnp.where(c3 < prev0, r + 1 + jnp.where(n >= 2, 0, BLK), 0))
    hi = jnp.where(c3 < N_META, r + jnp.where(n >= 1, BLK, 0),
                   jnp.where(c3 < prev0, BLK, r - jnp.where(n >= 1, 0, BLK)))
    return jnp.logical_and(c >= lo, c <= hi)


def _swa_probs(qs, kcats, valid, sinks):
    s = [jnp.where(valid, _dot(kc, q, 1, 1), -1e30) for q, kc in zip(qs, kcats)]
    m = [jnp.maximum(jnp.max(t, axis=0, keepdims=True), sk) for t, sk in zip(s, sinks)]
    e = [jnp.where(valid, jnp.exp(t - mx), 0.0) for t, mx in zip(s, m)]
    es = [jnp.exp(sk - mx) for sk, mx in zip(sinks, m)]
    inv = [1.0 / (jnp.sum(t, axis=0, keepdims=True) + u) for t, u in zip(e, es)]
    return [t * i for t, i in zip(e, inv)], [u * i for u, i in zip(es, inv)]


def _swa_group(q_ref, sk_ref, h):
    q4 = jnp.concatenate([q_ref[4 * h + g] for g in range(4)], axis=0)
    sink4 = jnp.concatenate([jnp.full((1, BLK), sk_ref[4 * h + g], F32) for g in range(4)], axis=1)
    return q4, sink4


def _swa_specs():
    q = pl.BlockSpec((SWA_H, BLK, SWA_D), lambda n: (0, n, 0))
    km = pl.BlockSpec((SWA_KV, N_META, SWA_D), lambda n: (0, PAD // N_META, 0))
    kp = pl.BlockSpec((SWA_KV, BLK, SWA_D), lambda n: (0, jnp.maximum(n - 1, 0), 0))
    kc = pl.BlockSpec((SWA_KV, BLK, SWA_D), lambda n: (0, n, 0))
    return [q, km, kp, kc, km, kp, kc]


def swa_fwd(qh, kh, vh, sinks):
    rows = qh.shape[1]
    nb = rows // BLK

    def body(q_ref, km, kp, kc, vm, vp, vc, sk_ref, o_ref):
        n = pl.program_id(0)
        valid = _swa_valid(n)
        kcats = [jnp.concatenate([km[h], kp[h], kc[h]], axis=0) for h in range(SWA_KV)]
        vcats = [jnp.concatenate([vm[h], vp[h], vc[h]], axis=0) for h in range(SWA_KV)]
        qs, sinks4 = zip(*[_swa_group(q_ref, sk_ref, h) for h in range(SWA_KV)])
        ps, _ = _swa_probs(qs, kcats, valid, sinks4)
        o4s = [_dot(p.astype(BF16), vc_, 0, 0) for p, vc_ in zip(ps, vcats)]
        o_ref[...] = jnp.concatenate([o4[g * BLK:(g + 1) * BLK] for o4 in o4s for g in range(4)],
                                     axis=1).astype(BF16)

    return pl.pallas_call(
        body, name="swa_fwd", interpret=False,
        out_shape=jax.ShapeDtypeStruct((rows, SWA_H * SWA_D), BF16),
        grid=(nb,),
        in_specs=_swa_specs() + [pl.BlockSpec(memory_space=pltpu.SMEM)],
        out_specs=pl.BlockSpec((BLK, SWA_H * SWA_D), lambda n: (n, 0)),
        compiler_params=_params(("parallel",)),
    )(qh, kh, kh, kh, vh, vh, vh, sinks)


def swa_bwd(qh, kh, vh, sinks, do):
    rows = qh.shape[1]
    nb = rows // BLK

    def body(q_ref, km, kp, kc, vm, vp, vc, do_ref, sk_ref, dq_ref, dk_ref, dv_ref, dsk_ref):
        n = pl.program_id(0)

        @pl.when(n == 0)
        def _():
            dk_ref[...] = jnp.zeros(dk_ref.shape, F32)
            dv_ref[...] = jnp.zeros(dv_ref.shape, F32)

        valid = _swa_valid(n)
        g_all = do_ref[...]
        rowi = lax.broadcasted_iota(jnp.int32, (SWA_H, 128), 0)
        dsk = jnp.zeros((SWA_H, 128), F32)
        pm = pl.multiple_of(jnp.maximum(n - 1, 0) * BLK, BLK)
        pc = pl.multiple_of(n * BLK, BLK)
        hs = range(SWA_KV)
        kcats = [jnp.concatenate([km[h], kp[h], kc[h]], axis=0) for h in hs]
        vcats = [jnp.concatenate([vm[h], vp[h], vc[h]], axis=0) for h in hs]
        qs, sinks4 = zip(*[_swa_group(q_ref, sk_ref, h) for h in hs])
        g4s = [jnp.concatenate([g_all[:, (4 * h + g) * SWA_D:(4 * h + g + 1) * SWA_D] for g in range(4)], axis=0)
               for h in hs]
        ps, pss = _swa_probs(qs, kcats, valid, sinks4)
        dps = [_dot(vc_, g4, 1, 1) for vc_, g4 in zip(vcats, g4s)]
        deltas = [jnp.sum(p * dp, axis=0, keepdims=True) for p, dp in zip(ps, dps)]
        dss = [(p * (dp - dl)).astype(BF16) for p, dp, dl in zip(ps, dps, deltas)]
        dq4s = [_dot(ds, kc_, 0, 0) for ds, kc_ in zip(dss, kcats)]
        dkcs = [_dot(ds, q4) for ds, q4 in zip(dss, qs)]
        dvcs = [_dot(p.astype(BF16), g4) for p, g4 in zip(ps, g4s)]
        for h in hs:
            t = pss[h] * deltas[h]
            for g in range(4):
                dq_ref[4 * h + g] = dq4s[h][g * BLK:(g + 1) * BLK]
                part = -jnp.sum(t[:, g * BLK:(g + 1) * BLK], axis=1, keepdims=True)
                dsk = dsk + jnp.where(rowi == 4 * h + g, part, 0.0)
            lanes = slice(h * SWA_D, (h + 1) * SWA_D)
            for ref, val in ((dk_ref, dkcs[h]), (dv_ref, dvcs[h])):
                ref[PAD:BLK, lanes] += val[0:N_META]
                ref[pl.ds(pm, BLK), lanes] += val[N_META:N_META + BLK]
                ref[pl.ds(pc, BLK), lanes] += val[N_META + BLK:]
        dsk_ref[0] = dsk

    return pl.pallas_call(
        body, name="swa_bwd", interpret=False,
        out_shape=[jax.ShapeDtypeStruct((SWA_H, rows, SWA_D), F32),
                   jax.ShapeDtypeStruct((rows, SWA_KV * SWA_D), F32),
                   jax.ShapeDtypeStruct((rows, SWA_KV * SWA_D), F32),
                   jax.ShapeDtypeStruct((nb, SWA_H, 128), F32)],
        grid=(nb,),
        in_specs=_swa_specs() + [pl.BlockSpec((BLK, SWA_H * SWA_D), lambda n: (n, 0)),
                                 pl.BlockSpec(memory_space=pltpu.SMEM)],
        out_specs=[pl.BlockSpec((SWA_H, BLK, SWA_D), lambda n: (0, n, 0)),
                   pl.BlockSpec((rows, SWA_KV * SWA_D), lambda n: (0, 0)),
                   pl.BlockSpec((rows, SWA_KV * SWA_D), lambda n: (0, 0)),
                   pl.BlockSpec((1, SWA_H, 128), lambda n: (n, 0, 0))],
        compiler_params=_params(("arbitrary",)),
    )(qh, kh, kh, kh, vh, vh, vh, do, sinks)


QK_W = (SWA_H + SWA_KV) * SWA_D


def _head_mean(t):
    r = lax.broadcasted_iota(jnp.int32, (128, 128), 0) // SWA_D
    c = lax.broadcasted_iota(jnp.int32, (128, 128), 1) // SWA_D
    blk = jnp.where(r == c, 1.0 / SWA_D, 0.0).astype(BF16)
    out = []
    for i in range(t.shape[1] // 128):
        hi, lo = _split(t[:, 128 * i:128 * (i + 1)])
        out.append(_dot(hi, blk) + _dot(lo, blk))
    return jnp.concatenate(out, axis=1)


def _qk_scales(qw, kw):
    scale = SWA_D ** -0.5
    wt = jnp.concatenate([jnp.tile(qw.astype(F32) * scale, (1, SWA_H)), jnp.tile(kw.astype(F32), (1, SWA_KV))], axis=1)
    st = jnp.concatenate([jnp.full((1, SWA_H * SWA_D), scale, F32), jnp.ones((1, SWA_KV * SWA_D), F32)], axis=1)
    return wt, st


def qknorm_fwd(qkv, qw, kw):
    rows = qkv.shape[0]
    tr = _pick(rows, (384, 128))
    wt, _ = _qk_scales(qw, kw)

    def fn(i, x, w):
        xq = x[:, :QK_W]
        y = xq * lax.rsqrt(_head_mean(xq * xq) + EPS) * w
        head = lambda t, j: t[:, j * SWA_D:(j + 1) * SWA_D][None]
        qo = jnp.concatenate([head(y, j) for j in range(SWA_H)], axis=0)
        ko = jnp.concatenate([head(y, SWA_H + j) for j in range(SWA_KV)], axis=0)
        vo = jnp.concatenate([head(x, SWA_H + SWA_KV + j) for j in range(SWA_KV)], axis=0)
        return qo, ko, vo

    hm = lambda nh: ((nh, rows, SWA_D), BF16, (nh, tr, SWA_D), lambda i: (0, i, 0), "r3")
    return rowwise(fn, [cols(qkv, tr), whole(wt)], [hm(SWA_H), hm(SWA_KV), hm(SWA_KV)],
                   steps=rows // tr, name="qknorm_fwd")


def qknorm_bwd(qkv, qw, kw, dqh, dk, dv):
    rows = qkv.shape[0]
    tr = _pick(rows, (384, 128))
    wt, st = _qk_scales(qw, kw)

    def fn(i, x, w, sc, dq, dkv, dvv):
        xq = x[:, :QK_W]
        dy = jnp.concatenate([dq[j] for j in range(SWA_H)] + [dkv], axis=1)
        r = lax.rsqrt(_head_mean(xq * xq) + EPS)
        xh = xq * r
        gw = dy * w
        dx = r * (gw - xh * _head_mean(gw * xh))
        return jnp.concatenate([dx, dvv], axis=1), jnp.sum(dy * sc * xh, axis=0, keepdims=True)

    dqkv, dw = rowwise(fn, [cols(qkv, tr), whole(wt), whole(st), heads(dqh, tr), cols(dk, tr), cols(dv, tr)],
                       [out2d(rows, 1536, BF16, tr)], steps=rows // tr, name="qknorm_bwd", accs=[((1, QK_W), F32)])
    dw = dw.reshape(SWA_H + SWA_KV, SWA_D)
    return dqkv, jnp.sum(dw[:SWA_H], axis=0, keepdims=True), jnp.sum(dw[SWA_H:], axis=0, keepdims=True)


def _place():
    return lax.axis_index("x"), lax.axis_index("y"), lax.axis_index("c")


ANY = pl.BlockSpec(memory_space=pl.ANY)


def _rcopy(ssem, rsem, k, src, dst, to):
    return pltpu.make_async_remote_copy(src_ref=src, dst_ref=dst, send_sem=ssem.at[k], recv_sem=rsem.at[k],
                                        device_id=to, device_id_type=MESH)


def gather_weights(shards, small):
    n = len(shards)
    halves = [t.shape[0] // 2 for t in shards]

    def body(*refs):
        s_refs, small_ref = refs[:n], refs[n]
        o_refs, osmall = refs[n + 1:2 * n + 1], refs[2 * n + 1]
        ssem, rsem, lsem = refs[2 * n + 2:]
        x, y, c = _place()
        me = 2 * x + y
        chips = [(1 - x, y), (x, 1 - y), (1 - x, 1 - y)]

        def half(k, s, hh):
            return o_refs[k].at[s, pl.ds(hh * halves[k], halves[k]), :]

        loc = pltpu.make_async_copy(small_ref, osmall.at[me], lsem)
        loc.start()
        sends = []
        for k in range(n):
            for j, (px, py) in enumerate(chips):
                sends.append(_rcopy(ssem, rsem, 6 * k + j, s_refs[k].at[pl.ds(c * halves[k], halves[k]), :],
                                    half(k, me, c), (px, py, c)))
        for j, (px, py) in enumerate(chips):
            sends.append(_rcopy(ssem, rsem, 6 * n + j, small_ref, osmall.at[me], (px, py, c)))
        for cp in sends:
            cp.start()
        for k in range(n):
            for j, (px, py) in enumerate(chips):
                s = 2 * px + py
                _rcopy(ssem, rsem, 6 * k + j, half(k, s, c), half(k, s, c), (x, y, c)).wait_recv()
                fwd = _rcopy(ssem, rsem, 6 * k + 3 + j, half(k, s, c), half(k, s, c), (x, y, 1 - c))
                fwd.start()
                sends.append(fwd)
        for k in range(n):
            for j, (px, py) in enumerate(chips):
                s = 2 * px + py
                _rcopy(ssem, rsem, 6 * k + 3 + j, half(k, s, 1 - c), half(k, s, 1 - c), (x, y, c)).wait_recv()
        for j, (px, py) in enumerate(chips):
            s = 2 * px + py
            _rcopy(ssem, rsem, 6 * n + j, osmall.at[s], osmall.at[s], (x, y, c)).wait_recv()
        for cp in sends:
            cp.wait_send()
        loc.wait()

    res = pl.pallas_call(
        body, name="gather_weights", interpret=False,
        out_shape=[jax.ShapeDtypeStruct((4,) + t.shape, t.dtype) for t in shards]
        + [jax.ShapeDtypeStruct((4, SW_ROWS, 1024), F32)],
        in_specs=[ANY] * (n + 1), out_specs=[ANY] * (n + 1),
        scratch_shapes=[pltpu.SemaphoreType.DMA((6 * n + 3,)), pltpu.SemaphoreType.DMA((6 * n + 3,)),
                        pltpu.SemaphoreType.DMA],
    )(*shards, small)
    return res[:n], res[n]


def _handshake(peers):
    barrier = pltpu.get_barrier_semaphore()
    for peer in peers:
        pl.semaphore_signal(barrier, inc=1, device_id=peer, device_id_type=MESH)
    pl.semaphore_wait(barrier, len(peers))


def gather_weights_beside(shards, cid, name):
    n = len(shards)
    halves = [t.shape[0] // 2 for t in shards]

    def body(*refs):
        s_refs, o_refs, ssem, rsem = refs[:n], refs[n:2 * n], refs[2 * n], refs[2 * n + 1]
        x, y, c = _place()
        me = 2 * x + y
        chips = [(1 - x, y), (x, 1 - y), (1 - x, 1 - y)]
        _handshake([(px, py, c) for px, py in chips] + [(x, y, 1 - c)])

        def half(k, s, hh):
            return o_refs[k].at[s, pl.ds(hh * halves[k], halves[k]), :]

        sends = []
        for k in range(n):
            for j, (px, py) in enumerate(chips):
                sends.append(_rcopy(ssem, rsem, 6 * k + j, s_refs[k].at[pl.ds(c * halves[k], halves[k]), :],
                                    half(k, me, c), (px, py, c)))
        for cp in sends:
            cp.start()
        for k in range(n):
            for j, (px, py) in enumerate(chips):
                s = 2 * px + py
                _rcopy(ssem, rsem, 6 * k + j, half(k, s, c), half(k, s, c), (x, y, c)).wait_recv()
                fwd = _rcopy(ssem, rsem, 6 * k + 3 + j, half(k, s, c), half(k, s, c), (x, y, 1 - c))
                fwd.start()
                sends.append(fwd)
        for k in range(n):
            for j, (px, py) in enumerate(chips):
                s = 2 * px + py
                _rcopy(ssem, rsem, 6 * k + 3 + j, half(k, s, 1 - c), half(k, s, 1 - c), (x, y, c)).wait_recv()
        for cp in sends:
            cp.wait_send()

    return pl.kernel(
        body, name=name,
        out_type=[jax.ShapeDtypeStruct((4,) + t.shape, t.dtype) for t in shards],
        mesh=plsc.ScalarSubcoreMesh(axis_name="sequencer", num_cores=1),
        scratch_types=[pltpu.SemaphoreType.DMA((6 * n,)), pltpu.SemaphoreType.DMA((6 * n,))],
        compiler_params=pltpu.CompilerParams(collective_id=cid),
    )(*shards)


def swap_halves(gs, *, name):
    n = len(gs)

    def body(*refs):
        g_refs, o_refs, ssem, rsem = refs[:n], refs[n:2 * n], refs[2 * n], refs[2 * n + 1]
        x, y, c = _place()
        cps = []
        for k in range(n):
            hk = g_refs[k].shape[1] // 2
            cps.append(_rcopy(ssem, rsem, k, g_refs[k].at[:, pl.ds((1 - c) * hk, hk), :], o_refs[k], (x, y, 1 - c)))
        for cp in cps:
            cp.start()
        for cp in cps:
            cp.wait()

    return pl.pallas_call(
        body, name=name, interpret=False,
        out_shape=[jax.ShapeDtypeStruct((4, t.shape[1] // 2, t.shape[2]), t.dtype) for t in gs],
        in_specs=[ANY] * n, out_specs=[ANY] * n,
        scratch_shapes=[pltpu.SemaphoreType.DMA((n,)), pltpu.SemaphoreType.DMA((n,))],
    )(*gs)


def _sum_rows(hk):
    return _pick(hk, (512, 352, 256, 128))


def pair_sum(g, other, c_idx, *, name):
    _, hk, width = other.shape
    tr = _sum_rows(hk)
    nbk = hk // tr

    def body(c_ref, g_ref, o_ref, out_ref):
        out_ref[...] = (g_ref[...].astype(F32) + o_ref[...].astype(F32)).astype(BF16)

    return pl.pallas_call(
        body, name=name, interpret=False,
        out_shape=jax.ShapeDtypeStruct((4, hk, width), BF16),
        grid_spec=pltpu.PrefetchScalarGridSpec(
            num_scalar_prefetch=1, grid=(4, nbk),
            in_specs=[pl.BlockSpec((1, tr, width), lambda s, i, c_ref: (s, c_ref[0] * nbk + i, 0)),
                      pl.BlockSpec((1, tr, width), lambda s, i, c_ref: (s, i, 0))],
            out_specs=pl.BlockSpec((1, tr, width), lambda s, i, c_ref: (s, i, 0))),
        compiler_params=_params(("parallel", "parallel")),
    )(c_idx, g, other)


def chip_sum(p, got, idx, *, name):
    _, hk, width = got.shape
    tr = _sum_rows(hk)
    nbk = hk // tr

    def body(idx_ref, p_ref, g_ref, out_ref):
        acc = p_ref[0].astype(F32)
        for j in range(3):
            acc = acc + g_ref[j].astype(F32)
        out_ref[0] = acc

    return pl.pallas_call(
        body, name=name, interpret=False,
        out_shape=jax.ShapeDtypeStruct((2, hk, width), F32),
        grid_spec=pltpu.PrefetchScalarGridSpec(
            num_scalar_prefetch=1, grid=(nbk,),
            in_specs=[pl.BlockSpec((1, tr, width), lambda i, idx_ref: (idx_ref[0], i, 0)),
                      pl.BlockSpec((3, tr, width), lambda i, idx_ref: (0, i, 0))],
            out_specs=pl.BlockSpec((1, tr, width), lambda i, idx_ref: (idx_ref[1], i, 0))),
        compiler_params=_params(("parallel",)),
    )(idx, p, got)


def join_halves(qs):
    n = len(qs)

    def body(*refs):
        q_refs, o_refs, ssem, rsem = refs[:n], refs[n:2 * n], refs[2 * n], refs[2 * n + 1]
        x, y, c = _place()
        cps = [_rcopy(ssem, rsem, k, q_refs[k].at[c], o_refs[k].at[c], (x, y, 1 - c)) for k in range(n)]
        for cp in cps:
            cp.start()
        for k in range(n):
            _rcopy(ssem, rsem, k, q_refs[k].at[c], o_refs[k].at[1 - c], (x, y, 1 - c)).wait_recv()
        for cp in cps:
            cp.wait_send()

    return pl.pallas_call(
        body, name="join_halves", interpret=False,
        out_shape=[jax.ShapeDtypeStruct(t.shape, t.dtype) for t in qs],
        in_specs=[ANY] * n, out_specs=[ANY] * n, input_output_aliases={k: k for k in range(n)},
        scratch_shapes=[pltpu.SemaphoreType.DMA((n,)), pltpu.SemaphoreType.DMA((n,))],
    )(*qs)


def scatter_chips_beside(ps, cid, name):
    n = len(ps)

    def body(*refs):
        p_refs, o_refs, ssem, rsem = refs[:n], refs[n:2 * n], refs[2 * n], refs[2 * n + 1]
        x, y, c = _place()
        chips = [(1 - x, y), (x, 1 - y), (1 - x, 1 - y)]
        _handshake([(px, py, c) for px, py in chips])
        cps = [_rcopy(ssem, rsem, 3 * k + j, p_refs[k].at[2 * px + py], o_refs[k].at[j], (px, py, c))
               for k in range(n) for j, (px, py) in enumerate(chips)]
        for cp in cps:
            cp.start()
        for cp in cps:
            cp.wait()

    return pl.kernel(
        body, name=name, out_type=[jax.ShapeDtypeStruct((3,) + t.shape[1:], t.dtype) for t in ps],
        mesh=plsc.ScalarSubcoreMesh(axis_name="sequencer", num_cores=1),
        scratch_types=[pltpu.SemaphoreType.DMA((3 * n,)), pltpu.SemaphoreType.DMA((3 * n,))],
        compiler_params=pltpu.CompilerParams(collective_id=cid),
    )(*ps)


def reduce_begin(gs, names, c_idx, cid, tag):
    others = swap_halves(gs, name=f"swap_halves_{tag}")
    pairs = [pair_sum(g, o, c_idx, name=f"pair_sum_{nm}") for g, o, nm in zip(gs, others, names)]
    return pairs, scatter_chips_beside(pairs, cid, f"scatter_chips_{tag}")


def reduce_end(pairs, gots, names, idx):
    mine = [chip_sum(p, g, idx, name=f"chip_sum_{nm}") for p, g, nm in zip(pairs, gots, names)]
    return [q.reshape(2 * q.shape[1], q.shape[2]) for q in join_halves(mine)]


def gather_small(v):
    def body(v_ref, o_ref, ssem, rsem, lsem):
        x, y, c = _place()
        peers = []
        for k in range(1, 8):
            fx, fy, fc = (k >> 2) & 1, (k >> 1) & 1, k & 1
            peers.append((1 - x if fx else x, 1 - y if fy else y, 1 - c if fc else c))
        _handshake(peers)
        loc = pltpu.make_async_copy(v_ref, o_ref.at[4 * x + 2 * y + c], lsem)
        loc.start()
        cps = []
        for k, (px, py, pc) in enumerate(peers):
            cps.append((pltpu.make_async_remote_copy(
                src_ref=v_ref, dst_ref=o_ref.at[4 * x + 2 * y + c], send_sem=ssem.at[k], recv_sem=rsem.at[k],
                device_id=(px, py, pc), device_id_type=MESH), 4 * px + 2 * py + pc))
        for cp, _ in cps:
            cp.start()
        for k, (cp, peer) in enumerate(cps):
            pltpu.make_async_remote_copy(
                src_ref=v_ref, dst_ref=o_ref.at[peer], send_sem=ssem.at[k], recv_sem=rsem.at[k],
                device_id=(x, y, c), device_id_type=MESH).wait_recv()
        for cp, _ in cps:
            cp.wait_send()
        loc.wait()

    return pl.kernel(
        body, name="gather_small", out_type=jax.ShapeDtypeStruct((8, SV_ROWS, 1024), F32),
        mesh=plsc.ScalarSubcoreMesh(axis_name="sequencer", num_cores=1),
        scratch_types=[pltpu.SemaphoreType.DMA((7,)), pltpu.SemaphoreType.DMA((7,)), pltpu.SemaphoreType.DMA],
        compiler_params=pltpu.CompilerParams(collective_id=6),
    )(v)


def sum_slots(a):
    def fn(i, t):
        acc = t[0]
        for k in range(1, 8):
            acc = acc + t[k]
        return acc

    return rowwise(fn, [whole(a)], [((SV_ROWS, 1024), F32, (SV_ROWS, 1024), lambda i: (0, 0), "w")], steps=1,
                   name="sum_slots")[0]


def _head_rms(x, nw):
    xs, rs = [], []
    for h in range(DN_H):
        xh = x[:, h * DN_D:(h + 1) * DN_D]
        r = lax.rsqrt(jnp.mean(xh * xh, axis=1, keepdims=True) + EPS)
        xs.append(xh * r)
        rs.append(r)
    return xs, rs


def bg_fwd(p, alog, dtb):
    rows = p.shape[0]
    tr = _pick(rows, (384, 128))

    def fn(i, x, al, dt):
        lane = lax.broadcasted_iota(jnp.int32, x.shape, 1)
        row = i + lax.broadcasted_iota(jnp.int32, x.shape, 0)
        g = -jnp.exp(al) * _softplus(x + dt)
        out = jnp.where(lane < 4, _sigmoid(x), jnp.where(lane < 8, g, 0.0))
        return jnp.where(row >= PAD, out, 0.0)

    return rowwise(fn, [cols(p, tr, 128, BG0 // 128), whole(alog), whole(dtb)], [out2d(rows, 128, F32, tr)],
                   steps=rows // tr, name="bg_fwd")[0]


def bg_bwd(p, alog, dtb, dbg):
    rows = p.shape[0]
    tr = _pick(rows, (384, 128))

    def fn(i, x, al, dt, g_in):
        lane = lax.broadcasted_iota(jnp.int32, x.shape, 1)
        row = i + lax.broadcasted_iota(jnp.int32, x.shape, 0)
        live = row >= PAD
        is_b = jnp.logical_and(live, lane < 4)
        is_g = jnp.logical_and(live, jnp.logical_and(lane >= 4, lane < 8))
        beta = _sigmoid(x)
        ea = jnp.exp(al)
        g = -ea * _softplus(x + dt)
        dalpha = jnp.where(is_g, g_in * (-ea) * _sigmoid(x + dt), 0.0)
        dx = jnp.where(is_b, g_in * beta * (1.0 - beta), dalpha)
        dal = jnp.sum(jnp.where(is_g, g_in * g, 0.0), axis=0, keepdims=True)
        return jnp.concatenate([dx, jnp.zeros(x.shape, F32)], axis=1), dal, jnp.sum(dalpha, axis=0, keepdims=True)

    return rowwise(fn, [cols(p, tr, 128, BG0 // 128), whole(alog), whole(dtb), cols(dbg, tr)],
                   [out2d(rows, 256, BF16, tr)], steps=rows // tr, name="bg_bwd",
                   accs=[((1, 128), F32), ((1, 128), F32)])


def dn_qkv_post(j, y):
    xs = _silu(y)
    sc = jnp.where(j == 0, DN_D ** -0.5, 1.0)
    outs = []
    for h in range(DN_H):
        xh = xs[:, h * DN_D:(h + 1) * DN_D]
        r = lax.rsqrt(jnp.sum(xh * xh, axis=1, keepdims=True) + EPS)
        outs.append(jnp.where(j < 2, xh * r * sc, xh))
    return jnp.concatenate(outs, axis=1), y


def dn_qkv_bwd(cq, dq, dk, dv):
    rows = cq.shape[0]
    tr = _pick(rows, (384, 128))

    def fn(i, c0, c1, c2, g0, g1, g2):
        pieces = []
        for kind, (cv, g) in enumerate(((c0, g0), (c1, g1), (c2, g2))):
            xs = _silu(cv)
            if kind < 2:
                sc = DN_D ** -0.5 if kind == 0 else 1.0
                ds = []
                for h in range(DN_H):
                    sl = slice(h * DN_D, (h + 1) * DN_D)
                    xh, gh = xs[:, sl], g[:, sl]
                    r = lax.rsqrt(jnp.sum(xh * xh, axis=1, keepdims=True) + EPS)
                    xn = xh * r
                    ds.append(sc * r * (gh - xn * jnp.sum(gh * xn, axis=1, keepdims=True)))
                dxs = jnp.concatenate(ds, axis=1)
            else:
                dxs = g
            pieces.append(dxs * _dsilu(cv))
        return jnp.concatenate(pieces, axis=1)

    ins = [cols(cq, tr, DN_DIM, k) for k in range(3)] + [cols(t, tr) for t in (dq, dk, dv)]
    return rowwise(fn, ins, [out2d(rows, 3 * DN_DIM, F32, tr)], steps=rows // tr, name="dn_qkv_bwd")[0]


def dn_out_fwd(o, p, nw):
    rows = o.shape[0]
    tr = _pick(rows, (384, 128))

    def fn(i, ov, z, w):
        xs, _ = _head_rms(ov, w)
        return jnp.concatenate(xs, axis=1) * jnp.concatenate([w] * DN_H, axis=1) * _silu(z)

    return rowwise(fn, [cols(o, tr), cols(p, tr, DN_DIM, 6), whole(nw)], [out2d(rows, DN_DIM, BF16, tr)],
                   steps=rows // tr, name="dn_out_fwd")[0]


def dn_out_bwd(o, p, nw, dymix):
    rows = o.shape[0]
    tr = _pick(rows, (384, 128))

    def fn(i, ov, z, w, dy):
        xs, rs = _head_rms(ov, w)
        sz = _silu(z)
        dn = dy * sz
        dos, dw = [], jnp.zeros((1, DN_D), F32)
        for h in range(DN_H):
            sl = slice(h * DN_D, (h + 1) * DN_D)
            gw = dn[:, sl] * w
            dos.append(rs[h] * (gw - xs[h] * jnp.mean(gw * xs[h], axis=1, keepdims=True)))
            dw = dw + jnp.sum(dn[:, sl] * xs[h], axis=0, keepdims=True)
        n = jnp.concatenate(xs, axis=1) * jnp.concatenate([w] * DN_H, axis=1)
        return jnp.concatenate(dos, axis=1), dy * n * _dsilu(z), dw

    return rowwise(fn, [cols(o, tr), cols(p, tr, DN_DIM, 6), whole(nw), cols(dymix, tr, DN_DIM, 1)],
                   [out2d(rows, DN_DIM, F32, tr), out2d(rows, DN_DIM, BF16, tr)], steps=rows // tr,
                   name="dn_out_bwd", accs=[((1, DN_D), F32)])


def conv_a_pre_bwd(dymix, cv, p):
    rows = cv.shape[0]
    tr = _pick(rows, (384, 128))

    def fn(i, dy, c, go):
        return dy * c, dy * go

    return rowwise(fn, [cols(dymix, tr, D_CONV, 0), cols(cv, tr), cols(p, tr, D_CONV, 1)],
                   [out2d(rows, D_CONV, BF16, tr), out2d(rows, D_CONV, F32, tr)], steps=rows // tr,
                   name="conv_a_pre_bwd")


def _rows8(w):
    return jnp.pad(w.astype(F32), ((0, 8 - w.shape[0]), (0, 0)))


def _lanes(v, at):
    return jnp.pad(v.astype(F32), (at, 128 - at - v.shape[0]))[None]


def add_norm(a, w, h, next_nw, *, name):
    return mm(a, w, name=name, epi=_add_norm_epi, epi_ins=[(h, lambda j: 0)], epi_consts=[next_nw],
              epi_outs=[F32, BF16])


def _add_norm_epi(row0, t, h, nw):
    x = t + h
    return x, x * lax.rsqrt(jnp.mean(x * x, axis=1, keepdims=True) + EPS) * nw


def ffn_up_conv(hn, w_up, cw8, *, name):
    rows = hn.shape[0]
    tn = w_up.shape[2]
    tm = _pick(rows, (384, 128))
    nr = rows // tm

    def body(x_ref, wg_ref, wv_ref, w_ref, ug_ref, uv_ref, gc_ref, a_ref, carry, scr):
        i = pl.program_id(1)
        x = x_ref[...]
        gate = _dot(x, wg_ref[...])
        val = _dot(x, wv_ref[...])
        ug_ref[...] = gate.astype(BF16)
        uv_ref[...] = val.astype(BF16)
        scr[0:8, :] = jnp.where(i > 0, carry[...], 0.0)
        scr[8:8 + tm, :] = gate
        carry[...] = gate[tm - 8:tm]
        y = jnp.zeros((tm, tn), F32)
        for q in range(3):
            sh = 2 - q
            y = y + w_ref[q:q + 1, :] * scr[8 - sh:8 - sh + tm, :]
        gc_ref[...] = y.astype(BF16)
        a_ref[...] = (_silu(y) * val).astype(BF16)

    half = pl.BlockSpec((tm, tn), lambda j, i: (i, j))
    return pl.pallas_call(
        body, name=name, interpret=False,
        out_shape=[jax.ShapeDtypeStruct((rows, D_FF), BF16)] * 4,
        grid=(D_FF // tn, nr),
        in_specs=[pl.BlockSpec((tm, D), lambda j, i: (i, 0)),
                  pl.BlockSpec((None, D, tn), lambda j, i: (j, 0, 0)),
                  pl.BlockSpec((None, D, tn), lambda j, i: (j + D_FF // tn, 0, 0)),
                  pl.BlockSpec((8, tn), lambda j, i: (0, j))],
        out_specs=[half] * 4,
        scratch_shapes=[pltpu.VMEM((8, tn), F32), pltpu.VMEM((tm + 8, tn), F32)],
        compiler_params=_params(("arbitrary", "arbitrary")),
    )(hn, w_up, w_up, cw8)


def ffn_down_bwd(dh, w_down, gc, uv, ug, cw8, *, name):
    rows = dh.shape[0]
    tn = D_FF // 2
    tm = _pick(rows, (384, 128))
    nr = rows // tm
    r8 = tm // 8

    def body(dh_ref, w_ref, gc_ref, uv_ref, ug_ref, halo_ref, cw_ref, du_ref, dw_ref, carry, gscr, xscr):
        ip = pl.program_id(1)
        i = nr - 1 - ip
        da = _dot(dh_ref[...].astype(BF16), w_ref[...], 1, 1)
        c, val = gc_ref[...].astype(F32), uv_ref[...].astype(F32)
        dgc = da * val * _dsilu(c)
        du_ref[:, tn:] = (da * _silu(c)).astype(BF16)
        gscr[0:tm, :] = dgc
        gscr[tm:tm + 8, :] = jnp.where(ip > 0, carry[...], 0.0)
        carry[...] = dgc[0:8]
        xscr[0:8, :] = jnp.where(i > 0, halo_ref[...].astype(F32), 0.0)
        xscr[8:8 + tm, :] = ug_ref[...].astype(F32)
        dx = jnp.zeros((tm, tn), F32)
        dws = []
        for q in range(3):
            sh = 2 - q
            dx = dx + cw_ref[q:q + 1, :] * gscr[sh:sh + tm, :]
            dws.append(jnp.sum(dgc * xscr[8 - sh:8 - sh + tm, :], axis=0, keepdims=True))
        du_ref[:, :tn] = dx.astype(BF16)

        @pl.when(ip == 0)
        def _():
            dw_ref[...] = jnp.zeros((8, tn), F32)

        dw_ref[...] += jnp.concatenate(dws + [jnp.zeros((5, tn), F32)], axis=0)

    rev = lambda ip: nr - 1 - ip
    tile = lambda arr: pl.BlockSpec((tm, tn), lambda j, ip: (rev(ip), j))
    return pl.pallas_call(
        body, name=name, interpret=False,
        out_shape=[jax.ShapeDtypeStruct((rows, 2 * D_FF), BF16), jax.ShapeDtypeStruct((8, D_FF), F32)],
        grid=(2, nr),
        in_specs=[pl.BlockSpec((tm, D), lambda j, ip: (rev(ip), 0)),
                  pl.BlockSpec((tn, D), lambda j, ip: (j, 0)),
                  tile(gc), tile(uv), tile(ug),
                  pl.BlockSpec((8, tn), lambda j, ip: (jnp.maximum(rev(ip) * r8 - 1, 0), j)),
                  pl.BlockSpec((8, tn), lambda j, ip: (0, j))],
        out_specs=[pl.BlockSpec((tm, 2 * tn), lambda j, ip: (rev(ip), j)),
                   pl.BlockSpec((8, tn), lambda j, ip: (0, j))],
        scratch_shapes=[pltpu.VMEM((8, tn), F32), pltpu.VMEM((tm + 8, tn), F32), pltpu.VMEM((tm + 8, tn), F32)],
        compiler_params=_params(("arbitrary", "arbitrary")),
    )(dh, w_down, gc, uv, ug, ug, cw8)


def ffn_fwd(h, hn, w_up, cw8, w_down, tag, next_nw=None, target=None):
    ug, uv, gc, a = ffn_up_conv(hn, w_up, cw8, name=f"ffn{tag}_up")
    if target is not None:
        out, hn_next = add_loss(a, w_down, h, target, name=f"ffn{tag}_down")
    else:
        out, hn_next = add_norm(a, w_down, h, next_nw, name=f"ffn{tag}_down")
    return out, hn_next, (hn, ug, uv, a, gc)


def ffn_bwd(h, nw, w_up, cw8, w_down, saved, dh, tag):
    hn, ug, uv, a, gc = saved
    du, d_cw = ffn_down_bwd(dh, w_down, gc, uv, ug, cw8, name=f"ffn{tag}_down_dx")
    d_w_down = mm(a, dh, ta=True, out_dtype=BF16, name=f"ffn{tag}_down_dw")
    dh_new, d_nw = dx_rms_bwd(du, w_up, h, nw, dh, name=f"ffn{tag}_up_dx", b_chip=True, swap_mid=True)
    d_w_up = mm(hn, du, ta=True, out_dtype=BF16, out_chip=True, swap_mid=True, name=f"ffn{tag}_up_dw")
    return dh_new, d_nw, d_w_up, d_cw, d_w_down


def mixer_fwd(h, nw, w_in, ca8, dc8, alog, dtb, dnw, w_out, tie=None, next_nw=None):
    rows = h.shape[0]
    tr = _pick(rows, (384, 128))
    hn = rms_fwd(h, nw, name="mix_norm")
    if callable(w_in):
        hn, w_in = w_in(hn)
    p = mm(hn, w_in, name="mix_in")
    y_a, cv = conv_fwd([(p, 0), (p, 2)], ca8, 3, rows=rows, c=D_CONV, tc=D_CONV, tr=tr, name="conv_a",
                       pre=lambda gi, ah: gi * ah, post=lambda j, y, go: (go * y, y), extras=[(p, 1)],
                       outs=[BF16, F32])
    qkv_n, cq = conv_fwd([(p, 3)], dc8, 4, rows=rows, c=3 * DN_DIM, tc=DN_DIM, tr=tr, name="dn_conv",
                         post=dn_qkv_post, outs=[F32, F32], strip=tr)
    bgcol = bg_fwd(p, alog, dtb)
    if tie is not None:
        bgcol = tie(bgcol)
    bgrow = bgcol[:, :8].reshape(rows // CH, CH, 8).transpose(0, 2, 1)
    o, s_all, ti_all = dn_fwd(qkv_n, bgcol, bgrow)
    y_b = dn_out_fwd(o, p, dnw)
    ymix = jnp.concatenate([y_a, y_b], axis=1)
    w_out = w_out() if callable(w_out) else w_out
    out, hn_next = add_norm(ymix, w_out, h, next_nw, name="mix_out")
    return out, hn_next, (hn, p, cv, qkv_n, cq, bgcol, bgrow, o, s_all, ti_all, ymix, w_in)


def mixer_bwd(h, nw, ca8, dc8, alog, dtb, dnw, w_out, saved, dh):
    hn, p, cv, qkv_n, cq, bgcol, bgrow, o, s_all, ti_all, ymix, w_in = saved
    rows = h.shape[0]
    tr = _pick(rows, (384, 128))
    dymix = mm(dh, w_out, tb=True, name="mix_out_dx")
    d_w_out = mm(ymix, dh, ta=True, out_dtype=BF16, name="mix_out_dw")
    do, dz, d_dnw = dn_out_bwd(o, p, dnw, dymix)
    dq, dk, dv, dbg = dn_bwd(qkv_n, bgcol, bgrow, s_all, ti_all, do)
    dbg_p, d_alog, d_dtb = bg_bwd(p, alog, dtb, dbg)
    dcq = dn_qkv_bwd(cq, dq, dk, dv)
    dqkv, d_dc = conv_bwd([(p, 3)], dc8, 4, dcq, rows=rows, c=3 * DN_DIM, tc=DN_DIM, tr=tr, name="dn_conv_bwd",
                          post=lambda dx: dx, outs=[BF16])
    dgo, dcv = conv_a_pre_bwd(dymix, cv, p)
    dgi, dah, d_ca = conv_bwd([(p, 0), (p, 2)], ca8, 3, dcv, rows=rows, c=D_CONV, tc=D_CONV, tr=tr,
                              name="conv_a_bwd", pre=lambda gi, ah: gi * ah,
                              post=lambda dm, gi, ah: (dm * ah, dm * gi), extras=[(p, 0), (p, 2)], outs=[BF16, BF16])
    dp = jnp.concatenate([dgi, dgo, dah, dqkv, dz, dbg_p], axis=1)
    dh_new, d_nw = dx_rms_bwd(dp, w_in, h, nw, dh, name="mix_in_dx")
    d_w_in = mm(hn, dp, ta=True, out_dtype=BF16, name="mix_in_dw")
    return dh_new, d_nw, d_w_in, d_ca, d_dc, d_alog, d_dtb, d_dnw, d_w_out


def swa_layer_fwd(h, hn, wqkv, qw, kw, sinks, wo, next_nw):
    qkv = mm(hn, wqkv, name="swa_qkv")
    qh, kh, vh = qknorm_fwd(qkv, qw, kw)
    att = swa_fwd(qh, kh, vh, sinks)
    out, hn_next = add_norm(att, wo, h, next_nw, name="swa_out")
    return out, hn_next, (hn, qkv, qh, kh, vh, att)


def swa_layer_bwd(h, nw, wqkv, qw, kw, sinks, wo, saved, dh):
    hn, qkv, qh, kh, vh, att = saved
    datt = mm(dh, wo, tb=True, out_dtype=BF16, name="swa_out_dx")
    d_wo = mm(att, dh, ta=True, out_dtype=BF16, name="swa_out_dw")
    dqh, dkh, dvh, dsk = swa_bwd(qh, kh, vh, sinks, datt)
    dqkv, d_qw, d_kw = qknorm_bwd(qkv, qw, kw, dqh, dkh, dvh)
    dh_new, d_nw = dx_rms_bwd(dqkv, wqkv, h, nw, dh, name="swa_qkv_dx")
    d_wqkv = mm(hn, dqkv, ta=True, out_dtype=BF16, name="swa_qkv_dw")
    d_sinks = jnp.sum(dsk[:, :, 0], axis=0)
    return dh_new, d_nw, d_wqkv, d_qw, d_kw, d_sinks, d_wo


BIG = ("mix_w_in", "mix_w_out", "swa_wq", "swa_wk", "swa_wv", "swa_wo", "ffn_w_up", "ffn_w_down")


def _flat_pad(parts, rows):
    v = jnp.concatenate([t.astype(F32).reshape(-1) for t in parts])
    return jnp.pad(v, (0, rows * 1024 - v.shape[0])).reshape(rows, 1024)


def _split_flat(flat, shapes):
    v = flat.reshape(-1)
    out, o = [], 0
    for s in shapes:
        n = 1
        for d_ in s:
            n *= d_
        out.append(v[o:o + n].reshape(s))
        o += n
    return out


def local_step(x0, target0, meta_full, anw, fnw, w_in, ca8, dc8, alog, dtb, dnw, qw, kw, sinks, fc8, late,
               begin=None, tie=None):
    begin = begin or (lambda tag, names, grads: None)
    h0 = jnp.concatenate([jnp.zeros((PAD, D), F32), meta_full, x0], axis=0)
    h1, hn1, s_mix = mixer_fwd(h0, anw[0], w_in, ca8, dc8, alog, dtb, dnw, lambda: late()[0], tie, fnw[0])
    w_out, wqkv, wo, w_up, w_down = late()
    h2, hn2, s_f0 = ffn_fwd(h1, hn1, w_up[0], fc8[0], w_down[0], 0, anw[1])
    h3, hn3, s_swa = swa_layer_fwd(h2, hn2, wqkv, qw, kw, sinks, wo, fnw[1])
    dh, loss_l, s_f1 = ffn_fwd(h3, hn3, w_up[1], fc8[1], w_down[1], 1, target=target0)
    dh, d_fnw1, d_up1, d_fc1, d_down1 = ffn_bwd(h3, fnw[1], w_up[1], fc8[1], w_down[1], s_f1, dh, 1)
    begin("ffn1", ("up1", "down1"), [d_up1, d_down1.reshape(4, 704, D)])
    dh, d_anw1, d_wqkv, d_qw, d_kw, d_sinks, d_wo = swa_layer_bwd(h2, anw[1], wqkv, qw, kw, sinks, wo, s_swa, dh)
    begin("swa", ("wq", "wk", "wv", "wo"),
          [d_wqkv[:, :D].reshape(4, 256, D), d_wqkv[:, D:D + 256].reshape(4, 256, 256),
           d_wqkv[:, D + 256:].reshape(4, 256, 256), d_wo.reshape(4, 256, D)])
    dh, d_fnw0, d_up0, d_fc0, d_down0 = ffn_bwd(h1, fnw[0], w_up[0], fc8[0], w_down[0], s_f0, dh, 0)
    begin("ffn0", ("up0", "down0"), [d_up0, d_down0.reshape(4, 704, D)])
    dh, d_anw0, d_w_in, d_ca, d_dc, d_alog, d_dtb, d_dnw, d_w_out = mixer_bwd(
        h0, anw[0], ca8, dc8, alog, dtb, dnw, w_out, s_mix, dh)
    begin("mix", ("w_in", "w_out"),
          [d_w_in[:, :IN_DIM].reshape(D, 4, 898).transpose(1, 0, 2), d_w_out.reshape(4, 256, D)])
    return (dh, loss_l, d_anw0, d_anw1, d_fnw0, d_fnw1, d_w_in, d_ca, d_dc, d_alog, d_dtb, d_dnw, d_w_out, d_wqkv,
            d_qw, d_kw, d_sinks, d_wo, d_up0, d_up1, d_fc0, d_fc1, d_down0, d_down1)


def kernel(x, meta_tokens, attn_norm_w, ffn_norm_w, mix_w_in, conv_a_w, dn_conv_w, dn_a_log, dn_dt_bias, dn_norm_w, mix_w_out, swa_wq, swa_wk, swa_wv, swa_q_norm_w, swa_k_norm_w, swa_sinks, swa_wo, ffn_w_up, ffn_conv_w, ffn_w_down, loss_target, m_meta_tokens, m_attn_norm_w, m_ffn_norm_w, m_mix_w_in, m_conv_a_w, m_dn_conv_w, m_dn_a_log, m_dn_dt_bias, m_dn_norm_w, m_mix_w_out, m_swa_wq, m_swa_wk, m_swa_wv, m_swa_q_norm_w, m_swa_k_norm_w, m_swa_sinks, m_swa_wo, m_ffn_w_up, m_ffn_conv_w, m_ffn_w_down, v_meta_tokens, v_attn_norm_w, v_ffn_norm_w, v_mix_w_in, v_conv_a_w, v_dn_conv_w, v_dn_a_log, v_dn_dt_bias, v_dn_norm_w, v_mix_w_out, v_swa_wq, v_swa_wk, v_swa_wv, v_swa_q_norm_w, v_swa_k_norm_w, v_swa_sinks, v_swa_wo, v_ffn_w_up, v_ffn_conv_w, v_ffn_w_down):
    ix, iy, ic = lax.axis_index("x"), lax.axis_index("y"), lax.axis_index("c")
    chip = 2 * ix + iy
    seq = x.shape[1]
    rows = HEAD0 + seq

    small_sharded = (conv_a_w, dn_conv_w, ffn_conv_w, meta_tokens)
    up_b, down_b = ffn_w_up.astype(BF16), ffn_w_down.astype(BF16)
    own = [mix_w_in[0].astype(BF16), mix_w_out[0].astype(BF16), swa_wq[0].astype(BF16), swa_wk[0].astype(BF16),
           swa_wv[0].astype(BF16), swa_wo[0].astype(BF16), up_b[0], up_b[1], down_b[0], down_b[1]]
    fill = lambda gathered, mine: [lax.dynamic_update_slice_in_dim(g, t[None], chip, axis=0)
                                   for g, t in zip(gathered, mine)]
    on_its_way, = gather_weights_beside(own[:1], 9, "gather_w_in")
    _, g_small = gather_weights([], _flat_pad(small_sharded, SW_ROWS))

    rest = {}

    def w_in(hn):
        hn, got, g_out = lax.optimization_barrier((hn, on_its_way, own[1]))
        rest["w_out"] = fill(gather_weights_beside([g_out], 1, "gather_w_out"), [g_out])
        g_in, = fill([got], own[:1])
        return hn, jnp.pad(g_in.transpose(1, 0, 2).reshape(D, IN_DIM), ((0, 0), (0, P_W - IN_DIM)))

    def tie(t):
        t, *mine = lax.optimization_barrier((t, *own[2:]))
        g_q, g_k, g_v, g_o, g_up0, g_up1, g_dn0, g_dn1 = mine
        soon, last = [g_up0, g_dn0, g_q, g_k, g_v, g_o], [g_up1, g_dn1]
        rest["soon"] = fill(gather_weights_beside(soon, 7, "gather_layers_12"), soon)
        rest["last"] = fill(gather_weights_beside(last, 8, "gather_layer_3"), last)
        return t

    def late():
        (g_out,), (g_up0, g_dn0, g_q, g_k, g_v, g_o), (g_up1, g_dn1) = rest["w_out"], rest["soon"], rest["last"]
        wqkv = jnp.concatenate([g_q.reshape(D, D), g_k.reshape(D, 256), g_v.reshape(D, 256)], axis=1)
        return (g_out.reshape(D, D), wqkv, g_o.reshape(D, D), [g_up0, g_up1],
                [g_dn0.reshape(D_FF, D), g_dn1.reshape(D_FF, D)])

    gs = g_small.reshape(4, -1)
    ca_full = gs[:, 0:384].reshape(4, 3, 128).transpose(1, 0, 2).reshape(3, D_CONV)
    dc_full = gs[:, 384:1920].reshape(4, 4, 384).transpose(1, 0, 2).reshape(4, 3 * DN_DIM)
    fc_full = gs[:, 1920:6144].reshape(4, 2, 3, 704).transpose(1, 2, 0, 3).reshape(2, 3, D_FF)
    meta_full = gs[:, 6144:10240].reshape(4, N_META, 256).transpose(1, 0, 2).reshape(N_META, D)
    ca8, dc8 = _rows8(ca_full), _rows8(dc_full)
    fc8 = [_rows8(fc_full[0]), _rows8(fc_full[1])]
    alog, dtb = _lanes(dn_a_log[0], 4), _lanes(dn_dt_bias[0], 4)
    dnw = dn_norm_w.astype(F32)
    qw, kw = swa_q_norm_w.astype(F32), swa_k_norm_w.astype(F32)
    sinks = swa_sinks[0].astype(F32)
    anw = [attn_norm_w[0:1], attn_norm_w[1:2]]
    fnw = [ffn_norm_w[0:1], ffn_norm_w[1:2]]

    c_idx = jnp.reshape(ic, (1,)).astype(jnp.int32)
    chip_idx = jnp.stack([chip, ic]).astype(jnp.int32)
    begun = []

    def begin(tag, names, grads):
        pairs, gots = reduce_begin(grads, names, c_idx, 2 + len(begun), tag)
        begun.append((names, pairs, gots))

    (dh, loss_l, d_anw0, d_anw1, d_fnw0, d_fnw1, d_w_in, d_ca, d_dc, d_alog, d_dtb, d_dnw, d_w_out, d_wqkv, d_qw,
     d_kw, d_sinks, d_wo, d_up0, d_up1, d_fc0, d_fc1, d_down0, d_down1) = local_step(
        x[0], loss_target[0], meta_full, anw, fnw, w_in, ca8, dc8, alog, dtb, dnw, qw, kw, sinks, fc8, late,
        begin, tie)
    grad_x = dh[HEAD0:][None]

    small_parts = [jnp.concatenate([d_anw0, d_anw1], axis=0), jnp.concatenate([d_fnw0, d_fnw1], axis=0),
                   d_alog[0, 4:8], d_dtb[0, 4:8], d_dnw, d_qw, d_kw, d_sinks,
                   d_ca[:3], d_dc[:4], jnp.stack([d_fc0[:3], d_fc1[:3]]), dh[PAD:HEAD0], loss_l[0, 0:1]]
    small_shapes = [(2, D), (2, D), (1, 4), (1, 4), (1, DN_D), (1, SWA_D), (1, SWA_D), (1, SWA_H),
                    (1, 3, D_CONV), (1, 4, 3 * DN_DIM), (2, 3, D_FF), (N_META, D), ()]
    gathered_small = gather_small(_flat_pad(small_parts, SV_ROWS))

    red_big = {}
    for part in (begun[:-1], begun[-1:]):
        part_names = [n for names, _, _ in part for n in names]
        red_big.update(zip(part_names, reduce_end([p for _, ps, _ in part for p in ps],
                                                  [g for _, _, gs_ in part for g in gs_], part_names, chip_idx)))
    g_w_in, g_w_out, g_wq, g_wk, g_wv, g_wo, g_up0, g_up1, g_dn0, g_dn1 = [
        red_big[n] for n in ("w_in", "w_out", "wq", "wk", "wv", "wo", "up0", "up1", "down0", "down1")]

    grads = dict(mix_w_in=g_w_in, mix_w_out=g_w_out, swa_wq=g_wq, swa_wk=g_wk, swa_wv=g_wv, swa_wo=g_wo,
                 ffn_w_up=[g_up0, g_up1], ffn_w_down=[g_dn0, g_dn1])
    weights = dict(meta_tokens=meta_tokens, attn_norm_w=attn_norm_w, ffn_norm_w=ffn_norm_w, mix_w_in=mix_w_in,
                   conv_a_w=conv_a_w, dn_conv_w=dn_conv_w, dn_a_log=dn_a_log, dn_dt_bias=dn_dt_bias,
                   dn_norm_w=dn_norm_w, mix_w_out=mix_w_out, swa_wq=swa_wq, swa_wk=swa_wk, swa_wv=swa_wv,
                   swa_q_norm_w=swa_q_norm_w, swa_k_norm_w=swa_k_norm_w, swa_sinks=swa_sinks, swa_wo=swa_wo,
                   ffn_w_up=ffn_w_up, ffn_conv_w=ffn_conv_w, ffn_w_down=ffn_w_down)
    m_in = dict(meta_tokens=m_meta_tokens, attn_norm_w=m_attn_norm_w, ffn_norm_w=m_ffn_norm_w, mix_w_in=m_mix_w_in,
                conv_a_w=m_conv_a_w, dn_conv_w=m_dn_conv_w, dn_a_log=m_dn_a_log, dn_dt_bias=m_dn_dt_bias,
                dn_norm_w=m_dn_norm_w, mix_w_out=m_mix_w_out, swa_wq=m_swa_wq, swa_wk=m_swa_wk, swa_wv=m_swa_wv,
                swa_q_norm_w=m_swa_q_norm_w, swa_k_norm_w=m_swa_k_norm_w, swa_sinks=m_swa_sinks, swa_wo=m_swa_wo,
                ffn_w_up=m_ffn_w_up, ffn_conv_w=m_ffn_conv_w, ffn_w_down=m_ffn_w_down)
    v_in = dict(meta_tokens=v_meta_tokens, attn_norm_w=v_attn_norm_w, ffn_norm_w=v_ffn_norm_w, mix_w_in=v_mix_w_in,
                conv_a_w=v_conv_a_w, dn_conv_w=v_dn_conv_w, dn_a_log=v_dn_a_log, dn_dt_bias=v_dn_dt_bias,
                dn_norm_w=v_dn_norm_w, mix_w_out=v_mix_w_out, swa_wq=v_swa_wq, swa_wk=v_swa_wk, swa_wv=v_swa_wv,
                swa_q_norm_w=v_swa_q_norm_w, swa_k_norm_w=v_swa_k_norm_w, swa_sinks=v_swa_sinks, swa_wo=v_swa_wo,
                ffn_w_up=v_ffn_w_up, ffn_conv_w=v_ffn_conv_w, ffn_w_down=v_ffn_w_down)
    names = list(weights)
    small = [n for n in names if n not in BIG]
    delta, new_m, new_v = {}, {}, {}
    for n in BIG:
        delta[n], new_m[n], new_v[n], grads[n] = adamw(weights[n], grads[n], m_in[n], v_in[n], name=f"adamw_{n}")
    gathered_small, _ = lax.optimization_barrier((gathered_small, new_v["ffn_w_down"]))
    (g_anw, g_fnw, g_alog, g_dtb, g_dnw, g_qw, g_kw, g_sinks, g_ca_f, g_dc_f, g_fc_f, g_meta_f,
     loss) = _split_flat(sum_slots(gathered_small), small_shapes)
    grads.update(meta_tokens=lax.dynamic_slice_in_dim(g_meta_f, chip * 256, 256, axis=1), attn_norm_w=g_anw,
                 ffn_norm_w=g_fnw, conv_a_w=lax.dynamic_slice_in_dim(g_ca_f, chip * 128, 128, axis=2),
                 dn_conv_w=lax.dynamic_slice_in_dim(g_dc_f, chip * 384, 384, axis=2), dn_a_log=g_alog,
                 dn_dt_bias=g_dtb, dn_norm_w=g_dnw, swa_q_norm_w=g_qw, swa_k_norm_w=g_kw, swa_sinks=g_sinks,
                 ffn_conv_w=lax.dynamic_slice_in_dim(g_fc_f, chip * 704, 704, axis=2))
    grads = {n: grads[n].reshape(weights[n].shape) for n in names}
    shapes = [weights[n].shape for n in small]
    packed = [_flat_pad([t[n] for n in small], SW_ROWS) for t in (weights, grads, m_in, v_in)]
    for store, flat in zip((delta, new_m, new_v), adamw(*packed, name="adamw_small")):
        for n, t in zip(small, _split_flat(flat, shapes)):
            store[n] = t
    return (loss, grad_x, *[grads[n] for n in names], *[delta[n] for n in names],
            *[new_m[n] for n in names], *[new_v[n] for n in names])
```
